```python
import math
import jax, jax.numpy as jnp
from jax import lax
import numpy as np

D_MODEL = 1024
BATCH = 16
SEQ = 2048
DEPTH = 1

MIX_WIDTH = D_MODEL
SSM_WIDTH = MIX_WIDTH // 2
CONV_WIDTH = MIX_WIDTH - SSM_WIDTH
SSM_GROUP = 16
SSM_N_GROUPS = SSM_WIDTH // SSM_GROUP
SSM_STATE = 64
CONV_HEAD_DIM = 64
CONV_N_HEADS = CONV_WIDTH // CONV_HEAD_DIM
CONV_K = 3
DT_MIN = 0.001
DT_MAX = 0.1
EPS = 1e-6
IN_COLS = 2 * SSM_WIDTH + 4 * CONV_WIDTH

kernel_name = "hybrid_s5_shortconv_parallel_heads"


def rmsnorm(x, g):
    x32 = x.astype(jnp.float32)
    y = x32 * lax.rsqrt(jnp.mean(x32 * x32, axis=-1, keepdims=True) + EPS)
    return (y * g.astype(jnp.float32)).astype(x.dtype)


def _scan_combine(c1, c2):
    a1r, a1i, b1r, b1i = c1
    a2r, a2i, b2r, b2i = c2
    ar = a2r * a1r - a2i * a1i
    ai = a2r * a1i + a2i * a1r
    br = a2r * b1r - a2i * b1i + b2r
    bi = a2r * b1i + a2i * b1r + b2i
    return (ar, ai, br, bi)


def s5_mixer(u, a_re, a_im, log_dt, b_re, b_im, c_re, c_im, d_skip, w_glu, b_glu):
    bsz, seq, _ = u.shape
    f32 = jnp.float32
    u32 = u.astype(f32).reshape(bsz, seq, SSM_N_GROUPS, SSM_GROUP)
    a_re = a_re.astype(f32); a_im = a_im.astype(f32)
    dt = jnp.exp(log_dt.astype(f32))[:, None]
    mag = jnp.exp(a_re * dt)
    ab_re = mag * jnp.cos(a_im * dt)
    ab_im = mag * jnp.sin(a_im * dt)
    den = a_re * a_re + a_im * a_im
    p_re = ab_re - 1.0
    p_im = ab_im
    q_re = (p_re * a_re + p_im * a_im) / den
    q_im = (p_im * a_re - p_re * a_im) / den
    b_re = b_re.astype(f32); b_im = b_im.astype(f32)
    bb_re = q_re[..., None] * b_re - q_im[..., None] * b_im
    bb_im = q_re[..., None] * b_im + q_im[..., None] * b_re
    bu_re = jnp.einsum('blgh,gph->blgp', u32, bb_re)
    bu_im = jnp.einsum('blgh,gph->blgp', u32, bb_im)
    a_seq_re = jnp.broadcast_to(ab_re[None, None], (1, seq, SSM_N_GROUPS, SSM_STATE))
    a_seq_im = jnp.broadcast_to(ab_im[None, None], (1, seq, SSM_N_GROUPS, SSM_STATE))
    _, _, s_re, s_im = lax.associative_scan(
        _scan_combine, (a_seq_re, a_seq_im, bu_re, bu_im), axis=1)
    y = (jnp.einsum('blgp,ghp->blgh', s_re, c_re.astype(f32))
         - jnp.einsum('blgp,ghp->blgh', s_im, c_im.astype(f32))
         + d_skip.astype(f32) * u32)
    y = y.reshape(bsz, seq, SSM_WIDTH)
    y = jax.nn.gelu(y)
    y = y * jax.nn.sigmoid(jnp.einsum('bld,de->ble', y, w_glu.astype(f32)) + b_glu.astype(f32))
    return y.astype(u.dtype)


def short_conv_mixer(h, gate_b, gate_c, conv_w):
    v = gate_c * h
    vp = jnp.pad(v, ((0, 0), (CONV_K - 1, 0), (0, 0)))
    seq = h.shape[1]
    y = sum(conv_w[k] * vp[:, k:k + seq] for k in range(CONV_K))
    return gate_b * y


def _fwd_setup_inputs(seed: int = 0) -> dict:
    key = jax.random.key(seed)
    ks = jax.random.split(key, 20)
    f32 = jnp.float32
    x = jax.random.normal(ks[0], (BATCH, SEQ, D_MODEL), f32)
    norm_gain = 1.0 + 0.02 * jax.random.normal(ks[1], (DEPTH, D_MODEL), f32)
    w_in = jax.random.normal(ks[2], (DEPTH, D_MODEL, IN_COLS), f32) * D_MODEL ** -0.5
    n = jnp.arange(SSM_STATE, dtype=f32)
    ssm_a_re = -0.5 + 0.01 * jax.random.normal(ks[3], (DEPTH, SSM_N_GROUPS, SSM_STATE), f32)
    ssm_a_im = math.pi * n + 0.01 * jax.random.normal(ks[4], (DEPTH, SSM_N_GROUPS, SSM_STATE), f32)
    ssm_log_dt = jax.random.uniform(ks[5], (DEPTH, SSM_N_GROUPS), f32,
                                    math.log(DT_MIN), math.log(DT_MAX))
    bscale = (2.0 * SSM_GROUP) ** -0.5
    ssm_b_re = jax.random.normal(ks[6], (DEPTH, SSM_N_GROUPS, SSM_STATE, SSM_GROUP), f32) * bscale
    ssm_b_im = jax.random.normal(ks[7], (DEPTH, SSM_N_GROUPS, SSM_STATE, SSM_GROUP), f32) * bscale
    cscale = (2.0 * SSM_STATE) ** -0.5
    ssm_c_re = jax.random.normal(ks[8], (DEPTH, SSM_N_GROUPS, SSM_GROUP, SSM_STATE), f32) * cscale
    ssm_c_im = jax.random.normal(ks[9], (DEPTH, SSM_N_GROUPS, SSM_GROUP, SSM_STATE), f32) * cscale
    ssm_d = 1.0 + 0.1 * jax.random.normal(ks[10], (DEPTH, SSM_N_GROUPS, SSM_GROUP), f32)
    w_glu = jax.random.normal(ks[11], (DEPTH, SSM_WIDTH, SSM_WIDTH), f32) * SSM_WIDTH ** -0.5
    b_glu = 0.01 * jax.random.normal(ks[12], (DEPTH, SSM_WIDTH), f32)
    conv_w = jax.random.normal(ks[13], (DEPTH, CONV_K, CONV_WIDTH), f32) * CONV_K ** -0.5
    w_out = jax.random.normal(ks[14], (DEPTH, MIX_WIDTH, D_MODEL), f32) * MIX_WIDTH ** -0.5
    final_norm_gain = 1.0 + 0.02 * jax.random.normal(ks[15], (D_MODEL,), f32)
    return {"x": x, "norm_gain": norm_gain, "w_in": w_in,
            "ssm_a_re": ssm_a_re, "ssm_a_im": ssm_a_im, "ssm_log_dt": ssm_log_dt,
            "ssm_b_re": ssm_b_re, "ssm_b_im": ssm_b_im,
            "ssm_c_re": ssm_c_re, "ssm_c_im": ssm_c_im, "ssm_d": ssm_d,
            "w_glu": w_glu, "b_glu": b_glu, "conv_w": conv_w, "w_out": w_out,
            "final_norm_gain": final_norm_gain}


def _fwd_reference(x, norm_gain, w_in, ssm_a_re, ssm_a_im, ssm_log_dt, ssm_b_re, ssm_b_im,
              ssm_c_re, ssm_c_im, ssm_d, w_glu, b_glu, conv_w, w_out, final_norm_gain):
    h = x
    for l in range(DEPTH):
        xn = rmsnorm(h, norm_gain[l])
        proj = jnp.einsum('bld,dc->blc', xn, w_in[l])
        s0 = SSM_WIDTH
        u_ssm = proj[..., :s0]
        z_ssm = proj[..., s0:2 * s0]
        c0 = 2 * s0
        h_conv = proj[..., c0:c0 + CONV_WIDTH]
        b_conv = proj[..., c0 + CONV_WIDTH:c0 + 2 * CONV_WIDTH]
        c_conv = proj[..., c0 + 2 * CONV_WIDTH:c0 + 3 * CONV_WIDTH]
        z_conv = proj[..., c0 + 3 * CONV_WIDTH:c0 + 4 * CONV_WIDTH]
        y_ssm = s5_mixer(u_ssm, ssm_a_re[l], ssm_a_im[l], ssm_log_dt[l],
                         ssm_b_re[l], ssm_b_im[l], ssm_c_re[l], ssm_c_im[l],
                         ssm_d[l], w_glu[l], b_glu[l]) * jax.nn.silu(z_ssm)
        y_conv = short_conv_mixer(h_conv, b_conv, c_conv, conv_w[l]) * jax.nn.silu(z_conv)
        y = jnp.concatenate([y_ssm, y_conv], axis=-1)
        h = h + jnp.einsum('blc,cd->bld', y, w_out[l])
    return rmsnorm(h, final_norm_gain)


import jax as _jax
import jax.numpy as _jnp

TWIN_FORMAT = 'train_step'
FWD_PARAMS = ['x', 'norm_gain', 'w_in', 'ssm_a_re', 'ssm_a_im', 'ssm_log_dt', 'ssm_b_re', 'ssm_b_im', 'ssm_c_re', 'ssm_c_im', 'ssm_d', 'w_glu', 'b_glu', 'conv_w', 'w_out', 'final_norm_gain']
TWIN_WEIGHTS = ['norm_gain', 'w_in', 'ssm_a_re', 'ssm_a_im', 'ssm_log_dt', 'ssm_b_re', 'ssm_b_im', 'ssm_c_re', 'ssm_c_im', 'ssm_d', 'w_glu', 'b_glu', 'conv_w', 'w_out', 'final_norm_gain']
TWIN_DIFF_INPUT = 'x'
TWIN_INPUTS = ['x', 'norm_gain', 'w_in', 'ssm_a_re', 'ssm_a_im', 'ssm_log_dt', 'ssm_b_re', 'ssm_b_im', 'ssm_c_re', 'ssm_c_im', 'ssm_d', 'w_glu', 'b_glu', 'conv_w', 'w_out', 'final_norm_gain', 'loss_target', 'm_norm_gain', 'm_w_in', 'm_ssm_a_re', 'm_ssm_a_im', 'm_ssm_log_dt', 'm_ssm_b_re', 'm_ssm_b_im', 'm_ssm_c_re', 'm_ssm_c_im', 'm_ssm_d', 'm_w_glu', 'm_b_glu', 'm_conv_w', 'm_w_out', 'm_final_norm_gain', 'v_norm_gain', 'v_w_in', 'v_ssm_a_re', 'v_ssm_a_im', 'v_ssm_log_dt', 'v_ssm_b_re', 'v_ssm_b_im', 'v_ssm_c_re', 'v_ssm_c_im', 'v_ssm_d', 'v_w_glu', 'v_b_glu', 'v_conv_w', 'v_w_out', 'v_final_norm_gain']
TWIN_OUTPUTS = ['loss', 'grad_x', 'grad_norm_gain', 'grad_w_in', 'grad_ssm_a_re', 'grad_ssm_a_im', 'grad_ssm_log_dt', 'grad_ssm_b_re', 'grad_ssm_b_im', 'grad_ssm_c_re', 'grad_ssm_c_im', 'grad_ssm_d', 'grad_w_glu', 'grad_b_glu', 'grad_conv_w', 'grad_w_out', 'grad_final_norm_gain', 'delta_norm_gain', 'delta_w_in', 'delta_ssm_a_re', 'delta_ssm_a_im', 'delta_ssm_log_dt', 'delta_ssm_b_re', 'delta_ssm_b_im', 'delta_ssm_c_re', 'delta_ssm_c_im', 'delta_ssm_d', 'delta_w_glu', 'delta_b_glu', 'delta_conv_w', 'delta_w_out', 'delta_final_norm_gain', 'new_m_norm_gain', 'new_m_w_in', 'new_m_ssm_a_re', 'new_m_ssm_a_im', 'new_m_ssm_log_dt', 'new_m_ssm_b_re', 'new_m_ssm_b_im', 'new_m_ssm_c_re', 'new_m_ssm_c_im', 'new_m_ssm_d', 'new_m_w_glu', 'new_m_b_glu', 'new_m_conv_w', 'new_m_w_out', 'new_m_final_norm_gain', 'new_v_norm_gain', 'new_v_w_in', 'new_v_ssm_a_re', 'new_v_ssm_a_im', 'new_v_ssm_log_dt', 'new_v_ssm_b_re', 'new_v_ssm_b_im', 'new_v_ssm_c_re', 'new_v_ssm_c_im', 'new_v_ssm_d', 'new_v_w_glu', 'new_v_b_glu', 'new_v_conv_w', 'new_v_w_out', 'new_v_final_norm_gain']
TWIN_LEAF_KINDS = {'loss': 'loss', 'grad_x': 'grad_x', 'grad_norm_gain': 'grad_w', 'grad_w_in': 'grad_w', 'grad_ssm_a_re': 'grad_w', 'grad_ssm_a_im': 'grad_w', 'grad_ssm_log_dt': 'grad_w', 'grad_ssm_b_re': 'grad_w', 'grad_ssm_b_im': 'grad_w', 'grad_ssm_c_re': 'grad_w', 'grad_ssm_c_im': 'grad_w', 'grad_ssm_d': 'grad_w', 'grad_w_glu': 'grad_w', 'grad_b_glu': 'grad_w', 'grad_conv_w': 'grad_w', 'grad_w_out': 'grad_w', 'grad_final_norm_gain': 'grad_w', 'delta_norm_gain': 'delta_w', 'delta_w_in': 'delta_w', 'delta_ssm_a_re': 'delta_w', 'delta_ssm_a_im': 'delta_w', 'delta_ssm_log_dt': 'delta_w', 'delta_ssm_b_re': 'delta_w', 'delta_ssm_b_im': 'delta_w', 'delta_ssm_c_re': 'delta_w', 'delta_ssm_c_im': 'delta_w', 'delta_ssm_d': 'delta_w', 'delta_w_glu': 'delta_w', 'delta_b_glu': 'delta_w', 'delta_conv_w': 'delta_w', 'delta_w_out': 'delta_w', 'delta_final_norm_gain': 'delta_w', 'new_m_norm_gain': 'new_m', 'new_m_w_in': 'new_m', 'new_m_ssm_a_re': 'new_m', 'new_m_ssm_a_im': 'new_m', 'new_m_ssm_log_dt': 'new_m', 'new_m_ssm_b_re': 'new_m', 'new_m_ssm_b_im': 'new_m', 'new_m_ssm_c_re': 'new_m', 'new_m_ssm_c_im': 'new_m', 'new_m_ssm_d': 'new_m', 'new_m_w_glu': 'new_m', 'new_m_b_glu': 'new_m', 'new_m_conv_w': 'new_m', 'new_m_w_out': 'new_m', 'new_m_final_norm_gain': 'new_m', 'new_v_norm_gain': 'new_v', 'new_v_w_in': 'new_v', 'new_v_ssm_a_re': 'new_v', 'new_v_ssm_a_im': 'new_v', 'new_v_ssm_log_dt': 'new_v', 'new_v_ssm_b_re': 'new_v', 'new_v_ssm_b_im': 'new_v', 'new_v_ssm_c_re': 'new_v', 'new_v_ssm_c_im': 'new_v', 'new_v_ssm_d': 'new_v', 'new_v_w_glu': 'new_v', 'new_v_b_glu': 'new_v', 'new_v_conv_w': 'new_v', 'new_v_w_out': 'new_v', 'new_v_final_norm_gain': 'new_v'}


def _forward(args):
    return _fwd_reference(*[args[k] for k in FWD_PARAMS])


def _output_shape():
    out = _jax.eval_shape(lambda: _forward(_fwd_setup_inputs(0)))
    return out.shape, out.dtype

N_MICROBATCH = 1
ADAM_LR = 0.001
ADAM_B1 = 0.9
ADAM_B2 = 0.999
ADAM_EPS = 1e-08
ADAM_WD = 0.01
ADAM_STEP = 10
PER_EXAMPLE_BATCH_AXIS = {'x': 0, 'loss_target': 0}
SHARED_INPUTS = []
_WEIGHT_DTYPES = {'norm_gain': _jnp.float32, 'w_in': _jnp.float32, 'ssm_a_re': _jnp.float32, 'ssm_a_im': _jnp.float32, 'ssm_log_dt': _jnp.float32, 'ssm_b_re': _jnp.float32, 'ssm_b_im': _jnp.float32, 'ssm_c_re': _jnp.float32, 'ssm_c_im': _jnp.float32, 'ssm_d': _jnp.float32, 'w_glu': _jnp.float32, 'b_glu': _jnp.float32, 'conv_w': _jnp.float32, 'w_out': _jnp.float32, 'final_norm_gain': _jnp.float32}
MOMENT_SCALE = {'norm_gain': 1.436309e-01, 'w_in': 8.196288e-02, 'ssm_a_re': 1.960313e-03, 'ssm_a_im': 1.915868e-03, 'ssm_log_dt': 1.046735e+00, 'ssm_b_re': 1.356739e-03, 'ssm_b_im': 1.268834e-03, 'ssm_c_re': 2.654040e-03, 'ssm_c_im': 2.740403e-03, 'ssm_d': 3.949457e-02, 'w_glu': 9.486242e-03, 'b_glu': 1.385875e-02, 'conv_w': 9.570267e-02, 'w_out': 7.201165e-02, 'final_norm_gain': 3.192650e+01}


def _to_microbatches(a, axis):
    t = _jnp.moveaxis(a, axis, 0)
    t = t.reshape((N_MICROBATCH, t.shape[0] // N_MICROBATCH) + t.shape[1:])
    return _jnp.moveaxis(t, 1, axis + 1)


def setup_inputs(seed: int = 0) -> dict:
    inp = _fwd_setup_inputs(seed)
    key = _jax.random.fold_in(_jax.random.key(seed), 7919)
    shape, _ = _output_shape()
    out = dict(inp)
    out["loss_target"] = _jax.random.normal(_jax.random.fold_in(key, 0), shape, _jnp.float32)
    for i, name in enumerate(TWIN_WEIGHTS):
        w = inp[name].astype(_jnp.float32)
        if MOMENT_SCALE is None:
            s = _jnp.sqrt(_jnp.mean(_jnp.square(w)) + 1e-30)
        else:
            s = MOMENT_SCALE[name]
        km, kv = _jax.random.split(_jax.random.fold_in(key, i + 1))
        out[name] = w
        out["m_" + name] = s * _jax.random.normal(km, w.shape, _jnp.float32)
        out["v_" + name] = (s * s) * _jax.random.uniform(kv, w.shape, _jnp.float32, 0.5, 1.5)
    if N_MICROBATCH > 1:
        for name, axis in PER_EXAMPLE_BATCH_AXIS.items():
            out[name] = _to_microbatches(out[name], axis)
    return {'x': out['x'], 'norm_gain': out['norm_gain'], 'w_in': out['w_in'], 'ssm_a_re': out['ssm_a_re'], 'ssm_a_im': out['ssm_a_im'], 'ssm_log_dt': out['ssm_log_dt'], 'ssm_b_re': out['ssm_b_re'], 'ssm_b_im': out['ssm_b_im'], 'ssm_c_re': out['ssm_c_re'], 'ssm_c_im': out['ssm_c_im'], 'ssm_d': out['ssm_d'], 'w_glu': out['w_glu'], 'b_glu': out['b_glu'], 'conv_w': out['conv_w'], 'w_out': out['w_out'], 'final_norm_gain': out['final_norm_gain'], 'loss_target': out['loss_target'], 'm_norm_gain': out['m_norm_gain'], 'm_w_in': out['m_w_in'], 'm_ssm_a_re': out['m_ssm_a_re'], 'm_ssm_a_im': out['m_ssm_a_im'], 'm_ssm_log_dt': out['m_ssm_log_dt'], 'm_ssm_b_re': out['m_ssm_b_re'], 'm_ssm_b_im': out['m_ssm_b_im'], 'm_ssm_c_re': out['m_ssm_c_re'], 'm_ssm_c_im': out['m_ssm_c_im'], 'm_ssm_d': out['m_ssm_d'], 'm_w_glu': out['m_w_glu'], 'm_b_glu': out['m_b_glu'], 'm_conv_w': out['m_conv_w'], 'm_w_out': out['m_w_out'], 'm_final_norm_gain': out['m_final_norm_gain'], 'v_norm_gain': out['v_norm_gain'], 'v_w_in': out['v_w_in'], 'v_ssm_a_re': out['v_ssm_a_re'], 'v_ssm_a_im': out['v_ssm_a_im'], 'v_ssm_log_dt': out['v_ssm_log_dt'], 'v_ssm_b_re': out['v_ssm_b_re'], 'v_ssm_b_im': out['v_ssm_b_im'], 'v_ssm_c_re': out['v_ssm_c_re'], 'v_ssm_c_im': out['v_ssm_c_im'], 'v_ssm_d': out['v_ssm_d'], 'v_w_glu': out['v_w_glu'], 'v_b_glu': out['v_b_glu'], 'v_conv_w': out['v_conv_w'], 'v_w_out': out['v_w_out'], 'v_final_norm_gain': out['v_final_norm_gain']}


def _loss(weights, diff, rest, loss_target):
    with _jax.named_scope("forward"):
        args = {**rest, TWIN_DIFF_INPUT: diff, **{k: w.astype(_WEIGHT_DTYPES[k]) for k, w in weights.items()}}
        y = _forward(args)
    with _jax.named_scope("loss_head"):
        err = _jnp.square(y.astype(_jnp.float32) - loss_target)
        return 0.5 * _jnp.sum(_jnp.mean(err, axis=-1)) if err.ndim else 0.5 * err


def _adamw(w, g, m, v):
    m = ADAM_B1 * m + (1.0 - ADAM_B1) * g
    v = ADAM_B2 * v + (1.0 - ADAM_B2) * _jnp.square(g)
    m_hat = m / (1.0 - ADAM_B1 ** ADAM_STEP)
    v_hat = v / (1.0 - ADAM_B2 ** ADAM_STEP)
    delta = -ADAM_LR * (m_hat / (_jnp.sqrt(v_hat) + ADAM_EPS) + ADAM_WD * w)
    return delta, m, v


def reference(x, norm_gain, w_in, ssm_a_re, ssm_a_im, ssm_log_dt, ssm_b_re, ssm_b_im, ssm_c_re, ssm_c_im, ssm_d, w_glu, b_glu, conv_w, w_out, final_norm_gain, loss_target, m_norm_gain, m_w_in, m_ssm_a_re, m_ssm_a_im, m_ssm_log_dt, m_ssm_b_re, m_ssm_b_im, m_ssm_c_re, m_ssm_c_im, m_ssm_d, m_w_glu, m_b_glu, m_conv_w, m_w_out, m_final_norm_gain, v_norm_gain, v_w_in, v_ssm_a_re, v_ssm_a_im, v_ssm_log_dt, v_ssm_b_re, v_ssm_b_im, v_ssm_c_re, v_ssm_c_im, v_ssm_d, v_w_glu, v_b_glu, v_conv_w, v_w_out, v_final_norm_gain):
    given = dict(x=x, norm_gain=norm_gain, w_in=w_in, ssm_a_re=ssm_a_re, ssm_a_im=ssm_a_im, ssm_log_dt=ssm_log_dt, ssm_b_re=ssm_b_re, ssm_b_im=ssm_b_im, ssm_c_re=ssm_c_re, ssm_c_im=ssm_c_im, ssm_d=ssm_d, w_glu=w_glu, b_glu=b_glu, conv_w=conv_w, w_out=w_out, final_norm_gain=final_norm_gain, loss_target=loss_target, m_norm_gain=m_norm_gain, m_w_in=m_w_in, m_ssm_a_re=m_ssm_a_re, m_ssm_a_im=m_ssm_a_im, m_ssm_log_dt=m_ssm_log_dt, m_ssm_b_re=m_ssm_b_re, m_ssm_b_im=m_ssm_b_im, m_ssm_c_re=m_ssm_c_re, m_ssm_c_im=m_ssm_c_im, m_ssm_d=m_ssm_d, m_w_glu=m_w_glu, m_b_glu=m_b_glu, m_conv_w=m_conv_w, m_w_out=m_w_out, m_final_norm_gain=m_final_norm_gain, v_norm_gain=v_norm_gain, v_w_in=v_w_in, v_ssm_a_re=v_ssm_a_re, v_ssm_a_im=v_ssm_a_im, v_ssm_log_dt=v_ssm_log_dt, v_ssm_b_re=v_ssm_b_re, v_ssm_b_im=v_ssm_b_im, v_ssm_c_re=v_ssm_c_re, v_ssm_c_im=v_ssm_c_im, v_ssm_d=v_ssm_d, v_w_glu=v_w_glu, v_b_glu=v_b_glu, v_conv_w=v_conv_w, v_w_out=v_w_out, v_final_norm_gain=v_final_norm_gain)
    weights = {n: given[n] for n in TWIN_WEIGHTS}
    shared = {n: given[n] for n in SHARED_INPUTS}
    per_example = {n: given[n] for n in ['x']}
    grad_fn = _jax.value_and_grad(_loss, argnums=(0, 1))

    def one_microbatch(ex, loss_target):
        ex = dict(ex)
        diff = ex.pop(TWIN_DIFF_INPUT)
        return grad_fn(weights, diff, {**shared, **ex}, loss_target)

    if N_MICROBATCH == 1:
        loss, (grad_w, grad_x) = one_microbatch(per_example, given["loss_target"])
    else:
        def body(carry, xs):
            loss_sum, grad_sum = carry
            l_k, (gw_k, gx_k) = one_microbatch(xs[0], xs[1])
            with _jax.named_scope("update"):
                return (loss_sum + l_k, _jax.tree.map(_jnp.add, grad_sum, gw_k)), gx_k

        init = (_jnp.zeros((), _jnp.float32), _jax.tree.map(_jnp.zeros_like, weights))
        (loss, grad_w), grad_x = _jax.lax.scan(body, init, (per_example, given["loss_target"]))
    with _jax.named_scope("update"):
        delta_w, new_m, new_v = {}, {}, {}
        for n in TWIN_WEIGHTS:
            delta_w[n], new_m[n], new_v[n] = _adamw(weights[n], grad_w[n], given["m_" + n], given["v_" + n])
    return (loss, grad_x, *[grad_w[n] for n in TWIN_WEIGHTS], *[delta_w[n] for n in TWIN_WEIGHTS],
            *[new_m[n] for n in TWIN_WEIGHTS], *[new_v[n] for n in TWIN_WEIGHTS])
```

```python
import functools
import math

import jax
import jax.numpy as jnp
from jax import lax
from jax.experimental import pallas as pl
from jax.experimental.pallas import tpu as pltpu

F32 = jnp.float32
BF16 = jnp.bfloat16

N_DEV = 8
D_MODEL = 1024
SSM_W = 512
CONV_W = 512
N_GROUPS = 32
GROUP = 16
STATE = 64
IN_COLS = 3072
COLS_PER_DEV = IN_COLS // N_DEV
OUT_ROWS_PER_DEV = D_MODEL // N_DEV
GLU_ROWS_PER_DEV = SSM_W // N_DEV
CONV_COLS_PER_DEV = CONV_W // N_DEV
EPS = 1e-6

N_JBLK = 4
JB_CH = SSM_W // N_JBLK
JB_ST = N_GROUPS * STATE // N_JBLK

ADAM_LR = 0.001
ADAM_B1 = 0.9
ADAM_B2 = 0.999
ADAM_EPS = 1e-08
ADAM_WD = 0.01
ADAM_STEP = 10

SUBLANES = 8
LANES = 128
VMEM_LIMIT = 48 * 1024 * 1024
TOK_TILE = 256
SCAN_TILE = 256

MESH = pl.DeviceIdType.MESH
HBM_SPEC = pl.BlockSpec(memory_space=pltpu.HBM)


def _pcall(body, **kw):
    return pl.pallas_call(body, **kw)


def _params(n_grid):
    return pltpu.CompilerParams(dimension_semantics=("arbitrary",) * n_grid,
                                vmem_limit_bytes=VMEM_LIMIT)


def _dot(a, b):
    return jnp.dot(a, b, preferred_element_type=F32)


def _dot_nt(a, b):
    return lax.dot_general(a, b, (((1,), (1,)), ((), ())), preferred_element_type=F32)


def _dot_tn(a, b):
    return lax.dot_general(a, b, (((0,), (0,)), ((), ())), preferred_element_type=F32)


def _sigmoid(z):
    return 1.0 / (1.0 + jnp.exp(-z))


_GELU_C = math.sqrt(2.0 / math.pi)


def _gelu_and_grad(y):
    inner = _GELU_C * (y + 0.044715 * (y * y * y))
    t = jnp.tanh(inner)
    g = 0.5 * y * (1.0 + t)
    dg = 0.5 * (1.0 + t) + 0.5 * y * (1.0 - t * t) * (_GELU_C * (1.0 + 3.0 * 0.044715 * (y * y)))
    return g, dg


def _silu_and_grad(z):
    s = _sigmoid(z)
    return z * s, s * (1.0 + z * (1.0 - s))


def _shift_down(v, halo, k):
    rolled = pltpu.roll(v, k, 0)
    row = lax.broadcasted_iota(jnp.int32, v.shape, 0)
    for r in range(k):
        rolled = jnp.where(row == r, halo[SUBLANES - k + r:SUBLANES - k + r + 1, :], rolled)
    return rolled


def _shift_up(v, halo, k):
    n = v.shape[0]
    rolled = pltpu.roll(v, n - k, 0)
    row = lax.broadcasted_iota(jnp.int32, v.shape, 0)
    for r in range(k):
        rolled = jnp.where(row == n - k + r, halo[r:r + 1, :], rolled)
    return rolled


def _mesh_pos():
    return lax.axis_index("x"), lax.axis_index("y"), lax.axis_index("c")


def _gather_weights(w_in_b, w_out_b, w_glu_b, conv_p):
    n_arr = 4

    def body(win, wout, wglu, cw, o_in, o_out, o_glu, o_cw, send_sems, recv_sems, loc_sems):
        x, y, c = _mesh_pos()
        me, sib = (x, y, c), (x, y, 1 - c)
        chips = [(1 - x, y), (x, 1 - y), (1 - x, 1 - y)]
        own = [win, wout, wglu, cw]

        def slots(px, py, pc):
            k = 4 * px + 2 * py + pc
            return [o_in.at[:, pl.ds(pl.multiple_of(k * COLS_PER_DEV, LANES), COLS_PER_DEV)],
                    o_out.at[pl.ds(pl.multiple_of(k * OUT_ROWS_PER_DEV, OUT_ROWS_PER_DEV), OUT_ROWS_PER_DEV), :],
                    o_glu.at[pl.ds(pl.multiple_of(k * GLU_ROWS_PER_DEV, GLU_ROWS_PER_DEV), GLU_ROWS_PER_DEV), :],
                    o_cw.at[k]]

        def copies(kidx, block, to, from_own=False):
            dst = slots(*block)
            src = own if from_own else dst
            return [pltpu.make_async_remote_copy(
                src_ref=src[a], dst_ref=dst[a],
                send_sem=send_sems.at[kidx * n_arr + a], recv_sem=recv_sems.at[kidx * n_arr + a],
                device_id=to, device_id_type=MESH) for a in range(n_arr)]

        mine = [pltpu.make_async_copy(own[a], slots(*me)[a], loc_sems.at[a]) for a in range(n_arr)]
        for cp in mine:
            cp.start()
        first = copies(0, me, sib, from_own=True)
        for j, chip in enumerate(chips):
            first += copies(1 + j, me, (*chip, c), from_own=True)
        for cp in first:
            cp.start()
        passed = []
        for j, chip in enumerate(chips):
            for cp in copies(1 + j, (*chip, c), me):
                cp.wait_recv()
            fwd = copies(4 + j, (*chip, c), sib)
            for cp in fwd:
                cp.start()
            passed += fwd
        for cp in copies(0, sib, me):
            cp.wait_recv()
        for j, chip in enumerate(chips):
            for cp in copies(4 + j, (*chip, 1 - c), me):
                cp.wait_recv()
        for cp in first + passed:
            cp.wait_send()
        for cp in mine:
            cp.wait()

    return _pcall(
        body, name="gather_weights",
        out_shape=(jax.ShapeDtypeStruct((D_MODEL, IN_COLS), BF16),
                   jax.ShapeDtypeStruct((D_MODEL, D_MODEL), BF16),
                   jax.ShapeDtypeStruct((SSM_W, SSM_W), BF16),
                   jax.ShapeDtypeStruct((N_DEV, SUBLANES, LANES), F32)),
        in_specs=[HBM_SPEC] * n_arr, out_specs=(HBM_SPEC,) * n_arr,
        scratch_shapes=[pltpu.SemaphoreType.DMA((7 * n_arr,)), pltpu.SemaphoreType.DMA((7 * n_arr,)),
                        pltpu.SemaphoreType.DMA((n_arr,))],
    )(w_in_b, w_out_b, w_glu_b, conv_p)


def _exchange(g_in, g_out, g_glu, g_small):
    n_arr = 4

    def body(gin, gout, gglu, gsm, r_in, r_out, r_glu, r_sm, send_sems, recv_sems, loc_sems):
        x, y, c = _mesh_pos()
        me_id = 4 * x + 2 * y + c
        srcs_for = lambda pid: [gin.at[pid], gout.at[pid], gglu.at[pid], gsm]
        dsts = [r_in.at[me_id], r_out.at[me_id], r_glu.at[me_id], r_sm.at[me_id]]
        mine = [pltpu.make_async_copy(srcs_for(me_id)[a], dsts[a], loc_sems.at[a]) for a in range(n_arr)]
        for cp in mine:
            cp.start()
        sends = []
        for k in range(1, N_DEV):
            kx, ky, kc = (k >> 2) & 1, (k >> 1) & 1, k & 1
            px, py, pc = x ^ kx, y ^ ky, c ^ kc
            pid = 4 * px + 2 * py + pc
            src = srcs_for(pid)
            for a in range(n_arr):
                sends.append(pltpu.make_async_remote_copy(
                    src_ref=src[a], dst_ref=dsts[a],
                    send_sem=send_sems.at[(k - 1) * n_arr + a], recv_sem=recv_sems.at[(k - 1) * n_arr + a],
                    device_id=(px, py, pc), device_id_type=MESH))
        for cp in sends:
            cp.start()
        for cp in sends:
            cp.wait()
        for cp in mine:
            cp.wait()

    shapes = [(N_DEV,) + g_in.shape[1:], (N_DEV,) + g_out.shape[1:], (N_DEV,) + g_glu.shape[1:],
              (N_DEV,) + g_small.shape]
    return _pcall(
        body, name="exchange",
        out_shape=tuple(jax.ShapeDtypeStruct(s, F32) for s in shapes),
        in_specs=[HBM_SPEC] * n_arr, out_specs=(HBM_SPEC,) * n_arr,
        scratch_shapes=[pltpu.SemaphoreType.DMA((7 * n_arr,)), pltpu.SemaphoreType.DMA((7 * n_arr,)),
                        pltpu.SemaphoreType.DMA((n_arr,))],
    )(g_in, g_out, g_glu, g_small)


def _disc(a_re, a_im, log_dt, b_re, b_im):
    dt = jnp.exp(log_dt)
    mag = jnp.exp(a_re * dt)
    ab_re = mag * jnp.cos(a_im * dt)
    ab_im = mag * jnp.sin(a_im * dt)
    den = a_re * a_re + a_im * a_im
    p_re = ab_re - 1.0
    p_im = ab_im
    q_re = (p_re * a_re + p_im * a_im) / den
    q_im = (p_im * a_re - p_re * a_im) / den
    bb_re = q_re * b_re - q_im * b_im
    bb_im = q_re * b_im + q_im * b_re
    return ab_re, ab_im, bb_re, bb_im


def _ssm_disc(a_re_r, a_im_r, log_dt, b_re, b_im):
    def body(are, aim, ldt, bre, bim, o_abre, o_abim, o_bbre, o_bbim):
        outs = _disc(are[...], aim[...], ldt[...], bre[...], bim[...])
        for o, v in zip((o_abre, o_abim, o_bbre, o_bbim), outs):
            o[...] = v

    shp = jax.ShapeDtypeStruct(a_re_r.shape, F32)
    return _pcall(body, name="ssm_disc", out_shape=(shp,) * 4)(a_re_r, a_im_r, log_dt, b_re, b_im)


def _ssm_disc_bwd(a_re_r, a_im_r, log_dt, b_re, b_im, g_abre, g_abim, g_bbre, g_bbim):
    width = a_re_r.shape[1]

    def body(are, aim, ldt, bre, bim, gabre, gabim, gbbre, gbbim, o_are, o_aim, o_ldt, o_bre, o_bim):
        _, vjp = jax.vjp(_disc, are[...], aim[...], ldt[...], bre[...], bim[...])
        d_are, d_aim, d_ldt, d_bre, d_bim = vjp((gabre[...], gabim[...], gbbre[...], gbbim[...]))

        def group_sum(v):
            for k in (1, 2, 4, 8):
                v = v + pltpu.roll(v, width - k, 1)
            return v

        o_are[...] = group_sum(d_are)
        o_aim[...] = group_sum(d_aim)
        o_ldt[...] = d_ldt
        o_bre[...] = d_bre
        o_bim[...] = d_bim

    shp = jax.ShapeDtypeStruct(a_re_r.shape, F32)
    return _pcall(body, name="ssm_disc_bwd",
                  out_shape=(shp, shp, jax.ShapeDtypeStruct(log_dt.shape, F32), shp, shp),
                  )(a_re_r, a_im_r, log_dt, b_re, b_im, g_abre, g_abim, g_bbre, g_bbim)


def _in_proj(x2, g1, w_full):
    n = x2.shape[0]
    tm = TOK_TILE

    def body(x_ref, g_ref, w_ref, xn_ref, *outs):
        x = x_ref[...]
        r = lax.rsqrt(jnp.mean(x * x, axis=-1, keepdims=True) + EPS)
        xn = ((x * r) * g_ref[...]).astype(BF16)
        xn_ref[...] = xn
        for i, o in enumerate(outs):
            o[...] = _dot(xn, w_ref[:, i * SSM_W:(i + 1) * SSM_W])

    seg = jax.ShapeDtypeStruct((n, SSM_W), F32)
    seg_spec = pl.BlockSpec((tm, SSM_W), lambda i: (i, 0))
    return _pcall(
        body, name="in_proj", grid=(n // tm,),
        out_shape=(jax.ShapeDtypeStruct((n, D_MODEL), BF16),) + (seg,) * 6,
        in_specs=[pl.BlockSpec((tm, D_MODEL), lambda i: (i, 0)),
                  pl.BlockSpec((1, D_MODEL), lambda i: (0, 0)),
                  pl.BlockSpec((D_MODEL, IN_COLS), lambda i: (0, 0))],
        out_specs=(pl.BlockSpec((tm, D_MODEL), lambda i: (i, 0)),) + (seg_spec,) * 6,
        compiler_params=_params(1),
    )(x2, g1, w_full)


def _cmul(p, q):
    return p[0] * q[0] - p[1] * q[1], p[0] * q[1] + p[1] * q[0]


def _scan_tables(ar, ai, width, reverse):
    pows = [(ar, ai)]
    for _ in range(SUBLANES - 1):
        pows.append(_cmul(pows[-1], (ar, ai)))
    row = lax.broadcasted_iota(jnp.int32, (SUBLANES, width), 0)

    def bc(v):
        return jnp.broadcast_to(v, (SUBLANES, width))

    levels = []
    for k in (1, 2, 4):
        keep = (row <= SUBLANES - 1 - k) if reverse else (row >= k)
        levels.append((jnp.where(keep, bc(pows[k - 1][0]), 0.0), jnp.where(keep, bc(pows[k - 1][1]), 0.0)))
    cre = jnp.zeros((SUBLANES, width), F32)
    cim = jnp.zeros((SUBLANES, width), F32)
    for r in range(SUBLANES):
        e = (SUBLANES - r) if reverse else (r + 1)
        cre = jnp.where(row == r, bc(pows[e - 1][0]), cre)
        cim = jnp.where(row == r, bc(pows[e - 1][1]), cim)
    return levels, (cre, cim)


def _scan_tile(re_ref, im_ref, car_ref, ar, ai, n_rows, reverse):
    width = re_ref.shape[1]
    n_blk = n_rows // SUBLANES
    lane_w = 2 * LANES
    for c0 in range(0, width, lane_w):
        cols = slice(c0, c0 + lane_w)
        levels, (pr, pi) = _scan_tables(ar[:, cols], ai[:, cols], lane_w, reverse)

        def blk(i, carry, cols=cols, levels=levels, pr=pr, pi=pi):
            cr, ci = carry
            rb = (n_blk - 1 - i) if reverse else i
            off = pl.multiple_of(rb * SUBLANES, SUBLANES)
            xr = re_ref[pl.ds(off, SUBLANES), cols]
            xi = im_ref[pl.ds(off, SUBLANES), cols]
            for (lr, li), k in zip(levels, (1, 2, 4)):
                sh = (SUBLANES - k) if reverse else k
                sr = pltpu.roll(xr, sh, 0)
                si = pltpu.roll(xi, sh, 0)
                xr, xi = xr + (lr * sr - li * si), xi + (lr * si + li * sr)
            xr = xr + (pr * cr - pi * ci)
            xi = xi + (pr * ci + pi * cr)
            re_ref[pl.ds(off, SUBLANES), cols] = xr
            im_ref[pl.ds(off, SUBLANES), cols] = xi
            edge = slice(0, 1) if reverse else slice(SUBLANES - 1, SUBLANES)
            return xr[edge, :], xi[edge, :]

        cr, ci = lax.fori_loop(0, n_blk, blk, (car_ref[0:1, cols], car_ref[1:2, cols]))
        car_ref[0:1, cols] = cr
        car_ref[1:2, cols] = ci


def _ssm_fwd(u, bb_re, bb_im, c_re_t, c_imn_t, d_row, ab_re, ab_im, n_seq, seq):
    n = u.shape[0]
    tt = SCAN_TILE
    nt = seq // tt

    def body(u_ref, bbre, bbim, cre, cimn, d_ref, are, aim, sre_ref, sim_ref, y_ref, car_ref):
        @pl.when(pl.program_id(2) == 0)
        def _():
            car_ref[...] = jnp.zeros_like(car_ref)

        uu = u_ref[...]
        ub = uu.astype(BF16)
        sre_ref[...] = _dot(ub, bbre[0])
        sim_ref[...] = _dot(ub, bbim[0])
        _scan_tile(sre_ref, sim_ref, car_ref, are[...], aim[...], tt, reverse=False)
        y_ref[...] = (_dot(sre_ref[...].astype(BF16), cre[0]) + _dot(sim_ref[...].astype(BF16), cimn[0])
                      + d_ref[...] * uu)

    tok = lambda j, b, t: (b * nt + t, j)
    blk3 = lambda j, b, t: (j, 0, 0)
    row = lambda j, b, t: (0, j)
    return _pcall(
        body, name="ssm_fwd", grid=(N_JBLK, n_seq, nt),
        out_shape=(jax.ShapeDtypeStruct((n, N_JBLK * JB_ST), F32), jax.ShapeDtypeStruct((n, N_JBLK * JB_ST), F32),
                   jax.ShapeDtypeStruct((n, SSM_W), F32)),
        in_specs=[pl.BlockSpec((tt, JB_CH), tok),
                  pl.BlockSpec((1, JB_CH, JB_ST), blk3), pl.BlockSpec((1, JB_CH, JB_ST), blk3),
                  pl.BlockSpec((1, JB_ST, JB_CH), blk3), pl.BlockSpec((1, JB_ST, JB_CH), blk3),
                  pl.BlockSpec((1, JB_CH), row), pl.BlockSpec((1, JB_ST), row), pl.BlockSpec((1, JB_ST), row)],
        out_specs=(pl.BlockSpec((tt, JB_ST), tok), pl.BlockSpec((tt, JB_ST), tok), pl.BlockSpec((tt, JB_CH), tok)),
        scratch_shapes=[pltpu.VMEM((SUBLANES, JB_ST), F32)],
        compiler_params=_params(3),
    )(u, bb_re, bb_im, c_re_t, c_imn_t, d_row, ab_re, ab_im)


def _ssm_bwd(dy, u, s_re, s_im, bb_re, bb_im, c_re_t, c_imn_t, d_row, ab_re, ab_im, n_seq, seq):
    n = u.shape[0]
    tt = SCAN_TILE
    nt = seq // tt
    rows8 = tt // SUBLANES

    def body(dy_ref, u_ref, sre_ref, sim_ref, pre_ref, pim_ref, bbre, bbim, cre, cimn, d_ref, are, aim,
             du_ref, dcre_ref, dcim_ref, dbbre_ref, dbbim_ref, dare_ref, daim_ref, dd_ref,
             lre_ref, lim_ref, car_ref):
        b = pl.program_id(1)
        tr = pl.program_id(2)

        @pl.when(tr == 0)
        def _():
            car_ref[...] = jnp.zeros_like(car_ref)

        @pl.when((b == 0) & (tr == 0))
        def _():
            for r in (dcre_ref, dcim_ref, dbbre_ref, dbbim_ref, dare_ref, daim_ref, dd_ref):
                r[...] = jnp.zeros_like(r)

        dyv = dy_ref[...]
        dyb = dyv.astype(BF16)
        lre_ref[...] = _dot_nt(dyb, cre[0])
        lim_ref[...] = _dot_nt(dyb, cimn[0])
        _scan_tile(lre_ref, lim_ref, car_ref, are[...], -aim[...], tt, reverse=True)
        lr = lre_ref[...]
        li = lim_ref[...]
        lrb = lr.astype(BF16)
        lib = li.astype(BF16)
        uu = u_ref[...]
        ub = uu.astype(BF16)
        du_ref[...] = d_ref[...] * dyv + _dot_nt(lrb, bbre[0]) + _dot_nt(lib, bbim[0])
        dbbre_ref[0] += _dot_tn(ub, lrb)
        dbbim_ref[0] += _dot_tn(ub, lib)
        sr = sre_ref[...]
        si = sim_ref[...]
        dcre_ref[0] += _dot_tn(dyb, sr.astype(BF16))
        dcim_ref[0] += _dot_tn(dyb, si.astype(BF16))
        first = tr == nt - 1
        hr = jnp.where(first, 0.0, pre_ref[...])
        hi = jnp.where(first, 0.0, pim_ref[...])
        spr = _shift_down(sr, hr, 1)
        spi = _shift_down(si, hi, 1)
        dare_ref[...] += jnp.sum(lr * spr + li * spi, axis=0, keepdims=True)
        daim_ref[...] += jnp.sum(li * spr - lr * spi, axis=0, keepdims=True)
        dd_ref[...] += jnp.sum(dyv * uu, axis=0, keepdims=True)

    tok = lambda j, b, t: (b * nt + (nt - 1 - t), j)
    halo = lambda j, b, t: (jnp.maximum((b * nt + (nt - 1 - t)) * rows8 - 1, 0), j)
    blk3 = lambda j, b, t: (j, 0, 0)
    row = lambda j, b, t: (0, j)
    acc = jax.ShapeDtypeStruct((N_JBLK, JB_CH, JB_ST), F32)
    return _pcall(
        body, name="ssm_bwd", grid=(N_JBLK, n_seq, nt),
        out_shape=(jax.ShapeDtypeStruct((n, SSM_W), F32), acc, acc, acc, acc,
                   jax.ShapeDtypeStruct((1, N_JBLK * JB_ST), F32), jax.ShapeDtypeStruct((1, N_JBLK * JB_ST), F32),
                   jax.ShapeDtypeStruct((1, SSM_W), F32)),
        in_specs=[pl.BlockSpec((tt, JB_CH), tok), pl.BlockSpec((tt, JB_CH), tok),
                  pl.BlockSpec((tt, JB_ST), tok), pl.BlockSpec((tt, JB_ST), tok),
                  pl.BlockSpec((SUBLANES, JB_ST), halo), pl.BlockSpec((SUBLANES, JB_ST), halo),
                  pl.BlockSpec((1, JB_CH, JB_ST), blk3), pl.BlockSpec((1, JB_CH, JB_ST), blk3),
                  pl.BlockSpec((1, JB_ST, JB_CH), blk3), pl.BlockSpec((1, JB_ST, JB_CH), blk3),
                  pl.BlockSpec((1, JB_CH), row), pl.BlockSpec((1, JB_ST), row), pl.BlockSpec((1, JB_ST), row)],
        out_specs=(pl.BlockSpec((tt, JB_CH), tok),
                   pl.BlockSpec((1, JB_CH, JB_ST), blk3), pl.BlockSpec((1, JB_CH, JB_ST), blk3),
                   pl.BlockSpec((1, JB_CH, JB_ST), blk3), pl.BlockSpec((1, JB_CH, JB_ST), blk3),
                   pl.BlockSpec((1, JB_ST), row), pl.BlockSpec((1, JB_ST), row), pl.BlockSpec((1, JB_CH), row)),
        scratch_shapes=[pltpu.VMEM((tt, JB_ST), F32), pltpu.VMEM((tt, JB_ST), F32),
                        pltpu.VMEM((SUBLANES, JB_ST), F32)],
        compiler_params=_params(3),
    )(dy, u, s_re, s_im, s_re, s_im, bb_re, bb_im, c_re_t, c_imn_t, d_row, ab_re, ab_im)


def _mix(x2, tgt2, y, zs, h, bc, cc, zc, gf, b_glu, conv8, w_glu_f, w_out_f, seq):
    n = x2.shape[0]
    tm = TOK_TILE
    tiles_per_seq = seq // tm
    rows8 = tm // SUBLANES

    def body(x_ref, t_ref, y_ref, zs_ref, h_ref, bc_ref, cc_ref, zc_ref, hp_ref, ccp_ref,
             gf_ref, bg_ref, cw_ref, wg_ref, wo_ref,
             dh2_ref, dy_ref, dzs_ref, dbc_ref, dzc_ref, dyc_ref,
             dwo_ref, dwg_ref, loss_ref, dgf_ref, dbg_ref, dcw_ref):
        i = pl.program_id(0)

        @pl.when(i == 0)
        def _():
            for r in (dwo_ref, dwg_ref, loss_ref, dgf_ref, dbg_ref, dcw_ref):
                r[...] = jnp.zeros_like(r)

        yv = y_ref[...]
        y1, dgelu = _gelu_and_grad(yv)
        y1b = y1.astype(BF16)
        gate = _sigmoid(_dot(y1b, wg_ref[...]) + bg_ref[...])
        y2 = y1 * gate
        szs, dszs = _silu_and_grad(zs_ref[...])
        yssm = y2 * szs
        hv = h_ref[...]
        ccv = cc_ref[...]
        bcv = bc_ref[...]
        v = ccv * hv
        first = (i % tiles_per_seq) == 0
        vhalo = jnp.where(first, 0.0, ccp_ref[...] * hp_ref[...])
        v1 = _shift_down(v, vhalo, 1)
        v2 = _shift_down(v, vhalo, 2)
        w0 = cw_ref[0:1, :]
        w1 = cw_ref[1:2, :]
        w2 = cw_ref[2:3, :]
        yc = w0 * v2 + w1 * v1 + w2 * v
        szc, dszc = _silu_and_grad(zc_ref[...])
        yconv = (bcv * yc) * szc
        ysb = yssm.astype(BF16)
        ycb = yconv.astype(BF16)
        h2 = x_ref[...] + _dot(ysb, wo_ref[0:SSM_W, :]) + _dot(ycb, wo_ref[SSM_W:, :])
        r2 = lax.rsqrt(jnp.mean(h2 * h2, axis=-1, keepdims=True) + EPS)
        hn = h2 * r2
        gfv = gf_ref[...]
        err = hn * gfv - t_ref[...]
        loss_ref[...] += 0.5 * jnp.sum(jnp.mean(err * err, axis=-1, keepdims=True))
        dout = err * (1.0 / D_MODEL)
        dgf_ref[...] += jnp.sum(dout * hn, axis=0, keepdims=True)
        dn = dout * gfv
        dh2 = r2 * (dn - hn * jnp.mean(dn * hn, axis=-1, keepdims=True))
        dh2_ref[...] = dh2
        dh2b = dh2.astype(BF16)
        dwo_ref[0:SSM_W, :] += _dot_tn(ysb, dh2b)
        dwo_ref[SSM_W:, :] += _dot_tn(ycb, dh2b)
        dyssm = _dot_nt(dh2b, wo_ref[0:SSM_W, :])
        dyconv = _dot_nt(dh2b, wo_ref[SSM_W:, :])
        dy2 = dyssm * szs
        dzs_ref[...] = dyssm * y2 * dszs
        dgp = dy2 * y1 * (gate * (1.0 - gate))
        dgpb = dgp.astype(BF16)
        dy1 = dy2 * gate + _dot_nt(dgpb, wg_ref[...])
        dwg_ref[...] += _dot_tn(y1b, dgpb)
        dbg_ref[...] += jnp.sum(dgp, axis=0, keepdims=True)
        dy_ref[...] = dy1 * dgelu
        dbc_ref[...] = dyconv * yc * szc
        dyc = dyconv * bcv * szc
        dyc_ref[...] = dyc
        dzc_ref[...] = dyconv * bcv * yc * dszc
        dcw_ref[0:1, :] += jnp.sum(dyc * v2, axis=0, keepdims=True)
        dcw_ref[1:2, :] += jnp.sum(dyc * v1, axis=0, keepdims=True)
        dcw_ref[2:3, :] += jnp.sum(dyc * v, axis=0, keepdims=True)

    tile_d = pl.BlockSpec((tm, D_MODEL), lambda i: (i, 0))
    tile_s = pl.BlockSpec((tm, SSM_W), lambda i: (i, 0))
    halo = pl.BlockSpec((SUBLANES, SSM_W), lambda i: (jnp.maximum(i * rows8 - 1, 0), 0))
    const = lambda shape: pl.BlockSpec(shape, lambda i: (0,) * len(shape))
    seg = jax.ShapeDtypeStruct((n, SSM_W), F32)
    return _pcall(
        body, name="mix", grid=(n // tm,),
        out_shape=(jax.ShapeDtypeStruct((n, D_MODEL), F32), seg, seg, seg, seg, seg,
                   jax.ShapeDtypeStruct((D_MODEL, D_MODEL), F32), jax.ShapeDtypeStruct((SSM_W, SSM_W), F32),
                   jax.ShapeDtypeStruct((SUBLANES, LANES), F32), jax.ShapeDtypeStruct((1, D_MODEL), F32),
                   jax.ShapeDtypeStruct((1, SSM_W), F32), jax.ShapeDtypeStruct((SUBLANES, CONV_W), F32)),
        in_specs=[tile_d, tile_d, tile_s, tile_s, tile_s, tile_s, tile_s, tile_s, halo, halo,
                  const((1, D_MODEL)), const((1, SSM_W)), const((SUBLANES, CONV_W)),
                  const((SSM_W, SSM_W)), const((D_MODEL, D_MODEL))],
        out_specs=(tile_d, tile_s, tile_s, tile_s, tile_s, tile_s,
                   const((D_MODEL, D_MODEL)), const((SSM_W, SSM_W)), const((SUBLANES, LANES)),
                   const((1, D_MODEL)), const((1, SSM_W)), const((SUBLANES, CONV_W))),
        compiler_params=_params(1),
    )(x2, tgt2, y, zs, h, bc, cc, zc, h, cc, gf, b_glu, conv8, w_glu_f, w_out_f)


def _in_bwd(x2, dh2, du, dzs, dyc, h, cc, dbc, dzc, g1, conv8, w_full, seq):
    n = x2.shape[0]
    tm = TOK_TILE
    tiles_per_seq = seq // tm
    rows8 = tm // SUBLANES
    n_blk8 = n // SUBLANES

    def body(x_ref, dh2_ref, du_ref, dzs_ref, dyc_ref, dycn_ref, h_ref, cc_ref, dbc_ref, dzc_ref,
             g_ref, cw_ref, w_ref, gx_ref, dp_ref, dg_ref):
        i = pl.program_id(0)

        @pl.when(i == 0)
        def _():
            dg_ref[...] = jnp.zeros_like(dg_ref)

        dyc = dyc_ref[...]
        last = (i % tiles_per_seq) == tiles_per_seq - 1
        nhalo = jnp.where(last, 0.0, dycn_ref[...])
        dv = (cw_ref[2:3, :] * dyc + cw_ref[1:2, :] * _shift_up(dyc, nhalo, 1)
              + cw_ref[0:1, :] * _shift_up(dyc, nhalo, 2))
        parts = (du_ref[...], dzs_ref[...], dv * cc_ref[...], dbc_ref[...], dv * h_ref[...], dzc_ref[...])
        dxn = jnp.zeros((tm, D_MODEL), F32)
        for k, p in enumerate(parts):
            pb = p.astype(BF16)
            dp_ref[:, k * SSM_W:(k + 1) * SSM_W] = pb
            dxn = dxn + _dot_nt(pb, w_ref[:, k * SSM_W:(k + 1) * SSM_W])
        x = x_ref[...]
        r = lax.rsqrt(jnp.mean(x * x, axis=-1, keepdims=True) + EPS)
        xh = x * r
        dg_ref[...] += jnp.sum(dxn * xh, axis=0, keepdims=True)
        dn = dxn * g_ref[...]
        gx_ref[...] = dh2_ref[...] + r * (dn - xh * jnp.mean(dn * xh, axis=-1, keepdims=True))

    tile_d = pl.BlockSpec((tm, D_MODEL), lambda i: (i, 0))
    tile_s = pl.BlockSpec((tm, SSM_W), lambda i: (i, 0))
    nhalo = pl.BlockSpec((SUBLANES, SSM_W), lambda i: (jnp.minimum((i + 1) * rows8, n_blk8 - 1), 0))
    const = lambda shape: pl.BlockSpec(shape, lambda i: (0,) * len(shape))
    return _pcall(
        body, name="in_bwd", grid=(n // tm,),
        out_shape=(jax.ShapeDtypeStruct((n, D_MODEL), F32), jax.ShapeDtypeStruct((n, IN_COLS), BF16),
                   jax.ShapeDtypeStruct((1, D_MODEL), F32)),
        in_specs=[tile_d, tile_d, tile_s, tile_s, tile_s, nhalo, tile_s, tile_s, tile_s, tile_s,
                  const((1, D_MODEL)), const((SUBLANES, CONV_W)), const((D_MODEL, IN_COLS))],
        out_specs=(tile_d, pl.BlockSpec((tm, IN_COLS), lambda i: (i, 0)), const((1, D_MODEL))),
        compiler_params=_params(1),
    )(x2, dh2, du, dzs, dyc, dyc, h, cc, dbc, dzc, g1, conv8, w_full)


def _dw_in(xn, dproj):
    n = xn.shape[0]
    tk = 512

    def body(xn_ref, dp_ref, o_ref):
        @pl.when(pl.program_id(1) == 0)
        def _():
            o_ref[...] = jnp.zeros_like(o_ref)

        o_ref[0] += _dot_tn(xn_ref[...], dp_ref[...])

    return _pcall(
        body, name="dw_in", grid=(N_DEV, n // tk),
        out_shape=jax.ShapeDtypeStruct((N_DEV, D_MODEL, COLS_PER_DEV), F32),
        in_specs=[pl.BlockSpec((tk, D_MODEL), lambda p, k: (k, 0)),
                  pl.BlockSpec((tk, COLS_PER_DEV), lambda p, k: (k, p))],
        out_specs=pl.BlockSpec((1, D_MODEL, COLS_PER_DEV), lambda p, k: (p, 0, 0)),
        compiler_params=_params(2),
    )(xn, dproj)


def _reduce_adam(recv, w, m, v, name, row_tile):
    rows, cols = w.shape

    def body(r_ref, w_ref, m_ref, v_ref, g_ref, d_ref, nm_ref, nv_ref):
        g = r_ref[0]
        for s in range(1, N_DEV):
            g = g + r_ref[s]
        g_ref[...] = g
        m_new = ADAM_B1 * m_ref[...] + (1.0 - ADAM_B1) * g
        v_new = ADAM_B2 * v_ref[...] + (1.0 - ADAM_B2) * (g * g)
        m_hat = m_new / (1.0 - ADAM_B1 ** ADAM_STEP)
        v_hat = v_new / (1.0 - ADAM_B2 ** ADAM_STEP)
        d_ref[...] = -ADAM_LR * (m_hat / (jnp.sqrt(v_hat) + ADAM_EPS) + ADAM_WD * w_ref[...])
        nm_ref[...] = m_new
        nv_ref[...] = v_new

    tile = pl.BlockSpec((row_tile, cols), lambda i: (i, 0))
    shp = jax.ShapeDtypeStruct((rows, cols), F32)
    return _pcall(
        body, name=name, grid=(rows // row_tile,),
        out_shape=(shp,) * 4,
        in_specs=[pl.BlockSpec((N_DEV, row_tile, cols), lambda i: (0, i, 0)), tile, tile, tile],
        out_specs=(tile,) * 4,
        compiler_params=_params(1),
    )(recv, w, m, v)


_SMALL = (("norm_gain", D_MODEL), ("final_norm_gain", D_MODEL), ("b_glu", SSM_W),
          ("ssm_a_re", N_GROUPS * STATE), ("ssm_a_im", N_GROUPS * STATE), ("ssm_log_dt", N_GROUPS),
          ("ssm_b_re", N_GROUPS * STATE * GROUP), ("ssm_b_im", N_GROUPS * STATE * GROUP),
          ("ssm_c_re", N_GROUPS * STATE * GROUP), ("ssm_c_im", N_GROUPS * STATE * GROUP),
          ("ssm_d", N_GROUPS * GROUP), ("conv_w", 3 * CONV_W))
_PACK_UNIT = SUBLANES * LANES


def _pack_small(parts):
    rows = []
    for name, size in _SMALL:
        flat = parts[name].reshape(-1).astype(F32)
        padded = -(-size // _PACK_UNIT) * _PACK_UNIT
        rows.append(jnp.pad(flat, (0, padded - size)).reshape(-1, LANES))
    return jnp.concatenate(rows, axis=0)


def _unpack_small(packed):
    out, r0 = {}, 0
    for name, size in _SMALL:
        nrows = -(-size // _PACK_UNIT) * SUBLANES
        out[name] = packed[r0:r0 + nrows].reshape(-1)[:size]
        r0 += nrows
    return out


def _block_diag(m4):
    eye = jnp.eye(SUBLANES, dtype=m4.dtype)
    j, g, a, b = m4.shape
    return jnp.einsum("jgab,gk->jgakb", m4, eye).reshape(j, g * a, g * b)


def _block_diag_extract(dense, a, b):
    d5 = dense.reshape(N_JBLK, SUBLANES, a, SUBLANES, b)
    return jnp.stack([d5[:, g, :, g, :] for g in range(SUBLANES)], axis=1)


def kernel(x, norm_gain, w_in, ssm_a_re, ssm_a_im, ssm_log_dt, ssm_b_re, ssm_b_im, ssm_c_re, ssm_c_im, ssm_d, w_glu, b_glu, conv_w, w_out, final_norm_gain, loss_target, m_norm_gain, m_w_in, m_ssm_a_re, m_ssm_a_im, m_ssm_log_dt, m_ssm_b_re, m_ssm_b_im, m_ssm_c_re, m_ssm_c_im, m_ssm_d, m_w_glu, m_b_glu, m_conv_w, m_w_out, m_final_norm_gain, v_norm_gain, v_w_in, v_ssm_a_re, v_ssm_a_im, v_ssm_log_dt, v_ssm_b_re, v_ssm_b_im, v_ssm_c_re, v_ssm_c_im, v_ssm_d, v_w_glu, v_b_glu, v_conv_w, v_w_out, v_final_norm_gain):
    n_seq, seq, _ = x.shape
    n = n_seq * seq
    me = 4 * lax.axis_index("x") + 2 * lax.axis_index("y") + lax.axis_index("c")

    conv_p = jnp.pad(conv_w[0], ((0, SUBLANES - 3), (0, LANES - CONV_COLS_PER_DEV)))
    w_in_f, w_out_f, w_glu_f, conv_all = _gather_weights(
        w_in[0].astype(BF16), w_out[0].astype(BF16), w_glu[0].astype(BF16), conv_p)
    conv8 = jnp.transpose(conv_all[:, :, :CONV_COLS_PER_DEV], (1, 0, 2)).reshape(SUBLANES, CONV_W)

    rep = lambda a: jnp.repeat(a[0], GROUP, axis=1)
    a_re_r, a_im_r = rep(ssm_a_re), rep(ssm_a_im)
    log_dt = ssm_log_dt[0].reshape(N_GROUPS, 1)
    b_re2 = ssm_b_re[0].reshape(N_GROUPS, STATE * GROUP)
    b_im2 = ssm_b_im[0].reshape(N_GROUPS, STATE * GROUP)
    ab_re_r, ab_im_r, bb_re2, bb_im2 = _ssm_disc(a_re_r, a_im_r, log_dt, b_re2, b_im2)
    ab_re = ab_re_r[:, ::GROUP].reshape(1, N_GROUPS * STATE)
    ab_im = ab_im_r[:, ::GROUP].reshape(1, N_GROUPS * STATE)

    def bb_mat(bb2):
        t = jnp.transpose(bb2.reshape(N_JBLK, SUBLANES, STATE, GROUP), (0, 1, 3, 2))
        return _block_diag(t).astype(BF16)

    def c_mat(c3, sign):
        t = jnp.transpose(c3.reshape(N_JBLK, SUBLANES, GROUP, STATE), (0, 1, 3, 2))
        return _block_diag(sign * t).astype(BF16)

    bb_re_m, bb_im_m = bb_mat(bb_re2), bb_mat(bb_im2)
    c_re_m, c_imn_m = c_mat(ssm_c_re[0], 1.0), c_mat(ssm_c_im[0], -1.0)
    d_row = ssm_d[0].reshape(1, SSM_W)

    x2 = x.reshape(n, D_MODEL)
    tgt2 = loss_target.reshape(n, D_MODEL)
    xn, u, zs, h, bc, cc, zc = _in_proj(x2, norm_gain, w_in_f)
    s_re, s_im, y = _ssm_fwd(u, bb_re_m, bb_im_m, c_re_m, c_imn_m, d_row, ab_re, ab_im, n_seq, seq)
    (dh2, dy, dzs, dbc, dzc, dyc, dw_out, dw_glu, loss_t, dgf, dbg, dcw) = _mix(
        x2, tgt2, y, zs, h, bc, cc, zc, final_norm_gain.reshape(1, D_MODEL), b_glu, conv8,
        w_glu_f, w_out_f, seq)

    du, dc_re_d, dc_im_d, dbb_re_d, dbb_im_d, dab_re, dab_im, dd = _ssm_bwd(
        dy, u, s_re, s_im, bb_re_m, bb_im_m, c_re_m, c_imn_m, d_row, ab_re, ab_im, n_seq, seq)
    g_c_re = _block_diag_extract(dc_re_d, GROUP, STATE).reshape(N_GROUPS, GROUP, STATE)
    g_c_im = -_block_diag_extract(dc_im_d, GROUP, STATE).reshape(N_GROUPS, GROUP, STATE)

    def bb_grad(dense):
        t = _block_diag_extract(dense, GROUP, STATE)
        return jnp.transpose(t, (0, 1, 3, 2)).reshape(N_GROUPS, STATE * GROUP)

    def ab_grad(row):
        z = jnp.zeros((N_GROUPS, STATE, GROUP), F32)
        return z.at[:, :, 0].set(row.reshape(N_GROUPS, STATE)).reshape(N_GROUPS, STATE * GROUP)

    g_are_r, g_aim_r, g_ldt, g_bre2, g_bim2 = _ssm_disc_bwd(
        a_re_r, a_im_r, log_dt, b_re2, b_im2, ab_grad(dab_re), ab_grad(dab_im),
        bb_grad(dbb_re_d), bb_grad(dbb_im_d))
    grad_x2, dproj, dg1 = _in_bwd(x2, dh2, du, dzs, dyc, h, cc, dbc, dzc, norm_gain, conv8, w_in_f, seq)
    dw_in = _dw_in(xn, dproj)

    small_grads = {"norm_gain": dg1, "final_norm_gain": dgf, "b_glu": dbg,
                   "ssm_a_re": g_are_r[:, ::GROUP], "ssm_a_im": g_aim_r[:, ::GROUP], "ssm_log_dt": g_ldt,
                   "ssm_b_re": g_bre2, "ssm_b_im": g_bim2, "ssm_c_re": g_c_re, "ssm_c_im": g_c_im,
                   "ssm_d": dd, "conv_w": dcw[0:3]}
    r_in, r_out, r_glu, r_small = _exchange(
        dw_in, dw_out.reshape(N_DEV, OUT_ROWS_PER_DEV, D_MODEL), dw_glu.reshape(N_DEV, GLU_ROWS_PER_DEV, SSM_W),
        _pack_small(small_grads))

    def conv_full(shard):
        return lax.dynamic_update_slice(jnp.zeros((3, CONV_W), F32), shard[0], (0, me * CONV_COLS_PER_DEV))

    def small_params(prefix, ns):
        d = {k: ns[prefix + k] for k, _ in _SMALL if k != "conv_w"}
        d["conv_w"] = conv_full(ns[prefix + "conv_w"])
        return _pack_small(d)

    ns = dict(norm_gain=norm_gain, final_norm_gain=final_norm_gain, b_glu=b_glu, ssm_a_re=ssm_a_re,
              ssm_a_im=ssm_a_im, ssm_log_dt=ssm_log_dt, ssm_b_re=ssm_b_re, ssm_b_im=ssm_b_im,
              ssm_c_re=ssm_c_re, ssm_c_im=ssm_c_im, ssm_d=ssm_d, conv_w=conv_w,
              m_norm_gain=m_norm_gain, m_final_norm_gain=m_final_norm_gain, m_b_glu=m_b_glu,
              m_ssm_a_re=m_ssm_a_re, m_ssm_a_im=m_ssm_a_im, m_ssm_log_dt=m_ssm_log_dt,
              m_ssm_b_re=m_ssm_b_re, m_ssm_b_im=m_ssm_b_im, m_ssm_c_re=m_ssm_c_re, m_ssm_c_im=m_ssm_c_im,
              m_ssm_d=m_ssm_d, m_conv_w=m_conv_w,
              v_norm_gain=v_norm_gain, v_final_norm_gain=v_final_norm_gain, v_b_glu=v_b_glu,
              v_ssm_a_re=v_ssm_a_re, v_ssm_a_im=v_ssm_a_im, v_ssm_log_dt=v_ssm_log_dt,
              v_ssm_b_re=v_ssm_b_re, v_ssm_b_im=v_ssm_b_im, v_ssm_c_re=v_ssm_c_re, v_ssm_c_im=v_ssm_c_im,
              v_ssm_d=v_ssm_d, v_conv_w=v_conv_w)
    w_small, m_small, v_small = small_params("", ns), small_params("m_", ns), small_params("v_", ns)

    res_in = _reduce_adam(r_in, w_in[0], m_w_in[0], v_w_in[0], "reduce_adam_w_in", 256)
    res_out = _reduce_adam(r_out, w_out[0], m_w_out[0], v_w_out[0], "reduce_adam_w_out", OUT_ROWS_PER_DEV)
    res_glu = _reduce_adam(r_glu, w_glu[0], m_w_glu[0], v_w_glu[0], "reduce_adam_w_glu", GLU_ROWS_PER_DEV)
    res_small = _reduce_adam(r_small, w_small, m_small, v_small, "reduce_adam_small", w_small.shape[0])
    small = [_unpack_small(r) for r in res_small]

    loss = lax.psum(loss_t[0, 0], ("x", "y", "c"))

    shapes = dict(norm_gain=(1, D_MODEL), w_in=None, ssm_a_re=(1, N_GROUPS, STATE), ssm_a_im=(1, N_GROUPS, STATE),
                  ssm_log_dt=(1, N_GROUPS), ssm_b_re=(1, N_GROUPS, STATE, GROUP), ssm_b_im=(1, N_GROUPS, STATE, GROUP),
                  ssm_c_re=(1, N_GROUPS, GROUP, STATE), ssm_c_im=(1, N_GROUPS, GROUP, STATE),
                  ssm_d=(1, N_GROUPS, GROUP), w_glu=None, b_glu=(1, SSM_W), conv_w=None, w_out=None,
                  final_norm_gain=(D_MODEL,))
    big = dict(w_in=res_in, w_glu=res_glu, w_out=res_out)

    def leaf(kind, name):
        if name in big:
            return big[name][kind][None]
        if name == "conv_w":
            full = small[kind]["conv_w"].reshape(3, CONV_W)
            return lax.dynamic_slice(full, (0, me * CONV_COLS_PER_DEV), (3, CONV_COLS_PER_DEV))[None]
        return small[kind][name].reshape(shapes[name])

    order = ["norm_gain", "w_in", "ssm_a_re", "ssm_a_im", "ssm_log_dt", "ssm_b_re", "ssm_b_im", "ssm_c_re",
             "ssm_c_im", "ssm_d", "w_glu", "b_glu", "conv_w", "w_out", "final_norm_gain"]
    outs = [loss, grad_x2.reshape(x.shape)]
    for kind in range(4):
        outs += [leaf(kind, name) for name in order]
    return tuple(outs)
```

```python
import functools
import math

import jax
import jax.numpy as jnp
from jax import lax
from jax.experimental import pallas as pl
from jax.experimental.pallas import tpu as pltpu

F32 = jnp.float32
BF16 = jnp.bfloat16

N_DEV = 8
D_MODEL = 1024
SSM_W = 512
CONV_W = 512
N_GROUPS = 32
GROUP = 16
STATE = 64
IN_COLS = 3072
COLS_PER_DEV = IN_COLS // N_DEV
OUT_ROWS_PER_DEV = D_MODEL // N_DEV
GLU_ROWS_PER_DEV = SSM_W // N_DEV
CONV_COLS_PER_DEV = CONV_W // N_DEV
EPS = 1e-6

N_JBLK = 4
JB_CH = SSM_W // N_JBLK
JB_ST = N_GROUPS * STATE // N_JBLK

ADAM_LR = 0.001
ADAM_B1 = 0.9
ADAM_B2 = 0.999
ADAM_EPS = 1e-08
ADAM_WD = 0.01
ADAM_STEP = 10

SUBLANES = 8
LANES = 128
VMEM_LIMIT = 48 * 1024 * 1024
TOK_TILE = 256
SCAN_TILE = 256

MESH = pl.DeviceIdType.MESH
HBM_SPEC = pl.BlockSpec(memory_space=pltpu.HBM)


def _pcall(body, **kw):
    return pl.pallas_call(body, **kw)


def _params(n_grid):
    return pltpu.CompilerParams(dimension_semantics=("arbitrary",) * n_grid,
                                vmem_limit_bytes=VMEM_LIMIT)


def _dot(a, b):
    return jnp.dot(a, b, preferred_element_type=F32)


def _dot_nt(a, b):
    return lax.dot_general(a, b, (((1,), (1,)), ((), ())), preferred_element_type=F32)


def _dot_tn(a, b):
    return lax.dot_general(a, b, (((0,), (0,)), ((), ())), preferred_element_type=F32)


def _sigmoid(z):
    return 1.0 / (1.0 + jnp.exp(-z))


_GELU_C = math.sqrt(2.0 / math.pi)


def _gelu_and_grad(y):
    inner = _GELU_C * (y + 0.044715 * (y * y * y))
    t = jnp.tanh(inner)
    g = 0.5 * y * (1.0 + t)
    dg = 0.5 * (1.0 + t) + 0.5 * y * (1.0 - t * t) * (_GELU_C * (1.0 + 3.0 * 0.044715 * (y * y)))
    return g, dg


def _silu_and_grad(z):
    s = _sigmoid(z)
    return z * s, s * (1.0 + z * (1.0 - s))


def _shift_down(v, halo, k):
    rolled = pltpu.roll(v, k, 0)
    row = lax.broadcasted_iota(jnp.int32, v.shape, 0)
    for r in range(k):
        rolled = jnp.where(row == r, halo[SUBLANES - k + r:SUBLANES - k + r + 1, :], rolled)
    return rolled


def _shift_up(v, halo, k):
    n = v.shape[0]
    rolled = pltpu.roll(v, n - k, 0)
    row = lax.broadcasted_iota(jnp.int32, v.shape, 0)
    for r in range(k):
        rolled = jnp.where(row == n - k + r, halo[r:r + 1, :], rolled)
    return rolled


def _mesh_pos():
    return lax.axis_index("x"), lax.axis_index("y"), lax.axis_index("c")


def _gather_weights(w_in_b, w_out_b, w_glu_b, conv_p):
    n_arr = 4

    def body(win, wout, wglu, cw, o_in, o_out, o_glu, o_cw, send_sems, recv_sems, loc_sems):
        x, y, c = _mesh_pos()
        me, sib = (x, y, c), (x, y, 1 - c)
        chips = [(1 - x, y), (x, 1 - y), (1 - x, 1 - y)]
        own = [win, wout, wglu, cw]

        def slots(px, py, pc):
            k = 4 * px + 2 * py + pc
            return [o_in.at[:, pl.ds(pl.multiple_of(k * COLS_PER_DEV, LANES), COLS_PER_DEV)],
                    o_out.at[pl.ds(pl.multiple_of(k * OUT_ROWS_PER_DEV, OUT_ROWS_PER_DEV), OUT_ROWS_PER_DEV), :],
                    o_glu.at[pl.ds(pl.multiple_of(k * GLU_ROWS_PER_DEV, GLU_ROWS_PER_DEV), GLU_ROWS_PER_DEV), :],
                    o_cw.at[k]]

        def copies(kidx, block, to, from_own=False):
            dst = slots(*block)
            src = own if from_own else dst
            return [pltpu.make_async_remote_copy(
                src_ref=src[a], dst_ref=dst[a],
                send_sem=send_sems.at[kidx * n_arr + a], recv_sem=recv_sems.at[kidx * n_arr + a],
                device_id=to, device_id_type=MESH) for a in range(n_arr)]

        mine = [pltpu.make_async_copy(own[a], slots(*me)[a], loc_sems.at[a]) for a in range(n_arr)]
        for cp in mine:
            cp.start()
        first = copies(0, me, sib, from_own=True)
        for j, chip in enumerate(chips):
            first += copies(1 + j, me, (*chip, c), from_own=True)
        for cp in first:
            cp.start()
        passed = []
        for j, chip in enumerate(chips):
            for cp in copies(1 + j, (*chip, c), me):
                cp.wait_recv()
            fwd = copies(4 + j, (*chip, c), sib)
            for cp in fwd:
                cp.start()
            passed += fwd
        for cp in copies(0, sib, me):
            cp.wait_recv()
        for j, chip in enumerate(chips):
            for cp in copies(4 + j, (*chip, 1 - c), me):
                cp.wait_recv()
        for cp in first + passed:
            cp.wait_send()
        for cp in mine:
            cp.wait()

    return _pcall(
        body, name="gather_weights",
        out_shape=(jax.ShapeDtypeStruct((D_MODEL, IN_COLS), BF16),
                   jax.ShapeDtypeStruct((D_MODEL, D_MODEL), BF16),
                   jax.ShapeDtypeStruct((SSM_W, SSM_W), BF16),
                   jax.ShapeDtypeStruct((N_DEV, SUBLANES, LANES), F32)),
        in_specs=[HBM_SPEC] * n_arr, out_specs=(HBM_SPEC,) * n_arr,
        scratch_shapes=[pltpu.SemaphoreType.DMA((7 * n_arr,)), pltpu.SemaphoreType.DMA((7 * n_arr,)),
                        pltpu.SemaphoreType.DMA((n_arr,))],
    )(w_in_b, w_out_b, w_glu_b, conv_p)


def _direct_copies(srcs_for, out_refs, send_sems, recv_sems, loc_sems):
    x, y, c = _mesh_pos()
    me_id = 4 * x + 2 * y + c
    n_arr = len(out_refs)
    dsts = [r.at[me_id] for r in out_refs]
    own = srcs_for(me_id)
    mine = [pltpu.make_async_copy(own[a], dsts[a], loc_sems.at[a]) for a in range(n_arr)]
    sends = []
    for k in range(1, N_DEV):
        px, py, pc = x ^ ((k >> 2) & 1), y ^ ((k >> 1) & 1), c ^ (k & 1)
        src = srcs_for(4 * px + 2 * py + pc)
        for a in range(n_arr):
            sends.append(pltpu.make_async_remote_copy(
                src_ref=src[a], dst_ref=dsts[a],
                send_sem=send_sems.at[(k - 1) * n_arr + a], recv_sem=recv_sems.at[(k - 1) * n_arr + a],
                device_id=(px, py, pc), device_id_type=MESH))
    return mine, sends


class _TwoLevelGather:
    def __init__(self, src_ref, out_ref, send_sems, recv_sems, loc_sem):
        self.src, self.out = src_ref, out_ref
        self.send_sems, self.recv_sems, self.loc_sem = send_sems, recv_sems, loc_sem
        x, y, c = _mesh_pos()
        self.c = c
        self.me, self.sib = (x, y, c), (x, y, 1 - c)
        self.chips = [(1 - x, y), (x, 1 - y), (1 - x, 1 - y)]

    def _slot(self, px, py, pc):
        return self.out.at[4 * px + 2 * py + pc]

    def _copy(self, k, block, to, from_src=False):
        return pltpu.make_async_remote_copy(
            src_ref=self.src if from_src else self._slot(*block), dst_ref=self._slot(*block),
            send_sem=self.send_sems.at[k], recv_sem=self.recv_sems.at[k], device_id=to, device_id_type=MESH)

    def _local(self):
        return pltpu.make_async_copy(self.src, self._slot(*self.me), self.loc_sem)

    def start(self):
        self._local().start()
        self._copy(0, self.me, self.sib, True).start()
        for j, chip in enumerate(self.chips):
            self._copy(1 + j, self.me, (*chip, self.c), True).start()

    def forward(self):
        for j, chip in enumerate(self.chips):
            self._copy(1 + j, (*chip, self.c), self.me).wait_recv()
            self._copy(4 + j, (*chip, self.c), self.sib).start()

    def finish(self):
        self._copy(0, self.sib, self.me).wait_recv()
        for j, chip in enumerate(self.chips):
            self._copy(4 + j, (*chip, 1 - self.c), self.me).wait_recv()
        self._copy(0, self.me, self.sib, True).wait_send()
        for j, chip in enumerate(self.chips):
            self._copy(1 + j, self.me, (*chip, self.c), True).wait_send()
            self._copy(4 + j, (*chip, self.c), self.sib).wait_send()
        self._local().wait()


def _disc(a_re, a_im, log_dt, b_re, b_im):
    dt = jnp.exp(log_dt)
    mag = jnp.exp(a_re * dt)
    ab_re = mag * jnp.cos(a_im * dt)
    ab_im = mag * jnp.sin(a_im * dt)
    den = a_re * a_re + a_im * a_im
    p_re = ab_re - 1.0
    p_im = ab_im
    q_re = (p_re * a_re + p_im * a_im) / den
    q_im = (p_im * a_re - p_re * a_im) / den
    bb_re = q_re * b_re - q_im * b_im
    bb_im = q_re * b_im + q_im * b_re
    return ab_re, ab_im, bb_re, bb_im


def _ssm_disc(a_re_r, a_im_r, log_dt, b_re, b_im):
    def body(are, aim, ldt, bre, bim, o_abre, o_abim, o_bbre, o_bbim):
        outs = _disc(are[...], aim[...], ldt[...], bre[...], bim[...])
        for o, v in zip((o_abre, o_abim, o_bbre, o_bbim), outs):
            o[...] = v

    shp = jax.ShapeDtypeStruct(a_re_r.shape, F32)
    return _pcall(body, name="ssm_disc", out_shape=(shp,) * 4)(a_re_r, a_im_r, log_dt, b_re, b_im)


def _ssm_disc_bwd(a_re_r, a_im_r, log_dt, b_re, b_im, g_abre, g_abim, g_bbre, g_bbim):
    width = a_re_r.shape[1]

    def body(are, aim, ldt, bre, bim, gabre, gabim, gbbre, gbbim, o_are, o_aim, o_ldt, o_bre, o_bim):
        _, vjp = jax.vjp(_disc, are[...], aim[...], ldt[...], bre[...], bim[...])
        d_are, d_aim, d_ldt, d_bre, d_bim = vjp((gabre[...], gabim[...], gbbre[...], gbbim[...]))

        def group_sum(v):
            for k in (1, 2, 4, 8):
                v = v + pltpu.roll(v, width - k, 1)
            return v

        o_are[...] = group_sum(d_are)
        o_aim[...] = group_sum(d_aim)
        o_ldt[...] = d_ldt
        o_bre[...] = d_bre
        o_bim[...] = d_bim

    shp = jax.ShapeDtypeStruct(a_re_r.shape, F32)
    return _pcall(body, name="ssm_disc_bwd",
                  out_shape=(shp, shp, jax.ShapeDtypeStruct(log_dt.shape, F32), shp, shp),
                  )(a_re_r, a_im_r, log_dt, b_re, b_im, g_abre, g_abim, g_bbre, g_bbim)


def _in_proj(x2, g1, w_full):
    n = x2.shape[0]
    tm = TOK_TILE

    def body(x_ref, g_ref, w_ref, xn_ref, *outs):
        x = x_ref[...]
        r = lax.rsqrt(jnp.mean(x * x, axis=-1, keepdims=True) + EPS)
        xn = ((x * r) * g_ref[...]).astype(BF16)
        xn_ref[...] = xn
        for i, o in enumerate(outs):
            o[...] = _dot(xn, w_ref[:, i * SSM_W:(i + 1) * SSM_W])

    seg = jax.ShapeDtypeStruct((n, SSM_W), F32)
    seg_spec = pl.BlockSpec((tm, SSM_W), lambda i: (i, 0))
    return _pcall(
        body, name="in_proj", grid=(n // tm,),
        out_shape=(jax.ShapeDtypeStruct((n, D_MODEL), BF16),) + (seg,) * 6,
        in_specs=[pl.BlockSpec((tm, D_MODEL), lambda i: (i, 0)),
                  pl.BlockSpec((1, D_MODEL), lambda i: (0, 0)),
                  pl.BlockSpec((D_MODEL, IN_COLS), lambda i: (0, 0))],
        out_specs=(pl.BlockSpec((tm, D_MODEL), lambda i: (i, 0)),) + (seg_spec,) * 6,
        compiler_params=_params(1),
    )(x2, g1, w_full)


def _cmul(p, q):
    return p[0] * q[0] - p[1] * q[1], p[0] * q[1] + p[1] * q[0]


def _scan_tables(ar, ai, width, reverse):
    pows = [(ar, ai)]
    for _ in range(SUBLANES - 1):
        pows.append(_cmul(pows[-1], (ar, ai)))
    row = lax.broadcasted_iota(jnp.int32, (SUBLANES, width), 0)

    def bc(v):
        return jnp.broadcast_to(v, (SUBLANES, width))

    levels = []
    for k in (1, 2, 4):
        keep = (row <= SUBLANES - 1 - k) if reverse else (row >= k)
        levels.append((jnp.where(keep, bc(pows[k - 1][0]), 0.0), jnp.where(keep, bc(pows[k - 1][1]), 0.0)))
    cre = jnp.zeros((SUBLANES, width), F32)
    cim = jnp.zeros((SUBLANES, width), F32)
    for r in range(SUBLANES):
        e = (SUBLANES - r) if reverse else (r + 1)
        cre = jnp.where(row == r, bc(pows[e - 1][0]), cre)
        cim = jnp.where(row == r, bc(pows[e - 1][1]), cim)
    return levels, (cre, cim)


def _scan_tile(re_ref, im_ref, car_ref, ar, ai, n_rows, reverse):
    width = re_ref.shape[1]
    n_blk = n_rows // SUBLANES
    lane_w = 2 * LANES
    for c0 in range(0, width, lane_w):
        cols = slice(c0, c0 + lane_w)
        levels, (pr, pi) = _scan_tables(ar[:, cols], ai[:, cols], lane_w, reverse)

        def blk(i, carry, cols=cols, levels=levels, pr=pr, pi=pi):
            cr, ci = carry
            rb = (n_blk - 1 - i) if reverse else i
            off = pl.multiple_of(rb * SUBLANES, SUBLANES)
            xr = re_ref[pl.ds(off, SUBLANES), cols]
            xi = im_ref[pl.ds(off, SUBLANES), cols]
            for (lr, li), k in zip(levels, (1, 2, 4)):
                sh = (SUBLANES - k) if reverse else k
                sr = pltpu.roll(xr, sh, 0)
                si = pltpu.roll(xi, sh, 0)
                xr, xi = xr + (lr * sr - li * si), xi + (lr * si + li * sr)
            xr = xr + (pr * cr - pi * ci)
            xi = xi + (pr * ci + pi * cr)
            re_ref[pl.ds(off, SUBLANES), cols] = xr
            im_ref[pl.ds(off, SUBLANES), cols] = xi
            edge = slice(0, 1) if reverse else slice(SUBLANES - 1, SUBLANES)
            return xr[edge, :], xi[edge, :]

        cr, ci = lax.fori_loop(0, n_blk, blk, (car_ref[0:1, cols], car_ref[1:2, cols]))
        car_ref[0:1, cols] = cr
        car_ref[1:2, cols] = ci


def _ssm_fwd(u, bb_re, bb_im, c_re_t, c_imn_t, d_row, ab_re, ab_im, n_seq, seq):
    n = u.shape[0]
    tt = SCAN_TILE
    nt = seq // tt

    def body(u_ref, bbre, bbim, cre, cimn, d_ref, are, aim, sre_ref, sim_ref, y_ref, car_ref):
        @pl.when(pl.program_id(2) == 0)
        def _():
            car_ref[...] = jnp.zeros_like(car_ref)

        uu = u_ref[...]
        ub = uu.astype(BF16)
        sre_ref[...] = _dot(ub, bbre[0])
        sim_ref[...] = _dot(ub, bbim[0])
        _scan_tile(sre_ref, sim_ref, car_ref, are[...], aim[...], tt, reverse=False)
        y_ref[...] = (_dot(sre_ref[...].astype(BF16), cre[0]) + _dot(sim_ref[...].astype(BF16), cimn[0])
                      + d_ref[...] * uu)

    tok = lambda j, b, t: (b * nt + t, j)
    blk3 = lambda j, b, t: (j, 0, 0)
    row = lambda j, b, t: (0, j)
    return _pcall(
        body, name="ssm_fwd", grid=(N_JBLK, n_seq, nt),
        out_shape=(jax.ShapeDtypeStruct((n, N_JBLK * JB_ST), F32), jax.ShapeDtypeStruct((n, N_JBLK * JB_ST), F32),
                   jax.ShapeDtypeStruct((n, SSM_W), F32)),
        in_specs=[pl.BlockSpec((tt, JB_CH), tok),
                  pl.BlockSpec((1, JB_CH, JB_ST), blk3), pl.BlockSpec((1, JB_CH, JB_ST), blk3),
                  pl.BlockSpec((1, JB_ST, JB_CH), blk3), pl.BlockSpec((1, JB_ST, JB_CH), blk3),
                  pl.BlockSpec((1, JB_CH), row), pl.BlockSpec((1, JB_ST), row), pl.BlockSpec((1, JB_ST), row)],
        out_specs=(pl.BlockSpec((tt, JB_ST), tok), pl.BlockSpec((tt, JB_ST), tok), pl.BlockSpec((tt, JB_CH), tok)),
        scratch_shapes=[pltpu.VMEM((SUBLANES, JB_ST), F32)],
        compiler_params=_params(3),
    )(u, bb_re, bb_im, c_re_t, c_imn_t, d_row, ab_re, ab_im)


def _ssm_bwd(dy, u, s_re, s_im, bb_re, bb_im, c_re_t, c_imn_t, d_row, ab_re, ab_im, g_out, g_glu, n_seq, seq):
    n = u.shape[0]
    tt = SCAN_TILE
    nt = seq // tt
    rows8 = tt // SUBLANES

    def body(dy_ref, u_ref, sre_ref, sim_ref, pre_ref, pim_ref, bbre, bbim, cre, cimn, d_ref, are, aim,
             gout_ref, gglu_ref,
             du_ref, dcre_ref, dcim_ref, dbbre_ref, dbbim_ref, dare_ref, daim_ref, dd_ref, rout_ref, rglu_ref,
             lre_ref, lim_ref, car_ref, send_sems, recv_sems, loc_sems):
        j = pl.program_id(0)
        b = pl.program_id(1)
        tr = pl.program_id(2)

        def exchange():
            return _direct_copies(lambda pid: [gout_ref.at[pid], gglu_ref.at[pid]], [rout_ref, rglu_ref],
                                  send_sems, recv_sems, loc_sems)

        @pl.when((j == 0) & (b == 0) & (tr == 0))
        def _():
            mine, sends = exchange()
            for cp in mine + sends:
                cp.start()

        @pl.when(tr == 0)
        def _():
            car_ref[...] = jnp.zeros_like(car_ref)

        @pl.when((b == 0) & (tr == 0))
        def _():
            for r in (dcre_ref, dcim_ref, dbbre_ref, dbbim_ref, dare_ref, daim_ref, dd_ref):
                r[...] = jnp.zeros_like(r)

        dyv = dy_ref[...]
        dyb = dyv.astype(BF16)
        lre_ref[...] = _dot_nt(dyb, cre[0])
        lim_ref[...] = _dot_nt(dyb, cimn[0])
        _scan_tile(lre_ref, lim_ref, car_ref, are[...], -aim[...], tt, reverse=True)
        lr = lre_ref[...]
        li = lim_ref[...]
        lrb = lr.astype(BF16)
        lib = li.astype(BF16)
        uu = u_ref[...]
        ub = uu.astype(BF16)
        du_ref[...] = d_ref[...] * dyv + _dot_nt(lrb, bbre[0]) + _dot_nt(lib, bbim[0])
        dbbre_ref[0] += _dot_tn(ub, lrb)
        dbbim_ref[0] += _dot_tn(ub, lib)
        sr = sre_ref[...]
        si = sim_ref[...]
        dcre_ref[0] += _dot_tn(dyb, sr.astype(BF16))
        dcim_ref[0] += _dot_tn(dyb, si.astype(BF16))
        first = tr == nt - 1
        hr = jnp.where(first, 0.0, pre_ref[...])
        hi = jnp.where(first, 0.0, pim_ref[...])
        spr = _shift_down(sr, hr, 1)
        spi = _shift_down(si, hi, 1)
        dare_ref[...] += jnp.sum(lr * spr + li * spi, axis=0, keepdims=True)
        daim_ref[...] += jnp.sum(li * spr - lr * spi, axis=0, keepdims=True)
        dd_ref[...] += jnp.sum(dyv * uu, axis=0, keepdims=True)

        @pl.when((j == N_JBLK - 1) & (b == n_seq - 1) & (tr == nt - 1))
        def _():
            mine, sends = exchange()
            for cp in sends + mine:
                cp.wait()

    tok = lambda j, b, t: (b * nt + (nt - 1 - t), j)
    halo = lambda j, b, t: (jnp.maximum((b * nt + (nt - 1 - t)) * rows8 - 1, 0), j)
    blk3 = lambda j, b, t: (j, 0, 0)
    row = lambda j, b, t: (0, j)
    acc = jax.ShapeDtypeStruct((N_JBLK, JB_CH, JB_ST), F32)
    return _pcall(
        body, name="ssm_bwd", grid=(N_JBLK, n_seq, nt),
        out_shape=(jax.ShapeDtypeStruct((n, SSM_W), F32), acc, acc, acc, acc,
                   jax.ShapeDtypeStruct((1, N_JBLK * JB_ST), F32), jax.ShapeDtypeStruct((1, N_JBLK * JB_ST), F32),
                   jax.ShapeDtypeStruct((1, SSM_W), F32),
                   jax.ShapeDtypeStruct((N_DEV,) + g_out.shape[1:], F32),
                   jax.ShapeDtypeStruct((N_DEV,) + g_glu.shape[1:], F32)),
        in_specs=[pl.BlockSpec((tt, JB_CH), tok), pl.BlockSpec((tt, JB_CH), tok),
                  pl.BlockSpec((tt, JB_ST), tok), pl.BlockSpec((tt, JB_ST), tok),
                  pl.BlockSpec((SUBLANES, JB_ST), halo), pl.BlockSpec((SUBLANES, JB_ST), halo),
                  pl.BlockSpec((1, JB_CH, JB_ST), blk3), pl.BlockSpec((1, JB_CH, JB_ST), blk3),
                  pl.BlockSpec((1, JB_ST, JB_CH), blk3), pl.BlockSpec((1, JB_ST, JB_CH), blk3),
                  pl.BlockSpec((1, JB_CH), row), pl.BlockSpec((1, JB_ST), row), pl.BlockSpec((1, JB_ST), row),
                  HBM_SPEC, HBM_SPEC],
        out_specs=(pl.BlockSpec((tt, JB_CH), tok),
                   pl.BlockSpec((1, JB_CH, JB_ST), blk3), pl.BlockSpec((1, JB_CH, JB_ST), blk3),
                   pl.BlockSpec((1, JB_CH, JB_ST), blk3), pl.BlockSpec((1, JB_CH, JB_ST), blk3),
                   pl.BlockSpec((1, JB_ST), row), pl.BlockSpec((1, JB_ST), row), pl.BlockSpec((1, JB_CH), row),
                   HBM_SPEC, HBM_SPEC),
        scratch_shapes=[pltpu.VMEM((tt, JB_ST), F32), pltpu.VMEM((tt, JB_ST), F32),
                        pltpu.VMEM((SUBLANES, JB_ST), F32),
                        pltpu.SemaphoreType.DMA((7 * 2,)), pltpu.SemaphoreType.DMA((7 * 2,)),
                        pltpu.SemaphoreType.DMA((2,))],
        compiler_params=_params(3),
    )(dy, u, s_re, s_im, s_re, s_im, bb_re, bb_im, c_re_t, c_imn_t, d_row, ab_re, ab_im, g_out, g_glu)


def _mix(x2, tgt2, y, zs, h, bc, cc, zc, gf, b_glu, conv8, w_glu_f, w_out_f, seq):
    n = x2.shape[0]
    tm = TOK_TILE
    tiles_per_seq = seq // tm
    rows8 = tm // SUBLANES

    def body(x_ref, t_ref, y_ref, zs_ref, h_ref, bc_ref, cc_ref, zc_ref, hp_ref, ccp_ref,
             gf_ref, bg_ref, cw_ref, wg_ref, wo_ref,
             dh2_ref, dy_ref, dzs_ref, dbc_ref, dzc_ref, dyc_ref,
             dwo_ref, dwg_ref, loss_ref, dgf_ref, dbg_ref, dcw_ref):
        i = pl.program_id(0)

        @pl.when(i == 0)
        def _():
            for r in (dwo_ref, dwg_ref, loss_ref, dgf_ref, dbg_ref, dcw_ref):
                r[...] = jnp.zeros_like(r)

        yv = y_ref[...]
        y1, dgelu = _gelu_and_grad(yv)
        y1b = y1.astype(BF16)
        gate = _sigmoid(_dot(y1b, wg_ref[...]) + bg_ref[...])
        y2 = y1 * gate
        szs, dszs = _silu_and_grad(zs_ref[...])
        yssm = y2 * szs
        hv = h_ref[...]
        ccv = cc_ref[...]
        bcv = bc_ref[...]
        v = ccv * hv
        first = (i % tiles_per_seq) == 0
        vhalo = jnp.where(first, 0.0, ccp_ref[...] * hp_ref[...])
        v1 = _shift_down(v, vhalo, 1)
        v2 = _shift_down(v, vhalo, 2)
        w0 = cw_ref[0:1, :]
        w1 = cw_ref[1:2, :]
        w2 = cw_ref[2:3, :]
        yc = w0 * v2 + w1 * v1 + w2 * v
        szc, dszc = _silu_and_grad(zc_ref[...])
        yconv = (bcv * yc) * szc
        ysb = yssm.astype(BF16)
        ycb = yconv.astype(BF16)
        h2 = x_ref[...] + _dot(ysb, wo_ref[0:SSM_W, :]) + _dot(ycb, wo_ref[SSM_W:, :])
        r2 = lax.rsqrt(jnp.mean(h2 * h2, axis=-1, keepdims=True) + EPS)
        hn = h2 * r2
        gfv = gf_ref[...]
        err = hn * gfv - t_ref[...]
        loss_ref[...] += 0.5 * jnp.sum(jnp.mean(err * err, axis=-1, keepdims=True))
        dout = err * (1.0 / D_MODEL)
        dgf_ref[...] += jnp.sum(dout * hn, axis=0, keepdims=True)
        dn = dout * gfv
        dh2 = r2 * (dn - hn * jnp.mean(dn * hn, axis=-1, keepdims=True))
        dh2_ref[...] = dh2
        dh2b = dh2.astype(BF16)
        dwo_ref[0:SSM_W, :] += _dot_tn(ysb, dh2b)
        dwo_ref[SSM_W:, :] += _dot_tn(ycb, dh2b)
        dyssm = _dot_nt(dh2b, wo_ref[0:SSM_W, :])
        dyconv = _dot_nt(dh2b, wo_ref[SSM_W:, :])
        dy2 = dyssm * szs
        dzs_ref[...] = dyssm * y2 * dszs
        dgp = dy2 * y1 * (gate * (1.0 - gate))
        dgpb = dgp.astype(BF16)
        dy1 = dy2 * gate + _dot_nt(dgpb, wg_ref[...])
        dwg_ref[...] += _dot_tn(y1b, dgpb)
        dbg_ref[...] += jnp.sum(dgp, axis=0, keepdims=True)
        dy_ref[...] = dy1 * dgelu
        dbc_ref[...] = dyconv * yc * szc
        dyc = dyconv * bcv * szc
        dyc_ref[...] = dyc
        dzc_ref[...] = dyconv * bcv * yc * dszc
        dcw_ref[0:1, :] += jnp.sum(dyc * v2, axis=0, keepdims=True)
        dcw_ref[1:2, :] += jnp.sum(dyc * v1, axis=0, keepdims=True)
        dcw_ref[2:3, :] += jnp.sum(dyc * v, axis=0, keepdims=True)

    tile_d = pl.BlockSpec((tm, D_MODEL), lambda i: (i, 0))
    tile_s = pl.BlockSpec((tm, SSM_W), lambda i: (i, 0))
    halo = pl.BlockSpec((SUBLANES, SSM_W), lambda i: (jnp.maximum(i * rows8 - 1, 0), 0))
    const = lambda shape: pl.BlockSpec(shape, lambda i: (0,) * len(shape))
    seg = jax.ShapeDtypeStruct((n, SSM_W), F32)
    return _pcall(
        body, name="mix", grid=(n // tm,),
        out_shape=(jax.ShapeDtypeStruct((n, D_MODEL), F32), seg, seg, seg, seg, seg,
                   jax.ShapeDtypeStruct((D_MODEL, D_MODEL), F32), jax.ShapeDtypeStruct((SSM_W, SSM_W), F32),
                   jax.ShapeDtypeStruct((SUBLANES, LANES), F32), jax.ShapeDtypeStruct((1, D_MODEL), F32),
                   jax.ShapeDtypeStruct((1, SSM_W), F32), jax.ShapeDtypeStruct((SUBLANES, CONV_W), F32)),
        in_specs=[tile_d, tile_d, tile_s, tile_s, tile_s, tile_s, tile_s, tile_s, halo, halo,
                  const((1, D_MODEL)), const((1, SSM_W)), const((SUBLANES, CONV_W)),
                  const((SSM_W, SSM_W)), const((D_MODEL, D_MODEL))],
        out_specs=(tile_d, tile_s, tile_s, tile_s, tile_s, tile_s,
                   const((D_MODEL, D_MODEL)), const((SSM_W, SSM_W)), const((SUBLANES, LANES)),
                   const((1, D_MODEL)), const((1, SSM_W)), const((SUBLANES, CONV_W))),
        compiler_params=_params(1),
    )(x2, tgt2, y, zs, h, bc, cc, zc, h, cc, gf, b_glu, conv8, w_glu_f, w_out_f)


def _in_bwd(x2, dh2, du, dzs, dyc, h, cc, dbc, dzc, g1, conv8, w_full, small, seq):
    n = x2.shape[0]
    tm = TOK_TILE
    n_tiles = n // tm
    tiles_per_seq = seq // tm
    rows8 = tm // SUBLANES
    n_blk8 = n // SUBLANES

    def body(x_ref, dh2_ref, du_ref, dzs_ref, dyc_ref, dycn_ref, h_ref, cc_ref, dbc_ref, dzc_ref,
             g_ref, cw_ref, w_ref, sm_ref, gx_ref, dp_ref, dg_ref, rsm_ref, send_sems, recv_sems, loc_sem):
        i = pl.program_id(0)
        gather = _TwoLevelGather(sm_ref, rsm_ref, send_sems, recv_sems, loc_sem.at[0])

        @pl.when(i == 0)
        def _():
            dg_ref[...] = jnp.zeros_like(dg_ref)
            gather.start()

        @pl.when(i == n_tiles // 2)
        def _():
            gather.forward()

        dyc = dyc_ref[...]
        last = (i % tiles_per_seq) == tiles_per_seq - 1
        nhalo = jnp.where(last, 0.0, dycn_ref[...])
        dv = (cw_ref[2:3, :] * dyc + cw_ref[1:2, :] * _shift_up(dyc, nhalo, 1)
              + cw_ref[0:1, :] * _shift_up(dyc, nhalo, 2))
        parts = (du_ref[...], dzs_ref[...], dv * cc_ref[...], dbc_ref[...], dv * h_ref[...], dzc_ref[...])
        dxn = jnp.zeros((tm, D_MODEL), F32)
        for k, p in enumerate(parts):
            pb = p.astype(BF16)
            dp_ref[:, k * SSM_W:(k + 1) * SSM_W] = pb
            dxn = dxn + _dot_nt(pb, w_ref[:, k * SSM_W:(k + 1) * SSM_W])
        x = x_ref[...]
        r = lax.rsqrt(jnp.mean(x * x, axis=-1, keepdims=True) + EPS)
        xh = x * r
        dg_ref[...] += jnp.sum(dxn * xh, axis=0, keepdims=True)
        dn = dxn * g_ref[...]
        gx_ref[...] = dh2_ref[...] + r * (dn - xh * jnp.mean(dn * xh, axis=-1, keepdims=True))

        @pl.when(i == n_tiles - 1)
        def _():
            gather.finish()

    tile_d = pl.BlockSpec((tm, D_MODEL), lambda i: (i, 0))
    tile_s = pl.BlockSpec((tm, SSM_W), lambda i: (i, 0))
    nhalo = pl.BlockSpec((SUBLANES, SSM_W), lambda i: (jnp.minimum((i + 1) * rows8, n_blk8 - 1), 0))
    const = lambda shape: pl.BlockSpec(shape, lambda i: (0,) * len(shape))
    return _pcall(
        body, name="in_bwd", grid=(n_tiles,),
        out_shape=(jax.ShapeDtypeStruct((n, D_MODEL), F32), jax.ShapeDtypeStruct((n, IN_COLS), BF16),
                   jax.ShapeDtypeStruct((SUBLANES, D_MODEL), F32),
                   jax.ShapeDtypeStruct((N_DEV,) + small.shape, F32)),
        in_specs=[tile_d, tile_d, tile_s, tile_s, tile_s, nhalo, tile_s, tile_s, tile_s, tile_s,
                  const((1, D_MODEL)), const((SUBLANES, CONV_W)), const((D_MODEL, IN_COLS)), HBM_SPEC],
        out_specs=(tile_d, pl.BlockSpec((tm, IN_COLS), lambda i: (i, 0)), const((SUBLANES, D_MODEL)), HBM_SPEC),
        scratch_shapes=[pltpu.SemaphoreType.DMA((7,)), pltpu.SemaphoreType.DMA((7,)), pltpu.SemaphoreType.DMA((1,))],
        compiler_params=_params(1),
    )(x2, dh2, du, dzs, dyc, dyc, h, cc, dbc, dzc, g1, conv8, w_full, small)


def _dw_in_exchange(order, xn, dproj, dg8):
    n = xn.shape[0]
    tk = 512
    nk = n // tk
    piece = (D_MODEL, COLS_PER_DEV)

    def body(order_ref, xn_ref, dp_ref, dg_ref, own_ref, rchip_ref, rdg_ref,
             acc, stage, sbuf, give_send, give_recv, keep_send, keep_recv, dg_send, dg_recv, dg_loc):
        del order_ref
        s = pl.program_id(0)
        x, y, c = _mesh_pos()
        sib = (x, y, 1 - c)
        chips = [(1 - x, y), (x, 1 - y), (1 - x, 1 - y)]

        def dg_copies():
            return _direct_copies(lambda pid: [dg_ref], [rdg_ref], dg_send, dg_recv, dg_loc)

        def give(i):
            return pltpu.make_async_remote_copy(src_ref=acc.at[0], dst_ref=stage.at[i], send_sem=give_send.at[i],
                                                recv_sem=give_recv.at[i], device_id=sib, device_id_type=MESH)

        def keep(i):
            return pltpu.make_async_remote_copy(src_ref=sbuf.at[i], dst_ref=rchip_ref.at[i], send_sem=keep_send.at[i],
                                                recv_sem=keep_recv.at[i], device_id=(*chips[i], c), device_id_type=MESH)

        @pl.when(s == 0)
        def _():
            mine, sends = dg_copies()
            for cp in mine + sends:
                cp.start()

        for k in (2, 4, 6):
            @pl.when(s == k)
            def _(k=k):
                give(k // 2 - 1).wait_send()

        slot = s % 2
        acc[slot] = _dot_tn(xn_ref[pl.ds(0, tk), :], dp_ref[pl.ds(0, tk), :])

        def kstep(kk, carry):
            off = pl.multiple_of(kk * tk, tk)
            acc[slot] += _dot_tn(xn_ref[pl.ds(off, tk), :], dp_ref[pl.ds(off, tk), :])
            return carry

        lax.fori_loop(1, nk, kstep, 0)

        for k in range(N_DEV):
            @pl.when(s == k)
            def _(k=k):
                i = k // 2
                if k % 2 == 0:
                    give(i).start()
                else:
                    give(i).wait_recv()
                    total = acc[1] + stage[i]
                    if i < 3:
                        sbuf[i] = total.astype(BF16)
                        keep(i).start()
                    else:
                        own_ref[...] = total

        @pl.when(s == N_DEV - 1)
        def _():
            give(3).wait_send()
            for i in range(3):
                keep(i).wait()
            mine, sends = dg_copies()
            for cp in sends + mine:
                cp.wait()

    grid_spec = pltpu.PrefetchScalarGridSpec(
        num_scalar_prefetch=1, grid=(N_DEV,),
        in_specs=[pl.BlockSpec(memory_space=pltpu.VMEM),
                  pl.BlockSpec((n, COLS_PER_DEV), lambda s, order: (0, order[s])),
                  HBM_SPEC],
        out_specs=(pl.BlockSpec(piece, lambda s, order: (0, 0)), HBM_SPEC, HBM_SPEC),
        scratch_shapes=[pltpu.VMEM((2,) + piece, F32), pltpu.VMEM((4,) + piece, F32), pltpu.VMEM((3,) + piece, BF16),
                        pltpu.SemaphoreType.DMA((4,)), pltpu.SemaphoreType.DMA((4,)),
                        pltpu.SemaphoreType.DMA((3,)), pltpu.SemaphoreType.DMA((3,)),
                        pltpu.SemaphoreType.DMA((7,)), pltpu.SemaphoreType.DMA((7,)), pltpu.SemaphoreType.DMA((1,))])
    return _pcall(
        body, name="dw_in_exchange", grid_spec=grid_spec,
        out_shape=(jax.ShapeDtypeStruct(piece, F32), jax.ShapeDtypeStruct((3,) + piece, BF16),
                   jax.ShapeDtypeStruct((N_DEV,) + dg8.shape, F32)),
        compiler_params=_params(1),
    )(order, xn, dproj, dg8)


def _adamw(g, w, m, v):
    m_new = ADAM_B1 * m + (1.0 - ADAM_B1) * g
    v_new = ADAM_B2 * v + (1.0 - ADAM_B2) * (g * g)
    m_hat = m_new / (1.0 - ADAM_B1 ** ADAM_STEP)
    v_hat = v_new / (1.0 - ADAM_B2 ** ADAM_STEP)
    delta = -ADAM_LR * (m_hat / (jnp.sqrt(v_hat) + ADAM_EPS) + ADAM_WD * w)
    return delta, m_new, v_new


def _reduce_adam(recv, w, m, v, name, row_tile):
    rows, cols = w.shape

    def body(r_ref, w_ref, m_ref, v_ref, g_ref, d_ref, nm_ref, nv_ref):
        g = r_ref[0]
        for s in range(1, N_DEV):
            g = g + r_ref[s]
        g_ref[...] = g
        d_ref[...], nm_ref[...], nv_ref[...] = _adamw(g, w_ref[...], m_ref[...], v_ref[...])

    tile = pl.BlockSpec((row_tile, cols), lambda i: (i, 0))
    shp = jax.ShapeDtypeStruct((rows, cols), F32)
    return _pcall(
        body, name=name, grid=(rows // row_tile,),
        out_shape=(shp,) * 4,
        in_specs=[pl.BlockSpec((N_DEV, row_tile, cols), lambda i: (0, i, 0)), tile, tile, tile],
        out_specs=(tile,) * 4,
        compiler_params=_params(1),
    )(recv, w, m, v)


def _reduce_adam_w_in(own, rchip, w, m, v):
    rows, cols = w.shape
    row_tile = 256

    def body(o_ref, r_ref, w_ref, m_ref, v_ref, g_ref, d_ref, nm_ref, nv_ref):
        g = o_ref[...]
        for s in range(3):
            g = g + r_ref[s].astype(F32)
        g_ref[...] = g
        d_ref[...], nm_ref[...], nv_ref[...] = _adamw(g, w_ref[...], m_ref[...], v_ref[...])

    tile = pl.BlockSpec((row_tile, cols), lambda i: (i, 0))
    shp = jax.ShapeDtypeStruct((rows, cols), F32)
    return _pcall(
        body, name="reduce_adam_w_in", grid=(rows // row_tile,),
        out_shape=(shp,) * 4,
        in_specs=[tile, pl.BlockSpec((3, row_tile, cols), lambda i: (0, i, 0)), tile, tile, tile],
        out_specs=(tile,) * 4,
        compiler_params=_params(1),
    )(own, rchip, w, m, v)


def _reduce_adam_stacked(recv, wmv, name):
    _, rows, cols = wmv.shape

    def body(r_ref, p_ref, o_ref):
        g = r_ref[0]
        for s in range(1, N_DEV):
            g = g + r_ref[s]
        o_ref[0] = g
        o_ref[1], o_ref[2], o_ref[3] = _adamw(g, p_ref[0], p_ref[1], p_ref[2])

    return _pcall(body, name=name, out_shape=jax.ShapeDtypeStruct((4, rows, cols), F32),
                  compiler_params=_params(0))(recv, wmv)


_SMALL = (("final_norm_gain", D_MODEL), ("b_glu", SSM_W),
          ("ssm_a_re", N_GROUPS * STATE), ("ssm_a_im", N_GROUPS * STATE), ("ssm_log_dt", N_GROUPS),
          ("ssm_b_re", N_GROUPS * STATE * GROUP), ("ssm_b_im", N_GROUPS * STATE * GROUP),
          ("ssm_c_re", N_GROUPS * STATE * GROUP), ("ssm_c_im", N_GROUPS * STATE * GROUP),
          ("ssm_d", N_GROUPS * GROUP), ("conv_w", 3 * CONV_W))
_PACK_UNIT = SUBLANES * LANES


def _pack_small(dicts):
    cols = []
    for name, size in _SMALL:
        flat = jnp.stack([d[name].reshape(-1) for d in dicts])
        padded = -(-size // _PACK_UNIT) * _PACK_UNIT
        cols.append(jnp.pad(flat, ((0, 0), (0, padded - size))).reshape(len(dicts), -1, LANES))
    return jnp.concatenate(cols, axis=1)


def _unpack_small(packed):
    out, r0 = {}, 0
    for name, size in _SMALL:
        nrows = -(-size // _PACK_UNIT) * SUBLANES
        out[name] = packed[:, r0:r0 + nrows].reshape(packed.shape[0], -1)[:, :size]
        r0 += nrows
    return out


def _block_diag(m4):
    eye = jnp.eye(SUBLANES, dtype=m4.dtype)
    j, g, a, b = m4.shape
    return jnp.einsum("jgab,gk->jgakb", m4, eye).reshape(j, g * a, g * b)


def _block_diag_extract(dense, a, b):
    d5 = dense.reshape(N_JBLK, SUBLANES, a, SUBLANES, b)
    return jnp.stack([d5[:, g, :, g, :] for g in range(SUBLANES)], axis=1)


def kernel(x, norm_gain, w_in, ssm_a_re, ssm_a_im, ssm_log_dt, ssm_b_re, ssm_b_im, ssm_c_re, ssm_c_im, ssm_d, w_glu, b_glu, conv_w, w_out, final_norm_gain, loss_target, m_norm_gain, m_w_in, m_ssm_a_re, m_ssm_a_im, m_ssm_log_dt, m_ssm_b_re, m_ssm_b_im, m_ssm_c_re, m_ssm_c_im, m_ssm_d, m_w_glu, m_b_glu, m_conv_w, m_w_out, m_final_norm_gain, v_norm_gain, v_w_in, v_ssm_a_re, v_ssm_a_im, v_ssm_log_dt, v_ssm_b_re, v_ssm_b_im, v_ssm_c_re, v_ssm_c_im, v_ssm_d, v_w_glu, v_b_glu, v_conv_w, v_w_out, v_final_norm_gain):
    n_seq, seq, _ = x.shape
    n = n_seq * seq
    me = 4 * lax.axis_index("x") + 2 * lax.axis_index("y") + lax.axis_index("c")

    conv_p = jnp.pad(conv_w[0], ((0, SUBLANES - 3), (0, LANES - CONV_COLS_PER_DEV)))
    w_in_f, w_out_f, w_glu_f, conv_all = _gather_weights(
        w_in[0].astype(BF16), w_out[0].astype(BF16), w_glu[0].astype(BF16), conv_p)
    conv8 = jnp.transpose(conv_all[:, :, :CONV_COLS_PER_DEV], (1, 0, 2)).reshape(SUBLANES, CONV_W)

    rep = lambda a: jnp.repeat(a[0], GROUP, axis=1)
    a_re_r, a_im_r = rep(ssm_a_re), rep(ssm_a_im)
    log_dt = ssm_log_dt[0].reshape(N_GROUPS, 1)
    b_re2 = ssm_b_re[0].reshape(N_GROUPS, STATE * GROUP)
    b_im2 = ssm_b_im[0].reshape(N_GROUPS, STATE * GROUP)
    ab_re_r, ab_im_r, bb_re2, bb_im2 = _ssm_disc(a_re_r, a_im_r, log_dt, b_re2, b_im2)
    ab_re = ab_re_r[:, ::GROUP].reshape(1, N_GROUPS * STATE)
    ab_im = ab_im_r[:, ::GROUP].reshape(1, N_GROUPS * STATE)

    def bb_mat(bb2):
        t = jnp.transpose(bb2.reshape(N_JBLK, SUBLANES, STATE, GROUP), (0, 1, 3, 2))
        return _block_diag(t).astype(BF16)

    def c_mat(c3, sign):
        t = jnp.transpose(c3.reshape(N_JBLK, SUBLANES, GROUP, STATE), (0, 1, 3, 2))
        return _block_diag(sign * t).astype(BF16)

    bb_re_m, bb_im_m = bb_mat(bb_re2), bb_mat(bb_im2)
    c_re_m, c_imn_m = c_mat(ssm_c_re[0], 1.0), c_mat(ssm_c_im[0], -1.0)
    d_row = ssm_d[0].reshape(1, SSM_W)

    x2 = x.reshape(n, D_MODEL)
    tgt2 = loss_target.reshape(n, D_MODEL)
    xn, u, zs, h, bc, cc, zc = _in_proj(x2, norm_gain, w_in_f)
    s_re, s_im, y = _ssm_fwd(u, bb_re_m, bb_im_m, c_re_m, c_imn_m, d_row, ab_re, ab_im, n_seq, seq)
    (dh2, dy, dzs, dbc, dzc, dyc, dw_out, dw_glu, loss_t, dgf, dbg, dcw) = _mix(
        x2, tgt2, y, zs, h, bc, cc, zc, final_norm_gain.reshape(1, D_MODEL), b_glu, conv8,
        w_glu_f, w_out_f, seq)

    du, dc_re_d, dc_im_d, dbb_re_d, dbb_im_d, dab_re, dab_im, dd, r_out, r_glu = _ssm_bwd(
        dy, u, s_re, s_im, bb_re_m, bb_im_m, c_re_m, c_imn_m, d_row, ab_re, ab_im,
        dw_out.reshape(N_DEV, OUT_ROWS_PER_DEV, D_MODEL), dw_glu.reshape(N_DEV, GLU_ROWS_PER_DEV, SSM_W), n_seq, seq)
    g_c_re = _block_diag_extract(dc_re_d, GROUP, STATE).reshape(N_GROUPS, GROUP, STATE)
    g_c_im = -_block_diag_extract(dc_im_d, GROUP, STATE).reshape(N_GROUPS, GROUP, STATE)

    def bb_grad(dense):
        t = _block_diag_extract(dense, GROUP, STATE)
        return jnp.transpose(t, (0, 1, 3, 2)).reshape(N_GROUPS, STATE * GROUP)

    def ab_grad(row):
        z = jnp.zeros((N_GROUPS, STATE, GROUP), F32)
        return z.at[:, :, 0].set(row.reshape(N_GROUPS, STATE)).reshape(N_GROUPS, STATE * GROUP)

    g_are_r, g_aim_r, g_ldt, g_bre2, g_bim2 = _ssm_disc_bwd(
        a_re_r, a_im_r, log_dt, b_re2, b_im2, ab_grad(dab_re), ab_grad(dab_im),
        bb_grad(dbb_re_d), bb_grad(dbb_im_d))
    small_grads = {"final_norm_gain": dgf, "b_glu": dbg,
                   "ssm_a_re": g_are_r[:, ::GROUP], "ssm_a_im": g_aim_r[:, ::GROUP], "ssm_log_dt": g_ldt,
                   "ssm_b_re": g_bre2, "ssm_b_im": g_bim2, "ssm_c_re": g_c_re, "ssm_c_im": g_c_im,
                   "ssm_d": dd, "conv_w": dcw[0:3]}
    grad_x2, dproj, dg8, r_small = _in_bwd(x2, dh2, du, dzs, dyc, h, cc, dbc, dzc, norm_gain, conv8, w_in_f,
                                           _pack_small([small_grads])[0], seq)

    mx, my, mc = lax.axis_index("x"), lax.axis_index("y"), lax.axis_index("c")
    order = []
    for cx, cy in ((1 - mx, my), (mx, 1 - my), (1 - mx, 1 - my), (mx, my)):
        order += [4 * cx + 2 * cy + (1 - mc), 4 * cx + 2 * cy + mc]
    own_in, rchip_in, r_dg = _dw_in_exchange(jnp.stack(order).astype(jnp.int32), xn, dproj, dg8)

    def conv_full(shard):
        return lax.dynamic_update_slice(jnp.zeros((3, CONV_W), F32), shard[0], (0, me * CONV_COLS_PER_DEV))

    triples = dict(final_norm_gain=(final_norm_gain, m_final_norm_gain, v_final_norm_gain),
                   b_glu=(b_glu, m_b_glu, v_b_glu), ssm_a_re=(ssm_a_re, m_ssm_a_re, v_ssm_a_re),
                   ssm_a_im=(ssm_a_im, m_ssm_a_im, v_ssm_a_im), ssm_log_dt=(ssm_log_dt, m_ssm_log_dt, v_ssm_log_dt),
                   ssm_b_re=(ssm_b_re, m_ssm_b_re, v_ssm_b_re), ssm_b_im=(ssm_b_im, m_ssm_b_im, v_ssm_b_im),
                   ssm_c_re=(ssm_c_re, m_ssm_c_re, v_ssm_c_re), ssm_c_im=(ssm_c_im, m_ssm_c_im, v_ssm_c_im),
                   ssm_d=(ssm_d, m_ssm_d, v_ssm_d),
                   conv_w=(conv_full(conv_w), conv_full(m_conv_w), conv_full(v_conv_w)))
    wmv_small = _pack_small([{k: t[i] for k, t in triples.items()} for i in range(3)])
    wmv_gain = jnp.pad(jnp.stack([norm_gain, m_norm_gain, v_norm_gain]), ((0, 0), (0, SUBLANES - 1), (0, 0)))

    res_in = _reduce_adam_w_in(own_in, rchip_in, w_in[0], m_w_in[0], v_w_in[0])
    res_out = _reduce_adam(r_out, w_out[0], m_w_out[0], v_w_out[0], "reduce_adam_w_out", OUT_ROWS_PER_DEV)
    res_glu = _reduce_adam(r_glu, w_glu[0], m_w_glu[0], v_w_glu[0], "reduce_adam_w_glu", GLU_ROWS_PER_DEV)
    small = _unpack_small(_reduce_adam_stacked(r_small, wmv_small, "reduce_adam_small"))
    res_gain = _reduce_adam_stacked(r_dg, wmv_gain, "reduce_adam_gain")

    loss = lax.psum(loss_t[0, 0], ("x", "y", "c"))

    shapes = dict(ssm_a_re=(1, N_GROUPS, STATE), ssm_a_im=(1, N_GROUPS, STATE),
                  ssm_log_dt=(1, N_GROUPS), ssm_b_re=(1, N_GROUPS, STATE, GROUP), ssm_b_im=(1, N_GROUPS, STATE, GROUP),
                  ssm_c_re=(1, N_GROUPS, GROUP, STATE), ssm_c_im=(1, N_GROUPS, GROUP, STATE),
                  ssm_d=(1, N_GROUPS, GROUP), b_glu=(1, SSM_W), final_norm_gain=(D_MODEL,))
    big = dict(w_in=res_in, w_glu=res_glu, w_out=res_out)
    small4 = {name: small[name].reshape((4,) + shp) for name, shp in shapes.items()}
    conv4 = lax.dynamic_slice(small["conv_w"].reshape(4, 1, 3, CONV_W), (0, 0, 0, me * CONV_COLS_PER_DEV),
                              (4, 1, 3, CONV_COLS_PER_DEV))

    def leaf(kind, name):
        if name in big:
            return big[name][kind][None]
        if name == "norm_gain":
            return res_gain[kind, 0:1, :]
        if name == "conv_w":
            return conv4[kind]
        return small4[name][kind]

    order = ["norm_gain", "w_in", "ssm_a_re", "ssm_a_im", "ssm_log_dt", "ssm_b_re", "ssm_b_im", "ssm_c_re",
             "ssm_c_im", "ssm_d", "w_glu", "b_glu", "conv_w", "w_out", "final_norm_gain"]
    outs = [loss, grad_x2.reshape(x.shape)]
    for kind in range(4):
        outs += [leaf(kind, name) for name in order]
    return tuple(outs)
```

```python
import functools
import math

import jax
import jax.numpy as jnp
from jax import lax
from jax.experimental import pallas as pl
from jax.experimental.pallas import tpu as pltpu

F32 = jnp.float32
BF16 = jnp.bfloat16

N_DEV = 8
D_MODEL = 1024
SSM_W = 512
CONV_W = 512
N_GROUPS = 32
GROUP = 16
STATE = 64
IN_COLS = 3072
COLS_PER_DEV = IN_COLS // N_DEV
OUT_ROWS_PER_DEV = D_MODEL // N_DEV
GLU_ROWS_PER_DEV = SSM_W // N_DEV
CONV_COLS_PER_DEV = CONV_W // N_DEV
EPS = 1e-6

N_JBLK = 4
JB_CH = SSM_W // N_JBLK
JB_ST = N_GROUPS * STATE // N_JBLK

ADAM_LR = 0.001
ADAM_B1 = 0.9
ADAM_B2 = 0.999
ADAM_EPS = 1e-08
ADAM_WD = 0.01
ADAM_STEP = 10

SUBLANES = 8
LANES = 128
VMEM_LIMIT = 48 * 1024 * 1024
TOK_TILE = 256
SCAN_TILE = 256

MESH = pl.DeviceIdType.MESH
HBM_SPEC = pl.BlockSpec(memory_space=pltpu.HBM)


def _pcall(body, **kw):
    return pl.pallas_call(body, **kw)


def _params(n_grid):
    return pltpu.CompilerParams(dimension_semantics=("arbitrary",) * n_grid,
                                vmem_limit_bytes=VMEM_LIMIT)


def _dot(a, b):
    return jnp.dot(a, b, preferred_element_type=F32)


def _dot_nt(a, b):
    return lax.dot_general(a, b, (((1,), (1,)), ((), ())), preferred_element_type=F32)


def _dot_tn(a, b):
    return lax.dot_general(a, b, (((0,), (0,)), ((), ())), preferred_element_type=F32)


def _sigmoid(z):
    return 1.0 / (1.0 + jnp.exp(-z))


_GELU_C = math.sqrt(2.0 / math.pi)


def _gelu_and_grad(y):
    inner = _GELU_C * (y + 0.044715 * (y * y * y))
    t = jnp.tanh(inner)
    g = 0.5 * y * (1.0 + t)
    dg = 0.5 * (1.0 + t) + 0.5 * y * (1.0 - t * t) * (_GELU_C * (1.0 + 3.0 * 0.044715 * (y * y)))
    return g, dg


def _silu_and_grad(z):
    s = _sigmoid(z)
    return z * s, s * (1.0 + z * (1.0 - s))


def _shift_down(v, halo, k):
    rolled = pltpu.roll(v, k, 0)
    row = lax.broadcasted_iota(jnp.int32, v.shape, 0)
    for r in range(k):
        rolled = jnp.where(row == r, halo[SUBLANES - k + r:SUBLANES - k + r + 1, :], rolled)
    return rolled


def _shift_up(v, halo, k):
    n = v.shape[0]
    rolled = pltpu.roll(v, n - k, 0)
    row = lax.broadcasted_iota(jnp.int32, v.shape, 0)
    for r in range(k):
        rolled = jnp.where(row == n - k + r, halo[r:r + 1, :], rolled)
    return rolled


def _mesh_pos():
    return lax.axis_index("x"), lax.axis_index("y"), lax.axis_index("c")


def _gather_weights(w_in_b, w_out_b, w_glu_b, conv_p):
    n_arr = 4

    def body(win, wout, wglu, cw, o_in, o_out, o_glu, o_cw, send_sems, recv_sems, loc_sems):
        x, y, c = _mesh_pos()
        me, sib = (x, y, c), (x, y, 1 - c)
        chips = [(1 - x, y), (x, 1 - y), (1 - x, 1 - y)]
        own = [win, wout, wglu, cw]

        def slots(px, py, pc):
            k = 4 * px + 2 * py + pc
            return [o_in.at[:, pl.ds(pl.multiple_of(k * COLS_PER_DEV, LANES), COLS_PER_DEV)],
                    o_out.at[pl.ds(pl.multiple_of(k * OUT_ROWS_PER_DEV, OUT_ROWS_PER_DEV), OUT_ROWS_PER_DEV), :],
                    o_glu.at[pl.ds(pl.multiple_of(k * GLU_ROWS_PER_DEV, GLU_ROWS_PER_DEV), GLU_ROWS_PER_DEV), :],
                    o_cw.at[k]]

        def copies(kidx, block, to, from_own=False):
            dst = slots(*block)
            src = own if from_own else dst
            return [pltpu.make_async_remote_copy(
                src_ref=src[a], dst_ref=dst[a],
                send_sem=send_sems.at[kidx * n_arr + a], recv_sem=recv_sems.at[kidx * n_arr + a],
                device_id=to, device_id_type=MESH) for a in range(n_arr)]

        mine = [pltpu.make_async_copy(own[a], slots(*me)[a], loc_sems.at[a]) for a in range(n_arr)]
        for cp in mine:
            cp.start()
        first = copies(0, me, sib, from_own=True)
        for j, chip in enumerate(chips):
            first += copies(1 + j, me, (*chip, c), from_own=True)
        for cp in first:
            cp.start()
        passed = []
        for j, chip in enumerate(chips):
            for cp in copies(1 + j, (*chip, c), me):
                cp.wait_recv()
            fwd = copies(4 + j, (*chip, c), sib)
            for cp in fwd:
                cp.start()
            passed += fwd
        for cp in copies(0, sib, me):
            cp.wait_recv()
        for j, chip in enumerate(chips):
            for cp in copies(4 + j, (*chip, 1 - c), me):
                cp.wait_recv()
        for cp in first + passed:
            cp.wait_send()
        for cp in mine:
            cp.wait()

    return _pcall(
        body, name="gather_weights",
        out_shape=(jax.ShapeDtypeStruct((D_MODEL, IN_COLS), BF16),
                   jax.ShapeDtypeStruct((D_MODEL, D_MODEL), BF16),
                   jax.ShapeDtypeStruct((SSM_W, SSM_W), BF16),
                   jax.ShapeDtypeStruct((N_DEV, SUBLANES, LANES), F32)),
        in_specs=[HBM_SPEC] * n_arr, out_specs=(HBM_SPEC,) * n_arr,
        scratch_shapes=[pltpu.SemaphoreType.DMA((7 * n_arr,)), pltpu.SemaphoreType.DMA((7 * n_arr,)),
                        pltpu.SemaphoreType.DMA((n_arr,))],
    )(w_in_b, w_out_b, w_glu_b, conv_p)


def _direct_copies(srcs_for, out_refs, send_sems, recv_sems, loc_sems):
    x, y, c = _mesh_pos()
    me_id = 4 * x + 2 * y + c
    n_arr = len(out_refs)
    dsts = [r.at[me_id] for r in out_refs]
    own = srcs_for(me_id)
    mine = [pltpu.make_async_copy(own[a], dsts[a], loc_sems.at[a]) for a in range(n_arr)]
    sends = []
    for k in range(1, N_DEV):
        px, py, pc = x ^ ((k >> 2) & 1), y ^ ((k >> 1) & 1), c ^ (k & 1)
        src = srcs_for(4 * px + 2 * py + pc)
        for a in range(n_arr):
            sends.append(pltpu.make_async_remote_copy(
                src_ref=src[a], dst_ref=dsts[a],
                send_sem=send_sems.at[(k - 1) * n_arr + a], recv_sem=recv_sems.at[(k - 1) * n_arr + a],
                device_id=(px, py, pc), device_id_type=MESH))
    return mine, sends


class _TwoLevelGather:
    def __init__(self, src_ref, out_ref, send_sems, recv_sems, loc_sem):
        self.src, self.out = src_ref, out_ref
        self.send_sems, self.recv_sems, self.loc_sem = send_sems, recv_sems, loc_sem
        x, y, c = _mesh_pos()
        self.c = c
        self.me, self.sib = (x, y, c), (x, y, 1 - c)
        self.chips = [(1 - x, y), (x, 1 - y), (1 - x, 1 - y)]

    def _slot(self, px, py, pc):
        return self.out.at[4 * px + 2 * py + pc]

    def _copy(self, k, block, to, from_src=False):
        return pltpu.make_async_remote_copy(
            src_ref=self.src if from_src else self._slot(*block), dst_ref=self._slot(*block),
            send_sem=self.send_sems.at[k], recv_sem=self.recv_sems.at[k], device_id=to, device_id_type=MESH)

    def _local(self):
        return pltpu.make_async_copy(self.src, self._slot(*self.me), self.loc_sem)

    def start(self):
        self._local().start()
        self._copy(0, self.me, self.sib, True).start()
        for j, chip in enumerate(self.chips):
            self._copy(1 + j, self.me, (*chip, self.c), True).start()

    def forward(self):
        for j, chip in enumerate(self.chips):
            self._copy(1 + j, (*chip, self.c), self.me).wait_recv()
            self._copy(4 + j, (*chip, self.c), self.sib).start()

    def finish(self):
        self._copy(0, self.sib, self.me).wait_recv()
        for j, chip in enumerate(self.chips):
            self._copy(4 + j, (*chip, 1 - self.c), self.me).wait_recv()
        self._copy(0, self.me, self.sib, True).wait_send()
        for j, chip in enumerate(self.chips):
            self._copy(1 + j, self.me, (*chip, self.c), True).wait_send()
            self._copy(4 + j, (*chip, self.c), self.sib).wait_send()
        self._local().wait()


def _disc(a_re, a_im, log_dt, b_re, b_im):
    dt = jnp.exp(log_dt)
    mag = jnp.exp(a_re * dt)
    ab_re = mag * jnp.cos(a_im * dt)
    ab_im = mag * jnp.sin(a_im * dt)
    den = a_re * a_re + a_im * a_im
    p_re = ab_re - 1.0
    p_im = ab_im
    q_re = (p_re * a_re + p_im * a_im) / den
    q_im = (p_im * a_re - p_re * a_im) / den
    bb_re = q_re * b_re - q_im * b_im
    bb_im = q_re * b_im + q_im * b_re
    return ab_re, ab_im, bb_re, bb_im


def _ssm_disc(a_re_r, a_im_r, log_dt, b_re, b_im):
    def body(are, aim, ldt, bre, bim, o_abre, o_abim, o_bbre, o_bbim):
        outs = _disc(are[...], aim[...], ldt[...], bre[...], bim[...])
        for o, v in zip((o_abre, o_abim, o_bbre, o_bbim), outs):
            o[...] = v

    shp = jax.ShapeDtypeStruct(a_re_r.shape, F32)
    return _pcall(body, name="ssm_disc", out_shape=(shp,) * 4)(a_re_r, a_im_r, log_dt, b_re, b_im)


def _ssm_disc_bwd(a_re_r, a_im_r, log_dt, b_re, b_im, g_abre, g_abim, g_bbre, g_bbim):
    width = a_re_r.shape[1]

    def body(are, aim, ldt, bre, bim, gabre, gabim, gbbre, gbbim, o_are, o_aim, o_ldt, o_bre, o_bim):
        _, vjp = jax.vjp(_disc, are[...], aim[...], ldt[...], bre[...], bim[...])
        d_are, d_aim, d_ldt, d_bre, d_bim = vjp((gabre[...], gabim[...], gbbre[...], gbbim[...]))

        def group_sum(v):
            for k in (1, 2, 4, 8):
                v = v + pltpu.roll(v, width - k, 1)
            return v

        o_are[...] = group_sum(d_are)
        o_aim[...] = group_sum(d_aim)
        o_ldt[...] = d_ldt
        o_bre[...] = d_bre
        o_bim[...] = d_bim

    shp = jax.ShapeDtypeStruct(a_re_r.shape, F32)
    return _pcall(body, name="ssm_disc_bwd",
                  out_shape=(shp, shp, jax.ShapeDtypeStruct(log_dt.shape, F32), shp, shp),
                  )(a_re_r, a_im_r, log_dt, b_re, b_im, g_abre, g_abim, g_bbre, g_bbim)


def _in_proj(x2, g1, w_full):
    n = x2.shape[0]
    tm = TOK_TILE

    def body(x_ref, g_ref, w_ref, xn_ref, *outs):
        x = x_ref[...]
        r = lax.rsqrt(jnp.mean(x * x, axis=-1, keepdims=True) + EPS)
        xn = ((x * r) * g_ref[...]).astype(BF16)
        xn_ref[...] = xn
        for i, o in enumerate(outs):
            o[...] = _dot(xn, w_ref[:, i * SSM_W:(i + 1) * SSM_W])

    seg = jax.ShapeDtypeStruct((n, SSM_W), F32)
    seg_spec = pl.BlockSpec((tm, SSM_W), lambda i: (i, 0))
    return _pcall(
        body, name="in_proj", grid=(n // tm,),
        out_shape=(jax.ShapeDtypeStruct((n, D_MODEL), BF16),) + (seg,) * 6,
        in_specs=[pl.BlockSpec((tm, D_MODEL), lambda i: (i, 0)),
                  pl.BlockSpec((1, D_MODEL), lambda i: (0, 0)),
                  pl.BlockSpec((D_MODEL, IN_COLS), lambda i: (0, 0))],
        out_specs=(pl.BlockSpec((tm, D_MODEL), lambda i: (i, 0)),) + (seg_spec,) * 6,
        compiler_params=_params(1),
    )(x2, g1, w_full)


def _cmul(p, q):
    return p[0] * q[0] - p[1] * q[1], p[0] * q[1] + p[1] * q[0]


def _scan_tables(ar, ai, width, reverse):
    pows = [(ar, ai)]
    for _ in range(SUBLANES - 1):
        pows.append(_cmul(pows[-1], (ar, ai)))
    row = lax.broadcasted_iota(jnp.int32, (SUBLANES, width), 0)

    def bc(v):
        return jnp.broadcast_to(v, (SUBLANES, width))

    levels = []
    for k in (1, 2, 4):
        keep = (row <= SUBLANES - 1 - k) if reverse else (row >= k)
        levels.append((jnp.where(keep, bc(pows[k - 1][0]), 0.0), jnp.where(keep, bc(pows[k - 1][1]), 0.0)))
    cre = jnp.zeros((SUBLANES, width), F32)
    cim = jnp.zeros((SUBLANES, width), F32)
    for r in range(SUBLANES):
        e = (SUBLANES - r) if reverse else (r + 1)
        cre = jnp.where(row == r, bc(pows[e - 1][0]), cre)
        cim = jnp.where(row == r, bc(pows[e - 1][1]), cim)
    return levels, (cre, cim)


def _load_chunked(src_ref, b, dst_ref, n_rows):
    n_blk = n_rows // SUBLANES
    for i in range(n_blk):
        dst_ref[b, i * SUBLANES:(i + 1) * SUBLANES, :] = src_ref[b, pl.ds(i, SUBLANES, stride=n_blk), :]


def _store_chunked(val, dst_ref, b, n_rows):
    n_blk = n_rows // SUBLANES
    for i in range(n_blk):
        dst_ref[b, pl.ds(i, SUBLANES, stride=n_blk), :] = val[i * SUBLANES:(i + 1) * SUBLANES, :]


def _chunk_scan(re_ref, im_ref, b, car_ref, ar, ai, n_rows, reverse, on_block=None):
    width = re_ref.shape[2]
    n_blk = n_rows // SUBLANES
    shape = (SUBLANES, width)
    abr = jnp.broadcast_to(ar, shape)
    abi = jnp.broadcast_to(ai, shape)
    order = list(range(n_blk - 1, -1, -1)) if reverse else list(range(n_blk))

    def blk(ref, i):
        return ref[b, i * SUBLANES:(i + 1) * SUBLANES, :]

    def step(sr, si, i):
        return abr * sr - abi * si + blk(re_ref, i), abr * si + abi * sr + blk(im_ref, i)

    fr, fi = blk(re_ref, order[0]), blk(im_ref, order[0])
    for i in order[1:]:
        fr, fi = step(fr, fi, i)

    mr, mi = ar, ai
    for _ in range(n_blk.bit_length() - 1):
        mr, mi = _cmul((mr, mi), (mr, mi))
    levels, _ = _scan_tables(mr, mi, width, reverse)
    row = lax.broadcasted_iota(jnp.int32, shape, 0)
    edge_in = SUBLANES - 1 if reverse else 0
    sh1 = SUBLANES - 1 if reverse else 1
    gr = jnp.where(row == edge_in, jnp.broadcast_to(car_ref[b, 0:1, :], shape), pltpu.roll(fr, sh1, 0))
    gi = jnp.where(row == edge_in, jnp.broadcast_to(car_ref[b, 1:2, :], shape), pltpu.roll(fi, sh1, 0))
    for (lr, li), k in zip(levels, (1, 2, 4)):
        sh = (SUBLANES - k) if reverse else k
        sr = pltpu.roll(gr, sh, 0)
        si = pltpu.roll(gi, sh, 0)
        gr, gi = gr + (lr * sr - li * si), gi + (lr * si + li * sr)
    mbr = jnp.broadcast_to(mr, shape)
    mbi = jnp.broadcast_to(mi, shape)
    edge_out = 0 if reverse else SUBLANES - 1
    car_ref[b, 0:1, :] = (fr + (mbr * gr - mbi * gi))[edge_out:edge_out + 1, :]
    car_ref[b, 1:2, :] = (fi + (mbr * gi + mbi * gr))[edge_out:edge_out + 1, :]

    sr, si = gr, gi
    for i in order:
        sr, si = step(sr, si, i)
        re_ref[b, i * SUBLANES:(i + 1) * SUBLANES, :] = sr
        im_ref[b, i * SUBLANES:(i + 1) * SUBLANES, :] = si
        if on_block is not None:
            on_block(i, sr, si)


def _ssm_fwd(u, bb_re, bb_im, c_re_t, c_imn_t, d_row, ab_re, ab_im, n_seq, seq):
    tt = SCAN_TILE
    nt = seq // tt

    def body(u_ref, bbre, bbim, cre, cimn, d_ref, are, aim, sre_ref, sim_ref, y_ref, up_ref, car_ref):
        @pl.when(pl.program_id(1) == 0)
        def _():
            car_ref[...] = jnp.zeros_like(car_ref)

        for b in range(n_seq):
            _load_chunked(u_ref, b, up_ref, tt)
            up = up_ref[b]
            ub = up.astype(BF16)
            sre_ref[b] = _dot(ub, bbre[0])
            sim_ref[b] = _dot(ub, bbim[0])
            _chunk_scan(sre_ref, sim_ref, b, car_ref, are[...], aim[...], tt, reverse=False)
            yp = (_dot(sre_ref[b].astype(BF16), cre[0]) + _dot(sim_ref[b].astype(BF16), cimn[0])
                  + d_ref[...] * up)
            _store_chunked(yp, y_ref, b, tt)

    tok = lambda j, t: (0, t, j)
    blk3 = lambda j, t: (j, 0, 0)
    row = lambda j, t: (0, j)
    st = jax.ShapeDtypeStruct((n_seq, seq, N_JBLK * JB_ST), F32)
    return _pcall(
        body, name="ssm_fwd", grid=(N_JBLK, nt),
        out_shape=(st, st, jax.ShapeDtypeStruct((n_seq, seq, SSM_W), F32)),
        in_specs=[pl.BlockSpec((n_seq, tt, JB_CH), tok),
                  pl.BlockSpec((1, JB_CH, JB_ST), blk3), pl.BlockSpec((1, JB_CH, JB_ST), blk3),
                  pl.BlockSpec((1, JB_ST, JB_CH), blk3), pl.BlockSpec((1, JB_ST, JB_CH), blk3),
                  pl.BlockSpec((1, JB_CH), row), pl.BlockSpec((1, JB_ST), row), pl.BlockSpec((1, JB_ST), row)],
        out_specs=(pl.BlockSpec((n_seq, tt, JB_ST), tok), pl.BlockSpec((n_seq, tt, JB_ST), tok),
                   pl.BlockSpec((n_seq, tt, JB_CH), tok)),
        scratch_shapes=[pltpu.VMEM((n_seq, tt, JB_CH), F32), pltpu.VMEM((n_seq, SUBLANES, JB_ST), F32)],
        compiler_params=_params(2),
    )(u, bb_re, bb_im, c_re_t, c_imn_t, d_row, ab_re, ab_im)


def _ssm_bwd(dy, u, s_re, s_im, bb_re, bb_im, c_re_t, c_imn_t, d_row, ab_re, ab_im, g_out, g_glu, n_seq, seq):
    tt = SCAN_TILE
    nt = seq // tt
    rows8 = tt // SUBLANES

    def body(dy_ref, u_ref, sre_ref, sim_ref, pre_ref, pim_ref, bbre, bbim, cre, cimn, d_ref, are, aim,
             gout_ref, gglu_ref,
             du_ref, dcre_ref, dcim_ref, dbbre_ref, dbbim_ref, dare_ref, daim_ref, dd_ref, rout_ref, rglu_ref,
             lre_ref, lim_ref, dyp_ref, up_ref, car_ref, send_sems, recv_sems, loc_sems):
        j = pl.program_id(0)
        tr = pl.program_id(1)

        def exchange():
            return _direct_copies(lambda pid: [gout_ref.at[pid], gglu_ref.at[pid]], [rout_ref, rglu_ref],
                                  send_sems, recv_sems, loc_sems)

        @pl.when((j == 0) & (tr == 0))
        def _():
            mine, sends = exchange()
            for cp in mine + sends:
                cp.start()

        @pl.when(tr == 0)
        def _():
            car_ref[...] = jnp.zeros_like(car_ref)
            for r in (dcre_ref, dcim_ref, dbbre_ref, dbbim_ref, dare_ref, daim_ref, dd_ref):
                r[...] = jnp.zeros_like(r)

        first = tr == nt - 1
        row = lax.broadcasted_iota(jnp.int32, (SUBLANES, JB_ST), 0)
        n_blk = tt // SUBLANES
        for b in range(n_seq):
            _load_chunked(dy_ref, b, dyp_ref, tt)
            _load_chunked(u_ref, b, up_ref, tt)
            dyp = dyp_ref[b]
            up = up_ref[b]
            dyb = dyp.astype(BF16)
            ub = up.astype(BF16)
            lre_ref[b] = _dot_nt(dyb, cre[0])
            lim_ref[b] = _dot_nt(dyb, cimn[0])
            acc = [jnp.zeros((SUBLANES, JB_ST), F32), jnp.zeros((SUBLANES, JB_ST), F32)]

            def on_block(i, lr, li, b=b, acc=acc):
                if i > 0:
                    spr = sre_ref[b, (i - 1) * SUBLANES:i * SUBLANES, :]
                    spi = sim_ref[b, (i - 1) * SUBLANES:i * SUBLANES, :]
                else:
                    hr = jnp.where(first, 0.0, pre_ref[b, SUBLANES - 1:SUBLANES, :])
                    hi = jnp.where(first, 0.0, pim_ref[b, SUBLANES - 1:SUBLANES, :])
                    last_r = sre_ref[b, (n_blk - 1) * SUBLANES:n_blk * SUBLANES, :]
                    last_i = sim_ref[b, (n_blk - 1) * SUBLANES:n_blk * SUBLANES, :]
                    spr = jnp.where(row == 0, jnp.broadcast_to(hr, row.shape), pltpu.roll(last_r, 1, 0))
                    spi = jnp.where(row == 0, jnp.broadcast_to(hi, row.shape), pltpu.roll(last_i, 1, 0))
                acc[0] = acc[0] + (lr * spr + li * spi)
                acc[1] = acc[1] + (li * spr - lr * spi)

            _chunk_scan(lre_ref, lim_ref, b, car_ref, are[...], -aim[...], tt, reverse=True, on_block=on_block)
            dare_ref[...] += jnp.sum(acc[0], axis=0, keepdims=True)
            daim_ref[...] += jnp.sum(acc[1], axis=0, keepdims=True)
            lrb = lre_ref[b].astype(BF16)
            lib = lim_ref[b].astype(BF16)
            dup = d_ref[...] * dyp + _dot_nt(lrb, bbre[0]) + _dot_nt(lib, bbim[0])
            _store_chunked(dup, du_ref, b, tt)
            dbbre_ref[0] += _dot_tn(ub, lrb)
            dbbim_ref[0] += _dot_tn(ub, lib)
            dcre_ref[0] += _dot_tn(dyb, sre_ref[b].astype(BF16))
            dcim_ref[0] += _dot_tn(dyb, sim_ref[b].astype(BF16))
            dd_ref[...] += jnp.sum(dyp * up, axis=0, keepdims=True)

        @pl.when((j == N_JBLK - 1) & (tr == nt - 1))
        def _():
            mine, sends = exchange()
            for cp in sends + mine:
                cp.wait()

    tok = lambda j, t: (0, nt - 1 - t, j)
    halo = lambda j, t: (0, jnp.maximum((nt - 1 - t) * rows8 - 1, 0), j)
    blk3 = lambda j, t: (j, 0, 0)
    row1 = lambda j, t: (0, j)
    acc_shape = jax.ShapeDtypeStruct((N_JBLK, JB_CH, JB_ST), F32)
    return _pcall(
        body, name="ssm_bwd", grid=(N_JBLK, nt),
        out_shape=(jax.ShapeDtypeStruct((n_seq, seq, SSM_W), F32), acc_shape, acc_shape, acc_shape, acc_shape,
                   jax.ShapeDtypeStruct((1, N_JBLK * JB_ST), F32), jax.ShapeDtypeStruct((1, N_JBLK * JB_ST), F32),
                   jax.ShapeDtypeStruct((1, SSM_W), F32),
                   jax.ShapeDtypeStruct((N_DEV,) + g_out.shape[1:], F32),
                   jax.ShapeDtypeStruct((N_DEV,) + g_glu.shape[1:], F32)),
        in_specs=[pl.BlockSpec((n_seq, tt, JB_CH), tok), pl.BlockSpec((n_seq, tt, JB_CH), tok),
                  pl.BlockSpec((n_seq, tt, JB_ST), tok), pl.BlockSpec((n_seq, tt, JB_ST), tok),
                  pl.BlockSpec((n_seq, SUBLANES, JB_ST), halo), pl.BlockSpec((n_seq, SUBLANES, JB_ST), halo),
                  pl.BlockSpec((1, JB_CH, JB_ST), blk3), pl.BlockSpec((1, JB_CH, JB_ST), blk3),
                  pl.BlockSpec((1, JB_ST, JB_CH), blk3), pl.BlockSpec((1, JB_ST, JB_CH), blk3),
                  pl.BlockSpec((1, JB_CH), row1), pl.BlockSpec((1, JB_ST), row1), pl.BlockSpec((1, JB_ST), row1),
                  HBM_SPEC, HBM_SPEC],
        out_specs=(pl.BlockSpec((n_seq, tt, JB_CH), tok),
                   pl.BlockSpec((1, JB_CH, JB_ST), blk3), pl.BlockSpec((1, JB_CH, JB_ST), blk3),
                   pl.BlockSpec((1, JB_CH, JB_ST), blk3), pl.BlockSpec((1, JB_CH, JB_ST), blk3),
                   pl.BlockSpec((1, JB_ST), row1), pl.BlockSpec((1, JB_ST), row1), pl.BlockSpec((1, JB_CH), row1),
                   HBM_SPEC, HBM_SPEC),
        scratch_shapes=[pltpu.VMEM((n_seq, tt, JB_ST), F32), pltpu.VMEM((n_seq, tt, JB_ST), F32),
                        pltpu.VMEM((n_seq, tt, JB_CH), F32), pltpu.VMEM((n_seq, tt, JB_CH), F32),
                        pltpu.VMEM((n_seq, SUBLANES, JB_ST), F32),
                        pltpu.SemaphoreType.DMA((7 * 2,)), pltpu.SemaphoreType.DMA((7 * 2,)),
                        pltpu.SemaphoreType.DMA((2,))],
        compiler_params=_params(2),
    )(dy, u, s_re, s_im, s_re, s_im, bb_re, bb_im, c_re_t, c_imn_t, d_row, ab_re, ab_im, g_out, g_glu)


def _mix(x2, tgt2, y, zs, h, bc, cc, zc, gf, b_glu, conv8, w_glu_f, w_out_f, seq):
    n = x2.shape[0]
    tm = TOK_TILE
    tiles_per_seq = seq // tm
    rows8 = tm // SUBLANES

    def body(x_ref, t_ref, y_ref, zs_ref, h_ref, bc_ref, cc_ref, zc_ref, hp_ref, ccp_ref,
             gf_ref, bg_ref, cw_ref, wg_ref, wo_ref,
             dh2_ref, dy_ref, dzs_ref, dbc_ref, dzc_ref, dyc_ref,
             dwo_ref, dwg_ref, loss_ref, dgf_ref, dbg_ref, dcw_ref):
        i = pl.program_id(0)

        @pl.when(i == 0)
        def _():
            for r in (dwo_ref, dwg_ref, loss_ref, dgf_ref, dbg_ref, dcw_ref):
                r[...] = jnp.zeros_like(r)

        yv = y_ref[...]
        y1, dgelu = _gelu_and_grad(yv)
        y1b = y1.astype(BF16)
        gate = _sigmoid(_dot(y1b, wg_ref[...]) + bg_ref[...])
        y2 = y1 * gate
        szs, dszs = _silu_and_grad(zs_ref[...])
        yssm = y2 * szs
        hv = h_ref[...]
        ccv = cc_ref[...]
        bcv = bc_ref[...]
        v = ccv * hv
        first = (i % tiles_per_seq) == 0
        vhalo = jnp.where(first, 0.0, ccp_ref[...] * hp_ref[...])
        v1 = _shift_down(v, vhalo, 1)
        v2 = _shift_down(v, vhalo, 2)
        w0 = cw_ref[0:1, :]
        w1 = cw_ref[1:2, :]
        w2 = cw_ref[2:3, :]
        yc = w0 * v2 + w1 * v1 + w2 * v
        szc, dszc = _silu_and_grad(zc_ref[...])
        yconv = (bcv * yc) * szc
        ysb = yssm.astype(BF16)
        ycb = yconv.astype(BF16)
        h2 = x_ref[...] + _dot(ysb, wo_ref[0:SSM_W, :]) + _dot(ycb, wo_ref[SSM_W:, :])
        r2 = lax.rsqrt(jnp.mean(h2 * h2, axis=-1, keepdims=True) + EPS)
        hn = h2 * r2
        gfv = gf_ref[...]
        err = hn * gfv - t_ref[...]
        loss_ref[...] += 0.5 * jnp.sum(jnp.mean(err * err, axis=-1, keepdims=True))
        dout = err * (1.0 / D_MODEL)
        dgf_ref[...] += jnp.sum(dout * hn, axis=0, keepdims=True)
        dn = dout * gfv
        dh2 = r2 * (dn - hn * jnp.mean(dn * hn, axis=-1, keepdims=True))
        dh2_ref[...] = dh2
        dh2b = dh2.astype(BF16)
        dwo_ref[0:SSM_W, :] += _dot_tn(ysb, dh2b)
        dwo_ref[SSM_W:, :] += _dot_tn(ycb, dh2b)
        dyssm = _dot_nt(dh2b, wo_ref[0:SSM_W, :])
        dyconv = _dot_nt(dh2b, wo_ref[SSM_W:, :])
        dy2 = dyssm * szs
        dzs_ref[...] = dyssm * y2 * dszs
        dgp = dy2 * y1 * (gate * (1.0 - gate))
        dgpb = dgp.astype(BF16)
        dy1 = dy2 * gate + _dot_nt(dgpb, wg_ref[...])
        dwg_ref[...] += _dot_tn(y1b, dgpb)
        dbg_ref[...] += jnp.sum(dgp, axis=0, keepdims=True)
        dy_ref[...] = dy1 * dgelu
        dbc_ref[...] = dyconv * yc * szc
        dyc = dyconv * bcv * szc
        dyc_ref[...] = dyc
        dzc_ref[...] = dyconv * bcv * yc * dszc
        dcw_ref[0:1, :] += jnp.sum(dyc * v2, axis=0, keepdims=True)
        dcw_ref[1:2, :] += jnp.sum(dyc * v1, axis=0, keepdims=True)
        dcw_ref[2:3, :] += jnp.sum(dyc * v, axis=0, keepdims=True)

    tile_d = pl.BlockSpec((tm, D_MODEL), lambda i: (i, 0))
    tile_s = pl.BlockSpec((tm, SSM_W), lambda i: (i, 0))
    halo = pl.BlockSpec((SUBLANES, SSM_W), lambda i: (jnp.maximum(i * rows8 - 1, 0), 0))
    const = lambda shape: pl.BlockSpec(shape, lambda i: (0,) * len(shape))
    seg = jax.ShapeDtypeStruct((n, SSM_W), F32)
    return _pcall(
        body, name="mix", grid=(n // tm,),
        out_shape=(jax.ShapeDtypeStruct((n, D_MODEL), F32), seg, seg, seg, seg, seg,
                   jax.ShapeDtypeStruct((D_MODEL, D_MODEL), F32), jax.ShapeDtypeStruct((SSM_W, SSM_W), F32),
                   jax.ShapeDtypeStruct((SUBLANES, LANES), F32), jax.ShapeDtypeStruct((1, D_MODEL), F32),
                   jax.ShapeDtypeStruct((1, SSM_W), F32), jax.ShapeDtypeStruct((SUBLANES, CONV_W), F32)),
        in_specs=[tile_d, tile_d, tile_s, tile_s, tile_s, tile_s, tile_s, tile_s, halo, halo,
                  const((1, D_MODEL)), const((1, SSM_W)), const((SUBLANES, CONV_W)),
                  const((SSM_W, SSM_W)), const((D_MODEL, D_MODEL))],
        out_specs=(tile_d, tile_s, tile_s, tile_s, tile_s, tile_s,
                   const((D_MODEL, D_MODEL)), const((SSM_W, SSM_W)), const((SUBLANES, LANES)),
                   const((1, D_MODEL)), const((1, SSM_W)), const((SUBLANES, CONV_W))),
        compiler_params=_params(1),
    )(x2, tgt2, y, zs, h, bc, cc, zc, h, cc, gf, b_glu, conv8, w_glu_f, w_out_f)


def _in_bwd(x2, dh2, du, dzs, dyc, h, cc, dbc, dzc, g1, conv8, w_full, small, seq):
    n = x2.shape[0]
    tm = TOK_TILE
    n_tiles = n // tm
    tiles_per_seq = seq // tm
    rows8 = tm // SUBLANES
    n_blk8 = n // SUBLANES

    def body(x_ref, dh2_ref, du_ref, dzs_ref, dyc_ref, dycn_ref, h_ref, cc_ref, dbc_ref, dzc_ref,
             g_ref, cw_ref, w_ref, sm_ref, gx_ref, dp_ref, dg_ref, rsm_ref, send_sems, recv_sems, loc_sem):
        i = pl.program_id(0)
        gather = _TwoLevelGather(sm_ref, rsm_ref, send_sems, recv_sems, loc_sem.at[0])

        @pl.when(i == 0)
        def _():
            dg_ref[...] = jnp.zeros_like(dg_ref)
            gather.start()

        @pl.when(i == n_tiles // 2)
        def _():
            gather.forward()

        dyc = dyc_ref[...]
        last = (i % tiles_per_seq) == tiles_per_seq - 1
        nhalo = jnp.where(last, 0.0, dycn_ref[...])
        dv = (cw_ref[2:3, :] * dyc + cw_ref[1:2, :] * _shift_up(dyc, nhalo, 1)
              + cw_ref[0:1, :] * _shift_up(dyc, nhalo, 2))
        parts = (du_ref[...], dzs_ref[...], dv * cc_ref[...], dbc_ref[...], dv * h_ref[...], dzc_ref[...])
        dxn = jnp.zeros((tm, D_MODEL), F32)
        for k, p in enumerate(parts):
            pb = p.astype(BF16)
            dp_ref[:, k * SSM_W:(k + 1) * SSM_W] = pb
            dxn = dxn + _dot_nt(pb, w_ref[:, k * SSM_W:(k + 1) * SSM_W])
        x = x_ref[...]
        r = lax.rsqrt(jnp.mean(x * x, axis=-1, keepdims=True) + EPS)
        xh = x * r
        dg_ref[...] += jnp.sum(dxn * xh, axis=0, keepdims=True)
        dn = dxn * g_ref[...]
        gx_ref[...] = dh2_ref[...] + r * (dn - xh * jnp.mean(dn * xh, axis=-1, keepdims=True))

        @pl.when(i == n_tiles - 1)
        def _():
            gather.finish()

    tile_d = pl.BlockSpec((tm, D_MODEL), lambda i: (i, 0))
    tile_s = pl.BlockSpec((tm, SSM_W), lambda i: (i, 0))
    nhalo = pl.BlockSpec((SUBLANES, SSM_W), lambda i: (jnp.minimum((i + 1) * rows8, n_blk8 - 1), 0))
    const = lambda shape: pl.BlockSpec(shape, lambda i: (0,) * len(shape))
    return _pcall(
        body, name="in_bwd", grid=(n_tiles,),
        out_shape=(jax.ShapeDtypeStruct((n, D_MODEL), F32), jax.ShapeDtypeStruct((n, IN_COLS), BF16),
                   jax.ShapeDtypeStruct((SUBLANES, D_MODEL), F32),
                   jax.ShapeDtypeStruct((N_DEV,) + small.shape, F32)),
        in_specs=[tile_d, tile_d, tile_s, tile_s, tile_s, nhalo, tile_s, tile_s, tile_s, tile_s,
                  const((1, D_MODEL)), const((SUBLANES, CONV_W)), const((D_MODEL, IN_COLS)), HBM_SPEC],
        out_specs=(tile_d, pl.BlockSpec((tm, IN_COLS), lambda i: (i, 0)), const((SUBLANES, D_MODEL)), HBM_SPEC),
        scratch_shapes=[pltpu.SemaphoreType.DMA((7,)), pltpu.SemaphoreType.DMA((7,)), pltpu.SemaphoreType.DMA((1,))],
        compiler_params=_params(1),
    )(x2, dh2, du, dzs, dyc, dyc, h, cc, dbc, dzc, g1, conv8, w_full, small)


def _dw_in_exchange(order, xn, dproj, dg8):
    n = xn.shape[0]
    tk = 512
    nk = n // tk
    piece = (D_MODEL, COLS_PER_DEV)

    def body(order_ref, xn_ref, dp_ref, dg_ref, own_ref, rchip_ref, rdg_ref,
             acc, stage, sbuf, give_send, give_recv, keep_send, keep_recv, dg_send, dg_recv, dg_loc):
        del order_ref
        s = pl.program_id(0)
        x, y, c = _mesh_pos()
        sib = (x, y, 1 - c)
        chips = [(1 - x, y), (x, 1 - y), (1 - x, 1 - y)]

        def dg_copies():
            return _direct_copies(lambda pid: [dg_ref], [rdg_ref], dg_send, dg_recv, dg_loc)

        def give(i):
            return pltpu.make_async_remote_copy(src_ref=acc.at[0], dst_ref=stage.at[i], send_sem=give_send.at[i],
                                                recv_sem=give_recv.at[i], device_id=sib, device_id_type=MESH)

        def keep(i):
            return pltpu.make_async_remote_copy(src_ref=sbuf.at[i], dst_ref=rchip_ref.at[i], send_sem=keep_send.at[i],
                                                recv_sem=keep_recv.at[i], device_id=(*chips[i], c), device_id_type=MESH)

        @pl.when(s == 0)
        def _():
            mine, sends = dg_copies()
            for cp in mine + sends:
                cp.start()

        for k in (2, 4, 6):
            @pl.when(s == k)
            def _(k=k):
                give(k // 2 - 1).wait_send()

        slot = s % 2
        acc[slot] = _dot_tn(xn_ref[pl.ds(0, tk), :], dp_ref[pl.ds(0, tk), :])

        def kstep(kk, carry):
            off = pl.multiple_of(kk * tk, tk)
            acc[slot] += _dot_tn(xn_ref[pl.ds(off, tk), :], dp_ref[pl.ds(off, tk), :])
            return carry

        lax.fori_loop(1, nk, kstep, 0)

        for k in range(N_DEV):
            @pl.when(s == k)
            def _(k=k):
                i = k // 2
                if k % 2 == 0:
                    give(i).start()
                else:
                    give(i).wait_recv()
                    total = acc[1] + stage[i]
                    if i < 3:
                        sbuf[i] = total.astype(BF16)
                        keep(i).start()
                    else:
                        own_ref[...] = total

        @pl.when(s == N_DEV - 1)
        def _():
            give(3).wait_send()
            for i in range(3):
                keep(i).wait()
            mine, sends = dg_copies()
            for cp in sends + mine:
                cp.wait()

    grid_spec = pltpu.PrefetchScalarGridSpec(
        num_scalar_prefetch=1, grid=(N_DEV,),
        in_specs=[pl.BlockSpec(memory_space=pltpu.VMEM),
                  pl.BlockSpec((n, COLS_PER_DEV), lambda s, order: (0, order[s])),
                  HBM_SPEC],
        out_specs=(pl.BlockSpec(piece, lambda s, order: (0, 0)), HBM_SPEC, HBM_SPEC),
        scratch_shapes=[pltpu.VMEM((2,) + piece, F32), pltpu.VMEM((4,) + piece, F32), pltpu.VMEM((3,) + piece, BF16),
                        pltpu.SemaphoreType.DMA((4,)), pltpu.SemaphoreType.DMA((4,)),
                        pltpu.SemaphoreType.DMA((3,)), pltpu.SemaphoreType.DMA((3,)),
                        pltpu.SemaphoreType.DMA((7,)), pltpu.SemaphoreType.DMA((7,)), pltpu.SemaphoreType.DMA((1,))])
    return _pcall(
        body, name="dw_in_exchange", grid_spec=grid_spec,
        out_shape=(jax.ShapeDtypeStruct(piece, F32), jax.ShapeDtypeStruct((3,) + piece, BF16),
                   jax.ShapeDtypeStruct((N_DEV,) + dg8.shape, F32)),
        compiler_params=_params(1),
    )(order, xn, dproj, dg8)


def _adamw(g, w, m, v):
    m_new = ADAM_B1 * m + (1.0 - ADAM_B1) * g
    v_new = ADAM_B2 * v + (1.0 - ADAM_B2) * (g * g)
    m_hat = m_new / (1.0 - ADAM_B1 ** ADAM_STEP)
    v_hat = v_new / (1.0 - ADAM_B2 ** ADAM_STEP)
    delta = -ADAM_LR * (m_hat / (jnp.sqrt(v_hat) + ADAM_EPS) + ADAM_WD * w)
    return delta, m_new, v_new


def _reduce_adam(recv, w, m, v, name, row_tile):
    rows, cols = w.shape

    def body(r_ref, w_ref, m_ref, v_ref, g_ref, d_ref, nm_ref, nv_ref):
        g = r_ref[0]
        for s in range(1, N_DEV):
            g = g + r_ref[s]
        g_ref[...] = g
        d_ref[...], nm_ref[...], nv_ref[...] = _adamw(g, w_ref[...], m_ref[...], v_ref[...])

    tile = pl.BlockSpec((row_tile, cols), lambda i: (i, 0))
    shp = jax.ShapeDtypeStruct((rows, cols), F32)
    return _pcall(
        body, name=name, grid=(rows // row_tile,),
        out_shape=(shp,) * 4,
        in_specs=[pl.BlockSpec((N_DEV, row_tile, cols), lambda i: (0, i, 0)), tile, tile, tile],
        out_specs=(tile,) * 4,
        compiler_params=_params(1),
    )(recv, w, m, v)


def _reduce_adam_w_in(own, rchip, w, m, v):
    rows, cols = w.shape
    row_tile = 256

    def body(o_ref, r_ref, w_ref, m_ref, v_ref, g_ref, d_ref, nm_ref, nv_ref):
        g = o_ref[...]
        for s in range(3):
            g = g + r_ref[s].astype(F32)
        g_ref[...] = g
        d_ref[...], nm_ref[...], nv_ref[...] = _adamw(g, w_ref[...], m_ref[...], v_ref[...])

    tile = pl.BlockSpec((row_tile, cols), lambda i: (i, 0))
    shp = jax.ShapeDtypeStruct((rows, cols), F32)
    return _pcall(
        body, name="reduce_adam_w_in", grid=(rows // row_tile,),
        out_shape=(shp,) * 4,
        in_specs=[tile, pl.BlockSpec((3, row_tile, cols), lambda i: (0, i, 0)), tile, tile, tile],
        out_specs=(tile,) * 4,
        compiler_params=_params(1),
    )(own, rchip, w, m, v)


def _reduce_adam_stacked(recv, wmv, name):
    _, rows, cols = wmv.shape

    def body(r_ref, p_ref, o_ref):
        g = r_ref[0]
        for s in range(1, N_DEV):
            g = g + r_ref[s]
        o_ref[0] = g
        o_ref[1], o_ref[2], o_ref[3] = _adamw(g, p_ref[0], p_ref[1], p_ref[2])

    return _pcall(body, name=name, out_shape=jax.ShapeDtypeStruct((4, rows, cols), F32),
                  compiler_params=_params(0))(recv, wmv)


_SMALL = (("final_norm_gain", D_MODEL), ("b_glu", SSM_W),
          ("ssm_a_re", N_GROUPS * STATE), ("ssm_a_im", N_GROUPS * STATE), ("ssm_log_dt", N_GROUPS),
          ("ssm_b_re", N_GROUPS * STATE * GROUP), ("ssm_b_im", N_GROUPS * STATE * GROUP),
          ("ssm_c_re", N_GROUPS * STATE * GROUP), ("ssm_c_im", N_GROUPS * STATE * GROUP),
          ("ssm_d", N_GROUPS * GROUP), ("conv_w", 3 * CONV_W))
_PACK_UNIT = SUBLANES * LANES


def _pack_small(dicts):
    cols = []
    for name, size in _SMALL:
        flat = jnp.stack([d[name].reshape(-1) for d in dicts])
        padded = -(-size // _PACK_UNIT) * _PACK_UNIT
        cols.append(jnp.pad(flat, ((0, 0), (0, padded - size))).reshape(len(dicts), -1, LANES))
    return jnp.concatenate(cols, axis=1)


def _unpack_small(packed):
    out, r0 = {}, 0
    for name, size in _SMALL:
        nrows = -(-size // _PACK_UNIT) * SUBLANES
        out[name] = packed[:, r0:r0 + nrows].reshape(packed.shape[0], -1)[:, :size]
        r0 += nrows
    return out


def _block_diag(m4):
    eye = jnp.eye(SUBLANES, dtype=m4.dtype)
    j, g, a, b = m4.shape
    return jnp.einsum("jgab,gk->jgakb", m4, eye).reshape(j, g * a, g * b)


def _block_diag_extract(dense, a, b):
    d5 = dense.reshape(N_JBLK, SUBLANES, a, SUBLANES, b)
    return jnp.stack([d5[:, g, :, g, :] for g in range(SUBLANES)], axis=1)


def kernel(x, norm_gain, w_in, ssm_a_re, ssm_a_im, ssm_log_dt, ssm_b_re, ssm_b_im, ssm_c_re, ssm_c_im, ssm_d, w_glu, b_glu, conv_w, w_out, final_norm_gain, loss_target, m_norm_gain, m_w_in, m_ssm_a_re, m_ssm_a_im, m_ssm_log_dt, m_ssm_b_re, m_ssm_b_im, m_ssm_c_re, m_ssm_c_im, m_ssm_d, m_w_glu, m_b_glu, m_conv_w, m_w_out, m_final_norm_gain, v_norm_gain, v_w_in, v_ssm_a_re, v_ssm_a_im, v_ssm_log_dt, v_ssm_b_re, v_ssm_b_im, v_ssm_c_re, v_ssm_c_im, v_ssm_d, v_w_glu, v_b_glu, v_conv_w, v_w_out, v_final_norm_gain):
    n_seq, seq, _ = x.shape
    n = n_seq * seq
    me = 4 * lax.axis_index("x") + 2 * lax.axis_index("y") + lax.axis_index("c")

    conv_p = jnp.pad(conv_w[0], ((0, SUBLANES - 3), (0, LANES - CONV_COLS_PER_DEV)))
    w_in_f, w_out_f, w_glu_f, conv_all = _gather_weights(
        w_in[0].astype(BF16), w_out[0].astype(BF16), w_glu[0].astype(BF16), conv_p)
    conv8 = jnp.transpose(conv_all[:, :, :CONV_COLS_PER_DEV], (1, 0, 2)).reshape(SUBLANES, CONV_W)

    rep = lambda a: jnp.repeat(a[0], GROUP, axis=1)
    a_re_r, a_im_r = rep(ssm_a_re), rep(ssm_a_im)
    log_dt = ssm_log_dt[0].reshape(N_GROUPS, 1)
    b_re2 = ssm_b_re[0].reshape(N_GROUPS, STATE * GROUP)
    b_im2 = ssm_b_im[0].reshape(N_GROUPS, STATE * GROUP)
    ab_re_r, ab_im_r, bb_re2, bb_im2 = _ssm_disc(a_re_r, a_im_r, log_dt, b_re2, b_im2)
    ab_re = ab_re_r[:, ::GROUP].reshape(1, N_GROUPS * STATE)
    ab_im = ab_im_r[:, ::GROUP].reshape(1, N_GROUPS * STATE)

    def bb_mat(bb2):
        t = jnp.transpose(bb2.reshape(N_JBLK, SUBLANES, STATE, GROUP), (0, 1, 3, 2))
        return _block_diag(t).astype(BF16)

    def c_mat(c3, sign):
        t = jnp.transpose(c3.reshape(N_JBLK, SUBLANES, GROUP, STATE), (0, 1, 3, 2))
        return _block_diag(sign * t).astype(BF16)

    bb_re_m, bb_im_m = bb_mat(bb_re2), bb_mat(bb_im2)
    c_re_m, c_imn_m = c_mat(ssm_c_re[0], 1.0), c_mat(ssm_c_im[0], -1.0)
    d_row = ssm_d[0].reshape(1, SSM_W)

    x2 = x.reshape(n, D_MODEL)
    tgt2 = loss_target.reshape(n, D_MODEL)
    xn, u, zs, h, bc, cc, zc = _in_proj(x2, norm_gain, w_in_f)
    u3 = u.reshape(n_seq, seq, SSM_W)
    s_re, s_im, y3 = _ssm_fwd(u3, bb_re_m, bb_im_m, c_re_m, c_imn_m, d_row, ab_re, ab_im, n_seq, seq)
    (dh2, dy, dzs, dbc, dzc, dyc, dw_out, dw_glu, loss_t, dgf, dbg, dcw) = _mix(
        x2, tgt2, y3.reshape(n, SSM_W), zs, h, bc, cc, zc, final_norm_gain.reshape(1, D_MODEL), b_glu, conv8,
        w_glu_f, w_out_f, seq)

    du3, dc_re_d, dc_im_d, dbb_re_d, dbb_im_d, dab_re, dab_im, dd, r_out, r_glu = _ssm_bwd(
        dy.reshape(n_seq, seq, SSM_W), u3, s_re, s_im, bb_re_m, bb_im_m, c_re_m, c_imn_m, d_row, ab_re, ab_im,
        dw_out.reshape(N_DEV, OUT_ROWS_PER_DEV, D_MODEL), dw_glu.reshape(N_DEV, GLU_ROWS_PER_DEV, SSM_W), n_seq, seq)
    du = du3.reshape(n, SSM_W)
    g_c_re = _block_diag_extract(dc_re_d, GROUP, STATE).reshape(N_GROUPS, GROUP, STATE)
    g_c_im = -_block_diag_extract(dc_im_d, GROUP, STATE).reshape(N_GROUPS, GROUP, STATE)

    def bb_grad(dense):
        t = _block_diag_extract(dense, GROUP, STATE)
        return jnp.transpose(t, (0, 1, 3, 2)).reshape(N_GROUPS, STATE * GROUP)

    def ab_grad(row):
        z = jnp.zeros((N_GROUPS, STATE, GROUP), F32)
        return z.at[:, :, 0].set(row.reshape(N_GROUPS, STATE)).reshape(N_GROUPS, STATE * GROUP)

    g_are_r, g_aim_r, g_ldt, g_bre2, g_bim2 = _ssm_disc_bwd(
        a_re_r, a_im_r, log_dt, b_re2, b_im2, ab_grad(dab_re), ab_grad(dab_im),
        bb_grad(dbb_re_d), bb_grad(dbb_im_d))
    small_grads = {"final_norm_gain": dgf, "b_glu": dbg,
                   "ssm_a_re": g_are_r[:, ::GROUP], "ssm_a_im": g_aim_r[:, ::GROUP], "ssm_log_dt": g_ldt,
                   "ssm_b_re": g_bre2, "ssm_b_im": g_bim2, "ssm_c_re": g_c_re, "ssm_c_im": g_c_im,
                   "ssm_d": dd, "conv_w": dcw[0:3]}
    grad_x2, dproj, dg8, r_small = _in_bwd(x2, dh2, du, dzs, dyc, h, cc, dbc, dzc, norm_gain, conv8, w_in_f,
                                           _pack_small([small_grads])[0], seq)

    mx, my, mc = lax.axis_index("x"), lax.axis_index("y"), lax.axis_index("c")
    order = []
    for cx, cy in ((1 - mx, my), (mx, 1 - my), (1 - mx, 1 - my), (mx, my)):
        order += [4 * cx + 2 * cy + (1 - mc), 4 * cx + 2 * cy + mc]
    own_in, rchip_in, r_dg = _dw_in_exchange(jnp.stack(order).astype(jnp.int32), xn, dproj, dg8)

    def conv_full(shard):
        return lax.dynamic_update_slice(jnp.zeros((3, CONV_W), F32), shard[0], (0, me * CONV_COLS_PER_DEV))

    triples = dict(final_norm_gain=(final_norm_gain, m_final_norm_gain, v_final_norm_gain),
                   b_glu=(b_glu, m_b_glu, v_b_glu), ssm_a_re=(ssm_a_re, m_ssm_a_re, v_ssm_a_re),
                   ssm_a_im=(ssm_a_im, m_ssm_a_im, v_ssm_a_im), ssm_log_dt=(ssm_log_dt, m_ssm_log_dt, v_ssm_log_dt),
                   ssm_b_re=(ssm_b_re, m_ssm_b_re, v_ssm_b_re), ssm_b_im=(ssm_b_im, m_ssm_b_im, v_ssm_b_im),
                   ssm_c_re=(ssm_c_re, m_ssm_c_re, v_ssm_c_re), ssm_c_im=(ssm_c_im, m_ssm_c_im, v_ssm_c_im),
                   ssm_d=(ssm_d, m_ssm_d, v_ssm_d),
                   conv_w=(conv_full(conv_w), conv_full(m_conv_w), conv_full(v_conv_w)))
    wmv_small = _pack_small([{k: t[i] for k, t in triples.items()} for i in range(3)])
    wmv_gain = jnp.pad(jnp.stack([norm_gain, m_norm_gain, v_norm_gain]), ((0, 0), (0, SUBLANES - 1), (0, 0)))

    res_in = _reduce_adam_w_in(own_in, rchip_in, w_in[0], m_w_in[0], v_w_in[0])
    res_out = _reduce_adam(r_out, w_out[0], m_w_out[0], v_w_out[0], "reduce_adam_w_out", OUT_ROWS_PER_DEV)
    res_glu = _reduce_adam(r_glu, w_glu[0], m_w_glu[0], v_w_glu[0], "reduce_adam_w_glu", GLU_ROWS_PER_DEV)
    small = _unpack_small(_reduce_adam_stacked(r_small, wmv_small, "reduce_adam_small"))
    res_gain = _reduce_adam_stacked(r_dg, wmv_gain, "reduce_adam_gain")

    loss = lax.psum(loss_t[0, 0], ("x", "y", "c"))

    shapes = dict(ssm_a_re=(1, N_GROUPS, STATE), ssm_a_im=(1, N_GROUPS, STATE),
                  ssm_log_dt=(1, N_GROUPS), ssm_b_re=(1, N_GROUPS, STATE, GROUP), ssm_b_im=(1, N_GROUPS, STATE, GROUP),
                  ssm_c_re=(1, N_GROUPS, GROUP, STATE), ssm_c_im=(1, N_GROUPS, GROUP, STATE),
                  ssm_d=(1, N_GROUPS, GROUP), b_glu=(1, SSM_W), final_norm_gain=(D_MODEL,))
    big = dict(w_in=res_in, w_glu=res_glu, w_out=res_out)
    small4 = {name: small[name].reshape((4,) + shp) for name, shp in shapes.items()}
    conv4 = lax.dynamic_slice(small["conv_w"].reshape(4, 1, 3, CONV_W), (0, 0, 0, me * CONV_COLS_PER_DEV),
                              (4, 1, 3, CONV_COLS_PER_DEV))

    def leaf(kind, name):
        if name in big:
            return big[name][kind][None]
        if name == "norm_gain":
            return res_gain[kind, 0:1, :]
        if name == "conv_w":
            return conv4[kind]
        return small4[name][kind]

    order = ["norm_gain", "w_in", "ssm_a_re", "ssm_a_im", "ssm_log_dt", "ssm_b_re", "ssm_b_im", "ssm_c_re",
             "ssm_c_im", "ssm_d", "w_glu", "b_glu", "conv_w", "w_out", "final_norm_gain"]
    outs = [loss, grad_x2.reshape(x.shape)]
    for kind in range(4):
        outs += [leaf(kind, name) for name in order]
    return tuple(outs)
```

```python
import functools
import math

import jax
import jax.numpy as jnp
from jax import lax
from jax.experimental import pallas as pl
from jax.experimental.pallas import tpu as pltpu

F32 = jnp.float32
BF16 = jnp.bfloat16

N_DEV = 8
D_MODEL = 1024
SSM_W = 512
CONV_W = 512
N_GROUPS = 32
GROUP = 16
STATE = 64
IN_COLS = 3072
SEG_U, SEG_ZS, SEG_H, SEG_BC, SEG_CC, SEG_ZC = range(6)
COLS_PER_DEV = IN_COLS // N_DEV
OUT_ROWS_PER_DEV = D_MODEL // N_DEV
GLU_ROWS_PER_DEV = SSM_W // N_DEV
CONV_COLS_PER_DEV = CONV_W // N_DEV
EPS = 1e-6

N_JBLK = 4
JB_CH = SSM_W // N_JBLK
JB_ST = N_GROUPS * STATE // N_JBLK

ADAM_LR = 0.001
ADAM_B1 = 0.9
ADAM_B2 = 0.999
ADAM_EPS = 1e-08
ADAM_WD = 0.01
ADAM_STEP = 10

SUBLANES = 8
LANES = 128
VMEM_LIMIT = 48 * 1024 * 1024
TOK_TILE = 256
IN_TILE = 1024
SCAN_TILE = 256

MESH = pl.DeviceIdType.MESH
HBM_SPEC = pl.BlockSpec(memory_space=pltpu.HBM)


def _pcall(body, **kw):
    return pl.pallas_call(body, **kw)


def _params(n_grid):
    return pltpu.CompilerParams(dimension_semantics=("arbitrary",) * n_grid,
                                vmem_limit_bytes=VMEM_LIMIT)


def _dot(a, b):
    return jnp.dot(a, b, preferred_element_type=F32)


def _dot_nt(a, b):
    return lax.dot_general(a, b, (((1,), (1,)), ((), ())), preferred_element_type=F32)


def _dot_tn(a, b):
    return lax.dot_general(a, b, (((0,), (0,)), ((), ())), preferred_element_type=F32)


def _sigmoid(z):
    return 1.0 / (1.0 + jnp.exp(-z))


_GELU_C = math.sqrt(2.0 / math.pi)


def _gelu_and_grad(y):
    inner = _GELU_C * (y + 0.044715 * (y * y * y))
    t = jnp.tanh(inner)
    g = 0.5 * y * (1.0 + t)
    dg = 0.5 * (1.0 + t) + 0.5 * y * (1.0 - t * t) * (_GELU_C * (1.0 + 3.0 * 0.044715 * (y * y)))
    return g, dg


def _silu_and_grad(z):
    s = _sigmoid(z)
    return z * s, s * (1.0 + z * (1.0 - s))


def _shift_down(v, halo, k):
    rolled = pltpu.roll(v, k, 0)
    row = lax.broadcasted_iota(jnp.int32, v.shape, 0)
    for r in range(k):
        rolled = jnp.where(row == r, halo[SUBLANES - k + r:SUBLANES - k + r + 1, :], rolled)
    return rolled


def _shift_up(v, halo, k):
    n = v.shape[0]
    rolled = pltpu.roll(v, n - k, 0)
    row = lax.broadcasted_iota(jnp.int32, v.shape, 0)
    for r in range(k):
        rolled = jnp.where(row == n - k + r, halo[r:r + 1, :], rolled)
    return rolled


def _mesh_pos():
    return lax.axis_index("x"), lax.axis_index("y"), lax.axis_index("c")


def _direct_copies(srcs_for, out_refs, send_sems, recv_sems, loc_sems):
    x, y, c = _mesh_pos()
    me_id = 4 * x + 2 * y + c
    n_arr = len(out_refs)
    dsts = [r.at[me_id] for r in out_refs]
    own = srcs_for(me_id)
    mine = [pltpu.make_async_copy(own[a], dsts[a], loc_sems.at[a]) for a in range(n_arr)]
    sends = []
    for k in range(1, N_DEV):
        px, py, pc = x ^ ((k >> 2) & 1), y ^ ((k >> 1) & 1), c ^ (k & 1)
        src = srcs_for(4 * px + 2 * py + pc)
        for a in range(n_arr):
            sends.append(pltpu.make_async_remote_copy(
                src_ref=src[a], dst_ref=dsts[a],
                send_sem=send_sems.at[(k - 1) * n_arr + a], recv_sem=recv_sems.at[(k - 1) * n_arr + a],
                device_id=(px, py, pc), device_id_type=MESH))
    return mine, sends


class _TwoLevelGather:
    def __init__(self, srcs, slots, send_sems, recv_sems, loc_sems):
        self.srcs, self.slots, self.n_arr = srcs, slots, len(srcs)
        self.send_sems, self.recv_sems, self.loc_sems = send_sems, recv_sems, loc_sems
        x, y, c = _mesh_pos()
        self.c = c
        self.me, self.sib = (x, y, c), (x, y, 1 - c)
        self.chips = [(1 - x, y), (x, 1 - y), (1 - x, 1 - y)]

    def _copies(self, k, block, to, from_src=False):
        dev = 4 * block[0] + 2 * block[1] + block[2]
        return [pltpu.make_async_remote_copy(
            src_ref=self.srcs[a] if from_src else self.slots[a](dev), dst_ref=self.slots[a](dev),
            send_sem=self.send_sems.at[k * self.n_arr + a], recv_sem=self.recv_sems.at[k * self.n_arr + a],
            device_id=to, device_id_type=MESH) for a in range(self.n_arr)]

    def _local(self):
        dev = 4 * self.me[0] + 2 * self.me[1] + self.me[2]
        return [pltpu.make_async_copy(self.srcs[a], self.slots[a](dev), self.loc_sems.at[a])
                for a in range(self.n_arr)]

    def start(self):
        for cp in self._local() + self._copies(0, self.me, self.sib, True):
            cp.start()
        for j, chip in enumerate(self.chips):
            for cp in self._copies(1 + j, self.me, (*chip, self.c), True):
                cp.start()

    def wait_own(self):
        for cp in self._local():
            cp.wait()

    def wait_sibling(self):
        for cp in self._copies(0, self.sib, self.me):
            cp.wait_recv()

    def wait_and_pass_on(self, j):
        chip = self.chips[j]
        for cp in self._copies(1 + j, (*chip, self.c), self.me):
            cp.wait_recv()
        for cp in self._copies(4 + j, (*chip, self.c), self.sib):
            cp.start()

    def wait_passed_on(self, j):
        for cp in self._copies(4 + j, (*self.chips[j], 1 - self.c), self.me):
            cp.wait_recv()

    def wait_sends(self):
        for cp in self._copies(0, self.me, self.sib, True):
            cp.wait_send()
        for j, chip in enumerate(self.chips):
            for cp in self._copies(1 + j, self.me, (*chip, self.c), True) + self._copies(4 + j, (*chip, self.c), self.sib):
                cp.wait_send()

    def forward(self):
        for j in range(3):
            self.wait_and_pass_on(j)

    def finish(self):
        self.wait_sibling()
        for j in range(3):
            self.wait_passed_on(j)
        self.wait_sends()
        self.wait_own()


def _disc(a_re, a_im, log_dt, b_re, b_im):
    dt = jnp.exp(log_dt)
    mag = jnp.exp(a_re * dt)
    ab_re = mag * jnp.cos(a_im * dt)
    ab_im = mag * jnp.sin(a_im * dt)
    den = a_re * a_re + a_im * a_im
    p_re = ab_re - 1.0
    p_im = ab_im
    q_re = (p_re * a_re + p_im * a_im) / den
    q_im = (p_im * a_re - p_re * a_im) / den
    bb_re = q_re * b_re - q_im * b_im
    bb_im = q_re * b_im + q_im * b_re
    return ab_re, ab_im, bb_re, bb_im


def _ssm_disc(a_re_r, a_im_r, log_dt, b_re, b_im):
    def body(are, aim, ldt, bre, bim, o_abre, o_abim, o_bbre, o_bbim):
        outs = _disc(are[...], aim[...], ldt[...], bre[...], bim[...])
        for o, v in zip((o_abre, o_abim, o_bbre, o_bbim), outs):
            o[...] = v

    shp = jax.ShapeDtypeStruct(a_re_r.shape, F32)
    return _pcall(body, name="ssm_disc", out_shape=(shp,) * 4)(a_re_r, a_im_r, log_dt, b_re, b_im)


def _ssm_disc_bwd(a_re_r, a_im_r, log_dt, b_re, b_im, g_abre, g_abim, g_bbre, g_bbim):
    width = a_re_r.shape[1]

    def body(are, aim, ldt, bre, bim, gabre, gabim, gbbre, gbbim, o_are, o_aim, o_ldt, o_bre, o_bim):
        _, vjp = jax.vjp(_disc, are[...], aim[...], ldt[...], bre[...], bim[...])
        d_are, d_aim, d_ldt, d_bre, d_bim = vjp((gabre[...], gabim[...], gbbre[...], gbbim[...]))

        def group_sum(v):
            for k in (1, 2, 4, 8):
                v = v + pltpu.roll(v, width - k, 1)
            return v

        o_are[...] = group_sum(d_are)
        o_aim[...] = group_sum(d_aim)
        o_ldt[...] = d_ldt
        o_bre[...] = d_bre
        o_bim[...] = d_bim

    shp = jax.ShapeDtypeStruct(a_re_r.shape, F32)
    return _pcall(body, name="ssm_disc_bwd",
                  out_shape=(shp, shp, jax.ShapeDtypeStruct(log_dt.shape, F32), shp, shp),
                  )(a_re_r, a_im_r, log_dt, b_re, b_im, g_abre, g_abim, g_bbre, g_bbim)


def _in_proj(order, x2, g1, w_in_b):
    n = x2.shape[0]
    tm = min(IN_TILE, n)
    n_tiles = n // tm

    def body(order_ref, x_ref, g_ref, w_ref, xn_ref, proj_ref, wall_ref,
             xn_scr, wbuf, send_sems, recv_sems, loc_sems, out_sem):
        k = pl.program_id(0)
        i = pl.program_id(1)
        gather = _TwoLevelGather([w_ref], [lambda dev: wbuf.at[dev]], send_sems, recv_sems, loc_sems)

        @pl.when((k == 0) & (i == 0))
        def _():
            gather.start()

        arrivals = [gather.wait_own, gather.wait_sibling]
        for j in range(3):
            arrivals += [functools.partial(gather.wait_and_pass_on, j), functools.partial(gather.wait_passed_on, j)]
        for kk, arrived in enumerate(arrivals):
            @pl.when((k == kk) & (i == 0))
            def _(arrived=arrived):
                arrived()

        rows = pl.ds(pl.multiple_of(i * tm, tm), tm)

        @pl.when(k == 0)
        def _():
            x = x_ref[...]
            r = lax.rsqrt(jnp.mean(x * x, axis=-1, keepdims=True) + EPS)
            xn = ((x * r) * g_ref[...]).astype(BF16)
            xn_scr[rows, :] = xn
            xn_ref[...] = xn

        proj_ref[...] = _dot(xn_scr[rows, :], wbuf[order_ref[k]])

        @pl.when((k == N_DEV - 1) & (i == n_tiles - 1))
        def _():
            gather.wait_sends()
            out = pltpu.make_async_copy(wbuf, wall_ref, out_sem.at[0])
            out.start()
            out.wait()

    tile_once = lambda k, i, order: (jnp.where(k == 0, i, n_tiles - 1), 0)
    grid_spec = pltpu.PrefetchScalarGridSpec(
        num_scalar_prefetch=1, grid=(N_DEV, n_tiles),
        in_specs=[pl.BlockSpec((tm, D_MODEL), tile_once),
                  pl.BlockSpec((1, D_MODEL), lambda k, i, order: (0, 0)),
                  HBM_SPEC],
        out_specs=(pl.BlockSpec((tm, D_MODEL), tile_once),
                   pl.BlockSpec((tm, COLS_PER_DEV), lambda k, i, order: (i, order[k])),
                   HBM_SPEC),
        scratch_shapes=[pltpu.VMEM((n, D_MODEL), BF16), pltpu.VMEM((N_DEV, D_MODEL, COLS_PER_DEV), BF16),
                        pltpu.SemaphoreType.DMA((7,)), pltpu.SemaphoreType.DMA((7,)), pltpu.SemaphoreType.DMA((1,)),
                        pltpu.SemaphoreType.DMA((1,))])
    return _pcall(
        body, name="in_proj", grid_spec=grid_spec,
        out_shape=(jax.ShapeDtypeStruct((n, D_MODEL), BF16), jax.ShapeDtypeStruct((n, IN_COLS), F32),
                   jax.ShapeDtypeStruct((N_DEV, D_MODEL, COLS_PER_DEV), BF16)),
        compiler_params=_params(2),
    )(order, x2, g1, w_in_b)


def _cmul(p, q):
    return p[0] * q[0] - p[1] * q[1], p[0] * q[1] + p[1] * q[0]


def _scan_tables(ar, ai, width, reverse):
    pows = [(ar, ai)]
    for _ in range(SUBLANES - 1):
        pows.append(_cmul(pows[-1], (ar, ai)))
    row = lax.broadcasted_iota(jnp.int32, (SUBLANES, width), 0)

    def bc(v):
        return jnp.broadcast_to(v, (SUBLANES, width))

    levels = []
    for k in (1, 2, 4):
        keep = (row <= SUBLANES - 1 - k) if reverse else (row >= k)
        levels.append((jnp.where(keep, bc(pows[k - 1][0]), 0.0), jnp.where(keep, bc(pows[k - 1][1]), 0.0)))
    cre = jnp.zeros((SUBLANES, width), F32)
    cim = jnp.zeros((SUBLANES, width), F32)
    for r in range(SUBLANES):
        e = (SUBLANES - r) if reverse else (r + 1)
        cre = jnp.where(row == r, bc(pows[e - 1][0]), cre)
        cim = jnp.where(row == r, bc(pows[e - 1][1]), cim)
    return levels, (cre, cim)


def _load_chunked(src_ref, b, dst_ref, n_rows):
    n_blk = n_rows // SUBLANES
    for i in range(n_blk):
        dst_ref[b, i * SUBLANES:(i + 1) * SUBLANES, :] = src_ref[b, pl.ds(i, SUBLANES, stride=n_blk), :]


def _store_chunked(val, dst_ref, b, n_rows):
    n_blk = n_rows // SUBLANES
    for i in range(n_blk):
        dst_ref[b, pl.ds(i, SUBLANES, stride=n_blk), :] = val[i * SUBLANES:(i + 1) * SUBLANES, :]


def _chunk_scan(re_ref, im_ref, b, car_ref, ar, ai, n_rows, reverse, on_block=None):
    width = re_ref.shape[2]
    n_blk = n_rows // SUBLANES
    shape = (SUBLANES, width)
    abr = jnp.broadcast_to(ar, shape)
    abi = jnp.broadcast_to(ai, shape)
    order = list(range(n_blk - 1, -1, -1)) if reverse else list(range(n_blk))

    def blk(ref, i):
        return ref[b, i * SUBLANES:(i + 1) * SUBLANES, :]

    def step(sr, si, i):
        return abr * sr - abi * si + blk(re_ref, i), abr * si + abi * sr + blk(im_ref, i)

    fr, fi = blk(re_ref, order[0]), blk(im_ref, order[0])
    for i in order[1:]:
        fr, fi = step(fr, fi, i)

    mr, mi = ar, ai
    for _ in range(n_blk.bit_length() - 1):
        mr, mi = _cmul((mr, mi), (mr, mi))
    levels, _ = _scan_tables(mr, mi, width, reverse)
    row = lax.broadcasted_iota(jnp.int32, shape, 0)
    edge_in = SUBLANES - 1 if reverse else 0
    sh1 = SUBLANES - 1 if reverse else 1
    gr = jnp.where(row == edge_in, jnp.broadcast_to(car_ref[b, 0:1, :], shape), pltpu.roll(fr, sh1, 0))
    gi = jnp.where(row == edge_in, jnp.broadcast_to(car_ref[b, 1:2, :], shape), pltpu.roll(fi, sh1, 0))
    for (lr, li), k in zip(levels, (1, 2, 4)):
        sh = (SUBLANES - k) if reverse else k
        sr = pltpu.roll(gr, sh, 0)
        si = pltpu.roll(gi, sh, 0)
        gr, gi = gr + (lr * sr - li * si), gi + (lr * si + li * sr)
    mbr = jnp.broadcast_to(mr, shape)
    mbi = jnp.broadcast_to(mi, shape)
    edge_out = 0 if reverse else SUBLANES - 1
    car_ref[b, 0:1, :] = (fr + (mbr * gr - mbi * gi))[edge_out:edge_out + 1, :]
    car_ref[b, 1:2, :] = (fi + (mbr * gi + mbi * gr))[edge_out:edge_out + 1, :]

    sr, si = gr, gi
    for i in order:
        sr, si = step(sr, si, i)
        re_ref[b, i * SUBLANES:(i + 1) * SUBLANES, :] = sr
        im_ref[b, i * SUBLANES:(i + 1) * SUBLANES, :] = si
        if on_block is not None:
            on_block(i, sr, si)


def _ssm_fwd(u, bb_re, bb_im, c_re_t, c_imn_t, d_row, ab_re, ab_im, w_out_b, w_glu_b, conv_p, n_seq, seq):
    tt = SCAN_TILE
    nt = seq // tt

    def body(u_ref, bbre, bbim, cre, cimn, d_ref, are, aim, wout_ref, wglu_ref, cw_ref,
             sre_ref, sim_ref, y_ref, oout_ref, oglu_ref, ocw_ref,
             up_ref, car_ref, send_sems, recv_sems, loc_sems):
        j = pl.program_id(0)
        t = pl.program_id(1)
        gather = _TwoLevelGather(
            [wout_ref, wglu_ref, cw_ref],
            [lambda dev: oout_ref.at[pl.ds(pl.multiple_of(dev * OUT_ROWS_PER_DEV, OUT_ROWS_PER_DEV), OUT_ROWS_PER_DEV), :],
             lambda dev: oglu_ref.at[pl.ds(pl.multiple_of(dev * GLU_ROWS_PER_DEV, GLU_ROWS_PER_DEV), GLU_ROWS_PER_DEV), :],
             lambda dev: ocw_ref.at[dev]],
            send_sems, recv_sems, loc_sems)

        @pl.when((j == 0) & (t == 0))
        def _():
            gather.start()

        @pl.when((j == N_JBLK // 2) & (t == 0))
        def _():
            gather.forward()

        @pl.when(t == 0)
        def _():
            car_ref[...] = jnp.zeros_like(car_ref)

        for b in range(n_seq):
            _load_chunked(u_ref, b, up_ref, tt)
            up = up_ref[b]
            ub = up.astype(BF16)
            sre_ref[b] = _dot(ub, bbre[0])
            sim_ref[b] = _dot(ub, bbim[0])
            _chunk_scan(sre_ref, sim_ref, b, car_ref, are[...], aim[...], tt, reverse=False)
            yp = (_dot(sre_ref[b].astype(BF16), cre[0]) + _dot(sim_ref[b].astype(BF16), cimn[0])
                  + d_ref[...] * up)
            _store_chunked(yp, y_ref, b, tt)

        @pl.when((j == N_JBLK - 1) & (t == nt - 1))
        def _():
            gather.finish()

    tok = lambda j, t: (0, t, j)
    blk3 = lambda j, t: (j, 0, 0)
    row = lambda j, t: (0, j)
    st = jax.ShapeDtypeStruct((n_seq, seq, N_JBLK * JB_ST), F32)
    n_arr = 3
    return _pcall(
        body, name="ssm_fwd", grid=(N_JBLK, nt),
        out_shape=(st, st, jax.ShapeDtypeStruct((n_seq, seq, SSM_W), F32),
                   jax.ShapeDtypeStruct((D_MODEL, D_MODEL), BF16), jax.ShapeDtypeStruct((SSM_W, SSM_W), BF16),
                   jax.ShapeDtypeStruct((N_DEV, SUBLANES, LANES), F32)),
        in_specs=[pl.BlockSpec((n_seq, tt, JB_CH), tok),
                  pl.BlockSpec((1, JB_CH, JB_ST), blk3), pl.BlockSpec((1, JB_CH, JB_ST), blk3),
                  pl.BlockSpec((1, JB_ST, JB_CH), blk3), pl.BlockSpec((1, JB_ST, JB_CH), blk3),
                  pl.BlockSpec((1, JB_CH), row), pl.BlockSpec((1, JB_ST), row), pl.BlockSpec((1, JB_ST), row),
                  HBM_SPEC, HBM_SPEC, HBM_SPEC],
        out_specs=(pl.BlockSpec((n_seq, tt, JB_ST), tok), pl.BlockSpec((n_seq, tt, JB_ST), tok),
                   pl.BlockSpec((n_seq, tt, JB_CH), tok), HBM_SPEC, HBM_SPEC, HBM_SPEC),
        scratch_shapes=[pltpu.VMEM((n_seq, tt, JB_CH), F32), pltpu.VMEM((n_seq, SUBLANES, JB_ST), F32),
                        pltpu.SemaphoreType.DMA((7 * n_arr,)), pltpu.SemaphoreType.DMA((7 * n_arr,)),
                        pltpu.SemaphoreType.DMA((n_arr,))],
        compiler_params=_params(2),
    )(u, bb_re, bb_im, c_re_t, c_imn_t, d_row, ab_re, ab_im, w_out_b, w_glu_b, conv_p)


def _ssm_bwd(dy, u, s_re, s_im, bb_re, bb_im, c_re_t, c_imn_t, d_row, ab_re, ab_im, g_out, g_glu, n_seq, seq):
    tt = SCAN_TILE
    nt = seq // tt
    rows8 = tt // SUBLANES

    def body(dy_ref, u_ref, sre_ref, sim_ref, pre_ref, pim_ref, bbre, bbim, cre, cimn, d_ref, are, aim,
             gout_ref, gglu_ref,
             du_ref, dcre_ref, dcim_ref, dbbre_ref, dbbim_ref, dare_ref, daim_ref, dd_ref, rout_ref, rglu_ref,
             lre_ref, lim_ref, dyp_ref, up_ref, car_ref, send_sems, recv_sems, loc_sems):
        j = pl.program_id(0)
        tr = pl.program_id(1)

        def exchange():
            return _direct_copies(lambda pid: [gout_ref.at[pid], gglu_ref.at[pid]], [rout_ref, rglu_ref],
                                  send_sems, recv_sems, loc_sems)

        @pl.when((j == 0) & (tr == 0))
        def _():
            mine, sends = exchange()
            for cp in mine + sends:
                cp.start()

        @pl.when(tr == 0)
        def _():
            car_ref[...] = jnp.zeros_like(car_ref)
            for r in (dcre_ref, dcim_ref, dbbre_ref, dbbim_ref, dare_ref, daim_ref, dd_ref):
                r[...] = jnp.zeros_like(r)

        first = tr == nt - 1
        row = lax.broadcasted_iota(jnp.int32, (SUBLANES, JB_ST), 0)
        n_blk = tt // SUBLANES
        for b in range(n_seq):
            _load_chunked(dy_ref, b, dyp_ref, tt)
            _load_chunked(u_ref, b, up_ref, tt)
            dyp = dyp_ref[b]
            up = up_ref[b]
            dyb = dyp.astype(BF16)
            ub = up.astype(BF16)
            lre_ref[b] = _dot_nt(dyb, cre[0])
            lim_ref[b] = _dot_nt(dyb, cimn[0])
            acc = [jnp.zeros((SUBLANES, JB_ST), F32), jnp.zeros((SUBLANES, JB_ST), F32)]

            def on_block(i, lr, li, b=b, acc=acc):
                if i > 0:
                    spr = sre_ref[b, (i - 1) * SUBLANES:i * SUBLANES, :]
                    spi = sim_ref[b, (i - 1) * SUBLANES:i * SUBLANES, :]
                else:
                    hr = jnp.where(first, 0.0, pre_ref[b, SUBLANES - 1:SUBLANES, :])
                    hi = jnp.where(first, 0.0, pim_ref[b, SUBLANES - 1:SUBLANES, :])
                    last_r = sre_ref[b, (n_blk - 1) * SUBLANES:n_blk * SUBLANES, :]
                    last_i = sim_ref[b, (n_blk - 1) * SUBLANES:n_blk * SUBLANES, :]
                    spr = jnp.where(row == 0, jnp.broadcast_to(hr, row.shape), pltpu.roll(last_r, 1, 0))
                    spi = jnp.where(row == 0, jnp.broadcast_to(hi, row.shape), pltpu.roll(last_i, 1, 0))
                acc[0] = acc[0] + (lr * spr + li * spi)
                acc[1] = acc[1] + (li * spr - lr * spi)

            _chunk_scan(lre_ref, lim_ref, b, car_ref, are[...], -aim[...], tt, reverse=True, on_block=on_block)
            dare_ref[...] += jnp.sum(acc[0], axis=0, keepdims=True)
            daim_ref[...] += jnp.sum(acc[1], axis=0, keepdims=True)
            lrb = lre_ref[b].astype(BF16)
            lib = lim_ref[b].astype(BF16)
            dup = d_ref[...] * dyp + _dot_nt(lrb, bbre[0]) + _dot_nt(lib, bbim[0])
            _store_chunked(dup, du_ref, b, tt)
            dbbre_ref[0] += _dot_tn(ub, lrb)
            dbbim_ref[0] += _dot_tn(ub, lib)
            dcre_ref[0] += _dot_tn(dyb, sre_ref[b].astype(BF16))
            dcim_ref[0] += _dot_tn(dyb, sim_ref[b].astype(BF16))
            dd_ref[...] += jnp.sum(dyp * up, axis=0, keepdims=True)

        @pl.when((j == N_JBLK - 1) & (tr == nt - 1))
        def _():
            mine, sends = exchange()
            for cp in sends + mine:
                cp.wait()

    tok = lambda j, t: (0, nt - 1 - t, j)
    halo = lambda j, t: (0, jnp.maximum((nt - 1 - t) * rows8 - 1, 0), j)
    blk3 = lambda j, t: (j, 0, 0)
    row1 = lambda j, t: (0, j)
    acc_shape = jax.ShapeDtypeStruct((N_JBLK, JB_CH, JB_ST), F32)
    return _pcall(
        body, name="ssm_bwd", grid=(N_JBLK, nt),
        out_shape=(jax.ShapeDtypeStruct((n_seq, seq, SSM_W), F32), acc_shape, acc_shape, acc_shape, acc_shape,
                   jax.ShapeDtypeStruct((1, N_JBLK * JB_ST), F32), jax.ShapeDtypeStruct((1, N_JBLK * JB_ST), F32),
                   jax.ShapeDtypeStruct((1, SSM_W), F32),
                   jax.ShapeDtypeStruct((N_DEV,) + g_out.shape[1:], F32),
                   jax.ShapeDtypeStruct((N_DEV,) + g_glu.shape[1:], F32)),
        in_specs=[pl.BlockSpec((n_seq, tt, JB_CH), tok), pl.BlockSpec((n_seq, tt, JB_CH), tok),
                  pl.BlockSpec((n_seq, tt, JB_ST), tok), pl.BlockSpec((n_seq, tt, JB_ST), tok),
                  pl.BlockSpec((n_seq, SUBLANES, JB_ST), halo), pl.BlockSpec((n_seq, SUBLANES, JB_ST), halo),
                  pl.BlockSpec((1, JB_CH, JB_ST), blk3), pl.BlockSpec((1, JB_CH, JB_ST), blk3),
                  pl.BlockSpec((1, JB_ST, JB_CH), blk3), pl.BlockSpec((1, JB_ST, JB_CH), blk3),
                  pl.BlockSpec((1, JB_CH), row1), pl.BlockSpec((1, JB_ST), row1), pl.BlockSpec((1, JB_ST), row1),
                  HBM_SPEC, HBM_SPEC],
        out_specs=(pl.BlockSpec((n_seq, tt, JB_CH), tok),
                   pl.BlockSpec((1, JB_CH, JB_ST), blk3), pl.BlockSpec((1, JB_CH, JB_ST), blk3),
                   pl.BlockSpec((1, JB_CH, JB_ST), blk3), pl.BlockSpec((1, JB_CH, JB_ST), blk3),
                   pl.BlockSpec((1, JB_ST), row1), pl.BlockSpec((1, JB_ST), row1), pl.BlockSpec((1, JB_CH), row1),
                   HBM_SPEC, HBM_SPEC),
        scratch_shapes=[pltpu.VMEM((n_seq, tt, JB_ST), F32), pltpu.VMEM((n_seq, tt, JB_ST), F32),
                        pltpu.VMEM((n_seq, tt, JB_CH), F32), pltpu.VMEM((n_seq, tt, JB_CH), F32),
                        pltpu.VMEM((n_seq, SUBLANES, JB_ST), F32),
                        pltpu.SemaphoreType.DMA((7 * 2,)), pltpu.SemaphoreType.DMA((7 * 2,)),
                        pltpu.SemaphoreType.DMA((2,))],
        compiler_params=_params(2),
    )(dy, u, s_re, s_im, s_re, s_im, bb_re, bb_im, c_re_t, c_imn_t, d_row, ab_re, ab_im, g_out, g_glu)


def _mix(x2, tgt2, y, proj, gf, b_glu, conv8, w_glu_f, w_out_f, seq):
    n = x2.shape[0]
    tm = TOK_TILE
    tiles_per_seq = seq // tm
    rows8 = tm // SUBLANES

    def body(x_ref, t_ref, y_ref, zs_ref, h_ref, bc_ref, cc_ref, zc_ref, hp_ref, ccp_ref,
             gf_ref, bg_ref, cw_ref, wg_ref, wo_ref,
             dh2_ref, dy_ref, dzs_ref, dbc_ref, dzc_ref, dyc_ref,
             dwo_ref, dwg_ref, loss_ref, dgf_ref, dbg_ref, dcw_ref):
        i = pl.program_id(0)

        @pl.when(i == 0)
        def _():
            for r in (dwo_ref, dwg_ref, loss_ref, dgf_ref, dbg_ref, dcw_ref):
                r[...] = jnp.zeros_like(r)

        yv = y_ref[...]
        y1, dgelu = _gelu_and_grad(yv)
        y1b = y1.astype(BF16)
        gate = _sigmoid(_dot(y1b, wg_ref[...]) + bg_ref[...])
        y2 = y1 * gate
        szs, dszs = _silu_and_grad(zs_ref[...])
        yssm = y2 * szs
        hv = h_ref[...]
        ccv = cc_ref[...]
        bcv = bc_ref[...]
        v = ccv * hv
        first = (i % tiles_per_seq) == 0
        vhalo = jnp.where(first, 0.0, ccp_ref[...] * hp_ref[...])
        v1 = _shift_down(v, vhalo, 1)
        v2 = _shift_down(v, vhalo, 2)
        w0 = cw_ref[0:1, :]
        w1 = cw_ref[1:2, :]
        w2 = cw_ref[2:3, :]
        yc = w0 * v2 + w1 * v1 + w2 * v
        szc, dszc = _silu_and_grad(zc_ref[...])
        yconv = (bcv * yc) * szc
        ysb = yssm.astype(BF16)
        ycb = yconv.astype(BF16)
        h2 = x_ref[...] + _dot(ysb, wo_ref[0:SSM_W, :]) + _dot(ycb, wo_ref[SSM_W:, :])
        r2 = lax.rsqrt(jnp.mean(h2 * h2, axis=-1, keepdims=True) + EPS)
        hn = h2 * r2
        gfv = gf_ref[...]
        err = hn * gfv - t_ref[...]
        loss_ref[...] += 0.5 * jnp.sum(jnp.mean(err * err, axis=-1, keepdims=True))
        dout = err * (1.0 / D_MODEL)
        dgf_ref[...] += jnp.sum(dout * hn, axis=0, keepdims=True)
        dn = dout * gfv
        dh2 = r2 * (dn - hn * jnp.mean(dn * hn, axis=-1, keepdims=True))
        dh2_ref[...] = dh2
        dh2b = dh2.astype(BF16)
        dwo_ref[0:SSM_W, :] += _dot_tn(ysb, dh2b)
        dwo_ref[SSM_W:, :] += _dot_tn(ycb, dh2b)
        dyssm = _dot_nt(dh2b, wo_ref[0:SSM_W, :])
        dyconv = _dot_nt(dh2b, wo_ref[SSM_W:, :])
        dy2 = dyssm * szs
        dzs_ref[...] = dyssm * y2 * dszs
        dgp = dy2 * y1 * (gate * (1.0 - gate))
        dgpb = dgp.astype(BF16)
        dy1 = dy2 * gate + _dot_nt(dgpb, wg_ref[...])
        dwg_ref[...] += _dot_tn(y1b, dgpb)
        dbg_ref[...] += jnp.sum(dgp, axis=0, keepdims=True)
        dy_ref[...] = dy1 * dgelu
        dbc_ref[...] = dyconv * yc * szc
        dyc = dyconv * bcv * szc
        dyc_ref[...] = dyc
        dzc_ref[...] = dyconv * bcv * yc * dszc
        dcw_ref[0:1, :] += jnp.sum(dyc * v2, axis=0, keepdims=True)
        dcw_ref[1:2, :] += jnp.sum(dyc * v1, axis=0, keepdims=True)
        dcw_ref[2:3, :] += jnp.sum(dyc * v, axis=0, keepdims=True)

    tile_d = pl.BlockSpec((tm, D_MODEL), lambda i: (i, 0))
    tile_s = pl.BlockSpec((tm, SSM_W), lambda i: (i, 0))
    seg_of = lambda c: pl.BlockSpec((tm, SSM_W), lambda i: (i, c))
    halo_of = lambda c: pl.BlockSpec((SUBLANES, SSM_W), lambda i: (jnp.maximum(i * rows8 - 1, 0), c))
    const = lambda shape: pl.BlockSpec(shape, lambda i: (0,) * len(shape))
    seg = jax.ShapeDtypeStruct((n, SSM_W), F32)
    return _pcall(
        body, name="mix", grid=(n // tm,),
        out_shape=(jax.ShapeDtypeStruct((n, D_MODEL), F32), seg, seg, seg, seg, seg,
                   jax.ShapeDtypeStruct((D_MODEL, D_MODEL), F32), jax.ShapeDtypeStruct((SSM_W, SSM_W), F32),
                   jax.ShapeDtypeStruct((SUBLANES, LANES), F32), jax.ShapeDtypeStruct((1, D_MODEL), F32),
                   jax.ShapeDtypeStruct((1, SSM_W), F32), jax.ShapeDtypeStruct((SUBLANES, CONV_W), F32)),
        in_specs=[tile_d, tile_d, tile_s, seg_of(SEG_ZS), seg_of(SEG_H), seg_of(SEG_BC), seg_of(SEG_CC), seg_of(SEG_ZC),
                  halo_of(SEG_H), halo_of(SEG_CC),
                  const((1, D_MODEL)), const((1, SSM_W)), const((SUBLANES, CONV_W)),
                  const((SSM_W, SSM_W)), const((D_MODEL, D_MODEL))],
        out_specs=(tile_d, tile_s, tile_s, tile_s, tile_s, tile_s,
                   const((D_MODEL, D_MODEL)), const((SSM_W, SSM_W)), const((SUBLANES, LANES)),
                   const((1, D_MODEL)), const((1, SSM_W)), const((SUBLANES, CONV_W))),
        compiler_params=_params(1),
    )(x2, tgt2, y, proj, proj, proj, proj, proj, proj, proj, gf, b_glu, conv8, w_glu_f, w_out_f)


def _in_bwd(x2, dh2, du, dzs, dyc, proj, dbc, dzc, g1, conv8, w8, small, seq):
    n = x2.shape[0]
    tm = TOK_TILE
    n_tiles = n // tm
    tiles_per_seq = seq // tm
    rows8 = tm // SUBLANES
    n_blk8 = n // SUBLANES

    def body(x_ref, dh2_ref, du_ref, dzs_ref, dyc_ref, dycn_ref, h_ref, cc_ref, dbc_ref, dzc_ref,
             g_ref, cw_ref, w_ref, sm_ref, gx_ref, dp_ref, dg_ref, rsm_ref, send_sems, recv_sems, loc_sem):
        i = pl.program_id(0)
        gather = _TwoLevelGather([sm_ref], [lambda dev: rsm_ref.at[dev]], send_sems, recv_sems, loc_sem)

        @pl.when(i == 0)
        def _():
            dg_ref[...] = jnp.zeros_like(dg_ref)
            gather.start()

        @pl.when(i == n_tiles // 2)
        def _():
            gather.forward()

        dyc = dyc_ref[...]
        last = (i % tiles_per_seq) == tiles_per_seq - 1
        nhalo = jnp.where(last, 0.0, dycn_ref[...])
        dv = (cw_ref[2:3, :] * dyc + cw_ref[1:2, :] * _shift_up(dyc, nhalo, 1)
              + cw_ref[0:1, :] * _shift_up(dyc, nhalo, 2))
        parts = (du_ref[...], dzs_ref[...], dv * cc_ref[...], dbc_ref[...], dv * h_ref[...], dzc_ref[...])
        for k, p in enumerate(parts):
            dp_ref[:, k * SSM_W:(k + 1) * SSM_W] = p.astype(BF16)
        dxn = jnp.zeros((tm, D_MODEL), F32)
        for s in range(N_DEV):
            dxn = dxn + _dot_nt(dp_ref[:, s * COLS_PER_DEV:(s + 1) * COLS_PER_DEV], w_ref[s])
        x = x_ref[...]
        r = lax.rsqrt(jnp.mean(x * x, axis=-1, keepdims=True) + EPS)
        xh = x * r
        dg_ref[...] += jnp.sum(dxn * xh, axis=0, keepdims=True)
        dn = dxn * g_ref[...]
        gx_ref[...] = dh2_ref[...] + r * (dn - xh * jnp.mean(dn * xh, axis=-1, keepdims=True))

        @pl.when(i == n_tiles - 1)
        def _():
            gather.finish()

    tile_d = pl.BlockSpec((tm, D_MODEL), lambda i: (i, 0))
    tile_s = pl.BlockSpec((tm, SSM_W), lambda i: (i, 0))
    seg_of = lambda c: pl.BlockSpec((tm, SSM_W), lambda i: (i, c))
    nhalo = pl.BlockSpec((SUBLANES, SSM_W), lambda i: (jnp.minimum((i + 1) * rows8, n_blk8 - 1), 0))
    const = lambda shape: pl.BlockSpec(shape, lambda i: (0,) * len(shape))
    return _pcall(
        body, name="in_bwd", grid=(n_tiles,),
        out_shape=(jax.ShapeDtypeStruct((n, D_MODEL), F32), jax.ShapeDtypeStruct((n, IN_COLS), BF16),
                   jax.ShapeDtypeStruct((SUBLANES, D_MODEL), F32),
                   jax.ShapeDtypeStruct((N_DEV,) + small.shape, F32)),
        in_specs=[tile_d, tile_d, tile_s, tile_s, tile_s, nhalo, seg_of(SEG_H), seg_of(SEG_CC), tile_s, tile_s,
                  const((1, D_MODEL)), const((SUBLANES, CONV_W)), const((N_DEV, D_MODEL, COLS_PER_DEV)), HBM_SPEC],
        out_specs=(tile_d, pl.BlockSpec((tm, IN_COLS), lambda i: (i, 0)), const((SUBLANES, D_MODEL)), HBM_SPEC),
        scratch_shapes=[pltpu.SemaphoreType.DMA((7,)), pltpu.SemaphoreType.DMA((7,)), pltpu.SemaphoreType.DMA((1,))],
        compiler_params=_params(1),
    )(x2, dh2, du, dzs, dyc, dyc, proj, proj, dbc, dzc, g1, conv8, w8, small)


def _dw_in_exchange(order, xn, dproj, dg8):
    n = xn.shape[0]
    tk = 512
    nk = n // tk
    piece = (D_MODEL, COLS_PER_DEV)

    def body(order_ref, xn_ref, dp_ref, dg_ref, own_ref, rchip_ref, rdg_ref,
             acc, stage, sbuf, give_send, give_recv, keep_send, keep_recv, dg_send, dg_recv, dg_loc):
        del order_ref
        s = pl.program_id(0)
        x, y, c = _mesh_pos()
        sib = (x, y, 1 - c)
        chips = [(1 - x, y), (x, 1 - y), (1 - x, 1 - y)]

        def dg_copies():
            return _direct_copies(lambda pid: [dg_ref], [rdg_ref], dg_send, dg_recv, dg_loc)

        def give(i):
            return pltpu.make_async_remote_copy(src_ref=acc.at[0], dst_ref=stage.at[i], send_sem=give_send.at[i],
                                                recv_sem=give_recv.at[i], device_id=sib, device_id_type=MESH)

        def keep(i):
            return pltpu.make_async_remote_copy(src_ref=sbuf.at[i], dst_ref=rchip_ref.at[i], send_sem=keep_send.at[i],
                                                recv_sem=keep_recv.at[i], device_id=(*chips[i], c), device_id_type=MESH)

        @pl.when(s == 0)
        def _():
            mine, sends = dg_copies()
            for cp in mine + sends:
                cp.start()

        for k in (2, 4, 6):
            @pl.when(s == k)
            def _(k=k):
                give(k // 2 - 1).wait_send()

        slot = s % 2
        acc[slot] = _dot_tn(xn_ref[pl.ds(0, tk), :], dp_ref[pl.ds(0, tk), :])

        def kstep(kk, carry):
            off = pl.multiple_of(kk * tk, tk)
            acc[slot] += _dot_tn(xn_ref[pl.ds(off, tk), :], dp_ref[pl.ds(off, tk), :])
            return carry

        lax.fori_loop(1, nk, kstep, 0)

        for k in range(N_DEV):
            @pl.when(s == k)
            def _(k=k):
                i = k // 2
                if k % 2 == 0:
                    give(i).start()
                else:
                    give(i).wait_recv()
                    total = acc[1] + stage[i]
                    if i < 3:
                        sbuf[i] = total.astype(BF16)
                        keep(i).start()
                    else:
                        own_ref[...] = total

        @pl.when(s == N_DEV - 1)
        def _():
            give(3).wait_send()
            for i in range(3):
                keep(i).wait()
            mine, sends = dg_copies()
            for cp in sends + mine:
                cp.wait()

    grid_spec = pltpu.PrefetchScalarGridSpec(
        num_scalar_prefetch=1, grid=(N_DEV,),
        in_specs=[pl.BlockSpec(memory_space=pltpu.VMEM),
                  pl.BlockSpec((n, COLS_PER_DEV), lambda s, order: (0, order[s])),
                  HBM_SPEC],
        out_specs=(pl.BlockSpec(piece, lambda s, order: (0, 0)), HBM_SPEC, HBM_SPEC),
        scratch_shapes=[pltpu.VMEM((2,) + piece, F32), pltpu.VMEM((4,) + piece, F32), pltpu.VMEM((3,) + piece, BF16),
                        pltpu.SemaphoreType.DMA((4,)), pltpu.SemaphoreType.DMA((4,)),
                        pltpu.SemaphoreType.DMA((3,)), pltpu.SemaphoreType.DMA((3,)),
                        pltpu.SemaphoreType.DMA((7,)), pltpu.SemaphoreType.DMA((7,)), pltpu.SemaphoreType.DMA((1,))])
    return _pcall(
        body, name="dw_in_exchange", grid_spec=grid_spec,
        out_shape=(jax.ShapeDtypeStruct(piece, F32), jax.ShapeDtypeStruct((3,) + piece, BF16),
                   jax.ShapeDtypeStruct((N_DEV,) + dg8.shape, F32)),
        compiler_params=_params(1),
    )(order, xn, dproj, dg8)


def _adamw(g, w, m, v):
    m_new = ADAM_B1 * m + (1.0 - ADAM_B1) * g
    v_new = ADAM_B2 * v + (1.0 - ADAM_B2) * (g * g)
    m_hat = m_new / (1.0 - ADAM_B1 ** ADAM_STEP)
    v_hat = v_new / (1.0 - ADAM_B2 ** ADAM_STEP)
    delta = -ADAM_LR * (m_hat / (jnp.sqrt(v_hat) + ADAM_EPS) + ADAM_WD * w)
    return delta, m_new, v_new


def _reduce_adam(recv, w, m, v, name, row_tile):
    rows, cols = w.shape

    def body(r_ref, w_ref, m_ref, v_ref, g_ref, d_ref, nm_ref, nv_ref):
        g = r_ref[0]
        for s in range(1, N_DEV):
            g = g + r_ref[s]
        g_ref[...] = g
        d_ref[...], nm_ref[...], nv_ref[...] = _adamw(g, w_ref[...], m_ref[...], v_ref[...])

    tile = pl.BlockSpec((row_tile, cols), lambda i: (i, 0))
    shp = jax.ShapeDtypeStruct((rows, cols), F32)
    return _pcall(
        body, name=name, grid=(rows // row_tile,),
        out_shape=(shp,) * 4,
        in_specs=[pl.BlockSpec((N_DEV, row_tile, cols), lambda i: (0, i, 0)), tile, tile, tile],
        out_specs=(tile,) * 4,
        compiler_params=_params(1),
    )(recv, w, m, v)


def _reduce_adam_w_in(own, rchip, w, m, v):
    rows, cols = w.shape
    row_tile = 256

    def body(o_ref, r_ref, w_ref, m_ref, v_ref, g_ref, d_ref, nm_ref, nv_ref):
        g = o_ref[...]
        for s in range(3):
            g = g + r_ref[s].astype(F32)
        g_ref[...] = g
        d_ref[...], nm_ref[...], nv_ref[...] = _adamw(g, w_ref[...], m_ref[...], v_ref[...])

    tile = pl.BlockSpec((row_tile, cols), lambda i: (i, 0))
    shp = jax.ShapeDtypeStruct((rows, cols), F32)
    return _pcall(
        body, name="reduce_adam_w_in", grid=(rows // row_tile,),
        out_shape=(shp,) * 4,
        in_specs=[tile, pl.BlockSpec((3, row_tile, cols), lambda i: (0, i, 0)), tile, tile, tile],
        out_specs=(tile,) * 4,
        compiler_params=_params(1),
    )(own, rchip, w, m, v)


def _reduce_adam_stacked(recv, wmv, name):
    _, rows, cols = wmv.shape

    def body(r_ref, p_ref, o_ref):
        g = r_ref[0]
        for s in range(1, N_DEV):
            g = g + r_ref[s]
        o_ref[0] = g
        o_ref[1], o_ref[2], o_ref[3] = _adamw(g, p_ref[0], p_ref[1], p_ref[2])

    return _pcall(body, name=name, out_shape=jax.ShapeDtypeStruct((4, rows, cols), F32),
                  compiler_params=_params(0))(recv, wmv)


_SMALL = (("final_norm_gain", D_MODEL), ("b_glu", SSM_W),
          ("ssm_a_re", N_GROUPS * STATE), ("ssm_a_im", N_GROUPS * STATE), ("ssm_log_dt", N_GROUPS),
          ("ssm_b_re", N_GROUPS * STATE * GROUP), ("ssm_b_im", N_GROUPS * STATE * GROUP),
          ("ssm_c_re", N_GROUPS * STATE * GROUP), ("ssm_c_im", N_GROUPS * STATE * GROUP),
          ("ssm_d", N_GROUPS * GROUP), ("conv_w", 3 * CONV_W))
_PACK_UNIT = SUBLANES * LANES


def _pack_small(dicts):
    cols = []
    for name, size in _SMALL:
        flat = jnp.stack([d[name].reshape(-1) for d in dicts])
        padded = -(-size // _PACK_UNIT) * _PACK_UNIT
        cols.append(jnp.pad(flat, ((0, 0), (0, padded - size))).reshape(len(dicts), -1, LANES))
    return jnp.concatenate(cols, axis=1)


def _unpack_small(packed):
    out, r0 = {}, 0
    for name, size in _SMALL:
        nrows = -(-size // _PACK_UNIT) * SUBLANES
        out[name] = packed[:, r0:r0 + nrows].reshape(packed.shape[0], -1)[:, :size]
        r0 += nrows
    return out


def _block_diag(m4):
    eye = jnp.eye(SUBLANES, dtype=m4.dtype)
    j, g, a, b = m4.shape
    return jnp.einsum("jgab,gk->jgakb", m4, eye).reshape(j, g * a, g * b)


def _block_diag_extract(dense, a, b):
    d5 = dense.reshape(N_JBLK, SUBLANES, a, SUBLANES, b)
    return jnp.stack([d5[:, g, :, g, :] for g in range(SUBLANES)], axis=1)


def kernel(x, norm_gain, w_in, ssm_a_re, ssm_a_im, ssm_log_dt, ssm_b_re, ssm_b_im, ssm_c_re, ssm_c_im, ssm_d, w_glu, b_glu, conv_w, w_out, final_norm_gain, loss_target, m_norm_gain, m_w_in, m_ssm_a_re, m_ssm_a_im, m_ssm_log_dt, m_ssm_b_re, m_ssm_b_im, m_ssm_c_re, m_ssm_c_im, m_ssm_d, m_w_glu, m_b_glu, m_conv_w, m_w_out, m_final_norm_gain, v_norm_gain, v_w_in, v_ssm_a_re, v_ssm_a_im, v_ssm_log_dt, v_ssm_b_re, v_ssm_b_im, v_ssm_c_re, v_ssm_c_im, v_ssm_d, v_w_glu, v_b_glu, v_conv_w, v_w_out, v_final_norm_gain):
    n_seq, seq, _ = x.shape
    n = n_seq * seq
    me = 4 * lax.axis_index("x") + 2 * lax.axis_index("y") + lax.axis_index("c")

    rep = lambda a: jnp.repeat(a[0], GROUP, axis=1)
    a_re_r, a_im_r = rep(ssm_a_re), rep(ssm_a_im)
    log_dt = ssm_log_dt[0].reshape(N_GROUPS, 1)
    b_re2 = ssm_b_re[0].reshape(N_GROUPS, STATE * GROUP)
    b_im2 = ssm_b_im[0].reshape(N_GROUPS, STATE * GROUP)
    ab_re_r, ab_im_r, bb_re2, bb_im2 = _ssm_disc(a_re_r, a_im_r, log_dt, b_re2, b_im2)
    ab_re = ab_re_r[:, ::GROUP].reshape(1, N_GROUPS * STATE)
    ab_im = ab_im_r[:, ::GROUP].reshape(1, N_GROUPS * STATE)

    def bb_mat(bb2):
        t = jnp.transpose(bb2.reshape(N_JBLK, SUBLANES, STATE, GROUP), (0, 1, 3, 2))
        return _block_diag(t).astype(BF16)

    def c_mat(c3, sign):
        t = jnp.transpose(c3.reshape(N_JBLK, SUBLANES, GROUP, STATE), (0, 1, 3, 2))
        return _block_diag(sign * t).astype(BF16)

    bb_re_m, bb_im_m = bb_mat(bb_re2), bb_mat(bb_im2)
    c_re_m, c_imn_m = c_mat(ssm_c_re[0], 1.0), c_mat(ssm_c_im[0], -1.0)
    d_row = ssm_d[0].reshape(1, SSM_W)

    x2 = x.reshape(n, D_MODEL)
    tgt2 = loss_target.reshape(n, D_MODEL)
    mx, my, mc = lax.axis_index("x"), lax.axis_index("y"), lax.axis_index("c")
    arrival = [4 * mx + 2 * my + mc, 4 * mx + 2 * my + (1 - mc)]
    for cx, cy in ((1 - mx, my), (mx, 1 - my), (1 - mx, 1 - my)):
        arrival += [4 * cx + 2 * cy + mc, 4 * cx + 2 * cy + (1 - mc)]
    xn, proj, w8 = _in_proj(jnp.stack(arrival).astype(jnp.int32), x2, norm_gain, w_in[0].astype(BF16))
    u3 = proj.reshape(n_seq, seq, IN_COLS)
    conv_p = jnp.pad(conv_w[0], ((0, SUBLANES - 3), (0, LANES - CONV_COLS_PER_DEV)))
    s_re, s_im, y3, w_out_f, w_glu_f, conv_all = _ssm_fwd(
        u3, bb_re_m, bb_im_m, c_re_m, c_imn_m, d_row, ab_re, ab_im,
        w_out[0].astype(BF16), w_glu[0].astype(BF16), conv_p, n_seq, seq)
    conv8 = jnp.transpose(conv_all[:, :, :CONV_COLS_PER_DEV], (1, 0, 2)).reshape(SUBLANES, CONV_W)
    (dh2, dy, dzs, dbc, dzc, dyc, dw_out, dw_glu, loss_t, dgf, dbg, dcw) = _mix(
        x2, tgt2, y3.reshape(n, SSM_W), proj, final_norm_gain.reshape(1, D_MODEL), b_glu, conv8,
        w_glu_f, w_out_f, seq)

    du3, dc_re_d, dc_im_d, dbb_re_d, dbb_im_d, dab_re, dab_im, dd, r_out, r_glu = _ssm_bwd(
        dy.reshape(n_seq, seq, SSM_W), u3, s_re, s_im, bb_re_m, bb_im_m, c_re_m, c_imn_m, d_row, ab_re, ab_im,
        dw_out.reshape(N_DEV, OUT_ROWS_PER_DEV, D_MODEL), dw_glu.reshape(N_DEV, GLU_ROWS_PER_DEV, SSM_W), n_seq, seq)
    du = du3.reshape(n, SSM_W)
    g_c_re = _block_diag_extract(dc_re_d, GROUP, STATE).reshape(N_GROUPS, GROUP, STATE)
    g_c_im = -_block_diag_extract(dc_im_d, GROUP, STATE).reshape(N_GROUPS, GROUP, STATE)

    def bb_grad(dense):
        t = _block_diag_extract(dense, GROUP, STATE)
        return jnp.transpose(t, (0, 1, 3, 2)).reshape(N_GROUPS, STATE * GROUP)

    def ab_grad(row):
        z = jnp.zeros((N_GROUPS, STATE, GROUP), F32)
        return z.at[:, :, 0].set(row.reshape(N_GROUPS, STATE)).reshape(N_GROUPS, STATE * GROUP)

    g_are_r, g_aim_r, g_ldt, g_bre2, g_bim2 = _ssm_disc_bwd(
        a_re_r, a_im_r, log_dt, b_re2, b_im2, ab_grad(dab_re), ab_grad(dab_im),
        bb_grad(dbb_re_d), bb_grad(dbb_im_d))
    small_grads = {"final_norm_gain": dgf, "b_glu": dbg,
                   "ssm_a_re": g_are_r[:, ::GROUP], "ssm_a_im": g_aim_r[:, ::GROUP], "ssm_log_dt": g_ldt,
                   "ssm_b_re": g_bre2, "ssm_b_im": g_bim2, "ssm_c_re": g_c_re, "ssm_c_im": g_c_im,
                   "ssm_d": dd, "conv_w": dcw[0:3]}
    grad_x2, dproj, dg8, r_small = _in_bwd(x2, dh2, du, dzs, dyc, proj, dbc, dzc, norm_gain, conv8, w8,
                                           _pack_small([small_grads])[0], seq)

    order = []
    for cx, cy in ((1 - mx, my), (mx, 1 - my), (1 - mx, 1 - my), (mx, my)):
        order += [4 * cx + 2 * cy + (1 - mc), 4 * cx + 2 * cy + mc]
    own_in, rchip_in, r_dg = _dw_in_exchange(jnp.stack(order).astype(jnp.int32), xn, dproj, dg8)

    def conv_full(shard):
        return lax.dynamic_update_slice(jnp.zeros((3, CONV_W), F32), shard[0], (0, me * CONV_COLS_PER_DEV))

    triples = dict(final_norm_gain=(final_norm_gain, m_final_norm_gain, v_final_norm_gain),
                   b_glu=(b_glu, m_b_glu, v_b_glu), ssm_a_re=(ssm_a_re, m_ssm_a_re, v_ssm_a_re),
                   ssm_a_im=(ssm_a_im, m_ssm_a_im, v_ssm_a_im), ssm_log_dt=(ssm_log_dt, m_ssm_log_dt, v_ssm_log_dt),
                   ssm_b_re=(ssm_b_re, m_ssm_b_re, v_ssm_b_re), ssm_b_im=(ssm_b_im, m_ssm_b_im, v_ssm_b_im),
                   ssm_c_re=(ssm_c_re, m_ssm_c_re, v_ssm_c_re), ssm_c_im=(ssm_c_im, m_ssm_c_im, v_ssm_c_im),
                   ssm_d=(ssm_d, m_ssm_d, v_ssm_d),
                   conv_w=(conv_full(conv_w), conv_full(m_conv_w), conv_full(v_conv_w)))
    wmv_small = _pack_small([{k: t[i] for k, t in triples.items()} for i in range(3)])
    wmv_gain = jnp.pad(jnp.stack([norm_gain, m_norm_gain, v_norm_gain]), ((0, 0), (0, SUBLANES - 1), (0, 0)))

    res_in = _reduce_adam_w_in(own_in, rchip_in, w_in[0], m_w_in[0], v_w_in[0])
    res_out = _reduce_adam(r_out, w_out[0], m_w_out[0], v_w_out[0], "reduce_adam_w_out", OUT_ROWS_PER_DEV)
    res_glu = _reduce_adam(r_glu, w_glu[0], m_w_glu[0], v_w_glu[0], "reduce_adam_w_glu", GLU_ROWS_PER_DEV)
    small = _unpack_small(_reduce_adam_stacked(r_small, wmv_small, "reduce_adam_small"))
    res_gain = _reduce_adam_stacked(r_dg, wmv_gain, "reduce_adam_gain")

    loss = lax.psum(loss_t[0, 0], ("x", "y", "c"))

    shapes = dict(ssm_a_re=(1, N_GROUPS, STATE), ssm_a_im=(1, N_GROUPS, STATE),
                  ssm_log_dt=(1, N_GROUPS), ssm_b_re=(1, N_GROUPS, STATE, GROUP), ssm_b_im=(1, N_GROUPS, STATE, GROUP),
                  ssm_c_re=(1, N_GROUPS, GROUP, STATE), ssm_c_im=(1, N_GROUPS, GROUP, STATE),
                  ssm_d=(1, N_GROUPS, GROUP), b_glu=(1, SSM_W), final_norm_gain=(D_MODEL,))
    big = dict(w_in=res_in, w_glu=res_glu, w_out=res_out)
    small4 = {name: small[name].reshape((4,) + shp) for name, shp in shapes.items()}
    conv4 = lax.dynamic_slice(small["conv_w"].reshape(4, 1, 3, CONV_W), (0, 0, 0, me * CONV_COLS_PER_DEV),
                              (4, 1, 3, CONV_COLS_PER_DEV))

    def leaf(kind, name):
        if name in big:
            return big[name][kind][None]
        if name == "norm_gain":
            return res_gain[kind, 0:1, :]
        if name == "conv_w":
            return conv4[kind]
        return small4[name][kind]

    order = ["norm_gain", "w_in", "ssm_a_re", "ssm_a_im", "ssm_log_dt", "ssm_b_re", "ssm_b_im", "ssm_c_re",
             "ssm_c_im", "ssm_d", "w_glu", "b_glu", "conv_w", "w_out", "final_norm_gain"]
    outs = [loss, grad_x2.reshape(x.shape)]
    for kind in range(4):
        outs += [leaf(kind, name) for name in order]
    return tuple(outs)
```

```python
import functools
import math

import jax
import jax.numpy as jnp
from jax import lax
from jax.experimental import pallas as pl
from jax.experimental.pallas import tpu as pltpu

F32 = jnp.float32
BF16 = jnp.bfloat16

N_DEV = 8
D_MODEL = 1024
SSM_W = 512
CONV_W = 512
N_GROUPS = 32
GROUP = 16
STATE = 64
IN_COLS = 3072
SEG_U, SEG_ZS, SEG_H, SEG_BC, SEG_CC, SEG_ZC = range(6)
COLS_PER_DEV = IN_COLS // N_DEV
OUT_ROWS_PER_DEV = D_MODEL // N_DEV
GLU_ROWS_PER_DEV = SSM_W // N_DEV
CONV_COLS_PER_DEV = CONV_W // N_DEV
EPS = 1e-6

N_JBLK = 4
JB_CH = SSM_W // N_JBLK
JB_ST = N_GROUPS * STATE // N_JBLK

ADAM_LR = 0.001
ADAM_B1 = 0.9
ADAM_B2 = 0.999
ADAM_EPS = 1e-08
ADAM_WD = 0.01
ADAM_STEP = 10

SUBLANES = 8
LANES = 128
VMEM_LIMIT = 48 * 1024 * 1024
TOK_TILE = 256
IN_TILE = 1024
SCAN_TILE = 256

MESH = pl.DeviceIdType.MESH
HBM_SPEC = pl.BlockSpec(memory_space=pltpu.HBM)


def _pcall(body, **kw):
    return pl.pallas_call(body, **kw)


def _params(n_grid):
    return pltpu.CompilerParams(dimension_semantics=("arbitrary",) * n_grid,
                                vmem_limit_bytes=VMEM_LIMIT)


def _dot(a, b):
    return jnp.dot(a, b, preferred_element_type=F32)


def _dot_nt(a, b):
    return lax.dot_general(a, b, (((1,), (1,)), ((), ())), preferred_element_type=F32)


def _dot_tn(a, b):
    return lax.dot_general(a, b, (((0,), (0,)), ((), ())), preferred_element_type=F32)


def _sigmoid(z):
    return 1.0 / (1.0 + jnp.exp(-z))


_GELU_C = math.sqrt(2.0 / math.pi)


def _gelu_and_grad(y):
    inner = _GELU_C * (y + 0.044715 * (y * y * y))
    t = jnp.tanh(inner)
    g = 0.5 * y * (1.0 + t)
    dg = 0.5 * (1.0 + t) + 0.5 * y * (1.0 - t * t) * (_GELU_C * (1.0 + 3.0 * 0.044715 * (y * y)))
    return g, dg


def _silu_and_grad(z):
    s = _sigmoid(z)
    return z * s, s * (1.0 + z * (1.0 - s))


def _shift_down(v, halo, k):
    rolled = pltpu.roll(v, k, 0)
    row = lax.broadcasted_iota(jnp.int32, v.shape, 0)
    for r in range(k):
        rolled = jnp.where(row == r, halo[SUBLANES - k + r:SUBLANES - k + r + 1, :], rolled)
    return rolled


def _shift_up(v, halo, k):
    n = v.shape[0]
    rolled = pltpu.roll(v, n - k, 0)
    row = lax.broadcasted_iota(jnp.int32, v.shape, 0)
    for r in range(k):
        rolled = jnp.where(row == n - k + r, halo[r:r + 1, :], rolled)
    return rolled


def _mesh_pos():
    return lax.axis_index("x"), lax.axis_index("y"), lax.axis_index("c")


def _direct_copies(srcs_for, out_refs, send_sems, recv_sems, loc_sems):
    x, y, c = _mesh_pos()
    me_id = 4 * x + 2 * y + c
    n_arr = len(out_refs)
    dsts = [r.at[me_id] for r in out_refs]
    own = srcs_for(me_id)
    mine = [pltpu.make_async_copy(own[a], dsts[a], loc_sems.at[a]) for a in range(n_arr)]
    sends = []
    for k in range(1, N_DEV):
        px, py, pc = x ^ ((k >> 2) & 1), y ^ ((k >> 1) & 1), c ^ (k & 1)
        src = srcs_for(4 * px + 2 * py + pc)
        for a in range(n_arr):
            sends.append(pltpu.make_async_remote_copy(
                src_ref=src[a], dst_ref=dsts[a],
                send_sem=send_sems.at[(k - 1) * n_arr + a], recv_sem=recv_sems.at[(k - 1) * n_arr + a],
                device_id=(px, py, pc), device_id_type=MESH))
    return mine, sends


class _TwoLevelGather:
    def __init__(self, srcs, slots, send_sems, recv_sems, loc_sems):
        self.srcs, self.slots, self.n_arr = srcs, slots, len(srcs)
        self.send_sems, self.recv_sems, self.loc_sems = send_sems, recv_sems, loc_sems
        x, y, c = _mesh_pos()
        self.c = c
        self.me, self.sib = (x, y, c), (x, y, 1 - c)
        self.chips = [(1 - x, y), (x, 1 - y), (1 - x, 1 - y)]

    def _copies(self, k, block, to, from_src=False):
        dev = 4 * block[0] + 2 * block[1] + block[2]
        return [pltpu.make_async_remote_copy(
            src_ref=self.srcs[a] if from_src else self.slots[a](dev), dst_ref=self.slots[a](dev),
            send_sem=self.send_sems.at[k * self.n_arr + a], recv_sem=self.recv_sems.at[k * self.n_arr + a],
            device_id=to, device_id_type=MESH) for a in range(self.n_arr)]

    def _local(self):
        dev = 4 * self.me[0] + 2 * self.me[1] + self.me[2]
        return [pltpu.make_async_copy(self.srcs[a], self.slots[a](dev), self.loc_sems.at[a])
                for a in range(self.n_arr)]

    def start(self, chips=(0, 1, 2)):
        for cp in self._local() + self._copies(0, self.me, self.sib, True):
            cp.start()
        self.start_to(chips)

    def start_to(self, chips):
        for j in chips:
            for cp in self._copies(1 + j, self.me, (*self.chips[j], self.c), True):
                cp.start()

    def wait_own(self):
        for cp in self._local():
            cp.wait()

    def wait_sibling(self):
        for cp in self._copies(0, self.sib, self.me):
            cp.wait_recv()

    def wait_and_pass_on(self, j):
        chip = self.chips[j]
        for cp in self._copies(1 + j, (*chip, self.c), self.me):
            cp.wait_recv()
        for cp in self._copies(4 + j, (*chip, self.c), self.sib):
            cp.start()

    def wait_passed_on(self, j):
        for cp in self._copies(4 + j, (*self.chips[j], 1 - self.c), self.me):
            cp.wait_recv()

    def wait_sends(self):
        for cp in self._copies(0, self.me, self.sib, True):
            cp.wait_send()
        for j, chip in enumerate(self.chips):
            for cp in self._copies(1 + j, self.me, (*chip, self.c), True) + self._copies(4 + j, (*chip, self.c), self.sib):
                cp.wait_send()

    def forward(self):
        for j in range(3):
            self.wait_and_pass_on(j)

    def finish(self):
        self.wait_sibling()
        for j in range(3):
            self.wait_passed_on(j)
        self.wait_sends()
        self.wait_own()


def _disc(a_re, a_im, log_dt, b_re, b_im):
    dt = jnp.exp(log_dt)
    mag = jnp.exp(a_re * dt)
    ab_re = mag * jnp.cos(a_im * dt)
    ab_im = mag * jnp.sin(a_im * dt)
    den = a_re * a_re + a_im * a_im
    p_re = ab_re - 1.0
    p_im = ab_im
    q_re = (p_re * a_re + p_im * a_im) / den
    q_im = (p_im * a_re - p_re * a_im) / den
    bb_re = q_re * b_re - q_im * b_im
    bb_im = q_re * b_im + q_im * b_re
    return ab_re, ab_im, bb_re, bb_im


def _ssm_disc(a_re_r, a_im_r, log_dt, b_re, b_im):
    def body(are, aim, ldt, bre, bim, o_abre, o_abim, o_bbre, o_bbim):
        outs = _disc(are[...], aim[...], ldt[...], bre[...], bim[...])
        for o, v in zip((o_abre, o_abim, o_bbre, o_bbim), outs):
            o[...] = v

    shp = jax.ShapeDtypeStruct(a_re_r.shape, F32)
    return _pcall(body, name="ssm_disc", out_shape=(shp,) * 4)(a_re_r, a_im_r, log_dt, b_re, b_im)


def _ssm_disc_bwd(a_re_r, a_im_r, log_dt, b_re, b_im, g_abre, g_abim, g_bbre, g_bbim):
    width = a_re_r.shape[1]

    def body(are, aim, ldt, bre, bim, gabre, gabim, gbbre, gbbim, o_are, o_aim, o_ldt, o_bre, o_bim):
        _, vjp = jax.vjp(_disc, are[...], aim[...], ldt[...], bre[...], bim[...])
        d_are, d_aim, d_ldt, d_bre, d_bim = vjp((gabre[...], gabim[...], gbbre[...], gbbim[...]))

        def group_sum(v):
            for k in (1, 2, 4, 8):
                v = v + pltpu.roll(v, width - k, 1)
            return v

        o_are[...] = group_sum(d_are)
        o_aim[...] = group_sum(d_aim)
        o_ldt[...] = d_ldt
        o_bre[...] = d_bre
        o_bim[...] = d_bim

    shp = jax.ShapeDtypeStruct(a_re_r.shape, F32)
    return _pcall(body, name="ssm_disc_bwd",
                  out_shape=(shp, shp, jax.ShapeDtypeStruct(log_dt.shape, F32), shp, shp),
                  )(a_re_r, a_im_r, log_dt, b_re, b_im, g_abre, g_abim, g_bbre, g_bbim)


def _in_proj(order, x2, g1, w_in_b):
    n = x2.shape[0]
    tm = min(IN_TILE, n)
    n_tiles = n // tm

    def body(order_ref, x_ref, g_ref, w_ref, xn_ref, proj_ref, wall_ref,
             xn_scr, wbuf, send_sems, recv_sems, loc_sems, out_sems):
        k = pl.program_id(0)
        i = pl.program_id(1)
        gather = _TwoLevelGather([w_ref], [lambda dev: wbuf.at[dev]], send_sems, recv_sems, loc_sems)

        @pl.when((k == 0) & (i == 0))
        def _():
            gather.start(chips=(0, 1))

        def first_neighbour_landed():
            gather.wait_and_pass_on(0)
            gather.start_to((2,))

        arrivals = [gather.wait_own, gather.wait_sibling, first_neighbour_landed,
                    functools.partial(gather.wait_passed_on, 0)]
        for j in (1, 2):
            arrivals += [functools.partial(gather.wait_and_pass_on, j), functools.partial(gather.wait_passed_on, j)]
        for kk, arrived in enumerate(arrivals):
            @pl.when((k == kk) & (i == 0))
            def _(arrived=arrived):
                arrived()

        rows = pl.ds(pl.multiple_of(i * tm, tm), tm)

        @pl.when(k == 0)
        def _():
            x = x_ref[...]
            r = lax.rsqrt(jnp.mean(x * x, axis=-1, keepdims=True) + EPS)
            xn = ((x * r) * g_ref[...]).astype(BF16)
            xn_scr[rows, :] = xn
            xn_ref[...] = xn

        proj_ref[...] = _dot(xn_scr[rows, :], wbuf[order_ref[k]])

        @pl.when((k == N_DEV - 1) & (i == n_tiles - 1))
        def _():
            gather.wait_sends()
            outs = [pltpu.make_async_copy(wbuf.at[s], wall_ref.at[:, s * COLS_PER_DEV:(s + 1) * COLS_PER_DEV],
                                          out_sems.at[s]) for s in range(N_DEV)]
            for cp in outs:
                cp.start()
            for cp in outs:
                cp.wait()

    tile_once = lambda k, i, order: (jnp.where(k == 0, i, n_tiles - 1), 0)
    grid_spec = pltpu.PrefetchScalarGridSpec(
        num_scalar_prefetch=1, grid=(N_DEV, n_tiles),
        in_specs=[pl.BlockSpec((tm, D_MODEL), tile_once),
                  pl.BlockSpec((1, D_MODEL), lambda k, i, order: (0, 0)),
                  HBM_SPEC],
        out_specs=(pl.BlockSpec((tm, D_MODEL), tile_once),
                   pl.BlockSpec((tm, COLS_PER_DEV), lambda k, i, order: (i, order[k])),
                   HBM_SPEC),
        scratch_shapes=[pltpu.VMEM((n, D_MODEL), BF16), pltpu.VMEM((N_DEV, D_MODEL, COLS_PER_DEV), BF16),
                        pltpu.SemaphoreType.DMA((7,)), pltpu.SemaphoreType.DMA((7,)), pltpu.SemaphoreType.DMA((1,)),
                        pltpu.SemaphoreType.DMA((N_DEV,))])
    return _pcall(
        body, name="in_proj", grid_spec=grid_spec,
        out_shape=(jax.ShapeDtypeStruct((n, D_MODEL), BF16), jax.ShapeDtypeStruct((n, IN_COLS), F32),
                   jax.ShapeDtypeStruct((D_MODEL, IN_COLS), BF16)),
        compiler_params=_params(2),
    )(order, x2, g1, w_in_b)


def _cmul(p, q):
    return p[0] * q[0] - p[1] * q[1], p[0] * q[1] + p[1] * q[0]


def _scan_tables(ar, ai, width, reverse):
    pows = [(ar, ai)]
    for _ in range(SUBLANES - 1):
        pows.append(_cmul(pows[-1], (ar, ai)))
    row = lax.broadcasted_iota(jnp.int32, (SUBLANES, width), 0)

    def bc(v):
        return jnp.broadcast_to(v, (SUBLANES, width))

    levels = []
    for k in (1, 2, 4):
        keep = (row <= SUBLANES - 1 - k) if reverse else (row >= k)
        levels.append((jnp.where(keep, bc(pows[k - 1][0]), 0.0), jnp.where(keep, bc(pows[k - 1][1]), 0.0)))
    cre = jnp.zeros((SUBLANES, width), F32)
    cim = jnp.zeros((SUBLANES, width), F32)
    for r in range(SUBLANES):
        e = (SUBLANES - r) if reverse else (r + 1)
        cre = jnp.where(row == r, bc(pows[e - 1][0]), cre)
        cim = jnp.where(row == r, bc(pows[e - 1][1]), cim)
    return levels, (cre, cim)


def _load_chunked(src_ref, b, dst_ref, n_rows):
    n_blk = n_rows // SUBLANES
    for i in range(n_blk):
        dst_ref[b, i * SUBLANES:(i + 1) * SUBLANES, :] = src_ref[b, pl.ds(i, SUBLANES, stride=n_blk), :]


def _store_chunked(val, dst_ref, b, n_rows):
    n_blk = n_rows // SUBLANES
    for i in range(n_blk):
        dst_ref[b, pl.ds(i, SUBLANES, stride=n_blk), :] = val[i * SUBLANES:(i + 1) * SUBLANES, :]


def _chunk_scan(re_ref, im_ref, b, car_ref, ar, ai, n_rows, reverse, on_block=None):
    width = re_ref.shape[2]
    n_blk = n_rows // SUBLANES
    shape = (SUBLANES, width)
    abr = jnp.broadcast_to(ar, shape)
    abi = jnp.broadcast_to(ai, shape)
    order = list(range(n_blk - 1, -1, -1)) if reverse else list(range(n_blk))

    def blk(ref, i):
        return ref[b, i * SUBLANES:(i + 1) * SUBLANES, :]

    def step(sr, si, i):
        return abr * sr - abi * si + blk(re_ref, i), abr * si + abi * sr + blk(im_ref, i)

    fr, fi = blk(re_ref, order[0]), blk(im_ref, order[0])
    for i in order[1:]:
        fr, fi = step(fr, fi, i)

    mr, mi = ar, ai
    for _ in range(n_blk.bit_length() - 1):
        mr, mi = _cmul((mr, mi), (mr, mi))
    levels, _ = _scan_tables(mr, mi, width, reverse)
    row = lax.broadcasted_iota(jnp.int32, shape, 0)
    edge_in = SUBLANES - 1 if reverse else 0
    sh1 = SUBLANES - 1 if reverse else 1
    gr = jnp.where(row == edge_in, jnp.broadcast_to(car_ref[b, 0:1, :], shape), pltpu.roll(fr, sh1, 0))
    gi = jnp.where(row == edge_in, jnp.broadcast_to(car_ref[b, 1:2, :], shape), pltpu.roll(fi, sh1, 0))
    for (lr, li), k in zip(levels, (1, 2, 4)):
        sh = (SUBLANES - k) if reverse else k
        sr = pltpu.roll(gr, sh, 0)
        si = pltpu.roll(gi, sh, 0)
        gr, gi = gr + (lr * sr - li * si), gi + (lr * si + li * sr)
    mbr = jnp.broadcast_to(mr, shape)
    mbi = jnp.broadcast_to(mi, shape)
    edge_out = 0 if reverse else SUBLANES - 1
    car_ref[b, 0:1, :] = (fr + (mbr * gr - mbi * gi))[edge_out:edge_out + 1, :]
    car_ref[b, 1:2, :] = (fi + (mbr * gi + mbi * gr))[edge_out:edge_out + 1, :]

    sr, si = gr, gi
    for i in order:
        sr, si = step(sr, si, i)
        re_ref[b, i * SUBLANES:(i + 1) * SUBLANES, :] = sr
        im_ref[b, i * SUBLANES:(i + 1) * SUBLANES, :] = si
        if on_block is not None:
            on_block(i, sr, si)


def _ssm_fwd(u, bb_re, bb_im, c_re_t, c_imn_t, d_row, ab_re, ab_im, w_out_b, w_glu_b, conv_p, n_seq, seq):
    tt = SCAN_TILE
    nt = seq // tt

    def body(u_ref, bbre, bbim, cre, cimn, d_ref, are, aim, wout_ref, wglu_ref, cw_ref,
             sre_ref, sim_ref, y_ref, oout_ref, oglu_ref, ocw_ref,
             up_ref, car_ref, send_sems, recv_sems, loc_sems):
        j = pl.program_id(0)
        t = pl.program_id(1)
        gather = _TwoLevelGather(
            [wout_ref, wglu_ref, cw_ref],
            [lambda dev: oout_ref.at[pl.ds(pl.multiple_of(dev * OUT_ROWS_PER_DEV, OUT_ROWS_PER_DEV), OUT_ROWS_PER_DEV), :],
             lambda dev: oglu_ref.at[pl.ds(pl.multiple_of(dev * GLU_ROWS_PER_DEV, GLU_ROWS_PER_DEV), GLU_ROWS_PER_DEV), :],
             lambda dev: ocw_ref.at[dev]],
            send_sems, recv_sems, loc_sems)

        @pl.when((j == 0) & (t == 0))
        def _():
            gather.start()

        @pl.when((j == N_JBLK // 2) & (t == 0))
        def _():
            gather.forward()

        @pl.when(t == 0)
        def _():
            car_ref[...] = jnp.zeros_like(car_ref)

        for b in range(n_seq):
            _load_chunked(u_ref, b, up_ref, tt)
            up = up_ref[b]
            ub = up.astype(BF16)
            sre_ref[b] = _dot(ub, bbre[0])
            sim_ref[b] = _dot(ub, bbim[0])
            _chunk_scan(sre_ref, sim_ref, b, car_ref, are[...], aim[...], tt, reverse=False)
            yp = (_dot(sre_ref[b].astype(BF16), cre[0]) + _dot(sim_ref[b].astype(BF16), cimn[0])
                  + d_ref[...] * up)
            _store_chunked(yp, y_ref, b, tt)

        @pl.when((j == N_JBLK - 1) & (t == nt - 1))
        def _():
            gather.finish()

    tok = lambda j, t: (0, t, j)
    blk3 = lambda j, t: (j, 0, 0)
    row = lambda j, t: (0, j)
    st = jax.ShapeDtypeStruct((n_seq, seq, N_JBLK * JB_ST), F32)
    n_arr = 3
    return _pcall(
        body, name="ssm_fwd", grid=(N_JBLK, nt),
        out_shape=(st, st, jax.ShapeDtypeStruct((n_seq, seq, SSM_W), F32),
                   jax.ShapeDtypeStruct((D_MODEL, D_MODEL), BF16), jax.ShapeDtypeStruct((SSM_W, SSM_W), BF16),
                   jax.ShapeDtypeStruct((N_DEV, SUBLANES, LANES), F32)),
        in_specs=[pl.BlockSpec((n_seq, tt, JB_CH), tok),
                  pl.BlockSpec((1, JB_CH, JB_ST), blk3), pl.BlockSpec((1, JB_CH, JB_ST), blk3),
                  pl.BlockSpec((1, JB_ST, JB_CH), blk3), pl.BlockSpec((1, JB_ST, JB_CH), blk3),
                  pl.BlockSpec((1, JB_CH), row), pl.BlockSpec((1, JB_ST), row), pl.BlockSpec((1, JB_ST), row),
                  HBM_SPEC, HBM_SPEC, HBM_SPEC],
        out_specs=(pl.BlockSpec((n_seq, tt, JB_ST), tok), pl.BlockSpec((n_seq, tt, JB_ST), tok),
                   pl.BlockSpec((n_seq, tt, JB_CH), tok), HBM_SPEC, HBM_SPEC, HBM_SPEC),
        scratch_shapes=[pltpu.VMEM((n_seq, tt, JB_CH), F32), pltpu.VMEM((n_seq, SUBLANES, JB_ST), F32),
                        pltpu.SemaphoreType.DMA((7 * n_arr,)), pltpu.SemaphoreType.DMA((7 * n_arr,)),
                        pltpu.SemaphoreType.DMA((n_arr,))],
        compiler_params=_params(2),
    )(u, bb_re, bb_im, c_re_t, c_imn_t, d_row, ab_re, ab_im, w_out_b, w_glu_b, conv_p)


def _ssm_bwd(dy, u, s_re, s_im, bb_re, bb_im, c_re_t, c_imn_t, d_row, ab_re, ab_im, g_out, g_glu, n_seq, seq):
    tt = SCAN_TILE
    nt = seq // tt
    rows8 = tt // SUBLANES

    def body(dy_ref, u_ref, sre_ref, sim_ref, pre_ref, pim_ref, bbre, bbim, cre, cimn, d_ref, are, aim,
             gout_ref, gglu_ref,
             du_ref, dcre_ref, dcim_ref, dbbre_ref, dbbim_ref, dare_ref, daim_ref, dd_ref, rout_ref, rglu_ref,
             lre_ref, lim_ref, dyp_ref, up_ref, car_ref, send_sems, recv_sems, loc_sems):
        j = pl.program_id(0)
        tr = pl.program_id(1)

        def exchange():
            return _direct_copies(lambda pid: [gout_ref.at[pid], gglu_ref.at[pid]], [rout_ref, rglu_ref],
                                  send_sems, recv_sems, loc_sems)

        @pl.when((j == 0) & (tr == 0))
        def _():
            mine, sends = exchange()
            for cp in mine + sends:
                cp.start()

        @pl.when(tr == 0)
        def _():
            car_ref[...] = jnp.zeros_like(car_ref)
            for r in (dcre_ref, dcim_ref, dbbre_ref, dbbim_ref, dare_ref, daim_ref, dd_ref):
                r[...] = jnp.zeros_like(r)

        first = tr == nt - 1
        row = lax.broadcasted_iota(jnp.int32, (SUBLANES, JB_ST), 0)
        n_blk = tt // SUBLANES
        for b in range(n_seq):
            _load_chunked(dy_ref, b, dyp_ref, tt)
            _load_chunked(u_ref, b, up_ref, tt)
            dyp = dyp_ref[b]
            up = up_ref[b]
            dyb = dyp.astype(BF16)
            ub = up.astype(BF16)
            lre_ref[b] = _dot_nt(dyb, cre[0])
            lim_ref[b] = _dot_nt(dyb, cimn[0])
            acc = [jnp.zeros((SUBLANES, JB_ST), F32), jnp.zeros((SUBLANES, JB_ST), F32)]

            def on_block(i, lr, li, b=b, acc=acc):
                if i > 0:
                    spr = sre_ref[b, (i - 1) * SUBLANES:i * SUBLANES, :]
                    spi = sim_ref[b, (i - 1) * SUBLANES:i * SUBLANES, :]
                else:
                    hr = jnp.where(first, 0.0, pre_ref[b, SUBLANES - 1:SUBLANES, :])
                    hi = jnp.where(first, 0.0, pim_ref[b, SUBLANES - 1:SUBLANES, :])
                    last_r = sre_ref[b, (n_blk - 1) * SUBLANES:n_blk * SUBLANES, :]
                    last_i = sim_ref[b, (n_blk - 1) * SUBLANES:n_blk * SUBLANES, :]
                    spr = jnp.where(row == 0, jnp.broadcast_to(hr, row.shape), pltpu.roll(last_r, 1, 0))
                    spi = jnp.where(row == 0, jnp.broadcast_to(hi, row.shape), pltpu.roll(last_i, 1, 0))
                acc[0] = acc[0] + (lr * spr + li * spi)
                acc[1] = acc[1] + (li * spr - lr * spi)

            _chunk_scan(lre_ref, lim_ref, b, car_ref, are[...], -aim[...], tt, reverse=True, on_block=on_block)
            dare_ref[...] += jnp.sum(acc[0], axis=0, keepdims=True)
            daim_ref[...] += jnp.sum(acc[1], axis=0, keepdims=True)
            lrb = lre_ref[b].astype(BF16)
            lib = lim_ref[b].astype(BF16)
            dup = d_ref[...] * dyp + _dot_nt(lrb, bbre[0]) + _dot_nt(lib, bbim[0])
            _store_chunked(dup, du_ref, b, tt)
            dbbre_ref[0] += _dot_tn(ub, lrb)
            dbbim_ref[0] += _dot_tn(ub, lib)
            dcre_ref[0] += _dot_tn(dyb, sre_ref[b].astype(BF16))
            dcim_ref[0] += _dot_tn(dyb, sim_ref[b].astype(BF16))
            dd_ref[...] += jnp.sum(dyp * up, axis=0, keepdims=True)

        @pl.when((j == N_JBLK - 1) & (tr == nt - 1))
        def _():
            mine, sends = exchange()
            for cp in sends + mine:
                cp.wait()

    tok = lambda j, t: (0, nt - 1 - t, j)
    halo = lambda j, t: (0, jnp.maximum((nt - 1 - t) * rows8 - 1, 0), j)
    blk3 = lambda j, t: (j, 0, 0)
    row1 = lambda j, t: (0, j)
    acc_shape = jax.ShapeDtypeStruct((N_JBLK, JB_CH, JB_ST), F32)
    return _pcall(
        body, name="ssm_bwd", grid=(N_JBLK, nt),
        out_shape=(jax.ShapeDtypeStruct((n_seq, seq, SSM_W), F32), acc_shape, acc_shape, acc_shape, acc_shape,
                   jax.ShapeDtypeStruct((1, N_JBLK * JB_ST), F32), jax.ShapeDtypeStruct((1, N_JBLK * JB_ST), F32),
                   jax.ShapeDtypeStruct((1, SSM_W), F32),
                   jax.ShapeDtypeStruct((N_DEV,) + g_out.shape[1:], F32),
                   jax.ShapeDtypeStruct((N_DEV,) + g_glu.shape[1:], F32)),
        in_specs=[pl.BlockSpec((n_seq, tt, JB_CH), tok), pl.BlockSpec((n_seq, tt, JB_CH), tok),
                  pl.BlockSpec((n_seq, tt, JB_ST), tok), pl.BlockSpec((n_seq, tt, JB_ST), tok),
                  pl.BlockSpec((n_seq, SUBLANES, JB_ST), halo), pl.BlockSpec((n_seq, SUBLANES, JB_ST), halo),
                  pl.BlockSpec((1, JB_CH, JB_ST), blk3), pl.BlockSpec((1, JB_CH, JB_ST), blk3),
                  pl.BlockSpec((1, JB_ST, JB_CH), blk3), pl.BlockSpec((1, JB_ST, JB_CH), blk3),
                  pl.BlockSpec((1, JB_CH), row1), pl.BlockSpec((1, JB_ST), row1), pl.BlockSpec((1, JB_ST), row1),
                  HBM_SPEC, HBM_SPEC],
        out_specs=(pl.BlockSpec((n_seq, tt, JB_CH), tok),
                   pl.BlockSpec((1, JB_CH, JB_ST), blk3), pl.BlockSpec((1, JB_CH, JB_ST), blk3),
                   pl.BlockSpec((1, JB_CH, JB_ST), blk3), pl.BlockSpec((1, JB_CH, JB_ST), blk3),
                   pl.BlockSpec((1, JB_ST), row1), pl.BlockSpec((1, JB_ST), row1), pl.BlockSpec((1, JB_CH), row1),
                   HBM_SPEC, HBM_SPEC),
        scratch_shapes=[pltpu.VMEM((n_seq, tt, JB_ST), F32), pltpu.VMEM((n_seq, tt, JB_ST), F32),
                        pltpu.VMEM((n_seq, tt, JB_CH), F32), pltpu.VMEM((n_seq, tt, JB_CH), F32),
                        pltpu.VMEM((n_seq, SUBLANES, JB_ST), F32),
                        pltpu.SemaphoreType.DMA((7 * 2,)), pltpu.SemaphoreType.DMA((7 * 2,)),
                        pltpu.SemaphoreType.DMA((2,))],
        compiler_params=_params(2),
    )(dy, u, s_re, s_im, s_re, s_im, bb_re, bb_im, c_re_t, c_imn_t, d_row, ab_re, ab_im, g_out, g_glu)


def _mix(x2, tgt2, y, proj, gf, b_glu, conv8, w_glu_f, w_out_f, seq):
    n = x2.shape[0]
    tm = TOK_TILE
    tiles_per_seq = seq // tm
    rows8 = tm // SUBLANES

    def body(x_ref, t_ref, y_ref, zs_ref, h_ref, bc_ref, cc_ref, zc_ref, hp_ref, ccp_ref,
             gf_ref, bg_ref, cw_ref, wg_ref, wo_ref,
             dh2_ref, dy_ref, dzs_ref, dbc_ref, dzc_ref, dyc_ref,
             dwo_ref, dwg_ref, loss_ref, dgf_ref, dbg_ref, dcw_ref):
        i = pl.program_id(0)

        @pl.when(i == 0)
        def _():
            for r in (dwo_ref, dwg_ref, loss_ref, dgf_ref, dbg_ref, dcw_ref):
                r[...] = jnp.zeros_like(r)

        yv = y_ref[...]
        y1, dgelu = _gelu_and_grad(yv)
        y1b = y1.astype(BF16)
        gate = _sigmoid(_dot(y1b, wg_ref[...]) + bg_ref[...])
        y2 = y1 * gate
        szs, dszs = _silu_and_grad(zs_ref[...])
        yssm = y2 * szs
        hv = h_ref[...]
        ccv = cc_ref[...]
        bcv = bc_ref[...]
        v = ccv * hv
        first = (i % tiles_per_seq) == 0
        vhalo = jnp.where(first, 0.0, ccp_ref[...] * hp_ref[...])
        v1 = _shift_down(v, vhalo, 1)
        v2 = _shift_down(v, vhalo, 2)
        w0 = cw_ref[0:1, :]
        w1 = cw_ref[1:2, :]
        w2 = cw_ref[2:3, :]
        yc = w0 * v2 + w1 * v1 + w2 * v
        szc, dszc = _silu_and_grad(zc_ref[...])
        yconv = (bcv * yc) * szc
        ysb = yssm.astype(BF16)
        ycb = yconv.astype(BF16)
        h2 = x_ref[...] + _dot(ysb, wo_ref[0:SSM_W, :]) + _dot(ycb, wo_ref[SSM_W:, :])
        r2 = lax.rsqrt(jnp.mean(h2 * h2, axis=-1, keepdims=True) + EPS)
        hn = h2 * r2
        gfv = gf_ref[...]
        err = hn * gfv - t_ref[...]
        loss_ref[...] += 0.5 * jnp.sum(jnp.mean(err * err, axis=-1, keepdims=True))
        dout = err * (1.0 / D_MODEL)
        dgf_ref[...] += jnp.sum(dout * hn, axis=0, keepdims=True)
        dn = dout * gfv
        dh2 = r2 * (dn - hn * jnp.mean(dn * hn, axis=-1, keepdims=True))
        dh2_ref[...] = dh2
        dh2b = dh2.astype(BF16)
        dwo_ref[0:SSM_W, :] += _dot_tn(ysb, dh2b)
        dwo_ref[SSM_W:, :] += _dot_tn(ycb, dh2b)
        dyssm = _dot_nt(dh2b, wo_ref[0:SSM_W, :])
        dyconv = _dot_nt(dh2b, wo_ref[SSM_W:, :])
        dy2 = dyssm * szs
        dzs_ref[...] = dyssm * y2 * dszs
        dgp = dy2 * y1 * (gate * (1.0 - gate))
        dgpb = dgp.astype(BF16)
        dy1 = dy2 * gate + _dot_nt(dgpb, wg_ref[...])
        dwg_ref[...] += _dot_tn(y1b, dgpb)
        dbg_ref[...] += jnp.sum(dgp, axis=0, keepdims=True)
        dy_ref[...] = dy1 * dgelu
        dbc_ref[...] = dyconv * yc * szc
        dyc = dyconv * bcv * szc
        dyc_ref[...] = dyc
        dzc_ref[...] = dyconv * bcv * yc * dszc
        dcw_ref[0:1, :] += jnp.sum(dyc * v2, axis=0, keepdims=True)
        dcw_ref[1:2, :] += jnp.sum(dyc * v1, axis=0, keepdims=True)
        dcw_ref[2:3, :] += jnp.sum(dyc * v, axis=0, keepdims=True)

    tile_d = pl.BlockSpec((tm, D_MODEL), lambda i: (i, 0))
    tile_s = pl.BlockSpec((tm, SSM_W), lambda i: (i, 0))
    seg_of = lambda c: pl.BlockSpec((tm, SSM_W), lambda i: (i, c))
    halo_of = lambda c: pl.BlockSpec((SUBLANES, SSM_W), lambda i: (jnp.maximum(i * rows8 - 1, 0), c))
    const = lambda shape: pl.BlockSpec(shape, lambda i: (0,) * len(shape))
    seg = jax.ShapeDtypeStruct((n, SSM_W), F32)
    return _pcall(
        body, name="mix", grid=(n // tm,),
        out_shape=(jax.ShapeDtypeStruct((n, D_MODEL), F32), seg, seg, seg, seg, seg,
                   jax.ShapeDtypeStruct((D_MODEL, D_MODEL), F32), jax.ShapeDtypeStruct((SSM_W, SSM_W), F32),
                   jax.ShapeDtypeStruct((SUBLANES, LANES), F32), jax.ShapeDtypeStruct((1, D_MODEL), F32),
                   jax.ShapeDtypeStruct((1, SSM_W), F32), jax.ShapeDtypeStruct((SUBLANES, CONV_W), F32)),
        in_specs=[tile_d, tile_d, tile_s, seg_of(SEG_ZS), seg_of(SEG_H), seg_of(SEG_BC), seg_of(SEG_CC), seg_of(SEG_ZC),
                  halo_of(SEG_H), halo_of(SEG_CC),
                  const((1, D_MODEL)), const((1, SSM_W)), const((SUBLANES, CONV_W)),
                  const((SSM_W, SSM_W)), const((D_MODEL, D_MODEL))],
        out_specs=(tile_d, tile_s, tile_s, tile_s, tile_s, tile_s,
                   const((D_MODEL, D_MODEL)), const((SSM_W, SSM_W)), const((SUBLANES, LANES)),
                   const((1, D_MODEL)), const((1, SSM_W)), const((SUBLANES, CONV_W))),
        compiler_params=_params(1),
    )(x2, tgt2, y, proj, proj, proj, proj, proj, proj, proj, gf, b_glu, conv8, w_glu_f, w_out_f)


def _in_bwd(x2, dh2, du, dzs, dyc, proj, dbc, dzc, g1, conv8, w_full, seq):
    n = x2.shape[0]
    tm = TOK_TILE
    n_tiles = n // tm
    tiles_per_seq = seq // tm
    rows8 = tm // SUBLANES
    n_blk8 = n // SUBLANES

    def body(x_ref, dh2_ref, du_ref, dzs_ref, dyc_ref, dycn_ref, h_ref, cc_ref, dbc_ref, dzc_ref,
             g_ref, cw_ref, w_ref, gx_ref, dp_ref, dg_ref):
        i = pl.program_id(0)

        @pl.when(i == 0)
        def _():
            dg_ref[...] = jnp.zeros_like(dg_ref)

        dyc = dyc_ref[...]
        last = (i % tiles_per_seq) == tiles_per_seq - 1
        nhalo = jnp.where(last, 0.0, dycn_ref[...])
        dv = (cw_ref[2:3, :] * dyc + cw_ref[1:2, :] * _shift_up(dyc, nhalo, 1)
              + cw_ref[0:1, :] * _shift_up(dyc, nhalo, 2))
        parts = (du_ref[...], dzs_ref[...], dv * cc_ref[...], dbc_ref[...], dv * h_ref[...], dzc_ref[...])
        dxn = jnp.zeros((tm, D_MODEL), F32)
        for k, p in enumerate(parts):
            pb = p.astype(BF16)
            dp_ref[:, k * SSM_W:(k + 1) * SSM_W] = pb
            dxn = dxn + _dot_nt(pb, w_ref[:, k * SSM_W:(k + 1) * SSM_W])
        x = x_ref[...]
        r = lax.rsqrt(jnp.mean(x * x, axis=-1, keepdims=True) + EPS)
        xh = x * r
        dg_ref[...] += jnp.sum(dxn * xh, axis=0, keepdims=True)
        dn = dxn * g_ref[...]
        gx_ref[...] = dh2_ref[...] + r * (dn - xh * jnp.mean(dn * xh, axis=-1, keepdims=True))

    tile_d = pl.BlockSpec((tm, D_MODEL), lambda i: (i, 0))
    tile_s = pl.BlockSpec((tm, SSM_W), lambda i: (i, 0))
    seg_of = lambda c: pl.BlockSpec((tm, SSM_W), lambda i: (i, c))
    nhalo = pl.BlockSpec((SUBLANES, SSM_W), lambda i: (jnp.minimum((i + 1) * rows8, n_blk8 - 1), 0))
    const = lambda shape: pl.BlockSpec(shape, lambda i: (0,) * len(shape))
    return _pcall(
        body, name="in_bwd", grid=(n_tiles,),
        out_shape=(jax.ShapeDtypeStruct((n, D_MODEL), F32), jax.ShapeDtypeStruct((n, IN_COLS), BF16),
                   jax.ShapeDtypeStruct((SUBLANES, D_MODEL), F32)),
        in_specs=[tile_d, tile_d, tile_s, tile_s, tile_s, nhalo, seg_of(SEG_H), seg_of(SEG_CC), tile_s, tile_s,
                  const((1, D_MODEL)), const((SUBLANES, CONV_W)), const((D_MODEL, IN_COLS))],
        out_specs=(tile_d, pl.BlockSpec((tm, IN_COLS), lambda i: (i, 0)), const((SUBLANES, D_MODEL))),
        compiler_params=_params(1),
    )(x2, dh2, du, dzs, dyc, dyc, proj, proj, dbc, dzc, g1, conv8, w_full)


def _dw_in_exchange(order, xn, dproj, small):
    n = xn.shape[0]
    tk = 512
    nk = n // tk
    piece = (D_MODEL, COLS_PER_DEV)

    def body(order_ref, xn_ref, dp_ref, sm_ref, own_ref, rchip_ref, rsm_ref,
             acc, stage, sbuf, give_send, give_recv, keep_send, keep_recv, sm_send, sm_recv, sm_loc):
        del order_ref
        s = pl.program_id(0)
        x, y, c = _mesh_pos()
        sib = (x, y, 1 - c)
        chips = [(1 - x, y), (x, 1 - y), (1 - x, 1 - y)]
        gather = _TwoLevelGather([sm_ref], [lambda dev: rsm_ref.at[dev]], sm_send, sm_recv, sm_loc)

        def give(i):
            return pltpu.make_async_remote_copy(src_ref=acc.at[0], dst_ref=stage.at[i], send_sem=give_send.at[i],
                                                recv_sem=give_recv.at[i], device_id=sib, device_id_type=MESH)

        def keep(i):
            return pltpu.make_async_remote_copy(src_ref=sbuf.at[i], dst_ref=rchip_ref.at[i], send_sem=keep_send.at[i],
                                                recv_sem=keep_recv.at[i], device_id=(*chips[i], c), device_id_type=MESH)

        @pl.when(s == 0)
        def _():
            gather.start()

        @pl.when(s == N_DEV // 2)
        def _():
            gather.forward()

        for k in (2, 4, 6):
            @pl.when(s == k)
            def _(k=k):
                give(k // 2 - 1).wait_send()

        slot = s % 2
        acc[slot] = _dot_tn(xn_ref[pl.ds(0, tk), :], dp_ref[pl.ds(0, tk), :])

        def kstep(kk, carry):
            off = pl.multiple_of(kk * tk, tk)
            acc[slot] += _dot_tn(xn_ref[pl.ds(off, tk), :], dp_ref[pl.ds(off, tk), :])
            return carry

        lax.fori_loop(1, nk, kstep, 0)

        for k in range(N_DEV):
            @pl.when(s == k)
            def _(k=k):
                i = k // 2
                if k % 2 == 0:
                    give(i).start()
                else:
                    give(i).wait_recv()
                    total = acc[1] + stage[i]
                    if i < 3:
                        sbuf[i] = total.astype(BF16)
                        keep(i).start()
                    else:
                        own_ref[...] = total

        @pl.when(s == N_DEV - 1)
        def _():
            give(3).wait_send()
            for i in range(3):
                keep(i).wait()
            gather.finish()

    grid_spec = pltpu.PrefetchScalarGridSpec(
        num_scalar_prefetch=1, grid=(N_DEV,),
        in_specs=[pl.BlockSpec(memory_space=pltpu.VMEM),
                  pl.BlockSpec((n, COLS_PER_DEV), lambda s, order: (0, order[s])),
                  HBM_SPEC],
        out_specs=(pl.BlockSpec(piece, lambda s, order: (0, 0)), HBM_SPEC, HBM_SPEC),
        scratch_shapes=[pltpu.VMEM((2,) + piece, F32), pltpu.VMEM((4,) + piece, F32), pltpu.VMEM((3,) + piece, BF16),
                        pltpu.SemaphoreType.DMA((4,)), pltpu.SemaphoreType.DMA((4,)),
                        pltpu.SemaphoreType.DMA((3,)), pltpu.SemaphoreType.DMA((3,)),
                        pltpu.SemaphoreType.DMA((7,)), pltpu.SemaphoreType.DMA((7,)), pltpu.SemaphoreType.DMA((1,))])
    return _pcall(
        body, name="dw_in_exchange", grid_spec=grid_spec,
        out_shape=(jax.ShapeDtypeStruct(piece, F32), jax.ShapeDtypeStruct((3,) + piece, BF16),
                   jax.ShapeDtypeStruct((N_DEV,) + small.shape, F32)),
        compiler_params=_params(1),
    )(order, xn, dproj, small)


def _adamw(g, w, m, v):
    m_new = ADAM_B1 * m + (1.0 - ADAM_B1) * g
    v_new = ADAM_B2 * v + (1.0 - ADAM_B2) * (g * g)
    m_hat = m_new / (1.0 - ADAM_B1 ** ADAM_STEP)
    v_hat = v_new / (1.0 - ADAM_B2 ** ADAM_STEP)
    delta = -ADAM_LR * (m_hat / (jnp.sqrt(v_hat) + ADAM_EPS) + ADAM_WD * w)
    return delta, m_new, v_new


def _reduce_adam(recv, w, m, v, name, row_tile):
    rows, cols = w.shape

    def body(r_ref, w_ref, m_ref, v_ref, g_ref, d_ref, nm_ref, nv_ref):
        g = r_ref[0]
        for s in range(1, N_DEV):
            g = g + r_ref[s]
        g_ref[...] = g
        d_ref[...], nm_ref[...], nv_ref[...] = _adamw(g, w_ref[...], m_ref[...], v_ref[...])

    tile = pl.BlockSpec((row_tile, cols), lambda i: (i, 0))
    shp = jax.ShapeDtypeStruct((rows, cols), F32)
    return _pcall(
        body, name=name, grid=(rows // row_tile,),
        out_shape=(shp,) * 4,
        in_specs=[pl.BlockSpec((N_DEV, row_tile, cols), lambda i: (0, i, 0)), tile, tile, tile],
        out_specs=(tile,) * 4,
        compiler_params=_params(1),
    )(recv, w, m, v)


def _reduce_adam_w_in(own, rchip, w, m, v):
    rows, cols = w.shape
    row_tile = 256

    def body(o_ref, r_ref, w_ref, m_ref, v_ref, g_ref, d_ref, nm_ref, nv_ref):
        g = o_ref[...]
        for s in range(3):
            g = g + r_ref[s].astype(F32)
        g_ref[...] = g
        d_ref[...], nm_ref[...], nv_ref[...] = _adamw(g, w_ref[...], m_ref[...], v_ref[...])

    tile = pl.BlockSpec((row_tile, cols), lambda i: (i, 0))
    shp = jax.ShapeDtypeStruct((rows, cols), F32)
    return _pcall(
        body, name="reduce_adam_w_in", grid=(rows // row_tile,),
        out_shape=(shp,) * 4,
        in_specs=[tile, pl.BlockSpec((3, row_tile, cols), lambda i: (0, i, 0)), tile, tile, tile],
        out_specs=(tile,) * 4,
        compiler_params=_params(1),
    )(own, rchip, w, m, v)


def _reduce_adam_stacked(recv, wmv, name):
    _, rows, cols = wmv.shape

    def body(r_ref, p_ref, o_ref):
        g = r_ref[0]
        for s in range(1, N_DEV):
            g = g + r_ref[s]
        o_ref[0] = g
        o_ref[1], o_ref[2], o_ref[3] = _adamw(g, p_ref[0], p_ref[1], p_ref[2])

    return _pcall(body, name=name, out_shape=jax.ShapeDtypeStruct((4, rows, cols), F32),
                  compiler_params=_params(0))(recv, wmv)


_SMALL = (("loss", 1), ("norm_gain", D_MODEL), ("final_norm_gain", D_MODEL), ("b_glu", SSM_W),
          ("ssm_a_re", N_GROUPS * STATE), ("ssm_a_im", N_GROUPS * STATE), ("ssm_log_dt", N_GROUPS),
          ("ssm_b_re", N_GROUPS * STATE * GROUP), ("ssm_b_im", N_GROUPS * STATE * GROUP),
          ("ssm_c_re", N_GROUPS * STATE * GROUP), ("ssm_c_im", N_GROUPS * STATE * GROUP),
          ("ssm_d", N_GROUPS * GROUP), ("conv_w", 3 * CONV_W))
_PACK_UNIT = SUBLANES * LANES


def _pack_small(dicts):
    cols = []
    for name, size in _SMALL:
        flat = jnp.stack([d[name].reshape(-1) for d in dicts])
        padded = -(-size // _PACK_UNIT) * _PACK_UNIT
        cols.append(jnp.pad(flat, ((0, 0), (0, padded - size))).reshape(len(dicts), -1, LANES))
    return jnp.concatenate(cols, axis=1)


def _unpack_small(packed):
    out, r0 = {}, 0
    for name, size in _SMALL:
        nrows = -(-size // _PACK_UNIT) * SUBLANES
        out[name] = packed[:, r0:r0 + nrows].reshape(packed.shape[0], -1)[:, :size]
        r0 += nrows
    return out


def _block_diag(m4):
    eye = jnp.eye(SUBLANES, dtype=m4.dtype)
    j, g, a, b = m4.shape
    return jnp.einsum("jgab,gk->jgakb", m4, eye).reshape(j, g * a, g * b)


def _block_diag_extract(dense, a, b):
    d5 = dense.reshape(N_JBLK, SUBLANES, a, SUBLANES, b)
    return jnp.stack([d5[:, g, :, g, :] for g in range(SUBLANES)], axis=1)


def kernel(x, norm_gain, w_in, ssm_a_re, ssm_a_im, ssm_log_dt, ssm_b_re, ssm_b_im, ssm_c_re, ssm_c_im, ssm_d, w_glu, b_glu, conv_w, w_out, final_norm_gain, loss_target, m_norm_gain, m_w_in, m_ssm_a_re, m_ssm_a_im, m_ssm_log_dt, m_ssm_b_re, m_ssm_b_im, m_ssm_c_re, m_ssm_c_im, m_ssm_d, m_w_glu, m_b_glu, m_conv_w, m_w_out, m_final_norm_gain, v_norm_gain, v_w_in, v_ssm_a_re, v_ssm_a_im, v_ssm_log_dt, v_ssm_b_re, v_ssm_b_im, v_ssm_c_re, v_ssm_c_im, v_ssm_d, v_w_glu, v_b_glu, v_conv_w, v_w_out, v_final_norm_gain):
    n_seq, seq, _ = x.shape
    n = n_seq * seq
    me = 4 * lax.axis_index("x") + 2 * lax.axis_index("y") + lax.axis_index("c")

    rep = lambda a: jnp.repeat(a[0], GROUP, axis=1)
    a_re_r, a_im_r = rep(ssm_a_re), rep(ssm_a_im)
    log_dt = ssm_log_dt[0].reshape(N_GROUPS, 1)
    b_re2 = ssm_b_re[0].reshape(N_GROUPS, STATE * GROUP)
    b_im2 = ssm_b_im[0].reshape(N_GROUPS, STATE * GROUP)
    ab_re_r, ab_im_r, bb_re2, bb_im2 = _ssm_disc(a_re_r, a_im_r, log_dt, b_re2, b_im2)
    ab_re = ab_re_r[:, ::GROUP].reshape(1, N_GROUPS * STATE)
    ab_im = ab_im_r[:, ::GROUP].reshape(1, N_GROUPS * STATE)

    def bb_mat(bb2):
        t = jnp.transpose(bb2.reshape(N_JBLK, SUBLANES, STATE, GROUP), (0, 1, 3, 2))
        return _block_diag(t).astype(BF16)

    def c_mat(c3, sign):
        t = jnp.transpose(c3.reshape(N_JBLK, SUBLANES, GROUP, STATE), (0, 1, 3, 2))
        return _block_diag(sign * t).astype(BF16)

    bb_re_m, bb_im_m = bb_mat(bb_re2), bb_mat(bb_im2)
    c_re_m, c_imn_m = c_mat(ssm_c_re[0], 1.0), c_mat(ssm_c_im[0], -1.0)
    d_row = ssm_d[0].reshape(1, SSM_W)

    x2 = x.reshape(n, D_MODEL)
    tgt2 = loss_target.reshape(n, D_MODEL)
    mx, my, mc = lax.axis_index("x"), lax.axis_index("y"), lax.axis_index("c")
    arrival = [4 * mx + 2 * my + mc, 4 * mx + 2 * my + (1 - mc)]
    for cx, cy in ((1 - mx, my), (mx, 1 - my), (1 - mx, 1 - my)):
        arrival += [4 * cx + 2 * cy + mc, 4 * cx + 2 * cy + (1 - mc)]
    xn, proj, w_in_f = _in_proj(jnp.stack(arrival).astype(jnp.int32), x2, norm_gain, w_in[0].astype(BF16))
    u3 = proj.reshape(n_seq, seq, IN_COLS)
    conv_p = jnp.pad(conv_w[0], ((0, SUBLANES - 3), (0, LANES - CONV_COLS_PER_DEV)))
    s_re, s_im, y3, w_out_f, w_glu_f, conv_all = _ssm_fwd(
        u3, bb_re_m, bb_im_m, c_re_m, c_imn_m, d_row, ab_re, ab_im,
        w_out[0].astype(BF16), w_glu[0].astype(BF16), conv_p, n_seq, seq)
    conv8 = jnp.transpose(conv_all[:, :, :CONV_COLS_PER_DEV], (1, 0, 2)).reshape(SUBLANES, CONV_W)
    (dh2, dy, dzs, dbc, dzc, dyc, dw_out, dw_glu, loss_t, dgf, dbg, dcw) = _mix(
        x2, tgt2, y3.reshape(n, SSM_W), proj, final_norm_gain.reshape(1, D_MODEL), b_glu, conv8,
        w_glu_f, w_out_f, seq)

    du3, dc_re_d, dc_im_d, dbb_re_d, dbb_im_d, dab_re, dab_im, dd, r_out, r_glu = _ssm_bwd(
        dy.reshape(n_seq, seq, SSM_W), u3, s_re, s_im, bb_re_m, bb_im_m, c_re_m, c_imn_m, d_row, ab_re, ab_im,
        dw_out.reshape(N_DEV, OUT_ROWS_PER_DEV, D_MODEL), dw_glu.reshape(N_DEV, GLU_ROWS_PER_DEV, SSM_W), n_seq, seq)
    du = du3.reshape(n, SSM_W)
    g_c_re = _block_diag_extract(dc_re_d, GROUP, STATE).reshape(N_GROUPS, GROUP, STATE)
    g_c_im = -_block_diag_extract(dc_im_d, GROUP, STATE).reshape(N_GROUPS, GROUP, STATE)

    def bb_grad(dense):
        t = _block_diag_extract(dense, GROUP, STATE)
        return jnp.transpose(t, (0, 1, 3, 2)).reshape(N_GROUPS, STATE * GROUP)

    def ab_grad(row):
        z = jnp.zeros((N_GROUPS, STATE, GROUP), F32)
        return z.at[:, :, 0].set(row.reshape(N_GROUPS, STATE)).reshape(N_GROUPS, STATE * GROUP)

    g_are_r, g_aim_r, g_ldt, g_bre2, g_bim2 = _ssm_disc_bwd(
        a_re_r, a_im_r, log_dt, b_re2, b_im2, ab_grad(dab_re), ab_grad(dab_im),
        bb_grad(dbb_re_d), bb_grad(dbb_im_d))
    grad_x2, dproj, dg8 = _in_bwd(x2, dh2, du, dzs, dyc, proj, dbc, dzc, norm_gain, conv8, w_in_f, seq)
    small_grads = {"loss": loss_t[0:1, 0:1], "norm_gain": dg8[0:1], "final_norm_gain": dgf, "b_glu": dbg,
                   "ssm_a_re": g_are_r[:, ::GROUP], "ssm_a_im": g_aim_r[:, ::GROUP], "ssm_log_dt": g_ldt,
                   "ssm_b_re": g_bre2, "ssm_b_im": g_bim2, "ssm_c_re": g_c_re, "ssm_c_im": g_c_im,
                   "ssm_d": dd, "conv_w": dcw[0:3]}

    order = []
    for cx, cy in ((1 - mx, my), (mx, 1 - my), (1 - mx, 1 - my), (mx, my)):
        order += [4 * cx + 2 * cy + (1 - mc), 4 * cx + 2 * cy + mc]
    own_in, rchip_in, r_small = _dw_in_exchange(jnp.stack(order).astype(jnp.int32), xn, dproj,
                                                _pack_small([small_grads])[0])

    def conv_full(shard):
        return lax.dynamic_update_slice(jnp.zeros((3, CONV_W), F32), shard[0], (0, me * CONV_COLS_PER_DEV))

    zero1 = jnp.zeros((1,), F32)
    triples = dict(loss=(zero1, zero1, zero1), norm_gain=(norm_gain, m_norm_gain, v_norm_gain),
                   final_norm_gain=(final_norm_gain, m_final_norm_gain, v_final_norm_gain),
                   b_glu=(b_glu, m_b_glu, v_b_glu), ssm_a_re=(ssm_a_re, m_ssm_a_re, v_ssm_a_re),
                   ssm_a_im=(ssm_a_im, m_ssm_a_im, v_ssm_a_im), ssm_log_dt=(ssm_log_dt, m_ssm_log_dt, v_ssm_log_dt),
                   ssm_b_re=(ssm_b_re, m_ssm_b_re, v_ssm_b_re), ssm_b_im=(ssm_b_im, m_ssm_b_im, v_ssm_b_im),
                   ssm_c_re=(ssm_c_re, m_ssm_c_re, v_ssm_c_re), ssm_c_im=(ssm_c_im, m_ssm_c_im, v_ssm_c_im),
                   ssm_d=(ssm_d, m_ssm_d, v_ssm_d),
                   conv_w=(conv_full(conv_w), conv_full(m_conv_w), conv_full(v_conv_w)))
    wmv_small = _pack_small([{k: t[i] for k, t in triples.items()} for i in range(3)])

    res_in = _reduce_adam_w_in(own_in, rchip_in, w_in[0], m_w_in[0], v_w_in[0])
    res_out = _reduce_adam(r_out, w_out[0], m_w_out[0], v_w_out[0], "reduce_adam_w_out", OUT_ROWS_PER_DEV)
    res_glu = _reduce_adam(r_glu, w_glu[0], m_w_glu[0], v_w_glu[0], "reduce_adam_w_glu", GLU_ROWS_PER_DEV)
    small = _unpack_small(_reduce_adam_stacked(r_small, wmv_small, "reduce_adam_small"))
    loss = small["loss"][0, 0]

    shapes = dict(norm_gain=(1, D_MODEL), ssm_a_re=(1, N_GROUPS, STATE), ssm_a_im=(1, N_GROUPS, STATE),
                  ssm_log_dt=(1, N_GROUPS), ssm_b_re=(1, N_GROUPS, STATE, GROUP), ssm_b_im=(1, N_GROUPS, STATE, GROUP),
                  ssm_c_re=(1, N_GROUPS, GROUP, STATE), ssm_c_im=(1, N_GROUPS, GROUP, STATE),
                  ssm_d=(1, N_GROUPS, GROUP), b_glu=(1, SSM_W), final_norm_gain=(D_MODEL,))
    big = dict(w_in=res_in, w_glu=res_glu, w_out=res_out)
    small4 = {name: small[name].reshape((4,) + shp) for name, shp in shapes.items()}
    conv4 = lax.dynamic_slice(small["conv_w"].reshape(4, 1, 3, CONV_W), (0, 0, 0, me * CONV_COLS_PER_DEV),
                              (4, 1, 3, CONV_COLS_PER_DEV))

    def leaf(kind, name):
        if name in big:
            return big[name][kind][None]
        if name == "conv_w":
            return conv4[kind]
        return small4[name][kind]

    order = ["norm_gain", "w_in", "ssm_a_re", "ssm_a_im", "ssm_log_dt", "ssm_b_re", "ssm_b_im", "ssm_c_re",
             "ssm_c_im", "ssm_d", "w_glu", "b_glu", "conv_w", "w_out", "final_norm_gain"]
    outs = [loss, grad_x2.reshape(x.shape)]
    for kind in range(4):
        outs += [leaf(kind, name) for name in order]
    return tuple(outs)
```

```python
import functools
import math

import jax
import jax.numpy as jnp
from jax import lax
from jax.experimental import pallas as pl
from jax.experimental.pallas import tpu as pltpu

F32 = jnp.float32
BF16 = jnp.bfloat16

N_DEV = 8
D_MODEL = 1024
SSM_W = 512
CONV_W = 512
N_GROUPS = 32
GROUP = 16
STATE = 64
IN_COLS = 3072
SEG_U, SEG_ZS, SEG_H, SEG_BC, SEG_CC, SEG_ZC = range(6)
COLS_PER_DEV = IN_COLS // N_DEV
N_CHIP = N_DEV // 2
COLS_PER_CHIP = 2 * COLS_PER_DEV
OUT_ROWS_PER_DEV = D_MODEL // N_DEV
GLU_ROWS_PER_DEV = SSM_W // N_DEV
CONV_COLS_PER_DEV = CONV_W // N_DEV
EPS = 1e-6

N_JBLK = 4
JB_CH = SSM_W // N_JBLK
JB_ST = N_GROUPS * STATE // N_JBLK

ADAM_LR = 0.001
ADAM_B1 = 0.9
ADAM_B2 = 0.999
ADAM_EPS = 1e-08
ADAM_WD = 0.01
ADAM_STEP = 10

SUBLANES = 8
LANES = 128
VMEM_LIMIT = 48 * 1024 * 1024
TOK_TILE = 256
IN_TILE = 1024
SCAN_TILE = 256

MESH = pl.DeviceIdType.MESH
HBM_SPEC = pl.BlockSpec(memory_space=pltpu.HBM)


def _pcall(body, **kw):
    return pl.pallas_call(body, **kw)


def _params(n_grid):
    return pltpu.CompilerParams(dimension_semantics=("arbitrary",) * n_grid,
                                vmem_limit_bytes=VMEM_LIMIT)


def _dot(a, b):
    return jnp.dot(a, b, preferred_element_type=F32)


def _dot_nt(a, b):
    return lax.dot_general(a, b, (((1,), (1,)), ((), ())), preferred_element_type=F32)


def _dot_tn(a, b):
    return lax.dot_general(a, b, (((0,), (0,)), ((), ())), preferred_element_type=F32)


def _sigmoid(z):
    return 1.0 / (1.0 + jnp.exp(-z))


_GELU_C = math.sqrt(2.0 / math.pi)


def _gelu_and_grad(y):
    inner = _GELU_C * (y + 0.044715 * (y * y * y))
    t = jnp.tanh(inner)
    g = 0.5 * y * (1.0 + t)
    dg = 0.5 * (1.0 + t) + 0.5 * y * (1.0 - t * t) * (_GELU_C * (1.0 + 3.0 * 0.044715 * (y * y)))
    return g, dg


def _silu_and_grad(z):
    s = _sigmoid(z)
    return z * s, s * (1.0 + z * (1.0 - s))


def _shift_down(v, halo, k):
    rolled = pltpu.roll(v, k, 0)
    row = lax.broadcasted_iota(jnp.int32, v.shape, 0)
    for r in range(k):
        rolled = jnp.where(row == r, halo[SUBLANES - k + r:SUBLANES - k + r + 1, :], rolled)
    return rolled


def _shift_up(v, halo, k):
    n = v.shape[0]
    rolled = pltpu.roll(v, n - k, 0)
    row = lax.broadcasted_iota(jnp.int32, v.shape, 0)
    for r in range(k):
        rolled = jnp.where(row == n - k + r, halo[r:r + 1, :], rolled)
    return rolled


def _mesh_pos():
    return lax.axis_index("x"), lax.axis_index("y"), lax.axis_index("c")


def _direct_copies(srcs_for, out_refs, send_sems, recv_sems, loc_sems):
    x, y, c = _mesh_pos()
    me_id = 4 * x + 2 * y + c
    n_arr = len(out_refs)
    dsts = [r.at[me_id] for r in out_refs]
    own = srcs_for(me_id)
    mine = [pltpu.make_async_copy(own[a], dsts[a], loc_sems.at[a]) for a in range(n_arr)]
    sends = []
    for k in range(1, N_DEV):
        px, py, pc = x ^ ((k >> 2) & 1), y ^ ((k >> 1) & 1), c ^ (k & 1)
        src = srcs_for(4 * px + 2 * py + pc)
        for a in range(n_arr):
            sends.append(pltpu.make_async_remote_copy(
                src_ref=src[a], dst_ref=dsts[a],
                send_sem=send_sems.at[(k - 1) * n_arr + a], recv_sem=recv_sems.at[(k - 1) * n_arr + a],
                device_id=(px, py, pc), device_id_type=MESH))
    return mine, sends


class _TwoLevelGather:
    def __init__(self, srcs, slots, send_sems, recv_sems, loc_sems):
        self.srcs, self.slots, self.n_arr = srcs, slots, len(srcs)
        self.send_sems, self.recv_sems, self.loc_sems = send_sems, recv_sems, loc_sems
        x, y, c = _mesh_pos()
        self.c = c
        self.me, self.sib = (x, y, c), (x, y, 1 - c)
        self.chips = [(1 - x, y), (x, 1 - y), (1 - x, 1 - y)]

    def _copies(self, k, block, to, from_src=False):
        dev = 4 * block[0] + 2 * block[1] + block[2]
        return [pltpu.make_async_remote_copy(
            src_ref=self.srcs[a] if from_src else self.slots[a](dev), dst_ref=self.slots[a](dev),
            send_sem=self.send_sems.at[k * self.n_arr + a], recv_sem=self.recv_sems.at[k * self.n_arr + a],
            device_id=to, device_id_type=MESH) for a in range(self.n_arr)]

    def _local(self):
        dev = 4 * self.me[0] + 2 * self.me[1] + self.me[2]
        return [pltpu.make_async_copy(self.srcs[a], self.slots[a](dev), self.loc_sems.at[a])
                for a in range(self.n_arr)]

    def start(self, chips=(0, 1, 2)):
        for cp in self._local() + self._copies(0, self.me, self.sib, True):
            cp.start()
        self.start_to(chips)

    def start_to(self, chips):
        for j in chips:
            for cp in self._copies(1 + j, self.me, (*self.chips[j], self.c), True):
                cp.start()

    def wait_own(self):
        for cp in self._local():
            cp.wait()

    def wait_sibling(self):
        for cp in self._copies(0, self.sib, self.me):
            cp.wait_recv()

    def wait_and_pass_on(self, j):
        chip = self.chips[j]
        for cp in self._copies(1 + j, (*chip, self.c), self.me):
            cp.wait_recv()
        for cp in self._copies(4 + j, (*chip, self.c), self.sib):
            cp.start()

    def wait_passed_on(self, j):
        for cp in self._copies(4 + j, (*self.chips[j], 1 - self.c), self.me):
            cp.wait_recv()

    def wait_sends(self):
        for cp in self._copies(0, self.me, self.sib, True):
            cp.wait_send()
        for j, chip in enumerate(self.chips):
            for cp in self._copies(1 + j, self.me, (*chip, self.c), True) + self._copies(4 + j, (*chip, self.c), self.sib):
                cp.wait_send()

    def forward(self):
        for j in range(3):
            self.wait_and_pass_on(j)

    def finish(self):
        self.wait_sibling()
        for j in range(3):
            self.wait_passed_on(j)
        self.wait_sends()
        self.wait_own()


def _disc(a_re, a_im, log_dt, b_re, b_im):
    dt = jnp.exp(log_dt)
    mag = jnp.exp(a_re * dt)
    ab_re = mag * jnp.cos(a_im * dt)
    ab_im = mag * jnp.sin(a_im * dt)
    den = a_re * a_re + a_im * a_im
    p_re = ab_re - 1.0
    p_im = ab_im
    q_re = (p_re * a_re + p_im * a_im) / den
    q_im = (p_im * a_re - p_re * a_im) / den
    bb_re = q_re * b_re - q_im * b_im
    bb_im = q_re * b_im + q_im * b_re
    return ab_re, ab_im, bb_re, bb_im


def _ssm_disc(a_re_r, a_im_r, log_dt, b_re, b_im):
    def body(are, aim, ldt, bre, bim, o_abre, o_abim, o_bbre, o_bbim):
        outs = _disc(are[...], aim[...], ldt[...], bre[...], bim[...])
        for o, v in zip((o_abre, o_abim, o_bbre, o_bbim), outs):
            o[...] = v

    shp = jax.ShapeDtypeStruct(a_re_r.shape, F32)
    return _pcall(body, name="ssm_disc", out_shape=(shp,) * 4)(a_re_r, a_im_r, log_dt, b_re, b_im)


def _ssm_disc_bwd(a_re_r, a_im_r, log_dt, b_re, b_im, g_abre, g_abim, g_bbre, g_bbim):
    width = a_re_r.shape[1]

    def body(are, aim, ldt, bre, bim, gabre, gabim, gbbre, gbbim, o_are, o_aim, o_ldt, o_bre, o_bim):
        _, vjp = jax.vjp(_disc, are[...], aim[...], ldt[...], bre[...], bim[...])
        d_are, d_aim, d_ldt, d_bre, d_bim = vjp((gabre[...], gabim[...], gbbre[...], gbbim[...]))

        def group_sum(v):
            for k in (1, 2, 4, 8):
                v = v + pltpu.roll(v, width - k, 1)
            return v

        o_are[...] = group_sum(d_are)
        o_aim[...] = group_sum(d_aim)
        o_ldt[...] = d_ldt
        o_bre[...] = d_bre
        o_bim[...] = d_bim

    shp = jax.ShapeDtypeStruct(a_re_r.shape, F32)
    return _pcall(body, name="ssm_disc_bwd",
                  out_shape=(shp, shp, jax.ShapeDtypeStruct(log_dt.shape, F32), shp, shp),
                  )(a_re_r, a_im_r, log_dt, b_re, b_im, g_abre, g_abim, g_bbre, g_bbim)


def _in_proj(order, x2, g1, w_in_b):
    n = x2.shape[0]
    tm = min(IN_TILE, n)
    n_tiles = n // tm

    def body(order_ref, x_ref, g_ref, w_ref, xn_ref, proj_ref, wall_ref,
             xn_scr, wbuf, send_sems, recv_sems, loc_sems, out_sems):
        k = pl.program_id(0)
        i = pl.program_id(1)

        def slot(dev):
            return wbuf.at[dev // 2, :, pl.ds(pl.multiple_of((dev % 2) * COLS_PER_DEV, LANES), COLS_PER_DEV)]

        gather = _TwoLevelGather([w_ref], [slot], send_sems, recv_sems, loc_sems)

        @pl.when((k == 0) & (i == 0))
        def _():
            gather.start(chips=(0, 1))

        def own_chip():
            gather.wait_own()
            gather.wait_sibling()

        def other_chip(j):
            gather.wait_and_pass_on(j)
            if j == 0:
                gather.start_to((2,))
            gather.wait_passed_on(j)

        arrivals = [own_chip] + [functools.partial(other_chip, j) for j in range(3)]
        for kk, arrived in enumerate(arrivals):
            @pl.when((k == kk) & (i == 0))
            def _(arrived=arrived):
                arrived()

        rows = pl.ds(pl.multiple_of(i * tm, tm), tm)

        @pl.when(k == 0)
        def _():
            x = x_ref[...]
            r = lax.rsqrt(jnp.mean(x * x, axis=-1, keepdims=True) + EPS)
            xn = ((x * r) * g_ref[...]).astype(BF16)
            xn_scr[rows, :] = xn
            xn_ref[...] = xn

        proj_ref[...] = _dot(xn_scr[rows, :], wbuf[order_ref[k]])

        @pl.when((k == N_CHIP - 1) & (i == n_tiles - 1))
        def _():
            gather.wait_sends()
            outs = [pltpu.make_async_copy(wbuf.at[q], wall_ref.at[:, q * COLS_PER_CHIP:(q + 1) * COLS_PER_CHIP],
                                          out_sems.at[q]) for q in range(N_CHIP)]
            for cp in outs:
                cp.start()
            for cp in outs:
                cp.wait()

    tile_once = lambda k, i, order: (jnp.where(k == 0, i, n_tiles - 1), 0)
    grid_spec = pltpu.PrefetchScalarGridSpec(
        num_scalar_prefetch=1, grid=(N_CHIP, n_tiles),
        in_specs=[pl.BlockSpec((tm, D_MODEL), tile_once),
                  pl.BlockSpec((1, D_MODEL), lambda k, i, order: (0, 0)),
                  HBM_SPEC],
        out_specs=(pl.BlockSpec((tm, D_MODEL), tile_once),
                   pl.BlockSpec((tm, COLS_PER_CHIP), lambda k, i, order: (i, order[k])),
                   HBM_SPEC),
        scratch_shapes=[pltpu.VMEM((n, D_MODEL), BF16), pltpu.VMEM((N_CHIP, D_MODEL, COLS_PER_CHIP), BF16),
                        pltpu.SemaphoreType.DMA((7,)), pltpu.SemaphoreType.DMA((7,)), pltpu.SemaphoreType.DMA((1,)),
                        pltpu.SemaphoreType.DMA((N_CHIP,))])
    return _pcall(
        body, name="in_proj", grid_spec=grid_spec,
        out_shape=(jax.ShapeDtypeStruct((n, D_MODEL), BF16), jax.ShapeDtypeStruct((n, IN_COLS), F32),
                   jax.ShapeDtypeStruct((D_MODEL, IN_COLS), BF16)),
        compiler_params=_params(2),
    )(order, x2, g1, w_in_b)


def _cmul(p, q):
    return p[0] * q[0] - p[1] * q[1], p[0] * q[1] + p[1] * q[0]


def _scan_tables(ar, ai, width, reverse):
    pows = [(ar, ai)]
    for _ in range(SUBLANES - 1):
        pows.append(_cmul(pows[-1], (ar, ai)))
    row = lax.broadcasted_iota(jnp.int32, (SUBLANES, width), 0)

    def bc(v):
        return jnp.broadcast_to(v, (SUBLANES, width))

    levels = []
    for k in (1, 2, 4):
        keep = (row <= SUBLANES - 1 - k) if reverse else (row >= k)
        levels.append((jnp.where(keep, bc(pows[k - 1][0]), 0.0), jnp.where(keep, bc(pows[k - 1][1]), 0.0)))
    cre = jnp.zeros((SUBLANES, width), F32)
    cim = jnp.zeros((SUBLANES, width), F32)
    for r in range(SUBLANES):
        e = (SUBLANES - r) if reverse else (r + 1)
        cre = jnp.where(row == r, bc(pows[e - 1][0]), cre)
        cim = jnp.where(row == r, bc(pows[e - 1][1]), cim)
    return levels, (cre, cim)


def _load_chunked(src_ref, b, dst_ref, n_rows):
    n_blk = n_rows // SUBLANES
    for i in range(n_blk):
        dst_ref[b, i * SUBLANES:(i + 1) * SUBLANES, :] = src_ref[b, pl.ds(i, SUBLANES, stride=n_blk), :]


def _store_chunked(val, dst_ref, b, n_rows):
    n_blk = n_rows // SUBLANES
    for i in range(n_blk):
        dst_ref[b, pl.ds(i, SUBLANES, stride=n_blk), :] = val[i * SUBLANES:(i + 1) * SUBLANES, :]


def _chunk_scan(re_ref, im_ref, b, car_ref, ar, ai, n_rows, reverse, on_block=None):
    width = re_ref.shape[2]
    n_blk = n_rows // SUBLANES
    shape = (SUBLANES, width)
    abr = jnp.broadcast_to(ar, shape)
    abi = jnp.broadcast_to(ai, shape)
    order = list(range(n_blk - 1, -1, -1)) if reverse else list(range(n_blk))

    def blk(ref, i):
        return ref[b, i * SUBLANES:(i + 1) * SUBLANES, :]

    def step(sr, si, i):
        return abr * sr - abi * si + blk(re_ref, i), abr * si + abi * sr + blk(im_ref, i)

    fr, fi = blk(re_ref, order[0]), blk(im_ref, order[0])
    for i in order[1:]:
        fr, fi = step(fr, fi, i)

    mr, mi = ar, ai
    for _ in range(n_blk.bit_length() - 1):
        mr, mi = _cmul((mr, mi), (mr, mi))
    levels, _ = _scan_tables(mr, mi, width, reverse)
    row = lax.broadcasted_iota(jnp.int32, shape, 0)
    edge_in = SUBLANES - 1 if reverse else 0
    sh1 = SUBLANES - 1 if reverse else 1
    gr = jnp.where(row == edge_in, jnp.broadcast_to(car_ref[b, 0:1, :], shape), pltpu.roll(fr, sh1, 0))
    gi = jnp.where(row == edge_in, jnp.broadcast_to(car_ref[b, 1:2, :], shape), pltpu.roll(fi, sh1, 0))
    for (lr, li), k in zip(levels, (1, 2, 4)):
        sh = (SUBLANES - k) if reverse else k
        sr = pltpu.roll(gr, sh, 0)
        si = pltpu.roll(gi, sh, 0)
        gr, gi = gr + (lr * sr - li * si), gi + (lr * si + li * sr)
    mbr = jnp.broadcast_to(mr, shape)
    mbi = jnp.broadcast_to(mi, shape)
    edge_out = 0 if reverse else SUBLANES - 1
    car_ref[b, 0:1, :] = (fr + (mbr * gr - mbi * gi))[edge_out:edge_out + 1, :]
    car_ref[b, 1:2, :] = (fi + (mbr * gi + mbi * gr))[edge_out:edge_out + 1, :]

    sr, si = gr, gi
    for i in order:
        sr, si = step(sr, si, i)
        re_ref[b, i * SUBLANES:(i + 1) * SUBLANES, :] = sr
        im_ref[b, i * SUBLANES:(i + 1) * SUBLANES, :] = si
        if on_block is not None:
            on_block(i, sr, si)


def _ssm_fwd(u, bb_re, bb_im, c_re_t, c_imn_t, d_row, ab_re, ab_im, w_out_b, w_glu_b, conv_p, n_seq, seq):
    tt = SCAN_TILE
    nt = seq // tt

    def body(u_ref, bbre, bbim, cre, cimn, d_ref, are, aim, wout_ref, wglu_ref, cw_ref,
             sre_ref, sim_ref, y_ref, oout_ref, oglu_ref, ocw_ref,
             up_ref, car_ref, send_sems, recv_sems, loc_sems):
        j = pl.program_id(0)
        t = pl.program_id(1)
        gather = _TwoLevelGather(
            [wout_ref, wglu_ref, cw_ref],
            [lambda dev: oout_ref.at[pl.ds(pl.multiple_of(dev * OUT_ROWS_PER_DEV, OUT_ROWS_PER_DEV), OUT_ROWS_PER_DEV), :],
             lambda dev: oglu_ref.at[pl.ds(pl.multiple_of(dev * GLU_ROWS_PER_DEV, GLU_ROWS_PER_DEV), GLU_ROWS_PER_DEV), :],
             lambda dev: ocw_ref.at[dev]],
            send_sems, recv_sems, loc_sems)

        @pl.when((j == 0) & (t == 0))
        def _():
            gather.start()

        @pl.when((j == N_JBLK // 2) & (t == 0))
        def _():
            gather.forward()

        @pl.when(t == 0)
        def _():
            car_ref[...] = jnp.zeros_like(car_ref)

        for b in range(n_seq):
            _load_chunked(u_ref, b, up_ref, tt)
            up = up_ref[b]
            ub = up.astype(BF16)
            sre_ref[b] = _dot(ub, bbre[0])
            sim_ref[b] = _dot(ub, bbim[0])
            _chunk_scan(sre_ref, sim_ref, b, car_ref, are[...], aim[...], tt, reverse=False)
            yp = (_dot(sre_ref[b].astype(BF16), cre[0]) + _dot(sim_ref[b].astype(BF16), cimn[0])
                  + d_ref[...] * up)
            _store_chunked(yp, y_ref, b, tt)

        @pl.when((j == N_JBLK - 1) & (t == nt - 1))
        def _():
            gather.finish()

    tok = lambda j, t: (0, t, j)
    blk3 = lambda j, t: (j, 0, 0)
    row = lambda j, t: (0, j)
    st = jax.ShapeDtypeStruct((n_seq, seq, N_JBLK * JB_ST), F32)
    n_arr = 3
    return _pcall(
        body, name="ssm_fwd", grid=(N_JBLK, nt),
        out_shape=(st, st, jax.ShapeDtypeStruct((n_seq, seq, SSM_W), F32),
                   jax.ShapeDtypeStruct((D_MODEL, D_MODEL), BF16), jax.ShapeDtypeStruct((SSM_W, SSM_W), BF16),
                   jax.ShapeDtypeStruct((N_DEV, SUBLANES, LANES), F32)),
        in_specs=[pl.BlockSpec((n_seq, tt, JB_CH), tok),
                  pl.BlockSpec((1, JB_CH, JB_ST), blk3), pl.BlockSpec((1, JB_CH, JB_ST), blk3),
                  pl.BlockSpec((1, JB_ST, JB_CH), blk3), pl.BlockSpec((1, JB_ST, JB_CH), blk3),
                  pl.BlockSpec((1, JB_CH), row), pl.BlockSpec((1, JB_ST), row), pl.BlockSpec((1, JB_ST), row),
                  HBM_SPEC, HBM_SPEC, HBM_SPEC],
        out_specs=(pl.BlockSpec((n_seq, tt, JB_ST), tok), pl.BlockSpec((n_seq, tt, JB_ST), tok),
                   pl.BlockSpec((n_seq, tt, JB_CH), tok), HBM_SPEC, HBM_SPEC, HBM_SPEC),
        scratch_shapes=[pltpu.VMEM((n_seq, tt, JB_CH), F32), pltpu.VMEM((n_seq, SUBLANES, JB_ST), F32),
                        pltpu.SemaphoreType.DMA((7 * n_arr,)), pltpu.SemaphoreType.DMA((7 * n_arr,)),
                        pltpu.SemaphoreType.DMA((n_arr,))],
        compiler_params=_params(2),
    )(u, bb_re, bb_im, c_re_t, c_imn_t, d_row, ab_re, ab_im, w_out_b, w_glu_b, conv_p)


def _ssm_bwd(dy, u, s_re, s_im, bb_re, bb_im, c_re_t, c_imn_t, d_row, ab_re, ab_im, g_out, g_glu, n_seq, seq):
    tt = SCAN_TILE
    nt = seq // tt
    rows8 = tt // SUBLANES

    def body(dy_ref, u_ref, sre_ref, sim_ref, pre_ref, pim_ref, bbre, bbim, cre, cimn, d_ref, are, aim,
             gout_ref, gglu_ref,
             du_ref, dcre_ref, dcim_ref, dbbre_ref, dbbim_ref, dare_ref, daim_ref, dd_ref, rout_ref, rglu_ref,
             lre_ref, lim_ref, dyp_ref, up_ref, car_ref, send_sems, recv_sems, loc_sems):
        j = pl.program_id(0)
        tr = pl.program_id(1)

        def exchange():
            return _direct_copies(lambda pid: [gout_ref.at[pid], gglu_ref.at[pid]], [rout_ref, rglu_ref],
                                  send_sems, recv_sems, loc_sems)

        @pl.when((j == 0) & (tr == 0))
        def _():
            mine, sends = exchange()
            for cp in mine + sends:
                cp.start()

        @pl.when(tr == 0)
        def _():
            car_ref[...] = jnp.zeros_like(car_ref)
            for r in (dcre_ref, dcim_ref, dbbre_ref, dbbim_ref, dare_ref, daim_ref, dd_ref):
                r[...] = jnp.zeros_like(r)

        first = tr == nt - 1
        row = lax.broadcasted_iota(jnp.int32, (SUBLANES, JB_ST), 0)
        n_blk = tt // SUBLANES
        for b in range(n_seq):
            _load_chunked(dy_ref, b, dyp_ref, tt)
            _load_chunked(u_ref, b, up_ref, tt)
            dyp = dyp_ref[b]
            up = up_ref[b]
            dyb = dyp.astype(BF16)
            ub = up.astype(BF16)
            lre_ref[b] = _dot_nt(dyb, cre[0])
            lim_ref[b] = _dot_nt(dyb, cimn[0])
            acc = [jnp.zeros((SUBLANES, JB_ST), F32), jnp.zeros((SUBLANES, JB_ST), F32)]

            def on_block(i, lr, li, b=b, acc=acc):
                if i > 0:
                    spr = sre_ref[b, (i - 1) * SUBLANES:i * SUBLANES, :]
                    spi = sim_ref[b, (i - 1) * SUBLANES:i * SUBLANES, :]
                else:
                    hr = jnp.where(first, 0.0, pre_ref[b, SUBLANES - 1:SUBLANES, :])
                    hi = jnp.where(first, 0.0, pim_ref[b, SUBLANES - 1:SUBLANES, :])
                    last_r = sre_ref[b, (n_blk - 1) * SUBLANES:n_blk * SUBLANES, :]
                    last_i = sim_ref[b, (n_blk - 1) * SUBLANES:n_blk * SUBLANES, :]
                    spr = jnp.where(row == 0, jnp.broadcast_to(hr, row.shape), pltpu.roll(last_r, 1, 0))
                    spi = jnp.where(row == 0, jnp.broadcast_to(hi, row.shape), pltpu.roll(last_i, 1, 0))
                acc[0] = acc[0] + (lr * spr + li * spi)
                acc[1] = acc[1] + (li * spr - lr * spi)

            _chunk_scan(lre_ref, lim_ref, b, car_ref, are[...], -aim[...], tt, reverse=True, on_block=on_block)
            dare_ref[...] += jnp.sum(acc[0], axis=0, keepdims=True)
            daim_ref[...] += jnp.sum(acc[1], axis=0, keepdims=True)
            lrb = lre_ref[b].astype(BF16)
            lib = lim_ref[b].astype(BF16)
            dup = d_ref[...] * dyp + _dot_nt(lrb, bbre[0]) + _dot_nt(lib, bbim[0])
            _store_chunked(dup, du_ref, b, tt)
            dbbre_ref[0] += _dot_tn(ub, lrb)
            dbbim_ref[0] += _dot_tn(ub, lib)
            dcre_ref[0] += _dot_tn(dyb, sre_ref[b].astype(BF16))
            dcim_ref[0] += _dot_tn(dyb, sim_ref[b].astype(BF16))
            dd_ref[...] += jnp.sum(dyp * up, axis=0, keepdims=True)

        @pl.when((j == N_JBLK - 1) & (tr == nt - 1))
        def _():
            mine, sends = exchange()
            for cp in sends + mine:
                cp.wait()

    tok = lambda j, t: (0, nt - 1 - t, j)
    halo = lambda j, t: (0, jnp.maximum((nt - 1 - t) * rows8 - 1, 0), j)
    blk3 = lambda j, t: (j, 0, 0)
    row1 = lambda j, t: (0, j)
    acc_shape = jax.ShapeDtypeStruct((N_JBLK, JB_CH, JB_ST), F32)
    return _pcall(
        body, name="ssm_bwd", grid=(N_JBLK, nt),
        out_shape=(jax.ShapeDtypeStruct((n_seq, seq, SSM_W), F32), acc_shape, acc_shape, acc_shape, acc_shape,
                   jax.ShapeDtypeStruct((1, N_JBLK * JB_ST), F32), jax.ShapeDtypeStruct((1, N_JBLK * JB_ST), F32),
                   jax.ShapeDtypeStruct((1, SSM_W), F32),
                   jax.ShapeDtypeStruct((N_DEV,) + g_out.shape[1:], F32),
                   jax.ShapeDtypeStruct((N_DEV,) + g_glu.shape[1:], F32)),
        in_specs=[pl.BlockSpec((n_seq, tt, JB_CH), tok), pl.BlockSpec((n_seq, tt, JB_CH), tok),
                  pl.BlockSpec((n_seq, tt, JB_ST), tok), pl.BlockSpec((n_seq, tt, JB_ST), tok),
                  pl.BlockSpec((n_seq, SUBLANES, JB_ST), halo), pl.BlockSpec((n_seq, SUBLANES, JB_ST), halo),
                  pl.BlockSpec((1, JB_CH, JB_ST), blk3), pl.BlockSpec((1, JB_CH, JB_ST), blk3),
                  pl.BlockSpec((1, JB_ST, JB_CH), blk3), pl.BlockSpec((1, JB_ST, JB_CH), blk3),
                  pl.BlockSpec((1, JB_CH), row1), pl.BlockSpec((1, JB_ST), row1), pl.BlockSpec((1, JB_ST), row1),
                  HBM_SPEC, HBM_SPEC],
        out_specs=(pl.BlockSpec((n_seq, tt, JB_CH), tok),
                   pl.BlockSpec((1, JB_CH, JB_ST), blk3), pl.BlockSpec((1, JB_CH, JB_ST), blk3),
                   pl.BlockSpec((1, JB_CH, JB_ST), blk3), pl.BlockSpec((1, JB_CH, JB_ST), blk3),
                   pl.BlockSpec((1, JB_ST), row1), pl.BlockSpec((1, JB_ST), row1), pl.BlockSpec((1, JB_CH), row1),
                   HBM_SPEC, HBM_SPEC),
        scratch_shapes=[pltpu.VMEM((n_seq, tt, JB_ST), F32), pltpu.VMEM((n_seq, tt, JB_ST), F32),
                        pltpu.VMEM((n_seq, tt, JB_CH), F32), pltpu.VMEM((n_seq, tt, JB_CH), F32),
                        pltpu.VMEM((n_seq, SUBLANES, JB_ST), F32),
                        pltpu.SemaphoreType.DMA((7 * 2,)), pltpu.SemaphoreType.DMA((7 * 2,)),
                        pltpu.SemaphoreType.DMA((2,))],
        compiler_params=_params(2),
    )(dy, u, s_re, s_im, s_re, s_im, bb_re, bb_im, c_re_t, c_imn_t, d_row, ab_re, ab_im, g_out, g_glu)


def _mix(x2, tgt2, y, proj, gf, b_glu, conv8, w_glu_f, w_out_f, seq):
    n = x2.shape[0]
    tm = TOK_TILE
    tiles_per_seq = seq // tm
    rows8 = tm // SUBLANES

    def body(x_ref, t_ref, y_ref, zs_ref, h_ref, bc_ref, cc_ref, zc_ref, hp_ref, ccp_ref,
             gf_ref, bg_ref, cw_ref, wg_ref, wo_ref,
             dh2_ref, dy_ref, dzs_ref, dbc_ref, dzc_ref, dyc_ref,
             dwo_ref, dwg_ref, loss_ref, dgf_ref, dbg_ref, dcw_ref):
        i = pl.program_id(0)

        @pl.when(i == 0)
        def _():
            for r in (dwo_ref, dwg_ref, loss_ref, dgf_ref, dbg_ref, dcw_ref):
                r[...] = jnp.zeros_like(r)

        yv = y_ref[...]
        y1, dgelu = _gelu_and_grad(yv)
        y1b = y1.astype(BF16)
        gate = _sigmoid(_dot(y1b, wg_ref[...]) + bg_ref[...])
        y2 = y1 * gate
        szs, dszs = _silu_and_grad(zs_ref[...])
        yssm = y2 * szs
        hv = h_ref[...]
        ccv = cc_ref[...]
        bcv = bc_ref[...]
        v = ccv * hv
        first = (i % tiles_per_seq) == 0
        vhalo = jnp.where(first, 0.0, ccp_ref[...] * hp_ref[...])
        v1 = _shift_down(v, vhalo, 1)
        v2 = _shift_down(v, vhalo, 2)
        w0 = cw_ref[0:1, :]
        w1 = cw_ref[1:2, :]
        w2 = cw_ref[2:3, :]
        yc = w0 * v2 + w1 * v1 + w2 * v
        szc, dszc = _silu_and_grad(zc_ref[...])
        yconv = (bcv * yc) * szc
        ysb = yssm.astype(BF16)
        ycb = yconv.astype(BF16)
        h2 = x_ref[...] + _dot(ysb, wo_ref[0:SSM_W, :]) + _dot(ycb, wo_ref[SSM_W:, :])
        r2 = lax.rsqrt(jnp.mean(h2 * h2, axis=-1, keepdims=True) + EPS)
        hn = h2 * r2
        gfv = gf_ref[...]
        err = hn * gfv - t_ref[...]
        loss_ref[...] += 0.5 * jnp.sum(jnp.mean(err * err, axis=-1, keepdims=True))
        dout = err * (1.0 / D_MODEL)
        dgf_ref[...] += jnp.sum(dout * hn, axis=0, keepdims=True)
        dn = dout * gfv
        dh2 = r2 * (dn - hn * jnp.mean(dn * hn, axis=-1, keepdims=True))
        dh2_ref[...] = dh2
        dh2b = dh2.astype(BF16)
        dwo_ref[0:SSM_W, :] += _dot_tn(ysb, dh2b)
        dwo_ref[SSM_W:, :] += _dot_tn(ycb, dh2b)
        dyssm = _dot_nt(dh2b, wo_ref[0:SSM_W, :])
        dyconv = _dot_nt(dh2b, wo_ref[SSM_W:, :])
        dy2 = dyssm * szs
        dzs_ref[...] = dyssm * y2 * dszs
        dgp = dy2 * y1 * (gate * (1.0 - gate))
        dgpb = dgp.astype(BF16)
        dy1 = dy2 * gate + _dot_nt(dgpb, wg_ref[...])
        dwg_ref[...] += _dot_tn(y1b, dgpb)
        dbg_ref[...] += jnp.sum(dgp, axis=0, keepdims=True)
        dy_ref[...] = dy1 * dgelu
        dbc_ref[...] = dyconv * yc * szc
        dyc = dyconv * bcv * szc
        dyc_ref[...] = dyc
        dzc_ref[...] = dyconv * bcv * yc * dszc
        dcw_ref[0:1, :] += jnp.sum(dyc * v2, axis=0, keepdims=True)
        dcw_ref[1:2, :] += jnp.sum(dyc * v1, axis=0, keepdims=True)
        dcw_ref[2:3, :] += jnp.sum(dyc * v, axis=0, keepdims=True)

    tile_d = pl.BlockSpec((tm, D_MODEL), lambda i: (i, 0))
    tile_s = pl.BlockSpec((tm, SSM_W), lambda i: (i, 0))
    seg_of = lambda c: pl.BlockSpec((tm, SSM_W), lambda i: (i, c))
    halo_of = lambda c: pl.BlockSpec((SUBLANES, SSM_W), lambda i: (jnp.maximum(i * rows8 - 1, 0), c))
    const = lambda shape: pl.BlockSpec(shape, lambda i: (0,) * len(shape))
    seg = jax.ShapeDtypeStruct((n, SSM_W), F32)
    return _pcall(
        body, name="mix", grid=(n // tm,),
        out_shape=(jax.ShapeDtypeStruct((n, D_MODEL), F32), seg, seg, seg, seg, seg,
                   jax.ShapeDtypeStruct((D_MODEL, D_MODEL), F32), jax.ShapeDtypeStruct((SSM_W, SSM_W), F32),
                   jax.ShapeDtypeStruct((SUBLANES, LANES), F32), jax.ShapeDtypeStruct((1, D_MODEL), F32),
                   jax.ShapeDtypeStruct((1, SSM_W), F32), jax.ShapeDtypeStruct((SUBLANES, CONV_W), F32)),
        in_specs=[tile_d, tile_d, tile_s, seg_of(SEG_ZS), seg_of(SEG_H), seg_of(SEG_BC), seg_of(SEG_CC), seg_of(SEG_ZC),
                  halo_of(SEG_H), halo_of(SEG_CC),
                  const((1, D_MODEL)), const((1, SSM_W)), const((SUBLANES, CONV_W)),
                  const((SSM_W, SSM_W)), const((D_MODEL, D_MODEL))],
        out_specs=(tile_d, tile_s, tile_s, tile_s, tile_s, tile_s,
                   const((D_MODEL, D_MODEL)), const((SSM_W, SSM_W)), const((SUBLANES, LANES)),
                   const((1, D_MODEL)), const((1, SSM_W)), const((SUBLANES, CONV_W))),
        compiler_params=_params(1),
    )(x2, tgt2, y, proj, proj, proj, proj, proj, proj, proj, gf, b_glu, conv8, w_glu_f, w_out_f)


def _in_bwd(x2, dh2, du, dzs, dyc, proj, dbc, dzc, g1, conv8, w_full, seq):
    n = x2.shape[0]
    tm = TOK_TILE
    n_tiles = n // tm
    tiles_per_seq = seq // tm
    rows8 = tm // SUBLANES
    n_blk8 = n // SUBLANES

    def body(x_ref, dh2_ref, du_ref, dzs_ref, dyc_ref, dycn_ref, h_ref, cc_ref, dbc_ref, dzc_ref,
             g_ref, cw_ref, w_ref, gx_ref, dp_ref, dg_ref):
        i = pl.program_id(0)

        @pl.when(i == 0)
        def _():
            dg_ref[...] = jnp.zeros_like(dg_ref)

        dyc = dyc_ref[...]
        last = (i % tiles_per_seq) == tiles_per_seq - 1
        nhalo = jnp.where(last, 0.0, dycn_ref[...])
        dv = (cw_ref[2:3, :] * dyc + cw_ref[1:2, :] * _shift_up(dyc, nhalo, 1)
              + cw_ref[0:1, :] * _shift_up(dyc, nhalo, 2))
        parts = (du_ref[...], dzs_ref[...], dv * cc_ref[...], dbc_ref[...], dv * h_ref[...], dzc_ref[...])
        dxn = jnp.zeros((tm, D_MODEL), F32)
        for k, p in enumerate(parts):
            pb = p.astype(BF16)
            dp_ref[:, k * SSM_W:(k + 1) * SSM_W] = pb
            dxn = dxn + _dot_nt(pb, w_ref[:, k * SSM_W:(k + 1) * SSM_W])
        x = x_ref[...]
        r = lax.rsqrt(jnp.mean(x * x, axis=-1, keepdims=True) + EPS)
        xh = x * r
        dg_ref[...] += jnp.sum(dxn * xh, axis=0, keepdims=True)
        dn = dxn * g_ref[...]
        gx_ref[...] = dh2_ref[...] + r * (dn - xh * jnp.mean(dn * xh, axis=-1, keepdims=True))

    tile_d = pl.BlockSpec((tm, D_MODEL), lambda i: (i, 0))
    tile_s = pl.BlockSpec((tm, SSM_W), lambda i: (i, 0))
    seg_of = lambda c: pl.BlockSpec((tm, SSM_W), lambda i: (i, c))
    nhalo = pl.BlockSpec((SUBLANES, SSM_W), lambda i: (jnp.minimum((i + 1) * rows8, n_blk8 - 1), 0))
    const = lambda shape: pl.BlockSpec(shape, lambda i: (0,) * len(shape))
    return _pcall(
        body, name="in_bwd", grid=(n_tiles,),
        out_shape=(jax.ShapeDtypeStruct((n, D_MODEL), F32), jax.ShapeDtypeStruct((n, IN_COLS), BF16),
                   jax.ShapeDtypeStruct((SUBLANES, D_MODEL), F32)),
        in_specs=[tile_d, tile_d, tile_s, tile_s, tile_s, nhalo, seg_of(SEG_H), seg_of(SEG_CC), tile_s, tile_s,
                  const((1, D_MODEL)), const((SUBLANES, CONV_W)), const((D_MODEL, IN_COLS))],
        out_specs=(tile_d, pl.BlockSpec((tm, IN_COLS), lambda i: (i, 0)), const((SUBLANES, D_MODEL))),
        compiler_params=_params(1),
    )(x2, dh2, du, dzs, dyc, dyc, proj, proj, dbc, dzc, g1, conv8, w_full)


def _dw_in_exchange(order, xn, dproj, small):
    n = xn.shape[0]
    tk = 512
    nk = n // tk
    piece = (D_MODEL, COLS_PER_DEV)

    def body(order_ref, xn_ref, dp_ref, sm_ref, own_ref, rchip_ref, rsm_ref,
             acc, stage, sbuf, give_send, give_recv, keep_send, keep_recv, sm_send, sm_recv, sm_loc):
        del order_ref
        s = pl.program_id(0)
        x, y, c = _mesh_pos()
        sib = (x, y, 1 - c)
        chips = [(1 - x, y), (x, 1 - y), (1 - x, 1 - y)]
        gather = _TwoLevelGather([sm_ref], [lambda dev: rsm_ref.at[dev]], sm_send, sm_recv, sm_loc)

        def half(i, core):
            return acc.at[i % 2, :, pl.ds(pl.multiple_of(core * COLS_PER_DEV, LANES), COLS_PER_DEV)]

        def give(i):
            return pltpu.make_async_remote_copy(src_ref=half(i, 1 - c), dst_ref=stage.at[i], send_sem=give_send.at[i],
                                                recv_sem=give_recv.at[i], device_id=sib, device_id_type=MESH)

        def keep(i):
            return pltpu.make_async_remote_copy(src_ref=sbuf.at[i], dst_ref=rchip_ref.at[i], send_sem=keep_send.at[i],
                                                recv_sem=keep_recv.at[i], device_id=(*chips[i], c), device_id_type=MESH)

        def chip_sum(i):
            give(i).wait_recv()
            mine = [acc[i % 2, :, cc * COLS_PER_DEV:(cc + 1) * COLS_PER_DEV] for cc in range(2)]
            return jnp.where(c == 0, mine[0], mine[1]) + stage[i]

        @pl.when(s == 0)
        def _():
            gather.start()

        @pl.when(s == N_CHIP // 2)
        def _():
            gather.forward()

        for k in range(1, N_CHIP):
            @pl.when(s == k)
            def _(k=k):
                sbuf[k - 1] = chip_sum(k - 1).astype(BF16)
                keep(k - 1).start()
                if k >= 2:
                    give(k - 2).wait_send()

        slot = s % 2
        acc[slot] = _dot_tn(xn_ref[pl.ds(0, tk), :], dp_ref[pl.ds(0, tk), :])

        def kstep(kk, carry):
            off = pl.multiple_of(kk * tk, tk)
            acc[slot] += _dot_tn(xn_ref[pl.ds(off, tk), :], dp_ref[pl.ds(off, tk), :])
            return carry

        lax.fori_loop(1, nk, kstep, 0)

        for k in range(N_CHIP):
            @pl.when(s == k)
            def _(k=k):
                give(k).start()

        @pl.when(s == N_CHIP - 1)
        def _():
            own_ref[...] = chip_sum(N_CHIP - 1)
            give(N_CHIP - 2).wait_send()
            give(N_CHIP - 1).wait_send()
            for i in range(3):
                keep(i).wait()
            gather.finish()

    grid_spec = pltpu.PrefetchScalarGridSpec(
        num_scalar_prefetch=1, grid=(N_CHIP,),
        in_specs=[pl.BlockSpec(memory_space=pltpu.VMEM),
                  pl.BlockSpec((n, COLS_PER_CHIP), lambda s, order: (0, order[s])),
                  HBM_SPEC],
        out_specs=(pl.BlockSpec(piece, lambda s, order: (0, 0)), HBM_SPEC, HBM_SPEC),
        scratch_shapes=[pltpu.VMEM((2, D_MODEL, COLS_PER_CHIP), F32), pltpu.VMEM((4,) + piece, F32),
                        pltpu.VMEM((3,) + piece, BF16),
                        pltpu.SemaphoreType.DMA((4,)), pltpu.SemaphoreType.DMA((4,)),
                        pltpu.SemaphoreType.DMA((3,)), pltpu.SemaphoreType.DMA((3,)),
                        pltpu.SemaphoreType.DMA((7,)), pltpu.SemaphoreType.DMA((7,)), pltpu.SemaphoreType.DMA((1,))])
    return _pcall(
        body, name="dw_in_exchange", grid_spec=grid_spec,
        out_shape=(jax.ShapeDtypeStruct(piece, F32), jax.ShapeDtypeStruct((3,) + piece, BF16),
                   jax.ShapeDtypeStruct((N_DEV,) + small.shape, F32)),
        compiler_params=_params(1),
    )(order, xn, dproj, small)


def _adamw(g, w, m, v):
    m_new = ADAM_B1 * m + (1.0 - ADAM_B1) * g
    v_new = ADAM_B2 * v + (1.0 - ADAM_B2) * (g * g)
    m_hat = m_new / (1.0 - ADAM_B1 ** ADAM_STEP)
    v_hat = v_new / (1.0 - ADAM_B2 ** ADAM_STEP)
    delta = -ADAM_LR * (m_hat / (jnp.sqrt(v_hat) + ADAM_EPS) + ADAM_WD * w)
    return delta, m_new, v_new


def _reduce_adam(recv, w, m, v, name, row_tile):
    rows, cols = w.shape

    def body(r_ref, w_ref, m_ref, v_ref, g_ref, d_ref, nm_ref, nv_ref):
        g = r_ref[0]
        for s in range(1, N_DEV):
            g = g + r_ref[s]
        g_ref[...] = g
        d_ref[...], nm_ref[...], nv_ref[...] = _adamw(g, w_ref[...], m_ref[...], v_ref[...])

    tile = pl.BlockSpec((row_tile, cols), lambda i: (i, 0))
    shp = jax.ShapeDtypeStruct((rows, cols), F32)
    return _pcall(
        body, name=name, grid=(rows // row_tile,),
        out_shape=(shp,) * 4,
        in_specs=[pl.BlockSpec((N_DEV, row_tile, cols), lambda i: (0, i, 0)), tile, tile, tile],
        out_specs=(tile,) * 4,
        compiler_params=_params(1),
    )(recv, w, m, v)


def _reduce_adam_w_in(own, rchip, w, m, v):
    rows, cols = w.shape
    row_tile = 256

    def body(o_ref, r_ref, w_ref, m_ref, v_ref, g_ref, d_ref, nm_ref, nv_ref):
        g = o_ref[...]
        for s in range(3):
            g = g + r_ref[s].astype(F32)
        g_ref[...] = g
        d_ref[...], nm_ref[...], nv_ref[...] = _adamw(g, w_ref[...], m_ref[...], v_ref[...])

    tile = pl.BlockSpec((row_tile, cols), lambda i: (i, 0))
    shp = jax.ShapeDtypeStruct((rows, cols), F32)
    return _pcall(
        body, name="reduce_adam_w_in", grid=(rows // row_tile,),
        out_shape=(shp,) * 4,
        in_specs=[tile, pl.BlockSpec((3, row_tile, cols), lambda i: (0, i, 0)), tile, tile, tile],
        out_specs=(tile,) * 4,
        compiler_params=_params(1),
    )(own, rchip, w, m, v)


def _reduce_adam_stacked(recv, wmv, name):
    _, rows, cols = wmv.shape

    def body(r_ref, p_ref, o_ref):
        g = r_ref[0]
        for s in range(1, N_DEV):
            g = g + r_ref[s]
        o_ref[0] = g
        o_ref[1], o_ref[2], o_ref[3] = _adamw(g, p_ref[0], p_ref[1], p_ref[2])

    return _pcall(body, name=name, out_shape=jax.ShapeDtypeStruct((4, rows, cols), F32),
                  compiler_params=_params(0))(recv, wmv)


_SMALL = (("loss", 1), ("norm_gain", D_MODEL), ("final_norm_gain", D_MODEL), ("b_glu", SSM_W),
          ("ssm_a_re", N_GROUPS * STATE), ("ssm_a_im", N_GROUPS * STATE), ("ssm_log_dt", N_GROUPS),
          ("ssm_b_re", N_GROUPS * STATE * GROUP), ("ssm_b_im", N_GROUPS * STATE * GROUP),
          ("ssm_c_re", N_GROUPS * STATE * GROUP), ("ssm_c_im", N_GROUPS * STATE * GROUP),
          ("ssm_d", N_GROUPS * GROUP), ("conv_w", 3 * CONV_W))
_PACK_UNIT = SUBLANES * LANES


def _pack_small(dicts):
    cols = []
    for name, size in _SMALL:
        flat = jnp.stack([d[name].reshape(-1) for d in dicts])
        padded = -(-size // _PACK_UNIT) * _PACK_UNIT
        cols.append(jnp.pad(flat, ((0, 0), (0, padded - size))).reshape(len(dicts), -1, LANES))
    return jnp.concatenate(cols, axis=1)


def _unpack_small(packed):
    out, r0 = {}, 0
    for name, size in _SMALL:
        nrows = -(-size // _PACK_UNIT) * SUBLANES
        out[name] = packed[:, r0:r0 + nrows].reshape(packed.shape[0], -1)[:, :size]
        r0 += nrows
    return out


def _block_diag(m4):
    eye = jnp.eye(SUBLANES, dtype=m4.dtype)
    j, g, a, b = m4.shape
    return jnp.einsum("jgab,gk->jgakb", m4, eye).reshape(j, g * a, g * b)


def _block_diag_extract(dense, a, b):
    d5 = dense.reshape(N_JBLK, SUBLANES, a, SUBLANES, b)
    return jnp.stack([d5[:, g, :, g, :] for g in range(SUBLANES)], axis=1)


def kernel(x, norm_gain, w_in, ssm_a_re, ssm_a_im, ssm_log_dt, ssm_b_re, ssm_b_im, ssm_c_re, ssm_c_im, ssm_d, w_glu, b_glu, conv_w, w_out, final_norm_gain, loss_target, m_norm_gain, m_w_in, m_ssm_a_re, m_ssm_a_im, m_ssm_log_dt, m_ssm_b_re, m_ssm_b_im, m_ssm_c_re, m_ssm_c_im, m_ssm_d, m_w_glu, m_b_glu, m_conv_w, m_w_out, m_final_norm_gain, v_norm_gain, v_w_in, v_ssm_a_re, v_ssm_a_im, v_ssm_log_dt, v_ssm_b_re, v_ssm_b_im, v_ssm_c_re, v_ssm_c_im, v_ssm_d, v_w_glu, v_b_glu, v_conv_w, v_w_out, v_final_norm_gain):
    n_seq, seq, _ = x.shape
    n = n_seq * seq
    me = 4 * lax.axis_index("x") + 2 * lax.axis_index("y") + lax.axis_index("c")

    rep = lambda a: jnp.repeat(a[0], GROUP, axis=1)
    a_re_r, a_im_r = rep(ssm_a_re), rep(ssm_a_im)
    log_dt = ssm_log_dt[0].reshape(N_GROUPS, 1)
    b_re2 = ssm_b_re[0].reshape(N_GROUPS, STATE * GROUP)
    b_im2 = ssm_b_im[0].reshape(N_GROUPS, STATE * GROUP)
    ab_re_r, ab_im_r, bb_re2, bb_im2 = _ssm_disc(a_re_r, a_im_r, log_dt, b_re2, b_im2)
    ab_re = ab_re_r[:, ::GROUP].reshape(1, N_GROUPS * STATE)
    ab_im = ab_im_r[:, ::GROUP].reshape(1, N_GROUPS * STATE)

    def bb_mat(bb2):
        t = jnp.transpose(bb2.reshape(N_JBLK, SUBLANES, STATE, GROUP), (0, 1, 3, 2))
        return _block_diag(t).astype(BF16)

    def c_mat(c3, sign):
        t = jnp.transpose(c3.reshape(N_JBLK, SUBLANES, GROUP, STATE), (0, 1, 3, 2))
        return _block_diag(sign * t).astype(BF16)

    bb_re_m, bb_im_m = bb_mat(bb_re2), bb_mat(bb_im2)
    c_re_m, c_imn_m = c_mat(ssm_c_re[0], 1.0), c_mat(ssm_c_im[0], -1.0)
    d_row = ssm_d[0].reshape(1, SSM_W)

    x2 = x.reshape(n, D_MODEL)
    tgt2 = loss_target.reshape(n, D_MODEL)
    mx, my, mc = lax.axis_index("x"), lax.axis_index("y"), lax.axis_index("c")
    chip_ids = [2 * cx + cy for cx, cy in ((mx, my), (1 - mx, my), (mx, 1 - my), (1 - mx, 1 - my))]
    arrival = chip_ids
    xn, proj, w_in_f = _in_proj(jnp.stack(arrival).astype(jnp.int32), x2, norm_gain, w_in[0].astype(BF16))
    u3 = proj.reshape(n_seq, seq, IN_COLS)
    conv_p = jnp.pad(conv_w[0], ((0, SUBLANES - 3), (0, LANES - CONV_COLS_PER_DEV)))
    s_re, s_im, y3, w_out_f, w_glu_f, conv_all = _ssm_fwd(
        u3, bb_re_m, bb_im_m, c_re_m, c_imn_m, d_row, ab_re, ab_im,
        w_out[0].astype(BF16), w_glu[0].astype(BF16), conv_p, n_seq, seq)
    conv8 = jnp.transpose(conv_all[:, :, :CONV_COLS_PER_DEV], (1, 0, 2)).reshape(SUBLANES, CONV_W)
    (dh2, dy, dzs, dbc, dzc, dyc, dw_out, dw_glu, loss_t, dgf, dbg, dcw) = _mix(
        x2, tgt2, y3.reshape(n, SSM_W), proj, final_norm_gain.reshape(1, D_MODEL), b_glu, conv8,
        w_glu_f, w_out_f, seq)

    du3, dc_re_d, dc_im_d, dbb_re_d, dbb_im_d, dab_re, dab_im, dd, r_out, r_glu = _ssm_bwd(
        dy.reshape(n_seq, seq, SSM_W), u3, s_re, s_im, bb_re_m, bb_im_m, c_re_m, c_imn_m, d_row, ab_re, ab_im,
        dw_out.reshape(N_DEV, OUT_ROWS_PER_DEV, D_MODEL), dw_glu.reshape(N_DEV, GLU_ROWS_PER_DEV, SSM_W), n_seq, seq)
    du = du3.reshape(n, SSM_W)
    g_c_re = _block_diag_extract(dc_re_d, GROUP, STATE).reshape(N_GROUPS, GROUP, STATE)
    g_c_im = -_block_diag_extract(dc_im_d, GROUP, STATE).reshape(N_GROUPS, GROUP, STATE)

    def bb_grad(dense):
        t = _block_diag_extract(dense, GROUP, STATE)
        return jnp.transpose(t, (0, 1, 3, 2)).reshape(N_GROUPS, STATE * GROUP)

    def ab_grad(row):
        z = jnp.zeros((N_GROUPS, STATE, GROUP), F32)
        return z.at[:, :, 0].set(row.reshape(N_GROUPS, STATE)).reshape(N_GROUPS, STATE * GROUP)

    g_are_r, g_aim_r, g_ldt, g_bre2, g_bim2 = _ssm_disc_bwd(
        a_re_r, a_im_r, log_dt, b_re2, b_im2, ab_grad(dab_re), ab_grad(dab_im),
        bb_grad(dbb_re_d), bb_grad(dbb_im_d))
    grad_x2, dproj, dg8 = _in_bwd(x2, dh2, du, dzs, dyc, proj, dbc, dzc, norm_gain, conv8, w_in_f, seq)
    small_grads = {"loss": loss_t[0:1, 0:1], "norm_gain": dg8[0:1], "final_norm_gain": dgf, "b_glu": dbg,
                   "ssm_a_re": g_are_r[:, ::GROUP], "ssm_a_im": g_aim_r[:, ::GROUP], "ssm_log_dt": g_ldt,
                   "ssm_b_re": g_bre2, "ssm_b_im": g_bim2, "ssm_c_re": g_c_re, "ssm_c_im": g_c_im,
                   "ssm_d": dd, "conv_w": dcw[0:3]}

    order = chip_ids[1:] + chip_ids[:1]
    own_in, rchip_in, r_small = _dw_in_exchange(jnp.stack(order).astype(jnp.int32), xn, dproj,
                                                _pack_small([small_grads])[0])

    def conv_full(shard):
        return lax.dynamic_update_slice(jnp.zeros((3, CONV_W), F32), shard[0], (0, me * CONV_COLS_PER_DEV))

    zero1 = jnp.zeros((1,), F32)
    triples = dict(loss=(zero1, zero1, zero1), norm_gain=(norm_gain, m_norm_gain, v_norm_gain),
                   final_norm_gain=(final_norm_gain, m_final_norm_gain, v_final_norm_gain),
                   b_glu=(b_glu, m_b_glu, v_b_glu), ssm_a_re=(ssm_a_re, m_ssm_a_re, v_ssm_a_re),
                   ssm_a_im=(ssm_a_im, m_ssm_a_im, v_ssm_a_im), ssm_log_dt=(ssm_log_dt, m_ssm_log_dt, v_ssm_log_dt),
                   ssm_b_re=(ssm_b_re, m_ssm_b_re, v_ssm_b_re), ssm_b_im=(ssm_b_im, m_ssm_b_im, v_ssm_b_im),
                   ssm_c_re=(ssm_c_re, m_ssm_c_re, v_ssm_c_re), ssm_c_im=(ssm_c_im, m_ssm_c_im, v_ssm_c_im),
                   ssm_d=(ssm_d, m_ssm_d, v_ssm_d),
                   conv_w=(conv_full(conv_w), conv_full(m_conv_w), conv_full(v_conv_w)))
    wmv_small = _pack_small([{k: t[i] for k, t in triples.items()} for i in range(3)])

    res_in = _reduce_adam_w_in(own_in, rchip_in, w_in[0], m_w_in[0], v_w_in[0])
    res_out = _reduce_adam(r_out, w_out[0], m_w_out[0], v_w_out[0], "reduce_adam_w_out", OUT_ROWS_PER_DEV)
    res_glu = _reduce_adam(r_glu, w_glu[0], m_w_glu[0], v_w_glu[0], "reduce_adam_w_glu", GLU_ROWS_PER_DEV)
    small = _unpack_small(_reduce_adam_stacked(r_small, wmv_small, "reduce_adam_small"))
    loss = small["loss"][0, 0]

    shapes = dict(norm_gain=(1, D_MODEL), ssm_a_re=(1, N_GROUPS, STATE), ssm_a_im=(1, N_GROUPS, STATE),
                  ssm_log_dt=(1, N_GROUPS), ssm_b_re=(1, N_GROUPS, STATE, GROUP), ssm_b_im=(1, N_GROUPS, STATE, GROUP),
                  ssm_c_re=(1, N_GROUPS, GROUP, STATE), ssm_c_im=(1, N_GROUPS, GROUP, STATE),
                  ssm_d=(1, N_GROUPS, GROUP), b_glu=(1, SSM_W), final_norm_gain=(D_MODEL,))
    big = dict(w_in=res_in, w_glu=res_glu, w_out=res_out)
    small4 = {name: small[name].reshape((4,) + shp) for name, shp in shapes.items()}
    conv4 = lax.dynamic_slice(small["conv_w"].reshape(4, 1, 3, CONV_W), (0, 0, 0, me * CONV_COLS_PER_DEV),
                              (4, 1, 3, CONV_COLS_PER_DEV))

    def leaf(kind, name):
        if name in big:
            return big[name][kind][None]
        if name == "conv_w":
            return conv4[kind]
        return small4[name][kind]

    order = ["norm_gain", "w_in", "ssm_a_re", "ssm_a_im", "ssm_log_dt", "ssm_b_re", "ssm_b_im", "ssm_c_re",
             "ssm_c_im", "ssm_d", "w_glu", "b_glu", "conv_w", "w_out", "final_norm_gain"]
    outs = [loss, grad_x2.reshape(x.shape)]
    for kind in range(4):
        outs += [leaf(kind, name) for name in order]
    return tuple(outs)
```

```python
import functools
import math

import jax
import jax.numpy as jnp
from jax import lax
from jax.experimental import pallas as pl
from jax.experimental.pallas import tpu as pltpu

F32 = jnp.float32
BF16 = jnp.bfloat16

N_DEV = 8
D_MODEL = 1024
SSM_W = 512
CONV_W = 512
N_GROUPS = 32
GROUP = 16
STATE = 64
IN_COLS = 3072
SEG_U, SEG_ZS, SEG_H, SEG_BC, SEG_CC, SEG_ZC = range(6)
COLS_PER_DEV = IN_COLS // N_DEV
N_CHIP = N_DEV // 2
COLS_PER_CHIP = 2 * COLS_PER_DEV
OUT_ROWS_PER_DEV = D_MODEL // N_DEV
GLU_ROWS_PER_DEV = SSM_W // N_DEV
CONV_COLS_PER_DEV = CONV_W // N_DEV
EPS = 1e-6

N_JBLK = 4
JB_CH = SSM_W // N_JBLK
JB_ST = N_GROUPS * STATE // N_JBLK

ADAM_LR = 0.001
ADAM_B1 = 0.9
ADAM_B2 = 0.999
ADAM_EPS = 1e-08
ADAM_WD = 0.01
ADAM_STEP = 10

SUBLANES = 8
LANES = 128
VMEM_LIMIT = 48 * 1024 * 1024
TOK_TILE = 256
IN_TILE = 1024
SCAN_TILE = 256

MESH = pl.DeviceIdType.MESH
HBM_SPEC = pl.BlockSpec(memory_space=pltpu.HBM)


def _pcall(body, **kw):
    return pl.pallas_call(body, **kw)


def _params(n_grid):
    return pltpu.CompilerParams(dimension_semantics=("arbitrary",) * n_grid,
                                vmem_limit_bytes=VMEM_LIMIT)


def _dot(a, b):
    return jnp.dot(a, b, preferred_element_type=F32)


def _dot_nt(a, b):
    return lax.dot_general(a, b, (((1,), (1,)), ((), ())), preferred_element_type=F32)


def _dot_tn(a, b):
    return lax.dot_general(a, b, (((0,), (0,)), ((), ())), preferred_element_type=F32)


def _sigmoid(z):
    return 1.0 / (1.0 + jnp.exp(-z))


_GELU_C = math.sqrt(2.0 / math.pi)


def _gelu_and_grad(y):
    inner = _GELU_C * (y + 0.044715 * (y * y * y))
    t = jnp.tanh(inner)
    g = 0.5 * y * (1.0 + t)
    dg = 0.5 * (1.0 + t) + 0.5 * y * (1.0 - t * t) * (_GELU_C * (1.0 + 3.0 * 0.044715 * (y * y)))
    return g, dg


def _silu_and_grad(z):
    s = _sigmoid(z)
    return z * s, s * (1.0 + z * (1.0 - s))


def _shift_down(v, halo, k):
    rolled = pltpu.roll(v, k, 0)
    row = lax.broadcasted_iota(jnp.int32, v.shape, 0)
    for r in range(k):
        rolled = jnp.where(row == r, halo[SUBLANES - k + r:SUBLANES - k + r + 1, :], rolled)
    return rolled


def _shift_up(v, halo, k):
    n = v.shape[0]
    rolled = pltpu.roll(v, n - k, 0)
    row = lax.broadcasted_iota(jnp.int32, v.shape, 0)
    for r in range(k):
        rolled = jnp.where(row == n - k + r, halo[r:r + 1, :], rolled)
    return rolled


def _mesh_pos():
    return lax.axis_index("x"), lax.axis_index("y"), lax.axis_index("c")


def _direct_copies(srcs_for, out_refs, send_sems, recv_sems, loc_sems):
    x, y, c = _mesh_pos()
    me_id = 4 * x + 2 * y + c
    n_arr = len(out_refs)
    dsts = [r.at[me_id] for r in out_refs]
    own = srcs_for(me_id)
    mine = [pltpu.make_async_copy(own[a], dsts[a], loc_sems.at[a]) for a in range(n_arr)]
    sends = []
    for k in range(1, N_DEV):
        px, py, pc = x ^ ((k >> 2) & 1), y ^ ((k >> 1) & 1), c ^ (k & 1)
        src = srcs_for(4 * px + 2 * py + pc)
        for a in range(n_arr):
            sends.append(pltpu.make_async_remote_copy(
                src_ref=src[a], dst_ref=dsts[a],
                send_sem=send_sems.at[(k - 1) * n_arr + a], recv_sem=recv_sems.at[(k - 1) * n_arr + a],
                device_id=(px, py, pc), device_id_type=MESH))
    return mine, sends


class _TwoLevelGather:
    def __init__(self, srcs, slots, send_sems, recv_sems, loc_sems):
        self.srcs, self.slots, self.n_arr = srcs, slots, len(srcs)
        self.send_sems, self.recv_sems, self.loc_sems = send_sems, recv_sems, loc_sems
        x, y, c = _mesh_pos()
        self.c = c
        self.me, self.sib = (x, y, c), (x, y, 1 - c)
        self.chips = [(1 - x, y), (x, 1 - y), (1 - x, 1 - y)]

    def _copies(self, k, block, to, from_src=False):
        dev = 4 * block[0] + 2 * block[1] + block[2]
        return [pltpu.make_async_remote_copy(
            src_ref=self.srcs[a] if from_src else self.slots[a](dev), dst_ref=self.slots[a](dev),
            send_sem=self.send_sems.at[k * self.n_arr + a], recv_sem=self.recv_sems.at[k * self.n_arr + a],
            device_id=to, device_id_type=MESH) for a in range(self.n_arr)]

    def _local(self):
        dev = 4 * self.me[0] + 2 * self.me[1] + self.me[2]
        return [pltpu.make_async_copy(self.srcs[a], self.slots[a](dev), self.loc_sems.at[a])
                for a in range(self.n_arr)]

    def start(self, chips=(0, 1, 2)):
        for cp in self._local() + self._copies(0, self.me, self.sib, True):
            cp.start()
        self.start_to(chips)

    def start_to(self, chips):
        for j in chips:
            for cp in self._copies(1 + j, self.me, (*self.chips[j], self.c), True):
                cp.start()

    def wait_own(self):
        for cp in self._local():
            cp.wait()

    def wait_sibling(self):
        for cp in self._copies(0, self.sib, self.me):
            cp.wait_recv()

    def wait_and_pass_on(self, j):
        chip = self.chips[j]
        for cp in self._copies(1 + j, (*chip, self.c), self.me):
            cp.wait_recv()
        for cp in self._copies(4 + j, (*chip, self.c), self.sib):
            cp.start()

    def wait_passed_on(self, j):
        for cp in self._copies(4 + j, (*self.chips[j], 1 - self.c), self.me):
            cp.wait_recv()

    def wait_sends(self):
        for cp in self._copies(0, self.me, self.sib, True):
            cp.wait_send()
        for j, chip in enumerate(self.chips):
            for cp in self._copies(1 + j, self.me, (*chip, self.c), True) + self._copies(4 + j, (*chip, self.c), self.sib):
                cp.wait_send()

    def forward(self):
        for j in range(3):
            self.wait_and_pass_on(j)

    def finish(self):
        self.wait_sibling()
        for j in range(3):
            self.wait_passed_on(j)
        self.wait_sends()
        self.wait_own()


def _disc(a_re, a_im, log_dt, b_re, b_im):
    dt = jnp.exp(log_dt)
    mag = jnp.exp(a_re * dt)
    ab_re = mag * jnp.cos(a_im * dt)
    ab_im = mag * jnp.sin(a_im * dt)
    den = a_re * a_re + a_im * a_im
    p_re = ab_re - 1.0
    p_im = ab_im
    q_re = (p_re * a_re + p_im * a_im) / den
    q_im = (p_im * a_re - p_re * a_im) / den
    bb_re = q_re * b_re - q_im * b_im
    bb_im = q_re * b_im + q_im * b_re
    return ab_re, ab_im, bb_re, bb_im


def _ssm_disc(a_re_r, a_im_r, log_dt, b_re, b_im):
    def body(are, aim, ldt, bre, bim, o_abre, o_abim, o_bbre, o_bbim):
        outs = _disc(are[...], aim[...], ldt[...], bre[...], bim[...])
        for o, v in zip((o_abre, o_abim, o_bbre, o_bbim), outs):
            o[...] = v

    shp = jax.ShapeDtypeStruct(a_re_r.shape, F32)
    return _pcall(body, name="ssm_disc", out_shape=(shp,) * 4)(a_re_r, a_im_r, log_dt, b_re, b_im)


def _ssm_disc_bwd(a_re_r, a_im_r, log_dt, b_re, b_im, g_abre, g_abim, g_bbre, g_bbim):
    width = a_re_r.shape[1]

    def body(are, aim, ldt, bre, bim, gabre, gabim, gbbre, gbbim, o_are, o_aim, o_ldt, o_bre, o_bim):
        _, vjp = jax.vjp(_disc, are[...], aim[...], ldt[...], bre[...], bim[...])
        d_are, d_aim, d_ldt, d_bre, d_bim = vjp((gabre[...], gabim[...], gbbre[...], gbbim[...]))

        def group_sum(v):
            for k in (1, 2, 4, 8):
                v = v + pltpu.roll(v, width - k, 1)
            return v

        o_are[...] = group_sum(d_are)
        o_aim[...] = group_sum(d_aim)
        o_ldt[...] = d_ldt
        o_bre[...] = d_bre
        o_bim[...] = d_bim

    shp = jax.ShapeDtypeStruct(a_re_r.shape, F32)
    return _pcall(body, name="ssm_disc_bwd",
                  out_shape=(shp, shp, jax.ShapeDtypeStruct(log_dt.shape, F32), shp, shp),
                  )(a_re_r, a_im_r, log_dt, b_re, b_im, g_abre, g_abim, g_bbre, g_bbim)


def _in_proj(order, x2, g1, w_in_b):
    n = x2.shape[0]
    tm = min(IN_TILE, n)
    n_tiles = n // tm

    def body(order_ref, x_ref, g_ref, w_ref, xn_ref, proj_ref, wall_ref,
             xn_scr, wbuf, send_sems, recv_sems, loc_sems, out_sems):
        k = pl.program_id(0)
        i = pl.program_id(1)

        def slot(dev):
            return wbuf.at[dev // 2, :, pl.ds(pl.multiple_of((dev % 2) * COLS_PER_DEV, LANES), COLS_PER_DEV)]

        gather = _TwoLevelGather([w_ref], [slot], send_sems, recv_sems, loc_sems)

        @pl.when((k == 0) & (i == 0))
        def _():
            gather.start(chips=(0, 1))

        def own_chip():
            gather.wait_own()
            gather.wait_sibling()

        def other_chip(j):
            gather.wait_and_pass_on(j)
            if j == 0:
                gather.start_to((2,))
            gather.wait_passed_on(j)

        arrivals = [own_chip] + [functools.partial(other_chip, j) for j in range(3)]
        for kk, arrived in enumerate(arrivals):
            @pl.when((k == kk) & (i == 0))
            def _(arrived=arrived):
                arrived()

        rows = pl.ds(pl.multiple_of(i * tm, tm), tm)

        @pl.when(k == 0)
        def _():
            x = x_ref[...]
            r = lax.rsqrt(jnp.mean(x * x, axis=-1, keepdims=True) + EPS)
            xn = ((x * r) * g_ref[...]).astype(BF16)
            xn_scr[rows, :] = xn
            xn_ref[...] = xn

        proj_ref[...] = _dot(xn_scr[rows, :], wbuf[order_ref[k]])

        @pl.when((k == N_CHIP - 1) & (i == n_tiles - 1))
        def _():
            gather.wait_sends()
            outs = [pltpu.make_async_copy(wbuf.at[q], wall_ref.at[:, q * COLS_PER_CHIP:(q + 1) * COLS_PER_CHIP],
                                          out_sems.at[q]) for q in range(N_CHIP)]
            for cp in outs:
                cp.start()
            for cp in outs:
                cp.wait()

    tile_once = lambda k, i, order: (jnp.where(k == 0, i, n_tiles - 1), 0)
    grid_spec = pltpu.PrefetchScalarGridSpec(
        num_scalar_prefetch=1, grid=(N_CHIP, n_tiles),
        in_specs=[pl.BlockSpec((tm, D_MODEL), tile_once),
                  pl.BlockSpec((1, D_MODEL), lambda k, i, order: (0, 0)),
                  HBM_SPEC],
        out_specs=(pl.BlockSpec((tm, D_MODEL), tile_once),
                   pl.BlockSpec((tm, COLS_PER_CHIP), lambda k, i, order: (i, order[k])),
                   HBM_SPEC),
        scratch_shapes=[pltpu.VMEM((n, D_MODEL), BF16), pltpu.VMEM((N_CHIP, D_MODEL, COLS_PER_CHIP), BF16),
                        pltpu.SemaphoreType.DMA((7,)), pltpu.SemaphoreType.DMA((7,)), pltpu.SemaphoreType.DMA((1,)),
                        pltpu.SemaphoreType.DMA((N_CHIP,))])
    return _pcall(
        body, name="in_proj", grid_spec=grid_spec,
        out_shape=(jax.ShapeDtypeStruct((n, D_MODEL), BF16), jax.ShapeDtypeStruct((n, IN_COLS), F32),
                   jax.ShapeDtypeStruct((D_MODEL, IN_COLS), BF16)),
        compiler_params=_params(2),
    )(order, x2, g1, w_in_b)


def _cmul(p, q):
    return p[0] * q[0] - p[1] * q[1], p[0] * q[1] + p[1] * q[0]


def _scan_tables(ar, ai, width, reverse):
    pows = [(ar, ai)]
    for _ in range(SUBLANES - 1):
        pows.append(_cmul(pows[-1], (ar, ai)))
    row = lax.broadcasted_iota(jnp.int32, (SUBLANES, width), 0)

    def bc(v):
        return jnp.broadcast_to(v, (SUBLANES, width))

    levels = []
    for k in (1, 2, 4):
        keep = (row <= SUBLANES - 1 - k) if reverse else (row >= k)
        levels.append((jnp.where(keep, bc(pows[k - 1][0]), 0.0), jnp.where(keep, bc(pows[k - 1][1]), 0.0)))
    cre = jnp.zeros((SUBLANES, width), F32)
    cim = jnp.zeros((SUBLANES, width), F32)
    for r in range(SUBLANES):
        e = (SUBLANES - r) if reverse else (r + 1)
        cre = jnp.where(row == r, bc(pows[e - 1][0]), cre)
        cim = jnp.where(row == r, bc(pows[e - 1][1]), cim)
    return levels, (cre, cim)


def _load_chunked(src_ref, b, dst_ref, n_rows):
    n_blk = n_rows // SUBLANES
    for i in range(n_blk):
        dst_ref[b, i * SUBLANES:(i + 1) * SUBLANES, :] = src_ref[b, pl.ds(i, SUBLANES, stride=n_blk), :]


def _store_chunked(val, dst_ref, b, n_rows):
    n_blk = n_rows // SUBLANES
    for i in range(n_blk):
        dst_ref[b, pl.ds(i, SUBLANES, stride=n_blk), :] = val[i * SUBLANES:(i + 1) * SUBLANES, :]


def _chunk_scan(re_ref, im_ref, b, car_ref, ar, ai, n_rows, reverse, on_block=None):
    width = re_ref.shape[2]
    n_blk = n_rows // SUBLANES
    shape = (SUBLANES, width)
    abr = jnp.broadcast_to(ar, shape)
    abi = jnp.broadcast_to(ai, shape)
    order = list(range(n_blk - 1, -1, -1)) if reverse else list(range(n_blk))

    def blk(ref, i):
        return ref[b, i * SUBLANES:(i + 1) * SUBLANES, :]

    def step(sr, si, i):
        return abr * sr - abi * si + blk(re_ref, i), abr * si + abi * sr + blk(im_ref, i)

    fr, fi = blk(re_ref, order[0]), blk(im_ref, order[0])
    for i in order[1:]:
        fr, fi = step(fr, fi, i)

    mr, mi = ar, ai
    for _ in range(n_blk.bit_length() - 1):
        mr, mi = _cmul((mr, mi), (mr, mi))
    levels, _ = _scan_tables(mr, mi, width, reverse)
    row = lax.broadcasted_iota(jnp.int32, shape, 0)
    edge_in = SUBLANES - 1 if reverse else 0
    sh1 = SUBLANES - 1 if reverse else 1
    gr = jnp.where(row == edge_in, jnp.broadcast_to(car_ref[b, 0:1, :], shape), pltpu.roll(fr, sh1, 0))
    gi = jnp.where(row == edge_in, jnp.broadcast_to(car_ref[b, 1:2, :], shape), pltpu.roll(fi, sh1, 0))
    for (lr, li), k in zip(levels, (1, 2, 4)):
        sh = (SUBLANES - k) if reverse else k
        sr = pltpu.roll(gr, sh, 0)
        si = pltpu.roll(gi, sh, 0)
        gr, gi = gr + (lr * sr - li * si), gi + (lr * si + li * sr)
    mbr = jnp.broadcast_to(mr, shape)
    mbi = jnp.broadcast_to(mi, shape)
    edge_out = 0 if reverse else SUBLANES - 1
    car_ref[b, 0:1, :] = (fr + (mbr * gr - mbi * gi))[edge_out:edge_out + 1, :]
    car_ref[b, 1:2, :] = (fi + (mbr * gi + mbi * gr))[edge_out:edge_out + 1, :]

    sr, si = gr, gi
    for i in order:
        sr, si = step(sr, si, i)
        re_ref[b, i * SUBLANES:(i + 1) * SUBLANES, :] = sr
        im_ref[b, i * SUBLANES:(i + 1) * SUBLANES, :] = si
        if on_block is not None:
            on_block(i, sr, si)


def _ssm_fwd(u, bb_re, bb_im, c_re_t, c_imn_t, d_row, ab_re, ab_im, w_out_b, w_glu_b, conv_p, n_seq, seq):
    tt = SCAN_TILE
    nt = seq // tt

    def body(u_ref, bbre, bbim, cre, cimn, d_ref, are, aim, wout_ref, wglu_ref, cw_ref,
             sre_ref, sim_ref, y_ref, oout_ref, oglu_ref, ocw_ref,
             up_ref, car_ref, send_sems, recv_sems, loc_sems):
        j = pl.program_id(0)
        t = pl.program_id(1)
        gather = _TwoLevelGather(
            [wout_ref, wglu_ref, cw_ref],
            [lambda dev: oout_ref.at[pl.ds(pl.multiple_of(dev * OUT_ROWS_PER_DEV, OUT_ROWS_PER_DEV), OUT_ROWS_PER_DEV), :],
             lambda dev: oglu_ref.at[pl.ds(pl.multiple_of(dev * GLU_ROWS_PER_DEV, GLU_ROWS_PER_DEV), GLU_ROWS_PER_DEV), :],
             lambda dev: ocw_ref.at[dev]],
            send_sems, recv_sems, loc_sems)

        @pl.when((j == 0) & (t == 0))
        def _():
            gather.start()

        @pl.when((j == N_JBLK // 2) & (t == 0))
        def _():
            gather.forward()

        @pl.when(t == 0)
        def _():
            car_ref[...] = jnp.zeros_like(car_ref)

        for b in range(n_seq):
            _load_chunked(u_ref, b, up_ref, tt)
            up = up_ref[b]
            ub = up.astype(BF16)
            sre_ref[b] = _dot(ub, bbre[0])
            sim_ref[b] = _dot(ub, bbim[0])
            _chunk_scan(sre_ref, sim_ref, b, car_ref, are[...], aim[...], tt, reverse=False)
            yp = (_dot(sre_ref[b].astype(BF16), cre[0]) + _dot(sim_ref[b].astype(BF16), cimn[0])
                  + d_ref[...] * up)
            _store_chunked(yp, y_ref, b, tt)

        @pl.when((j == N_JBLK - 1) & (t == nt - 1))
        def _():
            gather.finish()

    tok = lambda j, t: (0, t, j)
    blk3 = lambda j, t: (j, 0, 0)
    row = lambda j, t: (0, j)
    st = jax.ShapeDtypeStruct((n_seq, seq, N_JBLK * JB_ST), F32)
    n_arr = 3
    return _pcall(
        body, name="ssm_fwd", grid=(N_JBLK, nt),
        out_shape=(st, st, jax.ShapeDtypeStruct((n_seq, seq, SSM_W), F32),
                   jax.ShapeDtypeStruct((D_MODEL, D_MODEL), BF16), jax.ShapeDtypeStruct((SSM_W, SSM_W), BF16),
                   jax.ShapeDtypeStruct((N_DEV, SUBLANES, LANES), F32)),
        in_specs=[pl.BlockSpec((n_seq, tt, JB_CH), tok),
                  pl.BlockSpec((1, JB_CH, JB_ST), blk3), pl.BlockSpec((1, JB_CH, JB_ST), blk3),
                  pl.BlockSpec((1, JB_ST, JB_CH), blk3), pl.BlockSpec((1, JB_ST, JB_CH), blk3),
                  pl.BlockSpec((1, JB_CH), row), pl.BlockSpec((1, JB_ST), row), pl.BlockSpec((1, JB_ST), row),
                  HBM_SPEC, HBM_SPEC, HBM_SPEC],
        out_specs=(pl.BlockSpec((n_seq, tt, JB_ST), tok), pl.BlockSpec((n_seq, tt, JB_ST), tok),
                   pl.BlockSpec((n_seq, tt, JB_CH), tok), HBM_SPEC, HBM_SPEC, HBM_SPEC),
        scratch_shapes=[pltpu.VMEM((n_seq, tt, JB_CH), F32), pltpu.VMEM((n_seq, SUBLANES, JB_ST), F32),
                        pltpu.SemaphoreType.DMA((7 * n_arr,)), pltpu.SemaphoreType.DMA((7 * n_arr,)),
                        pltpu.SemaphoreType.DMA((n_arr,))],
        compiler_params=_params(2),
    )(u, bb_re, bb_im, c_re_t, c_imn_t, d_row, ab_re, ab_im, w_out_b, w_glu_b, conv_p)


def _ssm_bwd(dy, u, s_re, s_im, bb_re, bb_im, c_re_t, c_imn_t, d_row, ab_re, ab_im, g_out, g_glu, n_seq, seq):
    tt = SCAN_TILE
    nt = seq // tt
    rows8 = tt // SUBLANES

    def body(dy_ref, u_ref, sre_ref, sim_ref, pre_ref, pim_ref, bbre, bbim, cre, cimn, d_ref, are, aim,
             gout_ref, gglu_ref,
             du_ref, dcre_ref, dcim_ref, dbbre_ref, dbbim_ref, dare_ref, daim_ref, dd_ref, rout_ref, rglu_ref,
             lre_ref, lim_ref, dyp_ref, up_ref, car_ref, send_sems, recv_sems, loc_sems):
        j = pl.program_id(0)
        tr = pl.program_id(1)

        def exchange():
            return _direct_copies(lambda pid: [gout_ref.at[pid], gglu_ref.at[pid]], [rout_ref, rglu_ref],
                                  send_sems, recv_sems, loc_sems)

        @pl.when((j == 0) & (tr == 0))
        def _():
            mine, sends = exchange()
            for cp in mine + sends:
                cp.start()

        @pl.when(tr == 0)
        def _():
            car_ref[...] = jnp.zeros_like(car_ref)
            for r in (dcre_ref, dcim_ref, dbbre_ref, dbbim_ref, dare_ref, daim_ref, dd_ref):
                r[...] = jnp.zeros_like(r)

        first = tr == nt - 1
        row = lax.broadcasted_iota(jnp.int32, (SUBLANES, JB_ST), 0)
        n_blk = tt // SUBLANES
        for b in range(n_seq):
            _load_chunked(dy_ref, b, dyp_ref, tt)
            _load_chunked(u_ref, b, up_ref, tt)
            dyp = dyp_ref[b]
            up = up_ref[b]
            dyb = dyp.astype(BF16)
            ub = up.astype(BF16)
            lre_ref[b] = _dot_nt(dyb, cre[0])
            lim_ref[b] = _dot_nt(dyb, cimn[0])
            acc = [jnp.zeros((SUBLANES, JB_ST), F32), jnp.zeros((SUBLANES, JB_ST), F32)]

            def on_block(i, lr, li, b=b, acc=acc):
                if i > 0:
                    spr = sre_ref[b, (i - 1) * SUBLANES:i * SUBLANES, :]
                    spi = sim_ref[b, (i - 1) * SUBLANES:i * SUBLANES, :]
                else:
                    hr = jnp.where(first, 0.0, pre_ref[b, SUBLANES - 1:SUBLANES, :])
                    hi = jnp.where(first, 0.0, pim_ref[b, SUBLANES - 1:SUBLANES, :])
                    last_r = sre_ref[b, (n_blk - 1) * SUBLANES:n_blk * SUBLANES, :]
                    last_i = sim_ref[b, (n_blk - 1) * SUBLANES:n_blk * SUBLANES, :]
                    spr = jnp.where(row == 0, jnp.broadcast_to(hr, row.shape), pltpu.roll(last_r, 1, 0))
                    spi = jnp.where(row == 0, jnp.broadcast_to(hi, row.shape), pltpu.roll(last_i, 1, 0))
                acc[0] = acc[0] + (lr * spr + li * spi)
                acc[1] = acc[1] + (li * spr - lr * spi)

            _chunk_scan(lre_ref, lim_ref, b, car_ref, are[...], -aim[...], tt, reverse=True, on_block=on_block)
            dare_ref[...] += jnp.sum(acc[0], axis=0, keepdims=True)
            daim_ref[...] += jnp.sum(acc[1], axis=0, keepdims=True)
            lrb = lre_ref[b].astype(BF16)
            lib = lim_ref[b].astype(BF16)
            dup = d_ref[...] * dyp + _dot_nt(lrb, bbre[0]) + _dot_nt(lib, bbim[0])
            _store_chunked(dup, du_ref, b, tt)
            dbbre_ref[0] += _dot_tn(ub, lrb)
            dbbim_ref[0] += _dot_tn(ub, lib)
            dcre_ref[0] += _dot_tn(dyb, sre_ref[b].astype(BF16))
            dcim_ref[0] += _dot_tn(dyb, sim_ref[b].astype(BF16))
            dd_ref[...] += jnp.sum(dyp * up, axis=0, keepdims=True)

        @pl.when((j == N_JBLK - 1) & (tr == nt - 1))
        def _():
            mine, sends = exchange()
            for cp in sends + mine:
                cp.wait()

    tok = lambda j, t: (0, nt - 1 - t, j)
    halo = lambda j, t: (0, jnp.maximum((nt - 1 - t) * rows8 - 1, 0), j)
    blk3 = lambda j, t: (j, 0, 0)
    row1 = lambda j, t: (0, j)
    acc_shape = jax.ShapeDtypeStruct((N_JBLK, JB_CH, JB_ST), F32)
    return _pcall(
        body, name="ssm_bwd", grid=(N_JBLK, nt),
        out_shape=(jax.ShapeDtypeStruct((n_seq, seq, SSM_W), F32), acc_shape, acc_shape, acc_shape, acc_shape,
                   jax.ShapeDtypeStruct((1, N_JBLK * JB_ST), F32), jax.ShapeDtypeStruct((1, N_JBLK * JB_ST), F32),
                   jax.ShapeDtypeStruct((1, SSM_W), F32),
                   jax.ShapeDtypeStruct((N_DEV,) + g_out.shape[1:], F32),
                   jax.ShapeDtypeStruct((N_DEV,) + g_glu.shape[1:], F32)),
        in_specs=[pl.BlockSpec((n_seq, tt, JB_CH), tok), pl.BlockSpec((n_seq, tt, JB_CH), tok),
                  pl.BlockSpec((n_seq, tt, JB_ST), tok), pl.BlockSpec((n_seq, tt, JB_ST), tok),
                  pl.BlockSpec((n_seq, SUBLANES, JB_ST), halo), pl.BlockSpec((n_seq, SUBLANES, JB_ST), halo),
                  pl.BlockSpec((1, JB_CH, JB_ST), blk3), pl.BlockSpec((1, JB_CH, JB_ST), blk3),
                  pl.BlockSpec((1, JB_ST, JB_CH), blk3), pl.BlockSpec((1, JB_ST, JB_CH), blk3),
                  pl.BlockSpec((1, JB_CH), row1), pl.BlockSpec((1, JB_ST), row1), pl.BlockSpec((1, JB_ST), row1),
                  HBM_SPEC, HBM_SPEC],
        out_specs=(pl.BlockSpec((n_seq, tt, JB_CH), tok),
                   pl.BlockSpec((1, JB_CH, JB_ST), blk3), pl.BlockSpec((1, JB_CH, JB_ST), blk3),
                   pl.BlockSpec((1, JB_CH, JB_ST), blk3), pl.BlockSpec((1, JB_CH, JB_ST), blk3),
                   pl.BlockSpec((1, JB_ST), row1), pl.BlockSpec((1, JB_ST), row1), pl.BlockSpec((1, JB_CH), row1),
                   HBM_SPEC, HBM_SPEC),
        scratch_shapes=[pltpu.VMEM((n_seq, tt, JB_ST), F32), pltpu.VMEM((n_seq, tt, JB_ST), F32),
                        pltpu.VMEM((n_seq, tt, JB_CH), F32), pltpu.VMEM((n_seq, tt, JB_CH), F32),
                        pltpu.VMEM((n_seq, SUBLANES, JB_ST), F32),
                        pltpu.SemaphoreType.DMA((7 * 2,)), pltpu.SemaphoreType.DMA((7 * 2,)),
                        pltpu.SemaphoreType.DMA((2,))],
        compiler_params=_params(2),
    )(dy, u, s_re, s_im, s_re, s_im, bb_re, bb_im, c_re_t, c_imn_t, d_row, ab_re, ab_im, g_out, g_glu)


def _mix(x2, tgt2, y, proj, gf, b_glu, conv8, w_glu_f, w_out_f, seq):
    n = x2.shape[0]
    tm = TOK_TILE
    tiles_per_seq = seq // tm
    rows8 = tm // SUBLANES

    def body(x_ref, t_ref, y_ref, zs_ref, h_ref, bc_ref, cc_ref, zc_ref, hp_ref, ccp_ref,
             gf_ref, bg_ref, cw_ref, wg_ref, wo_ref,
             dh2_ref, dy_ref, dzs_ref, dbc_ref, dzc_ref, dyc_ref,
             dwo_ref, dwg_ref, loss_ref, dgf_ref, dbg_ref, dcw_ref):
        i = pl.program_id(0)

        @pl.when(i == 0)
        def _():
            for r in (dwo_ref, dwg_ref, loss_ref, dgf_ref, dbg_ref, dcw_ref):
                r[...] = jnp.zeros_like(r)

        yv = y_ref[...]
        y1, dgelu = _gelu_and_grad(yv)
        y1b = y1.astype(BF16)
        gate = _sigmoid(_dot(y1b, wg_ref[...]) + bg_ref[...])
        y2 = y1 * gate
        szs, dszs = _silu_and_grad(zs_ref[...])
        yssm = y2 * szs
        hv = h_ref[...]
        ccv = cc_ref[...]
        bcv = bc_ref[...]
        v = ccv * hv
        first = (i % tiles_per_seq) == 0
        vhalo = jnp.where(first, 0.0, ccp_ref[...] * hp_ref[...])
        v1 = _shift_down(v, vhalo, 1)
        v2 = _shift_down(v, vhalo, 2)
        w0 = cw_ref[0:1, :]
        w1 = cw_ref[1:2, :]
        w2 = cw_ref[2:3, :]
        yc = w0 * v2 + w1 * v1 + w2 * v
        szc, dszc = _silu_and_grad(zc_ref[...])
        yconv = (bcv * yc) * szc
        ysb = yssm.astype(BF16)
        ycb = yconv.astype(BF16)
        h2 = x_ref[...] + _dot(ysb, wo_ref[0:SSM_W, :]) + _dot(ycb, wo_ref[SSM_W:, :])
        r2 = lax.rsqrt(jnp.mean(h2 * h2, axis=-1, keepdims=True) + EPS)
        hn = h2 * r2
        gfv = gf_ref[...]
        err = hn * gfv - t_ref[...]
        loss_ref[...] += 0.5 * jnp.sum(jnp.mean(err * err, axis=-1, keepdims=True))
        dout = err * (1.0 / D_MODEL)
        dgf_ref[...] += jnp.sum(dout * hn, axis=0, keepdims=True)
        dn = dout * gfv
        dh2 = r2 * (dn - hn * jnp.mean(dn * hn, axis=-1, keepdims=True))
        dh2_ref[...] = dh2
        dh2b = dh2.astype(BF16)
        dwo_ref[0:SSM_W, :] += _dot_tn(ysb, dh2b)
        dwo_ref[SSM_W:, :] += _dot_tn(ycb, dh2b)
        dyssm = _dot_nt(dh2b, wo_ref[0:SSM_W, :])
        dyconv = _dot_nt(dh2b, wo_ref[SSM_W:, :])
        dy2 = dyssm * szs
        dzs_ref[...] = dyssm * y2 * dszs
        dgp = dy2 * y1 * (gate * (1.0 - gate))
        dgpb = dgp.astype(BF16)
        dy1 = dy2 * gate + _dot_nt(dgpb, wg_ref[...])
        dwg_ref[...] += _dot_tn(y1b, dgpb)
        dbg_ref[...] += jnp.sum(dgp, axis=0, keepdims=True)
        dy_ref[...] = dy1 * dgelu
        dbc_ref[...] = dyconv * yc * szc
        dyc = dyconv * bcv * szc
        dyc_ref[...] = dyc
        dzc_ref[...] = dyconv * bcv * yc * dszc
        dcw_ref[0:1, :] += jnp.sum(dyc * v2, axis=0, keepdims=True)
        dcw_ref[1:2, :] += jnp.sum(dyc * v1, axis=0, keepdims=True)
        dcw_ref[2:3, :] += jnp.sum(dyc * v, axis=0, keepdims=True)

    tile_d = pl.BlockSpec((tm, D_MODEL), lambda i: (i, 0))
    tile_s = pl.BlockSpec((tm, SSM_W), lambda i: (i, 0))
    seg_of = lambda c: pl.BlockSpec((tm, SSM_W), lambda i: (i, c))
    halo_of = lambda c: pl.BlockSpec((SUBLANES, SSM_W), lambda i: (jnp.maximum(i * rows8 - 1, 0), c))
    const = lambda shape: pl.BlockSpec(shape, lambda i: (0,) * len(shape))
    seg = jax.ShapeDtypeStruct((n, SSM_W), F32)
    return _pcall(
        body, name="mix", grid=(n // tm,),
        out_shape=(jax.ShapeDtypeStruct((n, D_MODEL), F32), seg, seg, seg, seg, seg,
                   jax.ShapeDtypeStruct((D_MODEL, D_MODEL), F32), jax.ShapeDtypeStruct((SSM_W, SSM_W), F32),
                   jax.ShapeDtypeStruct((SUBLANES, LANES), F32), jax.ShapeDtypeStruct((1, D_MODEL), F32),
                   jax.ShapeDtypeStruct((1, SSM_W), F32), jax.ShapeDtypeStruct((SUBLANES, CONV_W), F32)),
        in_specs=[tile_d, tile_d, tile_s, seg_of(SEG_ZS), seg_of(SEG_H), seg_of(SEG_BC), seg_of(SEG_CC), seg_of(SEG_ZC),
                  halo_of(SEG_H), halo_of(SEG_CC),
                  const((1, D_MODEL)), const((1, SSM_W)), const((SUBLANES, CONV_W)),
                  const((SSM_W, SSM_W)), const((D_MODEL, D_MODEL))],
        out_specs=(tile_d, tile_s, tile_s, tile_s, tile_s, tile_s,
                   const((D_MODEL, D_MODEL)), const((SSM_W, SSM_W)), const((SUBLANES, LANES)),
                   const((1, D_MODEL)), const((1, SSM_W)), const((SUBLANES, CONV_W))),
        compiler_params=_params(1),
    )(x2, tgt2, y, proj, proj, proj, proj, proj, proj, proj, gf, b_glu, conv8, w_glu_f, w_out_f)


def _in_bwd(x2, dh2, du, dzs, dyc, proj, dbc, dzc, g1, conv8, w_full, seq):
    n = x2.shape[0]
    tm = TOK_TILE
    n_tiles = n // tm
    tiles_per_seq = seq // tm
    rows8 = tm // SUBLANES
    n_blk8 = n // SUBLANES

    def body(x_ref, dh2_ref, du_ref, dzs_ref, dyc_ref, dycn_ref, h_ref, cc_ref, dbc_ref, dzc_ref,
             g_ref, cw_ref, w_ref, gx_ref, dp_ref, dg_ref):
        i = pl.program_id(0)

        @pl.when(i == 0)
        def _():
            dg_ref[...] = jnp.zeros_like(dg_ref)

        dyc = dyc_ref[...]
        last = (i % tiles_per_seq) == tiles_per_seq - 1
        nhalo = jnp.where(last, 0.0, dycn_ref[...])
        dv = (cw_ref[2:3, :] * dyc + cw_ref[1:2, :] * _shift_up(dyc, nhalo, 1)
              + cw_ref[0:1, :] * _shift_up(dyc, nhalo, 2))
        parts = (du_ref[...], dzs_ref[...], dv * cc_ref[...], dbc_ref[...], dv * h_ref[...], dzc_ref[...])
        dxn = jnp.zeros((tm, D_MODEL), F32)
        for k, p in enumerate(parts):
            pb = p.astype(BF16)
            dp_ref[:, k * SSM_W:(k + 1) * SSM_W] = pb
            dxn = dxn + _dot_nt(pb, w_ref[:, k * SSM_W:(k + 1) * SSM_W])
        x = x_ref[...]
        r = lax.rsqrt(jnp.mean(x * x, axis=-1, keepdims=True) + EPS)
        xh = x * r
        dg_ref[...] += jnp.sum(dxn * xh, axis=0, keepdims=True)
        dn = dxn * g_ref[...]
        gx_ref[...] = dh2_ref[...] + r * (dn - xh * jnp.mean(dn * xh, axis=-1, keepdims=True))

    tile_d = pl.BlockSpec((tm, D_MODEL), lambda i: (i, 0))
    tile_s = pl.BlockSpec((tm, SSM_W), lambda i: (i, 0))
    seg_of = lambda c: pl.BlockSpec((tm, SSM_W), lambda i: (i, c))
    nhalo = pl.BlockSpec((SUBLANES, SSM_W), lambda i: (jnp.minimum((i + 1) * rows8, n_blk8 - 1), 0))
    const = lambda shape: pl.BlockSpec(shape, lambda i: (0,) * len(shape))
    return _pcall(
        body, name="in_bwd", grid=(n_tiles,),
        out_shape=(jax.ShapeDtypeStruct((n, D_MODEL), F32), jax.ShapeDtypeStruct((n, IN_COLS), BF16),
                   jax.ShapeDtypeStruct((SUBLANES, D_MODEL), F32)),
        in_specs=[tile_d, tile_d, tile_s, tile_s, tile_s, nhalo, seg_of(SEG_H), seg_of(SEG_CC), tile_s, tile_s,
                  const((1, D_MODEL)), const((SUBLANES, CONV_W)), const((D_MODEL, IN_COLS))],
        out_specs=(tile_d, pl.BlockSpec((tm, IN_COLS), lambda i: (i, 0)), const((SUBLANES, D_MODEL))),
        compiler_params=_params(1),
    )(x2, dh2, du, dzs, dyc, dyc, proj, proj, dbc, dzc, g1, conv8, w_full)


def _dw_in_exchange(order, xn, dproj, small_f, small_b):
    n = xn.shape[0]
    tk = 512
    nk = n // tk
    piece = (D_MODEL, COLS_PER_DEV)

    def body(order_ref, xn_ref, dp_ref, smf_ref, smb_ref, own_ref, rchip_ref, rsmf_ref, rsmb_ref,
             acc, stage, sbuf, give_send, give_recv, keep_send, keep_recv, sm_send, sm_recv, sm_loc):
        del order_ref
        s = pl.program_id(0)
        x, y, c = _mesh_pos()
        sib = (x, y, 1 - c)
        chips = [(1 - x, y), (x, 1 - y), (1 - x, 1 - y)]
        gather = _TwoLevelGather([smf_ref, smb_ref], [lambda dev: rsmf_ref.at[dev], lambda dev: rsmb_ref.at[dev]],
                                 sm_send, sm_recv, sm_loc)

        def half(i, core):
            return acc.at[i % 2, :, pl.ds(pl.multiple_of(core * COLS_PER_DEV, LANES), COLS_PER_DEV)]

        def give(i):
            return pltpu.make_async_remote_copy(src_ref=half(i, 1 - c), dst_ref=stage.at[i], send_sem=give_send.at[i],
                                                recv_sem=give_recv.at[i], device_id=sib, device_id_type=MESH)

        def keep(i):
            return pltpu.make_async_remote_copy(src_ref=sbuf.at[i], dst_ref=rchip_ref.at[i], send_sem=keep_send.at[i],
                                                recv_sem=keep_recv.at[i], device_id=(*chips[i], c), device_id_type=MESH)

        def chip_sum(i):
            give(i).wait_recv()
            mine = [acc[i % 2, :, cc * COLS_PER_DEV:(cc + 1) * COLS_PER_DEV] for cc in range(2)]
            return jnp.where(c == 0, mine[0], mine[1]) + stage[i]

        @pl.when(s == 0)
        def _():
            gather.start()

        @pl.when(s == N_CHIP // 2)
        def _():
            gather.forward()

        for k in range(1, N_CHIP):
            @pl.when(s == k)
            def _(k=k):
                sbuf[k - 1] = chip_sum(k - 1).astype(BF16)
                keep(k - 1).start()
                if k >= 2:
                    give(k - 2).wait_send()

        slot = s % 2
        acc[slot] = _dot_tn(xn_ref[pl.ds(0, tk), :], dp_ref[pl.ds(0, tk), :])

        def kstep(kk, carry):
            off = pl.multiple_of(kk * tk, tk)
            acc[slot] += _dot_tn(xn_ref[pl.ds(off, tk), :], dp_ref[pl.ds(off, tk), :])
            return carry

        lax.fori_loop(1, nk, kstep, 0)

        for k in range(N_CHIP):
            @pl.when(s == k)
            def _(k=k):
                give(k).start()

        @pl.when(s == N_CHIP - 1)
        def _():
            own_ref[...] = chip_sum(N_CHIP - 1)
            give(N_CHIP - 2).wait_send()
            give(N_CHIP - 1).wait_send()
            for i in range(3):
                keep(i).wait()
            gather.finish()

    grid_spec = pltpu.PrefetchScalarGridSpec(
        num_scalar_prefetch=1, grid=(N_CHIP,),
        in_specs=[pl.BlockSpec(memory_space=pltpu.VMEM),
                  pl.BlockSpec((n, COLS_PER_CHIP), lambda s, order: (0, order[s])),
                  HBM_SPEC, HBM_SPEC],
        out_specs=(pl.BlockSpec(piece, lambda s, order: (0, 0)), HBM_SPEC, HBM_SPEC, HBM_SPEC),
        scratch_shapes=[pltpu.VMEM((2, D_MODEL, COLS_PER_CHIP), F32), pltpu.VMEM((4,) + piece, F32),
                        pltpu.VMEM((3,) + piece, BF16),
                        pltpu.SemaphoreType.DMA((4,)), pltpu.SemaphoreType.DMA((4,)),
                        pltpu.SemaphoreType.DMA((3,)), pltpu.SemaphoreType.DMA((3,)),
                        pltpu.SemaphoreType.DMA((7 * 2,)), pltpu.SemaphoreType.DMA((7 * 2,)),
                        pltpu.SemaphoreType.DMA((2,))])
    return _pcall(
        body, name="dw_in_exchange", grid_spec=grid_spec,
        out_shape=(jax.ShapeDtypeStruct(piece, F32), jax.ShapeDtypeStruct((3,) + piece, BF16),
                   jax.ShapeDtypeStruct((N_DEV,) + small_f.shape, small_f.dtype),
                   jax.ShapeDtypeStruct((N_DEV,) + small_b.shape, small_b.dtype)),
        compiler_params=_params(1),
    )(order, xn, dproj, small_f, small_b)


def _adamw(g, w, m, v):
    m_new = ADAM_B1 * m + (1.0 - ADAM_B1) * g
    v_new = ADAM_B2 * v + (1.0 - ADAM_B2) * (g * g)
    m_hat = m_new / (1.0 - ADAM_B1 ** ADAM_STEP)
    v_hat = v_new / (1.0 - ADAM_B2 ** ADAM_STEP)
    delta = -ADAM_LR * (m_hat / (jnp.sqrt(v_hat) + ADAM_EPS) + ADAM_WD * w)
    return delta, m_new, v_new


def _reduce_adam(recv, w, m, v, name, row_tile):
    rows, cols = w.shape

    def body(r_ref, w_ref, m_ref, v_ref, g_ref, d_ref, nm_ref, nv_ref):
        g = r_ref[0]
        for s in range(1, N_DEV):
            g = g + r_ref[s]
        g_ref[...] = g
        d_ref[...], nm_ref[...], nv_ref[...] = _adamw(g, w_ref[...], m_ref[...], v_ref[...])

    tile = pl.BlockSpec((row_tile, cols), lambda i: (i, 0))
    shp = jax.ShapeDtypeStruct((rows, cols), F32)
    return _pcall(
        body, name=name, grid=(rows // row_tile,),
        out_shape=(shp,) * 4,
        in_specs=[pl.BlockSpec((N_DEV, row_tile, cols), lambda i: (0, i, 0)), tile, tile, tile],
        out_specs=(tile,) * 4,
        compiler_params=_params(1),
    )(recv, w, m, v)


def _reduce_adam_w_in(own, rchip, w, m, v):
    rows, cols = w.shape
    row_tile = 256

    def body(o_ref, r_ref, w_ref, m_ref, v_ref, g_ref, d_ref, nm_ref, nv_ref):
        g = o_ref[...]
        for s in range(3):
            g = g + r_ref[s].astype(F32)
        g_ref[...] = g
        d_ref[...], nm_ref[...], nv_ref[...] = _adamw(g, w_ref[...], m_ref[...], v_ref[...])

    tile = pl.BlockSpec((row_tile, cols), lambda i: (i, 0))
    shp = jax.ShapeDtypeStruct((rows, cols), F32)
    return _pcall(
        body, name="reduce_adam_w_in", grid=(rows // row_tile,),
        out_shape=(shp,) * 4,
        in_specs=[tile, pl.BlockSpec((3, row_tile, cols), lambda i: (0, i, 0)), tile, tile, tile],
        out_specs=(tile,) * 4,
        compiler_params=_params(1),
    )(own, rchip, w, m, v)


def _reduce_adam_stacked(recvs, wmvs, name):
    n_pairs = len(recvs)

    def body(*refs):
        for r_ref, p_ref, o_ref in zip(refs[:n_pairs], refs[n_pairs:2 * n_pairs], refs[2 * n_pairs:]):
            g = r_ref[0].astype(F32)
            for s in range(1, N_DEV):
                g = g + r_ref[s].astype(F32)
            o_ref[0] = g
            o_ref[1], o_ref[2], o_ref[3] = _adamw(g, p_ref[0], p_ref[1], p_ref[2])

    return _pcall(body, name=name,
                  out_shape=tuple(jax.ShapeDtypeStruct((4,) + wmv.shape[1:], F32) for wmv in wmvs),
                  compiler_params=_params(0))(*recvs, *wmvs)


_SMALL_F32 = (("loss", 1), ("norm_gain", D_MODEL), ("final_norm_gain", D_MODEL), ("b_glu", SSM_W),
              ("ssm_a_re", N_GROUPS * STATE), ("ssm_a_im", N_GROUPS * STATE), ("ssm_log_dt", N_GROUPS),
              ("ssm_d", N_GROUPS * GROUP), ("conv_w", 3 * CONV_W))
_SMALL_BF16 = (("ssm_b_re", N_GROUPS * STATE * GROUP), ("ssm_b_im", N_GROUPS * STATE * GROUP),
               ("ssm_c_re", N_GROUPS * STATE * GROUP), ("ssm_c_im", N_GROUPS * STATE * GROUP))
_PACK_UNIT = 2 * SUBLANES * LANES


def _pack_small(dicts, names):
    cols = []
    for name, size in names:
        flat = jnp.stack([d[name].reshape(-1) for d in dicts])
        padded = -(-size // _PACK_UNIT) * _PACK_UNIT
        cols.append(jnp.pad(flat, ((0, 0), (0, padded - size))).reshape(len(dicts), -1, LANES))
    return jnp.concatenate(cols, axis=1)


def _unpack_small(packed, names):
    out, r0 = {}, 0
    for name, size in names:
        nrows = -(-size // _PACK_UNIT) * 2 * SUBLANES
        out[name] = packed[:, r0:r0 + nrows].reshape(packed.shape[0], -1)[:, :size]
        r0 += nrows
    return out


def _block_diag(m4):
    eye = jnp.eye(SUBLANES, dtype=m4.dtype)
    j, g, a, b = m4.shape
    return jnp.einsum("jgab,gk->jgakb", m4, eye).reshape(j, g * a, g * b)


def _block_diag_extract(dense, a, b):
    d5 = dense.reshape(N_JBLK, SUBLANES, a, SUBLANES, b)
    return jnp.stack([d5[:, g, :, g, :] for g in range(SUBLANES)], axis=1)


def kernel(x, norm_gain, w_in, ssm_a_re, ssm_a_im, ssm_log_dt, ssm_b_re, ssm_b_im, ssm_c_re, ssm_c_im, ssm_d, w_glu, b_glu, conv_w, w_out, final_norm_gain, loss_target, m_norm_gain, m_w_in, m_ssm_a_re, m_ssm_a_im, m_ssm_log_dt, m_ssm_b_re, m_ssm_b_im, m_ssm_c_re, m_ssm_c_im, m_ssm_d, m_w_glu, m_b_glu, m_conv_w, m_w_out, m_final_norm_gain, v_norm_gain, v_w_in, v_ssm_a_re, v_ssm_a_im, v_ssm_log_dt, v_ssm_b_re, v_ssm_b_im, v_ssm_c_re, v_ssm_c_im, v_ssm_d, v_w_glu, v_b_glu, v_conv_w, v_w_out, v_final_norm_gain):
    n_seq, seq, _ = x.shape
    n = n_seq * seq
    me = 4 * lax.axis_index("x") + 2 * lax.axis_index("y") + lax.axis_index("c")

    rep = lambda a: jnp.repeat(a[0], GROUP, axis=1)
    a_re_r, a_im_r = rep(ssm_a_re), rep(ssm_a_im)
    log_dt = ssm_log_dt[0].reshape(N_GROUPS, 1)
    b_re2 = ssm_b_re[0].reshape(N_GROUPS, STATE * GROUP)
    b_im2 = ssm_b_im[0].reshape(N_GROUPS, STATE * GROUP)
    ab_re_r, ab_im_r, bb_re2, bb_im2 = _ssm_disc(a_re_r, a_im_r, log_dt, b_re2, b_im2)
    ab_re = ab_re_r[:, ::GROUP].reshape(1, N_GROUPS * STATE)
    ab_im = ab_im_r[:, ::GROUP].reshape(1, N_GROUPS * STATE)

    def bb_mat(bb2):
        t = jnp.transpose(bb2.reshape(N_JBLK, SUBLANES, STATE, GROUP), (0, 1, 3, 2))
        return _block_diag(t).astype(BF16)

    def c_mat(c3, sign):
        t = jnp.transpose(c3.reshape(N_JBLK, SUBLANES, GROUP, STATE), (0, 1, 3, 2))
        return _block_diag(sign * t).astype(BF16)

    bb_re_m, bb_im_m = bb_mat(bb_re2), bb_mat(bb_im2)
    c_re_m, c_imn_m = c_mat(ssm_c_re[0], 1.0), c_mat(ssm_c_im[0], -1.0)
    d_row = ssm_d[0].reshape(1, SSM_W)

    x2 = x.reshape(n, D_MODEL)
    tgt2 = loss_target.reshape(n, D_MODEL)
    mx, my, mc = lax.axis_index("x"), lax.axis_index("y"), lax.axis_index("c")
    chip_ids = [2 * cx + cy for cx, cy in ((mx, my), (1 - mx, my), (mx, 1 - my), (1 - mx, 1 - my))]
    arrival = chip_ids
    xn, proj, w_in_f = _in_proj(jnp.stack(arrival).astype(jnp.int32), x2, norm_gain, w_in[0].astype(BF16))
    u3 = proj.reshape(n_seq, seq, IN_COLS)
    conv_p = jnp.pad(conv_w[0], ((0, SUBLANES - 3), (0, LANES - CONV_COLS_PER_DEV)))
    s_re, s_im, y3, w_out_f, w_glu_f, conv_all = _ssm_fwd(
        u3, bb_re_m, bb_im_m, c_re_m, c_imn_m, d_row, ab_re, ab_im,
        w_out[0].astype(BF16), w_glu[0].astype(BF16), conv_p, n_seq, seq)
    conv8 = jnp.transpose(conv_all[:, :, :CONV_COLS_PER_DEV], (1, 0, 2)).reshape(SUBLANES, CONV_W)
    (dh2, dy, dzs, dbc, dzc, dyc, dw_out, dw_glu, loss_t, dgf, dbg, dcw) = _mix(
        x2, tgt2, y3.reshape(n, SSM_W), proj, final_norm_gain.reshape(1, D_MODEL), b_glu, conv8,
        w_glu_f, w_out_f, seq)

    du3, dc_re_d, dc_im_d, dbb_re_d, dbb_im_d, dab_re, dab_im, dd, r_out, r_glu = _ssm_bwd(
        dy.reshape(n_seq, seq, SSM_W), u3, s_re, s_im, bb_re_m, bb_im_m, c_re_m, c_imn_m, d_row, ab_re, ab_im,
        dw_out.reshape(N_DEV, OUT_ROWS_PER_DEV, D_MODEL), dw_glu.reshape(N_DEV, GLU_ROWS_PER_DEV, SSM_W), n_seq, seq)
    du = du3.reshape(n, SSM_W)
    g_c_re = _block_diag_extract(dc_re_d, GROUP, STATE).reshape(N_GROUPS, GROUP, STATE)
    g_c_im = -_block_diag_extract(dc_im_d, GROUP, STATE).reshape(N_GROUPS, GROUP, STATE)

    def bb_grad(dense):
        t = _block_diag_extract(dense, GROUP, STATE)
        return jnp.transpose(t, (0, 1, 3, 2)).reshape(N_GROUPS, STATE * GROUP)

    def ab_grad(row):
        z = jnp.zeros((N_GROUPS, STATE, GROUP), F32)
        return z.at[:, :, 0].set(row.reshape(N_GROUPS, STATE)).reshape(N_GROUPS, STATE * GROUP)

    g_are_r, g_aim_r, g_ldt, g_bre2, g_bim2 = _ssm_disc_bwd(
        a_re_r, a_im_r, log_dt, b_re2, b_im2, ab_grad(dab_re), ab_grad(dab_im),
        bb_grad(dbb_re_d), bb_grad(dbb_im_d))
    grad_x2, dproj, dg8 = _in_bwd(x2, dh2, du, dzs, dyc, proj, dbc, dzc, norm_gain, conv8, w_in_f, seq)
    small_grads = {"loss": loss_t[0:1, 0:1], "norm_gain": dg8[0:1], "final_norm_gain": dgf, "b_glu": dbg,
                   "ssm_a_re": g_are_r[:, ::GROUP], "ssm_a_im": g_aim_r[:, ::GROUP], "ssm_log_dt": g_ldt,
                   "ssm_b_re": g_bre2, "ssm_b_im": g_bim2, "ssm_c_re": g_c_re, "ssm_c_im": g_c_im,
                   "ssm_d": dd, "conv_w": dcw[0:3]}

    order = chip_ids[1:] + chip_ids[:1]
    own_in, rchip_in, r_small_f, r_small_b = _dw_in_exchange(
        jnp.stack(order).astype(jnp.int32), xn, dproj, _pack_small([small_grads], _SMALL_F32)[0],
        _pack_small([small_grads], _SMALL_BF16)[0].astype(BF16))

    def conv_full(shard):
        return lax.dynamic_update_slice(jnp.zeros((3, CONV_W), F32), shard[0], (0, me * CONV_COLS_PER_DEV))

    zero1 = jnp.zeros((1,), F32)
    triples = dict(loss=(zero1, zero1, zero1), norm_gain=(norm_gain, m_norm_gain, v_norm_gain),
                   final_norm_gain=(final_norm_gain, m_final_norm_gain, v_final_norm_gain),
                   b_glu=(b_glu, m_b_glu, v_b_glu), ssm_a_re=(ssm_a_re, m_ssm_a_re, v_ssm_a_re),
                   ssm_a_im=(ssm_a_im, m_ssm_a_im, v_ssm_a_im), ssm_log_dt=(ssm_log_dt, m_ssm_log_dt, v_ssm_log_dt),
                   ssm_b_re=(ssm_b_re, m_ssm_b_re, v_ssm_b_re), ssm_b_im=(ssm_b_im, m_ssm_b_im, v_ssm_b_im),
                   ssm_c_re=(ssm_c_re, m_ssm_c_re, v_ssm_c_re), ssm_c_im=(ssm_c_im, m_ssm_c_im, v_ssm_c_im),
                   ssm_d=(ssm_d, m_ssm_d, v_ssm_d),
                   conv_w=(conv_full(conv_w), conv_full(m_conv_w), conv_full(v_conv_w)))
    wmv = [{k: t[i] for k, t in triples.items()} for i in range(3)]

    res_in = _reduce_adam_w_in(own_in, rchip_in, w_in[0], m_w_in[0], v_w_in[0])
    res_out = _reduce_adam(r_out, w_out[0], m_w_out[0], v_w_out[0], "reduce_adam_w_out", OUT_ROWS_PER_DEV)
    res_glu = _reduce_adam(r_glu, w_glu[0], m_w_glu[0], v_w_glu[0], "reduce_adam_w_glu", GLU_ROWS_PER_DEV)
    res_f, res_b = _reduce_adam_stacked([r_small_f, r_small_b],
                                        [_pack_small(wmv, _SMALL_F32), _pack_small(wmv, _SMALL_BF16)],
                                        "reduce_adam_small")
    small = {**_unpack_small(res_f, _SMALL_F32), **_unpack_small(res_b, _SMALL_BF16)}
    loss = small["loss"][0, 0]

    shapes = dict(norm_gain=(1, D_MODEL), ssm_a_re=(1, N_GROUPS, STATE), ssm_a_im=(1, N_GROUPS, STATE),
                  ssm_log_dt=(1, N_GROUPS), ssm_b_re=(1, N_GROUPS, STATE, GROUP), ssm_b_im=(1, N_GROUPS, STATE, GROUP),
                  ssm_c_re=(1, N_GROUPS, GROUP, STATE), ssm_c_im=(1, N_GROUPS, GROUP, STATE),
                  ssm_d=(1, N_GROUPS, GROUP), b_glu=(1, SSM_W), final_norm_gain=(D_MODEL,))
    big = dict(w_in=res_in, w_glu=res_glu, w_out=res_out)
    small4 = {name: small[name].reshape((4,) + shp) for name, shp in shapes.items()}
    conv4 = lax.dynamic_slice(small["conv_w"].reshape(4, 1, 3, CONV_W), (0, 0, 0, me * CONV_COLS_PER_DEV),
                              (4, 1, 3, CONV_COLS_PER_DEV))

    def leaf(kind, name):
        if name in big:
            return big[name][kind][None]
        if name == "conv_w":
            return conv4[kind]
        return small4[name][kind]

    order = ["norm_gain", "w_in", "ssm_a_re", "ssm_a_im", "ssm_log_dt", "ssm_b_re", "ssm_b_im", "ssm_c_re",
             "ssm_c_im", "ssm_d", "w_glu", "b_glu", "conv_w", "w_out", "final_norm_gain"]
    outs = [loss, grad_x2.reshape(x.shape)]
    for kind in range(4):
        outs += [leaf(kind, name) for name in order]
    return tuple(outs)
```

```python
import functools
import math

import jax
import jax.numpy as jnp
from jax import lax
from jax.experimental import pallas as pl
from jax.experimental.pallas import tpu as pltpu

F32 = jnp.float32
BF16 = jnp.bfloat16

N_DEV = 8
D_MODEL = 1024
SSM_W = 512
CONV_W = 512
N_GROUPS = 32
GROUP = 16
STATE = 64
IN_COLS = 3072
SEG_U, SEG_ZS, SEG_H, SEG_BC, SEG_CC, SEG_ZC = range(6)
COLS_PER_DEV = IN_COLS // N_DEV
N_CHIP = N_DEV // 2
COLS_PER_CHIP = 2 * COLS_PER_DEV
OUT_ROWS_PER_DEV = D_MODEL // N_DEV
GLU_ROWS_PER_DEV = SSM_W // N_DEV
CONV_COLS_PER_DEV = CONV_W // N_DEV
EPS = 1e-6

N_JBLK = 4
JB_CH = SSM_W // N_JBLK
JB_ST = N_GROUPS * STATE // N_JBLK

ADAM_LR = 0.001
ADAM_B1 = 0.9
ADAM_B2 = 0.999
ADAM_EPS = 1e-08
ADAM_WD = 0.01
ADAM_STEP = 10

SUBLANES = 8
LANES = 128
VMEM_LIMIT = 48 * 1024 * 1024
TOK_TILE = 256
IN_TILE = 1024
SCAN_TILE = 512

MESH = pl.DeviceIdType.MESH
HBM_SPEC = pl.BlockSpec(memory_space=pltpu.HBM)


def _pcall(body, **kw):
    return pl.pallas_call(body, **kw)


def _params(n_grid):
    return pltpu.CompilerParams(dimension_semantics=("arbitrary",) * n_grid,
                                vmem_limit_bytes=VMEM_LIMIT)


def _dot(a, b):
    return jnp.dot(a, b, preferred_element_type=F32)


def _dot_nt(a, b):
    return lax.dot_general(a, b, (((1,), (1,)), ((), ())), preferred_element_type=F32)


def _dot_tn(a, b):
    return lax.dot_general(a, b, (((0,), (0,)), ((), ())), preferred_element_type=F32)


def _sigmoid(z):
    return 1.0 / (1.0 + jnp.exp(-z))


_GELU_C = math.sqrt(2.0 / math.pi)


def _gelu_and_grad(y):
    inner = _GELU_C * (y + 0.044715 * (y * y * y))
    t = jnp.tanh(inner)
    g = 0.5 * y * (1.0 + t)
    dg = 0.5 * (1.0 + t) + 0.5 * y * (1.0 - t * t) * (_GELU_C * (1.0 + 3.0 * 0.044715 * (y * y)))
    return g, dg


def _silu_and_grad(z):
    s = _sigmoid(z)
    return z * s, s * (1.0 + z * (1.0 - s))


def _shift_down(v, halo, k):
    rolled = pltpu.roll(v, k, 0)
    row = lax.broadcasted_iota(jnp.int32, v.shape, 0)
    for r in range(k):
        rolled = jnp.where(row == r, halo[SUBLANES - k + r:SUBLANES - k + r + 1, :], rolled)
    return rolled


def _shift_up(v, halo, k):
    n = v.shape[0]
    rolled = pltpu.roll(v, n - k, 0)
    row = lax.broadcasted_iota(jnp.int32, v.shape, 0)
    for r in range(k):
        rolled = jnp.where(row == n - k + r, halo[r:r + 1, :], rolled)
    return rolled


def _mesh_pos():
    return lax.axis_index("x"), lax.axis_index("y"), lax.axis_index("c")


def _direct_copies(srcs_for, out_refs, send_sems, recv_sems, loc_sems):
    x, y, c = _mesh_pos()
    me_id = 4 * x + 2 * y + c
    n_arr = len(out_refs)
    dsts = [r.at[me_id] for r in out_refs]
    own = srcs_for(me_id)
    mine = [pltpu.make_async_copy(own[a], dsts[a], loc_sems.at[a]) for a in range(n_arr)]
    sends = []
    for k in range(1, N_DEV):
        px, py, pc = x ^ ((k >> 2) & 1), y ^ ((k >> 1) & 1), c ^ (k & 1)
        src = srcs_for(4 * px + 2 * py + pc)
        for a in range(n_arr):
            sends.append(pltpu.make_async_remote_copy(
                src_ref=src[a], dst_ref=dsts[a],
                send_sem=send_sems.at[(k - 1) * n_arr + a], recv_sem=recv_sems.at[(k - 1) * n_arr + a],
                device_id=(px, py, pc), device_id_type=MESH))
    return mine, sends


class _TwoLevelGather:
    def __init__(self, srcs, slots, send_sems, recv_sems, loc_sems):
        self.srcs, self.slots, self.n_arr = srcs, slots, len(srcs)
        self.send_sems, self.recv_sems, self.loc_sems = send_sems, recv_sems, loc_sems
        x, y, c = _mesh_pos()
        self.c = c
        self.me, self.sib = (x, y, c), (x, y, 1 - c)
        self.chips = [(1 - x, y), (x, 1 - y), (1 - x, 1 - y)]

    def _copies(self, k, block, to, from_src=False):
        dev = 4 * block[0] + 2 * block[1] + block[2]
        return [pltpu.make_async_remote_copy(
            src_ref=self.srcs[a] if from_src else self.slots[a](dev), dst_ref=self.slots[a](dev),
            send_sem=self.send_sems.at[k * self.n_arr + a], recv_sem=self.recv_sems.at[k * self.n_arr + a],
            device_id=to, device_id_type=MESH) for a in range(self.n_arr)]

    def _local(self):
        dev = 4 * self.me[0] + 2 * self.me[1] + self.me[2]
        return [pltpu.make_async_copy(self.srcs[a], self.slots[a](dev), self.loc_sems.at[a])
                for a in range(self.n_arr)]

    def start(self, chips=(0, 1, 2)):
        for cp in self._local() + self._copies(0, self.me, self.sib, True):
            cp.start()
        self.start_to(chips)

    def start_to(self, chips):
        for j in chips:
            for cp in self._copies(1 + j, self.me, (*self.chips[j], self.c), True):
                cp.start()

    def wait_own(self):
        for cp in self._local():
            cp.wait()

    def wait_sibling(self):
        for cp in self._copies(0, self.sib, self.me):
            cp.wait_recv()

    def wait_and_pass_on(self, j):
        chip = self.chips[j]
        for cp in self._copies(1 + j, (*chip, self.c), self.me):
            cp.wait_recv()
        for cp in self._copies(4 + j, (*chip, self.c), self.sib):
            cp.start()

    def wait_passed_on(self, j):
        for cp in self._copies(4 + j, (*self.chips[j], 1 - self.c), self.me):
            cp.wait_recv()

    def wait_sends(self):
        for cp in self._copies(0, self.me, self.sib, True):
            cp.wait_send()
        for j, chip in enumerate(self.chips):
            for cp in self._copies(1 + j, self.me, (*chip, self.c), True) + self._copies(4 + j, (*chip, self.c), self.sib):
                cp.wait_send()

    def forward(self):
        for j in range(3):
            self.wait_and_pass_on(j)

    def finish(self):
        self.wait_sibling()
        for j in range(3):
            self.wait_passed_on(j)
        self.wait_sends()
        self.wait_own()


def _disc(a_re, a_im, log_dt, b_re, b_im):
    dt = jnp.exp(log_dt)
    mag = jnp.exp(a_re * dt)
    ab_re = mag * jnp.cos(a_im * dt)
    ab_im = mag * jnp.sin(a_im * dt)
    den = a_re * a_re + a_im * a_im
    p_re = ab_re - 1.0
    p_im = ab_im
    q_re = (p_re * a_re + p_im * a_im) / den
    q_im = (p_im * a_re - p_re * a_im) / den
    bb_re = q_re * b_re - q_im * b_im
    bb_im = q_re * b_im + q_im * b_re
    return ab_re, ab_im, bb_re, bb_im


def _ssm_disc(a_re_r, a_im_r, log_dt, b_re, b_im):
    def body(are, aim, ldt, bre, bim, o_abre, o_abim, o_bbre, o_bbim):
        outs = _disc(are[...], aim[...], ldt[...], bre[...], bim[...])
        for o, v in zip((o_abre, o_abim, o_bbre, o_bbim), outs):
            o[...] = v

    shp = jax.ShapeDtypeStruct(a_re_r.shape, F32)
    return _pcall(body, name="ssm_disc", out_shape=(shp,) * 4)(a_re_r, a_im_r, log_dt, b_re, b_im)


def _ssm_disc_bwd(a_re_r, a_im_r, log_dt, b_re, b_im, g_abre, g_abim, g_bbre, g_bbim):
    width = a_re_r.shape[1]

    def body(are, aim, ldt, bre, bim, gabre, gabim, gbbre, gbbim, o_are, o_aim, o_ldt, o_bre, o_bim):
        _, vjp = jax.vjp(_disc, are[...], aim[...], ldt[...], bre[...], bim[...])
        d_are, d_aim, d_ldt, d_bre, d_bim = vjp((gabre[...], gabim[...], gbbre[...], gbbim[...]))

        def group_sum(v):
            for k in (1, 2, 4, 8):
                v = v + pltpu.roll(v, width - k, 1)
            return v

        o_are[...] = group_sum(d_are)
        o_aim[...] = group_sum(d_aim)
        o_ldt[...] = d_ldt
        o_bre[...] = d_bre
        o_bim[...] = d_bim

    shp = jax.ShapeDtypeStruct(a_re_r.shape, F32)
    return _pcall(body, name="ssm_disc_bwd",
                  out_shape=(shp, shp, jax.ShapeDtypeStruct(log_dt.shape, F32), shp, shp),
                  )(a_re_r, a_im_r, log_dt, b_re, b_im, g_abre, g_abim, g_bbre, g_bbim)


def _in_proj(order, x2, g1, w_in_b):
    n = x2.shape[0]
    tm = min(IN_TILE, n)
    n_tiles = n // tm

    def body(order_ref, x_ref, g_ref, w_ref, xn_ref, proj_ref, wall_ref,
             xn_scr, wbuf, send_sems, recv_sems, loc_sems, out_sems):
        k = pl.program_id(0)
        i = pl.program_id(1)

        def slot(dev):
            return wbuf.at[dev // 2, :, pl.ds(pl.multiple_of((dev % 2) * COLS_PER_DEV, LANES), COLS_PER_DEV)]

        gather = _TwoLevelGather([w_ref], [slot], send_sems, recv_sems, loc_sems)

        @pl.when((k == 0) & (i == 0))
        def _():
            gather.start(chips=(0, 1))

        def own_chip():
            gather.wait_own()
            gather.wait_sibling()

        def other_chip(j):
            gather.wait_and_pass_on(j)
            if j == 0:
                gather.start_to((2,))
            gather.wait_passed_on(j)

        arrivals = [own_chip] + [functools.partial(other_chip, j) for j in range(3)]
        for kk, arrived in enumerate(arrivals):
            @pl.when((k == kk) & (i == 0))
            def _(arrived=arrived):
                arrived()

        rows = pl.ds(pl.multiple_of(i * tm, tm), tm)

        @pl.when(k == 0)
        def _():
            x = x_ref[...]
            r = lax.rsqrt(jnp.mean(x * x, axis=-1, keepdims=True) + EPS)
            xn = ((x * r) * g_ref[...]).astype(BF16)
            xn_scr[rows, :] = xn
            xn_ref[...] = xn

        proj_ref[...] = _dot(xn_scr[rows, :], wbuf[order_ref[k]])

        @pl.when((k == N_CHIP - 1) & (i == n_tiles - 1))
        def _():
            gather.wait_sends()
            outs = [pltpu.make_async_copy(wbuf.at[q], wall_ref.at[:, q * COLS_PER_CHIP:(q + 1) * COLS_PER_CHIP],
                                          out_sems.at[q]) for q in range(N_CHIP)]
            for cp in outs:
                cp.start()
            for cp in outs:
                cp.wait()

    tile_once = lambda k, i, order: (jnp.where(k == 0, i, n_tiles - 1), 0)
    grid_spec = pltpu.PrefetchScalarGridSpec(
        num_scalar_prefetch=1, grid=(N_CHIP, n_tiles),
        in_specs=[pl.BlockSpec((tm, D_MODEL), tile_once),
                  pl.BlockSpec((1, D_MODEL), lambda k, i, order: (0, 0)),
                  HBM_SPEC],
        out_specs=(pl.BlockSpec((tm, D_MODEL), tile_once),
                   pl.BlockSpec((tm, COLS_PER_CHIP), lambda k, i, order: (i, order[k])),
                   HBM_SPEC),
        scratch_shapes=[pltpu.VMEM((n, D_MODEL), BF16), pltpu.VMEM((N_CHIP, D_MODEL, COLS_PER_CHIP), BF16),
                        pltpu.SemaphoreType.DMA((7,)), pltpu.SemaphoreType.DMA((7,)), pltpu.SemaphoreType.DMA((1,)),
                        pltpu.SemaphoreType.DMA((N_CHIP,))])
    return _pcall(
        body, name="in_proj", grid_spec=grid_spec,
        out_shape=(jax.ShapeDtypeStruct((n, D_MODEL), BF16), jax.ShapeDtypeStruct((n, IN_COLS), F32),
                   jax.ShapeDtypeStruct((D_MODEL, IN_COLS), BF16)),
        compiler_params=_params(2),
    )(order, x2, g1, w_in_b)


def _cmul(p, q):
    return p[0] * q[0] - p[1] * q[1], p[0] * q[1] + p[1] * q[0]


def _scan_tables(ar, ai, width, reverse):
    pows = [(ar, ai)]
    for _ in range(SUBLANES - 1):
        pows.append(_cmul(pows[-1], (ar, ai)))
    row = lax.broadcasted_iota(jnp.int32, (SUBLANES, width), 0)

    def bc(v):
        return jnp.broadcast_to(v, (SUBLANES, width))

    levels = []
    for k in (1, 2, 4):
        keep = (row <= SUBLANES - 1 - k) if reverse else (row >= k)
        levels.append((jnp.where(keep, bc(pows[k - 1][0]), 0.0), jnp.where(keep, bc(pows[k - 1][1]), 0.0)))
    cre = jnp.zeros((SUBLANES, width), F32)
    cim = jnp.zeros((SUBLANES, width), F32)
    for r in range(SUBLANES):
        e = (SUBLANES - r) if reverse else (r + 1)
        cre = jnp.where(row == r, bc(pows[e - 1][0]), cre)
        cim = jnp.where(row == r, bc(pows[e - 1][1]), cim)
    return levels, (cre, cim)


def _load_chunked(src_ref, b, dst_ref, n_rows):
    n_blk = n_rows // SUBLANES
    for i in range(n_blk):
        dst_ref[b, i * SUBLANES:(i + 1) * SUBLANES, :] = src_ref[b, pl.ds(i, SUBLANES, stride=n_blk), :]


def _store_chunked(val, dst_ref, b, n_rows):
    n_blk = n_rows // SUBLANES
    for i in range(n_blk):
        dst_ref[b, pl.ds(i, SUBLANES, stride=n_blk), :] = val[i * SUBLANES:(i + 1) * SUBLANES, :]


def _chunk_scan(re_ref, im_ref, bs, car_ref, ar, ai, n_rows, reverse, on_block=None):
    width = re_ref.shape[2]
    n_blk = n_rows // SUBLANES
    shape = (SUBLANES, width)
    abr = jnp.broadcast_to(ar, shape)
    abi = jnp.broadcast_to(ai, shape)
    order = list(range(n_blk - 1, -1, -1)) if reverse else list(range(n_blk))

    def blk(ref, b, i):
        return ref[b, i * SUBLANES:(i + 1) * SUBLANES, :]

    def step(state, b, i):
        sr, si = state
        return abr * sr - abi * si + blk(re_ref, b, i), abr * si + abi * sr + blk(im_ref, b, i)

    finals = {b: (blk(re_ref, b, order[0]), blk(im_ref, b, order[0])) for b in bs}
    for i in order[1:]:
        for b in bs:
            finals[b] = step(finals[b], b, i)

    mr, mi = ar, ai
    for _ in range(n_blk.bit_length() - 1):
        mr, mi = _cmul((mr, mi), (mr, mi))
    levels, _ = _scan_tables(mr, mi, width, reverse)
    mbr = jnp.broadcast_to(mr, shape)
    mbi = jnp.broadcast_to(mi, shape)
    row = lax.broadcasted_iota(jnp.int32, shape, 0)
    edge_in = SUBLANES - 1 if reverse else 0
    edge_out = 0 if reverse else SUBLANES - 1
    sh1 = SUBLANES - 1 if reverse else 1
    states = {}
    for b in bs:
        fr, fi = finals[b]
        gr = jnp.where(row == edge_in, jnp.broadcast_to(car_ref[b, 0:1, :], shape), pltpu.roll(fr, sh1, 0))
        gi = jnp.where(row == edge_in, jnp.broadcast_to(car_ref[b, 1:2, :], shape), pltpu.roll(fi, sh1, 0))
        for (lr, li), k in zip(levels, (1, 2, 4)):
            sh = (SUBLANES - k) if reverse else k
            sr = pltpu.roll(gr, sh, 0)
            si = pltpu.roll(gi, sh, 0)
            gr, gi = gr + (lr * sr - li * si), gi + (lr * si + li * sr)
        car_ref[b, 0:1, :] = (fr + (mbr * gr - mbi * gi))[edge_out:edge_out + 1, :]
        car_ref[b, 1:2, :] = (fi + (mbr * gi + mbi * gr))[edge_out:edge_out + 1, :]
        states[b] = (gr, gi)

    for i in order:
        for b in bs:
            states[b] = step(states[b], b, i)
            re_ref[b, i * SUBLANES:(i + 1) * SUBLANES, :] = states[b][0]
            im_ref[b, i * SUBLANES:(i + 1) * SUBLANES, :] = states[b][1]
            if on_block is not None:
                on_block(b, i, *states[b])


def _ssm_fwd(u, bb_re, bb_im, c_re_t, c_imn_t, d_row, ab_re, ab_im, w_out_b, w_glu_b, conv_p, n_seq, seq):
    tt = SCAN_TILE
    nt = seq // tt

    def body(u_ref, bbre, bbim, cre, cimn, d_ref, are, aim, wout_ref, wglu_ref, cw_ref,
             sre_ref, sim_ref, y_ref, oout_ref, oglu_ref, ocw_ref,
             up_ref, car_ref, send_sems, recv_sems, loc_sems):
        j = pl.program_id(0)
        t = pl.program_id(1)
        gather = _TwoLevelGather(
            [wout_ref, wglu_ref, cw_ref],
            [lambda dev: oout_ref.at[pl.ds(pl.multiple_of(dev * OUT_ROWS_PER_DEV, OUT_ROWS_PER_DEV), OUT_ROWS_PER_DEV), :],
             lambda dev: oglu_ref.at[pl.ds(pl.multiple_of(dev * GLU_ROWS_PER_DEV, GLU_ROWS_PER_DEV), GLU_ROWS_PER_DEV), :],
             lambda dev: ocw_ref.at[dev]],
            send_sems, recv_sems, loc_sems)

        @pl.when((j == 0) & (t == 0))
        def _():
            gather.start()

        @pl.when((j == N_JBLK // 2) & (t == 0))
        def _():
            gather.forward()

        @pl.when(t == 0)
        def _():
            car_ref[...] = jnp.zeros_like(car_ref)

        bs = list(range(n_seq))
        for b in bs:
            _load_chunked(u_ref, b, up_ref, tt)
        for b in bs:
            ub = up_ref[b].astype(BF16)
            sre_ref[b] = _dot(ub, bbre[0])
            sim_ref[b] = _dot(ub, bbim[0])
            _chunk_scan(sre_ref, sim_ref, [b], car_ref, are[...], aim[...], tt, reverse=False)
        for b in bs:
            yp = (_dot(sre_ref[b].astype(BF16), cre[0]) + _dot(sim_ref[b].astype(BF16), cimn[0])
                  + d_ref[...] * up_ref[b])
            _store_chunked(yp, y_ref, b, tt)

        @pl.when((j == N_JBLK - 1) & (t == nt - 1))
        def _():
            gather.finish()

    tok = lambda j, t: (0, t, j)
    blk3 = lambda j, t: (j, 0, 0)
    row = lambda j, t: (0, j)
    st = jax.ShapeDtypeStruct((n_seq, seq, N_JBLK * JB_ST), F32)
    n_arr = 3
    return _pcall(
        body, name="ssm_fwd", grid=(N_JBLK, nt),
        out_shape=(st, st, jax.ShapeDtypeStruct((n_seq, seq, SSM_W), F32),
                   jax.ShapeDtypeStruct((D_MODEL, D_MODEL), BF16), jax.ShapeDtypeStruct((SSM_W, SSM_W), BF16),
                   jax.ShapeDtypeStruct((N_DEV, SUBLANES, LANES), F32)),
        in_specs=[pl.BlockSpec((n_seq, tt, JB_CH), tok),
                  pl.BlockSpec((1, JB_CH, JB_ST), blk3), pl.BlockSpec((1, JB_CH, JB_ST), blk3),
                  pl.BlockSpec((1, JB_ST, JB_CH), blk3), pl.BlockSpec((1, JB_ST, JB_CH), blk3),
                  pl.BlockSpec((1, JB_CH), row), pl.BlockSpec((1, JB_ST), row), pl.BlockSpec((1, JB_ST), row),
                  HBM_SPEC, HBM_SPEC, HBM_SPEC],
        out_specs=(pl.BlockSpec((n_seq, tt, JB_ST), tok), pl.BlockSpec((n_seq, tt, JB_ST), tok),
                   pl.BlockSpec((n_seq, tt, JB_CH), tok), HBM_SPEC, HBM_SPEC, HBM_SPEC),
        scratch_shapes=[pltpu.VMEM((n_seq, tt, JB_CH), F32), pltpu.VMEM((n_seq, SUBLANES, JB_ST), F32),
                        pltpu.SemaphoreType.DMA((7 * n_arr,)), pltpu.SemaphoreType.DMA((7 * n_arr,)),
                        pltpu.SemaphoreType.DMA((n_arr,))],
        compiler_params=_params(2),
    )(u, bb_re, bb_im, c_re_t, c_imn_t, d_row, ab_re, ab_im, w_out_b, w_glu_b, conv_p)


def _ssm_bwd(dy, u, s_re, s_im, bb_re, bb_im, c_re_t, c_imn_t, d_row, ab_re, ab_im, g_out, g_glu, n_seq, seq):
    tt = SCAN_TILE
    nt = seq // tt
    rows8 = tt // SUBLANES

    def body(dy_ref, u_ref, sre_ref, sim_ref, pre_ref, pim_ref, bbre, bbim, cre, cimn, d_ref, are, aim,
             gout_ref, gglu_ref,
             du_ref, dcre_ref, dcim_ref, dbbre_ref, dbbim_ref, dare_ref, daim_ref, dd_ref, rout_ref, rglu_ref,
             lre_ref, lim_ref, dyp_ref, up_ref, car_ref, send_sems, recv_sems, loc_sems):
        j = pl.program_id(0)
        tr = pl.program_id(1)

        def exchange():
            return _direct_copies(lambda pid: [gout_ref.at[pid], gglu_ref.at[pid]], [rout_ref, rglu_ref],
                                  send_sems, recv_sems, loc_sems)

        @pl.when((j == 0) & (tr == 0))
        def _():
            mine, sends = exchange()
            for cp in mine + sends:
                cp.start()

        @pl.when(tr == 0)
        def _():
            car_ref[...] = jnp.zeros_like(car_ref)
            for r in (dcre_ref, dcim_ref, dbbre_ref, dbbim_ref, dare_ref, daim_ref, dd_ref):
                r[...] = jnp.zeros_like(r)

        first = tr == nt - 1
        row = lax.broadcasted_iota(jnp.int32, (SUBLANES, JB_ST), 0)
        n_blk = tt // SUBLANES
        bs = list(range(n_seq))
        for b in bs:
            _load_chunked(dy_ref, b, dyp_ref, tt)
            _load_chunked(u_ref, b, up_ref, tt)
        for b in bs:
            dyb = dyp_ref[b].astype(BF16)
            lre_ref[b] = _dot_nt(dyb, cre[0])
            lim_ref[b] = _dot_nt(dyb, cimn[0])
        acc = {b: [jnp.zeros((SUBLANES, JB_ST), F32), jnp.zeros((SUBLANES, JB_ST), F32)] for b in bs}

        def on_block(b, i, lr, li):
            if i > 0:
                spr = sre_ref[b, (i - 1) * SUBLANES:i * SUBLANES, :]
                spi = sim_ref[b, (i - 1) * SUBLANES:i * SUBLANES, :]
            else:
                hr = jnp.where(first, 0.0, pre_ref[b, SUBLANES - 1:SUBLANES, :])
                hi = jnp.where(first, 0.0, pim_ref[b, SUBLANES - 1:SUBLANES, :])
                last_r = sre_ref[b, (n_blk - 1) * SUBLANES:n_blk * SUBLANES, :]
                last_i = sim_ref[b, (n_blk - 1) * SUBLANES:n_blk * SUBLANES, :]
                spr = jnp.where(row == 0, jnp.broadcast_to(hr, row.shape), pltpu.roll(last_r, 1, 0))
                spi = jnp.where(row == 0, jnp.broadcast_to(hi, row.shape), pltpu.roll(last_i, 1, 0))
            acc[b][0] = acc[b][0] + (lr * spr + li * spi)
            acc[b][1] = acc[b][1] + (li * spr - lr * spi)

        _chunk_scan(lre_ref, lim_ref, bs, car_ref, are[...], -aim[...], tt, reverse=True, on_block=on_block)
        for b in bs:
            dare_ref[...] += jnp.sum(acc[b][0], axis=0, keepdims=True)
            daim_ref[...] += jnp.sum(acc[b][1], axis=0, keepdims=True)
            dyp = dyp_ref[b]
            up = up_ref[b]
            dyb = dyp.astype(BF16)
            ub = up.astype(BF16)
            lrb = lre_ref[b].astype(BF16)
            lib = lim_ref[b].astype(BF16)
            dup = d_ref[...] * dyp + _dot_nt(lrb, bbre[0]) + _dot_nt(lib, bbim[0])
            _store_chunked(dup, du_ref, b, tt)
            dbbre_ref[0] += _dot_tn(ub, lrb)
            dbbim_ref[0] += _dot_tn(ub, lib)
            dcre_ref[0] += _dot_tn(dyb, sre_ref[b].astype(BF16))
            dcim_ref[0] += _dot_tn(dyb, sim_ref[b].astype(BF16))
            dd_ref[...] += jnp.sum(dyp * up, axis=0, keepdims=True)

        @pl.when((j == N_JBLK - 1) & (tr == nt - 1))
        def _():
            mine, sends = exchange()
            for cp in sends + mine:
                cp.wait()

    tok = lambda j, t: (0, nt - 1 - t, j)
    halo = lambda j, t: (0, jnp.maximum((nt - 1 - t) * rows8 - 1, 0), j)
    blk3 = lambda j, t: (j, 0, 0)
    row1 = lambda j, t: (0, j)
    acc_shape = jax.ShapeDtypeStruct((N_JBLK, JB_CH, JB_ST), F32)
    return _pcall(
        body, name="ssm_bwd", grid=(N_JBLK, nt),
        out_shape=(jax.ShapeDtypeStruct((n_seq, seq, SSM_W), F32), acc_shape, acc_shape, acc_shape, acc_shape,
                   jax.ShapeDtypeStruct((1, N_JBLK * JB_ST), F32), jax.ShapeDtypeStruct((1, N_JBLK * JB_ST), F32),
                   jax.ShapeDtypeStruct((1, SSM_W), F32),
                   jax.ShapeDtypeStruct((N_DEV,) + g_out.shape[1:], F32),
                   jax.ShapeDtypeStruct((N_DEV,) + g_glu.shape[1:], F32)),
        in_specs=[pl.BlockSpec((n_seq, tt, JB_CH), tok), pl.BlockSpec((n_seq, tt, JB_CH), tok),
                  pl.BlockSpec((n_seq, tt, JB_ST), tok), pl.BlockSpec((n_seq, tt, JB_ST), tok),
                  pl.BlockSpec((n_seq, SUBLANES, JB_ST), halo), pl.BlockSpec((n_seq, SUBLANES, JB_ST), halo),
                  pl.BlockSpec((1, JB_CH, JB_ST), blk3), pl.BlockSpec((1, JB_CH, JB_ST), blk3),
                  pl.BlockSpec((1, JB_ST, JB_CH), blk3), pl.BlockSpec((1, JB_ST, JB_CH), blk3),
                  pl.BlockSpec((1, JB_CH), row1), pl.BlockSpec((1, JB_ST), row1), pl.BlockSpec((1, JB_ST), row1),
                  HBM_SPEC, HBM_SPEC],
        out_specs=(pl.BlockSpec((n_seq, tt, JB_CH), tok),
                   pl.BlockSpec((1, JB_CH, JB_ST), blk3), pl.BlockSpec((1, JB_CH, JB_ST), blk3),
                   pl.BlockSpec((1, JB_CH, JB_ST), blk3), pl.BlockSpec((1, JB_CH, JB_ST), blk3),
                   pl.BlockSpec((1, JB_ST), row1), pl.BlockSpec((1, JB_ST), row1), pl.BlockSpec((1, JB_CH), row1),
                   HBM_SPEC, HBM_SPEC),
        scratch_shapes=[pltpu.VMEM((n_seq, tt, JB_ST), F32), pltpu.VMEM((n_seq, tt, JB_ST), F32),
                        pltpu.VMEM((n_seq, tt, JB_CH), F32), pltpu.VMEM((n_seq, tt, JB_CH), F32),
                        pltpu.VMEM((n_seq, SUBLANES, JB_ST), F32),
                        pltpu.SemaphoreType.DMA((7 * 2,)), pltpu.SemaphoreType.DMA((7 * 2,)),
                        pltpu.SemaphoreType.DMA((2,))],
        compiler_params=_params(2),
    )(dy, u, s_re, s_im, s_re, s_im, bb_re, bb_im, c_re_t, c_imn_t, d_row, ab_re, ab_im, g_out, g_glu)


def _mix(x2, tgt2, y, proj, gf, b_glu, conv8, w_glu_f, w_out_f, seq):
    n = x2.shape[0]
    tm = TOK_TILE
    tiles_per_seq = seq // tm
    rows8 = tm // SUBLANES

    def body(x_ref, t_ref, y_ref, zs_ref, h_ref, bc_ref, cc_ref, zc_ref, hp_ref, ccp_ref,
             gf_ref, bg_ref, cw_ref, wg_ref, wo_ref,
             dh2_ref, dy_ref, dzs_ref, dbc_ref, dzc_ref, dyc_ref,
             dwo_ref, dwg_ref, loss_ref, dgf_ref, dbg_ref, dcw_ref):
        i = pl.program_id(0)

        @pl.when(i == 0)
        def _():
            for r in (dwo_ref, dwg_ref, loss_ref, dgf_ref, dbg_ref, dcw_ref):
                r[...] = jnp.zeros_like(r)

        yv = y_ref[...]
        y1, dgelu = _gelu_and_grad(yv)
        y1b = y1.astype(BF16)
        gate = _sigmoid(_dot(y1b, wg_ref[...]) + bg_ref[...])
        y2 = y1 * gate
        szs, dszs = _silu_and_grad(zs_ref[...])
        yssm = y2 * szs
        hv = h_ref[...]
        ccv = cc_ref[...]
        bcv = bc_ref[...]
        v = ccv * hv
        first = (i % tiles_per_seq) == 0
        vhalo = jnp.where(first, 0.0, ccp_ref[...] * hp_ref[...])
        v1 = _shift_down(v, vhalo, 1)
        v2 = _shift_down(v, vhalo, 2)
        w0 = cw_ref[0:1, :]
        w1 = cw_ref[1:2, :]
        w2 = cw_ref[2:3, :]
        yc = w0 * v2 + w1 * v1 + w2 * v
        szc, dszc = _silu_and_grad(zc_ref[...])
        yconv = (bcv * yc) * szc
        ysb = yssm.astype(BF16)
        ycb = yconv.astype(BF16)
        h2 = x_ref[...] + _dot(ysb, wo_ref[0:SSM_W, :]) + _dot(ycb, wo_ref[SSM_W:, :])
        r2 = lax.rsqrt(jnp.mean(h2 * h2, axis=-1, keepdims=True) + EPS)
        hn = h2 * r2
        gfv = gf_ref[...]
        err = hn * gfv - t_ref[...]
        loss_ref[...] += 0.5 * jnp.sum(jnp.mean(err * err, axis=-1, keepdims=True))
        dout = err * (1.0 / D_MODEL)
        dgf_ref[...] += jnp.sum(dout * hn, axis=0, keepdims=True)
        dn = dout * gfv
        dh2 = r2 * (dn - hn * jnp.mean(dn * hn, axis=-1, keepdims=True))
        dh2_ref[...] = dh2
        dh2b = dh2.astype(BF16)
        dwo_ref[0:SSM_W, :] += _dot_tn(ysb, dh2b)
        dwo_ref[SSM_W:, :] += _dot_tn(ycb, dh2b)
        dyssm = _dot_nt(dh2b, wo_ref[0:SSM_W, :])
        dyconv = _dot_nt(dh2b, wo_ref[SSM_W:, :])
        dy2 = dyssm * szs
        dzs_ref[...] = dyssm * y2 * dszs
        dgp = dy2 * y1 * (gate * (1.0 - gate))
        dgpb = dgp.astype(BF16)
        dy1 = dy2 * gate + _dot_nt(dgpb, wg_ref[...])
        dwg_ref[...] += _dot_tn(y1b, dgpb)
        dbg_ref[...] += jnp.sum(dgp, axis=0, keepdims=True)
        dy_ref[...] = dy1 * dgelu
        dbc_ref[...] = dyconv * yc * szc
        dyc = dyconv * bcv * szc
        dyc_ref[...] = dyc
        dzc_ref[...] = dyconv * bcv * yc * dszc
        dcw_ref[0:1, :] += jnp.sum(dyc * v2, axis=0, keepdims=True)
        dcw_ref[1:2, :] += jnp.sum(dyc * v1, axis=0, keepdims=True)
        dcw_ref[2:3, :] += jnp.sum(dyc * v, axis=0, keepdims=True)

    tile_d = pl.BlockSpec((tm, D_MODEL), lambda i: (i, 0))
    tile_s = pl.BlockSpec((tm, SSM_W), lambda i: (i, 0))
    seg_of = lambda c: pl.BlockSpec((tm, SSM_W), lambda i: (i, c))
    halo_of = lambda c: pl.BlockSpec((SUBLANES, SSM_W), lambda i: (jnp.maximum(i * rows8 - 1, 0), c))
    const = lambda shape: pl.BlockSpec(shape, lambda i: (0,) * len(shape))
    seg = jax.ShapeDtypeStruct((n, SSM_W), F32)
    return _pcall(
        body, name="mix", grid=(n // tm,),
        out_shape=(jax.ShapeDtypeStruct((n, D_MODEL), F32), seg, seg, seg, seg, seg,
                   jax.ShapeDtypeStruct((D_MODEL, D_MODEL), F32), jax.ShapeDtypeStruct((SSM_W, SSM_W), F32),
                   jax.ShapeDtypeStruct((SUBLANES, LANES), F32), jax.ShapeDtypeStruct((1, D_MODEL), F32),
                   jax.ShapeDtypeStruct((1, SSM_W), F32), jax.ShapeDtypeStruct((SUBLANES, CONV_W), F32)),
        in_specs=[tile_d, tile_d, tile_s, seg_of(SEG_ZS), seg_of(SEG_H), seg_of(SEG_BC), seg_of(SEG_CC), seg_of(SEG_ZC),
                  halo_of(SEG_H), halo_of(SEG_CC),
                  const((1, D_MODEL)), const((1, SSM_W)), const((SUBLANES, CONV_W)),
                  const((SSM_W, SSM_W)), const((D_MODEL, D_MODEL))],
        out_specs=(tile_d, tile_s, tile_s, tile_s, tile_s, tile_s,
                   const((D_MODEL, D_MODEL)), const((SSM_W, SSM_W)), const((SUBLANES, LANES)),
                   const((1, D_MODEL)), const((1, SSM_W)), const((SUBLANES, CONV_W))),
        compiler_params=_params(1),
    )(x2, tgt2, y, proj, proj, proj, proj, proj, proj, proj, gf, b_glu, conv8, w_glu_f, w_out_f)


def _in_bwd(x2, dh2, du, dzs, dyc, proj, dbc, dzc, g1, conv8, w_full, seq):
    n = x2.shape[0]
    tm = TOK_TILE
    n_tiles = n // tm
    tiles_per_seq = seq // tm
    rows8 = tm // SUBLANES
    n_blk8 = n // SUBLANES

    def body(x_ref, dh2_ref, du_ref, dzs_ref, dyc_ref, dycn_ref, h_ref, cc_ref, dbc_ref, dzc_ref,
             g_ref, cw_ref, w_ref, gx_ref, dp_ref, dg_ref):
        i = pl.program_id(0)

        @pl.when(i == 0)
        def _():
            dg_ref[...] = jnp.zeros_like(dg_ref)

        dyc = dyc_ref[...]
        last = (i % tiles_per_seq) == tiles_per_seq - 1
        nhalo = jnp.where(last, 0.0, dycn_ref[...])
        dv = (cw_ref[2:3, :] * dyc + cw_ref[1:2, :] * _shift_up(dyc, nhalo, 1)
              + cw_ref[0:1, :] * _shift_up(dyc, nhalo, 2))
        parts = (du_ref[...], dzs_ref[...], dv * cc_ref[...], dbc_ref[...], dv * h_ref[...], dzc_ref[...])
        dxn = jnp.zeros((tm, D_MODEL), F32)
        for k, p in enumerate(parts):
            pb = p.astype(BF16)
            dp_ref[:, k * SSM_W:(k + 1) * SSM_W] = pb
            dxn = dxn + _dot_nt(pb, w_ref[:, k * SSM_W:(k + 1) * SSM_W])
        x = x_ref[...]
        r = lax.rsqrt(jnp.mean(x * x, axis=-1, keepdims=True) + EPS)
        xh = x * r
        dg_ref[...] += jnp.sum(dxn * xh, axis=0, keepdims=True)
        dn = dxn * g_ref[...]
        gx_ref[...] = dh2_ref[...] + r * (dn - xh * jnp.mean(dn * xh, axis=-1, keepdims=True))

    tile_d = pl.BlockSpec((tm, D_MODEL), lambda i: (i, 0))
    tile_s = pl.BlockSpec((tm, SSM_W), lambda i: (i, 0))
    seg_of = lambda c: pl.BlockSpec((tm, SSM_W), lambda i: (i, c))
    nhalo = pl.BlockSpec((SUBLANES, SSM_W), lambda i: (jnp.minimum((i + 1) * rows8, n_blk8 - 1), 0))
    const = lambda shape: pl.BlockSpec(shape, lambda i: (0,) * len(shape))
    return _pcall(
        body, name="in_bwd", grid=(n_tiles,),
        out_shape=(jax.ShapeDtypeStruct((n, D_MODEL), F32), jax.ShapeDtypeStruct((n, IN_COLS), BF16),
                   jax.ShapeDtypeStruct((SUBLANES, D_MODEL), F32)),
        in_specs=[tile_d, tile_d, tile_s, tile_s, tile_s, nhalo, seg_of(SEG_H), seg_of(SEG_CC), tile_s, tile_s,
                  const((1, D_MODEL)), const((SUBLANES, CONV_W)), const((D_MODEL, IN_COLS))],
        out_specs=(tile_d, pl.BlockSpec((tm, IN_COLS), lambda i: (i, 0)), const((SUBLANES, D_MODEL))),
        compiler_params=_params(1),
    )(x2, dh2, du, dzs, dyc, dyc, proj, proj, dbc, dzc, g1, conv8, w_full)


def _dw_in_exchange(order, xn, dproj, small_f, small_b):
    n = xn.shape[0]
    tk = 512
    nk = n // tk
    piece = (D_MODEL, COLS_PER_DEV)

    def body(order_ref, xn_ref, dp_ref, smf_ref, smb_ref, own_ref, rchip_ref, rsmf_ref, rsmb_ref,
             acc, stage, sbuf, give_send, give_recv, keep_send, keep_recv, sm_send, sm_recv, sm_loc):
        del order_ref
        s = pl.program_id(0)
        x, y, c = _mesh_pos()
        sib = (x, y, 1 - c)
        chips = [(1 - x, y), (x, 1 - y), (1 - x, 1 - y)]
        gather = _TwoLevelGather([smf_ref, smb_ref], [lambda dev: rsmf_ref.at[dev], lambda dev: rsmb_ref.at[dev]],
                                 sm_send, sm_recv, sm_loc)

        def half(i, core):
            return acc.at[i % 2, :, pl.ds(pl.multiple_of(core * COLS_PER_DEV, LANES), COLS_PER_DEV)]

        def give(i):
            return pltpu.make_async_remote_copy(src_ref=half(i, 1 - c), dst_ref=stage.at[i], send_sem=give_send.at[i],
                                                recv_sem=give_recv.at[i], device_id=sib, device_id_type=MESH)

        def keep(i):
            return pltpu.make_async_remote_copy(src_ref=sbuf.at[i], dst_ref=rchip_ref.at[i], send_sem=keep_send.at[i],
                                                recv_sem=keep_recv.at[i], device_id=(*chips[i], c), device_id_type=MESH)

        def chip_sum(i):
            give(i).wait_recv()
            mine = [acc[i % 2, :, cc * COLS_PER_DEV:(cc + 1) * COLS_PER_DEV] for cc in range(2)]
            return jnp.where(c == 0, mine[0], mine[1]) + stage[i]

        @pl.when(s == 0)
        def _():
            gather.start()

        @pl.when(s == N_CHIP // 2)
        def _():
            gather.forward()

        for k in range(2, N_CHIP):
            @pl.when(s == k)
            def _(k=k):
                give(k - 2).wait_send()

        slot = s % 2
        acc[slot] = _dot_tn(xn_ref[pl.ds(0, tk), :], dp_ref[pl.ds(0, tk), :])

        def kstep(kk, carry):
            off = pl.multiple_of(kk * tk, tk)
            acc[slot] += _dot_tn(xn_ref[pl.ds(off, tk), :], dp_ref[pl.ds(off, tk), :])
            return carry

        n_first = min(nk, 3)
        lax.fori_loop(1, n_first, kstep, 0)
        for k in range(1, N_CHIP):
            @pl.when(s == k)
            def _(k=k):
                sbuf[k - 1] = chip_sum(k - 1).astype(BF16)
                keep(k - 1).start()

        lax.fori_loop(n_first, nk, kstep, 0)

        for k in range(N_CHIP):
            @pl.when(s == k)
            def _(k=k):
                give(k).start()

        @pl.when(s == N_CHIP - 1)
        def _():
            own_ref[...] = chip_sum(N_CHIP - 1)
            give(N_CHIP - 2).wait_send()
            give(N_CHIP - 1).wait_send()
            for i in range(3):
                keep(i).wait()
            gather.finish()

    grid_spec = pltpu.PrefetchScalarGridSpec(
        num_scalar_prefetch=1, grid=(N_CHIP,),
        in_specs=[pl.BlockSpec(memory_space=pltpu.VMEM),
                  pl.BlockSpec((n, COLS_PER_CHIP), lambda s, order: (0, order[s])),
                  HBM_SPEC, HBM_SPEC],
        out_specs=(pl.BlockSpec(piece, lambda s, order: (0, 0)), HBM_SPEC, HBM_SPEC, HBM_SPEC),
        scratch_shapes=[pltpu.VMEM((2, D_MODEL, COLS_PER_CHIP), F32), pltpu.VMEM((4,) + piece, F32),
                        pltpu.VMEM((3,) + piece, BF16),
                        pltpu.SemaphoreType.DMA((4,)), pltpu.SemaphoreType.DMA((4,)),
                        pltpu.SemaphoreType.DMA((3,)), pltpu.SemaphoreType.DMA((3,)),
                        pltpu.SemaphoreType.DMA((7 * 2,)), pltpu.SemaphoreType.DMA((7 * 2,)),
                        pltpu.SemaphoreType.DMA((2,))])
    return _pcall(
        body, name="dw_in_exchange", grid_spec=grid_spec,
        out_shape=(jax.ShapeDtypeStruct(piece, F32), jax.ShapeDtypeStruct((3,) + piece, BF16),
                   jax.ShapeDtypeStruct((N_DEV,) + small_f.shape, small_f.dtype),
                   jax.ShapeDtypeStruct((N_DEV,) + small_b.shape, small_b.dtype)),
        compiler_params=_params(1),
    )(order, xn, dproj, small_f, small_b)


def _adamw(g, w, m, v):
    m_new = ADAM_B1 * m + (1.0 - ADAM_B1) * g
    v_new = ADAM_B2 * v + (1.0 - ADAM_B2) * (g * g)
    m_hat = m_new / (1.0 - ADAM_B1 ** ADAM_STEP)
    v_hat = v_new / (1.0 - ADAM_B2 ** ADAM_STEP)
    delta = -ADAM_LR * (m_hat / (jnp.sqrt(v_hat) + ADAM_EPS) + ADAM_WD * w)
    return delta, m_new, v_new


def _reduce_adam(recv, w, m, v, name, row_tile):
    rows, cols = w.shape

    def body(r_ref, w_ref, m_ref, v_ref, g_ref, d_ref, nm_ref, nv_ref):
        g = r_ref[0]
        for s in range(1, N_DEV):
            g = g + r_ref[s]
        g_ref[...] = g
        d_ref[...], nm_ref[...], nv_ref[...] = _adamw(g, w_ref[...], m_ref[...], v_ref[...])

    tile = pl.BlockSpec((row_tile, cols), lambda i: (i, 0))
    shp = jax.ShapeDtypeStruct((rows, cols), F32)
    return _pcall(
        body, name=name, grid=(rows // row_tile,),
        out_shape=(shp,) * 4,
        in_specs=[pl.BlockSpec((N_DEV, row_tile, cols), lambda i: (0, i, 0)), tile, tile, tile],
        out_specs=(tile,) * 4,
        compiler_params=_params(1),
    )(recv, w, m, v)


def _reduce_adam_w_in(own, rchip, w, m, v):
    rows, cols = w.shape
    row_tile = 256

    def body(o_ref, r_ref, w_ref, m_ref, v_ref, g_ref, d_ref, nm_ref, nv_ref):
        g = o_ref[...]
        for s in range(3):
            g = g + r_ref[s].astype(F32)
        g_ref[...] = g
        d_ref[...], nm_ref[...], nv_ref[...] = _adamw(g, w_ref[...], m_ref[...], v_ref[...])

    tile = pl.BlockSpec((row_tile, cols), lambda i: (i, 0))
    shp = jax.ShapeDtypeStruct((rows, cols), F32)
    return _pcall(
        body, name="reduce_adam_w_in", grid=(rows // row_tile,),
        out_shape=(shp,) * 4,
        in_specs=[tile, pl.BlockSpec((3, row_tile, cols), lambda i: (0, i, 0)), tile, tile, tile],
        out_specs=(tile,) * 4,
        compiler_params=_params(1),
    )(own, rchip, w, m, v)


def _reduce_adam_stacked(recvs, wmvs, name):
    n_pairs = len(recvs)

    def body(*refs):
        for r_ref, p_ref, o_ref in zip(refs[:n_pairs], refs[n_pairs:2 * n_pairs], refs[2 * n_pairs:]):
            g = r_ref[0].astype(F32)
            for s in range(1, N_DEV):
                g = g + r_ref[s].astype(F32)
            o_ref[0] = g
            o_ref[1], o_ref[2], o_ref[3] = _adamw(g, p_ref[0], p_ref[1], p_ref[2])

    return _pcall(body, name=name,
                  out_shape=tuple(jax.ShapeDtypeStruct((4,) + wmv.shape[1:], F32) for wmv in wmvs),
                  compiler_params=_params(0))(*recvs, *wmvs)


_SMALL_F32 = (("loss", 1), ("norm_gain", D_MODEL), ("final_norm_gain", D_MODEL), ("b_glu", SSM_W),
              ("ssm_a_re", N_GROUPS * STATE), ("ssm_a_im", N_GROUPS * STATE), ("ssm_log_dt", N_GROUPS),
              ("ssm_d", N_GROUPS * GROUP), ("conv_w", 3 * CONV_W))
_SMALL_BF16 = (("ssm_b_re", N_GROUPS * STATE * GROUP), ("ssm_b_im", N_GROUPS * STATE * GROUP),
               ("ssm_c_re", N_GROUPS * STATE * GROUP), ("ssm_c_im", N_GROUPS * STATE * GROUP))
_PACK_UNIT = 2 * SUBLANES * LANES


def _pack_small(dicts, names):
    cols = []
    for name, size in names:
        flat = jnp.stack([d[name].reshape(-1) for d in dicts])
        padded = -(-size // _PACK_UNIT) * _PACK_UNIT
        cols.append(jnp.pad(flat, ((0, 0), (0, padded - size))).reshape(len(dicts), -1, LANES))
    return jnp.concatenate(cols, axis=1)


def _unpack_small(packed, names):
    out, r0 = {}, 0
    for name, size in names:
        nrows = -(-size // _PACK_UNIT) * 2 * SUBLANES
        out[name] = packed[:, r0:r0 + nrows].reshape(packed.shape[0], -1)[:, :size]
        r0 += nrows
    return out


def _block_diag(m4):
    eye = jnp.eye(SUBLANES, dtype=m4.dtype)
    j, g, a, b = m4.shape
    return jnp.einsum("jgab,gk->jgakb", m4, eye).reshape(j, g * a, g * b)


def _block_diag_extract(dense, a, b):
    d5 = dense.reshape(N_JBLK, SUBLANES, a, SUBLANES, b)
    return jnp.stack([d5[:, g, :, g, :] for g in range(SUBLANES)], axis=1)


def kernel(x, norm_gain, w_in, ssm_a_re, ssm_a_im, ssm_log_dt, ssm_b_re, ssm_b_im, ssm_c_re, ssm_c_im, ssm_d, w_glu, b_glu, conv_w, w_out, final_norm_gain, loss_target, m_norm_gain, m_w_in, m_ssm_a_re, m_ssm_a_im, m_ssm_log_dt, m_ssm_b_re, m_ssm_b_im, m_ssm_c_re, m_ssm_c_im, m_ssm_d, m_w_glu, m_b_glu, m_conv_w, m_w_out, m_final_norm_gain, v_norm_gain, v_w_in, v_ssm_a_re, v_ssm_a_im, v_ssm_log_dt, v_ssm_b_re, v_ssm_b_im, v_ssm_c_re, v_ssm_c_im, v_ssm_d, v_w_glu, v_b_glu, v_conv_w, v_w_out, v_final_norm_gain):
    n_seq, seq, _ = x.shape
    n = n_seq * seq
    me = 4 * lax.axis_index("x") + 2 * lax.axis_index("y") + lax.axis_index("c")

    rep = lambda a: jnp.repeat(a[0], GROUP, axis=1)
    a_re_r, a_im_r = rep(ssm_a_re), rep(ssm_a_im)
    log_dt = ssm_log_dt[0].reshape(N_GROUPS, 1)
    b_re2 = ssm_b_re[0].reshape(N_GROUPS, STATE * GROUP)
    b_im2 = ssm_b_im[0].reshape(N_GROUPS, STATE * GROUP)
    ab_re_r, ab_im_r, bb_re2, bb_im2 = _ssm_disc(a_re_r, a_im_r, log_dt, b_re2, b_im2)
    ab_re = ab_re_r[:, ::GROUP].reshape(1, N_GROUPS * STATE)
    ab_im = ab_im_r[:, ::GROUP].reshape(1, N_GROUPS * STATE)

    def bb_mat(bb2):
        t = jnp.transpose(bb2.reshape(N_JBLK, SUBLANES, STATE, GROUP), (0, 1, 3, 2))
        return _block_diag(t).astype(BF16)

    def c_mat(c3, sign):
        t = jnp.transpose(c3.reshape(N_JBLK, SUBLANES, GROUP, STATE), (0, 1, 3, 2))
        return _block_diag(sign * t).astype(BF16)

    bb_re_m, bb_im_m = bb_mat(bb_re2), bb_mat(bb_im2)
    c_re_m, c_imn_m = c_mat(ssm_c_re[0], 1.0), c_mat(ssm_c_im[0], -1.0)
    d_row = ssm_d[0].reshape(1, SSM_W)

    x2 = x.reshape(n, D_MODEL)
    tgt2 = loss_target.reshape(n, D_MODEL)
    mx, my, mc = lax.axis_index("x"), lax.axis_index("y"), lax.axis_index("c")
    chip_ids = [2 * cx + cy for cx, cy in ((mx, my), (1 - mx, my), (mx, 1 - my), (1 - mx, 1 - my))]
    arrival = chip_ids
    xn, proj, w_in_f = _in_proj(jnp.stack(arrival).astype(jnp.int32), x2, norm_gain, w_in[0].astype(BF16))
    u3 = proj.reshape(n_seq, seq, IN_COLS)
    conv_p = jnp.pad(conv_w[0], ((0, SUBLANES - 3), (0, LANES - CONV_COLS_PER_DEV)))
    s_re, s_im, y3, w_out_f, w_glu_f, conv_all = _ssm_fwd(
        u3, bb_re_m, bb_im_m, c_re_m, c_imn_m, d_row, ab_re, ab_im,
        w_out[0].astype(BF16), w_glu[0].astype(BF16), conv_p, n_seq, seq)
    conv8 = jnp.transpose(conv_all[:, :, :CONV_COLS_PER_DEV], (1, 0, 2)).reshape(SUBLANES, CONV_W)
    (dh2, dy, dzs, dbc, dzc, dyc, dw_out, dw_glu, loss_t, dgf, dbg, dcw) = _mix(
        x2, tgt2, y3.reshape(n, SSM_W), proj, final_norm_gain.reshape(1, D_MODEL), b_glu, conv8,
        w_glu_f, w_out_f, seq)

    du3, dc_re_d, dc_im_d, dbb_re_d, dbb_im_d, dab_re, dab_im, dd, r_out, r_glu = _ssm_bwd(
        dy.reshape(n_seq, seq, SSM_W), u3, s_re, s_im, bb_re_m, bb_im_m, c_re_m, c_imn_m, d_row, ab_re, ab_im,
        dw_out.reshape(N_DEV, OUT_ROWS_PER_DEV, D_MODEL), dw_glu.reshape(N_DEV, GLU_ROWS_PER_DEV, SSM_W), n_seq, seq)
    du = du3.reshape(n, SSM_W)
    g_c_re = _block_diag_extract(dc_re_d, GROUP, STATE).reshape(N_GROUPS, GROUP, STATE)
    g_c_im = -_block_diag_extract(dc_im_d, GROUP, STATE).reshape(N_GROUPS, GROUP, STATE)

    def bb_grad(dense):
        t = _block_diag_extract(dense, GROUP, STATE)
        return jnp.transpose(t, (0, 1, 3, 2)).reshape(N_GROUPS, STATE * GROUP)

    def ab_grad(row):
        z = jnp.zeros((N_GROUPS, STATE, GROUP), F32)
        return z.at[:, :, 0].set(row.reshape(N_GROUPS, STATE)).reshape(N_GROUPS, STATE * GROUP)

    g_are_r, g_aim_r, g_ldt, g_bre2, g_bim2 = _ssm_disc_bwd(
        a_re_r, a_im_r, log_dt, b_re2, b_im2, ab_grad(dab_re), ab_grad(dab_im),
        bb_grad(dbb_re_d), bb_grad(dbb_im_d))
    grad_x2, dproj, dg8 = _in_bwd(x2, dh2, du, dzs, dyc, proj, dbc, dzc, norm_gain, conv8, w_in_f, seq)
    small_grads = {"loss": loss_t[0:1, 0:1], "norm_gain": dg8[0:1], "final_norm_gain": dgf, "b_glu": dbg,
                   "ssm_a_re": g_are_r[:, ::GROUP], "ssm_a_im": g_aim_r[:, ::GROUP], "ssm_log_dt": g_ldt,
                   "ssm_b_re": g_bre2, "ssm_b_im": g_bim2, "ssm_c_re": g_c_re, "ssm_c_im": g_c_im,
                   "ssm_d": dd, "conv_w": dcw[0:3]}

    order = chip_ids[1:] + chip_ids[:1]
    own_in, rchip_in, r_small_f, r_small_b = _dw_in_exchange(
        jnp.stack(order).astype(jnp.int32), xn, dproj, _pack_small([small_grads], _SMALL_F32)[0],
        _pack_small([small_grads], _SMALL_BF16)[0].astype(BF16))

    def conv_full(shard):
        return lax.dynamic_update_slice(jnp.zeros((3, CONV_W), F32), shard[0], (0, me * CONV_COLS_PER_DEV))

    zero1 = jnp.zeros((1,), F32)
    triples = dict(loss=(zero1, zero1, zero1), norm_gain=(norm_gain, m_norm_gain, v_norm_gain),
                   final_norm_gain=(final_norm_gain, m_final_norm_gain, v_final_norm_gain),
                   b_glu=(b_glu, m_b_glu, v_b_glu), ssm_a_re=(ssm_a_re, m_ssm_a_re, v_ssm_a_re),
                   ssm_a_im=(ssm_a_im, m_ssm_a_im, v_ssm_a_im), ssm_log_dt=(ssm_log_dt, m_ssm_log_dt, v_ssm_log_dt),
                   ssm_b_re=(ssm_b_re, m_ssm_b_re, v_ssm_b_re), ssm_b_im=(ssm_b_im, m_ssm_b_im, v_ssm_b_im),
                   ssm_c_re=(ssm_c_re, m_ssm_c_re, v_ssm_c_re), ssm_c_im=(ssm_c_im, m_ssm_c_im, v_ssm_c_im),
                   ssm_d=(ssm_d, m_ssm_d, v_ssm_d),
                   conv_w=(conv_full(conv_w), conv_full(m_conv_w), conv_full(v_conv_w)))
    wmv = [{k: t[i] for k, t in triples.items()} for i in range(3)]

    res_in = _reduce_adam_w_in(own_in, rchip_in, w_in[0], m_w_in[0], v_w_in[0])
    res_out = _reduce_adam(r_out, w_out[0], m_w_out[0], v_w_out[0], "reduce_adam_w_out", OUT_ROWS_PER_DEV)
    res_glu = _reduce_adam(r_glu, w_glu[0], m_w_glu[0], v_w_glu[0], "reduce_adam_w_glu", GLU_ROWS_PER_DEV)
    res_f, res_b = _reduce_adam_stacked([r_small_f, r_small_b],
                                        [_pack_small(wmv, _SMALL_F32), _pack_small(wmv, _SMALL_BF16)],
                                        "reduce_adam_small")
    small = {**_unpack_small(res_f, _SMALL_F32), **_unpack_small(res_b, _SMALL_BF16)}
    loss = small["loss"][0, 0]

    shapes = dict(norm_gain=(1, D_MODEL), ssm_a_re=(1, N_GROUPS, STATE), ssm_a_im=(1, N_GROUPS, STATE),
                  ssm_log_dt=(1, N_GROUPS), ssm_b_re=(1, N_GROUPS, STATE, GROUP), ssm_b_im=(1, N_GROUPS, STATE, GROUP),
                  ssm_c_re=(1, N_GROUPS, GROUP, STATE), ssm_c_im=(1, N_GROUPS, GROUP, STATE),
                  ssm_d=(1, N_GROUPS, GROUP), b_glu=(1, SSM_W), final_norm_gain=(D_MODEL,))
    big = dict(w_in=res_in, w_glu=res_glu, w_out=res_out)
    small4 = {name: small[name].reshape((4,) + shp) for name, shp in shapes.items()}
    conv4 = lax.dynamic_slice(small["conv_w"].reshape(4, 1, 3, CONV_W), (0, 0, 0, me * CONV_COLS_PER_DEV),
                              (4, 1, 3, CONV_COLS_PER_DEV))

    def leaf(kind, name):
        if name in big:
            return big[name][kind][None]
        if name == "conv_w":
            return conv4[kind]
        return small4[name][kind]

    order = ["norm_gain", "w_in", "ssm_a_re", "ssm_a_im", "ssm_log_dt", "ssm_b_re", "ssm_b_im", "ssm_c_re",
             "ssm_c_im", "ssm_d", "w_glu", "b_glu", "conv_w", "w_out", "final_norm_gain"]
    outs = [loss, grad_x2.reshape(x.shape)]
    for kind in range(4):
        outs += [leaf(kind, name) for name in order]
    return tuple(outs)
```

```python
import functools
import math

import jax
import jax.numpy as jnp
from jax import lax
from jax.experimental import pallas as pl
from jax.experimental.pallas import tpu as pltpu

F32 = jnp.float32
BF16 = jnp.bfloat16

N_DEV = 8
D_MODEL = 1024
SSM_W = 512
CONV_W = 512
N_GROUPS = 32
GROUP = 16
STATE = 64
IN_COLS = 3072
SEG_U, SEG_ZS, SEG_H, SEG_BC, SEG_CC, SEG_ZC = range(6)
COLS_PER_DEV = IN_COLS // N_DEV
N_CHIP = N_DEV // 2
COLS_PER_CHIP = 2 * COLS_PER_DEV
OUT_ROWS_PER_DEV = D_MODEL // N_DEV
GLU_ROWS_PER_DEV = SSM_W // N_DEV
CONV_COLS_PER_DEV = CONV_W // N_DEV
EPS = 1e-6

N_JBLK = 4
JB_CH = SSM_W // N_JBLK
JB_ST = N_GROUPS * STATE // N_JBLK

ADAM_LR = 0.001
ADAM_B1 = 0.9
ADAM_B2 = 0.999
ADAM_EPS = 1e-08
ADAM_WD = 0.01
ADAM_STEP = 10

SUBLANES = 8
LANES = 128
VMEM_LIMIT = 48 * 1024 * 1024
TOK_TILE = 256
IN_TILE = 1024
SCAN_TILE = 512

MESH = pl.DeviceIdType.MESH
HBM_SPEC = pl.BlockSpec(memory_space=pltpu.HBM)


def _pcall(body, **kw):
    return pl.pallas_call(body, **kw)


def _params(n_grid):
    return pltpu.CompilerParams(dimension_semantics=("arbitrary",) * n_grid,
                                vmem_limit_bytes=VMEM_LIMIT)


def _dot(a, b):
    return jnp.dot(a, b, preferred_element_type=F32)


def _dot_nt(a, b):
    return lax.dot_general(a, b, (((1,), (1,)), ((), ())), preferred_element_type=F32)


def _dot_tn(a, b):
    return lax.dot_general(a, b, (((0,), (0,)), ((), ())), preferred_element_type=F32)


def _sigmoid(z):
    return 1.0 / (1.0 + jnp.exp(-z))


_GELU_C = math.sqrt(2.0 / math.pi)


def _gelu_and_grad(y):
    inner = _GELU_C * (y + 0.044715 * (y * y * y))
    t = jnp.tanh(inner)
    g = 0.5 * y * (1.0 + t)
    dg = 0.5 * (1.0 + t) + 0.5 * y * (1.0 - t * t) * (_GELU_C * (1.0 + 3.0 * 0.044715 * (y * y)))
    return g, dg


def _silu_and_grad(z):
    s = _sigmoid(z)
    return z * s, s * (1.0 + z * (1.0 - s))


def _shift_down(v, halo, k):
    rolled = pltpu.roll(v, k, 0)
    row = lax.broadcasted_iota(jnp.int32, v.shape, 0)
    for r in range(k):
        rolled = jnp.where(row == r, halo[SUBLANES - k + r:SUBLANES - k + r + 1, :], rolled)
    return rolled


def _shift_up(v, halo, k):
    n = v.shape[0]
    rolled = pltpu.roll(v, n - k, 0)
    row = lax.broadcasted_iota(jnp.int32, v.shape, 0)
    for r in range(k):
        rolled = jnp.where(row == n - k + r, halo[r:r + 1, :], rolled)
    return rolled


def _mesh_pos():
    return lax.axis_index("x"), lax.axis_index("y"), lax.axis_index("c")


def _direct_copies(srcs_for, out_refs, send_sems, recv_sems, loc_sems):
    x, y, c = _mesh_pos()
    me_id = 4 * x + 2 * y + c
    n_arr = len(out_refs)
    dsts = [r.at[me_id] for r in out_refs]
    own = srcs_for(me_id)
    mine = [pltpu.make_async_copy(own[a], dsts[a], loc_sems.at[a]) for a in range(n_arr)]
    sends = []
    for k in range(1, N_DEV):
        px, py, pc = x ^ ((k >> 2) & 1), y ^ ((k >> 1) & 1), c ^ (k & 1)
        src = srcs_for(4 * px + 2 * py + pc)
        for a in range(n_arr):
            sends.append(pltpu.make_async_remote_copy(
                src_ref=src[a], dst_ref=dsts[a],
                send_sem=send_sems.at[(k - 1) * n_arr + a], recv_sem=recv_sems.at[(k - 1) * n_arr + a],
                device_id=(px, py, pc), device_id_type=MESH))
    return mine, sends


class _TwoLevelGather:
    def __init__(self, srcs, slots, send_sems, recv_sems, loc_sems):
        self.srcs, self.slots, self.n_arr = srcs, slots, len(srcs)
        self.send_sems, self.recv_sems, self.loc_sems = send_sems, recv_sems, loc_sems
        x, y, c = _mesh_pos()
        self.c = c
        self.me, self.sib = (x, y, c), (x, y, 1 - c)
        self.chips = [(1 - x, y), (x, 1 - y), (1 - x, 1 - y)]

    def _copies(self, k, block, to, from_src=False):
        dev = 4 * block[0] + 2 * block[1] + block[2]
        return [pltpu.make_async_remote_copy(
            src_ref=self.srcs[a] if from_src else self.slots[a](dev), dst_ref=self.slots[a](dev),
            send_sem=self.send_sems.at[k * self.n_arr + a], recv_sem=self.recv_sems.at[k * self.n_arr + a],
            device_id=to, device_id_type=MESH) for a in range(self.n_arr)]

    def _local(self):
        dev = 4 * self.me[0] + 2 * self.me[1] + self.me[2]
        return [pltpu.make_async_copy(self.srcs[a], self.slots[a](dev), self.loc_sems.at[a])
                for a in range(self.n_arr)]

    def start(self, chips=(0, 1, 2)):
        for cp in self._local() + self._copies(0, self.me, self.sib, True):
            cp.start()
        self.start_to(chips)

    def start_to(self, chips):
        for j in chips:
            for cp in self._copies(1 + j, self.me, (*self.chips[j], self.c), True):
                cp.start()

    def wait_own(self):
        for cp in self._local():
            cp.wait()

    def wait_sibling(self):
        for cp in self._copies(0, self.sib, self.me):
            cp.wait_recv()

    def wait_and_pass_on(self, j):
        chip = self.chips[j]
        for cp in self._copies(1 + j, (*chip, self.c), self.me):
            cp.wait_recv()
        for cp in self._copies(4 + j, (*chip, self.c), self.sib):
            cp.start()

    def wait_passed_on(self, j):
        for cp in self._copies(4 + j, (*self.chips[j], 1 - self.c), self.me):
            cp.wait_recv()

    def wait_sends(self):
        for cp in self._copies(0, self.me, self.sib, True):
            cp.wait_send()
        for j, chip in enumerate(self.chips):
            for cp in self._copies(1 + j, self.me, (*chip, self.c), True) + self._copies(4 + j, (*chip, self.c), self.sib):
                cp.wait_send()

    def forward(self):
        for j in range(3):
            self.wait_and_pass_on(j)

    def finish(self):
        self.wait_sibling()
        for j in range(3):
            self.wait_passed_on(j)
        self.wait_sends()
        self.wait_own()


def _disc(a_re, a_im, log_dt, b_re, b_im):
    dt = jnp.exp(log_dt)
    mag = jnp.exp(a_re * dt)
    ab_re = mag * jnp.cos(a_im * dt)
    ab_im = mag * jnp.sin(a_im * dt)
    den = a_re * a_re + a_im * a_im
    p_re = ab_re - 1.0
    p_im = ab_im
    q_re = (p_re * a_re + p_im * a_im) / den
    q_im = (p_im * a_re - p_re * a_im) / den
    bb_re = q_re * b_re - q_im * b_im
    bb_im = q_re * b_im + q_im * b_re
    return ab_re, ab_im, bb_re, bb_im


def _ssm_disc(a_re_r, a_im_r, log_dt, b_re, b_im):
    def body(are, aim, ldt, bre, bim, o_abre, o_abim, o_bbre, o_bbim):
        outs = _disc(are[...], aim[...], ldt[...], bre[...], bim[...])
        for o, v in zip((o_abre, o_abim, o_bbre, o_bbim), outs):
            o[...] = v

    shp = jax.ShapeDtypeStruct(a_re_r.shape, F32)
    return _pcall(body, name="ssm_disc", out_shape=(shp,) * 4)(a_re_r, a_im_r, log_dt, b_re, b_im)


def _split3(v):
    hi = v.astype(BF16)
    r1 = v - hi.astype(F32)
    mid = r1.astype(BF16)
    lo = (r1 - mid.astype(F32)).astype(BF16)
    return hi, mid, lo


def _select_dot(v, sel):
    return sum(_dot(t, sel) for t in _split3(v))


PACK_ROWS = 72
PACK_W = 512
ROW_FINAL_GAIN, ROW_NORM_GAIN, ROW_BGLU_D, ROW_CONV, ROW_LOSS, ROW_S5 = 0, 8, 16, 24, 32, 40
LANE_A_RE, LANE_A_IM, LANE_LOG_DT = 0, 128, 256


def _ssm_disc_bwd_pack(a_re_r, a_im_r, log_dt, b_re, b_im, g_ab_re, g_ab_im, g_bbre, g_bbim,
                       loss_t, dg8, dgf, dbg, dd, dcw, dc_re_d, dc_im_d):
    def body(are, aim, ldt, bre, bim, gabre, gabim, gbbre, gbbim,
             loss_ref, dg8_ref, dgf_ref, dbg_ref, dd_ref, dcw_ref, dcre_ref, dcim_ref,
             p_ref, gc_ref, gb_ref):
        lane = lax.broadcasted_iota(jnp.int32, (STATE * GROUP, STATE), 0)
        col = lax.broadcasted_iota(jnp.int32, (STATE * GROUP, STATE), 1)
        gather16 = (lane // GROUP == col).astype(BF16)
        lane_t = lax.broadcasted_iota(jnp.int32, (STATE, STATE * GROUP), 1)
        row_t = lax.broadcasted_iota(jnp.int32, (STATE, STATE * GROUP), 0)
        place16 = (lane_t == row_t * GROUP).astype(BF16)

        _, vjp = jax.vjp(_disc, are[...], aim[...], ldt[...], bre[...], bim[...])
        d_are, d_aim, d_ldt, d_bre, d_bim = vjp((_select_dot(gabre[...], place16), _select_dot(gabim[...], place16),
                                                 gbbre[...], gbbim[...]))
        gb_ref[0] = d_bre.astype(BF16)
        gb_ref[1] = d_bim.astype(BF16)

        p_ref[...] = jnp.zeros_like(p_ref)
        half = D_MODEL // 2
        for r, src in ((ROW_FINAL_GAIN, dgf_ref), (ROW_NORM_GAIN, dg8_ref)):
            p_ref[r:r + 1, :] = src[0:1, 0:half]
            p_ref[r + 1:r + 2, :] = src[0:1, half:D_MODEL]
        p_ref[ROW_BGLU_D:ROW_BGLU_D + 1, :] = dbg_ref[...]
        p_ref[ROW_BGLU_D + 1:ROW_BGLU_D + 2, :] = dd_ref[...]
        p_ref[ROW_CONV:ROW_CONV + SUBLANES, :] = dcw_ref[...]
        p_ref[ROW_LOSS:ROW_LOSS + SUBLANES, 0:LANES] = loss_ref[...]
        s5 = slice(ROW_S5, ROW_S5 + N_GROUPS)
        p_ref[s5, LANE_A_RE:LANE_A_RE + STATE] = _select_dot(d_are, gather16)
        p_ref[s5, LANE_A_IM:LANE_A_IM + STATE] = _select_dot(d_aim, gather16)
        p_ref[s5, LANE_LOG_DT:LANE_LOG_DT + 1] = d_ldt

        for j in range(N_JBLK):
            for gi in range(SUBLANES):
                rows = slice(gi * GROUP, (gi + 1) * GROUP)
                cols = slice(gi * STATE, (gi + 1) * STATE)
                both = jnp.concatenate([dcre_ref[j, rows, cols], -dcim_ref[j, rows, cols]], axis=1)
                r0 = (j * SUBLANES + gi) * GROUP
                gc_ref[r0:r0 + GROUP, :] = both.astype(BF16)

    return _pcall(body, name="ssm_disc_bwd_pack",
                  out_shape=(jax.ShapeDtypeStruct((PACK_ROWS, PACK_W), F32),
                             jax.ShapeDtypeStruct((N_GROUPS * GROUP, 2 * STATE), BF16),
                             jax.ShapeDtypeStruct((2, N_GROUPS, STATE * GROUP), BF16)),
                  )(a_re_r, a_im_r, log_dt, b_re, b_im, g_ab_re, g_ab_im, g_bbre, g_bbim,
                    loss_t, dg8, dgf, dbg, dd, dcw, dc_re_d, dc_im_d)


def _in_proj(order, x2, g1, w_in_b):
    n = x2.shape[0]
    tm = min(IN_TILE, n)
    n_tiles = n // tm

    def body(order_ref, x_ref, g_ref, w_ref, xn_ref, proj_ref, wall_ref,
             xn_scr, wbuf, send_sems, recv_sems, loc_sems, out_sems):
        k = pl.program_id(0)
        i = pl.program_id(1)

        def slot(dev):
            return wbuf.at[dev // 2, :, pl.ds(pl.multiple_of((dev % 2) * COLS_PER_DEV, LANES), COLS_PER_DEV)]

        gather = _TwoLevelGather([w_ref], [slot], send_sems, recv_sems, loc_sems)

        @pl.when((k == 0) & (i == 0))
        def _():
            gather.start(chips=(0, 1))

        def own_chip():
            gather.wait_own()
            gather.wait_sibling()

        def other_chip(j):
            gather.wait_and_pass_on(j)
            if j == 0:
                gather.start_to((2,))
            gather.wait_passed_on(j)

        arrivals = [own_chip] + [functools.partial(other_chip, j) for j in range(3)]
        for kk, arrived in enumerate(arrivals):
            @pl.when((k == kk) & (i == 0))
            def _(arrived=arrived):
                arrived()

        rows = pl.ds(pl.multiple_of(i * tm, tm), tm)

        @pl.when(k == 0)
        def _():
            x = x_ref[...]
            r = lax.rsqrt(jnp.mean(x * x, axis=-1, keepdims=True) + EPS)
            xn = ((x * r) * g_ref[...]).astype(BF16)
            xn_scr[rows, :] = xn
            xn_ref[...] = xn

        proj_ref[...] = _dot(xn_scr[rows, :], wbuf[order_ref[k]])

        @pl.when((k == N_CHIP - 1) & (i == n_tiles - 1))
        def _():
            gather.wait_sends()
            outs = [pltpu.make_async_copy(wbuf.at[q], wall_ref.at[:, q * COLS_PER_CHIP:(q + 1) * COLS_PER_CHIP],
                                          out_sems.at[q]) for q in range(N_CHIP)]
            for cp in outs:
                cp.start()
            for cp in outs:
                cp.wait()

    tile_once = lambda k, i, order: (jnp.where(k == 0, i, n_tiles - 1), 0)
    grid_spec = pltpu.PrefetchScalarGridSpec(
        num_scalar_prefetch=1, grid=(N_CHIP, n_tiles),
        in_specs=[pl.BlockSpec((tm, D_MODEL), tile_once),
                  pl.BlockSpec((1, D_MODEL), lambda k, i, order: (0, 0)),
                  HBM_SPEC],
        out_specs=(pl.BlockSpec((tm, D_MODEL), tile_once),
                   pl.BlockSpec((tm, COLS_PER_CHIP), lambda k, i, order: (i, order[k])),
                   HBM_SPEC),
        scratch_shapes=[pltpu.VMEM((n, D_MODEL), BF16), pltpu.VMEM((N_CHIP, D_MODEL, COLS_PER_CHIP), BF16),
                        pltpu.SemaphoreType.DMA((7,)), pltpu.SemaphoreType.DMA((7,)), pltpu.SemaphoreType.DMA((1,)),
                        pltpu.SemaphoreType.DMA((N_CHIP,))])
    return _pcall(
        body, name="in_proj", grid_spec=grid_spec,
        out_shape=(jax.ShapeDtypeStruct((n, D_MODEL), BF16), jax.ShapeDtypeStruct((n, IN_COLS), F32),
                   jax.ShapeDtypeStruct((D_MODEL, IN_COLS), BF16)),
        compiler_params=_params(2),
    )(order, x2, g1, w_in_b)


def _cmul(p, q):
    return p[0] * q[0] - p[1] * q[1], p[0] * q[1] + p[1] * q[0]


def _scan_tables(ar, ai, width, reverse):
    pows = [(ar, ai)]
    for _ in range(SUBLANES - 1):
        pows.append(_cmul(pows[-1], (ar, ai)))
    row = lax.broadcasted_iota(jnp.int32, (SUBLANES, width), 0)

    def bc(v):
        return jnp.broadcast_to(v, (SUBLANES, width))

    levels = []
    for k in (1, 2, 4):
        keep = (row <= SUBLANES - 1 - k) if reverse else (row >= k)
        levels.append((jnp.where(keep, bc(pows[k - 1][0]), 0.0), jnp.where(keep, bc(pows[k - 1][1]), 0.0)))
    cre = jnp.zeros((SUBLANES, width), F32)
    cim = jnp.zeros((SUBLANES, width), F32)
    for r in range(SUBLANES):
        e = (SUBLANES - r) if reverse else (r + 1)
        cre = jnp.where(row == r, bc(pows[e - 1][0]), cre)
        cim = jnp.where(row == r, bc(pows[e - 1][1]), cim)
    return levels, (cre, cim)


def _load_chunked(src_ref, b, dst_ref, n_rows):
    n_blk = n_rows // SUBLANES
    for i in range(n_blk):
        dst_ref[b, i * SUBLANES:(i + 1) * SUBLANES, :] = src_ref[b, pl.ds(i, SUBLANES, stride=n_blk), :]


def _store_chunked(val, dst_ref, b, n_rows):
    n_blk = n_rows // SUBLANES
    for i in range(n_blk):
        dst_ref[b, pl.ds(i, SUBLANES, stride=n_blk), :] = val[i * SUBLANES:(i + 1) * SUBLANES, :]


def _chunk_scan(re_ref, im_ref, bs, car_ref, ar, ai, n_rows, reverse, on_block=None):
    width = re_ref.shape[2]
    n_blk = n_rows // SUBLANES
    shape = (SUBLANES, width)
    abr = jnp.broadcast_to(ar, shape)
    abi = jnp.broadcast_to(ai, shape)
    order = list(range(n_blk - 1, -1, -1)) if reverse else list(range(n_blk))

    def blk(ref, b, i):
        return ref[b, i * SUBLANES:(i + 1) * SUBLANES, :]

    def step(state, b, i):
        sr, si = state
        return abr * sr - abi * si + blk(re_ref, b, i), abr * si + abi * sr + blk(im_ref, b, i)

    finals = {b: (blk(re_ref, b, order[0]), blk(im_ref, b, order[0])) for b in bs}
    for i in order[1:]:
        for b in bs:
            finals[b] = step(finals[b], b, i)

    mr, mi = ar, ai
    for _ in range(n_blk.bit_length() - 1):
        mr, mi = _cmul((mr, mi), (mr, mi))
    levels, _ = _scan_tables(mr, mi, width, reverse)
    mbr = jnp.broadcast_to(mr, shape)
    mbi = jnp.broadcast_to(mi, shape)
    row = lax.broadcasted_iota(jnp.int32, shape, 0)
    edge_in = SUBLANES - 1 if reverse else 0
    edge_out = 0 if reverse else SUBLANES - 1
    sh1 = SUBLANES - 1 if reverse else 1
    states = {}
    for b in bs:
        fr, fi = finals[b]
        gr = jnp.where(row == edge_in, jnp.broadcast_to(car_ref[b, 0:1, :], shape), pltpu.roll(fr, sh1, 0))
        gi = jnp.where(row == edge_in, jnp.broadcast_to(car_ref[b, 1:2, :], shape), pltpu.roll(fi, sh1, 0))
        for (lr, li), k in zip(levels, (1, 2, 4)):
            sh = (SUBLANES - k) if reverse else k
            sr = pltpu.roll(gr, sh, 0)
            si = pltpu.roll(gi, sh, 0)
            gr, gi = gr + (lr * sr - li * si), gi + (lr * si + li * sr)
        car_ref[b, 0:1, :] = (fr + (mbr * gr - mbi * gi))[edge_out:edge_out + 1, :]
        car_ref[b, 1:2, :] = (fi + (mbr * gi + mbi * gr))[edge_out:edge_out + 1, :]
        states[b] = (gr, gi)

    for i in order:
        for b in bs:
            states[b] = step(states[b], b, i)
            re_ref[b, i * SUBLANES:(i + 1) * SUBLANES, :] = states[b][0]
            im_ref[b, i * SUBLANES:(i + 1) * SUBLANES, :] = states[b][1]
            if on_block is not None:
                on_block(b, i, *states[b])


def _ssm_fwd(u, bb_re, bb_im, c_re_t, c_imn_t, d_row, ab_re, ab_im, w_out_b, w_glu_b, conv_p, n_seq, seq):
    tt = min(SCAN_TILE, seq)
    nt = seq // tt

    def body(u_ref, bbre, bbim, cre, cimn, d_ref, are, aim, wout_ref, wglu_ref, cw_ref,
             sre_ref, sim_ref, y_ref, oout_ref, oglu_ref, ocw_ref,
             up_ref, car_ref, send_sems, recv_sems, loc_sems):
        j = pl.program_id(0)
        t = pl.program_id(1)
        gather = _TwoLevelGather(
            [wout_ref, wglu_ref, cw_ref],
            [lambda dev: oout_ref.at[pl.ds(pl.multiple_of(dev * OUT_ROWS_PER_DEV, OUT_ROWS_PER_DEV), OUT_ROWS_PER_DEV), :],
             lambda dev: oglu_ref.at[pl.ds(pl.multiple_of(dev * GLU_ROWS_PER_DEV, GLU_ROWS_PER_DEV), GLU_ROWS_PER_DEV), :],
             lambda dev: ocw_ref.at[dev]],
            send_sems, recv_sems, loc_sems)

        @pl.when((j == 0) & (t == 0))
        def _():
            gather.start()

        @pl.when((j == N_JBLK // 2) & (t == 0))
        def _():
            gather.forward()

        @pl.when(t == 0)
        def _():
            car_ref[...] = jnp.zeros_like(car_ref)

        bs = list(range(n_seq))
        for b in bs:
            _load_chunked(u_ref, b, up_ref, tt)
        for b in bs:
            ub = up_ref[b].astype(BF16)
            sre_ref[b] = _dot(ub, bbre[0])
            sim_ref[b] = _dot(ub, bbim[0])
            _chunk_scan(sre_ref, sim_ref, [b], car_ref, are[...], aim[...], tt, reverse=False)
        for b in bs:
            yp = (_dot(sre_ref[b].astype(BF16), cre[0]) + _dot(sim_ref[b].astype(BF16), cimn[0])
                  + d_ref[...] * up_ref[b])
            _store_chunked(yp, y_ref, b, tt)

        @pl.when((j == N_JBLK - 1) & (t == nt - 1))
        def _():
            gather.finish()

    tok = lambda j, t: (0, t, j)
    blk3 = lambda j, t: (j, 0, 0)
    row = lambda j, t: (0, j)
    st = jax.ShapeDtypeStruct((n_seq, seq, N_JBLK * JB_ST), F32)
    n_arr = 3
    return _pcall(
        body, name="ssm_fwd", grid=(N_JBLK, nt),
        out_shape=(st, st, jax.ShapeDtypeStruct((n_seq, seq, SSM_W), F32),
                   jax.ShapeDtypeStruct((D_MODEL, D_MODEL), BF16), jax.ShapeDtypeStruct((SSM_W, SSM_W), BF16),
                   jax.ShapeDtypeStruct((N_DEV, SUBLANES, LANES), F32)),
        in_specs=[pl.BlockSpec((n_seq, tt, JB_CH), tok),
                  pl.BlockSpec((1, JB_CH, JB_ST), blk3), pl.BlockSpec((1, JB_CH, JB_ST), blk3),
                  pl.BlockSpec((1, JB_ST, JB_CH), blk3), pl.BlockSpec((1, JB_ST, JB_CH), blk3),
                  pl.BlockSpec((1, JB_CH), row), pl.BlockSpec((1, JB_ST), row), pl.BlockSpec((1, JB_ST), row),
                  HBM_SPEC, HBM_SPEC, HBM_SPEC],
        out_specs=(pl.BlockSpec((n_seq, tt, JB_ST), tok), pl.BlockSpec((n_seq, tt, JB_ST), tok),
                   pl.BlockSpec((n_seq, tt, JB_CH), tok), HBM_SPEC, HBM_SPEC, HBM_SPEC),
        scratch_shapes=[pltpu.VMEM((n_seq, tt, JB_CH), F32), pltpu.VMEM((n_seq, SUBLANES, JB_ST), F32),
                        pltpu.SemaphoreType.DMA((7 * n_arr,)), pltpu.SemaphoreType.DMA((7 * n_arr,)),
                        pltpu.SemaphoreType.DMA((n_arr,))],
        compiler_params=_params(2),
    )(u, bb_re, bb_im, c_re_t, c_imn_t, d_row, ab_re, ab_im, w_out_b, w_glu_b, conv_p)


def _ssm_bwd(dy, u, s_re, s_im, bb_re, bb_im, c_re_t, c_imn_t, d_row, ab_re, ab_im, g_out, g_glu, n_seq, seq):
    tt = min(SCAN_TILE, seq)
    nt = seq // tt
    rows8 = tt // SUBLANES

    def body(dy_ref, u_ref, sre_ref, sim_ref, pre_ref, pim_ref, bbre, bbim, cre, cimn, d_ref, are, aim,
             gout_ref, gglu_ref,
             du_ref, dcre_ref, dcim_ref, dbbre_ref, dbbim_ref, dare_ref, daim_ref, dd_ref, rout_ref, rglu_ref,
             lre_ref, lim_ref, dyp_ref, up_ref, car_ref, send_sems, recv_sems, loc_sems):
        j = pl.program_id(0)
        tr = pl.program_id(1)

        def exchange():
            return _direct_copies(lambda pid: [gout_ref.at[pid], gglu_ref.at[pid]], [rout_ref, rglu_ref],
                                  send_sems, recv_sems, loc_sems)

        @pl.when((j == 0) & (tr == 0))
        def _():
            mine, sends = exchange()
            for cp in mine + sends:
                cp.start()

        @pl.when(tr == 0)
        def _():
            car_ref[...] = jnp.zeros_like(car_ref)
            for r in (dcre_ref, dcim_ref, dbbre_ref, dbbim_ref, dare_ref, daim_ref, dd_ref):
                r[...] = jnp.zeros_like(r)

        first = tr == nt - 1
        row = lax.broadcasted_iota(jnp.int32, (SUBLANES, JB_ST), 0)
        n_blk = tt // SUBLANES
        bs = list(range(n_seq))
        for b in bs:
            _load_chunked(dy_ref, b, dyp_ref, tt)
            _load_chunked(u_ref, b, up_ref, tt)
        for b in bs:
            dyb = dyp_ref[b].astype(BF16)
            lre_ref[b] = _dot_nt(dyb, cre[0])
            lim_ref[b] = _dot_nt(dyb, cimn[0])
        acc = {b: [jnp.zeros((SUBLANES, JB_ST), F32), jnp.zeros((SUBLANES, JB_ST), F32)] for b in bs}

        def on_block(b, i, lr, li):
            if i > 0:
                spr = sre_ref[b, (i - 1) * SUBLANES:i * SUBLANES, :]
                spi = sim_ref[b, (i - 1) * SUBLANES:i * SUBLANES, :]
            else:
                hr = jnp.where(first, 0.0, pre_ref[b, SUBLANES - 1:SUBLANES, :])
                hi = jnp.where(first, 0.0, pim_ref[b, SUBLANES - 1:SUBLANES, :])
                last_r = sre_ref[b, (n_blk - 1) * SUBLANES:n_blk * SUBLANES, :]
                last_i = sim_ref[b, (n_blk - 1) * SUBLANES:n_blk * SUBLANES, :]
                spr = jnp.where(row == 0, jnp.broadcast_to(hr, row.shape), pltpu.roll(last_r, 1, 0))
                spi = jnp.where(row == 0, jnp.broadcast_to(hi, row.shape), pltpu.roll(last_i, 1, 0))
            acc[b][0] = acc[b][0] + (lr * spr + li * spi)
            acc[b][1] = acc[b][1] + (li * spr - lr * spi)

        _chunk_scan(lre_ref, lim_ref, bs, car_ref, are[...], -aim[...], tt, reverse=True, on_block=on_block)
        for b in bs:
            dare_ref[...] += jnp.sum(acc[b][0], axis=0, keepdims=True)
            daim_ref[...] += jnp.sum(acc[b][1], axis=0, keepdims=True)
            dyp = dyp_ref[b]
            up = up_ref[b]
            dyb = dyp.astype(BF16)
            ub = up.astype(BF16)
            lrb = lre_ref[b].astype(BF16)
            lib = lim_ref[b].astype(BF16)
            dup = d_ref[...] * dyp + _dot_nt(lrb, bbre[0]) + _dot_nt(lib, bbim[0])
            _store_chunked(dup, du_ref, b, tt)
            dbbre_ref[0] += _dot_tn(ub, lrb)
            dbbim_ref[0] += _dot_tn(ub, lib)
            dcre_ref[0] += _dot_tn(dyb, sre_ref[b].astype(BF16))
            dcim_ref[0] += _dot_tn(dyb, sim_ref[b].astype(BF16))
            dd_ref[...] += jnp.sum(dyp * up, axis=0, keepdims=True)

        @pl.when((j == N_JBLK - 1) & (tr == nt - 1))
        def _():
            mine, sends = exchange()
            for cp in sends + mine:
                cp.wait()

    tok = lambda j, t: (0, nt - 1 - t, j)
    halo = lambda j, t: (0, jnp.maximum((nt - 1 - t) * rows8 - 1, 0), j)
    blk3 = lambda j, t: (j, 0, 0)
    row1 = lambda j, t: (0, j)
    acc_shape = jax.ShapeDtypeStruct((N_JBLK, JB_CH, JB_ST), F32)
    return _pcall(
        body, name="ssm_bwd", grid=(N_JBLK, nt),
        out_shape=(jax.ShapeDtypeStruct((n_seq, seq, SSM_W), F32), acc_shape, acc_shape, acc_shape, acc_shape,
                   jax.ShapeDtypeStruct((1, N_JBLK * JB_ST), F32), jax.ShapeDtypeStruct((1, N_JBLK * JB_ST), F32),
                   jax.ShapeDtypeStruct((1, SSM_W), F32),
                   jax.ShapeDtypeStruct((N_DEV,) + g_out.shape[1:], F32),
                   jax.ShapeDtypeStruct((N_DEV,) + g_glu.shape[1:], F32)),
        in_specs=[pl.BlockSpec((n_seq, tt, JB_CH), tok), pl.BlockSpec((n_seq, tt, JB_CH), tok),
                  pl.BlockSpec((n_seq, tt, JB_ST), tok), pl.BlockSpec((n_seq, tt, JB_ST), tok),
                  pl.BlockSpec((n_seq, SUBLANES, JB_ST), halo), pl.BlockSpec((n_seq, SUBLANES, JB_ST), halo),
                  pl.BlockSpec((1, JB_CH, JB_ST), blk3), pl.BlockSpec((1, JB_CH, JB_ST), blk3),
                  pl.BlockSpec((1, JB_ST, JB_CH), blk3), pl.BlockSpec((1, JB_ST, JB_CH), blk3),
                  pl.BlockSpec((1, JB_CH), row1), pl.BlockSpec((1, JB_ST), row1), pl.BlockSpec((1, JB_ST), row1),
                  HBM_SPEC, HBM_SPEC],
        out_specs=(pl.BlockSpec((n_seq, tt, JB_CH), tok),
                   pl.BlockSpec((1, JB_CH, JB_ST), blk3), pl.BlockSpec((1, JB_CH, JB_ST), blk3),
                   pl.BlockSpec((1, JB_CH, JB_ST), blk3), pl.BlockSpec((1, JB_CH, JB_ST), blk3),
                   pl.BlockSpec((1, JB_ST), row1), pl.BlockSpec((1, JB_ST), row1), pl.BlockSpec((1, JB_CH), row1),
                   HBM_SPEC, HBM_SPEC),
        scratch_shapes=[pltpu.VMEM((n_seq, tt, JB_ST), F32), pltpu.VMEM((n_seq, tt, JB_ST), F32),
                        pltpu.VMEM((n_seq, tt, JB_CH), F32), pltpu.VMEM((n_seq, tt, JB_CH), F32),
                        pltpu.VMEM((n_seq, SUBLANES, JB_ST), F32),
                        pltpu.SemaphoreType.DMA((7 * 2,)), pltpu.SemaphoreType.DMA((7 * 2,)),
                        pltpu.SemaphoreType.DMA((2,))],
        compiler_params=_params(2),
    )(dy, u, s_re, s_im, s_re, s_im, bb_re, bb_im, c_re_t, c_imn_t, d_row, ab_re, ab_im, g_out, g_glu)


def _mix(x2, tgt2, y, proj, gf, b_glu, conv8, w_glu_f, w_out_f, seq):
    n = x2.shape[0]
    tm = TOK_TILE
    tiles_per_seq = seq // tm
    rows8 = tm // SUBLANES

    def body(x_ref, t_ref, y_ref, zs_ref, h_ref, bc_ref, cc_ref, zc_ref, hp_ref, ccp_ref,
             gf_ref, bg_ref, cw_ref, wg_ref, wo_ref,
             dh2_ref, dy_ref, dzs_ref, dbc_ref, dzc_ref, dyc_ref,
             dwo_ref, dwg_ref, loss_ref, dgf_ref, dbg_ref, dcw_ref):
        i = pl.program_id(0)

        @pl.when(i == 0)
        def _():
            for r in (dwo_ref, dwg_ref, loss_ref, dgf_ref, dbg_ref, dcw_ref):
                r[...] = jnp.zeros_like(r)

        yv = y_ref[...]
        y1, dgelu = _gelu_and_grad(yv)
        y1b = y1.astype(BF16)
        gate = _sigmoid(_dot(y1b, wg_ref[...]) + bg_ref[...])
        y2 = y1 * gate
        szs, dszs = _silu_and_grad(zs_ref[...])
        yssm = y2 * szs
        hv = h_ref[...]
        ccv = cc_ref[...]
        bcv = bc_ref[...]
        v = ccv * hv
        first = (i % tiles_per_seq) == 0
        vhalo = jnp.where(first, 0.0, ccp_ref[...] * hp_ref[...])
        v1 = _shift_down(v, vhalo, 1)
        v2 = _shift_down(v, vhalo, 2)
        w0 = cw_ref[0:1, :]
        w1 = cw_ref[1:2, :]
        w2 = cw_ref[2:3, :]
        yc = w0 * v2 + w1 * v1 + w2 * v
        szc, dszc = _silu_and_grad(zc_ref[...])
        yconv = (bcv * yc) * szc
        ysb = yssm.astype(BF16)
        ycb = yconv.astype(BF16)
        h2 = x_ref[...] + _dot(ysb, wo_ref[0:SSM_W, :]) + _dot(ycb, wo_ref[SSM_W:, :])
        r2 = lax.rsqrt(jnp.mean(h2 * h2, axis=-1, keepdims=True) + EPS)
        hn = h2 * r2
        gfv = gf_ref[...]
        err = hn * gfv - t_ref[...]
        loss_ref[...] += 0.5 * jnp.sum(jnp.mean(err * err, axis=-1, keepdims=True))
        dout = err * (1.0 / D_MODEL)
        dgf_ref[...] += jnp.sum(dout * hn, axis=0, keepdims=True)
        dn = dout * gfv
        dh2 = r2 * (dn - hn * jnp.mean(dn * hn, axis=-1, keepdims=True))
        dh2_ref[...] = dh2
        dh2b = dh2.astype(BF16)
        dwo_ref[0:SSM_W, :] += _dot_tn(ysb, dh2b)
        dwo_ref[SSM_W:, :] += _dot_tn(ycb, dh2b)
        dyssm = _dot_nt(dh2b, wo_ref[0:SSM_W, :])
        dyconv = _dot_nt(dh2b, wo_ref[SSM_W:, :])
        dy2 = dyssm * szs
        dzs_ref[...] = dyssm * y2 * dszs
        dgp = dy2 * y1 * (gate * (1.0 - gate))
        dgpb = dgp.astype(BF16)
        dy1 = dy2 * gate + _dot_nt(dgpb, wg_ref[...])
        dwg_ref[...] += _dot_tn(y1b, dgpb)
        dbg_ref[...] += jnp.sum(dgp, axis=0, keepdims=True)
        dy_ref[...] = dy1 * dgelu
        dbc_ref[...] = dyconv * yc * szc
        dyc = dyconv * bcv * szc
        dyc_ref[...] = dyc
        dzc_ref[...] = dyconv * bcv * yc * dszc
        dcw_ref[0:1, :] += jnp.sum(dyc * v2, axis=0, keepdims=True)
        dcw_ref[1:2, :] += jnp.sum(dyc * v1, axis=0, keepdims=True)
        dcw_ref[2:3, :] += jnp.sum(dyc * v, axis=0, keepdims=True)

    tile_d = pl.BlockSpec((tm, D_MODEL), lambda i: (i, 0))
    tile_s = pl.BlockSpec((tm, SSM_W), lambda i: (i, 0))
    seg_of = lambda c: pl.BlockSpec((tm, SSM_W), lambda i: (i, c))
    halo_of = lambda c: pl.BlockSpec((SUBLANES, SSM_W), lambda i: (jnp.maximum(i * rows8 - 1, 0), c))
    const = lambda shape: pl.BlockSpec(shape, lambda i: (0,) * len(shape))
    seg = jax.ShapeDtypeStruct((n, SSM_W), F32)
    return _pcall(
        body, name="mix", grid=(n // tm,),
        out_shape=(jax.ShapeDtypeStruct((n, D_MODEL), F32), seg, seg, seg, seg, seg,
                   jax.ShapeDtypeStruct((D_MODEL, D_MODEL), F32), jax.ShapeDtypeStruct((SSM_W, SSM_W), F32),
                   jax.ShapeDtypeStruct((SUBLANES, LANES), F32), jax.ShapeDtypeStruct((1, D_MODEL), F32),
                   jax.ShapeDtypeStruct((1, SSM_W), F32), jax.ShapeDtypeStruct((SUBLANES, CONV_W), F32)),
        in_specs=[tile_d, tile_d, tile_s, seg_of(SEG_ZS), seg_of(SEG_H), seg_of(SEG_BC), seg_of(SEG_CC), seg_of(SEG_ZC),
                  halo_of(SEG_H), halo_of(SEG_CC),
                  const((1, D_MODEL)), const((1, SSM_W)), const((SUBLANES, CONV_W)),
                  const((SSM_W, SSM_W)), const((D_MODEL, D_MODEL))],
        out_specs=(tile_d, tile_s, tile_s, tile_s, tile_s, tile_s,
                   const((D_MODEL, D_MODEL)), const((SSM_W, SSM_W)), const((SUBLANES, LANES)),
                   const((1, D_MODEL)), const((1, SSM_W)), const((SUBLANES, CONV_W))),
        compiler_params=_params(1),
    )(x2, tgt2, y, proj, proj, proj, proj, proj, proj, proj, gf, b_glu, conv8, w_glu_f, w_out_f)


def _in_bwd(x2, dh2, du, dzs, dyc, proj, dbc, dzc, g1, conv8, w_full, seq):
    n = x2.shape[0]
    tm = TOK_TILE
    n_tiles = n // tm
    tiles_per_seq = seq // tm
    rows8 = tm // SUBLANES
    n_blk8 = n // SUBLANES

    def body(x_ref, dh2_ref, du_ref, dzs_ref, dyc_ref, dycn_ref, h_ref, cc_ref, dbc_ref, dzc_ref,
             g_ref, cw_ref, w_ref, gx_ref, dp_ref, dg_ref):
        i = pl.program_id(0)

        @pl.when(i == 0)
        def _():
            dg_ref[...] = jnp.zeros_like(dg_ref)

        dyc = dyc_ref[...]
        last = (i % tiles_per_seq) == tiles_per_seq - 1
        nhalo = jnp.where(last, 0.0, dycn_ref[...])
        dv = (cw_ref[2:3, :] * dyc + cw_ref[1:2, :] * _shift_up(dyc, nhalo, 1)
              + cw_ref[0:1, :] * _shift_up(dyc, nhalo, 2))
        parts = (du_ref[...], dzs_ref[...], dv * cc_ref[...], dbc_ref[...], dv * h_ref[...], dzc_ref[...])
        dxn = jnp.zeros((tm, D_MODEL), F32)
        for k, p in enumerate(parts):
            pb = p.astype(BF16)
            dp_ref[:, k * SSM_W:(k + 1) * SSM_W] = pb
            dxn = dxn + _dot_nt(pb, w_ref[:, k * SSM_W:(k + 1) * SSM_W])
        x = x_ref[...]
        r = lax.rsqrt(jnp.mean(x * x, axis=-1, keepdims=True) + EPS)
        xh = x * r
        dg_ref[...] += jnp.sum(dxn * xh, axis=0, keepdims=True)
        dn = dxn * g_ref[...]
        gx_ref[...] = dh2_ref[...] + r * (dn - xh * jnp.mean(dn * xh, axis=-1, keepdims=True))

    tile_d = pl.BlockSpec((tm, D_MODEL), lambda i: (i, 0))
    tile_s = pl.BlockSpec((tm, SSM_W), lambda i: (i, 0))
    seg_of = lambda c: pl.BlockSpec((tm, SSM_W), lambda i: (i, c))
    nhalo = pl.BlockSpec((SUBLANES, SSM_W), lambda i: (jnp.minimum((i + 1) * rows8, n_blk8 - 1), 0))
    const = lambda shape: pl.BlockSpec(shape, lambda i: (0,) * len(shape))
    return _pcall(
        body, name="in_bwd", grid=(n_tiles,),
        out_shape=(jax.ShapeDtypeStruct((n, D_MODEL), F32), jax.ShapeDtypeStruct((n, IN_COLS), BF16),
                   jax.ShapeDtypeStruct((SUBLANES, D_MODEL), F32)),
        in_specs=[tile_d, tile_d, tile_s, tile_s, tile_s, nhalo, seg_of(SEG_H), seg_of(SEG_CC), tile_s, tile_s,
                  const((1, D_MODEL)), const((SUBLANES, CONV_W)), const((D_MODEL, IN_COLS))],
        out_specs=(tile_d, pl.BlockSpec((tm, IN_COLS), lambda i: (i, 0)), const((SUBLANES, D_MODEL))),
        compiler_params=_params(1),
    )(x2, dh2, du, dzs, dyc, dyc, proj, proj, dbc, dzc, g1, conv8, w_full)


def _dw_in_exchange(order, xn, dproj, smalls):
    n = xn.shape[0]
    tk = 512
    nk = n // tk
    piece = (D_MODEL, COLS_PER_DEV)
    n_small = len(smalls)

    def body(order_ref, xn_ref, dp_ref, *refs):
        del order_ref
        sm_refs = refs[:n_small]
        own_ref, rchip_ref = refs[n_small:n_small + 2]
        rsm_refs = refs[n_small + 2:2 * n_small + 2]
        (acc, stage, sbuf, give_send, give_recv, keep_send, keep_recv,
         sm_send, sm_recv, sm_loc) = refs[2 * n_small + 2:]
        s = pl.program_id(0)
        x, y, c = _mesh_pos()
        sib = (x, y, 1 - c)
        chips = [(1 - x, 1 - y), (1 - x, y), (x, 1 - y)]
        gather = _TwoLevelGather(list(sm_refs), [functools.partial(lambda r, dev: r.at[dev], r) for r in rsm_refs],
                                 sm_send, sm_recv, sm_loc)

        def half(i, core):
            return acc.at[i % 2, :, pl.ds(pl.multiple_of(core * COLS_PER_DEV, LANES), COLS_PER_DEV)]

        def give(i):
            return pltpu.make_async_remote_copy(src_ref=half(i, 1 - c), dst_ref=stage.at[i], send_sem=give_send.at[i],
                                                recv_sem=give_recv.at[i], device_id=sib, device_id_type=MESH)

        def keep(i):
            return pltpu.make_async_remote_copy(src_ref=sbuf.at[i], dst_ref=rchip_ref.at[i], send_sem=keep_send.at[i],
                                                recv_sem=keep_recv.at[i], device_id=(*chips[i], c), device_id_type=MESH)

        def chip_sum(i):
            give(i).wait_recv()
            mine = [acc[i % 2, :, cc * COLS_PER_DEV:(cc + 1) * COLS_PER_DEV] for cc in range(2)]
            return jnp.where(c == 0, mine[0], mine[1]) + stage[i]

        @pl.when(s == 0)
        def _():
            gather.start()

        @pl.when(s == N_CHIP // 2)
        def _():
            gather.forward()

        for k in range(2, N_CHIP):
            @pl.when(s == k)
            def _(k=k):
                give(k - 2).wait_send()

        slot = s % 2
        acc[slot] = _dot_tn(xn_ref[pl.ds(0, tk), :], dp_ref[pl.ds(0, tk), :])

        def kstep(kk, carry):
            off = pl.multiple_of(kk * tk, tk)
            acc[slot] += _dot_tn(xn_ref[pl.ds(off, tk), :], dp_ref[pl.ds(off, tk), :])
            return carry

        n_first = min(nk, 3)
        lax.fori_loop(1, n_first, kstep, 0)
        for k in range(1, N_CHIP):
            @pl.when(s == k)
            def _(k=k):
                sbuf[k - 1] = chip_sum(k - 1).astype(BF16)
                keep(k - 1).start()

        lax.fori_loop(n_first, nk, kstep, 0)

        for k in range(N_CHIP):
            @pl.when(s == k)
            def _(k=k):
                give(k).start()

        @pl.when(s == N_CHIP - 1)
        def _():
            own_ref[...] = chip_sum(N_CHIP - 1)
            give(N_CHIP - 2).wait_send()
            give(N_CHIP - 1).wait_send()
            for i in range(3):
                keep(i).wait()
            gather.finish()

    grid_spec = pltpu.PrefetchScalarGridSpec(
        num_scalar_prefetch=1, grid=(N_CHIP,),
        in_specs=[pl.BlockSpec(memory_space=pltpu.VMEM),
                  pl.BlockSpec((n, COLS_PER_CHIP), lambda s, order: (0, order[s])),
                  *([HBM_SPEC] * n_small)],
        out_specs=(pl.BlockSpec(piece, lambda s, order: (0, 0)), HBM_SPEC, *([HBM_SPEC] * n_small)),
        scratch_shapes=[pltpu.VMEM((2, D_MODEL, COLS_PER_CHIP), F32), pltpu.VMEM((4,) + piece, F32),
                        pltpu.VMEM((3,) + piece, BF16),
                        pltpu.SemaphoreType.DMA((4,)), pltpu.SemaphoreType.DMA((4,)),
                        pltpu.SemaphoreType.DMA((3,)), pltpu.SemaphoreType.DMA((3,)),
                        pltpu.SemaphoreType.DMA((7 * n_small,)), pltpu.SemaphoreType.DMA((7 * n_small,)),
                        pltpu.SemaphoreType.DMA((n_small,))])
    return _pcall(
        body, name="dw_in_exchange", grid_spec=grid_spec,
        out_shape=(jax.ShapeDtypeStruct(piece, F32), jax.ShapeDtypeStruct((3,) + piece, BF16),
                   *(jax.ShapeDtypeStruct((N_DEV,) + a.shape, a.dtype) for a in smalls)),
        compiler_params=_params(1),
    )(order, xn, dproj, *smalls)


def _adamw(g, w, m, v):
    m_new = ADAM_B1 * m + (1.0 - ADAM_B1) * g
    v_new = ADAM_B2 * v + (1.0 - ADAM_B2) * (g * g)
    m_hat = m_new / (1.0 - ADAM_B1 ** ADAM_STEP)
    v_hat = v_new / (1.0 - ADAM_B2 ** ADAM_STEP)
    delta = -ADAM_LR * (m_hat / (jnp.sqrt(v_hat) + ADAM_EPS) + ADAM_WD * w)
    return delta, m_new, v_new


def _reduce_adam(recv, w, m, v, name, row_tile):
    rows, cols = w.shape

    def body(r_ref, w_ref, m_ref, v_ref, g_ref, d_ref, nm_ref, nv_ref):
        g = r_ref[0]
        for s in range(1, N_DEV):
            g = g + r_ref[s]
        g_ref[...] = g
        d_ref[...], nm_ref[...], nv_ref[...] = _adamw(g, w_ref[...], m_ref[...], v_ref[...])

    tile = pl.BlockSpec((row_tile, cols), lambda i: (i, 0))
    shp = jax.ShapeDtypeStruct((rows, cols), F32)
    return _pcall(
        body, name=name, grid=(rows // row_tile,),
        out_shape=(shp,) * 4,
        in_specs=[pl.BlockSpec((N_DEV, row_tile, cols), lambda i: (0, i, 0)), tile, tile, tile],
        out_specs=(tile,) * 4,
        compiler_params=_params(1),
    )(recv, w, m, v)


def _reduce_adam_w_in(own, rchip, w, m, v):
    rows, cols = w.shape
    row_tile = 256

    def body(o_ref, r_ref, w_ref, m_ref, v_ref, g_ref, d_ref, nm_ref, nv_ref):
        g = o_ref[...]
        for s in range(3):
            g = g + r_ref[s].astype(F32)
        g_ref[...] = g
        d_ref[...], nm_ref[...], nv_ref[...] = _adamw(g, w_ref[...], m_ref[...], v_ref[...])

    tile = pl.BlockSpec((row_tile, cols), lambda i: (i, 0))
    shp = jax.ShapeDtypeStruct((rows, cols), F32)
    return _pcall(
        body, name="reduce_adam_w_in", grid=(rows // row_tile,),
        out_shape=(shp,) * 4,
        in_specs=[tile, pl.BlockSpec((3, row_tile, cols), lambda i: (0, i, 0)), tile, tile, tile],
        out_specs=(tile,) * 4,
        compiler_params=_params(1),
    )(own, rchip, w, m, v)


_SMALL_LEAVES = ("norm_gain", "final_norm_gain", "b_glu", "ssm_a_re", "ssm_a_im", "ssm_log_dt", "ssm_d", "conv_w",
                 "ssm_c_re", "ssm_c_im")


def _reduce_adam_small(r_pack, r_gc, r_gb, wmv, wmv_b):
    n_leaf = len(_SMALL_LEAVES)

    def body(*refs):
        rp_ref, rgc_ref, rgb_ref, wb_ref = refs[:4]
        w_refs = refs[4:4 + 3 * n_leaf]
        loss_ref, ob_ref = refs[4 + 3 * n_leaf:6 + 3 * n_leaf]
        o_refs = refs[6 + 3 * n_leaf:6 + 7 * n_leaf]
        own_conv = refs[-1]

        def total(ref):
            acc = ref[0].astype(F32)
            for s in range(1, N_DEV):
                acc = acc + ref[s].astype(F32)
            return acc

        sp = total(rp_ref)
        sgc = total(rgc_ref)
        sgb = total(rgb_ref)
        loss_ref[...] = sp[ROW_LOSS:ROW_LOSS + SUBLANES, 0:LANES]

        def wide(r):
            return jnp.concatenate([sp[r:r + 1, :], sp[r + 1:r + 2, :]], axis=1)

        s5 = slice(ROW_S5, ROW_S5 + N_GROUPS)
        eye = (lax.broadcasted_iota(jnp.int32, (N_GROUPS, N_GROUPS), 0)
               == lax.broadcasted_iota(jnp.int32, (N_GROUPS, N_GROUPS), 1)).astype(F32)
        d_row = sp[ROW_BGLU_D + 1:ROW_BGLU_D + 2, :]
        me = 4 * lax.axis_index("x") + 2 * lax.axis_index("y") + lax.axis_index("c")
        for k in range(N_DEV):
            @pl.when(me == k)
            def _(k=k):
                own_conv[...] = sp[ROW_CONV:ROW_CONV + SUBLANES, k * CONV_COLS_PER_DEV:(k + 1) * CONV_COLS_PER_DEV]
        grads = {
            "norm_gain": wide(ROW_NORM_GAIN),
            "final_norm_gain": wide(ROW_FINAL_GAIN),
            "b_glu": sp[ROW_BGLU_D:ROW_BGLU_D + 1, :],
            "ssm_a_re": sp[s5, LANE_A_RE:LANE_A_RE + STATE],
            "ssm_a_im": sp[s5, LANE_A_IM:LANE_A_IM + STATE],
            "ssm_log_dt": jnp.sum(sp[s5, LANE_LOG_DT:LANE_LOG_DT + 1] * eye, axis=0, keepdims=True),
            "ssm_d": jnp.concatenate([d_row[:, g * GROUP:(g + 1) * GROUP] for g in range(N_GROUPS)], axis=0),
            "conv_w": own_conv[0:3, :],
            "ssm_c_re": sgc[:, 0:STATE],
            "ssm_c_im": sgc[:, STATE:2 * STATE],
        }
        for i, name in enumerate(_SMALL_LEAVES):
            g = grads[name]
            w_ref, m_ref, v_ref = w_refs[3 * i:3 * i + 3]
            o_g, o_d, o_m, o_v = o_refs[4 * i:4 * i + 4]
            o_g[...] = g
            o_d[...], o_m[...], o_v[...] = _adamw(g, w_ref[...], m_ref[...], v_ref[...])
        ob_ref[0] = sgb
        ob_ref[1], ob_ref[2], ob_ref[3] = _adamw(sgb, wb_ref[0], wb_ref[1], wb_ref[2])

    flat_w = [a for name in _SMALL_LEAVES for a in wmv[name]]
    leaf_shapes = [jax.ShapeDtypeStruct(wmv[name][0].shape, F32) for name in _SMALL_LEAVES for _ in range(4)]
    outs = _pcall(
        body, name="reduce_adam_small",
        out_shape=(jax.ShapeDtypeStruct((SUBLANES, LANES), F32), jax.ShapeDtypeStruct((4,) + wmv_b.shape[1:], F32),
                   *leaf_shapes),
        scratch_shapes=[pltpu.VMEM((SUBLANES, CONV_COLS_PER_DEV), F32)],
        compiler_params=_params(0),
    )(r_pack, r_gc, r_gb, wmv_b, *flat_w)
    leaves = {name: outs[2 + 4 * i:6 + 4 * i] for i, name in enumerate(_SMALL_LEAVES)}
    return outs[0], leaves, outs[1]


def _block_diag(m4):
    eye = jnp.eye(SUBLANES, dtype=m4.dtype)
    j, g, a, b = m4.shape
    return jnp.einsum("jgab,gk->jgakb", m4, eye).reshape(j, g * a, g * b)


def _block_diag_extract(dense, a, b):
    d5 = dense.reshape(N_JBLK, SUBLANES, a, SUBLANES, b)
    return jnp.stack([d5[:, g, :, g, :] for g in range(SUBLANES)], axis=1)


def kernel(x, norm_gain, w_in, ssm_a_re, ssm_a_im, ssm_log_dt, ssm_b_re, ssm_b_im, ssm_c_re, ssm_c_im, ssm_d, w_glu, b_glu, conv_w, w_out, final_norm_gain, loss_target, m_norm_gain, m_w_in, m_ssm_a_re, m_ssm_a_im, m_ssm_log_dt, m_ssm_b_re, m_ssm_b_im, m_ssm_c_re, m_ssm_c_im, m_ssm_d, m_w_glu, m_b_glu, m_conv_w, m_w_out, m_final_norm_gain, v_norm_gain, v_w_in, v_ssm_a_re, v_ssm_a_im, v_ssm_log_dt, v_ssm_b_re, v_ssm_b_im, v_ssm_c_re, v_ssm_c_im, v_ssm_d, v_w_glu, v_b_glu, v_conv_w, v_w_out, v_final_norm_gain):
    n_seq, seq, _ = x.shape
    n = n_seq * seq

    rep = lambda a: jnp.repeat(a[0], GROUP, axis=1)
    a_re_r, a_im_r = rep(ssm_a_re), rep(ssm_a_im)
    log_dt = ssm_log_dt[0].reshape(N_GROUPS, 1)
    b_re2 = ssm_b_re[0].reshape(N_GROUPS, STATE * GROUP)
    b_im2 = ssm_b_im[0].reshape(N_GROUPS, STATE * GROUP)
    ab_re_r, ab_im_r, bb_re2, bb_im2 = _ssm_disc(a_re_r, a_im_r, log_dt, b_re2, b_im2)
    ab_re = ab_re_r[:, ::GROUP].reshape(1, N_GROUPS * STATE)
    ab_im = ab_im_r[:, ::GROUP].reshape(1, N_GROUPS * STATE)

    def bb_mat(bb2):
        t = jnp.transpose(bb2.reshape(N_JBLK, SUBLANES, STATE, GROUP), (0, 1, 3, 2))
        return _block_diag(t).astype(BF16)

    def c_mat(c3, sign):
        t = jnp.transpose(c3.reshape(N_JBLK, SUBLANES, GROUP, STATE), (0, 1, 3, 2))
        return _block_diag(sign * t).astype(BF16)

    bb_re_m, bb_im_m = bb_mat(bb_re2), bb_mat(bb_im2)
    c_re_m, c_imn_m = c_mat(ssm_c_re[0], 1.0), c_mat(ssm_c_im[0], -1.0)
    d_row = ssm_d[0].reshape(1, SSM_W)

    x2 = x.reshape(n, D_MODEL)
    tgt2 = loss_target.reshape(n, D_MODEL)
    mx, my, mc = lax.axis_index("x"), lax.axis_index("y"), lax.axis_index("c")
    chip_ids = [2 * cx + cy for cx, cy in ((mx, my), (1 - mx, my), (mx, 1 - my), (1 - mx, 1 - my))]
    arrival = chip_ids
    xn, proj, w_in_f = _in_proj(jnp.stack(arrival).astype(jnp.int32), x2, norm_gain, w_in[0].astype(BF16))
    u3 = proj.reshape(n_seq, seq, IN_COLS)
    conv_p = jnp.pad(conv_w[0], ((0, SUBLANES - 3), (0, LANES - CONV_COLS_PER_DEV)))
    s_re, s_im, y3, w_out_f, w_glu_f, conv_all = _ssm_fwd(
        u3, bb_re_m, bb_im_m, c_re_m, c_imn_m, d_row, ab_re, ab_im,
        w_out[0].astype(BF16), w_glu[0].astype(BF16), conv_p, n_seq, seq)
    conv8 = jnp.transpose(conv_all[:, :, :CONV_COLS_PER_DEV], (1, 0, 2)).reshape(SUBLANES, CONV_W)
    (dh2, dy, dzs, dbc, dzc, dyc, dw_out, dw_glu, loss_t, dgf, dbg, dcw) = _mix(
        x2, tgt2, y3.reshape(n, SSM_W), proj, final_norm_gain.reshape(1, D_MODEL), b_glu, conv8,
        w_glu_f, w_out_f, seq)

    du3, dc_re_d, dc_im_d, dbb_re_d, dbb_im_d, dab_re, dab_im, dd, r_out, r_glu = _ssm_bwd(
        dy.reshape(n_seq, seq, SSM_W), u3, s_re, s_im, bb_re_m, bb_im_m, c_re_m, c_imn_m, d_row, ab_re, ab_im,
        dw_out.reshape(N_DEV, OUT_ROWS_PER_DEV, D_MODEL), dw_glu.reshape(N_DEV, GLU_ROWS_PER_DEV, SSM_W), n_seq, seq)
    du = du3.reshape(n, SSM_W)
    def bb_grad(dense):
        t = _block_diag_extract(dense, GROUP, STATE)
        return jnp.transpose(t, (0, 1, 3, 2)).reshape(N_GROUPS, STATE * GROUP)

    grad_x2, dproj, dg8 = _in_bwd(x2, dh2, du, dzs, dyc, proj, dbc, dzc, norm_gain, conv8, w_in_f, seq)
    pack, gc, gb = _ssm_disc_bwd_pack(
        a_re_r, a_im_r, log_dt, b_re2, b_im2, dab_re.reshape(N_GROUPS, STATE), dab_im.reshape(N_GROUPS, STATE),
        bb_grad(dbb_re_d), bb_grad(dbb_im_d), loss_t, dg8, dgf, dbg, dd, dcw, dc_re_d, dc_im_d)

    order = [chip_ids[3], chip_ids[1], chip_ids[2], chip_ids[0]]
    own_in, rchip_in, r_pack, r_gc, r_gb = _dw_in_exchange(
        jnp.stack(order).astype(jnp.int32), xn, dproj, [pack, gc, gb])

    flat2 = lambda a: a.reshape(a.shape[-2:]) if a.ndim > 2 else a.reshape(1, -1)
    c2 = lambda a: a.reshape(N_GROUPS * GROUP, STATE)
    wmv = dict(norm_gain=(norm_gain, m_norm_gain, v_norm_gain),
               final_norm_gain=tuple(flat2(a) for a in (final_norm_gain, m_final_norm_gain, v_final_norm_gain)),
               b_glu=(b_glu, m_b_glu, v_b_glu),
               ssm_a_re=tuple(flat2(a) for a in (ssm_a_re, m_ssm_a_re, v_ssm_a_re)),
               ssm_a_im=tuple(flat2(a) for a in (ssm_a_im, m_ssm_a_im, v_ssm_a_im)),
               ssm_log_dt=(ssm_log_dt, m_ssm_log_dt, v_ssm_log_dt),
               ssm_d=tuple(flat2(a) for a in (ssm_d, m_ssm_d, v_ssm_d)),
               conv_w=tuple(flat2(a) for a in (conv_w, m_conv_w, v_conv_w)),
               ssm_c_re=tuple(c2(a) for a in (ssm_c_re, m_ssm_c_re, v_ssm_c_re)),
               ssm_c_im=tuple(c2(a) for a in (ssm_c_im, m_ssm_c_im, v_ssm_c_im)))
    wmv_b = jnp.stack([ssm_b_re, ssm_b_im, m_ssm_b_re, m_ssm_b_im, v_ssm_b_re, v_ssm_b_im]).reshape(
        3, 2, N_GROUPS, STATE * GROUP)

    res_in = _reduce_adam_w_in(own_in, rchip_in, w_in[0], m_w_in[0], v_w_in[0])
    res_out = _reduce_adam(r_out, w_out[0], m_w_out[0], v_w_out[0], "reduce_adam_w_out", OUT_ROWS_PER_DEV)
    res_glu = _reduce_adam(r_glu, w_glu[0], m_w_glu[0], v_w_glu[0], "reduce_adam_w_glu", GLU_ROWS_PER_DEV)
    loss8, small, res_b = _reduce_adam_small(r_pack, r_gc, r_gb, wmv, wmv_b)
    loss = loss8[0, 0]

    shapes = dict(norm_gain=(1, D_MODEL), ssm_a_re=(1, N_GROUPS, STATE), ssm_a_im=(1, N_GROUPS, STATE),
                  ssm_log_dt=(1, N_GROUPS), ssm_c_re=(1, N_GROUPS, GROUP, STATE), ssm_c_im=(1, N_GROUPS, GROUP, STATE),
                  ssm_d=(1, N_GROUPS, GROUP), b_glu=(1, SSM_W), final_norm_gain=(D_MODEL,),
                  conv_w=(1, 3, CONV_COLS_PER_DEV))
    big = dict(w_in=res_in, w_glu=res_glu, w_out=res_out)
    b5 = res_b.reshape(4, 2, 1, N_GROUPS, STATE, GROUP)

    def leaf(kind, name):
        if name in big:
            return big[name][kind][None]
        if name in ("ssm_b_re", "ssm_b_im"):
            return b5[kind, 0 if name == "ssm_b_re" else 1]
        return small[name][kind].reshape(shapes[name])

    order = ["norm_gain", "w_in", "ssm_a_re", "ssm_a_im", "ssm_log_dt", "ssm_b_re", "ssm_b_im", "ssm_c_re",
             "ssm_c_im", "ssm_d", "w_glu", "b_glu", "conv_w", "w_out", "final_norm_gain"]
    outs = [loss, grad_x2.reshape(x.shape)]
    for kind in range(4):
        outs += [leaf(kind, name) for name in order]
    return tuple(outs)
```

```python
import functools
import math

import jax
import jax.numpy as jnp
from jax import lax
from jax.experimental import pallas as pl
from jax.experimental.pallas import tpu as pltpu

F32 = jnp.float32
BF16 = jnp.bfloat16

N_DEV = 8
D_MODEL = 1024
SSM_W = 512
CONV_W = 512
N_GROUPS = 32
GROUP = 16
STATE = 64
IN_COLS = 3072
SEG_U, SEG_ZS, SEG_H, SEG_BC, SEG_CC, SEG_ZC = range(6)
COLS_PER_DEV = IN_COLS // N_DEV
N_CHIP = N_DEV // 2
COLS_PER_CHIP = 2 * COLS_PER_DEV
OUT_ROWS_PER_DEV = D_MODEL // N_DEV
GLU_ROWS_PER_DEV = SSM_W // N_DEV
CONV_COLS_PER_DEV = CONV_W // N_DEV
EPS = 1e-6

N_JBLK = 4
JB_CH = SSM_W // N_JBLK
JB_ST = N_GROUPS * STATE // N_JBLK

ADAM_LR = 0.001
ADAM_B1 = 0.9
ADAM_B2 = 0.999
ADAM_EPS = 1e-08
ADAM_WD = 0.01
ADAM_STEP = 10

SUBLANES = 8
LANES = 128
VMEM_LIMIT = 48 * 1024 * 1024
TOK_TILE = 256
IN_TILE = 1024
SCAN_TILE = 512

MESH = pl.DeviceIdType.MESH
HBM_SPEC = pl.BlockSpec(memory_space=pltpu.HBM)


def _pcall(body, **kw):
    return pl.pallas_call(body, **kw)


def _params(n_grid):
    return pltpu.CompilerParams(dimension_semantics=("arbitrary",) * n_grid,
                                vmem_limit_bytes=VMEM_LIMIT)


def _dot(a, b):
    return jnp.dot(a, b, preferred_element_type=F32)


def _dot_nt(a, b):
    return lax.dot_general(a, b, (((1,), (1,)), ((), ())), preferred_element_type=F32)


def _dot_tn(a, b):
    return lax.dot_general(a, b, (((0,), (0,)), ((), ())), preferred_element_type=F32)


def _sigmoid(z):
    return 1.0 / (1.0 + jnp.exp(-z))


_GELU_C = math.sqrt(2.0 / math.pi)


def _gelu_and_grad(y):
    inner = _GELU_C * (y + 0.044715 * (y * y * y))
    t = jnp.tanh(inner)
    g = 0.5 * y * (1.0 + t)
    dg = 0.5 * (1.0 + t) + 0.5 * y * (1.0 - t * t) * (_GELU_C * (1.0 + 3.0 * 0.044715 * (y * y)))
    return g, dg


def _silu_and_grad(z):
    s = _sigmoid(z)
    return z * s, s * (1.0 + z * (1.0 - s))


def _shift_down(v, halo, k):
    rolled = pltpu.roll(v, k, 0)
    row = lax.broadcasted_iota(jnp.int32, v.shape, 0)
    for r in range(k):
        rolled = jnp.where(row == r, halo[SUBLANES - k + r:SUBLANES - k + r + 1, :], rolled)
    return rolled


def _shift_up(v, halo, k):
    n = v.shape[0]
    rolled = pltpu.roll(v, n - k, 0)
    row = lax.broadcasted_iota(jnp.int32, v.shape, 0)
    for r in range(k):
        rolled = jnp.where(row == n - k + r, halo[r:r + 1, :], rolled)
    return rolled


def _mesh_pos():
    return lax.axis_index("x"), lax.axis_index("y"), lax.axis_index("c")


def _direct_copies(srcs_for, out_refs, send_sems, recv_sems, loc_sems):
    x, y, c = _mesh_pos()
    me_id = 4 * x + 2 * y + c
    n_arr = len(out_refs)
    dsts = [r.at[me_id] for r in out_refs]
    own = srcs_for(me_id)
    mine = [pltpu.make_async_copy(own[a], dsts[a], loc_sems.at[a]) for a in range(n_arr)]
    sends = []
    for k in range(1, N_DEV):
        px, py, pc = x ^ ((k >> 2) & 1), y ^ ((k >> 1) & 1), c ^ (k & 1)
        src = srcs_for(4 * px + 2 * py + pc)
        for a in range(n_arr):
            sends.append(pltpu.make_async_remote_copy(
                src_ref=src[a], dst_ref=dsts[a],
                send_sem=send_sems.at[(k - 1) * n_arr + a], recv_sem=recv_sems.at[(k - 1) * n_arr + a],
                device_id=(px, py, pc), device_id_type=MESH))
    return mine, sends


class _TwoLevelGather:
    def __init__(self, srcs, slots, send_sems, recv_sems, loc_sems):
        self.srcs, self.slots, self.n_arr = srcs, slots, len(srcs)
        self.send_sems, self.recv_sems, self.loc_sems = send_sems, recv_sems, loc_sems
        x, y, c = _mesh_pos()
        self.c = c
        self.me, self.sib = (x, y, c), (x, y, 1 - c)
        self.chips = [(1 - x, y), (x, 1 - y), (1 - x, 1 - y)]

    def _copies(self, k, block, to, from_src=False):
        dev = 4 * block[0] + 2 * block[1] + block[2]
        return [pltpu.make_async_remote_copy(
            src_ref=self.srcs[a] if from_src else self.slots[a](dev), dst_ref=self.slots[a](dev),
            send_sem=self.send_sems.at[k * self.n_arr + a], recv_sem=self.recv_sems.at[k * self.n_arr + a],
            device_id=to, device_id_type=MESH) for a in range(self.n_arr)]

    def _local(self):
        dev = 4 * self.me[0] + 2 * self.me[1] + self.me[2]
        return [pltpu.make_async_copy(self.srcs[a], self.slots[a](dev), self.loc_sems.at[a])
                for a in range(self.n_arr)]

    def start(self, chips=(0, 1, 2)):
        for cp in self._local() + self._copies(0, self.me, self.sib, True):
            cp.start()
        self.start_to(chips)

    def start_to(self, chips):
        for j in chips:
            for cp in self._copies(1 + j, self.me, (*self.chips[j], self.c), True):
                cp.start()

    def wait_own(self):
        for cp in self._local():
            cp.wait()

    def wait_sibling(self):
        for cp in self._copies(0, self.sib, self.me):
            cp.wait_recv()

    def wait_and_pass_on(self, j):
        chip = self.chips[j]
        for cp in self._copies(1 + j, (*chip, self.c), self.me):
            cp.wait_recv()
        for cp in self._copies(4 + j, (*chip, self.c), self.sib):
            cp.start()

    def wait_passed_on(self, j):
        for cp in self._copies(4 + j, (*self.chips[j], 1 - self.c), self.me):
            cp.wait_recv()

    def wait_sends(self):
        for cp in self._copies(0, self.me, self.sib, True):
            cp.wait_send()
        for j, chip in enumerate(self.chips):
            for cp in self._copies(1 + j, self.me, (*chip, self.c), True) + self._copies(4 + j, (*chip, self.c), self.sib):
                cp.wait_send()

    def forward(self):
        for j in range(3):
            self.wait_and_pass_on(j)

    def finish(self):
        self.wait_sibling()
        for j in range(3):
            self.wait_passed_on(j)
        self.wait_sends()
        self.wait_own()


def _disc(a_re, a_im, log_dt, b_re, b_im):
    dt = jnp.exp(log_dt)
    mag = jnp.exp(a_re * dt)
    ab_re = mag * jnp.cos(a_im * dt)
    ab_im = mag * jnp.sin(a_im * dt)
    den = a_re * a_re + a_im * a_im
    p_re = ab_re - 1.0
    p_im = ab_im
    q_re = (p_re * a_re + p_im * a_im) / den
    q_im = (p_im * a_re - p_re * a_im) / den
    bb_re = q_re * b_re - q_im * b_im
    bb_im = q_re * b_im + q_im * b_re
    return ab_re, ab_im, bb_re, bb_im


def _ssm_disc(a_re_x, a_im_x, log_dt_x, b_re, b_im):
    def body(are, aim, ldt, bre, bim, o_abre, o_abim, o_bbre, o_bbim):
        outs = _disc(are[...], aim[...], ldt[...], bre[...], bim[...])
        for o, v in zip((o_abre, o_abim, o_bbre, o_bbim), outs):
            o[...] = v

    shp = jax.ShapeDtypeStruct(a_re_x.shape, F32)
    return _pcall(body, name="ssm_disc", out_shape=(shp,) * 4)(a_re_x, a_im_x, log_dt_x, b_re, b_im)


def _split3(v):
    hi = v.astype(BF16)
    r1 = v - hi.astype(F32)
    mid = r1.astype(BF16)
    lo = (r1 - mid.astype(F32)).astype(BF16)
    return hi, mid, lo


def _select_dot(sel, v):
    return sum(_dot(sel, t) for t in _split3(v))


PACK_ROWS = 72
PACK_W = 512
ROW_FINAL_GAIN, ROW_NORM_GAIN, ROW_BGLU_D, ROW_CONV, ROW_LOSS, ROW_S5 = 0, 8, 16, 24, 32, 40
LANE_A_RE, LANE_A_IM, LANE_LOG_DT = 0, 128, 256


def _ssm_disc_bwd_pack(a_re_x, a_im_x, log_dt_x, b_re, b_im, g_ab_re, g_ab_im, dbb_re_d, dbb_im_d,
                       loss_t, dg8, dgf, dbg, dd, dcw, dc_re_d, dc_im_d):
    rows_gh = N_GROUPS * GROUP

    def body(are, aim, ldt, bre, bim, gabre, gabim, dbbre_ref, dbbim_ref,
             loss_ref, dg8_ref, dgf_ref, dbg_ref, dd_ref, dcw_ref, dcre_ref, dcim_ref,
             p_ref, gc_ref, gb_ref, gbb_re, gbb_im):
        r_g = lax.broadcasted_iota(jnp.int32, (N_GROUPS, rows_gh), 0)
        c_gh = lax.broadcasted_iota(jnp.int32, (N_GROUPS, rows_gh), 1)
        group_sum = (c_gh // GROUP == r_g).astype(BF16)
        r_gh = lax.broadcasted_iota(jnp.int32, (rows_gh, N_GROUPS), 0)
        c_g = lax.broadcasted_iota(jnp.int32, (rows_gh, N_GROUPS), 1)
        first_row = (r_gh == c_g * GROUP).astype(BF16)

        def diag_block(ref, j, gi):
            return ref[j, gi * GROUP:(gi + 1) * GROUP, gi * STATE:(gi + 1) * STATE]

        for j in range(N_JBLK):
            for gi in range(SUBLANES):
                r0 = (j * SUBLANES + gi) * GROUP
                gbb_re[r0:r0 + GROUP, :] = diag_block(dbbre_ref, j, gi)
                gbb_im[r0:r0 + GROUP, :] = diag_block(dbbim_ref, j, gi)
                both = jnp.concatenate([diag_block(dcre_ref, j, gi), -diag_block(dcim_ref, j, gi)], axis=1)
                gc_ref[r0:r0 + GROUP, :] = both.astype(BF16)

        _, vjp = jax.vjp(_disc, are[...], aim[...], ldt[...], bre[...], bim[...])
        d_are, d_aim, d_ldt, d_bre, d_bim = vjp((_select_dot(first_row, gabre[...]), _select_dot(first_row, gabim[...]),
                                                 gbb_re[...], gbb_im[...]))
        gb_ref[...] = jnp.concatenate([d_bre, d_bim], axis=1).astype(BF16)

        p_ref[...] = jnp.zeros_like(p_ref)
        half = D_MODEL // 2
        for r, src in ((ROW_FINAL_GAIN, dgf_ref), (ROW_NORM_GAIN, dg8_ref)):
            p_ref[r:r + 1, :] = src[0:1, 0:half]
            p_ref[r + 1:r + 2, :] = src[0:1, half:D_MODEL]
        p_ref[ROW_BGLU_D:ROW_BGLU_D + 1, :] = dbg_ref[...]
        p_ref[ROW_BGLU_D + 1:ROW_BGLU_D + 2, :] = dd_ref[...]
        p_ref[ROW_CONV:ROW_CONV + SUBLANES, :] = dcw_ref[...]
        p_ref[ROW_LOSS:ROW_LOSS + SUBLANES, 0:LANES] = loss_ref[...]
        s5 = slice(ROW_S5, ROW_S5 + N_GROUPS)
        p_ref[s5, LANE_A_RE:LANE_A_RE + STATE] = _select_dot(group_sum, d_are)
        p_ref[s5, LANE_A_IM:LANE_A_IM + STATE] = _select_dot(group_sum, d_aim)
        p_ref[s5, LANE_LOG_DT:LANE_LOG_DT + LANES] = _select_dot(group_sum, jnp.broadcast_to(d_ldt, (rows_gh, LANES)))

    return _pcall(body, name="ssm_disc_bwd_pack",
                  out_shape=(jax.ShapeDtypeStruct((PACK_ROWS, PACK_W), F32),
                             jax.ShapeDtypeStruct((rows_gh, 2 * STATE), BF16),
                             jax.ShapeDtypeStruct((rows_gh, 2 * STATE), BF16)),
                  scratch_shapes=[pltpu.VMEM((rows_gh, STATE), F32), pltpu.VMEM((rows_gh, STATE), F32)],
                  )(a_re_x, a_im_x, log_dt_x, b_re, b_im, g_ab_re, g_ab_im, dbb_re_d, dbb_im_d,
                    loss_t, dg8, dgf, dbg, dd, dcw, dc_re_d, dc_im_d)


def _in_proj(order, x2, g1, w_in_b):
    n = x2.shape[0]
    tm = min(IN_TILE, n)
    n_tiles = n // tm

    def body(order_ref, x_ref, g_ref, w_ref, xn_ref, proj_ref, wall_ref,
             xn_scr, wbuf, send_sems, recv_sems, loc_sems, out_sems):
        k = pl.program_id(0)
        i = pl.program_id(1)

        def slot(dev):
            return wbuf.at[dev // 2, :, pl.ds(pl.multiple_of((dev % 2) * COLS_PER_DEV, LANES), COLS_PER_DEV)]

        gather = _TwoLevelGather([w_ref], [slot], send_sems, recv_sems, loc_sems)

        @pl.when((k == 0) & (i == 0))
        def _():
            gather.start(chips=(0, 1))

        def own_chip():
            gather.wait_own()
            gather.wait_sibling()

        def other_chip(j):
            gather.wait_and_pass_on(j)
            if j == 0:
                gather.start_to((2,))
            gather.wait_passed_on(j)

        arrivals = [own_chip] + [functools.partial(other_chip, j) for j in range(3)]
        for kk, arrived in enumerate(arrivals):
            @pl.when((k == kk) & (i == 0))
            def _(arrived=arrived):
                arrived()

        rows = pl.ds(pl.multiple_of(i * tm, tm), tm)

        @pl.when(k == 0)
        def _():
            x = x_ref[...]
            r = lax.rsqrt(jnp.mean(x * x, axis=-1, keepdims=True) + EPS)
            xn = ((x * r) * g_ref[...]).astype(BF16)
            xn_scr[rows, :] = xn
            xn_ref[...] = xn

        proj_ref[...] = _dot(xn_scr[rows, :], wbuf[order_ref[k]])

        @pl.when((k == N_CHIP - 1) & (i == n_tiles - 1))
        def _():
            gather.wait_sends()
            outs = [pltpu.make_async_copy(wbuf.at[q], wall_ref.at[:, q * COLS_PER_CHIP:(q + 1) * COLS_PER_CHIP],
                                          out_sems.at[q]) for q in range(N_CHIP)]
            for cp in outs:
                cp.start()
            for cp in outs:
                cp.wait()

    tile_once = lambda k, i, order: (jnp.where(k == 0, i, n_tiles - 1), 0)
    grid_spec = pltpu.PrefetchScalarGridSpec(
        num_scalar_prefetch=1, grid=(N_CHIP, n_tiles),
        in_specs=[pl.BlockSpec((tm, D_MODEL), tile_once),
                  pl.BlockSpec((1, D_MODEL), lambda k, i, order: (0, 0)),
                  HBM_SPEC],
        out_specs=(pl.BlockSpec((tm, D_MODEL), tile_once),
                   pl.BlockSpec((tm, COLS_PER_CHIP), lambda k, i, order: (i, order[k])),
                   HBM_SPEC),
        scratch_shapes=[pltpu.VMEM((n, D_MODEL), BF16), pltpu.VMEM((N_CHIP, D_MODEL, COLS_PER_CHIP), BF16),
                        pltpu.SemaphoreType.DMA((7,)), pltpu.SemaphoreType.DMA((7,)), pltpu.SemaphoreType.DMA((1,)),
                        pltpu.SemaphoreType.DMA((N_CHIP,))])
    return _pcall(
        body, name="in_proj", grid_spec=grid_spec,
        out_shape=(jax.ShapeDtypeStruct((n, D_MODEL), BF16), jax.ShapeDtypeStruct((n, IN_COLS), F32),
                   jax.ShapeDtypeStruct((D_MODEL, IN_COLS), BF16)),
        compiler_params=_params(2),
    )(order, x2, g1, w_in_b)


def _cmul(p, q):
    return p[0] * q[0] - p[1] * q[1], p[0] * q[1] + p[1] * q[0]


def _scan_tables(ar, ai, width, reverse):
    pows = [(ar, ai)]
    for _ in range(SUBLANES - 1):
        pows.append(_cmul(pows[-1], (ar, ai)))
    row = lax.broadcasted_iota(jnp.int32, (SUBLANES, width), 0)

    def bc(v):
        return jnp.broadcast_to(v, (SUBLANES, width))

    levels = []
    for k in (1, 2, 4):
        keep = (row <= SUBLANES - 1 - k) if reverse else (row >= k)
        levels.append((jnp.where(keep, bc(pows[k - 1][0]), 0.0), jnp.where(keep, bc(pows[k - 1][1]), 0.0)))
    cre = jnp.zeros((SUBLANES, width), F32)
    cim = jnp.zeros((SUBLANES, width), F32)
    for r in range(SUBLANES):
        e = (SUBLANES - r) if reverse else (r + 1)
        cre = jnp.where(row == r, bc(pows[e - 1][0]), cre)
        cim = jnp.where(row == r, bc(pows[e - 1][1]), cim)
    return levels, (cre, cim)


def _load_chunked(src_ref, b, dst_ref, n_rows):
    n_blk = n_rows // SUBLANES
    for i in range(n_blk):
        dst_ref[b, i * SUBLANES:(i + 1) * SUBLANES, :] = src_ref[b, pl.ds(i, SUBLANES, stride=n_blk), :]


def _store_chunked(val, dst_ref, b, n_rows):
    n_blk = n_rows // SUBLANES
    for i in range(n_blk):
        dst_ref[b, pl.ds(i, SUBLANES, stride=n_blk), :] = val[i * SUBLANES:(i + 1) * SUBLANES, :]


def _chunk_scan(re_ref, im_ref, bs, car_ref, ar, ai, n_rows, reverse, on_block=None):
    width = re_ref.shape[2]
    n_blk = n_rows // SUBLANES
    shape = (SUBLANES, width)
    abr = jnp.broadcast_to(ar, shape)
    abi = jnp.broadcast_to(ai, shape)
    order = list(range(n_blk - 1, -1, -1)) if reverse else list(range(n_blk))

    def blk(ref, b, i):
        return ref[b, i * SUBLANES:(i + 1) * SUBLANES, :]

    def step(state, b, i):
        sr, si = state
        return abr * sr - abi * si + blk(re_ref, b, i), abr * si + abi * sr + blk(im_ref, b, i)

    finals = {b: (blk(re_ref, b, order[0]), blk(im_ref, b, order[0])) for b in bs}
    for i in order[1:]:
        for b in bs:
            finals[b] = step(finals[b], b, i)

    mr, mi = ar, ai
    for _ in range(n_blk.bit_length() - 1):
        mr, mi = _cmul((mr, mi), (mr, mi))
    levels, _ = _scan_tables(mr, mi, width, reverse)
    mbr = jnp.broadcast_to(mr, shape)
    mbi = jnp.broadcast_to(mi, shape)
    row = lax.broadcasted_iota(jnp.int32, shape, 0)
    edge_in = SUBLANES - 1 if reverse else 0
    edge_out = 0 if reverse else SUBLANES - 1
    sh1 = SUBLANES - 1 if reverse else 1
    states = {}
    for b in bs:
        fr, fi = finals[b]
        gr = jnp.where(row == edge_in, jnp.broadcast_to(car_ref[b, 0:1, :], shape), pltpu.roll(fr, sh1, 0))
        gi = jnp.where(row == edge_in, jnp.broadcast_to(car_ref[b, 1:2, :], shape), pltpu.roll(fi, sh1, 0))
        for (lr, li), k in zip(levels, (1, 2, 4)):
            sh = (SUBLANES - k) if reverse else k
            sr = pltpu.roll(gr, sh, 0)
            si = pltpu.roll(gi, sh, 0)
            gr, gi = gr + (lr * sr - li * si), gi + (lr * si + li * sr)
        car_ref[b, 0:1, :] = (fr + (mbr * gr - mbi * gi))[edge_out:edge_out + 1, :]
        car_ref[b, 1:2, :] = (fi + (mbr * gi + mbi * gr))[edge_out:edge_out + 1, :]
        states[b] = (gr, gi)

    for i in order:
        for b in bs:
            states[b] = step(states[b], b, i)
            re_ref[b, i * SUBLANES:(i + 1) * SUBLANES, :] = states[b][0]
            im_ref[b, i * SUBLANES:(i + 1) * SUBLANES, :] = states[b][1]
            if on_block is not None:
                on_block(b, i, *states[b])


def _ssm_fwd(u, bb_re, bb_im, c_re_t, c_imn_t, d_row, ab_re, ab_im, w_out_b, w_glu_b, conv_p, n_seq, seq):
    tt = min(SCAN_TILE, seq)
    nt = seq // tt

    def body(u_ref, bbre, bbim, cre, cimn, d_ref, are, aim, wout_ref, wglu_ref, cw_ref,
             sre_ref, sim_ref, y_ref, oout_ref, oglu_ref, ocw_ref,
             up_ref, car_ref, send_sems, recv_sems, loc_sems):
        j = pl.program_id(0)
        t = pl.program_id(1)
        gather = _TwoLevelGather(
            [wout_ref, wglu_ref, cw_ref],
            [lambda dev: oout_ref.at[pl.ds(pl.multiple_of(dev * OUT_ROWS_PER_DEV, OUT_ROWS_PER_DEV), OUT_ROWS_PER_DEV), :],
             lambda dev: oglu_ref.at[pl.ds(pl.multiple_of(dev * GLU_ROWS_PER_DEV, GLU_ROWS_PER_DEV), GLU_ROWS_PER_DEV), :],
             lambda dev: ocw_ref.at[dev]],
            send_sems, recv_sems, loc_sems)

        @pl.when((j == 0) & (t == 0))
        def _():
            gather.start()

        @pl.when((j == N_JBLK // 2) & (t == 0))
        def _():
            gather.forward()

        @pl.when(t == 0)
        def _():
            car_ref[...] = jnp.zeros_like(car_ref)

        bs = list(range(n_seq))
        for b in bs:
            _load_chunked(u_ref, b, up_ref, tt)
        for b in bs:
            ub = up_ref[b].astype(BF16)
            sre_ref[b] = _dot(ub, bbre[0])
            sim_ref[b] = _dot(ub, bbim[0])
            _chunk_scan(sre_ref, sim_ref, [b], car_ref, are[...], aim[...], tt, reverse=False)
        for b in bs:
            yp = (_dot(sre_ref[b].astype(BF16), cre[0]) + _dot(sim_ref[b].astype(BF16), cimn[0])
                  + d_ref[...] * up_ref[b])
            _store_chunked(yp, y_ref, b, tt)

        @pl.when((j == N_JBLK - 1) & (t == nt - 1))
        def _():
            gather.finish()

    tok = lambda j, t: (0, t, j)
    blk3 = lambda j, t: (j, 0, 0)
    row = lambda j, t: (0, j)
    st = jax.ShapeDtypeStruct((n_seq, seq, N_JBLK * JB_ST), F32)
    n_arr = 3
    return _pcall(
        body, name="ssm_fwd", grid=(N_JBLK, nt),
        out_shape=(st, st, jax.ShapeDtypeStruct((n_seq, seq, SSM_W), F32),
                   jax.ShapeDtypeStruct((D_MODEL, D_MODEL), BF16), jax.ShapeDtypeStruct((SSM_W, SSM_W), BF16),
                   jax.ShapeDtypeStruct((N_DEV, SUBLANES, LANES), F32)),
        in_specs=[pl.BlockSpec((n_seq, tt, JB_CH), tok),
                  pl.BlockSpec((1, JB_CH, JB_ST), blk3), pl.BlockSpec((1, JB_CH, JB_ST), blk3),
                  pl.BlockSpec((1, JB_ST, JB_CH), blk3), pl.BlockSpec((1, JB_ST, JB_CH), blk3),
                  pl.BlockSpec((1, JB_CH), row), pl.BlockSpec((1, JB_ST), row), pl.BlockSpec((1, JB_ST), row),
                  HBM_SPEC, HBM_SPEC, HBM_SPEC],
        out_specs=(pl.BlockSpec((n_seq, tt, JB_ST), tok), pl.BlockSpec((n_seq, tt, JB_ST), tok),
                   pl.BlockSpec((n_seq, tt, JB_CH), tok), HBM_SPEC, HBM_SPEC, HBM_SPEC),
        scratch_shapes=[pltpu.VMEM((n_seq, tt, JB_CH), F32), pltpu.VMEM((n_seq, SUBLANES, JB_ST), F32),
                        pltpu.SemaphoreType.DMA((7 * n_arr,)), pltpu.SemaphoreType.DMA((7 * n_arr,)),
                        pltpu.SemaphoreType.DMA((n_arr,))],
        compiler_params=_params(2),
    )(u, bb_re, bb_im, c_re_t, c_imn_t, d_row, ab_re, ab_im, w_out_b, w_glu_b, conv_p)


def _ssm_bwd(dy, u, s_re, s_im, bb_re, bb_im, c_re_t, c_imn_t, d_row, ab_re, ab_im, g_out, g_glu, n_seq, seq):
    tt = min(SCAN_TILE, seq)
    nt = seq // tt
    rows8 = tt // SUBLANES

    def body(dy_ref, u_ref, sre_ref, sim_ref, pre_ref, pim_ref, bbre, bbim, cre, cimn, d_ref, are, aim,
             gout_ref, gglu_ref,
             du_ref, dcre_ref, dcim_ref, dbbre_ref, dbbim_ref, dare_ref, daim_ref, dd_ref, rout_ref, rglu_ref,
             lre_ref, lim_ref, dyp_ref, up_ref, car_ref, send_sems, recv_sems, loc_sems):
        j = pl.program_id(0)
        tr = pl.program_id(1)

        def exchange():
            return _direct_copies(lambda pid: [gout_ref.at[pid], gglu_ref.at[pid]], [rout_ref, rglu_ref],
                                  send_sems, recv_sems, loc_sems)

        @pl.when((j == 0) & (tr == 0))
        def _():
            mine, sends = exchange()
            for cp in mine + sends:
                cp.start()

        @pl.when(tr == 0)
        def _():
            car_ref[...] = jnp.zeros_like(car_ref)
            for r in (dcre_ref, dcim_ref, dbbre_ref, dbbim_ref, dare_ref, daim_ref, dd_ref):
                r[...] = jnp.zeros_like(r)

        first = tr == nt - 1
        row = lax.broadcasted_iota(jnp.int32, (SUBLANES, JB_ST), 0)
        n_blk = tt // SUBLANES
        bs = list(range(n_seq))
        for b in bs:
            _load_chunked(dy_ref, b, dyp_ref, tt)
            _load_chunked(u_ref, b, up_ref, tt)
        for b in bs:
            dyb = dyp_ref[b].astype(BF16)
            lre_ref[b] = _dot_nt(dyb, cre[0])
            lim_ref[b] = _dot_nt(dyb, cimn[0])
        acc = {b: [jnp.zeros((SUBLANES, JB_ST), F32), jnp.zeros((SUBLANES, JB_ST), F32)] for b in bs}

        def on_block(b, i, lr, li):
            if i > 0:
                spr = sre_ref[b, (i - 1) * SUBLANES:i * SUBLANES, :]
                spi = sim_ref[b, (i - 1) * SUBLANES:i * SUBLANES, :]
            else:
                hr = jnp.where(first, 0.0, pre_ref[b, SUBLANES - 1:SUBLANES, :])
                hi = jnp.where(first, 0.0, pim_ref[b, SUBLANES - 1:SUBLANES, :])
                last_r = sre_ref[b, (n_blk - 1) * SUBLANES:n_blk * SUBLANES, :]
                last_i = sim_ref[b, (n_blk - 1) * SUBLANES:n_blk * SUBLANES, :]
                spr = jnp.where(row == 0, jnp.broadcast_to(hr, row.shape), pltpu.roll(last_r, 1, 0))
                spi = jnp.where(row == 0, jnp.broadcast_to(hi, row.shape), pltpu.roll(last_i, 1, 0))
            acc[b][0] = acc[b][0] + (lr * spr + li * spi)
            acc[b][1] = acc[b][1] + (li * spr - lr * spi)

        _chunk_scan(lre_ref, lim_ref, bs, car_ref, are[...], -aim[...], tt, reverse=True, on_block=on_block)
        for b in bs:
            dare_ref[...] += jnp.sum(acc[b][0], axis=0, keepdims=True)
            daim_ref[...] += jnp.sum(acc[b][1], axis=0, keepdims=True)
            dyp = dyp_ref[b]
            up = up_ref[b]
            dyb = dyp.astype(BF16)
            ub = up.astype(BF16)
            lrb = lre_ref[b].astype(BF16)
            lib = lim_ref[b].astype(BF16)
            dup = d_ref[...] * dyp + _dot_nt(lrb, bbre[0]) + _dot_nt(lib, bbim[0])
            _store_chunked(dup, du_ref, b, tt)
            dbbre_ref[0] += _dot_tn(ub, lrb)
            dbbim_ref[0] += _dot_tn(ub, lib)
            dcre_ref[0] += _dot_tn(dyb, sre_ref[b].astype(BF16))
            dcim_ref[0] += _dot_tn(dyb, sim_ref[b].astype(BF16))
            dd_ref[...] += jnp.sum(dyp * up, axis=0, keepdims=True)

        @pl.when((j == N_JBLK - 1) & (tr == nt - 1))
        def _():
            mine, sends = exchange()
            for cp in sends + mine:
                cp.wait()

    tok = lambda j, t: (0, nt - 1 - t, j)
    halo = lambda j, t: (0, jnp.maximum((nt - 1 - t) * rows8 - 1, 0), j)
    blk3 = lambda j, t: (j, 0, 0)
    row1 = lambda j, t: (0, j)
    acc_shape = jax.ShapeDtypeStruct((N_JBLK, JB_CH, JB_ST), F32)
    return _pcall(
        body, name="ssm_bwd", grid=(N_JBLK, nt),
        out_shape=(jax.ShapeDtypeStruct((n_seq, seq, SSM_W), F32), acc_shape, acc_shape, acc_shape, acc_shape,
                   jax.ShapeDtypeStruct((1, N_JBLK * JB_ST), F32), jax.ShapeDtypeStruct((1, N_JBLK * JB_ST), F32),
                   jax.ShapeDtypeStruct((1, SSM_W), F32),
                   jax.ShapeDtypeStruct((N_DEV,) + g_out.shape[1:], F32),
                   jax.ShapeDtypeStruct((N_DEV,) + g_glu.shape[1:], F32)),
        in_specs=[pl.BlockSpec((n_seq, tt, JB_CH), tok), pl.BlockSpec((n_seq, tt, JB_CH), tok),
                  pl.BlockSpec((n_seq, tt, JB_ST), tok), pl.BlockSpec((n_seq, tt, JB_ST), tok),
                  pl.BlockSpec((n_seq, SUBLANES, JB_ST), halo), pl.BlockSpec((n_seq, SUBLANES, JB_ST), halo),
                  pl.BlockSpec((1, JB_CH, JB_ST), blk3), pl.BlockSpec((1, JB_CH, JB_ST), blk3),
                  pl.BlockSpec((1, JB_ST, JB_CH), blk3), pl.BlockSpec((1, JB_ST, JB_CH), blk3),
                  pl.BlockSpec((1, JB_CH), row1), pl.BlockSpec((1, JB_ST), row1), pl.BlockSpec((1, JB_ST), row1),
                  HBM_SPEC, HBM_SPEC],
        out_specs=(pl.BlockSpec((n_seq, tt, JB_CH), tok),
                   pl.BlockSpec((1, JB_CH, JB_ST), blk3), pl.BlockSpec((1, JB_CH, JB_ST), blk3),
                   pl.BlockSpec((1, JB_CH, JB_ST), blk3), pl.BlockSpec((1, JB_CH, JB_ST), blk3),
                   pl.BlockSpec((1, JB_ST), row1), pl.BlockSpec((1, JB_ST), row1), pl.BlockSpec((1, JB_CH), row1),
                   HBM_SPEC, HBM_SPEC),
        scratch_shapes=[pltpu.VMEM((n_seq, tt, JB_ST), F32), pltpu.VMEM((n_seq, tt, JB_ST), F32),
                        pltpu.VMEM((n_seq, tt, JB_CH), F32), pltpu.VMEM((n_seq, tt, JB_CH), F32),
                        pltpu.VMEM((n_seq, SUBLANES, JB_ST), F32),
                        pltpu.SemaphoreType.DMA((7 * 2,)), pltpu.SemaphoreType.DMA((7 * 2,)),
                        pltpu.SemaphoreType.DMA((2,))],
        compiler_params=_params(2),
    )(dy, u, s_re, s_im, s_re, s_im, bb_re, bb_im, c_re_t, c_imn_t, d_row, ab_re, ab_im, g_out, g_glu)


def _mix(x2, tgt2, y, proj, gf, b_glu, conv8, w_glu_f, w_out_f, seq):
    n = x2.shape[0]
    tm = TOK_TILE
    tiles_per_seq = seq // tm
    rows8 = tm // SUBLANES

    def body(x_ref, t_ref, y_ref, zs_ref, h_ref, bc_ref, cc_ref, zc_ref, hp_ref, ccp_ref,
             gf_ref, bg_ref, cw_ref, wg_ref, wo_ref,
             dh2_ref, dy_ref, dzs_ref, dbc_ref, dzc_ref, dyc_ref,
             dwo_ref, dwg_ref, loss_ref, dgf_ref, dbg_ref, dcw_ref):
        i = pl.program_id(0)

        @pl.when(i == 0)
        def _():
            for r in (dwo_ref, dwg_ref, loss_ref, dgf_ref, dbg_ref, dcw_ref):
                r[...] = jnp.zeros_like(r)

        yv = y_ref[...]
        y1, dgelu = _gelu_and_grad(yv)
        y1b = y1.astype(BF16)
        gate = _sigmoid(_dot(y1b, wg_ref[...]) + bg_ref[...])
        y2 = y1 * gate
        szs, dszs = _silu_and_grad(zs_ref[...])
        yssm = y2 * szs
        hv = h_ref[...]
        ccv = cc_ref[...]
        bcv = bc_ref[...]
        v = ccv * hv
        first = (i % tiles_per_seq) == 0
        vhalo = jnp.where(first, 0.0, ccp_ref[...] * hp_ref[...])
        v1 = _shift_down(v, vhalo, 1)
        v2 = _shift_down(v, vhalo, 2)
        w0 = cw_ref[0:1, :]
        w1 = cw_ref[1:2, :]
        w2 = cw_ref[2:3, :]
        yc = w0 * v2 + w1 * v1 + w2 * v
        szc, dszc = _silu_and_grad(zc_ref[...])
        yconv = (bcv * yc) * szc
        ysb = yssm.astype(BF16)
        ycb = yconv.astype(BF16)
        h2 = x_ref[...] + _dot(ysb, wo_ref[0:SSM_W, :]) + _dot(ycb, wo_ref[SSM_W:, :])
        r2 = lax.rsqrt(jnp.mean(h2 * h2, axis=-1, keepdims=True) + EPS)
        hn = h2 * r2
        gfv = gf_ref[...]
        err = hn * gfv - t_ref[...]
        loss_ref[...] += 0.5 * jnp.sum(jnp.mean(err * err, axis=-1, keepdims=True))
        dout = err * (1.0 / D_MODEL)
        dgf_ref[...] += jnp.sum(dout * hn, axis=0, keepdims=True)
        dn = dout * gfv
        dh2 = r2 * (dn - hn * jnp.mean(dn * hn, axis=-1, keepdims=True))
        dh2_ref[...] = dh2
        dh2b = dh2.astype(BF16)
        dwo_ref[0:SSM_W, :] += _dot_tn(ysb, dh2b)
        dwo_ref[SSM_W:, :] += _dot_tn(ycb, dh2b)
        dyssm = _dot_nt(dh2b, wo_ref[0:SSM_W, :])
        dyconv = _dot_nt(dh2b, wo_ref[SSM_W:, :])
        dy2 = dyssm * szs
        dzs_ref[...] = dyssm * y2 * dszs
        dgp = dy2 * y1 * (gate * (1.0 - gate))
        dgpb = dgp.astype(BF16)
        dy1 = dy2 * gate + _dot_nt(dgpb, wg_ref[...])
        dwg_ref[...] += _dot_tn(y1b, dgpb)
        dbg_ref[...] += jnp.sum(dgp, axis=0, keepdims=True)
        dy_ref[...] = dy1 * dgelu
        dbc_ref[...] = dyconv * yc * szc
        dyc = dyconv * bcv * szc
        dyc_ref[...] = dyc
        dzc_ref[...] = dyconv * bcv * yc * dszc
        dcw_ref[0:1, :] += jnp.sum(dyc * v2, axis=0, keepdims=True)
        dcw_ref[1:2, :] += jnp.sum(dyc * v1, axis=0, keepdims=True)
        dcw_ref[2:3, :] += jnp.sum(dyc * v, axis=0, keepdims=True)

    tile_d = pl.BlockSpec((tm, D_MODEL), lambda i: (i, 0))
    tile_s = pl.BlockSpec((tm, SSM_W), lambda i: (i, 0))
    seg_of = lambda c: pl.BlockSpec((tm, SSM_W), lambda i: (i, c))
    halo_of = lambda c: pl.BlockSpec((SUBLANES, SSM_W), lambda i: (jnp.maximum(i * rows8 - 1, 0), c))
    const = lambda shape: pl.BlockSpec(shape, lambda i: (0,) * len(shape))
    seg = jax.ShapeDtypeStruct((n, SSM_W), F32)
    return _pcall(
        body, name="mix", grid=(n // tm,),
        out_shape=(jax.ShapeDtypeStruct((n, D_MODEL), F32), seg, seg, seg, seg, seg,
                   jax.ShapeDtypeStruct((D_MODEL, D_MODEL), F32), jax.ShapeDtypeStruct((SSM_W, SSM_W), F32),
                   jax.ShapeDtypeStruct((SUBLANES, LANES), F32), jax.ShapeDtypeStruct((1, D_MODEL), F32),
                   jax.ShapeDtypeStruct((1, SSM_W), F32), jax.ShapeDtypeStruct((SUBLANES, CONV_W), F32)),
        in_specs=[tile_d, tile_d, tile_s, seg_of(SEG_ZS), seg_of(SEG_H), seg_of(SEG_BC), seg_of(SEG_CC), seg_of(SEG_ZC),
                  halo_of(SEG_H), halo_of(SEG_CC),
                  const((1, D_MODEL)), const((1, SSM_W)), const((SUBLANES, CONV_W)),
                  const((SSM_W, SSM_W)), const((D_MODEL, D_MODEL))],
        out_specs=(tile_d, tile_s, tile_s, tile_s, tile_s, tile_s,
                   const((D_MODEL, D_MODEL)), const((SSM_W, SSM_W)), const((SUBLANES, LANES)),
                   const((1, D_MODEL)), const((1, SSM_W)), const((SUBLANES, CONV_W))),
        compiler_params=_params(1),
    )(x2, tgt2, y, proj, proj, proj, proj, proj, proj, proj, gf, b_glu, conv8, w_glu_f, w_out_f)


def _in_bwd(x2, dh2, du, dzs, dyc, proj, dbc, dzc, g1, conv8, w_full, seq):
    n = x2.shape[0]
    tm = TOK_TILE
    n_tiles = n // tm
    tiles_per_seq = seq // tm
    rows8 = tm // SUBLANES
    n_blk8 = n // SUBLANES

    def body(x_ref, dh2_ref, du_ref, dzs_ref, dyc_ref, dycn_ref, h_ref, cc_ref, dbc_ref, dzc_ref,
             g_ref, cw_ref, w_ref, gx_ref, dp_ref, dg_ref):
        i = pl.program_id(0)

        @pl.when(i == 0)
        def _():
            dg_ref[...] = jnp.zeros_like(dg_ref)

        dyc = dyc_ref[...]
        last = (i % tiles_per_seq) == tiles_per_seq - 1
        nhalo = jnp.where(last, 0.0, dycn_ref[...])
        dv = (cw_ref[2:3, :] * dyc + cw_ref[1:2, :] * _shift_up(dyc, nhalo, 1)
              + cw_ref[0:1, :] * _shift_up(dyc, nhalo, 2))
        parts = (du_ref[...], dzs_ref[...], dv * cc_ref[...], dbc_ref[...], dv * h_ref[...], dzc_ref[...])
        dxn = jnp.zeros((tm, D_MODEL), F32)
        for k, p in enumerate(parts):
            pb = p.astype(BF16)
            dp_ref[:, k * SSM_W:(k + 1) * SSM_W] = pb
            dxn = dxn + _dot_nt(pb, w_ref[:, k * SSM_W:(k + 1) * SSM_W])
        x = x_ref[...]
        r = lax.rsqrt(jnp.mean(x * x, axis=-1, keepdims=True) + EPS)
        xh = x * r
        dg_ref[...] += jnp.sum(dxn * xh, axis=0, keepdims=True)
        dn = dxn * g_ref[...]
        gx_ref[...] = dh2_ref[...] + r * (dn - xh * jnp.mean(dn * xh, axis=-1, keepdims=True))

    tile_d = pl.BlockSpec((tm, D_MODEL), lambda i: (i, 0))
    tile_s = pl.BlockSpec((tm, SSM_W), lambda i: (i, 0))
    seg_of = lambda c: pl.BlockSpec((tm, SSM_W), lambda i: (i, c))
    nhalo = pl.BlockSpec((SUBLANES, SSM_W), lambda i: (jnp.minimum((i + 1) * rows8, n_blk8 - 1), 0))
    const = lambda shape: pl.BlockSpec(shape, lambda i: (0,) * len(shape))
    return _pcall(
        body, name="in_bwd", grid=(n_tiles,),
        out_shape=(jax.ShapeDtypeStruct((n, D_MODEL), F32), jax.ShapeDtypeStruct((n, IN_COLS), BF16),
                   jax.ShapeDtypeStruct((SUBLANES, D_MODEL), F32)),
        in_specs=[tile_d, tile_d, tile_s, tile_s, tile_s, nhalo, seg_of(SEG_H), seg_of(SEG_CC), tile_s, tile_s,
                  const((1, D_MODEL)), const((SUBLANES, CONV_W)), const((D_MODEL, IN_COLS))],
        out_specs=(tile_d, pl.BlockSpec((tm, IN_COLS), lambda i: (i, 0)), const((SUBLANES, D_MODEL))),
        compiler_params=_params(1),
    )(x2, dh2, du, dzs, dyc, dyc, proj, proj, dbc, dzc, g1, conv8, w_full)


def _dw_in_exchange(order, xn, dproj, smalls):
    n = xn.shape[0]
    tk = 512
    nk = n // tk
    piece = (D_MODEL, COLS_PER_DEV)
    n_small = len(smalls)

    def body(order_ref, xn_ref, dp_ref, *refs):
        del order_ref
        sm_refs = refs[:n_small]
        own_ref, rchip_ref = refs[n_small:n_small + 2]
        rsm_refs = refs[n_small + 2:2 * n_small + 2]
        (acc, stage, sbuf, give_send, give_recv, keep_send, keep_recv,
         sm_send, sm_recv, sm_loc) = refs[2 * n_small + 2:]
        s = pl.program_id(0)
        x, y, c = _mesh_pos()
        sib = (x, y, 1 - c)
        chips = [(1 - x, 1 - y), (1 - x, y), (x, 1 - y)]
        gather = _TwoLevelGather(list(sm_refs), [functools.partial(lambda r, dev: r.at[dev], r) for r in rsm_refs],
                                 sm_send, sm_recv, sm_loc)

        def half(i, core):
            return acc.at[i % 2, :, pl.ds(pl.multiple_of(core * COLS_PER_DEV, LANES), COLS_PER_DEV)]

        def give(i):
            return pltpu.make_async_remote_copy(src_ref=half(i, 1 - c), dst_ref=stage.at[i], send_sem=give_send.at[i],
                                                recv_sem=give_recv.at[i], device_id=sib, device_id_type=MESH)

        def keep(i):
            return pltpu.make_async_remote_copy(src_ref=sbuf.at[i], dst_ref=rchip_ref.at[i], send_sem=keep_send.at[i],
                                                recv_sem=keep_recv.at[i], device_id=(*chips[i], c), device_id_type=MESH)

        def chip_sum(i):
            give(i).wait_recv()
            mine = [acc[i % 2, :, cc * COLS_PER_DEV:(cc + 1) * COLS_PER_DEV] for cc in range(2)]
            return jnp.where(c == 0, mine[0], mine[1]) + stage[i]

        @pl.when(s == 0)
        def _():
            gather.start()

        @pl.when(s == N_CHIP // 2)
        def _():
            gather.forward()

        for k in range(2, N_CHIP):
            @pl.when(s == k)
            def _(k=k):
                give(k - 2).wait_send()

        slot = s % 2
        acc[slot] = _dot_tn(xn_ref[pl.ds(0, tk), :], dp_ref[pl.ds(0, tk), :])

        def kstep(kk, carry):
            off = pl.multiple_of(kk * tk, tk)
            acc[slot] += _dot_tn(xn_ref[pl.ds(off, tk), :], dp_ref[pl.ds(off, tk), :])
            return carry

        n_first = min(nk, 3)
        lax.fori_loop(1, n_first, kstep, 0)
        for k in range(1, N_CHIP):
            @pl.when(s == k)
            def _(k=k):
                sbuf[k - 1] = chip_sum(k - 1).astype(BF16)
                keep(k - 1).start()

        lax.fori_loop(n_first, nk, kstep, 0)

        for k in range(N_CHIP):
            @pl.when(s == k)
            def _(k=k):
                give(k).start()

        @pl.when(s == N_CHIP - 1)
        def _():
            own_ref[...] = chip_sum(N_CHIP - 1)
            give(N_CHIP - 2).wait_send()
            give(N_CHIP - 1).wait_send()
            for i in range(3):
                keep(i).wait()
            gather.finish()

    grid_spec = pltpu.PrefetchScalarGridSpec(
        num_scalar_prefetch=1, grid=(N_CHIP,),
        in_specs=[pl.BlockSpec(memory_space=pltpu.VMEM),
                  pl.BlockSpec((n, COLS_PER_CHIP), lambda s, order: (0, order[s])),
                  *([HBM_SPEC] * n_small)],
        out_specs=(pl.BlockSpec(piece, lambda s, order: (0, 0)), HBM_SPEC, *([HBM_SPEC] * n_small)),
        scratch_shapes=[pltpu.VMEM((2, D_MODEL, COLS_PER_CHIP), F32), pltpu.VMEM((4,) + piece, F32),
                        pltpu.VMEM((3,) + piece, BF16),
                        pltpu.SemaphoreType.DMA((4,)), pltpu.SemaphoreType.DMA((4,)),
                        pltpu.SemaphoreType.DMA((3,)), pltpu.SemaphoreType.DMA((3,)),
                        pltpu.SemaphoreType.DMA((7 * n_small,)), pltpu.SemaphoreType.DMA((7 * n_small,)),
                        pltpu.SemaphoreType.DMA((n_small,))])
    return _pcall(
        body, name="dw_in_exchange", grid_spec=grid_spec,
        out_shape=(jax.ShapeDtypeStruct(piece, F32), jax.ShapeDtypeStruct((3,) + piece, BF16),
                   *(jax.ShapeDtypeStruct((N_DEV,) + a.shape, a.dtype) for a in smalls)),
        compiler_params=_params(1),
    )(order, xn, dproj, *smalls)


def _adamw(g, w, m, v):
    m_new = ADAM_B1 * m + (1.0 - ADAM_B1) * g
    v_new = ADAM_B2 * v + (1.0 - ADAM_B2) * (g * g)
    m_hat = m_new / (1.0 - ADAM_B1 ** ADAM_STEP)
    v_hat = v_new / (1.0 - ADAM_B2 ** ADAM_STEP)
    delta = -ADAM_LR * (m_hat / (jnp.sqrt(v_hat) + ADAM_EPS) + ADAM_WD * w)
    return delta, m_new, v_new


def _reduce_adam(recv, w, m, v, name, row_tile):
    rows, cols = w.shape

    def body(r_ref, w_ref, m_ref, v_ref, g_ref, d_ref, nm_ref, nv_ref):
        g = r_ref[0]
        for s in range(1, N_DEV):
            g = g + r_ref[s]
        g_ref[...] = g
        d_ref[...], nm_ref[...], nv_ref[...] = _adamw(g, w_ref[...], m_ref[...], v_ref[...])

    tile = pl.BlockSpec((row_tile, cols), lambda i: (i, 0))
    shp = jax.ShapeDtypeStruct((rows, cols), F32)
    return _pcall(
        body, name=name, grid=(rows // row_tile,),
        out_shape=(shp,) * 4,
        in_specs=[pl.BlockSpec((N_DEV, row_tile, cols), lambda i: (0, i, 0)), tile, tile, tile],
        out_specs=(tile,) * 4,
        compiler_params=_params(1),
    )(recv, w, m, v)


def _reduce_adam_w_in(own, rchip, w, m, v):
    rows, cols = w.shape
    row_tile = 256

    def body(o_ref, r_ref, w_ref, m_ref, v_ref, g_ref, d_ref, nm_ref, nv_ref):
        g = o_ref[...]
        for s in range(3):
            g = g + r_ref[s].astype(F32)
        g_ref[...] = g
        d_ref[...], nm_ref[...], nv_ref[...] = _adamw(g, w_ref[...], m_ref[...], v_ref[...])

    tile = pl.BlockSpec((row_tile, cols), lambda i: (i, 0))
    shp = jax.ShapeDtypeStruct((rows, cols), F32)
    return _pcall(
        body, name="reduce_adam_w_in", grid=(rows // row_tile,),
        out_shape=(shp,) * 4,
        in_specs=[tile, pl.BlockSpec((3, row_tile, cols), lambda i: (0, i, 0)), tile, tile, tile],
        out_specs=(tile,) * 4,
        compiler_params=_params(1),
    )(own, rchip, w, m, v)


_SMALL_LEAVES = ("norm_gain", "final_norm_gain", "b_glu", "ssm_a_re", "ssm_a_im", "ssm_log_dt", "ssm_d", "conv_w",
                 "ssm_c_re", "ssm_c_im", "ssm_b_re", "ssm_b_im")


def _reduce_adam_small(r_pack, r_gc, r_gb, wmv):
    n_leaf = len(_SMALL_LEAVES)

    def body(*refs):
        rp_ref, rgc_ref, rgb_ref = refs[:3]
        w_refs = refs[3:3 + 3 * n_leaf]
        loss_ref = refs[3 + 3 * n_leaf]
        o_refs = refs[4 + 3 * n_leaf:4 + 7 * n_leaf]
        own_conv = refs[-1]

        def total(ref):
            acc = ref[0].astype(F32)
            for s in range(1, N_DEV):
                acc = acc + ref[s].astype(F32)
            return acc

        sp = total(rp_ref)
        sgc = total(rgc_ref)
        sgb = total(rgb_ref)
        loss_ref[...] = sp[ROW_LOSS:ROW_LOSS + SUBLANES, 0:LANES]

        def wide(r):
            return jnp.concatenate([sp[r:r + 1, :], sp[r + 1:r + 2, :]], axis=1)

        s5 = slice(ROW_S5, ROW_S5 + N_GROUPS)
        eye = (lax.broadcasted_iota(jnp.int32, (N_GROUPS, N_GROUPS), 0)
               == lax.broadcasted_iota(jnp.int32, (N_GROUPS, N_GROUPS), 1)).astype(F32)
        d_row = sp[ROW_BGLU_D + 1:ROW_BGLU_D + 2, :]
        me = 4 * lax.axis_index("x") + 2 * lax.axis_index("y") + lax.axis_index("c")
        for k in range(N_DEV):
            @pl.when(me == k)
            def _(k=k):
                own_conv[...] = sp[ROW_CONV:ROW_CONV + SUBLANES, k * CONV_COLS_PER_DEV:(k + 1) * CONV_COLS_PER_DEV]
        grads = {
            "norm_gain": wide(ROW_NORM_GAIN),
            "final_norm_gain": wide(ROW_FINAL_GAIN),
            "b_glu": sp[ROW_BGLU_D:ROW_BGLU_D + 1, :],
            "ssm_a_re": sp[s5, LANE_A_RE:LANE_A_RE + STATE],
            "ssm_a_im": sp[s5, LANE_A_IM:LANE_A_IM + STATE],
            "ssm_log_dt": jnp.sum(sp[s5, LANE_LOG_DT:LANE_LOG_DT + 1] * eye, axis=0, keepdims=True),
            "ssm_d": jnp.concatenate([d_row[:, g * GROUP:(g + 1) * GROUP] for g in range(N_GROUPS)], axis=0),
            "conv_w": own_conv[0:3, :],
            "ssm_c_re": sgc[:, 0:STATE],
            "ssm_c_im": sgc[:, STATE:2 * STATE],
            "ssm_b_re": sgb[:, 0:STATE],
            "ssm_b_im": sgb[:, STATE:2 * STATE],
        }
        for i, name in enumerate(_SMALL_LEAVES):
            g = grads[name]
            w_ref, m_ref, v_ref = w_refs[3 * i:3 * i + 3]
            o_g, o_d, o_m, o_v = o_refs[4 * i:4 * i + 4]
            o_g[...] = g
            o_d[...], o_m[...], o_v[...] = _adamw(g, w_ref[...], m_ref[...], v_ref[...])

    flat_w = [a for name in _SMALL_LEAVES for a in wmv[name]]
    leaf_shapes = [jax.ShapeDtypeStruct(wmv[name][0].shape, F32) for name in _SMALL_LEAVES for _ in range(4)]
    outs = _pcall(
        body, name="reduce_adam_small",
        out_shape=(jax.ShapeDtypeStruct((SUBLANES, LANES), F32), *leaf_shapes),
        scratch_shapes=[pltpu.VMEM((SUBLANES, CONV_COLS_PER_DEV), F32)],
        compiler_params=_params(0),
    )(r_pack, r_gc, r_gb, *flat_w)
    leaves = {name: outs[1 + 4 * i:5 + 4 * i] for i, name in enumerate(_SMALL_LEAVES)}
    return outs[0], leaves


def _block_diag(m4):
    eye = jnp.eye(SUBLANES, dtype=m4.dtype)
    j, g, a, b = m4.shape
    return jnp.einsum("jgab,gk->jgakb", m4, eye).reshape(j, g * a, g * b)


def _block_diag_extract(dense, a, b):
    d5 = dense.reshape(N_JBLK, SUBLANES, a, SUBLANES, b)
    return jnp.stack([d5[:, g, :, g, :] for g in range(SUBLANES)], axis=1)


def kernel(x, norm_gain, w_in, ssm_a_re, ssm_a_im, ssm_log_dt, ssm_b_re, ssm_b_im, ssm_c_re, ssm_c_im, ssm_d, w_glu, b_glu, conv_w, w_out, final_norm_gain, loss_target, m_norm_gain, m_w_in, m_ssm_a_re, m_ssm_a_im, m_ssm_log_dt, m_ssm_b_re, m_ssm_b_im, m_ssm_c_re, m_ssm_c_im, m_ssm_d, m_w_glu, m_b_glu, m_conv_w, m_w_out, m_final_norm_gain, v_norm_gain, v_w_in, v_ssm_a_re, v_ssm_a_im, v_ssm_log_dt, v_ssm_b_re, v_ssm_b_im, v_ssm_c_re, v_ssm_c_im, v_ssm_d, v_w_glu, v_b_glu, v_conv_w, v_w_out, v_final_norm_gain):
    n_seq, seq, _ = x.shape
    n = n_seq * seq

    gh_p = lambda b4: jnp.transpose(b4, (0, 1, 3, 2)).reshape(N_GROUPS * GROUP, STATE)
    rep = lambda a: jnp.repeat(a, GROUP, axis=0)
    a_re_x, a_im_x = rep(ssm_a_re[0]), rep(ssm_a_im[0])
    log_dt_x = rep(ssm_log_dt[0].reshape(N_GROUPS, 1))
    b_re2, b_im2 = gh_p(ssm_b_re), gh_p(ssm_b_im)
    ab_re_x, ab_im_x, bb_re2, bb_im2 = _ssm_disc(a_re_x, a_im_x, log_dt_x, b_re2, b_im2)
    ab_re = ab_re_x[::GROUP].reshape(1, N_GROUPS * STATE)
    ab_im = ab_im_x[::GROUP].reshape(1, N_GROUPS * STATE)

    def bb_mat(bb2):
        return _block_diag(bb2.reshape(N_JBLK, SUBLANES, GROUP, STATE)).astype(BF16)

    def c_mat(c3, sign):
        t = jnp.transpose(c3.reshape(N_JBLK, SUBLANES, GROUP, STATE), (0, 1, 3, 2))
        return _block_diag(sign * t).astype(BF16)

    bb_re_m, bb_im_m = bb_mat(bb_re2), bb_mat(bb_im2)
    c_re_m, c_imn_m = c_mat(ssm_c_re[0], 1.0), c_mat(ssm_c_im[0], -1.0)
    d_row = ssm_d[0].reshape(1, SSM_W)

    x2 = x.reshape(n, D_MODEL)
    tgt2 = loss_target.reshape(n, D_MODEL)
    mx, my, mc = lax.axis_index("x"), lax.axis_index("y"), lax.axis_index("c")
    chip_ids = [2 * cx + cy for cx, cy in ((mx, my), (1 - mx, my), (mx, 1 - my), (1 - mx, 1 - my))]
    arrival = chip_ids
    xn, proj, w_in_f = _in_proj(jnp.stack(arrival).astype(jnp.int32), x2, norm_gain, w_in[0].astype(BF16))
    u3 = proj.reshape(n_seq, seq, IN_COLS)
    conv_p = jnp.pad(conv_w[0], ((0, SUBLANES - 3), (0, LANES - CONV_COLS_PER_DEV)))
    s_re, s_im, y3, w_out_f, w_glu_f, conv_all = _ssm_fwd(
        u3, bb_re_m, bb_im_m, c_re_m, c_imn_m, d_row, ab_re, ab_im,
        w_out[0].astype(BF16), w_glu[0].astype(BF16), conv_p, n_seq, seq)
    conv8 = jnp.transpose(conv_all[:, :, :CONV_COLS_PER_DEV], (1, 0, 2)).reshape(SUBLANES, CONV_W)
    (dh2, dy, dzs, dbc, dzc, dyc, dw_out, dw_glu, loss_t, dgf, dbg, dcw) = _mix(
        x2, tgt2, y3.reshape(n, SSM_W), proj, final_norm_gain.reshape(1, D_MODEL), b_glu, conv8,
        w_glu_f, w_out_f, seq)

    du3, dc_re_d, dc_im_d, dbb_re_d, dbb_im_d, dab_re, dab_im, dd, r_out, r_glu = _ssm_bwd(
        dy.reshape(n_seq, seq, SSM_W), u3, s_re, s_im, bb_re_m, bb_im_m, c_re_m, c_imn_m, d_row, ab_re, ab_im,
        dw_out.reshape(N_DEV, OUT_ROWS_PER_DEV, D_MODEL), dw_glu.reshape(N_DEV, GLU_ROWS_PER_DEV, SSM_W), n_seq, seq)
    du = du3.reshape(n, SSM_W)
    grad_x2, dproj, dg8 = _in_bwd(x2, dh2, du, dzs, dyc, proj, dbc, dzc, norm_gain, conv8, w_in_f, seq)
    pack, gc, gb = _ssm_disc_bwd_pack(
        a_re_x, a_im_x, log_dt_x, b_re2, b_im2, dab_re.reshape(N_GROUPS, STATE), dab_im.reshape(N_GROUPS, STATE),
        dbb_re_d, dbb_im_d, loss_t, dg8, dgf, dbg, dd, dcw, dc_re_d, dc_im_d)

    order = [chip_ids[3], chip_ids[1], chip_ids[2], chip_ids[0]]
    own_in, rchip_in, r_pack, r_gc, r_gb = _dw_in_exchange(
        jnp.stack(order).astype(jnp.int32), xn, dproj, [pack, gc, gb])

    flat2 = lambda a: a.reshape(a.shape[-2:]) if a.ndim > 2 else a.reshape(1, -1)
    c2 = lambda a: a.reshape(N_GROUPS * GROUP, STATE)
    wmv = dict(norm_gain=(norm_gain, m_norm_gain, v_norm_gain),
               final_norm_gain=tuple(flat2(a) for a in (final_norm_gain, m_final_norm_gain, v_final_norm_gain)),
               b_glu=(b_glu, m_b_glu, v_b_glu),
               ssm_a_re=tuple(flat2(a) for a in (ssm_a_re, m_ssm_a_re, v_ssm_a_re)),
               ssm_a_im=tuple(flat2(a) for a in (ssm_a_im, m_ssm_a_im, v_ssm_a_im)),
               ssm_log_dt=(ssm_log_dt, m_ssm_log_dt, v_ssm_log_dt),
               ssm_d=tuple(flat2(a) for a in (ssm_d, m_ssm_d, v_ssm_d)),
               conv_w=tuple(flat2(a) for a in (conv_w, m_conv_w, v_conv_w)),
               ssm_c_re=tuple(c2(a) for a in (ssm_c_re, m_ssm_c_re, v_ssm_c_re)),
               ssm_c_im=tuple(c2(a) for a in (ssm_c_im, m_ssm_c_im, v_ssm_c_im)),
               ssm_b_re=(b_re2, gh_p(m_ssm_b_re), gh_p(v_ssm_b_re)),
               ssm_b_im=(b_im2, gh_p(m_ssm_b_im), gh_p(v_ssm_b_im)))

    res_in = _reduce_adam_w_in(own_in, rchip_in, w_in[0], m_w_in[0], v_w_in[0])
    res_out = _reduce_adam(r_out, w_out[0], m_w_out[0], v_w_out[0], "reduce_adam_w_out", OUT_ROWS_PER_DEV)
    res_glu = _reduce_adam(r_glu, w_glu[0], m_w_glu[0], v_w_glu[0], "reduce_adam_w_glu", GLU_ROWS_PER_DEV)
    loss8, small = _reduce_adam_small(r_pack, r_gc, r_gb, wmv)
    loss = loss8[0, 0]

    shapes = dict(norm_gain=(1, D_MODEL), ssm_a_re=(1, N_GROUPS, STATE), ssm_a_im=(1, N_GROUPS, STATE),
                  ssm_log_dt=(1, N_GROUPS), ssm_c_re=(1, N_GROUPS, GROUP, STATE), ssm_c_im=(1, N_GROUPS, GROUP, STATE),
                  ssm_d=(1, N_GROUPS, GROUP), b_glu=(1, SSM_W), final_norm_gain=(D_MODEL,),
                  conv_w=(1, 3, CONV_COLS_PER_DEV))
    big = dict(w_in=res_in, w_glu=res_glu, w_out=res_out)

    def leaf(kind, name):
        if name in big:
            return big[name][kind][None]
        if name in ("ssm_b_re", "ssm_b_im"):
            return jnp.transpose(small[name][kind].reshape(1, N_GROUPS, GROUP, STATE), (0, 1, 3, 2))
        return small[name][kind].reshape(shapes[name])

    order = ["norm_gain", "w_in", "ssm_a_re", "ssm_a_im", "ssm_log_dt", "ssm_b_re", "ssm_b_im", "ssm_c_re",
             "ssm_c_im", "ssm_d", "w_glu", "b_glu", "conv_w", "w_out", "final_norm_gain"]
    outs = [loss, grad_x2.reshape(x.shape)]
    for kind in range(4):
        outs += [leaf(kind, name) for name in order]
    return tuple(outs)
```

```python
import functools
import math

import jax
import jax.numpy as jnp
from jax import lax
from jax.experimental import pallas as pl
from jax.experimental.pallas import tpu as pltpu

F32 = jnp.float32
BF16 = jnp.bfloat16

N_DEV = 8
D_MODEL = 1024
SSM_W = 512
CONV_W = 512
N_GROUPS = 32
GROUP = 16
STATE = 64
IN_COLS = 3072
SEG_U, SEG_ZS, SEG_H, SEG_BC, SEG_CC, SEG_ZC = range(6)
COLS_PER_DEV = IN_COLS // N_DEV
N_CHIP = N_DEV // 2
COLS_PER_CHIP = 2 * COLS_PER_DEV
OUT_ROWS_PER_DEV = D_MODEL // N_DEV
GLU_ROWS_PER_DEV = SSM_W // N_DEV
CONV_COLS_PER_DEV = CONV_W // N_DEV
EPS = 1e-6

N_JBLK = 4
JB_CH = SSM_W // N_JBLK
JB_ST = N_GROUPS * STATE // N_JBLK

ADAM_LR = 0.001
ADAM_B1 = 0.9
ADAM_B2 = 0.999
ADAM_EPS = 1e-08
ADAM_WD = 0.01
ADAM_STEP = 10

SUBLANES = 8
LANES = 128
VMEM_LIMIT = 48 * 1024 * 1024
TOK_TILE = 256
IN_TILE = 1024
SCAN_TILE = 1024

MESH = pl.DeviceIdType.MESH
HBM_SPEC = pl.BlockSpec(memory_space=pltpu.HBM)


def _pcall(body, **kw):
    return pl.pallas_call(body, **kw)


def _params(n_grid):
    return pltpu.CompilerParams(dimension_semantics=("arbitrary",) * n_grid,
                                vmem_limit_bytes=VMEM_LIMIT)


def _dot(a, b):
    return jnp.dot(a, b, preferred_element_type=F32)


def _dot_nt(a, b):
    return lax.dot_general(a, b, (((1,), (1,)), ((), ())), preferred_element_type=F32)


def _dot_tn(a, b):
    return lax.dot_general(a, b, (((0,), (0,)), ((), ())), preferred_element_type=F32)


def _sigmoid(z):
    return 1.0 / (1.0 + jnp.exp(-z))


_GELU_C = math.sqrt(2.0 / math.pi)


def _gelu_and_grad(y):
    inner = _GELU_C * (y + 0.044715 * (y * y * y))
    t = jnp.tanh(inner)
    g = 0.5 * y * (1.0 + t)
    dg = 0.5 * (1.0 + t) + 0.5 * y * (1.0 - t * t) * (_GELU_C * (1.0 + 3.0 * 0.044715 * (y * y)))
    return g, dg


def _silu_and_grad(z):
    s = _sigmoid(z)
    return z * s, s * (1.0 + z * (1.0 - s))


def _shift_down(v, halo, k):
    rolled = pltpu.roll(v, k, 0)
    row = lax.broadcasted_iota(jnp.int32, v.shape, 0)
    for r in range(k):
        rolled = jnp.where(row == r, halo[SUBLANES - k + r:SUBLANES - k + r + 1, :], rolled)
    return rolled


def _shift_up(v, halo, k):
    n = v.shape[0]
    rolled = pltpu.roll(v, n - k, 0)
    row = lax.broadcasted_iota(jnp.int32, v.shape, 0)
    for r in range(k):
        rolled = jnp.where(row == n - k + r, halo[r:r + 1, :], rolled)
    return rolled


def _mesh_pos():
    return lax.axis_index("x"), lax.axis_index("y"), lax.axis_index("c")


def _direct_copies(srcs_for, out_refs, send_sems, recv_sems, loc_sems):
    x, y, c = _mesh_pos()
    me_id = 4 * x + 2 * y + c
    n_arr = len(out_refs)
    dsts = [r.at[me_id] for r in out_refs]
    own = srcs_for(me_id)
    mine = [pltpu.make_async_copy(own[a], dsts[a], loc_sems.at[a]) for a in range(n_arr)]
    sends = []
    for k in range(1, N_DEV):
        px, py, pc = x ^ ((k >> 2) & 1), y ^ ((k >> 1) & 1), c ^ (k & 1)
        src = srcs_for(4 * px + 2 * py + pc)
        for a in range(n_arr):
            sends.append(pltpu.make_async_remote_copy(
                src_ref=src[a], dst_ref=dsts[a],
                send_sem=send_sems.at[(k - 1) * n_arr + a], recv_sem=recv_sems.at[(k - 1) * n_arr + a],
                device_id=(px, py, pc), device_id_type=MESH))
    return mine, sends


class _TwoLevelGather:
    def __init__(self, srcs, slots, send_sems, recv_sems, loc_sems):
        self.srcs, self.slots, self.n_arr = srcs, slots, len(srcs)
        self.send_sems, self.recv_sems, self.loc_sems = send_sems, recv_sems, loc_sems
        x, y, c = _mesh_pos()
        self.c = c
        self.me, self.sib = (x, y, c), (x, y, 1 - c)
        self.chips = [(1 - x, y), (x, 1 - y), (1 - x, 1 - y)]

    def _copies(self, k, block, to, from_src=False):
        dev = 4 * block[0] + 2 * block[1] + block[2]
        return [pltpu.make_async_remote_copy(
            src_ref=self.srcs[a] if from_src else self.slots[a](dev), dst_ref=self.slots[a](dev),
            send_sem=self.send_sems.at[k * self.n_arr + a], recv_sem=self.recv_sems.at[k * self.n_arr + a],
            device_id=to, device_id_type=MESH) for a in range(self.n_arr)]

    def _local(self):
        dev = 4 * self.me[0] + 2 * self.me[1] + self.me[2]
        return [pltpu.make_async_copy(self.srcs[a], self.slots[a](dev), self.loc_sems.at[a])
                for a in range(self.n_arr)]

    def start(self, chips=(0, 1, 2)):
        for cp in self._local() + self._copies(0, self.me, self.sib, True):
            cp.start()
        self.start_to(chips)

    def start_to(self, chips):
        for j in chips:
            for cp in self._copies(1 + j, self.me, (*self.chips[j], self.c), True):
                cp.start()

    def wait_own(self):
        for cp in self._local():
            cp.wait()

    def wait_sibling(self):
        for cp in self._copies(0, self.sib, self.me):
            cp.wait_recv()

    def wait_and_pass_on(self, j):
        chip = self.chips[j]
        for cp in self._copies(1 + j, (*chip, self.c), self.me):
            cp.wait_recv()
        for cp in self._copies(4 + j, (*chip, self.c), self.sib):
            cp.start()

    def wait_passed_on(self, j):
        for cp in self._copies(4 + j, (*self.chips[j], 1 - self.c), self.me):
            cp.wait_recv()

    def wait_sends(self):
        for cp in self._copies(0, self.me, self.sib, True):
            cp.wait_send()
        for j, chip in enumerate(self.chips):
            for cp in self._copies(1 + j, self.me, (*chip, self.c), True) + self._copies(4 + j, (*chip, self.c), self.sib):
                cp.wait_send()

    def forward(self):
        for j in range(3):
            self.wait_and_pass_on(j)

    def finish(self):
        self.wait_sibling()
        for j in range(3):
            self.wait_passed_on(j)
        self.wait_sends()
        self.wait_own()


def _disc(a_re, a_im, log_dt, b_re, b_im):
    dt = jnp.exp(log_dt)
    mag = jnp.exp(a_re * dt)
    ab_re = mag * jnp.cos(a_im * dt)
    ab_im = mag * jnp.sin(a_im * dt)
    den = a_re * a_re + a_im * a_im
    p_re = ab_re - 1.0
    p_im = ab_im
    q_re = (p_re * a_re + p_im * a_im) / den
    q_im = (p_im * a_re - p_re * a_im) / den
    bb_re = q_re * b_re - q_im * b_im
    bb_im = q_re * b_im + q_im * b_re
    return ab_re, ab_im, bb_re, bb_im


def _ssm_disc(a_re_x, a_im_x, log_dt_x, b_re, b_im):
    def body(are, aim, ldt, bre, bim, o_abre, o_abim, o_bbre, o_bbim):
        outs = _disc(are[...], aim[...], ldt[...], bre[...], bim[...])
        for o, v in zip((o_abre, o_abim, o_bbre, o_bbim), outs):
            o[...] = v

    shp = jax.ShapeDtypeStruct(a_re_x.shape, F32)
    return _pcall(body, name="ssm_disc", out_shape=(shp,) * 4)(a_re_x, a_im_x, log_dt_x, b_re, b_im)


def _split3(v):
    hi = v.astype(BF16)
    r1 = v - hi.astype(F32)
    mid = r1.astype(BF16)
    lo = (r1 - mid.astype(F32)).astype(BF16)
    return hi, mid, lo


def _select_dot(sel, v):
    return sum(_dot(sel, t) for t in _split3(v))


PACK_ROWS = 72
PACK_W = 512
ROW_FINAL_GAIN, ROW_NORM_GAIN, ROW_BGLU_D, ROW_CONV, ROW_LOSS, ROW_S5 = 0, 8, 16, 24, 32, 40
LANE_A_RE, LANE_A_IM, LANE_LOG_DT = 0, 128, 256


def _ssm_disc_bwd_pack(a_re_x, a_im_x, log_dt_x, b_re, b_im, g_ab_re, g_ab_im, dbb_re_d, dbb_im_d,
                       loss_t, dg8, dgf, dbg, dd, dcw, dc_re_d, dc_im_d):
    rows_gh = N_GROUPS * GROUP

    def body(are, aim, ldt, bre, bim, gabre, gabim, dbbre_ref, dbbim_ref,
             loss_ref, dg8_ref, dgf_ref, dbg_ref, dd_ref, dcw_ref, dcre_ref, dcim_ref,
             p_ref, gc_ref, gb_ref, gbb_re, gbb_im):
        r_g = lax.broadcasted_iota(jnp.int32, (N_GROUPS, rows_gh), 0)
        c_gh = lax.broadcasted_iota(jnp.int32, (N_GROUPS, rows_gh), 1)
        group_sum = (c_gh // GROUP == r_g).astype(BF16)
        r_gh = lax.broadcasted_iota(jnp.int32, (rows_gh, N_GROUPS), 0)
        c_g = lax.broadcasted_iota(jnp.int32, (rows_gh, N_GROUPS), 1)
        first_row = (r_gh == c_g * GROUP).astype(BF16)

        def diag_block(ref, j, gi):
            return ref[j, gi * GROUP:(gi + 1) * GROUP, gi * STATE:(gi + 1) * STATE]

        for j in range(N_JBLK):
            for gi in range(SUBLANES):
                r0 = (j * SUBLANES + gi) * GROUP
                gbb_re[r0:r0 + GROUP, :] = diag_block(dbbre_ref, j, gi)
                gbb_im[r0:r0 + GROUP, :] = diag_block(dbbim_ref, j, gi)
                both = jnp.concatenate([diag_block(dcre_ref, j, gi), -diag_block(dcim_ref, j, gi)], axis=1)
                gc_ref[r0:r0 + GROUP, :] = both.astype(BF16)

        _, vjp = jax.vjp(_disc, are[...], aim[...], ldt[...], bre[...], bim[...])
        d_are, d_aim, d_ldt, d_bre, d_bim = vjp((_select_dot(first_row, gabre[...]), _select_dot(first_row, gabim[...]),
                                                 gbb_re[...], gbb_im[...]))
        gb_ref[...] = jnp.concatenate([d_bre, d_bim], axis=1).astype(BF16)

        p_ref[...] = jnp.zeros_like(p_ref)
        half = D_MODEL // 2
        for r, src in ((ROW_FINAL_GAIN, dgf_ref), (ROW_NORM_GAIN, dg8_ref)):
            p_ref[r:r + 1, :] = src[0:1, 0:half]
            p_ref[r + 1:r + 2, :] = src[0:1, half:D_MODEL]
        p_ref[ROW_BGLU_D:ROW_BGLU_D + 1, :] = dbg_ref[...]
        p_ref[ROW_BGLU_D + 1:ROW_BGLU_D + 2, :] = dd_ref[...]
        p_ref[ROW_CONV:ROW_CONV + SUBLANES, :] = dcw_ref[...]
        p_ref[ROW_LOSS:ROW_LOSS + SUBLANES, 0:LANES] = loss_ref[...]
        s5 = slice(ROW_S5, ROW_S5 + N_GROUPS)
        p_ref[s5, LANE_A_RE:LANE_A_RE + STATE] = _select_dot(group_sum, d_are)
        p_ref[s5, LANE_A_IM:LANE_A_IM + STATE] = _select_dot(group_sum, d_aim)
        p_ref[s5, LANE_LOG_DT:LANE_LOG_DT + LANES] = _select_dot(group_sum, jnp.broadcast_to(d_ldt, (rows_gh, LANES)))

    return _pcall(body, name="ssm_disc_bwd_pack",
                  out_shape=(jax.ShapeDtypeStruct((PACK_ROWS, PACK_W), F32),
                             jax.ShapeDtypeStruct((rows_gh, 2 * STATE), BF16),
                             jax.ShapeDtypeStruct((rows_gh, 2 * STATE), BF16)),
                  scratch_shapes=[pltpu.VMEM((rows_gh, STATE), F32), pltpu.VMEM((rows_gh, STATE), F32)],
                  )(a_re_x, a_im_x, log_dt_x, b_re, b_im, g_ab_re, g_ab_im, dbb_re_d, dbb_im_d,
                    loss_t, dg8, dgf, dbg, dd, dcw, dc_re_d, dc_im_d)


def _in_proj(order, x2, g1, w_in_b):
    n = x2.shape[0]
    tm = min(IN_TILE, n)
    n_tiles = n // tm

    def body(order_ref, x_ref, g_ref, w_ref, xn_ref, proj_ref, wall_ref,
             xn_scr, wbuf, send_sems, recv_sems, loc_sems, out_sems):
        k = pl.program_id(0)
        i = pl.program_id(1)

        def slot(dev):
            return wbuf.at[dev // 2, :, pl.ds(pl.multiple_of((dev % 2) * COLS_PER_DEV, LANES), COLS_PER_DEV)]

        gather = _TwoLevelGather([w_ref], [slot], send_sems, recv_sems, loc_sems)

        @pl.when((k == 0) & (i == 0))
        def _():
            gather.start(chips=(0, 1))

        def own_chip():
            gather.wait_own()
            gather.wait_sibling()

        def other_chip(j):
            gather.wait_and_pass_on(j)
            if j == 0:
                gather.start_to((2,))
            gather.wait_passed_on(j)

        arrivals = [own_chip] + [functools.partial(other_chip, j) for j in range(3)]
        for kk, arrived in enumerate(arrivals):
            @pl.when((k == kk) & (i == 0))
            def _(arrived=arrived):
                arrived()

        rows = pl.ds(pl.multiple_of(i * tm, tm), tm)

        @pl.when(k == 0)
        def _():
            x = x_ref[...]
            r = lax.rsqrt(jnp.mean(x * x, axis=-1, keepdims=True) + EPS)
            xn = ((x * r) * g_ref[...]).astype(BF16)
            xn_scr[rows, :] = xn
            xn_ref[...] = xn

        proj_ref[...] = _dot(xn_scr[rows, :], wbuf[order_ref[k]])

        @pl.when((k == N_CHIP - 1) & (i == n_tiles - 1))
        def _():
            gather.wait_sends()
            outs = [pltpu.make_async_copy(wbuf.at[q], wall_ref.at[:, q * COLS_PER_CHIP:(q + 1) * COLS_PER_CHIP],
                                          out_sems.at[q]) for q in range(N_CHIP)]
            for cp in outs:
                cp.start()
            for cp in outs:
                cp.wait()

    tile_once = lambda k, i, order: (jnp.where(k == 0, i, n_tiles - 1), 0)
    grid_spec = pltpu.PrefetchScalarGridSpec(
        num_scalar_prefetch=1, grid=(N_CHIP, n_tiles),
        in_specs=[pl.BlockSpec((tm, D_MODEL), tile_once),
                  pl.BlockSpec((1, D_MODEL), lambda k, i, order: (0, 0)),
                  HBM_SPEC],
        out_specs=(pl.BlockSpec((tm, D_MODEL), tile_once),
                   pl.BlockSpec((tm, COLS_PER_CHIP), lambda k, i, order: (i, order[k])),
                   HBM_SPEC),
        scratch_shapes=[pltpu.VMEM((n, D_MODEL), BF16), pltpu.VMEM((N_CHIP, D_MODEL, COLS_PER_CHIP), BF16),
                        pltpu.SemaphoreType.DMA((7,)), pltpu.SemaphoreType.DMA((7,)), pltpu.SemaphoreType.DMA((1,)),
                        pltpu.SemaphoreType.DMA((N_CHIP,))])
    return _pcall(
        body, name="in_proj", grid_spec=grid_spec,
        out_shape=(jax.ShapeDtypeStruct((n, D_MODEL), BF16), jax.ShapeDtypeStruct((n, IN_COLS), F32),
                   jax.ShapeDtypeStruct((D_MODEL, IN_COLS), BF16)),
        compiler_params=_params(2),
    )(order, x2, g1, w_in_b)


def _cmul(p, q):
    return p[0] * q[0] - p[1] * q[1], p[0] * q[1] + p[1] * q[0]


def _scan_tables(ar, ai, width, reverse):
    pows = [(ar, ai)]
    for _ in range(SUBLANES - 1):
        pows.append(_cmul(pows[-1], (ar, ai)))
    row = lax.broadcasted_iota(jnp.int32, (SUBLANES, width), 0)

    def bc(v):
        return jnp.broadcast_to(v, (SUBLANES, width))

    levels = []
    for k in (1, 2, 4):
        keep = (row <= SUBLANES - 1 - k) if reverse else (row >= k)
        levels.append((jnp.where(keep, bc(pows[k - 1][0]), 0.0), jnp.where(keep, bc(pows[k - 1][1]), 0.0)))
    cre = jnp.zeros((SUBLANES, width), F32)
    cim = jnp.zeros((SUBLANES, width), F32)
    for r in range(SUBLANES):
        e = (SUBLANES - r) if reverse else (r + 1)
        cre = jnp.where(row == r, bc(pows[e - 1][0]), cre)
        cim = jnp.where(row == r, bc(pows[e - 1][1]), cim)
    return levels, (cre, cim)


def _load_chunked(src_ref, b, dst_ref, n_rows):
    n_blk = n_rows // SUBLANES
    for i in range(n_blk):
        dst_ref[b, i * SUBLANES:(i + 1) * SUBLANES, :] = src_ref[b, pl.ds(i, SUBLANES, stride=n_blk), :]


def _store_chunked(val, dst_ref, b, n_rows):
    n_blk = n_rows // SUBLANES
    for i in range(n_blk):
        dst_ref[b, pl.ds(i, SUBLANES, stride=n_blk), :] = val[i * SUBLANES:(i + 1) * SUBLANES, :]


def _chunk_scan(re_ref, im_ref, bs, car_ref, ar, ai, n_rows, reverse, on_block=None):
    width = re_ref.shape[2]
    n_blk = n_rows // SUBLANES
    shape = (SUBLANES, width)
    abr = jnp.broadcast_to(ar, shape)
    abi = jnp.broadcast_to(ai, shape)
    order = list(range(n_blk - 1, -1, -1)) if reverse else list(range(n_blk))

    def blk(ref, b, i):
        return ref[b, i * SUBLANES:(i + 1) * SUBLANES, :]

    def step(state, b, i):
        sr, si = state
        return abr * sr - abi * si + blk(re_ref, b, i), abr * si + abi * sr + blk(im_ref, b, i)

    finals = {b: (blk(re_ref, b, order[0]), blk(im_ref, b, order[0])) for b in bs}
    for i in order[1:]:
        for b in bs:
            finals[b] = step(finals[b], b, i)

    mr, mi = ar, ai
    for _ in range(n_blk.bit_length() - 1):
        mr, mi = _cmul((mr, mi), (mr, mi))
    levels, _ = _scan_tables(mr, mi, width, reverse)
    mbr = jnp.broadcast_to(mr, shape)
    mbi = jnp.broadcast_to(mi, shape)
    row = lax.broadcasted_iota(jnp.int32, shape, 0)
    edge_in = SUBLANES - 1 if reverse else 0
    edge_out = 0 if reverse else SUBLANES - 1
    sh1 = SUBLANES - 1 if reverse else 1
    states = {}
    for b in bs:
        fr, fi = finals[b]
        gr = jnp.where(row == edge_in, jnp.broadcast_to(car_ref[b, 0:1, :], shape), pltpu.roll(fr, sh1, 0))
        gi = jnp.where(row == edge_in, jnp.broadcast_to(car_ref[b, 1:2, :], shape), pltpu.roll(fi, sh1, 0))
        for (lr, li), k in zip(levels, (1, 2, 4)):
            sh = (SUBLANES - k) if reverse else k
            sr = pltpu.roll(gr, sh, 0)
            si = pltpu.roll(gi, sh, 0)
            gr, gi = gr + (lr * sr - li * si), gi + (lr * si + li * sr)
        car_ref[b, 0:1, :] = (fr + (mbr * gr - mbi * gi))[edge_out:edge_out + 1, :]
        car_ref[b, 1:2, :] = (fi + (mbr * gi + mbi * gr))[edge_out:edge_out + 1, :]
        states[b] = (gr, gi)

    for i in order:
        for b in bs:
            states[b] = step(states[b], b, i)
            re_ref[b, i * SUBLANES:(i + 1) * SUBLANES, :] = states[b][0]
            im_ref[b, i * SUBLANES:(i + 1) * SUBLANES, :] = states[b][1]
            if on_block is not None:
                on_block(b, i, *states[b])


def _ssm_fwd(u, bb_re, bb_im, c_re_t, c_imn_t, d_row, ab_re, ab_im, w_out_b, w_glu_b, conv_p, n_seq, seq):
    tt = min(SCAN_TILE, seq)
    nt = seq // tt

    def body(u_ref, bbre, bbim, cre, cimn, d_ref, are, aim, wout_ref, wglu_ref, cw_ref,
             sre_ref, sim_ref, y_ref, oout_ref, oglu_ref, ocw_ref,
             up_ref, car_ref, send_sems, recv_sems, loc_sems):
        j = pl.program_id(0)
        t = pl.program_id(1)
        gather = _TwoLevelGather(
            [wout_ref, wglu_ref, cw_ref],
            [lambda dev: oout_ref.at[pl.ds(pl.multiple_of(dev * OUT_ROWS_PER_DEV, OUT_ROWS_PER_DEV), OUT_ROWS_PER_DEV), :],
             lambda dev: oglu_ref.at[pl.ds(pl.multiple_of(dev * GLU_ROWS_PER_DEV, GLU_ROWS_PER_DEV), GLU_ROWS_PER_DEV), :],
             lambda dev: ocw_ref.at[dev]],
            send_sems, recv_sems, loc_sems)

        @pl.when((j == 0) & (t == 0))
        def _():
            gather.start()

        @pl.when((j == N_JBLK // 2) & (t == 0))
        def _():
            gather.forward()

        @pl.when(t == 0)
        def _():
            car_ref[...] = jnp.zeros_like(car_ref)

        bs = list(range(n_seq))
        for b in bs:
            _load_chunked(u_ref, b, up_ref, tt)
        for b in bs:
            ub = up_ref[b].astype(BF16)
            sre_ref[b] = _dot(ub, bbre[0])
            sim_ref[b] = _dot(ub, bbim[0])
            _chunk_scan(sre_ref, sim_ref, [b], car_ref, are[...], aim[...], tt, reverse=False)
        for b in bs:
            yp = (_dot(sre_ref[b].astype(BF16), cre[0]) + _dot(sim_ref[b].astype(BF16), cimn[0])
                  + d_ref[...] * up_ref[b])
            _store_chunked(yp, y_ref, b, tt)

        @pl.when((j == N_JBLK - 1) & (t == nt - 1))
        def _():
            gather.finish()

    tok = lambda j, t: (0, t, j)
    blk3 = lambda j, t: (j, 0, 0)
    row = lambda j, t: (0, j)
    st = jax.ShapeDtypeStruct((n_seq, seq, N_JBLK * JB_ST), F32)
    n_arr = 3
    return _pcall(
        body, name="ssm_fwd", grid=(N_JBLK, nt),
        out_shape=(st, st, jax.ShapeDtypeStruct((n_seq, seq, SSM_W), F32),
                   jax.ShapeDtypeStruct((D_MODEL, D_MODEL), BF16), jax.ShapeDtypeStruct((SSM_W, SSM_W), BF16),
                   jax.ShapeDtypeStruct((N_DEV, SUBLANES, LANES), F32)),
        in_specs=[pl.BlockSpec((n_seq, tt, JB_CH), tok),
                  pl.BlockSpec((1, JB_CH, JB_ST), blk3), pl.BlockSpec((1, JB_CH, JB_ST), blk3),
                  pl.BlockSpec((1, JB_ST, JB_CH), blk3), pl.BlockSpec((1, JB_ST, JB_CH), blk3),
                  pl.BlockSpec((1, JB_CH), row), pl.BlockSpec((1, JB_ST), row), pl.BlockSpec((1, JB_ST), row),
                  HBM_SPEC, HBM_SPEC, HBM_SPEC],
        out_specs=(pl.BlockSpec((n_seq, tt, JB_ST), tok), pl.BlockSpec((n_seq, tt, JB_ST), tok),
                   pl.BlockSpec((n_seq, tt, JB_CH), tok), HBM_SPEC, HBM_SPEC, HBM_SPEC),
        scratch_shapes=[pltpu.VMEM((n_seq, tt, JB_CH), F32), pltpu.VMEM((n_seq, SUBLANES, JB_ST), F32),
                        pltpu.SemaphoreType.DMA((7 * n_arr,)), pltpu.SemaphoreType.DMA((7 * n_arr,)),
                        pltpu.SemaphoreType.DMA((n_arr,))],
        compiler_params=_params(2),
    )(u, bb_re, bb_im, c_re_t, c_imn_t, d_row, ab_re, ab_im, w_out_b, w_glu_b, conv_p)


def _ssm_bwd(dy, u, s_re, s_im, bb_re, bb_im, c_re_t, c_imn_t, d_row, ab_re, ab_im, g_out, g_glu, n_seq, seq):
    tt = min(SCAN_TILE, seq)
    nt = seq // tt
    rows8 = tt // SUBLANES

    def body(dy_ref, u_ref, sre_ref, sim_ref, pre_ref, pim_ref, bbre, bbim, cre, cimn, d_ref, are, aim,
             gout_ref, gglu_ref,
             du_ref, dcre_ref, dcim_ref, dbbre_ref, dbbim_ref, dare_ref, daim_ref, dd_ref, rout_ref, rglu_ref,
             lre_ref, lim_ref, dyp_ref, up_ref, car_ref, send_sems, recv_sems, loc_sems):
        j = pl.program_id(0)
        tr = pl.program_id(1)

        def exchange():
            return _direct_copies(lambda pid: [gout_ref.at[pid], gglu_ref.at[pid]], [rout_ref, rglu_ref],
                                  send_sems, recv_sems, loc_sems)

        @pl.when((j == 0) & (tr == 0))
        def _():
            mine, sends = exchange()
            for cp in mine + sends:
                cp.start()

        @pl.when(tr == 0)
        def _():
            car_ref[...] = jnp.zeros_like(car_ref)
            for r in (dcre_ref, dcim_ref, dbbre_ref, dbbim_ref, dare_ref, daim_ref, dd_ref):
                r[...] = jnp.zeros_like(r)

        first = tr == nt - 1
        row = lax.broadcasted_iota(jnp.int32, (SUBLANES, JB_ST), 0)
        n_blk = tt // SUBLANES
        bs = list(range(n_seq))
        for b in bs:
            _load_chunked(dy_ref, b, dyp_ref, tt)
            _load_chunked(u_ref, b, up_ref, tt)
        for b in bs:
            dyb = dyp_ref[b].astype(BF16)
            lre_ref[b] = _dot_nt(dyb, cre[0])
            lim_ref[b] = _dot_nt(dyb, cimn[0])
        acc = {b: [jnp.zeros((SUBLANES, JB_ST), F32), jnp.zeros((SUBLANES, JB_ST), F32)] for b in bs}

        def on_block(b, i, lr, li):
            if i > 0:
                spr = sre_ref[b, (i - 1) * SUBLANES:i * SUBLANES, :]
                spi = sim_ref[b, (i - 1) * SUBLANES:i * SUBLANES, :]
            else:
                hr = jnp.where(first, 0.0, pre_ref[b, SUBLANES - 1:SUBLANES, :])
                hi = jnp.where(first, 0.0, pim_ref[b, SUBLANES - 1:SUBLANES, :])
                last_r = sre_ref[b, (n_blk - 1) * SUBLANES:n_blk * SUBLANES, :]
                last_i = sim_ref[b, (n_blk - 1) * SUBLANES:n_blk * SUBLANES, :]
                spr = jnp.where(row == 0, jnp.broadcast_to(hr, row.shape), pltpu.roll(last_r, 1, 0))
                spi = jnp.where(row == 0, jnp.broadcast_to(hi, row.shape), pltpu.roll(last_i, 1, 0))
            acc[b][0] = acc[b][0] + (lr * spr + li * spi)
            acc[b][1] = acc[b][1] + (li * spr - lr * spi)

        _chunk_scan(lre_ref, lim_ref, bs, car_ref, are[...], -aim[...], tt, reverse=True, on_block=on_block)
        for b in bs:
            dare_ref[...] += jnp.sum(acc[b][0], axis=0, keepdims=True)
            daim_ref[...] += jnp.sum(acc[b][1], axis=0, keepdims=True)
            dyp = dyp_ref[b]
            up = up_ref[b]
            dyb = dyp.astype(BF16)
            ub = up.astype(BF16)
            lrb = lre_ref[b].astype(BF16)
            lib = lim_ref[b].astype(BF16)
            dup = d_ref[...] * dyp + _dot_nt(lrb, bbre[0]) + _dot_nt(lib, bbim[0])
            _store_chunked(dup, du_ref, b, tt)
            dbbre_ref[0] += _dot_tn(ub, lrb)
            dbbim_ref[0] += _dot_tn(ub, lib)
            dcre_ref[0] += _dot_tn(dyb, sre_ref[b].astype(BF16))
            dcim_ref[0] += _dot_tn(dyb, sim_ref[b].astype(BF16))
            dd_ref[...] += jnp.sum(dyp * up, axis=0, keepdims=True)

        @pl.when((j == N_JBLK - 1) & (tr == nt - 1))
        def _():
            mine, sends = exchange()
            for cp in sends + mine:
                cp.wait()

    tok = lambda j, t: (0, nt - 1 - t, j)
    halo = lambda j, t: (0, jnp.maximum((nt - 1 - t) * rows8 - 1, 0), j)
    blk3 = lambda j, t: (j, 0, 0)
    row1 = lambda j, t: (0, j)
    acc_shape = jax.ShapeDtypeStruct((N_JBLK, JB_CH, JB_ST), F32)
    return _pcall(
        body, name="ssm_bwd", grid=(N_JBLK, nt),
        out_shape=(jax.ShapeDtypeStruct((n_seq, seq, SSM_W), F32), acc_shape, acc_shape, acc_shape, acc_shape,
                   jax.ShapeDtypeStruct((1, N_JBLK * JB_ST), F32), jax.ShapeDtypeStruct((1, N_JBLK * JB_ST), F32),
                   jax.ShapeDtypeStruct((1, SSM_W), F32),
                   jax.ShapeDtypeStruct((N_DEV,) + g_out.shape[1:], F32),
                   jax.ShapeDtypeStruct((N_DEV,) + g_glu.shape[1:], F32)),
        in_specs=[pl.BlockSpec((n_seq, tt, JB_CH), tok), pl.BlockSpec((n_seq, tt, JB_CH), tok),
                  pl.BlockSpec((n_seq, tt, JB_ST), tok), pl.BlockSpec((n_seq, tt, JB_ST), tok),
                  pl.BlockSpec((n_seq, SUBLANES, JB_ST), halo), pl.BlockSpec((n_seq, SUBLANES, JB_ST), halo),
                  pl.BlockSpec((1, JB_CH, JB_ST), blk3), pl.BlockSpec((1, JB_CH, JB_ST), blk3),
                  pl.BlockSpec((1, JB_ST, JB_CH), blk3), pl.BlockSpec((1, JB_ST, JB_CH), blk3),
                  pl.BlockSpec((1, JB_CH), row1), pl.BlockSpec((1, JB_ST), row1), pl.BlockSpec((1, JB_ST), row1),
                  HBM_SPEC, HBM_SPEC],
        out_specs=(pl.BlockSpec((n_seq, tt, JB_CH), tok),
                   pl.BlockSpec((1, JB_CH, JB_ST), blk3), pl.BlockSpec((1, JB_CH, JB_ST), blk3),
                   pl.BlockSpec((1, JB_CH, JB_ST), blk3), pl.BlockSpec((1, JB_CH, JB_ST), blk3),
                   pl.BlockSpec((1, JB_ST), row1), pl.BlockSpec((1, JB_ST), row1), pl.BlockSpec((1, JB_CH), row1),
                   HBM_SPEC, HBM_SPEC),
        scratch_shapes=[pltpu.VMEM((n_seq, tt, JB_ST), F32), pltpu.VMEM((n_seq, tt, JB_ST), F32),
                        pltpu.VMEM((n_seq, tt, JB_CH), F32), pltpu.VMEM((n_seq, tt, JB_CH), F32),
                        pltpu.VMEM((n_seq, SUBLANES, JB_ST), F32),
                        pltpu.SemaphoreType.DMA((7 * 2,)), pltpu.SemaphoreType.DMA((7 * 2,)),
                        pltpu.SemaphoreType.DMA((2,))],
        compiler_params=_params(2),
    )(dy, u, s_re, s_im, s_re, s_im, bb_re, bb_im, c_re_t, c_imn_t, d_row, ab_re, ab_im, g_out, g_glu)


def _mix(x2, tgt2, y, proj, gf, b_glu, conv8, w_glu_f, w_out_f, seq):
    n = x2.shape[0]
    tm = TOK_TILE
    tiles_per_seq = seq // tm
    rows8 = tm // SUBLANES

    def body(x_ref, t_ref, y_ref, zs_ref, h_ref, bc_ref, cc_ref, zc_ref, hp_ref, ccp_ref,
             gf_ref, bg_ref, cw_ref, wg_ref, wo_ref,
             dh2_ref, dy_ref, dzs_ref, dbc_ref, dzc_ref, dyc_ref,
             dwo_ref, dwg_ref, loss_ref, dgf_ref, dbg_ref, dcw_ref):
        i = pl.program_id(0)

        @pl.when(i == 0)
        def _():
            for r in (dwo_ref, dwg_ref, loss_ref, dgf_ref, dbg_ref, dcw_ref):
                r[...] = jnp.zeros_like(r)

        yv = y_ref[...]
        y1, dgelu = _gelu_and_grad(yv)
        y1b = y1.astype(BF16)
        gate = _sigmoid(_dot(y1b, wg_ref[...]) + bg_ref[...])
        y2 = y1 * gate
        szs, dszs = _silu_and_grad(zs_ref[...])
        yssm = y2 * szs
        hv = h_ref[...]
        ccv = cc_ref[...]
        bcv = bc_ref[...]
        v = ccv * hv
        first = (i % tiles_per_seq) == 0
        vhalo = jnp.where(first, 0.0, ccp_ref[...] * hp_ref[...])
        v1 = _shift_down(v, vhalo, 1)
        v2 = _shift_down(v, vhalo, 2)
        w0 = cw_ref[0:1, :]
        w1 = cw_ref[1:2, :]
        w2 = cw_ref[2:3, :]
        yc = w0 * v2 + w1 * v1 + w2 * v
        szc, dszc = _silu_and_grad(zc_ref[...])
        yconv = (bcv * yc) * szc
        ysb = yssm.astype(BF16)
        ycb = yconv.astype(BF16)
        h2 = x_ref[...] + _dot(ysb, wo_ref[0:SSM_W, :]) + _dot(ycb, wo_ref[SSM_W:, :])
        r2 = lax.rsqrt(jnp.mean(h2 * h2, axis=-1, keepdims=True) + EPS)
        hn = h2 * r2
        gfv = gf_ref[...]
        err = hn * gfv - t_ref[...]
        loss_ref[...] += 0.5 * jnp.sum(jnp.mean(err * err, axis=-1, keepdims=True))
        dout = err * (1.0 / D_MODEL)
        dgf_ref[...] += jnp.sum(dout * hn, axis=0, keepdims=True)
        dn = dout * gfv
        dh2 = r2 * (dn - hn * jnp.mean(dn * hn, axis=-1, keepdims=True))
        dh2_ref[...] = dh2
        dh2b = dh2.astype(BF16)
        dwo_ref[0:SSM_W, :] += _dot_tn(ysb, dh2b)
        dwo_ref[SSM_W:, :] += _dot_tn(ycb, dh2b)
        dyssm = _dot_nt(dh2b, wo_ref[0:SSM_W, :])
        dyconv = _dot_nt(dh2b, wo_ref[SSM_W:, :])
        dy2 = dyssm * szs
        dzs_ref[...] = (dyssm * y2 * dszs).astype(BF16)
        dgp = dy2 * y1 * (gate * (1.0 - gate))
        dgpb = dgp.astype(BF16)
        dy1 = dy2 * gate + _dot_nt(dgpb, wg_ref[...])
        dwg_ref[...] += _dot_tn(y1b, dgpb)
        dbg_ref[...] += jnp.sum(dgp, axis=0, keepdims=True)
        dy_ref[...] = dy1 * dgelu
        dbc_ref[...] = (dyconv * yc * szc).astype(BF16)
        dyc = dyconv * bcv * szc
        dyc_ref[...] = dyc
        dzc_ref[...] = (dyconv * bcv * yc * dszc).astype(BF16)
        dcw_ref[0:1, :] += jnp.sum(dyc * v2, axis=0, keepdims=True)
        dcw_ref[1:2, :] += jnp.sum(dyc * v1, axis=0, keepdims=True)
        dcw_ref[2:3, :] += jnp.sum(dyc * v, axis=0, keepdims=True)

    tile_d = pl.BlockSpec((tm, D_MODEL), lambda i: (i, 0))
    tile_s = pl.BlockSpec((tm, SSM_W), lambda i: (i, 0))
    seg_of = lambda c: pl.BlockSpec((tm, SSM_W), lambda i: (i, c))
    halo_of = lambda c: pl.BlockSpec((SUBLANES, SSM_W), lambda i: (jnp.maximum(i * rows8 - 1, 0), c))
    const = lambda shape: pl.BlockSpec(shape, lambda i: (0,) * len(shape))
    seg = jax.ShapeDtypeStruct((n, SSM_W), F32)
    seg_b = jax.ShapeDtypeStruct((n, SSM_W), BF16)
    return _pcall(
        body, name="mix", grid=(n // tm,),
        out_shape=(jax.ShapeDtypeStruct((n, D_MODEL), F32), seg, seg_b, seg_b, seg_b, seg,
                   jax.ShapeDtypeStruct((D_MODEL, D_MODEL), F32), jax.ShapeDtypeStruct((SSM_W, SSM_W), F32),
                   jax.ShapeDtypeStruct((SUBLANES, LANES), F32), jax.ShapeDtypeStruct((1, D_MODEL), F32),
                   jax.ShapeDtypeStruct((1, SSM_W), F32), jax.ShapeDtypeStruct((SUBLANES, CONV_W), F32)),
        in_specs=[tile_d, tile_d, tile_s, seg_of(SEG_ZS), seg_of(SEG_H), seg_of(SEG_BC), seg_of(SEG_CC), seg_of(SEG_ZC),
                  halo_of(SEG_H), halo_of(SEG_CC),
                  const((1, D_MODEL)), const((1, SSM_W)), const((SUBLANES, CONV_W)),
                  const((SSM_W, SSM_W)), const((D_MODEL, D_MODEL))],
        out_specs=(tile_d, tile_s, tile_s, tile_s, tile_s, tile_s,
                   const((D_MODEL, D_MODEL)), const((SSM_W, SSM_W)), const((SUBLANES, LANES)),
                   const((1, D_MODEL)), const((1, SSM_W)), const((SUBLANES, CONV_W))),
        compiler_params=_params(1),
    )(x2, tgt2, y, proj, proj, proj, proj, proj, proj, proj, gf, b_glu, conv8, w_glu_f, w_out_f)


def _in_bwd(x2, dh2, du, dzs, dyc, proj, dbc, dzc, g1, conv8, w_full, seq):
    n = x2.shape[0]
    tm = TOK_TILE
    n_tiles = n // tm
    tiles_per_seq = seq // tm
    rows8 = tm // SUBLANES
    n_blk8 = n // SUBLANES

    def body(x_ref, dh2_ref, du_ref, dzs_ref, dyc_ref, dycn_ref, h_ref, cc_ref, dbc_ref, dzc_ref,
             g_ref, cw_ref, w_ref, gx_ref, dp_ref, dg_ref):
        i = pl.program_id(0)

        @pl.when(i == 0)
        def _():
            dg_ref[...] = jnp.zeros_like(dg_ref)

        dyc = dyc_ref[...]
        last = (i % tiles_per_seq) == tiles_per_seq - 1
        nhalo = jnp.where(last, 0.0, dycn_ref[...])
        dv = (cw_ref[2:3, :] * dyc + cw_ref[1:2, :] * _shift_up(dyc, nhalo, 1)
              + cw_ref[0:1, :] * _shift_up(dyc, nhalo, 2))
        parts = (du_ref[...], dzs_ref[...], dv * cc_ref[...], dbc_ref[...], dv * h_ref[...], dzc_ref[...])
        dxn = jnp.zeros((tm, D_MODEL), F32)
        for k, p in enumerate(parts):
            pb = p.astype(BF16)
            dp_ref[:, k * SSM_W:(k + 1) * SSM_W] = pb
            dxn = dxn + _dot_nt(pb, w_ref[:, k * SSM_W:(k + 1) * SSM_W])
        x = x_ref[...]
        r = lax.rsqrt(jnp.mean(x * x, axis=-1, keepdims=True) + EPS)
        xh = x * r
        dg_ref[...] += jnp.sum(dxn * xh, axis=0, keepdims=True)
        dn = dxn * g_ref[...]
        gx_ref[...] = dh2_ref[...] + r * (dn - xh * jnp.mean(dn * xh, axis=-1, keepdims=True))

    tile_d = pl.BlockSpec((tm, D_MODEL), lambda i: (i, 0))
    tile_s = pl.BlockSpec((tm, SSM_W), lambda i: (i, 0))
    seg_of = lambda c: pl.BlockSpec((tm, SSM_W), lambda i: (i, c))
    nhalo = pl.BlockSpec((SUBLANES, SSM_W), lambda i: (jnp.minimum((i + 1) * rows8, n_blk8 - 1), 0))
    const = lambda shape: pl.BlockSpec(shape, lambda i: (0,) * len(shape))
    return _pcall(
        body, name="in_bwd", grid=(n_tiles,),
        out_shape=(jax.ShapeDtypeStruct((n, D_MODEL), F32), jax.ShapeDtypeStruct((n, IN_COLS), BF16),
                   jax.ShapeDtypeStruct((SUBLANES, D_MODEL), F32)),
        in_specs=[tile_d, tile_d, tile_s, tile_s, tile_s, nhalo, seg_of(SEG_H), seg_of(SEG_CC), tile_s, tile_s,
                  const((1, D_MODEL)), const((SUBLANES, CONV_W)), const((D_MODEL, IN_COLS))],
        out_specs=(tile_d, pl.BlockSpec((tm, IN_COLS), lambda i: (i, 0)), const((SUBLANES, D_MODEL))),
        compiler_params=_params(1),
    )(x2, dh2, du, dzs, dyc, dyc, proj, proj, dbc, dzc, g1, conv8, w_full)


def _dw_in_exchange(order, xn, dproj, smalls):
    n = xn.shape[0]
    tk = 512
    nk = n // tk
    piece = (D_MODEL, COLS_PER_DEV)
    n_small = len(smalls)

    def body(order_ref, xn_hbm, dp_ref, *refs):
        del order_ref
        sm_refs = refs[:n_small]
        own_ref, rchip_ref = refs[n_small:n_small + 2]
        rsm_refs = refs[n_small + 2:2 * n_small + 2]
        (xn_ref, acc, stage, sbuf, xn_sems, give_send, give_recv, keep_send, keep_recv,
         sm_send, sm_recv, sm_loc) = refs[2 * n_small + 2:]
        s = pl.program_id(0)

        def xn_copy(kk):
            rows = pl.ds(pl.multiple_of(kk * tk, tk), tk)
            return pltpu.make_async_copy(xn_hbm.at[rows, :], xn_ref.at[rows, :], xn_sems.at[kk])

        @pl.when(s == 0)
        def _():
            for kk in range(nk):
                xn_copy(kk).start()
            xn_copy(0).wait()

        x, y, c = _mesh_pos()
        sib = (x, y, 1 - c)
        chips = [(1 - x, 1 - y), (1 - x, y), (x, 1 - y)]
        gather = _TwoLevelGather(list(sm_refs), [functools.partial(lambda r, dev: r.at[dev], r) for r in rsm_refs],
                                 sm_send, sm_recv, sm_loc)

        def half(i, core):
            return acc.at[i % 2, :, pl.ds(pl.multiple_of(core * COLS_PER_DEV, LANES), COLS_PER_DEV)]

        def give(i):
            return pltpu.make_async_remote_copy(src_ref=half(i, 1 - c), dst_ref=stage.at[i], send_sem=give_send.at[i],
                                                recv_sem=give_recv.at[i], device_id=sib, device_id_type=MESH)

        def keep(i):
            return pltpu.make_async_remote_copy(src_ref=sbuf.at[i], dst_ref=rchip_ref.at[i], send_sem=keep_send.at[i],
                                                recv_sem=keep_recv.at[i], device_id=(*chips[i], c), device_id_type=MESH)

        def chip_sum(i):
            give(i).wait_recv()
            mine = [acc[i % 2, :, cc * COLS_PER_DEV:(cc + 1) * COLS_PER_DEV] for cc in range(2)]
            return jnp.where(c == 0, mine[0], mine[1]) + stage[i]

        @pl.when(s == 0)
        def _():
            gather.start()

        @pl.when(s == N_CHIP // 2)
        def _():
            gather.forward()

        for k in range(2, N_CHIP):
            @pl.when(s == k)
            def _(k=k):
                give(k - 2).wait_send()

        slot = s % 2
        acc[slot] = _dot_tn(xn_ref[pl.ds(0, tk), :], dp_ref[pl.ds(0, tk), :])

        def kstep(kk, carry):
            @pl.when(s == 0)
            def _():
                xn_copy(kk).wait()

            off = pl.multiple_of(kk * tk, tk)
            acc[slot] += _dot_tn(xn_ref[pl.ds(off, tk), :], dp_ref[pl.ds(off, tk), :])
            return carry

        n_first = min(nk, 3)
        lax.fori_loop(1, n_first, kstep, 0)
        for k in range(1, N_CHIP):
            @pl.when(s == k)
            def _(k=k):
                sbuf[k - 1] = chip_sum(k - 1).astype(BF16)
                keep(k - 1).start()

        lax.fori_loop(n_first, nk, kstep, 0)

        for k in range(N_CHIP):
            @pl.when(s == k)
            def _(k=k):
                give(k).start()

        @pl.when(s == N_CHIP - 1)
        def _():
            own_ref[...] = chip_sum(N_CHIP - 1)
            give(N_CHIP - 2).wait_send()
            give(N_CHIP - 1).wait_send()
            for i in range(3):
                keep(i).wait()
            gather.finish()

    grid_spec = pltpu.PrefetchScalarGridSpec(
        num_scalar_prefetch=1, grid=(N_CHIP,),
        in_specs=[HBM_SPEC,
                  pl.BlockSpec((n, COLS_PER_CHIP), lambda s, order: (0, order[s])),
                  *([HBM_SPEC] * n_small)],
        out_specs=(pl.BlockSpec(piece, lambda s, order: (0, 0)), HBM_SPEC, *([HBM_SPEC] * n_small)),
        scratch_shapes=[pltpu.VMEM((n, D_MODEL), BF16),
                        pltpu.VMEM((2, D_MODEL, COLS_PER_CHIP), F32), pltpu.VMEM((4,) + piece, F32),
                        pltpu.VMEM((3,) + piece, BF16),
                        pltpu.SemaphoreType.DMA((nk,)),
                        pltpu.SemaphoreType.DMA((4,)), pltpu.SemaphoreType.DMA((4,)),
                        pltpu.SemaphoreType.DMA((3,)), pltpu.SemaphoreType.DMA((3,)),
                        pltpu.SemaphoreType.DMA((7 * n_small,)), pltpu.SemaphoreType.DMA((7 * n_small,)),
                        pltpu.SemaphoreType.DMA((n_small,))])
    return _pcall(
        body, name="dw_in_exchange", grid_spec=grid_spec,
        out_shape=(jax.ShapeDtypeStruct(piece, F32), jax.ShapeDtypeStruct((3,) + piece, BF16),
                   *(jax.ShapeDtypeStruct((N_DEV,) + a.shape, a.dtype) for a in smalls)),
        compiler_params=_params(1),
    )(order, xn, dproj, *smalls)


def _adamw(g, w, m, v):
    m_new = ADAM_B1 * m + (1.0 - ADAM_B1) * g
    v_new = ADAM_B2 * v + (1.0 - ADAM_B2) * (g * g)
    m_hat = m_new / (1.0 - ADAM_B1 ** ADAM_STEP)
    v_hat = v_new / (1.0 - ADAM_B2 ** ADAM_STEP)
    delta = -ADAM_LR * (m_hat / (jnp.sqrt(v_hat) + ADAM_EPS) + ADAM_WD * w)
    return delta, m_new, v_new


def _reduce_adam(recv, w, m, v, name, row_tile):
    rows, cols = w.shape

    def body(r_ref, w_ref, m_ref, v_ref, g_ref, d_ref, nm_ref, nv_ref):
        g = r_ref[0]
        for s in range(1, N_DEV):
            g = g + r_ref[s]
        g_ref[...] = g
        d_ref[...], nm_ref[...], nv_ref[...] = _adamw(g, w_ref[...], m_ref[...], v_ref[...])

    tile = pl.BlockSpec((row_tile, cols), lambda i: (i, 0))
    shp = jax.ShapeDtypeStruct((rows, cols), F32)
    return _pcall(
        body, name=name, grid=(rows // row_tile,),
        out_shape=(shp,) * 4,
        in_specs=[pl.BlockSpec((N_DEV, row_tile, cols), lambda i: (0, i, 0)), tile, tile, tile],
        out_specs=(tile,) * 4,
        compiler_params=_params(1),
    )(recv, w, m, v)


def _reduce_adam_w_in(own, rchip, w, m, v):
    rows, cols = w.shape
    row_tile = 256

    def body(o_ref, r_ref, w_ref, m_ref, v_ref, g_ref, d_ref, nm_ref, nv_ref):
        g = o_ref[...]
        for s in range(3):
            g = g + r_ref[s].astype(F32)
        g_ref[...] = g
        d_ref[...], nm_ref[...], nv_ref[...] = _adamw(g, w_ref[...], m_ref[...], v_ref[...])

    tile = pl.BlockSpec((row_tile, cols), lambda i: (i, 0))
    shp = jax.ShapeDtypeStruct((rows, cols), F32)
    return _pcall(
        body, name="reduce_adam_w_in", grid=(rows // row_tile,),
        out_shape=(shp,) * 4,
        in_specs=[tile, pl.BlockSpec((3, row_tile, cols), lambda i: (0, i, 0)), tile, tile, tile],
        out_specs=(tile,) * 4,
        compiler_params=_params(1),
    )(own, rchip, w, m, v)


_SMALL_LEAVES = ("norm_gain", "final_norm_gain", "b_glu", "ssm_a_re", "ssm_a_im", "ssm_log_dt", "ssm_d", "conv_w",
                 "ssm_c_re", "ssm_c_im", "ssm_b_re", "ssm_b_im")


def _reduce_adam_small(r_pack, r_gc, r_gb, wmv):
    n_leaf = len(_SMALL_LEAVES)

    def body(*refs):
        rp_ref, rgc_ref, rgb_ref = refs[:3]
        w_refs = refs[3:3 + 3 * n_leaf]
        loss_ref = refs[3 + 3 * n_leaf]
        o_refs = refs[4 + 3 * n_leaf:4 + 7 * n_leaf]
        own_conv = refs[-1]

        def total(ref):
            acc = ref[0].astype(F32)
            for s in range(1, N_DEV):
                acc = acc + ref[s].astype(F32)
            return acc

        sp = total(rp_ref)
        sgc = total(rgc_ref)
        sgb = total(rgb_ref)
        loss_ref[...] = sp[ROW_LOSS:ROW_LOSS + SUBLANES, 0:LANES]

        def wide(r):
            return jnp.concatenate([sp[r:r + 1, :], sp[r + 1:r + 2, :]], axis=1)

        s5 = slice(ROW_S5, ROW_S5 + N_GROUPS)
        eye = (lax.broadcasted_iota(jnp.int32, (N_GROUPS, N_GROUPS), 0)
               == lax.broadcasted_iota(jnp.int32, (N_GROUPS, N_GROUPS), 1)).astype(F32)
        d_row = sp[ROW_BGLU_D + 1:ROW_BGLU_D + 2, :]
        me = 4 * lax.axis_index("x") + 2 * lax.axis_index("y") + lax.axis_index("c")
        for k in range(N_DEV):
            @pl.when(me == k)
            def _(k=k):
                own_conv[...] = sp[ROW_CONV:ROW_CONV + SUBLANES, k * CONV_COLS_PER_DEV:(k + 1) * CONV_COLS_PER_DEV]
        grads = {
            "norm_gain": wide(ROW_NORM_GAIN),
            "final_norm_gain": wide(ROW_FINAL_GAIN),
            "b_glu": sp[ROW_BGLU_D:ROW_BGLU_D + 1, :],
            "ssm_a_re": sp[s5, LANE_A_RE:LANE_A_RE + STATE],
            "ssm_a_im": sp[s5, LANE_A_IM:LANE_A_IM + STATE],
            "ssm_log_dt": jnp.sum(sp[s5, LANE_LOG_DT:LANE_LOG_DT + 1] * eye, axis=0, keepdims=True),
            "ssm_d": jnp.concatenate([d_row[:, g * GROUP:(g + 1) * GROUP] for g in range(N_GROUPS)], axis=0),
            "conv_w": own_conv[0:3, :],
            "ssm_c_re": sgc[:, 0:STATE],
            "ssm_c_im": sgc[:, STATE:2 * STATE],
            "ssm_b_re": sgb[:, 0:STATE],
            "ssm_b_im": sgb[:, STATE:2 * STATE],
        }
        for i, name in enumerate(_SMALL_LEAVES):
            g = grads[name]
            w_ref, m_ref, v_ref = w_refs[3 * i:3 * i + 3]
            o_g, o_d, o_m, o_v = o_refs[4 * i:4 * i + 4]
            o_g[...] = g
            o_d[...], o_m[...], o_v[...] = _adamw(g, w_ref[...], m_ref[...], v_ref[...])

    flat_w = [a for name in _SMALL_LEAVES for a in wmv[name]]
    leaf_shapes = [jax.ShapeDtypeStruct(wmv[name][0].shape, F32) for name in _SMALL_LEAVES for _ in range(4)]
    outs = _pcall(
        body, name="reduce_adam_small",
        out_shape=(jax.ShapeDtypeStruct((SUBLANES, LANES), F32), *leaf_shapes),
        scratch_shapes=[pltpu.VMEM((SUBLANES, CONV_COLS_PER_DEV), F32)],
        compiler_params=_params(0),
    )(r_pack, r_gc, r_gb, *flat_w)
    leaves = {name: outs[1 + 4 * i:5 + 4 * i] for i, name in enumerate(_SMALL_LEAVES)}
    return outs[0], leaves


def _block_diag(m4):
    eye = jnp.eye(SUBLANES, dtype=m4.dtype)
    j, g, a, b = m4.shape
    return jnp.einsum("jgab,gk->jgakb", m4, eye).reshape(j, g * a, g * b)


def kernel(x, norm_gain, w_in, ssm_a_re, ssm_a_im, ssm_log_dt, ssm_b_re, ssm_b_im, ssm_c_re, ssm_c_im, ssm_d, w_glu, b_glu, conv_w, w_out, final_norm_gain, loss_target, m_norm_gain, m_w_in, m_ssm_a_re, m_ssm_a_im, m_ssm_log_dt, m_ssm_b_re, m_ssm_b_im, m_ssm_c_re, m_ssm_c_im, m_ssm_d, m_w_glu, m_b_glu, m_conv_w, m_w_out, m_final_norm_gain, v_norm_gain, v_w_in, v_ssm_a_re, v_ssm_a_im, v_ssm_log_dt, v_ssm_b_re, v_ssm_b_im, v_ssm_c_re, v_ssm_c_im, v_ssm_d, v_w_glu, v_b_glu, v_conv_w, v_w_out, v_final_norm_gain):
    n_seq, seq, _ = x.shape
    n = n_seq * seq

    gh_p = lambda b4: jnp.transpose(b4, (0, 1, 3, 2)).reshape(N_GROUPS * GROUP, STATE)
    rep = lambda a: jnp.repeat(a, GROUP, axis=0)
    a_re_x, a_im_x = rep(ssm_a_re[0]), rep(ssm_a_im[0])
    log_dt_x = rep(ssm_log_dt[0].reshape(N_GROUPS, 1))
    b_re2, b_im2 = gh_p(ssm_b_re), gh_p(ssm_b_im)
    ab_re_x, ab_im_x, bb_re2, bb_im2 = _ssm_disc(a_re_x, a_im_x, log_dt_x, b_re2, b_im2)
    ab_re = ab_re_x[::GROUP].reshape(1, N_GROUPS * STATE)
    ab_im = ab_im_x[::GROUP].reshape(1, N_GROUPS * STATE)

    def bb_mat(bb2):
        return _block_diag(bb2.reshape(N_JBLK, SUBLANES, GROUP, STATE)).astype(BF16)

    def c_mat(c3, sign):
        t = jnp.transpose(c3.reshape(N_JBLK, SUBLANES, GROUP, STATE), (0, 1, 3, 2))
        return _block_diag(sign * t).astype(BF16)

    bb_re_m, bb_im_m = bb_mat(bb_re2), bb_mat(bb_im2)
    c_re_m, c_imn_m = c_mat(ssm_c_re[0], 1.0), c_mat(ssm_c_im[0], -1.0)
    d_row = ssm_d[0].reshape(1, SSM_W)

    x2 = x.reshape(n, D_MODEL)
    tgt2 = loss_target.reshape(n, D_MODEL)
    mx, my, mc = lax.axis_index("x"), lax.axis_index("y"), lax.axis_index("c")
    chip_ids = [2 * cx + cy for cx, cy in ((mx, my), (1 - mx, my), (mx, 1 - my), (1 - mx, 1 - my))]
    arrival = chip_ids
    xn, proj, w_in_f = _in_proj(jnp.stack(arrival).astype(jnp.int32), x2, norm_gain, w_in[0].astype(BF16))
    u3 = proj.reshape(n_seq, seq, IN_COLS)
    conv_p = jnp.pad(conv_w[0], ((0, SUBLANES - 3), (0, LANES - CONV_COLS_PER_DEV)))
    s_re, s_im, y3, w_out_f, w_glu_f, conv_all = _ssm_fwd(
        u3, bb_re_m, bb_im_m, c_re_m, c_imn_m, d_row, ab_re, ab_im,
        w_out[0].astype(BF16), w_glu[0].astype(BF16), conv_p, n_seq, seq)
    conv8 = jnp.transpose(conv_all[:, :, :CONV_COLS_PER_DEV], (1, 0, 2)).reshape(SUBLANES, CONV_W)
    (dh2, dy, dzs, dbc, dzc, dyc, dw_out, dw_glu, loss_t, dgf, dbg, dcw) = _mix(
        x2, tgt2, y3.reshape(n, SSM_W), proj, final_norm_gain.reshape(1, D_MODEL), b_glu, conv8,
        w_glu_f, w_out_f, seq)

    du3, dc_re_d, dc_im_d, dbb_re_d, dbb_im_d, dab_re, dab_im, dd, r_out, r_glu = _ssm_bwd(
        dy.reshape(n_seq, seq, SSM_W), u3, s_re, s_im, bb_re_m, bb_im_m, c_re_m, c_imn_m, d_row, ab_re, ab_im,
        dw_out.reshape(N_DEV, OUT_ROWS_PER_DEV, D_MODEL), dw_glu.reshape(N_DEV, GLU_ROWS_PER_DEV, SSM_W), n_seq, seq)
    du = du3.reshape(n, SSM_W)
    grad_x2, dproj, dg8 = _in_bwd(x2, dh2, du, dzs, dyc, proj, dbc, dzc, norm_gain, conv8, w_in_f, seq)
    pack, gc, gb = _ssm_disc_bwd_pack(
        a_re_x, a_im_x, log_dt_x, b_re2, b_im2, dab_re.reshape(N_GROUPS, STATE), dab_im.reshape(N_GROUPS, STATE),
        dbb_re_d, dbb_im_d, loss_t, dg8, dgf, dbg, dd, dcw, dc_re_d, dc_im_d)

    order = [chip_ids[3], chip_ids[1], chip_ids[2], chip_ids[0]]
    own_in, rchip_in, r_pack, r_gc, r_gb = _dw_in_exchange(
        jnp.stack(order).astype(jnp.int32), xn, dproj, [pack, gc, gb])

    flat2 = lambda a: a.reshape(a.shape[-2:]) if a.ndim > 2 else a.reshape(1, -1)
    c2 = lambda a: a.reshape(N_GROUPS * GROUP, STATE)
    wmv = dict(norm_gain=(norm_gain, m_norm_gain, v_norm_gain),
               final_norm_gain=tuple(flat2(a) for a in (final_norm_gain, m_final_norm_gain, v_final_norm_gain)),
               b_glu=(b_glu, m_b_glu, v_b_glu),
               ssm_a_re=tuple(flat2(a) for a in (ssm_a_re, m_ssm_a_re, v_ssm_a_re)),
               ssm_a_im=tuple(flat2(a) for a in (ssm_a_im, m_ssm_a_im, v_ssm_a_im)),
               ssm_log_dt=(ssm_log_dt, m_ssm_log_dt, v_ssm_log_dt),
               ssm_d=tuple(flat2(a) for a in (ssm_d, m_ssm_d, v_ssm_d)),
               conv_w=tuple(flat2(a) for a in (conv_w, m_conv_w, v_conv_w)),
               ssm_c_re=tuple(c2(a) for a in (ssm_c_re, m_ssm_c_re, v_ssm_c_re)),
               ssm_c_im=tuple(c2(a) for a in (ssm_c_im, m_ssm_c_im, v_ssm_c_im)),
               ssm_b_re=(b_re2, gh_p(m_ssm_b_re), gh_p(v_ssm_b_re)),
               ssm_b_im=(b_im2, gh_p(m_ssm_b_im), gh_p(v_ssm_b_im)))

    res_in = _reduce_adam_w_in(own_in, rchip_in, w_in[0], m_w_in[0], v_w_in[0])
    res_out = _reduce_adam(r_out, w_out[0], m_w_out[0], v_w_out[0], "reduce_adam_w_out", OUT_ROWS_PER_DEV)
    res_glu = _reduce_adam(r_glu, w_glu[0], m_w_glu[0], v_w_glu[0], "reduce_adam_w_glu", GLU_ROWS_PER_DEV)
    loss8, small = _reduce_adam_small(r_pack, r_gc, r_gb, wmv)
    loss = loss8[0, 0]

    shapes = dict(norm_gain=(1, D_MODEL), ssm_a_re=(1, N_GROUPS, STATE), ssm_a_im=(1, N_GROUPS, STATE),
                  ssm_log_dt=(1, N_GROUPS), ssm_c_re=(1, N_GROUPS, GROUP, STATE), ssm_c_im=(1, N_GROUPS, GROUP, STATE),
                  ssm_d=(1, N_GROUPS, GROUP), b_glu=(1, SSM_W), final_norm_gain=(D_MODEL,),
                  conv_w=(1, 3, CONV_COLS_PER_DEV))
    big = dict(w_in=res_in, w_glu=res_glu, w_out=res_out)

    def leaf(kind, name):
        if name in big:
            return big[name][kind][None]
        if name in ("ssm_b_re", "ssm_b_im"):
            return jnp.transpose(small[name][kind].reshape(1, N_GROUPS, GROUP, STATE), (0, 1, 3, 2))
        return small[name][kind].reshape(shapes[name])

    order = ["norm_gain", "w_in", "ssm_a_re", "ssm_a_im", "ssm_log_dt", "ssm_b_re", "ssm_b_im", "ssm_c_re",
             "ssm_c_im", "ssm_d", "w_glu", "b_glu", "conv_w", "w_out", "final_norm_gain"]
    outs = [loss, grad_x2.reshape(x.shape)]
    for kind in range(4):
        outs += [leaf(kind, name) for name in order]
    return tuple(outs)
```

```python
import functools
import math

import jax
import jax.numpy as jnp
from jax import lax
from jax.experimental import pallas as pl
from jax.experimental.pallas import tpu as pltpu

F32 = jnp.float32
BF16 = jnp.bfloat16

N_DEV = 8
D_MODEL = 1024
SSM_W = 512
CONV_W = 512
N_GROUPS = 32
GROUP = 16
STATE = 64
IN_COLS = 3072
SEG_U, SEG_ZS, SEG_H, SEG_BC, SEG_CC, SEG_ZC = range(6)
COLS_PER_DEV = IN_COLS // N_DEV
N_CHIP = N_DEV // 2
COLS_PER_CHIP = 2 * COLS_PER_DEV
OUT_ROWS_PER_DEV = D_MODEL // N_DEV
GLU_ROWS_PER_DEV = SSM_W // N_DEV
CONV_COLS_PER_DEV = CONV_W // N_DEV
EPS = 1e-6

N_JBLK = 4
JB_CH = SSM_W // N_JBLK
JB_ST = N_GROUPS * STATE // N_JBLK

ADAM_LR = 0.001
ADAM_B1 = 0.9
ADAM_B2 = 0.999
ADAM_EPS = 1e-08
ADAM_WD = 0.01
ADAM_STEP = 10

SUBLANES = 8
LANES = 128
VMEM_LIMIT = 48 * 1024 * 1024
TOK_TILE = 256
IN_TILE = 1024
SCAN_TILE = 1024

MESH = pl.DeviceIdType.MESH
HBM_SPEC = pl.BlockSpec(memory_space=pltpu.HBM)


def _pcall(body, **kw):
    return pl.pallas_call(body, **kw)


def _params(n_grid):
    return pltpu.CompilerParams(dimension_semantics=("arbitrary",) * n_grid,
                                vmem_limit_bytes=VMEM_LIMIT)


def _dot(a, b):
    return jnp.dot(a, b, preferred_element_type=F32)


def _dot_nt(a, b):
    return lax.dot_general(a, b, (((1,), (1,)), ((), ())), preferred_element_type=F32)


def _dot_tn(a, b):
    return lax.dot_general(a, b, (((0,), (0,)), ((), ())), preferred_element_type=F32)


def _sigmoid(z):
    return 1.0 / (1.0 + jnp.exp(-z))


_GELU_C = math.sqrt(2.0 / math.pi)


def _gelu_and_grad(y):
    inner = _GELU_C * (y + 0.044715 * (y * y * y))
    t = jnp.tanh(inner)
    g = 0.5 * y * (1.0 + t)
    dg = 0.5 * (1.0 + t) + 0.5 * y * (1.0 - t * t) * (_GELU_C * (1.0 + 3.0 * 0.044715 * (y * y)))
    return g, dg


def _silu_and_grad(z):
    s = _sigmoid(z)
    return z * s, s * (1.0 + z * (1.0 - s))


def _shift_down(v, halo, k):
    rolled = pltpu.roll(v, k, 0)
    row = lax.broadcasted_iota(jnp.int32, v.shape, 0)
    for r in range(k):
        rolled = jnp.where(row == r, halo[SUBLANES - k + r:SUBLANES - k + r + 1, :], rolled)
    return rolled


def _shift_up(v, halo, k):
    n = v.shape[0]
    rolled = pltpu.roll(v, n - k, 0)
    row = lax.broadcasted_iota(jnp.int32, v.shape, 0)
    for r in range(k):
        rolled = jnp.where(row == n - k + r, halo[r:r + 1, :], rolled)
    return rolled


def _mesh_pos():
    return lax.axis_index("x"), lax.axis_index("y"), lax.axis_index("c")


def _direct_copies(srcs_for, out_refs, send_sems, recv_sems, loc_sems):
    x, y, c = _mesh_pos()
    me_id = 4 * x + 2 * y + c
    n_arr = len(out_refs)
    dsts = [r.at[me_id] for r in out_refs]
    own = srcs_for(me_id)
    mine = [pltpu.make_async_copy(own[a], dsts[a], loc_sems.at[a]) for a in range(n_arr)]
    sends = []
    for k in range(1, N_DEV):
        px, py, pc = x ^ ((k >> 2) & 1), y ^ ((k >> 1) & 1), c ^ (k & 1)
        src = srcs_for(4 * px + 2 * py + pc)
        for a in range(n_arr):
            sends.append(pltpu.make_async_remote_copy(
                src_ref=src[a], dst_ref=dsts[a],
                send_sem=send_sems.at[(k - 1) * n_arr + a], recv_sem=recv_sems.at[(k - 1) * n_arr + a],
                device_id=(px, py, pc), device_id_type=MESH))
    return mine, sends


class _TwoLevelGather:
    def __init__(self, srcs, slots, send_sems, recv_sems, loc_sems):
        self.srcs, self.slots, self.n_arr = srcs, slots, len(srcs)
        self.send_sems, self.recv_sems, self.loc_sems = send_sems, recv_sems, loc_sems
        x, y, c = _mesh_pos()
        self.c = c
        self.me, self.sib = (x, y, c), (x, y, 1 - c)
        self.chips = [(1 - x, y), (x, 1 - y), (1 - x, 1 - y)]

    def _copies(self, k, block, to, from_src=False):
        dev = 4 * block[0] + 2 * block[1] + block[2]
        return [pltpu.make_async_remote_copy(
            src_ref=self.srcs[a] if from_src else self.slots[a](dev), dst_ref=self.slots[a](dev),
            send_sem=self.send_sems.at[k * self.n_arr + a], recv_sem=self.recv_sems.at[k * self.n_arr + a],
            device_id=to, device_id_type=MESH) for a in range(self.n_arr)]

    def _local(self):
        dev = 4 * self.me[0] + 2 * self.me[1] + self.me[2]
        return [pltpu.make_async_copy(self.srcs[a], self.slots[a](dev), self.loc_sems.at[a])
                for a in range(self.n_arr)]

    def start(self, chips=(0, 1, 2)):
        for cp in self._local() + self._copies(0, self.me, self.sib, True):
            cp.start()
        self.start_to(chips)

    def start_to(self, chips):
        for j in chips:
            for cp in self._copies(1 + j, self.me, (*self.chips[j], self.c), True):
                cp.start()

    def wait_own(self):
        for cp in self._local():
            cp.wait()

    def wait_sibling(self):
        for cp in self._copies(0, self.sib, self.me):
            cp.wait_recv()

    def wait_and_pass_on(self, j):
        chip = self.chips[j]
        for cp in self._copies(1 + j, (*chip, self.c), self.me):
            cp.wait_recv()
        for cp in self._copies(4 + j, (*chip, self.c), self.sib):
            cp.start()

    def wait_passed_on(self, j):
        for cp in self._copies(4 + j, (*self.chips[j], 1 - self.c), self.me):
            cp.wait_recv()

    def wait_sends(self):
        for cp in self._copies(0, self.me, self.sib, True):
            cp.wait_send()
        for j, chip in enumerate(self.chips):
            for cp in self._copies(1 + j, self.me, (*chip, self.c), True) + self._copies(4 + j, (*chip, self.c), self.sib):
                cp.wait_send()

    def forward(self):
        for j in range(3):
            self.wait_and_pass_on(j)

    def finish(self):
        self.wait_sibling()
        for j in range(3):
            self.wait_passed_on(j)
        self.wait_sends()
        self.wait_own()


def _disc(a_re, a_im, log_dt, b_re, b_im):
    dt = jnp.exp(log_dt)
    mag = jnp.exp(a_re * dt)
    ab_re = mag * jnp.cos(a_im * dt)
    ab_im = mag * jnp.sin(a_im * dt)
    den = a_re * a_re + a_im * a_im
    p_re = ab_re - 1.0
    p_im = ab_im
    q_re = (p_re * a_re + p_im * a_im) / den
    q_im = (p_im * a_re - p_re * a_im) / den
    bb_re = q_re * b_re - q_im * b_im
    bb_im = q_re * b_im + q_im * b_re
    return ab_re, ab_im, bb_re, bb_im


def _split3(v):
    hi = v.astype(BF16)
    r1 = v - hi.astype(F32)
    mid = r1.astype(BF16)
    lo = (r1 - mid.astype(F32)).astype(BF16)
    return hi, mid, lo


def _select_dot(sel, v):
    return sum(_dot(sel, t) for t in _split3(v))


PACK_ROWS = 72
PACK_W = 512
ROW_FINAL_GAIN, ROW_NORM_GAIN, ROW_BGLU_D, ROW_CONV, ROW_LOSS, ROW_S5 = 0, 8, 16, 24, 32, 40
LANE_A_RE, LANE_A_IM, LANE_LOG_DT = 0, 128, 256


def _ssm_disc_bwd_pack(a_re_x, a_im_x, log_dt_x, b_re, b_im, g_ab_re, g_ab_im, dbb_re_d, dbb_im_d,
                       loss_t, dg8, dgf, dbg, dd, dcw, dc_re_d, dc_im_d):
    rows_gh = N_GROUPS * GROUP

    def body(are, aim, ldt, bre, bim, gabre, gabim, dbbre_ref, dbbim_ref,
             loss_ref, dg8_ref, dgf_ref, dbg_ref, dd_ref, dcw_ref, dcre_ref, dcim_ref,
             p_ref, gc_ref, gb_ref, gbb_re, gbb_im):
        r_g = lax.broadcasted_iota(jnp.int32, (N_GROUPS, rows_gh), 0)
        c_gh = lax.broadcasted_iota(jnp.int32, (N_GROUPS, rows_gh), 1)
        group_sum = (c_gh // GROUP == r_g).astype(BF16)
        r_gh = lax.broadcasted_iota(jnp.int32, (rows_gh, N_GROUPS), 0)
        c_g = lax.broadcasted_iota(jnp.int32, (rows_gh, N_GROUPS), 1)
        first_row = (r_gh == c_g * GROUP).astype(BF16)

        def diag_block(ref, j, gi):
            return ref[j, gi * GROUP:(gi + 1) * GROUP, gi * STATE:(gi + 1) * STATE]

        for j in range(N_JBLK):
            for gi in range(SUBLANES):
                r0 = (j * SUBLANES + gi) * GROUP
                gbb_re[r0:r0 + GROUP, :] = diag_block(dbbre_ref, j, gi)
                gbb_im[r0:r0 + GROUP, :] = diag_block(dbbim_ref, j, gi)
                both = jnp.concatenate([diag_block(dcre_ref, j, gi), -diag_block(dcim_ref, j, gi)], axis=1)
                gc_ref[r0:r0 + GROUP, :] = both.astype(BF16)

        _, vjp = jax.vjp(_disc, are[...], aim[...], ldt[...], bre[...], bim[...])
        d_are, d_aim, d_ldt, d_bre, d_bim = vjp((_select_dot(first_row, gabre[...]), _select_dot(first_row, gabim[...]),
                                                 gbb_re[...], gbb_im[...]))
        gb_ref[...] = jnp.concatenate([d_bre, d_bim], axis=1).astype(BF16)

        p_ref[...] = jnp.zeros_like(p_ref)
        half = D_MODEL // 2
        for r, src in ((ROW_FINAL_GAIN, dgf_ref), (ROW_NORM_GAIN, dg8_ref)):
            p_ref[r:r + 1, :] = src[0:1, 0:half]
            p_ref[r + 1:r + 2, :] = src[0:1, half:D_MODEL]
        p_ref[ROW_BGLU_D:ROW_BGLU_D + 1, :] = dbg_ref[...]
        p_ref[ROW_BGLU_D + 1:ROW_BGLU_D + 2, :] = dd_ref[...]
        p_ref[ROW_CONV:ROW_CONV + SUBLANES, :] = dcw_ref[...]
        p_ref[ROW_LOSS:ROW_LOSS + SUBLANES, 0:LANES] = loss_ref[...]
        s5 = slice(ROW_S5, ROW_S5 + N_GROUPS)
        p_ref[s5, LANE_A_RE:LANE_A_RE + STATE] = _select_dot(group_sum, d_are)
        p_ref[s5, LANE_A_IM:LANE_A_IM + STATE] = _select_dot(group_sum, d_aim)
        p_ref[s5, LANE_LOG_DT:LANE_LOG_DT + LANES] = _select_dot(group_sum, jnp.broadcast_to(d_ldt, (rows_gh, LANES)))

    return _pcall(body, name="ssm_disc_bwd_pack",
                  out_shape=(jax.ShapeDtypeStruct((PACK_ROWS, PACK_W), F32),
                             jax.ShapeDtypeStruct((rows_gh, 2 * STATE), BF16),
                             jax.ShapeDtypeStruct((rows_gh, 2 * STATE), BF16)),
                  scratch_shapes=[pltpu.VMEM((rows_gh, STATE), F32), pltpu.VMEM((rows_gh, STATE), F32)],
                  )(a_re_x, a_im_x, log_dt_x, b_re, b_im, g_ab_re, g_ab_im, dbb_re_d, dbb_im_d,
                    loss_t, dg8, dgf, dbg, dd, dcw, dc_re_d, dc_im_d)


def _s5_prepare(are, aim, ldt, bre, bim, cre, cim,
                o_ax_re, o_ax_im, o_ldt_x, o_ab_re, o_ab_im, o_bb_re, o_bb_im, o_c_re, o_c_imn):
    rows_gh = N_GROUPS * GROUP
    rep = (lax.broadcasted_iota(jnp.int32, (rows_gh, N_GROUPS), 0) // GROUP
           == lax.broadcasted_iota(jnp.int32, (rows_gh, N_GROUPS), 1)).astype(BF16)
    eye = (lax.broadcasted_iota(jnp.int32, (N_GROUPS, N_GROUPS), 0)
           == lax.broadcasted_iota(jnp.int32, (N_GROUPS, N_GROUPS), 1)).astype(F32)
    ldt_col = jnp.sum(eye * ldt[...], axis=1, keepdims=True)
    a_re_x = _select_dot(rep, are[...])
    a_im_x = _select_dot(rep, aim[...])
    ldt_x = _select_dot(rep, jnp.broadcast_to(ldt_col, (N_GROUPS, LANES)))[:, 0:1]
    o_ax_re[...] = a_re_x
    o_ax_im[...] = a_im_x
    o_ldt_x[...] = ldt_x
    ab_re, ab_im, bb_re, bb_im = _disc(a_re_x, a_im_x, ldt_x, bre[...], bim[...])
    for j in range(N_JBLK):
        first = [(j * SUBLANES + gi) * GROUP for gi in range(SUBLANES)]
        o_ab_re[j] = jnp.concatenate([ab_re[r:r + 1, :] for r in first], axis=1)
        o_ab_im[j] = jnp.concatenate([ab_im[r:r + 1, :] for r in first], axis=1)
    for o, v in ((o_bb_re, bb_re), (o_bb_im, bb_im), (o_c_re, cre[...]), (o_c_imn, -cim[...])):
        for j in range(N_JBLK):
            for gi in range(SUBLANES):
                r0 = (j * SUBLANES + gi) * GROUP
                parts = [v[r0:r0 + GROUP, :] if k == gi else jnp.zeros((GROUP, STATE), F32) for k in range(SUBLANES)]
                o[j, gi * GROUP:(gi + 1) * GROUP, :] = jnp.concatenate(parts, axis=1).astype(BF16)


def _in_proj(order, x2, g1, w_in_b, s5):
    n = x2.shape[0]
    tm = min(IN_TILE, n)
    n_tiles = n // tm
    n_s5_in = len(s5)
    n_s5_out = 9

    def body(order_ref, x_ref, g_ref, w_ref, *refs):
        s5_in = refs[:n_s5_in]
        xn_ref, proj_ref, wall_ref = refs[n_s5_in:n_s5_in + 3]
        s5_out = refs[n_s5_in + 3:n_s5_in + 3 + n_s5_out]
        xn_scr, wbuf, send_sems, recv_sems, loc_sems, out_sems = refs[n_s5_in + 3 + n_s5_out:]
        k = pl.program_id(0)
        i = pl.program_id(1)

        def slot(dev):
            return wbuf.at[dev // 2, :, pl.ds(pl.multiple_of((dev % 2) * COLS_PER_DEV, LANES), COLS_PER_DEV)]

        gather = _TwoLevelGather([w_ref], [slot], send_sems, recv_sems, loc_sems)

        @pl.when((k == 0) & (i == 0))
        def _():
            gather.start(chips=(0, 1))

        def own_chip():
            gather.wait_own()
            gather.wait_sibling()

        def other_chip(j):
            gather.wait_and_pass_on(j)
            if j == 0:
                gather.start_to((2,))
            gather.wait_passed_on(j)

        arrivals = [own_chip] + [functools.partial(other_chip, j) for j in range(3)]
        for kk, arrived in enumerate(arrivals):
            @pl.when((k == kk) & (i == 0))
            def _(arrived=arrived):
                arrived()

        rows = pl.ds(pl.multiple_of(i * tm, tm), tm)

        @pl.when(k == 0)
        def _():
            x = x_ref[...]
            r = lax.rsqrt(jnp.mean(x * x, axis=-1, keepdims=True) + EPS)
            xn = ((x * r) * g_ref[...]).astype(BF16)
            xn_scr[rows, :] = xn
            xn_ref[...] = xn

        proj_ref[...] = _dot(xn_scr[rows, :], wbuf[order_ref[k]])

        @pl.when((k == 0) & (i == n_tiles - 1))
        def _():
            _s5_prepare(*s5_in, *s5_out)

        @pl.when((k == N_CHIP - 1) & (i == n_tiles - 1))
        def _():
            gather.wait_sends()
            outs = [pltpu.make_async_copy(wbuf.at[q], wall_ref.at[:, q * COLS_PER_CHIP:(q + 1) * COLS_PER_CHIP],
                                          out_sems.at[q]) for q in range(N_CHIP)]
            for cp in outs:
                cp.start()
            for cp in outs:
                cp.wait()

    tile_once = lambda k, i, order: (jnp.where(k == 0, i, n_tiles - 1), 0)
    whole = lambda shape: pl.BlockSpec(shape, lambda k, i, order: (0,) * len(shape))
    rows_gh = N_GROUPS * GROUP
    s5_out_shapes = ([(rows_gh, STATE), F32], [(rows_gh, STATE), F32], [(rows_gh, 1), F32],
                     [(N_JBLK, 1, JB_ST), F32], [(N_JBLK, 1, JB_ST), F32]) + ([(N_JBLK, JB_CH, JB_ST), BF16],) * 4
    grid_spec = pltpu.PrefetchScalarGridSpec(
        num_scalar_prefetch=1, grid=(N_CHIP, n_tiles),
        in_specs=[pl.BlockSpec((tm, D_MODEL), tile_once),
                  whole((1, D_MODEL)),
                  HBM_SPEC,
                  *(whole(a.shape) for a in s5)],
        out_specs=(pl.BlockSpec((tm, D_MODEL), tile_once),
                   pl.BlockSpec((tm, COLS_PER_CHIP), lambda k, i, order: (i, order[k])),
                   HBM_SPEC,
                   *(whole(shape) for shape, _ in s5_out_shapes)),
        scratch_shapes=[pltpu.VMEM((n, D_MODEL), BF16), pltpu.VMEM((N_CHIP, D_MODEL, COLS_PER_CHIP), BF16),
                        pltpu.SemaphoreType.DMA((7,)), pltpu.SemaphoreType.DMA((7,)), pltpu.SemaphoreType.DMA((1,)),
                        pltpu.SemaphoreType.DMA((N_CHIP,))])
    outs = _pcall(
        body, name="in_proj", grid_spec=grid_spec,
        out_shape=(jax.ShapeDtypeStruct((n, D_MODEL), BF16), jax.ShapeDtypeStruct((n, IN_COLS), F32),
                   jax.ShapeDtypeStruct((D_MODEL, IN_COLS), BF16),
                   *(jax.ShapeDtypeStruct(shape, dt) for shape, dt in s5_out_shapes)),
        compiler_params=_params(2),
    )(order, x2, g1, w_in_b, *s5)
    return outs[0], outs[1], outs[2], outs[3:]


def _cmul(p, q):
    return p[0] * q[0] - p[1] * q[1], p[0] * q[1] + p[1] * q[0]


def _scan_tables(ar, ai, width, reverse):
    pows = [(ar, ai)]
    for _ in range(SUBLANES - 1):
        pows.append(_cmul(pows[-1], (ar, ai)))
    row = lax.broadcasted_iota(jnp.int32, (SUBLANES, width), 0)

    def bc(v):
        return jnp.broadcast_to(v, (SUBLANES, width))

    levels = []
    for k in (1, 2, 4):
        keep = (row <= SUBLANES - 1 - k) if reverse else (row >= k)
        levels.append((jnp.where(keep, bc(pows[k - 1][0]), 0.0), jnp.where(keep, bc(pows[k - 1][1]), 0.0)))
    cre = jnp.zeros((SUBLANES, width), F32)
    cim = jnp.zeros((SUBLANES, width), F32)
    for r in range(SUBLANES):
        e = (SUBLANES - r) if reverse else (r + 1)
        cre = jnp.where(row == r, bc(pows[e - 1][0]), cre)
        cim = jnp.where(row == r, bc(pows[e - 1][1]), cim)
    return levels, (cre, cim)


def _load_chunked(src_ref, b, dst_ref, n_rows):
    n_blk = n_rows // SUBLANES
    for i in range(n_blk):
        dst_ref[b, i * SUBLANES:(i + 1) * SUBLANES, :] = src_ref[b, pl.ds(i, SUBLANES, stride=n_blk), :]


def _store_chunked(val, dst_ref, b, n_rows):
    n_blk = n_rows // SUBLANES
    for i in range(n_blk):
        dst_ref[b, pl.ds(i, SUBLANES, stride=n_blk), :] = val[i * SUBLANES:(i + 1) * SUBLANES, :]


def _chunk_scan(re_ref, im_ref, bs, car_ref, ar, ai, n_rows, reverse, on_block=None):
    width = re_ref.shape[2]
    n_blk = n_rows // SUBLANES
    shape = (SUBLANES, width)
    abr = jnp.broadcast_to(ar, shape)
    abi = jnp.broadcast_to(ai, shape)
    order = list(range(n_blk - 1, -1, -1)) if reverse else list(range(n_blk))

    def blk(ref, b, i):
        return ref[b, i * SUBLANES:(i + 1) * SUBLANES, :]

    def step(state, b, i):
        sr, si = state
        return abr * sr - abi * si + blk(re_ref, b, i), abr * si + abi * sr + blk(im_ref, b, i)

    finals = {b: (blk(re_ref, b, order[0]), blk(im_ref, b, order[0])) for b in bs}
    for i in order[1:]:
        for b in bs:
            finals[b] = step(finals[b], b, i)

    mr, mi = ar, ai
    for _ in range(n_blk.bit_length() - 1):
        mr, mi = _cmul((mr, mi), (mr, mi))
    levels, _ = _scan_tables(mr, mi, width, reverse)
    mbr = jnp.broadcast_to(mr, shape)
    mbi = jnp.broadcast_to(mi, shape)
    row = lax.broadcasted_iota(jnp.int32, shape, 0)
    edge_in = SUBLANES - 1 if reverse else 0
    edge_out = 0 if reverse else SUBLANES - 1
    sh1 = SUBLANES - 1 if reverse else 1
    states = {}
    for b in bs:
        fr, fi = finals[b]
        gr = jnp.where(row == edge_in, jnp.broadcast_to(car_ref[b, 0:1, :], shape), pltpu.roll(fr, sh1, 0))
        gi = jnp.where(row == edge_in, jnp.broadcast_to(car_ref[b, 1:2, :], shape), pltpu.roll(fi, sh1, 0))
        for (lr, li), k in zip(levels, (1, 2, 4)):
            sh = (SUBLANES - k) if reverse else k
            sr = pltpu.roll(gr, sh, 0)
            si = pltpu.roll(gi, sh, 0)
            gr, gi = gr + (lr * sr - li * si), gi + (lr * si + li * sr)
        car_ref[b, 0:1, :] = (fr + (mbr * gr - mbi * gi))[edge_out:edge_out + 1, :]
        car_ref[b, 1:2, :] = (fi + (mbr * gi + mbi * gr))[edge_out:edge_out + 1, :]
        states[b] = (gr, gi)

    for i in order:
        for b in bs:
            states[b] = step(states[b], b, i)
            re_ref[b, i * SUBLANES:(i + 1) * SUBLANES, :] = states[b][0]
            im_ref[b, i * SUBLANES:(i + 1) * SUBLANES, :] = states[b][1]
            if on_block is not None:
                on_block(b, i, *states[b])


def _ssm_fwd(u, bb_re, bb_im, c_re_t, c_imn_t, d_row, ab_re, ab_im, w_out_b, w_glu_b, conv_p, n_seq, seq):
    tt = min(SCAN_TILE, seq)
    nt = seq // tt

    def body(u_ref, bbre, bbim, cre, cimn, d_ref, are, aim, wout_ref, wglu_ref, cw_ref,
             sre_ref, sim_ref, y_ref, oout_ref, oglu_ref, ocw_ref,
             up_ref, car_ref, send_sems, recv_sems, loc_sems):
        j = pl.program_id(0)
        t = pl.program_id(1)
        gather = _TwoLevelGather(
            [wout_ref, wglu_ref, cw_ref],
            [lambda dev: oout_ref.at[pl.ds(pl.multiple_of(dev * OUT_ROWS_PER_DEV, OUT_ROWS_PER_DEV), OUT_ROWS_PER_DEV), :],
             lambda dev: oglu_ref.at[pl.ds(pl.multiple_of(dev * GLU_ROWS_PER_DEV, GLU_ROWS_PER_DEV), GLU_ROWS_PER_DEV), :],
             lambda dev: ocw_ref.at[dev]],
            send_sems, recv_sems, loc_sems)

        @pl.when((j == 0) & (t == 0))
        def _():
            gather.start()

        @pl.when((j == N_JBLK // 2) & (t == 0))
        def _():
            gather.forward()

        @pl.when(t == 0)
        def _():
            car_ref[...] = jnp.zeros_like(car_ref)

        bs = list(range(n_seq))
        for b in bs:
            _load_chunked(u_ref, b, up_ref, tt)
        for b in bs:
            ub = up_ref[b].astype(BF16)
            sre_ref[b] = _dot(ub, bbre[0])
            sim_ref[b] = _dot(ub, bbim[0])
            _chunk_scan(sre_ref, sim_ref, [b], car_ref, are[0], aim[0], tt, reverse=False)
        for b in bs:
            yp = (_dot_nt(sre_ref[b].astype(BF16), cre[0]) + _dot_nt(sim_ref[b].astype(BF16), cimn[0])
                  + d_ref[...] * up_ref[b])
            _store_chunked(yp, y_ref, b, tt)

        @pl.when((j == N_JBLK - 1) & (t == nt - 1))
        def _():
            gather.finish()

    tok = lambda j, t: (0, t, j)
    blk3 = lambda j, t: (j, 0, 0)
    row = lambda j, t: (0, j)
    st = jax.ShapeDtypeStruct((n_seq, seq, N_JBLK * JB_ST), F32)
    n_arr = 3
    return _pcall(
        body, name="ssm_fwd", grid=(N_JBLK, nt),
        out_shape=(st, st, jax.ShapeDtypeStruct((n_seq, seq, SSM_W), F32),
                   jax.ShapeDtypeStruct((D_MODEL, D_MODEL), BF16), jax.ShapeDtypeStruct((SSM_W, SSM_W), BF16),
                   jax.ShapeDtypeStruct((N_DEV, SUBLANES, LANES), F32)),
        in_specs=[pl.BlockSpec((n_seq, tt, JB_CH), tok),
                  pl.BlockSpec((1, JB_CH, JB_ST), blk3), pl.BlockSpec((1, JB_CH, JB_ST), blk3),
                  pl.BlockSpec((1, JB_CH, JB_ST), blk3), pl.BlockSpec((1, JB_CH, JB_ST), blk3),
                  pl.BlockSpec((1, JB_CH), row), pl.BlockSpec((1, 1, JB_ST), blk3), pl.BlockSpec((1, 1, JB_ST), blk3),
                  HBM_SPEC, HBM_SPEC, HBM_SPEC],
        out_specs=(pl.BlockSpec((n_seq, tt, JB_ST), tok), pl.BlockSpec((n_seq, tt, JB_ST), tok),
                   pl.BlockSpec((n_seq, tt, JB_CH), tok), HBM_SPEC, HBM_SPEC, HBM_SPEC),
        scratch_shapes=[pltpu.VMEM((n_seq, tt, JB_CH), F32), pltpu.VMEM((n_seq, SUBLANES, JB_ST), F32),
                        pltpu.SemaphoreType.DMA((7 * n_arr,)), pltpu.SemaphoreType.DMA((7 * n_arr,)),
                        pltpu.SemaphoreType.DMA((n_arr,))],
        compiler_params=_params(2),
    )(u, bb_re, bb_im, c_re_t, c_imn_t, d_row, ab_re, ab_im, w_out_b, w_glu_b, conv_p)


def _ssm_bwd(dy, u, s_re, s_im, bb_re, bb_im, c_re_t, c_imn_t, d_row, ab_re, ab_im, g_out, g_glu, n_seq, seq):
    tt = min(SCAN_TILE, seq)
    nt = seq // tt
    rows8 = tt // SUBLANES

    def body(dy_ref, u_ref, sre_ref, sim_ref, pre_ref, pim_ref, bbre, bbim, cre, cimn, d_ref, are, aim,
             gout_ref, gglu_ref,
             du_ref, dcre_ref, dcim_ref, dbbre_ref, dbbim_ref, dare_ref, daim_ref, dd_ref, rout_ref, rglu_ref,
             lre_ref, lim_ref, dyp_ref, up_ref, car_ref, send_sems, recv_sems, loc_sems):
        j = pl.program_id(0)
        tr = pl.program_id(1)

        def exchange():
            return _direct_copies(lambda pid: [gout_ref.at[pid], gglu_ref.at[pid]], [rout_ref, rglu_ref],
                                  send_sems, recv_sems, loc_sems)

        @pl.when((j == 0) & (tr == 0))
        def _():
            mine, sends = exchange()
            for cp in mine + sends:
                cp.start()

        @pl.when(tr == 0)
        def _():
            car_ref[...] = jnp.zeros_like(car_ref)
            for r in (dcre_ref, dcim_ref, dbbre_ref, dbbim_ref, dare_ref, daim_ref, dd_ref):
                r[...] = jnp.zeros_like(r)

        first = tr == nt - 1
        row = lax.broadcasted_iota(jnp.int32, (SUBLANES, JB_ST), 0)
        n_blk = tt // SUBLANES
        bs = list(range(n_seq))
        for b in bs:
            _load_chunked(dy_ref, b, dyp_ref, tt)
            _load_chunked(u_ref, b, up_ref, tt)
        for b in bs:
            dyb = dyp_ref[b].astype(BF16)
            lre_ref[b] = _dot(dyb, cre[0])
            lim_ref[b] = _dot(dyb, cimn[0])
        acc = {b: [jnp.zeros((SUBLANES, JB_ST), F32), jnp.zeros((SUBLANES, JB_ST), F32)] for b in bs}

        def on_block(b, i, lr, li):
            if i > 0:
                spr = sre_ref[b, (i - 1) * SUBLANES:i * SUBLANES, :]
                spi = sim_ref[b, (i - 1) * SUBLANES:i * SUBLANES, :]
            else:
                hr = jnp.where(first, 0.0, pre_ref[b, SUBLANES - 1:SUBLANES, :])
                hi = jnp.where(first, 0.0, pim_ref[b, SUBLANES - 1:SUBLANES, :])
                last_r = sre_ref[b, (n_blk - 1) * SUBLANES:n_blk * SUBLANES, :]
                last_i = sim_ref[b, (n_blk - 1) * SUBLANES:n_blk * SUBLANES, :]
                spr = jnp.where(row == 0, jnp.broadcast_to(hr, row.shape), pltpu.roll(last_r, 1, 0))
                spi = jnp.where(row == 0, jnp.broadcast_to(hi, row.shape), pltpu.roll(last_i, 1, 0))
            acc[b][0] = acc[b][0] + (lr * spr + li * spi)
            acc[b][1] = acc[b][1] + (li * spr - lr * spi)

        _chunk_scan(lre_ref, lim_ref, bs, car_ref, are[0], -aim[0], tt, reverse=True, on_block=on_block)
        for b in bs:
            dare_ref[...] += jnp.sum(acc[b][0], axis=0, keepdims=True)
            daim_ref[...] += jnp.sum(acc[b][1], axis=0, keepdims=True)
            dyp = dyp_ref[b]
            up = up_ref[b]
            dyb = dyp.astype(BF16)
            ub = up.astype(BF16)
            lrb = lre_ref[b].astype(BF16)
            lib = lim_ref[b].astype(BF16)
            dup = d_ref[...] * dyp + _dot_nt(lrb, bbre[0]) + _dot_nt(lib, bbim[0])
            _store_chunked(dup, du_ref, b, tt)
            dbbre_ref[0] += _dot_tn(ub, lrb)
            dbbim_ref[0] += _dot_tn(ub, lib)
            dcre_ref[0] += _dot_tn(dyb, sre_ref[b].astype(BF16))
            dcim_ref[0] += _dot_tn(dyb, sim_ref[b].astype(BF16))
            dd_ref[...] += jnp.sum(dyp * up, axis=0, keepdims=True)

        @pl.when((j == N_JBLK - 1) & (tr == nt - 1))
        def _():
            mine, sends = exchange()
            for cp in sends + mine:
                cp.wait()

    tok = lambda j, t: (0, nt - 1 - t, j)
    halo = lambda j, t: (0, jnp.maximum((nt - 1 - t) * rows8 - 1, 0), j)
    blk3 = lambda j, t: (j, 0, 0)
    row1 = lambda j, t: (0, j)
    acc_shape = jax.ShapeDtypeStruct((N_JBLK, JB_CH, JB_ST), F32)
    return _pcall(
        body, name="ssm_bwd", grid=(N_JBLK, nt),
        out_shape=(jax.ShapeDtypeStruct((n_seq, seq, SSM_W), F32), acc_shape, acc_shape, acc_shape, acc_shape,
                   jax.ShapeDtypeStruct((1, N_JBLK * JB_ST), F32), jax.ShapeDtypeStruct((1, N_JBLK * JB_ST), F32),
                   jax.ShapeDtypeStruct((1, SSM_W), F32),
                   jax.ShapeDtypeStruct((N_DEV,) + g_out.shape[1:], F32),
                   jax.ShapeDtypeStruct((N_DEV,) + g_glu.shape[1:], F32)),
        in_specs=[pl.BlockSpec((n_seq, tt, JB_CH), tok), pl.BlockSpec((n_seq, tt, JB_CH), tok),
                  pl.BlockSpec((n_seq, tt, JB_ST), tok), pl.BlockSpec((n_seq, tt, JB_ST), tok),
                  pl.BlockSpec((n_seq, SUBLANES, JB_ST), halo), pl.BlockSpec((n_seq, SUBLANES, JB_ST), halo),
                  pl.BlockSpec((1, JB_CH, JB_ST), blk3), pl.BlockSpec((1, JB_CH, JB_ST), blk3),
                  pl.BlockSpec((1, JB_CH, JB_ST), blk3), pl.BlockSpec((1, JB_CH, JB_ST), blk3),
                  pl.BlockSpec((1, JB_CH), row1), pl.BlockSpec((1, 1, JB_ST), blk3), pl.BlockSpec((1, 1, JB_ST), blk3),
                  HBM_SPEC, HBM_SPEC],
        out_specs=(pl.BlockSpec((n_seq, tt, JB_CH), tok),
                   pl.BlockSpec((1, JB_CH, JB_ST), blk3), pl.BlockSpec((1, JB_CH, JB_ST), blk3),
                   pl.BlockSpec((1, JB_CH, JB_ST), blk3), pl.BlockSpec((1, JB_CH, JB_ST), blk3),
                   pl.BlockSpec((1, JB_ST), row1), pl.BlockSpec((1, JB_ST), row1), pl.BlockSpec((1, JB_CH), row1),
                   HBM_SPEC, HBM_SPEC),
        scratch_shapes=[pltpu.VMEM((n_seq, tt, JB_ST), F32), pltpu.VMEM((n_seq, tt, JB_ST), F32),
                        pltpu.VMEM((n_seq, tt, JB_CH), F32), pltpu.VMEM((n_seq, tt, JB_CH), F32),
                        pltpu.VMEM((n_seq, SUBLANES, JB_ST), F32),
                        pltpu.SemaphoreType.DMA((7 * 2,)), pltpu.SemaphoreType.DMA((7 * 2,)),
                        pltpu.SemaphoreType.DMA((2,))],
        compiler_params=_params(2),
    )(dy, u, s_re, s_im, s_re, s_im, bb_re, bb_im, c_re_t, c_imn_t, d_row, ab_re, ab_im, g_out, g_glu)


def _mix(x2, tgt2, y, proj, gf, b_glu, conv8, w_glu_f, w_out_f, seq):
    n = x2.shape[0]
    tm = TOK_TILE
    tiles_per_seq = seq // tm
    rows8 = tm // SUBLANES

    def body(x_ref, t_ref, y_ref, zs_ref, h_ref, bc_ref, cc_ref, zc_ref, hp_ref, ccp_ref,
             gf_ref, bg_ref, cw_ref, wg_ref, wo_ref,
             dh2_ref, dy_ref, dzs_ref, dbc_ref, dzc_ref, dyc_ref,
             dwo_ref, dwg_ref, loss_ref, dgf_ref, dbg_ref, dcw_ref):
        i = pl.program_id(0)

        @pl.when(i == 0)
        def _():
            for r in (dwo_ref, dwg_ref, loss_ref, dgf_ref, dbg_ref, dcw_ref):
                r[...] = jnp.zeros_like(r)

        yv = y_ref[...]
        y1, dgelu = _gelu_and_grad(yv)
        y1b = y1.astype(BF16)
        gate = _sigmoid(_dot(y1b, wg_ref[...]) + bg_ref[...])
        y2 = y1 * gate
        szs, dszs = _silu_and_grad(zs_ref[...])
        yssm = y2 * szs
        hv = h_ref[...]
        ccv = cc_ref[...]
        bcv = bc_ref[...]
        v = ccv * hv
        first = (i % tiles_per_seq) == 0
        vhalo = jnp.where(first, 0.0, ccp_ref[...] * hp_ref[...])
        v1 = _shift_down(v, vhalo, 1)
        v2 = _shift_down(v, vhalo, 2)
        w0 = cw_ref[0:1, :]
        w1 = cw_ref[1:2, :]
        w2 = cw_ref[2:3, :]
        yc = w0 * v2 + w1 * v1 + w2 * v
        szc, dszc = _silu_and_grad(zc_ref[...])
        yconv = (bcv * yc) * szc
        ysb = yssm.astype(BF16)
        ycb = yconv.astype(BF16)
        h2 = x_ref[...] + _dot(ysb, wo_ref[0:SSM_W, :]) + _dot(ycb, wo_ref[SSM_W:, :])
        r2 = lax.rsqrt(jnp.mean(h2 * h2, axis=-1, keepdims=True) + EPS)
        hn = h2 * r2
        gfv = gf_ref[...]
        err = hn * gfv - t_ref[...]
        loss_ref[...] += 0.5 * jnp.sum(jnp.mean(err * err, axis=-1, keepdims=True))
        dout = err * (1.0 / D_MODEL)
        dgf_ref[...] += jnp.sum(dout * hn, axis=0, keepdims=True)
        dn = dout * gfv
        dh2 = r2 * (dn - hn * jnp.mean(dn * hn, axis=-1, keepdims=True))
        dh2_ref[...] = dh2
        dh2b = dh2.astype(BF16)
        dwo_ref[0:SSM_W, :] += _dot_tn(ysb, dh2b)
        dwo_ref[SSM_W:, :] += _dot_tn(ycb, dh2b)
        dyssm = _dot_nt(dh2b, wo_ref[0:SSM_W, :])
        dyconv = _dot_nt(dh2b, wo_ref[SSM_W:, :])
        dy2 = dyssm * szs
        dzs_ref[...] = (dyssm * y2 * dszs).astype(BF16)
        dgp = dy2 * y1 * (gate * (1.0 - gate))
        dgpb = dgp.astype(BF16)
        dy1 = dy2 * gate + _dot_nt(dgpb, wg_ref[...])
        dwg_ref[...] += _dot_tn(y1b, dgpb)
        dbg_ref[...] += jnp.sum(dgp, axis=0, keepdims=True)
        dy_ref[...] = dy1 * dgelu
        dbc_ref[...] = (dyconv * yc * szc).astype(BF16)
        dyc = dyconv * bcv * szc
        dyc_ref[...] = dyc
        dzc_ref[...] = (dyconv * bcv * yc * dszc).astype(BF16)
        dcw_ref[0:1, :] += jnp.sum(dyc * v2, axis=0, keepdims=True)
        dcw_ref[1:2, :] += jnp.sum(dyc * v1, axis=0, keepdims=True)
        dcw_ref[2:3, :] += jnp.sum(dyc * v, axis=0, keepdims=True)

    tile_d = pl.BlockSpec((tm, D_MODEL), lambda i: (i, 0))
    tile_s = pl.BlockSpec((tm, SSM_W), lambda i: (i, 0))
    seg_of = lambda c: pl.BlockSpec((tm, SSM_W), lambda i: (i, c))
    halo_of = lambda c: pl.BlockSpec((SUBLANES, SSM_W), lambda i: (jnp.maximum(i * rows8 - 1, 0), c))
    const = lambda shape: pl.BlockSpec(shape, lambda i: (0,) * len(shape))
    seg = jax.ShapeDtypeStruct((n, SSM_W), F32)
    seg_b = jax.ShapeDtypeStruct((n, SSM_W), BF16)
    return _pcall(
        body, name="mix", grid=(n // tm,),
        out_shape=(jax.ShapeDtypeStruct((n, D_MODEL), F32), seg, seg_b, seg_b, seg_b, seg,
                   jax.ShapeDtypeStruct((D_MODEL, D_MODEL), F32), jax.ShapeDtypeStruct((SSM_W, SSM_W), F32),
                   jax.ShapeDtypeStruct((SUBLANES, LANES), F32), jax.ShapeDtypeStruct((1, D_MODEL), F32),
                   jax.ShapeDtypeStruct((1, SSM_W), F32), jax.ShapeDtypeStruct((SUBLANES, CONV_W), F32)),
        in_specs=[tile_d, tile_d, tile_s, seg_of(SEG_ZS), seg_of(SEG_H), seg_of(SEG_BC), seg_of(SEG_CC), seg_of(SEG_ZC),
                  halo_of(SEG_H), halo_of(SEG_CC),
                  const((1, D_MODEL)), const((1, SSM_W)), const((SUBLANES, CONV_W)),
                  const((SSM_W, SSM_W)), const((D_MODEL, D_MODEL))],
        out_specs=(tile_d, tile_s, tile_s, tile_s, tile_s, tile_s,
                   const((D_MODEL, D_MODEL)), const((SSM_W, SSM_W)), const((SUBLANES, LANES)),
                   const((1, D_MODEL)), const((1, SSM_W)), const((SUBLANES, CONV_W))),
        compiler_params=_params(1),
    )(x2, tgt2, y, proj, proj, proj, proj, proj, proj, proj, gf, b_glu, conv8, w_glu_f, w_out_f)


def _in_bwd(x2, dh2, du, dzs, dyc, proj, dbc, dzc, g1, conv8, w_full, seq):
    n = x2.shape[0]
    tm = TOK_TILE
    n_tiles = n // tm
    tiles_per_seq = seq // tm
    rows8 = tm // SUBLANES
    n_blk8 = n // SUBLANES

    def body(x_ref, dh2_ref, du_ref, dzs_ref, dyc_ref, dycn_ref, h_ref, cc_ref, dbc_ref, dzc_ref,
             g_ref, cw_ref, w_ref, gx_ref, dp_ref, dg_ref):
        i = pl.program_id(0)

        @pl.when(i == 0)
        def _():
            dg_ref[...] = jnp.zeros_like(dg_ref)

        dyc = dyc_ref[...]
        last = (i % tiles_per_seq) == tiles_per_seq - 1
        nhalo = jnp.where(last, 0.0, dycn_ref[...])
        dv = (cw_ref[2:3, :] * dyc + cw_ref[1:2, :] * _shift_up(dyc, nhalo, 1)
              + cw_ref[0:1, :] * _shift_up(dyc, nhalo, 2))
        parts = (du_ref[...], dzs_ref[...], dv * cc_ref[...], dbc_ref[...], dv * h_ref[...], dzc_ref[...])
        dxn = jnp.zeros((tm, D_MODEL), F32)
        for k, p in enumerate(parts):
            pb = p.astype(BF16)
            dp_ref[:, k * SSM_W:(k + 1) * SSM_W] = pb
            dxn = dxn + _dot_nt(pb, w_ref[:, k * SSM_W:(k + 1) * SSM_W])
        x = x_ref[...]
        r = lax.rsqrt(jnp.mean(x * x, axis=-1, keepdims=True) + EPS)
        xh = x * r
        dg_ref[...] += jnp.sum(dxn * xh, axis=0, keepdims=True)
        dn = dxn * g_ref[...]
        gx_ref[...] = dh2_ref[...] + r * (dn - xh * jnp.mean(dn * xh, axis=-1, keepdims=True))

    tile_d = pl.BlockSpec((tm, D_MODEL), lambda i: (i, 0))
    tile_s = pl.BlockSpec((tm, SSM_W), lambda i: (i, 0))
    seg_of = lambda c: pl.BlockSpec((tm, SSM_W), lambda i: (i, c))
    nhalo = pl.BlockSpec((SUBLANES, SSM_W), lambda i: (jnp.minimum((i + 1) * rows8, n_blk8 - 1), 0))
    const = lambda shape: pl.BlockSpec(shape, lambda i: (0,) * len(shape))
    return _pcall(
        body, name="in_bwd", grid=(n_tiles,),
        out_shape=(jax.ShapeDtypeStruct((n, D_MODEL), F32), jax.ShapeDtypeStruct((n, IN_COLS), BF16),
                   jax.ShapeDtypeStruct((SUBLANES, D_MODEL), F32)),
        in_specs=[tile_d, tile_d, tile_s, tile_s, tile_s, nhalo, seg_of(SEG_H), seg_of(SEG_CC), tile_s, tile_s,
                  const((1, D_MODEL)), const((SUBLANES, CONV_W)), const((D_MODEL, IN_COLS))],
        out_specs=(tile_d, pl.BlockSpec((tm, IN_COLS), lambda i: (i, 0)), const((SUBLANES, D_MODEL))),
        compiler_params=_params(1),
    )(x2, dh2, du, dzs, dyc, dyc, proj, proj, dbc, dzc, g1, conv8, w_full)


def _dw_in_exchange(order, xn, dproj, smalls):
    n = xn.shape[0]
    tk = 512
    nk = n // tk
    piece = (D_MODEL, COLS_PER_DEV)
    n_small = len(smalls)

    def body(order_ref, xn_hbm, dp_ref, *refs):
        del order_ref
        sm_refs = refs[:n_small]
        own_ref, rchip_ref = refs[n_small:n_small + 2]
        rsm_refs = refs[n_small + 2:2 * n_small + 2]
        (xn_ref, acc, stage, sbuf, xn_sems, give_send, give_recv, keep_send, keep_recv,
         sm_send, sm_recv, sm_loc) = refs[2 * n_small + 2:]
        s = pl.program_id(0)

        def xn_copy(kk):
            rows = pl.ds(pl.multiple_of(kk * tk, tk), tk)
            return pltpu.make_async_copy(xn_hbm.at[rows, :], xn_ref.at[rows, :], xn_sems.at[kk])

        @pl.when(s == 0)
        def _():
            for kk in range(nk):
                xn_copy(kk).start()
            xn_copy(0).wait()

        x, y, c = _mesh_pos()
        sib = (x, y, 1 - c)
        chips = [(1 - x, 1 - y), (1 - x, y), (x, 1 - y)]
        gather = _TwoLevelGather(list(sm_refs), [functools.partial(lambda r, dev: r.at[dev], r) for r in rsm_refs],
                                 sm_send, sm_recv, sm_loc)

        def half(i, core):
            return acc.at[i % 2, :, pl.ds(pl.multiple_of(core * COLS_PER_DEV, LANES), COLS_PER_DEV)]

        def give(i):
            return pltpu.make_async_remote_copy(src_ref=half(i, 1 - c), dst_ref=stage.at[i], send_sem=give_send.at[i],
                                                recv_sem=give_recv.at[i], device_id=sib, device_id_type=MESH)

        def keep(i):
            return pltpu.make_async_remote_copy(src_ref=sbuf.at[i], dst_ref=rchip_ref.at[i], send_sem=keep_send.at[i],
                                                recv_sem=keep_recv.at[i], device_id=(*chips[i], c), device_id_type=MESH)

        def chip_sum(i):
            give(i).wait_recv()
            mine = [acc[i % 2, :, cc * COLS_PER_DEV:(cc + 1) * COLS_PER_DEV] for cc in range(2)]
            return jnp.where(c == 0, mine[0], mine[1]) + stage[i]

        @pl.when(s == 0)
        def _():
            gather.start()

        @pl.when(s == N_CHIP // 2)
        def _():
            gather.forward()

        for k in range(2, N_CHIP):
            @pl.when(s == k)
            def _(k=k):
                give(k - 2).wait_send()

        slot = s % 2
        acc[slot] = _dot_tn(xn_ref[pl.ds(0, tk), :], dp_ref[pl.ds(0, tk), :])

        def kstep(kk, carry):
            @pl.when(s == 0)
            def _():
                xn_copy(kk).wait()

            off = pl.multiple_of(kk * tk, tk)
            acc[slot] += _dot_tn(xn_ref[pl.ds(off, tk), :], dp_ref[pl.ds(off, tk), :])
            return carry

        n_first = min(nk, 3)
        lax.fori_loop(1, n_first, kstep, 0)
        for k in range(1, N_CHIP):
            @pl.when(s == k)
            def _(k=k):
                sbuf[k - 1] = chip_sum(k - 1).astype(BF16)
                keep(k - 1).start()

        lax.fori_loop(n_first, nk, kstep, 0)

        for k in range(N_CHIP):
            @pl.when(s == k)
            def _(k=k):
                give(k).start()

        @pl.when(s == N_CHIP - 1)
        def _():
            own_ref[...] = chip_sum(N_CHIP - 1)
            give(N_CHIP - 2).wait_send()
            give(N_CHIP - 1).wait_send()
            for i in range(3):
                keep(i).wait()
            gather.finish()

    grid_spec = pltpu.PrefetchScalarGridSpec(
        num_scalar_prefetch=1, grid=(N_CHIP,),
        in_specs=[HBM_SPEC,
                  pl.BlockSpec((n, COLS_PER_CHIP), lambda s, order: (0, order[s])),
                  *([HBM_SPEC] * n_small)],
        out_specs=(pl.BlockSpec(piece, lambda s, order: (0, 0)), HBM_SPEC, *([HBM_SPEC] * n_small)),
        scratch_shapes=[pltpu.VMEM((n, D_MODEL), BF16),
                        pltpu.VMEM((2, D_MODEL, COLS_PER_CHIP), F32), pltpu.VMEM((4,) + piece, F32),
                        pltpu.VMEM((3,) + piece, BF16),
                        pltpu.SemaphoreType.DMA((nk,)),
                        pltpu.SemaphoreType.DMA((4,)), pltpu.SemaphoreType.DMA((4,)),
                        pltpu.SemaphoreType.DMA((3,)), pltpu.SemaphoreType.DMA((3,)),
                        pltpu.SemaphoreType.DMA((7 * n_small,)), pltpu.SemaphoreType.DMA((7 * n_small,)),
                        pltpu.SemaphoreType.DMA((n_small,))])
    return _pcall(
        body, name="dw_in_exchange", grid_spec=grid_spec,
        out_shape=(jax.ShapeDtypeStruct(piece, F32), jax.ShapeDtypeStruct((3,) + piece, BF16),
                   *(jax.ShapeDtypeStruct((N_DEV,) + a.shape, a.dtype) for a in smalls)),
        compiler_params=_params(1),
    )(order, xn, dproj, *smalls)


def _adamw(g, w, m, v):
    m_new = ADAM_B1 * m + (1.0 - ADAM_B1) * g
    v_new = ADAM_B2 * v + (1.0 - ADAM_B2) * (g * g)
    m_hat = m_new / (1.0 - ADAM_B1 ** ADAM_STEP)
    v_hat = v_new / (1.0 - ADAM_B2 ** ADAM_STEP)
    delta = -ADAM_LR * (m_hat / (jnp.sqrt(v_hat) + ADAM_EPS) + ADAM_WD * w)
    return delta, m_new, v_new


def _reduce_adam(recv, w, m, v, name, row_tile):
    rows, cols = w.shape

    def body(r_ref, w_ref, m_ref, v_ref, g_ref, d_ref, nm_ref, nv_ref):
        g = r_ref[0]
        for s in range(1, N_DEV):
            g = g + r_ref[s]
        g_ref[...] = g
        d_ref[...], nm_ref[...], nv_ref[...] = _adamw(g, w_ref[...], m_ref[...], v_ref[...])

    tile = pl.BlockSpec((row_tile, cols), lambda i: (i, 0))
    shp = jax.ShapeDtypeStruct((rows, cols), F32)
    return _pcall(
        body, name=name, grid=(rows // row_tile,),
        out_shape=(shp,) * 4,
        in_specs=[pl.BlockSpec((N_DEV, row_tile, cols), lambda i: (0, i, 0)), tile, tile, tile],
        out_specs=(tile,) * 4,
        compiler_params=_params(1),
    )(recv, w, m, v)


def _reduce_adam_w_in(own, rchip, w, m, v):
    rows, cols = w.shape
    row_tile = 256

    def body(o_ref, r_ref, w_ref, m_ref, v_ref, g_ref, d_ref, nm_ref, nv_ref):
        g = o_ref[...]
        for s in range(3):
            g = g + r_ref[s].astype(F32)
        g_ref[...] = g
        d_ref[...], nm_ref[...], nv_ref[...] = _adamw(g, w_ref[...], m_ref[...], v_ref[...])

    tile = pl.BlockSpec((row_tile, cols), lambda i: (i, 0))
    shp = jax.ShapeDtypeStruct((rows, cols), F32)
    return _pcall(
        body, name="reduce_adam_w_in", grid=(rows // row_tile,),
        out_shape=(shp,) * 4,
        in_specs=[tile, pl.BlockSpec((3, row_tile, cols), lambda i: (0, i, 0)), tile, tile, tile],
        out_specs=(tile,) * 4,
        compiler_params=_params(1),
    )(own, rchip, w, m, v)


_SMALL_LEAVES = ("norm_gain", "final_norm_gain", "b_glu", "ssm_a_re", "ssm_a_im", "ssm_log_dt", "ssm_d", "conv_w",
                 "ssm_c_re", "ssm_c_im", "ssm_b_re", "ssm_b_im")


def _reduce_adam_small(r_pack, r_gc, r_gb, wmv):
    n_leaf = len(_SMALL_LEAVES)

    def body(*refs):
        rp_ref, rgc_ref, rgb_ref = refs[:3]
        w_refs = refs[3:3 + 3 * n_leaf]
        loss_ref = refs[3 + 3 * n_leaf]
        o_refs = refs[4 + 3 * n_leaf:4 + 7 * n_leaf]
        own_conv = refs[-1]

        def total(ref):
            acc = ref[0].astype(F32)
            for s in range(1, N_DEV):
                acc = acc + ref[s].astype(F32)
            return acc

        sp = total(rp_ref)
        sgc = total(rgc_ref)
        sgb = total(rgb_ref)
        loss_ref[...] = sp[ROW_LOSS:ROW_LOSS + SUBLANES, 0:LANES]

        def wide(r):
            return jnp.concatenate([sp[r:r + 1, :], sp[r + 1:r + 2, :]], axis=1)

        s5 = slice(ROW_S5, ROW_S5 + N_GROUPS)
        eye = (lax.broadcasted_iota(jnp.int32, (N_GROUPS, N_GROUPS), 0)
               == lax.broadcasted_iota(jnp.int32, (N_GROUPS, N_GROUPS), 1)).astype(F32)
        d_row = sp[ROW_BGLU_D + 1:ROW_BGLU_D + 2, :]
        me = 4 * lax.axis_index("x") + 2 * lax.axis_index("y") + lax.axis_index("c")
        for k in range(N_DEV):
            @pl.when(me == k)
            def _(k=k):
                own_conv[...] = sp[ROW_CONV:ROW_CONV + SUBLANES, k * CONV_COLS_PER_DEV:(k + 1) * CONV_COLS_PER_DEV]
        grads = {
            "norm_gain": wide(ROW_NORM_GAIN),
            "final_norm_gain": wide(ROW_FINAL_GAIN),
            "b_glu": sp[ROW_BGLU_D:ROW_BGLU_D + 1, :],
            "ssm_a_re": sp[s5, LANE_A_RE:LANE_A_RE + STATE],
            "ssm_a_im": sp[s5, LANE_A_IM:LANE_A_IM + STATE],
            "ssm_log_dt": jnp.sum(sp[s5, LANE_LOG_DT:LANE_LOG_DT + 1] * eye, axis=0, keepdims=True),
            "ssm_d": jnp.concatenate([d_row[:, g * GROUP:(g + 1) * GROUP] for g in range(N_GROUPS)], axis=0),
            "conv_w": own_conv[0:3, :],
            "ssm_c_re": sgc[:, 0:STATE],
            "ssm_c_im": sgc[:, STATE:2 * STATE],
            "ssm_b_re": sgb[:, 0:STATE],
            "ssm_b_im": sgb[:, STATE:2 * STATE],
        }
        for i, name in enumerate(_SMALL_LEAVES):
            g = grads[name]
            w_ref, m_ref, v_ref = w_refs[3 * i:3 * i + 3]
            o_g, o_d, o_m, o_v = o_refs[4 * i:4 * i + 4]
            o_g[...] = g
            o_d[...], o_m[...], o_v[...] = _adamw(g, w_ref[...], m_ref[...], v_ref[...])

    flat_w = [a for name in _SMALL_LEAVES for a in wmv[name]]
    leaf_shapes = [jax.ShapeDtypeStruct(wmv[name][0].shape, F32) for name in _SMALL_LEAVES for _ in range(4)]
    outs = _pcall(
        body, name="reduce_adam_small",
        out_shape=(jax.ShapeDtypeStruct((SUBLANES, LANES), F32), *leaf_shapes),
        scratch_shapes=[pltpu.VMEM((SUBLANES, CONV_COLS_PER_DEV), F32)],
        compiler_params=_params(0),
    )(r_pack, r_gc, r_gb, *flat_w)
    leaves = {name: outs[1 + 4 * i:5 + 4 * i] for i, name in enumerate(_SMALL_LEAVES)}
    return outs[0], leaves


def kernel(x, norm_gain, w_in, ssm_a_re, ssm_a_im, ssm_log_dt, ssm_b_re, ssm_b_im, ssm_c_re, ssm_c_im, ssm_d, w_glu, b_glu, conv_w, w_out, final_norm_gain, loss_target, m_norm_gain, m_w_in, m_ssm_a_re, m_ssm_a_im, m_ssm_log_dt, m_ssm_b_re, m_ssm_b_im, m_ssm_c_re, m_ssm_c_im, m_ssm_d, m_w_glu, m_b_glu, m_conv_w, m_w_out, m_final_norm_gain, v_norm_gain, v_w_in, v_ssm_a_re, v_ssm_a_im, v_ssm_log_dt, v_ssm_b_re, v_ssm_b_im, v_ssm_c_re, v_ssm_c_im, v_ssm_d, v_w_glu, v_b_glu, v_conv_w, v_w_out, v_final_norm_gain):
    n_seq, seq, _ = x.shape
    n = n_seq * seq

    gh_p = lambda b4: jnp.transpose(b4, (0, 1, 3, 2)).reshape(N_GROUPS * GROUP, STATE)
    c2 = lambda a: a.reshape(N_GROUPS * GROUP, STATE)
    b_re2, b_im2 = gh_p(ssm_b_re), gh_p(ssm_b_im)
    d_row = ssm_d[0].reshape(1, SSM_W)

    x2 = x.reshape(n, D_MODEL)
    tgt2 = loss_target.reshape(n, D_MODEL)
    mx, my, mc = lax.axis_index("x"), lax.axis_index("y"), lax.axis_index("c")
    chip_ids = [2 * cx + cy for cx, cy in ((mx, my), (1 - mx, my), (mx, 1 - my), (1 - mx, 1 - my))]
    arrival = chip_ids
    xn, proj, w_in_f, s5 = _in_proj(
        jnp.stack(arrival).astype(jnp.int32), x2, norm_gain, w_in[0].astype(BF16),
        (ssm_a_re[0], ssm_a_im[0], ssm_log_dt, b_re2, b_im2, c2(ssm_c_re), c2(ssm_c_im)))
    a_re_x, a_im_x, log_dt_x, ab_re, ab_im, bb_re_m, bb_im_m, c_re_m, c_imn_m = s5
    u3 = proj.reshape(n_seq, seq, IN_COLS)
    conv_p = jnp.pad(conv_w[0], ((0, SUBLANES - 3), (0, LANES - CONV_COLS_PER_DEV)))
    s_re, s_im, y3, w_out_f, w_glu_f, conv_all = _ssm_fwd(
        u3, bb_re_m, bb_im_m, c_re_m, c_imn_m, d_row, ab_re, ab_im,
        w_out[0].astype(BF16), w_glu[0].astype(BF16), conv_p, n_seq, seq)
    conv8 = jnp.transpose(conv_all[:, :, :CONV_COLS_PER_DEV], (1, 0, 2)).reshape(SUBLANES, CONV_W)
    (dh2, dy, dzs, dbc, dzc, dyc, dw_out, dw_glu, loss_t, dgf, dbg, dcw) = _mix(
        x2, tgt2, y3.reshape(n, SSM_W), proj, final_norm_gain.reshape(1, D_MODEL), b_glu, conv8,
        w_glu_f, w_out_f, seq)

    du3, dc_re_d, dc_im_d, dbb_re_d, dbb_im_d, dab_re, dab_im, dd, r_out, r_glu = _ssm_bwd(
        dy.reshape(n_seq, seq, SSM_W), u3, s_re, s_im, bb_re_m, bb_im_m, c_re_m, c_imn_m, d_row, ab_re, ab_im,
        dw_out.reshape(N_DEV, OUT_ROWS_PER_DEV, D_MODEL), dw_glu.reshape(N_DEV, GLU_ROWS_PER_DEV, SSM_W), n_seq, seq)
    du = du3.reshape(n, SSM_W)
    grad_x2, dproj, dg8 = _in_bwd(x2, dh2, du, dzs, dyc, proj, dbc, dzc, norm_gain, conv8, w_in_f, seq)
    pack, gc, gb = _ssm_disc_bwd_pack(
        a_re_x, a_im_x, log_dt_x, b_re2, b_im2, dab_re.reshape(N_GROUPS, STATE), dab_im.reshape(N_GROUPS, STATE),
        dbb_re_d, dbb_im_d, loss_t, dg8, dgf, dbg, dd, dcw, dc_re_d, dc_im_d)

    order = [chip_ids[3], chip_ids[1], chip_ids[2], chip_ids[0]]
    own_in, rchip_in, r_pack, r_gc, r_gb = _dw_in_exchange(
        jnp.stack(order).astype(jnp.int32), xn, dproj, [pack, gc, gb])

    flat2 = lambda a: a.reshape(a.shape[-2:]) if a.ndim > 2 else a.reshape(1, -1)
    c2 = lambda a: a.reshape(N_GROUPS * GROUP, STATE)
    wmv = dict(norm_gain=(norm_gain, m_norm_gain, v_norm_gain),
               final_norm_gain=tuple(flat2(a) for a in (final_norm_gain, m_final_norm_gain, v_final_norm_gain)),
               b_glu=(b_glu, m_b_glu, v_b_glu),
               ssm_a_re=tuple(flat2(a) for a in (ssm_a_re, m_ssm_a_re, v_ssm_a_re)),
               ssm_a_im=tuple(flat2(a) for a in (ssm_a_im, m_ssm_a_im, v_ssm_a_im)),
               ssm_log_dt=(ssm_log_dt, m_ssm_log_dt, v_ssm_log_dt),
               ssm_d=tuple(flat2(a) for a in (ssm_d, m_ssm_d, v_ssm_d)),
               conv_w=tuple(flat2(a) for a in (conv_w, m_conv_w, v_conv_w)),
               ssm_c_re=tuple(c2(a) for a in (ssm_c_re, m_ssm_c_re, v_ssm_c_re)),
               ssm_c_im=tuple(c2(a) for a in (ssm_c_im, m_ssm_c_im, v_ssm_c_im)),
               ssm_b_re=(b_re2, gh_p(m_ssm_b_re), gh_p(v_ssm_b_re)),
               ssm_b_im=(b_im2, gh_p(m_ssm_b_im), gh_p(v_ssm_b_im)))

    res_in = _reduce_adam_w_in(own_in, rchip_in, w_in[0], m_w_in[0], v_w_in[0])
    res_out = _reduce_adam(r_out, w_out[0], m_w_out[0], v_w_out[0], "reduce_adam_w_out", OUT_ROWS_PER_DEV)
    res_glu = _reduce_adam(r_glu, w_glu[0], m_w_glu[0], v_w_glu[0], "reduce_adam_w_glu", GLU_ROWS_PER_DEV)
    loss8, small = _reduce_adam_small(r_pack, r_gc, r_gb, wmv)
    loss = loss8[0, 0]

    shapes = dict(norm_gain=(1, D_MODEL), ssm_a_re=(1, N_GROUPS, STATE), ssm_a_im=(1, N_GROUPS, STATE),
                  ssm_log_dt=(1, N_GROUPS), ssm_c_re=(1, N_GROUPS, GROUP, STATE), ssm_c_im=(1, N_GROUPS, GROUP, STATE),
                  ssm_d=(1, N_GROUPS, GROUP), b_glu=(1, SSM_W), final_norm_gain=(D_MODEL,),
                  conv_w=(1, 3, CONV_COLS_PER_DEV))
    big = dict(w_in=res_in, w_glu=res_glu, w_out=res_out)

    def leaf(kind, name):
        if name in big:
            return big[name][kind][None]
        if name in ("ssm_b_re", "ssm_b_im"):
            return jnp.transpose(small[name][kind].reshape(1, N_GROUPS, GROUP, STATE), (0, 1, 3, 2))
        return small[name][kind].reshape(shapes[name])

    order = ["norm_gain", "w_in", "ssm_a_re", "ssm_a_im", "ssm_log_dt", "ssm_b_re", "ssm_b_im", "ssm_c_re",
             "ssm_c_im", "ssm_d", "w_glu", "b_glu", "conv_w", "w_out", "final_norm_gain"]
    outs = [loss, grad_x2.reshape(x.shape)]
    for kind in range(4):
        outs += [leaf(kind, name) for name in order]
    return tuple(outs)
```

```python
import functools
import math

import jax
import jax.numpy as jnp
from jax import lax
from jax.experimental import pallas as pl
from jax.experimental.pallas import tpu as pltpu

F32 = jnp.float32
BF16 = jnp.bfloat16

N_DEV = 8
D_MODEL = 1024
SSM_W = 512
CONV_W = 512
N_GROUPS = 32
GROUP = 16
STATE = 64
IN_COLS = 3072
SEG_U, SEG_ZS, SEG_H, SEG_BC, SEG_CC, SEG_ZC = range(6)
COLS_PER_DEV = IN_COLS // N_DEV
N_CHIP = N_DEV // 2
COLS_PER_CHIP = 2 * COLS_PER_DEV
OUT_ROWS_PER_DEV = D_MODEL // N_DEV
GLU_ROWS_PER_DEV = SSM_W // N_DEV
CONV_COLS_PER_DEV = CONV_W // N_DEV
EPS = 1e-6

N_JBLK = 4
JB_CH = SSM_W // N_JBLK
JB_ST = N_GROUPS * STATE // N_JBLK

ADAM_LR = 0.001
ADAM_B1 = 0.9
ADAM_B2 = 0.999
ADAM_EPS = 1e-08
ADAM_WD = 0.01
ADAM_STEP = 10

SUBLANES = 8
LANES = 128
VMEM_LIMIT = 48 * 1024 * 1024
TOK_TILE = 256
IN_TILE = 1024
SCAN_TILE = 1024

MESH = pl.DeviceIdType.MESH
HBM_SPEC = pl.BlockSpec(memory_space=pltpu.HBM)


def _pcall(body, **kw):
    return pl.pallas_call(body, **kw)


def _whole_specs(arrays):
    return [pl.BlockSpec(a.shape, functools.partial(lambda nd, i: (0,) * nd, len(a.shape))) for a in arrays]


def _params(n_grid):
    return pltpu.CompilerParams(dimension_semantics=("arbitrary",) * n_grid,
                                vmem_limit_bytes=VMEM_LIMIT)


def _dot(a, b):
    return jnp.dot(a, b, preferred_element_type=F32)


def _dot_nt(a, b):
    return lax.dot_general(a, b, (((1,), (1,)), ((), ())), preferred_element_type=F32)


def _dot_tn(a, b):
    return lax.dot_general(a, b, (((0,), (0,)), ((), ())), preferred_element_type=F32)


def _sigmoid(z):
    return 1.0 / (1.0 + jnp.exp(-z))


_GELU_C = math.sqrt(2.0 / math.pi)


def _gelu_and_grad(y):
    inner = _GELU_C * (y + 0.044715 * (y * y * y))
    t = jnp.tanh(inner)
    g = 0.5 * y * (1.0 + t)
    dg = 0.5 * (1.0 + t) + 0.5 * y * (1.0 - t * t) * (_GELU_C * (1.0 + 3.0 * 0.044715 * (y * y)))
    return g, dg


def _silu_and_grad(z):
    s = _sigmoid(z)
    return z * s, s * (1.0 + z * (1.0 - s))


def _shift_down(v, halo, k):
    rolled = pltpu.roll(v, k, 0)
    row = lax.broadcasted_iota(jnp.int32, v.shape, 0)
    for r in range(k):
        rolled = jnp.where(row == r, halo[SUBLANES - k + r:SUBLANES - k + r + 1, :], rolled)
    return rolled


def _shift_up(v, halo, k):
    n = v.shape[0]
    rolled = pltpu.roll(v, n - k, 0)
    row = lax.broadcasted_iota(jnp.int32, v.shape, 0)
    for r in range(k):
        rolled = jnp.where(row == n - k + r, halo[r:r + 1, :], rolled)
    return rolled


def _mesh_pos():
    return lax.axis_index("x"), lax.axis_index("y"), lax.axis_index("c")


def _direct_copies(srcs_for, out_refs, send_sems, recv_sems, loc_sems):
    x, y, c = _mesh_pos()
    me_id = 4 * x + 2 * y + c
    n_arr = len(out_refs)
    dsts = [r.at[me_id] for r in out_refs]
    own = srcs_for(me_id)
    mine = [pltpu.make_async_copy(own[a], dsts[a], loc_sems.at[a]) for a in range(n_arr)]
    sends = []
    for k in range(1, N_DEV):
        px, py, pc = x ^ ((k >> 2) & 1), y ^ ((k >> 1) & 1), c ^ (k & 1)
        src = srcs_for(4 * px + 2 * py + pc)
        for a in range(n_arr):
            sends.append(pltpu.make_async_remote_copy(
                src_ref=src[a], dst_ref=dsts[a],
                send_sem=send_sems.at[(k - 1) * n_arr + a], recv_sem=recv_sems.at[(k - 1) * n_arr + a],
                device_id=(px, py, pc), device_id_type=MESH))
    return mine, sends


class _TwoLevelGather:
    def __init__(self, srcs, slots, send_sems, recv_sems, loc_sems):
        self.srcs, self.slots, self.n_arr = srcs, slots, len(srcs)
        self.send_sems, self.recv_sems, self.loc_sems = send_sems, recv_sems, loc_sems
        x, y, c = _mesh_pos()
        self.c = c
        self.me, self.sib = (x, y, c), (x, y, 1 - c)
        self.chips = [(1 - x, y), (x, 1 - y), (1 - x, 1 - y)]

    def _copies(self, k, block, to, from_src=False):
        dev = 4 * block[0] + 2 * block[1] + block[2]
        return [pltpu.make_async_remote_copy(
            src_ref=self.srcs[a] if from_src else self.slots[a](dev), dst_ref=self.slots[a](dev),
            send_sem=self.send_sems.at[k * self.n_arr + a], recv_sem=self.recv_sems.at[k * self.n_arr + a],
            device_id=to, device_id_type=MESH) for a in range(self.n_arr)]

    def _local(self):
        dev = 4 * self.me[0] + 2 * self.me[1] + self.me[2]
        return [pltpu.make_async_copy(self.srcs[a], self.slots[a](dev), self.loc_sems.at[a])
                for a in range(self.n_arr)]

    def start(self, chips=(0, 1, 2)):
        for cp in self._local() + self._copies(0, self.me, self.sib, True):
            cp.start()
        self.start_to(chips)

    def start_to(self, chips):
        for j in chips:
            for cp in self._copies(1 + j, self.me, (*self.chips[j], self.c), True):
                cp.start()

    def wait_own(self):
        for cp in self._local():
            cp.wait()

    def wait_sibling(self):
        for cp in self._copies(0, self.sib, self.me):
            cp.wait_recv()

    def wait_and_pass_on(self, j):
        chip = self.chips[j]
        for cp in self._copies(1 + j, (*chip, self.c), self.me):
            cp.wait_recv()
        for cp in self._copies(4 + j, (*chip, self.c), self.sib):
            cp.start()

    def wait_passed_on(self, j):
        for cp in self._copies(4 + j, (*self.chips[j], 1 - self.c), self.me):
            cp.wait_recv()

    def wait_sends(self):
        for cp in self._copies(0, self.me, self.sib, True):
            cp.wait_send()
        for j, chip in enumerate(self.chips):
            for cp in self._copies(1 + j, self.me, (*chip, self.c), True) + self._copies(4 + j, (*chip, self.c), self.sib):
                cp.wait_send()

    def forward(self):
        for j in range(3):
            self.wait_and_pass_on(j)

    def finish(self):
        self.wait_sibling()
        for j in range(3):
            self.wait_passed_on(j)
        self.wait_sends()
        self.wait_own()


def _disc(a_re, a_im, log_dt, b_re, b_im):
    dt = jnp.exp(log_dt)
    mag = jnp.exp(a_re * dt)
    ab_re = mag * jnp.cos(a_im * dt)
    ab_im = mag * jnp.sin(a_im * dt)
    den = a_re * a_re + a_im * a_im
    p_re = ab_re - 1.0
    p_im = ab_im
    q_re = (p_re * a_re + p_im * a_im) / den
    q_im = (p_im * a_re - p_re * a_im) / den
    bb_re = q_re * b_re - q_im * b_im
    bb_im = q_re * b_im + q_im * b_re
    return ab_re, ab_im, bb_re, bb_im


def _split3(v):
    hi = v.astype(BF16)
    r1 = v - hi.astype(F32)
    mid = r1.astype(BF16)
    lo = (r1 - mid.astype(F32)).astype(BF16)
    return hi, mid, lo


def _select_dot(sel, v):
    return sum(_dot(sel, t) for t in _split3(v))


PACK_ROWS = 72
PACK_W = 512
ROW_FINAL_GAIN, ROW_NORM_GAIN, ROW_BGLU_D, ROW_CONV, ROW_LOSS, ROW_S5 = 0, 8, 16, 24, 32, 40
LANE_A_RE, LANE_A_IM, LANE_LOG_DT = 0, 128, 256


def _ssm_disc_bwd_pack(a_re_x, a_im_x, log_dt_x, b_re, b_im, g_ab_re, g_ab_im, dbb_re_d, dbb_im_d,
                       loss_t, dg8, dgf, dbg, dd, dcw, dc_re_d, dc_im_d):
    rows_gh = N_GROUPS * GROUP

    def body(are, aim, ldt, bre, bim, gabre, gabim, dbbre_ref, dbbim_ref,
             loss_ref, dg8_ref, dgf_ref, dbg_ref, dd_ref, dcw_ref, dcre_ref, dcim_ref,
             p_ref, gc_ref, gb_ref, gbb_re, gbb_im):
        r_g = lax.broadcasted_iota(jnp.int32, (N_GROUPS, rows_gh), 0)
        c_gh = lax.broadcasted_iota(jnp.int32, (N_GROUPS, rows_gh), 1)
        group_sum = (c_gh // GROUP == r_g).astype(BF16)
        r_gh = lax.broadcasted_iota(jnp.int32, (rows_gh, N_GROUPS), 0)
        c_g = lax.broadcasted_iota(jnp.int32, (rows_gh, N_GROUPS), 1)
        first_row = (r_gh == c_g * GROUP).astype(BF16)

        def diag_block(ref, j, gi):
            return ref[j, gi * GROUP:(gi + 1) * GROUP, gi * STATE:(gi + 1) * STATE]

        for j in range(N_JBLK):
            for gi in range(SUBLANES):
                r0 = (j * SUBLANES + gi) * GROUP
                gbb_re[r0:r0 + GROUP, :] = diag_block(dbbre_ref, j, gi)
                gbb_im[r0:r0 + GROUP, :] = diag_block(dbbim_ref, j, gi)
                both = jnp.concatenate([diag_block(dcre_ref, j, gi), -diag_block(dcim_ref, j, gi)], axis=1)
                gc_ref[r0:r0 + GROUP, :] = both.astype(BF16)

        _, vjp = jax.vjp(_disc, are[...], aim[...], ldt[...], bre[...], bim[...])
        d_are, d_aim, d_ldt, d_bre, d_bim = vjp((_select_dot(first_row, gabre[...]), _select_dot(first_row, gabim[...]),
                                                 gbb_re[...], gbb_im[...]))
        gb_ref[...] = jnp.concatenate([d_bre, d_bim], axis=1).astype(BF16)

        p_ref[...] = jnp.zeros_like(p_ref)
        half = D_MODEL // 2
        for r, src in ((ROW_FINAL_GAIN, dgf_ref), (ROW_NORM_GAIN, dg8_ref)):
            p_ref[r:r + 1, :] = src[0:1, 0:half]
            p_ref[r + 1:r + 2, :] = src[0:1, half:D_MODEL]
        p_ref[ROW_BGLU_D:ROW_BGLU_D + 1, :] = dbg_ref[...]
        p_ref[ROW_BGLU_D + 1:ROW_BGLU_D + 2, :] = dd_ref[...]
        p_ref[ROW_CONV:ROW_CONV + SUBLANES, :] = dcw_ref[...]
        p_ref[ROW_LOSS:ROW_LOSS + SUBLANES, 0:LANES] = loss_ref[...]
        s5 = slice(ROW_S5, ROW_S5 + N_GROUPS)
        p_ref[s5, LANE_A_RE:LANE_A_RE + STATE] = _select_dot(group_sum, d_are)
        p_ref[s5, LANE_A_IM:LANE_A_IM + STATE] = _select_dot(group_sum, d_aim)
        p_ref[s5, LANE_LOG_DT:LANE_LOG_DT + LANES] = _select_dot(group_sum, jnp.broadcast_to(d_ldt, (rows_gh, LANES)))

    operands = (a_re_x, a_im_x, log_dt_x, b_re, b_im, g_ab_re, g_ab_im, dbb_re_d, dbb_im_d,
                loss_t, dg8, dgf, dbg, dd, dcw, dc_re_d, dc_im_d)
    out_shape = (jax.ShapeDtypeStruct((PACK_ROWS, PACK_W), F32),
                 jax.ShapeDtypeStruct((rows_gh, 2 * STATE), BF16),
                 jax.ShapeDtypeStruct((rows_gh, 2 * STATE), BF16))
    return _pcall(body, name="ssm_disc_bwd_pack", grid=(1,), out_shape=out_shape,
                  in_specs=_whole_specs(operands), out_specs=tuple(_whole_specs(out_shape)),
                  scratch_shapes=[pltpu.VMEM((rows_gh, STATE), F32), pltpu.VMEM((rows_gh, STATE), F32)],
                  compiler_params=_params(1))(*operands)


def _s5_prepare(are, aim, ldt, bre, bim, cre, cim,
                o_ax_re, o_ax_im, o_ldt_x, o_ab_re, o_ab_im, o_bb_re, o_bb_im, o_c_re, o_c_imn):
    rows_gh = N_GROUPS * GROUP
    rep = (lax.broadcasted_iota(jnp.int32, (rows_gh, N_GROUPS), 0) // GROUP
           == lax.broadcasted_iota(jnp.int32, (rows_gh, N_GROUPS), 1)).astype(BF16)
    eye = (lax.broadcasted_iota(jnp.int32, (N_GROUPS, N_GROUPS), 0)
           == lax.broadcasted_iota(jnp.int32, (N_GROUPS, N_GROUPS), 1)).astype(F32)
    ldt_col = jnp.sum(eye * ldt[...], axis=1, keepdims=True)
    a_re_x = _select_dot(rep, are[...])
    a_im_x = _select_dot(rep, aim[...])
    ldt_x = _select_dot(rep, jnp.broadcast_to(ldt_col, (N_GROUPS, LANES)))[:, 0:1]
    o_ax_re[...] = a_re_x
    o_ax_im[...] = a_im_x
    o_ldt_x[...] = ldt_x
    ab_re, ab_im, bb_re, bb_im = _disc(a_re_x, a_im_x, ldt_x, bre[...], bim[...])
    for j in range(N_JBLK):
        first = [(j * SUBLANES + gi) * GROUP for gi in range(SUBLANES)]
        o_ab_re[j] = jnp.concatenate([ab_re[r:r + 1, :] for r in first], axis=1)
        o_ab_im[j] = jnp.concatenate([ab_im[r:r + 1, :] for r in first], axis=1)
    for o, v in ((o_bb_re, bb_re), (o_bb_im, bb_im), (o_c_re, cre[...]), (o_c_imn, -cim[...])):
        for j in range(N_JBLK):
            for gi in range(SUBLANES):
                r0 = (j * SUBLANES + gi) * GROUP
                parts = [v[r0:r0 + GROUP, :] if k == gi else jnp.zeros((GROUP, STATE), F32) for k in range(SUBLANES)]
                o[j, gi * GROUP:(gi + 1) * GROUP, :] = jnp.concatenate(parts, axis=1).astype(BF16)


def _in_proj(order, x2, g1, w_in_b, s5):
    n = x2.shape[0]
    tm = min(IN_TILE, n)
    n_tiles = n // tm
    n_s5_in = len(s5)
    n_s5_out = 9

    def body(order_ref, x_ref, g_ref, w_ref, *refs):
        s5_in = refs[:n_s5_in]
        xn_ref, proj_ref, wall_ref = refs[n_s5_in:n_s5_in + 3]
        s5_out = refs[n_s5_in + 3:n_s5_in + 3 + n_s5_out]
        xn_scr, wbuf, send_sems, recv_sems, loc_sems, out_sems = refs[n_s5_in + 3 + n_s5_out:]
        k = pl.program_id(0)
        i = pl.program_id(1)

        def slot(dev):
            return wbuf.at[dev // 2, :, pl.ds(pl.multiple_of((dev % 2) * COLS_PER_DEV, LANES), COLS_PER_DEV)]

        gather = _TwoLevelGather([w_ref], [slot], send_sems, recv_sems, loc_sems)

        @pl.when((k == 0) & (i == 0))
        def _():
            gather.start(chips=(0, 1))

        def own_chip():
            gather.wait_own()
            gather.wait_sibling()

        def other_chip(j):
            gather.wait_and_pass_on(j)
            if j == 0:
                gather.start_to((2,))
            gather.wait_passed_on(j)

        arrivals = [own_chip] + [functools.partial(other_chip, j) for j in range(3)]
        for kk, arrived in enumerate(arrivals):
            @pl.when((k == kk) & (i == 0))
            def _(arrived=arrived):
                arrived()

        rows = pl.ds(pl.multiple_of(i * tm, tm), tm)

        @pl.when(k == 0)
        def _():
            x = x_ref[...]
            r = lax.rsqrt(jnp.mean(x * x, axis=-1, keepdims=True) + EPS)
            xn = ((x * r) * g_ref[...]).astype(BF16)
            xn_scr[rows, :] = xn
            xn_ref[...] = xn

        proj_ref[...] = _dot(xn_scr[rows, :], wbuf[order_ref[k]])

        @pl.when((k == 0) & (i == n_tiles - 1))
        def _():
            _s5_prepare(*s5_in, *s5_out)

        @pl.when((k == N_CHIP - 1) & (i == n_tiles - 1))
        def _():
            gather.wait_sends()
            outs = [pltpu.make_async_copy(wbuf.at[q], wall_ref.at[:, q * COLS_PER_CHIP:(q + 1) * COLS_PER_CHIP],
                                          out_sems.at[q]) for q in range(N_CHIP)]
            for cp in outs:
                cp.start()
            for cp in outs:
                cp.wait()

    tile_once = lambda k, i, order: (jnp.where(k == 0, i, n_tiles - 1), 0)
    whole = lambda shape: pl.BlockSpec(shape, lambda k, i, order: (0,) * len(shape))
    rows_gh = N_GROUPS * GROUP
    s5_out_shapes = ([(rows_gh, STATE), F32], [(rows_gh, STATE), F32], [(rows_gh, 1), F32],
                     [(N_JBLK, 1, JB_ST), F32], [(N_JBLK, 1, JB_ST), F32]) + ([(N_JBLK, JB_CH, JB_ST), BF16],) * 4
    grid_spec = pltpu.PrefetchScalarGridSpec(
        num_scalar_prefetch=1, grid=(N_CHIP, n_tiles),
        in_specs=[pl.BlockSpec((tm, D_MODEL), tile_once),
                  whole((1, D_MODEL)),
                  HBM_SPEC,
                  *(whole(a.shape) for a in s5)],
        out_specs=(pl.BlockSpec((tm, D_MODEL), tile_once),
                   pl.BlockSpec((tm, COLS_PER_CHIP), lambda k, i, order: (i, order[k])),
                   HBM_SPEC,
                   *(whole(shape) for shape, _ in s5_out_shapes)),
        scratch_shapes=[pltpu.VMEM((n, D_MODEL), BF16), pltpu.VMEM((N_CHIP, D_MODEL, COLS_PER_CHIP), BF16),
                        pltpu.SemaphoreType.DMA((7,)), pltpu.SemaphoreType.DMA((7,)), pltpu.SemaphoreType.DMA((1,)),
                        pltpu.SemaphoreType.DMA((N_CHIP,))])
    outs = _pcall(
        body, name="in_proj", grid_spec=grid_spec,
        out_shape=(jax.ShapeDtypeStruct((n, D_MODEL), BF16), jax.ShapeDtypeStruct((n, IN_COLS), F32),
                   jax.ShapeDtypeStruct((D_MODEL, IN_COLS), BF16),
                   *(jax.ShapeDtypeStruct(shape, dt) for shape, dt in s5_out_shapes)),
        compiler_params=_params(2),
    )(order, x2, g1, w_in_b, *s5)
    return outs[0], outs[1], outs[2], outs[3:]


def _cmul(p, q):
    return p[0] * q[0] - p[1] * q[1], p[0] * q[1] + p[1] * q[0]


def _scan_tables(ar, ai, width, reverse):
    pows = [(ar, ai)]
    for _ in range(SUBLANES - 1):
        pows.append(_cmul(pows[-1], (ar, ai)))
    row = lax.broadcasted_iota(jnp.int32, (SUBLANES, width), 0)

    def bc(v):
        return jnp.broadcast_to(v, (SUBLANES, width))

    levels = []
    for k in (1, 2, 4):
        keep = (row <= SUBLANES - 1 - k) if reverse else (row >= k)
        levels.append((jnp.where(keep, bc(pows[k - 1][0]), 0.0), jnp.where(keep, bc(pows[k - 1][1]), 0.0)))
    cre = jnp.zeros((SUBLANES, width), F32)
    cim = jnp.zeros((SUBLANES, width), F32)
    for r in range(SUBLANES):
        e = (SUBLANES - r) if reverse else (r + 1)
        cre = jnp.where(row == r, bc(pows[e - 1][0]), cre)
        cim = jnp.where(row == r, bc(pows[e - 1][1]), cim)
    return levels, (cre, cim)


def _load_chunked(src_ref, b, dst_ref, n_rows):
    n_blk = n_rows // SUBLANES
    for i in range(n_blk):
        dst_ref[b, i * SUBLANES:(i + 1) * SUBLANES, :] = src_ref[b, pl.ds(i, SUBLANES, stride=n_blk), :]


def _store_chunked(val, dst_ref, b, n_rows):
    n_blk = n_rows // SUBLANES
    for i in range(n_blk):
        dst_ref[b, pl.ds(i, SUBLANES, stride=n_blk), :] = val[i * SUBLANES:(i + 1) * SUBLANES, :]


def _chunk_scan(re_ref, im_ref, bs, car_ref, ar, ai, n_rows, reverse, on_block=None):
    width = re_ref.shape[2]
    n_blk = n_rows // SUBLANES
    shape = (SUBLANES, width)
    abr = jnp.broadcast_to(ar, shape)
    abi = jnp.broadcast_to(ai, shape)
    order = list(range(n_blk - 1, -1, -1)) if reverse else list(range(n_blk))

    def blk(ref, b, i):
        return ref[b, i * SUBLANES:(i + 1) * SUBLANES, :]

    def step(state, b, i):
        sr, si = state
        return abr * sr - abi * si + blk(re_ref, b, i), abr * si + abi * sr + blk(im_ref, b, i)

    finals = {b: (blk(re_ref, b, order[0]), blk(im_ref, b, order[0])) for b in bs}
    for i in order[1:]:
        for b in bs:
            finals[b] = step(finals[b], b, i)

    mr, mi = ar, ai
    for _ in range(n_blk.bit_length() - 1):
        mr, mi = _cmul((mr, mi), (mr, mi))
    levels, _ = _scan_tables(mr, mi, width, reverse)
    mbr = jnp.broadcast_to(mr, shape)
    mbi = jnp.broadcast_to(mi, shape)
    row = lax.broadcasted_iota(jnp.int32, shape, 0)
    edge_in = SUBLANES - 1 if reverse else 0
    edge_out = 0 if reverse else SUBLANES - 1
    sh1 = SUBLANES - 1 if reverse else 1
    states = {}
    for b in bs:
        fr, fi = finals[b]
        gr = jnp.where(row == edge_in, jnp.broadcast_to(car_ref[b, 0:1, :], shape), pltpu.roll(fr, sh1, 0))
        gi = jnp.where(row == edge_in, jnp.broadcast_to(car_ref[b, 1:2, :], shape), pltpu.roll(fi, sh1, 0))
        for (lr, li), k in zip(levels, (1, 2, 4)):
            sh = (SUBLANES - k) if reverse else k
            sr = pltpu.roll(gr, sh, 0)
            si = pltpu.roll(gi, sh, 0)
            gr, gi = gr + (lr * sr - li * si), gi + (lr * si + li * sr)
        car_ref[b, 0:1, :] = (fr + (mbr * gr - mbi * gi))[edge_out:edge_out + 1, :]
        car_ref[b, 1:2, :] = (fi + (mbr * gi + mbi * gr))[edge_out:edge_out + 1, :]
        states[b] = (gr, gi)

    for i in order:
        for b in bs:
            states[b] = step(states[b], b, i)
            re_ref[b, i * SUBLANES:(i + 1) * SUBLANES, :] = states[b][0]
            im_ref[b, i * SUBLANES:(i + 1) * SUBLANES, :] = states[b][1]
            if on_block is not None:
                on_block(b, i, *states[b])


def _ssm_fwd(u, bb_re, bb_im, c_re_t, c_imn_t, d_row, ab_re, ab_im, w_out_b, w_glu_b, conv_p, n_seq, seq):
    tt = min(SCAN_TILE, seq)
    nt = seq // tt

    def body(u_ref, bbre, bbim, cre, cimn, d_ref, are, aim, wout_ref, wglu_ref, cw_ref,
             sre_ref, sim_ref, y_ref, oout_ref, oglu_ref, ocw_ref,
             up_ref, car_ref, send_sems, recv_sems, loc_sems):
        j = pl.program_id(0)
        t = pl.program_id(1)
        gather = _TwoLevelGather(
            [wout_ref, wglu_ref, cw_ref],
            [lambda dev: oout_ref.at[pl.ds(pl.multiple_of(dev * OUT_ROWS_PER_DEV, OUT_ROWS_PER_DEV), OUT_ROWS_PER_DEV), :],
             lambda dev: oglu_ref.at[pl.ds(pl.multiple_of(dev * GLU_ROWS_PER_DEV, GLU_ROWS_PER_DEV), GLU_ROWS_PER_DEV), :],
             lambda dev: ocw_ref.at[dev]],
            send_sems, recv_sems, loc_sems)

        @pl.when((j == 0) & (t == 0))
        def _():
            gather.start()

        @pl.when((j == N_JBLK // 2) & (t == 0))
        def _():
            gather.forward()

        @pl.when(t == 0)
        def _():
            car_ref[...] = jnp.zeros_like(car_ref)

        bs = list(range(n_seq))
        for b in bs:
            _load_chunked(u_ref, b, up_ref, tt)
        for b in bs:
            ub = up_ref[b].astype(BF16)
            sre_ref[b] = _dot(ub, bbre[0])
            sim_ref[b] = _dot(ub, bbim[0])
            _chunk_scan(sre_ref, sim_ref, [b], car_ref, are[0], aim[0], tt, reverse=False)
        for b in bs:
            yp = (_dot_nt(sre_ref[b].astype(BF16), cre[0]) + _dot_nt(sim_ref[b].astype(BF16), cimn[0])
                  + d_ref[...] * up_ref[b])
            _store_chunked(yp, y_ref, b, tt)

        @pl.when((j == N_JBLK - 1) & (t == nt - 1))
        def _():
            gather.finish()

    tok = lambda j, t: (0, t, j)
    blk3 = lambda j, t: (j, 0, 0)
    row = lambda j, t: (0, j)
    st = jax.ShapeDtypeStruct((n_seq, seq, N_JBLK * JB_ST), F32)
    n_arr = 3
    return _pcall(
        body, name="ssm_fwd", grid=(N_JBLK, nt),
        out_shape=(st, st, jax.ShapeDtypeStruct((n_seq, seq, SSM_W), F32),
                   jax.ShapeDtypeStruct((D_MODEL, D_MODEL), BF16), jax.ShapeDtypeStruct((SSM_W, SSM_W), BF16),
                   jax.ShapeDtypeStruct((N_DEV, SUBLANES, LANES), F32)),
        in_specs=[pl.BlockSpec((n_seq, tt, JB_CH), tok),
                  pl.BlockSpec((1, JB_CH, JB_ST), blk3), pl.BlockSpec((1, JB_CH, JB_ST), blk3),
                  pl.BlockSpec((1, JB_CH, JB_ST), blk3), pl.BlockSpec((1, JB_CH, JB_ST), blk3),
                  pl.BlockSpec((1, JB_CH), row), pl.BlockSpec((1, 1, JB_ST), blk3), pl.BlockSpec((1, 1, JB_ST), blk3),
                  HBM_SPEC, HBM_SPEC, HBM_SPEC],
        out_specs=(pl.BlockSpec((n_seq, tt, JB_ST), tok), pl.BlockSpec((n_seq, tt, JB_ST), tok),
                   pl.BlockSpec((n_seq, tt, JB_CH), tok), HBM_SPEC, HBM_SPEC, HBM_SPEC),
        scratch_shapes=[pltpu.VMEM((n_seq, tt, JB_CH), F32), pltpu.VMEM((n_seq, SUBLANES, JB_ST), F32),
                        pltpu.SemaphoreType.DMA((7 * n_arr,)), pltpu.SemaphoreType.DMA((7 * n_arr,)),
                        pltpu.SemaphoreType.DMA((n_arr,))],
        compiler_params=_params(2),
    )(u, bb_re, bb_im, c_re_t, c_imn_t, d_row, ab_re, ab_im, w_out_b, w_glu_b, conv_p)


def _ssm_bwd(dy, u, s_re, s_im, bb_re, bb_im, c_re_t, c_imn_t, d_row, ab_re, ab_im, g_out, g_glu, n_seq, seq):
    tt = min(SCAN_TILE, seq)
    nt = seq // tt
    rows8 = tt // SUBLANES

    def body(dy_ref, u_ref, sre_ref, sim_ref, pre_ref, pim_ref, bbre, bbim, cre, cimn, d_ref, are, aim,
             gout_ref, gglu_ref,
             du_ref, dcre_ref, dcim_ref, dbbre_ref, dbbim_ref, dare_ref, daim_ref, dd_ref, rout_ref, rglu_ref,
             lre_ref, lim_ref, dyp_ref, up_ref, car_ref, send_sems, recv_sems, loc_sems):
        j = pl.program_id(0)
        tr = pl.program_id(1)

        def exchange():
            return _direct_copies(lambda pid: [gout_ref.at[pid], gglu_ref.at[pid]], [rout_ref, rglu_ref],
                                  send_sems, recv_sems, loc_sems)

        @pl.when((j == 0) & (tr == 0))
        def _():
            mine, sends = exchange()
            for cp in mine + sends:
                cp.start()

        @pl.when(tr == 0)
        def _():
            car_ref[...] = jnp.zeros_like(car_ref)
            for r in (dcre_ref, dcim_ref, dbbre_ref, dbbim_ref, dare_ref, daim_ref, dd_ref):
                r[...] = jnp.zeros_like(r)

        first = tr == nt - 1
        row = lax.broadcasted_iota(jnp.int32, (SUBLANES, JB_ST), 0)
        n_blk = tt // SUBLANES
        bs = list(range(n_seq))
        for b in bs:
            _load_chunked(dy_ref, b, dyp_ref, tt)
            _load_chunked(u_ref, b, up_ref, tt)
        for b in bs:
            dyb = dyp_ref[b].astype(BF16)
            lre_ref[b] = _dot(dyb, cre[0])
            lim_ref[b] = _dot(dyb, cimn[0])
        acc = {b: [jnp.zeros((SUBLANES, JB_ST), F32), jnp.zeros((SUBLANES, JB_ST), F32)] for b in bs}

        def on_block(b, i, lr, li):
            if i > 0:
                spr = sre_ref[b, (i - 1) * SUBLANES:i * SUBLANES, :]
                spi = sim_ref[b, (i - 1) * SUBLANES:i * SUBLANES, :]
            else:
                hr = jnp.where(first, 0.0, pre_ref[b, SUBLANES - 1:SUBLANES, :])
                hi = jnp.where(first, 0.0, pim_ref[b, SUBLANES - 1:SUBLANES, :])
                last_r = sre_ref[b, (n_blk - 1) * SUBLANES:n_blk * SUBLANES, :]
                last_i = sim_ref[b, (n_blk - 1) * SUBLANES:n_blk * SUBLANES, :]
                spr = jnp.where(row == 0, jnp.broadcast_to(hr, row.shape), pltpu.roll(last_r, 1, 0))
                spi = jnp.where(row == 0, jnp.broadcast_to(hi, row.shape), pltpu.roll(last_i, 1, 0))
            acc[b][0] = acc[b][0] + (lr * spr + li * spi)
            acc[b][1] = acc[b][1] + (li * spr - lr * spi)

        _chunk_scan(lre_ref, lim_ref, bs, car_ref, are[0], -aim[0], tt, reverse=True, on_block=on_block)
        for b in bs:
            dare_ref[...] += jnp.sum(acc[b][0], axis=0, keepdims=True)
            daim_ref[...] += jnp.sum(acc[b][1], axis=0, keepdims=True)
            dyp = dyp_ref[b]
            up = up_ref[b]
            dyb = dyp.astype(BF16)
            ub = up.astype(BF16)
            lrb = lre_ref[b].astype(BF16)
            lib = lim_ref[b].astype(BF16)
            dup = d_ref[...] * dyp + _dot_nt(lrb, bbre[0]) + _dot_nt(lib, bbim[0])
            _store_chunked(dup, du_ref, b, tt)
            dbbre_ref[0] += _dot_tn(ub, lrb)
            dbbim_ref[0] += _dot_tn(ub, lib)
            dcre_ref[0] += _dot_tn(dyb, sre_ref[b].astype(BF16))
            dcim_ref[0] += _dot_tn(dyb, sim_ref[b].astype(BF16))
            dd_ref[...] += jnp.sum(dyp * up, axis=0, keepdims=True)

        @pl.when((j == N_JBLK - 1) & (tr == nt - 1))
        def _():
            mine, sends = exchange()
            for cp in sends + mine:
                cp.wait()

    tok = lambda j, t: (0, nt - 1 - t, j)
    halo = lambda j, t: (0, jnp.maximum((nt - 1 - t) * rows8 - 1, 0), j)
    blk3 = lambda j, t: (j, 0, 0)
    row1 = lambda j, t: (0, j)
    acc_shape = jax.ShapeDtypeStruct((N_JBLK, JB_CH, JB_ST), F32)
    return _pcall(
        body, name="ssm_bwd", grid=(N_JBLK, nt),
        out_shape=(jax.ShapeDtypeStruct((n_seq, seq, SSM_W), F32), acc_shape, acc_shape, acc_shape, acc_shape,
                   jax.ShapeDtypeStruct((1, N_JBLK * JB_ST), F32), jax.ShapeDtypeStruct((1, N_JBLK * JB_ST), F32),
                   jax.ShapeDtypeStruct((1, SSM_W), F32),
                   jax.ShapeDtypeStruct((N_DEV,) + g_out.shape[1:], F32),
                   jax.ShapeDtypeStruct((N_DEV,) + g_glu.shape[1:], F32)),
        in_specs=[pl.BlockSpec((n_seq, tt, JB_CH), tok), pl.BlockSpec((n_seq, tt, JB_CH), tok),
                  pl.BlockSpec((n_seq, tt, JB_ST), tok), pl.BlockSpec((n_seq, tt, JB_ST), tok),
                  pl.BlockSpec((n_seq, SUBLANES, JB_ST), halo), pl.BlockSpec((n_seq, SUBLANES, JB_ST), halo),
                  pl.BlockSpec((1, JB_CH, JB_ST), blk3), pl.BlockSpec((1, JB_CH, JB_ST), blk3),
                  pl.BlockSpec((1, JB_CH, JB_ST), blk3), pl.BlockSpec((1, JB_CH, JB_ST), blk3),
                  pl.BlockSpec((1, JB_CH), row1), pl.BlockSpec((1, 1, JB_ST), blk3), pl.BlockSpec((1, 1, JB_ST), blk3),
                  HBM_SPEC, HBM_SPEC],
        out_specs=(pl.BlockSpec((n_seq, tt, JB_CH), tok),
                   pl.BlockSpec((1, JB_CH, JB_ST), blk3), pl.BlockSpec((1, JB_CH, JB_ST), blk3),
                   pl.BlockSpec((1, JB_CH, JB_ST), blk3), pl.BlockSpec((1, JB_CH, JB_ST), blk3),
                   pl.BlockSpec((1, JB_ST), row1), pl.BlockSpec((1, JB_ST), row1), pl.BlockSpec((1, JB_CH), row1),
                   HBM_SPEC, HBM_SPEC),
        scratch_shapes=[pltpu.VMEM((n_seq, tt, JB_ST), F32), pltpu.VMEM((n_seq, tt, JB_ST), F32),
                        pltpu.VMEM((n_seq, tt, JB_CH), F32), pltpu.VMEM((n_seq, tt, JB_CH), F32),
                        pltpu.VMEM((n_seq, SUBLANES, JB_ST), F32),
                        pltpu.SemaphoreType.DMA((7 * 2,)), pltpu.SemaphoreType.DMA((7 * 2,)),
                        pltpu.SemaphoreType.DMA((2,))],
        compiler_params=_params(2),
    )(dy, u, s_re, s_im, s_re, s_im, bb_re, bb_im, c_re_t, c_imn_t, d_row, ab_re, ab_im, g_out, g_glu)


def _mix(x2, tgt2, y, proj, gf, b_glu, conv8, w_glu_f, w_out_f, seq):
    n = x2.shape[0]
    tm = TOK_TILE
    tiles_per_seq = seq // tm
    rows8 = tm // SUBLANES

    def body(x_ref, t_ref, y_ref, zs_ref, h_ref, bc_ref, cc_ref, zc_ref, hp_ref, ccp_ref,
             gf_ref, bg_ref, cw_ref, wg_ref, wo_ref,
             dh2_ref, dy_ref, dzs_ref, dbc_ref, dzc_ref, dyc_ref,
             dwo_ref, dwg_ref, loss_ref, dgf_ref, dbg_ref, dcw_ref):
        i = pl.program_id(0)

        @pl.when(i == 0)
        def _():
            for r in (dwo_ref, dwg_ref, loss_ref, dgf_ref, dbg_ref, dcw_ref):
                r[...] = jnp.zeros_like(r)

        yv = y_ref[...]
        y1, dgelu = _gelu_and_grad(yv)
        y1b = y1.astype(BF16)
        gate = _sigmoid(_dot(y1b, wg_ref[...]) + bg_ref[...])
        y2 = y1 * gate
        szs, dszs = _silu_and_grad(zs_ref[...])
        yssm = y2 * szs
        hv = h_ref[...]
        ccv = cc_ref[...]
        bcv = bc_ref[...]
        v = ccv * hv
        first = (i % tiles_per_seq) == 0
        vhalo = jnp.where(first, 0.0, ccp_ref[...] * hp_ref[...])
        v1 = _shift_down(v, vhalo, 1)
        v2 = _shift_down(v, vhalo, 2)
        w0 = cw_ref[0:1, :]
        w1 = cw_ref[1:2, :]
        w2 = cw_ref[2:3, :]
        yc = w0 * v2 + w1 * v1 + w2 * v
        szc, dszc = _silu_and_grad(zc_ref[...])
        yconv = (bcv * yc) * szc
        ysb = yssm.astype(BF16)
        ycb = yconv.astype(BF16)
        h2 = x_ref[...] + _dot(ysb, wo_ref[0:SSM_W, :]) + _dot(ycb, wo_ref[SSM_W:, :])
        r2 = lax.rsqrt(jnp.mean(h2 * h2, axis=-1, keepdims=True) + EPS)
        hn = h2 * r2
        gfv = gf_ref[...]
        err = hn * gfv - t_ref[...]
        loss_ref[...] += 0.5 * jnp.sum(jnp.mean(err * err, axis=-1, keepdims=True))
        dout = err * (1.0 / D_MODEL)
        dgf_ref[...] += jnp.sum(dout * hn, axis=0, keepdims=True)
        dn = dout * gfv
        dh2 = r2 * (dn - hn * jnp.mean(dn * hn, axis=-1, keepdims=True))
        dh2_ref[...] = dh2
        dh2b = dh2.astype(BF16)
        dwo_ref[0:SSM_W, :] += _dot_tn(ysb, dh2b)
        dwo_ref[SSM_W:, :] += _dot_tn(ycb, dh2b)
        dyssm = _dot_nt(dh2b, wo_ref[0:SSM_W, :])
        dyconv = _dot_nt(dh2b, wo_ref[SSM_W:, :])
        dy2 = dyssm * szs
        dzs_ref[...] = (dyssm * y2 * dszs).astype(BF16)
        dgp = dy2 * y1 * (gate * (1.0 - gate))
        dgpb = dgp.astype(BF16)
        dy1 = dy2 * gate + _dot_nt(dgpb, wg_ref[...])
        dwg_ref[...] += _dot_tn(y1b, dgpb)
        dbg_ref[...] += jnp.sum(dgp, axis=0, keepdims=True)
        dy_ref[...] = dy1 * dgelu
        dbc_ref[...] = (dyconv * yc * szc).astype(BF16)
        dyc = dyconv * bcv * szc
        dyc_ref[...] = dyc
        dzc_ref[...] = (dyconv * bcv * yc * dszc).astype(BF16)
        dcw_ref[0:1, :] += jnp.sum(dyc * v2, axis=0, keepdims=True)
        dcw_ref[1:2, :] += jnp.sum(dyc * v1, axis=0, keepdims=True)
        dcw_ref[2:3, :] += jnp.sum(dyc * v, axis=0, keepdims=True)

    tile_d = pl.BlockSpec((tm, D_MODEL), lambda i: (i, 0))
    tile_s = pl.BlockSpec((tm, SSM_W), lambda i: (i, 0))
    seg_of = lambda c: pl.BlockSpec((tm, SSM_W), lambda i: (i, c))
    halo_of = lambda c: pl.BlockSpec((SUBLANES, SSM_W), lambda i: (jnp.maximum(i * rows8 - 1, 0), c))
    const = lambda shape: pl.BlockSpec(shape, lambda i: (0,) * len(shape))
    seg = jax.ShapeDtypeStruct((n, SSM_W), F32)
    seg_b = jax.ShapeDtypeStruct((n, SSM_W), BF16)
    return _pcall(
        body, name="mix", grid=(n // tm,),
        out_shape=(jax.ShapeDtypeStruct((n, D_MODEL), F32), seg, seg_b, seg_b, seg_b, seg,
                   jax.ShapeDtypeStruct((D_MODEL, D_MODEL), F32), jax.ShapeDtypeStruct((SSM_W, SSM_W), F32),
                   jax.ShapeDtypeStruct((SUBLANES, LANES), F32), jax.ShapeDtypeStruct((1, D_MODEL), F32),
                   jax.ShapeDtypeStruct((1, SSM_W), F32), jax.ShapeDtypeStruct((SUBLANES, CONV_W), F32)),
        in_specs=[tile_d, tile_d, tile_s, seg_of(SEG_ZS), seg_of(SEG_H), seg_of(SEG_BC), seg_of(SEG_CC), seg_of(SEG_ZC),
                  halo_of(SEG_H), halo_of(SEG_CC),
                  const((1, D_MODEL)), const((1, SSM_W)), const((SUBLANES, CONV_W)),
                  const((SSM_W, SSM_W)), const((D_MODEL, D_MODEL))],
        out_specs=(tile_d, tile_s, tile_s, tile_s, tile_s, tile_s,
                   const((D_MODEL, D_MODEL)), const((SSM_W, SSM_W)), const((SUBLANES, LANES)),
                   const((1, D_MODEL)), const((1, SSM_W)), const((SUBLANES, CONV_W))),
        compiler_params=_params(1),
    )(x2, tgt2, y, proj, proj, proj, proj, proj, proj, proj, gf, b_glu, conv8, w_glu_f, w_out_f)


def _in_bwd(x2, dh2, du, dzs, dyc, proj, dbc, dzc, g1, conv8, w_full, seq):
    n = x2.shape[0]
    tm = TOK_TILE
    n_tiles = n // tm
    tiles_per_seq = seq // tm
    rows8 = tm // SUBLANES
    n_blk8 = n // SUBLANES

    def body(x_ref, dh2_ref, du_ref, dzs_ref, dyc_ref, dycn_ref, h_ref, cc_ref, dbc_ref, dzc_ref,
             g_ref, cw_ref, w_ref, gx_ref, dp_ref, dg_ref):
        i = pl.program_id(0)

        @pl.when(i == 0)
        def _():
            dg_ref[...] = jnp.zeros_like(dg_ref)

        dyc = dyc_ref[...]
        last = (i % tiles_per_seq) == tiles_per_seq - 1
        nhalo = jnp.where(last, 0.0, dycn_ref[...])
        dv = (cw_ref[2:3, :] * dyc + cw_ref[1:2, :] * _shift_up(dyc, nhalo, 1)
              + cw_ref[0:1, :] * _shift_up(dyc, nhalo, 2))
        parts = (du_ref[...], dzs_ref[...], dv * cc_ref[...], dbc_ref[...], dv * h_ref[...], dzc_ref[...])
        dxn = jnp.zeros((tm, D_MODEL), F32)
        for k, p in enumerate(parts):
            pb = p.astype(BF16)
            dp_ref[:, k * SSM_W:(k + 1) * SSM_W] = pb
            dxn = dxn + _dot_nt(pb, w_ref[:, k * SSM_W:(k + 1) * SSM_W])
        x = x_ref[...]
        r = lax.rsqrt(jnp.mean(x * x, axis=-1, keepdims=True) + EPS)
        xh = x * r
        dg_ref[...] += jnp.sum(dxn * xh, axis=0, keepdims=True)
        dn = dxn * g_ref[...]
        gx_ref[...] = dh2_ref[...] + r * (dn - xh * jnp.mean(dn * xh, axis=-1, keepdims=True))

    tile_d = pl.BlockSpec((tm, D_MODEL), lambda i: (i, 0))
    tile_s = pl.BlockSpec((tm, SSM_W), lambda i: (i, 0))
    seg_of = lambda c: pl.BlockSpec((tm, SSM_W), lambda i: (i, c))
    nhalo = pl.BlockSpec((SUBLANES, SSM_W), lambda i: (jnp.minimum((i + 1) * rows8, n_blk8 - 1), 0))
    const = lambda shape: pl.BlockSpec(shape, lambda i: (0,) * len(shape))
    return _pcall(
        body, name="in_bwd", grid=(n_tiles,),
        out_shape=(jax.ShapeDtypeStruct((n, D_MODEL), F32), jax.ShapeDtypeStruct((n, IN_COLS), BF16),
                   jax.ShapeDtypeStruct((SUBLANES, D_MODEL), F32)),
        in_specs=[tile_d, tile_d, tile_s, tile_s, tile_s, nhalo, seg_of(SEG_H), seg_of(SEG_CC), tile_s, tile_s,
                  const((1, D_MODEL)), const((SUBLANES, CONV_W)), const((D_MODEL, IN_COLS))],
        out_specs=(tile_d, pl.BlockSpec((tm, IN_COLS), lambda i: (i, 0)), const((SUBLANES, D_MODEL))),
        compiler_params=_params(1),
    )(x2, dh2, du, dzs, dyc, dyc, proj, proj, dbc, dzc, g1, conv8, w_full)


def _dw_in_exchange(order, xn, dproj, smalls):
    n = xn.shape[0]
    tk = 512
    nk = n // tk
    piece = (D_MODEL, COLS_PER_DEV)
    n_small = len(smalls)

    def body(order_ref, xn_hbm, dp_ref, *refs):
        del order_ref
        sm_refs = refs[:n_small]
        own_ref, rchip_ref = refs[n_small:n_small + 2]
        rsm_refs = refs[n_small + 2:2 * n_small + 2]
        (xn_ref, acc, stage, sbuf, xn_sems, give_send, give_recv, keep_send, keep_recv,
         sm_send, sm_recv, sm_loc) = refs[2 * n_small + 2:]
        s = pl.program_id(0)

        def xn_copy(kk):
            rows = pl.ds(pl.multiple_of(kk * tk, tk), tk)
            return pltpu.make_async_copy(xn_hbm.at[rows, :], xn_ref.at[rows, :], xn_sems.at[kk])

        @pl.when(s == 0)
        def _():
            for kk in range(nk):
                xn_copy(kk).start()
            xn_copy(0).wait()

        x, y, c = _mesh_pos()
        sib = (x, y, 1 - c)
        chips = [(1 - x, 1 - y), (1 - x, y), (x, 1 - y)]
        gather = _TwoLevelGather(list(sm_refs), [functools.partial(lambda r, dev: r.at[dev], r) for r in rsm_refs],
                                 sm_send, sm_recv, sm_loc)

        def half(i, core):
            return acc.at[i % 2, :, pl.ds(pl.multiple_of(core * COLS_PER_DEV, LANES), COLS_PER_DEV)]

        def give(i):
            return pltpu.make_async_remote_copy(src_ref=half(i, 1 - c), dst_ref=stage.at[i], send_sem=give_send.at[i],
                                                recv_sem=give_recv.at[i], device_id=sib, device_id_type=MESH)

        def keep(i):
            return pltpu.make_async_remote_copy(src_ref=sbuf.at[i], dst_ref=rchip_ref.at[i], send_sem=keep_send.at[i],
                                                recv_sem=keep_recv.at[i], device_id=(*chips[i], c), device_id_type=MESH)

        def chip_sum(i):
            give(i).wait_recv()
            mine = [acc[i % 2, :, cc * COLS_PER_DEV:(cc + 1) * COLS_PER_DEV] for cc in range(2)]
            return jnp.where(c == 0, mine[0], mine[1]) + stage[i]

        @pl.when(s == 0)
        def _():
            gather.start()

        @pl.when(s == N_CHIP // 2)
        def _():
            gather.forward()

        for k in range(2, N_CHIP):
            @pl.when(s == k)
            def _(k=k):
                give(k - 2).wait_send()

        slot = s % 2
        acc[slot] = _dot_tn(xn_ref[pl.ds(0, tk), :], dp_ref[pl.ds(0, tk), :])

        def kstep(kk, carry):
            @pl.when(s == 0)
            def _():
                xn_copy(kk).wait()

            off = pl.multiple_of(kk * tk, tk)
            acc[slot] += _dot_tn(xn_ref[pl.ds(off, tk), :], dp_ref[pl.ds(off, tk), :])
            return carry

        n_first = min(nk, 3)
        lax.fori_loop(1, n_first, kstep, 0)
        for k in range(1, N_CHIP):
            @pl.when(s == k)
            def _(k=k):
                sbuf[k - 1] = chip_sum(k - 1).astype(BF16)
                keep(k - 1).start()

        lax.fori_loop(n_first, nk, kstep, 0)

        for k in range(N_CHIP):
            @pl.when(s == k)
            def _(k=k):
                give(k).start()

        @pl.when(s == N_CHIP - 1)
        def _():
            own_ref[...] = chip_sum(N_CHIP - 1)
            give(N_CHIP - 2).wait_send()
            give(N_CHIP - 1).wait_send()
            for i in range(3):
                keep(i).wait()
            gather.finish()

    grid_spec = pltpu.PrefetchScalarGridSpec(
        num_scalar_prefetch=1, grid=(N_CHIP,),
        in_specs=[HBM_SPEC,
                  pl.BlockSpec((n, COLS_PER_CHIP), lambda s, order: (0, order[s])),
                  *([HBM_SPEC] * n_small)],
        out_specs=(pl.BlockSpec(piece, lambda s, order: (0, 0)), HBM_SPEC, *([HBM_SPEC] * n_small)),
        scratch_shapes=[pltpu.VMEM((n, D_MODEL), BF16),
                        pltpu.VMEM((2, D_MODEL, COLS_PER_CHIP), F32), pltpu.VMEM((4,) + piece, F32),
                        pltpu.VMEM((3,) + piece, BF16),
                        pltpu.SemaphoreType.DMA((nk,)),
                        pltpu.SemaphoreType.DMA((4,)), pltpu.SemaphoreType.DMA((4,)),
                        pltpu.SemaphoreType.DMA((3,)), pltpu.SemaphoreType.DMA((3,)),
                        pltpu.SemaphoreType.DMA((7 * n_small,)), pltpu.SemaphoreType.DMA((7 * n_small,)),
                        pltpu.SemaphoreType.DMA((n_small,))])
    return _pcall(
        body, name="dw_in_exchange", grid_spec=grid_spec,
        out_shape=(jax.ShapeDtypeStruct(piece, F32), jax.ShapeDtypeStruct((3,) + piece, BF16),
                   *(jax.ShapeDtypeStruct((N_DEV,) + a.shape, a.dtype) for a in smalls)),
        compiler_params=_params(1),
    )(order, xn, dproj, *smalls)


def _adamw(g, w, m, v):
    m_new = ADAM_B1 * m + (1.0 - ADAM_B1) * g
    v_new = ADAM_B2 * v + (1.0 - ADAM_B2) * (g * g)
    m_hat = m_new / (1.0 - ADAM_B1 ** ADAM_STEP)
    v_hat = v_new / (1.0 - ADAM_B2 ** ADAM_STEP)
    delta = -ADAM_LR * (m_hat / (jnp.sqrt(v_hat) + ADAM_EPS) + ADAM_WD * w)
    return delta, m_new, v_new


def _reduce_adam(recv, w, m, v, name, row_tile):
    rows, cols = w.shape

    def body(r_ref, w_ref, m_ref, v_ref, g_ref, d_ref, nm_ref, nv_ref):
        g = r_ref[0]
        for s in range(1, N_DEV):
            g = g + r_ref[s]
        g_ref[...] = g
        d_ref[...], nm_ref[...], nv_ref[...] = _adamw(g, w_ref[...], m_ref[...], v_ref[...])

    tile = pl.BlockSpec((row_tile, cols), lambda i: (i, 0))
    shp = jax.ShapeDtypeStruct((rows, cols), F32)
    return _pcall(
        body, name=name, grid=(rows // row_tile,),
        out_shape=(shp,) * 4,
        in_specs=[pl.BlockSpec((N_DEV, row_tile, cols), lambda i: (0, i, 0)), tile, tile, tile],
        out_specs=(tile,) * 4,
        compiler_params=_params(1),
    )(recv, w, m, v)


def _reduce_adam_w_in(own, rchip, w, m, v):
    rows, cols = w.shape
    row_tile = 256

    def body(o_ref, r_ref, w_ref, m_ref, v_ref, g_ref, d_ref, nm_ref, nv_ref):
        g = o_ref[...]
        for s in range(3):
            g = g + r_ref[s].astype(F32)
        g_ref[...] = g
        d_ref[...], nm_ref[...], nv_ref[...] = _adamw(g, w_ref[...], m_ref[...], v_ref[...])

    tile = pl.BlockSpec((row_tile, cols), lambda i: (i, 0))
    shp = jax.ShapeDtypeStruct((rows, cols), F32)
    return _pcall(
        body, name="reduce_adam_w_in", grid=(rows // row_tile,),
        out_shape=(shp,) * 4,
        in_specs=[tile, pl.BlockSpec((3, row_tile, cols), lambda i: (0, i, 0)), tile, tile, tile],
        out_specs=(tile,) * 4,
        compiler_params=_params(1),
    )(own, rchip, w, m, v)


_SMALL_LEAVES = ("norm_gain", "final_norm_gain", "b_glu", "ssm_a_re", "ssm_a_im", "ssm_log_dt", "ssm_d", "conv_w",
                 "ssm_c_re", "ssm_c_im", "ssm_b_re", "ssm_b_im")


def _reduce_adam_small(r_pack, r_gc, r_gb, wmv):
    n_leaf = len(_SMALL_LEAVES)

    def body(*refs):
        rp_ref, rgc_ref, rgb_ref = refs[:3]
        w_refs = refs[3:3 + 3 * n_leaf]
        loss_ref = refs[3 + 3 * n_leaf]
        o_refs = refs[4 + 3 * n_leaf:4 + 7 * n_leaf]
        own_conv = refs[-1]

        def total(ref):
            acc = ref[0].astype(F32)
            for s in range(1, N_DEV):
                acc = acc + ref[s].astype(F32)
            return acc

        sp = total(rp_ref)
        sgc = total(rgc_ref)
        sgb = total(rgb_ref)
        loss_ref[...] = sp[ROW_LOSS:ROW_LOSS + SUBLANES, 0:LANES]

        def wide(r):
            return jnp.concatenate([sp[r:r + 1, :], sp[r + 1:r + 2, :]], axis=1)

        s5 = slice(ROW_S5, ROW_S5 + N_GROUPS)
        eye = (lax.broadcasted_iota(jnp.int32, (N_GROUPS, N_GROUPS), 0)
               == lax.broadcasted_iota(jnp.int32, (N_GROUPS, N_GROUPS), 1)).astype(F32)
        d_row = sp[ROW_BGLU_D + 1:ROW_BGLU_D + 2, :]
        me = 4 * lax.axis_index("x") + 2 * lax.axis_index("y") + lax.axis_index("c")
        for k in range(N_DEV):
            @pl.when(me == k)
            def _(k=k):
                own_conv[...] = sp[ROW_CONV:ROW_CONV + SUBLANES, k * CONV_COLS_PER_DEV:(k + 1) * CONV_COLS_PER_DEV]
        grads = {
            "norm_gain": wide(ROW_NORM_GAIN),
            "final_norm_gain": wide(ROW_FINAL_GAIN),
            "b_glu": sp[ROW_BGLU_D:ROW_BGLU_D + 1, :],
            "ssm_a_re": sp[s5, LANE_A_RE:LANE_A_RE + STATE],
            "ssm_a_im": sp[s5, LANE_A_IM:LANE_A_IM + STATE],
            "ssm_log_dt": jnp.sum(sp[s5, LANE_LOG_DT:LANE_LOG_DT + 1] * eye, axis=0, keepdims=True),
            "ssm_d": jnp.concatenate([d_row[:, g * GROUP:(g + 1) * GROUP] for g in range(N_GROUPS)], axis=0),
            "conv_w": own_conv[0:3, :],
            "ssm_c_re": sgc[:, 0:STATE],
            "ssm_c_im": sgc[:, STATE:2 * STATE],
            "ssm_b_re": sgb[:, 0:STATE],
            "ssm_b_im": sgb[:, STATE:2 * STATE],
        }
        for i, name in enumerate(_SMALL_LEAVES):
            g = grads[name]
            w_ref, m_ref, v_ref = w_refs[3 * i:3 * i + 3]
            o_g, o_d, o_m, o_v = o_refs[4 * i:4 * i + 4]
            o_g[...] = g
            o_d[...], o_m[...], o_v[...] = _adamw(g, w_ref[...], m_ref[...], v_ref[...])

    flat_w = [a for name in _SMALL_LEAVES for a in wmv[name]]
    leaf_shapes = [jax.ShapeDtypeStruct(wmv[name][0].shape, F32) for name in _SMALL_LEAVES for _ in range(4)]
    operands = (r_pack, r_gc, r_gb, *flat_w)
    out_shape = (jax.ShapeDtypeStruct((SUBLANES, LANES), F32), *leaf_shapes)
    outs = _pcall(
        body, name="reduce_adam_small", grid=(1,), out_shape=out_shape,
        in_specs=_whole_specs(operands), out_specs=tuple(_whole_specs(out_shape)),
        scratch_shapes=[pltpu.VMEM((SUBLANES, CONV_COLS_PER_DEV), F32)],
        compiler_params=_params(1),
    )(*operands)
    leaves = {name: outs[1 + 4 * i:5 + 4 * i] for i, name in enumerate(_SMALL_LEAVES)}
    return outs[0], leaves


def kernel(x, norm_gain, w_in, ssm_a_re, ssm_a_im, ssm_log_dt, ssm_b_re, ssm_b_im, ssm_c_re, ssm_c_im, ssm_d, w_glu, b_glu, conv_w, w_out, final_norm_gain, loss_target, m_norm_gain, m_w_in, m_ssm_a_re, m_ssm_a_im, m_ssm_log_dt, m_ssm_b_re, m_ssm_b_im, m_ssm_c_re, m_ssm_c_im, m_ssm_d, m_w_glu, m_b_glu, m_conv_w, m_w_out, m_final_norm_gain, v_norm_gain, v_w_in, v_ssm_a_re, v_ssm_a_im, v_ssm_log_dt, v_ssm_b_re, v_ssm_b_im, v_ssm_c_re, v_ssm_c_im, v_ssm_d, v_w_glu, v_b_glu, v_conv_w, v_w_out, v_final_norm_gain):
    n_seq, seq, _ = x.shape
    n = n_seq * seq

    gh_p = lambda b4: jnp.transpose(b4, (0, 1, 3, 2)).reshape(N_GROUPS * GROUP, STATE)
    c2 = lambda a: a.reshape(N_GROUPS * GROUP, STATE)
    b_re2, b_im2 = gh_p(ssm_b_re), gh_p(ssm_b_im)
    d_row = ssm_d[0].reshape(1, SSM_W)

    x2 = x.reshape(n, D_MODEL)
    tgt2 = loss_target.reshape(n, D_MODEL)
    mx, my, mc = lax.axis_index("x"), lax.axis_index("y"), lax.axis_index("c")
    chip_ids = [2 * cx + cy for cx, cy in ((mx, my), (1 - mx, my), (mx, 1 - my), (1 - mx, 1 - my))]
    arrival = chip_ids
    xn, proj, w_in_f, s5 = _in_proj(
        jnp.stack(arrival).astype(jnp.int32), x2, norm_gain, w_in[0].astype(BF16),
        (ssm_a_re[0], ssm_a_im[0], ssm_log_dt, b_re2, b_im2, c2(ssm_c_re), c2(ssm_c_im)))
    a_re_x, a_im_x, log_dt_x, ab_re, ab_im, bb_re_m, bb_im_m, c_re_m, c_imn_m = s5
    u3 = proj.reshape(n_seq, seq, IN_COLS)
    conv_p = jnp.pad(conv_w[0], ((0, SUBLANES - 3), (0, LANES - CONV_COLS_PER_DEV)))
    s_re, s_im, y3, w_out_f, w_glu_f, conv_all = _ssm_fwd(
        u3, bb_re_m, bb_im_m, c_re_m, c_imn_m, d_row, ab_re, ab_im,
        w_out[0].astype(BF16), w_glu[0].astype(BF16), conv_p, n_seq, seq)
    conv8 = jnp.transpose(conv_all[:, :, :CONV_COLS_PER_DEV], (1, 0, 2)).reshape(SUBLANES, CONV_W)
    (dh2, dy, dzs, dbc, dzc, dyc, dw_out, dw_glu, loss_t, dgf, dbg, dcw) = _mix(
        x2, tgt2, y3.reshape(n, SSM_W), proj, final_norm_gain.reshape(1, D_MODEL), b_glu, conv8,
        w_glu_f, w_out_f, seq)

    du3, dc_re_d, dc_im_d, dbb_re_d, dbb_im_d, dab_re, dab_im, dd, r_out, r_glu = _ssm_bwd(
        dy.reshape(n_seq, seq, SSM_W), u3, s_re, s_im, bb_re_m, bb_im_m, c_re_m, c_imn_m, d_row, ab_re, ab_im,
        dw_out.reshape(N_DEV, OUT_ROWS_PER_DEV, D_MODEL), dw_glu.reshape(N_DEV, GLU_ROWS_PER_DEV, SSM_W), n_seq, seq)
    du = du3.reshape(n, SSM_W)
    grad_x2, dproj, dg8 = _in_bwd(x2, dh2, du, dzs, dyc, proj, dbc, dzc, norm_gain, conv8, w_in_f, seq)
    pack, gc, gb = _ssm_disc_bwd_pack(
        a_re_x, a_im_x, log_dt_x, b_re2, b_im2, dab_re.reshape(N_GROUPS, STATE), dab_im.reshape(N_GROUPS, STATE),
        dbb_re_d, dbb_im_d, loss_t, dg8, dgf, dbg, dd, dcw, dc_re_d, dc_im_d)

    order = [chip_ids[3], chip_ids[1], chip_ids[2], chip_ids[0]]
    own_in, rchip_in, r_pack, r_gc, r_gb = _dw_in_exchange(
        jnp.stack(order).astype(jnp.int32), xn, dproj, [pack, gc, gb])

    flat2 = lambda a: a.reshape(a.shape[-2:]) if a.ndim > 2 else a.reshape(1, -1)
    c2 = lambda a: a.reshape(N_GROUPS * GROUP, STATE)
    wmv = dict(norm_gain=(norm_gain, m_norm_gain, v_norm_gain),
               final_norm_gain=tuple(flat2(a) for a in (final_norm_gain, m_final_norm_gain, v_final_norm_gain)),
               b_glu=(b_glu, m_b_glu, v_b_glu),
               ssm_a_re=tuple(flat2(a) for a in (ssm_a_re, m_ssm_a_re, v_ssm_a_re)),
               ssm_a_im=tuple(flat2(a) for a in (ssm_a_im, m_ssm_a_im, v_ssm_a_im)),
               ssm_log_dt=(ssm_log_dt, m_ssm_log_dt, v_ssm_log_dt),
               ssm_d=tuple(flat2(a) for a in (ssm_d, m_ssm_d, v_ssm_d)),
               conv_w=tuple(flat2(a) for a in (conv_w, m_conv_w, v_conv_w)),
               ssm_c_re=tuple(c2(a) for a in (ssm_c_re, m_ssm_c_re, v_ssm_c_re)),
               ssm_c_im=tuple(c2(a) for a in (ssm_c_im, m_ssm_c_im, v_ssm_c_im)),
               ssm_b_re=(b_re2, gh_p(m_ssm_b_re), gh_p(v_ssm_b_re)),
               ssm_b_im=(b_im2, gh_p(m_ssm_b_im), gh_p(v_ssm_b_im)))

    res_in = _reduce_adam_w_in(own_in, rchip_in, w_in[0], m_w_in[0], v_w_in[0])
    res_out = _reduce_adam(r_out, w_out[0], m_w_out[0], v_w_out[0], "reduce_adam_w_out", OUT_ROWS_PER_DEV)
    res_glu = _reduce_adam(r_glu, w_glu[0], m_w_glu[0], v_w_glu[0], "reduce_adam_w_glu", GLU_ROWS_PER_DEV)
    loss8, small = _reduce_adam_small(r_pack, r_gc, r_gb, wmv)
    loss = loss8[0, 0]

    shapes = dict(norm_gain=(1, D_MODEL), ssm_a_re=(1, N_GROUPS, STATE), ssm_a_im=(1, N_GROUPS, STATE),
                  ssm_log_dt=(1, N_GROUPS), ssm_c_re=(1, N_GROUPS, GROUP, STATE), ssm_c_im=(1, N_GROUPS, GROUP, STATE),
                  ssm_d=(1, N_GROUPS, GROUP), b_glu=(1, SSM_W), final_norm_gain=(D_MODEL,),
                  conv_w=(1, 3, CONV_COLS_PER_DEV))
    big = dict(w_in=res_in, w_glu=res_glu, w_out=res_out)

    def leaf(kind, name):
        if name in big:
            return big[name][kind][None]
        if name in ("ssm_b_re", "ssm_b_im"):
            return jnp.transpose(small[name][kind].reshape(1, N_GROUPS, GROUP, STATE), (0, 1, 3, 2))
        return small[name][kind].reshape(shapes[name])

    order = ["norm_gain", "w_in", "ssm_a_re", "ssm_a_im", "ssm_log_dt", "ssm_b_re", "ssm_b_im", "ssm_c_re",
             "ssm_c_im", "ssm_d", "w_glu", "b_glu", "conv_w", "w_out", "final_norm_gain"]
    outs = [loss, grad_x2.reshape(x.shape)]
    for kind in range(4):
        outs += [leaf(kind, name) for name in order]
    return tuple(outs)
```

```python
import functools
import math

import jax
import jax.numpy as jnp
from jax import lax
from jax.experimental import pallas as pl
from jax.experimental.pallas import tpu as pltpu

F32 = jnp.float32
BF16 = jnp.bfloat16

N_DEV = 8
D_MODEL = 1024
SSM_W = 512
CONV_W = 512
N_GROUPS = 32
GROUP = 16
STATE = 64
IN_COLS = 3072
SEG_U, SEG_ZS, SEG_H, SEG_BC, SEG_CC, SEG_ZC = range(6)
COLS_PER_DEV = IN_COLS // N_DEV
N_CHIP = N_DEV // 2
COLS_PER_CHIP = 2 * COLS_PER_DEV
OUT_ROWS_PER_DEV = D_MODEL // N_DEV
GLU_ROWS_PER_DEV = SSM_W // N_DEV
CONV_COLS_PER_DEV = CONV_W // N_DEV
EPS = 1e-6

N_JBLK = 4
JB_CH = SSM_W // N_JBLK
JB_ST = N_GROUPS * STATE // N_JBLK

ADAM_LR = 0.001
ADAM_B1 = 0.9
ADAM_B2 = 0.999
ADAM_EPS = 1e-08
ADAM_WD = 0.01
ADAM_STEP = 10

SUBLANES = 8
LANES = 128
VMEM_LIMIT = 48 * 1024 * 1024
TOK_TILE = 256
IN_TILE = 1024
SCAN_TILE = 1024

MESH = pl.DeviceIdType.MESH
HBM_SPEC = pl.BlockSpec(memory_space=pltpu.HBM)


def _build(body, **kw):
    return pl.pallas_call(body, **kw)


def _pcall(body, **kw):
    def call(*operands):
        pinned = [a if jnp.issubdtype(a.dtype, jnp.integer) else pltpu.with_memory_space_constraint(a, pltpu.HBM)
                  for a in operands]
        return _build(body, **kw)(*pinned)
    return call


def _whole_specs(arrays):
    return [pl.BlockSpec(a.shape, functools.partial(lambda nd, i: (0,) * nd, len(a.shape))) for a in arrays]


def _out(shape, dtype):
    return pltpu.HBM(tuple(shape), dtype)


def _params(n_grid):
    return pltpu.CompilerParams(dimension_semantics=("arbitrary",) * n_grid,
                                vmem_limit_bytes=VMEM_LIMIT)


def _dot(a, b):
    return jnp.dot(a, b, preferred_element_type=F32)


def _dot_nt(a, b):
    return lax.dot_general(a, b, (((1,), (1,)), ((), ())), preferred_element_type=F32)


def _dot_tn(a, b):
    return lax.dot_general(a, b, (((0,), (0,)), ((), ())), preferred_element_type=F32)


def _sigmoid(z):
    return 1.0 / (1.0 + jnp.exp(-z))


_GELU_C = math.sqrt(2.0 / math.pi)


def _gelu_and_grad(y):
    inner = _GELU_C * (y + 0.044715 * (y * y * y))
    t = jnp.tanh(inner)
    g = 0.5 * y * (1.0 + t)
    dg = 0.5 * (1.0 + t) + 0.5 * y * (1.0 - t * t) * (_GELU_C * (1.0 + 3.0 * 0.044715 * (y * y)))
    return g, dg


def _silu_and_grad(z):
    s = _sigmoid(z)
    return z * s, s * (1.0 + z * (1.0 - s))


def _shift_down(v, halo, k):
    rolled = pltpu.roll(v, k, 0)
    row = lax.broadcasted_iota(jnp.int32, v.shape, 0)
    for r in range(k):
        rolled = jnp.where(row == r, halo[SUBLANES - k + r:SUBLANES - k + r + 1, :], rolled)
    return rolled


def _shift_up(v, halo, k):
    n = v.shape[0]
    rolled = pltpu.roll(v, n - k, 0)
    row = lax.broadcasted_iota(jnp.int32, v.shape, 0)
    for r in range(k):
        rolled = jnp.where(row == n - k + r, halo[r:r + 1, :], rolled)
    return rolled


def _mesh_pos():
    return lax.axis_index("x"), lax.axis_index("y"), lax.axis_index("c")


def _direct_copies(srcs_for, out_refs, send_sems, recv_sems, loc_sems):
    x, y, c = _mesh_pos()
    me_id = 4 * x + 2 * y + c
    n_arr = len(out_refs)
    dsts = [r.at[me_id] for r in out_refs]
    own = srcs_for(me_id)
    mine = [pltpu.make_async_copy(own[a], dsts[a], loc_sems.at[a]) for a in range(n_arr)]
    sends = []
    for k in range(1, N_DEV):
        px, py, pc = x ^ ((k >> 2) & 1), y ^ ((k >> 1) & 1), c ^ (k & 1)
        src = srcs_for(4 * px + 2 * py + pc)
        for a in range(n_arr):
            sends.append(pltpu.make_async_remote_copy(
                src_ref=src[a], dst_ref=dsts[a],
                send_sem=send_sems.at[(k - 1) * n_arr + a], recv_sem=recv_sems.at[(k - 1) * n_arr + a],
                device_id=(px, py, pc), device_id_type=MESH))
    return mine, sends


class _TwoLevelGather:
    def __init__(self, srcs, slots, send_sems, recv_sems, loc_sems):
        self.srcs, self.slots, self.n_arr = srcs, slots, len(srcs)
        self.send_sems, self.recv_sems, self.loc_sems = send_sems, recv_sems, loc_sems
        x, y, c = _mesh_pos()
        self.c = c
        self.me, self.sib = (x, y, c), (x, y, 1 - c)
        self.chips = [(1 - x, y), (x, 1 - y), (1 - x, 1 - y)]

    def _copies(self, k, block, to, from_src=False):
        dev = 4 * block[0] + 2 * block[1] + block[2]
        return [pltpu.make_async_remote_copy(
            src_ref=self.srcs[a] if from_src else self.slots[a](dev), dst_ref=self.slots[a](dev),
            send_sem=self.send_sems.at[k * self.n_arr + a], recv_sem=self.recv_sems.at[k * self.n_arr + a],
            device_id=to, device_id_type=MESH) for a in range(self.n_arr)]

    def _local(self):
        dev = 4 * self.me[0] + 2 * self.me[1] + self.me[2]
        return [pltpu.make_async_copy(self.srcs[a], self.slots[a](dev), self.loc_sems.at[a])
                for a in range(self.n_arr)]

    def start(self, chips=(0, 1, 2)):
        for cp in self._local() + self._copies(0, self.me, self.sib, True):
            cp.start()
        self.start_to(chips)

    def start_to(self, chips):
        for j in chips:
            for cp in self._copies(1 + j, self.me, (*self.chips[j], self.c), True):
                cp.start()

    def wait_own(self):
        for cp in self._local():
            cp.wait()

    def wait_sibling(self):
        for cp in self._copies(0, self.sib, self.me):
            cp.wait_recv()

    def wait_and_pass_on(self, j):
        chip = self.chips[j]
        for cp in self._copies(1 + j, (*chip, self.c), self.me):
            cp.wait_recv()
        for cp in self._copies(4 + j, (*chip, self.c), self.sib):
            cp.start()

    def wait_passed_on(self, j):
        for cp in self._copies(4 + j, (*self.chips[j], 1 - self.c), self.me):
            cp.wait_recv()

    def wait_sends(self):
        for cp in self._copies(0, self.me, self.sib, True):
            cp.wait_send()
        for j, chip in enumerate(self.chips):
            for cp in self._copies(1 + j, self.me, (*chip, self.c), True) + self._copies(4 + j, (*chip, self.c), self.sib):
                cp.wait_send()

    def forward(self):
        for j in range(3):
            self.wait_and_pass_on(j)

    def finish(self):
        self.wait_sibling()
        for j in range(3):
            self.wait_passed_on(j)
        self.wait_sends()
        self.wait_own()


def _disc(a_re, a_im, log_dt, b_re, b_im):
    dt = jnp.exp(log_dt)
    mag = jnp.exp(a_re * dt)
    ab_re = mag * jnp.cos(a_im * dt)
    ab_im = mag * jnp.sin(a_im * dt)
    den = a_re * a_re + a_im * a_im
    p_re = ab_re - 1.0
    p_im = ab_im
    q_re = (p_re * a_re + p_im * a_im) / den
    q_im = (p_im * a_re - p_re * a_im) / den
    bb_re = q_re * b_re - q_im * b_im
    bb_im = q_re * b_im + q_im * b_re
    return ab_re, ab_im, bb_re, bb_im


def _split3(v):
    hi = v.astype(BF16)
    r1 = v - hi.astype(F32)
    mid = r1.astype(BF16)
    lo = (r1 - mid.astype(F32)).astype(BF16)
    return hi, mid, lo


def _select_dot(sel, v):
    return sum(_dot(sel, t) for t in _split3(v))


PACK_ROWS = 72
PACK_W = 512
ROW_FINAL_GAIN, ROW_NORM_GAIN, ROW_BGLU_D, ROW_CONV, ROW_LOSS, ROW_S5 = 0, 8, 16, 24, 32, 40
LANE_A_RE, LANE_A_IM, LANE_LOG_DT = 0, 128, 256


def _ssm_disc_bwd_pack(a_re_x, a_im_x, log_dt_x, b_re, b_im, g_ab_re, g_ab_im, dbb_re_d, dbb_im_d,
                       loss_t, dg8, dgf, dbg, dd, dcw, dc_re_d, dc_im_d):
    rows_gh = N_GROUPS * GROUP

    def body(are, aim, ldt, bre, bim, gabre, gabim, dbbre_ref, dbbim_ref,
             loss_ref, dg8_ref, dgf_ref, dbg_ref, dd_ref, dcw_ref, dcre_ref, dcim_ref,
             p_ref, gc_ref, gb_ref, gbb_re, gbb_im):
        r_g = lax.broadcasted_iota(jnp.int32, (N_GROUPS, rows_gh), 0)
        c_gh = lax.broadcasted_iota(jnp.int32, (N_GROUPS, rows_gh), 1)
        group_sum = (c_gh // GROUP == r_g).astype(BF16)
        r_gh = lax.broadcasted_iota(jnp.int32, (rows_gh, N_GROUPS), 0)
        c_g = lax.broadcasted_iota(jnp.int32, (rows_gh, N_GROUPS), 1)
        first_row = (r_gh == c_g * GROUP).astype(BF16)

        def diag_block(ref, j, gi):
            return ref[j, gi * GROUP:(gi + 1) * GROUP, gi * STATE:(gi + 1) * STATE]

        for j in range(N_JBLK):
            for gi in range(SUBLANES):
                r0 = (j * SUBLANES + gi) * GROUP
                gbb_re[r0:r0 + GROUP, :] = diag_block(dbbre_ref, j, gi)
                gbb_im[r0:r0 + GROUP, :] = diag_block(dbbim_ref, j, gi)
                both = jnp.concatenate([diag_block(dcre_ref, j, gi), -diag_block(dcim_ref, j, gi)], axis=1)
                gc_ref[r0:r0 + GROUP, :] = both.astype(BF16)

        _, vjp = jax.vjp(_disc, are[...], aim[...], ldt[...], bre[...], bim[...])
        d_are, d_aim, d_ldt, d_bre, d_bim = vjp((_select_dot(first_row, gabre[...]), _select_dot(first_row, gabim[...]),
                                                 gbb_re[...], gbb_im[...]))
        gb_ref[...] = jnp.concatenate([d_bre, d_bim], axis=1).astype(BF16)

        p_ref[...] = jnp.zeros_like(p_ref)
        half = D_MODEL // 2
        for r, src in ((ROW_FINAL_GAIN, dgf_ref), (ROW_NORM_GAIN, dg8_ref)):
            p_ref[r:r + 1, :] = src[0:1, 0:half]
            p_ref[r + 1:r + 2, :] = src[0:1, half:D_MODEL]
        p_ref[ROW_BGLU_D:ROW_BGLU_D + 1, :] = dbg_ref[...]
        p_ref[ROW_BGLU_D + 1:ROW_BGLU_D + 2, :] = dd_ref[...]
        p_ref[ROW_CONV:ROW_CONV + SUBLANES, :] = dcw_ref[...]
        p_ref[ROW_LOSS:ROW_LOSS + SUBLANES, 0:LANES] = loss_ref[...]
        s5 = slice(ROW_S5, ROW_S5 + N_GROUPS)
        p_ref[s5, LANE_A_RE:LANE_A_RE + STATE] = _select_dot(group_sum, d_are)
        p_ref[s5, LANE_A_IM:LANE_A_IM + STATE] = _select_dot(group_sum, d_aim)
        p_ref[s5, LANE_LOG_DT:LANE_LOG_DT + LANES] = _select_dot(group_sum, jnp.broadcast_to(d_ldt, (rows_gh, LANES)))

    operands = (a_re_x, a_im_x, log_dt_x, b_re, b_im, g_ab_re, g_ab_im, dbb_re_d, dbb_im_d,
                loss_t, dg8, dgf, dbg, dd, dcw, dc_re_d, dc_im_d)
    out_shape = (_out((PACK_ROWS, PACK_W), F32),
                 _out((rows_gh, 2 * STATE), BF16),
                 _out((rows_gh, 2 * STATE), BF16))
    return _pcall(body, name="ssm_disc_bwd_pack", grid=(1,), out_shape=out_shape,
                  in_specs=_whole_specs(operands), out_specs=tuple(_whole_specs(out_shape)),
                  scratch_shapes=[pltpu.VMEM((rows_gh, STATE), F32), pltpu.VMEM((rows_gh, STATE), F32)],
                  compiler_params=_params(1))(*operands)


def _s5_prepare(are, aim, ldt, bre, bim, cre, cim,
                o_ax_re, o_ax_im, o_ldt_x, o_ab_re, o_ab_im, o_bb_re, o_bb_im, o_c_re, o_c_imn):
    rows_gh = N_GROUPS * GROUP
    rep = (lax.broadcasted_iota(jnp.int32, (rows_gh, N_GROUPS), 0) // GROUP
           == lax.broadcasted_iota(jnp.int32, (rows_gh, N_GROUPS), 1)).astype(BF16)
    eye = (lax.broadcasted_iota(jnp.int32, (N_GROUPS, N_GROUPS), 0)
           == lax.broadcasted_iota(jnp.int32, (N_GROUPS, N_GROUPS), 1)).astype(F32)
    ldt_col = jnp.sum(eye * ldt[...], axis=1, keepdims=True)
    a_re_x = _select_dot(rep, are[...])
    a_im_x = _select_dot(rep, aim[...])
    ldt_x = _select_dot(rep, jnp.broadcast_to(ldt_col, (N_GROUPS, LANES)))[:, 0:1]
    o_ax_re[...] = a_re_x
    o_ax_im[...] = a_im_x
    o_ldt_x[...] = ldt_x
    ab_re, ab_im, bb_re, bb_im = _disc(a_re_x, a_im_x, ldt_x, bre[...], bim[...])
    for j in range(N_JBLK):
        first = [(j * SUBLANES + gi) * GROUP for gi in range(SUBLANES)]
        o_ab_re[j] = jnp.concatenate([ab_re[r:r + 1, :] for r in first], axis=1)
        o_ab_im[j] = jnp.concatenate([ab_im[r:r + 1, :] for r in first], axis=1)
    for o, v in ((o_bb_re, bb_re), (o_bb_im, bb_im), (o_c_re, cre[...]), (o_c_imn, -cim[...])):
        for j in range(N_JBLK):
            for gi in range(SUBLANES):
                r0 = (j * SUBLANES + gi) * GROUP
                parts = [v[r0:r0 + GROUP, :] if k == gi else jnp.zeros((GROUP, STATE), F32) for k in range(SUBLANES)]
                o[j, gi * GROUP:(gi + 1) * GROUP, :] = jnp.concatenate(parts, axis=1).astype(BF16)


def _in_proj(order, x2, g1, w_in_b, s5):
    n = x2.shape[0]
    tm = min(IN_TILE, n)
    n_tiles = n // tm
    n_s5_in = len(s5)
    n_s5_out = 9

    def body(order_ref, x_ref, g_ref, w_ref, *refs):
        s5_in = refs[:n_s5_in]
        xn_ref, proj_ref, wall_ref = refs[n_s5_in:n_s5_in + 3]
        s5_out = refs[n_s5_in + 3:n_s5_in + 3 + n_s5_out]
        xn_scr, wbuf, send_sems, recv_sems, loc_sems, out_sems = refs[n_s5_in + 3 + n_s5_out:]
        k = pl.program_id(0)
        i = pl.program_id(1)

        def slot(dev):
            return wbuf.at[dev // 2, :, pl.ds(pl.multiple_of((dev % 2) * COLS_PER_DEV, LANES), COLS_PER_DEV)]

        gather = _TwoLevelGather([w_ref], [slot], send_sems, recv_sems, loc_sems)

        @pl.when((k == 0) & (i == 0))
        def _():
            gather.start(chips=(0, 1))

        def own_chip():
            gather.wait_own()
            gather.wait_sibling()

        def other_chip(j):
            gather.wait_and_pass_on(j)
            if j == 0:
                gather.start_to((2,))
            gather.wait_passed_on(j)

        arrivals = [own_chip] + [functools.partial(other_chip, j) for j in range(3)]
        for kk, arrived in enumerate(arrivals):
            @pl.when((k == kk) & (i == 0))
            def _(arrived=arrived):
                arrived()

        rows = pl.ds(pl.multiple_of(i * tm, tm), tm)

        @pl.when(k == 0)
        def _():
            x = x_ref[...]
            r = lax.rsqrt(jnp.mean(x * x, axis=-1, keepdims=True) + EPS)
            xn = ((x * r) * g_ref[...]).astype(BF16)
            xn_scr[rows, :] = xn
            xn_ref[...] = xn

        proj_ref[...] = _dot(xn_scr[rows, :], wbuf[order_ref[k]])

        @pl.when((k == 0) & (i == n_tiles - 1))
        def _():
            _s5_prepare(*s5_in, *s5_out)

        @pl.when((k == N_CHIP - 1) & (i == n_tiles - 1))
        def _():
            gather.wait_sends()
            outs = [pltpu.make_async_copy(wbuf.at[q], wall_ref.at[:, q * COLS_PER_CHIP:(q + 1) * COLS_PER_CHIP],
                                          out_sems.at[q]) for q in range(N_CHIP)]
            for cp in outs:
                cp.start()
            for cp in outs:
                cp.wait()

    tile_once = lambda k, i, order: (jnp.where(k == 0, i, n_tiles - 1), 0)
    whole = lambda shape: pl.BlockSpec(shape, lambda k, i, order: (0,) * len(shape))
    rows_gh = N_GROUPS * GROUP
    s5_out_shapes = ([(rows_gh, STATE), F32], [(rows_gh, STATE), F32], [(rows_gh, 1), F32],
                     [(N_JBLK, 1, JB_ST), F32], [(N_JBLK, 1, JB_ST), F32]) + ([(N_JBLK, JB_CH, JB_ST), BF16],) * 4
    grid_spec = pltpu.PrefetchScalarGridSpec(
        num_scalar_prefetch=1, grid=(N_CHIP, n_tiles),
        in_specs=[pl.BlockSpec((tm, D_MODEL), tile_once),
                  whole((1, D_MODEL)),
                  HBM_SPEC,
                  *(whole(a.shape) for a in s5)],
        out_specs=(pl.BlockSpec((tm, D_MODEL), tile_once),
                   pl.BlockSpec((tm, COLS_PER_CHIP), lambda k, i, order: (i, order[k])),
                   HBM_SPEC,
                   *(whole(shape) for shape, _ in s5_out_shapes)),
        scratch_shapes=[pltpu.VMEM((n, D_MODEL), BF16), pltpu.VMEM((N_CHIP, D_MODEL, COLS_PER_CHIP), BF16),
                        pltpu.SemaphoreType.DMA((7,)), pltpu.SemaphoreType.DMA((7,)), pltpu.SemaphoreType.DMA((1,)),
                        pltpu.SemaphoreType.DMA((N_CHIP,))])
    outs = _pcall(
        body, name="in_proj", grid_spec=grid_spec,
        out_shape=(_out((n, D_MODEL), BF16), _out((n, IN_COLS), F32),
                   _out((D_MODEL, IN_COLS), BF16),
                   *(_out(shape, dt) for shape, dt in s5_out_shapes)),
        compiler_params=_params(2),
    )(order, x2, g1, w_in_b, *s5)
    return outs[0], outs[1], outs[2], outs[3:]


def _cmul(p, q):
    return p[0] * q[0] - p[1] * q[1], p[0] * q[1] + p[1] * q[0]


def _scan_tables(ar, ai, width, reverse):
    pows = [(ar, ai)]
    for _ in range(SUBLANES - 1):
        pows.append(_cmul(pows[-1], (ar, ai)))
    row = lax.broadcasted_iota(jnp.int32, (SUBLANES, width), 0)

    def bc(v):
        return jnp.broadcast_to(v, (SUBLANES, width))

    levels = []
    for k in (1, 2, 4):
        keep = (row <= SUBLANES - 1 - k) if reverse else (row >= k)
        levels.append((jnp.where(keep, bc(pows[k - 1][0]), 0.0), jnp.where(keep, bc(pows[k - 1][1]), 0.0)))
    cre = jnp.zeros((SUBLANES, width), F32)
    cim = jnp.zeros((SUBLANES, width), F32)
    for r in range(SUBLANES):
        e = (SUBLANES - r) if reverse else (r + 1)
        cre = jnp.where(row == r, bc(pows[e - 1][0]), cre)
        cim = jnp.where(row == r, bc(pows[e - 1][1]), cim)
    return levels, (cre, cim)


def _load_chunked(src_ref, b, dst_ref, n_rows):
    n_blk = n_rows // SUBLANES
    for i in range(n_blk):
        dst_ref[b, i * SUBLANES:(i + 1) * SUBLANES, :] = src_ref[b, pl.ds(i, SUBLANES, stride=n_blk), :]


def _store_chunked(val, dst_ref, b, n_rows):
    n_blk = n_rows // SUBLANES
    for i in range(n_blk):
        dst_ref[b, pl.ds(i, SUBLANES, stride=n_blk), :] = val[i * SUBLANES:(i + 1) * SUBLANES, :]


def _chunk_scan(re_ref, im_ref, bs, car_ref, ar, ai, n_rows, reverse, on_block=None):
    width = re_ref.shape[2]
    n_blk = n_rows // SUBLANES
    shape = (SUBLANES, width)
    abr = jnp.broadcast_to(ar, shape)
    abi = jnp.broadcast_to(ai, shape)
    order = list(range(n_blk - 1, -1, -1)) if reverse else list(range(n_blk))

    def blk(ref, b, i):
        return ref[b, i * SUBLANES:(i + 1) * SUBLANES, :]

    def step(state, b, i):
        sr, si = state
        return abr * sr - abi * si + blk(re_ref, b, i), abr * si + abi * sr + blk(im_ref, b, i)

    finals = {b: (blk(re_ref, b, order[0]), blk(im_ref, b, order[0])) for b in bs}
    for i in order[1:]:
        for b in bs:
            finals[b] = step(finals[b], b, i)

    mr, mi = ar, ai
    for _ in range(n_blk.bit_length() - 1):
        mr, mi = _cmul((mr, mi), (mr, mi))
    levels, _ = _scan_tables(mr, mi, width, reverse)
    mbr = jnp.broadcast_to(mr, shape)
    mbi = jnp.broadcast_to(mi, shape)
    row = lax.broadcasted_iota(jnp.int32, shape, 0)
    edge_in = SUBLANES - 1 if reverse else 0
    edge_out = 0 if reverse else SUBLANES - 1
    sh1 = SUBLANES - 1 if reverse else 1
    states = {}
    for b in bs:
        fr, fi = finals[b]
        gr = jnp.where(row == edge_in, jnp.broadcast_to(car_ref[b, 0:1, :], shape), pltpu.roll(fr, sh1, 0))
        gi = jnp.where(row == edge_in, jnp.broadcast_to(car_ref[b, 1:2, :], shape), pltpu.roll(fi, sh1, 0))
        for (lr, li), k in zip(levels, (1, 2, 4)):
            sh = (SUBLANES - k) if reverse else k
            sr = pltpu.roll(gr, sh, 0)
            si = pltpu.roll(gi, sh, 0)
            gr, gi = gr + (lr * sr - li * si), gi + (lr * si + li * sr)
        car_ref[b, 0:1, :] = (fr + (mbr * gr - mbi * gi))[edge_out:edge_out + 1, :]
        car_ref[b, 1:2, :] = (fi + (mbr * gi + mbi * gr))[edge_out:edge_out + 1, :]
        states[b] = (gr, gi)

    for i in order:
        for b in bs:
            states[b] = step(states[b], b, i)
            re_ref[b, i * SUBLANES:(i + 1) * SUBLANES, :] = states[b][0]
            im_ref[b, i * SUBLANES:(i + 1) * SUBLANES, :] = states[b][1]
            if on_block is not None:
                on_block(b, i, *states[b])


def _ssm_fwd(u, bb_re, bb_im, c_re_t, c_imn_t, d_row, ab_re, ab_im, w_out_b, w_glu_b, conv_p, n_seq, seq):
    tt = min(SCAN_TILE, seq)
    nt = seq // tt

    def body(u_ref, bbre, bbim, cre, cimn, d_ref, are, aim, wout_ref, wglu_ref, cw_ref,
             sre_ref, sim_ref, y_ref, oout_ref, oglu_ref, ocw_ref,
             up_ref, car_ref, send_sems, recv_sems, loc_sems):
        j = pl.program_id(0)
        t = pl.program_id(1)
        gather = _TwoLevelGather(
            [wout_ref, wglu_ref, cw_ref],
            [lambda dev: oout_ref.at[pl.ds(pl.multiple_of(dev * OUT_ROWS_PER_DEV, OUT_ROWS_PER_DEV), OUT_ROWS_PER_DEV), :],
             lambda dev: oglu_ref.at[pl.ds(pl.multiple_of(dev * GLU_ROWS_PER_DEV, GLU_ROWS_PER_DEV), GLU_ROWS_PER_DEV), :],
             lambda dev: ocw_ref.at[dev]],
            send_sems, recv_sems, loc_sems)

        @pl.when((j == 0) & (t == 0))
        def _():
            gather.start()

        @pl.when((j == N_JBLK // 2) & (t == 0))
        def _():
            gather.forward()

        @pl.when(t == 0)
        def _():
            car_ref[...] = jnp.zeros_like(car_ref)

        bs = list(range(n_seq))
        for b in bs:
            _load_chunked(u_ref, b, up_ref, tt)
        for b in bs:
            ub = up_ref[b].astype(BF16)
            sre_ref[b] = _dot(ub, bbre[0])
            sim_ref[b] = _dot(ub, bbim[0])
            _chunk_scan(sre_ref, sim_ref, [b], car_ref, are[0], aim[0], tt, reverse=False)
        for b in bs:
            yp = (_dot_nt(sre_ref[b].astype(BF16), cre[0]) + _dot_nt(sim_ref[b].astype(BF16), cimn[0])
                  + d_ref[...] * up_ref[b])
            _store_chunked(yp, y_ref, b, tt)

        @pl.when((j == N_JBLK - 1) & (t == nt - 1))
        def _():
            gather.finish()

    tok = lambda j, t: (0, t, j)
    blk3 = lambda j, t: (j, 0, 0)
    row = lambda j, t: (0, j)
    st = _out((n_seq, seq, N_JBLK * JB_ST), F32)
    n_arr = 3
    return _pcall(
        body, name="ssm_fwd", grid=(N_JBLK, nt),
        out_shape=(st, st, _out((n_seq, seq, SSM_W), F32),
                   _out((D_MODEL, D_MODEL), BF16), _out((SSM_W, SSM_W), BF16),
                   _out((N_DEV, SUBLANES, LANES), F32)),
        in_specs=[pl.BlockSpec((n_seq, tt, JB_CH), tok),
                  pl.BlockSpec((1, JB_CH, JB_ST), blk3), pl.BlockSpec((1, JB_CH, JB_ST), blk3),
                  pl.BlockSpec((1, JB_CH, JB_ST), blk3), pl.BlockSpec((1, JB_CH, JB_ST), blk3),
                  pl.BlockSpec((1, JB_CH), row), pl.BlockSpec((1, 1, JB_ST), blk3), pl.BlockSpec((1, 1, JB_ST), blk3),
                  HBM_SPEC, HBM_SPEC, HBM_SPEC],
        out_specs=(pl.BlockSpec((n_seq, tt, JB_ST), tok), pl.BlockSpec((n_seq, tt, JB_ST), tok),
                   pl.BlockSpec((n_seq, tt, JB_CH), tok), HBM_SPEC, HBM_SPEC, HBM_SPEC),
        scratch_shapes=[pltpu.VMEM((n_seq, tt, JB_CH), F32), pltpu.VMEM((n_seq, SUBLANES, JB_ST), F32),
                        pltpu.SemaphoreType.DMA((7 * n_arr,)), pltpu.SemaphoreType.DMA((7 * n_arr,)),
                        pltpu.SemaphoreType.DMA((n_arr,))],
        compiler_params=_params(2),
    )(u, bb_re, bb_im, c_re_t, c_imn_t, d_row, ab_re, ab_im, w_out_b, w_glu_b, conv_p)


def _ssm_bwd(dy, u, s_re, s_im, bb_re, bb_im, c_re_t, c_imn_t, d_row, ab_re, ab_im, g_out, g_glu, n_seq, seq):
    tt = min(SCAN_TILE, seq)
    nt = seq // tt
    rows8 = tt // SUBLANES

    def body(dy_ref, u_ref, sre_ref, sim_ref, pre_ref, pim_ref, bbre, bbim, cre, cimn, d_ref, are, aim,
             gout_ref, gglu_ref,
             du_ref, dcre_ref, dcim_ref, dbbre_ref, dbbim_ref, dare_ref, daim_ref, dd_ref, rout_ref, rglu_ref,
             lre_ref, lim_ref, dyp_ref, up_ref, car_ref, send_sems, recv_sems, loc_sems):
        j = pl.program_id(0)
        tr = pl.program_id(1)

        def exchange():
            return _direct_copies(lambda pid: [gout_ref.at[pid], gglu_ref.at[pid]], [rout_ref, rglu_ref],
                                  send_sems, recv_sems, loc_sems)

        @pl.when((j == 0) & (tr == 0))
        def _():
            mine, sends = exchange()
            for cp in mine + sends:
                cp.start()

        @pl.when(tr == 0)
        def _():
            car_ref[...] = jnp.zeros_like(car_ref)
            for r in (dcre_ref, dcim_ref, dbbre_ref, dbbim_ref, dare_ref, daim_ref, dd_ref):
                r[...] = jnp.zeros_like(r)

        first = tr == nt - 1
        row = lax.broadcasted_iota(jnp.int32, (SUBLANES, JB_ST), 0)
        n_blk = tt // SUBLANES
        bs = list(range(n_seq))
        for b in bs:
            _load_chunked(dy_ref, b, dyp_ref, tt)
            _load_chunked(u_ref, b, up_ref, tt)
        for b in bs:
            dyb = dyp_ref[b].astype(BF16)
            lre_ref[b] = _dot(dyb, cre[0])
            lim_ref[b] = _dot(dyb, cimn[0])
        acc = {b: [jnp.zeros((SUBLANES, JB_ST), F32), jnp.zeros((SUBLANES, JB_ST), F32)] for b in bs}

        def on_block(b, i, lr, li):
            if i > 0:
                spr = sre_ref[b, (i - 1) * SUBLANES:i * SUBLANES, :]
                spi = sim_ref[b, (i - 1) * SUBLANES:i * SUBLANES, :]
            else:
                hr = jnp.where(first, 0.0, pre_ref[b, SUBLANES - 1:SUBLANES, :])
                hi = jnp.where(first, 0.0, pim_ref[b, SUBLANES - 1:SUBLANES, :])
                last_r = sre_ref[b, (n_blk - 1) * SUBLANES:n_blk * SUBLANES, :]
                last_i = sim_ref[b, (n_blk - 1) * SUBLANES:n_blk * SUBLANES, :]
                spr = jnp.where(row == 0, jnp.broadcast_to(hr, row.shape), pltpu.roll(last_r, 1, 0))
                spi = jnp.where(row == 0, jnp.broadcast_to(hi, row.shape), pltpu.roll(last_i, 1, 0))
            acc[b][0] = acc[b][0] + (lr * spr + li * spi)
            acc[b][1] = acc[b][1] + (li * spr - lr * spi)

        _chunk_scan(lre_ref, lim_ref, bs, car_ref, are[0], -aim[0], tt, reverse=True, on_block=on_block)
        for b in bs:
            dare_ref[...] += jnp.sum(acc[b][0], axis=0, keepdims=True)
            daim_ref[...] += jnp.sum(acc[b][1], axis=0, keepdims=True)
            dyp = dyp_ref[b]
            up = up_ref[b]
            dyb = dyp.astype(BF16)
            ub = up.astype(BF16)
            lrb = lre_ref[b].astype(BF16)
            lib = lim_ref[b].astype(BF16)
            dup = d_ref[...] * dyp + _dot_nt(lrb, bbre[0]) + _dot_nt(lib, bbim[0])
            _store_chunked(dup, du_ref, b, tt)
            dbbre_ref[0] += _dot_tn(ub, lrb)
            dbbim_ref[0] += _dot_tn(ub, lib)
            dcre_ref[0] += _dot_tn(dyb, sre_ref[b].astype(BF16))
            dcim_ref[0] += _dot_tn(dyb, sim_ref[b].astype(BF16))
            dd_ref[...] += jnp.sum(dyp * up, axis=0, keepdims=True)

        @pl.when((j == N_JBLK - 1) & (tr == nt - 1))
        def _():
            mine, sends = exchange()
            for cp in sends + mine:
                cp.wait()

    tok = lambda j, t: (0, nt - 1 - t, j)
    halo = lambda j, t: (0, jnp.maximum((nt - 1 - t) * rows8 - 1, 0), j)
    blk3 = lambda j, t: (j, 0, 0)
    row1 = lambda j, t: (0, j)
    acc_shape = _out((N_JBLK, JB_CH, JB_ST), F32)
    return _pcall(
        body, name="ssm_bwd", grid=(N_JBLK, nt),
        out_shape=(_out((n_seq, seq, SSM_W), F32), acc_shape, acc_shape, acc_shape, acc_shape,
                   _out((1, N_JBLK * JB_ST), F32), _out((1, N_JBLK * JB_ST), F32),
                   _out((1, SSM_W), F32),
                   _out((N_DEV,) + g_out.shape[1:], F32),
                   _out((N_DEV,) + g_glu.shape[1:], F32)),
        in_specs=[pl.BlockSpec((n_seq, tt, JB_CH), tok), pl.BlockSpec((n_seq, tt, JB_CH), tok),
                  pl.BlockSpec((n_seq, tt, JB_ST), tok), pl.BlockSpec((n_seq, tt, JB_ST), tok),
                  pl.BlockSpec((n_seq, SUBLANES, JB_ST), halo), pl.BlockSpec((n_seq, SUBLANES, JB_ST), halo),
                  pl.BlockSpec((1, JB_CH, JB_ST), blk3), pl.BlockSpec((1, JB_CH, JB_ST), blk3),
                  pl.BlockSpec((1, JB_CH, JB_ST), blk3), pl.BlockSpec((1, JB_CH, JB_ST), blk3),
                  pl.BlockSpec((1, JB_CH), row1), pl.BlockSpec((1, 1, JB_ST), blk3), pl.BlockSpec((1, 1, JB_ST), blk3),
                  HBM_SPEC, HBM_SPEC],
        out_specs=(pl.BlockSpec((n_seq, tt, JB_CH), tok),
                   pl.BlockSpec((1, JB_CH, JB_ST), blk3), pl.BlockSpec((1, JB_CH, JB_ST), blk3),
                   pl.BlockSpec((1, JB_CH, JB_ST), blk3), pl.BlockSpec((1, JB_CH, JB_ST), blk3),
                   pl.BlockSpec((1, JB_ST), row1), pl.BlockSpec((1, JB_ST), row1), pl.BlockSpec((1, JB_CH), row1),
                   HBM_SPEC, HBM_SPEC),
        scratch_shapes=[pltpu.VMEM((n_seq, tt, JB_ST), F32), pltpu.VMEM((n_seq, tt, JB_ST), F32),
                        pltpu.VMEM((n_seq, tt, JB_CH), F32), pltpu.VMEM((n_seq, tt, JB_CH), F32),
                        pltpu.VMEM((n_seq, SUBLANES, JB_ST), F32),
                        pltpu.SemaphoreType.DMA((7 * 2,)), pltpu.SemaphoreType.DMA((7 * 2,)),
                        pltpu.SemaphoreType.DMA((2,))],
        compiler_params=_params(2),
    )(dy, u, s_re, s_im, s_re, s_im, bb_re, bb_im, c_re_t, c_imn_t, d_row, ab_re, ab_im, g_out, g_glu)


def _mix(x2, tgt2, y, proj, gf, b_glu, conv8, w_glu_f, w_out_f, seq):
    n = x2.shape[0]
    tm = TOK_TILE
    tiles_per_seq = seq // tm
    rows8 = tm // SUBLANES

    def body(x_ref, t_ref, y_ref, zs_ref, h_ref, bc_ref, cc_ref, zc_ref, hp_ref, ccp_ref,
             gf_ref, bg_ref, cw_ref, wg_ref, wo_ref,
             dh2_ref, dy_ref, dzs_ref, dbc_ref, dzc_ref, dyc_ref,
             dwo_ref, dwg_ref, loss_ref, dgf_ref, dbg_ref, dcw_ref):
        i = pl.program_id(0)

        @pl.when(i == 0)
        def _():
            for r in (dwo_ref, dwg_ref, loss_ref, dgf_ref, dbg_ref, dcw_ref):
                r[...] = jnp.zeros_like(r)

        yv = y_ref[...]
        y1, dgelu = _gelu_and_grad(yv)
        y1b = y1.astype(BF16)
        gate = _sigmoid(_dot(y1b, wg_ref[...]) + bg_ref[...])
        y2 = y1 * gate
        szs, dszs = _silu_and_grad(zs_ref[...])
        yssm = y2 * szs
        hv = h_ref[...]
        ccv = cc_ref[...]
        bcv = bc_ref[...]
        v = ccv * hv
        first = (i % tiles_per_seq) == 0
        vhalo = jnp.where(first, 0.0, ccp_ref[...] * hp_ref[...])
        v1 = _shift_down(v, vhalo, 1)
        v2 = _shift_down(v, vhalo, 2)
        w0 = cw_ref[0:1, :]
        w1 = cw_ref[1:2, :]
        w2 = cw_ref[2:3, :]
        yc = w0 * v2 + w1 * v1 + w2 * v
        szc, dszc = _silu_and_grad(zc_ref[...])
        yconv = (bcv * yc) * szc
        ysb = yssm.astype(BF16)
        ycb = yconv.astype(BF16)
        h2 = x_ref[...] + _dot(ysb, wo_ref[0:SSM_W, :]) + _dot(ycb, wo_ref[SSM_W:, :])
        r2 = lax.rsqrt(jnp.mean(h2 * h2, axis=-1, keepdims=True) + EPS)
        hn = h2 * r2
        gfv = gf_ref[...]
        err = hn * gfv - t_ref[...]
        loss_ref[...] += 0.5 * jnp.sum(jnp.mean(err * err, axis=-1, keepdims=True))
        dout = err * (1.0 / D_MODEL)
        dgf_ref[...] += jnp.sum(dout * hn, axis=0, keepdims=True)
        dn = dout * gfv
        dh2 = r2 * (dn - hn * jnp.mean(dn * hn, axis=-1, keepdims=True))
        dh2_ref[...] = dh2
        dh2b = dh2.astype(BF16)
        dwo_ref[0:SSM_W, :] += _dot_tn(ysb, dh2b)
        dwo_ref[SSM_W:, :] += _dot_tn(ycb, dh2b)
        dyssm = _dot_nt(dh2b, wo_ref[0:SSM_W, :])
        dyconv = _dot_nt(dh2b, wo_ref[SSM_W:, :])
        dy2 = dyssm * szs
        dzs_ref[...] = (dyssm * y2 * dszs).astype(BF16)
        dgp = dy2 * y1 * (gate * (1.0 - gate))
        dgpb = dgp.astype(BF16)
        dy1 = dy2 * gate + _dot_nt(dgpb, wg_ref[...])
        dwg_ref[...] += _dot_tn(y1b, dgpb)
        dbg_ref[...] += jnp.sum(dgp, axis=0, keepdims=True)
        dy_ref[...] = dy1 * dgelu
        dbc_ref[...] = (dyconv * yc * szc).astype(BF16)
        dyc = dyconv * bcv * szc
        dyc_ref[...] = dyc
        dzc_ref[...] = (dyconv * bcv * yc * dszc).astype(BF16)
        dcw_ref[0:1, :] += jnp.sum(dyc * v2, axis=0, keepdims=True)
        dcw_ref[1:2, :] += jnp.sum(dyc * v1, axis=0, keepdims=True)
        dcw_ref[2:3, :] += jnp.sum(dyc * v, axis=0, keepdims=True)

    tile_d = pl.BlockSpec((tm, D_MODEL), lambda i: (i, 0))
    tile_s = pl.BlockSpec((tm, SSM_W), lambda i: (i, 0))
    seg_of = lambda c: pl.BlockSpec((tm, SSM_W), lambda i: (i, c))
    halo_of = lambda c: pl.BlockSpec((SUBLANES, SSM_W), lambda i: (jnp.maximum(i * rows8 - 1, 0), c))
    const = lambda shape: pl.BlockSpec(shape, lambda i: (0,) * len(shape))
    seg = _out((n, SSM_W), F32)
    seg_b = _out((n, SSM_W), BF16)
    return _pcall(
        body, name="mix", grid=(n // tm,),
        out_shape=(_out((n, D_MODEL), F32), seg, seg_b, seg_b, seg_b, seg,
                   _out((D_MODEL, D_MODEL), F32), _out((SSM_W, SSM_W), F32),
                   _out((SUBLANES, LANES), F32), _out((1, D_MODEL), F32),
                   _out((1, SSM_W), F32), _out((SUBLANES, CONV_W), F32)),
        in_specs=[tile_d, tile_d, tile_s, seg_of(SEG_ZS), seg_of(SEG_H), seg_of(SEG_BC), seg_of(SEG_CC), seg_of(SEG_ZC),
                  halo_of(SEG_H), halo_of(SEG_CC),
                  const((1, D_MODEL)), const((1, SSM_W)), const((SUBLANES, CONV_W)),
                  const((SSM_W, SSM_W)), const((D_MODEL, D_MODEL))],
        out_specs=(tile_d, tile_s, tile_s, tile_s, tile_s, tile_s,
                   const((D_MODEL, D_MODEL)), const((SSM_W, SSM_W)), const((SUBLANES, LANES)),
                   const((1, D_MODEL)), const((1, SSM_W)), const((SUBLANES, CONV_W))),
        compiler_params=_params(1),
    )(x2, tgt2, y, proj, proj, proj, proj, proj, proj, proj, gf, b_glu, conv8, w_glu_f, w_out_f)


def _in_bwd(x2, dh2, du, dzs, dyc, proj, dbc, dzc, g1, conv8, w_full, seq):
    n = x2.shape[0]
    tm = TOK_TILE
    n_tiles = n // tm
    tiles_per_seq = seq // tm
    rows8 = tm // SUBLANES
    n_blk8 = n // SUBLANES

    def body(x_ref, dh2_ref, du_ref, dzs_ref, dyc_ref, dycn_ref, h_ref, cc_ref, dbc_ref, dzc_ref,
             g_ref, cw_ref, w_ref, gx_ref, dp_ref, dg_ref):
        i = pl.program_id(0)

        @pl.when(i == 0)
        def _():
            dg_ref[...] = jnp.zeros_like(dg_ref)

        dyc = dyc_ref[...]
        last = (i % tiles_per_seq) == tiles_per_seq - 1
        nhalo = jnp.where(last, 0.0, dycn_ref[...])
        dv = (cw_ref[2:3, :] * dyc + cw_ref[1:2, :] * _shift_up(dyc, nhalo, 1)
              + cw_ref[0:1, :] * _shift_up(dyc, nhalo, 2))
        parts = (du_ref[...], dzs_ref[...], dv * cc_ref[...], dbc_ref[...], dv * h_ref[...], dzc_ref[...])
        dxn = jnp.zeros((tm, D_MODEL), F32)
        for k, p in enumerate(parts):
            pb = p.astype(BF16)
            dp_ref[:, k * SSM_W:(k + 1) * SSM_W] = pb
            dxn = dxn + _dot_nt(pb, w_ref[:, k * SSM_W:(k + 1) * SSM_W])
        x = x_ref[...]
        r = lax.rsqrt(jnp.mean(x * x, axis=-1, keepdims=True) + EPS)
        xh = x * r
        dg_ref[...] += jnp.sum(dxn * xh, axis=0, keepdims=True)
        dn = dxn * g_ref[...]
        gx_ref[...] = dh2_ref[...] + r * (dn - xh * jnp.mean(dn * xh, axis=-1, keepdims=True))

    tile_d = pl.BlockSpec((tm, D_MODEL), lambda i: (i, 0))
    tile_s = pl.BlockSpec((tm, SSM_W), lambda i: (i, 0))
    seg_of = lambda c: pl.BlockSpec((tm, SSM_W), lambda i: (i, c))
    nhalo = pl.BlockSpec((SUBLANES, SSM_W), lambda i: (jnp.minimum((i + 1) * rows8, n_blk8 - 1), 0))
    const = lambda shape: pl.BlockSpec(shape, lambda i: (0,) * len(shape))
    return _pcall(
        body, name="in_bwd", grid=(n_tiles,),
        out_shape=(_out((n, D_MODEL), F32), _out((n, IN_COLS), BF16),
                   _out((SUBLANES, D_MODEL), F32)),
        in_specs=[tile_d, tile_d, tile_s, tile_s, tile_s, nhalo, seg_of(SEG_H), seg_of(SEG_CC), tile_s, tile_s,
                  const((1, D_MODEL)), const((SUBLANES, CONV_W)), const((D_MODEL, IN_COLS))],
        out_specs=(tile_d, pl.BlockSpec((tm, IN_COLS), lambda i: (i, 0)), const((SUBLANES, D_MODEL))),
        compiler_params=_params(1),
    )(x2, dh2, du, dzs, dyc, dyc, proj, proj, dbc, dzc, g1, conv8, w_full)


def _dw_in_exchange(order, xn, dproj, smalls):
    n = xn.shape[0]
    tk = 512
    nk = n // tk
    piece = (D_MODEL, COLS_PER_DEV)
    n_small = len(smalls)

    def body(order_ref, xn_hbm, dp_ref, *refs):
        del order_ref
        sm_refs = refs[:n_small]
        own_ref, rchip_ref = refs[n_small:n_small + 2]
        rsm_refs = refs[n_small + 2:2 * n_small + 2]
        (xn_ref, acc, stage, sbuf, xn_sems, give_send, give_recv, keep_send, keep_recv,
         sm_send, sm_recv, sm_loc) = refs[2 * n_small + 2:]
        s = pl.program_id(0)

        def xn_copy(kk):
            rows = pl.ds(pl.multiple_of(kk * tk, tk), tk)
            return pltpu.make_async_copy(xn_hbm.at[rows, :], xn_ref.at[rows, :], xn_sems.at[kk])

        @pl.when(s == 0)
        def _():
            for kk in range(nk):
                xn_copy(kk).start()
            xn_copy(0).wait()

        x, y, c = _mesh_pos()
        sib = (x, y, 1 - c)
        chips = [(1 - x, 1 - y), (1 - x, y), (x, 1 - y)]
        gather = _TwoLevelGather(list(sm_refs), [functools.partial(lambda r, dev: r.at[dev], r) for r in rsm_refs],
                                 sm_send, sm_recv, sm_loc)

        def half(i, core):
            return acc.at[i % 2, :, pl.ds(pl.multiple_of(core * COLS_PER_DEV, LANES), COLS_PER_DEV)]

        def give(i):
            return pltpu.make_async_remote_copy(src_ref=half(i, 1 - c), dst_ref=stage.at[i], send_sem=give_send.at[i],
                                                recv_sem=give_recv.at[i], device_id=sib, device_id_type=MESH)

        def keep(i):
            return pltpu.make_async_remote_copy(src_ref=sbuf.at[i], dst_ref=rchip_ref.at[i], send_sem=keep_send.at[i],
                                                recv_sem=keep_recv.at[i], device_id=(*chips[i], c), device_id_type=MESH)

        def chip_sum(i):
            give(i).wait_recv()
            mine = [acc[i % 2, :, cc * COLS_PER_DEV:(cc + 1) * COLS_PER_DEV] for cc in range(2)]
            return jnp.where(c == 0, mine[0], mine[1]) + stage[i]

        @pl.when(s == 0)
        def _():
            gather.start()

        @pl.when(s == N_CHIP // 2)
        def _():
            gather.forward()

        for k in range(2, N_CHIP):
            @pl.when(s == k)
            def _(k=k):
                give(k - 2).wait_send()

        slot = s % 2
        acc[slot] = _dot_tn(xn_ref[pl.ds(0, tk), :], dp_ref[pl.ds(0, tk), :])

        def kstep(kk, carry):
            @pl.when(s == 0)
            def _():
                xn_copy(kk).wait()

            off = pl.multiple_of(kk * tk, tk)
            acc[slot] += _dot_tn(xn_ref[pl.ds(off, tk), :], dp_ref[pl.ds(off, tk), :])
            return carry

        n_first = min(nk, 3)
        lax.fori_loop(1, n_first, kstep, 0)
        for k in range(1, N_CHIP):
            @pl.when(s == k)
            def _(k=k):
                sbuf[k - 1] = chip_sum(k - 1).astype(BF16)
                keep(k - 1).start()

        lax.fori_loop(n_first, nk, kstep, 0)

        for k in range(N_CHIP):
            @pl.when(s == k)
            def _(k=k):
                give(k).start()

        @pl.when(s == N_CHIP - 1)
        def _():
            own_ref[...] = chip_sum(N_CHIP - 1)
            give(N_CHIP - 2).wait_send()
            give(N_CHIP - 1).wait_send()
            for i in range(3):
                keep(i).wait()
            gather.finish()

    grid_spec = pltpu.PrefetchScalarGridSpec(
        num_scalar_prefetch=1, grid=(N_CHIP,),
        in_specs=[HBM_SPEC,
                  pl.BlockSpec((n, COLS_PER_CHIP), lambda s, order: (0, order[s])),
                  *([HBM_SPEC] * n_small)],
        out_specs=(pl.BlockSpec(piece, lambda s, order: (0, 0)), HBM_SPEC, *([HBM_SPEC] * n_small)),
        scratch_shapes=[pltpu.VMEM((n, D_MODEL), BF16),
                        pltpu.VMEM((2, D_MODEL, COLS_PER_CHIP), F32), pltpu.VMEM((4,) + piece, F32),
                        pltpu.VMEM((3,) + piece, BF16),
                        pltpu.SemaphoreType.DMA((nk,)),
                        pltpu.SemaphoreType.DMA((4,)), pltpu.SemaphoreType.DMA((4,)),
                        pltpu.SemaphoreType.DMA((3,)), pltpu.SemaphoreType.DMA((3,)),
                        pltpu.SemaphoreType.DMA((7 * n_small,)), pltpu.SemaphoreType.DMA((7 * n_small,)),
                        pltpu.SemaphoreType.DMA((n_small,))])
    return _pcall(
        body, name="dw_in_exchange", grid_spec=grid_spec,
        out_shape=(_out(piece, F32), _out((3,) + piece, BF16),
                   *(_out((N_DEV,) + a.shape, a.dtype) for a in smalls)),
        compiler_params=_params(1),
    )(order, xn, dproj, *smalls)


def _adamw(g, w, m, v):
    m_new = ADAM_B1 * m + (1.0 - ADAM_B1) * g
    v_new = ADAM_B2 * v + (1.0 - ADAM_B2) * (g * g)
    m_hat = m_new / (1.0 - ADAM_B1 ** ADAM_STEP)
    v_hat = v_new / (1.0 - ADAM_B2 ** ADAM_STEP)
    delta = -ADAM_LR * (m_hat / (jnp.sqrt(v_hat) + ADAM_EPS) + ADAM_WD * w)
    return delta, m_new, v_new


def _reduce_adam(recv, w, m, v, name, row_tile):
    rows, cols = w.shape

    def body(r_ref, w_ref, m_ref, v_ref, g_ref, d_ref, nm_ref, nv_ref):
        g = r_ref[0]
        for s in range(1, N_DEV):
            g = g + r_ref[s]
        g_ref[...] = g
        d_ref[...], nm_ref[...], nv_ref[...] = _adamw(g, w_ref[...], m_ref[...], v_ref[...])

    tile = pl.BlockSpec((row_tile, cols), lambda i: (i, 0))
    shp = _out((rows, cols), F32)
    return _pcall(
        body, name=name, grid=(rows // row_tile,),
        out_shape=(shp,) * 4,
        in_specs=[pl.BlockSpec((N_DEV, row_tile, cols), lambda i: (0, i, 0)), tile, tile, tile],
        out_specs=(tile,) * 4,
        compiler_params=_params(1),
    )(recv, w, m, v)


def _reduce_adam_w_in(own, rchip, w, m, v):
    rows, cols = w.shape
    row_tile = 256

    def body(o_ref, r_ref, w_ref, m_ref, v_ref, g_ref, d_ref, nm_ref, nv_ref):
        g = o_ref[...]
        for s in range(3):
            g = g + r_ref[s].astype(F32)
        g_ref[...] = g
        d_ref[...], nm_ref[...], nv_ref[...] = _adamw(g, w_ref[...], m_ref[...], v_ref[...])

    tile = pl.BlockSpec((row_tile, cols), lambda i: (i, 0))
    shp = _out((rows, cols), F32)
    return _pcall(
        body, name="reduce_adam_w_in", grid=(rows // row_tile,),
        out_shape=(shp,) * 4,
        in_specs=[tile, pl.BlockSpec((3, row_tile, cols), lambda i: (0, i, 0)), tile, tile, tile],
        out_specs=(tile,) * 4,
        compiler_params=_params(1),
    )(own, rchip, w, m, v)


_SMALL_LEAVES = ("norm_gain", "final_norm_gain", "b_glu", "ssm_a_re", "ssm_a_im", "ssm_log_dt", "ssm_d", "conv_w",
                 "ssm_c_re", "ssm_c_im", "ssm_b_re", "ssm_b_im")


def _reduce_adam_small(r_pack, r_gc, r_gb, wmv):
    n_leaf = len(_SMALL_LEAVES)

    def body(*refs):
        rp_ref, rgc_ref, rgb_ref = refs[:3]
        w_refs = refs[3:3 + 3 * n_leaf]
        loss_ref = refs[3 + 3 * n_leaf]
        o_refs = refs[4 + 3 * n_leaf:4 + 7 * n_leaf]
        own_conv = refs[-1]

        def total(ref):
            acc = ref[0].astype(F32)
            for s in range(1, N_DEV):
                acc = acc + ref[s].astype(F32)
            return acc

        sp = total(rp_ref)
        sgc = total(rgc_ref)
        sgb = total(rgb_ref)
        loss_ref[...] = sp[ROW_LOSS:ROW_LOSS + SUBLANES, 0:LANES]

        def wide(r):
            return jnp.concatenate([sp[r:r + 1, :], sp[r + 1:r + 2, :]], axis=1)

        s5 = slice(ROW_S5, ROW_S5 + N_GROUPS)
        eye = (lax.broadcasted_iota(jnp.int32, (N_GROUPS, N_GROUPS), 0)
               == lax.broadcasted_iota(jnp.int32, (N_GROUPS, N_GROUPS), 1)).astype(F32)
        d_row = sp[ROW_BGLU_D + 1:ROW_BGLU_D + 2, :]
        me = 4 * lax.axis_index("x") + 2 * lax.axis_index("y") + lax.axis_index("c")
        for k in range(N_DEV):
            @pl.when(me == k)
            def _(k=k):
                own_conv[...] = sp[ROW_CONV:ROW_CONV + SUBLANES, k * CONV_COLS_PER_DEV:(k + 1) * CONV_COLS_PER_DEV]
        grads = {
            "norm_gain": wide(ROW_NORM_GAIN),
            "final_norm_gain": wide(ROW_FINAL_GAIN),
            "b_glu": sp[ROW_BGLU_D:ROW_BGLU_D + 1, :],
            "ssm_a_re": sp[s5, LANE_A_RE:LANE_A_RE + STATE],
            "ssm_a_im": sp[s5, LANE_A_IM:LANE_A_IM + STATE],
            "ssm_log_dt": jnp.sum(sp[s5, LANE_LOG_DT:LANE_LOG_DT + 1] * eye, axis=0, keepdims=True),
            "ssm_d": jnp.concatenate([d_row[:, g * GROUP:(g + 1) * GROUP] for g in range(N_GROUPS)], axis=0),
            "conv_w": own_conv[0:3, :],
            "ssm_c_re": sgc[:, 0:STATE],
            "ssm_c_im": sgc[:, STATE:2 * STATE],
            "ssm_b_re": sgb[:, 0:STATE],
            "ssm_b_im": sgb[:, STATE:2 * STATE],
        }
        for i, name in enumerate(_SMALL_LEAVES):
            g = grads[name]
            w_ref, m_ref, v_ref = w_refs[3 * i:3 * i + 3]
            o_g, o_d, o_m, o_v = o_refs[4 * i:4 * i + 4]
            o_g[...] = g
            o_d[...], o_m[...], o_v[...] = _adamw(g, w_ref[...], m_ref[...], v_ref[...])

    flat_w = [a for name in _SMALL_LEAVES for a in wmv[name]]
    leaf_shapes = [_out(wmv[name][0].shape, F32) for name in _SMALL_LEAVES for _ in range(4)]
    operands = (r_pack, r_gc, r_gb, *flat_w)
    out_shape = (_out((SUBLANES, LANES), F32), *leaf_shapes)
    outs = _pcall(
        body, name="reduce_adam_small", grid=(1,), out_shape=out_shape,
        in_specs=_whole_specs(operands), out_specs=tuple(_whole_specs(out_shape)),
        scratch_shapes=[pltpu.VMEM((SUBLANES, CONV_COLS_PER_DEV), F32)],
        compiler_params=_params(1),
    )(*operands)
    leaves = {name: outs[1 + 4 * i:5 + 4 * i] for i, name in enumerate(_SMALL_LEAVES)}
    return outs[0], leaves


def kernel(x, norm_gain, w_in, ssm_a_re, ssm_a_im, ssm_log_dt, ssm_b_re, ssm_b_im, ssm_c_re, ssm_c_im, ssm_d, w_glu, b_glu, conv_w, w_out, final_norm_gain, loss_target, m_norm_gain, m_w_in, m_ssm_a_re, m_ssm_a_im, m_ssm_log_dt, m_ssm_b_re, m_ssm_b_im, m_ssm_c_re, m_ssm_c_im, m_ssm_d, m_w_glu, m_b_glu, m_conv_w, m_w_out, m_final_norm_gain, v_norm_gain, v_w_in, v_ssm_a_re, v_ssm_a_im, v_ssm_log_dt, v_ssm_b_re, v_ssm_b_im, v_ssm_c_re, v_ssm_c_im, v_ssm_d, v_w_glu, v_b_glu, v_conv_w, v_w_out, v_final_norm_gain):
    n_seq, seq, _ = x.shape
    n = n_seq * seq

    gh_p = lambda b4: jnp.transpose(b4, (0, 1, 3, 2)).reshape(N_GROUPS * GROUP, STATE)
    c2 = lambda a: a.reshape(N_GROUPS * GROUP, STATE)
    b_re2, b_im2 = gh_p(ssm_b_re), gh_p(ssm_b_im)
    d_row = ssm_d[0].reshape(1, SSM_W)

    x2 = x.reshape(n, D_MODEL)
    tgt2 = loss_target.reshape(n, D_MODEL)
    mx, my, mc = lax.axis_index("x"), lax.axis_index("y"), lax.axis_index("c")
    chip_ids = [2 * cx + cy for cx, cy in ((mx, my), (1 - mx, my), (mx, 1 - my), (1 - mx, 1 - my))]
    arrival = chip_ids
    xn, proj, w_in_f, s5 = _in_proj(
        jnp.stack(arrival).astype(jnp.int32), x2, norm_gain, w_in[0].astype(BF16),
        (ssm_a_re[0], ssm_a_im[0], ssm_log_dt, b_re2, b_im2, c2(ssm_c_re), c2(ssm_c_im)))
    a_re_x, a_im_x, log_dt_x, ab_re, ab_im, bb_re_m, bb_im_m, c_re_m, c_imn_m = s5
    u3 = proj.reshape(n_seq, seq, IN_COLS)
    conv_p = jnp.pad(conv_w[0], ((0, SUBLANES - 3), (0, LANES - CONV_COLS_PER_DEV)))
    s_re, s_im, y3, w_out_f, w_glu_f, conv_all = _ssm_fwd(
        u3, bb_re_m, bb_im_m, c_re_m, c_imn_m, d_row, ab_re, ab_im,
        w_out[0].astype(BF16), w_glu[0].astype(BF16), conv_p, n_seq, seq)
    conv8 = jnp.transpose(conv_all[:, :, :CONV_COLS_PER_DEV], (1, 0, 2)).reshape(SUBLANES, CONV_W)
    (dh2, dy, dzs, dbc, dzc, dyc, dw_out, dw_glu, loss_t, dgf, dbg, dcw) = _mix(
        x2, tgt2, y3.reshape(n, SSM_W), proj, final_norm_gain.reshape(1, D_MODEL), b_glu, conv8,
        w_glu_f, w_out_f, seq)

    du3, dc_re_d, dc_im_d, dbb_re_d, dbb_im_d, dab_re, dab_im, dd, r_out, r_glu = _ssm_bwd(
        dy.reshape(n_seq, seq, SSM_W), u3, s_re, s_im, bb_re_m, bb_im_m, c_re_m, c_imn_m, d_row, ab_re, ab_im,
        dw_out.reshape(N_DEV, OUT_ROWS_PER_DEV, D_MODEL), dw_glu.reshape(N_DEV, GLU_ROWS_PER_DEV, SSM_W), n_seq, seq)
    du = du3.reshape(n, SSM_W)
    grad_x2, dproj, dg8 = _in_bwd(x2, dh2, du, dzs, dyc, proj, dbc, dzc, norm_gain, conv8, w_in_f, seq)
    pack, gc, gb = _ssm_disc_bwd_pack(
        a_re_x, a_im_x, log_dt_x, b_re2, b_im2, dab_re.reshape(N_GROUPS, STATE), dab_im.reshape(N_GROUPS, STATE),
        dbb_re_d, dbb_im_d, loss_t, dg8, dgf, dbg, dd, dcw, dc_re_d, dc_im_d)

    order = [chip_ids[3], chip_ids[1], chip_ids[2], chip_ids[0]]
    own_in, rchip_in, r_pack, r_gc, r_gb = _dw_in_exchange(
        jnp.stack(order).astype(jnp.int32), xn, dproj, [pack, gc, gb])

    flat2 = lambda a: a.reshape(a.shape[-2:]) if a.ndim > 2 else a.reshape(1, -1)
    c2 = lambda a: a.reshape(N_GROUPS * GROUP, STATE)
    wmv = dict(norm_gain=(norm_gain, m_norm_gain, v_norm_gain),
               final_norm_gain=tuple(flat2(a) for a in (final_norm_gain, m_final_norm_gain, v_final_norm_gain)),
               b_glu=(b_glu, m_b_glu, v_b_glu),
               ssm_a_re=tuple(flat2(a) for a in (ssm_a_re, m_ssm_a_re, v_ssm_a_re)),
               ssm_a_im=tuple(flat2(a) for a in (ssm_a_im, m_ssm_a_im, v_ssm_a_im)),
               ssm_log_dt=(ssm_log_dt, m_ssm_log_dt, v_ssm_log_dt),
               ssm_d=tuple(flat2(a) for a in (ssm_d, m_ssm_d, v_ssm_d)),
               conv_w=tuple(flat2(a) for a in (conv_w, m_conv_w, v_conv_w)),
               ssm_c_re=tuple(c2(a) for a in (ssm_c_re, m_ssm_c_re, v_ssm_c_re)),
               ssm_c_im=tuple(c2(a) for a in (ssm_c_im, m_ssm_c_im, v_ssm_c_im)),
               ssm_b_re=(b_re2, gh_p(m_ssm_b_re), gh_p(v_ssm_b_re)),
               ssm_b_im=(b_im2, gh_p(m_ssm_b_im), gh_p(v_ssm_b_im)))

    res_in = _reduce_adam_w_in(own_in, rchip_in, w_in[0], m_w_in[0], v_w_in[0])
    res_out = _reduce_adam(r_out, w_out[0], m_w_out[0], v_w_out[0], "reduce_adam_w_out", OUT_ROWS_PER_DEV)
    res_glu = _reduce_adam(r_glu, w_glu[0], m_w_glu[0], v_w_glu[0], "reduce_adam_w_glu", GLU_ROWS_PER_DEV)
    loss8, small = _reduce_adam_small(r_pack, r_gc, r_gb, wmv)
    loss = loss8[0, 0]

    shapes = dict(norm_gain=(1, D_MODEL), ssm_a_re=(1, N_GROUPS, STATE), ssm_a_im=(1, N_GROUPS, STATE),
                  ssm_log_dt=(1, N_GROUPS), ssm_c_re=(1, N_GROUPS, GROUP, STATE), ssm_c_im=(1, N_GROUPS, GROUP, STATE),
                  ssm_d=(1, N_GROUPS, GROUP), b_glu=(1, SSM_W), final_norm_gain=(D_MODEL,),
                  conv_w=(1, 3, CONV_COLS_PER_DEV))
    big = dict(w_in=res_in, w_glu=res_glu, w_out=res_out)

    def leaf(kind, name):
        if name in big:
            return big[name][kind][None]
        if name in ("ssm_b_re", "ssm_b_im"):
            return jnp.transpose(small[name][kind].reshape(1, N_GROUPS, GROUP, STATE), (0, 1, 3, 2))
        return small[name][kind].reshape(shapes[name])

    order = ["norm_gain", "w_in", "ssm_a_re", "ssm_a_im", "ssm_log_dt", "ssm_b_re", "ssm_b_im", "ssm_c_re",
             "ssm_c_im", "ssm_d", "w_glu", "b_glu", "conv_w", "w_out", "final_norm_gain"]
    outs = [loss, grad_x2.reshape(x.shape)]
    for kind in range(4):
        outs += [leaf(kind, name) for name in order]
    return tuple(outs)
```

```python
import functools
import math

import jax
import jax.numpy as jnp
from jax import lax
from jax.experimental import pallas as pl
from jax.experimental.pallas import tpu as pltpu

F32 = jnp.float32
BF16 = jnp.bfloat16

N_DEV = 8
D_MODEL = 1024
SSM_W = 512
CONV_W = 512
N_GROUPS = 32
GROUP = 16
STATE = 64
IN_COLS = 3072
SEG_U, SEG_ZS, SEG_H, SEG_BC, SEG_CC, SEG_ZC = range(6)
COLS_PER_DEV = IN_COLS // N_DEV
N_CHIP = N_DEV // 2
COLS_PER_CHIP = 2 * COLS_PER_DEV
OUT_ROWS_PER_DEV = D_MODEL // N_DEV
GLU_ROWS_PER_DEV = SSM_W // N_DEV
CONV_COLS_PER_DEV = CONV_W // N_DEV
EPS = 1e-6

N_JBLK = 4
JB_CH = SSM_W // N_JBLK
JB_ST = N_GROUPS * STATE // N_JBLK

ADAM_LR = 0.001
ADAM_B1 = 0.9
ADAM_B2 = 0.999
ADAM_EPS = 1e-08
ADAM_WD = 0.01
ADAM_STEP = 10

SUBLANES = 8
LANES = 128
VMEM_LIMIT = 48 * 1024 * 1024
TOK_TILE = 256
IN_TILE = 1024
SCAN_TILE = 1024

MESH = pl.DeviceIdType.MESH
HBM_SPEC = pl.BlockSpec(memory_space=pltpu.HBM)


def _build(body, **kw):
    return pl.pallas_call(body, **kw)


def _pcall(body, **kw):
    def call(*operands):
        pinned = [a if jnp.issubdtype(a.dtype, jnp.integer) else pltpu.with_memory_space_constraint(a, pltpu.HBM)
                  for a in operands]
        return _build(body, **kw)(*pinned)
    return call


def _whole_specs(arrays):
    return [pl.BlockSpec(a.shape, functools.partial(lambda nd, i: (0,) * nd, len(a.shape))) for a in arrays]


def _out(shape, dtype):
    return pltpu.HBM(tuple(shape), dtype)


def _params(n_grid):
    return pltpu.CompilerParams(dimension_semantics=("arbitrary",) * n_grid,
                                vmem_limit_bytes=VMEM_LIMIT)


def _dot(a, b):
    return jnp.dot(a, b, preferred_element_type=F32)


def _dot_nt(a, b):
    return lax.dot_general(a, b, (((1,), (1,)), ((), ())), preferred_element_type=F32)


def _dot_tn(a, b):
    return lax.dot_general(a, b, (((0,), (0,)), ((), ())), preferred_element_type=F32)


def _sigmoid(z):
    return 1.0 / (1.0 + jnp.exp(-z))


_GELU_C = math.sqrt(2.0 / math.pi)


def _gelu_and_grad(y):
    inner = _GELU_C * (y + 0.044715 * (y * y * y))
    t = jnp.tanh(inner)
    g = 0.5 * y * (1.0 + t)
    dg = 0.5 * (1.0 + t) + 0.5 * y * (1.0 - t * t) * (_GELU_C * (1.0 + 3.0 * 0.044715 * (y * y)))
    return g, dg


def _silu_and_grad(z):
    s = _sigmoid(z)
    return z * s, s * (1.0 + z * (1.0 - s))


def _shift_down(v, halo, k):
    rolled = pltpu.roll(v, k, 0)
    row = lax.broadcasted_iota(jnp.int32, v.shape, 0)
    for r in range(k):
        rolled = jnp.where(row == r, halo[SUBLANES - k + r:SUBLANES - k + r + 1, :], rolled)
    return rolled


def _shift_up(v, halo, k):
    n = v.shape[0]
    rolled = pltpu.roll(v, n - k, 0)
    row = lax.broadcasted_iota(jnp.int32, v.shape, 0)
    for r in range(k):
        rolled = jnp.where(row == n - k + r, halo[r:r + 1, :], rolled)
    return rolled


def _mesh_pos():
    return lax.axis_index("x"), lax.axis_index("y"), lax.axis_index("c")


def _direct_copies(srcs_for, out_refs, send_sems, recv_sems, loc_sems):
    x, y, c = _mesh_pos()
    me_id = 4 * x + 2 * y + c
    n_arr = len(out_refs)
    dsts = [r.at[me_id] for r in out_refs]
    own = srcs_for(me_id)
    mine = [pltpu.make_async_copy(own[a], dsts[a], loc_sems.at[a]) for a in range(n_arr)]
    sends = []
    for k in range(1, N_DEV):
        px, py, pc = x ^ ((k >> 2) & 1), y ^ ((k >> 1) & 1), c ^ (k & 1)
        src = srcs_for(4 * px + 2 * py + pc)
        for a in range(n_arr):
            sends.append(pltpu.make_async_remote_copy(
                src_ref=src[a], dst_ref=dsts[a],
                send_sem=send_sems.at[(k - 1) * n_arr + a], recv_sem=recv_sems.at[(k - 1) * n_arr + a],
                device_id=(px, py, pc), device_id_type=MESH))
    return mine, sends


class _TwoLevelGather:
    def __init__(self, srcs, slots, send_sems, recv_sems, loc_sems):
        self.srcs, self.slots, self.n_arr = srcs, slots, len(srcs)
        self.send_sems, self.recv_sems, self.loc_sems = send_sems, recv_sems, loc_sems
        x, y, c = _mesh_pos()
        self.c = c
        self.me, self.sib = (x, y, c), (x, y, 1 - c)
        self.chips = [(1 - x, y), (x, 1 - y), (1 - x, 1 - y)]

    def _copies(self, k, block, to, from_src=False):
        dev = 4 * block[0] + 2 * block[1] + block[2]
        return [pltpu.make_async_remote_copy(
            src_ref=self.srcs[a] if from_src else self.slots[a](dev), dst_ref=self.slots[a](dev),
            send_sem=self.send_sems.at[k * self.n_arr + a], recv_sem=self.recv_sems.at[k * self.n_arr + a],
            device_id=to, device_id_type=MESH) for a in range(self.n_arr)]

    def _local(self):
        dev = 4 * self.me[0] + 2 * self.me[1] + self.me[2]
        return [pltpu.make_async_copy(self.srcs[a], self.slots[a](dev), self.loc_sems.at[a])
                for a in range(self.n_arr)]

    def start(self, chips=(0, 1, 2)):
        for cp in self._local() + self._copies(0, self.me, self.sib, True):
            cp.start()
        self.start_to(chips)

    def start_to(self, chips):
        for j in chips:
            for cp in self._copies(1 + j, self.me, (*self.chips[j], self.c), True):
                cp.start()

    def wait_own(self):
        for cp in self._local():
            cp.wait()

    def wait_sibling(self):
        for cp in self._copies(0, self.sib, self.me):
            cp.wait_recv()

    def wait_and_pass_on(self, j):
        chip = self.chips[j]
        for cp in self._copies(1 + j, (*chip, self.c), self.me):
            cp.wait_recv()
        for cp in self._copies(4 + j, (*chip, self.c), self.sib):
            cp.start()

    def wait_passed_on(self, j):
        for cp in self._copies(4 + j, (*self.chips[j], 1 - self.c), self.me):
            cp.wait_recv()

    def wait_sends(self):
        for cp in self._copies(0, self.me, self.sib, True):
            cp.wait_send()
        for j, chip in enumerate(self.chips):
            for cp in self._copies(1 + j, self.me, (*chip, self.c), True) + self._copies(4 + j, (*chip, self.c), self.sib):
                cp.wait_send()

    def forward(self):
        for j in range(3):
            self.wait_and_pass_on(j)

    def finish(self):
        self.wait_sibling()
        for j in range(3):
            self.wait_passed_on(j)
        self.wait_sends()
        self.wait_own()


def _disc(a_re, a_im, log_dt, b_re, b_im):
    dt = jnp.exp(log_dt)
    mag = jnp.exp(a_re * dt)
    ab_re = mag * jnp.cos(a_im * dt)
    ab_im = mag * jnp.sin(a_im * dt)
    den = a_re * a_re + a_im * a_im
    p_re = ab_re - 1.0
    p_im = ab_im
    q_re = (p_re * a_re + p_im * a_im) / den
    q_im = (p_im * a_re - p_re * a_im) / den
    bb_re = q_re * b_re - q_im * b_im
    bb_im = q_re * b_im + q_im * b_re
    return ab_re, ab_im, bb_re, bb_im


def _split3(v):
    hi = v.astype(BF16)
    r1 = v - hi.astype(F32)
    mid = r1.astype(BF16)
    lo = (r1 - mid.astype(F32)).astype(BF16)
    return hi, mid, lo


def _select_dot(sel, v):
    return sum(_dot(sel, t) for t in _split3(v))


PACK_ROWS = 72
PACK_W = 512
ROW_FINAL_GAIN, ROW_NORM_GAIN, ROW_BGLU_D, ROW_CONV, ROW_LOSS, ROW_S5 = 0, 8, 16, 24, 32, 40
LANE_A_RE, LANE_A_IM, LANE_LOG_DT = 0, 128, 256


def _ssm_disc_bwd_pack(a_re_x, a_im_x, log_dt_x, b_re, b_im, g_ab_re, g_ab_im, dbb_re_d, dbb_im_d,
                       loss_t, dg8, dgf, dbg, dd, dcw, dc_re_d, dc_im_d):
    rows_gh = N_GROUPS * GROUP

    def body(are, aim, ldt, bre, bim, gabre, gabim, dbbre_ref, dbbim_ref,
             loss_ref, dg8_ref, dgf_ref, dbg_ref, dd_ref, dcw_ref, dcre_ref, dcim_ref,
             p_ref, gc_ref, gb_ref, gbb_re, gbb_im):
        r_g = lax.broadcasted_iota(jnp.int32, (N_GROUPS, rows_gh), 0)
        c_gh = lax.broadcasted_iota(jnp.int32, (N_GROUPS, rows_gh), 1)
        group_sum = (c_gh // GROUP == r_g).astype(BF16)
        r_gh = lax.broadcasted_iota(jnp.int32, (rows_gh, N_GROUPS), 0)
        c_g = lax.broadcasted_iota(jnp.int32, (rows_gh, N_GROUPS), 1)
        first_row = (r_gh == c_g * GROUP).astype(BF16)

        def diag_block(ref, j, gi):
            return ref[j, gi * GROUP:(gi + 1) * GROUP, gi * STATE:(gi + 1) * STATE]

        for j in range(N_JBLK):
            for gi in range(SUBLANES):
                r0 = (j * SUBLANES + gi) * GROUP
                gbb_re[r0:r0 + GROUP, :] = diag_block(dbbre_ref, j, gi)
                gbb_im[r0:r0 + GROUP, :] = diag_block(dbbim_ref, j, gi)
                both = jnp.concatenate([diag_block(dcre_ref, j, gi), -diag_block(dcim_ref, j, gi)], axis=1)
                gc_ref[r0:r0 + GROUP, :] = both.astype(BF16)

        _, vjp = jax.vjp(_disc, are[...], aim[...], ldt[...], bre[...], bim[...])
        d_are, d_aim, d_ldt, d_bre, d_bim = vjp((_select_dot(first_row, gabre[...]), _select_dot(first_row, gabim[...]),
                                                 gbb_re[...], gbb_im[...]))
        gb_ref[...] = jnp.concatenate([d_bre, d_bim], axis=1).astype(BF16)

        p_ref[...] = jnp.zeros_like(p_ref)
        half = D_MODEL // 2
        for r, src in ((ROW_FINAL_GAIN, dgf_ref), (ROW_NORM_GAIN, dg8_ref)):
            p_ref[r:r + 1, :] = src[0:1, 0:half]
            p_ref[r + 1:r + 2, :] = src[0:1, half:D_MODEL]
        p_ref[ROW_BGLU_D:ROW_BGLU_D + 1, :] = dbg_ref[...]
        p_ref[ROW_BGLU_D + 1:ROW_BGLU_D + 2, :] = dd_ref[...]
        p_ref[ROW_CONV:ROW_CONV + SUBLANES, :] = dcw_ref[...]
        p_ref[ROW_LOSS:ROW_LOSS + SUBLANES, 0:LANES] = loss_ref[...]
        s5 = slice(ROW_S5, ROW_S5 + N_GROUPS)
        p_ref[s5, LANE_A_RE:LANE_A_RE + STATE] = _select_dot(group_sum, d_are)
        p_ref[s5, LANE_A_IM:LANE_A_IM + STATE] = _select_dot(group_sum, d_aim)
        p_ref[s5, LANE_LOG_DT:LANE_LOG_DT + LANES] = _select_dot(group_sum, jnp.broadcast_to(d_ldt, (rows_gh, LANES)))

    operands = (a_re_x, a_im_x, log_dt_x, b_re, b_im, g_ab_re, g_ab_im, dbb_re_d, dbb_im_d,
                loss_t, dg8, dgf, dbg, dd, dcw, dc_re_d, dc_im_d)
    out_shape = (_out((PACK_ROWS, PACK_W), F32),
                 _out((rows_gh, 2 * STATE), BF16),
                 _out((rows_gh, 2 * STATE), BF16))
    return _pcall(body, name="ssm_disc_bwd_pack", grid=(1,), out_shape=out_shape,
                  in_specs=_whole_specs(operands), out_specs=tuple(_whole_specs(out_shape)),
                  scratch_shapes=[pltpu.VMEM((rows_gh, STATE), F32), pltpu.VMEM((rows_gh, STATE), F32)],
                  compiler_params=_params(1))(*operands)


def _s5_prepare(are, aim, ldt, bre, bim, cre, cim,
                o_ax_re, o_ax_im, o_ldt_x, o_ab_re, o_ab_im, o_bb_re, o_bb_im, o_c_re, o_c_imn):
    rows_gh = N_GROUPS * GROUP
    rep = (lax.broadcasted_iota(jnp.int32, (rows_gh, N_GROUPS), 0) // GROUP
           == lax.broadcasted_iota(jnp.int32, (rows_gh, N_GROUPS), 1)).astype(BF16)
    eye = (lax.broadcasted_iota(jnp.int32, (N_GROUPS, N_GROUPS), 0)
           == lax.broadcasted_iota(jnp.int32, (N_GROUPS, N_GROUPS), 1)).astype(F32)
    ldt_col = jnp.sum(eye * ldt[...], axis=1, keepdims=True)
    a_re_x = _select_dot(rep, are[...])
    a_im_x = _select_dot(rep, aim[...])
    ldt_x = _select_dot(rep, jnp.broadcast_to(ldt_col, (N_GROUPS, LANES)))[:, 0:1]
    o_ax_re[...] = a_re_x
    o_ax_im[...] = a_im_x
    o_ldt_x[...] = ldt_x
    ab_re, ab_im, bb_re, bb_im = _disc(a_re_x, a_im_x, ldt_x, bre[...], bim[...])
    for j in range(N_JBLK):
        first = [(j * SUBLANES + gi) * GROUP for gi in range(SUBLANES)]
        o_ab_re[j] = jnp.concatenate([ab_re[r:r + 1, :] for r in first], axis=1)
        o_ab_im[j] = jnp.concatenate([ab_im[r:r + 1, :] for r in first], axis=1)
    for o, v in ((o_bb_re, bb_re), (o_bb_im, bb_im), (o_c_re, cre[...]), (o_c_imn, -cim[...])):
        for j in range(N_JBLK):
            for gi in range(SUBLANES):
                r0 = (j * SUBLANES + gi) * GROUP
                parts = [v[r0:r0 + GROUP, :] if k == gi else jnp.zeros((GROUP, STATE), F32) for k in range(SUBLANES)]
                o[j, gi * GROUP:(gi + 1) * GROUP, :] = jnp.concatenate(parts, axis=1).astype(BF16)


def _in_proj(order, x2, g1, w_in_b, s5):
    n = x2.shape[0]
    tm = min(IN_TILE, n)
    n_tiles = n // tm
    n_s5_in = len(s5)
    n_s5_out = 9

    def body(order_ref, x_ref, g_ref, w_ref, *refs):
        s5_in = refs[:n_s5_in]
        xn_ref, proj_ref, wall_ref = refs[n_s5_in:n_s5_in + 3]
        s5_out = refs[n_s5_in + 3:n_s5_in + 3 + n_s5_out]
        xn_scr, wbuf, send_sems, recv_sems, loc_sems, out_sems = refs[n_s5_in + 3 + n_s5_out:]
        k = pl.program_id(0)
        i = pl.program_id(1)

        def slot(dev):
            return wbuf.at[dev // 2, :, pl.ds(pl.multiple_of((dev % 2) * COLS_PER_DEV, LANES), COLS_PER_DEV)]

        gather = _TwoLevelGather([w_ref], [slot], send_sems, recv_sems, loc_sems)

        @pl.when((k == 0) & (i == 0))
        def _():
            gather.start(chips=(0, 1))

        def own_chip():
            gather.wait_own()
            gather.wait_sibling()

        def other_chip(j):
            gather.wait_and_pass_on(j)
            if j == 0:
                gather.start_to((2,))
            gather.wait_passed_on(j)

        arrivals = [own_chip] + [functools.partial(other_chip, j) for j in range(3)]
        for kk, arrived in enumerate(arrivals):
            @pl.when((k == kk) & (i == 0))
            def _(arrived=arrived):
                arrived()

        rows = pl.ds(pl.multiple_of(i * tm, tm), tm)

        @pl.when(k == 0)
        def _():
            x = x_ref[...]
            r = lax.rsqrt(jnp.mean(x * x, axis=-1, keepdims=True) + EPS)
            xn = ((x * r) * g_ref[...]).astype(BF16)
            xn_scr[rows, :] = xn
            xn_ref[...] = xn

        proj_ref[...] = _dot(xn_scr[rows, :], wbuf[order_ref[k]])

        @pl.when((k == 0) & (i == n_tiles - 1))
        def _():
            _s5_prepare(*s5_in, *s5_out)

        @pl.when((k == N_CHIP - 1) & (i == n_tiles - 1))
        def _():
            gather.wait_sends()
            outs = [pltpu.make_async_copy(wbuf.at[q], wall_ref.at[:, q * COLS_PER_CHIP:(q + 1) * COLS_PER_CHIP],
                                          out_sems.at[q]) for q in range(N_CHIP)]
            for cp in outs:
                cp.start()
            for cp in outs:
                cp.wait()

    tile_once = lambda k, i, order: (jnp.where(k == 0, i, n_tiles - 1), 0)
    whole = lambda shape: pl.BlockSpec(shape, lambda k, i, order: (0,) * len(shape))
    rows_gh = N_GROUPS * GROUP
    s5_out_shapes = ([(rows_gh, STATE), F32], [(rows_gh, STATE), F32], [(rows_gh, 1), F32],
                     [(N_JBLK, 1, JB_ST), F32], [(N_JBLK, 1, JB_ST), F32]) + ([(N_JBLK, JB_CH, JB_ST), BF16],) * 4
    grid_spec = pltpu.PrefetchScalarGridSpec(
        num_scalar_prefetch=1, grid=(N_CHIP, n_tiles),
        in_specs=[pl.BlockSpec((tm, D_MODEL), tile_once),
                  whole((1, D_MODEL)),
                  HBM_SPEC,
                  *(whole(a.shape) for a in s5)],
        out_specs=(pl.BlockSpec((tm, D_MODEL), tile_once),
                   pl.BlockSpec((tm, COLS_PER_CHIP), lambda k, i, order: (i, order[k])),
                   HBM_SPEC,
                   *(whole(shape) for shape, _ in s5_out_shapes)),
        scratch_shapes=[pltpu.VMEM((n, D_MODEL), BF16), pltpu.VMEM((N_CHIP, D_MODEL, COLS_PER_CHIP), BF16),
                        pltpu.SemaphoreType.DMA((7,)), pltpu.SemaphoreType.DMA((7,)), pltpu.SemaphoreType.DMA((1,)),
                        pltpu.SemaphoreType.DMA((N_CHIP,))])
    outs = _pcall(
        body, name="in_proj", grid_spec=grid_spec,
        out_shape=(_out((n, D_MODEL), BF16), _out((n, IN_COLS), F32),
                   _out((D_MODEL, IN_COLS), BF16),
                   *(_out(shape, dt) for shape, dt in s5_out_shapes)),
        compiler_params=_params(2),
    )(order, x2, g1, w_in_b, *s5)
    return outs[0], outs[1], outs[2], outs[3:]


def _cmul(p, q):
    return p[0] * q[0] - p[1] * q[1], p[0] * q[1] + p[1] * q[0]


def _scan_tables(ar, ai, width, reverse):
    pows = [(ar, ai)]
    for _ in range(SUBLANES - 1):
        pows.append(_cmul(pows[-1], (ar, ai)))
    row = lax.broadcasted_iota(jnp.int32, (SUBLANES, width), 0)

    def bc(v):
        return jnp.broadcast_to(v, (SUBLANES, width))

    levels = []
    for k in (1, 2, 4):
        keep = (row <= SUBLANES - 1 - k) if reverse else (row >= k)
        levels.append((jnp.where(keep, bc(pows[k - 1][0]), 0.0), jnp.where(keep, bc(pows[k - 1][1]), 0.0)))
    cre = jnp.zeros((SUBLANES, width), F32)
    cim = jnp.zeros((SUBLANES, width), F32)
    for r in range(SUBLANES):
        e = (SUBLANES - r) if reverse else (r + 1)
        cre = jnp.where(row == r, bc(pows[e - 1][0]), cre)
        cim = jnp.where(row == r, bc(pows[e - 1][1]), cim)
    return levels, (cre, cim)


def _load_chunked(src_ref, b, dst_ref, n_rows):
    n_blk = n_rows // SUBLANES
    for i in range(n_blk):
        dst_ref[b, i * SUBLANES:(i + 1) * SUBLANES, :] = src_ref[b, pl.ds(i, SUBLANES, stride=n_blk), :]


def _store_chunked(val, dst_ref, b, n_rows):
    n_blk = n_rows // SUBLANES
    for i in range(n_blk):
        dst_ref[b, pl.ds(i, SUBLANES, stride=n_blk), :] = val[i * SUBLANES:(i + 1) * SUBLANES, :]


def _chunk_scan(re_ref, im_ref, bs, car_ref, ar, ai, n_rows, reverse, on_block=None):
    width = re_ref.shape[2]
    n_blk = n_rows // SUBLANES
    shape = (SUBLANES, width)
    abr = jnp.broadcast_to(ar, shape)
    abi = jnp.broadcast_to(ai, shape)
    order = list(range(n_blk - 1, -1, -1)) if reverse else list(range(n_blk))

    def blk(ref, b, i):
        return ref[b, i * SUBLANES:(i + 1) * SUBLANES, :]

    def step(state, b, i):
        sr, si = state
        return abr * sr - abi * si + blk(re_ref, b, i), abr * si + abi * sr + blk(im_ref, b, i)

    finals = {b: (blk(re_ref, b, order[0]), blk(im_ref, b, order[0])) for b in bs}
    for i in order[1:]:
        for b in bs:
            finals[b] = step(finals[b], b, i)

    mr, mi = ar, ai
    for _ in range(n_blk.bit_length() - 1):
        mr, mi = _cmul((mr, mi), (mr, mi))
    levels, _ = _scan_tables(mr, mi, width, reverse)
    mbr = jnp.broadcast_to(mr, shape)
    mbi = jnp.broadcast_to(mi, shape)
    row = lax.broadcasted_iota(jnp.int32, shape, 0)
    edge_in = SUBLANES - 1 if reverse else 0
    edge_out = 0 if reverse else SUBLANES - 1
    sh1 = SUBLANES - 1 if reverse else 1
    states = {}
    for b in bs:
        fr, fi = finals[b]
        gr = jnp.where(row == edge_in, jnp.broadcast_to(car_ref[b, 0:1, :], shape), pltpu.roll(fr, sh1, 0))
        gi = jnp.where(row == edge_in, jnp.broadcast_to(car_ref[b, 1:2, :], shape), pltpu.roll(fi, sh1, 0))
        for (lr, li), k in zip(levels, (1, 2, 4)):
            sh = (SUBLANES - k) if reverse else k
            sr = pltpu.roll(gr, sh, 0)
            si = pltpu.roll(gi, sh, 0)
            gr, gi = gr + (lr * sr - li * si), gi + (lr * si + li * sr)
        car_ref[b, 0:1, :] = (fr + (mbr * gr - mbi * gi))[edge_out:edge_out + 1, :]
        car_ref[b, 1:2, :] = (fi + (mbr * gi + mbi * gr))[edge_out:edge_out + 1, :]
        states[b] = (gr, gi)

    for i in order:
        for b in bs:
            states[b] = step(states[b], b, i)
            re_ref[b, i * SUBLANES:(i + 1) * SUBLANES, :] = states[b][0]
            im_ref[b, i * SUBLANES:(i + 1) * SUBLANES, :] = states[b][1]
            if on_block is not None:
                on_block(b, i, *states[b])


def _ssm_fwd(u, bb_re, bb_im, c_re_t, c_imn_t, d_row, ab_re, ab_im, w_out_b, w_glu_b, conv_p, n_seq, seq):
    tt = min(SCAN_TILE, seq)
    nt = seq // tt

    def body(u_ref, bbre, bbim, cre, cimn, d_ref, are, aim, wout_ref, wglu_ref, cw_ref,
             sre_ref, sim_ref, y_ref, oout_ref, oglu_ref, ocw_ref,
             up_ref, car_ref, send_sems, recv_sems, loc_sems):
        j = pl.program_id(0)
        t = pl.program_id(1)
        gather = _TwoLevelGather(
            [wout_ref, wglu_ref, cw_ref],
            [lambda dev: oout_ref.at[pl.ds(pl.multiple_of(dev * OUT_ROWS_PER_DEV, OUT_ROWS_PER_DEV), OUT_ROWS_PER_DEV), :],
             lambda dev: oglu_ref.at[pl.ds(pl.multiple_of(dev * GLU_ROWS_PER_DEV, GLU_ROWS_PER_DEV), GLU_ROWS_PER_DEV), :],
             lambda dev: ocw_ref.at[dev]],
            send_sems, recv_sems, loc_sems)

        @pl.when((j == 0) & (t == 0))
        def _():
            gather.start()

        @pl.when((j == N_JBLK // 2) & (t == 0))
        def _():
            gather.forward()

        @pl.when(t == 0)
        def _():
            car_ref[...] = jnp.zeros_like(car_ref)

        bs = list(range(n_seq))
        for b in bs:
            _load_chunked(u_ref, b, up_ref, tt)
        for b in bs:
            ub = up_ref[b].astype(BF16)
            sre_ref[b] = _dot(ub, bbre[0])
            sim_ref[b] = _dot(ub, bbim[0])
            _chunk_scan(sre_ref, sim_ref, [b], car_ref, are[0], aim[0], tt, reverse=False)
        for b in bs:
            yp = (_dot_nt(sre_ref[b].astype(BF16), cre[0]) + _dot_nt(sim_ref[b].astype(BF16), cimn[0])
                  + d_ref[...] * up_ref[b])
            _store_chunked(yp, y_ref, b, tt)

        @pl.when((j == N_JBLK - 1) & (t == nt - 1))
        def _():
            gather.finish()

    tok = lambda j, t: (0, t, j)
    blk3 = lambda j, t: (j, 0, 0)
    row = lambda j, t: (0, j)
    st = _out((n_seq, seq, N_JBLK * JB_ST), F32)
    n_arr = 3
    return _pcall(
        body, name="ssm_fwd", grid=(N_JBLK, nt),
        out_shape=(st, st, _out((n_seq, seq, SSM_W), F32),
                   _out((D_MODEL, D_MODEL), BF16), _out((SSM_W, SSM_W), BF16),
                   _out((N_DEV, SUBLANES, LANES), F32)),
        in_specs=[pl.BlockSpec((n_seq, tt, JB_CH), tok),
                  pl.BlockSpec((1, JB_CH, JB_ST), blk3), pl.BlockSpec((1, JB_CH, JB_ST), blk3),
                  pl.BlockSpec((1, JB_CH, JB_ST), blk3), pl.BlockSpec((1, JB_CH, JB_ST), blk3),
                  pl.BlockSpec((1, JB_CH), row), pl.BlockSpec((1, 1, JB_ST), blk3), pl.BlockSpec((1, 1, JB_ST), blk3),
                  HBM_SPEC, HBM_SPEC, HBM_SPEC],
        out_specs=(pl.BlockSpec((n_seq, tt, JB_ST), tok), pl.BlockSpec((n_seq, tt, JB_ST), tok),
                   pl.BlockSpec((n_seq, tt, JB_CH), tok), HBM_SPEC, HBM_SPEC, HBM_SPEC),
        scratch_shapes=[pltpu.VMEM((n_seq, tt, JB_CH), F32), pltpu.VMEM((n_seq, SUBLANES, JB_ST), F32),
                        pltpu.SemaphoreType.DMA((7 * n_arr,)), pltpu.SemaphoreType.DMA((7 * n_arr,)),
                        pltpu.SemaphoreType.DMA((n_arr,))],
        compiler_params=_params(2),
    )(u, bb_re, bb_im, c_re_t, c_imn_t, d_row, ab_re, ab_im, w_out_b, w_glu_b, conv_p)


def _ssm_bwd(dy, u, s_re, s_im, bb_re, bb_im, c_re_t, c_imn_t, d_row, ab_re, ab_im, g_out, g_glu, n_seq, seq):
    tt = min(SCAN_TILE, seq)
    nt = seq // tt
    rows8 = tt // SUBLANES

    def body(dy_ref, u_ref, sre_ref, sim_ref, pre_ref, pim_ref, bbre, bbim, cre, cimn, d_ref, are, aim,
             gout_ref, gglu_ref,
             du_ref, dcre_ref, dcim_ref, dbbre_ref, dbbim_ref, dare_ref, daim_ref, dd_ref, rout_ref, rglu_ref,
             lre_ref, lim_ref, dyp_ref, up_ref, car_ref, send_sems, recv_sems, loc_sems):
        j = pl.program_id(0)
        tr = pl.program_id(1)

        def exchange():
            return _direct_copies(lambda pid: [gout_ref.at[pid], gglu_ref.at[pid]], [rout_ref, rglu_ref],
                                  send_sems, recv_sems, loc_sems)

        @pl.when((j == 0) & (tr == 0))
        def _():
            mine, sends = exchange()
            for cp in mine + sends:
                cp.start()

        @pl.when(tr == 0)
        def _():
            car_ref[...] = jnp.zeros_like(car_ref)
            for r in (dcre_ref, dcim_ref, dbbre_ref, dbbim_ref, dare_ref, daim_ref, dd_ref):
                r[...] = jnp.zeros_like(r)

        first = tr == nt - 1
        row = lax.broadcasted_iota(jnp.int32, (SUBLANES, JB_ST), 0)
        n_blk = tt // SUBLANES
        bs = list(range(n_seq))
        for b in bs:
            _load_chunked(dy_ref, b, dyp_ref, tt)
            _load_chunked(u_ref, b, up_ref, tt)
        for b in bs:
            dyb = dyp_ref[b].astype(BF16)
            lre_ref[b] = _dot(dyb, cre[0])
            lim_ref[b] = _dot(dyb, cimn[0])
        acc = {b: [jnp.zeros((SUBLANES, JB_ST), F32), jnp.zeros((SUBLANES, JB_ST), F32)] for b in bs}

        def on_block(b, i, lr, li):
            if i > 0:
                spr = sre_ref[b, (i - 1) * SUBLANES:i * SUBLANES, :]
                spi = sim_ref[b, (i - 1) * SUBLANES:i * SUBLANES, :]
            else:
                hr = jnp.where(first, 0.0, pre_ref[b, SUBLANES - 1:SUBLANES, :])
                hi = jnp.where(first, 0.0, pim_ref[b, SUBLANES - 1:SUBLANES, :])
                last_r = sre_ref[b, (n_blk - 1) * SUBLANES:n_blk * SUBLANES, :]
                last_i = sim_ref[b, (n_blk - 1) * SUBLANES:n_blk * SUBLANES, :]
                spr = jnp.where(row == 0, jnp.broadcast_to(hr, row.shape), pltpu.roll(last_r, 1, 0))
                spi = jnp.where(row == 0, jnp.broadcast_to(hi, row.shape), pltpu.roll(last_i, 1, 0))
            acc[b][0] = acc[b][0] + (lr * spr + li * spi)
            acc[b][1] = acc[b][1] + (li * spr - lr * spi)

        _chunk_scan(lre_ref, lim_ref, bs, car_ref, are[0], -aim[0], tt, reverse=True, on_block=on_block)
        for b in bs:
            dare_ref[...] += jnp.sum(acc[b][0], axis=0, keepdims=True)
            daim_ref[...] += jnp.sum(acc[b][1], axis=0, keepdims=True)
            dyp = dyp_ref[b]
            up = up_ref[b]
            dyb = dyp.astype(BF16)
            ub = up.astype(BF16)
            lrb = lre_ref[b].astype(BF16)
            lib = lim_ref[b].astype(BF16)
            dup = d_ref[...] * dyp + _dot_nt(lrb, bbre[0]) + _dot_nt(lib, bbim[0])
            _store_chunked(dup, du_ref, b, tt)
            dbbre_ref[0] += _dot_tn(ub, lrb)
            dbbim_ref[0] += _dot_tn(ub, lib)
            dcre_ref[0] += _dot_tn(dyb, sre_ref[b].astype(BF16))
            dcim_ref[0] += _dot_tn(dyb, sim_ref[b].astype(BF16))
            dd_ref[...] += jnp.sum(dyp * up, axis=0, keepdims=True)

        @pl.when((j == N_JBLK - 1) & (tr == nt - 1))
        def _():
            mine, sends = exchange()
            for cp in sends + mine:
                cp.wait()

    tok = lambda j, t: (0, nt - 1 - t, j)
    halo = lambda j, t: (0, jnp.maximum((nt - 1 - t) * rows8 - 1, 0), j)
    blk3 = lambda j, t: (j, 0, 0)
    row1 = lambda j, t: (0, j)
    acc_shape = _out((N_JBLK, JB_CH, JB_ST), F32)
    return _pcall(
        body, name="ssm_bwd", grid=(N_JBLK, nt),
        out_shape=(_out((n_seq, seq, SSM_W), F32), acc_shape, acc_shape, acc_shape, acc_shape,
                   _out((1, N_JBLK * JB_ST), F32), _out((1, N_JBLK * JB_ST), F32),
                   _out((1, SSM_W), F32),
                   _out((N_DEV,) + g_out.shape[1:], F32),
                   _out((N_DEV,) + g_glu.shape[1:], F32)),
        in_specs=[pl.BlockSpec((n_seq, tt, JB_CH), tok), pl.BlockSpec((n_seq, tt, JB_CH), tok),
                  pl.BlockSpec((n_seq, tt, JB_ST), tok), pl.BlockSpec((n_seq, tt, JB_ST), tok),
                  pl.BlockSpec((n_seq, SUBLANES, JB_ST), halo), pl.BlockSpec((n_seq, SUBLANES, JB_ST), halo),
                  pl.BlockSpec((1, JB_CH, JB_ST), blk3), pl.BlockSpec((1, JB_CH, JB_ST), blk3),
                  pl.BlockSpec((1, JB_CH, JB_ST), blk3), pl.BlockSpec((1, JB_CH, JB_ST), blk3),
                  pl.BlockSpec((1, JB_CH), row1), pl.BlockSpec((1, 1, JB_ST), blk3), pl.BlockSpec((1, 1, JB_ST), blk3),
                  HBM_SPEC, HBM_SPEC],
        out_specs=(pl.BlockSpec((n_seq, tt, JB_CH), tok),
                   pl.BlockSpec((1, JB_CH, JB_ST), blk3), pl.BlockSpec((1, JB_CH, JB_ST), blk3),
                   pl.BlockSpec((1, JB_CH, JB_ST), blk3), pl.BlockSpec((1, JB_CH, JB_ST), blk3),
                   pl.BlockSpec((1, JB_ST), row1), pl.BlockSpec((1, JB_ST), row1), pl.BlockSpec((1, JB_CH), row1),
                   HBM_SPEC, HBM_SPEC),
        scratch_shapes=[pltpu.VMEM((n_seq, tt, JB_ST), F32), pltpu.VMEM((n_seq, tt, JB_ST), F32),
                        pltpu.VMEM((n_seq, tt, JB_CH), F32), pltpu.VMEM((n_seq, tt, JB_CH), F32),
                        pltpu.VMEM((n_seq, SUBLANES, JB_ST), F32),
                        pltpu.SemaphoreType.DMA((7 * 2,)), pltpu.SemaphoreType.DMA((7 * 2,)),
                        pltpu.SemaphoreType.DMA((2,))],
        compiler_params=_params(2),
    )(dy, u, s_re, s_im, s_re, s_im, bb_re, bb_im, c_re_t, c_imn_t, d_row, ab_re, ab_im, g_out, g_glu)


def _mix(x2, tgt2, y, proj, gf, b_glu, conv8, w_glu_f, w_out_f, seq):
    n = x2.shape[0]
    tm = TOK_TILE
    tiles_per_seq = seq // tm
    rows8 = tm // SUBLANES

    def body(x_ref, t_ref, y_ref, zs_ref, h_ref, bc_ref, cc_ref, zc_ref, hp_ref, ccp_ref,
             gf_ref, bg_ref, cw_ref, wg_ref, wo_ref,
             dh2_ref, dy_ref, dzs_ref, dbc_ref, dzc_ref, dyc_ref,
             dwo_ref, dwg_ref, loss_ref, dgf_ref, dbg_ref, dcw_ref):
        i = pl.program_id(0)

        @pl.when(i == 0)
        def _():
            for r in (dwo_ref, dwg_ref, loss_ref, dgf_ref, dbg_ref, dcw_ref):
                r[...] = jnp.zeros_like(r)

        yv = y_ref[...]
        y1, dgelu = _gelu_and_grad(yv)
        y1b = y1.astype(BF16)
        gate = _sigmoid(_dot(y1b, wg_ref[...]) + bg_ref[...])
        y2 = y1 * gate
        szs, dszs = _silu_and_grad(zs_ref[...])
        yssm = y2 * szs
        hv = h_ref[...]
        ccv = cc_ref[...]
        bcv = bc_ref[...]
        v = ccv * hv
        first = (i % tiles_per_seq) == 0
        vhalo = jnp.where(first, 0.0, ccp_ref[...] * hp_ref[...])
        v1 = _shift_down(v, vhalo, 1)
        v2 = _shift_down(v, vhalo, 2)
        w0 = cw_ref[0:1, :]
        w1 = cw_ref[1:2, :]
        w2 = cw_ref[2:3, :]
        yc = w0 * v2 + w1 * v1 + w2 * v
        szc, dszc = _silu_and_grad(zc_ref[...])
        yconv = (bcv * yc) * szc
        ysb = yssm.astype(BF16)
        ycb = yconv.astype(BF16)
        h2 = x_ref[...] + _dot(ysb, wo_ref[0:SSM_W, :]) + _dot(ycb, wo_ref[SSM_W:, :])
        r2 = lax.rsqrt(jnp.mean(h2 * h2, axis=-1, keepdims=True) + EPS)
        hn = h2 * r2
        gfv = gf_ref[...]
        err = hn * gfv - t_ref[...]
        loss_ref[...] += 0.5 * jnp.sum(jnp.mean(err * err, axis=-1, keepdims=True))
        dout = err * (1.0 / D_MODEL)
        dgf_ref[...] += jnp.sum(dout * hn, axis=0, keepdims=True)
        dn = dout * gfv
        dh2 = r2 * (dn - hn * jnp.mean(dn * hn, axis=-1, keepdims=True))
        dh2_ref[...] = dh2
        dh2b = dh2.astype(BF16)
        dwo_ref[0:SSM_W, :] += _dot_tn(ysb, dh2b)
        dwo_ref[SSM_W:, :] += _dot_tn(ycb, dh2b)
        dyssm = _dot_nt(dh2b, wo_ref[0:SSM_W, :])
        dyconv = _dot_nt(dh2b, wo_ref[SSM_W:, :])
        dy2 = dyssm * szs
        dzs_ref[...] = (dyssm * y2 * dszs).astype(BF16)
        dgp = dy2 * y1 * (gate * (1.0 - gate))
        dgpb = dgp.astype(BF16)
        dy1 = dy2 * gate + _dot_nt(dgpb, wg_ref[...])
        dwg_ref[...] += _dot_tn(y1b, dgpb)
        dbg_ref[...] += jnp.sum(dgp, axis=0, keepdims=True)
        dy_ref[...] = dy1 * dgelu
        dbc_ref[...] = (dyconv * yc * szc).astype(BF16)
        dyc = dyconv * bcv * szc
        dyc_ref[...] = dyc
        dzc_ref[...] = (dyconv * bcv * yc * dszc).astype(BF16)
        dcw_ref[0:1, :] += jnp.sum(dyc * v2, axis=0, keepdims=True)
        dcw_ref[1:2, :] += jnp.sum(dyc * v1, axis=0, keepdims=True)
        dcw_ref[2:3, :] += jnp.sum(dyc * v, axis=0, keepdims=True)

    tile_d = pl.BlockSpec((tm, D_MODEL), lambda i: (i, 0))
    tile_s = pl.BlockSpec((tm, SSM_W), lambda i: (i, 0))
    seg_of = lambda c: pl.BlockSpec((tm, SSM_W), lambda i: (i, c))
    halo_of = lambda c: pl.BlockSpec((SUBLANES, SSM_W), lambda i: (jnp.maximum(i * rows8 - 1, 0), c))
    const = lambda shape: pl.BlockSpec(shape, lambda i: (0,) * len(shape))
    seg = _out((n, SSM_W), F32)
    seg_b = _out((n, SSM_W), BF16)
    return _pcall(
        body, name="mix", grid=(n // tm,),
        out_shape=(_out((n, D_MODEL), F32), seg, seg_b, seg_b, seg_b, seg,
                   _out((D_MODEL, D_MODEL), F32), _out((SSM_W, SSM_W), F32),
                   _out((SUBLANES, LANES), F32), _out((1, D_MODEL), F32),
                   _out((1, SSM_W), F32), _out((SUBLANES, CONV_W), F32)),
        in_specs=[tile_d, tile_d, tile_s, seg_of(SEG_ZS), seg_of(SEG_H), seg_of(SEG_BC), seg_of(SEG_CC), seg_of(SEG_ZC),
                  halo_of(SEG_H), halo_of(SEG_CC),
                  const((1, D_MODEL)), const((1, SSM_W)), const((SUBLANES, CONV_W)),
                  const((SSM_W, SSM_W)), const((D_MODEL, D_MODEL))],
        out_specs=(tile_d, tile_s, tile_s, tile_s, tile_s, tile_s,
                   const((D_MODEL, D_MODEL)), const((SSM_W, SSM_W)), const((SUBLANES, LANES)),
                   const((1, D_MODEL)), const((1, SSM_W)), const((SUBLANES, CONV_W))),
        compiler_params=_params(1),
    )(x2, tgt2, y, proj, proj, proj, proj, proj, proj, proj, gf, b_glu, conv8, w_glu_f, w_out_f)


def _in_bwd(x2, dh2, du, dzs, dyc, proj, dbc, dzc, g1, conv8, w_full, seq):
    n = x2.shape[0]
    tm = TOK_TILE
    n_tiles = n // tm
    tiles_per_seq = seq // tm
    rows8 = tm // SUBLANES
    n_blk8 = n // SUBLANES

    def body(x_ref, dh2_ref, du_ref, dzs_ref, dyc_ref, dycn_ref, h_ref, cc_ref, dbc_ref, dzc_ref,
             g_ref, cw_ref, w_ref, gx_ref, dp_ref, dg_ref):
        i = pl.program_id(0)

        @pl.when(i == 0)
        def _():
            dg_ref[...] = jnp.zeros_like(dg_ref)

        dyc = dyc_ref[...]
        last = (i % tiles_per_seq) == tiles_per_seq - 1
        nhalo = jnp.where(last, 0.0, dycn_ref[...])
        dv = (cw_ref[2:3, :] * dyc + cw_ref[1:2, :] * _shift_up(dyc, nhalo, 1)
              + cw_ref[0:1, :] * _shift_up(dyc, nhalo, 2))
        parts = (du_ref[...], dzs_ref[...], dv * cc_ref[...], dbc_ref[...], dv * h_ref[...], dzc_ref[...])
        dxn = jnp.zeros((tm, D_MODEL), F32)
        for k, p in enumerate(parts):
            pb = p.astype(BF16)
            dp_ref[:, k * SSM_W:(k + 1) * SSM_W] = pb
            dxn = dxn + _dot_nt(pb, w_ref[:, k * SSM_W:(k + 1) * SSM_W])
        x = x_ref[...]
        r = lax.rsqrt(jnp.mean(x * x, axis=-1, keepdims=True) + EPS)
        xh = x * r
        dg_ref[...] += jnp.sum(dxn * xh, axis=0, keepdims=True)
        dn = dxn * g_ref[...]
        gx_ref[...] = dh2_ref[...] + r * (dn - xh * jnp.mean(dn * xh, axis=-1, keepdims=True))

    tile_d = pl.BlockSpec((tm, D_MODEL), lambda i: (i, 0))
    tile_s = pl.BlockSpec((tm, SSM_W), lambda i: (i, 0))
    seg_of = lambda c: pl.BlockSpec((tm, SSM_W), lambda i: (i, c))
    nhalo = pl.BlockSpec((SUBLANES, SSM_W), lambda i: (jnp.minimum((i + 1) * rows8, n_blk8 - 1), 0))
    const = lambda shape: pl.BlockSpec(shape, lambda i: (0,) * len(shape))
    return _pcall(
        body, name="in_bwd", grid=(n_tiles,),
        out_shape=(_out((n, D_MODEL), F32), _out((n, IN_COLS), BF16),
                   _out((SUBLANES, D_MODEL), F32)),
        in_specs=[tile_d, tile_d, tile_s, tile_s, tile_s, nhalo, seg_of(SEG_H), seg_of(SEG_CC), tile_s, tile_s,
                  const((1, D_MODEL)), const((SUBLANES, CONV_W)), const((D_MODEL, IN_COLS))],
        out_specs=(tile_d, pl.BlockSpec((tm, IN_COLS), lambda i: (i, 0)), const((SUBLANES, D_MODEL))),
        compiler_params=_params(1),
    )(x2, dh2, du, dzs, dyc, dyc, proj, proj, dbc, dzc, g1, conv8, w_full)


def _dw_in_exchange(order, xn, dproj, smalls):
    n = xn.shape[0]
    tk = 512
    nk = n // tk
    piece = (D_MODEL, COLS_PER_DEV)
    n_small = len(smalls)

    def body(order_ref, xn_hbm, dp_ref, *refs):
        del order_ref
        sm_refs = refs[:n_small]
        own_ref, rchip_ref = refs[n_small:n_small + 2]
        rsm_refs = refs[n_small + 2:2 * n_small + 2]
        (xn_ref, acc, stage, sbuf, xn_sems, give_send, give_recv, keep_send, keep_recv,
         sm_send, sm_recv, sm_loc) = refs[2 * n_small + 2:]
        s = pl.program_id(0)

        def xn_copy(kk):
            rows = pl.ds(pl.multiple_of(kk * tk, tk), tk)
            return pltpu.make_async_copy(xn_hbm.at[rows, :], xn_ref.at[rows, :], xn_sems.at[kk])

        @pl.when(s == 0)
        def _():
            for kk in range(nk):
                xn_copy(kk).start()
            xn_copy(0).wait()

        x, y, c = _mesh_pos()
        sib = (x, y, 1 - c)
        chips = [(1 - x, 1 - y), (1 - x, y), (x, 1 - y)]
        gather = _TwoLevelGather(list(sm_refs), [functools.partial(lambda r, dev: r.at[dev], r) for r in rsm_refs],
                                 sm_send, sm_recv, sm_loc)

        def half(i, core):
            return acc.at[i % 2, :, pl.ds(pl.multiple_of(core * COLS_PER_DEV, LANES), COLS_PER_DEV)]

        def give(i):
            return pltpu.make_async_remote_copy(src_ref=half(i, 1 - c), dst_ref=stage.at[i], send_sem=give_send.at[i],
                                                recv_sem=give_recv.at[i], device_id=sib, device_id_type=MESH)

        def keep(i):
            return pltpu.make_async_remote_copy(src_ref=sbuf.at[i], dst_ref=rchip_ref.at[i], send_sem=keep_send.at[i],
                                                recv_sem=keep_recv.at[i], device_id=(*chips[i], c), device_id_type=MESH)

        def chip_sum(i):
            give(i).wait_recv()
            mine = [acc[i % 2, :, cc * COLS_PER_DEV:(cc + 1) * COLS_PER_DEV] for cc in range(2)]
            return jnp.where(c == 0, mine[0], mine[1]) + stage[i]

        @pl.when(s == 0)
        def _():
            gather.start()

        @pl.when(s == N_CHIP // 2)
        def _():
            gather.forward()

        for k in range(2, N_CHIP):
            @pl.when(s == k)
            def _(k=k):
                give(k - 2).wait_send()

        slot = s % 2
        acc[slot] = _dot_tn(xn_ref[pl.ds(0, tk), :], dp_ref[pl.ds(0, tk), :])

        def kstep(kk, carry):
            @pl.when(s == 0)
            def _():
                xn_copy(kk).wait()

            off = pl.multiple_of(kk * tk, tk)
            acc[slot] += _dot_tn(xn_ref[pl.ds(off, tk), :], dp_ref[pl.ds(off, tk), :])
            return carry

        n_first = min(nk, 3)
        lax.fori_loop(1, n_first, kstep, 0)
        for k in range(1, N_CHIP):
            @pl.when(s == k)
            def _(k=k):
                sbuf[k - 1] = chip_sum(k - 1).astype(BF16)
                keep(k - 1).start()

        lax.fori_loop(n_first, nk, kstep, 0)

        for k in range(N_CHIP):
            @pl.when(s == k)
            def _(k=k):
                give(k).start()

        @pl.when(s == N_CHIP - 1)
        def _():
            own_ref[...] = chip_sum(N_CHIP - 1)
            give(N_CHIP - 2).wait_send()
            give(N_CHIP - 1).wait_send()
            for i in range(3):
                keep(i).wait()
            gather.finish()

    grid_spec = pltpu.PrefetchScalarGridSpec(
        num_scalar_prefetch=1, grid=(N_CHIP,),
        in_specs=[HBM_SPEC,
                  pl.BlockSpec((n, COLS_PER_CHIP), lambda s, order: (0, order[s])),
                  *([HBM_SPEC] * n_small)],
        out_specs=(pl.BlockSpec(piece, lambda s, order: (0, 0)), HBM_SPEC, *([HBM_SPEC] * n_small)),
        scratch_shapes=[pltpu.VMEM((n, D_MODEL), BF16),
                        pltpu.VMEM((2, D_MODEL, COLS_PER_CHIP), F32), pltpu.VMEM((4,) + piece, F32),
                        pltpu.VMEM((3,) + piece, BF16),
                        pltpu.SemaphoreType.DMA((nk,)),
                        pltpu.SemaphoreType.DMA((4,)), pltpu.SemaphoreType.DMA((4,)),
                        pltpu.SemaphoreType.DMA((3,)), pltpu.SemaphoreType.DMA((3,)),
                        pltpu.SemaphoreType.DMA((7 * n_small,)), pltpu.SemaphoreType.DMA((7 * n_small,)),
                        pltpu.SemaphoreType.DMA((n_small,))])
    return _pcall(
        body, name="dw_in_exchange", grid_spec=grid_spec,
        out_shape=(_out(piece, F32), _out((3,) + piece, BF16),
                   *(_out((N_DEV,) + a.shape, a.dtype) for a in smalls)),
        compiler_params=_params(1),
    )(order, xn, dproj, *smalls)


def _adamw(g, w, m, v):
    m_new = ADAM_B1 * m + (1.0 - ADAM_B1) * g
    v_new = ADAM_B2 * v + (1.0 - ADAM_B2) * (g * g)
    m_hat = m_new / (1.0 - ADAM_B1 ** ADAM_STEP)
    v_hat = v_new / (1.0 - ADAM_B2 ** ADAM_STEP)
    delta = -ADAM_LR * (m_hat / (jnp.sqrt(v_hat) + ADAM_EPS) + ADAM_WD * w)
    return delta, m_new, v_new


def _reduce_adam_w_in(own, rchip, w, m, v):
    rows, cols = w.shape
    row_tile = 256

    def body(o_ref, r_ref, w_ref, m_ref, v_ref, g_ref, d_ref, nm_ref, nv_ref):
        g = o_ref[...]
        for s in range(3):
            g = g + r_ref[s].astype(F32)
        g_ref[...] = g
        d_ref[...], nm_ref[...], nv_ref[...] = _adamw(g, w_ref[...], m_ref[...], v_ref[...])

    tile = pl.BlockSpec((row_tile, cols), lambda i: (i, 0))
    shp = _out((rows, cols), F32)
    return _pcall(
        body, name="reduce_adam_w_in", grid=(rows // row_tile,),
        out_shape=(shp,) * 4,
        in_specs=[tile, pl.BlockSpec((3, row_tile, cols), lambda i: (0, i, 0)), tile, tile, tile],
        out_specs=(tile,) * 4,
        compiler_params=_params(1),
    )(own, rchip, w, m, v)


_SMALL_LEAVES = ("norm_gain", "final_norm_gain", "b_glu", "ssm_a_re", "ssm_a_im", "ssm_log_dt", "ssm_d", "conv_w",
                 "ssm_c_re", "ssm_c_im", "ssm_b_re", "ssm_b_im")


def _reduce_adam_small(r_pack, r_gc, r_gb, wmv, sharded):
    n_leaf = len(_SMALL_LEAVES)
    n_sh = len(sharded)

    def body(*refs):
        rp_ref, rgc_ref, rgb_ref = refs[:3]
        w_refs = refs[3:3 + 3 * n_leaf]
        sh_in = refs[3 + 3 * n_leaf:3 + 3 * n_leaf + 4 * n_sh]
        outs0 = 3 + 3 * n_leaf + 4 * n_sh
        loss_ref = refs[outs0]
        o_refs = refs[outs0 + 1:outs0 + 1 + 4 * n_leaf]
        sh_out = refs[outs0 + 1 + 4 * n_leaf:outs0 + 1 + 4 * n_leaf + 4 * n_sh]
        own_conv = refs[-1]

        def total(ref):
            acc = ref[0].astype(F32)
            for s in range(1, N_DEV):
                acc = acc + ref[s].astype(F32)
            return acc

        for i in range(n_sh):
            r_ref, w_ref, m_ref, v_ref = sh_in[4 * i:4 * i + 4]
            o_g, o_d, o_m, o_v = sh_out[4 * i:4 * i + 4]
            g = total(r_ref)
            o_g[...] = g
            o_d[...], o_m[...], o_v[...] = _adamw(g, w_ref[...], m_ref[...], v_ref[...])

        sp = total(rp_ref)
        sgc = total(rgc_ref)
        sgb = total(rgb_ref)
        loss_ref[...] = sp[ROW_LOSS:ROW_LOSS + SUBLANES, 0:LANES]

        def wide(r):
            return jnp.concatenate([sp[r:r + 1, :], sp[r + 1:r + 2, :]], axis=1)

        s5 = slice(ROW_S5, ROW_S5 + N_GROUPS)
        eye = (lax.broadcasted_iota(jnp.int32, (N_GROUPS, N_GROUPS), 0)
               == lax.broadcasted_iota(jnp.int32, (N_GROUPS, N_GROUPS), 1)).astype(F32)
        d_row = sp[ROW_BGLU_D + 1:ROW_BGLU_D + 2, :]
        me = 4 * lax.axis_index("x") + 2 * lax.axis_index("y") + lax.axis_index("c")
        for k in range(N_DEV):
            @pl.when(me == k)
            def _(k=k):
                own_conv[...] = sp[ROW_CONV:ROW_CONV + SUBLANES, k * CONV_COLS_PER_DEV:(k + 1) * CONV_COLS_PER_DEV]
        grads = {
            "norm_gain": wide(ROW_NORM_GAIN),
            "final_norm_gain": wide(ROW_FINAL_GAIN),
            "b_glu": sp[ROW_BGLU_D:ROW_BGLU_D + 1, :],
            "ssm_a_re": sp[s5, LANE_A_RE:LANE_A_RE + STATE],
            "ssm_a_im": sp[s5, LANE_A_IM:LANE_A_IM + STATE],
            "ssm_log_dt": jnp.sum(sp[s5, LANE_LOG_DT:LANE_LOG_DT + 1] * eye, axis=0, keepdims=True),
            "ssm_d": jnp.concatenate([d_row[:, g * GROUP:(g + 1) * GROUP] for g in range(N_GROUPS)], axis=0),
            "conv_w": own_conv[0:3, :],
            "ssm_c_re": sgc[:, 0:STATE],
            "ssm_c_im": sgc[:, STATE:2 * STATE],
            "ssm_b_re": sgb[:, 0:STATE],
            "ssm_b_im": sgb[:, STATE:2 * STATE],
        }
        for i, name in enumerate(_SMALL_LEAVES):
            g = grads[name]
            w_ref, m_ref, v_ref = w_refs[3 * i:3 * i + 3]
            o_g, o_d, o_m, o_v = o_refs[4 * i:4 * i + 4]
            o_g[...] = g
            o_d[...], o_m[...], o_v[...] = _adamw(g, w_ref[...], m_ref[...], v_ref[...])

    flat_w = [a for name in _SMALL_LEAVES for a in wmv[name]]
    leaf_shapes = [_out(wmv[name][0].shape, F32) for name in _SMALL_LEAVES for _ in range(4)]
    sh_shapes = [_out(entry[1].shape, F32) for entry in sharded for _ in range(4)]
    operands = (r_pack, r_gc, r_gb, *flat_w, *(a for entry in sharded for a in entry))
    out_shape = (_out((SUBLANES, LANES), F32), *leaf_shapes, *sh_shapes)
    outs = _pcall(
        body, name="reduce_adam_small", grid=(1,), out_shape=out_shape,
        in_specs=_whole_specs(operands), out_specs=tuple(_whole_specs(out_shape)),
        scratch_shapes=[pltpu.VMEM((SUBLANES, CONV_COLS_PER_DEV), F32)],
        compiler_params=_params(1),
    )(*operands)
    leaves = {name: outs[1 + 4 * i:5 + 4 * i] for i, name in enumerate(_SMALL_LEAVES)}
    first = 1 + 4 * n_leaf
    return outs[0], leaves, [outs[first + 4 * i:first + 4 * i + 4] for i in range(n_sh)]


def kernel(x, norm_gain, w_in, ssm_a_re, ssm_a_im, ssm_log_dt, ssm_b_re, ssm_b_im, ssm_c_re, ssm_c_im, ssm_d, w_glu, b_glu, conv_w, w_out, final_norm_gain, loss_target, m_norm_gain, m_w_in, m_ssm_a_re, m_ssm_a_im, m_ssm_log_dt, m_ssm_b_re, m_ssm_b_im, m_ssm_c_re, m_ssm_c_im, m_ssm_d, m_w_glu, m_b_glu, m_conv_w, m_w_out, m_final_norm_gain, v_norm_gain, v_w_in, v_ssm_a_re, v_ssm_a_im, v_ssm_log_dt, v_ssm_b_re, v_ssm_b_im, v_ssm_c_re, v_ssm_c_im, v_ssm_d, v_w_glu, v_b_glu, v_conv_w, v_w_out, v_final_norm_gain):
    n_seq, seq, _ = x.shape
    n = n_seq * seq

    gh_p = lambda b4: jnp.transpose(b4, (0, 1, 3, 2)).reshape(N_GROUPS * GROUP, STATE)
    c2 = lambda a: a.reshape(N_GROUPS * GROUP, STATE)
    b_re2, b_im2 = gh_p(ssm_b_re), gh_p(ssm_b_im)
    d_row = ssm_d[0].reshape(1, SSM_W)

    x2 = x.reshape(n, D_MODEL)
    tgt2 = loss_target.reshape(n, D_MODEL)
    mx, my, mc = lax.axis_index("x"), lax.axis_index("y"), lax.axis_index("c")
    chip_ids = [2 * cx + cy for cx, cy in ((mx, my), (1 - mx, my), (mx, 1 - my), (1 - mx, 1 - my))]
    arrival = chip_ids
    xn, proj, w_in_f, s5 = _in_proj(
        jnp.stack(arrival).astype(jnp.int32), x2, norm_gain, w_in[0].astype(BF16),
        (ssm_a_re[0], ssm_a_im[0], ssm_log_dt, b_re2, b_im2, c2(ssm_c_re), c2(ssm_c_im)))
    a_re_x, a_im_x, log_dt_x, ab_re, ab_im, bb_re_m, bb_im_m, c_re_m, c_imn_m = s5
    u3 = proj.reshape(n_seq, seq, IN_COLS)
    conv_p = jnp.pad(conv_w[0], ((0, SUBLANES - 3), (0, LANES - CONV_COLS_PER_DEV)))
    s_re, s_im, y3, w_out_f, w_glu_f, conv_all = _ssm_fwd(
        u3, bb_re_m, bb_im_m, c_re_m, c_imn_m, d_row, ab_re, ab_im,
        w_out[0].astype(BF16), w_glu[0].astype(BF16), conv_p, n_seq, seq)
    conv8 = jnp.transpose(conv_all[:, :, :CONV_COLS_PER_DEV], (1, 0, 2)).reshape(SUBLANES, CONV_W)
    (dh2, dy, dzs, dbc, dzc, dyc, dw_out, dw_glu, loss_t, dgf, dbg, dcw) = _mix(
        x2, tgt2, y3.reshape(n, SSM_W), proj, final_norm_gain.reshape(1, D_MODEL), b_glu, conv8,
        w_glu_f, w_out_f, seq)

    du3, dc_re_d, dc_im_d, dbb_re_d, dbb_im_d, dab_re, dab_im, dd, r_out, r_glu = _ssm_bwd(
        dy.reshape(n_seq, seq, SSM_W), u3, s_re, s_im, bb_re_m, bb_im_m, c_re_m, c_imn_m, d_row, ab_re, ab_im,
        dw_out.reshape(N_DEV, OUT_ROWS_PER_DEV, D_MODEL), dw_glu.reshape(N_DEV, GLU_ROWS_PER_DEV, SSM_W), n_seq, seq)
    du = du3.reshape(n, SSM_W)
    grad_x2, dproj, dg8 = _in_bwd(x2, dh2, du, dzs, dyc, proj, dbc, dzc, norm_gain, conv8, w_in_f, seq)
    pack, gc, gb = _ssm_disc_bwd_pack(
        a_re_x, a_im_x, log_dt_x, b_re2, b_im2, dab_re.reshape(N_GROUPS, STATE), dab_im.reshape(N_GROUPS, STATE),
        dbb_re_d, dbb_im_d, loss_t, dg8, dgf, dbg, dd, dcw, dc_re_d, dc_im_d)

    order = [chip_ids[3], chip_ids[1], chip_ids[2], chip_ids[0]]
    own_in, rchip_in, r_pack, r_gc, r_gb = _dw_in_exchange(
        jnp.stack(order).astype(jnp.int32), xn, dproj, [pack, gc, gb])

    flat2 = lambda a: a.reshape(a.shape[-2:]) if a.ndim > 2 else a.reshape(1, -1)
    c2 = lambda a: a.reshape(N_GROUPS * GROUP, STATE)
    wmv = dict(norm_gain=(norm_gain, m_norm_gain, v_norm_gain),
               final_norm_gain=tuple(flat2(a) for a in (final_norm_gain, m_final_norm_gain, v_final_norm_gain)),
               b_glu=(b_glu, m_b_glu, v_b_glu),
               ssm_a_re=tuple(flat2(a) for a in (ssm_a_re, m_ssm_a_re, v_ssm_a_re)),
               ssm_a_im=tuple(flat2(a) for a in (ssm_a_im, m_ssm_a_im, v_ssm_a_im)),
               ssm_log_dt=(ssm_log_dt, m_ssm_log_dt, v_ssm_log_dt),
               ssm_d=tuple(flat2(a) for a in (ssm_d, m_ssm_d, v_ssm_d)),
               conv_w=tuple(flat2(a) for a in (conv_w, m_conv_w, v_conv_w)),
               ssm_c_re=tuple(c2(a) for a in (ssm_c_re, m_ssm_c_re, v_ssm_c_re)),
               ssm_c_im=tuple(c2(a) for a in (ssm_c_im, m_ssm_c_im, v_ssm_c_im)),
               ssm_b_re=(b_re2, gh_p(m_ssm_b_re), gh_p(v_ssm_b_re)),
               ssm_b_im=(b_im2, gh_p(m_ssm_b_im), gh_p(v_ssm_b_im)))

    res_in = _reduce_adam_w_in(own_in, rchip_in, w_in[0], m_w_in[0], v_w_in[0])
    loss8, small, (res_out, res_glu) = _reduce_adam_small(
        r_pack, r_gc, r_gb, wmv,
        [(r_out, w_out[0], m_w_out[0], v_w_out[0]), (r_glu, w_glu[0], m_w_glu[0], v_w_glu[0])])
    loss = loss8[0, 0]

    shapes = dict(norm_gain=(1, D_MODEL), ssm_a_re=(1, N_GROUPS, STATE), ssm_a_im=(1, N_GROUPS, STATE),
                  ssm_log_dt=(1, N_GROUPS), ssm_c_re=(1, N_GROUPS, GROUP, STATE), ssm_c_im=(1, N_GROUPS, GROUP, STATE),
                  ssm_d=(1, N_GROUPS, GROUP), b_glu=(1, SSM_W), final_norm_gain=(D_MODEL,),
                  conv_w=(1, 3, CONV_COLS_PER_DEV))
    big = dict(w_in=res_in, w_glu=res_glu, w_out=res_out)

    def leaf(kind, name):
        if name in big:
            return big[name][kind][None]
        if name in ("ssm_b_re", "ssm_b_im"):
            return jnp.transpose(small[name][kind].reshape(1, N_GROUPS, GROUP, STATE), (0, 1, 3, 2))
        return small[name][kind].reshape(shapes[name])

    order = ["norm_gain", "w_in", "ssm_a_re", "ssm_a_im", "ssm_log_dt", "ssm_b_re", "ssm_b_im", "ssm_c_re",
             "ssm_c_im", "ssm_d", "w_glu", "b_glu", "conv_w", "w_out", "final_norm_gain"]
    outs = [loss, grad_x2.reshape(x.shape)]
    for kind in range(4):
        outs += [leaf(kind, name) for name in order]
    return tuple(outs)
```

```python
import functools
import math

import jax
import jax.numpy as jnp
from jax import lax
from jax.experimental import pallas as pl
from jax.experimental.pallas import tpu as pltpu

F32 = jnp.float32
BF16 = jnp.bfloat16

N_DEV = 8
D_MODEL = 1024
SSM_W = 512
CONV_W = 512
N_GROUPS = 32
GROUP = 16
STATE = 64
IN_COLS = 3072
SEG_U, SEG_ZS, SEG_H, SEG_BC, SEG_CC, SEG_ZC = range(6)
COLS_PER_DEV = IN_COLS // N_DEV
N_CHIP = N_DEV // 2
COLS_PER_CHIP = 2 * COLS_PER_DEV
OUT_ROWS_PER_DEV = D_MODEL // N_DEV
GLU_ROWS_PER_DEV = SSM_W // N_DEV
CONV_COLS_PER_DEV = CONV_W // N_DEV
EPS = 1e-6

N_JBLK = 4
JB_CH = SSM_W // N_JBLK
JB_ST = N_GROUPS * STATE // N_JBLK

ADAM_LR = 0.001
ADAM_B1 = 0.9
ADAM_B2 = 0.999
ADAM_EPS = 1e-08
ADAM_WD = 0.01
ADAM_STEP = 10

SUBLANES = 8
LANES = 128
VMEM_LIMIT = 48 * 1024 * 1024
TOK_TILE = 256
IN_TILE = 1024
SCAN_TILE = 1024

MESH = pl.DeviceIdType.MESH
HBM_SPEC = pl.BlockSpec(memory_space=pltpu.HBM)


def _build(body, **kw):
    return pl.pallas_call(body, **kw)


def _pcall(body, **kw):
    def call(*operands):
        pinned = [a if jnp.issubdtype(a.dtype, jnp.integer) else pltpu.with_memory_space_constraint(a, pltpu.HBM)
                  for a in operands]
        return _build(body, **kw)(*pinned)
    return call


def _whole_specs(arrays):
    return [pl.BlockSpec(a.shape, functools.partial(lambda nd, i: (0,) * nd, len(a.shape))) for a in arrays]


def _out(shape, dtype):
    return pltpu.HBM(tuple(shape), dtype)


def _params(n_grid):
    return pltpu.CompilerParams(dimension_semantics=("arbitrary",) * n_grid,
                                vmem_limit_bytes=VMEM_LIMIT)


def _dot(a, b):
    return jnp.dot(a, b, preferred_element_type=F32)


def _dot_nt(a, b):
    return lax.dot_general(a, b, (((1,), (1,)), ((), ())), preferred_element_type=F32)


def _dot_tn(a, b):
    return lax.dot_general(a, b, (((0,), (0,)), ((), ())), preferred_element_type=F32)


def _sigmoid(z):
    return 1.0 / (1.0 + jnp.exp(-z))


_GELU_C = math.sqrt(2.0 / math.pi)


def _gelu_and_grad(y):
    inner = _GELU_C * (y + 0.044715 * (y * y * y))
    t = jnp.tanh(inner)
    g = 0.5 * y * (1.0 + t)
    dg = 0.5 * (1.0 + t) + 0.5 * y * (1.0 - t * t) * (_GELU_C * (1.0 + 3.0 * 0.044715 * (y * y)))
    return g, dg


def _silu_and_grad(z):
    s = _sigmoid(z)
    return z * s, s * (1.0 + z * (1.0 - s))


def _shift_down(v, halo, k):
    rolled = pltpu.roll(v, k, 0)
    row = lax.broadcasted_iota(jnp.int32, v.shape, 0)
    for r in range(k):
        rolled = jnp.where(row == r, halo[SUBLANES - k + r:SUBLANES - k + r + 1, :], rolled)
    return rolled


def _shift_up(v, halo, k):
    n = v.shape[0]
    rolled = pltpu.roll(v, n - k, 0)
    row = lax.broadcasted_iota(jnp.int32, v.shape, 0)
    for r in range(k):
        rolled = jnp.where(row == n - k + r, halo[r:r + 1, :], rolled)
    return rolled


def _mesh_pos():
    return lax.axis_index("x"), lax.axis_index("y"), lax.axis_index("c")


def _direct_copies(srcs_for, out_refs, send_sems, recv_sems, loc_sems):
    x, y, c = _mesh_pos()
    me_id = 4 * x + 2 * y + c
    n_arr = len(out_refs)
    dsts = [r.at[me_id] for r in out_refs]
    own = srcs_for(me_id)
    mine = [pltpu.make_async_copy(own[a], dsts[a], loc_sems.at[a]) for a in range(n_arr)]
    sends = []
    for k in range(1, N_DEV):
        px, py, pc = x ^ ((k >> 2) & 1), y ^ ((k >> 1) & 1), c ^ (k & 1)
        src = srcs_for(4 * px + 2 * py + pc)
        for a in range(n_arr):
            sends.append(pltpu.make_async_remote_copy(
                src_ref=src[a], dst_ref=dsts[a],
                send_sem=send_sems.at[(k - 1) * n_arr + a], recv_sem=recv_sems.at[(k - 1) * n_arr + a],
                device_id=(px, py, pc), device_id_type=MESH))
    return mine, sends


class _TwoLevelGather:
    def __init__(self, srcs, slots, send_sems, recv_sems, loc_sems):
        self.srcs, self.slots, self.n_arr = srcs, slots, len(srcs)
        self.send_sems, self.recv_sems, self.loc_sems = send_sems, recv_sems, loc_sems
        x, y, c = _mesh_pos()
        self.c = c
        self.me, self.sib = (x, y, c), (x, y, 1 - c)
        self.chips = [(1 - x, y), (x, 1 - y), (1 - x, 1 - y)]

    def _copies(self, k, block, to, from_src=False):
        dev = 4 * block[0] + 2 * block[1] + block[2]
        return [pltpu.make_async_remote_copy(
            src_ref=self.srcs[a] if from_src else self.slots[a](dev), dst_ref=self.slots[a](dev),
            send_sem=self.send_sems.at[k * self.n_arr + a], recv_sem=self.recv_sems.at[k * self.n_arr + a],
            device_id=to, device_id_type=MESH) for a in range(self.n_arr)]

    def _local(self):
        dev = 4 * self.me[0] + 2 * self.me[1] + self.me[2]
        return [pltpu.make_async_copy(self.srcs[a], self.slots[a](dev), self.loc_sems.at[a])
                for a in range(self.n_arr)]

    def start(self, chips=(0, 1, 2)):
        for cp in self._local() + self._copies(0, self.me, self.sib, True):
            cp.start()
        self.start_to(chips)

    def start_to(self, chips):
        for j in chips:
            for cp in self._copies(1 + j, self.me, (*self.chips[j], self.c), True):
                cp.start()

    def wait_own(self):
        for cp in self._local():
            cp.wait()

    def wait_sibling(self):
        for cp in self._copies(0, self.sib, self.me):
            cp.wait_recv()

    def wait_and_pass_on(self, j):
        chip = self.chips[j]
        for cp in self._copies(1 + j, (*chip, self.c), self.me):
            cp.wait_recv()
        for cp in self._copies(4 + j, (*chip, self.c), self.sib):
            cp.start()

    def relay(self, k, block, to):
        for cp in self._copies(k, block, to):
            cp.start()

    def wait_passed_on(self, j):
        for cp in self._copies(4 + j, (*self.chips[j], 1 - self.c), self.me):
            cp.wait_recv()

    def wait_sends(self):
        for cp in self._copies(0, self.me, self.sib, True):
            cp.wait_send()
        for j, chip in enumerate(self.chips):
            for cp in self._copies(1 + j, self.me, (*chip, self.c), True) + self._copies(4 + j, (*chip, self.c), self.sib):
                cp.wait_send()

    def forward(self):
        for j in range(3):
            self.wait_and_pass_on(j)

    def finish(self):
        self.wait_sibling()
        for j in range(3):
            self.wait_passed_on(j)
        self.wait_sends()
        self.wait_own()


def _disc(a_re, a_im, log_dt, b_re, b_im):
    dt = jnp.exp(log_dt)
    mag = jnp.exp(a_re * dt)
    ab_re = mag * jnp.cos(a_im * dt)
    ab_im = mag * jnp.sin(a_im * dt)
    den = a_re * a_re + a_im * a_im
    p_re = ab_re - 1.0
    p_im = ab_im
    q_re = (p_re * a_re + p_im * a_im) / den
    q_im = (p_im * a_re - p_re * a_im) / den
    bb_re = q_re * b_re - q_im * b_im
    bb_im = q_re * b_im + q_im * b_re
    return ab_re, ab_im, bb_re, bb_im


def _split3(v):
    hi = v.astype(BF16)
    r1 = v - hi.astype(F32)
    mid = r1.astype(BF16)
    lo = (r1 - mid.astype(F32)).astype(BF16)
    return hi, mid, lo


def _select_dot(sel, v):
    return sum(_dot(sel, t) for t in _split3(v))


PACK_ROWS = 72
PACK_W = 512
ROW_FINAL_GAIN, ROW_NORM_GAIN, ROW_BGLU_D, ROW_CONV, ROW_LOSS, ROW_S5 = 0, 8, 16, 24, 32, 40
LANE_A_RE, LANE_A_IM, LANE_LOG_DT = 0, 128, 256


def _ssm_disc_bwd_pack(a_re_x, a_im_x, log_dt_x, b_re, b_im, g_ab_re, g_ab_im, dbb_re_d, dbb_im_d,
                       loss_t, dg8, dgf, dbg, dd, dcw, dc_re_d, dc_im_d):
    rows_gh = N_GROUPS * GROUP

    def body(are, aim, ldt, bre, bim, gabre, gabim, dbbre_ref, dbbim_ref,
             loss_ref, dg8_ref, dgf_ref, dbg_ref, dd_ref, dcw_ref, dcre_ref, dcim_ref,
             p_ref, gc_ref, gb_ref, gbb_re, gbb_im):
        r_g = lax.broadcasted_iota(jnp.int32, (N_GROUPS, rows_gh), 0)
        c_gh = lax.broadcasted_iota(jnp.int32, (N_GROUPS, rows_gh), 1)
        group_sum = (c_gh // GROUP == r_g).astype(BF16)
        r_gh = lax.broadcasted_iota(jnp.int32, (rows_gh, N_GROUPS), 0)
        c_g = lax.broadcasted_iota(jnp.int32, (rows_gh, N_GROUPS), 1)
        first_row = (r_gh == c_g * GROUP).astype(BF16)

        def diag_block(ref, j, gi):
            return ref[j, gi * GROUP:(gi + 1) * GROUP, gi * STATE:(gi + 1) * STATE]

        for j in range(N_JBLK):
            for gi in range(SUBLANES):
                r0 = (j * SUBLANES + gi) * GROUP
                gbb_re[r0:r0 + GROUP, :] = diag_block(dbbre_ref, j, gi)
                gbb_im[r0:r0 + GROUP, :] = diag_block(dbbim_ref, j, gi)
                both = jnp.concatenate([diag_block(dcre_ref, j, gi), -diag_block(dcim_ref, j, gi)], axis=1)
                gc_ref[r0:r0 + GROUP, :] = both.astype(BF16)

        _, vjp = jax.vjp(_disc, are[...], aim[...], ldt[...], bre[...], bim[...])
        d_are, d_aim, d_ldt, d_bre, d_bim = vjp((_select_dot(first_row, gabre[...]), _select_dot(first_row, gabim[...]),
                                                 gbb_re[...], gbb_im[...]))
        gb_ref[...] = jnp.concatenate([d_bre, d_bim], axis=1).astype(BF16)

        p_ref[...] = jnp.zeros_like(p_ref)
        half = D_MODEL // 2
        for r, src in ((ROW_FINAL_GAIN, dgf_ref), (ROW_NORM_GAIN, dg8_ref)):
            p_ref[r:r + 1, :] = src[0:1, 0:half]
            p_ref[r + 1:r + 2, :] = src[0:1, half:D_MODEL]
        p_ref[ROW_BGLU_D:ROW_BGLU_D + 1, :] = dbg_ref[...]
        p_ref[ROW_BGLU_D + 1:ROW_BGLU_D + 2, :] = dd_ref[...]
        p_ref[ROW_CONV:ROW_CONV + SUBLANES, :] = dcw_ref[...]
        p_ref[ROW_LOSS:ROW_LOSS + SUBLANES, 0:LANES] = loss_ref[...]
        s5 = slice(ROW_S5, ROW_S5 + N_GROUPS)
        p_ref[s5, LANE_A_RE:LANE_A_RE + STATE] = _select_dot(group_sum, d_are)
        p_ref[s5, LANE_A_IM:LANE_A_IM + STATE] = _select_dot(group_sum, d_aim)
        p_ref[s5, LANE_LOG_DT:LANE_LOG_DT + LANES] = _select_dot(group_sum, jnp.broadcast_to(d_ldt, (rows_gh, LANES)))

    operands = (a_re_x, a_im_x, log_dt_x, b_re, b_im, g_ab_re, g_ab_im, dbb_re_d, dbb_im_d,
                loss_t, dg8, dgf, dbg, dd, dcw, dc_re_d, dc_im_d)
    out_shape = (_out((PACK_ROWS, PACK_W), F32),
                 _out((rows_gh, 2 * STATE), BF16),
                 _out((rows_gh, 2 * STATE), BF16))
    return _pcall(body, name="ssm_disc_bwd_pack", grid=(1,), out_shape=out_shape,
                  in_specs=_whole_specs(operands), out_specs=tuple(_whole_specs(out_shape)),
                  scratch_shapes=[pltpu.VMEM((rows_gh, STATE), F32), pltpu.VMEM((rows_gh, STATE), F32)],
                  compiler_params=_params(1))(*operands)


def _s5_prepare(are, aim, ldt, bre, bim, cre, cim,
                o_ax_re, o_ax_im, o_ldt_x, o_ab_re, o_ab_im, o_bb_re, o_bb_im, o_c_re, o_c_imn):
    rows_gh = N_GROUPS * GROUP
    rep = (lax.broadcasted_iota(jnp.int32, (rows_gh, N_GROUPS), 0) // GROUP
           == lax.broadcasted_iota(jnp.int32, (rows_gh, N_GROUPS), 1)).astype(BF16)
    eye = (lax.broadcasted_iota(jnp.int32, (N_GROUPS, N_GROUPS), 0)
           == lax.broadcasted_iota(jnp.int32, (N_GROUPS, N_GROUPS), 1)).astype(F32)
    ldt_col = jnp.sum(eye * ldt[...], axis=1, keepdims=True)
    a_re_x = _select_dot(rep, are[...])
    a_im_x = _select_dot(rep, aim[...])
    ldt_x = _select_dot(rep, jnp.broadcast_to(ldt_col, (N_GROUPS, LANES)))[:, 0:1]
    o_ax_re[...] = a_re_x
    o_ax_im[...] = a_im_x
    o_ldt_x[...] = ldt_x
    ab_re, ab_im, bb_re, bb_im = _disc(a_re_x, a_im_x, ldt_x, bre[...], bim[...])
    for j in range(N_JBLK):
        first = [(j * SUBLANES + gi) * GROUP for gi in range(SUBLANES)]
        o_ab_re[j] = jnp.concatenate([ab_re[r:r + 1, :] for r in first], axis=1)
        o_ab_im[j] = jnp.concatenate([ab_im[r:r + 1, :] for r in first], axis=1)
    for o, v in ((o_bb_re, bb_re), (o_bb_im, bb_im), (o_c_re, cre[...]), (o_c_imn, -cim[...])):
        for j in range(N_JBLK):
            for gi in range(SUBLANES):
                r0 = (j * SUBLANES + gi) * GROUP
                parts = [v[r0:r0 + GROUP, :] if k == gi else jnp.zeros((GROUP, STATE), F32) for k in range(SUBLANES)]
                o[j, gi * GROUP:(gi + 1) * GROUP, :] = jnp.concatenate(parts, axis=1).astype(BF16)


def _in_proj(order, x2, g1, w_in_b, s5):
    n = x2.shape[0]
    tm = min(IN_TILE, n)
    n_tiles = n // tm
    n_s5_in = len(s5)
    n_s5_out = 9

    def body(order_ref, x_ref, g_ref, w_ref, *refs):
        s5_in = refs[:n_s5_in]
        xn_ref, proj_ref, wall_ref = refs[n_s5_in:n_s5_in + 3]
        s5_out = refs[n_s5_in + 3:n_s5_in + 3 + n_s5_out]
        xn_scr, wbuf, send_sems, recv_sems, loc_sems, out_sems = refs[n_s5_in + 3 + n_s5_out:]
        k = pl.program_id(0)
        i = pl.program_id(1)

        def slot(dev):
            return wbuf.at[dev // 2, :, pl.ds(pl.multiple_of((dev % 2) * COLS_PER_DEV, LANES), COLS_PER_DEV)]

        gather = _TwoLevelGather([w_ref], [slot], send_sems, recv_sems, loc_sems)

        @pl.when((k == 0) & (i == 0))
        def _():
            gather.start(chips=(0, 1))

        def own_chip():
            gather.wait_own()
            gather.wait_sibling()

        def x_chip():
            x, y, c = _mesh_pos()
            gather.wait_and_pass_on(0)
            gather.wait_and_pass_on(1)
            gather.relay(1 + 2, (x ^ c, y ^ (1 - c), c), (x ^ (1 - c), y ^ c, c))
            gather.wait_passed_on(0)

        def diag_chip():
            gather.wait_and_pass_on(2)
            gather.wait_passed_on(2)

        arrivals = [own_chip, x_chip, functools.partial(gather.wait_passed_on, 1), diag_chip]
        for kk, arrived in enumerate(arrivals):
            @pl.when((k == kk) & (i == 0))
            def _(arrived=arrived):
                arrived()

        rows = pl.ds(pl.multiple_of(i * tm, tm), tm)

        @pl.when(k == 0)
        def _():
            x = x_ref[...]
            r = lax.rsqrt(jnp.mean(x * x, axis=-1, keepdims=True) + EPS)
            xn = ((x * r) * g_ref[...]).astype(BF16)
            xn_scr[rows, :] = xn
            xn_ref[...] = xn

        proj_ref[...] = _dot(xn_scr[rows, :], wbuf[order_ref[k]])

        @pl.when((k == 0) & (i == n_tiles - 1))
        def _():
            _s5_prepare(*s5_in, *s5_out)

        @pl.when((k == N_CHIP - 1) & (i == n_tiles - 1))
        def _():
            gather.wait_sends()
            outs = [pltpu.make_async_copy(wbuf.at[q], wall_ref.at[:, q * COLS_PER_CHIP:(q + 1) * COLS_PER_CHIP],
                                          out_sems.at[q]) for q in range(N_CHIP)]
            for cp in outs:
                cp.start()
            for cp in outs:
                cp.wait()

    tile_once = lambda k, i, order: (jnp.where(k == 0, i, n_tiles - 1), 0)
    whole = lambda shape: pl.BlockSpec(shape, lambda k, i, order: (0,) * len(shape))
    rows_gh = N_GROUPS * GROUP
    s5_out_shapes = ([(rows_gh, STATE), F32], [(rows_gh, STATE), F32], [(rows_gh, 1), F32],
                     [(N_JBLK, 1, JB_ST), F32], [(N_JBLK, 1, JB_ST), F32]) + ([(N_JBLK, JB_CH, JB_ST), BF16],) * 4
    grid_spec = pltpu.PrefetchScalarGridSpec(
        num_scalar_prefetch=1, grid=(N_CHIP, n_tiles),
        in_specs=[pl.BlockSpec((tm, D_MODEL), tile_once),
                  whole((1, D_MODEL)),
                  HBM_SPEC,
                  *(whole(a.shape) for a in s5)],
        out_specs=(pl.BlockSpec((tm, D_MODEL), tile_once),
                   pl.BlockSpec((tm, COLS_PER_CHIP), lambda k, i, order: (i, order[k])),
                   HBM_SPEC,
                   *(whole(shape) for shape, _ in s5_out_shapes)),
        scratch_shapes=[pltpu.VMEM((n, D_MODEL), BF16), pltpu.VMEM((N_CHIP, D_MODEL, COLS_PER_CHIP), BF16),
                        pltpu.SemaphoreType.DMA((7,)), pltpu.SemaphoreType.DMA((7,)), pltpu.SemaphoreType.DMA((1,)),
                        pltpu.SemaphoreType.DMA((N_CHIP,))])
    outs = _pcall(
        body, name="in_proj", grid_spec=grid_spec,
        out_shape=(_out((n, D_MODEL), BF16), _out((n, IN_COLS), F32),
                   _out((D_MODEL, IN_COLS), BF16),
                   *(_out(shape, dt) for shape, dt in s5_out_shapes)),
        compiler_params=_params(2),
    )(order, x2, g1, w_in_b, *s5)
    return outs[0], outs[1], outs[2], outs[3:]


def _cmul(p, q):
    return p[0] * q[0] - p[1] * q[1], p[0] * q[1] + p[1] * q[0]


def _scan_tables(ar, ai, width, reverse):
    pows = [(ar, ai)]
    for _ in range(SUBLANES - 1):
        pows.append(_cmul(pows[-1], (ar, ai)))
    row = lax.broadcasted_iota(jnp.int32, (SUBLANES, width), 0)

    def bc(v):
        return jnp.broadcast_to(v, (SUBLANES, width))

    levels = []
    for k in (1, 2, 4):
        keep = (row <= SUBLANES - 1 - k) if reverse else (row >= k)
        levels.append((jnp.where(keep, bc(pows[k - 1][0]), 0.0), jnp.where(keep, bc(pows[k - 1][1]), 0.0)))
    cre = jnp.zeros((SUBLANES, width), F32)
    cim = jnp.zeros((SUBLANES, width), F32)
    for r in range(SUBLANES):
        e = (SUBLANES - r) if reverse else (r + 1)
        cre = jnp.where(row == r, bc(pows[e - 1][0]), cre)
        cim = jnp.where(row == r, bc(pows[e - 1][1]), cim)
    return levels, (cre, cim)


def _load_chunked(src_ref, b, dst_ref, n_rows):
    n_blk = n_rows // SUBLANES
    for i in range(n_blk):
        dst_ref[b, i * SUBLANES:(i + 1) * SUBLANES, :] = src_ref[b, pl.ds(i, SUBLANES, stride=n_blk), :]


def _store_chunked(val, dst_ref, b, n_rows):
    n_blk = n_rows // SUBLANES
    for i in range(n_blk):
        dst_ref[b, pl.ds(i, SUBLANES, stride=n_blk), :] = val[i * SUBLANES:(i + 1) * SUBLANES, :]


def _chunk_scan(re_ref, im_ref, bs, car_ref, ar, ai, n_rows, reverse, on_block=None):
    width = re_ref.shape[2]
    n_blk = n_rows // SUBLANES
    shape = (SUBLANES, width)
    abr = jnp.broadcast_to(ar, shape)
    abi = jnp.broadcast_to(ai, shape)
    order = list(range(n_blk - 1, -1, -1)) if reverse else list(range(n_blk))

    def blk(ref, b, i):
        return ref[b, i * SUBLANES:(i + 1) * SUBLANES, :]

    def step(state, b, i):
        sr, si = state
        return abr * sr - abi * si + blk(re_ref, b, i), abr * si + abi * sr + blk(im_ref, b, i)

    finals = {b: (blk(re_ref, b, order[0]), blk(im_ref, b, order[0])) for b in bs}
    for i in order[1:]:
        for b in bs:
            finals[b] = step(finals[b], b, i)

    mr, mi = ar, ai
    for _ in range(n_blk.bit_length() - 1):
        mr, mi = _cmul((mr, mi), (mr, mi))
    levels, _ = _scan_tables(mr, mi, width, reverse)
    mbr = jnp.broadcast_to(mr, shape)
    mbi = jnp.broadcast_to(mi, shape)
    row = lax.broadcasted_iota(jnp.int32, shape, 0)
    edge_in = SUBLANES - 1 if reverse else 0
    edge_out = 0 if reverse else SUBLANES - 1
    sh1 = SUBLANES - 1 if reverse else 1
    states = {}
    for b in bs:
        fr, fi = finals[b]
        gr = jnp.where(row == edge_in, jnp.broadcast_to(car_ref[b, 0:1, :], shape), pltpu.roll(fr, sh1, 0))
        gi = jnp.where(row == edge_in, jnp.broadcast_to(car_ref[b, 1:2, :], shape), pltpu.roll(fi, sh1, 0))
        for (lr, li), k in zip(levels, (1, 2, 4)):
            sh = (SUBLANES - k) if reverse else k
            sr = pltpu.roll(gr, sh, 0)
            si = pltpu.roll(gi, sh, 0)
            gr, gi = gr + (lr * sr - li * si), gi + (lr * si + li * sr)
        car_ref[b, 0:1, :] = (fr + (mbr * gr - mbi * gi))[edge_out:edge_out + 1, :]
        car_ref[b, 1:2, :] = (fi + (mbr * gi + mbi * gr))[edge_out:edge_out + 1, :]
        states[b] = (gr, gi)

    for i in order:
        for b in bs:
            states[b] = step(states[b], b, i)
            re_ref[b, i * SUBLANES:(i + 1) * SUBLANES, :] = states[b][0]
            im_ref[b, i * SUBLANES:(i + 1) * SUBLANES, :] = states[b][1]
            if on_block is not None:
                on_block(b, i, *states[b])


def _ssm_fwd(u, bb_re, bb_im, c_re_t, c_imn_t, d_row, ab_re, ab_im, w_out_b, w_glu_b, conv_p, n_seq, seq):
    tt = min(SCAN_TILE, seq)
    nt = seq // tt

    def body(u_ref, bbre, bbim, cre, cimn, d_ref, are, aim, wout_ref, wglu_ref, cw_ref,
             sre_ref, sim_ref, y_ref, oout_ref, oglu_ref, ocw_ref,
             up_ref, car_ref, send_sems, recv_sems, loc_sems):
        j = pl.program_id(0)
        t = pl.program_id(1)
        gather = _TwoLevelGather(
            [wout_ref, wglu_ref, cw_ref],
            [lambda dev: oout_ref.at[pl.ds(pl.multiple_of(dev * OUT_ROWS_PER_DEV, OUT_ROWS_PER_DEV), OUT_ROWS_PER_DEV), :],
             lambda dev: oglu_ref.at[pl.ds(pl.multiple_of(dev * GLU_ROWS_PER_DEV, GLU_ROWS_PER_DEV), GLU_ROWS_PER_DEV), :],
             lambda dev: ocw_ref.at[dev]],
            send_sems, recv_sems, loc_sems)

        @pl.when((j == 0) & (t == 0))
        def _():
            gather.start()

        @pl.when((j == N_JBLK // 2) & (t == 0))
        def _():
            gather.forward()

        @pl.when(t == 0)
        def _():
            car_ref[...] = jnp.zeros_like(car_ref)

        bs = list(range(n_seq))
        for b in bs:
            _load_chunked(u_ref, b, up_ref, tt)
        for b in bs:
            ub = up_ref[b].astype(BF16)
            sre_ref[b] = _dot(ub, bbre[0])
            sim_ref[b] = _dot(ub, bbim[0])
            _chunk_scan(sre_ref, sim_ref, [b], car_ref, are[0], aim[0], tt, reverse=False)
        for b in bs:
            yp = (_dot_nt(sre_ref[b].astype(BF16), cre[0]) + _dot_nt(sim_ref[b].astype(BF16), cimn[0])
                  + d_ref[...] * up_ref[b])
            _store_chunked(yp, y_ref, b, tt)

        @pl.when((j == N_JBLK - 1) & (t == nt - 1))
        def _():
            gather.finish()

    tok = lambda j, t: (0, t, j)
    blk3 = lambda j, t: (j, 0, 0)
    row = lambda j, t: (0, j)
    st = _out((n_seq, seq, N_JBLK * JB_ST), F32)
    n_arr = 3
    return _pcall(
        body, name="ssm_fwd", grid=(N_JBLK, nt),
        out_shape=(st, st, _out((n_seq, seq, SSM_W), F32),
                   _out((D_MODEL, D_MODEL), BF16), _out((SSM_W, SSM_W), BF16),
                   _out((N_DEV, SUBLANES, LANES), F32)),
        in_specs=[pl.BlockSpec((n_seq, tt, JB_CH), tok),
                  pl.BlockSpec((1, JB_CH, JB_ST), blk3), pl.BlockSpec((1, JB_CH, JB_ST), blk3),
                  pl.BlockSpec((1, JB_CH, JB_ST), blk3), pl.BlockSpec((1, JB_CH, JB_ST), blk3),
                  pl.BlockSpec((1, JB_CH), row), pl.BlockSpec((1, 1, JB_ST), blk3), pl.BlockSpec((1, 1, JB_ST), blk3),
                  HBM_SPEC, HBM_SPEC, HBM_SPEC],
        out_specs=(pl.BlockSpec((n_seq, tt, JB_ST), tok), pl.BlockSpec((n_seq, tt, JB_ST), tok),
                   pl.BlockSpec((n_seq, tt, JB_CH), tok), HBM_SPEC, HBM_SPEC, HBM_SPEC),
        scratch_shapes=[pltpu.VMEM((n_seq, tt, JB_CH), F32), pltpu.VMEM((n_seq, SUBLANES, JB_ST), F32),
                        pltpu.SemaphoreType.DMA((7 * n_arr,)), pltpu.SemaphoreType.DMA((7 * n_arr,)),
                        pltpu.SemaphoreType.DMA((n_arr,))],
        compiler_params=_params(2),
    )(u, bb_re, bb_im, c_re_t, c_imn_t, d_row, ab_re, ab_im, w_out_b, w_glu_b, conv_p)


def _ssm_bwd(dy, u, s_re, s_im, bb_re, bb_im, c_re_t, c_imn_t, d_row, ab_re, ab_im, g_out, g_glu, n_seq, seq):
    tt = min(SCAN_TILE, seq)
    nt = seq // tt
    rows8 = tt // SUBLANES

    def body(dy_ref, u_ref, sre_ref, sim_ref, pre_ref, pim_ref, bbre, bbim, cre, cimn, d_ref, are, aim,
             gout_ref, gglu_ref,
             du_ref, dcre_ref, dcim_ref, dbbre_ref, dbbim_ref, dare_ref, daim_ref, dd_ref, rout_ref, rglu_ref,
             lre_ref, lim_ref, dyp_ref, up_ref, car_ref, send_sems, recv_sems, loc_sems):
        j = pl.program_id(0)
        tr = pl.program_id(1)

        def exchange():
            return _direct_copies(lambda pid: [gout_ref.at[pid], gglu_ref.at[pid]], [rout_ref, rglu_ref],
                                  send_sems, recv_sems, loc_sems)

        @pl.when((j == 0) & (tr == 0))
        def _():
            mine, sends = exchange()
            for cp in mine + sends:
                cp.start()

        @pl.when(tr == 0)
        def _():
            car_ref[...] = jnp.zeros_like(car_ref)
            for r in (dcre_ref, dcim_ref, dbbre_ref, dbbim_ref, dare_ref, daim_ref, dd_ref):
                r[...] = jnp.zeros_like(r)

        first = tr == nt - 1
        row = lax.broadcasted_iota(jnp.int32, (SUBLANES, JB_ST), 0)
        n_blk = tt // SUBLANES
        bs = list(range(n_seq))
        for b in bs:
            _load_chunked(dy_ref, b, dyp_ref, tt)
            _load_chunked(u_ref, b, up_ref, tt)
        for b in bs:
            dyb = dyp_ref[b].astype(BF16)
            lre_ref[b] = _dot(dyb, cre[0])
            lim_ref[b] = _dot(dyb, cimn[0])
        acc = {b: [jnp.zeros((SUBLANES, JB_ST), F32), jnp.zeros((SUBLANES, JB_ST), F32)] for b in bs}

        def on_block(b, i, lr, li):
            if i > 0:
                spr = sre_ref[b, (i - 1) * SUBLANES:i * SUBLANES, :]
                spi = sim_ref[b, (i - 1) * SUBLANES:i * SUBLANES, :]
            else:
                hr = jnp.where(first, 0.0, pre_ref[b, SUBLANES - 1:SUBLANES, :])
                hi = jnp.where(first, 0.0, pim_ref[b, SUBLANES - 1:SUBLANES, :])
                last_r = sre_ref[b, (n_blk - 1) * SUBLANES:n_blk * SUBLANES, :]
                last_i = sim_ref[b, (n_blk - 1) * SUBLANES:n_blk * SUBLANES, :]
                spr = jnp.where(row == 0, jnp.broadcast_to(hr, row.shape), pltpu.roll(last_r, 1, 0))
                spi = jnp.where(row == 0, jnp.broadcast_to(hi, row.shape), pltpu.roll(last_i, 1, 0))
            acc[b][0] = acc[b][0] + (lr * spr + li * spi)
            acc[b][1] = acc[b][1] + (li * spr - lr * spi)

        _chunk_scan(lre_ref, lim_ref, bs, car_ref, are[0], -aim[0], tt, reverse=True, on_block=on_block)
        for b in bs:
            dare_ref[...] += jnp.sum(acc[b][0], axis=0, keepdims=True)
            daim_ref[...] += jnp.sum(acc[b][1], axis=0, keepdims=True)
            dyp = dyp_ref[b]
            up = up_ref[b]
            dyb = dyp.astype(BF16)
            ub = up.astype(BF16)
            lrb = lre_ref[b].astype(BF16)
            lib = lim_ref[b].astype(BF16)
            dup = d_ref[...] * dyp + _dot_nt(lrb, bbre[0]) + _dot_nt(lib, bbim[0])
            _store_chunked(dup, du_ref, b, tt)
            dbbre_ref[0] += _dot_tn(ub, lrb)
            dbbim_ref[0] += _dot_tn(ub, lib)
            dcre_ref[0] += _dot_tn(dyb, sre_ref[b].astype(BF16))
            dcim_ref[0] += _dot_tn(dyb, sim_ref[b].astype(BF16))
            dd_ref[...] += jnp.sum(dyp * up, axis=0, keepdims=True)

        @pl.when((j == N_JBLK - 1) & (tr == nt - 1))
        def _():
            mine, sends = exchange()
            for cp in sends + mine:
                cp.wait()

    tok = lambda j, t: (0, nt - 1 - t, j)
    halo = lambda j, t: (0, jnp.maximum((nt - 1 - t) * rows8 - 1, 0), j)
    blk3 = lambda j, t: (j, 0, 0)
    row1 = lambda j, t: (0, j)
    acc_shape = _out((N_JBLK, JB_CH, JB_ST), F32)
    return _pcall(
        body, name="ssm_bwd", grid=(N_JBLK, nt),
        out_shape=(_out((n_seq, seq, SSM_W), F32), acc_shape, acc_shape, acc_shape, acc_shape,
                   _out((1, N_JBLK * JB_ST), F32), _out((1, N_JBLK * JB_ST), F32),
                   _out((1, SSM_W), F32),
                   _out((N_DEV,) + g_out.shape[1:], F32),
                   _out((N_DEV,) + g_glu.shape[1:], F32)),
        in_specs=[pl.BlockSpec((n_seq, tt, JB_CH), tok), pl.BlockSpec((n_seq, tt, JB_CH), tok),
                  pl.BlockSpec((n_seq, tt, JB_ST), tok), pl.BlockSpec((n_seq, tt, JB_ST), tok),
                  pl.BlockSpec((n_seq, SUBLANES, JB_ST), halo), pl.BlockSpec((n_seq, SUBLANES, JB_ST), halo),
                  pl.BlockSpec((1, JB_CH, JB_ST), blk3), pl.BlockSpec((1, JB_CH, JB_ST), blk3),
                  pl.BlockSpec((1, JB_CH, JB_ST), blk3), pl.BlockSpec((1, JB_CH, JB_ST), blk3),
                  pl.BlockSpec((1, JB_CH), row1), pl.BlockSpec((1, 1, JB_ST), blk3), pl.BlockSpec((1, 1, JB_ST), blk3),
                  HBM_SPEC, HBM_SPEC],
        out_specs=(pl.BlockSpec((n_seq, tt, JB_CH), tok),
                   pl.BlockSpec((1, JB_CH, JB_ST), blk3), pl.BlockSpec((1, JB_CH, JB_ST), blk3),
                   pl.BlockSpec((1, JB_CH, JB_ST), blk3), pl.BlockSpec((1, JB_CH, JB_ST), blk3),
                   pl.BlockSpec((1, JB_ST), row1), pl.BlockSpec((1, JB_ST), row1), pl.BlockSpec((1, JB_CH), row1),
                   HBM_SPEC, HBM_SPEC),
        scratch_shapes=[pltpu.VMEM((n_seq, tt, JB_ST), F32), pltpu.VMEM((n_seq, tt, JB_ST), F32),
                        pltpu.VMEM((n_seq, tt, JB_CH), F32), pltpu.VMEM((n_seq, tt, JB_CH), F32),
                        pltpu.VMEM((n_seq, SUBLANES, JB_ST), F32),
                        pltpu.SemaphoreType.DMA((7 * 2,)), pltpu.SemaphoreType.DMA((7 * 2,)),
                        pltpu.SemaphoreType.DMA((2,))],
        compiler_params=_params(2),
    )(dy, u, s_re, s_im, s_re, s_im, bb_re, bb_im, c_re_t, c_imn_t, d_row, ab_re, ab_im, g_out, g_glu)


def _mix(x2, tgt2, y, proj, gf, b_glu, conv8, w_glu_f, w_out_f, seq):
    n = x2.shape[0]
    tm = TOK_TILE
    tiles_per_seq = seq // tm
    rows8 = tm // SUBLANES

    def body(x_ref, t_ref, y_ref, zs_ref, h_ref, bc_ref, cc_ref, zc_ref, hp_ref, ccp_ref,
             gf_ref, bg_ref, cw_ref, wg_ref, wo_ref,
             dh2_ref, dy_ref, dzs_ref, dbc_ref, dzc_ref, dyc_ref,
             dwo_ref, dwg_ref, loss_ref, dgf_ref, dbg_ref, dcw_ref):
        i = pl.program_id(0)

        @pl.when(i == 0)
        def _():
            for r in (dwo_ref, dwg_ref, loss_ref, dgf_ref, dbg_ref, dcw_ref):
                r[...] = jnp.zeros_like(r)

        yv = y_ref[...]
        y1, dgelu = _gelu_and_grad(yv)
        y1b = y1.astype(BF16)
        gate = _sigmoid(_dot(y1b, wg_ref[...]) + bg_ref[...])
        y2 = y1 * gate
        szs, dszs = _silu_and_grad(zs_ref[...])
        yssm = y2 * szs
        hv = h_ref[...]
        ccv = cc_ref[...]
        bcv = bc_ref[...]
        v = ccv * hv
        first = (i % tiles_per_seq) == 0
        vhalo = jnp.where(first, 0.0, ccp_ref[...] * hp_ref[...])
        v1 = _shift_down(v, vhalo, 1)
        v2 = _shift_down(v, vhalo, 2)
        w0 = cw_ref[0:1, :]
        w1 = cw_ref[1:2, :]
        w2 = cw_ref[2:3, :]
        yc = w0 * v2 + w1 * v1 + w2 * v
        szc, dszc = _silu_and_grad(zc_ref[...])
        yconv = (bcv * yc) * szc
        ysb = yssm.astype(BF16)
        ycb = yconv.astype(BF16)
        h2 = x_ref[...] + _dot(ysb, wo_ref[0:SSM_W, :]) + _dot(ycb, wo_ref[SSM_W:, :])
        r2 = lax.rsqrt(jnp.mean(h2 * h2, axis=-1, keepdims=True) + EPS)
        hn = h2 * r2
        gfv = gf_ref[...]
        err = hn * gfv - t_ref[...]
        loss_ref[...] += 0.5 * jnp.sum(jnp.mean(err * err, axis=-1, keepdims=True))
        dout = err * (1.0 / D_MODEL)
        dgf_ref[...] += jnp.sum(dout * hn, axis=0, keepdims=True)
        dn = dout * gfv
        dh2 = r2 * (dn - hn * jnp.mean(dn * hn, axis=-1, keepdims=True))
        dh2_ref[...] = dh2
        dh2b = dh2.astype(BF16)
        dwo_ref[0:SSM_W, :] += _dot_tn(ysb, dh2b)
        dwo_ref[SSM_W:, :] += _dot_tn(ycb, dh2b)
        dyssm = _dot_nt(dh2b, wo_ref[0:SSM_W, :])
        dyconv = _dot_nt(dh2b, wo_ref[SSM_W:, :])
        dy2 = dyssm * szs
        dzs_ref[...] = (dyssm * y2 * dszs).astype(BF16)
        dgp = dy2 * y1 * (gate * (1.0 - gate))
        dgpb = dgp.astype(BF16)
        dy1 = dy2 * gate + _dot_nt(dgpb, wg_ref[...])
        dwg_ref[...] += _dot_tn(y1b, dgpb)
        dbg_ref[...] += jnp.sum(dgp, axis=0, keepdims=True)
        dy_ref[...] = dy1 * dgelu
        dbc_ref[...] = (dyconv * yc * szc).astype(BF16)
        dyc = dyconv * bcv * szc
        dyc_ref[...] = dyc
        dzc_ref[...] = (dyconv * bcv * yc * dszc).astype(BF16)
        dcw_ref[0:1, :] += jnp.sum(dyc * v2, axis=0, keepdims=True)
        dcw_ref[1:2, :] += jnp.sum(dyc * v1, axis=0, keepdims=True)
        dcw_ref[2:3, :] += jnp.sum(dyc * v, axis=0, keepdims=True)

    tile_d = pl.BlockSpec((tm, D_MODEL), lambda i: (i, 0))
    tile_s = pl.BlockSpec((tm, SSM_W), lambda i: (i, 0))
    seg_of = lambda c: pl.BlockSpec((tm, SSM_W), lambda i: (i, c))
    halo_of = lambda c: pl.BlockSpec((SUBLANES, SSM_W), lambda i: (jnp.maximum(i * rows8 - 1, 0), c))
    const = lambda shape: pl.BlockSpec(shape, lambda i: (0,) * len(shape))
    seg = _out((n, SSM_W), F32)
    seg_b = _out((n, SSM_W), BF16)
    return _pcall(
        body, name="mix", grid=(n // tm,),
        out_shape=(_out((n, D_MODEL), F32), seg, seg_b, seg_b, seg_b, seg,
                   _out((D_MODEL, D_MODEL), F32), _out((SSM_W, SSM_W), F32),
                   _out((SUBLANES, LANES), F32), _out((1, D_MODEL), F32),
                   _out((1, SSM_W), F32), _out((SUBLANES, CONV_W), F32)),
        in_specs=[tile_d, tile_d, tile_s, seg_of(SEG_ZS), seg_of(SEG_H), seg_of(SEG_BC), seg_of(SEG_CC), seg_of(SEG_ZC),
                  halo_of(SEG_H), halo_of(SEG_CC),
                  const((1, D_MODEL)), const((1, SSM_W)), const((SUBLANES, CONV_W)),
                  const((SSM_W, SSM_W)), const((D_MODEL, D_MODEL))],
        out_specs=(tile_d, tile_s, tile_s, tile_s, tile_s, tile_s,
                   const((D_MODEL, D_MODEL)), const((SSM_W, SSM_W)), const((SUBLANES, LANES)),
                   const((1, D_MODEL)), const((1, SSM_W)), const((SUBLANES, CONV_W))),
        compiler_params=_params(1),
    )(x2, tgt2, y, proj, proj, proj, proj, proj, proj, proj, gf, b_glu, conv8, w_glu_f, w_out_f)


def _in_bwd(x2, dh2, du, dzs, dyc, proj, dbc, dzc, g1, conv8, w_full, seq):
    n = x2.shape[0]
    tm = TOK_TILE
    n_tiles = n // tm
    tiles_per_seq = seq // tm
    rows8 = tm // SUBLANES
    n_blk8 = n // SUBLANES

    def body(x_ref, dh2_ref, du_ref, dzs_ref, dyc_ref, dycn_ref, h_ref, cc_ref, dbc_ref, dzc_ref,
             g_ref, cw_ref, w_ref, gx_ref, dp_ref, dg_ref):
        i = pl.program_id(0)

        @pl.when(i == 0)
        def _():
            dg_ref[...] = jnp.zeros_like(dg_ref)

        dyc = dyc_ref[...]
        last = (i % tiles_per_seq) == tiles_per_seq - 1
        nhalo = jnp.where(last, 0.0, dycn_ref[...])
        dv = (cw_ref[2:3, :] * dyc + cw_ref[1:2, :] * _shift_up(dyc, nhalo, 1)
              + cw_ref[0:1, :] * _shift_up(dyc, nhalo, 2))
        parts = (du_ref[...], dzs_ref[...], dv * cc_ref[...], dbc_ref[...], dv * h_ref[...], dzc_ref[...])
        dxn = jnp.zeros((tm, D_MODEL), F32)
        for k, p in enumerate(parts):
            pb = p.astype(BF16)
            dp_ref[:, k * SSM_W:(k + 1) * SSM_W] = pb
            dxn = dxn + _dot_nt(pb, w_ref[:, k * SSM_W:(k + 1) * SSM_W])
        x = x_ref[...]
        r = lax.rsqrt(jnp.mean(x * x, axis=-1, keepdims=True) + EPS)
        xh = x * r
        dg_ref[...] += jnp.sum(dxn * xh, axis=0, keepdims=True)
        dn = dxn * g_ref[...]
        gx_ref[...] = dh2_ref[...] + r * (dn - xh * jnp.mean(dn * xh, axis=-1, keepdims=True))

    tile_d = pl.BlockSpec((tm, D_MODEL), lambda i: (i, 0))
    tile_s = pl.BlockSpec((tm, SSM_W), lambda i: (i, 0))
    seg_of = lambda c: pl.BlockSpec((tm, SSM_W), lambda i: (i, c))
    nhalo = pl.BlockSpec((SUBLANES, SSM_W), lambda i: (jnp.minimum((i + 1) * rows8, n_blk8 - 1), 0))
    const = lambda shape: pl.BlockSpec(shape, lambda i: (0,) * len(shape))
    return _pcall(
        body, name="in_bwd", grid=(n_tiles,),
        out_shape=(_out((n, D_MODEL), F32), _out((n, IN_COLS), BF16),
                   _out((SUBLANES, D_MODEL), F32)),
        in_specs=[tile_d, tile_d, tile_s, tile_s, tile_s, nhalo, seg_of(SEG_H), seg_of(SEG_CC), tile_s, tile_s,
                  const((1, D_MODEL)), const((SUBLANES, CONV_W)), const((D_MODEL, IN_COLS))],
        out_specs=(tile_d, pl.BlockSpec((tm, IN_COLS), lambda i: (i, 0)), const((SUBLANES, D_MODEL))),
        compiler_params=_params(1),
    )(x2, dh2, du, dzs, dyc, dyc, proj, proj, dbc, dzc, g1, conv8, w_full)


def _dw_in_exchange(order, xn, dproj, smalls):
    n = xn.shape[0]
    tk = 512
    nk = n // tk
    piece = (D_MODEL, COLS_PER_DEV)
    n_small = len(smalls)

    def body(order_ref, xn_hbm, dp_ref, *refs):
        del order_ref
        sm_refs = refs[:n_small]
        own_ref, rchip_ref = refs[n_small:n_small + 2]
        rsm_refs = refs[n_small + 2:2 * n_small + 2]
        (xn_ref, acc, stage, sbuf, xn_sems, give_send, give_recv, keep_send, keep_recv,
         sm_send, sm_recv, sm_loc) = refs[2 * n_small + 2:]
        s = pl.program_id(0)

        def xn_copy(kk):
            rows = pl.ds(pl.multiple_of(kk * tk, tk), tk)
            return pltpu.make_async_copy(xn_hbm.at[rows, :], xn_ref.at[rows, :], xn_sems.at[kk])

        @pl.when(s == 0)
        def _():
            for kk in range(nk):
                xn_copy(kk).start()
            xn_copy(0).wait()

        x, y, c = _mesh_pos()
        sib = (x, y, 1 - c)
        chips = [(1 - x, 1 - y), (1 - x, y), (x, 1 - y)]
        gather = _TwoLevelGather(list(sm_refs), [functools.partial(lambda r, dev: r.at[dev], r) for r in rsm_refs],
                                 sm_send, sm_recv, sm_loc)

        def half(i, core):
            return acc.at[i % 2, :, pl.ds(pl.multiple_of(core * COLS_PER_DEV, LANES), COLS_PER_DEV)]

        def give(i):
            return pltpu.make_async_remote_copy(src_ref=half(i, 1 - c), dst_ref=stage.at[i], send_sem=give_send.at[i],
                                                recv_sem=give_recv.at[i], device_id=sib, device_id_type=MESH)

        def keep(i):
            return pltpu.make_async_remote_copy(src_ref=sbuf.at[i], dst_ref=rchip_ref.at[i], send_sem=keep_send.at[i],
                                                recv_sem=keep_recv.at[i], device_id=(*chips[i], c), device_id_type=MESH)

        def chip_sum(i):
            give(i).wait_recv()
            mine = [acc[i % 2, :, cc * COLS_PER_DEV:(cc + 1) * COLS_PER_DEV] for cc in range(2)]
            return jnp.where(c == 0, mine[0], mine[1]) + stage[i]

        @pl.when(s == 0)
        def _():
            gather.start()

        @pl.when(s == N_CHIP // 2)
        def _():
            gather.forward()

        for k in range(2, N_CHIP):
            @pl.when(s == k)
            def _(k=k):
                give(k - 2).wait_send()

        slot = s % 2
        acc[slot] = _dot_tn(xn_ref[pl.ds(0, tk), :], dp_ref[pl.ds(0, tk), :])

        def kstep(kk, carry):
            @pl.when(s == 0)
            def _():
                xn_copy(kk).wait()

            off = pl.multiple_of(kk * tk, tk)
            acc[slot] += _dot_tn(xn_ref[pl.ds(off, tk), :], dp_ref[pl.ds(off, tk), :])
            return carry

        n_first = min(nk, 3)
        lax.fori_loop(1, n_first, kstep, 0)
        for k in range(1, N_CHIP):
            @pl.when(s == k)
            def _(k=k):
                sbuf[k - 1] = chip_sum(k - 1).astype(BF16)
                keep(k - 1).start()

        lax.fori_loop(n_first, nk, kstep, 0)

        for k in range(N_CHIP):
            @pl.when(s == k)
            def _(k=k):
                give(k).start()

        @pl.when(s == N_CHIP - 1)
        def _():
            own_ref[...] = chip_sum(N_CHIP - 1)
            give(N_CHIP - 2).wait_send()
            give(N_CHIP - 1).wait_send()
            for i in range(3):
                keep(i).wait()
            gather.finish()

    grid_spec = pltpu.PrefetchScalarGridSpec(
        num_scalar_prefetch=1, grid=(N_CHIP,),
        in_specs=[HBM_SPEC,
                  pl.BlockSpec((n, COLS_PER_CHIP), lambda s, order: (0, order[s])),
                  *([HBM_SPEC] * n_small)],
        out_specs=(pl.BlockSpec(piece, lambda s, order: (0, 0)), HBM_SPEC, *([HBM_SPEC] * n_small)),
        scratch_shapes=[pltpu.VMEM((n, D_MODEL), BF16),
                        pltpu.VMEM((2, D_MODEL, COLS_PER_CHIP), F32), pltpu.VMEM((4,) + piece, F32),
                        pltpu.VMEM((3,) + piece, BF16),
                        pltpu.SemaphoreType.DMA((nk,)),
                        pltpu.SemaphoreType.DMA((4,)), pltpu.SemaphoreType.DMA((4,)),
                        pltpu.SemaphoreType.DMA((3,)), pltpu.SemaphoreType.DMA((3,)),
                        pltpu.SemaphoreType.DMA((7 * n_small,)), pltpu.SemaphoreType.DMA((7 * n_small,)),
                        pltpu.SemaphoreType.DMA((n_small,))])
    return _pcall(
        body, name="dw_in_exchange", grid_spec=grid_spec,
        out_shape=(_out(piece, F32), _out((3,) + piece, BF16),
                   *(_out((N_DEV,) + a.shape, a.dtype) for a in smalls)),
        compiler_params=_params(1),
    )(order, xn, dproj, *smalls)


def _adamw(g, w, m, v):
    m_new = ADAM_B1 * m + (1.0 - ADAM_B1) * g
    v_new = ADAM_B2 * v + (1.0 - ADAM_B2) * (g * g)
    m_hat = m_new / (1.0 - ADAM_B1 ** ADAM_STEP)
    v_hat = v_new / (1.0 - ADAM_B2 ** ADAM_STEP)
    delta = -ADAM_LR * (m_hat / (jnp.sqrt(v_hat) + ADAM_EPS) + ADAM_WD * w)
    return delta, m_new, v_new


def _reduce_adam_w_in(own, rchip, w, m, v):
    rows, cols = w.shape
    row_tile = 256

    def body(o_ref, r_ref, w_ref, m_ref, v_ref, g_ref, d_ref, nm_ref, nv_ref):
        g = o_ref[...]
        for s in range(3):
            g = g + r_ref[s].astype(F32)
        g_ref[...] = g
        d_ref[...], nm_ref[...], nv_ref[...] = _adamw(g, w_ref[...], m_ref[...], v_ref[...])

    tile = pl.BlockSpec((row_tile, cols), lambda i: (i, 0))
    shp = _out((rows, cols), F32)
    return _pcall(
        body, name="reduce_adam_w_in", grid=(rows // row_tile,),
        out_shape=(shp,) * 4,
        in_specs=[tile, pl.BlockSpec((3, row_tile, cols), lambda i: (0, i, 0)), tile, tile, tile],
        out_specs=(tile,) * 4,
        compiler_params=_params(1),
    )(own, rchip, w, m, v)


_SMALL_LEAVES = ("norm_gain", "final_norm_gain", "b_glu", "ssm_a_re", "ssm_a_im", "ssm_log_dt", "ssm_d", "conv_w",
                 "ssm_c_re", "ssm_c_im", "ssm_b_re", "ssm_b_im")


def _reduce_adam_small(r_pack, r_gc, r_gb, wmv, sharded):
    n_leaf = len(_SMALL_LEAVES)
    n_sh = len(sharded)

    def body(*refs):
        rp_ref, rgc_ref, rgb_ref = refs[:3]
        w_refs = refs[3:3 + 3 * n_leaf]
        sh_in = refs[3 + 3 * n_leaf:3 + 3 * n_leaf + 4 * n_sh]
        outs0 = 3 + 3 * n_leaf + 4 * n_sh
        loss_ref = refs[outs0]
        o_refs = refs[outs0 + 1:outs0 + 1 + 4 * n_leaf]
        sh_out = refs[outs0 + 1 + 4 * n_leaf:outs0 + 1 + 4 * n_leaf + 4 * n_sh]
        own_conv = refs[-1]

        def total(ref):
            acc = ref[0].astype(F32)
            for s in range(1, N_DEV):
                acc = acc + ref[s].astype(F32)
            return acc

        for i in range(n_sh):
            r_ref, w_ref, m_ref, v_ref = sh_in[4 * i:4 * i + 4]
            o_g, o_d, o_m, o_v = sh_out[4 * i:4 * i + 4]
            g = total(r_ref)
            o_g[...] = g
            o_d[...], o_m[...], o_v[...] = _adamw(g, w_ref[...], m_ref[...], v_ref[...])

        sp = total(rp_ref)
        sgc = total(rgc_ref)
        sgb = total(rgb_ref)
        loss_ref[...] = sp[ROW_LOSS:ROW_LOSS + SUBLANES, 0:LANES]

        def wide(r):
            return jnp.concatenate([sp[r:r + 1, :], sp[r + 1:r + 2, :]], axis=1)

        s5 = slice(ROW_S5, ROW_S5 + N_GROUPS)
        eye = (lax.broadcasted_iota(jnp.int32, (N_GROUPS, N_GROUPS), 0)
               == lax.broadcasted_iota(jnp.int32, (N_GROUPS, N_GROUPS), 1)).astype(F32)
        d_row = sp[ROW_BGLU_D + 1:ROW_BGLU_D + 2, :]
        me = 4 * lax.axis_index("x") + 2 * lax.axis_index("y") + lax.axis_index("c")
        for k in range(N_DEV):
            @pl.when(me == k)
            def _(k=k):
                own_conv[...] = sp[ROW_CONV:ROW_CONV + SUBLANES, k * CONV_COLS_PER_DEV:(k + 1) * CONV_COLS_PER_DEV]
        grads = {
            "norm_gain": wide(ROW_NORM_GAIN),
            "final_norm_gain": wide(ROW_FINAL_GAIN),
            "b_glu": sp[ROW_BGLU_D:ROW_BGLU_D + 1, :],
            "ssm_a_re": sp[s5, LANE_A_RE:LANE_A_RE + STATE],
            "ssm_a_im": sp[s5, LANE_A_IM:LANE_A_IM + STATE],
            "ssm_log_dt": jnp.sum(sp[s5, LANE_LOG_DT:LANE_LOG_DT + 1] * eye, axis=0, keepdims=True),
            "ssm_d": jnp.concatenate([d_row[:, g * GROUP:(g + 1) * GROUP] for g in range(N_GROUPS)], axis=0),
            "conv_w": own_conv[0:3, :],
            "ssm_c_re": sgc[:, 0:STATE],
            "ssm_c_im": sgc[:, STATE:2 * STATE],
            "ssm_b_re": sgb[:, 0:STATE],
            "ssm_b_im": sgb[:, STATE:2 * STATE],
        }
        for i, name in enumerate(_SMALL_LEAVES):
            g = grads[name]
            w_ref, m_ref, v_ref = w_refs[3 * i:3 * i + 3]
            o_g, o_d, o_m, o_v = o_refs[4 * i:4 * i + 4]
            o_g[...] = g
            o_d[...], o_m[...], o_v[...] = _adamw(g, w_ref[...], m_ref[...], v_ref[...])

    flat_w = [a for name in _SMALL_LEAVES for a in wmv[name]]
    leaf_shapes = [_out(wmv[name][0].shape, F32) for name in _SMALL_LEAVES for _ in range(4)]
    sh_shapes = [_out(entry[1].shape, F32) for entry in sharded for _ in range(4)]
    operands = (r_pack, r_gc, r_gb, *flat_w, *(a for entry in sharded for a in entry))
    out_shape = (_out((SUBLANES, LANES), F32), *leaf_shapes, *sh_shapes)
    outs = _pcall(
        body, name="reduce_adam_small", grid=(1,), out_shape=out_shape,
        in_specs=_whole_specs(operands), out_specs=tuple(_whole_specs(out_shape)),
        scratch_shapes=[pltpu.VMEM((SUBLANES, CONV_COLS_PER_DEV), F32)],
        compiler_params=_params(1),
    )(*operands)
    leaves = {name: outs[1 + 4 * i:5 + 4 * i] for i, name in enumerate(_SMALL_LEAVES)}
    first = 1 + 4 * n_leaf
    return outs[0], leaves, [outs[first + 4 * i:first + 4 * i + 4] for i in range(n_sh)]


def kernel(x, norm_gain, w_in, ssm_a_re, ssm_a_im, ssm_log_dt, ssm_b_re, ssm_b_im, ssm_c_re, ssm_c_im, ssm_d, w_glu, b_glu, conv_w, w_out, final_norm_gain, loss_target, m_norm_gain, m_w_in, m_ssm_a_re, m_ssm_a_im, m_ssm_log_dt, m_ssm_b_re, m_ssm_b_im, m_ssm_c_re, m_ssm_c_im, m_ssm_d, m_w_glu, m_b_glu, m_conv_w, m_w_out, m_final_norm_gain, v_norm_gain, v_w_in, v_ssm_a_re, v_ssm_a_im, v_ssm_log_dt, v_ssm_b_re, v_ssm_b_im, v_ssm_c_re, v_ssm_c_im, v_ssm_d, v_w_glu, v_b_glu, v_conv_w, v_w_out, v_final_norm_gain):
    n_seq, seq, _ = x.shape
    n = n_seq * seq

    gh_p = lambda b4: jnp.transpose(b4, (0, 1, 3, 2)).reshape(N_GROUPS * GROUP, STATE)
    c2 = lambda a: a.reshape(N_GROUPS * GROUP, STATE)
    b_re2, b_im2 = gh_p(ssm_b_re), gh_p(ssm_b_im)
    d_row = ssm_d[0].reshape(1, SSM_W)

    x2 = x.reshape(n, D_MODEL)
    tgt2 = loss_target.reshape(n, D_MODEL)
    mx, my, mc = lax.axis_index("x"), lax.axis_index("y"), lax.axis_index("c")
    chip_ids = [2 * cx + cy for cx, cy in ((mx, my), (1 - mx, my), (mx, 1 - my), (1 - mx, 1 - my))]
    arrival = chip_ids
    xn, proj, w_in_f, s5 = _in_proj(
        jnp.stack(arrival).astype(jnp.int32), x2, norm_gain, w_in[0].astype(BF16),
        (ssm_a_re[0], ssm_a_im[0], ssm_log_dt, b_re2, b_im2, c2(ssm_c_re), c2(ssm_c_im)))
    a_re_x, a_im_x, log_dt_x, ab_re, ab_im, bb_re_m, bb_im_m, c_re_m, c_imn_m = s5
    u3 = proj.reshape(n_seq, seq, IN_COLS)
    conv_p = jnp.pad(conv_w[0], ((0, SUBLANES - 3), (0, LANES - CONV_COLS_PER_DEV)))
    s_re, s_im, y3, w_out_f, w_glu_f, conv_all = _ssm_fwd(
        u3, bb_re_m, bb_im_m, c_re_m, c_imn_m, d_row, ab_re, ab_im,
        w_out[0].astype(BF16), w_glu[0].astype(BF16), conv_p, n_seq, seq)
    conv8 = jnp.transpose(conv_all[:, :, :CONV_COLS_PER_DEV], (1, 0, 2)).reshape(SUBLANES, CONV_W)
    (dh2, dy, dzs, dbc, dzc, dyc, dw_out, dw_glu, loss_t, dgf, dbg, dcw) = _mix(
        x2, tgt2, y3.reshape(n, SSM_W), proj, final_norm_gain.reshape(1, D_MODEL), b_glu, conv8,
        w_glu_f, w_out_f, seq)

    du3, dc_re_d, dc_im_d, dbb_re_d, dbb_im_d, dab_re, dab_im, dd, r_out, r_glu = _ssm_bwd(
        dy.reshape(n_seq, seq, SSM_W), u3, s_re, s_im, bb_re_m, bb_im_m, c_re_m, c_imn_m, d_row, ab_re, ab_im,
        dw_out.reshape(N_DEV, OUT_ROWS_PER_DEV, D_MODEL), dw_glu.reshape(N_DEV, GLU_ROWS_PER_DEV, SSM_W), n_seq, seq)
    du = du3.reshape(n, SSM_W)
    grad_x2, dproj, dg8 = _in_bwd(x2, dh2, du, dzs, dyc, proj, dbc, dzc, norm_gain, conv8, w_in_f, seq)
    pack, gc, gb = _ssm_disc_bwd_pack(
        a_re_x, a_im_x, log_dt_x, b_re2, b_im2, dab_re.reshape(N_GROUPS, STATE), dab_im.reshape(N_GROUPS, STATE),
        dbb_re_d, dbb_im_d, loss_t, dg8, dgf, dbg, dd, dcw, dc_re_d, dc_im_d)

    order = [chip_ids[3], chip_ids[1], chip_ids[2], chip_ids[0]]
    own_in, rchip_in, r_pack, r_gc, r_gb = _dw_in_exchange(
        jnp.stack(order).astype(jnp.int32), xn, dproj, [pack, gc, gb])

    flat2 = lambda a: a.reshape(a.shape[-2:]) if a.ndim > 2 else a.reshape(1, -1)
    c2 = lambda a: a.reshape(N_GROUPS * GROUP, STATE)
    wmv = dict(norm_gain=(norm_gain, m_norm_gain, v_norm_gain),
               final_norm_gain=tuple(flat2(a) for a in (final_norm_gain, m_final_norm_gain, v_final_norm_gain)),
               b_glu=(b_glu, m_b_glu, v_b_glu),
               ssm_a_re=tuple(flat2(a) for a in (ssm_a_re, m_ssm_a_re, v_ssm_a_re)),
               ssm_a_im=tuple(flat2(a) for a in (ssm_a_im, m_ssm_a_im, v_ssm_a_im)),
               ssm_log_dt=(ssm_log_dt, m_ssm_log_dt, v_ssm_log_dt),
               ssm_d=tuple(flat2(a) for a in (ssm_d, m_ssm_d, v_ssm_d)),
               conv_w=tuple(flat2(a) for a in (conv_w, m_conv_w, v_conv_w)),
               ssm_c_re=tuple(c2(a) for a in (ssm_c_re, m_ssm_c_re, v_ssm_c_re)),
               ssm_c_im=tuple(c2(a) for a in (ssm_c_im, m_ssm_c_im, v_ssm_c_im)),
               ssm_b_re=(b_re2, gh_p(m_ssm_b_re), gh_p(v_ssm_b_re)),
               ssm_b_im=(b_im2, gh_p(m_ssm_b_im), gh_p(v_ssm_b_im)))

    res_in = _reduce_adam_w_in(own_in, rchip_in, w_in[0], m_w_in[0], v_w_in[0])
    loss8, small, (res_out, res_glu) = _reduce_adam_small(
        r_pack, r_gc, r_gb, wmv,
        [(r_out, w_out[0], m_w_out[0], v_w_out[0]), (r_glu, w_glu[0], m_w_glu[0], v_w_glu[0])])
    loss = loss8[0, 0]

    shapes = dict(norm_gain=(1, D_MODEL), ssm_a_re=(1, N_GROUPS, STATE), ssm_a_im=(1, N_GROUPS, STATE),
                  ssm_log_dt=(1, N_GROUPS), ssm_c_re=(1, N_GROUPS, GROUP, STATE), ssm_c_im=(1, N_GROUPS, GROUP, STATE),
                  ssm_d=(1, N_GROUPS, GROUP), b_glu=(1, SSM_W), final_norm_gain=(D_MODEL,),
                  conv_w=(1, 3, CONV_COLS_PER_DEV))
    big = dict(w_in=res_in, w_glu=res_glu, w_out=res_out)

    def leaf(kind, name):
        if name in big:
            return big[name][kind][None]
        if name in ("ssm_b_re", "ssm_b_im"):
            return jnp.transpose(small[name][kind].reshape(1, N_GROUPS, GROUP, STATE), (0, 1, 3, 2))
        return small[name][kind].reshape(shapes[name])

    order = ["norm_gain", "w_in", "ssm_a_re", "ssm_a_im", "ssm_log_dt", "ssm_b_re", "ssm_b_im", "ssm_c_re",
             "ssm_c_im", "ssm_d", "w_glu", "b_glu", "conv_w", "w_out", "final_norm_gain"]
    outs = [loss, grad_x2.reshape(x.shape)]
    for kind in range(4):
        outs += [leaf(kind, name) for name in order]
    return tuple(outs)
```

```python
import functools
import math

import jax
import jax.numpy as jnp
from jax import lax
from jax.experimental import pallas as pl
from jax.experimental.pallas import tpu as pltpu

F32 = jnp.float32
BF16 = jnp.bfloat16

N_DEV = 8
D_MODEL = 1024
SSM_W = 512
CONV_W = 512
N_GROUPS = 32
GROUP = 16
STATE = 64
IN_COLS = 3072
SEG_U, SEG_ZS, SEG_H, SEG_BC, SEG_CC, SEG_ZC = range(6)
COLS_PER_DEV = IN_COLS // N_DEV
N_CHIP = N_DEV // 2
COLS_PER_CHIP = 2 * COLS_PER_DEV
OUT_ROWS_PER_DEV = D_MODEL // N_DEV
GLU_ROWS_PER_DEV = SSM_W // N_DEV
CONV_COLS_PER_DEV = CONV_W // N_DEV
EPS = 1e-6

N_JBLK = 4
JB_CH = SSM_W // N_JBLK
JB_ST = N_GROUPS * STATE // N_JBLK

ADAM_LR = 0.001
ADAM_B1 = 0.9
ADAM_B2 = 0.999
ADAM_EPS = 1e-08
ADAM_WD = 0.01
ADAM_STEP = 10

SUBLANES = 8
LANES = 128
VMEM_LIMIT = 48 * 1024 * 1024
TOK_TILE = 256
IN_TILE = 1024
SCAN_TILE = 1024

MESH = pl.DeviceIdType.MESH
HBM_SPEC = pl.BlockSpec(memory_space=pltpu.HBM)


def _build(body, **kw):
    return pl.pallas_call(body, **kw)


def _pcall(body, **kw):
    def call(*operands):
        pinned = [a if jnp.issubdtype(a.dtype, jnp.integer) else pltpu.with_memory_space_constraint(a, pltpu.HBM)
                  for a in operands]
        return _build(body, **kw)(*pinned)
    return call


def _whole_specs(arrays):
    return [pl.BlockSpec(a.shape, functools.partial(lambda nd, i: (0,) * nd, len(a.shape))) for a in arrays]


def _out(shape, dtype):
    return pltpu.HBM(tuple(shape), dtype)


def _params(n_grid):
    return pltpu.CompilerParams(dimension_semantics=("arbitrary",) * n_grid,
                                vmem_limit_bytes=VMEM_LIMIT)


def _dot(a, b):
    return jnp.dot(a, b, preferred_element_type=F32)


def _dot_nt(a, b):
    return lax.dot_general(a, b, (((1,), (1,)), ((), ())), preferred_element_type=F32)


def _dot_tn(a, b):
    return lax.dot_general(a, b, (((0,), (0,)), ((), ())), preferred_element_type=F32)


def _sigmoid(z):
    return 1.0 / (1.0 + jnp.exp(-z))


_GELU_C = math.sqrt(2.0 / math.pi)


def _gelu_and_grad(y):
    inner = _GELU_C * (y + 0.044715 * (y * y * y))
    t = jnp.tanh(inner)
    g = 0.5 * y * (1.0 + t)
    dg = 0.5 * (1.0 + t) + 0.5 * y * (1.0 - t * t) * (_GELU_C * (1.0 + 3.0 * 0.044715 * (y * y)))
    return g, dg


def _silu_and_grad(z):
    s = _sigmoid(z)
    return z * s, s * (1.0 + z * (1.0 - s))


def _shift_down(v, halo, k):
    rolled = pltpu.roll(v, k, 0)
    row = lax.broadcasted_iota(jnp.int32, v.shape, 0)
    for r in range(k):
        rolled = jnp.where(row == r, halo[SUBLANES - k + r:SUBLANES - k + r + 1, :], rolled)
    return rolled


def _shift_up(v, halo, k):
    n = v.shape[0]
    rolled = pltpu.roll(v, n - k, 0)
    row = lax.broadcasted_iota(jnp.int32, v.shape, 0)
    for r in range(k):
        rolled = jnp.where(row == n - k + r, halo[r:r + 1, :], rolled)
    return rolled


def _mesh_pos():
    return lax.axis_index("x"), lax.axis_index("y"), lax.axis_index("c")


def _direct_copies(srcs_for, out_refs, send_sems, recv_sems, loc_sems):
    x, y, c = _mesh_pos()
    me_id = 4 * x + 2 * y + c
    n_arr = len(out_refs)
    dsts = [r.at[me_id] for r in out_refs]
    own = srcs_for(me_id)
    mine = [pltpu.make_async_copy(own[a], dsts[a], loc_sems.at[a]) for a in range(n_arr)]
    sends = []
    for k in range(1, N_DEV):
        px, py, pc = x ^ ((k >> 2) & 1), y ^ ((k >> 1) & 1), c ^ (k & 1)
        src = srcs_for(4 * px + 2 * py + pc)
        for a in range(n_arr):
            sends.append(pltpu.make_async_remote_copy(
                src_ref=src[a], dst_ref=dsts[a],
                send_sem=send_sems.at[(k - 1) * n_arr + a], recv_sem=recv_sems.at[(k - 1) * n_arr + a],
                device_id=(px, py, pc), device_id_type=MESH))
    return mine, sends


class _TwoLevelGather:
    def __init__(self, srcs, slots, send_sems, recv_sems, loc_sems):
        self.srcs, self.slots, self.n_arr = srcs, slots, len(srcs)
        self.send_sems, self.recv_sems, self.loc_sems = send_sems, recv_sems, loc_sems
        x, y, c = _mesh_pos()
        self.c = c
        self.me, self.sib = (x, y, c), (x, y, 1 - c)
        self.chips = [(1 - x, y), (x, 1 - y), (1 - x, 1 - y)]

    def _copies(self, k, block, to, from_src=False):
        dev = 4 * block[0] + 2 * block[1] + block[2]
        return [pltpu.make_async_remote_copy(
            src_ref=self.srcs[a] if from_src else self.slots[a](dev), dst_ref=self.slots[a](dev),
            send_sem=self.send_sems.at[k * self.n_arr + a], recv_sem=self.recv_sems.at[k * self.n_arr + a],
            device_id=to, device_id_type=MESH) for a in range(self.n_arr)]

    def _local(self):
        dev = 4 * self.me[0] + 2 * self.me[1] + self.me[2]
        return [pltpu.make_async_copy(self.srcs[a], self.slots[a](dev), self.loc_sems.at[a])
                for a in range(self.n_arr)]

    def start(self, chips=(0, 1, 2)):
        for cp in self._local() + self._copies(0, self.me, self.sib, True):
            cp.start()
        self.start_to(chips)

    def start_to(self, chips):
        for j in chips:
            for cp in self._copies(1 + j, self.me, (*self.chips[j], self.c), True):
                cp.start()

    def wait_own(self):
        for cp in self._local():
            cp.wait()

    def wait_sibling(self):
        for cp in self._copies(0, self.sib, self.me):
            cp.wait_recv()

    def wait_and_pass_on(self, j):
        chip = self.chips[j]
        for cp in self._copies(1 + j, (*chip, self.c), self.me):
            cp.wait_recv()
        for cp in self._copies(4 + j, (*chip, self.c), self.sib):
            cp.start()

    def relay(self, k, block, to):
        for cp in self._copies(k, block, to):
            cp.start()

    def wait_passed_on(self, j):
        for cp in self._copies(4 + j, (*self.chips[j], 1 - self.c), self.me):
            cp.wait_recv()

    def wait_sends(self):
        for cp in self._copies(0, self.me, self.sib, True):
            cp.wait_send()
        for j, chip in enumerate(self.chips):
            for cp in self._copies(1 + j, self.me, (*chip, self.c), True) + self._copies(4 + j, (*chip, self.c), self.sib):
                cp.wait_send()

    def forward(self):
        for j in range(3):
            self.wait_and_pass_on(j)

    def finish(self):
        self.wait_sibling()
        for j in range(3):
            self.wait_passed_on(j)
        self.wait_sends()
        self.wait_own()


def _disc(a_re, a_im, log_dt, b_re, b_im):
    dt = jnp.exp(log_dt)
    mag = jnp.exp(a_re * dt)
    ab_re = mag * jnp.cos(a_im * dt)
    ab_im = mag * jnp.sin(a_im * dt)
    den = a_re * a_re + a_im * a_im
    p_re = ab_re - 1.0
    p_im = ab_im
    q_re = (p_re * a_re + p_im * a_im) / den
    q_im = (p_im * a_re - p_re * a_im) / den
    bb_re = q_re * b_re - q_im * b_im
    bb_im = q_re * b_im + q_im * b_re
    return ab_re, ab_im, bb_re, bb_im


def _split3(v):
    hi = v.astype(BF16)
    r1 = v - hi.astype(F32)
    mid = r1.astype(BF16)
    lo = (r1 - mid.astype(F32)).astype(BF16)
    return hi, mid, lo


def _select_dot(sel, v):
    return sum(_dot(sel, t) for t in _split3(v))


PACK_ROWS = 72
PACK_W = 512
ROW_FINAL_GAIN, ROW_NORM_GAIN, ROW_BGLU_D, ROW_CONV, ROW_LOSS, ROW_S5 = 0, 8, 16, 24, 32, 40
LANE_A_RE, LANE_A_IM, LANE_LOG_DT = 0, 128, 256


def _ssm_disc_bwd_pack(a_re_x, a_im_x, log_dt_x, b_re, b_im, g_ab_re, g_ab_im, dbb_re_d, dbb_im_d,
                       loss_t, dg8, dgf, dbg, dd, dcw, dc_re_d, dc_im_d):
    rows_gh = N_GROUPS * GROUP

    def body(are, aim, ldt, bre, bim, gabre, gabim, dbbre_ref, dbbim_ref,
             loss_ref, dg8_ref, dgf_ref, dbg_ref, dd_ref, dcw_ref, dcre_ref, dcim_ref,
             p_ref, gc_ref, gb_ref, gbb_re, gbb_im):
        r_g = lax.broadcasted_iota(jnp.int32, (N_GROUPS, rows_gh), 0)
        c_gh = lax.broadcasted_iota(jnp.int32, (N_GROUPS, rows_gh), 1)
        group_sum = (c_gh // GROUP == r_g).astype(BF16)
        r_gh = lax.broadcasted_iota(jnp.int32, (rows_gh, N_GROUPS), 0)
        c_g = lax.broadcasted_iota(jnp.int32, (rows_gh, N_GROUPS), 1)
        first_row = (r_gh == c_g * GROUP).astype(BF16)

        def diag_block(ref, j, gi):
            return ref[j, gi * GROUP:(gi + 1) * GROUP, gi * STATE:(gi + 1) * STATE]

        for j in range(N_JBLK):
            for gi in range(SUBLANES):
                r0 = (j * SUBLANES + gi) * GROUP
                gbb_re[r0:r0 + GROUP, :] = diag_block(dbbre_ref, j, gi)
                gbb_im[r0:r0 + GROUP, :] = diag_block(dbbim_ref, j, gi)
                both = jnp.concatenate([diag_block(dcre_ref, j, gi), -diag_block(dcim_ref, j, gi)], axis=1)
                gc_ref[r0:r0 + GROUP, :] = both.astype(BF16)

        _, vjp = jax.vjp(_disc, are[...], aim[...], ldt[...], bre[...], bim[...])
        d_are, d_aim, d_ldt, d_bre, d_bim = vjp((_select_dot(first_row, gabre[...]), _select_dot(first_row, gabim[...]),
                                                 gbb_re[...], gbb_im[...]))
        gb_ref[...] = jnp.concatenate([d_bre, d_bim], axis=1).astype(BF16)

        p_ref[...] = jnp.zeros_like(p_ref)
        half = D_MODEL // 2
        for r, src in ((ROW_FINAL_GAIN, dgf_ref), (ROW_NORM_GAIN, dg8_ref)):
            p_ref[r:r + 1, :] = src[0:1, 0:half]
            p_ref[r + 1:r + 2, :] = src[0:1, half:D_MODEL]
        p_ref[ROW_BGLU_D:ROW_BGLU_D + 1, :] = dbg_ref[...]
        p_ref[ROW_BGLU_D + 1:ROW_BGLU_D + 2, :] = dd_ref[...]
        p_ref[ROW_CONV:ROW_CONV + SUBLANES, :] = dcw_ref[...]
        p_ref[ROW_LOSS:ROW_LOSS + SUBLANES, 0:LANES] = loss_ref[...]
        s5 = slice(ROW_S5, ROW_S5 + N_GROUPS)
        p_ref[s5, LANE_A_RE:LANE_A_RE + STATE] = _select_dot(group_sum, d_are)
        p_ref[s5, LANE_A_IM:LANE_A_IM + STATE] = _select_dot(group_sum, d_aim)
        p_ref[s5, LANE_LOG_DT:LANE_LOG_DT + LANES] = _select_dot(group_sum, jnp.broadcast_to(d_ldt, (rows_gh, LANES)))

    operands = (a_re_x, a_im_x, log_dt_x, b_re, b_im, g_ab_re, g_ab_im, dbb_re_d, dbb_im_d,
                loss_t, dg8, dgf, dbg, dd, dcw, dc_re_d, dc_im_d)
    out_shape = (_out((PACK_ROWS, PACK_W), F32),
                 _out((rows_gh, 2 * STATE), BF16),
                 _out((rows_gh, 2 * STATE), BF16))
    return _pcall(body, name="ssm_disc_bwd_pack", grid=(1,), out_shape=out_shape,
                  in_specs=_whole_specs(operands), out_specs=tuple(_whole_specs(out_shape)),
                  scratch_shapes=[pltpu.VMEM((rows_gh, STATE), F32), pltpu.VMEM((rows_gh, STATE), F32)],
                  compiler_params=_params(1))(*operands)


def _s5_prepare(are, aim, ldt, bre, bim, cre, cim,
                o_ax_re, o_ax_im, o_ldt_x, o_ab_re, o_ab_im, o_bb_re, o_bb_im, o_c_re, o_c_imn):
    rows_gh = N_GROUPS * GROUP
    rep = (lax.broadcasted_iota(jnp.int32, (rows_gh, N_GROUPS), 0) // GROUP
           == lax.broadcasted_iota(jnp.int32, (rows_gh, N_GROUPS), 1)).astype(BF16)
    eye = (lax.broadcasted_iota(jnp.int32, (N_GROUPS, N_GROUPS), 0)
           == lax.broadcasted_iota(jnp.int32, (N_GROUPS, N_GROUPS), 1)).astype(F32)
    ldt_col = jnp.sum(eye * ldt[...], axis=1, keepdims=True)
    a_re_x = _select_dot(rep, are[...])
    a_im_x = _select_dot(rep, aim[...])
    ldt_x = _select_dot(rep, jnp.broadcast_to(ldt_col, (N_GROUPS, LANES)))[:, 0:1]
    o_ax_re[...] = a_re_x
    o_ax_im[...] = a_im_x
    o_ldt_x[...] = ldt_x
    ab_re, ab_im, bb_re, bb_im = _disc(a_re_x, a_im_x, ldt_x, bre[...], bim[...])
    for j in range(N_JBLK):
        first = [(j * SUBLANES + gi) * GROUP for gi in range(SUBLANES)]
        o_ab_re[j] = jnp.concatenate([ab_re[r:r + 1, :] for r in first], axis=1)
        o_ab_im[j] = jnp.concatenate([ab_im[r:r + 1, :] for r in first], axis=1)
    for o, v in ((o_bb_re, bb_re), (o_bb_im, bb_im), (o_c_re, cre[...]), (o_c_imn, -cim[...])):
        for j in range(N_JBLK):
            for gi in range(SUBLANES):
                r0 = (j * SUBLANES + gi) * GROUP
                parts = [v[r0:r0 + GROUP, :] if k == gi else jnp.zeros((GROUP, STATE), F32) for k in range(SUBLANES)]
                o[j, gi * GROUP:(gi + 1) * GROUP, :] = jnp.concatenate(parts, axis=1).astype(BF16)


def _in_proj(order, x2, g1, w_in_b, s5):
    n = x2.shape[0]
    tm = min(IN_TILE, n)
    n_tiles = n // tm
    n_s5_in = len(s5)
    n_s5_out = 9

    def body(order_ref, x_ref, g_ref, w_ref, *refs):
        s5_in = refs[:n_s5_in]
        xn_ref, proj_ref, wall_ref = refs[n_s5_in:n_s5_in + 3]
        s5_out = refs[n_s5_in + 3:n_s5_in + 3 + n_s5_out]
        xn_scr, wbuf, send_sems, recv_sems, loc_sems, out_sems = refs[n_s5_in + 3 + n_s5_out:]
        k = pl.program_id(0)
        i = pl.program_id(1)

        def slot(dev):
            return wbuf.at[dev // 2, :, pl.ds(pl.multiple_of((dev % 2) * COLS_PER_DEV, LANES), COLS_PER_DEV)]

        gather = _TwoLevelGather([w_ref], [slot], send_sems, recv_sems, loc_sems)

        @pl.when((k == 0) & (i == 0))
        def _():
            gather.start(chips=(0, 1))

        def own_chip():
            gather.wait_own()
            gather.wait_sibling()

        def x_chip():
            x, y, c = _mesh_pos()
            gather.wait_and_pass_on(0)
            gather.wait_and_pass_on(1)
            gather.relay(1 + 2, (x ^ c, y ^ (1 - c), c), (x ^ (1 - c), y ^ c, c))
            gather.wait_passed_on(0)

        def diag_chip():
            gather.wait_and_pass_on(2)
            gather.wait_passed_on(2)

        arrivals = [own_chip, x_chip, functools.partial(gather.wait_passed_on, 1), diag_chip]
        for kk, arrived in enumerate(arrivals):
            @pl.when((k == kk) & (i == 0))
            def _(arrived=arrived):
                arrived()

        rows = pl.ds(pl.multiple_of(i * tm, tm), tm)

        @pl.when(k == 0)
        def _():
            x = x_ref[...]
            r = lax.rsqrt(jnp.mean(x * x, axis=-1, keepdims=True) + EPS)
            xn = ((x * r) * g_ref[...]).astype(BF16)
            xn_scr[rows, :] = xn
            xn_ref[...] = xn

        proj_ref[...] = _dot(xn_scr[rows, :], wbuf[order_ref[k]])

        @pl.when((k == 0) & (i == n_tiles - 1))
        def _():
            _s5_prepare(*s5_in, *s5_out)

        @pl.when((k == N_CHIP - 1) & (i == n_tiles - 1))
        def _():
            gather.wait_sends()
            outs = [pltpu.make_async_copy(wbuf.at[q], wall_ref.at[:, q * COLS_PER_CHIP:(q + 1) * COLS_PER_CHIP],
                                          out_sems.at[q]) for q in range(N_CHIP)]
            for cp in outs:
                cp.start()
            for cp in outs:
                cp.wait()

    tile_once = lambda k, i, order: (jnp.where(k == 0, i, n_tiles - 1), 0)
    whole = lambda shape: pl.BlockSpec(shape, lambda k, i, order: (0,) * len(shape))
    rows_gh = N_GROUPS * GROUP
    s5_out_shapes = ([(rows_gh, STATE), F32], [(rows_gh, STATE), F32], [(rows_gh, 1), F32],
                     [(N_JBLK, 1, JB_ST), F32], [(N_JBLK, 1, JB_ST), F32]) + ([(N_JBLK, JB_CH, JB_ST), BF16],) * 4
    grid_spec = pltpu.PrefetchScalarGridSpec(
        num_scalar_prefetch=1, grid=(N_CHIP, n_tiles),
        in_specs=[pl.BlockSpec((tm, D_MODEL), tile_once),
                  whole((1, D_MODEL)),
                  HBM_SPEC,
                  *(whole(a.shape) for a in s5)],
        out_specs=(pl.BlockSpec((tm, D_MODEL), tile_once),
                   pl.BlockSpec((tm, COLS_PER_CHIP), lambda k, i, order: (i, order[k])),
                   HBM_SPEC,
                   *(whole(shape) for shape, _ in s5_out_shapes)),
        scratch_shapes=[pltpu.VMEM((n, D_MODEL), BF16), pltpu.VMEM((N_CHIP, D_MODEL, COLS_PER_CHIP), BF16),
                        pltpu.SemaphoreType.DMA((7,)), pltpu.SemaphoreType.DMA((7,)), pltpu.SemaphoreType.DMA((1,)),
                        pltpu.SemaphoreType.DMA((N_CHIP,))])
    outs = _pcall(
        body, name="in_proj", grid_spec=grid_spec,
        out_shape=(_out((n, D_MODEL), BF16), _out((n, IN_COLS), F32),
                   _out((D_MODEL, IN_COLS), BF16),
                   *(_out(shape, dt) for shape, dt in s5_out_shapes)),
        compiler_params=_params(2),
    )(order, x2, g1, w_in_b, *s5)
    return outs[0], outs[1], outs[2], outs[3:]


def _cmul(p, q):
    return p[0] * q[0] - p[1] * q[1], p[0] * q[1] + p[1] * q[0]


def _scan_tables(ar, ai, width, reverse):
    pows = [(ar, ai)]
    for _ in range(SUBLANES - 1):
        pows.append(_cmul(pows[-1], (ar, ai)))
    row = lax.broadcasted_iota(jnp.int32, (SUBLANES, width), 0)

    def bc(v):
        return jnp.broadcast_to(v, (SUBLANES, width))

    levels = []
    for k in (1, 2, 4):
        keep = (row <= SUBLANES - 1 - k) if reverse else (row >= k)
        levels.append((jnp.where(keep, bc(pows[k - 1][0]), 0.0), jnp.where(keep, bc(pows[k - 1][1]), 0.0)))
    cre = jnp.zeros((SUBLANES, width), F32)
    cim = jnp.zeros((SUBLANES, width), F32)
    for r in range(SUBLANES):
        e = (SUBLANES - r) if reverse else (r + 1)
        cre = jnp.where(row == r, bc(pows[e - 1][0]), cre)
        cim = jnp.where(row == r, bc(pows[e - 1][1]), cim)
    return levels, (cre, cim)


def _load_chunked(src_ref, b, dst_ref, n_rows):
    n_blk = n_rows // SUBLANES
    for i in range(n_blk):
        dst_ref[b, i * SUBLANES:(i + 1) * SUBLANES, :] = src_ref[b, pl.ds(i, SUBLANES, stride=n_blk), :]


def _store_chunked(val, dst_ref, b, n_rows):
    n_blk = n_rows // SUBLANES
    for i in range(n_blk):
        dst_ref[b, pl.ds(i, SUBLANES, stride=n_blk), :] = val[i * SUBLANES:(i + 1) * SUBLANES, :]


def _chunk_scan(re_ref, im_ref, bs, car_ref, ar, ai, n_rows, reverse, on_block=None):
    width = re_ref.shape[2]
    n_blk = n_rows // SUBLANES
    shape = (SUBLANES, width)
    abr = jnp.broadcast_to(ar, shape)
    abi = jnp.broadcast_to(ai, shape)
    order = list(range(n_blk - 1, -1, -1)) if reverse else list(range(n_blk))

    def blk(ref, b, i):
        return ref[b, i * SUBLANES:(i + 1) * SUBLANES, :]

    def step(state, b, i):
        sr, si = state
        return abr * sr - abi * si + blk(re_ref, b, i), abr * si + abi * sr + blk(im_ref, b, i)

    finals = {b: (blk(re_ref, b, order[0]), blk(im_ref, b, order[0])) for b in bs}
    for i in order[1:]:
        for b in bs:
            finals[b] = step(finals[b], b, i)

    mr, mi = ar, ai
    for _ in range(n_blk.bit_length() - 1):
        mr, mi = _cmul((mr, mi), (mr, mi))
    levels, _ = _scan_tables(mr, mi, width, reverse)
    mbr = jnp.broadcast_to(mr, shape)
    mbi = jnp.broadcast_to(mi, shape)
    row = lax.broadcasted_iota(jnp.int32, shape, 0)
    edge_in = SUBLANES - 1 if reverse else 0
    edge_out = 0 if reverse else SUBLANES - 1
    sh1 = SUBLANES - 1 if reverse else 1
    states = {}
    for b in bs:
        fr, fi = finals[b]
        gr = jnp.where(row == edge_in, jnp.broadcast_to(car_ref[b, 0:1, :], shape), pltpu.roll(fr, sh1, 0))
        gi = jnp.where(row == edge_in, jnp.broadcast_to(car_ref[b, 1:2, :], shape), pltpu.roll(fi, sh1, 0))
        for (lr, li), k in zip(levels, (1, 2, 4)):
            sh = (SUBLANES - k) if reverse else k
            sr = pltpu.roll(gr, sh, 0)
            si = pltpu.roll(gi, sh, 0)
            gr, gi = gr + (lr * sr - li * si), gi + (lr * si + li * sr)
        car_ref[b, 0:1, :] = (fr + (mbr * gr - mbi * gi))[edge_out:edge_out + 1, :]
        car_ref[b, 1:2, :] = (fi + (mbr * gi + mbi * gr))[edge_out:edge_out + 1, :]
        states[b] = (gr, gi)

    for i in order:
        for b in bs:
            states[b] = step(states[b], b, i)
            re_ref[b, i * SUBLANES:(i + 1) * SUBLANES, :] = states[b][0]
            im_ref[b, i * SUBLANES:(i + 1) * SUBLANES, :] = states[b][1]
            if on_block is not None:
                on_block(b, i, *states[b])


def _ssm_fwd(u, bb_re, bb_im, c_re_t, c_imn_t, d_row, ab_re, ab_im, w_out_b, w_glu_b, conv_p, n_seq, seq):
    tt = min(SCAN_TILE, seq)
    nt = seq // tt

    def body(u_ref, bbre, bbim, cre, cimn, d_ref, are, aim, wout_ref, wglu_ref, cw_ref,
             sre_ref, sim_ref, y_ref, oout_ref, oglu_ref, ocw_ref,
             up_ref, car_ref, send_sems, recv_sems, loc_sems):
        j = pl.program_id(0)
        t = pl.program_id(1)
        gather = _TwoLevelGather(
            [wout_ref, wglu_ref, cw_ref],
            [lambda dev: oout_ref.at[pl.ds(pl.multiple_of(dev * OUT_ROWS_PER_DEV, OUT_ROWS_PER_DEV), OUT_ROWS_PER_DEV), :],
             lambda dev: oglu_ref.at[pl.ds(pl.multiple_of(dev * GLU_ROWS_PER_DEV, GLU_ROWS_PER_DEV), GLU_ROWS_PER_DEV), :],
             lambda dev: ocw_ref.at[dev]],
            send_sems, recv_sems, loc_sems)

        @pl.when((j == 0) & (t == 0))
        def _():
            gather.start()

        @pl.when((j == N_JBLK // 2) & (t == 0))
        def _():
            gather.forward()

        @pl.when(t == 0)
        def _():
            car_ref[...] = jnp.zeros_like(car_ref)

        bs = list(range(n_seq))
        for b in bs:
            _load_chunked(u_ref, b, up_ref, tt)
        for b in bs:
            ub = up_ref[b].astype(BF16)
            sre_ref[b] = _dot(ub, bbre[0])
            sim_ref[b] = _dot(ub, bbim[0])
            _chunk_scan(sre_ref, sim_ref, [b], car_ref, are[0], aim[0], tt, reverse=False)
        for b in bs:
            yp = (_dot_nt(sre_ref[b].astype(BF16), cre[0]) + _dot_nt(sim_ref[b].astype(BF16), cimn[0])
                  + d_ref[...] * up_ref[b])
            _store_chunked(yp, y_ref, b, tt)

        @pl.when((j == N_JBLK - 1) & (t == nt - 1))
        def _():
            gather.finish()

    tok = lambda j, t: (0, t, j)
    blk3 = lambda j, t: (j, 0, 0)
    row = lambda j, t: (0, j)
    st = _out((n_seq, seq, N_JBLK * JB_ST), F32)
    n_arr = 3
    return _pcall(
        body, name="ssm_fwd", grid=(N_JBLK, nt),
        out_shape=(st, st, _out((n_seq, seq, SSM_W), F32),
                   _out((D_MODEL, D_MODEL), BF16), _out((SSM_W, SSM_W), BF16),
                   _out((N_DEV, SUBLANES, LANES), F32)),
        in_specs=[pl.BlockSpec((n_seq, tt, JB_CH), tok),
                  pl.BlockSpec((1, JB_CH, JB_ST), blk3), pl.BlockSpec((1, JB_CH, JB_ST), blk3),
                  pl.BlockSpec((1, JB_CH, JB_ST), blk3), pl.BlockSpec((1, JB_CH, JB_ST), blk3),
                  pl.BlockSpec((1, JB_CH), row), pl.BlockSpec((1, 1, JB_ST), blk3), pl.BlockSpec((1, 1, JB_ST), blk3),
                  HBM_SPEC, HBM_SPEC, HBM_SPEC],
        out_specs=(pl.BlockSpec((n_seq, tt, JB_ST), tok), pl.BlockSpec((n_seq, tt, JB_ST), tok),
                   pl.BlockSpec((n_seq, tt, JB_CH), tok), HBM_SPEC, HBM_SPEC, HBM_SPEC),
        scratch_shapes=[pltpu.VMEM((n_seq, tt, JB_CH), F32), pltpu.VMEM((n_seq, SUBLANES, JB_ST), F32),
                        pltpu.SemaphoreType.DMA((7 * n_arr,)), pltpu.SemaphoreType.DMA((7 * n_arr,)),
                        pltpu.SemaphoreType.DMA((n_arr,))],
        compiler_params=_params(2),
    )(u, bb_re, bb_im, c_re_t, c_imn_t, d_row, ab_re, ab_im, w_out_b, w_glu_b, conv_p)


def _ssm_bwd(dy, u, s_re, s_im, bb_re, bb_im, c_re_t, c_imn_t, d_row, ab_re, ab_im, g_out, g_glu, n_seq, seq):
    tt = min(SCAN_TILE, seq)
    nt = seq // tt
    rows8 = tt // SUBLANES

    def body(dy_ref, u_ref, sre_ref, sim_ref, pre_ref, pim_ref, bbre, bbim, cre, cimn, d_ref, are, aim,
             gout_ref, gglu_ref,
             du_ref, dcre_ref, dcim_ref, dbbre_ref, dbbim_ref, dare_ref, daim_ref, dd_ref, rout_ref, rglu_ref,
             lre_ref, lim_ref, dyp_ref, up_ref, car_ref, send_sems, recv_sems, loc_sems):
        j = pl.program_id(0)
        tr = pl.program_id(1)

        def exchange():
            return _direct_copies(lambda pid: [gout_ref.at[pid], gglu_ref.at[pid]], [rout_ref, rglu_ref],
                                  send_sems, recv_sems, loc_sems)

        @pl.when((j == 0) & (tr == 0))
        def _():
            mine, sends = exchange()
            for cp in mine + sends:
                cp.start()

        @pl.when(tr == 0)
        def _():
            car_ref[...] = jnp.zeros_like(car_ref)
            for r in (dcre_ref, dcim_ref, dbbre_ref, dbbim_ref, dare_ref, daim_ref, dd_ref):
                r[...] = jnp.zeros_like(r)

        first = tr == nt - 1
        row = lax.broadcasted_iota(jnp.int32, (SUBLANES, JB_ST), 0)
        n_blk = tt // SUBLANES
        bs = list(range(n_seq))
        for b in bs:
            _load_chunked(dy_ref, b, dyp_ref, tt)
            _load_chunked(u_ref, b, up_ref, tt)
        for b in bs:
            dyb = dyp_ref[b].astype(BF16)
            lre_ref[b] = _dot(dyb, cre[0])
            lim_ref[b] = _dot(dyb, cimn[0])
        acc = {b: [jnp.zeros((SUBLANES, JB_ST), F32), jnp.zeros((SUBLANES, JB_ST), F32)] for b in bs}

        def on_block(b, i, lr, li):
            if i > 0:
                spr = sre_ref[b, (i - 1) * SUBLANES:i * SUBLANES, :]
                spi = sim_ref[b, (i - 1) * SUBLANES:i * SUBLANES, :]
            else:
                hr = jnp.where(first, 0.0, pre_ref[b, SUBLANES - 1:SUBLANES, :])
                hi = jnp.where(first, 0.0, pim_ref[b, SUBLANES - 1:SUBLANES, :])
                last_r = sre_ref[b, (n_blk - 1) * SUBLANES:n_blk * SUBLANES, :]
                last_i = sim_ref[b, (n_blk - 1) * SUBLANES:n_blk * SUBLANES, :]
                spr = jnp.where(row == 0, jnp.broadcast_to(hr, row.shape), pltpu.roll(last_r, 1, 0))
                spi = jnp.where(row == 0, jnp.broadcast_to(hi, row.shape), pltpu.roll(last_i, 1, 0))
            acc[b][0] = acc[b][0] + (lr * spr + li * spi)
            acc[b][1] = acc[b][1] + (li * spr - lr * spi)

        _chunk_scan(lre_ref, lim_ref, bs, car_ref, are[0], -aim[0], tt, reverse=True, on_block=on_block)
        for b in bs:
            dare_ref[...] += jnp.sum(acc[b][0], axis=0, keepdims=True)
            daim_ref[...] += jnp.sum(acc[b][1], axis=0, keepdims=True)
            dyp = dyp_ref[b]
            up = up_ref[b]
            dyb = dyp.astype(BF16)
            ub = up.astype(BF16)
            lrb = lre_ref[b].astype(BF16)
            lib = lim_ref[b].astype(BF16)
            dup = d_ref[...] * dyp + _dot_nt(lrb, bbre[0]) + _dot_nt(lib, bbim[0])
            _store_chunked(dup, du_ref, b, tt)
            dbbre_ref[0] += _dot_tn(ub, lrb)
            dbbim_ref[0] += _dot_tn(ub, lib)
            dcre_ref[0] += _dot_tn(dyb, sre_ref[b].astype(BF16))
            dcim_ref[0] += _dot_tn(dyb, sim_ref[b].astype(BF16))
            dd_ref[...] += jnp.sum(dyp * up, axis=0, keepdims=True)

        @pl.when((j == N_JBLK - 1) & (tr == nt - 1))
        def _():
            mine, sends = exchange()
            for cp in sends + mine:
                cp.wait()

    tok = lambda j, t: (0, nt - 1 - t, j)
    halo = lambda j, t: (0, jnp.maximum((nt - 1 - t) * rows8 - 1, 0), j)
    blk3 = lambda j, t: (j, 0, 0)
    row1 = lambda j, t: (0, j)
    acc_shape = _out((N_JBLK, JB_CH, JB_ST), F32)
    return _pcall(
        body, name="ssm_bwd", grid=(N_JBLK, nt),
        out_shape=(_out((n_seq, seq, SSM_W), F32), acc_shape, acc_shape, acc_shape, acc_shape,
                   _out((1, N_JBLK * JB_ST), F32), _out((1, N_JBLK * JB_ST), F32),
                   _out((1, SSM_W), F32),
                   _out((N_DEV,) + g_out.shape[1:], F32),
                   _out((N_DEV,) + g_glu.shape[1:], F32)),
        in_specs=[pl.BlockSpec((n_seq, tt, JB_CH), tok), pl.BlockSpec((n_seq, tt, JB_CH), tok),
                  pl.BlockSpec((n_seq, tt, JB_ST), tok), pl.BlockSpec((n_seq, tt, JB_ST), tok),
                  pl.BlockSpec((n_seq, SUBLANES, JB_ST), halo), pl.BlockSpec((n_seq, SUBLANES, JB_ST), halo),
                  pl.BlockSpec((1, JB_CH, JB_ST), blk3), pl.BlockSpec((1, JB_CH, JB_ST), blk3),
                  pl.BlockSpec((1, JB_CH, JB_ST), blk3), pl.BlockSpec((1, JB_CH, JB_ST), blk3),
                  pl.BlockSpec((1, JB_CH), row1), pl.BlockSpec((1, 1, JB_ST), blk3), pl.BlockSpec((1, 1, JB_ST), blk3),
                  HBM_SPEC, HBM_SPEC],
        out_specs=(pl.BlockSpec((n_seq, tt, JB_CH), tok),
                   pl.BlockSpec((1, JB_CH, JB_ST), blk3), pl.BlockSpec((1, JB_CH, JB_ST), blk3),
                   pl.BlockSpec((1, JB_CH, JB_ST), blk3), pl.BlockSpec((1, JB_CH, JB_ST), blk3),
                   pl.BlockSpec((1, JB_ST), row1), pl.BlockSpec((1, JB_ST), row1), pl.BlockSpec((1, JB_CH), row1),
                   HBM_SPEC, HBM_SPEC),
        scratch_shapes=[pltpu.VMEM((n_seq, tt, JB_ST), F32), pltpu.VMEM((n_seq, tt, JB_ST), F32),
                        pltpu.VMEM((n_seq, tt, JB_CH), F32), pltpu.VMEM((n_seq, tt, JB_CH), F32),
                        pltpu.VMEM((n_seq, SUBLANES, JB_ST), F32),
                        pltpu.SemaphoreType.DMA((7 * 2,)), pltpu.SemaphoreType.DMA((7 * 2,)),
                        pltpu.SemaphoreType.DMA((2,))],
        compiler_params=_params(2),
    )(dy, u, s_re, s_im, s_re, s_im, bb_re, bb_im, c_re_t, c_imn_t, d_row, ab_re, ab_im, g_out, g_glu)


def _mix(x2, tgt2, y, proj, gf, b_glu, conv8, w_glu_f, w_out_f, seq):
    n = x2.shape[0]
    tm = TOK_TILE
    tiles_per_seq = seq // tm
    rows8 = tm // SUBLANES

    def body(x_ref, t_ref, y_ref, zs_ref, h_ref, bc_ref, cc_ref, zc_ref, hp_ref, ccp_ref,
             gf_ref, bg_ref, cw_ref, wg_ref, wo_ref,
             dh2_ref, dy_ref, dzs_ref, dbc_ref, dzc_ref, dyc_ref,
             dwo_ref, dwg_ref, loss_ref, dgf_ref, dbg_ref, dcw_ref):
        i = pl.program_id(0)

        @pl.when(i == 0)
        def _():
            for r in (dwo_ref, dwg_ref, loss_ref, dgf_ref, dbg_ref, dcw_ref):
                r[...] = jnp.zeros_like(r)

        yv = y_ref[...]
        y1, dgelu = _gelu_and_grad(yv)
        y1b = y1.astype(BF16)
        gate = _sigmoid(_dot(y1b, wg_ref[...]) + bg_ref[...])
        y2 = y1 * gate
        szs, dszs = _silu_and_grad(zs_ref[...])
        yssm = y2 * szs
        hv = h_ref[...]
        ccv = cc_ref[...]
        bcv = bc_ref[...]
        v = ccv * hv
        first = (i % tiles_per_seq) == 0
        vhalo = jnp.where(first, 0.0, ccp_ref[...] * hp_ref[...])
        v1 = _shift_down(v, vhalo, 1)
        v2 = _shift_down(v, vhalo, 2)
        w0 = cw_ref[0:1, :]
        w1 = cw_ref[1:2, :]
        w2 = cw_ref[2:3, :]
        yc = w0 * v2 + w1 * v1 + w2 * v
        szc, dszc = _silu_and_grad(zc_ref[...])
        yconv = (bcv * yc) * szc
        ysb = yssm.astype(BF16)
        ycb = yconv.astype(BF16)
        h2 = x_ref[...] + _dot(ysb, wo_ref[0:SSM_W, :]) + _dot(ycb, wo_ref[SSM_W:, :])
        r2 = lax.rsqrt(jnp.mean(h2 * h2, axis=-1, keepdims=True) + EPS)
        hn = h2 * r2
        gfv = gf_ref[...]
        err = hn * gfv - t_ref[...]
        loss_ref[...] += 0.5 * jnp.sum(jnp.mean(err * err, axis=-1, keepdims=True))
        dout = err * (1.0 / D_MODEL)
        dgf_ref[...] += jnp.sum(dout * hn, axis=0, keepdims=True)
        dn = dout * gfv
        dh2 = r2 * (dn - hn * jnp.mean(dn * hn, axis=-1, keepdims=True))
        dh2_ref[...] = dh2
        dh2b = dh2.astype(BF16)
        dwo_ref[0:SSM_W, :] += _dot_tn(ysb, dh2b)
        dwo_ref[SSM_W:, :] += _dot_tn(ycb, dh2b)
        dyssm = _dot_nt(dh2b, wo_ref[0:SSM_W, :])
        dyconv = _dot_nt(dh2b, wo_ref[SSM_W:, :])
        dy2 = dyssm * szs
        dzs_ref[...] = (dyssm * y2 * dszs).astype(BF16)
        dgp = dy2 * y1 * (gate * (1.0 - gate))
        dgpb = dgp.astype(BF16)
        dy1 = dy2 * gate + _dot_nt(dgpb, wg_ref[...])
        dwg_ref[...] += _dot_tn(y1b, dgpb)
        dbg_ref[...] += jnp.sum(dgp, axis=0, keepdims=True)
        dy_ref[...] = dy1 * dgelu
        dbc_ref[...] = (dyconv * yc * szc).astype(BF16)
        dyc = dyconv * bcv * szc
        dyc_ref[...] = dyc
        dzc_ref[...] = (dyconv * bcv * yc * dszc).astype(BF16)
        dcw_ref[0:1, :] += jnp.sum(dyc * v2, axis=0, keepdims=True)
        dcw_ref[1:2, :] += jnp.sum(dyc * v1, axis=0, keepdims=True)
        dcw_ref[2:3, :] += jnp.sum(dyc * v, axis=0, keepdims=True)

    tile_d = pl.BlockSpec((tm, D_MODEL), lambda i: (i, 0))
    tile_s = pl.BlockSpec((tm, SSM_W), lambda i: (i, 0))
    seg_of = lambda c: pl.BlockSpec((tm, SSM_W), lambda i: (i, c))
    halo_of = lambda c: pl.BlockSpec((SUBLANES, SSM_W), lambda i: (jnp.maximum(i * rows8 - 1, 0), c))
    const = lambda shape: pl.BlockSpec(shape, lambda i: (0,) * len(shape))
    seg = _out((n, SSM_W), F32)
    seg_b = _out((n, SSM_W), BF16)
    return _pcall(
        body, name="mix", grid=(n // tm,),
        out_shape=(_out((n, D_MODEL), F32), seg, seg_b, seg_b, seg_b, seg,
                   _out((D_MODEL, D_MODEL), F32), _out((SSM_W, SSM_W), F32),
                   _out((SUBLANES, LANES), F32), _out((1, D_MODEL), F32),
                   _out((1, SSM_W), F32), _out((SUBLANES, CONV_W), F32)),
        in_specs=[tile_d, tile_d, tile_s, seg_of(SEG_ZS), seg_of(SEG_H), seg_of(SEG_BC), seg_of(SEG_CC), seg_of(SEG_ZC),
                  halo_of(SEG_H), halo_of(SEG_CC),
                  const((1, D_MODEL)), const((1, SSM_W)), const((SUBLANES, CONV_W)),
                  const((SSM_W, SSM_W)), const((D_MODEL, D_MODEL))],
        out_specs=(tile_d, tile_s, tile_s, tile_s, tile_s, tile_s,
                   const((D_MODEL, D_MODEL)), const((SSM_W, SSM_W)), const((SUBLANES, LANES)),
                   const((1, D_MODEL)), const((1, SSM_W)), const((SUBLANES, CONV_W))),
        compiler_params=_params(1),
    )(x2, tgt2, y, proj, proj, proj, proj, proj, proj, proj, gf, b_glu, conv8, w_glu_f, w_out_f)


def _in_bwd(x2, dh2, du, dzs, dyc, proj, dbc, dzc, g1, conv8, w_full, seq):
    n = x2.shape[0]
    tm = TOK_TILE
    n_tiles = n // tm
    tiles_per_seq = seq // tm
    rows8 = tm // SUBLANES
    n_blk8 = n // SUBLANES

    def body(x_ref, dh2_ref, du_ref, dzs_ref, dyc_ref, dycn_ref, h_ref, cc_ref, dbc_ref, dzc_ref,
             g_ref, cw_ref, w_ref, gx_ref, dp_ref, dg_ref):
        i = pl.program_id(0)

        @pl.when(i == 0)
        def _():
            dg_ref[...] = jnp.zeros_like(dg_ref)

        dyc = dyc_ref[...]
        last = (i % tiles_per_seq) == tiles_per_seq - 1
        nhalo = jnp.where(last, 0.0, dycn_ref[...])
        dv = (cw_ref[2:3, :] * dyc + cw_ref[1:2, :] * _shift_up(dyc, nhalo, 1)
              + cw_ref[0:1, :] * _shift_up(dyc, nhalo, 2))
        parts = (du_ref[...], dzs_ref[...], dv * cc_ref[...], dbc_ref[...], dv * h_ref[...], dzc_ref[...])
        dxn = jnp.zeros((tm, D_MODEL), F32)
        for k, p in enumerate(parts):
            pb = p.astype(BF16)
            dp_ref[:, k * SSM_W:(k + 1) * SSM_W] = pb
            dxn = dxn + _dot_nt(pb, w_ref[:, k * SSM_W:(k + 1) * SSM_W])
        x = x_ref[...]
        r = lax.rsqrt(jnp.mean(x * x, axis=-1, keepdims=True) + EPS)
        xh = x * r
        dg_ref[...] += jnp.sum(dxn * xh, axis=0, keepdims=True)
        dn = dxn * g_ref[...]
        gx_ref[...] = dh2_ref[...] + r * (dn - xh * jnp.mean(dn * xh, axis=-1, keepdims=True))

    tile_d = pl.BlockSpec((tm, D_MODEL), lambda i: (i, 0))
    tile_s = pl.BlockSpec((tm, SSM_W), lambda i: (i, 0))
    seg_of = lambda c: pl.BlockSpec((tm, SSM_W), lambda i: (i, c))
    nhalo = pl.BlockSpec((SUBLANES, SSM_W), lambda i: (jnp.minimum((i + 1) * rows8, n_blk8 - 1), 0))
    const = lambda shape: pl.BlockSpec(shape, lambda i: (0,) * len(shape))
    return _pcall(
        body, name="in_bwd", grid=(n_tiles,),
        out_shape=(_out((n, D_MODEL), F32), _out((n, IN_COLS), BF16),
                   _out((SUBLANES, D_MODEL), F32)),
        in_specs=[tile_d, tile_d, tile_s, tile_s, tile_s, nhalo, seg_of(SEG_H), seg_of(SEG_CC), tile_s, tile_s,
                  const((1, D_MODEL)), const((SUBLANES, CONV_W)), const((D_MODEL, IN_COLS))],
        out_specs=(tile_d, pl.BlockSpec((tm, IN_COLS), lambda i: (i, 0)), const((SUBLANES, D_MODEL))),
        compiler_params=_params(1),
    )(x2, dh2, du, dzs, dyc, dyc, proj, proj, dbc, dzc, g1, conv8, w_full)


def _dw_in_exchange(order, xn, dproj, smalls):
    n = xn.shape[0]
    tk = 512
    nk = n // tk
    piece = (D_MODEL, COLS_PER_DEV)
    n_small = len(smalls)

    def body(order_ref, xn_hbm, dp_ref, *refs):
        del order_ref
        sm_refs = refs[:n_small]
        own_ref, rchip_ref = refs[n_small:n_small + 2]
        rsm_refs = refs[n_small + 2:2 * n_small + 2]
        (xn_ref, acc, stage, sbuf, xn_sems, give_send, give_recv, keep_send, keep_recv,
         sm_send, sm_recv, sm_loc) = refs[2 * n_small + 2:]
        s = pl.program_id(0)

        def xn_copy(kk):
            rows = pl.ds(pl.multiple_of(kk * tk, tk), tk)
            return pltpu.make_async_copy(xn_hbm.at[rows, :], xn_ref.at[rows, :], xn_sems.at[kk])

        @pl.when(s == 0)
        def _():
            for kk in range(nk):
                xn_copy(kk).start()
            xn_copy(0).wait()

        x, y, c = _mesh_pos()
        sib = (x, y, 1 - c)
        chips = [(1 - x, 1 - y), (x, 1 - y), (1 - x, y)]
        gather = _TwoLevelGather(list(sm_refs), [functools.partial(lambda r, dev: r.at[dev], r) for r in rsm_refs],
                                 sm_send, sm_recv, sm_loc)

        def half(i, core):
            return acc.at[i % 2, :, pl.ds(pl.multiple_of(core * COLS_PER_DEV, LANES), COLS_PER_DEV)]

        def give(i):
            return pltpu.make_async_remote_copy(src_ref=half(i, 1 - c), dst_ref=stage.at[i], send_sem=give_send.at[i],
                                                recv_sem=give_recv.at[i], device_id=sib, device_id_type=MESH)

        def keep(i):
            return pltpu.make_async_remote_copy(src_ref=sbuf.at[i], dst_ref=rchip_ref.at[i], send_sem=keep_send.at[i],
                                                recv_sem=keep_recv.at[i], device_id=(*chips[i], c), device_id_type=MESH)

        def chip_sum(i):
            give(i).wait_recv()
            mine = [acc[i % 2, :, cc * COLS_PER_DEV:(cc + 1) * COLS_PER_DEV] for cc in range(2)]
            return jnp.where(c == 0, mine[0], mine[1]) + stage[i]

        @pl.when(s == 0)
        def _():
            gather.start()

        @pl.when(s == N_CHIP // 2)
        def _():
            gather.forward()

        for k in range(2, N_CHIP):
            @pl.when(s == k)
            def _(k=k):
                give(k - 2).wait_send()

        slot = s % 2
        acc[slot] = _dot_tn(xn_ref[pl.ds(0, tk), :], dp_ref[pl.ds(0, tk), :])

        def kstep(kk, carry):
            @pl.when(s == 0)
            def _():
                xn_copy(kk).wait()

            off = pl.multiple_of(kk * tk, tk)
            acc[slot] += _dot_tn(xn_ref[pl.ds(off, tk), :], dp_ref[pl.ds(off, tk), :])
            return carry

        n_first = min(nk, 3)
        lax.fori_loop(1, n_first, kstep, 0)
        for k in range(1, N_CHIP):
            @pl.when(s == k)
            def _(k=k):
                sbuf[k - 1] = chip_sum(k - 1).astype(BF16)
                keep(k - 1).start()

        lax.fori_loop(n_first, nk, kstep, 0)

        for k in range(N_CHIP):
            @pl.when(s == k)
            def _(k=k):
                give(k).start()

        @pl.when(s == N_CHIP - 1)
        def _():
            own_ref[...] = chip_sum(N_CHIP - 1)
            give(N_CHIP - 2).wait_send()
            give(N_CHIP - 1).wait_send()
            for i in range(3):
                keep(i).wait()
            gather.finish()

    grid_spec = pltpu.PrefetchScalarGridSpec(
        num_scalar_prefetch=1, grid=(N_CHIP,),
        in_specs=[HBM_SPEC,
                  pl.BlockSpec((n, COLS_PER_CHIP), lambda s, order: (0, order[s])),
                  *([HBM_SPEC] * n_small)],
        out_specs=(pl.BlockSpec(piece, lambda s, order: (0, 0)), HBM_SPEC, *([HBM_SPEC] * n_small)),
        scratch_shapes=[pltpu.VMEM((n, D_MODEL), BF16),
                        pltpu.VMEM((2, D_MODEL, COLS_PER_CHIP), F32), pltpu.VMEM((4,) + piece, F32),
                        pltpu.VMEM((3,) + piece, BF16),
                        pltpu.SemaphoreType.DMA((nk,)),
                        pltpu.SemaphoreType.DMA((4,)), pltpu.SemaphoreType.DMA((4,)),
                        pltpu.SemaphoreType.DMA((3,)), pltpu.SemaphoreType.DMA((3,)),
                        pltpu.SemaphoreType.DMA((7 * n_small,)), pltpu.SemaphoreType.DMA((7 * n_small,)),
                        pltpu.SemaphoreType.DMA((n_small,))])
    return _pcall(
        body, name="dw_in_exchange", grid_spec=grid_spec,
        out_shape=(_out(piece, F32), _out((3,) + piece, BF16),
                   *(_out((N_DEV,) + a.shape, a.dtype) for a in smalls)),
        compiler_params=_params(1),
    )(order, xn, dproj, *smalls)


def _adamw(g, w, m, v):
    m_new = ADAM_B1 * m + (1.0 - ADAM_B1) * g
    v_new = ADAM_B2 * v + (1.0 - ADAM_B2) * (g * g)
    m_hat = m_new / (1.0 - ADAM_B1 ** ADAM_STEP)
    v_hat = v_new / (1.0 - ADAM_B2 ** ADAM_STEP)
    delta = -ADAM_LR * (m_hat / (jnp.sqrt(v_hat) + ADAM_EPS) + ADAM_WD * w)
    return delta, m_new, v_new


def _reduce_adam_w_in(own, rchip, w, m, v):
    rows, cols = w.shape
    row_tile = 256

    def body(o_ref, r_ref, w_ref, m_ref, v_ref, g_ref, d_ref, nm_ref, nv_ref):
        g = o_ref[...]
        for s in range(3):
            g = g + r_ref[s].astype(F32)
        g_ref[...] = g
        d_ref[...], nm_ref[...], nv_ref[...] = _adamw(g, w_ref[...], m_ref[...], v_ref[...])

    tile = pl.BlockSpec((row_tile, cols), lambda i: (i, 0))
    shp = _out((rows, cols), F32)
    return _pcall(
        body, name="reduce_adam_w_in", grid=(rows // row_tile,),
        out_shape=(shp,) * 4,
        in_specs=[tile, pl.BlockSpec((3, row_tile, cols), lambda i: (0, i, 0)), tile, tile, tile],
        out_specs=(tile,) * 4,
        compiler_params=_params(1),
    )(own, rchip, w, m, v)


_SMALL_LEAVES = ("norm_gain", "final_norm_gain", "b_glu", "ssm_a_re", "ssm_a_im", "ssm_log_dt", "ssm_d", "conv_w",
                 "ssm_c_re", "ssm_c_im", "ssm_b_re", "ssm_b_im")


def _reduce_adam_small(r_pack, r_gc, r_gb, wmv, sharded):
    n_leaf = len(_SMALL_LEAVES)
    n_sh = len(sharded)

    def body(*refs):
        rp_ref, rgc_ref, rgb_ref = refs[:3]
        w_refs = refs[3:3 + 3 * n_leaf]
        sh_in = refs[3 + 3 * n_leaf:3 + 3 * n_leaf + 4 * n_sh]
        outs0 = 3 + 3 * n_leaf + 4 * n_sh
        loss_ref = refs[outs0]
        o_refs = refs[outs0 + 1:outs0 + 1 + 4 * n_leaf]
        sh_out = refs[outs0 + 1 + 4 * n_leaf:outs0 + 1 + 4 * n_leaf + 4 * n_sh]
        own_conv = refs[-1]

        def total(ref):
            acc = ref[0].astype(F32)
            for s in range(1, N_DEV):
                acc = acc + ref[s].astype(F32)
            return acc

        for i in range(n_sh):
            r_ref, w_ref, m_ref, v_ref = sh_in[4 * i:4 * i + 4]
            o_g, o_d, o_m, o_v = sh_out[4 * i:4 * i + 4]
            g = total(r_ref)
            o_g[...] = g
            o_d[...], o_m[...], o_v[...] = _adamw(g, w_ref[...], m_ref[...], v_ref[...])

        sp = total(rp_ref)
        sgc = total(rgc_ref)
        sgb = total(rgb_ref)
        loss_ref[...] = sp[ROW_LOSS:ROW_LOSS + SUBLANES, 0:LANES]

        def wide(r):
            return jnp.concatenate([sp[r:r + 1, :], sp[r + 1:r + 2, :]], axis=1)

        s5 = slice(ROW_S5, ROW_S5 + N_GROUPS)
        eye = (lax.broadcasted_iota(jnp.int32, (N_GROUPS, N_GROUPS), 0)
               == lax.broadcasted_iota(jnp.int32, (N_GROUPS, N_GROUPS), 1)).astype(F32)
        d_row = sp[ROW_BGLU_D + 1:ROW_BGLU_D + 2, :]
        me = 4 * lax.axis_index("x") + 2 * lax.axis_index("y") + lax.axis_index("c")
        for k in range(N_DEV):
            @pl.when(me == k)
            def _(k=k):
                own_conv[...] = sp[ROW_CONV:ROW_CONV + SUBLANES, k * CONV_COLS_PER_DEV:(k + 1) * CONV_COLS_PER_DEV]
        grads = {
            "norm_gain": wide(ROW_NORM_GAIN),
            "final_norm_gain": wide(ROW_FINAL_GAIN),
            "b_glu": sp[ROW_BGLU_D:ROW_BGLU_D + 1, :],
            "ssm_a_re": sp[s5, LANE_A_RE:LANE_A_RE + STATE],
            "ssm_a_im": sp[s5, LANE_A_IM:LANE_A_IM + STATE],
            "ssm_log_dt": jnp.sum(sp[s5, LANE_LOG_DT:LANE_LOG_DT + 1] * eye, axis=0, keepdims=True),
            "ssm_d": jnp.concatenate([d_row[:, g * GROUP:(g + 1) * GROUP] for g in range(N_GROUPS)], axis=0),
            "conv_w": own_conv[0:3, :],
            "ssm_c_re": sgc[:, 0:STATE],
            "ssm_c_im": sgc[:, STATE:2 * STATE],
            "ssm_b_re": sgb[:, 0:STATE],
            "ssm_b_im": sgb[:, STATE:2 * STATE],
        }
        for i, name in enumerate(_SMALL_LEAVES):
            g = grads[name]
            w_ref, m_ref, v_ref = w_refs[3 * i:3 * i + 3]
            o_g, o_d, o_m, o_v = o_refs[4 * i:4 * i + 4]
            o_g[...] = g
            o_d[...], o_m[...], o_v[...] = _adamw(g, w_ref[...], m_ref[...], v_ref[...])

    flat_w = [a for name in _SMALL_LEAVES for a in wmv[name]]
    leaf_shapes = [_out(wmv[name][0].shape, F32) for name in _SMALL_LEAVES for _ in range(4)]
    sh_shapes = [_out(entry[1].shape, F32) for entry in sharded for _ in range(4)]
    operands = (r_pack, r_gc, r_gb, *flat_w, *(a for entry in sharded for a in entry))
    out_shape = (_out((SUBLANES, LANES), F32), *leaf_shapes, *sh_shapes)
    outs = _pcall(
        body, name="reduce_adam_small", grid=(1,), out_shape=out_shape,
        in_specs=_whole_specs(operands), out_specs=tuple(_whole_specs(out_shape)),
        scratch_shapes=[pltpu.VMEM((SUBLANES, CONV_COLS_PER_DEV), F32)],
        compiler_params=_params(1),
    )(*operands)
    leaves = {name: outs[1 + 4 * i:5 + 4 * i] for i, name in enumerate(_SMALL_LEAVES)}
    first = 1 + 4 * n_leaf
    return outs[0], leaves, [outs[first + 4 * i:first + 4 * i + 4] for i in range(n_sh)]


def kernel(x, norm_gain, w_in, ssm_a_re, ssm_a_im, ssm_log_dt, ssm_b_re, ssm_b_im, ssm_c_re, ssm_c_im, ssm_d, w_glu, b_glu, conv_w, w_out, final_norm_gain, loss_target, m_norm_gain, m_w_in, m_ssm_a_re, m_ssm_a_im, m_ssm_log_dt, m_ssm_b_re, m_ssm_b_im, m_ssm_c_re, m_ssm_c_im, m_ssm_d, m_w_glu, m_b_glu, m_conv_w, m_w_out, m_final_norm_gain, v_norm_gain, v_w_in, v_ssm_a_re, v_ssm_a_im, v_ssm_log_dt, v_ssm_b_re, v_ssm_b_im, v_ssm_c_re, v_ssm_c_im, v_ssm_d, v_w_glu, v_b_glu, v_conv_w, v_w_out, v_final_norm_gain):
    n_seq, seq, _ = x.shape
    n = n_seq * seq

    gh_p = lambda b4: jnp.transpose(b4, (0, 1, 3, 2)).reshape(N_GROUPS * GROUP, STATE)
    c2 = lambda a: a.reshape(N_GROUPS * GROUP, STATE)
    b_re2, b_im2 = gh_p(ssm_b_re), gh_p(ssm_b_im)
    d_row = ssm_d[0].reshape(1, SSM_W)

    x2 = x.reshape(n, D_MODEL)
    tgt2 = loss_target.reshape(n, D_MODEL)
    mx, my, mc = lax.axis_index("x"), lax.axis_index("y"), lax.axis_index("c")
    chip_ids = [2 * cx + cy for cx, cy in ((mx, my), (1 - mx, my), (mx, 1 - my), (1 - mx, 1 - my))]
    arrival = chip_ids
    xn, proj, w_in_f, s5 = _in_proj(
        jnp.stack(arrival).astype(jnp.int32), x2, norm_gain, w_in[0].astype(BF16),
        (ssm_a_re[0], ssm_a_im[0], ssm_log_dt, b_re2, b_im2, c2(ssm_c_re), c2(ssm_c_im)))
    a_re_x, a_im_x, log_dt_x, ab_re, ab_im, bb_re_m, bb_im_m, c_re_m, c_imn_m = s5
    u3 = proj.reshape(n_seq, seq, IN_COLS)
    conv_p = jnp.pad(conv_w[0], ((0, SUBLANES - 3), (0, LANES - CONV_COLS_PER_DEV)))
    s_re, s_im, y3, w_out_f, w_glu_f, conv_all = _ssm_fwd(
        u3, bb_re_m, bb_im_m, c_re_m, c_imn_m, d_row, ab_re, ab_im,
        w_out[0].astype(BF16), w_glu[0].astype(BF16), conv_p, n_seq, seq)
    conv8 = jnp.transpose(conv_all[:, :, :CONV_COLS_PER_DEV], (1, 0, 2)).reshape(SUBLANES, CONV_W)
    (dh2, dy, dzs, dbc, dzc, dyc, dw_out, dw_glu, loss_t, dgf, dbg, dcw) = _mix(
        x2, tgt2, y3.reshape(n, SSM_W), proj, final_norm_gain.reshape(1, D_MODEL), b_glu, conv8,
        w_glu_f, w_out_f, seq)

    du3, dc_re_d, dc_im_d, dbb_re_d, dbb_im_d, dab_re, dab_im, dd, r_out, r_glu = _ssm_bwd(
        dy.reshape(n_seq, seq, SSM_W), u3, s_re, s_im, bb_re_m, bb_im_m, c_re_m, c_imn_m, d_row, ab_re, ab_im,
        dw_out.reshape(N_DEV, OUT_ROWS_PER_DEV, D_MODEL), dw_glu.reshape(N_DEV, GLU_ROWS_PER_DEV, SSM_W), n_seq, seq)
    du = du3.reshape(n, SSM_W)
    grad_x2, dproj, dg8 = _in_bwd(x2, dh2, du, dzs, dyc, proj, dbc, dzc, norm_gain, conv8, w_in_f, seq)
    pack, gc, gb = _ssm_disc_bwd_pack(
        a_re_x, a_im_x, log_dt_x, b_re2, b_im2, dab_re.reshape(N_GROUPS, STATE), dab_im.reshape(N_GROUPS, STATE),
        dbb_re_d, dbb_im_d, loss_t, dg8, dgf, dbg, dd, dcw, dc_re_d, dc_im_d)

    order = [chip_ids[3], chip_ids[2], chip_ids[1], chip_ids[0]]
    own_in, rchip_in, r_pack, r_gc, r_gb = _dw_in_exchange(
        jnp.stack(order).astype(jnp.int32), xn, dproj, [pack, gc, gb])

    flat2 = lambda a: a.reshape(a.shape[-2:]) if a.ndim > 2 else a.reshape(1, -1)
    c2 = lambda a: a.reshape(N_GROUPS * GROUP, STATE)
    wmv = dict(norm_gain=(norm_gain, m_norm_gain, v_norm_gain),
               final_norm_gain=tuple(flat2(a) for a in (final_norm_gain, m_final_norm_gain, v_final_norm_gain)),
               b_glu=(b_glu, m_b_glu, v_b_glu),
               ssm_a_re=tuple(flat2(a) for a in (ssm_a_re, m_ssm_a_re, v_ssm_a_re)),
               ssm_a_im=tuple(flat2(a) for a in (ssm_a_im, m_ssm_a_im, v_ssm_a_im)),
               ssm_log_dt=(ssm_log_dt, m_ssm_log_dt, v_ssm_log_dt),
               ssm_d=tuple(flat2(a) for a in (ssm_d, m_ssm_d, v_ssm_d)),
               conv_w=tuple(flat2(a) for a in (conv_w, m_conv_w, v_conv_w)),
               ssm_c_re=tuple(c2(a) for a in (ssm_c_re, m_ssm_c_re, v_ssm_c_re)),
               ssm_c_im=tuple(c2(a) for a in (ssm_c_im, m_ssm_c_im, v_ssm_c_im)),
               ssm_b_re=(b_re2, gh_p(m_ssm_b_re), gh_p(v_ssm_b_re)),
               ssm_b_im=(b_im2, gh_p(m_ssm_b_im), gh_p(v_ssm_b_im)))

    res_in = _reduce_adam_w_in(own_in, rchip_in, w_in[0], m_w_in[0], v_w_in[0])
    loss8, small, (res_out, res_glu) = _reduce_adam_small(
        r_pack, r_gc, r_gb, wmv,
        [(r_out, w_out[0], m_w_out[0], v_w_out[0]), (r_glu, w_glu[0], m_w_glu[0], v_w_glu[0])])
    loss = loss8[0, 0]

    shapes = dict(norm_gain=(1, D_MODEL), ssm_a_re=(1, N_GROUPS, STATE), ssm_a_im=(1, N_GROUPS, STATE),
                  ssm_log_dt=(1, N_GROUPS), ssm_c_re=(1, N_GROUPS, GROUP, STATE), ssm_c_im=(1, N_GROUPS, GROUP, STATE),
                  ssm_d=(1, N_GROUPS, GROUP), b_glu=(1, SSM_W), final_norm_gain=(D_MODEL,),
                  conv_w=(1, 3, CONV_COLS_PER_DEV))
    big = dict(w_in=res_in, w_glu=res_glu, w_out=res_out)

    def leaf(kind, name):
        if name in big:
            return big[name][kind][None]
        if name in ("ssm_b_re", "ssm_b_im"):
            return jnp.transpose(small[name][kind].reshape(1, N_GROUPS, GROUP, STATE), (0, 1, 3, 2))
        return small[name][kind].reshape(shapes[name])

    order = ["norm_gain", "w_in", "ssm_a_re", "ssm_a_im", "ssm_log_dt", "ssm_b_re", "ssm_b_im", "ssm_c_re",
             "ssm_c_im", "ssm_d", "w_glu", "b_glu", "conv_w", "w_out", "final_norm_gain"]
    outs = [loss, grad_x2.reshape(x.shape)]
    for kind in range(4):
        outs += [leaf(kind, name) for name in order]
    return tuple(outs)
```

```python
import functools
import math

import jax
import jax.numpy as jnp
from jax import lax
from jax.experimental import pallas as pl
from jax.experimental.pallas import tpu as pltpu

F32 = jnp.float32
BF16 = jnp.bfloat16

N_DEV = 8
D_MODEL = 1024
SSM_W = 512
CONV_W = 512
N_GROUPS = 32
GROUP = 16
STATE = 64
IN_COLS = 3072
SEG_U, SEG_ZS, SEG_H, SEG_BC, SEG_CC, SEG_ZC = range(6)
COLS_PER_DEV = IN_COLS // N_DEV
N_CHIP = N_DEV // 2
COLS_PER_CHIP = 2 * COLS_PER_DEV
OUT_ROWS_PER_DEV = D_MODEL // N_DEV
GLU_ROWS_PER_DEV = SSM_W // N_DEV
CONV_COLS_PER_DEV = CONV_W // N_DEV
EPS = 1e-6

N_JBLK = 4
JB_CH = SSM_W // N_JBLK
JB_ST = N_GROUPS * STATE // N_JBLK

ADAM_LR = 0.001
ADAM_B1 = 0.9
ADAM_B2 = 0.999
ADAM_EPS = 1e-08
ADAM_WD = 0.01
ADAM_STEP = 10

SUBLANES = 8
LANES = 128
VMEM_LIMIT = 48 * 1024 * 1024
TOK_TILE = 256
IN_TILE = 1024
SCAN_TILE = 1024

MESH = pl.DeviceIdType.MESH
HBM_SPEC = pl.BlockSpec(memory_space=pltpu.HBM)


def _build(body, **kw):
    return pl.pallas_call(body, **kw)


def _pcall(body, **kw):
    def call(*operands):
        pinned = [a if jnp.issubdtype(a.dtype, jnp.integer) else pltpu.with_memory_space_constraint(a, pltpu.HBM)
                  for a in operands]
        return _build(body, **kw)(*pinned)
    return call


def _whole_specs(arrays):
    return [pl.BlockSpec(a.shape, functools.partial(lambda nd, i: (0,) * nd, len(a.shape))) for a in arrays]


def _out(shape, dtype):
    return pltpu.HBM(tuple(shape), dtype)


def _params(n_grid):
    return pltpu.CompilerParams(dimension_semantics=("arbitrary",) * n_grid,
                                vmem_limit_bytes=VMEM_LIMIT)


def _dot(a, b):
    return jnp.dot(a, b, preferred_element_type=F32)


def _dot_nt(a, b):
    return lax.dot_general(a, b, (((1,), (1,)), ((), ())), preferred_element_type=F32)


def _dot_tn(a, b):
    return lax.dot_general(a, b, (((0,), (0,)), ((), ())), preferred_element_type=F32)


def _sigmoid(z):
    return 1.0 / (1.0 + jnp.exp(-z))


_GELU_C = math.sqrt(2.0 / math.pi)


def _gelu_and_grad(y):
    inner = _GELU_C * (y + 0.044715 * (y * y * y))
    t = jnp.tanh(inner)
    g = 0.5 * y * (1.0 + t)
    dg = 0.5 * (1.0 + t) + 0.5 * y * (1.0 - t * t) * (_GELU_C * (1.0 + 3.0 * 0.044715 * (y * y)))
    return g, dg


def _silu_and_grad(z):
    s = _sigmoid(z)
    return z * s, s * (1.0 + z * (1.0 - s))


def _shift_down(v, halo, k):
    rolled = pltpu.roll(v, k, 0)
    row = lax.broadcasted_iota(jnp.int32, v.shape, 0)
    for r in range(k):
        rolled = jnp.where(row == r, halo[SUBLANES - k + r:SUBLANES - k + r + 1, :], rolled)
    return rolled


def _shift_up(v, halo, k):
    n = v.shape[0]
    rolled = pltpu.roll(v, n - k, 0)
    row = lax.broadcasted_iota(jnp.int32, v.shape, 0)
    for r in range(k):
        rolled = jnp.where(row == n - k + r, halo[r:r + 1, :], rolled)
    return rolled


def _mesh_pos():
    return lax.axis_index("x"), lax.axis_index("y"), lax.axis_index("c")


def _direct_copies(srcs_for, out_refs, send_sems, recv_sems, loc_sems):
    x, y, c = _mesh_pos()
    me_id = 4 * x + 2 * y + c
    n_arr = len(out_refs)
    dsts = [r.at[me_id] for r in out_refs]
    own = srcs_for(me_id)
    mine = [pltpu.make_async_copy(own[a], dsts[a], loc_sems.at[a]) for a in range(n_arr)]
    sends = []
    for k in range(1, N_DEV):
        px, py, pc = x ^ ((k >> 2) & 1), y ^ ((k >> 1) & 1), c ^ (k & 1)
        src = srcs_for(4 * px + 2 * py + pc)
        for a in range(n_arr):
            sends.append(pltpu.make_async_remote_copy(
                src_ref=src[a], dst_ref=dsts[a],
                send_sem=send_sems.at[(k - 1) * n_arr + a], recv_sem=recv_sems.at[(k - 1) * n_arr + a],
                device_id=(px, py, pc), device_id_type=MESH))
    return mine, sends


class _TwoLevelGather:
    def __init__(self, srcs, slots, send_sems, recv_sems, loc_sems):
        self.srcs, self.slots, self.n_arr = srcs, slots, len(srcs)
        self.send_sems, self.recv_sems, self.loc_sems = send_sems, recv_sems, loc_sems
        x, y, c = _mesh_pos()
        self.c = c
        self.me, self.sib = (x, y, c), (x, y, 1 - c)
        self.chips = [(1 - x, y), (x, 1 - y), (1 - x, 1 - y)]

    def _copies(self, k, block, to, from_src=False):
        dev = 4 * block[0] + 2 * block[1] + block[2]
        return [pltpu.make_async_remote_copy(
            src_ref=self.srcs[a] if from_src else self.slots[a](dev), dst_ref=self.slots[a](dev),
            send_sem=self.send_sems.at[k * self.n_arr + a], recv_sem=self.recv_sems.at[k * self.n_arr + a],
            device_id=to, device_id_type=MESH) for a in range(self.n_arr)]

    def _local(self):
        dev = 4 * self.me[0] + 2 * self.me[1] + self.me[2]
        return [pltpu.make_async_copy(self.srcs[a], self.slots[a](dev), self.loc_sems.at[a])
                for a in range(self.n_arr)]

    def start(self, chips=(0, 1, 2)):
        for cp in self._local() + self._copies(0, self.me, self.sib, True):
            cp.start()
        self.start_to(chips)

    def start_to(self, chips):
        for j in chips:
            for cp in self._copies(1 + j, self.me, (*self.chips[j], self.c), True):
                cp.start()

    def wait_own(self):
        for cp in self._local():
            cp.wait()

    def wait_sibling(self):
        for cp in self._copies(0, self.sib, self.me):
            cp.wait_recv()

    def wait_and_pass_on(self, j):
        chip = self.chips[j]
        for cp in self._copies(1 + j, (*chip, self.c), self.me):
            cp.wait_recv()
        for cp in self._copies(4 + j, (*chip, self.c), self.sib):
            cp.start()

    def relay(self, k, block, to):
        for cp in self._copies(k, block, to):
            cp.start()

    def wait_passed_on(self, j):
        for cp in self._copies(4 + j, (*self.chips[j], 1 - self.c), self.me):
            cp.wait_recv()

    def wait_sends(self):
        for cp in self._copies(0, self.me, self.sib, True):
            cp.wait_send()
        for j, chip in enumerate(self.chips):
            for cp in self._copies(1 + j, self.me, (*chip, self.c), True) + self._copies(4 + j, (*chip, self.c), self.sib):
                cp.wait_send()

    def forward(self):
        for j in range(3):
            self.wait_and_pass_on(j)

    def finish(self):
        self.wait_sibling()
        for j in range(3):
            self.wait_passed_on(j)
        self.wait_sends()
        self.wait_own()


def _disc(a_re, a_im, log_dt, b_re, b_im):
    dt = jnp.exp(log_dt)
    mag = jnp.exp(a_re * dt)
    ab_re = mag * jnp.cos(a_im * dt)
    ab_im = mag * jnp.sin(a_im * dt)
    den = a_re * a_re + a_im * a_im
    p_re = ab_re - 1.0
    p_im = ab_im
    q_re = (p_re * a_re + p_im * a_im) / den
    q_im = (p_im * a_re - p_re * a_im) / den
    bb_re = q_re * b_re - q_im * b_im
    bb_im = q_re * b_im + q_im * b_re
    return ab_re, ab_im, bb_re, bb_im


def _split3(v):
    hi = v.astype(BF16)
    r1 = v - hi.astype(F32)
    mid = r1.astype(BF16)
    lo = (r1 - mid.astype(F32)).astype(BF16)
    return hi, mid, lo


def _select_dot(sel, v):
    return sum(_dot(sel, t) for t in _split3(v))


PACK_ROWS = 72
PACK_W = 512
ROW_FINAL_GAIN, ROW_NORM_GAIN, ROW_BGLU_D, ROW_CONV, ROW_LOSS, ROW_S5 = 0, 8, 16, 24, 32, 40
LANE_A_RE, LANE_A_IM, LANE_LOG_DT = 0, 128, 256


def _ssm_disc_bwd_pack(a_re_x, a_im_x, log_dt_x, b_re, b_im, g_ab_re, g_ab_im, dbb_re_d, dbb_im_d,
                       loss_t, dg8, dgf, dbg, dd, dcw, dc_re_d, dc_im_d):
    rows_gh = N_GROUPS * GROUP

    def body(are, aim, ldt, bre, bim, gabre, gabim, dbbre_ref, dbbim_ref,
             loss_ref, dg8_ref, dgf_ref, dbg_ref, dd_ref, dcw_ref, dcre_ref, dcim_ref,
             p_ref, gc_ref, gb_ref, gbb_re, gbb_im):
        r_g = lax.broadcasted_iota(jnp.int32, (N_GROUPS, rows_gh), 0)
        c_gh = lax.broadcasted_iota(jnp.int32, (N_GROUPS, rows_gh), 1)
        group_sum = (c_gh // GROUP == r_g).astype(BF16)
        r_gh = lax.broadcasted_iota(jnp.int32, (rows_gh, N_GROUPS), 0)
        c_g = lax.broadcasted_iota(jnp.int32, (rows_gh, N_GROUPS), 1)
        first_row = (r_gh == c_g * GROUP).astype(BF16)

        def diag_block(ref, j, gi):
            return ref[j, gi * GROUP:(gi + 1) * GROUP, gi * STATE:(gi + 1) * STATE]

        for j in range(N_JBLK):
            for gi in range(SUBLANES):
                r0 = (j * SUBLANES + gi) * GROUP
                gbb_re[r0:r0 + GROUP, :] = diag_block(dbbre_ref, j, gi)
                gbb_im[r0:r0 + GROUP, :] = diag_block(dbbim_ref, j, gi)
                both = jnp.concatenate([diag_block(dcre_ref, j, gi), -diag_block(dcim_ref, j, gi)], axis=1)
                gc_ref[r0:r0 + GROUP, :] = both.astype(BF16)

        _, vjp = jax.vjp(_disc, are[...], aim[...], ldt[...], bre[...], bim[...])
        d_are, d_aim, d_ldt, d_bre, d_bim = vjp((_select_dot(first_row, gabre[...]), _select_dot(first_row, gabim[...]),
                                                 gbb_re[...], gbb_im[...]))
        gb_ref[...] = jnp.concatenate([d_bre, d_bim], axis=1).astype(BF16)

        p_ref[...] = jnp.zeros_like(p_ref)
        half = D_MODEL // 2
        for r, src in ((ROW_FINAL_GAIN, dgf_ref), (ROW_NORM_GAIN, dg8_ref)):
            p_ref[r:r + 1, :] = src[0:1, 0:half]
            p_ref[r + 1:r + 2, :] = src[0:1, half:D_MODEL]
        p_ref[ROW_BGLU_D:ROW_BGLU_D + 1, :] = dbg_ref[...]
        p_ref[ROW_BGLU_D + 1:ROW_BGLU_D + 2, :] = dd_ref[...]
        p_ref[ROW_CONV:ROW_CONV + SUBLANES, :] = dcw_ref[...]
        p_ref[ROW_LOSS:ROW_LOSS + SUBLANES, 0:LANES] = loss_ref[...]
        s5 = slice(ROW_S5, ROW_S5 + N_GROUPS)
        p_ref[s5, LANE_A_RE:LANE_A_RE + STATE] = _select_dot(group_sum, d_are)
        p_ref[s5, LANE_A_IM:LANE_A_IM + STATE] = _select_dot(group_sum, d_aim)
        p_ref[s5, LANE_LOG_DT:LANE_LOG_DT + LANES] = _select_dot(group_sum, jnp.broadcast_to(d_ldt, (rows_gh, LANES)))

    operands = (a_re_x, a_im_x, log_dt_x, b_re, b_im, g_ab_re, g_ab_im, dbb_re_d, dbb_im_d,
                loss_t, dg8, dgf, dbg, dd, dcw, dc_re_d, dc_im_d)
    out_shape = (_out((PACK_ROWS, PACK_W), F32),
                 _out((rows_gh, 2 * STATE), BF16),
                 _out((rows_gh, 2 * STATE), BF16))
    return _pcall(body, name="ssm_disc_bwd_pack", grid=(1,), out_shape=out_shape,
                  in_specs=_whole_specs(operands), out_specs=tuple(_whole_specs(out_shape)),
                  scratch_shapes=[pltpu.VMEM((rows_gh, STATE), F32), pltpu.VMEM((rows_gh, STATE), F32)],
                  compiler_params=_params(1))(*operands)


def _s5_prepare(are, aim, ldt, bre, bim, cre, cim,
                o_ax_re, o_ax_im, o_ldt_x, o_ab_re, o_ab_im, o_bb_re, o_bb_im, o_c_re, o_c_imn):
    rows_gh = N_GROUPS * GROUP
    rep = (lax.broadcasted_iota(jnp.int32, (rows_gh, N_GROUPS), 0) // GROUP
           == lax.broadcasted_iota(jnp.int32, (rows_gh, N_GROUPS), 1)).astype(BF16)
    eye = (lax.broadcasted_iota(jnp.int32, (N_GROUPS, N_GROUPS), 0)
           == lax.broadcasted_iota(jnp.int32, (N_GROUPS, N_GROUPS), 1)).astype(F32)
    ldt_col = jnp.sum(eye * ldt[...], axis=1, keepdims=True)
    a_re_x = _select_dot(rep, are[...])
    a_im_x = _select_dot(rep, aim[...])
    ldt_x = _select_dot(rep, jnp.broadcast_to(ldt_col, (N_GROUPS, LANES)))[:, 0:1]
    o_ax_re[...] = a_re_x
    o_ax_im[...] = a_im_x
    o_ldt_x[...] = ldt_x
    ab_re, ab_im, bb_re, bb_im = _disc(a_re_x, a_im_x, ldt_x, bre[...], bim[...])
    for j in range(N_JBLK):
        first = [(j * SUBLANES + gi) * GROUP for gi in range(SUBLANES)]
        o_ab_re[j] = jnp.concatenate([ab_re[r:r + 1, :] for r in first], axis=1)
        o_ab_im[j] = jnp.concatenate([ab_im[r:r + 1, :] for r in first], axis=1)
    for o, v in ((o_bb_re, bb_re), (o_bb_im, bb_im), (o_c_re, cre[...]), (o_c_imn, -cim[...])):
        for j in range(N_JBLK):
            for gi in range(SUBLANES):
                r0 = (j * SUBLANES + gi) * GROUP
                parts = [v[r0:r0 + GROUP, :] if k == gi else jnp.zeros((GROUP, STATE), F32) for k in range(SUBLANES)]
                o[j, gi * GROUP:(gi + 1) * GROUP, :] = jnp.concatenate(parts, axis=1).astype(BF16)


def _in_proj(order, x2, g1, w_in_b, s5):
    n = x2.shape[0]
    tm = min(IN_TILE, n)
    n_tiles = n // tm
    n_s5_in = len(s5)
    n_s5_out = 9

    def body(order_ref, x_ref, g_ref, w_ref, *refs):
        s5_in = refs[:n_s5_in]
        xn_ref, proj_ref, wall_ref = refs[n_s5_in:n_s5_in + 3]
        s5_out = refs[n_s5_in + 3:n_s5_in + 3 + n_s5_out]
        xn_scr, wbuf, send_sems, recv_sems, loc_sems, out_sems = refs[n_s5_in + 3 + n_s5_out:]
        k = pl.program_id(0)
        i = pl.program_id(1)

        def slot(dev):
            return wbuf.at[dev // 2, :, pl.ds(pl.multiple_of((dev % 2) * COLS_PER_DEV, LANES), COLS_PER_DEV)]

        gather = _TwoLevelGather([w_ref], [slot], send_sems, recv_sems, loc_sems)

        @pl.when((k == 0) & (i == 0))
        def _():
            gather.start(chips=(0, 1))

        def own_chip():
            gather.wait_own()
            gather.wait_sibling()

        def x_chip():
            x, y, c = _mesh_pos()
            gather.wait_and_pass_on(0)
            gather.wait_and_pass_on(1)
            gather.relay(1 + 2, (x ^ c, y ^ (1 - c), c), (x ^ (1 - c), y ^ c, c))
            gather.wait_passed_on(0)

        def diag_chip():
            gather.wait_and_pass_on(2)
            gather.wait_passed_on(2)

        arrivals = [own_chip, x_chip, functools.partial(gather.wait_passed_on, 1), diag_chip]
        for kk, arrived in enumerate(arrivals):
            @pl.when((k == kk) & (i == 0))
            def _(arrived=arrived):
                arrived()

        rows = pl.ds(pl.multiple_of(i * tm, tm), tm)

        @pl.when(k == 0)
        def _():
            x = x_ref[...]
            r = lax.rsqrt(jnp.mean(x * x, axis=-1, keepdims=True) + EPS)
            xn = ((x * r) * g_ref[...]).astype(BF16)
            xn_scr[rows, :] = xn
            xn_ref[...] = xn

        proj_ref[...] = _dot(xn_scr[rows, :], wbuf[order_ref[k]])

        @pl.when((k == 0) & (i == n_tiles - 1))
        def _():
            _s5_prepare(*s5_in, *s5_out)

        @pl.when((k == N_CHIP - 1) & (i == n_tiles - 1))
        def _():
            gather.wait_sends()
            outs = [pltpu.make_async_copy(wbuf.at[q], wall_ref.at[:, q * COLS_PER_CHIP:(q + 1) * COLS_PER_CHIP],
                                          out_sems.at[q]) for q in range(N_CHIP)]
            for cp in outs:
                cp.start()
            for cp in outs:
                cp.wait()

    tile_once = lambda k, i, order: (jnp.where(k == 0, i, n_tiles - 1), 0)
    whole = lambda shape: pl.BlockSpec(shape, lambda k, i, order: (0,) * len(shape))
    rows_gh = N_GROUPS * GROUP
    s5_out_shapes = ([(rows_gh, STATE), F32], [(rows_gh, STATE), F32], [(rows_gh, 1), F32],
                     [(N_JBLK, 1, JB_ST), F32], [(N_JBLK, 1, JB_ST), F32]) + ([(N_JBLK, JB_CH, JB_ST), BF16],) * 4
    grid_spec = pltpu.PrefetchScalarGridSpec(
        num_scalar_prefetch=1, grid=(N_CHIP, n_tiles),
        in_specs=[pl.BlockSpec((tm, D_MODEL), tile_once),
                  whole((1, D_MODEL)),
                  HBM_SPEC,
                  *(whole(a.shape) for a in s5)],
        out_specs=(pl.BlockSpec((tm, D_MODEL), tile_once),
                   pl.BlockSpec((tm, COLS_PER_CHIP), lambda k, i, order: (i, order[k])),
                   HBM_SPEC,
                   *(whole(shape) for shape, _ in s5_out_shapes)),
        scratch_shapes=[pltpu.VMEM((n, D_MODEL), BF16), pltpu.VMEM((N_CHIP, D_MODEL, COLS_PER_CHIP), BF16),
                        pltpu.SemaphoreType.DMA((7,)), pltpu.SemaphoreType.DMA((7,)), pltpu.SemaphoreType.DMA((1,)),
                        pltpu.SemaphoreType.DMA((N_CHIP,))])
    outs = _pcall(
        body, name="in_proj", grid_spec=grid_spec,
        out_shape=(_out((n, D_MODEL), BF16), _out((n, IN_COLS), F32),
                   _out((D_MODEL, IN_COLS), BF16),
                   *(_out(shape, dt) for shape, dt in s5_out_shapes)),
        compiler_params=_params(2),
    )(order, x2, g1, w_in_b, *s5)
    return outs[0], outs[1], outs[2], outs[3:]


def _cmul(p, q):
    return p[0] * q[0] - p[1] * q[1], p[0] * q[1] + p[1] * q[0]


def _scan_tables(ar, ai, width, reverse):
    pows = [(ar, ai)]
    for _ in range(SUBLANES - 1):
        pows.append(_cmul(pows[-1], (ar, ai)))
    row = lax.broadcasted_iota(jnp.int32, (SUBLANES, width), 0)

    def bc(v):
        return jnp.broadcast_to(v, (SUBLANES, width))

    levels = []
    for k in (1, 2, 4):
        keep = (row <= SUBLANES - 1 - k) if reverse else (row >= k)
        levels.append((jnp.where(keep, bc(pows[k - 1][0]), 0.0), jnp.where(keep, bc(pows[k - 1][1]), 0.0)))
    cre = jnp.zeros((SUBLANES, width), F32)
    cim = jnp.zeros((SUBLANES, width), F32)
    for r in range(SUBLANES):
        e = (SUBLANES - r) if reverse else (r + 1)
        cre = jnp.where(row == r, bc(pows[e - 1][0]), cre)
        cim = jnp.where(row == r, bc(pows[e - 1][1]), cim)
    return levels, (cre, cim)


def _load_chunked(src_ref, b, dst_ref, n_rows):
    n_blk = n_rows // SUBLANES
    for i in range(n_blk):
        dst_ref[b, i * SUBLANES:(i + 1) * SUBLANES, :] = src_ref[b, pl.ds(i, SUBLANES, stride=n_blk), :]


def _store_chunked(val, dst_ref, b, n_rows):
    n_blk = n_rows // SUBLANES
    for i in range(n_blk):
        dst_ref[b, pl.ds(i, SUBLANES, stride=n_blk), :] = val[i * SUBLANES:(i + 1) * SUBLANES, :]


def _chunk_scan(re_ref, im_ref, bs, car_ref, ar, ai, n_rows, reverse, on_block=None):
    width = re_ref.shape[2]
    n_blk = n_rows // SUBLANES
    shape = (SUBLANES, width)
    abr = jnp.broadcast_to(ar, shape)
    abi = jnp.broadcast_to(ai, shape)
    order = list(range(n_blk - 1, -1, -1)) if reverse else list(range(n_blk))

    def blk(ref, b, i):
        return ref[b, i * SUBLANES:(i + 1) * SUBLANES, :]

    def step(state, b, i):
        sr, si = state
        return abr * sr - abi * si + blk(re_ref, b, i), abr * si + abi * sr + blk(im_ref, b, i)

    finals = {b: (blk(re_ref, b, order[0]), blk(im_ref, b, order[0])) for b in bs}
    for i in order[1:]:
        for b in bs:
            finals[b] = step(finals[b], b, i)

    mr, mi = ar, ai
    for _ in range(n_blk.bit_length() - 1):
        mr, mi = _cmul((mr, mi), (mr, mi))
    levels, _ = _scan_tables(mr, mi, width, reverse)
    mbr = jnp.broadcast_to(mr, shape)
    mbi = jnp.broadcast_to(mi, shape)
    row = lax.broadcasted_iota(jnp.int32, shape, 0)
    edge_in = SUBLANES - 1 if reverse else 0
    edge_out = 0 if reverse else SUBLANES - 1
    sh1 = SUBLANES - 1 if reverse else 1
    states = {}
    for b in bs:
        fr, fi = finals[b]
        gr = jnp.where(row == edge_in, jnp.broadcast_to(car_ref[b, 0:1, :], shape), pltpu.roll(fr, sh1, 0))
        gi = jnp.where(row == edge_in, jnp.broadcast_to(car_ref[b, 1:2, :], shape), pltpu.roll(fi, sh1, 0))
        for (lr, li), k in zip(levels, (1, 2, 4)):
            sh = (SUBLANES - k) if reverse else k
            sr = pltpu.roll(gr, sh, 0)
            si = pltpu.roll(gi, sh, 0)
            gr, gi = gr + (lr * sr - li * si), gi + (lr * si + li * sr)
        car_ref[b, 0:1, :] = (fr + (mbr * gr - mbi * gi))[edge_out:edge_out + 1, :]
        car_ref[b, 1:2, :] = (fi + (mbr * gi + mbi * gr))[edge_out:edge_out + 1, :]
        states[b] = (gr, gi)

    for i in order:
        for b in bs:
            states[b] = step(states[b], b, i)
            re_ref[b, i * SUBLANES:(i + 1) * SUBLANES, :] = states[b][0]
            im_ref[b, i * SUBLANES:(i + 1) * SUBLANES, :] = states[b][1]
            if on_block is not None:
                on_block(b, i, *states[b])


def _ssm_fwd(u, bb_re, bb_im, c_re_t, c_imn_t, d_row, ab_re, ab_im, w_out_b, w_glu_b, conv_p, n_seq, seq):
    tt = min(SCAN_TILE, seq)
    nt = seq // tt

    def body(u_ref, bbre, bbim, cre, cimn, d_ref, are, aim, wout_ref, wglu_ref, cw_ref,
             sre_ref, sim_ref, y_ref, oout_ref, oglu_ref, ocw_ref,
             up_ref, car_ref, send_sems, recv_sems, loc_sems):
        j = pl.program_id(0)
        t = pl.program_id(1)
        gather = _TwoLevelGather(
            [wout_ref, wglu_ref, cw_ref],
            [lambda dev: oout_ref.at[pl.ds(pl.multiple_of(dev * OUT_ROWS_PER_DEV, OUT_ROWS_PER_DEV), OUT_ROWS_PER_DEV), :],
             lambda dev: oglu_ref.at[pl.ds(pl.multiple_of(dev * GLU_ROWS_PER_DEV, GLU_ROWS_PER_DEV), GLU_ROWS_PER_DEV), :],
             lambda dev: ocw_ref.at[dev]],
            send_sems, recv_sems, loc_sems)

        @pl.when((j == 0) & (t == 0))
        def _():
            gather.start()

        @pl.when((j == N_JBLK // 2) & (t == 0))
        def _():
            gather.forward()

        @pl.when(t == 0)
        def _():
            car_ref[...] = jnp.zeros_like(car_ref)

        bs = list(range(n_seq))
        for b in bs:
            _load_chunked(u_ref, b, up_ref, tt)
        for b in bs:
            ub = up_ref[b].astype(BF16)
            sre_ref[b] = _dot(ub, bbre[0])
            sim_ref[b] = _dot(ub, bbim[0])
            _chunk_scan(sre_ref, sim_ref, [b], car_ref, are[0], aim[0], tt, reverse=False)
        for b in bs:
            yp = (_dot_nt(sre_ref[b].astype(BF16), cre[0]) + _dot_nt(sim_ref[b].astype(BF16), cimn[0])
                  + d_ref[...] * up_ref[b])
            _store_chunked(yp, y_ref, b, tt)

        @pl.when((j == N_JBLK - 1) & (t == nt - 1))
        def _():
            gather.finish()

    tok = lambda j, t: (0, t, j)
    blk3 = lambda j, t: (j, 0, 0)
    row = lambda j, t: (0, j)
    st = _out((n_seq, seq, N_JBLK * JB_ST), F32)
    n_arr = 3
    return _pcall(
        body, name="ssm_fwd", grid=(N_JBLK, nt),
        out_shape=(st, st, _out((n_seq, seq, SSM_W), F32),
                   _out((D_MODEL, D_MODEL), BF16), _out((SSM_W, SSM_W), BF16),
                   _out((N_DEV, SUBLANES, LANES), F32)),
        in_specs=[pl.BlockSpec((n_seq, tt, JB_CH), tok),
                  pl.BlockSpec((1, JB_CH, JB_ST), blk3), pl.BlockSpec((1, JB_CH, JB_ST), blk3),
                  pl.BlockSpec((1, JB_CH, JB_ST), blk3), pl.BlockSpec((1, JB_CH, JB_ST), blk3),
                  pl.BlockSpec((1, JB_CH), row), pl.BlockSpec((1, 1, JB_ST), blk3), pl.BlockSpec((1, 1, JB_ST), blk3),
                  HBM_SPEC, HBM_SPEC, HBM_SPEC],
        out_specs=(pl.BlockSpec((n_seq, tt, JB_ST), tok), pl.BlockSpec((n_seq, tt, JB_ST), tok),
                   pl.BlockSpec((n_seq, tt, JB_CH), tok), HBM_SPEC, HBM_SPEC, HBM_SPEC),
        scratch_shapes=[pltpu.VMEM((n_seq, tt, JB_CH), F32), pltpu.VMEM((n_seq, SUBLANES, JB_ST), F32),
                        pltpu.SemaphoreType.DMA((7 * n_arr,)), pltpu.SemaphoreType.DMA((7 * n_arr,)),
                        pltpu.SemaphoreType.DMA((n_arr,))],
        compiler_params=_params(2),
    )(u, bb_re, bb_im, c_re_t, c_imn_t, d_row, ab_re, ab_im, w_out_b, w_glu_b, conv_p)


def _ssm_bwd(dy, u, s_re, s_im, bb_re, bb_im, c_re_t, c_imn_t, d_row, ab_re, ab_im, g_out, g_glu, n_seq, seq):
    tt = min(SCAN_TILE, seq)
    nt = seq // tt
    rows8 = tt // SUBLANES

    def body(dy_ref, u_ref, sre_ref, sim_ref, pre_ref, pim_ref, bbre, bbim, cre, cimn, d_ref, are, aim,
             gout_ref, gglu_ref,
             du_ref, dcre_ref, dcim_ref, dbbre_ref, dbbim_ref, dare_ref, daim_ref, dd_ref, rout_ref, rglu_ref,
             lre_ref, lim_ref, dyp_ref, up_ref, car_ref, send_sems, recv_sems, loc_sems):
        j = pl.program_id(0)
        tr = pl.program_id(1)

        def exchange():
            return _direct_copies(lambda pid: [gout_ref.at[pid], gglu_ref.at[pid]], [rout_ref, rglu_ref],
                                  send_sems, recv_sems, loc_sems)

        @pl.when((j == 0) & (tr == 0))
        def _():
            mine, sends = exchange()
            for cp in mine + sends:
                cp.start()

        @pl.when(tr == 0)
        def _():
            car_ref[...] = jnp.zeros_like(car_ref)
            for r in (dcre_ref, dcim_ref, dbbre_ref, dbbim_ref, dare_ref, daim_ref, dd_ref):
                r[...] = jnp.zeros_like(r)

        first = tr == nt - 1
        row = lax.broadcasted_iota(jnp.int32, (SUBLANES, JB_ST), 0)
        n_blk = tt // SUBLANES
        bs = list(range(n_seq))
        for b in bs:
            _load_chunked(dy_ref, b, dyp_ref, tt)
            _load_chunked(u_ref, b, up_ref, tt)
        for b in bs:
            dyb = dyp_ref[b].astype(BF16)
            lre_ref[b] = _dot(dyb, cre[0])
            lim_ref[b] = _dot(dyb, cimn[0])
        acc = {b: [jnp.zeros((SUBLANES, JB_ST), F32), jnp.zeros((SUBLANES, JB_ST), F32)] for b in bs}

        def on_block(b, i, lr, li):
            if i > 0:
                spr = sre_ref[b, (i - 1) * SUBLANES:i * SUBLANES, :]
                spi = sim_ref[b, (i - 1) * SUBLANES:i * SUBLANES, :]
            else:
                hr = jnp.where(first, 0.0, pre_ref[b, SUBLANES - 1:SUBLANES, :])
                hi = jnp.where(first, 0.0, pim_ref[b, SUBLANES - 1:SUBLANES, :])
                last_r = sre_ref[b, (n_blk - 1) * SUBLANES:n_blk * SUBLANES, :]
                last_i = sim_ref[b, (n_blk - 1) * SUBLANES:n_blk * SUBLANES, :]
                spr = jnp.where(row == 0, jnp.broadcast_to(hr, row.shape), pltpu.roll(last_r, 1, 0))
                spi = jnp.where(row == 0, jnp.broadcast_to(hi, row.shape), pltpu.roll(last_i, 1, 0))
            acc[b][0] = acc[b][0] + (lr * spr + li * spi)
            acc[b][1] = acc[b][1] + (li * spr - lr * spi)

        _chunk_scan(lre_ref, lim_ref, bs, car_ref, are[0], -aim[0], tt, reverse=True, on_block=on_block)
        for b in bs:
            dare_ref[...] += jnp.sum(acc[b][0], axis=0, keepdims=True)
            daim_ref[...] += jnp.sum(acc[b][1], axis=0, keepdims=True)
            dyp = dyp_ref[b]
            up = up_ref[b]
            dyb = dyp.astype(BF16)
            ub = up.astype(BF16)
            lrb = lre_ref[b].astype(BF16)
            lib = lim_ref[b].astype(BF16)
            dup = d_ref[...] * dyp + _dot_nt(lrb, bbre[0]) + _dot_nt(lib, bbim[0])
            _store_chunked(dup, du_ref, b, tt)
            dbbre_ref[0] += _dot_tn(ub, lrb)
            dbbim_ref[0] += _dot_tn(ub, lib)
            dcre_ref[0] += _dot_tn(dyb, sre_ref[b].astype(BF16))
            dcim_ref[0] += _dot_tn(dyb, sim_ref[b].astype(BF16))
            dd_ref[...] += jnp.sum(dyp * up, axis=0, keepdims=True)

        @pl.when((j == N_JBLK - 1) & (tr == nt - 1))
        def _():
            mine, sends = exchange()
            for cp in sends + mine:
                cp.wait()

    tok = lambda j, t: (0, nt - 1 - t, j)
    halo = lambda j, t: (0, jnp.maximum((nt - 1 - t) * rows8 - 1, 0), j)
    blk3 = lambda j, t: (j, 0, 0)
    row1 = lambda j, t: (0, j)
    acc_shape = _out((N_JBLK, JB_CH, JB_ST), F32)
    return _pcall(
        body, name="ssm_bwd", grid=(N_JBLK, nt),
        out_shape=(_out((n_seq, seq, SSM_W), F32), acc_shape, acc_shape, acc_shape, acc_shape,
                   _out((1, N_JBLK * JB_ST), F32), _out((1, N_JBLK * JB_ST), F32),
                   _out((1, SSM_W), F32),
                   _out((N_DEV,) + g_out.shape[1:], F32),
                   _out((N_DEV,) + g_glu.shape[1:], F32)),
        in_specs=[pl.BlockSpec((n_seq, tt, JB_CH), tok), pl.BlockSpec((n_seq, tt, JB_CH), tok),
                  pl.BlockSpec((n_seq, tt, JB_ST), tok), pl.BlockSpec((n_seq, tt, JB_ST), tok),
                  pl.BlockSpec((n_seq, SUBLANES, JB_ST), halo), pl.BlockSpec((n_seq, SUBLANES, JB_ST), halo),
                  pl.BlockSpec((1, JB_CH, JB_ST), blk3), pl.BlockSpec((1, JB_CH, JB_ST), blk3),
                  pl.BlockSpec((1, JB_CH, JB_ST), blk3), pl.BlockSpec((1, JB_CH, JB_ST), blk3),
                  pl.BlockSpec((1, JB_CH), row1), pl.BlockSpec((1, 1, JB_ST), blk3), pl.BlockSpec((1, 1, JB_ST), blk3),
                  HBM_SPEC, HBM_SPEC],
        out_specs=(pl.BlockSpec((n_seq, tt, JB_CH), tok),
                   pl.BlockSpec((1, JB_CH, JB_ST), blk3), pl.BlockSpec((1, JB_CH, JB_ST), blk3),
                   pl.BlockSpec((1, JB_CH, JB_ST), blk3), pl.BlockSpec((1, JB_CH, JB_ST), blk3),
                   pl.BlockSpec((1, JB_ST), row1), pl.BlockSpec((1, JB_ST), row1), pl.BlockSpec((1, JB_CH), row1),
                   HBM_SPEC, HBM_SPEC),
        scratch_shapes=[pltpu.VMEM((n_seq, tt, JB_ST), F32), pltpu.VMEM((n_seq, tt, JB_ST), F32),
                        pltpu.VMEM((n_seq, tt, JB_CH), F32), pltpu.VMEM((n_seq, tt, JB_CH), F32),
                        pltpu.VMEM((n_seq, SUBLANES, JB_ST), F32),
                        pltpu.SemaphoreType.DMA((7 * 2,)), pltpu.SemaphoreType.DMA((7 * 2,)),
                        pltpu.SemaphoreType.DMA((2,))],
        compiler_params=_params(2),
    )(dy, u, s_re, s_im, s_re, s_im, bb_re, bb_im, c_re_t, c_imn_t, d_row, ab_re, ab_im, g_out, g_glu)


def _mix(x2, tgt2, y, proj, gf, b_glu, conv8, w_glu_f, w_out_f, seq):
    n = x2.shape[0]
    tm = TOK_TILE
    tiles_per_seq = seq // tm
    rows8 = tm // SUBLANES

    def body(x_ref, t_ref, y_ref, zs_ref, h_ref, bc_ref, cc_ref, zc_ref, hp_ref, ccp_ref,
             gf_ref, bg_ref, cw_ref, wg_ref, wo_ref,
             dh2_ref, dy_ref, dzs_ref, dbc_ref, dzc_ref, dyc_ref,
             dwo_ref, dwg_ref, loss_ref, dgf_ref, dbg_ref, dcw_ref):
        i = pl.program_id(0)

        @pl.when(i == 0)
        def _():
            for r in (dwo_ref, dwg_ref, loss_ref, dgf_ref, dbg_ref, dcw_ref):
                r[...] = jnp.zeros_like(r)

        yv = y_ref[...]
        y1, dgelu = _gelu_and_grad(yv)
        y1b = y1.astype(BF16)
        gate = _sigmoid(_dot(y1b, wg_ref[...]) + bg_ref[...])
        y2 = y1 * gate
        szs, dszs = _silu_and_grad(zs_ref[...])
        yssm = y2 * szs
        hv = h_ref[...]
        ccv = cc_ref[...]
        bcv = bc_ref[...]
        v = ccv * hv
        first = (i % tiles_per_seq) == 0
        vhalo = jnp.where(first, 0.0, ccp_ref[...] * hp_ref[...])
        v1 = _shift_down(v, vhalo, 1)
        v2 = _shift_down(v, vhalo, 2)
        w0 = cw_ref[0:1, :]
        w1 = cw_ref[1:2, :]
        w2 = cw_ref[2:3, :]
        yc = w0 * v2 + w1 * v1 + w2 * v
        szc, dszc = _silu_and_grad(zc_ref[...])
        yconv = (bcv * yc) * szc
        ysb = yssm.astype(BF16)
        ycb = yconv.astype(BF16)
        h2 = x_ref[...] + _dot(ysb, wo_ref[0:SSM_W, :]) + _dot(ycb, wo_ref[SSM_W:, :])
        r2 = lax.rsqrt(jnp.mean(h2 * h2, axis=-1, keepdims=True) + EPS)
        hn = h2 * r2
        gfv = gf_ref[...]
        err = hn * gfv - t_ref[...]
        loss_ref[...] += 0.5 * jnp.sum(jnp.mean(err * err, axis=-1, keepdims=True))
        dout = err * (1.0 / D_MODEL)
        dgf_ref[...] += jnp.sum(dout * hn, axis=0, keepdims=True)
        dn = dout * gfv
        dh2 = r2 * (dn - hn * jnp.mean(dn * hn, axis=-1, keepdims=True))
        dh2_ref[...] = dh2
        dh2b = dh2.astype(BF16)
        dwo_ref[0:SSM_W, :] += _dot_tn(ysb, dh2b)
        dwo_ref[SSM_W:, :] += _dot_tn(ycb, dh2b)
        dyssm = _dot_nt(dh2b, wo_ref[0:SSM_W, :])
        dyconv = _dot_nt(dh2b, wo_ref[SSM_W:, :])
        dy2 = dyssm * szs
        dzs_ref[...] = (dyssm * y2 * dszs).astype(BF16)
        dgp = dy2 * y1 * (gate * (1.0 - gate))
        dgpb = dgp.astype(BF16)
        dy1 = dy2 * gate + _dot_nt(dgpb, wg_ref[...])
        dwg_ref[...] += _dot_tn(y1b, dgpb)
        dbg_ref[...] += jnp.sum(dgp, axis=0, keepdims=True)
        dy_ref[...] = dy1 * dgelu
        dbc_ref[...] = (dyconv * yc * szc).astype(BF16)
        dyc = dyconv * bcv * szc
        dyc_ref[...] = dyc
        dzc_ref[...] = (dyconv * bcv * yc * dszc).astype(BF16)
        dcw_ref[0:1, :] += jnp.sum(dyc * v2, axis=0, keepdims=True)
        dcw_ref[1:2, :] += jnp.sum(dyc * v1, axis=0, keepdims=True)
        dcw_ref[2:3, :] += jnp.sum(dyc * v, axis=0, keepdims=True)

    tile_d = pl.BlockSpec((tm, D_MODEL), lambda i: (i, 0))
    tile_s = pl.BlockSpec((tm, SSM_W), lambda i: (i, 0))
    seg_of = lambda c: pl.BlockSpec((tm, SSM_W), lambda i: (i, c))
    halo_of = lambda c: pl.BlockSpec((SUBLANES, SSM_W), lambda i: (jnp.maximum(i * rows8 - 1, 0), c))
    const = lambda shape: pl.BlockSpec(shape, lambda i: (0,) * len(shape))
    seg = _out((n, SSM_W), F32)
    seg_b = _out((n, SSM_W), BF16)
    return _pcall(
        body, name="mix", grid=(n // tm,),
        out_shape=(_out((n, D_MODEL), F32), seg, seg_b, seg_b, seg_b, seg,
                   _out((D_MODEL, D_MODEL), F32), _out((SSM_W, SSM_W), F32),
                   _out((SUBLANES, LANES), F32), _out((1, D_MODEL), F32),
                   _out((1, SSM_W), F32), _out((SUBLANES, CONV_W), F32)),
        in_specs=[tile_d, tile_d, tile_s, seg_of(SEG_ZS), seg_of(SEG_H), seg_of(SEG_BC), seg_of(SEG_CC), seg_of(SEG_ZC),
                  halo_of(SEG_H), halo_of(SEG_CC),
                  const((1, D_MODEL)), const((1, SSM_W)), const((SUBLANES, CONV_W)),
                  const((SSM_W, SSM_W)), const((D_MODEL, D_MODEL))],
        out_specs=(tile_d, tile_s, tile_s, tile_s, tile_s, tile_s,
                   const((D_MODEL, D_MODEL)), const((SSM_W, SSM_W)), const((SUBLANES, LANES)),
                   const((1, D_MODEL)), const((1, SSM_W)), const((SUBLANES, CONV_W))),
        compiler_params=_params(1),
    )(x2, tgt2, y, proj, proj, proj, proj, proj, proj, proj, gf, b_glu, conv8, w_glu_f, w_out_f)


def _in_bwd(x2, dh2, du, dzs, dyc, proj, dbc, dzc, g1, conv8, w_full, seq):
    n = x2.shape[0]
    tm = TOK_TILE
    n_tiles = n // tm
    tiles_per_seq = seq // tm
    rows8 = tm // SUBLANES
    n_blk8 = n // SUBLANES

    def body(x_ref, dh2_ref, du_ref, dzs_ref, dyc_ref, dycn_ref, h_ref, cc_ref, dbc_ref, dzc_ref,
             g_ref, cw_ref, w_ref, gx_ref, dp_ref, dg_ref):
        i = pl.program_id(0)

        @pl.when(i == 0)
        def _():
            dg_ref[...] = jnp.zeros_like(dg_ref)

        dyc = dyc_ref[...]
        last = (i % tiles_per_seq) == tiles_per_seq - 1
        nhalo = jnp.where(last, 0.0, dycn_ref[...])
        dv = (cw_ref[2:3, :] * dyc + cw_ref[1:2, :] * _shift_up(dyc, nhalo, 1)
              + cw_ref[0:1, :] * _shift_up(dyc, nhalo, 2))
        parts = (du_ref[...], dzs_ref[...], dv * cc_ref[...], dbc_ref[...], dv * h_ref[...], dzc_ref[...])
        dxn = jnp.zeros((tm, D_MODEL), F32)
        for k, p in enumerate(parts):
            pb = p.astype(BF16)
            dp_ref[:, k * SSM_W:(k + 1) * SSM_W] = pb
            dxn = dxn + _dot_nt(pb, w_ref[:, k * SSM_W:(k + 1) * SSM_W])
        x = x_ref[...]
        r = lax.rsqrt(jnp.mean(x * x, axis=-1, keepdims=True) + EPS)
        xh = x * r
        dg_ref[...] += jnp.sum(dxn * xh, axis=0, keepdims=True)
        dn = dxn * g_ref[...]
        gx_ref[...] = dh2_ref[...] + r * (dn - xh * jnp.mean(dn * xh, axis=-1, keepdims=True))

    tile_d = pl.BlockSpec((tm, D_MODEL), lambda i: (i, 0))
    tile_s = pl.BlockSpec((tm, SSM_W), lambda i: (i, 0))
    seg_of = lambda c: pl.BlockSpec((tm, SSM_W), lambda i: (i, c))
    nhalo = pl.BlockSpec((SUBLANES, SSM_W), lambda i: (jnp.minimum((i + 1) * rows8, n_blk8 - 1), 0))
    const = lambda shape: pl.BlockSpec(shape, lambda i: (0,) * len(shape))
    return _pcall(
        body, name="in_bwd", grid=(n_tiles,),
        out_shape=(_out((n, D_MODEL), F32), _out((n, IN_COLS), BF16),
                   _out((SUBLANES, D_MODEL), F32)),
        in_specs=[tile_d, tile_d, tile_s, tile_s, tile_s, nhalo, seg_of(SEG_H), seg_of(SEG_CC), tile_s, tile_s,
                  const((1, D_MODEL)), const((SUBLANES, CONV_W)), const((D_MODEL, IN_COLS))],
        out_specs=(tile_d, pl.BlockSpec((tm, IN_COLS), lambda i: (i, 0)), const((SUBLANES, D_MODEL))),
        compiler_params=_params(1),
    )(x2, dh2, du, dzs, dyc, dyc, proj, proj, dbc, dzc, g1, conv8, w_full)


def _dw_in_exchange(order, xn, dproj, smalls):
    n = xn.shape[0]
    tk = 512
    nk = n // tk
    piece = (D_MODEL, COLS_PER_DEV)
    n_small = len(smalls)

    def body(order_ref, xn_hbm, dp_ref, *refs):
        del order_ref
        sm_refs = refs[:n_small]
        own_ref, rchip_ref = refs[n_small:n_small + 2]
        rsm_refs = refs[n_small + 2:2 * n_small + 2]
        (xn_ref, acc, stage, sbuf, relay_in, xn_sems, give_send, give_recv, keep_send, keep_recv,
         relay_send, relay_recv, sm_send, sm_recv, sm_loc) = refs[2 * n_small + 2:]
        s = pl.program_id(0)

        def xn_copy(kk):
            rows = pl.ds(pl.multiple_of(kk * tk, tk), tk)
            return pltpu.make_async_copy(xn_hbm.at[rows, :], xn_ref.at[rows, :], xn_sems.at[kk])

        @pl.when(s == 0)
        def _():
            for kk in range(nk):
                xn_copy(kk).start()
            xn_copy(0).wait()

        x, y, c = _mesh_pos()
        sib = (x, y, 1 - c)
        y_nbr, x_nbr = (x, 1 - y, c), (1 - x, y, c)
        half_rows = D_MODEL // 2
        gather = _TwoLevelGather(list(sm_refs), [functools.partial(lambda r, dev: r.at[dev], r) for r in rsm_refs],
                                 sm_send, sm_recv, sm_loc)

        def half(i, core):
            return acc.at[i % 2, :, pl.ds(pl.multiple_of(core * COLS_PER_DEV, LANES), COLS_PER_DEV)]

        def give(i):
            return pltpu.make_async_remote_copy(src_ref=half(i, 1 - c), dst_ref=stage.at[i], send_sem=give_send.at[i],
                                                recv_sem=give_recv.at[i], device_id=sib, device_id_type=MESH)

        def relay(r):
            rows = pl.ds(r * half_rows, half_rows)
            return pltpu.make_async_remote_copy(src_ref=sbuf.at[0, rows, :], dst_ref=relay_in.at[r],
                                                send_sem=relay_send.at[r], recv_sem=relay_recv.at[r],
                                                device_id=(x_nbr, y_nbr)[r], device_id_type=MESH)

        def keep(i):
            return pltpu.make_async_remote_copy(src_ref=sbuf.at[i], dst_ref=rchip_ref.at[i - 1],
                                                send_sem=keep_send.at[i - 1], recv_sem=keep_recv.at[i - 1],
                                                device_id=(None, y_nbr, x_nbr)[i], device_id_type=MESH)

        def chip_sum(i):
            give(i).wait_recv()
            mine = [acc[i % 2, :, cc * COLS_PER_DEV:(cc + 1) * COLS_PER_DEV] for cc in range(2)]
            return jnp.where(c == 0, mine[0], mine[1]) + stage[i]

        @pl.when(s == 0)
        def _():
            gather.start()

        @pl.when(s == N_CHIP // 2)
        def _():
            gather.forward()

        for k in range(2, N_CHIP):
            @pl.when(s == k)
            def _(k=k):
                give(k - 2).wait_send()

        slot = s % 2
        acc[slot] = _dot_tn(xn_ref[pl.ds(0, tk), :], dp_ref[pl.ds(0, tk), :])

        def kstep(kk, carry):
            @pl.when(s == 0)
            def _():
                xn_copy(kk).wait()

            off = pl.multiple_of(kk * tk, tk)
            acc[slot] += _dot_tn(xn_ref[pl.ds(off, tk), :], dp_ref[pl.ds(off, tk), :])
            return carry

        n_first = min(nk, 3)
        lax.fori_loop(1, n_first, kstep, 0)
        @pl.when(s == 1)
        def _():
            sbuf[0] = chip_sum(0).astype(BF16)
            relay(0).start()
            relay(1).start()

        for k in (2, 3):
            @pl.when(s == k)
            def _(k=k):
                i = k - 1
                r = i - 1
                total = chip_sum(i)
                relay(r).wait_recv()
                rows = slice(r * half_rows, (r + 1) * half_rows)
                other = slice((1 - r) * half_rows, (2 - r) * half_rows)
                sbuf[i, rows, :] = (total[rows, :] + relay_in[r].astype(F32)).astype(BF16)
                sbuf[i, other, :] = total[other, :].astype(BF16)
                keep(i).start()

        lax.fori_loop(n_first, nk, kstep, 0)

        for k in range(N_CHIP):
            @pl.when(s == k)
            def _(k=k):
                give(k).start()

        @pl.when(s == N_CHIP - 1)
        def _():
            own_ref[...] = chip_sum(N_CHIP - 1)
            give(N_CHIP - 2).wait_send()
            give(N_CHIP - 1).wait_send()
            for r in range(2):
                relay(r).wait_send()
            for i in (1, 2):
                keep(i).wait()
            gather.finish()

    grid_spec = pltpu.PrefetchScalarGridSpec(
        num_scalar_prefetch=1, grid=(N_CHIP,),
        in_specs=[HBM_SPEC,
                  pl.BlockSpec((n, COLS_PER_CHIP), lambda s, order: (0, order[s])),
                  *([HBM_SPEC] * n_small)],
        out_specs=(pl.BlockSpec(piece, lambda s, order: (0, 0)), HBM_SPEC, *([HBM_SPEC] * n_small)),
        scratch_shapes=[pltpu.VMEM((n, D_MODEL), BF16),
                        pltpu.VMEM((2, D_MODEL, COLS_PER_CHIP), F32), pltpu.VMEM((4,) + piece, F32),
                        pltpu.VMEM((3,) + piece, BF16), pltpu.VMEM((2, D_MODEL // 2, COLS_PER_DEV), BF16),
                        pltpu.SemaphoreType.DMA((nk,)),
                        pltpu.SemaphoreType.DMA((4,)), pltpu.SemaphoreType.DMA((4,)),
                        pltpu.SemaphoreType.DMA((2,)), pltpu.SemaphoreType.DMA((2,)),
                        pltpu.SemaphoreType.DMA((2,)), pltpu.SemaphoreType.DMA((2,)),
                        pltpu.SemaphoreType.DMA((7 * n_small,)), pltpu.SemaphoreType.DMA((7 * n_small,)),
                        pltpu.SemaphoreType.DMA((n_small,))])
    return _pcall(
        body, name="dw_in_exchange", grid_spec=grid_spec,
        out_shape=(_out(piece, F32), _out((2,) + piece, BF16),
                   *(_out((N_DEV,) + a.shape, a.dtype) for a in smalls)),
        compiler_params=_params(1),
    )(order, xn, dproj, *smalls)


def _adamw(g, w, m, v):
    m_new = ADAM_B1 * m + (1.0 - ADAM_B1) * g
    v_new = ADAM_B2 * v + (1.0 - ADAM_B2) * (g * g)
    m_hat = m_new / (1.0 - ADAM_B1 ** ADAM_STEP)
    v_hat = v_new / (1.0 - ADAM_B2 ** ADAM_STEP)
    delta = -ADAM_LR * (m_hat / (jnp.sqrt(v_hat) + ADAM_EPS) + ADAM_WD * w)
    return delta, m_new, v_new


def _reduce_adam_w_in(own, rchip, w, m, v):
    rows, cols = w.shape
    row_tile = 256

    def body(o_ref, r_ref, w_ref, m_ref, v_ref, g_ref, d_ref, nm_ref, nv_ref):
        g = o_ref[...]
        for s in range(2):
            g = g + r_ref[s].astype(F32)
        g_ref[...] = g
        d_ref[...], nm_ref[...], nv_ref[...] = _adamw(g, w_ref[...], m_ref[...], v_ref[...])

    tile = pl.BlockSpec((row_tile, cols), lambda i: (i, 0))
    shp = _out((rows, cols), F32)
    return _pcall(
        body, name="reduce_adam_w_in", grid=(rows // row_tile,),
        out_shape=(shp,) * 4,
        in_specs=[tile, pl.BlockSpec((2, row_tile, cols), lambda i: (0, i, 0)), tile, tile, tile],
        out_specs=(tile,) * 4,
        compiler_params=_params(1),
    )(own, rchip, w, m, v)


_SMALL_LEAVES = ("norm_gain", "final_norm_gain", "b_glu", "ssm_a_re", "ssm_a_im", "ssm_log_dt", "ssm_d", "conv_w",
                 "ssm_c_re", "ssm_c_im", "ssm_b_re", "ssm_b_im")


def _reduce_adam_small(r_pack, r_gc, r_gb, wmv, sharded):
    n_leaf = len(_SMALL_LEAVES)
    n_sh = len(sharded)

    def body(*refs):
        rp_ref, rgc_ref, rgb_ref = refs[:3]
        w_refs = refs[3:3 + 3 * n_leaf]
        sh_in = refs[3 + 3 * n_leaf:3 + 3 * n_leaf + 4 * n_sh]
        outs0 = 3 + 3 * n_leaf + 4 * n_sh
        loss_ref = refs[outs0]
        o_refs = refs[outs0 + 1:outs0 + 1 + 4 * n_leaf]
        sh_out = refs[outs0 + 1 + 4 * n_leaf:outs0 + 1 + 4 * n_leaf + 4 * n_sh]
        own_conv = refs[-1]

        def total(ref):
            acc = ref[0].astype(F32)
            for s in range(1, N_DEV):
                acc = acc + ref[s].astype(F32)
            return acc

        for i in range(n_sh):
            r_ref, w_ref, m_ref, v_ref = sh_in[4 * i:4 * i + 4]
            o_g, o_d, o_m, o_v = sh_out[4 * i:4 * i + 4]
            g = total(r_ref)
            o_g[...] = g
            o_d[...], o_m[...], o_v[...] = _adamw(g, w_ref[...], m_ref[...], v_ref[...])

        sp = total(rp_ref)
        sgc = total(rgc_ref)
        sgb = total(rgb_ref)
        loss_ref[...] = sp[ROW_LOSS:ROW_LOSS + SUBLANES, 0:LANES]

        def wide(r):
            return jnp.concatenate([sp[r:r + 1, :], sp[r + 1:r + 2, :]], axis=1)

        s5 = slice(ROW_S5, ROW_S5 + N_GROUPS)
        eye = (lax.broadcasted_iota(jnp.int32, (N_GROUPS, N_GROUPS), 0)
               == lax.broadcasted_iota(jnp.int32, (N_GROUPS, N_GROUPS), 1)).astype(F32)
        d_row = sp[ROW_BGLU_D + 1:ROW_BGLU_D + 2, :]
        me = 4 * lax.axis_index("x") + 2 * lax.axis_index("y") + lax.axis_index("c")
        for k in range(N_DEV):
            @pl.when(me == k)
            def _(k=k):
                own_conv[...] = sp[ROW_CONV:ROW_CONV + SUBLANES, k * CONV_COLS_PER_DEV:(k + 1) * CONV_COLS_PER_DEV]
        grads = {
            "norm_gain": wide(ROW_NORM_GAIN),
            "final_norm_gain": wide(ROW_FINAL_GAIN),
            "b_glu": sp[ROW_BGLU_D:ROW_BGLU_D + 1, :],
            "ssm_a_re": sp[s5, LANE_A_RE:LANE_A_RE + STATE],
            "ssm_a_im": sp[s5, LANE_A_IM:LANE_A_IM + STATE],
            "ssm_log_dt": jnp.sum(sp[s5, LANE_LOG_DT:LANE_LOG_DT + 1] * eye, axis=0, keepdims=True),
            "ssm_d": jnp.concatenate([d_row[:, g * GROUP:(g + 1) * GROUP] for g in range(N_GROUPS)], axis=0),
            "conv_w": own_conv[0:3, :],
            "ssm_c_re": sgc[:, 0:STATE],
            "ssm_c_im": sgc[:, STATE:2 * STATE],
            "ssm_b_re": sgb[:, 0:STATE],
            "ssm_b_im": sgb[:, STATE:2 * STATE],
        }
        for i, name in enumerate(_SMALL_LEAVES):
            g = grads[name]
            w_ref, m_ref, v_ref = w_refs[3 * i:3 * i + 3]
            o_g, o_d, o_m, o_v = o_refs[4 * i:4 * i + 4]
            o_g[...] = g
            o_d[...], o_m[...], o_v[...] = _adamw(g, w_ref[...], m_ref[...], v_ref[...])

    flat_w = [a for name in _SMALL_LEAVES for a in wmv[name]]
    leaf_shapes = [_out(wmv[name][0].shape, F32) for name in _SMALL_LEAVES for _ in range(4)]
    sh_shapes = [_out(entry[1].shape, F32) for entry in sharded for _ in range(4)]
    operands = (r_pack, r_gc, r_gb, *flat_w, *(a for entry in sharded for a in entry))
    out_shape = (_out((SUBLANES, LANES), F32), *leaf_shapes, *sh_shapes)
    outs = _pcall(
        body, name="reduce_adam_small", grid=(1,), out_shape=out_shape,
        in_specs=_whole_specs(operands), out_specs=tuple(_whole_specs(out_shape)),
        scratch_shapes=[pltpu.VMEM((SUBLANES, CONV_COLS_PER_DEV), F32)],
        compiler_params=_params(1),
    )(*operands)
    leaves = {name: outs[1 + 4 * i:5 + 4 * i] for i, name in enumerate(_SMALL_LEAVES)}
    first = 1 + 4 * n_leaf
    return outs[0], leaves, [outs[first + 4 * i:first + 4 * i + 4] for i in range(n_sh)]


def kernel(x, norm_gain, w_in, ssm_a_re, ssm_a_im, ssm_log_dt, ssm_b_re, ssm_b_im, ssm_c_re, ssm_c_im, ssm_d, w_glu, b_glu, conv_w, w_out, final_norm_gain, loss_target, m_norm_gain, m_w_in, m_ssm_a_re, m_ssm_a_im, m_ssm_log_dt, m_ssm_b_re, m_ssm_b_im, m_ssm_c_re, m_ssm_c_im, m_ssm_d, m_w_glu, m_b_glu, m_conv_w, m_w_out, m_final_norm_gain, v_norm_gain, v_w_in, v_ssm_a_re, v_ssm_a_im, v_ssm_log_dt, v_ssm_b_re, v_ssm_b_im, v_ssm_c_re, v_ssm_c_im, v_ssm_d, v_w_glu, v_b_glu, v_conv_w, v_w_out, v_final_norm_gain):
    n_seq, seq, _ = x.shape
    n = n_seq * seq

    gh_p = lambda b4: jnp.transpose(b4, (0, 1, 3, 2)).reshape(N_GROUPS * GROUP, STATE)
    c2 = lambda a: a.reshape(N_GROUPS * GROUP, STATE)
    b_re2, b_im2 = gh_p(ssm_b_re), gh_p(ssm_b_im)
    d_row = ssm_d[0].reshape(1, SSM_W)

    x2 = x.reshape(n, D_MODEL)
    tgt2 = loss_target.reshape(n, D_MODEL)
    mx, my, mc = lax.axis_index("x"), lax.axis_index("y"), lax.axis_index("c")
    chip_ids = [2 * cx + cy for cx, cy in ((mx, my), (1 - mx, my), (mx, 1 - my), (1 - mx, 1 - my))]
    arrival = chip_ids
    xn, proj, w_in_f, s5 = _in_proj(
        jnp.stack(arrival).astype(jnp.int32), x2, norm_gain, w_in[0].astype(BF16),
        (ssm_a_re[0], ssm_a_im[0], ssm_log_dt, b_re2, b_im2, c2(ssm_c_re), c2(ssm_c_im)))
    a_re_x, a_im_x, log_dt_x, ab_re, ab_im, bb_re_m, bb_im_m, c_re_m, c_imn_m = s5
    u3 = proj.reshape(n_seq, seq, IN_COLS)
    conv_p = jnp.pad(conv_w[0], ((0, SUBLANES - 3), (0, LANES - CONV_COLS_PER_DEV)))
    s_re, s_im, y3, w_out_f, w_glu_f, conv_all = _ssm_fwd(
        u3, bb_re_m, bb_im_m, c_re_m, c_imn_m, d_row, ab_re, ab_im,
        w_out[0].astype(BF16), w_glu[0].astype(BF16), conv_p, n_seq, seq)
    conv8 = jnp.transpose(conv_all[:, :, :CONV_COLS_PER_DEV], (1, 0, 2)).reshape(SUBLANES, CONV_W)
    (dh2, dy, dzs, dbc, dzc, dyc, dw_out, dw_glu, loss_t, dgf, dbg, dcw) = _mix(
        x2, tgt2, y3.reshape(n, SSM_W), proj, final_norm_gain.reshape(1, D_MODEL), b_glu, conv8,
        w_glu_f, w_out_f, seq)

    du3, dc_re_d, dc_im_d, dbb_re_d, dbb_im_d, dab_re, dab_im, dd, r_out, r_glu = _ssm_bwd(
        dy.reshape(n_seq, seq, SSM_W), u3, s_re, s_im, bb_re_m, bb_im_m, c_re_m, c_imn_m, d_row, ab_re, ab_im,
        dw_out.reshape(N_DEV, OUT_ROWS_PER_DEV, D_MODEL), dw_glu.reshape(N_DEV, GLU_ROWS_PER_DEV, SSM_W), n_seq, seq)
    du = du3.reshape(n, SSM_W)
    grad_x2, dproj, dg8 = _in_bwd(x2, dh2, du, dzs, dyc, proj, dbc, dzc, norm_gain, conv8, w_in_f, seq)
    pack, gc, gb = _ssm_disc_bwd_pack(
        a_re_x, a_im_x, log_dt_x, b_re2, b_im2, dab_re.reshape(N_GROUPS, STATE), dab_im.reshape(N_GROUPS, STATE),
        dbb_re_d, dbb_im_d, loss_t, dg8, dgf, dbg, dd, dcw, dc_re_d, dc_im_d)

    order = [chip_ids[3], chip_ids[2], chip_ids[1], chip_ids[0]]
    own_in, rchip_in, r_pack, r_gc, r_gb = _dw_in_exchange(
        jnp.stack(order).astype(jnp.int32), xn, dproj, [pack, gc, gb])

    flat2 = lambda a: a.reshape(a.shape[-2:]) if a.ndim > 2 else a.reshape(1, -1)
    c2 = lambda a: a.reshape(N_GROUPS * GROUP, STATE)
    wmv = dict(norm_gain=(norm_gain, m_norm_gain, v_norm_gain),
               final_norm_gain=tuple(flat2(a) for a in (final_norm_gain, m_final_norm_gain, v_final_norm_gain)),
               b_glu=(b_glu, m_b_glu, v_b_glu),
               ssm_a_re=tuple(flat2(a) for a in (ssm_a_re, m_ssm_a_re, v_ssm_a_re)),
               ssm_a_im=tuple(flat2(a) for a in (ssm_a_im, m_ssm_a_im, v_ssm_a_im)),
               ssm_log_dt=(ssm_log_dt, m_ssm_log_dt, v_ssm_log_dt),
               ssm_d=tuple(flat2(a) for a in (ssm_d, m_ssm_d, v_ssm_d)),
               conv_w=tuple(flat2(a) for a in (conv_w, m_conv_w, v_conv_w)),
               ssm_c_re=tuple(c2(a) for a in (ssm_c_re, m_ssm_c_re, v_ssm_c_re)),
               ssm_c_im=tuple(c2(a) for a in (ssm_c_im, m_ssm_c_im, v_ssm_c_im)),
               ssm_b_re=(b_re2, gh_p(m_ssm_b_re), gh_p(v_ssm_b_re)),
               ssm_b_im=(b_im2, gh_p(m_ssm_b_im), gh_p(v_ssm_b_im)))

    res_in = _reduce_adam_w_in(own_in, rchip_in, w_in[0], m_w_in[0], v_w_in[0])
    loss8, small, (res_out, res_glu) = _reduce_adam_small(
        r_pack, r_gc, r_gb, wmv,
        [(r_out, w_out[0], m_w_out[0], v_w_out[0]), (r_glu, w_glu[0], m_w_glu[0], v_w_glu[0])])
    loss = loss8[0, 0]

    shapes = dict(norm_gain=(1, D_MODEL), ssm_a_re=(1, N_GROUPS, STATE), ssm_a_im=(1, N_GROUPS, STATE),
                  ssm_log_dt=(1, N_GROUPS), ssm_c_re=(1, N_GROUPS, GROUP, STATE), ssm_c_im=(1, N_GROUPS, GROUP, STATE),
                  ssm_d=(1, N_GROUPS, GROUP), b_glu=(1, SSM_W), final_norm_gain=(D_MODEL,),
                  conv_w=(1, 3, CONV_COLS_PER_DEV))
    big = dict(w_in=res_in, w_glu=res_glu, w_out=res_out)

    def leaf(kind, name):
        if name in big:
            return big[name][kind][None]
        if name in ("ssm_b_re", "ssm_b_im"):
            return jnp.transpose(small[name][kind].reshape(1, N_GROUPS, GROUP, STATE), (0, 1, 3, 2))
        return small[name][kind].reshape(shapes[name])

    order = ["norm_gain", "w_in", "ssm_a_re", "ssm_a_im", "ssm_log_dt", "ssm_b_re", "ssm_b_im", "ssm_c_re",
             "ssm_c_im", "ssm_d", "w_glu", "b_glu", "conv_w", "w_out", "final_norm_gain"]
    outs = [loss, grad_x2.reshape(x.shape)]
    for kind in range(4):
        outs += [leaf(kind, name) for name in order]
    return tuple(outs)
```

```python
import functools
import math

import jax
import jax.numpy as jnp
from jax import lax
from jax.experimental import pallas as pl
from jax.experimental.pallas import tpu as pltpu

F32 = jnp.float32
BF16 = jnp.bfloat16

N_DEV = 8
D_MODEL = 1024
SSM_W = 512
CONV_W = 512
N_GROUPS = 32
GROUP = 16
STATE = 64
IN_COLS = 3072
SEG_U, SEG_ZS, SEG_H, SEG_BC, SEG_CC, SEG_ZC = range(6)
COLS_PER_DEV = IN_COLS // N_DEV
N_CHIP = N_DEV // 2
COLS_PER_CHIP = 2 * COLS_PER_DEV
OUT_ROWS_PER_DEV = D_MODEL // N_DEV
GLU_ROWS_PER_DEV = SSM_W // N_DEV
CONV_COLS_PER_DEV = CONV_W // N_DEV
EPS = 1e-6

N_JBLK = 4
JB_CH = SSM_W // N_JBLK
JB_ST = N_GROUPS * STATE // N_JBLK

ADAM_LR = 0.001
ADAM_B1 = 0.9
ADAM_B2 = 0.999
ADAM_EPS = 1e-08
ADAM_WD = 0.01
ADAM_STEP = 10

SUBLANES = 8
LANES = 128
VMEM_LIMIT = 48 * 1024 * 1024
TOK_TILE = 256
IN_TILE = 1024
SCAN_TILE = 1024

MESH = pl.DeviceIdType.MESH
HBM_SPEC = pl.BlockSpec(memory_space=pltpu.HBM)


def _build(body, **kw):
    return pl.pallas_call(body, **kw)


def _pcall(body, **kw):
    def call(*operands):
        pinned = [a if jnp.issubdtype(a.dtype, jnp.integer) else pltpu.with_memory_space_constraint(a, pltpu.HBM)
                  for a in operands]
        return _build(body, **kw)(*pinned)
    return call


def _whole_specs(arrays):
    return [pl.BlockSpec(a.shape, functools.partial(lambda nd, i: (0,) * nd, len(a.shape))) for a in arrays]


def _out(shape, dtype):
    return pltpu.HBM(tuple(shape), dtype)


def _params(n_grid):
    return pltpu.CompilerParams(dimension_semantics=("arbitrary",) * n_grid,
                                vmem_limit_bytes=VMEM_LIMIT)


def _dot(a, b):
    return jnp.dot(a, b, preferred_element_type=F32)


def _dot_nt(a, b):
    return lax.dot_general(a, b, (((1,), (1,)), ((), ())), preferred_element_type=F32)


def _dot_tn(a, b):
    return lax.dot_general(a, b, (((0,), (0,)), ((), ())), preferred_element_type=F32)


def _sigmoid(z):
    return 1.0 / (1.0 + jnp.exp(-z))


_GELU_C = math.sqrt(2.0 / math.pi)


def _gelu_and_grad(y):
    inner = _GELU_C * (y + 0.044715 * (y * y * y))
    t = jnp.tanh(inner)
    g = 0.5 * y * (1.0 + t)
    dg = 0.5 * (1.0 + t) + 0.5 * y * (1.0 - t * t) * (_GELU_C * (1.0 + 3.0 * 0.044715 * (y * y)))
    return g, dg


def _silu_and_grad(z):
    s = _sigmoid(z)
    return z * s, s * (1.0 + z * (1.0 - s))


def _shift_down(v, halo, k):
    rolled = pltpu.roll(v, k, 0)
    row = lax.broadcasted_iota(jnp.int32, v.shape, 0)
    for r in range(k):
        rolled = jnp.where(row == r, halo[SUBLANES - k + r:SUBLANES - k + r + 1, :], rolled)
    return rolled


def _shift_up(v, halo, k):
    n = v.shape[0]
    rolled = pltpu.roll(v, n - k, 0)
    row = lax.broadcasted_iota(jnp.int32, v.shape, 0)
    for r in range(k):
        rolled = jnp.where(row == n - k + r, halo[r:r + 1, :], rolled)
    return rolled


def _mesh_pos():
    return lax.axis_index("x"), lax.axis_index("y"), lax.axis_index("c")


def _direct_copies(srcs_for, out_refs, send_sems, recv_sems, loc_sems):
    x, y, c = _mesh_pos()
    me_id = 4 * x + 2 * y + c
    n_arr = len(out_refs)
    dsts = [r.at[me_id] for r in out_refs]
    own = srcs_for(me_id)
    mine = [pltpu.make_async_copy(own[a], dsts[a], loc_sems.at[a]) for a in range(n_arr)]
    sends = []
    for k in range(1, N_DEV):
        px, py, pc = x ^ ((k >> 2) & 1), y ^ ((k >> 1) & 1), c ^ (k & 1)
        src = srcs_for(4 * px + 2 * py + pc)
        for a in range(n_arr):
            sends.append(pltpu.make_async_remote_copy(
                src_ref=src[a], dst_ref=dsts[a],
                send_sem=send_sems.at[(k - 1) * n_arr + a], recv_sem=recv_sems.at[(k - 1) * n_arr + a],
                device_id=(px, py, pc), device_id_type=MESH))
    return mine, sends


class _TwoLevelGather:
    def __init__(self, srcs, slots, send_sems, recv_sems, loc_sems):
        self.srcs, self.slots, self.n_arr = srcs, slots, len(srcs)
        self.send_sems, self.recv_sems, self.loc_sems = send_sems, recv_sems, loc_sems
        x, y, c = _mesh_pos()
        self.c = c
        self.me, self.sib = (x, y, c), (x, y, 1 - c)
        self.chips = [(1 - x, y), (x, 1 - y), (1 - x, 1 - y)]

    def _copies(self, k, block, to, from_src=False):
        dev = 4 * block[0] + 2 * block[1] + block[2]
        return [pltpu.make_async_remote_copy(
            src_ref=self.srcs[a] if from_src else self.slots[a](dev), dst_ref=self.slots[a](dev),
            send_sem=self.send_sems.at[k * self.n_arr + a], recv_sem=self.recv_sems.at[k * self.n_arr + a],
            device_id=to, device_id_type=MESH) for a in range(self.n_arr)]

    def _local(self):
        dev = 4 * self.me[0] + 2 * self.me[1] + self.me[2]
        return [pltpu.make_async_copy(self.srcs[a], self.slots[a](dev), self.loc_sems.at[a])
                for a in range(self.n_arr)]

    def start(self):
        for cp in self._local() + self._copies(0, self.me, self.sib, True):
            cp.start()
        for j in (0, 1):
            for cp in self._copies(1 + j, self.me, (*self.chips[j], self.c), True):
                cp.start()

    def wait_own(self):
        for cp in self._local():
            cp.wait()

    def wait_sibling(self):
        for cp in self._copies(0, self.sib, self.me):
            cp.wait_recv()

    def wait_and_pass_on(self, j):
        chip = self.chips[j]
        for cp in self._copies(1 + j, (*chip, self.c), self.me):
            cp.wait_recv()
        for cp in self._copies(4 + j, (*chip, self.c), self.sib):
            cp.start()

    def neighbours_landed(self):
        x, y, c = self.me
        self.wait_and_pass_on(0)
        self.wait_and_pass_on(1)
        for cp in self._copies(1 + 2, (x ^ c, y ^ (1 - c), c), (x ^ (1 - c), y ^ c, c)):
            cp.start()

    def diagonal_landed(self):
        self.wait_and_pass_on(2)

    def wait_passed_on(self, j):
        for cp in self._copies(4 + j, (*self.chips[j], 1 - self.c), self.me):
            cp.wait_recv()

    def wait_sends(self):
        for cp in self._copies(0, self.me, self.sib, True):
            cp.wait_send()
        for j, chip in enumerate(self.chips):
            for cp in self._copies(1 + j, self.me, (*chip, self.c), True) + self._copies(4 + j, (*chip, self.c), self.sib):
                cp.wait_send()

    def finish(self):
        self.wait_sibling()
        for j in range(3):
            self.wait_passed_on(j)
        self.wait_sends()
        self.wait_own()


def _disc(a_re, a_im, log_dt, b_re, b_im):
    dt = jnp.exp(log_dt)
    mag = jnp.exp(a_re * dt)
    ab_re = mag * jnp.cos(a_im * dt)
    ab_im = mag * jnp.sin(a_im * dt)
    den = a_re * a_re + a_im * a_im
    p_re = ab_re - 1.0
    p_im = ab_im
    q_re = (p_re * a_re + p_im * a_im) / den
    q_im = (p_im * a_re - p_re * a_im) / den
    bb_re = q_re * b_re - q_im * b_im
    bb_im = q_re * b_im + q_im * b_re
    return ab_re, ab_im, bb_re, bb_im


def _split3(v):
    hi = v.astype(BF16)
    r1 = v - hi.astype(F32)
    mid = r1.astype(BF16)
    lo = (r1 - mid.astype(F32)).astype(BF16)
    return hi, mid, lo


def _select_dot(sel, v):
    return sum(_dot(sel, t) for t in _split3(v))


PACK_ROWS = 72
PACK_W = 512
ROW_FINAL_GAIN, ROW_NORM_GAIN, ROW_BGLU_D, ROW_CONV, ROW_LOSS, ROW_S5 = 0, 8, 16, 24, 32, 40
LANE_A_RE, LANE_A_IM, LANE_LOG_DT = 0, 128, 256


def _ssm_disc_bwd_pack(a_re_x, a_im_x, log_dt_x, b_re, b_im, g_ab_re, g_ab_im, dbb_re_d, dbb_im_d,
                       loss_t, dg8, dgf, dbg, dd, dcw, dc_re_d, dc_im_d):
    rows_gh = N_GROUPS * GROUP

    def body(are, aim, ldt, bre, bim, gabre, gabim, dbbre_ref, dbbim_ref,
             loss_ref, dg8_ref, dgf_ref, dbg_ref, dd_ref, dcw_ref, dcre_ref, dcim_ref,
             p_ref, gc_ref, gb_ref, gbb_re, gbb_im):
        r_g = lax.broadcasted_iota(jnp.int32, (N_GROUPS, rows_gh), 0)
        c_gh = lax.broadcasted_iota(jnp.int32, (N_GROUPS, rows_gh), 1)
        group_sum = (c_gh // GROUP == r_g).astype(BF16)
        r_gh = lax.broadcasted_iota(jnp.int32, (rows_gh, N_GROUPS), 0)
        c_g = lax.broadcasted_iota(jnp.int32, (rows_gh, N_GROUPS), 1)
        first_row = (r_gh == c_g * GROUP).astype(BF16)

        def diag_block(ref, j, gi):
            return ref[j, gi * GROUP:(gi + 1) * GROUP, gi * STATE:(gi + 1) * STATE]

        for j in range(N_JBLK):
            for gi in range(SUBLANES):
                r0 = (j * SUBLANES + gi) * GROUP
                gbb_re[r0:r0 + GROUP, :] = diag_block(dbbre_ref, j, gi)
                gbb_im[r0:r0 + GROUP, :] = diag_block(dbbim_ref, j, gi)
                both = jnp.concatenate([diag_block(dcre_ref, j, gi), -diag_block(dcim_ref, j, gi)], axis=1)
                gc_ref[r0:r0 + GROUP, :] = both.astype(BF16)

        _, vjp = jax.vjp(_disc, are[...], aim[...], ldt[...], bre[...], bim[...])
        d_are, d_aim, d_ldt, d_bre, d_bim = vjp((_select_dot(first_row, gabre[...]), _select_dot(first_row, gabim[...]),
                                                 gbb_re[...], gbb_im[...]))
        gb_ref[...] = jnp.concatenate([d_bre, d_bim], axis=1).astype(BF16)

        p_ref[...] = jnp.zeros_like(p_ref)
        half = D_MODEL // 2
        for r, src in ((ROW_FINAL_GAIN, dgf_ref), (ROW_NORM_GAIN, dg8_ref)):
            p_ref[r:r + 1, :] = src[0:1, 0:half]
            p_ref[r + 1:r + 2, :] = src[0:1, half:D_MODEL]
        p_ref[ROW_BGLU_D:ROW_BGLU_D + 1, :] = dbg_ref[...]
        p_ref[ROW_BGLU_D + 1:ROW_BGLU_D + 2, :] = dd_ref[...]
        p_ref[ROW_CONV:ROW_CONV + SUBLANES, :] = dcw_ref[...]
        p_ref[ROW_LOSS:ROW_LOSS + SUBLANES, 0:LANES] = loss_ref[...]
        s5 = slice(ROW_S5, ROW_S5 + N_GROUPS)
        p_ref[s5, LANE_A_RE:LANE_A_RE + STATE] = _select_dot(group_sum, d_are)
        p_ref[s5, LANE_A_IM:LANE_A_IM + STATE] = _select_dot(group_sum, d_aim)
        p_ref[s5, LANE_LOG_DT:LANE_LOG_DT + LANES] = _select_dot(group_sum, jnp.broadcast_to(d_ldt, (rows_gh, LANES)))

    operands = (a_re_x, a_im_x, log_dt_x, b_re, b_im, g_ab_re, g_ab_im, dbb_re_d, dbb_im_d,
                loss_t, dg8, dgf, dbg, dd, dcw, dc_re_d, dc_im_d)
    out_shape = (_out((PACK_ROWS, PACK_W), F32),
                 _out((rows_gh, 2 * STATE), BF16),
                 _out((rows_gh, 2 * STATE), BF16))
    return _pcall(body, name="ssm_disc_bwd_pack", grid=(1,), out_shape=out_shape,
                  in_specs=_whole_specs(operands), out_specs=tuple(_whole_specs(out_shape)),
                  scratch_shapes=[pltpu.VMEM((rows_gh, STATE), F32), pltpu.VMEM((rows_gh, STATE), F32)],
                  compiler_params=_params(1))(*operands)


def _s5_prepare(are, aim, ldt, bre, bim, cre, cim,
                o_ax_re, o_ax_im, o_ldt_x, o_ab_re, o_ab_im, o_bb_re, o_bb_im, o_c_re, o_c_imn):
    rows_gh = N_GROUPS * GROUP
    rep = (lax.broadcasted_iota(jnp.int32, (rows_gh, N_GROUPS), 0) // GROUP
           == lax.broadcasted_iota(jnp.int32, (rows_gh, N_GROUPS), 1)).astype(BF16)
    eye = (lax.broadcasted_iota(jnp.int32, (N_GROUPS, N_GROUPS), 0)
           == lax.broadcasted_iota(jnp.int32, (N_GROUPS, N_GROUPS), 1)).astype(F32)
    ldt_col = jnp.sum(eye * ldt[...], axis=1, keepdims=True)
    a_re_x = _select_dot(rep, are[...])
    a_im_x = _select_dot(rep, aim[...])
    ldt_x = _select_dot(rep, jnp.broadcast_to(ldt_col, (N_GROUPS, LANES)))[:, 0:1]
    o_ax_re[...] = a_re_x
    o_ax_im[...] = a_im_x
    o_ldt_x[...] = ldt_x
    ab_re, ab_im, bb_re, bb_im = _disc(a_re_x, a_im_x, ldt_x, bre[...], bim[...])
    for j in range(N_JBLK):
        first = [(j * SUBLANES + gi) * GROUP for gi in range(SUBLANES)]
        o_ab_re[j] = jnp.concatenate([ab_re[r:r + 1, :] for r in first], axis=1)
        o_ab_im[j] = jnp.concatenate([ab_im[r:r + 1, :] for r in first], axis=1)
    for o, v in ((o_bb_re, bb_re), (o_bb_im, bb_im), (o_c_re, cre[...]), (o_c_imn, -cim[...])):
        for j in range(N_JBLK):
            for gi in range(SUBLANES):
                r0 = (j * SUBLANES + gi) * GROUP
                parts = [v[r0:r0 + GROUP, :] if k == gi else jnp.zeros((GROUP, STATE), F32) for k in range(SUBLANES)]
                o[j, gi * GROUP:(gi + 1) * GROUP, :] = jnp.concatenate(parts, axis=1).astype(BF16)


def _in_proj(order, x2, g1, w_in_b, s5):
    n = x2.shape[0]
    tm = min(IN_TILE, n)
    n_tiles = n // tm
    n_s5_in = len(s5)
    n_s5_out = 9

    def body(order_ref, x_ref, g_ref, w_ref, *refs):
        s5_in = refs[:n_s5_in]
        xn_ref, proj_ref, wall_ref = refs[n_s5_in:n_s5_in + 3]
        s5_out = refs[n_s5_in + 3:n_s5_in + 3 + n_s5_out]
        xn_scr, wbuf, send_sems, recv_sems, loc_sems, out_sems = refs[n_s5_in + 3 + n_s5_out:]
        k = pl.program_id(0)
        i = pl.program_id(1)

        def slot(dev):
            return wbuf.at[dev // 2, :, pl.ds(pl.multiple_of((dev % 2) * COLS_PER_DEV, LANES), COLS_PER_DEV)]

        gather = _TwoLevelGather([w_ref], [slot], send_sems, recv_sems, loc_sems)

        @pl.when((k == 0) & (i == 0))
        def _():
            gather.start()

        def own_chip():
            gather.wait_own()
            gather.wait_sibling()

        def x_chip():
            gather.neighbours_landed()
            gather.wait_passed_on(0)

        def diag_chip():
            gather.diagonal_landed()
            gather.wait_passed_on(2)

        arrivals = [own_chip, x_chip, functools.partial(gather.wait_passed_on, 1), diag_chip]
        for kk, arrived in enumerate(arrivals):
            @pl.when((k == kk) & (i == 0))
            def _(arrived=arrived):
                arrived()

        rows = pl.ds(pl.multiple_of(i * tm, tm), tm)

        @pl.when(k == 0)
        def _():
            x = x_ref[...]
            r = lax.rsqrt(jnp.mean(x * x, axis=-1, keepdims=True) + EPS)
            xn = ((x * r) * g_ref[...]).astype(BF16)
            xn_scr[rows, :] = xn
            xn_ref[...] = xn

        proj_ref[...] = _dot(xn_scr[rows, :], wbuf[order_ref[k]])

        @pl.when((k == 0) & (i == n_tiles - 1))
        def _():
            _s5_prepare(*s5_in, *s5_out)

        @pl.when((k == N_CHIP - 1) & (i == n_tiles - 1))
        def _():
            gather.wait_sends()
            outs = [pltpu.make_async_copy(wbuf.at[q], wall_ref.at[:, q * COLS_PER_CHIP:(q + 1) * COLS_PER_CHIP],
                                          out_sems.at[q]) for q in range(N_CHIP)]
            for cp in outs:
                cp.start()
            for cp in outs:
                cp.wait()

    tile_once = lambda k, i, order: (jnp.where(k == 0, i, n_tiles - 1), 0)
    whole = lambda shape: pl.BlockSpec(shape, lambda k, i, order: (0,) * len(shape))
    rows_gh = N_GROUPS * GROUP
    s5_out_shapes = ([(rows_gh, STATE), F32], [(rows_gh, STATE), F32], [(rows_gh, 1), F32],
                     [(N_JBLK, 1, JB_ST), F32], [(N_JBLK, 1, JB_ST), F32]) + ([(N_JBLK, JB_CH, JB_ST), BF16],) * 4
    grid_spec = pltpu.PrefetchScalarGridSpec(
        num_scalar_prefetch=1, grid=(N_CHIP, n_tiles),
        in_specs=[pl.BlockSpec((tm, D_MODEL), tile_once),
                  whole((1, D_MODEL)),
                  HBM_SPEC,
                  *(whole(a.shape) for a in s5)],
        out_specs=(pl.BlockSpec((tm, D_MODEL), tile_once),
                   pl.BlockSpec((tm, COLS_PER_CHIP), lambda k, i, order: (i, order[k])),
                   HBM_SPEC,
                   *(whole(shape) for shape, _ in s5_out_shapes)),
        scratch_shapes=[pltpu.VMEM((n, D_MODEL), BF16), pltpu.VMEM((N_CHIP, D_MODEL, COLS_PER_CHIP), BF16),
                        pltpu.SemaphoreType.DMA((7,)), pltpu.SemaphoreType.DMA((7,)), pltpu.SemaphoreType.DMA((1,)),
                        pltpu.SemaphoreType.DMA((N_CHIP,))])
    outs = _pcall(
        body, name="in_proj", grid_spec=grid_spec,
        out_shape=(_out((n, D_MODEL), BF16), _out((n, IN_COLS), F32),
                   _out((D_MODEL, IN_COLS), BF16),
                   *(_out(shape, dt) for shape, dt in s5_out_shapes)),
        compiler_params=_params(2),
    )(order, x2, g1, w_in_b, *s5)
    return outs[0], outs[1], outs[2], outs[3:]


def _cmul(p, q):
    return p[0] * q[0] - p[1] * q[1], p[0] * q[1] + p[1] * q[0]


def _scan_tables(ar, ai, width, reverse):
    pows = [(ar, ai)]
    for _ in range(SUBLANES - 1):
        pows.append(_cmul(pows[-1], (ar, ai)))
    row = lax.broadcasted_iota(jnp.int32, (SUBLANES, width), 0)

    def bc(v):
        return jnp.broadcast_to(v, (SUBLANES, width))

    levels = []
    for k in (1, 2, 4):
        keep = (row <= SUBLANES - 1 - k) if reverse else (row >= k)
        levels.append((jnp.where(keep, bc(pows[k - 1][0]), 0.0), jnp.where(keep, bc(pows[k - 1][1]), 0.0)))
    cre = jnp.zeros((SUBLANES, width), F32)
    cim = jnp.zeros((SUBLANES, width), F32)
    for r in range(SUBLANES):
        e = (SUBLANES - r) if reverse else (r + 1)
        cre = jnp.where(row == r, bc(pows[e - 1][0]), cre)
        cim = jnp.where(row == r, bc(pows[e - 1][1]), cim)
    return levels, (cre, cim)


def _load_chunked(src_ref, b, dst_ref, n_rows):
    n_blk = n_rows // SUBLANES
    for i in range(n_blk):
        dst_ref[b, i * SUBLANES:(i + 1) * SUBLANES, :] = src_ref[b, pl.ds(i, SUBLANES, stride=n_blk), :]


def _store_chunked(val, dst_ref, b, n_rows):
    n_blk = n_rows // SUBLANES
    for i in range(n_blk):
        dst_ref[b, pl.ds(i, SUBLANES, stride=n_blk), :] = val[i * SUBLANES:(i + 1) * SUBLANES, :]


def _chunk_scan(re_ref, im_ref, bs, car_ref, ar, ai, n_rows, reverse, on_block=None):
    width = re_ref.shape[2]
    n_blk = n_rows // SUBLANES
    shape = (SUBLANES, width)
    abr = jnp.broadcast_to(ar, shape)
    abi = jnp.broadcast_to(ai, shape)
    order = list(range(n_blk - 1, -1, -1)) if reverse else list(range(n_blk))

    def blk(ref, b, i):
        return ref[b, i * SUBLANES:(i + 1) * SUBLANES, :]

    def step(state, b, i):
        sr, si = state
        return abr * sr - abi * si + blk(re_ref, b, i), abr * si + abi * sr + blk(im_ref, b, i)

    finals = {b: (blk(re_ref, b, order[0]), blk(im_ref, b, order[0])) for b in bs}
    for i in order[1:]:
        for b in bs:
            finals[b] = step(finals[b], b, i)

    mr, mi = ar, ai
    for _ in range(n_blk.bit_length() - 1):
        mr, mi = _cmul((mr, mi), (mr, mi))
    levels, _ = _scan_tables(mr, mi, width, reverse)
    mbr = jnp.broadcast_to(mr, shape)
    mbi = jnp.broadcast_to(mi, shape)
    row = lax.broadcasted_iota(jnp.int32, shape, 0)
    edge_in = SUBLANES - 1 if reverse else 0
    edge_out = 0 if reverse else SUBLANES - 1
    sh1 = SUBLANES - 1 if reverse else 1
    states = {}
    for b in bs:
        fr, fi = finals[b]
        gr = jnp.where(row == edge_in, jnp.broadcast_to(car_ref[b, 0:1, :], shape), pltpu.roll(fr, sh1, 0))
        gi = jnp.where(row == edge_in, jnp.broadcast_to(car_ref[b, 1:2, :], shape), pltpu.roll(fi, sh1, 0))
        for (lr, li), k in zip(levels, (1, 2, 4)):
            sh = (SUBLANES - k) if reverse else k
            sr = pltpu.roll(gr, sh, 0)
            si = pltpu.roll(gi, sh, 0)
            gr, gi = gr + (lr * sr - li * si), gi + (lr * si + li * sr)
        car_ref[b, 0:1, :] = (fr + (mbr * gr - mbi * gi))[edge_out:edge_out + 1, :]
        car_ref[b, 1:2, :] = (fi + (mbr * gi + mbi * gr))[edge_out:edge_out + 1, :]
        states[b] = (gr, gi)

    for i in order:
        for b in bs:
            states[b] = step(states[b], b, i)
            re_ref[b, i * SUBLANES:(i + 1) * SUBLANES, :] = states[b][0]
            im_ref[b, i * SUBLANES:(i + 1) * SUBLANES, :] = states[b][1]
            if on_block is not None:
                on_block(b, i, *states[b])


def _ssm_fwd(u, bb_re, bb_im, c_re_t, c_imn_t, d_row, ab_re, ab_im, w_out_b, w_glu_b, conv_p, n_seq, seq):
    tt = min(SCAN_TILE, seq)
    nt = seq // tt

    def body(u_ref, bbre, bbim, cre, cimn, d_ref, are, aim, wout_ref, wglu_ref, cw_ref,
             sre_ref, sim_ref, y_ref, oout_ref, oglu_ref, ocw_ref,
             up_ref, car_ref, send_sems, recv_sems, loc_sems):
        j = pl.program_id(0)
        t = pl.program_id(1)
        gather = _TwoLevelGather(
            [wout_ref, wglu_ref, cw_ref],
            [lambda dev: oout_ref.at[pl.ds(pl.multiple_of(dev * OUT_ROWS_PER_DEV, OUT_ROWS_PER_DEV), OUT_ROWS_PER_DEV), :],
             lambda dev: oglu_ref.at[pl.ds(pl.multiple_of(dev * GLU_ROWS_PER_DEV, GLU_ROWS_PER_DEV), GLU_ROWS_PER_DEV), :],
             lambda dev: ocw_ref.at[dev]],
            send_sems, recv_sems, loc_sems)

        @pl.when((j == 0) & (t == 0))
        def _():
            gather.start()

        @pl.when((j == N_JBLK // 2) & (t == 0))
        def _():
            gather.neighbours_landed()

        @pl.when((j == N_JBLK - 1) & (t == 0))
        def _():
            gather.diagonal_landed()

        @pl.when(t == 0)
        def _():
            car_ref[...] = jnp.zeros_like(car_ref)

        bs = list(range(n_seq))
        for b in bs:
            _load_chunked(u_ref, b, up_ref, tt)
        for b in bs:
            ub = up_ref[b].astype(BF16)
            sre_ref[b] = _dot(ub, bbre[0])
            sim_ref[b] = _dot(ub, bbim[0])
            _chunk_scan(sre_ref, sim_ref, [b], car_ref, are[0], aim[0], tt, reverse=False)
        for b in bs:
            yp = (_dot_nt(sre_ref[b].astype(BF16), cre[0]) + _dot_nt(sim_ref[b].astype(BF16), cimn[0])
                  + d_ref[...] * up_ref[b])
            _store_chunked(yp, y_ref, b, tt)

        @pl.when((j == N_JBLK - 1) & (t == nt - 1))
        def _():
            gather.finish()

    tok = lambda j, t: (0, t, j)
    blk3 = lambda j, t: (j, 0, 0)
    row = lambda j, t: (0, j)
    st = _out((n_seq, seq, N_JBLK * JB_ST), F32)
    n_arr = 3
    return _pcall(
        body, name="ssm_fwd", grid=(N_JBLK, nt),
        out_shape=(st, st, _out((n_seq, seq, SSM_W), F32),
                   _out((D_MODEL, D_MODEL), BF16), _out((SSM_W, SSM_W), BF16),
                   _out((N_DEV, SUBLANES, LANES), F32)),
        in_specs=[pl.BlockSpec((n_seq, tt, JB_CH), tok),
                  pl.BlockSpec((1, JB_CH, JB_ST), blk3), pl.BlockSpec((1, JB_CH, JB_ST), blk3),
                  pl.BlockSpec((1, JB_CH, JB_ST), blk3), pl.BlockSpec((1, JB_CH, JB_ST), blk3),
                  pl.BlockSpec((1, JB_CH), row), pl.BlockSpec((1, 1, JB_ST), blk3), pl.BlockSpec((1, 1, JB_ST), blk3),
                  HBM_SPEC, HBM_SPEC, HBM_SPEC],
        out_specs=(pl.BlockSpec((n_seq, tt, JB_ST), tok), pl.BlockSpec((n_seq, tt, JB_ST), tok),
                   pl.BlockSpec((n_seq, tt, JB_CH), tok), HBM_SPEC, HBM_SPEC, HBM_SPEC),
        scratch_shapes=[pltpu.VMEM((n_seq, tt, JB_CH), F32), pltpu.VMEM((n_seq, SUBLANES, JB_ST), F32),
                        pltpu.SemaphoreType.DMA((7 * n_arr,)), pltpu.SemaphoreType.DMA((7 * n_arr,)),
                        pltpu.SemaphoreType.DMA((n_arr,))],
        compiler_params=_params(2),
    )(u, bb_re, bb_im, c_re_t, c_imn_t, d_row, ab_re, ab_im, w_out_b, w_glu_b, conv_p)


def _ssm_bwd(dy, u, s_re, s_im, bb_re, bb_im, c_re_t, c_imn_t, d_row, ab_re, ab_im, g_out, g_glu, n_seq, seq):
    tt = min(SCAN_TILE, seq)
    nt = seq // tt
    rows8 = tt // SUBLANES

    def body(dy_ref, u_ref, sre_ref, sim_ref, pre_ref, pim_ref, bbre, bbim, cre, cimn, d_ref, are, aim,
             gout_ref, gglu_ref,
             du_ref, dcre_ref, dcim_ref, dbbre_ref, dbbim_ref, dare_ref, daim_ref, dd_ref, rout_ref, rglu_ref,
             lre_ref, lim_ref, dyp_ref, up_ref, car_ref, send_sems, recv_sems, loc_sems):
        j = pl.program_id(0)
        tr = pl.program_id(1)

        def exchange():
            return _direct_copies(lambda pid: [gout_ref.at[pid], gglu_ref.at[pid]], [rout_ref, rglu_ref],
                                  send_sems, recv_sems, loc_sems)

        @pl.when((j == 0) & (tr == 0))
        def _():
            mine, sends = exchange()
            for cp in mine + sends:
                cp.start()

        @pl.when(tr == 0)
        def _():
            car_ref[...] = jnp.zeros_like(car_ref)
            for r in (dcre_ref, dcim_ref, dbbre_ref, dbbim_ref, dare_ref, daim_ref, dd_ref):
                r[...] = jnp.zeros_like(r)

        first = tr == nt - 1
        row = lax.broadcasted_iota(jnp.int32, (SUBLANES, JB_ST), 0)
        n_blk = tt // SUBLANES
        bs = list(range(n_seq))
        for b in bs:
            _load_chunked(dy_ref, b, dyp_ref, tt)
            _load_chunked(u_ref, b, up_ref, tt)
        for b in bs:
            dyb = dyp_ref[b].astype(BF16)
            lre_ref[b] = _dot(dyb, cre[0])
            lim_ref[b] = _dot(dyb, cimn[0])
        acc = {b: [jnp.zeros((SUBLANES, JB_ST), F32), jnp.zeros((SUBLANES, JB_ST), F32)] for b in bs}

        def on_block(b, i, lr, li):
            if i > 0:
                spr = sre_ref[b, (i - 1) * SUBLANES:i * SUBLANES, :]
                spi = sim_ref[b, (i - 1) * SUBLANES:i * SUBLANES, :]
            else:
                hr = jnp.where(first, 0.0, pre_ref[b, SUBLANES - 1:SUBLANES, :])
                hi = jnp.where(first, 0.0, pim_ref[b, SUBLANES - 1:SUBLANES, :])
                last_r = sre_ref[b, (n_blk - 1) * SUBLANES:n_blk * SUBLANES, :]
                last_i = sim_ref[b, (n_blk - 1) * SUBLANES:n_blk * SUBLANES, :]
                spr = jnp.where(row == 0, jnp.broadcast_to(hr, row.shape), pltpu.roll(last_r, 1, 0))
                spi = jnp.where(row == 0, jnp.broadcast_to(hi, row.shape), pltpu.roll(last_i, 1, 0))
            acc[b][0] = acc[b][0] + (lr * spr + li * spi)
            acc[b][1] = acc[b][1] + (li * spr - lr * spi)

        _chunk_scan(lre_ref, lim_ref, bs, car_ref, are[0], -aim[0], tt, reverse=True, on_block=on_block)
        for b in bs:
            dare_ref[...] += jnp.sum(acc[b][0], axis=0, keepdims=True)
            daim_ref[...] += jnp.sum(acc[b][1], axis=0, keepdims=True)
            dyp = dyp_ref[b]
            up = up_ref[b]
            dyb = dyp.astype(BF16)
            ub = up.astype(BF16)
            lrb = lre_ref[b].astype(BF16)
            lib = lim_ref[b].astype(BF16)
            dup = d_ref[...] * dyp + _dot_nt(lrb, bbre[0]) + _dot_nt(lib, bbim[0])
            _store_chunked(dup, du_ref, b, tt)
            dbbre_ref[0] += _dot_tn(ub, lrb)
            dbbim_ref[0] += _dot_tn(ub, lib)
            dcre_ref[0] += _dot_tn(dyb, sre_ref[b].astype(BF16))
            dcim_ref[0] += _dot_tn(dyb, sim_ref[b].astype(BF16))
            dd_ref[...] += jnp.sum(dyp * up, axis=0, keepdims=True)

        @pl.when((j == N_JBLK - 1) & (tr == nt - 1))
        def _():
            mine, sends = exchange()
            for cp in sends + mine:
                cp.wait()

    tok = lambda j, t: (0, nt - 1 - t, j)
    halo = lambda j, t: (0, jnp.maximum((nt - 1 - t) * rows8 - 1, 0), j)
    blk3 = lambda j, t: (j, 0, 0)
    row1 = lambda j, t: (0, j)
    acc_shape = _out((N_JBLK, JB_CH, JB_ST), F32)
    return _pcall(
        body, name="ssm_bwd", grid=(N_JBLK, nt),
        out_shape=(_out((n_seq, seq, SSM_W), F32), acc_shape, acc_shape, acc_shape, acc_shape,
                   _out((1, N_JBLK * JB_ST), F32), _out((1, N_JBLK * JB_ST), F32),
                   _out((1, SSM_W), F32),
                   _out((N_DEV,) + g_out.shape[1:], F32),
                   _out((N_DEV,) + g_glu.shape[1:], F32)),
        in_specs=[pl.BlockSpec((n_seq, tt, JB_CH), tok), pl.BlockSpec((n_seq, tt, JB_CH), tok),
                  pl.BlockSpec((n_seq, tt, JB_ST), tok), pl.BlockSpec((n_seq, tt, JB_ST), tok),
                  pl.BlockSpec((n_seq, SUBLANES, JB_ST), halo), pl.BlockSpec((n_seq, SUBLANES, JB_ST), halo),
                  pl.BlockSpec((1, JB_CH, JB_ST), blk3), pl.BlockSpec((1, JB_CH, JB_ST), blk3),
                  pl.BlockSpec((1, JB_CH, JB_ST), blk3), pl.BlockSpec((1, JB_CH, JB_ST), blk3),
                  pl.BlockSpec((1, JB_CH), row1), pl.BlockSpec((1, 1, JB_ST), blk3), pl.BlockSpec((1, 1, JB_ST), blk3),
                  HBM_SPEC, HBM_SPEC],
        out_specs=(pl.BlockSpec((n_seq, tt, JB_CH), tok),
                   pl.BlockSpec((1, JB_CH, JB_ST), blk3), pl.BlockSpec((1, JB_CH, JB_ST), blk3),
                   pl.BlockSpec((1, JB_CH, JB_ST), blk3), pl.BlockSpec((1, JB_CH, JB_ST), blk3),
                   pl.BlockSpec((1, JB_ST), row1), pl.BlockSpec((1, JB_ST), row1), pl.BlockSpec((1, JB_CH), row1),
                   HBM_SPEC, HBM_SPEC),
        scratch_shapes=[pltpu.VMEM((n_seq, tt, JB_ST), F32), pltpu.VMEM((n_seq, tt, JB_ST), F32),
                        pltpu.VMEM((n_seq, tt, JB_CH), F32), pltpu.VMEM((n_seq, tt, JB_CH), F32),
                        pltpu.VMEM((n_seq, SUBLANES, JB_ST), F32),
                        pltpu.SemaphoreType.DMA((7 * 2,)), pltpu.SemaphoreType.DMA((7 * 2,)),
                        pltpu.SemaphoreType.DMA((2,))],
        compiler_params=_params(2),
    )(dy, u, s_re, s_im, s_re, s_im, bb_re, bb_im, c_re_t, c_imn_t, d_row, ab_re, ab_im, g_out, g_glu)


def _mix(x2, tgt2, y, proj, gf, b_glu, conv8, w_glu_f, w_out_f, seq):
    n = x2.shape[0]
    tm = TOK_TILE
    tiles_per_seq = seq // tm
    rows8 = tm // SUBLANES

    def body(x_ref, t_ref, y_ref, zs_ref, h_ref, bc_ref, cc_ref, zc_ref, hp_ref, ccp_ref,
             gf_ref, bg_ref, cw_ref, wg_ref, wo_ref,
             dh2_ref, dy_ref, dzs_ref, dbc_ref, dzc_ref, dyc_ref,
             dwo_ref, dwg_ref, loss_ref, dgf_ref, dbg_ref, dcw_ref):
        i = pl.program_id(0)

        @pl.when(i == 0)
        def _():
            for r in (dwo_ref, dwg_ref, loss_ref, dgf_ref, dbg_ref, dcw_ref):
                r[...] = jnp.zeros_like(r)

        yv = y_ref[...]
        y1, dgelu = _gelu_and_grad(yv)
        y1b = y1.astype(BF16)
        gate = _sigmoid(_dot(y1b, wg_ref[...]) + bg_ref[...])
        y2 = y1 * gate
        szs, dszs = _silu_and_grad(zs_ref[...])
        yssm = y2 * szs
        hv = h_ref[...]
        ccv = cc_ref[...]
        bcv = bc_ref[...]
        v = ccv * hv
        first = (i % tiles_per_seq) == 0
        vhalo = jnp.where(first, 0.0, ccp_ref[...] * hp_ref[...])
        v1 = _shift_down(v, vhalo, 1)
        v2 = _shift_down(v, vhalo, 2)
        w0 = cw_ref[0:1, :]
        w1 = cw_ref[1:2, :]
        w2 = cw_ref[2:3, :]
        yc = w0 * v2 + w1 * v1 + w2 * v
        szc, dszc = _silu_and_grad(zc_ref[...])
        yconv = (bcv * yc) * szc
        ysb = yssm.astype(BF16)
        ycb = yconv.astype(BF16)
        h2 = x_ref[...] + _dot(ysb, wo_ref[0:SSM_W, :]) + _dot(ycb, wo_ref[SSM_W:, :])
        r2 = lax.rsqrt(jnp.mean(h2 * h2, axis=-1, keepdims=True) + EPS)
        hn = h2 * r2
        gfv = gf_ref[...]
        err = hn * gfv - t_ref[...]
        loss_ref[...] += 0.5 * jnp.sum(jnp.mean(err * err, axis=-1, keepdims=True))
        dout = err * (1.0 / D_MODEL)
        dgf_ref[...] += jnp.sum(dout * hn, axis=0, keepdims=True)
        dn = dout * gfv
        dh2 = r2 * (dn - hn * jnp.mean(dn * hn, axis=-1, keepdims=True))
        dh2_ref[...] = dh2
        dh2b = dh2.astype(BF16)
        dwo_ref[0:SSM_W, :] += _dot_tn(ysb, dh2b)
        dwo_ref[SSM_W:, :] += _dot_tn(ycb, dh2b)
        dyssm = _dot_nt(dh2b, wo_ref[0:SSM_W, :])
        dyconv = _dot_nt(dh2b, wo_ref[SSM_W:, :])
        dy2 = dyssm * szs
        dzs_ref[...] = (dyssm * y2 * dszs).astype(BF16)
        dgp = dy2 * y1 * (gate * (1.0 - gate))
        dgpb = dgp.astype(BF16)
        dy1 = dy2 * gate + _dot_nt(dgpb, wg_ref[...])
        dwg_ref[...] += _dot_tn(y1b, dgpb)
        dbg_ref[...] += jnp.sum(dgp, axis=0, keepdims=True)
        dy_ref[...] = dy1 * dgelu
        dbc_ref[...] = (dyconv * yc * szc).astype(BF16)
        dyc = dyconv * bcv * szc
        dyc_ref[...] = dyc
        dzc_ref[...] = (dyconv * bcv * yc * dszc).astype(BF16)
        dcw_ref[0:1, :] += jnp.sum(dyc * v2, axis=0, keepdims=True)
        dcw_ref[1:2, :] += jnp.sum(dyc * v1, axis=0, keepdims=True)
        dcw_ref[2:3, :] += jnp.sum(dyc * v, axis=0, keepdims=True)

    tile_d = pl.BlockSpec((tm, D_MODEL), lambda i: (i, 0))
    tile_s = pl.BlockSpec((tm, SSM_W), lambda i: (i, 0))
    seg_of = lambda c: pl.BlockSpec((tm, SSM_W), lambda i: (i, c))
    halo_of = lambda c: pl.BlockSpec((SUBLANES, SSM_W), lambda i: (jnp.maximum(i * rows8 - 1, 0), c))
    const = lambda shape: pl.BlockSpec(shape, lambda i: (0,) * len(shape))
    seg = _out((n, SSM_W), F32)
    seg_b = _out((n, SSM_W), BF16)
    return _pcall(
        body, name="mix", grid=(n // tm,),
        out_shape=(_out((n, D_MODEL), F32), seg, seg_b, seg_b, seg_b, seg,
                   _out((D_MODEL, D_MODEL), F32), _out((SSM_W, SSM_W), F32),
                   _out((SUBLANES, LANES), F32), _out((1, D_MODEL), F32),
                   _out((1, SSM_W), F32), _out((SUBLANES, CONV_W), F32)),
        in_specs=[tile_d, tile_d, tile_s, seg_of(SEG_ZS), seg_of(SEG_H), seg_of(SEG_BC), seg_of(SEG_CC), seg_of(SEG_ZC),
                  halo_of(SEG_H), halo_of(SEG_CC),
                  const((1, D_MODEL)), const((1, SSM_W)), const((SUBLANES, CONV_W)),
                  const((SSM_W, SSM_W)), const((D_MODEL, D_MODEL))],
        out_specs=(tile_d, tile_s, tile_s, tile_s, tile_s, tile_s,
                   const((D_MODEL, D_MODEL)), const((SSM_W, SSM_W)), const((SUBLANES, LANES)),
                   const((1, D_MODEL)), const((1, SSM_W)), const((SUBLANES, CONV_W))),
        compiler_params=_params(1),
    )(x2, tgt2, y, proj, proj, proj, proj, proj, proj, proj, gf, b_glu, conv8, w_glu_f, w_out_f)


def _in_bwd(x2, dh2, du, dzs, dyc, proj, dbc, dzc, g1, conv8, w_full, seq):
    n = x2.shape[0]
    tm = TOK_TILE
    n_tiles = n // tm
    tiles_per_seq = seq // tm
    rows8 = tm // SUBLANES
    n_blk8 = n // SUBLANES

    def body(x_ref, dh2_ref, du_ref, dzs_ref, dyc_ref, dycn_ref, h_ref, cc_ref, dbc_ref, dzc_ref,
             g_ref, cw_ref, w_ref, gx_ref, dp_ref, dg_ref):
        i = pl.program_id(0)

        @pl.when(i == 0)
        def _():
            dg_ref[...] = jnp.zeros_like(dg_ref)

        dyc = dyc_ref[...]
        last = (i % tiles_per_seq) == tiles_per_seq - 1
        nhalo = jnp.where(last, 0.0, dycn_ref[...])
        dv = (cw_ref[2:3, :] * dyc + cw_ref[1:2, :] * _shift_up(dyc, nhalo, 1)
              + cw_ref[0:1, :] * _shift_up(dyc, nhalo, 2))
        parts = (du_ref[...], dzs_ref[...], dv * cc_ref[...], dbc_ref[...], dv * h_ref[...], dzc_ref[...])
        dxn = jnp.zeros((tm, D_MODEL), F32)
        for k, p in enumerate(parts):
            pb = p.astype(BF16)
            dp_ref[:, k * SSM_W:(k + 1) * SSM_W] = pb
            dxn = dxn + _dot_nt(pb, w_ref[:, k * SSM_W:(k + 1) * SSM_W])
        x = x_ref[...]
        r = lax.rsqrt(jnp.mean(x * x, axis=-1, keepdims=True) + EPS)
        xh = x * r
        dg_ref[...] += jnp.sum(dxn * xh, axis=0, keepdims=True)
        dn = dxn * g_ref[...]
        gx_ref[...] = dh2_ref[...] + r * (dn - xh * jnp.mean(dn * xh, axis=-1, keepdims=True))

    tile_d = pl.BlockSpec((tm, D_MODEL), lambda i: (i, 0))
    tile_s = pl.BlockSpec((tm, SSM_W), lambda i: (i, 0))
    seg_of = lambda c: pl.BlockSpec((tm, SSM_W), lambda i: (i, c))
    nhalo = pl.BlockSpec((SUBLANES, SSM_W), lambda i: (jnp.minimum((i + 1) * rows8, n_blk8 - 1), 0))
    const = lambda shape: pl.BlockSpec(shape, lambda i: (0,) * len(shape))
    return _pcall(
        body, name="in_bwd", grid=(n_tiles,),
        out_shape=(_out((n, D_MODEL), F32), _out((n, IN_COLS), BF16),
                   _out((SUBLANES, D_MODEL), F32)),
        in_specs=[tile_d, tile_d, tile_s, tile_s, tile_s, nhalo, seg_of(SEG_H), seg_of(SEG_CC), tile_s, tile_s,
                  const((1, D_MODEL)), const((SUBLANES, CONV_W)), const((D_MODEL, IN_COLS))],
        out_specs=(tile_d, pl.BlockSpec((tm, IN_COLS), lambda i: (i, 0)), const((SUBLANES, D_MODEL))),
        compiler_params=_params(1),
    )(x2, dh2, du, dzs, dyc, dyc, proj, proj, dbc, dzc, g1, conv8, w_full)


def _dw_in_exchange(order, xn, dproj, smalls):
    n = xn.shape[0]
    tk = 512
    nk = n // tk
    piece = (D_MODEL, COLS_PER_DEV)
    n_small = len(smalls)

    def body(order_ref, xn_hbm, dp_ref, *refs):
        del order_ref
        sm_refs = refs[:n_small]
        own_ref, rchip_ref = refs[n_small:n_small + 2]
        rsm_refs = refs[n_small + 2:2 * n_small + 2]
        (xn_ref, acc, stage, sbuf, relay_in, xn_sems, give_send, give_recv, keep_send, keep_recv,
         relay_send, relay_recv, sm_send, sm_recv, sm_loc) = refs[2 * n_small + 2:]
        s = pl.program_id(0)

        def xn_copy(kk):
            rows = pl.ds(pl.multiple_of(kk * tk, tk), tk)
            return pltpu.make_async_copy(xn_hbm.at[rows, :], xn_ref.at[rows, :], xn_sems.at[kk])

        @pl.when(s == 0)
        def _():
            for kk in range(nk):
                xn_copy(kk).start()
            xn_copy(0).wait()

        x, y, c = _mesh_pos()
        sib = (x, y, 1 - c)
        y_nbr, x_nbr = (x, 1 - y, c), (1 - x, y, c)
        half_rows = D_MODEL // 2
        gather = _TwoLevelGather(list(sm_refs), [functools.partial(lambda r, dev: r.at[dev], r) for r in rsm_refs],
                                 sm_send, sm_recv, sm_loc)

        def half(i, core):
            return acc.at[i % 2, :, pl.ds(pl.multiple_of(core * COLS_PER_DEV, LANES), COLS_PER_DEV)]

        def give(i):
            return pltpu.make_async_remote_copy(src_ref=half(i, 1 - c), dst_ref=stage.at[i], send_sem=give_send.at[i],
                                                recv_sem=give_recv.at[i], device_id=sib, device_id_type=MESH)

        def relay(r):
            rows = pl.ds(r * half_rows, half_rows)
            return pltpu.make_async_remote_copy(src_ref=sbuf.at[0, rows, :], dst_ref=relay_in.at[r],
                                                send_sem=relay_send.at[r], recv_sem=relay_recv.at[r],
                                                device_id=(x_nbr, y_nbr)[r], device_id_type=MESH)

        def keep(i):
            return pltpu.make_async_remote_copy(src_ref=sbuf.at[i], dst_ref=rchip_ref.at[i - 1],
                                                send_sem=keep_send.at[i - 1], recv_sem=keep_recv.at[i - 1],
                                                device_id=(None, y_nbr, x_nbr)[i], device_id_type=MESH)

        def chip_sum(i):
            give(i).wait_recv()
            mine = [acc[i % 2, :, cc * COLS_PER_DEV:(cc + 1) * COLS_PER_DEV] for cc in range(2)]
            return jnp.where(c == 0, mine[0], mine[1]) + stage[i]

        @pl.when(s == 0)
        def _():
            gather.start()

        @pl.when(s == 1)
        def _():
            gather.neighbours_landed()

        @pl.when(s == N_CHIP - 1)
        def _():
            gather.diagonal_landed()

        for k in range(2, N_CHIP):
            @pl.when(s == k)
            def _(k=k):
                give(k - 2).wait_send()

        slot = s % 2
        acc[slot] = _dot_tn(xn_ref[pl.ds(0, tk), :], dp_ref[pl.ds(0, tk), :])

        def kstep(kk, carry):
            @pl.when(s == 0)
            def _():
                xn_copy(kk).wait()

            off = pl.multiple_of(kk * tk, tk)
            acc[slot] += _dot_tn(xn_ref[pl.ds(off, tk), :], dp_ref[pl.ds(off, tk), :])
            return carry

        n_first = min(nk, 3)
        lax.fori_loop(1, n_first, kstep, 0)
        @pl.when(s == 1)
        def _():
            sbuf[0] = chip_sum(0).astype(BF16)
            relay(0).start()
            relay(1).start()

        for k in (2, 3):
            @pl.when(s == k)
            def _(k=k):
                i = k - 1
                r = i - 1
                total = chip_sum(i)
                relay(r).wait_recv()
                rows = slice(r * half_rows, (r + 1) * half_rows)
                other = slice((1 - r) * half_rows, (2 - r) * half_rows)
                sbuf[i, rows, :] = (total[rows, :] + relay_in[r].astype(F32)).astype(BF16)
                sbuf[i, other, :] = total[other, :].astype(BF16)
                keep(i).start()

        lax.fori_loop(n_first, nk, kstep, 0)

        for k in range(N_CHIP):
            @pl.when(s == k)
            def _(k=k):
                give(k).start()

        @pl.when(s == N_CHIP - 1)
        def _():
            own_ref[...] = chip_sum(N_CHIP - 1)
            give(N_CHIP - 2).wait_send()
            give(N_CHIP - 1).wait_send()
            for r in range(2):
                relay(r).wait_send()
            for i in (1, 2):
                keep(i).wait()
            gather.finish()

    grid_spec = pltpu.PrefetchScalarGridSpec(
        num_scalar_prefetch=1, grid=(N_CHIP,),
        in_specs=[HBM_SPEC,
                  pl.BlockSpec((n, COLS_PER_CHIP), lambda s, order: (0, order[s])),
                  *([HBM_SPEC] * n_small)],
        out_specs=(pl.BlockSpec(piece, lambda s, order: (0, 0)), HBM_SPEC, *([HBM_SPEC] * n_small)),
        scratch_shapes=[pltpu.VMEM((n, D_MODEL), BF16),
                        pltpu.VMEM((2, D_MODEL, COLS_PER_CHIP), F32), pltpu.VMEM((4,) + piece, F32),
                        pltpu.VMEM((3,) + piece, BF16), pltpu.VMEM((2, D_MODEL // 2, COLS_PER_DEV), BF16),
                        pltpu.SemaphoreType.DMA((nk,)),
                        pltpu.SemaphoreType.DMA((4,)), pltpu.SemaphoreType.DMA((4,)),
                        pltpu.SemaphoreType.DMA((2,)), pltpu.SemaphoreType.DMA((2,)),
                        pltpu.SemaphoreType.DMA((2,)), pltpu.SemaphoreType.DMA((2,)),
                        pltpu.SemaphoreType.DMA((7 * n_small,)), pltpu.SemaphoreType.DMA((7 * n_small,)),
                        pltpu.SemaphoreType.DMA((n_small,))])
    return _pcall(
        body, name="dw_in_exchange", grid_spec=grid_spec,
        out_shape=(_out(piece, F32), _out((2,) + piece, BF16),
                   *(_out((N_DEV,) + a.shape, a.dtype) for a in smalls)),
        compiler_params=_params(1),
    )(order, xn, dproj, *smalls)


def _adamw(g, w, m, v):
    m_new = ADAM_B1 * m + (1.0 - ADAM_B1) * g
    v_new = ADAM_B2 * v + (1.0 - ADAM_B2) * (g * g)
    m_hat = m_new / (1.0 - ADAM_B1 ** ADAM_STEP)
    v_hat = v_new / (1.0 - ADAM_B2 ** ADAM_STEP)
    delta = -ADAM_LR * (m_hat / (jnp.sqrt(v_hat) + ADAM_EPS) + ADAM_WD * w)
    return delta, m_new, v_new


def _reduce_adam_w_in(own, rchip, w, m, v):
    rows, cols = w.shape
    row_tile = 256

    def body(o_ref, r_ref, w_ref, m_ref, v_ref, g_ref, d_ref, nm_ref, nv_ref):
        g = o_ref[...]
        for s in range(2):
            g = g + r_ref[s].astype(F32)
        g_ref[...] = g
        d_ref[...], nm_ref[...], nv_ref[...] = _adamw(g, w_ref[...], m_ref[...], v_ref[...])

    tile = pl.BlockSpec((row_tile, cols), lambda i: (i, 0))
    shp = _out((rows, cols), F32)
    return _pcall(
        body, name="reduce_adam_w_in", grid=(rows // row_tile,),
        out_shape=(shp,) * 4,
        in_specs=[tile, pl.BlockSpec((2, row_tile, cols), lambda i: (0, i, 0)), tile, tile, tile],
        out_specs=(tile,) * 4,
        compiler_params=_params(1),
    )(own, rchip, w, m, v)


_SMALL_LEAVES = ("norm_gain", "final_norm_gain", "b_glu", "ssm_a_re", "ssm_a_im", "ssm_log_dt", "ssm_d", "conv_w",
                 "ssm_c_re", "ssm_c_im", "ssm_b_re", "ssm_b_im")


def _reduce_adam_small(r_pack, r_gc, r_gb, wmv, sharded):
    n_leaf = len(_SMALL_LEAVES)
    n_sh = len(sharded)

    def body(*refs):
        rp_ref, rgc_ref, rgb_ref = refs[:3]
        w_refs = refs[3:3 + 3 * n_leaf]
        sh_in = refs[3 + 3 * n_leaf:3 + 3 * n_leaf + 4 * n_sh]
        outs0 = 3 + 3 * n_leaf + 4 * n_sh
        loss_ref = refs[outs0]
        o_refs = refs[outs0 + 1:outs0 + 1 + 4 * n_leaf]
        sh_out = refs[outs0 + 1 + 4 * n_leaf:outs0 + 1 + 4 * n_leaf + 4 * n_sh]
        own_conv = refs[-1]

        def total(ref):
            acc = ref[0].astype(F32)
            for s in range(1, N_DEV):
                acc = acc + ref[s].astype(F32)
            return acc

        for i in range(n_sh):
            r_ref, w_ref, m_ref, v_ref = sh_in[4 * i:4 * i + 4]
            o_g, o_d, o_m, o_v = sh_out[4 * i:4 * i + 4]
            g = total(r_ref)
            o_g[...] = g
            o_d[...], o_m[...], o_v[...] = _adamw(g, w_ref[...], m_ref[...], v_ref[...])

        sp = total(rp_ref)
        sgc = total(rgc_ref)
        sgb = total(rgb_ref)
        loss_ref[...] = sp[ROW_LOSS:ROW_LOSS + SUBLANES, 0:LANES]

        def wide(r):
            return jnp.concatenate([sp[r:r + 1, :], sp[r + 1:r + 2, :]], axis=1)

        s5 = slice(ROW_S5, ROW_S5 + N_GROUPS)
        eye = (lax.broadcasted_iota(jnp.int32, (N_GROUPS, N_GROUPS), 0)
               == lax.broadcasted_iota(jnp.int32, (N_GROUPS, N_GROUPS), 1)).astype(F32)
        d_row = sp[ROW_BGLU_D + 1:ROW_BGLU_D + 2, :]
        me = 4 * lax.axis_index("x") + 2 * lax.axis_index("y") + lax.axis_index("c")
        for k in range(N_DEV):
            @pl.when(me == k)
            def _(k=k):
                own_conv[...] = sp[ROW_CONV:ROW_CONV + SUBLANES, k * CONV_COLS_PER_DEV:(k + 1) * CONV_COLS_PER_DEV]
        grads = {
            "norm_gain": wide(ROW_NORM_GAIN),
            "final_norm_gain": wide(ROW_FINAL_GAIN),
            "b_glu": sp[ROW_BGLU_D:ROW_BGLU_D + 1, :],
            "ssm_a_re": sp[s5, LANE_A_RE:LANE_A_RE + STATE],
            "ssm_a_im": sp[s5, LANE_A_IM:LANE_A_IM + STATE],
            "ssm_log_dt": jnp.sum(sp[s5, LANE_LOG_DT:LANE_LOG_DT + 1] * eye, axis=0, keepdims=True),
            "ssm_d": jnp.concatenate([d_row[:, g * GROUP:(g + 1) * GROUP] for g in range(N_GROUPS)], axis=0),
            "conv_w": own_conv[0:3, :],
            "ssm_c_re": sgc[:, 0:STATE],
            "ssm_c_im": sgc[:, STATE:2 * STATE],
            "ssm_b_re": sgb[:, 0:STATE],
            "ssm_b_im": sgb[:, STATE:2 * STATE],
        }
        for i, name in enumerate(_SMALL_LEAVES):
            g = grads[name]
            w_ref, m_ref, v_ref = w_refs[3 * i:3 * i + 3]
            o_g, o_d, o_m, o_v = o_refs[4 * i:4 * i + 4]
            o_g[...] = g
            o_d[...], o_m[...], o_v[...] = _adamw(g, w_ref[...], m_ref[...], v_ref[...])

    flat_w = [a for name in _SMALL_LEAVES for a in wmv[name]]
    leaf_shapes = [_out(wmv[name][0].shape, F32) for name in _SMALL_LEAVES for _ in range(4)]
    sh_shapes = [_out(entry[1].shape, F32) for entry in sharded for _ in range(4)]
    operands = (r_pack, r_gc, r_gb, *flat_w, *(a for entry in sharded for a in entry))
    out_shape = (_out((SUBLANES, LANES), F32), *leaf_shapes, *sh_shapes)
    outs = _pcall(
        body, name="reduce_adam_small", grid=(1,), out_shape=out_shape,
        in_specs=_whole_specs(operands), out_specs=tuple(_whole_specs(out_shape)),
        scratch_shapes=[pltpu.VMEM((SUBLANES, CONV_COLS_PER_DEV), F32)],
        compiler_params=_params(1),
    )(*operands)
    leaves = {name: outs[1 + 4 * i:5 + 4 * i] for i, name in enumerate(_SMALL_LEAVES)}
    first = 1 + 4 * n_leaf
    return outs[0], leaves, [outs[first + 4 * i:first + 4 * i + 4] for i in range(n_sh)]


def kernel(x, norm_gain, w_in, ssm_a_re, ssm_a_im, ssm_log_dt, ssm_b_re, ssm_b_im, ssm_c_re, ssm_c_im, ssm_d, w_glu, b_glu, conv_w, w_out, final_norm_gain, loss_target, m_norm_gain, m_w_in, m_ssm_a_re, m_ssm_a_im, m_ssm_log_dt, m_ssm_b_re, m_ssm_b_im, m_ssm_c_re, m_ssm_c_im, m_ssm_d, m_w_glu, m_b_glu, m_conv_w, m_w_out, m_final_norm_gain, v_norm_gain, v_w_in, v_ssm_a_re, v_ssm_a_im, v_ssm_log_dt, v_ssm_b_re, v_ssm_b_im, v_ssm_c_re, v_ssm_c_im, v_ssm_d, v_w_glu, v_b_glu, v_conv_w, v_w_out, v_final_norm_gain):
    n_seq, seq, _ = x.shape
    n = n_seq * seq

    gh_p = lambda b4: jnp.transpose(b4, (0, 1, 3, 2)).reshape(N_GROUPS * GROUP, STATE)
    c2 = lambda a: a.reshape(N_GROUPS * GROUP, STATE)
    b_re2, b_im2 = gh_p(ssm_b_re), gh_p(ssm_b_im)
    d_row = ssm_d[0].reshape(1, SSM_W)

    x2 = x.reshape(n, D_MODEL)
    tgt2 = loss_target.reshape(n, D_MODEL)
    mx, my, mc = lax.axis_index("x"), lax.axis_index("y"), lax.axis_index("c")
    chip_ids = [2 * cx + cy for cx, cy in ((mx, my), (1 - mx, my), (mx, 1 - my), (1 - mx, 1 - my))]
    arrival = chip_ids
    xn, proj, w_in_f, s5 = _in_proj(
        jnp.stack(arrival).astype(jnp.int32), x2, norm_gain, w_in[0].astype(BF16),
        (ssm_a_re[0], ssm_a_im[0], ssm_log_dt, b_re2, b_im2, c2(ssm_c_re), c2(ssm_c_im)))
    a_re_x, a_im_x, log_dt_x, ab_re, ab_im, bb_re_m, bb_im_m, c_re_m, c_imn_m = s5
    u3 = proj.reshape(n_seq, seq, IN_COLS)
    conv_p = jnp.pad(conv_w[0], ((0, SUBLANES - 3), (0, LANES - CONV_COLS_PER_DEV)))
    s_re, s_im, y3, w_out_f, w_glu_f, conv_all = _ssm_fwd(
        u3, bb_re_m, bb_im_m, c_re_m, c_imn_m, d_row, ab_re, ab_im,
        w_out[0].astype(BF16), w_glu[0].astype(BF16), conv_p, n_seq, seq)
    conv8 = jnp.transpose(conv_all[:, :, :CONV_COLS_PER_DEV], (1, 0, 2)).reshape(SUBLANES, CONV_W)
    (dh2, dy, dzs, dbc, dzc, dyc, dw_out, dw_glu, loss_t, dgf, dbg, dcw) = _mix(
        x2, tgt2, y3.reshape(n, SSM_W), proj, final_norm_gain.reshape(1, D_MODEL), b_glu, conv8,
        w_glu_f, w_out_f, seq)

    du3, dc_re_d, dc_im_d, dbb_re_d, dbb_im_d, dab_re, dab_im, dd, r_out, r_glu = _ssm_bwd(
        dy.reshape(n_seq, seq, SSM_W), u3, s_re, s_im, bb_re_m, bb_im_m, c_re_m, c_imn_m, d_row, ab_re, ab_im,
        dw_out.reshape(N_DEV, OUT_ROWS_PER_DEV, D_MODEL), dw_glu.reshape(N_DEV, GLU_ROWS_PER_DEV, SSM_W), n_seq, seq)
    du = du3.reshape(n, SSM_W)
    grad_x2, dproj, dg8 = _in_bwd(x2, dh2, du, dzs, dyc, proj, dbc, dzc, norm_gain, conv8, w_in_f, seq)
    pack, gc, gb = _ssm_disc_bwd_pack(
        a_re_x, a_im_x, log_dt_x, b_re2, b_im2, dab_re.reshape(N_GROUPS, STATE), dab_im.reshape(N_GROUPS, STATE),
        dbb_re_d, dbb_im_d, loss_t, dg8, dgf, dbg, dd, dcw, dc_re_d, dc_im_d)

    order = [chip_ids[3], chip_ids[2], chip_ids[1], chip_ids[0]]
    own_in, rchip_in, r_pack, r_gc, r_gb = _dw_in_exchange(
        jnp.stack(order).astype(jnp.int32), xn, dproj, [pack, gc, gb])

    flat2 = lambda a: a.reshape(a.shape[-2:]) if a.ndim > 2 else a.reshape(1, -1)
    c2 = lambda a: a.reshape(N_GROUPS * GROUP, STATE)
    wmv = dict(norm_gain=(norm_gain, m_norm_gain, v_norm_gain),
               final_norm_gain=tuple(flat2(a) for a in (final_norm_gain, m_final_norm_gain, v_final_norm_gain)),
               b_glu=(b_glu, m_b_glu, v_b_glu),
               ssm_a_re=tuple(flat2(a) for a in (ssm_a_re, m_ssm_a_re, v_ssm_a_re)),
               ssm_a_im=tuple(flat2(a) for a in (ssm_a_im, m_ssm_a_im, v_ssm_a_im)),
               ssm_log_dt=(ssm_log_dt, m_ssm_log_dt, v_ssm_log_dt),
               ssm_d=tuple(flat2(a) for a in (ssm_d, m_ssm_d, v_ssm_d)),
               conv_w=tuple(flat2(a) for a in (conv_w, m_conv_w, v_conv_w)),
               ssm_c_re=tuple(c2(a) for a in (ssm_c_re, m_ssm_c_re, v_ssm_c_re)),
               ssm_c_im=tuple(c2(a) for a in (ssm_c_im, m_ssm_c_im, v_ssm_c_im)),
               ssm_b_re=(b_re2, gh_p(m_ssm_b_re), gh_p(v_ssm_b_re)),
               ssm_b_im=(b_im2, gh_p(m_ssm_b_im), gh_p(v_ssm_b_im)))

    res_in = _reduce_adam_w_in(own_in, rchip_in, w_in[0], m_w_in[0], v_w_in[0])
    loss8, small, (res_out, res_glu) = _reduce_adam_small(
        r_pack, r_gc, r_gb, wmv,
        [(r_out, w_out[0], m_w_out[0], v_w_out[0]), (r_glu, w_glu[0], m_w_glu[0], v_w_glu[0])])
    loss = loss8[0, 0]

    shapes = dict(norm_gain=(1, D_MODEL), ssm_a_re=(1, N_GROUPS, STATE), ssm_a_im=(1, N_GROUPS, STATE),
                  ssm_log_dt=(1, N_GROUPS), ssm_c_re=(1, N_GROUPS, GROUP, STATE), ssm_c_im=(1, N_GROUPS, GROUP, STATE),
                  ssm_d=(1, N_GROUPS, GROUP), b_glu=(1, SSM_W), final_norm_gain=(D_MODEL,),
                  conv_w=(1, 3, CONV_COLS_PER_DEV))
    big = dict(w_in=res_in, w_glu=res_glu, w_out=res_out)

    def leaf(kind, name):
        if name in big:
            return big[name][kind][None]
        if name in ("ssm_b_re", "ssm_b_im"):
            return jnp.transpose(small[name][kind].reshape(1, N_GROUPS, GROUP, STATE), (0, 1, 3, 2))
        return small[name][kind].reshape(shapes[name])

    order = ["norm_gain", "w_in", "ssm_a_re", "ssm_a_im", "ssm_log_dt", "ssm_b_re", "ssm_b_im", "ssm_c_re",
             "ssm_c_im", "ssm_d", "w_glu", "b_glu", "conv_w", "w_out", "final_norm_gain"]
    outs = [loss, grad_x2.reshape(x.shape)]
    for kind in range(4):
        outs += [leaf(kind, name) for name in order]
    return tuple(outs)
```

```python
import functools
import math

import jax
import jax.numpy as jnp
from jax import lax
from jax.experimental import pallas as pl
from jax.experimental.pallas import tpu as pltpu

F32 = jnp.float32
BF16 = jnp.bfloat16

N_DEV = 8
D_MODEL = 1024
SSM_W = 512
CONV_W = 512
N_GROUPS = 32
GROUP = 16
STATE = 64
IN_COLS = 3072
SEG_U, SEG_ZS, SEG_H, SEG_BC, SEG_CC, SEG_ZC = range(6)
COLS_PER_DEV = IN_COLS // N_DEV
N_CHIP = N_DEV // 2
COLS_PER_CHIP = 2 * COLS_PER_DEV
OUT_ROWS_PER_DEV = D_MODEL // N_DEV
GLU_ROWS_PER_DEV = SSM_W // N_DEV
CONV_COLS_PER_DEV = CONV_W // N_DEV
EPS = 1e-6

N_JBLK = 4
JB_CH = SSM_W // N_JBLK
JB_ST = N_GROUPS * STATE // N_JBLK

ADAM_LR = 0.001
ADAM_B1 = 0.9
ADAM_B2 = 0.999
ADAM_EPS = 1e-08
ADAM_WD = 0.01
ADAM_STEP = 10

SUBLANES = 8
LANES = 128
VMEM_LIMIT = 48 * 1024 * 1024
TOK_TILE = 256
IN_TILE = 1024
SCAN_TILE = 1024

MESH = pl.DeviceIdType.MESH
HBM_SPEC = pl.BlockSpec(memory_space=pltpu.HBM)


def _build(body, **kw):
    return pl.pallas_call(body, **kw)


def _pcall(body, **kw):
    def call(*operands):
        pinned = [a if jnp.issubdtype(a.dtype, jnp.integer) else pltpu.with_memory_space_constraint(a, pltpu.HBM)
                  for a in operands]
        return _build(body, **kw)(*pinned)
    return call


def _whole_specs(arrays):
    return [pl.BlockSpec(a.shape, functools.partial(lambda nd, i: (0,) * nd, len(a.shape))) for a in arrays]


def _out(shape, dtype):
    return pltpu.HBM(tuple(shape), dtype)


def _params(n_grid):
    return pltpu.CompilerParams(dimension_semantics=("arbitrary",) * n_grid,
                                vmem_limit_bytes=VMEM_LIMIT)


def _dot(a, b):
    return jnp.dot(a, b, preferred_element_type=F32)


def _dot_nt(a, b):
    return lax.dot_general(a, b, (((1,), (1,)), ((), ())), preferred_element_type=F32)


def _dot_tn(a, b):
    return lax.dot_general(a, b, (((0,), (0,)), ((), ())), preferred_element_type=F32)


def _sigmoid(z):
    return 1.0 / (1.0 + jnp.exp(-z))


_GELU_C = math.sqrt(2.0 / math.pi)


def _gelu_and_grad(y):
    inner = _GELU_C * (y + 0.044715 * (y * y * y))
    t = jnp.tanh(inner)
    g = 0.5 * y * (1.0 + t)
    dg = 0.5 * (1.0 + t) + 0.5 * y * (1.0 - t * t) * (_GELU_C * (1.0 + 3.0 * 0.044715 * (y * y)))
    return g, dg


def _silu_and_grad(z):
    s = _sigmoid(z)
    return z * s, s * (1.0 + z * (1.0 - s))


def _shift_down(v, halo, k):
    rolled = pltpu.roll(v, k, 0)
    row = lax.broadcasted_iota(jnp.int32, v.shape, 0)
    for r in range(k):
        rolled = jnp.where(row == r, halo[SUBLANES - k + r:SUBLANES - k + r + 1, :], rolled)
    return rolled


def _shift_up(v, halo, k):
    n = v.shape[0]
    rolled = pltpu.roll(v, n - k, 0)
    row = lax.broadcasted_iota(jnp.int32, v.shape, 0)
    for r in range(k):
        rolled = jnp.where(row == n - k + r, halo[r:r + 1, :], rolled)
    return rolled


def _mesh_pos():
    return lax.axis_index("x"), lax.axis_index("y"), lax.axis_index("c")


def _direct_copies(srcs_for, out_refs, send_sems, recv_sems, loc_sems):
    x, y, c = _mesh_pos()
    me_id = 4 * x + 2 * y + c
    n_arr = len(out_refs)
    dsts = [r.at[me_id] for r in out_refs]
    own = srcs_for(me_id)
    mine = [pltpu.make_async_copy(own[a], dsts[a], loc_sems.at[a]) for a in range(n_arr)]
    sends = []
    for k in range(1, N_DEV):
        px, py, pc = x ^ ((k >> 2) & 1), y ^ ((k >> 1) & 1), c ^ (k & 1)
        src = srcs_for(4 * px + 2 * py + pc)
        for a in range(n_arr):
            sends.append(pltpu.make_async_remote_copy(
                src_ref=src[a], dst_ref=dsts[a],
                send_sem=send_sems.at[(k - 1) * n_arr + a], recv_sem=recv_sems.at[(k - 1) * n_arr + a],
                device_id=(px, py, pc), device_id_type=MESH))
    return mine, sends


class _TwoLevelGather:
    def __init__(self, srcs, slots, send_sems, recv_sems, loc_sems):
        self.srcs, self.slots, self.n_arr = srcs, slots, len(srcs)
        self.send_sems, self.recv_sems, self.loc_sems = send_sems, recv_sems, loc_sems
        x, y, c = _mesh_pos()
        self.c = c
        self.me, self.sib = (x, y, c), (x, y, 1 - c)
        self.chips = [(1 - x, y), (x, 1 - y), (1 - x, 1 - y)]

    def _copies(self, k, block, to, from_src=False):
        dev = 4 * block[0] + 2 * block[1] + block[2]
        return [pltpu.make_async_remote_copy(
            src_ref=self.srcs[a] if from_src else self.slots[a](dev), dst_ref=self.slots[a](dev),
            send_sem=self.send_sems.at[k * self.n_arr + a], recv_sem=self.recv_sems.at[k * self.n_arr + a],
            device_id=to, device_id_type=MESH) for a in range(self.n_arr)]

    def _local(self):
        dev = 4 * self.me[0] + 2 * self.me[1] + self.me[2]
        return [pltpu.make_async_copy(self.srcs[a], self.slots[a](dev), self.loc_sems.at[a])
                for a in range(self.n_arr)]

    def start(self):
        for cp in self._local() + self._copies(0, self.me, self.sib, True):
            cp.start()
        for j in (0, 1):
            for cp in self._copies(1 + j, self.me, (*self.chips[j], self.c), True):
                cp.start()

    def wait_own(self):
        for cp in self._local():
            cp.wait()

    def wait_sibling(self):
        for cp in self._copies(0, self.sib, self.me):
            cp.wait_recv()

    def wait_and_pass_on(self, j):
        chip = self.chips[j]
        for cp in self._copies(1 + j, (*chip, self.c), self.me):
            cp.wait_recv()
        for cp in self._copies(4 + j, (*chip, self.c), self.sib):
            cp.start()

    def neighbours_landed(self):
        x, y, c = self.me
        self.wait_and_pass_on(0)
        self.wait_and_pass_on(1)
        for cp in self._copies(1 + 2, (x ^ c, y ^ (1 - c), c), (x ^ (1 - c), y ^ c, c)):
            cp.start()

    def diagonal_landed(self):
        self.wait_and_pass_on(2)

    def wait_passed_on(self, j):
        for cp in self._copies(4 + j, (*self.chips[j], 1 - self.c), self.me):
            cp.wait_recv()

    def wait_sends(self):
        for cp in self._copies(0, self.me, self.sib, True):
            cp.wait_send()
        for j, chip in enumerate(self.chips):
            for cp in self._copies(1 + j, self.me, (*chip, self.c), True) + self._copies(4 + j, (*chip, self.c), self.sib):
                cp.wait_send()

    def finish(self):
        self.wait_sibling()
        for j in range(3):
            self.wait_passed_on(j)
        self.wait_sends()
        self.wait_own()


def _disc(a_re, a_im, log_dt, b_re, b_im):
    dt = jnp.exp(log_dt)
    mag = jnp.exp(a_re * dt)
    ab_re = mag * jnp.cos(a_im * dt)
    ab_im = mag * jnp.sin(a_im * dt)
    den = a_re * a_re + a_im * a_im
    p_re = ab_re - 1.0
    p_im = ab_im
    q_re = (p_re * a_re + p_im * a_im) / den
    q_im = (p_im * a_re - p_re * a_im) / den
    bb_re = q_re * b_re - q_im * b_im
    bb_im = q_re * b_im + q_im * b_re
    return ab_re, ab_im, bb_re, bb_im


def _split3(v):
    hi = v.astype(BF16)
    r1 = v - hi.astype(F32)
    mid = r1.astype(BF16)
    lo = (r1 - mid.astype(F32)).astype(BF16)
    return hi, mid, lo


def _select_dot(sel, v):
    return sum(_dot(sel, t) for t in _split3(v))


PACK_ROWS = 72
PACK_W = 512
ROW_FINAL_GAIN, ROW_NORM_GAIN, ROW_BGLU_D, ROW_CONV, ROW_LOSS, ROW_S5 = 0, 8, 16, 24, 32, 40
LANE_A_RE, LANE_A_IM, LANE_LOG_DT = 0, 128, 256


def _ssm_disc_bwd_pack(a_re_x, a_im_x, log_dt_x, b_re, b_im, g_ab_re, g_ab_im, dbb_re_d, dbb_im_d,
                       loss_t, dg8, dgf, dbg, dd, dcw, dc_re_d, dc_im_d):
    rows_gh = N_GROUPS * GROUP

    def body(are, aim, ldt, bre, bim, gabre, gabim, dbbre_ref, dbbim_ref,
             loss_ref, dg8_ref, dgf_ref, dbg_ref, dd_ref, dcw_ref, dcre_ref, dcim_ref,
             p_ref, gc_ref, gb_ref, gbb_re, gbb_im):
        r_g = lax.broadcasted_iota(jnp.int32, (N_GROUPS, rows_gh), 0)
        c_gh = lax.broadcasted_iota(jnp.int32, (N_GROUPS, rows_gh), 1)
        group_sum = (c_gh // GROUP == r_g).astype(BF16)
        r_gh = lax.broadcasted_iota(jnp.int32, (rows_gh, N_GROUPS), 0)
        c_g = lax.broadcasted_iota(jnp.int32, (rows_gh, N_GROUPS), 1)
        first_row = (r_gh == c_g * GROUP).astype(BF16)

        def diag_block(ref, j, gi):
            return ref[j, gi * GROUP:(gi + 1) * GROUP, gi * STATE:(gi + 1) * STATE]

        for j in range(N_JBLK):
            for gi in range(SUBLANES):
                r0 = (j * SUBLANES + gi) * GROUP
                gbb_re[r0:r0 + GROUP, :] = diag_block(dbbre_ref, j, gi)
                gbb_im[r0:r0 + GROUP, :] = diag_block(dbbim_ref, j, gi)
                both = jnp.concatenate([diag_block(dcre_ref, j, gi), -diag_block(dcim_ref, j, gi)], axis=1)
                gc_ref[r0:r0 + GROUP, :] = both.astype(BF16)

        _, vjp = jax.vjp(_disc, are[...], aim[...], ldt[...], bre[...], bim[...])
        d_are, d_aim, d_ldt, d_bre, d_bim = vjp((_select_dot(first_row, gabre[...]), _select_dot(first_row, gabim[...]),
                                                 gbb_re[...], gbb_im[...]))
        gb_ref[...] = jnp.concatenate([d_bre, d_bim], axis=1).astype(BF16)

        p_ref[...] = jnp.zeros_like(p_ref)
        half = D_MODEL // 2
        for r, src in ((ROW_FINAL_GAIN, dgf_ref), (ROW_NORM_GAIN, dg8_ref)):
            p_ref[r:r + 1, :] = src[0:1, 0:half]
            p_ref[r + 1:r + 2, :] = src[0:1, half:D_MODEL]
        p_ref[ROW_BGLU_D:ROW_BGLU_D + 1, :] = dbg_ref[...]
        p_ref[ROW_BGLU_D + 1:ROW_BGLU_D + 2, :] = dd_ref[...]
        p_ref[ROW_CONV:ROW_CONV + SUBLANES, :] = dcw_ref[...]
        p_ref[ROW_LOSS:ROW_LOSS + SUBLANES, 0:LANES] = loss_ref[...]
        s5 = slice(ROW_S5, ROW_S5 + N_GROUPS)
        p_ref[s5, LANE_A_RE:LANE_A_RE + STATE] = _select_dot(group_sum, d_are)
        p_ref[s5, LANE_A_IM:LANE_A_IM + STATE] = _select_dot(group_sum, d_aim)
        p_ref[s5, LANE_LOG_DT:LANE_LOG_DT + LANES] = _select_dot(group_sum, jnp.broadcast_to(d_ldt, (rows_gh, LANES)))

    operands = (a_re_x, a_im_x, log_dt_x, b_re, b_im, g_ab_re, g_ab_im, dbb_re_d, dbb_im_d,
                loss_t, dg8, dgf, dbg, dd, dcw, dc_re_d, dc_im_d)
    out_shape = (_out((PACK_ROWS, PACK_W), F32),
                 _out((rows_gh, 2 * STATE), BF16),
                 _out((rows_gh, 2 * STATE), BF16))
    return _pcall(body, name="ssm_disc_bwd_pack", grid=(1,), out_shape=out_shape,
                  in_specs=_whole_specs(operands), out_specs=tuple(_whole_specs(out_shape)),
                  scratch_shapes=[pltpu.VMEM((rows_gh, STATE), F32), pltpu.VMEM((rows_gh, STATE), F32)],
                  compiler_params=_params(1))(*operands)


def _s5_prepare(are, aim, ldt, bre, bim, cre, cim,
                o_ax_re, o_ax_im, o_ldt_x, o_ab_re, o_ab_im, o_bb_re, o_bb_im, o_c_re, o_c_imn):
    rows_gh = N_GROUPS * GROUP
    rep = (lax.broadcasted_iota(jnp.int32, (rows_gh, N_GROUPS), 0) // GROUP
           == lax.broadcasted_iota(jnp.int32, (rows_gh, N_GROUPS), 1)).astype(BF16)
    eye = (lax.broadcasted_iota(jnp.int32, (N_GROUPS, N_GROUPS), 0)
           == lax.broadcasted_iota(jnp.int32, (N_GROUPS, N_GROUPS), 1)).astype(F32)
    ldt_col = jnp.sum(eye * ldt[...], axis=1, keepdims=True)
    a_re_x = _select_dot(rep, are[...])
    a_im_x = _select_dot(rep, aim[...])
    ldt_x = _select_dot(rep, jnp.broadcast_to(ldt_col, (N_GROUPS, LANES)))[:, 0:1]
    o_ax_re[...] = a_re_x
    o_ax_im[...] = a_im_x
    o_ldt_x[...] = ldt_x
    ab_re, ab_im, bb_re, bb_im = _disc(a_re_x, a_im_x, ldt_x, bre[...], bim[...])
    for j in range(N_JBLK):
        first = [(j * SUBLANES + gi) * GROUP for gi in range(SUBLANES)]
        o_ab_re[j] = jnp.concatenate([ab_re[r:r + 1, :] for r in first], axis=1)
        o_ab_im[j] = jnp.concatenate([ab_im[r:r + 1, :] for r in first], axis=1)
    for o, v in ((o_bb_re, bb_re), (o_bb_im, bb_im), (o_c_re, cre[...]), (o_c_imn, -cim[...])):
        for j in range(N_JBLK):
            for gi in range(SUBLANES):
                r0 = (j * SUBLANES + gi) * GROUP
                parts = [v[r0:r0 + GROUP, :] if k == gi else jnp.zeros((GROUP, STATE), F32) for k in range(SUBLANES)]
                o[j, gi * GROUP:(gi + 1) * GROUP, :] = jnp.concatenate(parts, axis=1).astype(BF16)


def _in_proj(order, x2, g1, w_in_b, s5):
    n = x2.shape[0]
    tm = min(IN_TILE, n)
    n_tiles = n // tm
    n_s5_in = len(s5)
    n_s5_out = 9

    def body(order_ref, x_ref, g_ref, w_ref, *refs):
        s5_in = refs[:n_s5_in]
        xn_ref, proj_ref, wall_ref = refs[n_s5_in:n_s5_in + 3]
        s5_out = refs[n_s5_in + 3:n_s5_in + 3 + n_s5_out]
        xn_scr, wbuf, send_sems, recv_sems, loc_sems, out_sems = refs[n_s5_in + 3 + n_s5_out:]
        k = pl.program_id(0)
        i = pl.program_id(1)

        def slot(dev):
            return wbuf.at[dev // 2, :, pl.ds(pl.multiple_of((dev % 2) * COLS_PER_DEV, LANES), COLS_PER_DEV)]

        gather = _TwoLevelGather([w_ref], [slot], send_sems, recv_sems, loc_sems)

        @pl.when((k == 0) & (i == 0))
        def _():
            gather.start()

        def own_chip():
            gather.wait_own()
            gather.wait_sibling()

        def x_chip():
            gather.neighbours_landed()
            gather.wait_passed_on(0)

        def diag_chip():
            gather.diagonal_landed()
            gather.wait_passed_on(2)

        arrivals = [own_chip, x_chip, functools.partial(gather.wait_passed_on, 1), diag_chip]
        for kk, arrived in enumerate(arrivals):
            @pl.when((k == kk) & (i == 0))
            def _(arrived=arrived):
                arrived()

        rows = pl.ds(pl.multiple_of(i * tm, tm), tm)

        @pl.when(k == 0)
        def _():
            x = x_ref[...]
            r = lax.rsqrt(jnp.mean(x * x, axis=-1, keepdims=True) + EPS)
            xn = ((x * r) * g_ref[...]).astype(BF16)
            xn_scr[rows, :] = xn
            xn_ref[...] = xn

        proj_ref[...] = _dot(xn_scr[rows, :], wbuf[order_ref[k]])

        @pl.when((k == 0) & (i == n_tiles - 1))
        def _():
            _s5_prepare(*s5_in, *s5_out)

        @pl.when((k == N_CHIP - 1) & (i == n_tiles - 1))
        def _():
            gather.wait_sends()
            outs = [pltpu.make_async_copy(wbuf.at[q], wall_ref.at[:, q * COLS_PER_CHIP:(q + 1) * COLS_PER_CHIP],
                                          out_sems.at[q]) for q in range(N_CHIP)]
            for cp in outs:
                cp.start()
            for cp in outs:
                cp.wait()

    tile_once = lambda k, i, order: (jnp.where(k == 0, i, n_tiles - 1), 0)
    whole = lambda shape: pl.BlockSpec(shape, lambda k, i, order: (0,) * len(shape))
    rows_gh = N_GROUPS * GROUP
    s5_out_shapes = ([(rows_gh, STATE), F32], [(rows_gh, STATE), F32], [(rows_gh, 1), F32],
                     [(N_JBLK, 1, JB_ST), F32], [(N_JBLK, 1, JB_ST), F32]) + ([(N_JBLK, JB_CH, JB_ST), BF16],) * 4
    grid_spec = pltpu.PrefetchScalarGridSpec(
        num_scalar_prefetch=1, grid=(N_CHIP, n_tiles),
        in_specs=[pl.BlockSpec((tm, D_MODEL), tile_once),
                  whole((1, D_MODEL)),
                  HBM_SPEC,
                  *(whole(a.shape) for a in s5)],
        out_specs=(pl.BlockSpec((tm, D_MODEL), tile_once),
                   pl.BlockSpec((tm, COLS_PER_CHIP), lambda k, i, order: (i, order[k])),
                   HBM_SPEC,
                   *(whole(shape) for shape, _ in s5_out_shapes)),
        scratch_shapes=[pltpu.VMEM((n, D_MODEL), BF16), pltpu.VMEM((N_CHIP, D_MODEL, COLS_PER_CHIP), BF16),
                        pltpu.SemaphoreType.DMA((7,)), pltpu.SemaphoreType.DMA((7,)), pltpu.SemaphoreType.DMA((1,)),
                        pltpu.SemaphoreType.DMA((N_CHIP,))])
    outs = _pcall(
        body, name="in_proj", grid_spec=grid_spec,
        out_shape=(_out((n, D_MODEL), BF16), _out((n, IN_COLS), F32),
                   _out((D_MODEL, IN_COLS), BF16),
                   *(_out(shape, dt) for shape, dt in s5_out_shapes)),
        compiler_params=_params(2),
    )(order, x2, g1, w_in_b, *s5)
    return outs[0], outs[1], outs[2], outs[3:]


def _cmul(p, q):
    return p[0] * q[0] - p[1] * q[1], p[0] * q[1] + p[1] * q[0]


def _scan_tables(ar, ai, width, reverse):
    pows = [(ar, ai)]
    for _ in range(SUBLANES - 1):
        pows.append(_cmul(pows[-1], (ar, ai)))
    row = lax.broadcasted_iota(jnp.int32, (SUBLANES, width), 0)

    def bc(v):
        return jnp.broadcast_to(v, (SUBLANES, width))

    levels = []
    for k in (1, 2, 4):
        keep = (row <= SUBLANES - 1 - k) if reverse else (row >= k)
        levels.append((jnp.where(keep, bc(pows[k - 1][0]), 0.0), jnp.where(keep, bc(pows[k - 1][1]), 0.0)))
    cre = jnp.zeros((SUBLANES, width), F32)
    cim = jnp.zeros((SUBLANES, width), F32)
    for r in range(SUBLANES):
        e = (SUBLANES - r) if reverse else (r + 1)
        cre = jnp.where(row == r, bc(pows[e - 1][0]), cre)
        cim = jnp.where(row == r, bc(pows[e - 1][1]), cim)
    return levels, (cre, cim)


def _load_chunked(src_ref, b, dst_ref, n_rows):
    n_blk = n_rows // SUBLANES
    for i in range(n_blk):
        dst_ref[b, i * SUBLANES:(i + 1) * SUBLANES, :] = src_ref[b, pl.ds(i, SUBLANES, stride=n_blk), :]


def _store_chunked(val, dst_ref, b, n_rows):
    n_blk = n_rows // SUBLANES
    for i in range(n_blk):
        dst_ref[b, pl.ds(i, SUBLANES, stride=n_blk), :] = val[i * SUBLANES:(i + 1) * SUBLANES, :]


def _chunk_scan(re_ref, im_ref, bs, car_ref, ar, ai, n_rows, reverse, on_block=None):
    width = re_ref.shape[2]
    n_blk = n_rows // SUBLANES
    shape = (SUBLANES, width)
    abr = jnp.broadcast_to(ar, shape)
    abi = jnp.broadcast_to(ai, shape)
    order = list(range(n_blk - 1, -1, -1)) if reverse else list(range(n_blk))

    def blk(ref, b, i):
        return ref[b, i * SUBLANES:(i + 1) * SUBLANES, :]

    def step(state, b, i):
        sr, si = state
        return abr * sr - abi * si + blk(re_ref, b, i), abr * si + abi * sr + blk(im_ref, b, i)

    finals = {b: (blk(re_ref, b, order[0]), blk(im_ref, b, order[0])) for b in bs}
    for i in order[1:]:
        for b in bs:
            finals[b] = step(finals[b], b, i)

    mr, mi = ar, ai
    for _ in range(n_blk.bit_length() - 1):
        mr, mi = _cmul((mr, mi), (mr, mi))
    levels, _ = _scan_tables(mr, mi, width, reverse)
    mbr = jnp.broadcast_to(mr, shape)
    mbi = jnp.broadcast_to(mi, shape)
    row = lax.broadcasted_iota(jnp.int32, shape, 0)
    edge_in = SUBLANES - 1 if reverse else 0
    edge_out = 0 if reverse else SUBLANES - 1
    sh1 = SUBLANES - 1 if reverse else 1
    states = {}
    for b in bs:
        fr, fi = finals[b]
        gr = jnp.where(row == edge_in, jnp.broadcast_to(car_ref[b, 0:1, :], shape), pltpu.roll(fr, sh1, 0))
        gi = jnp.where(row == edge_in, jnp.broadcast_to(car_ref[b, 1:2, :], shape), pltpu.roll(fi, sh1, 0))
        for (lr, li), k in zip(levels, (1, 2, 4)):
            sh = (SUBLANES - k) if reverse else k
            sr = pltpu.roll(gr, sh, 0)
            si = pltpu.roll(gi, sh, 0)
            gr, gi = gr + (lr * sr - li * si), gi + (lr * si + li * sr)
        car_ref[b, 0:1, :] = (fr + (mbr * gr - mbi * gi))[edge_out:edge_out + 1, :]
        car_ref[b, 1:2, :] = (fi + (mbr * gi + mbi * gr))[edge_out:edge_out + 1, :]
        states[b] = (gr, gi)

    for i in order:
        for b in bs:
            states[b] = step(states[b], b, i)
            re_ref[b, i * SUBLANES:(i + 1) * SUBLANES, :] = states[b][0]
            im_ref[b, i * SUBLANES:(i + 1) * SUBLANES, :] = states[b][1]
            if on_block is not None:
                on_block(b, i, *states[b])


def _ssm_fwd(u, bb_re, bb_im, c_re_t, c_imn_t, d_row, ab_re, ab_im, w_out_b, w_glu_b, conv_p, n_seq, seq):
    tt = min(SCAN_TILE, seq)
    nt = seq // tt

    def body(u_ref, bbre, bbim, cre, cimn, d_ref, are, aim, wout_ref, wglu_ref, cw_ref,
             sre_ref, sim_ref, y_ref, oout_ref, oglu_ref, ocw_ref,
             up_ref, car_ref, send_sems, recv_sems, loc_sems):
        j = pl.program_id(0)
        t = pl.program_id(1)
        gather = _TwoLevelGather(
            [wout_ref, wglu_ref, cw_ref],
            [lambda dev: oout_ref.at[pl.ds(pl.multiple_of(dev * OUT_ROWS_PER_DEV, OUT_ROWS_PER_DEV), OUT_ROWS_PER_DEV), :],
             lambda dev: oglu_ref.at[pl.ds(pl.multiple_of(dev * GLU_ROWS_PER_DEV, GLU_ROWS_PER_DEV), GLU_ROWS_PER_DEV), :],
             lambda dev: ocw_ref.at[dev]],
            send_sems, recv_sems, loc_sems)

        @pl.when((j == 0) & (t == 0))
        def _():
            gather.start()

        @pl.when((j == N_JBLK // 2) & (t == 0))
        def _():
            gather.neighbours_landed()

        @pl.when((j == N_JBLK - 1) & (t == 0))
        def _():
            gather.diagonal_landed()

        @pl.when(t == 0)
        def _():
            car_ref[...] = jnp.zeros_like(car_ref)

        bs = list(range(n_seq))
        for b in bs:
            _load_chunked(u_ref, b, up_ref, tt)
        for b in bs:
            ub = up_ref[b].astype(BF16)
            sre_ref[b] = _dot(ub, bbre[0])
            sim_ref[b] = _dot(ub, bbim[0])
            _chunk_scan(sre_ref, sim_ref, [b], car_ref, are[0], aim[0], tt, reverse=False)
        for b in bs:
            yp = (_dot_nt(sre_ref[b].astype(BF16), cre[0]) + _dot_nt(sim_ref[b].astype(BF16), cimn[0])
                  + d_ref[...] * up_ref[b])
            _store_chunked(yp, y_ref, b, tt)

        @pl.when((j == N_JBLK - 1) & (t == nt - 1))
        def _():
            gather.finish()

    tok = lambda j, t: (0, t, j)
    blk3 = lambda j, t: (j, 0, 0)
    row = lambda j, t: (0, j)
    st = _out((n_seq, seq, N_JBLK * JB_ST), F32)
    n_arr = 3
    return _pcall(
        body, name="ssm_fwd", grid=(N_JBLK, nt),
        out_shape=(st, st, _out((n_seq, seq, SSM_W), F32),
                   _out((D_MODEL, D_MODEL), BF16), _out((SSM_W, SSM_W), BF16),
                   _out((N_DEV, SUBLANES, LANES), F32)),
        in_specs=[pl.BlockSpec((n_seq, tt, JB_CH), tok),
                  pl.BlockSpec((1, JB_CH, JB_ST), blk3), pl.BlockSpec((1, JB_CH, JB_ST), blk3),
                  pl.BlockSpec((1, JB_CH, JB_ST), blk3), pl.BlockSpec((1, JB_CH, JB_ST), blk3),
                  pl.BlockSpec((1, JB_CH), row), pl.BlockSpec((1, 1, JB_ST), blk3), pl.BlockSpec((1, 1, JB_ST), blk3),
                  HBM_SPEC, HBM_SPEC, HBM_SPEC],
        out_specs=(pl.BlockSpec((n_seq, tt, JB_ST), tok), pl.BlockSpec((n_seq, tt, JB_ST), tok),
                   pl.BlockSpec((n_seq, tt, JB_CH), tok), HBM_SPEC, HBM_SPEC, HBM_SPEC),
        scratch_shapes=[pltpu.VMEM((n_seq, tt, JB_CH), F32), pltpu.VMEM((n_seq, SUBLANES, JB_ST), F32),
                        pltpu.SemaphoreType.DMA((7 * n_arr,)), pltpu.SemaphoreType.DMA((7 * n_arr,)),
                        pltpu.SemaphoreType.DMA((n_arr,))],
        compiler_params=_params(2),
    )(u, bb_re, bb_im, c_re_t, c_imn_t, d_row, ab_re, ab_im, w_out_b, w_glu_b, conv_p)


def _ssm_bwd(dy, u, s_re, s_im, bb_re, bb_im, c_re_t, c_imn_t, d_row, ab_re, ab_im, g_out, g_glu, n_seq, seq):
    tt = min(SCAN_TILE, seq)
    nt = seq // tt
    rows8 = tt // SUBLANES

    def body(dy_ref, u_ref, sre_ref, sim_ref, pre_ref, pim_ref, bbre, bbim, cre, cimn, d_ref, are, aim,
             gout_ref, gglu_ref,
             du_ref, dcre_ref, dcim_ref, dbbre_ref, dbbim_ref, dare_ref, daim_ref, dd_ref, rout_ref, rglu_ref,
             lre_ref, lim_ref, dyp_ref, up_ref, car_ref, send_sems, recv_sems, loc_sems):
        j = pl.program_id(0)
        tr = pl.program_id(1)

        def exchange():
            return _direct_copies(lambda pid: [gout_ref.at[pid], gglu_ref.at[pid]], [rout_ref, rglu_ref],
                                  send_sems, recv_sems, loc_sems)

        @pl.when((j == 0) & (tr == 0))
        def _():
            mine, sends = exchange()
            for cp in mine + sends:
                cp.start()

        @pl.when(tr == 0)
        def _():
            car_ref[...] = jnp.zeros_like(car_ref)
            for r in (dcre_ref, dcim_ref, dbbre_ref, dbbim_ref, dare_ref, daim_ref, dd_ref):
                r[...] = jnp.zeros_like(r)

        first = tr == nt - 1
        row = lax.broadcasted_iota(jnp.int32, (SUBLANES, JB_ST), 0)
        n_blk = tt // SUBLANES
        bs = list(range(n_seq))
        for b in bs:
            _load_chunked(dy_ref, b, dyp_ref, tt)
            _load_chunked(u_ref, b, up_ref, tt)
        for b in bs:
            dyb = dyp_ref[b].astype(BF16)
            lre_ref[b] = _dot(dyb, cre[0])
            lim_ref[b] = _dot(dyb, cimn[0])
        acc = {b: [jnp.zeros((SUBLANES, JB_ST), F32), jnp.zeros((SUBLANES, JB_ST), F32)] for b in bs}

        def on_block(b, i, lr, li):
            if i > 0:
                spr = sre_ref[b, (i - 1) * SUBLANES:i * SUBLANES, :]
                spi = sim_ref[b, (i - 1) * SUBLANES:i * SUBLANES, :]
            else:
                hr = jnp.where(first, 0.0, pre_ref[b, SUBLANES - 1:SUBLANES, :])
                hi = jnp.where(first, 0.0, pim_ref[b, SUBLANES - 1:SUBLANES, :])
                last_r = sre_ref[b, (n_blk - 1) * SUBLANES:n_blk * SUBLANES, :]
                last_i = sim_ref[b, (n_blk - 1) * SUBLANES:n_blk * SUBLANES, :]
                spr = jnp.where(row == 0, jnp.broadcast_to(hr, row.shape), pltpu.roll(last_r, 1, 0))
                spi = jnp.where(row == 0, jnp.broadcast_to(hi, row.shape), pltpu.roll(last_i, 1, 0))
            acc[b][0] = acc[b][0] + (lr * spr + li * spi)
            acc[b][1] = acc[b][1] + (li * spr - lr * spi)

        _chunk_scan(lre_ref, lim_ref, bs, car_ref, are[0], -aim[0], tt, reverse=True, on_block=on_block)
        for b in bs:
            dare_ref[...] += jnp.sum(acc[b][0], axis=0, keepdims=True)
            daim_ref[...] += jnp.sum(acc[b][1], axis=0, keepdims=True)
            dyp = dyp_ref[b]
            up = up_ref[b]
            dyb = dyp.astype(BF16)
            ub = up.astype(BF16)
            lrb = lre_ref[b].astype(BF16)
            lib = lim_ref[b].astype(BF16)
            dup = d_ref[...] * dyp + _dot_nt(lrb, bbre[0]) + _dot_nt(lib, bbim[0])
            _store_chunked(dup, du_ref, b, tt)
            dbbre_ref[0] += _dot_tn(ub, lrb)
            dbbim_ref[0] += _dot_tn(ub, lib)
            dcre_ref[0] += _dot_tn(dyb, sre_ref[b].astype(BF16))
            dcim_ref[0] += _dot_tn(dyb, sim_ref[b].astype(BF16))
            dd_ref[...] += jnp.sum(dyp * up, axis=0, keepdims=True)

        @pl.when((j == N_JBLK - 1) & (tr == nt - 1))
        def _():
            mine, sends = exchange()
            for cp in sends + mine:
                cp.wait()

    tok = lambda j, t: (0, nt - 1 - t, j)
    halo = lambda j, t: (0, jnp.maximum((nt - 1 - t) * rows8 - 1, 0), j)
    blk3 = lambda j, t: (j, 0, 0)
    row1 = lambda j, t: (0, j)
    acc_shape = _out((N_JBLK, JB_CH, JB_ST), F32)
    return _pcall(
        body, name="ssm_bwd", grid=(N_JBLK, nt),
        out_shape=(_out((n_seq, seq, SSM_W), F32), acc_shape, acc_shape, acc_shape, acc_shape,
                   _out((1, N_JBLK * JB_ST), F32), _out((1, N_JBLK * JB_ST), F32),
                   _out((1, SSM_W), F32),
                   _out((N_DEV,) + g_out.shape[1:], F32),
                   _out((N_DEV,) + g_glu.shape[1:], F32)),
        in_specs=[pl.BlockSpec((n_seq, tt, JB_CH), tok), pl.BlockSpec((n_seq, tt, JB_CH), tok),
                  pl.BlockSpec((n_seq, tt, JB_ST), tok), pl.BlockSpec((n_seq, tt, JB_ST), tok),
                  pl.BlockSpec((n_seq, SUBLANES, JB_ST), halo), pl.BlockSpec((n_seq, SUBLANES, JB_ST), halo),
                  pl.BlockSpec((1, JB_CH, JB_ST), blk3), pl.BlockSpec((1, JB_CH, JB_ST), blk3),
                  pl.BlockSpec((1, JB_CH, JB_ST), blk3), pl.BlockSpec((1, JB_CH, JB_ST), blk3),
                  pl.BlockSpec((1, JB_CH), row1), pl.BlockSpec((1, 1, JB_ST), blk3), pl.BlockSpec((1, 1, JB_ST), blk3),
                  HBM_SPEC, HBM_SPEC],
        out_specs=(pl.BlockSpec((n_seq, tt, JB_CH), tok),
                   pl.BlockSpec((1, JB_CH, JB_ST), blk3), pl.BlockSpec((1, JB_CH, JB_ST), blk3),
                   pl.BlockSpec((1, JB_CH, JB_ST), blk3), pl.BlockSpec((1, JB_CH, JB_ST), blk3),
                   pl.BlockSpec((1, JB_ST), row1), pl.BlockSpec((1, JB_ST), row1), pl.BlockSpec((1, JB_CH), row1),
                   HBM_SPEC, HBM_SPEC),
        scratch_shapes=[pltpu.VMEM((n_seq, tt, JB_ST), F32), pltpu.VMEM((n_seq, tt, JB_ST), F32),
                        pltpu.VMEM((n_seq, tt, JB_CH), F32), pltpu.VMEM((n_seq, tt, JB_CH), F32),
                        pltpu.VMEM((n_seq, SUBLANES, JB_ST), F32),
                        pltpu.SemaphoreType.DMA((7 * 2,)), pltpu.SemaphoreType.DMA((7 * 2,)),
                        pltpu.SemaphoreType.DMA((2,))],
        compiler_params=_params(2),
    )(dy, u, s_re, s_im, s_re, s_im, bb_re, bb_im, c_re_t, c_imn_t, d_row, ab_re, ab_im, g_out, g_glu)


def _mix(x2, tgt2, y, proj, gf, b_glu, conv8, w_glu_f, w_out_f, seq):
    n = x2.shape[0]
    tm = TOK_TILE
    tiles_per_seq = seq // tm
    rows8 = tm // SUBLANES

    def body(x_ref, t_ref, y_ref, zs_ref, h_ref, bc_ref, cc_ref, zc_ref, hp_ref, ccp_ref,
             gf_ref, bg_ref, cw_ref, wg_ref, wo_ref,
             dh2_ref, dy_ref, dzs_ref, dbc_ref, dzc_ref, dyc_ref,
             dwo_ref, dwg_ref, loss_ref, dgf_ref, dbg_ref, dcw_ref):
        i = pl.program_id(0)

        @pl.when(i == 0)
        def _():
            for r in (dwo_ref, dwg_ref, loss_ref, dgf_ref, dbg_ref, dcw_ref):
                r[...] = jnp.zeros_like(r)

        yv = y_ref[...]
        y1, dgelu = _gelu_and_grad(yv)
        y1b = y1.astype(BF16)
        gate = _sigmoid(_dot(y1b, wg_ref[...]) + bg_ref[...])
        y2 = y1 * gate
        szs, dszs = _silu_and_grad(zs_ref[...])
        yssm = y2 * szs
        hv = h_ref[...]
        ccv = cc_ref[...]
        bcv = bc_ref[...]
        v = ccv * hv
        first = (i % tiles_per_seq) == 0
        vhalo = jnp.where(first, 0.0, ccp_ref[...] * hp_ref[...])
        v1 = _shift_down(v, vhalo, 1)
        v2 = _shift_down(v, vhalo, 2)
        w0 = cw_ref[0:1, :]
        w1 = cw_ref[1:2, :]
        w2 = cw_ref[2:3, :]
        yc = w0 * v2 + w1 * v1 + w2 * v
        szc, dszc = _silu_and_grad(zc_ref[...])
        yconv = (bcv * yc) * szc
        ysb = yssm.astype(BF16)
        ycb = yconv.astype(BF16)
        h2 = x_ref[...] + _dot(ysb, wo_ref[0:SSM_W, :]) + _dot(ycb, wo_ref[SSM_W:, :])
        r2 = lax.rsqrt(jnp.mean(h2 * h2, axis=-1, keepdims=True) + EPS)
        hn = h2 * r2
        gfv = gf_ref[...]
        err = hn * gfv - t_ref[...]
        loss_ref[...] += 0.5 * jnp.sum(jnp.mean(err * err, axis=-1, keepdims=True))
        dout = err * (1.0 / D_MODEL)
        dgf_ref[...] += jnp.sum(dout * hn, axis=0, keepdims=True)
        dn = dout * gfv
        dh2 = r2 * (dn - hn * jnp.mean(dn * hn, axis=-1, keepdims=True))
        dh2_ref[...] = dh2
        dh2b = dh2.astype(BF16)
        dwo_ref[0:SSM_W, :] += _dot_tn(ysb, dh2b)
        dwo_ref[SSM_W:, :] += _dot_tn(ycb, dh2b)
        dyssm = _dot_nt(dh2b, wo_ref[0:SSM_W, :])
        dyconv = _dot_nt(dh2b, wo_ref[SSM_W:, :])
        dy2 = dyssm * szs
        dzs_ref[...] = (dyssm * y2 * dszs).astype(BF16)
        dgp = dy2 * y1 * (gate * (1.0 - gate))
        dgpb = dgp.astype(BF16)
        dy1 = dy2 * gate + _dot_nt(dgpb, wg_ref[...])
        dwg_ref[...] += _dot_tn(y1b, dgpb)
        dbg_ref[...] += jnp.sum(dgp, axis=0, keepdims=True)
        dy_ref[...] = dy1 * dgelu
        dbc_ref[...] = (dyconv * yc * szc).astype(BF16)
        dyc = dyconv * bcv * szc
        dyc_ref[...] = dyc
        dzc_ref[...] = (dyconv * bcv * yc * dszc).astype(BF16)
        dcw_ref[0:1, :] += jnp.sum(dyc * v2, axis=0, keepdims=True)
        dcw_ref[1:2, :] += jnp.sum(dyc * v1, axis=0, keepdims=True)
        dcw_ref[2:3, :] += jnp.sum(dyc * v, axis=0, keepdims=True)

    tile_d = pl.BlockSpec((tm, D_MODEL), lambda i: (i, 0))
    tile_s = pl.BlockSpec((tm, SSM_W), lambda i: (i, 0))
    seg_of = lambda c: pl.BlockSpec((tm, SSM_W), lambda i: (i, c))
    halo_of = lambda c: pl.BlockSpec((SUBLANES, SSM_W), lambda i: (jnp.maximum(i * rows8 - 1, 0), c))
    const = lambda shape: pl.BlockSpec(shape, lambda i: (0,) * len(shape))
    seg = _out((n, SSM_W), F32)
    seg_b = _out((n, SSM_W), BF16)
    return _pcall(
        body, name="mix", grid=(n // tm,),
        out_shape=(_out((n, D_MODEL), F32), seg, seg_b, seg_b, seg_b, seg,
                   _out((D_MODEL, D_MODEL), F32), _out((SSM_W, SSM_W), F32),
                   _out((SUBLANES, LANES), F32), _out((1, D_MODEL), F32),
                   _out((1, SSM_W), F32), _out((SUBLANES, CONV_W), F32)),
        in_specs=[tile_d, tile_d, tile_s, seg_of(SEG_ZS), seg_of(SEG_H), seg_of(SEG_BC), seg_of(SEG_CC), seg_of(SEG_ZC),
                  halo_of(SEG_H), halo_of(SEG_CC),
                  const((1, D_MODEL)), const((1, SSM_W)), const((SUBLANES, CONV_W)),
                  const((SSM_W, SSM_W)), const((D_MODEL, D_MODEL))],
        out_specs=(tile_d, tile_s, tile_s, tile_s, tile_s, tile_s,
                   const((D_MODEL, D_MODEL)), const((SSM_W, SSM_W)), const((SUBLANES, LANES)),
                   const((1, D_MODEL)), const((1, SSM_W)), const((SUBLANES, CONV_W))),
        compiler_params=_params(1),
    )(x2, tgt2, y, proj, proj, proj, proj, proj, proj, proj, gf, b_glu, conv8, w_glu_f, w_out_f)


def _in_bwd(x2, dh2, du, dzs, dyc, proj, dbc, dzc, g1, conv8, w_full, seq):
    n = x2.shape[0]
    tm = TOK_TILE
    n_tiles = n // tm
    tiles_per_seq = seq // tm
    rows8 = tm // SUBLANES
    n_blk8 = n // SUBLANES

    def body(x_ref, dh2_ref, du_ref, dzs_ref, dyc_ref, dycn_ref, h_ref, cc_ref, dbc_ref, dzc_ref,
             g_ref, cw_ref, w_ref, gx_ref, dp_ref, dg_ref):
        i = pl.program_id(0)

        @pl.when(i == 0)
        def _():
            dg_ref[...] = jnp.zeros_like(dg_ref)

        dyc = dyc_ref[...]
        last = (i % tiles_per_seq) == tiles_per_seq - 1
        nhalo = jnp.where(last, 0.0, dycn_ref[...])
        dv = (cw_ref[2:3, :] * dyc + cw_ref[1:2, :] * _shift_up(dyc, nhalo, 1)
              + cw_ref[0:1, :] * _shift_up(dyc, nhalo, 2))
        parts = (du_ref[...], dzs_ref[...], dv * cc_ref[...], dbc_ref[...], dv * h_ref[...], dzc_ref[...])
        dxn = jnp.zeros((tm, D_MODEL), F32)
        for k, p in enumerate(parts):
            pb = p.astype(BF16)
            dp_ref[:, k * SSM_W:(k + 1) * SSM_W] = pb
            dxn = dxn + _dot_nt(pb, w_ref[:, k * SSM_W:(k + 1) * SSM_W])
        x = x_ref[...]
        r = lax.rsqrt(jnp.mean(x * x, axis=-1, keepdims=True) + EPS)
        xh = x * r
        dg_ref[...] += jnp.sum(dxn * xh, axis=0, keepdims=True)
        dn = dxn * g_ref[...]
        gx_ref[...] = dh2_ref[...] + r * (dn - xh * jnp.mean(dn * xh, axis=-1, keepdims=True))

    tile_d = pl.BlockSpec((tm, D_MODEL), lambda i: (i, 0))
    tile_s = pl.BlockSpec((tm, SSM_W), lambda i: (i, 0))
    seg_of = lambda c: pl.BlockSpec((tm, SSM_W), lambda i: (i, c))
    nhalo = pl.BlockSpec((SUBLANES, SSM_W), lambda i: (jnp.minimum((i + 1) * rows8, n_blk8 - 1), 0))
    const = lambda shape: pl.BlockSpec(shape, lambda i: (0,) * len(shape))
    return _pcall(
        body, name="in_bwd", grid=(n_tiles,),
        out_shape=(_out((n, D_MODEL), F32), _out((n, IN_COLS), BF16),
                   _out((SUBLANES, D_MODEL), F32)),
        in_specs=[tile_d, tile_d, tile_s, tile_s, tile_s, nhalo, seg_of(SEG_H), seg_of(SEG_CC), tile_s, tile_s,
                  const((1, D_MODEL)), const((SUBLANES, CONV_W)), const((D_MODEL, IN_COLS))],
        out_specs=(tile_d, pl.BlockSpec((tm, IN_COLS), lambda i: (i, 0)), const((SUBLANES, D_MODEL))),
        compiler_params=_params(1),
    )(x2, dh2, du, dzs, dyc, dyc, proj, proj, dbc, dzc, g1, conv8, w_full)


def _dw_in_exchange(order, xn, dproj, smalls):
    n = xn.shape[0]
    tk = min(1024, n)
    nk = n // tk
    piece = (D_MODEL, COLS_PER_DEV)
    n_small = len(smalls)

    def body(order_ref, xn_hbm, dp_ref, *refs):
        del order_ref
        sm_refs = refs[:n_small]
        own_ref, rchip_ref = refs[n_small:n_small + 2]
        rsm_refs = refs[n_small + 2:2 * n_small + 2]
        (xn_ref, acc, stage, sbuf, relay_in, xn_sems, give_send, give_recv, keep_send, keep_recv,
         relay_send, relay_recv, sm_send, sm_recv, sm_loc) = refs[2 * n_small + 2:]
        s = pl.program_id(0)

        def xn_copy(kk):
            rows = pl.ds(pl.multiple_of(kk * tk, tk), tk)
            return pltpu.make_async_copy(xn_hbm.at[rows, :], xn_ref.at[rows, :], xn_sems.at[kk])

        @pl.when(s == 0)
        def _():
            for kk in range(nk):
                xn_copy(kk).start()
            xn_copy(0).wait()

        x, y, c = _mesh_pos()
        sib = (x, y, 1 - c)
        y_nbr, x_nbr = (x, 1 - y, c), (1 - x, y, c)
        half_rows = D_MODEL // 2
        gather = _TwoLevelGather(list(sm_refs), [functools.partial(lambda r, dev: r.at[dev], r) for r in rsm_refs],
                                 sm_send, sm_recv, sm_loc)

        def half(i, core):
            return acc.at[i % 2, :, pl.ds(pl.multiple_of(core * COLS_PER_DEV, LANES), COLS_PER_DEV)]

        def give(i):
            return pltpu.make_async_remote_copy(src_ref=half(i, 1 - c), dst_ref=stage.at[i], send_sem=give_send.at[i],
                                                recv_sem=give_recv.at[i], device_id=sib, device_id_type=MESH)

        def relay(r):
            rows = pl.ds(r * half_rows, half_rows)
            return pltpu.make_async_remote_copy(src_ref=sbuf.at[0, rows, :], dst_ref=relay_in.at[r],
                                                send_sem=relay_send.at[r], recv_sem=relay_recv.at[r],
                                                device_id=(x_nbr, y_nbr)[r], device_id_type=MESH)

        def keep(i):
            return pltpu.make_async_remote_copy(src_ref=sbuf.at[i], dst_ref=rchip_ref.at[i - 1],
                                                send_sem=keep_send.at[i - 1], recv_sem=keep_recv.at[i - 1],
                                                device_id=(None, y_nbr, x_nbr)[i], device_id_type=MESH)

        def chip_sum(i):
            give(i).wait_recv()
            mine = [acc[i % 2, :, cc * COLS_PER_DEV:(cc + 1) * COLS_PER_DEV] for cc in range(2)]
            return jnp.where(c == 0, mine[0], mine[1]) + stage[i]

        @pl.when(s == 0)
        def _():
            gather.start()

        @pl.when(s == 1)
        def _():
            gather.neighbours_landed()

        @pl.when(s == N_CHIP - 1)
        def _():
            gather.diagonal_landed()

        for k in range(2, N_CHIP):
            @pl.when(s == k)
            def _(k=k):
                give(k - 2).wait_send()

        slot = s % 2
        acc[slot] = _dot_tn(xn_ref[pl.ds(0, tk), :], dp_ref[pl.ds(0, tk), :])

        def kstep(kk, carry):
            @pl.when(s == 0)
            def _():
                xn_copy(kk).wait()

            off = pl.multiple_of(kk * tk, tk)
            acc[slot] += _dot_tn(xn_ref[pl.ds(off, tk), :], dp_ref[pl.ds(off, tk), :])
            return carry

        n_first = max(1, (3 * nk) // 8)
        lax.fori_loop(1, n_first, kstep, 0)
        @pl.when(s == 1)
        def _():
            sbuf[0] = chip_sum(0).astype(BF16)
            relay(0).start()
            relay(1).start()

        for k in (2, 3):
            @pl.when(s == k)
            def _(k=k):
                i = k - 1
                r = i - 1
                total = chip_sum(i)
                relay(r).wait_recv()
                rows = slice(r * half_rows, (r + 1) * half_rows)
                other = slice((1 - r) * half_rows, (2 - r) * half_rows)
                sbuf[i, rows, :] = (total[rows, :] + relay_in[r].astype(F32)).astype(BF16)
                sbuf[i, other, :] = total[other, :].astype(BF16)
                keep(i).start()

        lax.fori_loop(n_first, nk, kstep, 0)

        for k in range(N_CHIP):
            @pl.when(s == k)
            def _(k=k):
                give(k).start()

        @pl.when(s == N_CHIP - 1)
        def _():
            own_ref[...] = chip_sum(N_CHIP - 1)
            give(N_CHIP - 2).wait_send()
            give(N_CHIP - 1).wait_send()
            for r in range(2):
                relay(r).wait_send()
            for i in (1, 2):
                keep(i).wait()
            gather.finish()

    grid_spec = pltpu.PrefetchScalarGridSpec(
        num_scalar_prefetch=1, grid=(N_CHIP,),
        in_specs=[HBM_SPEC,
                  pl.BlockSpec((n, COLS_PER_CHIP), lambda s, order: (0, order[s])),
                  *([HBM_SPEC] * n_small)],
        out_specs=(pl.BlockSpec(piece, lambda s, order: (0, 0)), HBM_SPEC, *([HBM_SPEC] * n_small)),
        scratch_shapes=[pltpu.VMEM((n, D_MODEL), BF16),
                        pltpu.VMEM((2, D_MODEL, COLS_PER_CHIP), F32), pltpu.VMEM((4,) + piece, F32),
                        pltpu.VMEM((3,) + piece, BF16), pltpu.VMEM((2, D_MODEL // 2, COLS_PER_DEV), BF16),
                        pltpu.SemaphoreType.DMA((nk,)),
                        pltpu.SemaphoreType.DMA((4,)), pltpu.SemaphoreType.DMA((4,)),
                        pltpu.SemaphoreType.DMA((2,)), pltpu.SemaphoreType.DMA((2,)),
                        pltpu.SemaphoreType.DMA((2,)), pltpu.SemaphoreType.DMA((2,)),
                        pltpu.SemaphoreType.DMA((7 * n_small,)), pltpu.SemaphoreType.DMA((7 * n_small,)),
                        pltpu.SemaphoreType.DMA((n_small,))])
    return _pcall(
        body, name="dw_in_exchange", grid_spec=grid_spec,
        out_shape=(_out(piece, F32), _out((2,) + piece, BF16),
                   *(_out((N_DEV,) + a.shape, a.dtype) for a in smalls)),
        compiler_params=_params(1),
    )(order, xn, dproj, *smalls)


def _adamw(g, w, m, v):
    m_new = ADAM_B1 * m + (1.0 - ADAM_B1) * g
    v_new = ADAM_B2 * v + (1.0 - ADAM_B2) * (g * g)
    m_hat = m_new / (1.0 - ADAM_B1 ** ADAM_STEP)
    v_hat = v_new / (1.0 - ADAM_B2 ** ADAM_STEP)
    delta = -ADAM_LR * (m_hat / (jnp.sqrt(v_hat) + ADAM_EPS) + ADAM_WD * w)
    return delta, m_new, v_new


def _reduce_adam_w_in(own, rchip, w, m, v):
    rows, cols = w.shape
    row_tile = 256

    def body(o_ref, r_ref, w_ref, m_ref, v_ref, g_ref, d_ref, nm_ref, nv_ref):
        g = o_ref[...]
        for s in range(2):
            g = g + r_ref[s].astype(F32)
        g_ref[...] = g
        d_ref[...], nm_ref[...], nv_ref[...] = _adamw(g, w_ref[...], m_ref[...], v_ref[...])

    tile = pl.BlockSpec((row_tile, cols), lambda i: (i, 0))
    shp = _out((rows, cols), F32)
    return _pcall(
        body, name="reduce_adam_w_in", grid=(rows // row_tile,),
        out_shape=(shp,) * 4,
        in_specs=[tile, pl.BlockSpec((2, row_tile, cols), lambda i: (0, i, 0)), tile, tile, tile],
        out_specs=(tile,) * 4,
        compiler_params=_params(1),
    )(own, rchip, w, m, v)


_SMALL_LEAVES = ("norm_gain", "final_norm_gain", "b_glu", "ssm_a_re", "ssm_a_im", "ssm_log_dt", "ssm_d", "conv_w",
                 "ssm_c_re", "ssm_c_im", "ssm_b_re", "ssm_b_im")


def _reduce_adam_small(r_pack, r_gc, r_gb, wmv, sharded):
    n_leaf = len(_SMALL_LEAVES)
    n_sh = len(sharded)

    def body(*refs):
        rp_ref, rgc_ref, rgb_ref = refs[:3]
        w_refs = refs[3:3 + 3 * n_leaf]
        sh_in = refs[3 + 3 * n_leaf:3 + 3 * n_leaf + 4 * n_sh]
        outs0 = 3 + 3 * n_leaf + 4 * n_sh
        loss_ref = refs[outs0]
        o_refs = refs[outs0 + 1:outs0 + 1 + 4 * n_leaf]
        sh_out = refs[outs0 + 1 + 4 * n_leaf:outs0 + 1 + 4 * n_leaf + 4 * n_sh]
        own_conv = refs[-1]

        def total(ref):
            acc = ref[0].astype(F32)
            for s in range(1, N_DEV):
                acc = acc + ref[s].astype(F32)
            return acc

        for i in range(n_sh):
            r_ref, w_ref, m_ref, v_ref = sh_in[4 * i:4 * i + 4]
            o_g, o_d, o_m, o_v = sh_out[4 * i:4 * i + 4]
            g = total(r_ref)
            o_g[...] = g
            o_d[...], o_m[...], o_v[...] = _adamw(g, w_ref[...], m_ref[...], v_ref[...])

        sp = total(rp_ref)
        sgc = total(rgc_ref)
        sgb = total(rgb_ref)
        loss_ref[...] = sp[ROW_LOSS:ROW_LOSS + SUBLANES, 0:LANES]

        def wide(r):
            return jnp.concatenate([sp[r:r + 1, :], sp[r + 1:r + 2, :]], axis=1)

        s5 = slice(ROW_S5, ROW_S5 + N_GROUPS)
        eye = (lax.broadcasted_iota(jnp.int32, (N_GROUPS, N_GROUPS), 0)
               == lax.broadcasted_iota(jnp.int32, (N_GROUPS, N_GROUPS), 1)).astype(F32)
        d_row = sp[ROW_BGLU_D + 1:ROW_BGLU_D + 2, :]
        me = 4 * lax.axis_index("x") + 2 * lax.axis_index("y") + lax.axis_index("c")
        for k in range(N_DEV):
            @pl.when(me == k)
            def _(k=k):
                own_conv[...] = sp[ROW_CONV:ROW_CONV + SUBLANES, k * CONV_COLS_PER_DEV:(k + 1) * CONV_COLS_PER_DEV]
        grads = {
            "norm_gain": wide(ROW_NORM_GAIN),
            "final_norm_gain": wide(ROW_FINAL_GAIN),
            "b_glu": sp[ROW_BGLU_D:ROW_BGLU_D + 1, :],
            "ssm_a_re": sp[s5, LANE_A_RE:LANE_A_RE + STATE],
            "ssm_a_im": sp[s5, LANE_A_IM:LANE_A_IM + STATE],
            "ssm_log_dt": jnp.sum(sp[s5, LANE_LOG_DT:LANE_LOG_DT + 1] * eye, axis=0, keepdims=True),
            "ssm_d": jnp.concatenate([d_row[:, g * GROUP:(g + 1) * GROUP] for g in range(N_GROUPS)], axis=0),
            "conv_w": own_conv[0:3, :],
            "ssm_c_re": sgc[:, 0:STATE],
            "ssm_c_im": sgc[:, STATE:2 * STATE],
            "ssm_b_re": sgb[:, 0:STATE],
            "ssm_b_im": sgb[:, STATE:2 * STATE],
        }
        for i, name in enumerate(_SMALL_LEAVES):
            g = grads[name]
            w_ref, m_ref, v_ref = w_refs[3 * i:3 * i + 3]
            o_g, o_d, o_m, o_v = o_refs[4 * i:4 * i + 4]
            o_g[...] = g
            o_d[...], o_m[...], o_v[...] = _adamw(g, w_ref[...], m_ref[...], v_ref[...])

    flat_w = [a for name in _SMALL_LEAVES for a in wmv[name]]
    leaf_shapes = [_out(wmv[name][0].shape, F32) for name in _SMALL_LEAVES for _ in range(4)]
    sh_shapes = [_out(entry[1].shape, F32) for entry in sharded for _ in range(4)]
    operands = (r_pack, r_gc, r_gb, *flat_w, *(a for entry in sharded for a in entry))
    out_shape = (_out((SUBLANES, LANES), F32), *leaf_shapes, *sh_shapes)
    outs = _pcall(
        body, name="reduce_adam_small", grid=(1,), out_shape=out_shape,
        in_specs=_whole_specs(operands), out_specs=tuple(_whole_specs(out_shape)),
        scratch_shapes=[pltpu.VMEM((SUBLANES, CONV_COLS_PER_DEV), F32)],
        compiler_params=_params(1),
    )(*operands)
    leaves = {name: outs[1 + 4 * i:5 + 4 * i] for i, name in enumerate(_SMALL_LEAVES)}
    first = 1 + 4 * n_leaf
    return outs[0], leaves, [outs[first + 4 * i:first + 4 * i + 4] for i in range(n_sh)]


def kernel(x, norm_gain, w_in, ssm_a_re, ssm_a_im, ssm_log_dt, ssm_b_re, ssm_b_im, ssm_c_re, ssm_c_im, ssm_d, w_glu, b_glu, conv_w, w_out, final_norm_gain, loss_target, m_norm_gain, m_w_in, m_ssm_a_re, m_ssm_a_im, m_ssm_log_dt, m_ssm_b_re, m_ssm_b_im, m_ssm_c_re, m_ssm_c_im, m_ssm_d, m_w_glu, m_b_glu, m_conv_w, m_w_out, m_final_norm_gain, v_norm_gain, v_w_in, v_ssm_a_re, v_ssm_a_im, v_ssm_log_dt, v_ssm_b_re, v_ssm_b_im, v_ssm_c_re, v_ssm_c_im, v_ssm_d, v_w_glu, v_b_glu, v_conv_w, v_w_out, v_final_norm_gain):
    n_seq, seq, _ = x.shape
    n = n_seq * seq

    gh_p = lambda b4: jnp.transpose(b4, (0, 1, 3, 2)).reshape(N_GROUPS * GROUP, STATE)
    c2 = lambda a: a.reshape(N_GROUPS * GROUP, STATE)
    b_re2, b_im2 = gh_p(ssm_b_re), gh_p(ssm_b_im)
    d_row = ssm_d[0].reshape(1, SSM_W)

    x2 = x.reshape(n, D_MODEL)
    tgt2 = loss_target.reshape(n, D_MODEL)
    mx, my, mc = lax.axis_index("x"), lax.axis_index("y"), lax.axis_index("c")
    chip_ids = [2 * cx + cy for cx, cy in ((mx, my), (1 - mx, my), (mx, 1 - my), (1 - mx, 1 - my))]
    arrival = chip_ids
    xn, proj, w_in_f, s5 = _in_proj(
        jnp.stack(arrival).astype(jnp.int32), x2, norm_gain, w_in[0].astype(BF16),
        (ssm_a_re[0], ssm_a_im[0], ssm_log_dt, b_re2, b_im2, c2(ssm_c_re), c2(ssm_c_im)))
    a_re_x, a_im_x, log_dt_x, ab_re, ab_im, bb_re_m, bb_im_m, c_re_m, c_imn_m = s5
    u3 = proj.reshape(n_seq, seq, IN_COLS)
    conv_p = jnp.pad(conv_w[0], ((0, SUBLANES - 3), (0, LANES - CONV_COLS_PER_DEV)))
    s_re, s_im, y3, w_out_f, w_glu_f, conv_all = _ssm_fwd(
        u3, bb_re_m, bb_im_m, c_re_m, c_imn_m, d_row, ab_re, ab_im,
        w_out[0].astype(BF16), w_glu[0].astype(BF16), conv_p, n_seq, seq)
    conv8 = jnp.transpose(conv_all[:, :, :CONV_COLS_PER_DEV], (1, 0, 2)).reshape(SUBLANES, CONV_W)
    (dh2, dy, dzs, dbc, dzc, dyc, dw_out, dw_glu, loss_t, dgf, dbg, dcw) = _mix(
        x2, tgt2, y3.reshape(n, SSM_W), proj, final_norm_gain.reshape(1, D_MODEL), b_glu, conv8,
        w_glu_f, w_out_f, seq)

    du3, dc_re_d, dc_im_d, dbb_re_d, dbb_im_d, dab_re, dab_im, dd, r_out, r_glu = _ssm_bwd(
        dy.reshape(n_seq, seq, SSM_W), u3, s_re, s_im, bb_re_m, bb_im_m, c_re_m, c_imn_m, d_row, ab_re, ab_im,
        dw_out.reshape(N_DEV, OUT_ROWS_PER_DEV, D_MODEL), dw_glu.reshape(N_DEV, GLU_ROWS_PER_DEV, SSM_W), n_seq, seq)
    du = du3.reshape(n, SSM_W)
    grad_x2, dproj, dg8 = _in_bwd(x2, dh2, du, dzs, dyc, proj, dbc, dzc, norm_gain, conv8, w_in_f, seq)
    pack, gc, gb = _ssm_disc_bwd_pack(
        a_re_x, a_im_x, log_dt_x, b_re2, b_im2, dab_re.reshape(N_GROUPS, STATE), dab_im.reshape(N_GROUPS, STATE),
        dbb_re_d, dbb_im_d, loss_t, dg8, dgf, dbg, dd, dcw, dc_re_d, dc_im_d)

    order = [chip_ids[3], chip_ids[2], chip_ids[1], chip_ids[0]]
    own_in, rchip_in, r_pack, r_gc, r_gb = _dw_in_exchange(
        jnp.stack(order).astype(jnp.int32), xn, dproj, [pack, gc, gb])

    flat2 = lambda a: a.reshape(a.shape[-2:]) if a.ndim > 2 else a.reshape(1, -1)
    c2 = lambda a: a.reshape(N_GROUPS * GROUP, STATE)
    wmv = dict(norm_gain=(norm_gain, m_norm_gain, v_norm_gain),
               final_norm_gain=tuple(flat2(a) for a in (final_norm_gain, m_final_norm_gain, v_final_norm_gain)),
               b_glu=(b_glu, m_b_glu, v_b_glu),
               ssm_a_re=tuple(flat2(a) for a in (ssm_a_re, m_ssm_a_re, v_ssm_a_re)),
               ssm_a_im=tuple(flat2(a) for a in (ssm_a_im, m_ssm_a_im, v_ssm_a_im)),
               ssm_log_dt=(ssm_log_dt, m_ssm_log_dt, v_ssm_log_dt),
               ssm_d=tuple(flat2(a) for a in (ssm_d, m_ssm_d, v_ssm_d)),
               conv_w=tuple(flat2(a) for a in (conv_w, m_conv_w, v_conv_w)),
               ssm_c_re=tuple(c2(a) for a in (ssm_c_re, m_ssm_c_re, v_ssm_c_re)),
               ssm_c_im=tuple(c2(a) for a in (ssm_c_im, m_ssm_c_im, v_ssm_c_im)),
               ssm_b_re=(b_re2, gh_p(m_ssm_b_re), gh_p(v_ssm_b_re)),
               ssm_b_im=(b_im2, gh_p(m_ssm_b_im), gh_p(v_ssm_b_im)))

    res_in = _reduce_adam_w_in(own_in, rchip_in, w_in[0], m_w_in[0], v_w_in[0])
    loss8, small, (res_out, res_glu) = _reduce_adam_small(
        r_pack, r_gc, r_gb, wmv,
        [(r_out, w_out[0], m_w_out[0], v_w_out[0]), (r_glu, w_glu[0], m_w_glu[0], v_w_glu[0])])
    loss = loss8[0, 0]

    shapes = dict(norm_gain=(1, D_MODEL), ssm_a_re=(1, N_GROUPS, STATE), ssm_a_im=(1, N_GROUPS, STATE),
                  ssm_log_dt=(1, N_GROUPS), ssm_c_re=(1, N_GROUPS, GROUP, STATE), ssm_c_im=(1, N_GROUPS, GROUP, STATE),
                  ssm_d=(1, N_GROUPS, GROUP), b_glu=(1, SSM_W), final_norm_gain=(D_MODEL,),
                  conv_w=(1, 3, CONV_COLS_PER_DEV))
    big = dict(w_in=res_in, w_glu=res_glu, w_out=res_out)

    def leaf(kind, name):
        if name in big:
            return big[name][kind][None]
        if name in ("ssm_b_re", "ssm_b_im"):
            return jnp.transpose(small[name][kind].reshape(1, N_GROUPS, GROUP, STATE), (0, 1, 3, 2))
        return small[name][kind].reshape(shapes[name])

    order = ["norm_gain", "w_in", "ssm_a_re", "ssm_a_im", "ssm_log_dt", "ssm_b_re", "ssm_b_im", "ssm_c_re",
             "ssm_c_im", "ssm_d", "w_glu", "b_glu", "conv_w", "w_out", "final_norm_gain"]
    outs = [loss, grad_x2.reshape(x.shape)]
    for kind in range(4):
        outs += [leaf(kind, name) for name in order]
    return tuple(outs)
```

```python
import functools
import math

import jax
import jax.numpy as jnp
from jax import lax
from jax.experimental import pallas as pl
from jax.experimental.pallas import tpu as pltpu

F32 = jnp.float32
BF16 = jnp.bfloat16

N_DEV = 8
D_MODEL = 1024
SSM_W = 512
CONV_W = 512
N_GROUPS = 32
GROUP = 16
STATE = 64
IN_COLS = 3072
SEG_U, SEG_ZS, SEG_H, SEG_BC, SEG_CC, SEG_ZC = range(6)
COLS_PER_DEV = IN_COLS // N_DEV
N_CHIP = N_DEV // 2
COLS_PER_CHIP = 2 * COLS_PER_DEV
OUT_ROWS_PER_DEV = D_MODEL // N_DEV
GLU_ROWS_PER_DEV = SSM_W // N_DEV
CONV_COLS_PER_DEV = CONV_W // N_DEV
EPS = 1e-6

N_JBLK = 4
JB_CH = SSM_W // N_JBLK
JB_ST = N_GROUPS * STATE // N_JBLK

ADAM_LR = 0.001
ADAM_B1 = 0.9
ADAM_B2 = 0.999
ADAM_EPS = 1e-08
ADAM_WD = 0.01
ADAM_STEP = 10

SUBLANES = 8
LANES = 128
VMEM_LIMIT = 48 * 1024 * 1024
TOK_TILE = 256
IN_TILE = 1024
SCAN_TILE = 1024

MESH = pl.DeviceIdType.MESH
HBM_SPEC = pl.BlockSpec(memory_space=pltpu.HBM)


def _build(body, **kw):
    return pl.pallas_call(body, **kw)


def _pcall(body, **kw):
    def call(*operands):
        pinned = [a if jnp.issubdtype(a.dtype, jnp.integer) else pltpu.with_memory_space_constraint(a, pltpu.HBM)
                  for a in operands]
        return _build(body, **kw)(*pinned)
    return call


def _whole_specs(arrays):
    return [pl.BlockSpec(a.shape, functools.partial(lambda nd, i: (0,) * nd, len(a.shape))) for a in arrays]


def _out(shape, dtype):
    return pltpu.HBM(tuple(shape), dtype)


def _params(n_grid):
    return pltpu.CompilerParams(dimension_semantics=("arbitrary",) * n_grid,
                                vmem_limit_bytes=VMEM_LIMIT)


def _dot(a, b):
    return jnp.dot(a, b, preferred_element_type=F32)


def _dot_nt(a, b):
    return lax.dot_general(a, b, (((1,), (1,)), ((), ())), preferred_element_type=F32)


def _dot_tn(a, b):
    return lax.dot_general(a, b, (((0,), (0,)), ((), ())), preferred_element_type=F32)


def _sigmoid(z):
    return 1.0 / (1.0 + jnp.exp(-z))


_GELU_C = math.sqrt(2.0 / math.pi)


def _gelu_and_grad(y):
    inner = _GELU_C * (y + 0.044715 * (y * y * y))
    t = jnp.tanh(inner)
    g = 0.5 * y * (1.0 + t)
    dg = 0.5 * (1.0 + t) + 0.5 * y * (1.0 - t * t) * (_GELU_C * (1.0 + 3.0 * 0.044715 * (y * y)))
    return g, dg


def _silu_and_grad(z):
    s = _sigmoid(z)
    return z * s, s * (1.0 + z * (1.0 - s))


def _shift_down(v, halo, k):
    rolled = pltpu.roll(v, k, 0)
    row = lax.broadcasted_iota(jnp.int32, v.shape, 0)
    for r in range(k):
        rolled = jnp.where(row == r, halo[SUBLANES - k + r:SUBLANES - k + r + 1, :], rolled)
    return rolled


def _shift_up(v, halo, k):
    n = v.shape[0]
    rolled = pltpu.roll(v, n - k, 0)
    row = lax.broadcasted_iota(jnp.int32, v.shape, 0)
    for r in range(k):
        rolled = jnp.where(row == n - k + r, halo[r:r + 1, :], rolled)
    return rolled


def _mesh_pos():
    return lax.axis_index("x"), lax.axis_index("y"), lax.axis_index("c")


def _direct_copies(srcs_for, out_refs, send_sems, recv_sems, loc_sems):
    x, y, c = _mesh_pos()
    me_id = 4 * x + 2 * y + c
    n_arr = len(out_refs)
    dsts = [r.at[me_id] for r in out_refs]
    own = srcs_for(me_id)
    mine = [pltpu.make_async_copy(own[a], dsts[a], loc_sems.at[a]) for a in range(n_arr)]
    sends = []
    for k in range(1, N_DEV):
        px, py, pc = x ^ ((k >> 2) & 1), y ^ ((k >> 1) & 1), c ^ (k & 1)
        src = srcs_for(4 * px + 2 * py + pc)
        for a in range(n_arr):
            sends.append(pltpu.make_async_remote_copy(
                src_ref=src[a], dst_ref=dsts[a],
                send_sem=send_sems.at[(k - 1) * n_arr + a], recv_sem=recv_sems.at[(k - 1) * n_arr + a],
                device_id=(px, py, pc), device_id_type=MESH))
    return mine, sends


class _TwoLevelGather:
    def __init__(self, srcs, slots, send_sems, recv_sems, loc_sems):
        self.srcs, self.slots, self.n_arr = srcs, slots, len(srcs)
        self.send_sems, self.recv_sems, self.loc_sems = send_sems, recv_sems, loc_sems
        x, y, c = _mesh_pos()
        self.c = c
        self.me, self.sib = (x, y, c), (x, y, 1 - c)
        self.chips = [(1 - x, y), (x, 1 - y), (1 - x, 1 - y)]

    def _copies(self, k, block, to, from_src=False):
        dev = 4 * block[0] + 2 * block[1] + block[2]
        return [pltpu.make_async_remote_copy(
            src_ref=self.srcs[a] if from_src else self.slots[a](dev), dst_ref=self.slots[a](dev),
            send_sem=self.send_sems.at[k * self.n_arr + a], recv_sem=self.recv_sems.at[k * self.n_arr + a],
            device_id=to, device_id_type=MESH) for a in range(self.n_arr)]

    def _local(self):
        dev = 4 * self.me[0] + 2 * self.me[1] + self.me[2]
        return [pltpu.make_async_copy(self.srcs[a], self.slots[a](dev), self.loc_sems.at[a])
                for a in range(self.n_arr)]

    def start(self):
        for cp in self._local() + self._copies(0, self.me, self.sib, True):
            cp.start()
        for j in (0, 1):
            for cp in self._copies(1 + j, self.me, (*self.chips[j], self.c), True):
                cp.start()

    def wait_own(self):
        for cp in self._local():
            cp.wait()

    def wait_sibling(self):
        for cp in self._copies(0, self.sib, self.me):
            cp.wait_recv()

    def wait_and_pass_on(self, j):
        chip = self.chips[j]
        for cp in self._copies(1 + j, (*chip, self.c), self.me):
            cp.wait_recv()
        for cp in self._copies(4 + j, (*chip, self.c), self.sib):
            cp.start()

    def neighbours_landed(self):
        x, y, c = self.me
        self.wait_and_pass_on(0)
        self.wait_and_pass_on(1)
        for cp in self._copies(1 + 2, (x ^ c, y ^ (1 - c), c), (x ^ (1 - c), y ^ c, c)):
            cp.start()

    def diagonal_landed(self):
        self.wait_and_pass_on(2)

    def wait_passed_on(self, j):
        for cp in self._copies(4 + j, (*self.chips[j], 1 - self.c), self.me):
            cp.wait_recv()

    def wait_sends(self):
        for cp in self._copies(0, self.me, self.sib, True):
            cp.wait_send()
        for j, chip in enumerate(self.chips):
            for cp in self._copies(1 + j, self.me, (*chip, self.c), True) + self._copies(4 + j, (*chip, self.c), self.sib):
                cp.wait_send()

    def finish(self):
        self.wait_sibling()
        for j in range(3):
            self.wait_passed_on(j)
        self.wait_sends()
        self.wait_own()


def _disc(a_re, a_im, log_dt, b_re, b_im):
    dt = jnp.exp(log_dt)
    mag = jnp.exp(a_re * dt)
    ab_re = mag * jnp.cos(a_im * dt)
    ab_im = mag * jnp.sin(a_im * dt)
    den = a_re * a_re + a_im * a_im
    p_re = ab_re - 1.0
    p_im = ab_im
    q_re = (p_re * a_re + p_im * a_im) / den
    q_im = (p_im * a_re - p_re * a_im) / den
    bb_re = q_re * b_re - q_im * b_im
    bb_im = q_re * b_im + q_im * b_re
    return ab_re, ab_im, bb_re, bb_im


def _split3(v):
    hi = v.astype(BF16)
    r1 = v - hi.astype(F32)
    mid = r1.astype(BF16)
    lo = (r1 - mid.astype(F32)).astype(BF16)
    return hi, mid, lo


def _select_dot(sel, v):
    return sum(_dot(sel, t) for t in _split3(v))


PACK_ROWS = 72
PACK_W = 512
ROW_FINAL_GAIN, ROW_NORM_GAIN, ROW_BGLU_D, ROW_CONV, ROW_LOSS, ROW_S5 = 0, 8, 16, 24, 32, 40
LANE_A_RE, LANE_A_IM, LANE_LOG_DT = 0, 128, 256


def _ssm_disc_bwd_pack(a_re_x, a_im_x, log_dt_x, b_re, b_im, g_ab_re, g_ab_im, dbb_re_d, dbb_im_d,
                       loss_t, dg8, dgf, dbg, dd, dcw, dc_re_d, dc_im_d):
    rows_gh = N_GROUPS * GROUP

    def body(are, aim, ldt, bre, bim, gabre, gabim, dbbre_ref, dbbim_ref,
             loss_ref, dg8_ref, dgf_ref, dbg_ref, dd_ref, dcw_ref, dcre_ref, dcim_ref,
             p_ref, gc_ref, gb_ref, gbb_re, gbb_im):
        r_g = lax.broadcasted_iota(jnp.int32, (N_GROUPS, rows_gh), 0)
        c_gh = lax.broadcasted_iota(jnp.int32, (N_GROUPS, rows_gh), 1)
        group_sum = (c_gh // GROUP == r_g).astype(BF16)
        r_gh = lax.broadcasted_iota(jnp.int32, (rows_gh, N_GROUPS), 0)
        c_g = lax.broadcasted_iota(jnp.int32, (rows_gh, N_GROUPS), 1)
        first_row = (r_gh == c_g * GROUP).astype(BF16)

        def diag_block(ref, j, gi):
            return ref[j, gi * GROUP:(gi + 1) * GROUP, gi * STATE:(gi + 1) * STATE]

        for j in range(N_JBLK):
            for gi in range(SUBLANES):
                r0 = (j * SUBLANES + gi) * GROUP
                gbb_re[r0:r0 + GROUP, :] = diag_block(dbbre_ref, j, gi)
                gbb_im[r0:r0 + GROUP, :] = diag_block(dbbim_ref, j, gi)
                both = jnp.concatenate([diag_block(dcre_ref, j, gi), -diag_block(dcim_ref, j, gi)], axis=1)
                gc_ref[r0:r0 + GROUP, :] = both.astype(BF16)

        _, vjp = jax.vjp(_disc, are[...], aim[...], ldt[...], bre[...], bim[...])
        d_are, d_aim, d_ldt, d_bre, d_bim = vjp((_select_dot(first_row, gabre[...]), _select_dot(first_row, gabim[...]),
                                                 gbb_re[...], gbb_im[...]))
        gb_ref[...] = jnp.concatenate([d_bre, d_bim], axis=1).astype(BF16)

        p_ref[...] = jnp.zeros_like(p_ref)
        half = D_MODEL // 2
        for r, src in ((ROW_FINAL_GAIN, dgf_ref), (ROW_NORM_GAIN, dg8_ref)):
            p_ref[r:r + 1, :] = src[0:1, 0:half]
            p_ref[r + 1:r + 2, :] = src[0:1, half:D_MODEL]
        p_ref[ROW_BGLU_D:ROW_BGLU_D + 1, :] = dbg_ref[...]
        p_ref[ROW_BGLU_D + 1:ROW_BGLU_D + 2, :] = dd_ref[...]
        p_ref[ROW_CONV:ROW_CONV + SUBLANES, :] = dcw_ref[...]
        p_ref[ROW_LOSS:ROW_LOSS + SUBLANES, 0:LANES] = loss_ref[...]
        s5 = slice(ROW_S5, ROW_S5 + N_GROUPS)
        p_ref[s5, LANE_A_RE:LANE_A_RE + STATE] = _select_dot(group_sum, d_are)
        p_ref[s5, LANE_A_IM:LANE_A_IM + STATE] = _select_dot(group_sum, d_aim)
        p_ref[s5, LANE_LOG_DT:LANE_LOG_DT + LANES] = _select_dot(group_sum, jnp.broadcast_to(d_ldt, (rows_gh, LANES)))

    operands = (a_re_x, a_im_x, log_dt_x, b_re, b_im, g_ab_re, g_ab_im, dbb_re_d, dbb_im_d,
                loss_t, dg8, dgf, dbg, dd, dcw, dc_re_d, dc_im_d)
    out_shape = (_out((PACK_ROWS, PACK_W), F32),
                 _out((rows_gh, 2 * STATE), BF16),
                 _out((rows_gh, 2 * STATE), BF16))
    return _pcall(body, name="ssm_disc_bwd_pack", grid=(1,), out_shape=out_shape,
                  in_specs=_whole_specs(operands), out_specs=tuple(_whole_specs(out_shape)),
                  scratch_shapes=[pltpu.VMEM((rows_gh, STATE), F32), pltpu.VMEM((rows_gh, STATE), F32)],
                  compiler_params=_params(1))(*operands)


def _s5_prepare(are, aim, ldt, bre, bim, cre, cim,
                o_ax_re, o_ax_im, o_ldt_x, o_ab_re, o_ab_im, o_bb_re, o_bb_im, o_c_re, o_c_imn):
    rows_gh = N_GROUPS * GROUP
    rep = (lax.broadcasted_iota(jnp.int32, (rows_gh, N_GROUPS), 0) // GROUP
           == lax.broadcasted_iota(jnp.int32, (rows_gh, N_GROUPS), 1)).astype(BF16)
    eye = (lax.broadcasted_iota(jnp.int32, (N_GROUPS, N_GROUPS), 0)
           == lax.broadcasted_iota(jnp.int32, (N_GROUPS, N_GROUPS), 1)).astype(F32)
    ldt_col = jnp.sum(eye * ldt[...], axis=1, keepdims=True)
    a_re_x = _select_dot(rep, are[...])
    a_im_x = _select_dot(rep, aim[...])
    ldt_x = _select_dot(rep, jnp.broadcast_to(ldt_col, (N_GROUPS, LANES)))[:, 0:1]
    o_ax_re[...] = a_re_x
    o_ax_im[...] = a_im_x
    o_ldt_x[...] = ldt_x
    ab_re, ab_im, bb_re, bb_im = _disc(a_re_x, a_im_x, ldt_x, bre[...], bim[...])
    for j in range(N_JBLK):
        first = [(j * SUBLANES + gi) * GROUP for gi in range(SUBLANES)]
        o_ab_re[j] = jnp.concatenate([ab_re[r:r + 1, :] for r in first], axis=1)
        o_ab_im[j] = jnp.concatenate([ab_im[r:r + 1, :] for r in first], axis=1)
    for o, v in ((o_bb_re, bb_re), (o_bb_im, bb_im), (o_c_re, cre[...]), (o_c_imn, -cim[...])):
        for j in range(N_JBLK):
            for gi in range(SUBLANES):
                r0 = (j * SUBLANES + gi) * GROUP
                parts = [v[r0:r0 + GROUP, :] if k == gi else jnp.zeros((GROUP, STATE), F32) for k in range(SUBLANES)]
                o[j, gi * GROUP:(gi + 1) * GROUP, :] = jnp.concatenate(parts, axis=1).astype(BF16)


def _in_proj(order, x2, g1, w_in_b, s5):
    n = x2.shape[0]
    tm = min(IN_TILE, n)
    n_tiles = n // tm
    n_s5_in = len(s5)
    n_s5_out = 9

    def body(order_ref, x_ref, g_ref, w_ref, *refs):
        s5_in = refs[:n_s5_in]
        xn_ref, proj_ref, wall_ref = refs[n_s5_in:n_s5_in + 3]
        s5_out = refs[n_s5_in + 3:n_s5_in + 3 + n_s5_out]
        xn_scr, wbuf, send_sems, recv_sems, loc_sems, out_sems = refs[n_s5_in + 3 + n_s5_out:]
        k = pl.program_id(0)
        i = pl.program_id(1)

        def slot(dev):
            return wbuf.at[dev // 2, :, pl.ds(pl.multiple_of((dev % 2) * COLS_PER_DEV, LANES), COLS_PER_DEV)]

        gather = _TwoLevelGather([w_ref], [slot], send_sems, recv_sems, loc_sems)

        @pl.when((k == 0) & (i == 0))
        def _():
            gather.start()

        def own_chip():
            gather.wait_own()
            gather.wait_sibling()

        def x_chip():
            gather.neighbours_landed()
            gather.wait_passed_on(0)

        def diag_chip():
            gather.diagonal_landed()
            gather.wait_passed_on(2)

        arrivals = [own_chip, x_chip, functools.partial(gather.wait_passed_on, 1), diag_chip]
        for kk, arrived in enumerate(arrivals):
            @pl.when((k == kk) & (i == 0))
            def _(arrived=arrived):
                arrived()

        rows = pl.ds(pl.multiple_of(i * tm, tm), tm)

        @pl.when(k == 0)
        def _():
            x = x_ref[...]
            r = lax.rsqrt(jnp.mean(x * x, axis=-1, keepdims=True) + EPS)
            xn = ((x * r) * g_ref[...]).astype(BF16)
            xn_scr[rows, :] = xn
            xn_ref[...] = xn

        proj_ref[...] = _dot(xn_scr[rows, :], wbuf[order_ref[k]])

        @pl.when((k == 0) & (i == n_tiles - 1))
        def _():
            _s5_prepare(*s5_in, *s5_out)

        @pl.when((k == N_CHIP - 1) & (i == n_tiles - 1))
        def _():
            gather.wait_sends()
            outs = [pltpu.make_async_copy(wbuf.at[q], wall_ref.at[:, q * COLS_PER_CHIP:(q + 1) * COLS_PER_CHIP],
                                          out_sems.at[q]) for q in range(N_CHIP)]
            for cp in outs:
                cp.start()
            for cp in outs:
                cp.wait()

    tile_once = lambda k, i, order: (jnp.where(k == 0, i, n_tiles - 1), 0)
    whole = lambda shape: pl.BlockSpec(shape, lambda k, i, order: (0,) * len(shape))
    rows_gh = N_GROUPS * GROUP
    s5_out_shapes = ([(rows_gh, STATE), F32], [(rows_gh, STATE), F32], [(rows_gh, 1), F32],
                     [(N_JBLK, 1, JB_ST), F32], [(N_JBLK, 1, JB_ST), F32]) + ([(N_JBLK, JB_CH, JB_ST), BF16],) * 4
    grid_spec = pltpu.PrefetchScalarGridSpec(
        num_scalar_prefetch=1, grid=(N_CHIP, n_tiles),
        in_specs=[pl.BlockSpec((tm, D_MODEL), tile_once),
                  whole((1, D_MODEL)),
                  HBM_SPEC,
                  *(whole(a.shape) for a in s5)],
        out_specs=(pl.BlockSpec((tm, D_MODEL), tile_once),
                   pl.BlockSpec((tm, COLS_PER_CHIP), lambda k, i, order: (i, order[k])),
                   HBM_SPEC,
                   *(whole(shape) for shape, _ in s5_out_shapes)),
        scratch_shapes=[pltpu.VMEM((n, D_MODEL), BF16), pltpu.VMEM((N_CHIP, D_MODEL, COLS_PER_CHIP), BF16),
                        pltpu.SemaphoreType.DMA((7,)), pltpu.SemaphoreType.DMA((7,)), pltpu.SemaphoreType.DMA((1,)),
                        pltpu.SemaphoreType.DMA((N_CHIP,))])
    outs = _pcall(
        body, name="in_proj", grid_spec=grid_spec,
        out_shape=(_out((n, D_MODEL), BF16), _out((n, IN_COLS), F32),
                   _out((D_MODEL, IN_COLS), BF16),
                   *(_out(shape, dt) for shape, dt in s5_out_shapes)),
        compiler_params=_params(2),
    )(order, x2, g1, w_in_b, *s5)
    return outs[0], outs[1], outs[2], outs[3:]


def _cmul(p, q):
    return p[0] * q[0] - p[1] * q[1], p[0] * q[1] + p[1] * q[0]


def _scan_tables(ar, ai, width, reverse):
    pows = [(ar, ai)]
    for _ in range(SUBLANES - 1):
        pows.append(_cmul(pows[-1], (ar, ai)))
    row = lax.broadcasted_iota(jnp.int32, (SUBLANES, width), 0)

    def bc(v):
        return jnp.broadcast_to(v, (SUBLANES, width))

    levels = []
    for k in (1, 2, 4):
        keep = (row <= SUBLANES - 1 - k) if reverse else (row >= k)
        levels.append((jnp.where(keep, bc(pows[k - 1][0]), 0.0), jnp.where(keep, bc(pows[k - 1][1]), 0.0)))
    cre = jnp.zeros((SUBLANES, width), F32)
    cim = jnp.zeros((SUBLANES, width), F32)
    for r in range(SUBLANES):
        e = (SUBLANES - r) if reverse else (r + 1)
        cre = jnp.where(row == r, bc(pows[e - 1][0]), cre)
        cim = jnp.where(row == r, bc(pows[e - 1][1]), cim)
    return levels, (cre, cim)


def _load_chunked(src_ref, b, dst_ref, n_rows):
    n_blk = n_rows // SUBLANES
    for i in range(n_blk):
        dst_ref[b, i * SUBLANES:(i + 1) * SUBLANES, :] = src_ref[b, pl.ds(i, SUBLANES, stride=n_blk), :]


def _store_chunked(val, dst_ref, b, n_rows):
    n_blk = n_rows // SUBLANES
    for i in range(n_blk):
        dst_ref[b, pl.ds(i, SUBLANES, stride=n_blk), :] = val[i * SUBLANES:(i + 1) * SUBLANES, :]


def _chunk_scan(re_ref, im_ref, bs, car_ref, ar, ai, n_rows, reverse, on_block=None):
    width = re_ref.shape[2]
    n_blk = n_rows // SUBLANES
    shape = (SUBLANES, width)
    abr = jnp.broadcast_to(ar, shape)
    abi = jnp.broadcast_to(ai, shape)
    order = list(range(n_blk - 1, -1, -1)) if reverse else list(range(n_blk))

    def blk(ref, b, i):
        return ref[b, i * SUBLANES:(i + 1) * SUBLANES, :]

    def step(state, b, i):
        sr, si = state
        return abr * sr - abi * si + blk(re_ref, b, i), abr * si + abi * sr + blk(im_ref, b, i)

    finals = {b: (blk(re_ref, b, order[0]), blk(im_ref, b, order[0])) for b in bs}
    for i in order[1:]:
        for b in bs:
            finals[b] = step(finals[b], b, i)

    mr, mi = ar, ai
    for _ in range(n_blk.bit_length() - 1):
        mr, mi = _cmul((mr, mi), (mr, mi))
    levels, _ = _scan_tables(mr, mi, width, reverse)
    mbr = jnp.broadcast_to(mr, shape)
    mbi = jnp.broadcast_to(mi, shape)
    row = lax.broadcasted_iota(jnp.int32, shape, 0)
    edge_in = SUBLANES - 1 if reverse else 0
    edge_out = 0 if reverse else SUBLANES - 1
    sh1 = SUBLANES - 1 if reverse else 1
    states = {}
    for b in bs:
        fr, fi = finals[b]
        gr = jnp.where(row == edge_in, jnp.broadcast_to(car_ref[b, 0:1, :], shape), pltpu.roll(fr, sh1, 0))
        gi = jnp.where(row == edge_in, jnp.broadcast_to(car_ref[b, 1:2, :], shape), pltpu.roll(fi, sh1, 0))
        for (lr, li), k in zip(levels, (1, 2, 4)):
            sh = (SUBLANES - k) if reverse else k
            sr = pltpu.roll(gr, sh, 0)
            si = pltpu.roll(gi, sh, 0)
            gr, gi = gr + (lr * sr - li * si), gi + (lr * si + li * sr)
        car_ref[b, 0:1, :] = (fr + (mbr * gr - mbi * gi))[edge_out:edge_out + 1, :]
        car_ref[b, 1:2, :] = (fi + (mbr * gi + mbi * gr))[edge_out:edge_out + 1, :]
        states[b] = (gr, gi)

    for i in order:
        for b in bs:
            states[b] = step(states[b], b, i)
            re_ref[b, i * SUBLANES:(i + 1) * SUBLANES, :] = states[b][0]
            im_ref[b, i * SUBLANES:(i + 1) * SUBLANES, :] = states[b][1]
            if on_block is not None:
                on_block(b, i, *states[b])


def _ssm_fwd(u, bb_re, bb_im, c_re_t, c_imn_t, d_row, ab_re, ab_im, w_out_b, w_glu_b, conv_p, n_seq, seq):
    tt = min(SCAN_TILE, seq)
    nt = seq // tt

    def body(u_ref, bbre, bbim, cre, cimn, d_ref, are, aim, wout_ref, wglu_ref, cw_ref,
             sre_ref, sim_ref, y_ref, oout_ref, oglu_ref, ocw_ref,
             up_ref, car_ref, send_sems, recv_sems, loc_sems):
        j = pl.program_id(0)
        t = pl.program_id(1)
        gather = _TwoLevelGather(
            [wout_ref, wglu_ref, cw_ref],
            [lambda dev: oout_ref.at[pl.ds(pl.multiple_of(dev * OUT_ROWS_PER_DEV, OUT_ROWS_PER_DEV), OUT_ROWS_PER_DEV), :],
             lambda dev: oglu_ref.at[pl.ds(pl.multiple_of(dev * GLU_ROWS_PER_DEV, GLU_ROWS_PER_DEV), GLU_ROWS_PER_DEV), :],
             lambda dev: ocw_ref.at[dev]],
            send_sems, recv_sems, loc_sems)

        @pl.when((j == 0) & (t == 0))
        def _():
            gather.start()

        @pl.when((j == N_JBLK // 2) & (t == 0))
        def _():
            gather.neighbours_landed()

        @pl.when((j == N_JBLK - 1) & (t == 0))
        def _():
            gather.diagonal_landed()

        @pl.when(t == 0)
        def _():
            car_ref[...] = jnp.zeros_like(car_ref)

        bs = list(range(n_seq))
        for b in bs:
            _load_chunked(u_ref, b, up_ref, tt)
        for b in bs:
            ub = up_ref[b].astype(BF16)
            sre_ref[b] = _dot(ub, bbre[0])
            sim_ref[b] = _dot(ub, bbim[0])
            _chunk_scan(sre_ref, sim_ref, [b], car_ref, are[0], aim[0], tt, reverse=False)
        for b in bs:
            yp = (_dot_nt(sre_ref[b].astype(BF16), cre[0]) + _dot_nt(sim_ref[b].astype(BF16), cimn[0])
                  + d_ref[...] * up_ref[b])
            _store_chunked(yp, y_ref, b, tt)

        @pl.when((j == N_JBLK - 1) & (t == nt - 1))
        def _():
            gather.finish()

    tok = lambda j, t: (0, t, j)
    blk3 = lambda j, t: (j, 0, 0)
    row = lambda j, t: (0, j)
    st = _out((n_seq, seq, N_JBLK * JB_ST), F32)
    n_arr = 3
    return _pcall(
        body, name="ssm_fwd", grid=(N_JBLK, nt),
        out_shape=(st, st, _out((n_seq, seq, SSM_W), F32),
                   _out((D_MODEL, D_MODEL), BF16), _out((SSM_W, SSM_W), BF16),
                   _out((N_DEV, SUBLANES, LANES), F32)),
        in_specs=[pl.BlockSpec((n_seq, tt, JB_CH), tok),
                  pl.BlockSpec((1, JB_CH, JB_ST), blk3), pl.BlockSpec((1, JB_CH, JB_ST), blk3),
                  pl.BlockSpec((1, JB_CH, JB_ST), blk3), pl.BlockSpec((1, JB_CH, JB_ST), blk3),
                  pl.BlockSpec((1, JB_CH), row), pl.BlockSpec((1, 1, JB_ST), blk3), pl.BlockSpec((1, 1, JB_ST), blk3),
                  HBM_SPEC, HBM_SPEC, HBM_SPEC],
        out_specs=(pl.BlockSpec((n_seq, tt, JB_ST), tok), pl.BlockSpec((n_seq, tt, JB_ST), tok),
                   pl.BlockSpec((n_seq, tt, JB_CH), tok), HBM_SPEC, HBM_SPEC, HBM_SPEC),
        scratch_shapes=[pltpu.VMEM((n_seq, tt, JB_CH), F32), pltpu.VMEM((n_seq, SUBLANES, JB_ST), F32),
                        pltpu.SemaphoreType.DMA((7 * n_arr,)), pltpu.SemaphoreType.DMA((7 * n_arr,)),
                        pltpu.SemaphoreType.DMA((n_arr,))],
        compiler_params=_params(2),
    )(u, bb_re, bb_im, c_re_t, c_imn_t, d_row, ab_re, ab_im, w_out_b, w_glu_b, conv_p)


def _ssm_bwd(dy, u, s_re, s_im, bb_re, bb_im, c_re_t, c_imn_t, d_row, ab_re, ab_im, g_out, g_glu, n_seq, seq):
    tt = min(SCAN_TILE, seq)
    nt = seq // tt
    rows8 = tt // SUBLANES

    def body(dy_ref, u_ref, sre_ref, sim_ref, pre_ref, pim_ref, bbre, bbim, cre, cimn, d_ref, are, aim,
             gout_ref, gglu_ref,
             du_ref, dcre_ref, dcim_ref, dbbre_ref, dbbim_ref, dare_ref, daim_ref, dd_ref, rout_ref, rglu_ref,
             lre_ref, lim_ref, dyp_ref, up_ref, car_ref, send_sems, recv_sems, loc_sems):
        j = pl.program_id(0)
        tr = pl.program_id(1)

        def exchange():
            return _direct_copies(lambda pid: [gout_ref.at[pid], gglu_ref.at[pid]], [rout_ref, rglu_ref],
                                  send_sems, recv_sems, loc_sems)

        @pl.when((j == 0) & (tr == 0))
        def _():
            mine, sends = exchange()
            for cp in mine + sends:
                cp.start()

        @pl.when(tr == 0)
        def _():
            car_ref[...] = jnp.zeros_like(car_ref)
            for r in (dcre_ref, dcim_ref, dbbre_ref, dbbim_ref, dare_ref, daim_ref, dd_ref):
                r[...] = jnp.zeros_like(r)

        first = tr == nt - 1
        row = lax.broadcasted_iota(jnp.int32, (SUBLANES, JB_ST), 0)
        n_blk = tt // SUBLANES
        bs = list(range(n_seq))
        for b in bs:
            _load_chunked(dy_ref, b, dyp_ref, tt)
            _load_chunked(u_ref, b, up_ref, tt)
        for b in bs:
            dyb = dyp_ref[b].astype(BF16)
            lre_ref[b] = _dot(dyb, cre[0])
            lim_ref[b] = _dot(dyb, cimn[0])
        acc = {b: [jnp.zeros((SUBLANES, JB_ST), F32), jnp.zeros((SUBLANES, JB_ST), F32)] for b in bs}

        def on_block(b, i, lr, li):
            if i > 0:
                spr = sre_ref[b, (i - 1) * SUBLANES:i * SUBLANES, :]
                spi = sim_ref[b, (i - 1) * SUBLANES:i * SUBLANES, :]
            else:
                hr = jnp.where(first, 0.0, pre_ref[b, SUBLANES - 1:SUBLANES, :])
                hi = jnp.where(first, 0.0, pim_ref[b, SUBLANES - 1:SUBLANES, :])
                last_r = sre_ref[b, (n_blk - 1) * SUBLANES:n_blk * SUBLANES, :]
                last_i = sim_ref[b, (n_blk - 1) * SUBLANES:n_blk * SUBLANES, :]
                spr = jnp.where(row == 0, jnp.broadcast_to(hr, row.shape), pltpu.roll(last_r, 1, 0))
                spi = jnp.where(row == 0, jnp.broadcast_to(hi, row.shape), pltpu.roll(last_i, 1, 0))
            acc[b][0] = acc[b][0] + (lr * spr + li * spi)
            acc[b][1] = acc[b][1] + (li * spr - lr * spi)

        _chunk_scan(lre_ref, lim_ref, bs, car_ref, are[0], -aim[0], tt, reverse=True, on_block=on_block)
        for b in bs:
            dare_ref[...] += jnp.sum(acc[b][0], axis=0, keepdims=True)
            daim_ref[...] += jnp.sum(acc[b][1], axis=0, keepdims=True)
            dyp = dyp_ref[b]
            up = up_ref[b]
            dyb = dyp.astype(BF16)
            ub = up.astype(BF16)
            lrb = lre_ref[b].astype(BF16)
            lib = lim_ref[b].astype(BF16)
            dup = d_ref[...] * dyp + _dot_nt(lrb, bbre[0]) + _dot_nt(lib, bbim[0])
            _store_chunked(dup, du_ref, b, tt)
            dbbre_ref[0] += _dot_tn(ub, lrb)
            dbbim_ref[0] += _dot_tn(ub, lib)
            dcre_ref[0] += _dot_tn(dyb, sre_ref[b].astype(BF16))
            dcim_ref[0] += _dot_tn(dyb, sim_ref[b].astype(BF16))
            dd_ref[...] += jnp.sum(dyp * up, axis=0, keepdims=True)

        @pl.when((j == N_JBLK - 1) & (tr == nt - 1))
        def _():
            mine, sends = exchange()
            for cp in sends + mine:
                cp.wait()

    tok = lambda j, t: (0, nt - 1 - t, j)
    halo = lambda j, t: (0, jnp.maximum((nt - 1 - t) * rows8 - 1, 0), j)
    blk3 = lambda j, t: (j, 0, 0)
    row1 = lambda j, t: (0, j)
    acc_shape = _out((N_JBLK, JB_CH, JB_ST), F32)
    return _pcall(
        body, name="ssm_bwd", grid=(N_JBLK, nt),
        out_shape=(_out((n_seq, seq, SSM_W), F32), acc_shape, acc_shape, acc_shape, acc_shape,
                   _out((1, N_JBLK * JB_ST), F32), _out((1, N_JBLK * JB_ST), F32),
                   _out((1, SSM_W), F32),
                   _out((N_DEV,) + g_out.shape[1:], F32),
                   _out((N_DEV,) + g_glu.shape[1:], F32)),
        in_specs=[pl.BlockSpec((n_seq, tt, JB_CH), tok), pl.BlockSpec((n_seq, tt, JB_CH), tok),
                  pl.BlockSpec((n_seq, tt, JB_ST), tok), pl.BlockSpec((n_seq, tt, JB_ST), tok),
                  pl.BlockSpec((n_seq, SUBLANES, JB_ST), halo), pl.BlockSpec((n_seq, SUBLANES, JB_ST), halo),
                  pl.BlockSpec((1, JB_CH, JB_ST), blk3), pl.BlockSpec((1, JB_CH, JB_ST), blk3),
                  pl.BlockSpec((1, JB_CH, JB_ST), blk3), pl.BlockSpec((1, JB_CH, JB_ST), blk3),
                  pl.BlockSpec((1, JB_CH), row1), pl.BlockSpec((1, 1, JB_ST), blk3), pl.BlockSpec((1, 1, JB_ST), blk3),
                  HBM_SPEC, HBM_SPEC],
        out_specs=(pl.BlockSpec((n_seq, tt, JB_CH), tok),
                   pl.BlockSpec((1, JB_CH, JB_ST), blk3), pl.BlockSpec((1, JB_CH, JB_ST), blk3),
                   pl.BlockSpec((1, JB_CH, JB_ST), blk3), pl.BlockSpec((1, JB_CH, JB_ST), blk3),
                   pl.BlockSpec((1, JB_ST), row1), pl.BlockSpec((1, JB_ST), row1), pl.BlockSpec((1, JB_CH), row1),
                   HBM_SPEC, HBM_SPEC),
        scratch_shapes=[pltpu.VMEM((n_seq, tt, JB_ST), F32), pltpu.VMEM((n_seq, tt, JB_ST), F32),
                        pltpu.VMEM((n_seq, tt, JB_CH), F32), pltpu.VMEM((n_seq, tt, JB_CH), F32),
                        pltpu.VMEM((n_seq, SUBLANES, JB_ST), F32),
                        pltpu.SemaphoreType.DMA((7 * 2,)), pltpu.SemaphoreType.DMA((7 * 2,)),
                        pltpu.SemaphoreType.DMA((2,))],
        compiler_params=_params(2),
    )(dy, u, s_re, s_im, s_re, s_im, bb_re, bb_im, c_re_t, c_imn_t, d_row, ab_re, ab_im, g_out, g_glu)


def _mix(x2, tgt2, y, proj, gf, b_glu, conv8, w_glu_f, w_out_f, seq):
    n = x2.shape[0]
    tm = TOK_TILE
    tiles_per_seq = seq // tm
    rows8 = tm // SUBLANES

    def body(x_ref, t_ref, y_ref, zs_ref, h_ref, bc_ref, cc_ref, zc_ref, hp_ref, ccp_ref,
             gf_ref, bg_ref, cw_ref, wg_ref, wo_ref,
             dh2_ref, dy_ref, dzs_ref, dbc_ref, dzc_ref, dyc_ref,
             dwo_ref, dwg_ref, loss_ref, dgf_ref, dbg_ref, dcw_ref):
        i = pl.program_id(0)

        @pl.when(i == 0)
        def _():
            for r in (dwo_ref, dwg_ref, loss_ref, dgf_ref, dbg_ref, dcw_ref):
                r[...] = jnp.zeros_like(r)

        yv = y_ref[...]
        y1, dgelu = _gelu_and_grad(yv)
        y1b = y1.astype(BF16)
        gate = _sigmoid(_dot(y1b, wg_ref[...]) + bg_ref[...])
        y2 = y1 * gate
        szs, dszs = _silu_and_grad(zs_ref[...])
        yssm = y2 * szs
        hv = h_ref[...]
        ccv = cc_ref[...]
        bcv = bc_ref[...]
        v = ccv * hv
        first = (i % tiles_per_seq) == 0
        vhalo = jnp.where(first, 0.0, ccp_ref[...] * hp_ref[...])
        v1 = _shift_down(v, vhalo, 1)
        v2 = _shift_down(v, vhalo, 2)
        w0 = cw_ref[0:1, :]
        w1 = cw_ref[1:2, :]
        w2 = cw_ref[2:3, :]
        yc = w0 * v2 + w1 * v1 + w2 * v
        szc, dszc = _silu_and_grad(zc_ref[...])
        yconv = (bcv * yc) * szc
        ysb = yssm.astype(BF16)
        ycb = yconv.astype(BF16)
        h2 = x_ref[...] + _dot(ysb, wo_ref[0:SSM_W, :]) + _dot(ycb, wo_ref[SSM_W:, :])
        r2 = lax.rsqrt(jnp.mean(h2 * h2, axis=-1, keepdims=True) + EPS)
        hn = h2 * r2
        gfv = gf_ref[...]
        err = hn * gfv - t_ref[...]
        loss_ref[...] += 0.5 * jnp.sum(jnp.mean(err * err, axis=-1, keepdims=True))
        dout = err * (1.0 / D_MODEL)
        dgf_ref[...] += jnp.sum(dout * hn, axis=0, keepdims=True)
        dn = dout * gfv
        dh2 = r2 * (dn - hn * jnp.mean(dn * hn, axis=-1, keepdims=True))
        dh2_ref[...] = dh2
        dh2b = dh2.astype(BF16)
        dwo_ref[0:SSM_W, :] += _dot_tn(ysb, dh2b)
        dwo_ref[SSM_W:, :] += _dot_tn(ycb, dh2b)
        dyssm = _dot_nt(dh2b, wo_ref[0:SSM_W, :])
        dyconv = _dot_nt(dh2b, wo_ref[SSM_W:, :])
        dy2 = dyssm * szs
        dzs_ref[...] = (dyssm * y2 * dszs).astype(BF16)
        dgp = dy2 * y1 * (gate * (1.0 - gate))
        dgpb = dgp.astype(BF16)
        dy1 = dy2 * gate + _dot_nt(dgpb, wg_ref[...])
        dwg_ref[...] += _dot_tn(y1b, dgpb)
        dbg_ref[...] += jnp.sum(dgp, axis=0, keepdims=True)
        dy_ref[...] = dy1 * dgelu
        dbc_ref[...] = (dyconv * yc * szc).astype(BF16)
        dyc = dyconv * bcv * szc
        dyc_ref[...] = dyc
        dzc_ref[...] = (dyconv * bcv * yc * dszc).astype(BF16)
        dcw_ref[0:1, :] += jnp.sum(dyc * v2, axis=0, keepdims=True)
        dcw_ref[1:2, :] += jnp.sum(dyc * v1, axis=0, keepdims=True)
        dcw_ref[2:3, :] += jnp.sum(dyc * v, axis=0, keepdims=True)

    tile_d = pl.BlockSpec((tm, D_MODEL), lambda i: (i, 0))
    tile_s = pl.BlockSpec((tm, SSM_W), lambda i: (i, 0))
    seg_of = lambda c: pl.BlockSpec((tm, SSM_W), lambda i: (i, c))
    halo_of = lambda c: pl.BlockSpec((SUBLANES, SSM_W), lambda i: (jnp.maximum(i * rows8 - 1, 0), c))
    const = lambda shape: pl.BlockSpec(shape, lambda i: (0,) * len(shape))
    seg = _out((n, SSM_W), F32)
    seg_b = _out((n, SSM_W), BF16)
    return _pcall(
        body, name="mix", grid=(n // tm,),
        out_shape=(_out((n, D_MODEL), F32), seg, seg_b, seg_b, seg_b, seg,
                   _out((D_MODEL, D_MODEL), F32), _out((SSM_W, SSM_W), F32),
                   _out((SUBLANES, LANES), F32), _out((1, D_MODEL), F32),
                   _out((1, SSM_W), F32), _out((SUBLANES, CONV_W), F32)),
        in_specs=[tile_d, tile_d, tile_s, seg_of(SEG_ZS), seg_of(SEG_H), seg_of(SEG_BC), seg_of(SEG_CC), seg_of(SEG_ZC),
                  halo_of(SEG_H), halo_of(SEG_CC),
                  const((1, D_MODEL)), const((1, SSM_W)), const((SUBLANES, CONV_W)),
                  const((SSM_W, SSM_W)), const((D_MODEL, D_MODEL))],
        out_specs=(tile_d, tile_s, tile_s, tile_s, tile_s, tile_s,
                   const((D_MODEL, D_MODEL)), const((SSM_W, SSM_W)), const((SUBLANES, LANES)),
                   const((1, D_MODEL)), const((1, SSM_W)), const((SUBLANES, CONV_W))),
        compiler_params=_params(1),
    )(x2, tgt2, y, proj, proj, proj, proj, proj, proj, proj, gf, b_glu, conv8, w_glu_f, w_out_f)


def _in_bwd(x2, dh2, du, dzs, dyc, proj, dbc, dzc, g1, conv8, w_full, seq):
    n = x2.shape[0]
    tm = TOK_TILE
    n_tiles = n // tm
    tiles_per_seq = seq // tm
    rows8 = tm // SUBLANES
    n_blk8 = n // SUBLANES

    def body(x_ref, dh2_ref, du_ref, dzs_ref, dyc_ref, dycn_ref, h_ref, cc_ref, dbc_ref, dzc_ref,
             g_ref, cw_ref, w_ref, gx_ref, dp_ref, dg_ref):
        i = pl.program_id(0)

        @pl.when(i == 0)
        def _():
            dg_ref[...] = jnp.zeros_like(dg_ref)

        dyc = dyc_ref[...]
        last = (i % tiles_per_seq) == tiles_per_seq - 1
        nhalo = jnp.where(last, 0.0, dycn_ref[...])
        dv = (cw_ref[2:3, :] * dyc + cw_ref[1:2, :] * _shift_up(dyc, nhalo, 1)
              + cw_ref[0:1, :] * _shift_up(dyc, nhalo, 2))
        parts = (du_ref[...], dzs_ref[...], dv * cc_ref[...], dbc_ref[...], dv * h_ref[...], dzc_ref[...])
        dxn = jnp.zeros((tm, D_MODEL), F32)
        for k, p in enumerate(parts):
            pb = p.astype(BF16)
            dp_ref[:, k * SSM_W:(k + 1) * SSM_W] = pb
            dxn = dxn + _dot_nt(pb, w_ref[:, k * SSM_W:(k + 1) * SSM_W])
        x = x_ref[...]
        r = lax.rsqrt(jnp.mean(x * x, axis=-1, keepdims=True) + EPS)
        xh = x * r
        dg_ref[...] += jnp.sum(dxn * xh, axis=0, keepdims=True)
        dn = dxn * g_ref[...]
        gx_ref[...] = dh2_ref[...] + r * (dn - xh * jnp.mean(dn * xh, axis=-1, keepdims=True))

    tile_d = pl.BlockSpec((tm, D_MODEL), lambda i: (i, 0))
    tile_s = pl.BlockSpec((tm, SSM_W), lambda i: (i, 0))
    seg_of = lambda c: pl.BlockSpec((tm, SSM_W), lambda i: (i, c))
    nhalo = pl.BlockSpec((SUBLANES, SSM_W), lambda i: (jnp.minimum((i + 1) * rows8, n_blk8 - 1), 0))
    const = lambda shape: pl.BlockSpec(shape, lambda i: (0,) * len(shape))
    return _pcall(
        body, name="in_bwd", grid=(n_tiles,),
        out_shape=(_out((n, D_MODEL), F32), _out((n, IN_COLS), BF16),
                   _out((SUBLANES, D_MODEL), F32)),
        in_specs=[tile_d, tile_d, tile_s, tile_s, tile_s, nhalo, seg_of(SEG_H), seg_of(SEG_CC), tile_s, tile_s,
                  const((1, D_MODEL)), const((SUBLANES, CONV_W)), const((D_MODEL, IN_COLS))],
        out_specs=(tile_d, pl.BlockSpec((tm, IN_COLS), lambda i: (i, 0)), const((SUBLANES, D_MODEL))),
        compiler_params=_params(1),
    )(x2, dh2, du, dzs, dyc, dyc, proj, proj, dbc, dzc, g1, conv8, w_full)


def _dw_in_exchange(order, xn, dproj, smalls):
    n = xn.shape[0]
    tk = min(1024, n)
    nk = n // tk
    piece = (D_MODEL, COLS_PER_DEV)
    hr = D_MODEL // 2
    n_half = 2 * N_CHIP
    n_small = len(smalls)

    def body(order_ref, xn_hbm, dp_ref, *refs):
        del order_ref
        sm_refs = refs[:n_small]
        own_ref, rchip_ref = refs[n_small:n_small + 2]
        rsm_refs = refs[n_small + 2:2 * n_small + 2]
        (xn_ref, acc, stage, rbuf, kbuf, relay_in, xn_sems, give_send, give_recv, keep_send, keep_recv,
         relay_send, relay_recv, sm_send, sm_recv, sm_loc) = refs[2 * n_small + 2:]
        s = pl.program_id(0)

        def xn_copy(kk, t):
            rows = pl.ds(pl.multiple_of(kk * tk, tk), tk)
            return pltpu.make_async_copy(xn_hbm.at[rows, t * hr:(t + 1) * hr], xn_ref.at[t, rows, :],
                                         xn_sems.at[2 * kk + t])

        @pl.when(s == 0)
        def _():
            for kk in range(nk):
                for t in range(2):
                    xn_copy(kk, t).start()
            xn_copy(0, 0).wait()

        @pl.when(s == 1)
        def _():
            xn_copy(0, 1).wait()

        x, y, c = _mesh_pos()
        sib = (x, y, 1 - c)
        y_nbr, x_nbr = (x, 1 - y, c), (1 - x, y, c)
        gather = _TwoLevelGather(list(sm_refs), [functools.partial(lambda r, dev: r.at[dev], r) for r in rsm_refs],
                                 sm_send, sm_recv, sm_loc)

        def give(h):
            cols = pl.ds(pl.multiple_of((1 - c) * COLS_PER_DEV, LANES), COLS_PER_DEV)
            return pltpu.make_async_remote_copy(src_ref=acc.at[h % 2, :, cols], dst_ref=stage.at[h],
                                                send_sem=give_send.at[h], recv_sem=give_recv.at[h],
                                                device_id=sib, device_id_type=MESH)

        def relay(r):
            return pltpu.make_async_remote_copy(src_ref=rbuf.at[r], dst_ref=relay_in.at[r],
                                                send_sem=relay_send.at[r], recv_sem=relay_recv.at[r],
                                                device_id=(x_nbr, y_nbr)[r], device_id_type=MESH)

        def keep(q):
            return pltpu.make_async_remote_copy(src_ref=kbuf.at[q], dst_ref=rchip_ref.at[q // 2, pl.ds((q % 2) * hr, hr), :],
                                                send_sem=keep_send.at[q], recv_sem=keep_recv.at[q],
                                                device_id=(y_nbr, x_nbr)[q // 2], device_id_type=MESH)

        def chip_sum(h):
            give(h).wait_recv()
            mine = [acc[h % 2, :, cc * COLS_PER_DEV:(cc + 1) * COLS_PER_DEV] for cc in range(2)]
            return jnp.where(c == 0, mine[0], mine[1]) + stage[h]

        @pl.when(s == 0)
        def _():
            gather.start()

        @pl.when(s == 2)
        def _():
            gather.neighbours_landed()

        @pl.when(s == n_half - 2)
        def _():
            gather.diagonal_landed()

        for k in range(2, n_half):
            @pl.when(s == k)
            def _(k=k):
                give(k - 2).wait_send()

        slot = s % 2
        t_half = s % 2
        acc[slot] = _dot_tn(xn_ref[t_half, pl.ds(0, tk), :], dp_ref[pl.ds(0, tk), :])

        def kstep(kk, carry):
            for t in range(2):
                @pl.when(s == t)
                def _(t=t):
                    xn_copy(kk, t).wait()

            off = pl.multiple_of(kk * tk, tk)
            acc[slot] += _dot_tn(xn_ref[t_half, pl.ds(off, tk), :], dp_ref[pl.ds(off, tk), :])
            return carry

        n_first = max(1, (3 * nk) // 8)
        lax.fori_loop(1, n_first, kstep, 0)
        for k in range(1, n_half):
            @pl.when(s == k)
            def _(k=k):
                h = k - 1
                total = chip_sum(h)
                if h < 2:
                    rbuf[h] = total.astype(BF16)
                    relay(h).start()
                elif h < 6:
                    if h in (2, 5):
                        r = 0 if h == 2 else 1
                        relay(r).wait_recv()
                        total = total + relay_in[r].astype(F32)
                    kbuf[h - 2] = total.astype(BF16)
                    keep(h - 2).start()
                else:
                    own_ref[0:hr, :] = total

        lax.fori_loop(n_first, nk, kstep, 0)

        for k in range(n_half):
            @pl.when(s == k)
            def _(k=k):
                give(k).start()

        @pl.when(s == n_half - 1)
        def _():
            own_ref[hr:D_MODEL, :] = chip_sum(n_half - 1)
            give(n_half - 2).wait_send()
            give(n_half - 1).wait_send()
            for r in range(2):
                relay(r).wait_send()
            for q in range(4):
                keep(q).wait()
            gather.finish()

    half_piece = (hr, COLS_PER_DEV)
    grid_spec = pltpu.PrefetchScalarGridSpec(
        num_scalar_prefetch=1, grid=(n_half,),
        in_specs=[HBM_SPEC,
                  pl.BlockSpec((n, COLS_PER_CHIP), lambda s, order: (0, order[s // 2])),
                  *([HBM_SPEC] * n_small)],
        out_specs=(pl.BlockSpec(piece, lambda s, order: (0, 0)), HBM_SPEC, *([HBM_SPEC] * n_small)),
        scratch_shapes=[pltpu.VMEM((2, n, hr), BF16),
                        pltpu.VMEM((2, hr, COLS_PER_CHIP), F32), pltpu.VMEM((n_half,) + half_piece, F32),
                        pltpu.VMEM((2,) + half_piece, BF16), pltpu.VMEM((4,) + half_piece, BF16),
                        pltpu.VMEM((2,) + half_piece, BF16),
                        pltpu.SemaphoreType.DMA((2 * nk,)),
                        pltpu.SemaphoreType.DMA((n_half,)), pltpu.SemaphoreType.DMA((n_half,)),
                        pltpu.SemaphoreType.DMA((4,)), pltpu.SemaphoreType.DMA((4,)),
                        pltpu.SemaphoreType.DMA((2,)), pltpu.SemaphoreType.DMA((2,)),
                        pltpu.SemaphoreType.DMA((7 * n_small,)), pltpu.SemaphoreType.DMA((7 * n_small,)),
                        pltpu.SemaphoreType.DMA((n_small,))])
    return _pcall(
        body, name="dw_in_exchange", grid_spec=grid_spec,
        out_shape=(_out(piece, F32), _out((2,) + piece, BF16),
                   *(_out((N_DEV,) + a.shape, a.dtype) for a in smalls)),
        compiler_params=_params(1),
    )(order, xn, dproj, *smalls)


def _adamw(g, w, m, v):
    m_new = ADAM_B1 * m + (1.0 - ADAM_B1) * g
    v_new = ADAM_B2 * v + (1.0 - ADAM_B2) * (g * g)
    m_hat = m_new / (1.0 - ADAM_B1 ** ADAM_STEP)
    v_hat = v_new / (1.0 - ADAM_B2 ** ADAM_STEP)
    delta = -ADAM_LR * (m_hat / (jnp.sqrt(v_hat) + ADAM_EPS) + ADAM_WD * w)
    return delta, m_new, v_new


def _reduce_adam_w_in(own, rchip, w, m, v):
    rows, cols = w.shape
    row_tile = 256

    def body(o_ref, r_ref, w_ref, m_ref, v_ref, g_ref, d_ref, nm_ref, nv_ref):
        g = o_ref[...]
        for s in range(2):
            g = g + r_ref[s].astype(F32)
        g_ref[...] = g
        d_ref[...], nm_ref[...], nv_ref[...] = _adamw(g, w_ref[...], m_ref[...], v_ref[...])

    tile = pl.BlockSpec((row_tile, cols), lambda i: (i, 0))
    shp = _out((rows, cols), F32)
    return _pcall(
        body, name="reduce_adam_w_in", grid=(rows // row_tile,),
        out_shape=(shp,) * 4,
        in_specs=[tile, pl.BlockSpec((2, row_tile, cols), lambda i: (0, i, 0)), tile, tile, tile],
        out_specs=(tile,) * 4,
        compiler_params=_params(1),
    )(own, rchip, w, m, v)


_SMALL_LEAVES = ("norm_gain", "final_norm_gain", "b_glu", "ssm_a_re", "ssm_a_im", "ssm_log_dt", "ssm_d", "conv_w",
                 "ssm_c_re", "ssm_c_im", "ssm_b_re", "ssm_b_im")


def _reduce_adam_small(r_pack, r_gc, r_gb, wmv, sharded):
    n_leaf = len(_SMALL_LEAVES)
    n_sh = len(sharded)

    def body(*refs):
        rp_ref, rgc_ref, rgb_ref = refs[:3]
        w_refs = refs[3:3 + 3 * n_leaf]
        sh_in = refs[3 + 3 * n_leaf:3 + 3 * n_leaf + 4 * n_sh]
        outs0 = 3 + 3 * n_leaf + 4 * n_sh
        loss_ref = refs[outs0]
        o_refs = refs[outs0 + 1:outs0 + 1 + 4 * n_leaf]
        sh_out = refs[outs0 + 1 + 4 * n_leaf:outs0 + 1 + 4 * n_leaf + 4 * n_sh]
        own_conv = refs[-1]

        def total(ref):
            acc = ref[0].astype(F32)
            for s in range(1, N_DEV):
                acc = acc + ref[s].astype(F32)
            return acc

        for i in range(n_sh):
            r_ref, w_ref, m_ref, v_ref = sh_in[4 * i:4 * i + 4]
            o_g, o_d, o_m, o_v = sh_out[4 * i:4 * i + 4]
            g = total(r_ref)
            o_g[...] = g
            o_d[...], o_m[...], o_v[...] = _adamw(g, w_ref[...], m_ref[...], v_ref[...])

        sp = total(rp_ref)
        sgc = total(rgc_ref)
        sgb = total(rgb_ref)
        loss_ref[...] = sp[ROW_LOSS:ROW_LOSS + SUBLANES, 0:LANES]

        def wide(r):
            return jnp.concatenate([sp[r:r + 1, :], sp[r + 1:r + 2, :]], axis=1)

        s5 = slice(ROW_S5, ROW_S5 + N_GROUPS)
        eye = (lax.broadcasted_iota(jnp.int32, (N_GROUPS, N_GROUPS), 0)
               == lax.broadcasted_iota(jnp.int32, (N_GROUPS, N_GROUPS), 1)).astype(F32)
        d_row = sp[ROW_BGLU_D + 1:ROW_BGLU_D + 2, :]
        me = 4 * lax.axis_index("x") + 2 * lax.axis_index("y") + lax.axis_index("c")
        for k in range(N_DEV):
            @pl.when(me == k)
            def _(k=k):
                own_conv[...] = sp[ROW_CONV:ROW_CONV + SUBLANES, k * CONV_COLS_PER_DEV:(k + 1) * CONV_COLS_PER_DEV]
        grads = {
            "norm_gain": wide(ROW_NORM_GAIN),
            "final_norm_gain": wide(ROW_FINAL_GAIN),
            "b_glu": sp[ROW_BGLU_D:ROW_BGLU_D + 1, :],
            "ssm_a_re": sp[s5, LANE_A_RE:LANE_A_RE + STATE],
            "ssm_a_im": sp[s5, LANE_A_IM:LANE_A_IM + STATE],
            "ssm_log_dt": jnp.sum(sp[s5, LANE_LOG_DT:LANE_LOG_DT + 1] * eye, axis=0, keepdims=True),
            "ssm_d": jnp.concatenate([d_row[:, g * GROUP:(g + 1) * GROUP] for g in range(N_GROUPS)], axis=0),
            "conv_w": own_conv[0:3, :],
            "ssm_c_re": sgc[:, 0:STATE],
            "ssm_c_im": sgc[:, STATE:2 * STATE],
            "ssm_b_re": sgb[:, 0:STATE],
            "ssm_b_im": sgb[:, STATE:2 * STATE],
        }
        for i, name in enumerate(_SMALL_LEAVES):
            g = grads[name]
            w_ref, m_ref, v_ref = w_refs[3 * i:3 * i + 3]
            o_g, o_d, o_m, o_v = o_refs[4 * i:4 * i + 4]
            o_g[...] = g
            o_d[...], o_m[...], o_v[...] = _adamw(g, w_ref[...], m_ref[...], v_ref[...])

    flat_w = [a for name in _SMALL_LEAVES for a in wmv[name]]
    leaf_shapes = [_out(wmv[name][0].shape, F32) for name in _SMALL_LEAVES for _ in range(4)]
    sh_shapes = [_out(entry[1].shape, F32) for entry in sharded for _ in range(4)]
    operands = (r_pack, r_gc, r_gb, *flat_w, *(a for entry in sharded for a in entry))
    out_shape = (_out((SUBLANES, LANES), F32), *leaf_shapes, *sh_shapes)
    outs = _pcall(
        body, name="reduce_adam_small", grid=(1,), out_shape=out_shape,
        in_specs=_whole_specs(operands), out_specs=tuple(_whole_specs(out_shape)),
        scratch_shapes=[pltpu.VMEM((SUBLANES, CONV_COLS_PER_DEV), F32)],
        compiler_params=_params(1),
    )(*operands)
    leaves = {name: outs[1 + 4 * i:5 + 4 * i] for i, name in enumerate(_SMALL_LEAVES)}
    first = 1 + 4 * n_leaf
    return outs[0], leaves, [outs[first + 4 * i:first + 4 * i + 4] for i in range(n_sh)]


def kernel(x, norm_gain, w_in, ssm_a_re, ssm_a_im, ssm_log_dt, ssm_b_re, ssm_b_im, ssm_c_re, ssm_c_im, ssm_d, w_glu, b_glu, conv_w, w_out, final_norm_gain, loss_target, m_norm_gain, m_w_in, m_ssm_a_re, m_ssm_a_im, m_ssm_log_dt, m_ssm_b_re, m_ssm_b_im, m_ssm_c_re, m_ssm_c_im, m_ssm_d, m_w_glu, m_b_glu, m_conv_w, m_w_out, m_final_norm_gain, v_norm_gain, v_w_in, v_ssm_a_re, v_ssm_a_im, v_ssm_log_dt, v_ssm_b_re, v_ssm_b_im, v_ssm_c_re, v_ssm_c_im, v_ssm_d, v_w_glu, v_b_glu, v_conv_w, v_w_out, v_final_norm_gain):
    n_seq, seq, _ = x.shape
    n = n_seq * seq

    gh_p = lambda b4: jnp.transpose(b4, (0, 1, 3, 2)).reshape(N_GROUPS * GROUP, STATE)
    c2 = lambda a: a.reshape(N_GROUPS * GROUP, STATE)
    b_re2, b_im2 = gh_p(ssm_b_re), gh_p(ssm_b_im)
    d_row = ssm_d[0].reshape(1, SSM_W)

    x2 = x.reshape(n, D_MODEL)
    tgt2 = loss_target.reshape(n, D_MODEL)
    mx, my, mc = lax.axis_index("x"), lax.axis_index("y"), lax.axis_index("c")
    chip_ids = [2 * cx + cy for cx, cy in ((mx, my), (1 - mx, my), (mx, 1 - my), (1 - mx, 1 - my))]
    arrival = chip_ids
    xn, proj, w_in_f, s5 = _in_proj(
        jnp.stack(arrival).astype(jnp.int32), x2, norm_gain, w_in[0].astype(BF16),
        (ssm_a_re[0], ssm_a_im[0], ssm_log_dt, b_re2, b_im2, c2(ssm_c_re), c2(ssm_c_im)))
    a_re_x, a_im_x, log_dt_x, ab_re, ab_im, bb_re_m, bb_im_m, c_re_m, c_imn_m = s5
    u3 = proj.reshape(n_seq, seq, IN_COLS)
    conv_p = jnp.pad(conv_w[0], ((0, SUBLANES - 3), (0, LANES - CONV_COLS_PER_DEV)))
    s_re, s_im, y3, w_out_f, w_glu_f, conv_all = _ssm_fwd(
        u3, bb_re_m, bb_im_m, c_re_m, c_imn_m, d_row, ab_re, ab_im,
        w_out[0].astype(BF16), w_glu[0].astype(BF16), conv_p, n_seq, seq)
    conv8 = jnp.transpose(conv_all[:, :, :CONV_COLS_PER_DEV], (1, 0, 2)).reshape(SUBLANES, CONV_W)
    (dh2, dy, dzs, dbc, dzc, dyc, dw_out, dw_glu, loss_t, dgf, dbg, dcw) = _mix(
        x2, tgt2, y3.reshape(n, SSM_W), proj, final_norm_gain.reshape(1, D_MODEL), b_glu, conv8,
        w_glu_f, w_out_f, seq)

    du3, dc_re_d, dc_im_d, dbb_re_d, dbb_im_d, dab_re, dab_im, dd, r_out, r_glu = _ssm_bwd(
        dy.reshape(n_seq, seq, SSM_W), u3, s_re, s_im, bb_re_m, bb_im_m, c_re_m, c_imn_m, d_row, ab_re, ab_im,
        dw_out.reshape(N_DEV, OUT_ROWS_PER_DEV, D_MODEL), dw_glu.reshape(N_DEV, GLU_ROWS_PER_DEV, SSM_W), n_seq, seq)
    du = du3.reshape(n, SSM_W)
    grad_x2, dproj, dg8 = _in_bwd(x2, dh2, du, dzs, dyc, proj, dbc, dzc, norm_gain, conv8, w_in_f, seq)
    pack, gc, gb = _ssm_disc_bwd_pack(
        a_re_x, a_im_x, log_dt_x, b_re2, b_im2, dab_re.reshape(N_GROUPS, STATE), dab_im.reshape(N_GROUPS, STATE),
        dbb_re_d, dbb_im_d, loss_t, dg8, dgf, dbg, dd, dcw, dc_re_d, dc_im_d)

    order = [chip_ids[3], chip_ids[2], chip_ids[1], chip_ids[0]]
    own_in, rchip_in, r_pack, r_gc, r_gb = _dw_in_exchange(
        jnp.stack(order).astype(jnp.int32), xn, dproj, [pack, gc, gb])

    flat2 = lambda a: a.reshape(a.shape[-2:]) if a.ndim > 2 else a.reshape(1, -1)
    c2 = lambda a: a.reshape(N_GROUPS * GROUP, STATE)
    wmv = dict(norm_gain=(norm_gain, m_norm_gain, v_norm_gain),
               final_norm_gain=tuple(flat2(a) for a in (final_norm_gain, m_final_norm_gain, v_final_norm_gain)),
               b_glu=(b_glu, m_b_glu, v_b_glu),
               ssm_a_re=tuple(flat2(a) for a in (ssm_a_re, m_ssm_a_re, v_ssm_a_re)),
               ssm_a_im=tuple(flat2(a) for a in (ssm_a_im, m_ssm_a_im, v_ssm_a_im)),
               ssm_log_dt=(ssm_log_dt, m_ssm_log_dt, v_ssm_log_dt),
               ssm_d=tuple(flat2(a) for a in (ssm_d, m_ssm_d, v_ssm_d)),
               conv_w=tuple(flat2(a) for a in (conv_w, m_conv_w, v_conv_w)),
               ssm_c_re=tuple(c2(a) for a in (ssm_c_re, m_ssm_c_re, v_ssm_c_re)),
               ssm_c_im=tuple(c2(a) for a in (ssm_c_im, m_ssm_c_im, v_ssm_c_im)),
               ssm_b_re=(b_re2, gh_p(m_ssm_b_re), gh_p(v_ssm_b_re)),
               ssm_b_im=(b_im2, gh_p(m_ssm_b_im), gh_p(v_ssm_b_im)))

    res_in = _reduce_adam_w_in(own_in, rchip_in, w_in[0], m_w_in[0], v_w_in[0])
    loss8, small, (res_out, res_glu) = _reduce_adam_small(
        r_pack, r_gc, r_gb, wmv,
        [(r_out, w_out[0], m_w_out[0], v_w_out[0]), (r_glu, w_glu[0], m_w_glu[0], v_w_glu[0])])
    loss = loss8[0, 0]

    shapes = dict(norm_gain=(1, D_MODEL), ssm_a_re=(1, N_GROUPS, STATE), ssm_a_im=(1, N_GROUPS, STATE),
                  ssm_log_dt=(1, N_GROUPS), ssm_c_re=(1, N_GROUPS, GROUP, STATE), ssm_c_im=(1, N_GROUPS, GROUP, STATE),
                  ssm_d=(1, N_GROUPS, GROUP), b_glu=(1, SSM_W), final_norm_gain=(D_MODEL,),
                  conv_w=(1, 3, CONV_COLS_PER_DEV))
    big = dict(w_in=res_in, w_glu=res_glu, w_out=res_out)

    def leaf(kind, name):
        if name in big:
            return big[name][kind][None]
        if name in ("ssm_b_re", "ssm_b_im"):
            return jnp.transpose(small[name][kind].reshape(1, N_GROUPS, GROUP, STATE), (0, 1, 3, 2))
        return small[name][kind].reshape(shapes[name])

    order = ["norm_gain", "w_in", "ssm_a_re", "ssm_a_im", "ssm_log_dt", "ssm_b_re", "ssm_b_im", "ssm_c_re",
             "ssm_c_im", "ssm_d", "w_glu", "b_glu", "conv_w", "w_out", "final_norm_gain"]
    outs = [loss, grad_x2.reshape(x.shape)]
    for kind in range(4):
        outs += [leaf(kind, name) for name in order]
    return tuple(outs)
```

```python
import functools
import math

import jax
import jax.numpy as jnp
from jax import lax
from jax.experimental import pallas as pl
from jax.experimental.pallas import tpu as pltpu

F32 = jnp.float32
BF16 = jnp.bfloat16

N_DEV = 8
D_MODEL = 1024
SSM_W = 512
CONV_W = 512
N_GROUPS = 32
GROUP = 16
STATE = 64
IN_COLS = 3072
SEG_U, SEG_ZS, SEG_H, SEG_BC, SEG_CC, SEG_ZC = range(6)
COLS_PER_DEV = IN_COLS // N_DEV
N_CHIP = N_DEV // 2
COLS_PER_CHIP = 2 * COLS_PER_DEV
OUT_ROWS_PER_DEV = D_MODEL // N_DEV
GLU_ROWS_PER_DEV = SSM_W // N_DEV
CONV_COLS_PER_DEV = CONV_W // N_DEV
EPS = 1e-6

N_JBLK = 4
JB_CH = SSM_W // N_JBLK
JB_ST = N_GROUPS * STATE // N_JBLK

ADAM_LR = 0.001
ADAM_B1 = 0.9
ADAM_B2 = 0.999
ADAM_EPS = 1e-08
ADAM_WD = 0.01
ADAM_STEP = 10

SUBLANES = 8
LANES = 128
VMEM_LIMIT = 48 * 1024 * 1024
TOK_TILE = 256
IN_TILE = 1024
SCAN_TILE = 1024

MESH = pl.DeviceIdType.MESH
HBM_SPEC = pl.BlockSpec(memory_space=pltpu.HBM)


def _build(body, **kw):
    return pl.pallas_call(body, **kw)


def _pcall(body, **kw):
    def call(*operands):
        pinned = [a if jnp.issubdtype(a.dtype, jnp.integer) else pltpu.with_memory_space_constraint(a, pltpu.HBM)
                  for a in operands]
        return _build(body, **kw)(*pinned)
    return call


def _whole_specs(arrays):
    return [pl.BlockSpec(a.shape, functools.partial(lambda nd, i: (0,) * nd, len(a.shape))) for a in arrays]


def _out(shape, dtype):
    return pltpu.HBM(tuple(shape), dtype)


def _params(n_grid):
    return pltpu.CompilerParams(dimension_semantics=("arbitrary",) * n_grid,
                                vmem_limit_bytes=VMEM_LIMIT)


def _dot(a, b):
    return jnp.dot(a, b, preferred_element_type=F32)


def _dot_nt(a, b):
    return lax.dot_general(a, b, (((1,), (1,)), ((), ())), preferred_element_type=F32)


def _dot_tn(a, b):
    return lax.dot_general(a, b, (((0,), (0,)), ((), ())), preferred_element_type=F32)


def _sigmoid(z):
    return 1.0 / (1.0 + jnp.exp(-z))


_GELU_C = math.sqrt(2.0 / math.pi)


def _gelu_and_grad(y):
    inner = _GELU_C * (y + 0.044715 * (y * y * y))
    t = jnp.tanh(inner)
    g = 0.5 * y * (1.0 + t)
    dg = 0.5 * (1.0 + t) + 0.5 * y * (1.0 - t * t) * (_GELU_C * (1.0 + 3.0 * 0.044715 * (y * y)))
    return g, dg


def _silu_and_grad(z):
    s = _sigmoid(z)
    return z * s, s * (1.0 + z * (1.0 - s))


def _shift_down(v, halo, k):
    rolled = pltpu.roll(v, k, 0)
    row = lax.broadcasted_iota(jnp.int32, v.shape, 0)
    for r in range(k):
        rolled = jnp.where(row == r, halo[SUBLANES - k + r:SUBLANES - k + r + 1, :], rolled)
    return rolled


def _shift_up(v, halo, k):
    n = v.shape[0]
    rolled = pltpu.roll(v, n - k, 0)
    row = lax.broadcasted_iota(jnp.int32, v.shape, 0)
    for r in range(k):
        rolled = jnp.where(row == n - k + r, halo[r:r + 1, :], rolled)
    return rolled


def _mesh_pos():
    return lax.axis_index("x"), lax.axis_index("y"), lax.axis_index("c")


def _direct_copies(srcs_for, out_refs, send_sems, recv_sems, loc_sems):
    x, y, c = _mesh_pos()
    me_id = 4 * x + 2 * y + c
    n_arr = len(out_refs)
    dsts = [r.at[me_id] for r in out_refs]
    own = srcs_for(me_id)
    mine = [pltpu.make_async_copy(own[a], dsts[a], loc_sems.at[a]) for a in range(n_arr)]
    sends = []
    for k in range(1, N_DEV):
        px, py, pc = x ^ ((k >> 2) & 1), y ^ ((k >> 1) & 1), c ^ (k & 1)
        src = srcs_for(4 * px + 2 * py + pc)
        for a in range(n_arr):
            sends.append(pltpu.make_async_remote_copy(
                src_ref=src[a], dst_ref=dsts[a],
                send_sem=send_sems.at[(k - 1) * n_arr + a], recv_sem=recv_sems.at[(k - 1) * n_arr + a],
                device_id=(px, py, pc), device_id_type=MESH))
    return mine, sends


class _TwoLevelGather:
    def __init__(self, srcs, slots, send_sems, recv_sems, loc_sems):
        self.srcs, self.slots, self.n_arr = srcs, slots, len(srcs)
        self.send_sems, self.recv_sems, self.loc_sems = send_sems, recv_sems, loc_sems
        x, y, c = _mesh_pos()
        self.c = c
        self.me, self.sib = (x, y, c), (x, y, 1 - c)
        self.chips = [(1 - x, y), (x, 1 - y), (1 - x, 1 - y)]

    def _copies(self, k, block, to, from_src=False):
        dev = 4 * block[0] + 2 * block[1] + block[2]
        return [pltpu.make_async_remote_copy(
            src_ref=self.srcs[a] if from_src else self.slots[a](dev), dst_ref=self.slots[a](dev),
            send_sem=self.send_sems.at[k * self.n_arr + a], recv_sem=self.recv_sems.at[k * self.n_arr + a],
            device_id=to, device_id_type=MESH) for a in range(self.n_arr)]

    def _local(self):
        dev = 4 * self.me[0] + 2 * self.me[1] + self.me[2]
        return [pltpu.make_async_copy(self.srcs[a], self.slots[a](dev), self.loc_sems.at[a])
                for a in range(self.n_arr)]

    def start(self):
        for cp in self._local() + self._copies(0, self.me, self.sib, True):
            cp.start()
        for j in (0, 1):
            for cp in self._copies(1 + j, self.me, (*self.chips[j], self.c), True):
                cp.start()

    def wait_own(self):
        for cp in self._local():
            cp.wait()

    def wait_sibling(self):
        for cp in self._copies(0, self.sib, self.me):
            cp.wait_recv()

    def wait_and_pass_on(self, j):
        chip = self.chips[j]
        for cp in self._copies(1 + j, (*chip, self.c), self.me):
            cp.wait_recv()
        for cp in self._copies(4 + j, (*chip, self.c), self.sib):
            cp.start()

    def neighbours_landed(self):
        x, y, c = self.me
        self.wait_and_pass_on(0)
        self.wait_and_pass_on(1)
        for cp in self._copies(1 + 2, (x ^ c, y ^ (1 - c), c), (x ^ (1 - c), y ^ c, c)):
            cp.start()

    def diagonal_landed(self):
        self.wait_and_pass_on(2)

    def wait_passed_on(self, j):
        for cp in self._copies(4 + j, (*self.chips[j], 1 - self.c), self.me):
            cp.wait_recv()

    def wait_sends(self):
        for cp in self._copies(0, self.me, self.sib, True):
            cp.wait_send()
        for j, chip in enumerate(self.chips):
            for cp in self._copies(1 + j, self.me, (*chip, self.c), True) + self._copies(4 + j, (*chip, self.c), self.sib):
                cp.wait_send()

    def finish(self):
        self.wait_sibling()
        for j in range(3):
            self.wait_passed_on(j)
        self.wait_sends()
        self.wait_own()


def _disc(a_re, a_im, log_dt, b_re, b_im):
    dt = jnp.exp(log_dt)
    mag = jnp.exp(a_re * dt)
    ab_re = mag * jnp.cos(a_im * dt)
    ab_im = mag * jnp.sin(a_im * dt)
    den = a_re * a_re + a_im * a_im
    p_re = ab_re - 1.0
    p_im = ab_im
    q_re = (p_re * a_re + p_im * a_im) / den
    q_im = (p_im * a_re - p_re * a_im) / den
    bb_re = q_re * b_re - q_im * b_im
    bb_im = q_re * b_im + q_im * b_re
    return ab_re, ab_im, bb_re, bb_im


def _split3(v):
    hi = v.astype(BF16)
    r1 = v - hi.astype(F32)
    mid = r1.astype(BF16)
    lo = (r1 - mid.astype(F32)).astype(BF16)
    return hi, mid, lo


def _select_dot(sel, v):
    return sum(_dot(sel, t) for t in _split3(v))


PACK_ROWS = 72
PACK_W = 512
ROW_FINAL_GAIN, ROW_NORM_GAIN, ROW_BGLU_D, ROW_CONV, ROW_LOSS, ROW_S5 = 0, 8, 16, 24, 32, 40
LANE_A_RE, LANE_A_IM, LANE_LOG_DT = 0, 128, 256


def _ssm_disc_bwd_pack(a_re_x, a_im_x, log_dt_x, b_re, b_im, g_ab_re, g_ab_im, dbb_re_d, dbb_im_d,
                       loss_t, dg8, dgf, dbg, dd, dcw, dc_re_d, dc_im_d):
    rows_gh = N_GROUPS * GROUP

    def body(are, aim, ldt, bre, bim, gabre, gabim, dbbre_ref, dbbim_ref,
             loss_ref, dg8_ref, dgf_ref, dbg_ref, dd_ref, dcw_ref, dcre_ref, dcim_ref,
             p_ref, gc_ref, gb_ref, gbb_re, gbb_im):
        r_g = lax.broadcasted_iota(jnp.int32, (N_GROUPS, rows_gh), 0)
        c_gh = lax.broadcasted_iota(jnp.int32, (N_GROUPS, rows_gh), 1)
        group_sum = (c_gh // GROUP == r_g).astype(BF16)
        r_gh = lax.broadcasted_iota(jnp.int32, (rows_gh, N_GROUPS), 0)
        c_g = lax.broadcasted_iota(jnp.int32, (rows_gh, N_GROUPS), 1)
        first_row = (r_gh == c_g * GROUP).astype(BF16)

        def diag_block(ref, j, gi):
            return ref[j, gi * GROUP:(gi + 1) * GROUP, gi * STATE:(gi + 1) * STATE]

        for j in range(N_JBLK):
            for gi in range(SUBLANES):
                r0 = (j * SUBLANES + gi) * GROUP
                gbb_re[r0:r0 + GROUP, :] = diag_block(dbbre_ref, j, gi)
                gbb_im[r0:r0 + GROUP, :] = diag_block(dbbim_ref, j, gi)
                both = jnp.concatenate([diag_block(dcre_ref, j, gi), -diag_block(dcim_ref, j, gi)], axis=1)
                gc_ref[r0:r0 + GROUP, :] = both.astype(BF16)

        _, vjp = jax.vjp(_disc, are[...], aim[...], ldt[...], bre[...], bim[...])
        d_are, d_aim, d_ldt, d_bre, d_bim = vjp((_select_dot(first_row, gabre[...]), _select_dot(first_row, gabim[...]),
                                                 gbb_re[...], gbb_im[...]))
        gb_ref[...] = jnp.concatenate([d_bre, d_bim], axis=1).astype(BF16)

        p_ref[...] = jnp.zeros_like(p_ref)
        half = D_MODEL // 2
        for r, src in ((ROW_FINAL_GAIN, dgf_ref), (ROW_NORM_GAIN, dg8_ref)):
            p_ref[r:r + 1, :] = src[0:1, 0:half]
            p_ref[r + 1:r + 2, :] = src[0:1, half:D_MODEL]
        p_ref[ROW_BGLU_D:ROW_BGLU_D + 1, :] = dbg_ref[...]
        p_ref[ROW_BGLU_D + 1:ROW_BGLU_D + 2, :] = dd_ref[...]
        p_ref[ROW_CONV:ROW_CONV + SUBLANES, :] = dcw_ref[...]
        p_ref[ROW_LOSS:ROW_LOSS + SUBLANES, 0:LANES] = loss_ref[...]
        s5 = slice(ROW_S5, ROW_S5 + N_GROUPS)
        p_ref[s5, LANE_A_RE:LANE_A_RE + STATE] = _select_dot(group_sum, d_are)
        p_ref[s5, LANE_A_IM:LANE_A_IM + STATE] = _select_dot(group_sum, d_aim)
        p_ref[s5, LANE_LOG_DT:LANE_LOG_DT + LANES] = _select_dot(group_sum, jnp.broadcast_to(d_ldt, (rows_gh, LANES)))

    operands = (a_re_x, a_im_x, log_dt_x, b_re, b_im, g_ab_re, g_ab_im, dbb_re_d, dbb_im_d,
                loss_t, dg8, dgf, dbg, dd, dcw, dc_re_d, dc_im_d)
    out_shape = (_out((PACK_ROWS, PACK_W), F32),
                 _out((rows_gh, 2 * STATE), BF16),
                 _out((rows_gh, 2 * STATE), BF16))
    return _pcall(body, name="ssm_disc_bwd_pack", grid=(1,), out_shape=out_shape,
                  in_specs=_whole_specs(operands), out_specs=tuple(_whole_specs(out_shape)),
                  scratch_shapes=[pltpu.VMEM((rows_gh, STATE), F32), pltpu.VMEM((rows_gh, STATE), F32)],
                  compiler_params=_params(1))(*operands)


def _s5_prepare(are, aim, ldt, bre, bim, cre, cim,
                o_ax_re, o_ax_im, o_ldt_x, o_ab_re, o_ab_im, o_bb_re, o_bb_im, o_c_re, o_c_imn):
    rows_gh = N_GROUPS * GROUP
    rep = (lax.broadcasted_iota(jnp.int32, (rows_gh, N_GROUPS), 0) // GROUP
           == lax.broadcasted_iota(jnp.int32, (rows_gh, N_GROUPS), 1)).astype(BF16)
    eye = (lax.broadcasted_iota(jnp.int32, (N_GROUPS, N_GROUPS), 0)
           == lax.broadcasted_iota(jnp.int32, (N_GROUPS, N_GROUPS), 1)).astype(F32)
    ldt_col = jnp.sum(eye * ldt[...], axis=1, keepdims=True)
    a_re_x = _select_dot(rep, are[...])
    a_im_x = _select_dot(rep, aim[...])
    ldt_x = _select_dot(rep, jnp.broadcast_to(ldt_col, (N_GROUPS, LANES)))[:, 0:1]
    o_ax_re[...] = a_re_x
    o_ax_im[...] = a_im_x
    o_ldt_x[...] = ldt_x
    ab_re, ab_im, bb_re, bb_im = _disc(a_re_x, a_im_x, ldt_x, bre[...], bim[...])
    for j in range(N_JBLK):
        first = [(j * SUBLANES + gi) * GROUP for gi in range(SUBLANES)]
        o_ab_re[j] = jnp.concatenate([ab_re[r:r + 1, :] for r in first], axis=1)
        o_ab_im[j] = jnp.concatenate([ab_im[r:r + 1, :] for r in first], axis=1)
    for o, v in ((o_bb_re, bb_re), (o_bb_im, bb_im), (o_c_re, cre[...]), (o_c_imn, -cim[...])):
        for j in range(N_JBLK):
            for gi in range(SUBLANES):
                r0 = (j * SUBLANES + gi) * GROUP
                parts = [v[r0:r0 + GROUP, :] if k == gi else jnp.zeros((GROUP, STATE), F32) for k in range(SUBLANES)]
                o[j, gi * GROUP:(gi + 1) * GROUP, :] = jnp.concatenate(parts, axis=1).astype(BF16)


def _in_proj(order, x2, g1, w_in_b, s5):
    n = x2.shape[0]
    tm = min(IN_TILE, n)
    n_tiles = n // tm
    n_s5_in = len(s5)
    n_s5_out = 9

    def body(order_ref, x_ref, g_ref, w_ref, *refs):
        s5_in = refs[:n_s5_in]
        xn_ref, proj_ref, wall_ref = refs[n_s5_in:n_s5_in + 3]
        s5_out = refs[n_s5_in + 3:n_s5_in + 3 + n_s5_out]
        xn_scr, wbuf, send_sems, recv_sems, loc_sems, out_sems = refs[n_s5_in + 3 + n_s5_out:]
        k = pl.program_id(0)
        i = pl.program_id(1)

        def slot(dev):
            return wbuf.at[dev // 2, :, pl.ds(pl.multiple_of((dev % 2) * COLS_PER_DEV, LANES), COLS_PER_DEV)]

        gather = _TwoLevelGather([w_ref], [slot], send_sems, recv_sems, loc_sems)

        @pl.when((k == 0) & (i == 0))
        def _():
            gather.start()

        def own_chip():
            gather.wait_own()
            gather.wait_sibling()

        def x_chip():
            gather.neighbours_landed()
            gather.wait_passed_on(0)

        def diag_chip():
            gather.diagonal_landed()
            gather.wait_passed_on(2)

        arrivals = [own_chip, x_chip, functools.partial(gather.wait_passed_on, 1), diag_chip]
        for kk, arrived in enumerate(arrivals):
            @pl.when((k == kk) & (i == 0))
            def _(arrived=arrived):
                arrived()

        rows = pl.ds(pl.multiple_of(i * tm, tm), tm)

        @pl.when(k == 0)
        def _():
            x = x_ref[...]
            r = lax.rsqrt(jnp.mean(x * x, axis=-1, keepdims=True) + EPS)
            xn = ((x * r) * g_ref[...]).astype(BF16)
            xn_scr[rows, :] = xn
            xn_ref[...] = xn

        proj_ref[...] = _dot(xn_scr[rows, :], wbuf[order_ref[k]])

        @pl.when((k == 0) & (i == n_tiles - 1))
        def _():
            _s5_prepare(*s5_in, *s5_out)

        @pl.when((k == N_CHIP - 1) & (i == n_tiles - 1))
        def _():
            gather.wait_sends()
            outs = [pltpu.make_async_copy(wbuf.at[q], wall_ref.at[:, q * COLS_PER_CHIP:(q + 1) * COLS_PER_CHIP],
                                          out_sems.at[q]) for q in range(N_CHIP)]
            for cp in outs:
                cp.start()
            for cp in outs:
                cp.wait()

    tile_once = lambda k, i, order: (jnp.where(k == 0, i, n_tiles - 1), 0)
    whole = lambda shape: pl.BlockSpec(shape, lambda k, i, order: (0,) * len(shape))
    rows_gh = N_GROUPS * GROUP
    s5_out_shapes = ([(rows_gh, STATE), F32], [(rows_gh, STATE), F32], [(rows_gh, 1), F32],
                     [(N_JBLK, 1, JB_ST), F32], [(N_JBLK, 1, JB_ST), F32]) + ([(N_JBLK, JB_CH, JB_ST), BF16],) * 4
    grid_spec = pltpu.PrefetchScalarGridSpec(
        num_scalar_prefetch=1, grid=(N_CHIP, n_tiles),
        in_specs=[pl.BlockSpec((tm, D_MODEL), tile_once),
                  whole((1, D_MODEL)),
                  HBM_SPEC,
                  *(whole(a.shape) for a in s5)],
        out_specs=(pl.BlockSpec((tm, D_MODEL), tile_once),
                   pl.BlockSpec((tm, COLS_PER_CHIP), lambda k, i, order: (i, order[k])),
                   HBM_SPEC,
                   *(whole(shape) for shape, _ in s5_out_shapes)),
        scratch_shapes=[pltpu.VMEM((n, D_MODEL), BF16), pltpu.VMEM((N_CHIP, D_MODEL, COLS_PER_CHIP), BF16),
                        pltpu.SemaphoreType.DMA((7,)), pltpu.SemaphoreType.DMA((7,)), pltpu.SemaphoreType.DMA((1,)),
                        pltpu.SemaphoreType.DMA((N_CHIP,))])
    outs = _pcall(
        body, name="in_proj", grid_spec=grid_spec,
        out_shape=(_out((n, D_MODEL), BF16), _out((n, IN_COLS), F32),
                   _out((D_MODEL, IN_COLS), BF16),
                   *(_out(shape, dt) for shape, dt in s5_out_shapes)),
        compiler_params=_params(2),
    )(order, x2, g1, w_in_b, *s5)
    return outs[0], outs[1], outs[2], outs[3:]


def _cmul(p, q):
    return p[0] * q[0] - p[1] * q[1], p[0] * q[1] + p[1] * q[0]


def _scan_tables(ar, ai, width, reverse):
    pows = [(ar, ai)]
    for _ in range(SUBLANES - 1):
        pows.append(_cmul(pows[-1], (ar, ai)))
    row = lax.broadcasted_iota(jnp.int32, (SUBLANES, width), 0)

    def bc(v):
        return jnp.broadcast_to(v, (SUBLANES, width))

    levels = []
    for k in (1, 2, 4):
        keep = (row <= SUBLANES - 1 - k) if reverse else (row >= k)
        levels.append((jnp.where(keep, bc(pows[k - 1][0]), 0.0), jnp.where(keep, bc(pows[k - 1][1]), 0.0)))
    cre = jnp.zeros((SUBLANES, width), F32)
    cim = jnp.zeros((SUBLANES, width), F32)
    for r in range(SUBLANES):
        e = (SUBLANES - r) if reverse else (r + 1)
        cre = jnp.where(row == r, bc(pows[e - 1][0]), cre)
        cim = jnp.where(row == r, bc(pows[e - 1][1]), cim)
    return levels, (cre, cim)


def _load_chunked(src_ref, b, dst_ref, n_rows):
    n_blk = n_rows // SUBLANES
    for i in range(n_blk):
        dst_ref[b, i * SUBLANES:(i + 1) * SUBLANES, :] = src_ref[b, pl.ds(i, SUBLANES, stride=n_blk), :]


def _store_chunked(val, dst_ref, b, n_rows):
    n_blk = n_rows // SUBLANES
    for i in range(n_blk):
        dst_ref[b, pl.ds(i, SUBLANES, stride=n_blk), :] = val[i * SUBLANES:(i + 1) * SUBLANES, :]


def _chunk_scan(re_ref, im_ref, bs, car_ref, ar, ai, n_rows, reverse, on_block=None):
    width = re_ref.shape[2]
    n_blk = n_rows // SUBLANES
    shape = (SUBLANES, width)
    abr = jnp.broadcast_to(ar, shape)
    abi = jnp.broadcast_to(ai, shape)
    order = list(range(n_blk - 1, -1, -1)) if reverse else list(range(n_blk))

    def blk(ref, b, i):
        return ref[b, i * SUBLANES:(i + 1) * SUBLANES, :]

    def step(state, b, i):
        sr, si = state
        return abr * sr - abi * si + blk(re_ref, b, i), abr * si + abi * sr + blk(im_ref, b, i)

    finals = {b: (blk(re_ref, b, order[0]), blk(im_ref, b, order[0])) for b in bs}
    for i in order[1:]:
        for b in bs:
            finals[b] = step(finals[b], b, i)

    mr, mi = ar, ai
    for _ in range(n_blk.bit_length() - 1):
        mr, mi = _cmul((mr, mi), (mr, mi))
    levels, _ = _scan_tables(mr, mi, width, reverse)
    mbr = jnp.broadcast_to(mr, shape)
    mbi = jnp.broadcast_to(mi, shape)
    row = lax.broadcasted_iota(jnp.int32, shape, 0)
    edge_in = SUBLANES - 1 if reverse else 0
    edge_out = 0 if reverse else SUBLANES - 1
    sh1 = SUBLANES - 1 if reverse else 1
    states = {}
    for b in bs:
        fr, fi = finals[b]
        gr = jnp.where(row == edge_in, jnp.broadcast_to(car_ref[b, 0:1, :], shape), pltpu.roll(fr, sh1, 0))
        gi = jnp.where(row == edge_in, jnp.broadcast_to(car_ref[b, 1:2, :], shape), pltpu.roll(fi, sh1, 0))
        for (lr, li), k in zip(levels, (1, 2, 4)):
            sh = (SUBLANES - k) if reverse else k
            sr = pltpu.roll(gr, sh, 0)
            si = pltpu.roll(gi, sh, 0)
            gr, gi = gr + (lr * sr - li * si), gi + (lr * si + li * sr)
        car_ref[b, 0:1, :] = (fr + (mbr * gr - mbi * gi))[edge_out:edge_out + 1, :]
        car_ref[b, 1:2, :] = (fi + (mbr * gi + mbi * gr))[edge_out:edge_out + 1, :]
        states[b] = (gr, gi)

    for i in order:
        for b in bs:
            states[b] = step(states[b], b, i)
            re_ref[b, i * SUBLANES:(i + 1) * SUBLANES, :] = states[b][0]
            im_ref[b, i * SUBLANES:(i + 1) * SUBLANES, :] = states[b][1]
            if on_block is not None:
                on_block(b, i, *states[b])


def _ssm_fwd(u, bb_re, bb_im, c_re_t, c_imn_t, d_row, ab_re, ab_im, w_out_b, w_glu_b, conv_p, n_seq, seq):
    tt = min(SCAN_TILE, seq)
    nt = seq // tt

    def body(u_ref, bbre, bbim, cre, cimn, d_ref, are, aim, wout_ref, wglu_ref, cw_ref,
             sre_ref, sim_ref, y_ref, oout_ref, oglu_ref, ocw_ref,
             up_ref, car_ref, send_sems, recv_sems, loc_sems):
        j = pl.program_id(0)
        t = pl.program_id(1)
        gather = _TwoLevelGather(
            [wout_ref, wglu_ref, cw_ref],
            [lambda dev: oout_ref.at[pl.ds(pl.multiple_of(dev * OUT_ROWS_PER_DEV, OUT_ROWS_PER_DEV), OUT_ROWS_PER_DEV), :],
             lambda dev: oglu_ref.at[pl.ds(pl.multiple_of(dev * GLU_ROWS_PER_DEV, GLU_ROWS_PER_DEV), GLU_ROWS_PER_DEV), :],
             lambda dev: ocw_ref.at[dev]],
            send_sems, recv_sems, loc_sems)

        @pl.when((j == 0) & (t == 0))
        def _():
            gather.start()

        @pl.when((j == N_JBLK // 2) & (t == 0))
        def _():
            gather.neighbours_landed()

        @pl.when((j == N_JBLK - 1) & (t == 0))
        def _():
            gather.diagonal_landed()

        @pl.when(t == 0)
        def _():
            car_ref[...] = jnp.zeros_like(car_ref)

        bs = list(range(n_seq))
        for b in bs:
            _load_chunked(u_ref, b, up_ref, tt)
        for b in bs:
            ub = up_ref[b].astype(BF16)
            sre_ref[b] = _dot(ub, bbre[0])
            sim_ref[b] = _dot(ub, bbim[0])
            _chunk_scan(sre_ref, sim_ref, [b], car_ref, are[0], aim[0], tt, reverse=False)
        for b in bs:
            yp = (_dot_nt(sre_ref[b].astype(BF16), cre[0]) + _dot_nt(sim_ref[b].astype(BF16), cimn[0])
                  + d_ref[...] * up_ref[b])
            _store_chunked(yp, y_ref, b, tt)

        @pl.when((j == N_JBLK - 1) & (t == nt - 1))
        def _():
            gather.finish()

    tok = lambda j, t: (0, t, j)
    blk3 = lambda j, t: (j, 0, 0)
    row = lambda j, t: (0, j)
    st = _out((n_seq, seq, N_JBLK * JB_ST), F32)
    n_arr = 3
    return _pcall(
        body, name="ssm_fwd", grid=(N_JBLK, nt),
        out_shape=(st, st, _out((n_seq, seq, SSM_W), F32),
                   _out((D_MODEL, D_MODEL), BF16), _out((SSM_W, SSM_W), BF16),
                   _out((N_DEV, SUBLANES, LANES), F32)),
        in_specs=[pl.BlockSpec((n_seq, tt, JB_CH), tok),
                  pl.BlockSpec((1, JB_CH, JB_ST), blk3), pl.BlockSpec((1, JB_CH, JB_ST), blk3),
                  pl.BlockSpec((1, JB_CH, JB_ST), blk3), pl.BlockSpec((1, JB_CH, JB_ST), blk3),
                  pl.BlockSpec((1, JB_CH), row), pl.BlockSpec((1, 1, JB_ST), blk3), pl.BlockSpec((1, 1, JB_ST), blk3),
                  HBM_SPEC, HBM_SPEC, HBM_SPEC],
        out_specs=(pl.BlockSpec((n_seq, tt, JB_ST), tok), pl.BlockSpec((n_seq, tt, JB_ST), tok),
                   pl.BlockSpec((n_seq, tt, JB_CH), tok), HBM_SPEC, HBM_SPEC, HBM_SPEC),
        scratch_shapes=[pltpu.VMEM((n_seq, tt, JB_CH), F32), pltpu.VMEM((n_seq, SUBLANES, JB_ST), F32),
                        pltpu.SemaphoreType.DMA((7 * n_arr,)), pltpu.SemaphoreType.DMA((7 * n_arr,)),
                        pltpu.SemaphoreType.DMA((n_arr,))],
        compiler_params=_params(2),
    )(u, bb_re, bb_im, c_re_t, c_imn_t, d_row, ab_re, ab_im, w_out_b, w_glu_b, conv_p)


def _ssm_bwd(dy, u, s_re, s_im, bb_re, bb_im, c_re_t, c_imn_t, d_row, ab_re, ab_im, g_out, g_glu, n_seq, seq):
    tt = min(SCAN_TILE, seq)
    nt = seq // tt
    rows8 = tt // SUBLANES

    def body(dy_ref, u_ref, sre_ref, sim_ref, pre_ref, pim_ref, bbre, bbim, cre, cimn, d_ref, are, aim,
             gout_ref, gglu_ref,
             du_ref, dcre_ref, dcim_ref, dbbre_ref, dbbim_ref, dare_ref, daim_ref, dd_ref, rout_ref, rglu_ref,
             lre_ref, lim_ref, dyp_ref, up_ref, car_ref, send_sems, recv_sems, loc_sems):
        j = pl.program_id(0)
        tr = pl.program_id(1)

        def exchange():
            return _direct_copies(lambda pid: [gout_ref.at[pid], gglu_ref.at[pid]], [rout_ref, rglu_ref],
                                  send_sems, recv_sems, loc_sems)

        @pl.when((j == 0) & (tr == 0))
        def _():
            mine, sends = exchange()
            for cp in mine + sends:
                cp.start()

        @pl.when(tr == 0)
        def _():
            car_ref[...] = jnp.zeros_like(car_ref)
            for r in (dcre_ref, dcim_ref, dbbre_ref, dbbim_ref, dare_ref, daim_ref, dd_ref):
                r[...] = jnp.zeros_like(r)

        first = tr == nt - 1
        row = lax.broadcasted_iota(jnp.int32, (SUBLANES, JB_ST), 0)
        n_blk = tt // SUBLANES
        bs = list(range(n_seq))
        for b in bs:
            _load_chunked(dy_ref, b, dyp_ref, tt)
            _load_chunked(u_ref, b, up_ref, tt)
        for b in bs:
            dyb = dyp_ref[b].astype(BF16)
            lre_ref[b] = _dot(dyb, cre[0])
            lim_ref[b] = _dot(dyb, cimn[0])
        acc = {b: [jnp.zeros((SUBLANES, JB_ST), F32), jnp.zeros((SUBLANES, JB_ST), F32)] for b in bs}

        def on_block(b, i, lr, li):
            if i > 0:
                spr = sre_ref[b, (i - 1) * SUBLANES:i * SUBLANES, :]
                spi = sim_ref[b, (i - 1) * SUBLANES:i * SUBLANES, :]
            else:
                hr = jnp.where(first, 0.0, pre_ref[b, SUBLANES - 1:SUBLANES, :])
                hi = jnp.where(first, 0.0, pim_ref[b, SUBLANES - 1:SUBLANES, :])
                last_r = sre_ref[b, (n_blk - 1) * SUBLANES:n_blk * SUBLANES, :]
                last_i = sim_ref[b, (n_blk - 1) * SUBLANES:n_blk * SUBLANES, :]
                spr = jnp.where(row == 0, jnp.broadcast_to(hr, row.shape), pltpu.roll(last_r, 1, 0))
                spi = jnp.where(row == 0, jnp.broadcast_to(hi, row.shape), pltpu.roll(last_i, 1, 0))
            acc[b][0] = acc[b][0] + (lr * spr + li * spi)
            acc[b][1] = acc[b][1] + (li * spr - lr * spi)

        _chunk_scan(lre_ref, lim_ref, bs, car_ref, are[0], -aim[0], tt, reverse=True, on_block=on_block)
        for b in bs:
            dare_ref[...] += jnp.sum(acc[b][0], axis=0, keepdims=True)
            daim_ref[...] += jnp.sum(acc[b][1], axis=0, keepdims=True)
            dyp = dyp_ref[b]
            up = up_ref[b]
            dyb = dyp.astype(BF16)
            ub = up.astype(BF16)
            lrb = lre_ref[b].astype(BF16)
            lib = lim_ref[b].astype(BF16)
            dup = d_ref[...] * dyp + _dot_nt(lrb, bbre[0]) + _dot_nt(lib, bbim[0])
            _store_chunked(dup, du_ref, b, tt)
            dbbre_ref[0] += _dot_tn(ub, lrb)
            dbbim_ref[0] += _dot_tn(ub, lib)
            dcre_ref[0] += _dot_tn(dyb, sre_ref[b].astype(BF16))
            dcim_ref[0] += _dot_tn(dyb, sim_ref[b].astype(BF16))
            dd_ref[...] += jnp.sum(dyp * up, axis=0, keepdims=True)

        @pl.when((j == N_JBLK - 1) & (tr == nt - 1))
        def _():
            mine, sends = exchange()
            for cp in sends + mine:
                cp.wait()

    tok = lambda j, t: (0, nt - 1 - t, j)
    halo = lambda j, t: (0, jnp.maximum((nt - 1 - t) * rows8 - 1, 0), j)
    blk3 = lambda j, t: (j, 0, 0)
    row1 = lambda j, t: (0, j)
    acc_shape = _out((N_JBLK, JB_CH, JB_ST), F32)
    return _pcall(
        body, name="ssm_bwd", grid=(N_JBLK, nt),
        out_shape=(_out((n_seq, seq, SSM_W), F32), acc_shape, acc_shape, acc_shape, acc_shape,
                   _out((1, N_JBLK * JB_ST), F32), _out((1, N_JBLK * JB_ST), F32),
                   _out((1, SSM_W), F32),
                   _out((N_DEV,) + g_out.shape[1:], F32),
                   _out((N_DEV,) + g_glu.shape[1:], F32)),
        in_specs=[pl.BlockSpec((n_seq, tt, JB_CH), tok), pl.BlockSpec((n_seq, tt, JB_CH), tok),
                  pl.BlockSpec((n_seq, tt, JB_ST), tok), pl.BlockSpec((n_seq, tt, JB_ST), tok),
                  pl.BlockSpec((n_seq, SUBLANES, JB_ST), halo), pl.BlockSpec((n_seq, SUBLANES, JB_ST), halo),
                  pl.BlockSpec((1, JB_CH, JB_ST), blk3), pl.BlockSpec((1, JB_CH, JB_ST), blk3),
                  pl.BlockSpec((1, JB_CH, JB_ST), blk3), pl.BlockSpec((1, JB_CH, JB_ST), blk3),
                  pl.BlockSpec((1, JB_CH), row1), pl.BlockSpec((1, 1, JB_ST), blk3), pl.BlockSpec((1, 1, JB_ST), blk3),
                  HBM_SPEC, HBM_SPEC],
        out_specs=(pl.BlockSpec((n_seq, tt, JB_CH), tok),
                   pl.BlockSpec((1, JB_CH, JB_ST), blk3), pl.BlockSpec((1, JB_CH, JB_ST), blk3),
                   pl.BlockSpec((1, JB_CH, JB_ST), blk3), pl.BlockSpec((1, JB_CH, JB_ST), blk3),
                   pl.BlockSpec((1, JB_ST), row1), pl.BlockSpec((1, JB_ST), row1), pl.BlockSpec((1, JB_CH), row1),
                   HBM_SPEC, HBM_SPEC),
        scratch_shapes=[pltpu.VMEM((n_seq, tt, JB_ST), F32), pltpu.VMEM((n_seq, tt, JB_ST), F32),
                        pltpu.VMEM((n_seq, tt, JB_CH), F32), pltpu.VMEM((n_seq, tt, JB_CH), F32),
                        pltpu.VMEM((n_seq, SUBLANES, JB_ST), F32),
                        pltpu.SemaphoreType.DMA((7 * 2,)), pltpu.SemaphoreType.DMA((7 * 2,)),
                        pltpu.SemaphoreType.DMA((2,))],
        compiler_params=_params(2),
    )(dy, u, s_re, s_im, s_re, s_im, bb_re, bb_im, c_re_t, c_imn_t, d_row, ab_re, ab_im, g_out, g_glu)


def _mix(x2, tgt2, y, proj, gf, b_glu, conv8, w_glu_f, w_out_f, seq):
    n = x2.shape[0]
    tm = TOK_TILE
    tiles_per_seq = seq // tm
    rows8 = tm // SUBLANES

    def body(x_ref, t_ref, y_ref, zs_ref, h_ref, bc_ref, cc_ref, zc_ref, hp_ref, ccp_ref,
             gf_ref, bg_ref, cw_ref, wg_ref, wo_ref,
             dh2_ref, dy_ref, dzs_ref, dbc_ref, dzc_ref, dyc_ref,
             dwo_ref, dwg_ref, loss_ref, dgf_ref, dbg_ref, dcw_ref):
        i = pl.program_id(0)

        @pl.when(i == 0)
        def _():
            for r in (dwo_ref, dwg_ref, loss_ref, dgf_ref, dbg_ref, dcw_ref):
                r[...] = jnp.zeros_like(r)

        yv = y_ref[...]
        y1, dgelu = _gelu_and_grad(yv)
        y1b = y1.astype(BF16)
        gate = _sigmoid(_dot(y1b, wg_ref[...]) + bg_ref[...])
        y2 = y1 * gate
        szs, dszs = _silu_and_grad(zs_ref[...])
        yssm = y2 * szs
        hv = h_ref[...]
        ccv = cc_ref[...]
        bcv = bc_ref[...]
        v = ccv * hv
        first = (i % tiles_per_seq) == 0
        vhalo = jnp.where(first, 0.0, ccp_ref[...] * hp_ref[...])
        v1 = _shift_down(v, vhalo, 1)
        v2 = _shift_down(v, vhalo, 2)
        w0 = cw_ref[0:1, :]
        w1 = cw_ref[1:2, :]
        w2 = cw_ref[2:3, :]
        yc = w0 * v2 + w1 * v1 + w2 * v
        szc, dszc = _silu_and_grad(zc_ref[...])
        yconv = (bcv * yc) * szc
        ysb = yssm.astype(BF16)
        ycb = yconv.astype(BF16)
        h2 = x_ref[...] + _dot(ysb, wo_ref[0:SSM_W, :]) + _dot(ycb, wo_ref[SSM_W:, :])
        r2 = lax.rsqrt(jnp.mean(h2 * h2, axis=-1, keepdims=True) + EPS)
        hn = h2 * r2
        gfv = gf_ref[...]
        err = hn * gfv - t_ref[...]
        loss_ref[...] += 0.5 * jnp.sum(jnp.mean(err * err, axis=-1, keepdims=True))
        dout = err * (1.0 / D_MODEL)
        dgf_ref[...] += jnp.sum(dout * hn, axis=0, keepdims=True)
        dn = dout * gfv
        dh2 = r2 * (dn - hn * jnp.mean(dn * hn, axis=-1, keepdims=True))
        dh2_ref[...] = dh2
        dh2b = dh2.astype(BF16)
        dwo_ref[0:SSM_W, :] += _dot_tn(ysb, dh2b)
        dwo_ref[SSM_W:, :] += _dot_tn(ycb, dh2b)
        dyssm = _dot_nt(dh2b, wo_ref[0:SSM_W, :])
        dyconv = _dot_nt(dh2b, wo_ref[SSM_W:, :])
        dy2 = dyssm * szs
        dzs_ref[...] = (dyssm * y2 * dszs).astype(BF16)
        dgp = dy2 * y1 * (gate * (1.0 - gate))
        dgpb = dgp.astype(BF16)
        dy1 = dy2 * gate + _dot_nt(dgpb, wg_ref[...])
        dwg_ref[...] += _dot_tn(y1b, dgpb)
        dbg_ref[...] += jnp.sum(dgp, axis=0, keepdims=True)
        dy_ref[...] = dy1 * dgelu
        dbc_ref[...] = (dyconv * yc * szc).astype(BF16)
        dyc = dyconv * bcv * szc
        dyc_ref[...] = dyc
        dzc_ref[...] = (dyconv * bcv * yc * dszc).astype(BF16)
        dcw_ref[0:1, :] += jnp.sum(dyc * v2, axis=0, keepdims=True)
        dcw_ref[1:2, :] += jnp.sum(dyc * v1, axis=0, keepdims=True)
        dcw_ref[2:3, :] += jnp.sum(dyc * v, axis=0, keepdims=True)

    tile_d = pl.BlockSpec((tm, D_MODEL), lambda i: (i, 0))
    tile_s = pl.BlockSpec((tm, SSM_W), lambda i: (i, 0))
    seg_of = lambda c: pl.BlockSpec((tm, SSM_W), lambda i: (i, c))
    halo_of = lambda c: pl.BlockSpec((SUBLANES, SSM_W), lambda i: (jnp.maximum(i * rows8 - 1, 0), c))
    const = lambda shape: pl.BlockSpec(shape, lambda i: (0,) * len(shape))
    seg = _out((n, SSM_W), F32)
    seg_b = _out((n, SSM_W), BF16)
    return _pcall(
        body, name="mix", grid=(n // tm,),
        out_shape=(_out((n, D_MODEL), F32), seg, seg_b, seg_b, seg_b, seg,
                   _out((D_MODEL, D_MODEL), F32), _out((SSM_W, SSM_W), F32),
                   _out((SUBLANES, LANES), F32), _out((1, D_MODEL), F32),
                   _out((1, SSM_W), F32), _out((SUBLANES, CONV_W), F32)),
        in_specs=[tile_d, tile_d, tile_s, seg_of(SEG_ZS), seg_of(SEG_H), seg_of(SEG_BC), seg_of(SEG_CC), seg_of(SEG_ZC),
                  halo_of(SEG_H), halo_of(SEG_CC),
                  const((1, D_MODEL)), const((1, SSM_W)), const((SUBLANES, CONV_W)),
                  const((SSM_W, SSM_W)), const((D_MODEL, D_MODEL))],
        out_specs=(tile_d, tile_s, tile_s, tile_s, tile_s, tile_s,
                   const((D_MODEL, D_MODEL)), const((SSM_W, SSM_W)), const((SUBLANES, LANES)),
                   const((1, D_MODEL)), const((1, SSM_W)), const((SUBLANES, CONV_W))),
        compiler_params=_params(1),
    )(x2, tgt2, y, proj, proj, proj, proj, proj, proj, proj, gf, b_glu, conv8, w_glu_f, w_out_f)


def _in_bwd(x2, dh2, du, dzs, dyc, proj, dbc, dzc, g1, conv8, w_full, seq):
    n = x2.shape[0]
    tm = TOK_TILE
    n_tiles = n // tm
    tiles_per_seq = seq // tm
    rows8 = tm // SUBLANES
    n_blk8 = n // SUBLANES

    def body(x_ref, dh2_ref, du_ref, dzs_ref, dyc_ref, dycn_ref, h_ref, cc_ref, dbc_ref, dzc_ref,
             g_ref, cw_ref, w_ref, gx_ref, dp_ref, dg_ref):
        i = pl.program_id(0)

        @pl.when(i == 0)
        def _():
            dg_ref[...] = jnp.zeros_like(dg_ref)

        dyc = dyc_ref[...]
        last = (i % tiles_per_seq) == tiles_per_seq - 1
        nhalo = jnp.where(last, 0.0, dycn_ref[...])
        dv = (cw_ref[2:3, :] * dyc + cw_ref[1:2, :] * _shift_up(dyc, nhalo, 1)
              + cw_ref[0:1, :] * _shift_up(dyc, nhalo, 2))
        parts = (du_ref[...], dzs_ref[...], dv * cc_ref[...], dbc_ref[...], dv * h_ref[...], dzc_ref[...])
        dxn = jnp.zeros((tm, D_MODEL), F32)
        for k, p in enumerate(parts):
            pb = p.astype(BF16)
            dp_ref[:, k * SSM_W:(k + 1) * SSM_W] = pb
            dxn = dxn + _dot_nt(pb, w_ref[:, k * SSM_W:(k + 1) * SSM_W])
        x = x_ref[...]
        r = lax.rsqrt(jnp.mean(x * x, axis=-1, keepdims=True) + EPS)
        xh = x * r
        dg_ref[...] += jnp.sum(dxn * xh, axis=0, keepdims=True)
        dn = dxn * g_ref[...]
        gx_ref[...] = dh2_ref[...] + r * (dn - xh * jnp.mean(dn * xh, axis=-1, keepdims=True))

    tile_d = pl.BlockSpec((tm, D_MODEL), lambda i: (i, 0))
    tile_s = pl.BlockSpec((tm, SSM_W), lambda i: (i, 0))
    seg_of = lambda c: pl.BlockSpec((tm, SSM_W), lambda i: (i, c))
    nhalo = pl.BlockSpec((SUBLANES, SSM_W), lambda i: (jnp.minimum((i + 1) * rows8, n_blk8 - 1), 0))
    const = lambda shape: pl.BlockSpec(shape, lambda i: (0,) * len(shape))
    return _pcall(
        body, name="in_bwd", grid=(n_tiles,),
        out_shape=(_out((n, D_MODEL), F32), _out((n, IN_COLS), BF16),
                   _out((SUBLANES, D_MODEL), F32)),
        in_specs=[tile_d, tile_d, tile_s, tile_s, tile_s, nhalo, seg_of(SEG_H), seg_of(SEG_CC), tile_s, tile_s,
                  const((1, D_MODEL)), const((SUBLANES, CONV_W)), const((D_MODEL, IN_COLS))],
        out_specs=(tile_d, pl.BlockSpec((tm, IN_COLS), lambda i: (i, 0)), const((SUBLANES, D_MODEL))),
        compiler_params=_params(1),
    )(x2, dh2, du, dzs, dyc, dyc, proj, proj, dbc, dzc, g1, conv8, w_full)


_HALF_BLOCKS = ((0, 0), (0, 1), (1, 0), (2, 0), (1, 1), (2, 1), (3, 0), (3, 1))


def _dw_in_exchange(chips, xn, dproj, smalls):
    n = xn.shape[0]
    tk = min(1024, n)
    nk = n // tk
    piece = (D_MODEL, COLS_PER_DEV)
    hr = D_MODEL // 2
    n_half = len(_HALF_BLOCKS)
    n_small = len(smalls)
    assert _HALF_BLOCKS[0][1] == 0 and _HALF_BLOCKS[1][1] == 1
    order = jnp.stack([chips[b] for b, _ in _HALF_BLOCKS]
                      + [jnp.int32(t) for _, t in _HALF_BLOCKS]).astype(jnp.int32)

    def body(order_ref, xn_hbm, dp_ref, *refs):
        sm_refs = refs[:n_small]
        own_ref, rchip_ref = refs[n_small:n_small + 2]
        rsm_refs = refs[n_small + 2:2 * n_small + 2]
        (xn_ref, acc, stage, rbuf, kbuf, relay_in, xn_sems, give_send, give_recv, keep_send, keep_recv,
         relay_send, relay_recv, sm_send, sm_recv, sm_loc) = refs[2 * n_small + 2:]
        s = pl.program_id(0)

        def xn_copy(kk, t):
            rows = pl.ds(pl.multiple_of(kk * tk, tk), tk)
            return pltpu.make_async_copy(xn_hbm.at[rows, t * hr:(t + 1) * hr], xn_ref.at[t, rows, :],
                                         xn_sems.at[2 * kk + t])

        @pl.when(s == 0)
        def _():
            for kk in range(nk):
                for t in range(2):
                    xn_copy(kk, t).start()
            xn_copy(0, 0).wait()

        @pl.when(s == 1)
        def _():
            xn_copy(0, 1).wait()

        x, y, c = _mesh_pos()
        sib = (x, y, 1 - c)
        y_nbr, x_nbr = (x, 1 - y, c), (1 - x, y, c)
        gather = _TwoLevelGather(list(sm_refs), [functools.partial(lambda r, dev: r.at[dev], r) for r in rsm_refs],
                                 sm_send, sm_recv, sm_loc)

        def give(h):
            cols = pl.ds(pl.multiple_of((1 - c) * COLS_PER_DEV, LANES), COLS_PER_DEV)
            return pltpu.make_async_remote_copy(src_ref=acc.at[h % 2, :, cols], dst_ref=stage.at[h],
                                                send_sem=give_send.at[h], recv_sem=give_recv.at[h],
                                                device_id=sib, device_id_type=MESH)

        def relay(r):
            return pltpu.make_async_remote_copy(src_ref=rbuf.at[r], dst_ref=relay_in.at[r],
                                                send_sem=relay_send.at[r], recv_sem=relay_recv.at[r],
                                                device_id=(x_nbr, y_nbr)[r], device_id_type=MESH)

        def keep(q):
            return pltpu.make_async_remote_copy(src_ref=kbuf.at[q], dst_ref=rchip_ref.at[q // 2, pl.ds((q % 2) * hr, hr), :],
                                                send_sem=keep_send.at[q], recv_sem=keep_recv.at[q],
                                                device_id=(y_nbr, x_nbr)[q // 2], device_id_type=MESH)

        def chip_sum(h):
            give(h).wait_recv()
            mine = [acc[h % 2, :, cc * COLS_PER_DEV:(cc + 1) * COLS_PER_DEV] for cc in range(2)]
            return jnp.where(c == 0, mine[0], mine[1]) + stage[h]

        @pl.when(s == 0)
        def _():
            gather.start()

        @pl.when(s == 2)
        def _():
            gather.neighbours_landed()

        @pl.when(s == n_half - 2)
        def _():
            gather.diagonal_landed()

        for k in range(2, n_half):
            @pl.when(s == k)
            def _(k=k):
                give(k - 2).wait_send()

        slot = s % 2
        t_half = order_ref[n_half + s]
        acc[slot] = _dot_tn(xn_ref[t_half, pl.ds(0, tk), :], dp_ref[pl.ds(0, tk), :])

        def kstep(kk, carry):
            for t in range(2):
                @pl.when(s == t)
                def _(t=t):
                    xn_copy(kk, t).wait()

            off = pl.multiple_of(kk * tk, tk)
            acc[slot] += _dot_tn(xn_ref[t_half, pl.ds(off, tk), :], dp_ref[pl.ds(off, tk), :])
            return carry

        n_first = max(1, (3 * nk) // 8)
        lax.fori_loop(1, n_first, kstep, 0)
        for k in range(1, n_half):
            @pl.when(s == k)
            def _(k=k):
                h = k - 1
                b, t = _HALF_BLOCKS[h]
                total = chip_sum(h)
                if b == 0:
                    rbuf[t] = total.astype(BF16)
                    relay(t).start()
                elif b < 3:
                    if (b, t) in ((1, 0), (2, 1)):
                        relay(t).wait_recv()
                        total = total + relay_in[t].astype(F32)
                    q = 2 * (b - 1) + t
                    kbuf[q] = total.astype(BF16)
                    keep(q).start()
                else:
                    own_ref[0:hr, :] = total

        lax.fori_loop(n_first, nk, kstep, 0)

        for k in range(n_half):
            @pl.when(s == k)
            def _(k=k):
                give(k).start()

        @pl.when(s == n_half - 1)
        def _():
            own_ref[hr:D_MODEL, :] = chip_sum(n_half - 1)
            give(n_half - 2).wait_send()
            give(n_half - 1).wait_send()
            for r in range(2):
                relay(r).wait_send()
            for q in range(4):
                keep(q).wait()
            gather.finish()

    half_piece = (hr, COLS_PER_DEV)
    grid_spec = pltpu.PrefetchScalarGridSpec(
        num_scalar_prefetch=1, grid=(n_half,),
        in_specs=[HBM_SPEC,
                  pl.BlockSpec((n, COLS_PER_CHIP), lambda s, order: (0, order[s])),
                  *([HBM_SPEC] * n_small)],
        out_specs=(pl.BlockSpec(piece, lambda s, order: (0, 0)), HBM_SPEC, *([HBM_SPEC] * n_small)),
        scratch_shapes=[pltpu.VMEM((2, n, hr), BF16),
                        pltpu.VMEM((2, hr, COLS_PER_CHIP), F32), pltpu.VMEM((n_half,) + half_piece, F32),
                        pltpu.VMEM((2,) + half_piece, BF16), pltpu.VMEM((4,) + half_piece, BF16),
                        pltpu.VMEM((2,) + half_piece, BF16),
                        pltpu.SemaphoreType.DMA((2 * nk,)),
                        pltpu.SemaphoreType.DMA((n_half,)), pltpu.SemaphoreType.DMA((n_half,)),
                        pltpu.SemaphoreType.DMA((4,)), pltpu.SemaphoreType.DMA((4,)),
                        pltpu.SemaphoreType.DMA((2,)), pltpu.SemaphoreType.DMA((2,)),
                        pltpu.SemaphoreType.DMA((7 * n_small,)), pltpu.SemaphoreType.DMA((7 * n_small,)),
                        pltpu.SemaphoreType.DMA((n_small,))])
    return _pcall(
        body, name="dw_in_exchange", grid_spec=grid_spec,
        out_shape=(_out(piece, F32), _out((2,) + piece, BF16),
                   *(_out((N_DEV,) + a.shape, a.dtype) for a in smalls)),
        compiler_params=_params(1),
    )(order, xn, dproj, *smalls)


def _adamw(g, w, m, v):
    m_new = ADAM_B1 * m + (1.0 - ADAM_B1) * g
    v_new = ADAM_B2 * v + (1.0 - ADAM_B2) * (g * g)
    m_hat = m_new / (1.0 - ADAM_B1 ** ADAM_STEP)
    v_hat = v_new / (1.0 - ADAM_B2 ** ADAM_STEP)
    delta = -ADAM_LR * (m_hat / (jnp.sqrt(v_hat) + ADAM_EPS) + ADAM_WD * w)
    return delta, m_new, v_new


def _reduce_adam_w_in(own, rchip, w, m, v):
    rows, cols = w.shape
    row_tile = 256

    def body(o_ref, r_ref, w_ref, m_ref, v_ref, g_ref, d_ref, nm_ref, nv_ref):
        g = o_ref[...]
        for s in range(2):
            g = g + r_ref[s].astype(F32)
        g_ref[...] = g
        d_ref[...], nm_ref[...], nv_ref[...] = _adamw(g, w_ref[...], m_ref[...], v_ref[...])

    tile = pl.BlockSpec((row_tile, cols), lambda i: (i, 0))
    shp = _out((rows, cols), F32)
    return _pcall(
        body, name="reduce_adam_w_in", grid=(rows // row_tile,),
        out_shape=(shp,) * 4,
        in_specs=[tile, pl.BlockSpec((2, row_tile, cols), lambda i: (0, i, 0)), tile, tile, tile],
        out_specs=(tile,) * 4,
        compiler_params=_params(1),
    )(own, rchip, w, m, v)


_SMALL_LEAVES = ("norm_gain", "final_norm_gain", "b_glu", "ssm_a_re", "ssm_a_im", "ssm_log_dt", "ssm_d", "conv_w",
                 "ssm_c_re", "ssm_c_im", "ssm_b_re", "ssm_b_im")


def _reduce_adam_small(r_pack, r_gc, r_gb, wmv, sharded):
    n_leaf = len(_SMALL_LEAVES)
    n_sh = len(sharded)

    def body(*refs):
        rp_ref, rgc_ref, rgb_ref = refs[:3]
        w_refs = refs[3:3 + 3 * n_leaf]
        sh_in = refs[3 + 3 * n_leaf:3 + 3 * n_leaf + 4 * n_sh]
        outs0 = 3 + 3 * n_leaf + 4 * n_sh
        loss_ref = refs[outs0]
        o_refs = refs[outs0 + 1:outs0 + 1 + 4 * n_leaf]
        sh_out = refs[outs0 + 1 + 4 * n_leaf:outs0 + 1 + 4 * n_leaf + 4 * n_sh]
        own_conv = refs[-1]

        def total(ref):
            acc = ref[0].astype(F32)
            for s in range(1, N_DEV):
                acc = acc + ref[s].astype(F32)
            return acc

        for i in range(n_sh):
            r_ref, w_ref, m_ref, v_ref = sh_in[4 * i:4 * i + 4]
            o_g, o_d, o_m, o_v = sh_out[4 * i:4 * i + 4]
            g = total(r_ref)
            o_g[...] = g
            o_d[...], o_m[...], o_v[...] = _adamw(g, w_ref[...], m_ref[...], v_ref[...])

        sp = total(rp_ref)
        sgc = total(rgc_ref)
        sgb = total(rgb_ref)
        loss_ref[...] = sp[ROW_LOSS:ROW_LOSS + SUBLANES, 0:LANES]

        def wide(r):
            return jnp.concatenate([sp[r:r + 1, :], sp[r + 1:r + 2, :]], axis=1)

        s5 = slice(ROW_S5, ROW_S5 + N_GROUPS)
        eye = (lax.broadcasted_iota(jnp.int32, (N_GROUPS, N_GROUPS), 0)
               == lax.broadcasted_iota(jnp.int32, (N_GROUPS, N_GROUPS), 1)).astype(F32)
        d_row = sp[ROW_BGLU_D + 1:ROW_BGLU_D + 2, :]
        me = 4 * lax.axis_index("x") + 2 * lax.axis_index("y") + lax.axis_index("c")
        for k in range(N_DEV):
            @pl.when(me == k)
            def _(k=k):
                own_conv[...] = sp[ROW_CONV:ROW_CONV + SUBLANES, k * CONV_COLS_PER_DEV:(k + 1) * CONV_COLS_PER_DEV]
        grads = {
            "norm_gain": wide(ROW_NORM_GAIN),
            "final_norm_gain": wide(ROW_FINAL_GAIN),
            "b_glu": sp[ROW_BGLU_D:ROW_BGLU_D + 1, :],
            "ssm_a_re": sp[s5, LANE_A_RE:LANE_A_RE + STATE],
            "ssm_a_im": sp[s5, LANE_A_IM:LANE_A_IM + STATE],
            "ssm_log_dt": jnp.sum(sp[s5, LANE_LOG_DT:LANE_LOG_DT + 1] * eye, axis=0, keepdims=True),
            "ssm_d": jnp.concatenate([d_row[:, g * GROUP:(g + 1) * GROUP] for g in range(N_GROUPS)], axis=0),
            "conv_w": own_conv[0:3, :],
            "ssm_c_re": sgc[:, 0:STATE],
            "ssm_c_im": sgc[:, STATE:2 * STATE],
            "ssm_b_re": sgb[:, 0:STATE],
            "ssm_b_im": sgb[:, STATE:2 * STATE],
        }
        for i, name in enumerate(_SMALL_LEAVES):
            g = grads[name]
            w_ref, m_ref, v_ref = w_refs[3 * i:3 * i + 3]
            o_g, o_d, o_m, o_v = o_refs[4 * i:4 * i + 4]
            o_g[...] = g
            o_d[...], o_m[...], o_v[...] = _adamw(g, w_ref[...], m_ref[...], v_ref[...])

    flat_w = [a for name in _SMALL_LEAVES for a in wmv[name]]
    leaf_shapes = [_out(wmv[name][0].shape, F32) for name in _SMALL_LEAVES for _ in range(4)]
    sh_shapes = [_out(entry[1].shape, F32) for entry in sharded for _ in range(4)]
    operands = (r_pack, r_gc, r_gb, *flat_w, *(a for entry in sharded for a in entry))
    out_shape = (_out((SUBLANES, LANES), F32), *leaf_shapes, *sh_shapes)
    outs = _pcall(
        body, name="reduce_adam_small", grid=(1,), out_shape=out_shape,
        in_specs=_whole_specs(operands), out_specs=tuple(_whole_specs(out_shape)),
        scratch_shapes=[pltpu.VMEM((SUBLANES, CONV_COLS_PER_DEV), F32)],
        compiler_params=_params(1),
    )(*operands)
    leaves = {name: outs[1 + 4 * i:5 + 4 * i] for i, name in enumerate(_SMALL_LEAVES)}
    first = 1 + 4 * n_leaf
    return outs[0], leaves, [outs[first + 4 * i:first + 4 * i + 4] for i in range(n_sh)]


def kernel(x, norm_gain, w_in, ssm_a_re, ssm_a_im, ssm_log_dt, ssm_b_re, ssm_b_im, ssm_c_re, ssm_c_im, ssm_d, w_glu, b_glu, conv_w, w_out, final_norm_gain, loss_target, m_norm_gain, m_w_in, m_ssm_a_re, m_ssm_a_im, m_ssm_log_dt, m_ssm_b_re, m_ssm_b_im, m_ssm_c_re, m_ssm_c_im, m_ssm_d, m_w_glu, m_b_glu, m_conv_w, m_w_out, m_final_norm_gain, v_norm_gain, v_w_in, v_ssm_a_re, v_ssm_a_im, v_ssm_log_dt, v_ssm_b_re, v_ssm_b_im, v_ssm_c_re, v_ssm_c_im, v_ssm_d, v_w_glu, v_b_glu, v_conv_w, v_w_out, v_final_norm_gain):
    n_seq, seq, _ = x.shape
    n = n_seq * seq

    gh_p = lambda b4: jnp.transpose(b4, (0, 1, 3, 2)).reshape(N_GROUPS * GROUP, STATE)
    c2 = lambda a: a.reshape(N_GROUPS * GROUP, STATE)
    b_re2, b_im2 = gh_p(ssm_b_re), gh_p(ssm_b_im)
    d_row = ssm_d[0].reshape(1, SSM_W)

    x2 = x.reshape(n, D_MODEL)
    tgt2 = loss_target.reshape(n, D_MODEL)
    mx, my, mc = lax.axis_index("x"), lax.axis_index("y"), lax.axis_index("c")
    chip_ids = [2 * cx + cy for cx, cy in ((mx, my), (1 - mx, my), (mx, 1 - my), (1 - mx, 1 - my))]
    arrival = chip_ids
    xn, proj, w_in_f, s5 = _in_proj(
        jnp.stack(arrival).astype(jnp.int32), x2, norm_gain, w_in[0].astype(BF16),
        (ssm_a_re[0], ssm_a_im[0], ssm_log_dt, b_re2, b_im2, c2(ssm_c_re), c2(ssm_c_im)))
    a_re_x, a_im_x, log_dt_x, ab_re, ab_im, bb_re_m, bb_im_m, c_re_m, c_imn_m = s5
    u3 = proj.reshape(n_seq, seq, IN_COLS)
    conv_p = jnp.pad(conv_w[0], ((0, SUBLANES - 3), (0, LANES - CONV_COLS_PER_DEV)))
    s_re, s_im, y3, w_out_f, w_glu_f, conv_all = _ssm_fwd(
        u3, bb_re_m, bb_im_m, c_re_m, c_imn_m, d_row, ab_re, ab_im,
        w_out[0].astype(BF16), w_glu[0].astype(BF16), conv_p, n_seq, seq)
    conv8 = jnp.transpose(conv_all[:, :, :CONV_COLS_PER_DEV], (1, 0, 2)).reshape(SUBLANES, CONV_W)
    (dh2, dy, dzs, dbc, dzc, dyc, dw_out, dw_glu, loss_t, dgf, dbg, dcw) = _mix(
        x2, tgt2, y3.reshape(n, SSM_W), proj, final_norm_gain.reshape(1, D_MODEL), b_glu, conv8,
        w_glu_f, w_out_f, seq)

    du3, dc_re_d, dc_im_d, dbb_re_d, dbb_im_d, dab_re, dab_im, dd, r_out, r_glu = _ssm_bwd(
        dy.reshape(n_seq, seq, SSM_W), u3, s_re, s_im, bb_re_m, bb_im_m, c_re_m, c_imn_m, d_row, ab_re, ab_im,
        dw_out.reshape(N_DEV, OUT_ROWS_PER_DEV, D_MODEL), dw_glu.reshape(N_DEV, GLU_ROWS_PER_DEV, SSM_W), n_seq, seq)
    du = du3.reshape(n, SSM_W)
    grad_x2, dproj, dg8 = _in_bwd(x2, dh2, du, dzs, dyc, proj, dbc, dzc, norm_gain, conv8, w_in_f, seq)
    pack, gc, gb = _ssm_disc_bwd_pack(
        a_re_x, a_im_x, log_dt_x, b_re2, b_im2, dab_re.reshape(N_GROUPS, STATE), dab_im.reshape(N_GROUPS, STATE),
        dbb_re_d, dbb_im_d, loss_t, dg8, dgf, dbg, dd, dcw, dc_re_d, dc_im_d)

    own_in, rchip_in, r_pack, r_gc, r_gb = _dw_in_exchange(
        [chip_ids[3], chip_ids[2], chip_ids[1], chip_ids[0]],
        xn, dproj, [pack, gc, gb])

    flat2 = lambda a: a.reshape(a.shape[-2:]) if a.ndim > 2 else a.reshape(1, -1)
    c2 = lambda a: a.reshape(N_GROUPS * GROUP, STATE)
    wmv = dict(norm_gain=(norm_gain, m_norm_gain, v_norm_gain),
               final_norm_gain=tuple(flat2(a) for a in (final_norm_gain, m_final_norm_gain, v_final_norm_gain)),
               b_glu=(b_glu, m_b_glu, v_b_glu),
               ssm_a_re=tuple(flat2(a) for a in (ssm_a_re, m_ssm_a_re, v_ssm_a_re)),
               ssm_a_im=tuple(flat2(a) for a in (ssm_a_im, m_ssm_a_im, v_ssm_a_im)),
               ssm_log_dt=(ssm_log_dt, m_ssm_log_dt, v_ssm_log_dt),
               ssm_d=tuple(flat2(a) for a in (ssm_d, m_ssm_d, v_ssm_d)),
               conv_w=tuple(flat2(a) for a in (conv_w, m_conv_w, v_conv_w)),
               ssm_c_re=tuple(c2(a) for a in (ssm_c_re, m_ssm_c_re, v_ssm_c_re)),
               ssm_c_im=tuple(c2(a) for a in (ssm_c_im, m_ssm_c_im, v_ssm_c_im)),
               ssm_b_re=(b_re2, gh_p(m_ssm_b_re), gh_p(v_ssm_b_re)),
               ssm_b_im=(b_im2, gh_p(m_ssm_b_im), gh_p(v_ssm_b_im)))

    res_in = _reduce_adam_w_in(own_in, rchip_in, w_in[0], m_w_in[0], v_w_in[0])
    loss8, small, (res_out, res_glu) = _reduce_adam_small(
        r_pack, r_gc, r_gb, wmv,
        [(r_out, w_out[0], m_w_out[0], v_w_out[0]), (r_glu, w_glu[0], m_w_glu[0], v_w_glu[0])])
    loss = loss8[0, 0]

    shapes = dict(norm_gain=(1, D_MODEL), ssm_a_re=(1, N_GROUPS, STATE), ssm_a_im=(1, N_GROUPS, STATE),
                  ssm_log_dt=(1, N_GROUPS), ssm_c_re=(1, N_GROUPS, GROUP, STATE), ssm_c_im=(1, N_GROUPS, GROUP, STATE),
                  ssm_d=(1, N_GROUPS, GROUP), b_glu=(1, SSM_W), final_norm_gain=(D_MODEL,),
                  conv_w=(1, 3, CONV_COLS_PER_DEV))
    big = dict(w_in=res_in, w_glu=res_glu, w_out=res_out)

    def leaf(kind, name):
        if name in big:
            return big[name][kind][None]
        if name in ("ssm_b_re", "ssm_b_im"):
            return jnp.transpose(small[name][kind].reshape(1, N_GROUPS, GROUP, STATE), (0, 1, 3, 2))
        return small[name][kind].reshape(shapes[name])

    order = ["norm_gain", "w_in", "ssm_a_re", "ssm_a_im", "ssm_log_dt", "ssm_b_re", "ssm_b_im", "ssm_c_re",
             "ssm_c_im", "ssm_d", "w_glu", "b_glu", "conv_w", "w_out", "final_norm_gain"]
    outs = [loss, grad_x2.reshape(x.shape)]
    for kind in range(4):
        outs += [leaf(kind, name) for name in order]
    return tuple(outs)
```

```python
import functools
import math

import jax
import jax.numpy as jnp
from jax import lax
from jax.experimental import pallas as pl
from jax.experimental.pallas import tpu as pltpu

F32 = jnp.float32
BF16 = jnp.bfloat16

N_DEV = 8
D_MODEL = 1024
SSM_W = 512
CONV_W = 512
N_GROUPS = 32
GROUP = 16
STATE = 64
IN_COLS = 3072
SEG_U, SEG_ZS, SEG_H, SEG_BC, SEG_CC, SEG_ZC = range(6)
COLS_PER_DEV = IN_COLS // N_DEV
N_CHIP = N_DEV // 2
COLS_PER_CHIP = 2 * COLS_PER_DEV
OUT_ROWS_PER_DEV = D_MODEL // N_DEV
GLU_ROWS_PER_DEV = SSM_W // N_DEV
CONV_COLS_PER_DEV = CONV_W // N_DEV
EPS = 1e-6

N_JBLK = 4
JB_CH = SSM_W // N_JBLK
JB_ST = N_GROUPS * STATE // N_JBLK

ADAM_LR = 0.001
ADAM_B1 = 0.9
ADAM_B2 = 0.999
ADAM_EPS = 1e-08
ADAM_WD = 0.01
ADAM_STEP = 10

SUBLANES = 8
LANES = 128
VMEM_LIMIT = 48 * 1024 * 1024
TOK_TILE = 256
IN_TILE = 1024
SCAN_TILE = 1024

MESH = pl.DeviceIdType.MESH
HBM_SPEC = pl.BlockSpec(memory_space=pltpu.HBM)


def _build(body, **kw):
    return pl.pallas_call(body, **kw)


def _pcall(body, **kw):
    def call(*operands):
        pinned = [a if jnp.issubdtype(a.dtype, jnp.integer) else pltpu.with_memory_space_constraint(a, pltpu.HBM)
                  for a in operands]
        return _build(body, **kw)(*pinned)
    return call


def _whole_specs(arrays):
    return [pl.BlockSpec(a.shape, functools.partial(lambda nd, i: (0,) * nd, len(a.shape))) for a in arrays]


def _out(shape, dtype):
    return pltpu.HBM(tuple(shape), dtype)


def _params(n_grid):
    return pltpu.CompilerParams(dimension_semantics=("arbitrary",) * n_grid,
                                vmem_limit_bytes=VMEM_LIMIT)


def _dot(a, b):
    return jnp.dot(a, b, preferred_element_type=F32)


def _dot_nt(a, b):
    return lax.dot_general(a, b, (((1,), (1,)), ((), ())), preferred_element_type=F32)


def _dot_tn(a, b):
    return lax.dot_general(a, b, (((0,), (0,)), ((), ())), preferred_element_type=F32)


def _sigmoid(z):
    return 1.0 / (1.0 + jnp.exp(-z))


_GELU_C = math.sqrt(2.0 / math.pi)


def _gelu_and_grad(y):
    inner = _GELU_C * (y + 0.044715 * (y * y * y))
    t = jnp.tanh(inner)
    g = 0.5 * y * (1.0 + t)
    dg = 0.5 * (1.0 + t) + 0.5 * y * (1.0 - t * t) * (_GELU_C * (1.0 + 3.0 * 0.044715 * (y * y)))
    return g, dg


def _silu_and_grad(z):
    s = _sigmoid(z)
    return z * s, s * (1.0 + z * (1.0 - s))


def _shift_down(v, halo, k):
    rolled = pltpu.roll(v, k, 0)
    row = lax.broadcasted_iota(jnp.int32, v.shape, 0)
    for r in range(k):
        rolled = jnp.where(row == r, halo[SUBLANES - k + r:SUBLANES - k + r + 1, :], rolled)
    return rolled


def _shift_up(v, halo, k):
    n = v.shape[0]
    rolled = pltpu.roll(v, n - k, 0)
    row = lax.broadcasted_iota(jnp.int32, v.shape, 0)
    for r in range(k):
        rolled = jnp.where(row == n - k + r, halo[r:r + 1, :], rolled)
    return rolled


def _mesh_pos():
    return lax.axis_index("x"), lax.axis_index("y"), lax.axis_index("c")


def _direct_copies(srcs_for, out_refs, send_sems, recv_sems, loc_sems):
    x, y, c = _mesh_pos()
    me_id = 4 * x + 2 * y + c
    n_arr = len(out_refs)
    dsts = [r.at[me_id] for r in out_refs]
    own = srcs_for(me_id)
    mine = [pltpu.make_async_copy(own[a], dsts[a], loc_sems.at[a]) for a in range(n_arr)]
    sends = []
    for k in range(1, N_DEV):
        px, py, pc = x ^ ((k >> 2) & 1), y ^ ((k >> 1) & 1), c ^ (k & 1)
        src = srcs_for(4 * px + 2 * py + pc)
        for a in range(n_arr):
            sends.append(pltpu.make_async_remote_copy(
                src_ref=src[a], dst_ref=dsts[a],
                send_sem=send_sems.at[(k - 1) * n_arr + a], recv_sem=recv_sems.at[(k - 1) * n_arr + a],
                device_id=(px, py, pc), device_id_type=MESH))
    return mine, sends


class _TwoLevelGather:
    def __init__(self, srcs, slots, send_sems, recv_sems, loc_sems):
        self.srcs, self.slots, self.n_arr = srcs, slots, len(srcs)
        self.send_sems, self.recv_sems, self.loc_sems = send_sems, recv_sems, loc_sems
        x, y, c = _mesh_pos()
        self.c = c
        self.me, self.sib = (x, y, c), (x, y, 1 - c)
        self.chips = [(1 - x, y), (x, 1 - y), (1 - x, 1 - y)]

    def _copies(self, k, block, to, from_src=False):
        dev = 4 * block[0] + 2 * block[1] + block[2]
        return [pltpu.make_async_remote_copy(
            src_ref=self.srcs[a] if from_src else self.slots[a](dev), dst_ref=self.slots[a](dev),
            send_sem=self.send_sems.at[k * self.n_arr + a], recv_sem=self.recv_sems.at[k * self.n_arr + a],
            device_id=to, device_id_type=MESH) for a in range(self.n_arr)]

    def _local(self):
        dev = 4 * self.me[0] + 2 * self.me[1] + self.me[2]
        return [pltpu.make_async_copy(self.srcs[a], self.slots[a](dev), self.loc_sems.at[a])
                for a in range(self.n_arr)]

    def start(self):
        for cp in self._local() + self._copies(0, self.me, self.sib, True):
            cp.start()
        for j in (0, 1):
            for cp in self._copies(1 + j, self.me, (*self.chips[j], self.c), True):
                cp.start()

    def wait_own(self):
        for cp in self._local():
            cp.wait()

    def wait_sibling(self):
        for cp in self._copies(0, self.sib, self.me):
            cp.wait_recv()

    def wait_and_pass_on(self, j):
        chip = self.chips[j]
        for cp in self._copies(1 + j, (*chip, self.c), self.me):
            cp.wait_recv()
        for cp in self._copies(4 + j, (*chip, self.c), self.sib):
            cp.start()

    def neighbours_landed(self):
        x, y, c = self.me
        self.wait_and_pass_on(0)
        self.wait_and_pass_on(1)
        for cp in self._copies(1 + 2, (x ^ c, y ^ (1 - c), c), (x ^ (1 - c), y ^ c, c)):
            cp.start()

    def diagonal_landed(self):
        self.wait_and_pass_on(2)

    def wait_passed_on(self, j):
        for cp in self._copies(4 + j, (*self.chips[j], 1 - self.c), self.me):
            cp.wait_recv()

    def wait_sends(self):
        for cp in self._copies(0, self.me, self.sib, True):
            cp.wait_send()
        for j, chip in enumerate(self.chips):
            for cp in self._copies(1 + j, self.me, (*chip, self.c), True) + self._copies(4 + j, (*chip, self.c), self.sib):
                cp.wait_send()

    def finish(self):
        self.wait_sibling()
        for j in range(3):
            self.wait_passed_on(j)
        self.wait_sends()
        self.wait_own()


def _disc(a_re, a_im, log_dt, b_re, b_im):
    dt = jnp.exp(log_dt)
    mag = jnp.exp(a_re * dt)
    ab_re = mag * jnp.cos(a_im * dt)
    ab_im = mag * jnp.sin(a_im * dt)
    den = a_re * a_re + a_im * a_im
    p_re = ab_re - 1.0
    p_im = ab_im
    q_re = (p_re * a_re + p_im * a_im) / den
    q_im = (p_im * a_re - p_re * a_im) / den
    bb_re = q_re * b_re - q_im * b_im
    bb_im = q_re * b_im + q_im * b_re
    return ab_re, ab_im, bb_re, bb_im


def _split3(v):
    hi = v.astype(BF16)
    r1 = v - hi.astype(F32)
    mid = r1.astype(BF16)
    lo = (r1 - mid.astype(F32)).astype(BF16)
    return hi, mid, lo


def _select_dot(sel, v):
    return sum(_dot(sel, t) for t in _split3(v))


PACK_ROWS = 72
PACK_W = 512
ROW_FINAL_GAIN, ROW_NORM_GAIN, ROW_BGLU_D, ROW_CONV, ROW_LOSS, ROW_S5 = 0, 8, 16, 24, 32, 40
LANE_A_RE, LANE_A_IM, LANE_LOG_DT = 0, 128, 256


def _ssm_disc_bwd_pack(a_re_x, a_im_x, log_dt_x, b_re, b_im, g_ab_re, g_ab_im, dbb_re_d, dbb_im_d,
                       loss_t, dg8, dgf, dbg, dd, dcw, dc_re_d, dc_im_d):
    rows_gh = N_GROUPS * GROUP

    def body(are, aim, ldt, bre, bim, gabre, gabim, dbbre_ref, dbbim_ref,
             loss_ref, dg8_ref, dgf_ref, dbg_ref, dd_ref, dcw_ref, dcre_ref, dcim_ref,
             p_ref, gc_ref, gb_ref, gbb_re, gbb_im):
        r_g = lax.broadcasted_iota(jnp.int32, (N_GROUPS, rows_gh), 0)
        c_gh = lax.broadcasted_iota(jnp.int32, (N_GROUPS, rows_gh), 1)
        group_sum = (c_gh // GROUP == r_g).astype(BF16)
        r_gh = lax.broadcasted_iota(jnp.int32, (rows_gh, N_GROUPS), 0)
        c_g = lax.broadcasted_iota(jnp.int32, (rows_gh, N_GROUPS), 1)
        first_row = (r_gh == c_g * GROUP).astype(BF16)

        def diag_block(ref, j, gi):
            return ref[j, gi * GROUP:(gi + 1) * GROUP, gi * STATE:(gi + 1) * STATE]

        for j in range(N_JBLK):
            for gi in range(SUBLANES):
                r0 = (j * SUBLANES + gi) * GROUP
                gbb_re[r0:r0 + GROUP, :] = diag_block(dbbre_ref, j, gi)
                gbb_im[r0:r0 + GROUP, :] = diag_block(dbbim_ref, j, gi)
                both = jnp.concatenate([diag_block(dcre_ref, j, gi), -diag_block(dcim_ref, j, gi)], axis=1)
                gc_ref[r0:r0 + GROUP, :] = both.astype(BF16)

        _, vjp = jax.vjp(_disc, are[...], aim[...], ldt[...], bre[...], bim[...])
        d_are, d_aim, d_ldt, d_bre, d_bim = vjp((_select_dot(first_row, gabre[...]), _select_dot(first_row, gabim[...]),
                                                 gbb_re[...], gbb_im[...]))
        gb_ref[...] = jnp.concatenate([d_bre, d_bim], axis=1).astype(BF16)

        p_ref[...] = jnp.zeros_like(p_ref)
        half = D_MODEL // 2
        for r, src in ((ROW_FINAL_GAIN, dgf_ref), (ROW_NORM_GAIN, dg8_ref)):
            p_ref[r:r + 1, :] = src[0:1, 0:half]
            p_ref[r + 1:r + 2, :] = src[0:1, half:D_MODEL]
        p_ref[ROW_BGLU_D:ROW_BGLU_D + 1, :] = dbg_ref[...]
        p_ref[ROW_BGLU_D + 1:ROW_BGLU_D + 2, :] = dd_ref[...]
        p_ref[ROW_CONV:ROW_CONV + SUBLANES, :] = dcw_ref[...]
        p_ref[ROW_LOSS:ROW_LOSS + SUBLANES, 0:LANES] = loss_ref[...]
        s5 = slice(ROW_S5, ROW_S5 + N_GROUPS)
        p_ref[s5, LANE_A_RE:LANE_A_RE + STATE] = _select_dot(group_sum, d_are)
        p_ref[s5, LANE_A_IM:LANE_A_IM + STATE] = _select_dot(group_sum, d_aim)
        p_ref[s5, LANE_LOG_DT:LANE_LOG_DT + LANES] = _select_dot(group_sum, jnp.broadcast_to(d_ldt, (rows_gh, LANES)))

    operands = (a_re_x, a_im_x, log_dt_x, b_re, b_im, g_ab_re, g_ab_im, dbb_re_d, dbb_im_d,
                loss_t, dg8, dgf, dbg, dd, dcw, dc_re_d, dc_im_d)
    out_shape = (_out((PACK_ROWS, PACK_W), F32),
                 _out((rows_gh, 2 * STATE), BF16),
                 _out((rows_gh, 2 * STATE), BF16))
    return _pcall(body, name="ssm_disc_bwd_pack", grid=(1,), out_shape=out_shape,
                  in_specs=_whole_specs(operands), out_specs=tuple(_whole_specs(out_shape)),
                  scratch_shapes=[pltpu.VMEM((rows_gh, STATE), F32), pltpu.VMEM((rows_gh, STATE), F32)],
                  compiler_params=_params(1))(*operands)


def _s5_prepare(are, aim, ldt, bre, bim, cre, cim,
                o_ax_re, o_ax_im, o_ldt_x, o_ab_re, o_ab_im, o_bb_re, o_bb_im, o_c_re, o_c_imn):
    rows_gh = N_GROUPS * GROUP
    rep = (lax.broadcasted_iota(jnp.int32, (rows_gh, N_GROUPS), 0) // GROUP
           == lax.broadcasted_iota(jnp.int32, (rows_gh, N_GROUPS), 1)).astype(BF16)
    eye = (lax.broadcasted_iota(jnp.int32, (N_GROUPS, N_GROUPS), 0)
           == lax.broadcasted_iota(jnp.int32, (N_GROUPS, N_GROUPS), 1)).astype(F32)
    ldt_col = jnp.sum(eye * ldt[...], axis=1, keepdims=True)
    a_re_x = _select_dot(rep, are[...])
    a_im_x = _select_dot(rep, aim[...])
    ldt_x = _select_dot(rep, jnp.broadcast_to(ldt_col, (N_GROUPS, LANES)))[:, 0:1]
    o_ax_re[...] = a_re_x
    o_ax_im[...] = a_im_x
    o_ldt_x[...] = ldt_x
    ab_re, ab_im, bb_re, bb_im = _disc(a_re_x, a_im_x, ldt_x, bre[...], bim[...])
    for j in range(N_JBLK):
        first = [(j * SUBLANES + gi) * GROUP for gi in range(SUBLANES)]
        o_ab_re[j] = jnp.concatenate([ab_re[r:r + 1, :] for r in first], axis=1)
        o_ab_im[j] = jnp.concatenate([ab_im[r:r + 1, :] for r in first], axis=1)
    for o, v in ((o_bb_re, bb_re), (o_bb_im, bb_im), (o_c_re, cre[...]), (o_c_imn, -cim[...])):
        for j in range(N_JBLK):
            for gi in range(SUBLANES):
                r0 = (j * SUBLANES + gi) * GROUP
                parts = [v[r0:r0 + GROUP, :] if k == gi else jnp.zeros((GROUP, STATE), F32) for k in range(SUBLANES)]
                o[j, gi * GROUP:(gi + 1) * GROUP, :] = jnp.concatenate(parts, axis=1).astype(BF16)


def _in_proj(order, x2, g1, w_in_b, s5):
    n = x2.shape[0]
    tm = min(IN_TILE, n)
    n_tiles = n // tm
    n_s5_in = len(s5)
    n_s5_out = 9

    def body(order_ref, x_ref, g_ref, w_ref, *refs):
        s5_in = refs[:n_s5_in]
        xn_ref, proj_ref, wall_ref = refs[n_s5_in:n_s5_in + 3]
        s5_out = refs[n_s5_in + 3:n_s5_in + 3 + n_s5_out]
        xn_scr, wbuf, send_sems, recv_sems, loc_sems, out_sems = refs[n_s5_in + 3 + n_s5_out:]
        k = pl.program_id(0)
        i = pl.program_id(1)

        def slot(dev):
            return wbuf.at[dev // 2, :, pl.ds(pl.multiple_of((dev % 2) * COLS_PER_DEV, LANES), COLS_PER_DEV)]

        gather = _TwoLevelGather([w_ref], [slot], send_sems, recv_sems, loc_sems)

        @pl.when((k == 0) & (i == 0))
        def _():
            gather.start()

        def own_chip():
            gather.wait_own()
            gather.wait_sibling()

        def x_chip():
            gather.neighbours_landed()
            gather.wait_passed_on(0)

        def diag_chip():
            gather.diagonal_landed()
            gather.wait_passed_on(2)

        arrivals = [own_chip, x_chip, functools.partial(gather.wait_passed_on, 1), diag_chip]
        for kk, arrived in enumerate(arrivals):
            @pl.when((k == kk) & (i == 0))
            def _(arrived=arrived):
                arrived()

        rows = pl.ds(pl.multiple_of(i * tm, tm), tm)

        @pl.when(k == 0)
        def _():
            x = x_ref[...]
            r = lax.rsqrt(jnp.mean(x * x, axis=-1, keepdims=True) + EPS)
            xn = ((x * r) * g_ref[...]).astype(BF16)
            xn_scr[rows, :] = xn
            xn_ref[...] = xn

        proj_ref[...] = _dot(xn_scr[rows, :], wbuf[order_ref[k]])

        @pl.when((k == 0) & (i == n_tiles - 1))
        def _():
            _s5_prepare(*s5_in, *s5_out)

        @pl.when((k == N_CHIP - 1) & (i == n_tiles - 1))
        def _():
            gather.wait_sends()
            outs = [pltpu.make_async_copy(wbuf.at[q], wall_ref.at[:, q * COLS_PER_CHIP:(q + 1) * COLS_PER_CHIP],
                                          out_sems.at[q]) for q in range(N_CHIP)]
            for cp in outs:
                cp.start()
            for cp in outs:
                cp.wait()

    tile_once = lambda k, i, order: (jnp.where(k == 0, i, n_tiles - 1), 0)
    whole = lambda shape: pl.BlockSpec(shape, lambda k, i, order: (0,) * len(shape))
    rows_gh = N_GROUPS * GROUP
    s5_out_shapes = ([(rows_gh, STATE), F32], [(rows_gh, STATE), F32], [(rows_gh, 1), F32],
                     [(N_JBLK, 1, JB_ST), F32], [(N_JBLK, 1, JB_ST), F32]) + ([(N_JBLK, JB_CH, JB_ST), BF16],) * 4
    grid_spec = pltpu.PrefetchScalarGridSpec(
        num_scalar_prefetch=1, grid=(N_CHIP, n_tiles),
        in_specs=[pl.BlockSpec((tm, D_MODEL), tile_once),
                  whole((1, D_MODEL)),
                  HBM_SPEC,
                  *(whole(a.shape) for a in s5)],
        out_specs=(pl.BlockSpec((tm, D_MODEL), tile_once),
                   pl.BlockSpec((tm, COLS_PER_CHIP), lambda k, i, order: (i, order[k])),
                   HBM_SPEC,
                   *(whole(shape) for shape, _ in s5_out_shapes)),
        scratch_shapes=[pltpu.VMEM((n, D_MODEL), BF16), pltpu.VMEM((N_CHIP, D_MODEL, COLS_PER_CHIP), BF16),
                        pltpu.SemaphoreType.DMA((7,)), pltpu.SemaphoreType.DMA((7,)), pltpu.SemaphoreType.DMA((1,)),
                        pltpu.SemaphoreType.DMA((N_CHIP,))])
    outs = _pcall(
        body, name="in_proj", grid_spec=grid_spec,
        out_shape=(_out((n, D_MODEL), BF16), _out((n, IN_COLS), F32),
                   _out((D_MODEL, IN_COLS), BF16),
                   *(_out(shape, dt) for shape, dt in s5_out_shapes)),
        compiler_params=_params(2),
    )(order, x2, g1, w_in_b, *s5)
    return outs[0], outs[1], outs[2], outs[3:]


def _cmul(p, q):
    return p[0] * q[0] - p[1] * q[1], p[0] * q[1] + p[1] * q[0]


def _scan_tables(ar, ai, width, reverse):
    pows = [(ar, ai)]
    for _ in range(SUBLANES - 1):
        pows.append(_cmul(pows[-1], (ar, ai)))
    row = lax.broadcasted_iota(jnp.int32, (SUBLANES, width), 0)

    def bc(v):
        return jnp.broadcast_to(v, (SUBLANES, width))

    levels = []
    for k in (1, 2, 4):
        keep = (row <= SUBLANES - 1 - k) if reverse else (row >= k)
        levels.append((jnp.where(keep, bc(pows[k - 1][0]), 0.0), jnp.where(keep, bc(pows[k - 1][1]), 0.0)))
    cre = jnp.zeros((SUBLANES, width), F32)
    cim = jnp.zeros((SUBLANES, width), F32)
    for r in range(SUBLANES):
        e = (SUBLANES - r) if reverse else (r + 1)
        cre = jnp.where(row == r, bc(pows[e - 1][0]), cre)
        cim = jnp.where(row == r, bc(pows[e - 1][1]), cim)
    return levels, (cre, cim)


def _load_chunked(src_ref, b, dst_ref, n_rows):
    n_blk = n_rows // SUBLANES
    for i in range(n_blk):
        dst_ref[b, i * SUBLANES:(i + 1) * SUBLANES, :] = src_ref[b, pl.ds(i, SUBLANES, stride=n_blk), :]


def _store_chunked(val, dst_ref, b, n_rows):
    n_blk = n_rows // SUBLANES
    for i in range(n_blk):
        dst_ref[b, pl.ds(i, SUBLANES, stride=n_blk), :] = val[i * SUBLANES:(i + 1) * SUBLANES, :]


def _chunk_scan(re_ref, im_ref, bs, car_ref, ar, ai, n_rows, reverse, on_block=None):
    width = re_ref.shape[2]
    n_blk = n_rows // SUBLANES
    shape = (SUBLANES, width)
    abr = jnp.broadcast_to(ar, shape)
    abi = jnp.broadcast_to(ai, shape)
    order = list(range(n_blk - 1, -1, -1)) if reverse else list(range(n_blk))

    def blk(ref, b, i):
        return ref[b, i * SUBLANES:(i + 1) * SUBLANES, :]

    def step(state, b, i):
        sr, si = state
        return abr * sr - abi * si + blk(re_ref, b, i), abr * si + abi * sr + blk(im_ref, b, i)

    finals = {b: (blk(re_ref, b, order[0]), blk(im_ref, b, order[0])) for b in bs}
    for i in order[1:]:
        for b in bs:
            finals[b] = step(finals[b], b, i)

    mr, mi = ar, ai
    for _ in range(n_blk.bit_length() - 1):
        mr, mi = _cmul((mr, mi), (mr, mi))
    levels, _ = _scan_tables(mr, mi, width, reverse)
    mbr = jnp.broadcast_to(mr, shape)
    mbi = jnp.broadcast_to(mi, shape)
    row = lax.broadcasted_iota(jnp.int32, shape, 0)
    edge_in = SUBLANES - 1 if reverse else 0
    edge_out = 0 if reverse else SUBLANES - 1
    sh1 = SUBLANES - 1 if reverse else 1
    states = {}
    for b in bs:
        fr, fi = finals[b]
        gr = jnp.where(row == edge_in, jnp.broadcast_to(car_ref[b, 0:1, :], shape), pltpu.roll(fr, sh1, 0))
        gi = jnp.where(row == edge_in, jnp.broadcast_to(car_ref[b, 1:2, :], shape), pltpu.roll(fi, sh1, 0))
        for (lr, li), k in zip(levels, (1, 2, 4)):
            sh = (SUBLANES - k) if reverse else k
            sr = pltpu.roll(gr, sh, 0)
            si = pltpu.roll(gi, sh, 0)
            gr, gi = gr + (lr * sr - li * si), gi + (lr * si + li * sr)
        car_ref[b, 0:1, :] = (fr + (mbr * gr - mbi * gi))[edge_out:edge_out + 1, :]
        car_ref[b, 1:2, :] = (fi + (mbr * gi + mbi * gr))[edge_out:edge_out + 1, :]
        states[b] = (gr, gi)

    for i in order:
        for b in bs:
            states[b] = step(states[b], b, i)
            re_ref[b, i * SUBLANES:(i + 1) * SUBLANES, :] = states[b][0]
            im_ref[b, i * SUBLANES:(i + 1) * SUBLANES, :] = states[b][1]
            if on_block is not None:
                on_block(b, i, *states[b])


def _ssm_fwd(u, bb_re, bb_im, c_re_t, c_imn_t, d_row, ab_re, ab_im, w_out_b, w_glu_b, conv_p, n_seq, seq):
    tt = min(SCAN_TILE, seq)
    nt = seq // tt

    def body(u_ref, bbre, bbim, cre, cimn, d_ref, are, aim, wout_ref, wglu_ref, cw_ref,
             sre_ref, sim_ref, y_ref, oout_ref, oglu_ref, ocw_ref,
             up_ref, car_ref, send_sems, recv_sems, loc_sems):
        j = pl.program_id(0)
        t = pl.program_id(1)
        gather = _TwoLevelGather(
            [wout_ref, wglu_ref, cw_ref],
            [lambda dev: oout_ref.at[pl.ds(pl.multiple_of(dev * OUT_ROWS_PER_DEV, OUT_ROWS_PER_DEV), OUT_ROWS_PER_DEV), :],
             lambda dev: oglu_ref.at[pl.ds(pl.multiple_of(dev * GLU_ROWS_PER_DEV, GLU_ROWS_PER_DEV), GLU_ROWS_PER_DEV), :],
             lambda dev: ocw_ref.at[dev]],
            send_sems, recv_sems, loc_sems)

        @pl.when((j == 0) & (t == 0))
        def _():
            gather.start()

        @pl.when((j == N_JBLK // 2) & (t == 0))
        def _():
            gather.neighbours_landed()

        @pl.when((j == N_JBLK - 1) & (t == 0))
        def _():
            gather.diagonal_landed()

        @pl.when(t == 0)
        def _():
            car_ref[...] = jnp.zeros_like(car_ref)

        bs = list(range(n_seq))
        for b in bs:
            _load_chunked(u_ref, b, up_ref, tt)
        for b in bs:
            ub = up_ref[b].astype(BF16)
            sre_ref[b] = _dot(ub, bbre[0])
            sim_ref[b] = _dot(ub, bbim[0])
            _chunk_scan(sre_ref, sim_ref, [b], car_ref, are[0], aim[0], tt, reverse=False)
        for b in bs:
            yp = (_dot_nt(sre_ref[b].astype(BF16), cre[0]) + _dot_nt(sim_ref[b].astype(BF16), cimn[0])
                  + d_ref[...] * up_ref[b])
            _store_chunked(yp, y_ref, b, tt)

        @pl.when((j == N_JBLK - 1) & (t == nt - 1))
        def _():
            gather.finish()

    tok = lambda j, t: (0, t, j)
    blk3 = lambda j, t: (j, 0, 0)
    row = lambda j, t: (0, j)
    st = _out((n_seq, seq, N_JBLK * JB_ST), F32)
    n_arr = 3
    return _pcall(
        body, name="ssm_fwd", grid=(N_JBLK, nt),
        out_shape=(st, st, _out((n_seq, seq, SSM_W), F32),
                   _out((D_MODEL, D_MODEL), BF16), _out((SSM_W, SSM_W), BF16),
                   _out((N_DEV, SUBLANES, LANES), F32)),
        in_specs=[pl.BlockSpec((n_seq, tt, JB_CH), tok),
                  pl.BlockSpec((1, JB_CH, JB_ST), blk3), pl.BlockSpec((1, JB_CH, JB_ST), blk3),
                  pl.BlockSpec((1, JB_CH, JB_ST), blk3), pl.BlockSpec((1, JB_CH, JB_ST), blk3),
                  pl.BlockSpec((1, JB_CH), row), pl.BlockSpec((1, 1, JB_ST), blk3), pl.BlockSpec((1, 1, JB_ST), blk3),
                  HBM_SPEC, HBM_SPEC, HBM_SPEC],
        out_specs=(pl.BlockSpec((n_seq, tt, JB_ST), tok), pl.BlockSpec((n_seq, tt, JB_ST), tok),
                   pl.BlockSpec((n_seq, tt, JB_CH), tok), HBM_SPEC, HBM_SPEC, HBM_SPEC),
        scratch_shapes=[pltpu.VMEM((n_seq, tt, JB_CH), F32), pltpu.VMEM((n_seq, SUBLANES, JB_ST), F32),
                        pltpu.SemaphoreType.DMA((7 * n_arr,)), pltpu.SemaphoreType.DMA((7 * n_arr,)),
                        pltpu.SemaphoreType.DMA((n_arr,))],
        compiler_params=_params(2),
    )(u, bb_re, bb_im, c_re_t, c_imn_t, d_row, ab_re, ab_im, w_out_b, w_glu_b, conv_p)


def _ssm_bwd(dy, u, s_re, s_im, bb_re, bb_im, c_re_t, c_imn_t, d_row, ab_re, ab_im, g_out, g_glu, n_seq, seq):
    tt = min(SCAN_TILE, seq)
    nt = seq // tt
    rows8 = tt // SUBLANES

    def body(dy_ref, u_ref, sre_ref, sim_ref, pre_ref, pim_ref, bbre, bbim, cre, cimn, d_ref, are, aim,
             gout_ref, gglu_ref,
             du_ref, dcre_ref, dcim_ref, dbbre_ref, dbbim_ref, dare_ref, daim_ref, dd_ref, rout_ref, rglu_ref,
             lre_ref, lim_ref, dyp_ref, up_ref, car_ref, send_sems, recv_sems, loc_sems):
        j = pl.program_id(0)
        tr = pl.program_id(1)

        def exchange():
            return _direct_copies(lambda pid: [gout_ref.at[pid], gglu_ref.at[pid]], [rout_ref, rglu_ref],
                                  send_sems, recv_sems, loc_sems)

        @pl.when((j == 0) & (tr == 0))
        def _():
            mine, sends = exchange()
            for cp in mine + sends:
                cp.start()

        @pl.when(tr == 0)
        def _():
            car_ref[...] = jnp.zeros_like(car_ref)
            for r in (dcre_ref, dcim_ref, dbbre_ref, dbbim_ref, dare_ref, daim_ref, dd_ref):
                r[...] = jnp.zeros_like(r)

        first = tr == nt - 1
        row = lax.broadcasted_iota(jnp.int32, (SUBLANES, JB_ST), 0)
        n_blk = tt // SUBLANES
        bs = list(range(n_seq))
        for b in bs:
            _load_chunked(dy_ref, b, dyp_ref, tt)
            _load_chunked(u_ref, b, up_ref, tt)
        for b in bs:
            dyb = dyp_ref[b].astype(BF16)
            lre_ref[b] = _dot(dyb, cre[0])
            lim_ref[b] = _dot(dyb, cimn[0])
        acc = {b: [jnp.zeros((SUBLANES, JB_ST), F32), jnp.zeros((SUBLANES, JB_ST), F32)] for b in bs}

        def on_block(b, i, lr, li):
            if i > 0:
                spr = sre_ref[b, (i - 1) * SUBLANES:i * SUBLANES, :]
                spi = sim_ref[b, (i - 1) * SUBLANES:i * SUBLANES, :]
            else:
                hr = jnp.where(first, 0.0, pre_ref[b, SUBLANES - 1:SUBLANES, :])
                hi = jnp.where(first, 0.0, pim_ref[b, SUBLANES - 1:SUBLANES, :])
                last_r = sre_ref[b, (n_blk - 1) * SUBLANES:n_blk * SUBLANES, :]
                last_i = sim_ref[b, (n_blk - 1) * SUBLANES:n_blk * SUBLANES, :]
                spr = jnp.where(row == 0, jnp.broadcast_to(hr, row.shape), pltpu.roll(last_r, 1, 0))
                spi = jnp.where(row == 0, jnp.broadcast_to(hi, row.shape), pltpu.roll(last_i, 1, 0))
            acc[b][0] = acc[b][0] + (lr * spr + li * spi)
            acc[b][1] = acc[b][1] + (li * spr - lr * spi)

        _chunk_scan(lre_ref, lim_ref, bs, car_ref, are[0], -aim[0], tt, reverse=True, on_block=on_block)
        for b in bs:
            dare_ref[...] += jnp.sum(acc[b][0], axis=0, keepdims=True)
            daim_ref[...] += jnp.sum(acc[b][1], axis=0, keepdims=True)
            dyp = dyp_ref[b]
            up = up_ref[b]
            dyb = dyp.astype(BF16)
            ub = up.astype(BF16)
            lrb = lre_ref[b].astype(BF16)
            lib = lim_ref[b].astype(BF16)
            dup = d_ref[...] * dyp + _dot_nt(lrb, bbre[0]) + _dot_nt(lib, bbim[0])
            _store_chunked(dup, du_ref, b, tt)
            dbbre_ref[0] += _dot_tn(ub, lrb)
            dbbim_ref[0] += _dot_tn(ub, lib)
            dcre_ref[0] += _dot_tn(dyb, sre_ref[b].astype(BF16))
            dcim_ref[0] += _dot_tn(dyb, sim_ref[b].astype(BF16))
            dd_ref[...] += jnp.sum(dyp * up, axis=0, keepdims=True)

        @pl.when((j == N_JBLK - 1) & (tr == nt - 1))
        def _():
            mine, sends = exchange()
            for cp in sends + mine:
                cp.wait()

    tok = lambda j, t: (0, nt - 1 - t, j)
    halo = lambda j, t: (0, jnp.maximum((nt - 1 - t) * rows8 - 1, 0), j)
    blk3 = lambda j, t: (j, 0, 0)
    row1 = lambda j, t: (0, j)
    acc_shape = _out((N_JBLK, JB_CH, JB_ST), F32)
    return _pcall(
        body, name="ssm_bwd", grid=(N_JBLK, nt),
        out_shape=(_out((n_seq, seq, SSM_W), F32), acc_shape, acc_shape, acc_shape, acc_shape,
                   _out((1, N_JBLK * JB_ST), F32), _out((1, N_JBLK * JB_ST), F32),
                   _out((1, SSM_W), F32),
                   _out((N_DEV,) + g_out.shape[1:], F32),
                   _out((N_DEV,) + g_glu.shape[1:], F32)),
        in_specs=[pl.BlockSpec((n_seq, tt, JB_CH), tok), pl.BlockSpec((n_seq, tt, JB_CH), tok),
                  pl.BlockSpec((n_seq, tt, JB_ST), tok), pl.BlockSpec((n_seq, tt, JB_ST), tok),
                  pl.BlockSpec((n_seq, SUBLANES, JB_ST), halo), pl.BlockSpec((n_seq, SUBLANES, JB_ST), halo),
                  pl.BlockSpec((1, JB_CH, JB_ST), blk3), pl.BlockSpec((1, JB_CH, JB_ST), blk3),
                  pl.BlockSpec((1, JB_CH, JB_ST), blk3), pl.BlockSpec((1, JB_CH, JB_ST), blk3),
                  pl.BlockSpec((1, JB_CH), row1), pl.BlockSpec((1, 1, JB_ST), blk3), pl.BlockSpec((1, 1, JB_ST), blk3),
                  HBM_SPEC, HBM_SPEC],
        out_specs=(pl.BlockSpec((n_seq, tt, JB_CH), tok),
                   pl.BlockSpec((1, JB_CH, JB_ST), blk3), pl.BlockSpec((1, JB_CH, JB_ST), blk3),
                   pl.BlockSpec((1, JB_CH, JB_ST), blk3), pl.BlockSpec((1, JB_CH, JB_ST), blk3),
                   pl.BlockSpec((1, JB_ST), row1), pl.BlockSpec((1, JB_ST), row1), pl.BlockSpec((1, JB_CH), row1),
                   HBM_SPEC, HBM_SPEC),
        scratch_shapes=[pltpu.VMEM((n_seq, tt, JB_ST), F32), pltpu.VMEM((n_seq, tt, JB_ST), F32),
                        pltpu.VMEM((n_seq, tt, JB_CH), F32), pltpu.VMEM((n_seq, tt, JB_CH), F32),
                        pltpu.VMEM((n_seq, SUBLANES, JB_ST), F32),
                        pltpu.SemaphoreType.DMA((7 * 2,)), pltpu.SemaphoreType.DMA((7 * 2,)),
                        pltpu.SemaphoreType.DMA((2,))],
        compiler_params=_params(2),
    )(dy, u, s_re, s_im, s_re, s_im, bb_re, bb_im, c_re_t, c_imn_t, d_row, ab_re, ab_im, g_out, g_glu)


def _mix(x2, tgt2, y, proj, gf, b_glu, conv8, w_glu_f, w_out_f, seq):
    n = x2.shape[0]
    tm = TOK_TILE
    tiles_per_seq = seq // tm
    rows8 = tm // SUBLANES

    def body(x_ref, t_ref, y_ref, zs_ref, h_ref, bc_ref, cc_ref, zc_ref, hp_ref, ccp_ref,
             gf_ref, bg_ref, cw_ref, wg_ref, wo_ref,
             dh2_ref, dy_ref, dzs_ref, dbc_ref, dzc_ref, dyc_ref,
             dwo_ref, dwg_ref, loss_ref, dgf_ref, dbg_ref, dcw_ref):
        i = pl.program_id(0)

        @pl.when(i == 0)
        def _():
            for r in (dwo_ref, dwg_ref, loss_ref, dgf_ref, dbg_ref, dcw_ref):
                r[...] = jnp.zeros_like(r)

        yv = y_ref[...]
        y1, dgelu = _gelu_and_grad(yv)
        y1b = y1.astype(BF16)
        gate = _sigmoid(_dot(y1b, wg_ref[...]) + bg_ref[...])
        y2 = y1 * gate
        szs, dszs = _silu_and_grad(zs_ref[...])
        yssm = y2 * szs
        hv = h_ref[...]
        ccv = cc_ref[...]
        bcv = bc_ref[...]
        v = ccv * hv
        first = (i % tiles_per_seq) == 0
        vhalo = jnp.where(first, 0.0, ccp_ref[...] * hp_ref[...])
        v1 = _shift_down(v, vhalo, 1)
        v2 = _shift_down(v, vhalo, 2)
        w0 = cw_ref[0:1, :]
        w1 = cw_ref[1:2, :]
        w2 = cw_ref[2:3, :]
        yc = w0 * v2 + w1 * v1 + w2 * v
        szc, dszc = _silu_and_grad(zc_ref[...])
        yconv = (bcv * yc) * szc
        ysb = yssm.astype(BF16)
        ycb = yconv.astype(BF16)
        h2 = x_ref[...] + _dot(ysb, wo_ref[0:SSM_W, :]) + _dot(ycb, wo_ref[SSM_W:, :])
        r2 = lax.rsqrt(jnp.mean(h2 * h2, axis=-1, keepdims=True) + EPS)
        hn = h2 * r2
        gfv = gf_ref[...]
        err = hn * gfv - t_ref[...]
        loss_ref[...] += 0.5 * jnp.sum(jnp.mean(err * err, axis=-1, keepdims=True))
        dout = err * (1.0 / D_MODEL)
        dgf_ref[...] += jnp.sum(dout * hn, axis=0, keepdims=True)
        dn = dout * gfv
        dh2 = r2 * (dn - hn * jnp.mean(dn * hn, axis=-1, keepdims=True))
        dh2_ref[...] = dh2
        dh2b = dh2.astype(BF16)
        dwo_ref[0:SSM_W, :] += _dot_tn(ysb, dh2b)
        dwo_ref[SSM_W:, :] += _dot_tn(ycb, dh2b)
        dyssm = _dot_nt(dh2b, wo_ref[0:SSM_W, :])
        dyconv = _dot_nt(dh2b, wo_ref[SSM_W:, :])
        dy2 = dyssm * szs
        dzs_ref[...] = (dyssm * y2 * dszs).astype(BF16)
        dgp = dy2 * y1 * (gate * (1.0 - gate))
        dgpb = dgp.astype(BF16)
        dy1 = dy2 * gate + _dot_nt(dgpb, wg_ref[...])
        dwg_ref[...] += _dot_tn(y1b, dgpb)
        dbg_ref[...] += jnp.sum(dgp, axis=0, keepdims=True)
        dy_ref[...] = dy1 * dgelu
        dbc_ref[...] = (dyconv * yc * szc).astype(BF16)
        dyc = dyconv * bcv * szc
        dyc_ref[...] = dyc
        dzc_ref[...] = (dyconv * bcv * yc * dszc).astype(BF16)
        dcw_ref[0:1, :] += jnp.sum(dyc * v2, axis=0, keepdims=True)
        dcw_ref[1:2, :] += jnp.sum(dyc * v1, axis=0, keepdims=True)
        dcw_ref[2:3, :] += jnp.sum(dyc * v, axis=0, keepdims=True)

    tile_d = pl.BlockSpec((tm, D_MODEL), lambda i: (i, 0))
    tile_s = pl.BlockSpec((tm, SSM_W), lambda i: (i, 0))
    seg_of = lambda c: pl.BlockSpec((tm, SSM_W), lambda i: (i, c))
    halo_of = lambda c: pl.BlockSpec((SUBLANES, SSM_W), lambda i: (jnp.maximum(i * rows8 - 1, 0), c))
    const = lambda shape: pl.BlockSpec(shape, lambda i: (0,) * len(shape))
    seg = _out((n, SSM_W), F32)
    seg_b = _out((n, SSM_W), BF16)
    return _pcall(
        body, name="mix", grid=(n // tm,),
        out_shape=(_out((n, D_MODEL), F32), seg, seg_b, seg_b, seg_b, seg,
                   _out((D_MODEL, D_MODEL), F32), _out((SSM_W, SSM_W), F32),
                   _out((SUBLANES, LANES), F32), _out((1, D_MODEL), F32),
                   _out((1, SSM_W), F32), _out((SUBLANES, CONV_W), F32)),
        in_specs=[tile_d, tile_d, tile_s, seg_of(SEG_ZS), seg_of(SEG_H), seg_of(SEG_BC), seg_of(SEG_CC), seg_of(SEG_ZC),
                  halo_of(SEG_H), halo_of(SEG_CC),
                  const((1, D_MODEL)), const((1, SSM_W)), const((SUBLANES, CONV_W)),
                  const((SSM_W, SSM_W)), const((D_MODEL, D_MODEL))],
        out_specs=(tile_d, tile_s, tile_s, tile_s, tile_s, tile_s,
                   const((D_MODEL, D_MODEL)), const((SSM_W, SSM_W)), const((SUBLANES, LANES)),
                   const((1, D_MODEL)), const((1, SSM_W)), const((SUBLANES, CONV_W))),
        compiler_params=_params(1),
    )(x2, tgt2, y, proj, proj, proj, proj, proj, proj, proj, gf, b_glu, conv8, w_glu_f, w_out_f)


def _in_bwd(x2, dh2, du, dzs, dyc, proj, dbc, dzc, g1, conv8, w_full, seq):
    n = x2.shape[0]
    tm = TOK_TILE
    n_tiles = n // tm
    tiles_per_seq = seq // tm
    rows8 = tm // SUBLANES
    n_blk8 = n // SUBLANES

    def body(x_ref, dh2_ref, du_ref, dzs_ref, dyc_ref, dycn_ref, h_ref, cc_ref, dbc_ref, dzc_ref,
             g_ref, cw_ref, w_ref, gx_ref, dp_ref, dg_ref):
        i = pl.program_id(0)

        @pl.when(i == 0)
        def _():
            dg_ref[...] = jnp.zeros_like(dg_ref)

        dyc = dyc_ref[...]
        last = (i % tiles_per_seq) == tiles_per_seq - 1
        nhalo = jnp.where(last, 0.0, dycn_ref[...])
        dv = (cw_ref[2:3, :] * dyc + cw_ref[1:2, :] * _shift_up(dyc, nhalo, 1)
              + cw_ref[0:1, :] * _shift_up(dyc, nhalo, 2))
        parts = (du_ref[...], dzs_ref[...], dv * cc_ref[...], dbc_ref[...], dv * h_ref[...], dzc_ref[...])
        dxn = jnp.zeros((tm, D_MODEL), F32)
        for k, p in enumerate(parts):
            pb = p.astype(BF16)
            dp_ref[:, k * SSM_W:(k + 1) * SSM_W] = pb
            dxn = dxn + _dot_nt(pb, w_ref[:, k * SSM_W:(k + 1) * SSM_W])
        x = x_ref[...]
        r = lax.rsqrt(jnp.mean(x * x, axis=-1, keepdims=True) + EPS)
        xh = x * r
        dg_ref[...] += jnp.sum(dxn * xh, axis=0, keepdims=True)
        dn = dxn * g_ref[...]
        gx_ref[...] = dh2_ref[...] + r * (dn - xh * jnp.mean(dn * xh, axis=-1, keepdims=True))

    tile_d = pl.BlockSpec((tm, D_MODEL), lambda i: (i, 0))
    tile_s = pl.BlockSpec((tm, SSM_W), lambda i: (i, 0))
    seg_of = lambda c: pl.BlockSpec((tm, SSM_W), lambda i: (i, c))
    nhalo = pl.BlockSpec((SUBLANES, SSM_W), lambda i: (jnp.minimum((i + 1) * rows8, n_blk8 - 1), 0))
    const = lambda shape: pl.BlockSpec(shape, lambda i: (0,) * len(shape))
    return _pcall(
        body, name="in_bwd", grid=(n_tiles,),
        out_shape=(_out((n, D_MODEL), F32), _out((n, IN_COLS), BF16),
                   _out((SUBLANES, D_MODEL), F32)),
        in_specs=[tile_d, tile_d, tile_s, tile_s, tile_s, nhalo, seg_of(SEG_H), seg_of(SEG_CC), tile_s, tile_s,
                  const((1, D_MODEL)), const((SUBLANES, CONV_W)), const((D_MODEL, IN_COLS))],
        out_specs=(tile_d, pl.BlockSpec((tm, IN_COLS), lambda i: (i, 0)), const((SUBLANES, D_MODEL))),
        compiler_params=_params(1),
    )(x2, dh2, du, dzs, dyc, dyc, proj, proj, dbc, dzc, g1, conv8, w_full)


_HALF_BLOCKS = ((0, 0), (0, 1), (1, 0), (2, 0), (1, 1), (2, 1), (3, 0), (3, 1))


def _dw_in_exchange(chips, xn, dproj, smalls):
    n = xn.shape[0]
    tk = min(1024, n)
    nk = n // tk
    piece = (D_MODEL, COLS_PER_DEV)
    hr = D_MODEL // 2
    n_half = len(_HALF_BLOCKS)
    n_small = len(smalls)
    assert _HALF_BLOCKS[0][1] == 0 and _HALF_BLOCKS[1][1] == 1
    order = jnp.stack([chips[b] for b, _ in _HALF_BLOCKS]
                      + [jnp.int32(t) for _, t in _HALF_BLOCKS]).astype(jnp.int32)

    def body(order_ref, xn_hbm, dp_ref, *refs):
        sm_refs = refs[:n_small]
        own_ref, rchip_ref = refs[n_small:n_small + 2]
        rsm_refs = refs[n_small + 2:2 * n_small + 2]
        (xn_ref, acc, stage, rbuf, kbuf, relay_in, xn_sems, give_send, give_recv, keep_send, keep_recv,
         relay_send, relay_recv, sm_send, sm_recv, sm_loc) = refs[2 * n_small + 2:]
        s = pl.program_id(0)

        def xn_copy(kk, t):
            rows = pl.ds(pl.multiple_of(kk * tk, tk), tk)
            return pltpu.make_async_copy(xn_hbm.at[rows, t * hr:(t + 1) * hr], xn_ref.at[t, rows, :],
                                         xn_sems.at[2 * kk + t])

        @pl.when(s == 0)
        def _():
            for kk in range(nk):
                for t in range(2):
                    xn_copy(kk, t).start()
            xn_copy(0, 0).wait()

        @pl.when(s == 1)
        def _():
            xn_copy(0, 1).wait()

        x, y, c = _mesh_pos()
        sib = (x, y, 1 - c)
        y_nbr, x_nbr = (x, 1 - y, c), (1 - x, y, c)
        gather = _TwoLevelGather(list(sm_refs), [functools.partial(lambda r, dev: r.at[dev], r) for r in rsm_refs],
                                 sm_send, sm_recv, sm_loc)

        def give(h):
            cols = pl.ds(pl.multiple_of((1 - c) * COLS_PER_DEV, LANES), COLS_PER_DEV)
            return pltpu.make_async_remote_copy(src_ref=acc.at[h % 2, :, cols], dst_ref=stage.at[h],
                                                send_sem=give_send.at[h], recv_sem=give_recv.at[h],
                                                device_id=sib, device_id_type=MESH)

        def relay(r):
            return pltpu.make_async_remote_copy(src_ref=rbuf.at[r], dst_ref=relay_in.at[r],
                                                send_sem=relay_send.at[r], recv_sem=relay_recv.at[r],
                                                device_id=(x_nbr, y_nbr)[r], device_id_type=MESH)

        def keep(q):
            return pltpu.make_async_remote_copy(src_ref=kbuf.at[q], dst_ref=rchip_ref.at[q // 2, pl.ds((q % 2) * hr, hr), :],
                                                send_sem=keep_send.at[q], recv_sem=keep_recv.at[q],
                                                device_id=(y_nbr, x_nbr)[q // 2], device_id_type=MESH)

        def chip_sum(h):
            give(h).wait_recv()
            mine = [acc[h % 2, :, cc * COLS_PER_DEV:(cc + 1) * COLS_PER_DEV] for cc in range(2)]
            return jnp.where(c == 0, mine[0], mine[1]) + stage[h]

        @pl.when(s == 0)
        def _():
            gather.start()

        @pl.when(s == 2)
        def _():
            gather.neighbours_landed()

        @pl.when(s == n_half - 2)
        def _():
            gather.diagonal_landed()

        for k in range(2, n_half):
            @pl.when(s == k)
            def _(k=k):
                give(k - 2).wait_send()

        slot = s % 2
        t_half = order_ref[n_half + s]
        acc[slot] = _dot_tn(xn_ref[t_half, pl.ds(0, tk), :], dp_ref[pl.ds(0, tk), :])

        def kstep(kk, carry):
            for t in range(2):
                @pl.when(s == t)
                def _(t=t):
                    xn_copy(kk, t).wait()

            off = pl.multiple_of(kk * tk, tk)
            acc[slot] += _dot_tn(xn_ref[t_half, pl.ds(off, tk), :], dp_ref[pl.ds(off, tk), :])
            return carry

        n_first = max(1, (3 * nk) // 8)
        lax.fori_loop(1, n_first, kstep, 0)
        for k in range(1, n_half):
            @pl.when(s == k)
            def _(k=k):
                h = k - 1
                b, t = _HALF_BLOCKS[h]
                total = chip_sum(h)
                if b == 0:
                    rbuf[t] = total.astype(BF16)
                    relay(t).start()
                elif b < 3:
                    if (b, t) in ((1, 0), (2, 1)):
                        relay(t).wait_recv()
                        total = total + relay_in[t].astype(F32)
                    q = 2 * (b - 1) + t
                    kbuf[q] = total.astype(BF16)
                    keep(q).start()
                else:
                    own_ref[0:hr, :] = total

        lax.fori_loop(n_first, nk, kstep, 0)

        for k in range(n_half):
            @pl.when(s == k)
            def _(k=k):
                give(k).start()

        @pl.when(s == n_half - 1)
        def _():
            own_ref[hr:D_MODEL, :] = chip_sum(n_half - 1)
            give(n_half - 2).wait_send()
            give(n_half - 1).wait_send()
            for r in range(2):
                relay(r).wait_send()
            for q in range(4):
                keep(q).wait()
            gather.finish()

    half_piece = (hr, COLS_PER_DEV)
    grid_spec = pltpu.PrefetchScalarGridSpec(
        num_scalar_prefetch=1, grid=(n_half,),
        in_specs=[HBM_SPEC,
                  pl.BlockSpec((n, COLS_PER_CHIP), lambda s, order: (0, order[s])),
                  *([HBM_SPEC] * n_small)],
        out_specs=(pl.BlockSpec(piece, lambda s, order: (0, 0)), HBM_SPEC, *([HBM_SPEC] * n_small)),
        scratch_shapes=[pltpu.VMEM((2, n, hr), BF16),
                        pltpu.VMEM((2, hr, COLS_PER_CHIP), F32), pltpu.VMEM((n_half,) + half_piece, F32),
                        pltpu.VMEM((2,) + half_piece, BF16), pltpu.VMEM((4,) + half_piece, BF16),
                        pltpu.VMEM((2,) + half_piece, BF16),
                        pltpu.SemaphoreType.DMA((2 * nk,)),
                        pltpu.SemaphoreType.DMA((n_half,)), pltpu.SemaphoreType.DMA((n_half,)),
                        pltpu.SemaphoreType.DMA((4,)), pltpu.SemaphoreType.DMA((4,)),
                        pltpu.SemaphoreType.DMA((2,)), pltpu.SemaphoreType.DMA((2,)),
                        pltpu.SemaphoreType.DMA((7 * n_small,)), pltpu.SemaphoreType.DMA((7 * n_small,)),
                        pltpu.SemaphoreType.DMA((n_small,))])
    return _pcall(
        body, name="dw_in_exchange", grid_spec=grid_spec,
        out_shape=(_out(piece, F32), _out((2,) + piece, BF16),
                   *(_out((N_DEV,) + a.shape, a.dtype) for a in smalls)),
        compiler_params=_params(1),
    )(order, xn, dproj, *smalls)


def _adamw(g, w, m, v):
    m_new = ADAM_B1 * m + (1.0 - ADAM_B1) * g
    v_new = ADAM_B2 * v + (1.0 - ADAM_B2) * (g * g)
    m_hat = m_new / (1.0 - ADAM_B1 ** ADAM_STEP)
    v_hat = v_new / (1.0 - ADAM_B2 ** ADAM_STEP)
    delta = -ADAM_LR * (m_hat / (jnp.sqrt(v_hat) + ADAM_EPS) + ADAM_WD * w)
    return delta, m_new, v_new


def _reduce_adam_w_in(own, rchip, w, m, v):
    rows, cols = w.shape
    row_tile = 256

    def body(o_ref, r_ref, w_ref, m_ref, v_ref, g_ref, d_ref, nm_ref, nv_ref):
        g = o_ref[...]
        for s in range(2):
            g = g + r_ref[s].astype(F32)
        g_ref[...] = g
        d_ref[...], nm_ref[...], nv_ref[...] = _adamw(g, w_ref[...], m_ref[...], v_ref[...])

    tile = pl.BlockSpec((row_tile, cols), lambda i: (i, 0))
    shp = _out((rows, cols), F32)
    return _pcall(
        body, name="reduce_adam_w_in", grid=(rows // row_tile,),
        out_shape=(shp,) * 4,
        in_specs=[tile, pl.BlockSpec((2, row_tile, cols), lambda i: (0, i, 0)), tile, tile, tile],
        out_specs=(tile,) * 4,
        compiler_params=_params(1),
    )(own, rchip, w, m, v)


_SMALL_LEAVES = ("norm_gain", "final_norm_gain", "b_glu", "ssm_a_re", "ssm_a_im", "ssm_log_dt", "ssm_d", "conv_w",
                 "ssm_c_re", "ssm_c_im", "ssm_b_re", "ssm_b_im")


def _reduce_adam_small(r_pack, r_gc, r_gb, wmv, sharded):
    n_leaf = len(_SMALL_LEAVES)
    n_sh = len(sharded)

    def body(*refs):
        rp_ref, rgc_ref, rgb_ref = refs[:3]
        w_refs = refs[3:3 + 3 * n_leaf]
        sh_in = refs[3 + 3 * n_leaf:3 + 3 * n_leaf + 4 * n_sh]
        outs0 = 3 + 3 * n_leaf + 4 * n_sh
        loss_ref = refs[outs0]
        o_refs = refs[outs0 + 1:outs0 + 1 + 4 * n_leaf]
        sh_out = refs[outs0 + 1 + 4 * n_leaf:outs0 + 1 + 4 * n_leaf + 4 * n_sh]
        own_conv = refs[-1]

        def total(ref):
            acc = ref[0].astype(F32)
            for s in range(1, N_DEV):
                acc = acc + ref[s].astype(F32)
            return acc

        for i in range(n_sh):
            r_ref, w_ref, m_ref, v_ref = sh_in[4 * i:4 * i + 4]
            o_g, o_d, o_m, o_v = sh_out[4 * i:4 * i + 4]
            g = total(r_ref)
            o_g[...] = g
            o_d[...], o_m[...], o_v[...] = _adamw(g, w_ref[...], m_ref[...], v_ref[...])

        sp = total(rp_ref)
        sgc = total(rgc_ref)
        sgb = total(rgb_ref)
        loss_ref[...] = sp[ROW_LOSS:ROW_LOSS + SUBLANES, 0:LANES]

        def wide(r):
            return jnp.concatenate([sp[r:r + 1, :], sp[r + 1:r + 2, :]], axis=1)

        s5 = slice(ROW_S5, ROW_S5 + N_GROUPS)
        eye = (lax.broadcasted_iota(jnp.int32, (N_GROUPS, N_GROUPS), 0)
               == lax.broadcasted_iota(jnp.int32, (N_GROUPS, N_GROUPS), 1)).astype(F32)
        d_rows = jnp.broadcast_to(sp[ROW_BGLU_D + 1:ROW_BGLU_D + 2, :], (GROUP, SSM_W))
        own_p = (lax.broadcasted_iota(jnp.int32, (GROUP, SSM_W), 1) % GROUP
                 == lax.broadcasted_iota(jnp.int32, (GROUP, SSM_W), 0))
        of_group = (lax.broadcasted_iota(jnp.int32, (SSM_W, N_GROUPS), 0) // GROUP
                    == lax.broadcasted_iota(jnp.int32, (SSM_W, N_GROUPS), 1)).astype(BF16)
        d_pg = sum(_dot(t, of_group) for t in _split3(jnp.where(own_p, d_rows, 0.0)))
        me = 4 * lax.axis_index("x") + 2 * lax.axis_index("y") + lax.axis_index("c")
        for k in range(N_DEV):
            @pl.when(me == k)
            def _(k=k):
                own_conv[...] = sp[ROW_CONV:ROW_CONV + SUBLANES, k * CONV_COLS_PER_DEV:(k + 1) * CONV_COLS_PER_DEV]
        grads = {
            "norm_gain": wide(ROW_NORM_GAIN),
            "final_norm_gain": wide(ROW_FINAL_GAIN),
            "b_glu": sp[ROW_BGLU_D:ROW_BGLU_D + 1, :],
            "ssm_a_re": sp[s5, LANE_A_RE:LANE_A_RE + STATE],
            "ssm_a_im": sp[s5, LANE_A_IM:LANE_A_IM + STATE],
            "ssm_log_dt": jnp.sum(sp[s5, LANE_LOG_DT:LANE_LOG_DT + 1] * eye, axis=0, keepdims=True),
            "ssm_d": d_pg,
            "ssm_c_re": sgc[:, 0:STATE],
            "ssm_c_im": sgc[:, STATE:2 * STATE],
            "ssm_b_re": sgb[:, 0:STATE],
            "ssm_b_im": sgb[:, STATE:2 * STATE],
        }
        for i, name in enumerate(_SMALL_LEAVES):
            w_ref, m_ref, v_ref = w_refs[3 * i:3 * i + 3]
            o_g, o_d, o_m, o_v = o_refs[4 * i:4 * i + 4]
            if name == "conv_w":
                for k in range(w_ref.shape[0]):
                    g = own_conv[k:k + 1, :]
                    o_g[k] = g
                    o_d[k], o_m[k], o_v[k] = _adamw(g, w_ref[k], m_ref[k], v_ref[k])
                continue
            g = grads[name]
            o_g[...] = g
            o_d[...], o_m[...], o_v[...] = _adamw(g, w_ref[...], m_ref[...], v_ref[...])

    flat_w = [a for name in _SMALL_LEAVES for a in wmv[name]]
    leaf_shapes = [_out(wmv[name][0].shape, F32) for name in _SMALL_LEAVES for _ in range(4)]
    sh_shapes = [_out(entry[1].shape, F32) for entry in sharded for _ in range(4)]
    operands = (r_pack, r_gc, r_gb, *flat_w, *(a for entry in sharded for a in entry))
    out_shape = (_out((SUBLANES, LANES), F32), *leaf_shapes, *sh_shapes)
    outs = _pcall(
        body, name="reduce_adam_small", grid=(1,), out_shape=out_shape,
        in_specs=_whole_specs(operands), out_specs=tuple(_whole_specs(out_shape)),
        scratch_shapes=[pltpu.VMEM((SUBLANES, CONV_COLS_PER_DEV), F32)],
        compiler_params=_params(1),
    )(*operands)
    leaves = {name: outs[1 + 4 * i:5 + 4 * i] for i, name in enumerate(_SMALL_LEAVES)}
    first = 1 + 4 * n_leaf
    return outs[0], leaves, [outs[first + 4 * i:first + 4 * i + 4] for i in range(n_sh)]


def kernel(x, norm_gain, w_in, ssm_a_re, ssm_a_im, ssm_log_dt, ssm_b_re, ssm_b_im, ssm_c_re, ssm_c_im, ssm_d, w_glu, b_glu, conv_w, w_out, final_norm_gain, loss_target, m_norm_gain, m_w_in, m_ssm_a_re, m_ssm_a_im, m_ssm_log_dt, m_ssm_b_re, m_ssm_b_im, m_ssm_c_re, m_ssm_c_im, m_ssm_d, m_w_glu, m_b_glu, m_conv_w, m_w_out, m_final_norm_gain, v_norm_gain, v_w_in, v_ssm_a_re, v_ssm_a_im, v_ssm_log_dt, v_ssm_b_re, v_ssm_b_im, v_ssm_c_re, v_ssm_c_im, v_ssm_d, v_w_glu, v_b_glu, v_conv_w, v_w_out, v_final_norm_gain):
    n_seq, seq, _ = x.shape
    n = n_seq * seq

    gh_p = lambda b4: jnp.transpose(b4, (0, 1, 3, 2)).reshape(N_GROUPS * GROUP, STATE)
    c2 = lambda a: a.reshape(N_GROUPS * GROUP, STATE)
    b_re2, b_im2 = gh_p(ssm_b_re), gh_p(ssm_b_im)
    d_row = ssm_d[0].reshape(1, SSM_W)

    x2 = x.reshape(n, D_MODEL)
    tgt2 = loss_target.reshape(n, D_MODEL)
    mx, my, mc = lax.axis_index("x"), lax.axis_index("y"), lax.axis_index("c")
    chip_ids = [2 * cx + cy for cx, cy in ((mx, my), (1 - mx, my), (mx, 1 - my), (1 - mx, 1 - my))]
    arrival = chip_ids
    xn, proj, w_in_f, s5 = _in_proj(
        jnp.stack(arrival).astype(jnp.int32), x2, norm_gain, w_in[0].astype(BF16),
        (ssm_a_re[0], ssm_a_im[0], ssm_log_dt, b_re2, b_im2, c2(ssm_c_re), c2(ssm_c_im)))
    a_re_x, a_im_x, log_dt_x, ab_re, ab_im, bb_re_m, bb_im_m, c_re_m, c_imn_m = s5
    u3 = proj.reshape(n_seq, seq, IN_COLS)
    conv_p = jnp.pad(conv_w[0], ((0, SUBLANES - 3), (0, LANES - CONV_COLS_PER_DEV)))
    s_re, s_im, y3, w_out_f, w_glu_f, conv_all = _ssm_fwd(
        u3, bb_re_m, bb_im_m, c_re_m, c_imn_m, d_row, ab_re, ab_im,
        w_out[0].astype(BF16), w_glu[0].astype(BF16), conv_p, n_seq, seq)
    conv8 = jnp.transpose(conv_all[:, :, :CONV_COLS_PER_DEV], (1, 0, 2)).reshape(SUBLANES, CONV_W)
    (dh2, dy, dzs, dbc, dzc, dyc, dw_out, dw_glu, loss_t, dgf, dbg, dcw) = _mix(
        x2, tgt2, y3.reshape(n, SSM_W), proj, final_norm_gain.reshape(1, D_MODEL), b_glu, conv8,
        w_glu_f, w_out_f, seq)

    du3, dc_re_d, dc_im_d, dbb_re_d, dbb_im_d, dab_re, dab_im, dd, r_out, r_glu = _ssm_bwd(
        dy.reshape(n_seq, seq, SSM_W), u3, s_re, s_im, bb_re_m, bb_im_m, c_re_m, c_imn_m, d_row, ab_re, ab_im,
        dw_out.reshape(N_DEV, OUT_ROWS_PER_DEV, D_MODEL), dw_glu.reshape(N_DEV, GLU_ROWS_PER_DEV, SSM_W), n_seq, seq)
    du = du3.reshape(n, SSM_W)
    grad_x2, dproj, dg8 = _in_bwd(x2, dh2, du, dzs, dyc, proj, dbc, dzc, norm_gain, conv8, w_in_f, seq)
    pack, gc, gb = _ssm_disc_bwd_pack(
        a_re_x, a_im_x, log_dt_x, b_re2, b_im2, dab_re.reshape(N_GROUPS, STATE), dab_im.reshape(N_GROUPS, STATE),
        dbb_re_d, dbb_im_d, loss_t, dg8, dgf, dbg, dd, dcw, dc_re_d, dc_im_d)

    own_in, rchip_in, r_pack, r_gc, r_gb = _dw_in_exchange(
        [chip_ids[3], chip_ids[2], chip_ids[1], chip_ids[0]],
        xn, dproj, [pack, gc, gb])

    flat2 = lambda a: a.reshape(a.shape[-2:]) if a.ndim > 2 else a.reshape(1, -1)
    c2 = lambda a: a.reshape(N_GROUPS * GROUP, STATE)
    wmv = dict(norm_gain=(norm_gain, m_norm_gain, v_norm_gain),
               final_norm_gain=tuple(flat2(a) for a in (final_norm_gain, m_final_norm_gain, v_final_norm_gain)),
               b_glu=(b_glu, m_b_glu, v_b_glu),
               ssm_a_re=tuple(flat2(a) for a in (ssm_a_re, m_ssm_a_re, v_ssm_a_re)),
               ssm_a_im=tuple(flat2(a) for a in (ssm_a_im, m_ssm_a_im, v_ssm_a_im)),
               ssm_log_dt=(ssm_log_dt, m_ssm_log_dt, v_ssm_log_dt),
               ssm_d=tuple(jnp.transpose(a, (0, 2, 1)).reshape(GROUP, N_GROUPS) for a in (ssm_d, m_ssm_d, v_ssm_d)),
               conv_w=tuple(jnp.transpose(a, (1, 0, 2)) for a in (conv_w, m_conv_w, v_conv_w)),
               ssm_c_re=tuple(c2(a) for a in (ssm_c_re, m_ssm_c_re, v_ssm_c_re)),
               ssm_c_im=tuple(c2(a) for a in (ssm_c_im, m_ssm_c_im, v_ssm_c_im)),
               ssm_b_re=(b_re2, gh_p(m_ssm_b_re), gh_p(v_ssm_b_re)),
               ssm_b_im=(b_im2, gh_p(m_ssm_b_im), gh_p(v_ssm_b_im)))

    res_in = _reduce_adam_w_in(own_in, rchip_in, w_in[0], m_w_in[0], v_w_in[0])
    loss8, small, (res_out, res_glu) = _reduce_adam_small(
        r_pack, r_gc, r_gb, wmv,
        [(r_out, w_out[0], m_w_out[0], v_w_out[0]), (r_glu, w_glu[0], m_w_glu[0], v_w_glu[0])])
    loss = loss8[0, 0]

    shapes = dict(norm_gain=(1, D_MODEL), ssm_a_re=(1, N_GROUPS, STATE), ssm_a_im=(1, N_GROUPS, STATE),
                  ssm_log_dt=(1, N_GROUPS), ssm_c_re=(1, N_GROUPS, GROUP, STATE), ssm_c_im=(1, N_GROUPS, GROUP, STATE),
                  b_glu=(1, SSM_W), final_norm_gain=(D_MODEL,))
    big = dict(w_in=res_in, w_glu=res_glu, w_out=res_out)

    def leaf(kind, name):
        if name in big:
            return big[name][kind][None]
        if name in ("ssm_b_re", "ssm_b_im"):
            return jnp.transpose(small[name][kind].reshape(1, N_GROUPS, GROUP, STATE), (0, 1, 3, 2))
        if name == "ssm_d":
            return jnp.transpose(small[name][kind].reshape(1, GROUP, N_GROUPS), (0, 2, 1))
        if name == "conv_w":
            return jnp.transpose(small[name][kind], (1, 0, 2))
        return small[name][kind].reshape(shapes[name])

    order = ["norm_gain", "w_in", "ssm_a_re", "ssm_a_im", "ssm_log_dt", "ssm_b_re", "ssm_b_im", "ssm_c_re",
             "ssm_c_im", "ssm_d", "w_glu", "b_glu", "conv_w", "w_out", "final_norm_gain"]
    outs = [loss, grad_x2.reshape(x.shape)]
    for kind in range(4):
        outs += [leaf(kind, name) for name in order]
    return tuple(outs)
```

```python
import functools
import math

import jax
import jax.numpy as jnp
from jax import lax
from jax.experimental import pallas as pl
from jax.experimental.pallas import tpu as pltpu

F32 = jnp.float32
BF16 = jnp.bfloat16

N_DEV = 8
D_MODEL = 1024
SSM_W = 512
CONV_W = 512
N_GROUPS = 32
GROUP = 16
STATE = 64
IN_COLS = 3072
SEG_U, SEG_ZS, SEG_H, SEG_BC, SEG_CC, SEG_ZC = range(6)
COLS_PER_DEV = IN_COLS // N_DEV
N_CHIP = N_DEV // 2
COLS_PER_CHIP = 2 * COLS_PER_DEV
OUT_ROWS_PER_DEV = D_MODEL // N_DEV
GLU_ROWS_PER_DEV = SSM_W // N_DEV
CONV_COLS_PER_DEV = CONV_W // N_DEV
EPS = 1e-6

N_JBLK = 4
JB_CH = SSM_W // N_JBLK
JB_ST = N_GROUPS * STATE // N_JBLK

ADAM_LR = 0.001
ADAM_B1 = 0.9
ADAM_B2 = 0.999
ADAM_EPS = 1e-08
ADAM_WD = 0.01
ADAM_STEP = 10

SUBLANES = 8
LANES = 128
VMEM_LIMIT = 48 * 1024 * 1024
TOK_TILE = 256
IN_TILE = 1024
SCAN_TILE = 1024

MESH = pl.DeviceIdType.MESH
HBM_SPEC = pl.BlockSpec(memory_space=pltpu.HBM)


def _build(body, **kw):
    return pl.pallas_call(body, **kw)


def _pcall(body, **kw):
    def call(*operands):
        pinned = [a if jnp.issubdtype(a.dtype, jnp.integer) else pltpu.with_memory_space_constraint(a, pltpu.HBM)
                  for a in operands]
        return _build(body, **kw)(*pinned)
    return call


def _whole_specs(arrays):
    return [pl.BlockSpec(a.shape, functools.partial(lambda nd, i: (0,) * nd, len(a.shape))) for a in arrays]


def _out(shape, dtype):
    return pltpu.HBM(tuple(shape), dtype)


def _params(n_grid):
    return pltpu.CompilerParams(dimension_semantics=("arbitrary",) * n_grid,
                                vmem_limit_bytes=VMEM_LIMIT)


def _dot(a, b):
    return jnp.dot(a, b, preferred_element_type=F32)


def _dot_nt(a, b):
    return lax.dot_general(a, b, (((1,), (1,)), ((), ())), preferred_element_type=F32)


def _dot_tn(a, b):
    return lax.dot_general(a, b, (((0,), (0,)), ((), ())), preferred_element_type=F32)


def _sigmoid(z):
    return 1.0 / (1.0 + jnp.exp(-z))


_GELU_C = math.sqrt(2.0 / math.pi)


def _gelu_and_grad(y):
    inner = _GELU_C * (y + 0.044715 * (y * y * y))
    t = jnp.tanh(inner)
    g = 0.5 * y * (1.0 + t)
    dg = 0.5 * (1.0 + t) + 0.5 * y * (1.0 - t * t) * (_GELU_C * (1.0 + 3.0 * 0.044715 * (y * y)))
    return g, dg


def _silu_and_grad(z):
    s = _sigmoid(z)
    return z * s, s * (1.0 + z * (1.0 - s))


def _shift_down(v, halo, k):
    rolled = pltpu.roll(v, k, 0)
    row = lax.broadcasted_iota(jnp.int32, v.shape, 0)
    for r in range(k):
        rolled = jnp.where(row == r, halo[SUBLANES - k + r:SUBLANES - k + r + 1, :], rolled)
    return rolled


def _shift_up(v, halo, k):
    n = v.shape[0]
    rolled = pltpu.roll(v, n - k, 0)
    row = lax.broadcasted_iota(jnp.int32, v.shape, 0)
    for r in range(k):
        rolled = jnp.where(row == n - k + r, halo[r:r + 1, :], rolled)
    return rolled


def _mesh_pos():
    return lax.axis_index("x"), lax.axis_index("y"), lax.axis_index("c")


def _direct_copies(srcs_for, out_refs, send_sems, recv_sems, loc_sems):
    x, y, c = _mesh_pos()
    me_id = 4 * x + 2 * y + c
    n_arr = len(out_refs)
    dsts = [r.at[me_id] for r in out_refs]
    own = srcs_for(me_id)
    mine = [pltpu.make_async_copy(own[a], dsts[a], loc_sems.at[a]) for a in range(n_arr)]
    sends = []
    for k in range(1, N_DEV):
        px, py, pc = x ^ ((k >> 2) & 1), y ^ ((k >> 1) & 1), c ^ (k & 1)
        src = srcs_for(4 * px + 2 * py + pc)
        for a in range(n_arr):
            sends.append(pltpu.make_async_remote_copy(
                src_ref=src[a], dst_ref=dsts[a],
                send_sem=send_sems.at[(k - 1) * n_arr + a], recv_sem=recv_sems.at[(k - 1) * n_arr + a],
                device_id=(px, py, pc), device_id_type=MESH))
    return mine, sends


class _TwoLevelGather:
    def __init__(self, srcs, slots, send_sems, recv_sems, loc_sems):
        self.srcs, self.slots, self.n_arr = srcs, slots, len(srcs)
        self.send_sems, self.recv_sems, self.loc_sems = send_sems, recv_sems, loc_sems
        x, y, c = _mesh_pos()
        self.c = c
        self.me, self.sib = (x, y, c), (x, y, 1 - c)
        self.chips = [(1 - x, y), (x, 1 - y), (1 - x, 1 - y)]

    def _copies(self, k, block, to, from_src=False):
        dev = 4 * block[0] + 2 * block[1] + block[2]
        return [pltpu.make_async_remote_copy(
            src_ref=self.srcs[a] if from_src else self.slots[a](dev), dst_ref=self.slots[a](dev),
            send_sem=self.send_sems.at[k * self.n_arr + a], recv_sem=self.recv_sems.at[k * self.n_arr + a],
            device_id=to, device_id_type=MESH) for a in range(self.n_arr)]

    def _local(self):
        dev = 4 * self.me[0] + 2 * self.me[1] + self.me[2]
        return [pltpu.make_async_copy(self.srcs[a], self.slots[a](dev), self.loc_sems.at[a])
                for a in range(self.n_arr)]

    def start(self):
        for cp in self._local() + self._copies(0, self.me, self.sib, True):
            cp.start()
        for j in (0, 1):
            for cp in self._copies(1 + j, self.me, (*self.chips[j], self.c), True):
                cp.start()

    def wait_own(self):
        for cp in self._local():
            cp.wait()

    def wait_sibling(self):
        for cp in self._copies(0, self.sib, self.me):
            cp.wait_recv()

    def wait_and_pass_on(self, j):
        chip = self.chips[j]
        for cp in self._copies(1 + j, (*chip, self.c), self.me):
            cp.wait_recv()
        for cp in self._copies(4 + j, (*chip, self.c), self.sib):
            cp.start()

    def neighbours_landed(self):
        x, y, c = self.me
        self.wait_and_pass_on(0)
        self.wait_and_pass_on(1)
        for cp in self._copies(1 + 2, (x ^ c, y ^ (1 - c), c), (x ^ (1 - c), y ^ c, c)):
            cp.start()

    def diagonal_landed(self):
        self.wait_and_pass_on(2)

    def wait_passed_on(self, j):
        for cp in self._copies(4 + j, (*self.chips[j], 1 - self.c), self.me):
            cp.wait_recv()

    def wait_sends(self):
        for cp in self._copies(0, self.me, self.sib, True):
            cp.wait_send()
        for j, chip in enumerate(self.chips):
            for cp in self._copies(1 + j, self.me, (*chip, self.c), True) + self._copies(4 + j, (*chip, self.c), self.sib):
                cp.wait_send()

    def finish(self):
        self.wait_sibling()
        for j in range(3):
            self.wait_passed_on(j)
        self.wait_sends()
        self.wait_own()


def _disc(a_re, a_im, log_dt, b_re, b_im):
    dt = jnp.exp(log_dt)
    mag = jnp.exp(a_re * dt)
    ab_re = mag * jnp.cos(a_im * dt)
    ab_im = mag * jnp.sin(a_im * dt)
    den = a_re * a_re + a_im * a_im
    p_re = ab_re - 1.0
    p_im = ab_im
    q_re = (p_re * a_re + p_im * a_im) / den
    q_im = (p_im * a_re - p_re * a_im) / den
    bb_re = q_re * b_re - q_im * b_im
    bb_im = q_re * b_im + q_im * b_re
    return ab_re, ab_im, bb_re, bb_im


def _split3(v):
    hi = v.astype(BF16)
    r1 = v - hi.astype(F32)
    mid = r1.astype(BF16)
    lo = (r1 - mid.astype(F32)).astype(BF16)
    return hi, mid, lo


def _select_dot(sel, v):
    return sum(_dot(sel, t) for t in _split3(v))


PACK_ROWS = 72
PACK_W = 512
ROW_FINAL_GAIN, ROW_NORM_GAIN, ROW_BGLU_D, ROW_CONV, ROW_LOSS, ROW_S5 = 0, 8, 16, 24, 32, 40
LANE_A_RE, LANE_A_IM, LANE_LOG_DT = 0, 128, 256


def _ssm_disc_bwd_pack(a_re_x, a_im_x, log_dt_x, b_re, b_im, g_ab_re, g_ab_im, dbb_re_d, dbb_im_d,
                       loss_t, dg8, dgf, dbg, dd, dcw, dc_re_d, dc_im_d):
    rows_gh = N_GROUPS * GROUP

    def body(are, aim, ldt, bre, bim, gabre, gabim, dbbre_ref, dbbim_ref,
             loss_ref, dg8_ref, dgf_ref, dbg_ref, dd_ref, dcw_ref, dcre_ref, dcim_ref,
             p_ref, gc_ref, gb_ref, gbb_re, gbb_im):
        r_g = lax.broadcasted_iota(jnp.int32, (N_GROUPS, rows_gh), 0)
        c_gh = lax.broadcasted_iota(jnp.int32, (N_GROUPS, rows_gh), 1)
        group_sum = (c_gh // GROUP == r_g).astype(BF16)
        r_gh = lax.broadcasted_iota(jnp.int32, (rows_gh, N_GROUPS), 0)
        c_g = lax.broadcasted_iota(jnp.int32, (rows_gh, N_GROUPS), 1)
        first_row = (r_gh == c_g * GROUP).astype(BF16)

        def diag_block(ref, j, gi):
            return ref[j, gi * GROUP:(gi + 1) * GROUP, gi * STATE:(gi + 1) * STATE]

        for j in range(N_JBLK):
            for gi in range(SUBLANES):
                r0 = (j * SUBLANES + gi) * GROUP
                gbb_re[r0:r0 + GROUP, :] = diag_block(dbbre_ref, j, gi)
                gbb_im[r0:r0 + GROUP, :] = diag_block(dbbim_ref, j, gi)
                both = jnp.concatenate([diag_block(dcre_ref, j, gi), -diag_block(dcim_ref, j, gi)], axis=1)
                gc_ref[r0:r0 + GROUP, :] = both.astype(BF16)

        _, vjp = jax.vjp(_disc, are[...], aim[...], ldt[...], bre[...], bim[...])
        d_are, d_aim, d_ldt, d_bre, d_bim = vjp((_select_dot(first_row, gabre[...]), _select_dot(first_row, gabim[...]),
                                                 gbb_re[...], gbb_im[...]))
        gb_ref[...] = jnp.concatenate([d_bre, d_bim], axis=1).astype(BF16)

        p_ref[...] = jnp.zeros_like(p_ref)
        half = D_MODEL // 2
        for r, src in ((ROW_FINAL_GAIN, dgf_ref), (ROW_NORM_GAIN, dg8_ref)):
            p_ref[r:r + 1, :] = src[0:1, 0:half]
            p_ref[r + 1:r + 2, :] = src[0:1, half:D_MODEL]
        p_ref[ROW_BGLU_D:ROW_BGLU_D + 1, :] = dbg_ref[...]
        p_ref[ROW_BGLU_D + 1:ROW_BGLU_D + 2, :] = dd_ref[...]
        p_ref[ROW_CONV:ROW_CONV + SUBLANES, :] = dcw_ref[...]
        p_ref[ROW_LOSS:ROW_LOSS + SUBLANES, 0:LANES] = loss_ref[...]
        s5 = slice(ROW_S5, ROW_S5 + N_GROUPS)
        p_ref[s5, LANE_A_RE:LANE_A_RE + STATE] = _select_dot(group_sum, d_are)
        p_ref[s5, LANE_A_IM:LANE_A_IM + STATE] = _select_dot(group_sum, d_aim)
        p_ref[s5, LANE_LOG_DT:LANE_LOG_DT + LANES] = _select_dot(group_sum, jnp.broadcast_to(d_ldt, (rows_gh, LANES)))

    operands = (a_re_x, a_im_x, log_dt_x, b_re, b_im, g_ab_re, g_ab_im, dbb_re_d, dbb_im_d,
                loss_t, dg8, dgf, dbg, dd, dcw, dc_re_d, dc_im_d)
    out_shape = (_out((PACK_ROWS, PACK_W), F32),
                 _out((rows_gh, 2 * STATE), BF16),
                 _out((rows_gh, 2 * STATE), BF16))
    return _pcall(body, name="ssm_disc_bwd_pack", grid=(1,), out_shape=out_shape,
                  in_specs=_whole_specs(operands), out_specs=tuple(_whole_specs(out_shape)),
                  scratch_shapes=[pltpu.VMEM((rows_gh, STATE), F32), pltpu.VMEM((rows_gh, STATE), F32)],
                  compiler_params=_params(1))(*operands)


def _s5_prepare(are, aim, ldt, bre, bim, cre, cim,
                o_ax_re, o_ax_im, o_ldt_x, o_ab_re, o_ab_im, o_bb_re, o_bb_im, o_c_re, o_c_imn):
    rows_gh = N_GROUPS * GROUP
    rep = (lax.broadcasted_iota(jnp.int32, (rows_gh, N_GROUPS), 0) // GROUP
           == lax.broadcasted_iota(jnp.int32, (rows_gh, N_GROUPS), 1)).astype(BF16)
    eye = (lax.broadcasted_iota(jnp.int32, (N_GROUPS, N_GROUPS), 0)
           == lax.broadcasted_iota(jnp.int32, (N_GROUPS, N_GROUPS), 1)).astype(F32)
    ldt_col = jnp.sum(eye * ldt[...], axis=1, keepdims=True)
    a_re_x = _select_dot(rep, are[...])
    a_im_x = _select_dot(rep, aim[...])
    ldt_x = _select_dot(rep, jnp.broadcast_to(ldt_col, (N_GROUPS, LANES)))[:, 0:1]
    o_ax_re[...] = a_re_x
    o_ax_im[...] = a_im_x
    o_ldt_x[...] = ldt_x
    ab_re, ab_im, bb_re, bb_im = _disc(a_re_x, a_im_x, ldt_x, bre[...], bim[...])
    for j in range(N_JBLK):
        first = [(j * SUBLANES + gi) * GROUP for gi in range(SUBLANES)]
        o_ab_re[j] = jnp.concatenate([ab_re[r:r + 1, :] for r in first], axis=1)
        o_ab_im[j] = jnp.concatenate([ab_im[r:r + 1, :] for r in first], axis=1)
    for o, v in ((o_bb_re, bb_re), (o_bb_im, bb_im), (o_c_re, cre[...]), (o_c_imn, -cim[...])):
        for j in range(N_JBLK):
            for gi in range(SUBLANES):
                r0 = (j * SUBLANES + gi) * GROUP
                parts = [v[r0:r0 + GROUP, :] if k == gi else jnp.zeros((GROUP, STATE), F32) for k in range(SUBLANES)]
                o[j, gi * GROUP:(gi + 1) * GROUP, :] = jnp.concatenate(parts, axis=1).astype(BF16)


def _in_proj(order, x2, g1, w_in_b, s5):
    n = x2.shape[0]
    tm = min(IN_TILE, n)
    n_tiles = n // tm
    n_s5_in = len(s5)
    n_s5_out = 9

    def body(order_ref, x_ref, g_ref, w_ref, *refs):
        s5_in = refs[:n_s5_in]
        xn_ref, proj_ref, wall_ref = refs[n_s5_in:n_s5_in + 3]
        s5_out = refs[n_s5_in + 3:n_s5_in + 3 + n_s5_out]
        xn_scr, wbuf, send_sems, recv_sems, loc_sems, out_sems = refs[n_s5_in + 3 + n_s5_out:]
        k = pl.program_id(0)
        i = pl.program_id(1)

        def slot(dev):
            return wbuf.at[dev // 2, :, pl.ds(pl.multiple_of((dev % 2) * COLS_PER_DEV, LANES), COLS_PER_DEV)]

        gather = _TwoLevelGather([w_ref], [slot], send_sems, recv_sems, loc_sems)

        @pl.when((k == 0) & (i == 0))
        def _():
            gather.start()

        def own_chip():
            gather.wait_own()
            gather.wait_sibling()

        def x_chip():
            gather.neighbours_landed()
            gather.wait_passed_on(0)

        def diag_chip():
            gather.diagonal_landed()
            gather.wait_passed_on(2)

        arrivals = [own_chip, x_chip, functools.partial(gather.wait_passed_on, 1), diag_chip]
        for kk, arrived in enumerate(arrivals):
            @pl.when((k == kk) & (i == 0))
            def _(arrived=arrived):
                arrived()

        rows = pl.ds(pl.multiple_of(i * tm, tm), tm)

        @pl.when(k == 0)
        def _():
            x = x_ref[...]
            r = lax.rsqrt(jnp.mean(x * x, axis=-1, keepdims=True) + EPS)
            xn = ((x * r) * g_ref[...]).astype(BF16)
            xn_scr[rows, :] = xn
            xn_ref[...] = xn

        proj_ref[...] = _dot(xn_scr[rows, :], wbuf[order_ref[k]])

        @pl.when((k == 0) & (i == n_tiles - 1))
        def _():
            _s5_prepare(*s5_in, *s5_out)

        @pl.when((k == N_CHIP - 1) & (i == n_tiles - 1))
        def _():
            gather.wait_sends()
            outs = [pltpu.make_async_copy(wbuf.at[q], wall_ref.at[:, q * COLS_PER_CHIP:(q + 1) * COLS_PER_CHIP],
                                          out_sems.at[q]) for q in range(N_CHIP)]
            for cp in outs:
                cp.start()
            for cp in outs:
                cp.wait()

    tile_once = lambda k, i, order: (jnp.where(k == 0, i, n_tiles - 1), 0)
    whole = lambda shape: pl.BlockSpec(shape, lambda k, i, order: (0,) * len(shape))
    rows_gh = N_GROUPS * GROUP
    s5_out_shapes = ([(rows_gh, STATE), F32], [(rows_gh, STATE), F32], [(rows_gh, 1), F32],
                     [(N_JBLK, 1, JB_ST), F32], [(N_JBLK, 1, JB_ST), F32]) + ([(N_JBLK, JB_CH, JB_ST), BF16],) * 4
    grid_spec = pltpu.PrefetchScalarGridSpec(
        num_scalar_prefetch=1, grid=(N_CHIP, n_tiles),
        in_specs=[pl.BlockSpec((tm, D_MODEL), tile_once),
                  whole((1, D_MODEL)),
                  HBM_SPEC,
                  *(whole(a.shape) for a in s5)],
        out_specs=(pl.BlockSpec((tm, D_MODEL), tile_once),
                   pl.BlockSpec((tm, COLS_PER_CHIP), lambda k, i, order: (i, order[k])),
                   HBM_SPEC,
                   *(whole(shape) for shape, _ in s5_out_shapes)),
        scratch_shapes=[pltpu.VMEM((n, D_MODEL), BF16), pltpu.VMEM((N_CHIP, D_MODEL, COLS_PER_CHIP), BF16),
                        pltpu.SemaphoreType.DMA((7,)), pltpu.SemaphoreType.DMA((7,)), pltpu.SemaphoreType.DMA((1,)),
                        pltpu.SemaphoreType.DMA((N_CHIP,))])
    outs = _pcall(
        body, name="in_proj", grid_spec=grid_spec,
        out_shape=(_out((n, D_MODEL), BF16), _out((n, IN_COLS), F32),
                   _out((D_MODEL, IN_COLS), BF16),
                   *(_out(shape, dt) for shape, dt in s5_out_shapes)),
        compiler_params=_params(2),
    )(order, x2, g1, w_in_b, *s5)
    return outs[0], outs[1], outs[2], outs[3:]


def _cmul(p, q):
    return p[0] * q[0] - p[1] * q[1], p[0] * q[1] + p[1] * q[0]


def _scan_tables(ar, ai, width, reverse):
    pows = [(ar, ai)]
    for _ in range(SUBLANES - 1):
        pows.append(_cmul(pows[-1], (ar, ai)))
    row = lax.broadcasted_iota(jnp.int32, (SUBLANES, width), 0)

    def bc(v):
        return jnp.broadcast_to(v, (SUBLANES, width))

    levels = []
    for k in (1, 2, 4):
        keep = (row <= SUBLANES - 1 - k) if reverse else (row >= k)
        levels.append((jnp.where(keep, bc(pows[k - 1][0]), 0.0), jnp.where(keep, bc(pows[k - 1][1]), 0.0)))
    cre = jnp.zeros((SUBLANES, width), F32)
    cim = jnp.zeros((SUBLANES, width), F32)
    for r in range(SUBLANES):
        e = (SUBLANES - r) if reverse else (r + 1)
        cre = jnp.where(row == r, bc(pows[e - 1][0]), cre)
        cim = jnp.where(row == r, bc(pows[e - 1][1]), cim)
    return levels, (cre, cim)


def _load_chunked(src_ref, b, dst_ref, n_rows):
    n_blk = n_rows // SUBLANES
    for i in range(n_blk):
        dst_ref[b, i * SUBLANES:(i + 1) * SUBLANES, :] = src_ref[b, pl.ds(i, SUBLANES, stride=n_blk), :]


def _store_chunked(val, dst_ref, b, n_rows):
    n_blk = n_rows // SUBLANES
    for i in range(n_blk):
        dst_ref[b, pl.ds(i, SUBLANES, stride=n_blk), :] = val[i * SUBLANES:(i + 1) * SUBLANES, :]


def _chunk_scan(re_ref, im_ref, bs, car_ref, ar, ai, n_rows, reverse, on_block=None):
    width = re_ref.shape[2]
    n_blk = n_rows // SUBLANES
    shape = (SUBLANES, width)
    abr = jnp.broadcast_to(ar, shape)
    abi = jnp.broadcast_to(ai, shape)
    order = list(range(n_blk - 1, -1, -1)) if reverse else list(range(n_blk))

    def blk(ref, b, i):
        return ref[b, i * SUBLANES:(i + 1) * SUBLANES, :]

    def step(state, b, i):
        sr, si = state
        return abr * sr - abi * si + blk(re_ref, b, i), abr * si + abi * sr + blk(im_ref, b, i)

    finals = {b: (blk(re_ref, b, order[0]), blk(im_ref, b, order[0])) for b in bs}
    for i in order[1:]:
        for b in bs:
            finals[b] = step(finals[b], b, i)

    mr, mi = ar, ai
    for _ in range(n_blk.bit_length() - 1):
        mr, mi = _cmul((mr, mi), (mr, mi))
    levels, _ = _scan_tables(mr, mi, width, reverse)
    mbr = jnp.broadcast_to(mr, shape)
    mbi = jnp.broadcast_to(mi, shape)
    row = lax.broadcasted_iota(jnp.int32, shape, 0)
    edge_in = SUBLANES - 1 if reverse else 0
    edge_out = 0 if reverse else SUBLANES - 1
    sh1 = SUBLANES - 1 if reverse else 1
    states = {}
    for b in bs:
        fr, fi = finals[b]
        gr = jnp.where(row == edge_in, jnp.broadcast_to(car_ref[b, 0:1, :], shape), pltpu.roll(fr, sh1, 0))
        gi = jnp.where(row == edge_in, jnp.broadcast_to(car_ref[b, 1:2, :], shape), pltpu.roll(fi, sh1, 0))
        for (lr, li), k in zip(levels, (1, 2, 4)):
            sh = (SUBLANES - k) if reverse else k
            sr = pltpu.roll(gr, sh, 0)
            si = pltpu.roll(gi, sh, 0)
            gr, gi = gr + (lr * sr - li * si), gi + (lr * si + li * sr)
        car_ref[b, 0:1, :] = (fr + (mbr * gr - mbi * gi))[edge_out:edge_out + 1, :]
        car_ref[b, 1:2, :] = (fi + (mbr * gi + mbi * gr))[edge_out:edge_out + 1, :]
        states[b] = (gr, gi)

    for i in order:
        for b in bs:
            states[b] = step(states[b], b, i)
            re_ref[b, i * SUBLANES:(i + 1) * SUBLANES, :] = states[b][0]
            im_ref[b, i * SUBLANES:(i + 1) * SUBLANES, :] = states[b][1]
            if on_block is not None:
                on_block(b, i, *states[b])


def _ssm_fwd(u, bb_re, bb_im, c_re_t, c_imn_t, d_row, ab_re, ab_im, w_out_b, w_glu_b, conv_p, n_seq, seq):
    tt = min(SCAN_TILE, seq)
    nt = seq // tt

    def body(u_ref, bbre, bbim, cre, cimn, d_ref, are, aim, wout_ref, wglu_ref, cw_ref,
             sre_ref, sim_ref, y_ref, oout_ref, oglu_ref, ocw_ref,
             up_ref, car_ref, send_sems, recv_sems, loc_sems):
        j = pl.program_id(0)
        t = pl.program_id(1)
        gather = _TwoLevelGather(
            [wout_ref, wglu_ref, cw_ref],
            [lambda dev: oout_ref.at[pl.ds(pl.multiple_of(dev * OUT_ROWS_PER_DEV, OUT_ROWS_PER_DEV), OUT_ROWS_PER_DEV), :],
             lambda dev: oglu_ref.at[pl.ds(pl.multiple_of(dev * GLU_ROWS_PER_DEV, GLU_ROWS_PER_DEV), GLU_ROWS_PER_DEV), :],
             lambda dev: ocw_ref.at[dev]],
            send_sems, recv_sems, loc_sems)

        @pl.when((j == 0) & (t == 0))
        def _():
            gather.start()

        @pl.when((j == N_JBLK // 2) & (t == 0))
        def _():
            gather.neighbours_landed()

        @pl.when((j == N_JBLK - 1) & (t == 0))
        def _():
            gather.diagonal_landed()

        @pl.when(t == 0)
        def _():
            car_ref[...] = jnp.zeros_like(car_ref)

        bs = list(range(n_seq))
        for b in bs:
            _load_chunked(u_ref, b, up_ref, tt)
        for b in bs:
            ub = up_ref[b].astype(BF16)
            sre_ref[b] = _dot(ub, bbre[0])
            sim_ref[b] = _dot(ub, bbim[0])
            _chunk_scan(sre_ref, sim_ref, [b], car_ref, are[0], aim[0], tt, reverse=False)
        for b in bs:
            yp = (_dot_nt(sre_ref[b].astype(BF16), cre[0]) + _dot_nt(sim_ref[b].astype(BF16), cimn[0])
                  + d_ref[...] * up_ref[b])
            _store_chunked(yp, y_ref, b, tt)

        @pl.when((j == N_JBLK - 1) & (t == nt - 1))
        def _():
            gather.finish()

    tok = lambda j, t: (0, t, j)
    blk3 = lambda j, t: (j, 0, 0)
    row = lambda j, t: (0, j)
    st = _out((n_seq, seq, N_JBLK * JB_ST), F32)
    n_arr = 3
    return _pcall(
        body, name="ssm_fwd", grid=(N_JBLK, nt),
        out_shape=(st, st, _out((n_seq, seq, SSM_W), F32),
                   _out((D_MODEL, D_MODEL), BF16), _out((SSM_W, SSM_W), BF16),
                   _out((N_DEV, SUBLANES, LANES), F32)),
        in_specs=[pl.BlockSpec((n_seq, tt, JB_CH), tok),
                  pl.BlockSpec((1, JB_CH, JB_ST), blk3), pl.BlockSpec((1, JB_CH, JB_ST), blk3),
                  pl.BlockSpec((1, JB_CH, JB_ST), blk3), pl.BlockSpec((1, JB_CH, JB_ST), blk3),
                  pl.BlockSpec((1, JB_CH), row), pl.BlockSpec((1, 1, JB_ST), blk3), pl.BlockSpec((1, 1, JB_ST), blk3),
                  HBM_SPEC, HBM_SPEC, HBM_SPEC],
        out_specs=(pl.BlockSpec((n_seq, tt, JB_ST), tok), pl.BlockSpec((n_seq, tt, JB_ST), tok),
                   pl.BlockSpec((n_seq, tt, JB_CH), tok), HBM_SPEC, HBM_SPEC, HBM_SPEC),
        scratch_shapes=[pltpu.VMEM((n_seq, tt, JB_CH), F32), pltpu.VMEM((n_seq, SUBLANES, JB_ST), F32),
                        pltpu.SemaphoreType.DMA((7 * n_arr,)), pltpu.SemaphoreType.DMA((7 * n_arr,)),
                        pltpu.SemaphoreType.DMA((n_arr,))],
        compiler_params=_params(2),
    )(u, bb_re, bb_im, c_re_t, c_imn_t, d_row, ab_re, ab_im, w_out_b, w_glu_b, conv_p)


def _ssm_bwd(dy, u, s_re, s_im, bb_re, bb_im, c_re_t, c_imn_t, d_row, ab_re, ab_im, g_out, g_glu, n_seq, seq):
    tt = min(SCAN_TILE, seq)
    nt = seq // tt
    rows8 = tt // SUBLANES

    def body(dy_ref, u_ref, sre_ref, sim_ref, pre_ref, pim_ref, bbre, bbim, cre, cimn, d_ref, are, aim,
             gout_ref, gglu_ref,
             du_ref, dcre_ref, dcim_ref, dbbre_ref, dbbim_ref, dare_ref, daim_ref, dd_ref, rout_ref, rglu_ref,
             lre_ref, lim_ref, dyp_ref, up_ref, car_ref, send_sems, recv_sems, loc_sems):
        j = pl.program_id(0)
        tr = pl.program_id(1)

        def exchange():
            return _direct_copies(lambda pid: [gout_ref.at[pid], gglu_ref.at[pid]], [rout_ref, rglu_ref],
                                  send_sems, recv_sems, loc_sems)

        @pl.when((j == 0) & (tr == 0))
        def _():
            mine, sends = exchange()
            for cp in mine + sends:
                cp.start()

        @pl.when(tr == 0)
        def _():
            car_ref[...] = jnp.zeros_like(car_ref)
            for r in (dcre_ref, dcim_ref, dbbre_ref, dbbim_ref, dare_ref, daim_ref, dd_ref):
                r[...] = jnp.zeros_like(r)

        first = tr == nt - 1
        row = lax.broadcasted_iota(jnp.int32, (SUBLANES, JB_ST), 0)
        n_blk = tt // SUBLANES
        bs = list(range(n_seq))
        for b in bs:
            _load_chunked(dy_ref, b, dyp_ref, tt)
            _load_chunked(u_ref, b, up_ref, tt)
        for b in bs:
            dyb = dyp_ref[b].astype(BF16)
            lre_ref[b] = _dot(dyb, cre[0])
            lim_ref[b] = _dot(dyb, cimn[0])
        acc = {b: [jnp.zeros((SUBLANES, JB_ST), F32), jnp.zeros((SUBLANES, JB_ST), F32)] for b in bs}

        def on_block(b, i, lr, li):
            if i > 0:
                spr = sre_ref[b, (i - 1) * SUBLANES:i * SUBLANES, :]
                spi = sim_ref[b, (i - 1) * SUBLANES:i * SUBLANES, :]
            else:
                hr = jnp.where(first, 0.0, pre_ref[b, SUBLANES - 1:SUBLANES, :])
                hi = jnp.where(first, 0.0, pim_ref[b, SUBLANES - 1:SUBLANES, :])
                last_r = sre_ref[b, (n_blk - 1) * SUBLANES:n_blk * SUBLANES, :]
                last_i = sim_ref[b, (n_blk - 1) * SUBLANES:n_blk * SUBLANES, :]
                spr = jnp.where(row == 0, jnp.broadcast_to(hr, row.shape), pltpu.roll(last_r, 1, 0))
                spi = jnp.where(row == 0, jnp.broadcast_to(hi, row.shape), pltpu.roll(last_i, 1, 0))
            acc[b][0] = acc[b][0] + (lr * spr + li * spi)
            acc[b][1] = acc[b][1] + (li * spr - lr * spi)

        _chunk_scan(lre_ref, lim_ref, bs, car_ref, are[0], -aim[0], tt, reverse=True, on_block=on_block)
        def by_group(r):
            return jnp.concatenate([r[:, g * STATE:(g + 1) * STATE] for g in range(JB_ST // STATE)], axis=0)

        dare_ref[...] += by_group(sum(jnp.sum(acc[b][0], axis=0, keepdims=True) for b in bs))
        daim_ref[...] += by_group(sum(jnp.sum(acc[b][1], axis=0, keepdims=True) for b in bs))
        for b in bs:
            dyp = dyp_ref[b]
            up = up_ref[b]
            dyb = dyp.astype(BF16)
            ub = up.astype(BF16)
            lrb = lre_ref[b].astype(BF16)
            lib = lim_ref[b].astype(BF16)
            dup = d_ref[...] * dyp + _dot_nt(lrb, bbre[0]) + _dot_nt(lib, bbim[0])
            _store_chunked(dup, du_ref, b, tt)
            dbbre_ref[0] += _dot_tn(ub, lrb)
            dbbim_ref[0] += _dot_tn(ub, lib)
            dcre_ref[0] += _dot_tn(dyb, sre_ref[b].astype(BF16))
            dcim_ref[0] += _dot_tn(dyb, sim_ref[b].astype(BF16))
            dd_ref[...] += jnp.sum(dyp * up, axis=0, keepdims=True)

        @pl.when((j == N_JBLK - 1) & (tr == nt - 1))
        def _():
            mine, sends = exchange()
            for cp in sends + mine:
                cp.wait()

    tok = lambda j, t: (0, nt - 1 - t, j)
    halo = lambda j, t: (0, jnp.maximum((nt - 1 - t) * rows8 - 1, 0), j)
    blk3 = lambda j, t: (j, 0, 0)
    row1 = lambda j, t: (0, j)
    grp = lambda j, t: (j, 0)
    acc_shape = _out((N_JBLK, JB_CH, JB_ST), F32)
    return _pcall(
        body, name="ssm_bwd", grid=(N_JBLK, nt),
        out_shape=(_out((n_seq, seq, SSM_W), F32), acc_shape, acc_shape, acc_shape, acc_shape,
                   _out((N_GROUPS, STATE), F32), _out((N_GROUPS, STATE), F32),
                   _out((1, SSM_W), F32),
                   _out((N_DEV,) + g_out.shape[1:], F32),
                   _out((N_DEV,) + g_glu.shape[1:], F32)),
        in_specs=[pl.BlockSpec((n_seq, tt, JB_CH), tok), pl.BlockSpec((n_seq, tt, JB_CH), tok),
                  pl.BlockSpec((n_seq, tt, JB_ST), tok), pl.BlockSpec((n_seq, tt, JB_ST), tok),
                  pl.BlockSpec((n_seq, SUBLANES, JB_ST), halo), pl.BlockSpec((n_seq, SUBLANES, JB_ST), halo),
                  pl.BlockSpec((1, JB_CH, JB_ST), blk3), pl.BlockSpec((1, JB_CH, JB_ST), blk3),
                  pl.BlockSpec((1, JB_CH, JB_ST), blk3), pl.BlockSpec((1, JB_CH, JB_ST), blk3),
                  pl.BlockSpec((1, JB_CH), row1), pl.BlockSpec((1, 1, JB_ST), blk3), pl.BlockSpec((1, 1, JB_ST), blk3),
                  HBM_SPEC, HBM_SPEC],
        out_specs=(pl.BlockSpec((n_seq, tt, JB_CH), tok),
                   pl.BlockSpec((1, JB_CH, JB_ST), blk3), pl.BlockSpec((1, JB_CH, JB_ST), blk3),
                   pl.BlockSpec((1, JB_CH, JB_ST), blk3), pl.BlockSpec((1, JB_CH, JB_ST), blk3),
                   pl.BlockSpec((JB_ST // STATE, STATE), grp), pl.BlockSpec((JB_ST // STATE, STATE), grp),
                   pl.BlockSpec((1, JB_CH), row1),
                   HBM_SPEC, HBM_SPEC),
        scratch_shapes=[pltpu.VMEM((n_seq, tt, JB_ST), F32), pltpu.VMEM((n_seq, tt, JB_ST), F32),
                        pltpu.VMEM((n_seq, tt, JB_CH), F32), pltpu.VMEM((n_seq, tt, JB_CH), F32),
                        pltpu.VMEM((n_seq, SUBLANES, JB_ST), F32),
                        pltpu.SemaphoreType.DMA((7 * 2,)), pltpu.SemaphoreType.DMA((7 * 2,)),
                        pltpu.SemaphoreType.DMA((2,))],
        compiler_params=_params(2),
    )(dy, u, s_re, s_im, s_re, s_im, bb_re, bb_im, c_re_t, c_imn_t, d_row, ab_re, ab_im, g_out, g_glu)


def _mix(x2, tgt2, y, proj, gf, b_glu, conv8, w_glu_f, w_out_f, seq):
    n = x2.shape[0]
    tm = TOK_TILE
    tiles_per_seq = seq // tm
    rows8 = tm // SUBLANES

    def body(x_ref, t_ref, y_ref, zs_ref, h_ref, bc_ref, cc_ref, zc_ref, hp_ref, ccp_ref,
             gf_ref, bg_ref, cw_ref, wg_ref, wo_ref,
             dh2_ref, dy_ref, dzs_ref, dbc_ref, dzc_ref, dyc_ref,
             dwo_ref, dwg_ref, loss_ref, dgf_ref, dbg_ref, dcw_ref):
        i = pl.program_id(0)

        @pl.when(i == 0)
        def _():
            for r in (dwo_ref, dwg_ref, loss_ref, dgf_ref, dbg_ref, dcw_ref):
                r[...] = jnp.zeros_like(r)

        yv = y_ref[...]
        y1, dgelu = _gelu_and_grad(yv)
        y1b = y1.astype(BF16)
        gate = _sigmoid(_dot(y1b, wg_ref[...]) + bg_ref[...])
        y2 = y1 * gate
        szs, dszs = _silu_and_grad(zs_ref[...])
        yssm = y2 * szs
        hv = h_ref[...]
        ccv = cc_ref[...]
        bcv = bc_ref[...]
        v = ccv * hv
        first = (i % tiles_per_seq) == 0
        vhalo = jnp.where(first, 0.0, ccp_ref[...] * hp_ref[...])
        v1 = _shift_down(v, vhalo, 1)
        v2 = _shift_down(v, vhalo, 2)
        w0 = cw_ref[0:1, :]
        w1 = cw_ref[1:2, :]
        w2 = cw_ref[2:3, :]
        yc = w0 * v2 + w1 * v1 + w2 * v
        szc, dszc = _silu_and_grad(zc_ref[...])
        yconv = (bcv * yc) * szc
        ysb = yssm.astype(BF16)
        ycb = yconv.astype(BF16)
        h2 = x_ref[...] + _dot(ysb, wo_ref[0:SSM_W, :]) + _dot(ycb, wo_ref[SSM_W:, :])
        r2 = lax.rsqrt(jnp.mean(h2 * h2, axis=-1, keepdims=True) + EPS)
        hn = h2 * r2
        gfv = gf_ref[...]
        err = hn * gfv - t_ref[...]
        loss_ref[...] += 0.5 * jnp.sum(jnp.mean(err * err, axis=-1, keepdims=True))
        dout = err * (1.0 / D_MODEL)
        dgf_ref[...] += jnp.sum(dout * hn, axis=0, keepdims=True)
        dn = dout * gfv
        dh2 = r2 * (dn - hn * jnp.mean(dn * hn, axis=-1, keepdims=True))
        dh2_ref[...] = dh2
        dh2b = dh2.astype(BF16)
        dwo_ref[0:SSM_W, :] += _dot_tn(ysb, dh2b)
        dwo_ref[SSM_W:, :] += _dot_tn(ycb, dh2b)
        dyssm = _dot_nt(dh2b, wo_ref[0:SSM_W, :])
        dyconv = _dot_nt(dh2b, wo_ref[SSM_W:, :])
        dy2 = dyssm * szs
        dzs_ref[...] = (dyssm * y2 * dszs).astype(BF16)
        dgp = dy2 * y1 * (gate * (1.0 - gate))
        dgpb = dgp.astype(BF16)
        dy1 = dy2 * gate + _dot_nt(dgpb, wg_ref[...])
        dwg_ref[...] += _dot_tn(y1b, dgpb)
        dbg_ref[...] += jnp.sum(dgp, axis=0, keepdims=True)
        dy_ref[...] = dy1 * dgelu
        dbc_ref[...] = (dyconv * yc * szc).astype(BF16)
        dyc = dyconv * bcv * szc
        dyc_ref[...] = dyc
        dzc_ref[...] = (dyconv * bcv * yc * dszc).astype(BF16)
        dcw_ref[0:1, :] += jnp.sum(dyc * v2, axis=0, keepdims=True)
        dcw_ref[1:2, :] += jnp.sum(dyc * v1, axis=0, keepdims=True)
        dcw_ref[2:3, :] += jnp.sum(dyc * v, axis=0, keepdims=True)

    tile_d = pl.BlockSpec((tm, D_MODEL), lambda i: (i, 0))
    tile_s = pl.BlockSpec((tm, SSM_W), lambda i: (i, 0))
    seg_of = lambda c: pl.BlockSpec((tm, SSM_W), lambda i: (i, c))
    halo_of = lambda c: pl.BlockSpec((SUBLANES, SSM_W), lambda i: (jnp.maximum(i * rows8 - 1, 0), c))
    const = lambda shape: pl.BlockSpec(shape, lambda i: (0,) * len(shape))
    seg = _out((n, SSM_W), F32)
    seg_b = _out((n, SSM_W), BF16)
    return _pcall(
        body, name="mix", grid=(n // tm,),
        out_shape=(_out((n, D_MODEL), F32), seg, seg_b, seg_b, seg_b, seg,
                   _out((D_MODEL, D_MODEL), F32), _out((SSM_W, SSM_W), F32),
                   _out((SUBLANES, LANES), F32), _out((1, D_MODEL), F32),
                   _out((1, SSM_W), F32), _out((SUBLANES, CONV_W), F32)),
        in_specs=[tile_d, tile_d, tile_s, seg_of(SEG_ZS), seg_of(SEG_H), seg_of(SEG_BC), seg_of(SEG_CC), seg_of(SEG_ZC),
                  halo_of(SEG_H), halo_of(SEG_CC),
                  const((1, D_MODEL)), const((1, SSM_W)), const((SUBLANES, CONV_W)),
                  const((SSM_W, SSM_W)), const((D_MODEL, D_MODEL))],
        out_specs=(tile_d, tile_s, tile_s, tile_s, tile_s, tile_s,
                   const((D_MODEL, D_MODEL)), const((SSM_W, SSM_W)), const((SUBLANES, LANES)),
                   const((1, D_MODEL)), const((1, SSM_W)), const((SUBLANES, CONV_W))),
        compiler_params=_params(1),
    )(x2, tgt2, y, proj, proj, proj, proj, proj, proj, proj, gf, b_glu, conv8, w_glu_f, w_out_f)


def _in_bwd(x2, dh2, du, dzs, dyc, proj, dbc, dzc, g1, conv8, w_full, seq):
    n = x2.shape[0]
    tm = TOK_TILE
    n_tiles = n // tm
    tiles_per_seq = seq // tm
    rows8 = tm // SUBLANES
    n_blk8 = n // SUBLANES

    def body(x_ref, dh2_ref, du_ref, dzs_ref, dyc_ref, dycn_ref, h_ref, cc_ref, dbc_ref, dzc_ref,
             g_ref, cw_ref, w_ref, gx_ref, dp_ref, dg_ref):
        i = pl.program_id(0)

        @pl.when(i == 0)
        def _():
            dg_ref[...] = jnp.zeros_like(dg_ref)

        dyc = dyc_ref[...]
        last = (i % tiles_per_seq) == tiles_per_seq - 1
        nhalo = jnp.where(last, 0.0, dycn_ref[...])
        dv = (cw_ref[2:3, :] * dyc + cw_ref[1:2, :] * _shift_up(dyc, nhalo, 1)
              + cw_ref[0:1, :] * _shift_up(dyc, nhalo, 2))
        parts = (du_ref[...], dzs_ref[...], dv * cc_ref[...], dbc_ref[...], dv * h_ref[...], dzc_ref[...])
        dxn = jnp.zeros((tm, D_MODEL), F32)
        for k, p in enumerate(parts):
            pb = p.astype(BF16)
            dp_ref[:, k * SSM_W:(k + 1) * SSM_W] = pb
            dxn = dxn + _dot_nt(pb, w_ref[:, k * SSM_W:(k + 1) * SSM_W])
        x = x_ref[...]
        r = lax.rsqrt(jnp.mean(x * x, axis=-1, keepdims=True) + EPS)
        xh = x * r
        dg_ref[...] += jnp.sum(dxn * xh, axis=0, keepdims=True)
        dn = dxn * g_ref[...]
        gx_ref[...] = dh2_ref[...] + r * (dn - xh * jnp.mean(dn * xh, axis=-1, keepdims=True))

    tile_d = pl.BlockSpec((tm, D_MODEL), lambda i: (i, 0))
    tile_s = pl.BlockSpec((tm, SSM_W), lambda i: (i, 0))
    seg_of = lambda c: pl.BlockSpec((tm, SSM_W), lambda i: (i, c))
    nhalo = pl.BlockSpec((SUBLANES, SSM_W), lambda i: (jnp.minimum((i + 1) * rows8, n_blk8 - 1), 0))
    const = lambda shape: pl.BlockSpec(shape, lambda i: (0,) * len(shape))
    return _pcall(
        body, name="in_bwd", grid=(n_tiles,),
        out_shape=(_out((n, D_MODEL), F32), _out((n, IN_COLS), BF16),
                   _out((SUBLANES, D_MODEL), F32)),
        in_specs=[tile_d, tile_d, tile_s, tile_s, tile_s, nhalo, seg_of(SEG_H), seg_of(SEG_CC), tile_s, tile_s,
                  const((1, D_MODEL)), const((SUBLANES, CONV_W)), const((D_MODEL, IN_COLS))],
        out_specs=(tile_d, pl.BlockSpec((tm, IN_COLS), lambda i: (i, 0)), const((SUBLANES, D_MODEL))),
        compiler_params=_params(1),
    )(x2, dh2, du, dzs, dyc, dyc, proj, proj, dbc, dzc, g1, conv8, w_full)


_HALF_BLOCKS = ((0, 0), (0, 1), (1, 0), (2, 0), (1, 1), (2, 1), (3, 0), (3, 1))


def _dw_in_exchange(chips, xn, dproj, smalls):
    n = xn.shape[0]
    tk = min(1024, n)
    nk = n // tk
    piece = (D_MODEL, COLS_PER_DEV)
    hr = D_MODEL // 2
    n_half = len(_HALF_BLOCKS)
    n_small = len(smalls)
    assert _HALF_BLOCKS[0][1] == 0 and _HALF_BLOCKS[1][1] == 1
    order = jnp.stack([chips[b] for b, _ in _HALF_BLOCKS]
                      + [jnp.int32(t) for _, t in _HALF_BLOCKS]).astype(jnp.int32)

    def body(order_ref, xn_hbm, dp_ref, *refs):
        sm_refs = refs[:n_small]
        own_ref, rchip_ref = refs[n_small:n_small + 2]
        rsm_refs = refs[n_small + 2:2 * n_small + 2]
        (xn_ref, acc, stage, rbuf, kbuf, relay_in, xn_sems, give_send, give_recv, keep_send, keep_recv,
         relay_send, relay_recv, sm_send, sm_recv, sm_loc) = refs[2 * n_small + 2:]
        s = pl.program_id(0)

        def xn_copy(kk, t):
            rows = pl.ds(pl.multiple_of(kk * tk, tk), tk)
            return pltpu.make_async_copy(xn_hbm.at[rows, t * hr:(t + 1) * hr], xn_ref.at[t, rows, :],
                                         xn_sems.at[2 * kk + t])

        @pl.when(s == 0)
        def _():
            for kk in range(nk):
                for t in range(2):
                    xn_copy(kk, t).start()
            xn_copy(0, 0).wait()

        @pl.when(s == 1)
        def _():
            xn_copy(0, 1).wait()

        x, y, c = _mesh_pos()
        sib = (x, y, 1 - c)
        y_nbr, x_nbr = (x, 1 - y, c), (1 - x, y, c)
        gather = _TwoLevelGather(list(sm_refs), [functools.partial(lambda r, dev: r.at[dev], r) for r in rsm_refs],
                                 sm_send, sm_recv, sm_loc)

        def give(h):
            cols = pl.ds(pl.multiple_of((1 - c) * COLS_PER_DEV, LANES), COLS_PER_DEV)
            return pltpu.make_async_remote_copy(src_ref=acc.at[h % 2, :, cols], dst_ref=stage.at[h],
                                                send_sem=give_send.at[h], recv_sem=give_recv.at[h],
                                                device_id=sib, device_id_type=MESH)

        def relay(r):
            return pltpu.make_async_remote_copy(src_ref=rbuf.at[r], dst_ref=relay_in.at[r],
                                                send_sem=relay_send.at[r], recv_sem=relay_recv.at[r],
                                                device_id=(x_nbr, y_nbr)[r], device_id_type=MESH)

        def keep(q):
            return pltpu.make_async_remote_copy(src_ref=kbuf.at[q], dst_ref=rchip_ref.at[q // 2, pl.ds((q % 2) * hr, hr), :],
                                                send_sem=keep_send.at[q], recv_sem=keep_recv.at[q],
                                                device_id=(y_nbr, x_nbr)[q // 2], device_id_type=MESH)

        def chip_sum(h):
            give(h).wait_recv()
            mine = [acc[h % 2, :, cc * COLS_PER_DEV:(cc + 1) * COLS_PER_DEV] for cc in range(2)]
            return jnp.where(c == 0, mine[0], mine[1]) + stage[h]

        @pl.when(s == 0)
        def _():
            gather.start()

        @pl.when(s == 2)
        def _():
            gather.neighbours_landed()

        @pl.when(s == n_half - 2)
        def _():
            gather.diagonal_landed()

        for k in range(2, n_half):
            @pl.when(s == k)
            def _(k=k):
                give(k - 2).wait_send()

        slot = s % 2
        t_half = order_ref[n_half + s]
        acc[slot] = _dot_tn(xn_ref[t_half, pl.ds(0, tk), :], dp_ref[pl.ds(0, tk), :])

        def kstep(kk, carry):
            for t in range(2):
                @pl.when(s == t)
                def _(t=t):
                    xn_copy(kk, t).wait()

            off = pl.multiple_of(kk * tk, tk)
            acc[slot] += _dot_tn(xn_ref[t_half, pl.ds(off, tk), :], dp_ref[pl.ds(off, tk), :])
            return carry

        n_first = max(1, (3 * nk) // 8)
        lax.fori_loop(1, n_first, kstep, 0)
        for k in range(1, n_half):
            @pl.when(s == k)
            def _(k=k):
                h = k - 1
                b, t = _HALF_BLOCKS[h]
                total = chip_sum(h)
                if b == 0:
                    rbuf[t] = total.astype(BF16)
                    relay(t).start()
                elif b < 3:
                    if (b, t) in ((1, 0), (2, 1)):
                        relay(t).wait_recv()
                        total = total + relay_in[t].astype(F32)
                    q = 2 * (b - 1) + t
                    kbuf[q] = total.astype(BF16)
                    keep(q).start()
                else:
                    own_ref[0:hr, :] = total

        lax.fori_loop(n_first, nk, kstep, 0)

        for k in range(n_half):
            @pl.when(s == k)
            def _(k=k):
                give(k).start()

        @pl.when(s == n_half - 1)
        def _():
            own_ref[hr:D_MODEL, :] = chip_sum(n_half - 1)
            give(n_half - 2).wait_send()
            give(n_half - 1).wait_send()
            for r in range(2):
                relay(r).wait_send()
            for q in range(4):
                keep(q).wait()
            gather.finish()

    half_piece = (hr, COLS_PER_DEV)
    grid_spec = pltpu.PrefetchScalarGridSpec(
        num_scalar_prefetch=1, grid=(n_half,),
        in_specs=[HBM_SPEC,
                  pl.BlockSpec((n, COLS_PER_CHIP), lambda s, order: (0, order[s])),
                  *([HBM_SPEC] * n_small)],
        out_specs=(pl.BlockSpec(piece, lambda s, order: (0, 0)), HBM_SPEC, *([HBM_SPEC] * n_small)),
        scratch_shapes=[pltpu.VMEM((2, n, hr), BF16),
                        pltpu.VMEM((2, hr, COLS_PER_CHIP), F32), pltpu.VMEM((n_half,) + half_piece, F32),
                        pltpu.VMEM((2,) + half_piece, BF16), pltpu.VMEM((4,) + half_piece, BF16),
                        pltpu.VMEM((2,) + half_piece, BF16),
                        pltpu.SemaphoreType.DMA((2 * nk,)),
                        pltpu.SemaphoreType.DMA((n_half,)), pltpu.SemaphoreType.DMA((n_half,)),
                        pltpu.SemaphoreType.DMA((4,)), pltpu.SemaphoreType.DMA((4,)),
                        pltpu.SemaphoreType.DMA((2,)), pltpu.SemaphoreType.DMA((2,)),
                        pltpu.SemaphoreType.DMA((7 * n_small,)), pltpu.SemaphoreType.DMA((7 * n_small,)),
                        pltpu.SemaphoreType.DMA((n_small,))])
    return _pcall(
        body, name="dw_in_exchange", grid_spec=grid_spec,
        out_shape=(_out(piece, F32), _out((2,) + piece, BF16),
                   *(_out((N_DEV,) + a.shape, a.dtype) for a in smalls)),
        compiler_params=_params(1),
    )(order, xn, dproj, *smalls)


def _adamw(g, w, m, v):
    m_new = ADAM_B1 * m + (1.0 - ADAM_B1) * g
    v_new = ADAM_B2 * v + (1.0 - ADAM_B2) * (g * g)
    m_hat = m_new / (1.0 - ADAM_B1 ** ADAM_STEP)
    v_hat = v_new / (1.0 - ADAM_B2 ** ADAM_STEP)
    delta = -ADAM_LR * (m_hat / (jnp.sqrt(v_hat) + ADAM_EPS) + ADAM_WD * w)
    return delta, m_new, v_new


def _reduce_adam_w_in(own, rchip, w, m, v):
    rows, cols = w.shape
    row_tile = 256

    def body(o_ref, r_ref, w_ref, m_ref, v_ref, g_ref, d_ref, nm_ref, nv_ref):
        g = o_ref[...]
        for s in range(2):
            g = g + r_ref[s].astype(F32)
        g_ref[...] = g
        d_ref[...], nm_ref[...], nv_ref[...] = _adamw(g, w_ref[...], m_ref[...], v_ref[...])

    tile = pl.BlockSpec((row_tile, cols), lambda i: (i, 0))
    shp = _out((rows, cols), F32)
    return _pcall(
        body, name="reduce_adam_w_in", grid=(rows // row_tile,),
        out_shape=(shp,) * 4,
        in_specs=[tile, pl.BlockSpec((2, row_tile, cols), lambda i: (0, i, 0)), tile, tile, tile],
        out_specs=(tile,) * 4,
        compiler_params=_params(1),
    )(own, rchip, w, m, v)


_SMALL_LEAVES = ("norm_gain", "final_norm_gain", "b_glu", "ssm_a_re", "ssm_a_im", "ssm_log_dt", "ssm_d", "conv_w",
                 "ssm_c_re", "ssm_c_im", "ssm_b_re", "ssm_b_im")


def _reduce_adam_small(r_pack, r_gc, r_gb, wmv, sharded):
    n_leaf = len(_SMALL_LEAVES)
    n_sh = len(sharded)

    def body(*refs):
        rp_ref, rgc_ref, rgb_ref = refs[:3]
        w_refs = refs[3:3 + 3 * n_leaf]
        sh_in = refs[3 + 3 * n_leaf:3 + 3 * n_leaf + 4 * n_sh]
        outs0 = 3 + 3 * n_leaf + 4 * n_sh
        loss_ref = refs[outs0]
        o_refs = refs[outs0 + 1:outs0 + 1 + 4 * n_leaf]
        sh_out = refs[outs0 + 1 + 4 * n_leaf:outs0 + 1 + 4 * n_leaf + 4 * n_sh]
        own_conv = refs[-1]

        def total(ref):
            acc = ref[0].astype(F32)
            for s in range(1, N_DEV):
                acc = acc + ref[s].astype(F32)
            return acc

        for i in range(n_sh):
            r_ref, w_ref, m_ref, v_ref = sh_in[4 * i:4 * i + 4]
            o_g, o_d, o_m, o_v = sh_out[4 * i:4 * i + 4]
            g = total(r_ref)
            o_g[...] = g
            o_d[...], o_m[...], o_v[...] = _adamw(g, w_ref[...], m_ref[...], v_ref[...])

        sp = total(rp_ref)
        sgc = total(rgc_ref)
        sgb = total(rgb_ref)
        loss_ref[...] = sp[ROW_LOSS:ROW_LOSS + 1, 0:1]

        def wide(r):
            return jnp.concatenate([sp[r:r + 1, :], sp[r + 1:r + 2, :]], axis=1)

        s5 = slice(ROW_S5, ROW_S5 + N_GROUPS)
        eye = (lax.broadcasted_iota(jnp.int32, (N_GROUPS, N_GROUPS), 0)
               == lax.broadcasted_iota(jnp.int32, (N_GROUPS, N_GROUPS), 1)).astype(F32)
        d_rows = jnp.broadcast_to(sp[ROW_BGLU_D + 1:ROW_BGLU_D + 2, :], (GROUP, SSM_W))
        own_p = (lax.broadcasted_iota(jnp.int32, (GROUP, SSM_W), 1) % GROUP
                 == lax.broadcasted_iota(jnp.int32, (GROUP, SSM_W), 0))
        of_group = (lax.broadcasted_iota(jnp.int32, (SSM_W, N_GROUPS), 0) // GROUP
                    == lax.broadcasted_iota(jnp.int32, (SSM_W, N_GROUPS), 1)).astype(BF16)
        d_pg = sum(_dot(t, of_group) for t in _split3(jnp.where(own_p, d_rows, 0.0)))
        me = 4 * lax.axis_index("x") + 2 * lax.axis_index("y") + lax.axis_index("c")
        for k in range(N_DEV):
            @pl.when(me == k)
            def _(k=k):
                own_conv[...] = sp[ROW_CONV:ROW_CONV + SUBLANES, k * CONV_COLS_PER_DEV:(k + 1) * CONV_COLS_PER_DEV]
        grads = {
            "norm_gain": wide(ROW_NORM_GAIN),
            "final_norm_gain": wide(ROW_FINAL_GAIN),
            "b_glu": sp[ROW_BGLU_D:ROW_BGLU_D + 1, :],
            "ssm_a_re": sp[s5, LANE_A_RE:LANE_A_RE + STATE],
            "ssm_a_im": sp[s5, LANE_A_IM:LANE_A_IM + STATE],
            "ssm_log_dt": jnp.sum(sp[s5, LANE_LOG_DT:LANE_LOG_DT + 1] * eye, axis=0, keepdims=True),
            "ssm_d": d_pg,
            "ssm_c_re": sgc[:, 0:STATE],
            "ssm_c_im": sgc[:, STATE:2 * STATE],
            "ssm_b_re": sgb[:, 0:STATE],
            "ssm_b_im": sgb[:, STATE:2 * STATE],
        }
        for i, name in enumerate(_SMALL_LEAVES):
            w_ref, m_ref, v_ref = w_refs[3 * i:3 * i + 3]
            o_g, o_d, o_m, o_v = o_refs[4 * i:4 * i + 4]
            if name == "conv_w":
                for k in range(w_ref.shape[0]):
                    g = own_conv[k:k + 1, :]
                    o_g[k] = g
                    o_d[k], o_m[k], o_v[k] = _adamw(g, w_ref[k], m_ref[k], v_ref[k])
                continue
            g = grads[name]
            o_g[...] = g
            o_d[...], o_m[...], o_v[...] = _adamw(g, w_ref[...], m_ref[...], v_ref[...])

    flat_w = [a for name in _SMALL_LEAVES for a in wmv[name]]
    leaf_shapes = [_out(wmv[name][0].shape, F32) for name in _SMALL_LEAVES for _ in range(4)]
    sh_shapes = [_out(entry[1].shape, F32) for entry in sharded for _ in range(4)]
    operands = (r_pack, r_gc, r_gb, *flat_w, *(a for entry in sharded for a in entry))
    out_shape = (_out((1, 1), F32), *leaf_shapes, *sh_shapes)
    outs = _pcall(
        body, name="reduce_adam_small", grid=(1,), out_shape=out_shape,
        in_specs=_whole_specs(operands), out_specs=tuple(_whole_specs(out_shape)),
        scratch_shapes=[pltpu.VMEM((SUBLANES, CONV_COLS_PER_DEV), F32)],
        compiler_params=_params(1),
    )(*operands)
    leaves = {name: outs[1 + 4 * i:5 + 4 * i] for i, name in enumerate(_SMALL_LEAVES)}
    first = 1 + 4 * n_leaf
    return outs[0], leaves, [outs[first + 4 * i:first + 4 * i + 4] for i in range(n_sh)]


def kernel(x, norm_gain, w_in, ssm_a_re, ssm_a_im, ssm_log_dt, ssm_b_re, ssm_b_im, ssm_c_re, ssm_c_im, ssm_d, w_glu, b_glu, conv_w, w_out, final_norm_gain, loss_target, m_norm_gain, m_w_in, m_ssm_a_re, m_ssm_a_im, m_ssm_log_dt, m_ssm_b_re, m_ssm_b_im, m_ssm_c_re, m_ssm_c_im, m_ssm_d, m_w_glu, m_b_glu, m_conv_w, m_w_out, m_final_norm_gain, v_norm_gain, v_w_in, v_ssm_a_re, v_ssm_a_im, v_ssm_log_dt, v_ssm_b_re, v_ssm_b_im, v_ssm_c_re, v_ssm_c_im, v_ssm_d, v_w_glu, v_b_glu, v_conv_w, v_w_out, v_final_norm_gain):
    n_seq, seq, _ = x.shape
    n = n_seq * seq

    gh_p = lambda b4: jnp.transpose(b4, (0, 1, 3, 2)).reshape(N_GROUPS * GROUP, STATE)
    c2 = lambda a: a.reshape(N_GROUPS * GROUP, STATE)
    b_re2, b_im2 = gh_p(ssm_b_re), gh_p(ssm_b_im)
    d_row = ssm_d[0].reshape(1, SSM_W)

    x2 = x.reshape(n, D_MODEL)
    tgt2 = loss_target.reshape(n, D_MODEL)
    mx, my, mc = lax.axis_index("x"), lax.axis_index("y"), lax.axis_index("c")
    chip_ids = [2 * cx + cy for cx, cy in ((mx, my), (1 - mx, my), (mx, 1 - my), (1 - mx, 1 - my))]
    arrival = chip_ids
    xn, proj, w_in_f, s5 = _in_proj(
        jnp.stack(arrival).astype(jnp.int32), x2, norm_gain, w_in[0].astype(BF16),
        (ssm_a_re[0], ssm_a_im[0], ssm_log_dt, b_re2, b_im2, c2(ssm_c_re), c2(ssm_c_im)))
    a_re_x, a_im_x, log_dt_x, ab_re, ab_im, bb_re_m, bb_im_m, c_re_m, c_imn_m = s5
    u3 = proj.reshape(n_seq, seq, IN_COLS)
    conv_p = jnp.pad(conv_w[0], ((0, SUBLANES - 3), (0, LANES - CONV_COLS_PER_DEV)))
    s_re, s_im, y3, w_out_f, w_glu_f, conv_all = _ssm_fwd(
        u3, bb_re_m, bb_im_m, c_re_m, c_imn_m, d_row, ab_re, ab_im,
        w_out[0].astype(BF16), w_glu[0].astype(BF16), conv_p, n_seq, seq)
    conv8 = jnp.transpose(conv_all[:, :, :CONV_COLS_PER_DEV], (1, 0, 2)).reshape(SUBLANES, CONV_W)
    (dh2, dy, dzs, dbc, dzc, dyc, dw_out, dw_glu, loss_t, dgf, dbg, dcw) = _mix(
        x2, tgt2, y3.reshape(n, SSM_W), proj, final_norm_gain.reshape(1, D_MODEL), b_glu, conv8,
        w_glu_f, w_out_f, seq)

    du3, dc_re_d, dc_im_d, dbb_re_d, dbb_im_d, dab_re, dab_im, dd, r_out, r_glu = _ssm_bwd(
        dy.reshape(n_seq, seq, SSM_W), u3, s_re, s_im, bb_re_m, bb_im_m, c_re_m, c_imn_m, d_row, ab_re, ab_im,
        dw_out.reshape(N_DEV, OUT_ROWS_PER_DEV, D_MODEL), dw_glu.reshape(N_DEV, GLU_ROWS_PER_DEV, SSM_W), n_seq, seq)
    du = du3.reshape(n, SSM_W)
    grad_x2, dproj, dg8 = _in_bwd(x2, dh2, du, dzs, dyc, proj, dbc, dzc, norm_gain, conv8, w_in_f, seq)
    pack, gc, gb = _ssm_disc_bwd_pack(
        a_re_x, a_im_x, log_dt_x, b_re2, b_im2, dab_re, dab_im,
        dbb_re_d, dbb_im_d, loss_t, dg8, dgf, dbg, dd, dcw, dc_re_d, dc_im_d)

    own_in, rchip_in, r_pack, r_gc, r_gb = _dw_in_exchange(
        [chip_ids[3], chip_ids[2], chip_ids[1], chip_ids[0]],
        xn, dproj, [pack, gc, gb])

    flat2 = lambda a: a.reshape(a.shape[-2:]) if a.ndim > 2 else a.reshape(1, -1)
    c2 = lambda a: a.reshape(N_GROUPS * GROUP, STATE)
    wmv = dict(norm_gain=(norm_gain, m_norm_gain, v_norm_gain),
               final_norm_gain=tuple(flat2(a) for a in (final_norm_gain, m_final_norm_gain, v_final_norm_gain)),
               b_glu=(b_glu, m_b_glu, v_b_glu),
               ssm_a_re=tuple(flat2(a) for a in (ssm_a_re, m_ssm_a_re, v_ssm_a_re)),
               ssm_a_im=tuple(flat2(a) for a in (ssm_a_im, m_ssm_a_im, v_ssm_a_im)),
               ssm_log_dt=(ssm_log_dt, m_ssm_log_dt, v_ssm_log_dt),
               ssm_d=tuple(jnp.transpose(a, (0, 2, 1)).reshape(GROUP, N_GROUPS) for a in (ssm_d, m_ssm_d, v_ssm_d)),
               conv_w=tuple(jnp.transpose(a, (1, 0, 2)) for a in (conv_w, m_conv_w, v_conv_w)),
               ssm_c_re=tuple(c2(a) for a in (ssm_c_re, m_ssm_c_re, v_ssm_c_re)),
               ssm_c_im=tuple(c2(a) for a in (ssm_c_im, m_ssm_c_im, v_ssm_c_im)),
               ssm_b_re=(b_re2, gh_p(m_ssm_b_re), gh_p(v_ssm_b_re)),
               ssm_b_im=(b_im2, gh_p(m_ssm_b_im), gh_p(v_ssm_b_im)))

    res_in = _reduce_adam_w_in(own_in, rchip_in, w_in[0], m_w_in[0], v_w_in[0])
    loss11, small, (res_out, res_glu) = _reduce_adam_small(
        r_pack, r_gc, r_gb, wmv,
        [(r_out, w_out[0], m_w_out[0], v_w_out[0]), (r_glu, w_glu[0], m_w_glu[0], v_w_glu[0])])
    loss = loss11.reshape(())

    shapes = dict(norm_gain=(1, D_MODEL), ssm_a_re=(1, N_GROUPS, STATE), ssm_a_im=(1, N_GROUPS, STATE),
                  ssm_log_dt=(1, N_GROUPS), ssm_c_re=(1, N_GROUPS, GROUP, STATE), ssm_c_im=(1, N_GROUPS, GROUP, STATE),
                  b_glu=(1, SSM_W), final_norm_gain=(D_MODEL,))
    big = dict(w_in=res_in, w_glu=res_glu, w_out=res_out)

    def leaf(kind, name):
        if name in big:
            return big[name][kind][None]
        if name in ("ssm_b_re", "ssm_b_im"):
            return jnp.transpose(small[name][kind].reshape(1, N_GROUPS, GROUP, STATE), (0, 1, 3, 2))
        if name == "ssm_d":
            return jnp.transpose(small[name][kind].reshape(1, GROUP, N_GROUPS), (0, 2, 1))
        if name == "conv_w":
            return jnp.transpose(small[name][kind], (1, 0, 2))
        return small[name][kind].reshape(shapes[name])

    order = ["norm_gain", "w_in", "ssm_a_re", "ssm_a_im", "ssm_log_dt", "ssm_b_re", "ssm_b_im", "ssm_c_re",
             "ssm_c_im", "ssm_d", "w_glu", "b_glu", "conv_w", "w_out", "final_norm_gain"]
    outs = [loss, grad_x2.reshape(x.shape)]
    for kind in range(4):
        outs += [leaf(kind, name) for name in order]
    return tuple(outs)
```

```python
import functools
import math

import jax
import jax.numpy as jnp
from jax import lax
from jax.experimental import pallas as pl
from jax.experimental.pallas import tpu as pltpu

F32 = jnp.float32
BF16 = jnp.bfloat16

N_DEV = 8
D_MODEL = 1024
SSM_W = 512
CONV_W = 512
N_GROUPS = 32
GROUP = 16
STATE = 64
IN_COLS = 3072
SEG_U, SEG_ZS, SEG_H, SEG_BC, SEG_CC, SEG_ZC = range(6)
COLS_PER_DEV = IN_COLS // N_DEV
N_CHIP = N_DEV // 2
COLS_PER_CHIP = 2 * COLS_PER_DEV
OUT_ROWS_PER_DEV = D_MODEL // N_DEV
GLU_ROWS_PER_DEV = SSM_W // N_DEV
CONV_COLS_PER_DEV = CONV_W // N_DEV
EPS = 1e-6

N_JBLK = 4
JB_CH = SSM_W // N_JBLK
JB_ST = N_GROUPS * STATE // N_JBLK

ADAM_LR = 0.001
ADAM_B1 = 0.9
ADAM_B2 = 0.999
ADAM_EPS = 1e-08
ADAM_WD = 0.01
ADAM_STEP = 10

SUBLANES = 8
LANES = 128
VMEM_LIMIT = 48 * 1024 * 1024
TOK_TILE = 256
IN_TILE = 1024
SCAN_TILE = 1024

MESH = pl.DeviceIdType.MESH
HBM_SPEC = pl.BlockSpec(memory_space=pltpu.HBM)


def _build(body, **kw):
    return pl.pallas_call(body, **kw)


def _pcall(body, **kw):
    def call(*operands):
        pinned = [a if jnp.issubdtype(a.dtype, jnp.integer) else pltpu.with_memory_space_constraint(a, pltpu.HBM)
                  for a in operands]
        return _build(body, **kw)(*pinned)
    return call


def _whole_specs(arrays):
    return [pl.BlockSpec(a.shape, functools.partial(lambda nd, i: (0,) * nd, len(a.shape))) for a in arrays]


def _out(shape, dtype):
    return pltpu.HBM(tuple(shape), dtype)


def _params(n_grid):
    return pltpu.CompilerParams(dimension_semantics=("arbitrary",) * n_grid,
                                vmem_limit_bytes=VMEM_LIMIT)


def _dot(a, b):
    return jnp.dot(a, b, preferred_element_type=F32)


def _dot_nt(a, b):
    return lax.dot_general(a, b, (((1,), (1,)), ((), ())), preferred_element_type=F32)


def _dot_tn(a, b):
    return lax.dot_general(a, b, (((0,), (0,)), ((), ())), preferred_element_type=F32)


def _sigmoid(z):
    return 1.0 / (1.0 + jnp.exp(-z))


_GELU_C = math.sqrt(2.0 / math.pi)


def _gelu_and_grad(y):
    inner = _GELU_C * (y + 0.044715 * (y * y * y))
    t = jnp.tanh(inner)
    g = 0.5 * y * (1.0 + t)
    dg = 0.5 * (1.0 + t) + 0.5 * y * (1.0 - t * t) * (_GELU_C * (1.0 + 3.0 * 0.044715 * (y * y)))
    return g, dg


def _silu_and_grad(z):
    s = _sigmoid(z)
    return z * s, s * (1.0 + z * (1.0 - s))


def _shift_down(v, halo, k):
    rolled = pltpu.roll(v, k, 0)
    row = lax.broadcasted_iota(jnp.int32, v.shape, 0)
    for r in range(k):
        rolled = jnp.where(row == r, halo[SUBLANES - k + r:SUBLANES - k + r + 1, :], rolled)
    return rolled


def _shift_up(v, halo, k):
    n = v.shape[0]
    rolled = pltpu.roll(v, n - k, 0)
    row = lax.broadcasted_iota(jnp.int32, v.shape, 0)
    for r in range(k):
        rolled = jnp.where(row == n - k + r, halo[r:r + 1, :], rolled)
    return rolled


def _mesh_pos():
    return lax.axis_index("x"), lax.axis_index("y"), lax.axis_index("c")


def _direct_copies(srcs_for, out_refs, send_sems, recv_sems, loc_sems):
    x, y, c = _mesh_pos()
    me_id = 4 * x + 2 * y + c
    n_arr = len(out_refs)
    dsts = [r.at[me_id] for r in out_refs]
    own = srcs_for(me_id)
    mine = [pltpu.make_async_copy(own[a], dsts[a], loc_sems.at[a]) for a in range(n_arr)]
    sends = []
    for k in range(1, N_DEV):
        px, py, pc = x ^ ((k >> 2) & 1), y ^ ((k >> 1) & 1), c ^ (k & 1)
        src = srcs_for(4 * px + 2 * py + pc)
        for a in range(n_arr):
            sends.append(pltpu.make_async_remote_copy(
                src_ref=src[a], dst_ref=dsts[a],
                send_sem=send_sems.at[(k - 1) * n_arr + a], recv_sem=recv_sems.at[(k - 1) * n_arr + a],
                device_id=(px, py, pc), device_id_type=MESH))
    return mine, sends


class _TwoLevelGather:
    def __init__(self, srcs, slots, send_sems, recv_sems, loc_sems):
        self.srcs, self.slots, self.n_arr = srcs, slots, len(srcs)
        self.send_sems, self.recv_sems, self.loc_sems = send_sems, recv_sems, loc_sems
        x, y, c = _mesh_pos()
        self.c = c
        self.me, self.sib = (x, y, c), (x, y, 1 - c)
        self.chips = [(1 - x, y), (x, 1 - y), (1 - x, 1 - y)]

    def _copies(self, k, block, to, from_src=False):
        dev = 4 * block[0] + 2 * block[1] + block[2]
        return [pltpu.make_async_remote_copy(
            src_ref=self.srcs[a] if from_src else self.slots[a](dev), dst_ref=self.slots[a](dev),
            send_sem=self.send_sems.at[k * self.n_arr + a], recv_sem=self.recv_sems.at[k * self.n_arr + a],
            device_id=to, device_id_type=MESH) for a in range(self.n_arr)]

    def _local(self):
        dev = 4 * self.me[0] + 2 * self.me[1] + self.me[2]
        return [pltpu.make_async_copy(self.srcs[a], self.slots[a](dev), self.loc_sems.at[a])
                for a in range(self.n_arr)]

    def start(self):
        for cp in self._local() + self._copies(0, self.me, self.sib, True):
            cp.start()
        for j in (0, 1):
            for cp in self._copies(1 + j, self.me, (*self.chips[j], self.c), True):
                cp.start()

    def wait_own(self):
        for cp in self._local():
            cp.wait()

    def wait_sibling(self):
        for cp in self._copies(0, self.sib, self.me):
            cp.wait_recv()

    def wait_and_pass_on(self, j):
        chip = self.chips[j]
        for cp in self._copies(1 + j, (*chip, self.c), self.me):
            cp.wait_recv()
        for cp in self._copies(4 + j, (*chip, self.c), self.sib):
            cp.start()

    def neighbours_landed(self):
        x, y, c = self.me
        self.wait_and_pass_on(0)
        self.wait_and_pass_on(1)
        for cp in self._copies(1 + 2, (x ^ c, y ^ (1 - c), c), (x ^ (1 - c), y ^ c, c)):
            cp.start()

    def diagonal_landed(self):
        self.wait_and_pass_on(2)

    def wait_passed_on(self, j):
        for cp in self._copies(4 + j, (*self.chips[j], 1 - self.c), self.me):
            cp.wait_recv()

    def wait_sends(self):
        for cp in self._copies(0, self.me, self.sib, True):
            cp.wait_send()
        for j, chip in enumerate(self.chips):
            for cp in self._copies(1 + j, self.me, (*chip, self.c), True) + self._copies(4 + j, (*chip, self.c), self.sib):
                cp.wait_send()

    def finish(self):
        self.wait_sibling()
        for j in range(3):
            self.wait_passed_on(j)
        self.wait_sends()
        self.wait_own()


def _disc(a_re, a_im, log_dt, b_re, b_im):
    dt = jnp.exp(log_dt)
    mag = jnp.exp(a_re * dt)
    ab_re = mag * jnp.cos(a_im * dt)
    ab_im = mag * jnp.sin(a_im * dt)
    den = a_re * a_re + a_im * a_im
    p_re = ab_re - 1.0
    p_im = ab_im
    q_re = (p_re * a_re + p_im * a_im) / den
    q_im = (p_im * a_re - p_re * a_im) / den
    bb_re = q_re * b_re - q_im * b_im
    bb_im = q_re * b_im + q_im * b_re
    return ab_re, ab_im, bb_re, bb_im


def _split3(v):
    hi = v.astype(BF16)
    r1 = v - hi.astype(F32)
    mid = r1.astype(BF16)
    lo = (r1 - mid.astype(F32)).astype(BF16)
    return hi, mid, lo


def _select_dot(sel, v):
    return sum(_dot(sel, t) for t in _split3(v))


PACK_ROWS = 72
PACK_W = 512
ROW_FINAL_GAIN, ROW_NORM_GAIN, ROW_BGLU_D, ROW_CONV, ROW_LOSS, ROW_S5 = 0, 8, 16, 24, 32, 40
LANE_A_RE, LANE_A_IM, LANE_LOG_DT = 0, 128, 256


def _ssm_disc_bwd_pack(a_re_x, a_im_x, log_dt_x, b_re, b_im, g_ab_re, g_ab_im, dbb_re_d, dbb_im_d,
                       loss_t, dg8, dgf, dbg, dd, dcw, dc_re_d, dc_im_d):
    rows_gh = N_GROUPS * GROUP

    def body(are, aim, ldt, bre, bim, gabre, gabim, dbbre_ref, dbbim_ref,
             loss_ref, dg8_ref, dgf_ref, dbg_ref, dd_ref, dcw_ref, dcre_ref, dcim_ref,
             p_ref, gc_ref, gb_ref, gbb_re, gbb_im):
        r_g = lax.broadcasted_iota(jnp.int32, (N_GROUPS, rows_gh), 0)
        c_gh = lax.broadcasted_iota(jnp.int32, (N_GROUPS, rows_gh), 1)
        group_sum = (c_gh // GROUP == r_g).astype(BF16)
        r_gh = lax.broadcasted_iota(jnp.int32, (rows_gh, N_GROUPS), 0)
        c_g = lax.broadcasted_iota(jnp.int32, (rows_gh, N_GROUPS), 1)
        first_row = (r_gh == c_g * GROUP).astype(BF16)

        def diag_block(ref, j, gi):
            return ref[j, gi * GROUP:(gi + 1) * GROUP, gi * STATE:(gi + 1) * STATE]

        for j in range(N_JBLK):
            for gi in range(SUBLANES):
                r0 = (j * SUBLANES + gi) * GROUP
                gbb_re[r0:r0 + GROUP, :] = diag_block(dbbre_ref, j, gi)
                gbb_im[r0:r0 + GROUP, :] = diag_block(dbbim_ref, j, gi)
                both = jnp.concatenate([diag_block(dcre_ref, j, gi), -diag_block(dcim_ref, j, gi)], axis=1)
                gc_ref[r0:r0 + GROUP, :] = both.astype(BF16)

        _, vjp = jax.vjp(_disc, are[...], aim[...], ldt[...], bre[...], bim[...])
        d_are, d_aim, d_ldt, d_bre, d_bim = vjp((_select_dot(first_row, gabre[...]), _select_dot(first_row, gabim[...]),
                                                 gbb_re[...], gbb_im[...]))
        gb_ref[...] = jnp.concatenate([d_bre, d_bim], axis=1).astype(BF16)

        p_ref[...] = jnp.zeros_like(p_ref)
        half = D_MODEL // 2
        for r, src in ((ROW_FINAL_GAIN, dgf_ref), (ROW_NORM_GAIN, dg8_ref)):
            p_ref[r:r + 1, :] = src[0:1, 0:half]
            p_ref[r + 1:r + 2, :] = src[0:1, half:D_MODEL]
        p_ref[ROW_BGLU_D:ROW_BGLU_D + 1, :] = dbg_ref[...]
        p_ref[ROW_BGLU_D + 1:ROW_BGLU_D + 2, :] = dd_ref[...]
        p_ref[ROW_CONV:ROW_CONV + SUBLANES, :] = dcw_ref[...]
        p_ref[ROW_LOSS:ROW_LOSS + SUBLANES, 0:LANES] = loss_ref[...]
        s5 = slice(ROW_S5, ROW_S5 + N_GROUPS)
        p_ref[s5, LANE_A_RE:LANE_A_RE + STATE] = _select_dot(group_sum, d_are)
        p_ref[s5, LANE_A_IM:LANE_A_IM + STATE] = _select_dot(group_sum, d_aim)
        p_ref[s5, LANE_LOG_DT:LANE_LOG_DT + LANES] = _select_dot(group_sum, jnp.broadcast_to(d_ldt, (rows_gh, LANES)))

    operands = (a_re_x, a_im_x, log_dt_x, b_re, b_im, g_ab_re, g_ab_im, dbb_re_d, dbb_im_d,
                loss_t, dg8, dgf, dbg, dd, dcw, dc_re_d, dc_im_d)
    out_shape = (_out((PACK_ROWS, PACK_W), F32),
                 _out((rows_gh, 2 * STATE), BF16),
                 _out((rows_gh, 2 * STATE), BF16))
    return _pcall(body, name="ssm_disc_bwd_pack", grid=(1,), out_shape=out_shape,
                  in_specs=_whole_specs(operands), out_specs=tuple(_whole_specs(out_shape)),
                  scratch_shapes=[pltpu.VMEM((rows_gh, STATE), F32), pltpu.VMEM((rows_gh, STATE), F32)],
                  compiler_params=_params(1))(*operands)


def _s5_prepare(are, aim, ldt, bre, bim, cre, cim,
                o_ax_re, o_ax_im, o_ldt_x, o_ab_re, o_ab_im, o_bb_re, o_bb_im, o_c_re, o_c_imn):
    rows_gh = N_GROUPS * GROUP
    rep = (lax.broadcasted_iota(jnp.int32, (rows_gh, N_GROUPS), 0) // GROUP
           == lax.broadcasted_iota(jnp.int32, (rows_gh, N_GROUPS), 1)).astype(BF16)
    eye = (lax.broadcasted_iota(jnp.int32, (N_GROUPS, N_GROUPS), 0)
           == lax.broadcasted_iota(jnp.int32, (N_GROUPS, N_GROUPS), 1)).astype(F32)
    ldt_col = jnp.sum(eye * ldt[...], axis=1, keepdims=True)
    a_re_x = _select_dot(rep, are[...])
    a_im_x = _select_dot(rep, aim[...])
    ldt_x = _select_dot(rep, jnp.broadcast_to(ldt_col, (N_GROUPS, LANES)))[:, 0:1]
    o_ax_re[...] = a_re_x
    o_ax_im[...] = a_im_x
    o_ldt_x[...] = ldt_x
    ab_re, ab_im, bb_re, bb_im = _disc(a_re_x, a_im_x, ldt_x, bre[...], bim[...])
    for j in range(N_JBLK):
        first = [(j * SUBLANES + gi) * GROUP for gi in range(SUBLANES)]
        o_ab_re[j] = jnp.concatenate([ab_re[r:r + 1, :] for r in first], axis=1)
        o_ab_im[j] = jnp.concatenate([ab_im[r:r + 1, :] for r in first], axis=1)
    for o, v in ((o_bb_re, bb_re), (o_bb_im, bb_im), (o_c_re, cre[...]), (o_c_imn, -cim[...])):
        for j in range(N_JBLK):
            for gi in range(SUBLANES):
                r0 = (j * SUBLANES + gi) * GROUP
                parts = [v[r0:r0 + GROUP, :] if k == gi else jnp.zeros((GROUP, STATE), F32) for k in range(SUBLANES)]
                o[j, gi * GROUP:(gi + 1) * GROUP, :] = jnp.concatenate(parts, axis=1).astype(BF16)


def _in_proj(order, x2, g1, w_in_b, s5):
    n = x2.shape[0]
    tm = min(IN_TILE, n)
    n_tiles = n // tm
    n_s5_in = len(s5)
    n_s5_out = 9

    def body(order_ref, x_ref, g_ref, w_ref, *refs):
        s5_in = refs[:n_s5_in]
        xn_ref, proj_ref, wall_ref = refs[n_s5_in:n_s5_in + 3]
        s5_out = refs[n_s5_in + 3:n_s5_in + 3 + n_s5_out]
        xn_scr, wbuf, send_sems, recv_sems, loc_sems, out_sems = refs[n_s5_in + 3 + n_s5_out:]
        k = pl.program_id(0)
        i = pl.program_id(1)

        def slot(dev):
            return wbuf.at[dev // 2, :, pl.ds(pl.multiple_of((dev % 2) * COLS_PER_DEV, LANES), COLS_PER_DEV)]

        gather = _TwoLevelGather([w_ref], [slot], send_sems, recv_sems, loc_sems)

        @pl.when((k == 0) & (i == 0))
        def _():
            gather.start()

        def own_chip():
            gather.wait_own()
            gather.wait_sibling()

        def x_chip():
            gather.neighbours_landed()
            gather.wait_passed_on(0)

        def diag_chip():
            gather.diagonal_landed()
            gather.wait_passed_on(2)

        arrivals = [own_chip, x_chip, functools.partial(gather.wait_passed_on, 1), diag_chip]
        for kk, arrived in enumerate(arrivals):
            @pl.when((k == kk) & (i == 0))
            def _(arrived=arrived):
                arrived()

        rows = pl.ds(pl.multiple_of(i * tm, tm), tm)

        @pl.when(k == 0)
        def _():
            x = x_ref[...]
            r = lax.rsqrt(jnp.mean(x * x, axis=-1, keepdims=True) + EPS)
            xn = ((x * r) * g_ref[...]).astype(BF16)
            xn_scr[rows, :] = xn
            xn_ref[...] = xn

        proj_ref[...] = _dot(xn_scr[rows, :], wbuf[order_ref[k]])

        @pl.when((k == 0) & (i == n_tiles - 1))
        def _():
            _s5_prepare(*s5_in, *s5_out)

        @pl.when((k == N_CHIP - 1) & (i == n_tiles - 1))
        def _():
            gather.wait_sends()
            outs = [pltpu.make_async_copy(wbuf.at[q], wall_ref.at[:, q * COLS_PER_CHIP:(q + 1) * COLS_PER_CHIP],
                                          out_sems.at[q]) for q in range(N_CHIP)]
            for cp in outs:
                cp.start()
            for cp in outs:
                cp.wait()

    tile_once = lambda k, i, order: (jnp.where(k == 0, i, n_tiles - 1), 0)
    whole = lambda shape: pl.BlockSpec(shape, lambda k, i, order: (0,) * len(shape))
    rows_gh = N_GROUPS * GROUP
    s5_out_shapes = ([(rows_gh, STATE), F32], [(rows_gh, STATE), F32], [(rows_gh, 1), F32],
                     [(N_JBLK, 1, JB_ST), F32], [(N_JBLK, 1, JB_ST), F32]) + ([(N_JBLK, JB_CH, JB_ST), BF16],) * 4
    grid_spec = pltpu.PrefetchScalarGridSpec(
        num_scalar_prefetch=1, grid=(N_CHIP, n_tiles),
        in_specs=[pl.BlockSpec((tm, D_MODEL), tile_once),
                  whole((1, D_MODEL)),
                  HBM_SPEC,
                  *(whole(a.shape) for a in s5)],
        out_specs=(pl.BlockSpec((tm, D_MODEL), tile_once),
                   pl.BlockSpec((tm, COLS_PER_CHIP), lambda k, i, order: (i, order[k])),
                   HBM_SPEC,
                   *(whole(shape) for shape, _ in s5_out_shapes)),
        scratch_shapes=[pltpu.VMEM((n, D_MODEL), BF16), pltpu.VMEM((N_CHIP, D_MODEL, COLS_PER_CHIP), BF16),
                        pltpu.SemaphoreType.DMA((7,)), pltpu.SemaphoreType.DMA((7,)), pltpu.SemaphoreType.DMA((1,)),
                        pltpu.SemaphoreType.DMA((N_CHIP,))])
    outs = _pcall(
        body, name="in_proj", grid_spec=grid_spec,
        out_shape=(_out((n, D_MODEL), BF16), _out((n, IN_COLS), F32),
                   _out((D_MODEL, IN_COLS), BF16),
                   *(_out(shape, dt) for shape, dt in s5_out_shapes)),
        compiler_params=_params(2),
    )(order, x2, g1, w_in_b, *s5)
    return outs[0], outs[1], outs[2], outs[3:]


def _cmul(p, q):
    return p[0] * q[0] - p[1] * q[1], p[0] * q[1] + p[1] * q[0]


def _scan_tables(ar, ai, width, reverse):
    pows = [(ar, ai)]
    for _ in range(SUBLANES - 1):
        pows.append(_cmul(pows[-1], (ar, ai)))
    row = lax.broadcasted_iota(jnp.int32, (SUBLANES, width), 0)

    def bc(v):
        return jnp.broadcast_to(v, (SUBLANES, width))

    levels = []
    for k in (1, 2, 4):
        keep = (row <= SUBLANES - 1 - k) if reverse else (row >= k)
        levels.append((jnp.where(keep, bc(pows[k - 1][0]), 0.0), jnp.where(keep, bc(pows[k - 1][1]), 0.0)))
    cre = jnp.zeros((SUBLANES, width), F32)
    cim = jnp.zeros((SUBLANES, width), F32)
    for r in range(SUBLANES):
        e = (SUBLANES - r) if reverse else (r + 1)
        cre = jnp.where(row == r, bc(pows[e - 1][0]), cre)
        cim = jnp.where(row == r, bc(pows[e - 1][1]), cim)
    return levels, (cre, cim)


def _load_chunked(src_ref, b, dst_ref, n_rows):
    n_blk = n_rows // SUBLANES
    for i in range(n_blk):
        dst_ref[b, i * SUBLANES:(i + 1) * SUBLANES, :] = src_ref[b, pl.ds(i, SUBLANES, stride=n_blk), :]


def _store_chunked(val, dst_ref, b, n_rows):
    n_blk = n_rows // SUBLANES
    for i in range(n_blk):
        dst_ref[b, pl.ds(i, SUBLANES, stride=n_blk), :] = val[i * SUBLANES:(i + 1) * SUBLANES, :]


def _chunk_scan(re_ref, im_ref, bs, car_ref, ar, ai, n_rows, reverse, on_block=None):
    width = re_ref.shape[2]
    n_blk = n_rows // SUBLANES
    shape = (SUBLANES, width)
    abr = jnp.broadcast_to(ar, shape)
    abi = jnp.broadcast_to(ai, shape)
    order = list(range(n_blk - 1, -1, -1)) if reverse else list(range(n_blk))

    def blk(ref, b, i):
        return ref[b, i * SUBLANES:(i + 1) * SUBLANES, :]

    def step(state, b, i):
        sr, si = state
        return abr * sr - abi * si + blk(re_ref, b, i), abr * si + abi * sr + blk(im_ref, b, i)

    finals = {b: (blk(re_ref, b, order[0]), blk(im_ref, b, order[0])) for b in bs}
    for i in order[1:]:
        for b in bs:
            finals[b] = step(finals[b], b, i)

    mr, mi = ar, ai
    for _ in range(n_blk.bit_length() - 1):
        mr, mi = _cmul((mr, mi), (mr, mi))
    levels, _ = _scan_tables(mr, mi, width, reverse)
    mbr = jnp.broadcast_to(mr, shape)
    mbi = jnp.broadcast_to(mi, shape)
    row = lax.broadcasted_iota(jnp.int32, shape, 0)
    edge_in = SUBLANES - 1 if reverse else 0
    edge_out = 0 if reverse else SUBLANES - 1
    sh1 = SUBLANES - 1 if reverse else 1
    states = {}
    for b in bs:
        fr, fi = finals[b]
        gr = jnp.where(row == edge_in, jnp.broadcast_to(car_ref[b, 0:1, :], shape), pltpu.roll(fr, sh1, 0))
        gi = jnp.where(row == edge_in, jnp.broadcast_to(car_ref[b, 1:2, :], shape), pltpu.roll(fi, sh1, 0))
        for (lr, li), k in zip(levels, (1, 2, 4)):
            sh = (SUBLANES - k) if reverse else k
            sr = pltpu.roll(gr, sh, 0)
            si = pltpu.roll(gi, sh, 0)
            gr, gi = gr + (lr * sr - li * si), gi + (lr * si + li * sr)
        car_ref[b, 0:1, :] = (fr + (mbr * gr - mbi * gi))[edge_out:edge_out + 1, :]
        car_ref[b, 1:2, :] = (fi + (mbr * gi + mbi * gr))[edge_out:edge_out + 1, :]
        states[b] = (gr, gi)

    for i in order:
        for b in bs:
            states[b] = step(states[b], b, i)
            re_ref[b, i * SUBLANES:(i + 1) * SUBLANES, :] = states[b][0]
            im_ref[b, i * SUBLANES:(i + 1) * SUBLANES, :] = states[b][1]
            if on_block is not None:
                on_block(b, i, *states[b])


def _ssm_fwd(u, bb_re, bb_im, c_re_t, c_imn_t, d_row, ab_re, ab_im, w_out_own, w_glu_own, conv_p, n_seq, seq):
    tt = min(SCAN_TILE, seq)
    nt = seq // tt

    def body(u_ref, bbre, bbim, cre, cimn, d_ref, are, aim, wout_ref, wglu_ref, cw_ref,
             sre_ref, sim_ref, y_ref, oout_ref, oglu_ref, ocw_ref,
             up_ref, car_ref, woutb_ref, wglub_ref, send_sems, recv_sems, loc_sems):
        j = pl.program_id(0)
        t = pl.program_id(1)
        gather = _TwoLevelGather(
            [woutb_ref, wglub_ref, cw_ref],
            [lambda dev: oout_ref.at[pl.ds(pl.multiple_of(dev * OUT_ROWS_PER_DEV, OUT_ROWS_PER_DEV), OUT_ROWS_PER_DEV), :],
             lambda dev: oglu_ref.at[pl.ds(pl.multiple_of(dev * GLU_ROWS_PER_DEV, GLU_ROWS_PER_DEV), GLU_ROWS_PER_DEV), :],
             lambda dev: ocw_ref.at[dev]],
            send_sems, recv_sems, loc_sems)

        @pl.when((j == 0) & (t == 0))
        def _():
            woutb_ref[...] = wout_ref[...].astype(BF16)
            wglub_ref[...] = wglu_ref[...].astype(BF16)
            gather.start()

        @pl.when((j == N_JBLK // 2) & (t == 0))
        def _():
            gather.neighbours_landed()

        @pl.when((j == N_JBLK - 1) & (t == 0))
        def _():
            gather.diagonal_landed()

        @pl.when(t == 0)
        def _():
            car_ref[...] = jnp.zeros_like(car_ref)

        bs = list(range(n_seq))
        for b in bs:
            _load_chunked(u_ref, b, up_ref, tt)
        for b in bs:
            ub = up_ref[b].astype(BF16)
            sre_ref[b] = _dot(ub, bbre[0])
            sim_ref[b] = _dot(ub, bbim[0])
            _chunk_scan(sre_ref, sim_ref, [b], car_ref, are[0], aim[0], tt, reverse=False)
        for b in bs:
            yp = (_dot_nt(sre_ref[b].astype(BF16), cre[0]) + _dot_nt(sim_ref[b].astype(BF16), cimn[0])
                  + d_ref[...] * up_ref[b])
            _store_chunked(yp, y_ref, b, tt)

        @pl.when((j == N_JBLK - 1) & (t == nt - 1))
        def _():
            gather.finish()

    tok = lambda j, t: (0, t, j)
    blk3 = lambda j, t: (j, 0, 0)
    row = lambda j, t: (0, j)
    whole = lambda j, t: (0, 0)
    st = _out((n_seq, seq, N_JBLK * JB_ST), F32)
    n_arr = 3
    return _pcall(
        body, name="ssm_fwd", grid=(N_JBLK, nt),
        out_shape=(st, st, _out((n_seq, seq, SSM_W), F32),
                   _out((D_MODEL, D_MODEL), BF16), _out((SSM_W, SSM_W), BF16),
                   _out((N_DEV, SUBLANES, LANES), F32)),
        in_specs=[pl.BlockSpec((n_seq, tt, JB_CH), tok),
                  pl.BlockSpec((1, JB_CH, JB_ST), blk3), pl.BlockSpec((1, JB_CH, JB_ST), blk3),
                  pl.BlockSpec((1, JB_CH, JB_ST), blk3), pl.BlockSpec((1, JB_CH, JB_ST), blk3),
                  pl.BlockSpec((1, JB_CH), row), pl.BlockSpec((1, 1, JB_ST), blk3), pl.BlockSpec((1, 1, JB_ST), blk3),
                  pl.BlockSpec(w_out_own.shape, whole), pl.BlockSpec(w_glu_own.shape, whole), HBM_SPEC],
        out_specs=(pl.BlockSpec((n_seq, tt, JB_ST), tok), pl.BlockSpec((n_seq, tt, JB_ST), tok),
                   pl.BlockSpec((n_seq, tt, JB_CH), tok), HBM_SPEC, HBM_SPEC, HBM_SPEC),
        scratch_shapes=[pltpu.VMEM((n_seq, tt, JB_CH), F32), pltpu.VMEM((n_seq, SUBLANES, JB_ST), F32),
                        pltpu.VMEM(w_out_own.shape, BF16), pltpu.VMEM(w_glu_own.shape, BF16),
                        pltpu.SemaphoreType.DMA((7 * n_arr,)), pltpu.SemaphoreType.DMA((7 * n_arr,)),
                        pltpu.SemaphoreType.DMA((n_arr,))],
        compiler_params=_params(2),
    )(u, bb_re, bb_im, c_re_t, c_imn_t, d_row, ab_re, ab_im, w_out_own, w_glu_own, conv_p)


def _ssm_bwd(dy, u, s_re, s_im, bb_re, bb_im, c_re_t, c_imn_t, d_row, ab_re, ab_im, g_out, g_glu, n_seq, seq):
    tt = min(SCAN_TILE, seq)
    nt = seq // tt
    rows8 = tt // SUBLANES

    def body(dy_ref, u_ref, sre_ref, sim_ref, pre_ref, pim_ref, bbre, bbim, cre, cimn, d_ref, are, aim,
             gout_ref, gglu_ref,
             du_ref, dcre_ref, dcim_ref, dbbre_ref, dbbim_ref, dare_ref, daim_ref, dd_ref, rout_ref, rglu_ref,
             lre_ref, lim_ref, dyp_ref, up_ref, car_ref, da_ref, send_sems, recv_sems, loc_sems):
        j = pl.program_id(0)
        tr = pl.program_id(1)

        def exchange():
            return _direct_copies(lambda pid: [gout_ref.at[pid], gglu_ref.at[pid]], [rout_ref, rglu_ref],
                                  send_sems, recv_sems, loc_sems)

        @pl.when((j == 0) & (tr == 0))
        def _():
            mine, sends = exchange()
            for cp in mine + sends:
                cp.start()

        @pl.when(tr == 0)
        def _():
            car_ref[...] = jnp.zeros_like(car_ref)
            for r in (dcre_ref, dcim_ref, dbbre_ref, dbbim_ref, da_ref, dd_ref):
                r[...] = jnp.zeros_like(r)

        first = tr == nt - 1
        row = lax.broadcasted_iota(jnp.int32, (SUBLANES, JB_ST), 0)
        n_blk = tt // SUBLANES
        bs = list(range(n_seq))
        for b in bs:
            _load_chunked(dy_ref, b, dyp_ref, tt)
            _load_chunked(u_ref, b, up_ref, tt)
        for b in bs:
            dyb = dyp_ref[b].astype(BF16)
            lre_ref[b] = _dot(dyb, cre[0])
            lim_ref[b] = _dot(dyb, cimn[0])
        acc = {b: [jnp.zeros((SUBLANES, JB_ST), F32), jnp.zeros((SUBLANES, JB_ST), F32)] for b in bs}

        def on_block(b, i, lr, li):
            if i > 0:
                spr = sre_ref[b, (i - 1) * SUBLANES:i * SUBLANES, :]
                spi = sim_ref[b, (i - 1) * SUBLANES:i * SUBLANES, :]
            else:
                hr = jnp.where(first, 0.0, pre_ref[b, SUBLANES - 1:SUBLANES, :])
                hi = jnp.where(first, 0.0, pim_ref[b, SUBLANES - 1:SUBLANES, :])
                last_r = sre_ref[b, (n_blk - 1) * SUBLANES:n_blk * SUBLANES, :]
                last_i = sim_ref[b, (n_blk - 1) * SUBLANES:n_blk * SUBLANES, :]
                spr = jnp.where(row == 0, jnp.broadcast_to(hr, row.shape), pltpu.roll(last_r, 1, 0))
                spi = jnp.where(row == 0, jnp.broadcast_to(hi, row.shape), pltpu.roll(last_i, 1, 0))
            acc[b][0] = acc[b][0] + (lr * spr + li * spi)
            acc[b][1] = acc[b][1] + (li * spr - lr * spi)

        _chunk_scan(lre_ref, lim_ref, bs, car_ref, are[0], -aim[0], tt, reverse=True, on_block=on_block)
        for b in bs:
            da_ref[0] += jnp.sum(acc[b][0], axis=0, keepdims=True)
            da_ref[1] += jnp.sum(acc[b][1], axis=0, keepdims=True)

        @pl.when(tr == nt - 1)
        def _():
            def by_group(r):
                return jnp.concatenate([r[:, g * STATE:(g + 1) * STATE] for g in range(JB_ST // STATE)], axis=0)

            dare_ref[...] = by_group(da_ref[0])
            daim_ref[...] = by_group(da_ref[1])

        for b in bs:
            dyp = dyp_ref[b]
            up = up_ref[b]
            dyb = dyp.astype(BF16)
            ub = up.astype(BF16)
            lrb = lre_ref[b].astype(BF16)
            lib = lim_ref[b].astype(BF16)
            dup = d_ref[...] * dyp + _dot_nt(lrb, bbre[0]) + _dot_nt(lib, bbim[0])
            _store_chunked(dup, du_ref, b, tt)
            dbbre_ref[0] += _dot_tn(ub, lrb)
            dbbim_ref[0] += _dot_tn(ub, lib)
            dcre_ref[0] += _dot_tn(dyb, sre_ref[b].astype(BF16))
            dcim_ref[0] += _dot_tn(dyb, sim_ref[b].astype(BF16))
            dd_ref[...] += jnp.sum(dyp * up, axis=0, keepdims=True)

        @pl.when((j == N_JBLK - 1) & (tr == nt - 1))
        def _():
            mine, sends = exchange()
            for cp in sends + mine:
                cp.wait()

    tok = lambda j, t: (0, nt - 1 - t, j)
    halo = lambda j, t: (0, jnp.maximum((nt - 1 - t) * rows8 - 1, 0), j)
    blk3 = lambda j, t: (j, 0, 0)
    row1 = lambda j, t: (0, j)
    grp = lambda j, t: (j, 0)
    acc_shape = _out((N_JBLK, JB_CH, JB_ST), F32)
    return _pcall(
        body, name="ssm_bwd", grid=(N_JBLK, nt),
        out_shape=(_out((n_seq, seq, SSM_W), F32), acc_shape, acc_shape, acc_shape, acc_shape,
                   _out((N_GROUPS, STATE), F32), _out((N_GROUPS, STATE), F32),
                   _out((1, SSM_W), F32),
                   _out((N_DEV,) + g_out.shape[1:], F32),
                   _out((N_DEV,) + g_glu.shape[1:], F32)),
        in_specs=[pl.BlockSpec((n_seq, tt, JB_CH), tok), pl.BlockSpec((n_seq, tt, JB_CH), tok),
                  pl.BlockSpec((n_seq, tt, JB_ST), tok), pl.BlockSpec((n_seq, tt, JB_ST), tok),
                  pl.BlockSpec((n_seq, SUBLANES, JB_ST), halo), pl.BlockSpec((n_seq, SUBLANES, JB_ST), halo),
                  pl.BlockSpec((1, JB_CH, JB_ST), blk3), pl.BlockSpec((1, JB_CH, JB_ST), blk3),
                  pl.BlockSpec((1, JB_CH, JB_ST), blk3), pl.BlockSpec((1, JB_CH, JB_ST), blk3),
                  pl.BlockSpec((1, JB_CH), row1), pl.BlockSpec((1, 1, JB_ST), blk3), pl.BlockSpec((1, 1, JB_ST), blk3),
                  HBM_SPEC, HBM_SPEC],
        out_specs=(pl.BlockSpec((n_seq, tt, JB_CH), tok),
                   pl.BlockSpec((1, JB_CH, JB_ST), blk3), pl.BlockSpec((1, JB_CH, JB_ST), blk3),
                   pl.BlockSpec((1, JB_CH, JB_ST), blk3), pl.BlockSpec((1, JB_CH, JB_ST), blk3),
                   pl.BlockSpec((JB_ST // STATE, STATE), grp), pl.BlockSpec((JB_ST // STATE, STATE), grp),
                   pl.BlockSpec((1, JB_CH), row1),
                   HBM_SPEC, HBM_SPEC),
        scratch_shapes=[pltpu.VMEM((n_seq, tt, JB_ST), F32), pltpu.VMEM((n_seq, tt, JB_ST), F32),
                        pltpu.VMEM((n_seq, tt, JB_CH), F32), pltpu.VMEM((n_seq, tt, JB_CH), F32),
                        pltpu.VMEM((n_seq, SUBLANES, JB_ST), F32), pltpu.VMEM((2, 1, JB_ST), F32),
                        pltpu.SemaphoreType.DMA((7 * 2,)), pltpu.SemaphoreType.DMA((7 * 2,)),
                        pltpu.SemaphoreType.DMA((2,))],
        compiler_params=_params(2),
    )(dy, u, s_re, s_im, s_re, s_im, bb_re, bb_im, c_re_t, c_imn_t, d_row, ab_re, ab_im, g_out, g_glu)


def _mix(x2, tgt2, y, proj, gf, b_glu, conv8, w_glu_f, w_out_f, seq):
    n = x2.shape[0]
    tm = TOK_TILE
    tiles_per_seq = seq // tm
    rows8 = tm // SUBLANES

    def body(x_ref, t_ref, y_ref, zs_ref, h_ref, bc_ref, cc_ref, zc_ref, hp_ref, ccp_ref,
             gf_ref, bg_ref, cw_ref, wg_ref, wo_ref,
             dh2_ref, dy_ref, dzs_ref, dbc_ref, dzc_ref, dyc_ref,
             dwo_ref, dwg_ref, loss_ref, dgf_ref, dbg_ref, dcw_ref):
        i = pl.program_id(0)

        @pl.when(i == 0)
        def _():
            for r in (dwo_ref, dwg_ref, loss_ref, dgf_ref, dbg_ref, dcw_ref):
                r[...] = jnp.zeros_like(r)

        yv = y_ref[...]
        y1, dgelu = _gelu_and_grad(yv)
        y1b = y1.astype(BF16)
        gate = _sigmoid(_dot(y1b, wg_ref[...]) + bg_ref[...])
        y2 = y1 * gate
        szs, dszs = _silu_and_grad(zs_ref[...])
        yssm = y2 * szs
        hv = h_ref[...]
        ccv = cc_ref[...]
        bcv = bc_ref[...]
        v = ccv * hv
        first = (i % tiles_per_seq) == 0
        vhalo = jnp.where(first, 0.0, ccp_ref[...] * hp_ref[...])
        v1 = _shift_down(v, vhalo, 1)
        v2 = _shift_down(v, vhalo, 2)
        w0 = cw_ref[0:1, :]
        w1 = cw_ref[1:2, :]
        w2 = cw_ref[2:3, :]
        yc = w0 * v2 + w1 * v1 + w2 * v
        szc, dszc = _silu_and_grad(zc_ref[...])
        yconv = (bcv * yc) * szc
        ysb = yssm.astype(BF16)
        ycb = yconv.astype(BF16)
        h2 = x_ref[...] + _dot(ysb, wo_ref[0:SSM_W, :]) + _dot(ycb, wo_ref[SSM_W:, :])
        r2 = lax.rsqrt(jnp.mean(h2 * h2, axis=-1, keepdims=True) + EPS)
        hn = h2 * r2
        gfv = gf_ref[...]
        err = hn * gfv - t_ref[...]
        loss_ref[...] += 0.5 * jnp.sum(jnp.mean(err * err, axis=-1, keepdims=True))
        dout = err * (1.0 / D_MODEL)
        dgf_ref[...] += jnp.sum(dout * hn, axis=0, keepdims=True)
        dn = dout * gfv
        dh2 = r2 * (dn - hn * jnp.mean(dn * hn, axis=-1, keepdims=True))
        dh2_ref[...] = dh2
        dh2b = dh2.astype(BF16)
        dwo_ref[0:SSM_W, :] += _dot_tn(ysb, dh2b)
        dwo_ref[SSM_W:, :] += _dot_tn(ycb, dh2b)
        dyssm = _dot_nt(dh2b, wo_ref[0:SSM_W, :])
        dyconv = _dot_nt(dh2b, wo_ref[SSM_W:, :])
        dy2 = dyssm * szs
        dzs_ref[...] = (dyssm * y2 * dszs).astype(BF16)
        dgp = dy2 * y1 * (gate * (1.0 - gate))
        dgpb = dgp.astype(BF16)
        dy1 = dy2 * gate + _dot_nt(dgpb, wg_ref[...])
        dwg_ref[...] += _dot_tn(y1b, dgpb)
        dbg_ref[...] += jnp.sum(dgp, axis=0, keepdims=True)
        dy_ref[...] = dy1 * dgelu
        dbc_ref[...] = (dyconv * yc * szc).astype(BF16)
        dyc = dyconv * bcv * szc
        dyc_ref[...] = dyc
        dzc_ref[...] = (dyconv * bcv * yc * dszc).astype(BF16)
        dcw_ref[0:1, :] += jnp.sum(dyc * v2, axis=0, keepdims=True)
        dcw_ref[1:2, :] += jnp.sum(dyc * v1, axis=0, keepdims=True)
        dcw_ref[2:3, :] += jnp.sum(dyc * v, axis=0, keepdims=True)

    tile_d = pl.BlockSpec((tm, D_MODEL), lambda i: (i, 0))
    tile_s = pl.BlockSpec((tm, SSM_W), lambda i: (i, 0))
    seg_of = lambda c: pl.BlockSpec((tm, SSM_W), lambda i: (i, c))
    halo_of = lambda c: pl.BlockSpec((SUBLANES, SSM_W), lambda i: (jnp.maximum(i * rows8 - 1, 0), c))
    const = lambda shape: pl.BlockSpec(shape, lambda i: (0,) * len(shape))
    seg = _out((n, SSM_W), F32)
    seg_b = _out((n, SSM_W), BF16)
    return _pcall(
        body, name="mix", grid=(n // tm,),
        out_shape=(_out((n, D_MODEL), F32), seg, seg_b, seg_b, seg_b, seg,
                   _out((D_MODEL, D_MODEL), F32), _out((SSM_W, SSM_W), F32),
                   _out((SUBLANES, LANES), F32), _out((1, D_MODEL), F32),
                   _out((1, SSM_W), F32), _out((SUBLANES, CONV_W), F32)),
        in_specs=[tile_d, tile_d, tile_s, seg_of(SEG_ZS), seg_of(SEG_H), seg_of(SEG_BC), seg_of(SEG_CC), seg_of(SEG_ZC),
                  halo_of(SEG_H), halo_of(SEG_CC),
                  const((1, D_MODEL)), const((1, SSM_W)), const((SUBLANES, CONV_W)),
                  const((SSM_W, SSM_W)), const((D_MODEL, D_MODEL))],
        out_specs=(tile_d, tile_s, tile_s, tile_s, tile_s, tile_s,
                   const((D_MODEL, D_MODEL)), const((SSM_W, SSM_W)), const((SUBLANES, LANES)),
                   const((1, D_MODEL)), const((1, SSM_W)), const((SUBLANES, CONV_W))),
        compiler_params=_params(1),
    )(x2, tgt2, y, proj, proj, proj, proj, proj, proj, proj, gf, b_glu, conv8, w_glu_f, w_out_f)


def _in_bwd(x2, dh2, du, dzs, dyc, proj, dbc, dzc, g1, conv8, w_full, seq):
    n = x2.shape[0]
    tm = TOK_TILE
    n_tiles = n // tm
    tiles_per_seq = seq // tm
    rows8 = tm // SUBLANES
    n_blk8 = n // SUBLANES

    def body(x_ref, dh2_ref, du_ref, dzs_ref, dyc_ref, dycn_ref, h_ref, cc_ref, dbc_ref, dzc_ref,
             g_ref, cw_ref, w_ref, gx_ref, dp_ref, dg_ref):
        i = pl.program_id(0)

        @pl.when(i == 0)
        def _():
            dg_ref[...] = jnp.zeros_like(dg_ref)

        dyc = dyc_ref[...]
        last = (i % tiles_per_seq) == tiles_per_seq - 1
        nhalo = jnp.where(last, 0.0, dycn_ref[...])
        dv = (cw_ref[2:3, :] * dyc + cw_ref[1:2, :] * _shift_up(dyc, nhalo, 1)
              + cw_ref[0:1, :] * _shift_up(dyc, nhalo, 2))
        parts = (du_ref[...], dzs_ref[...], dv * cc_ref[...], dbc_ref[...], dv * h_ref[...], dzc_ref[...])
        dxn = jnp.zeros((tm, D_MODEL), F32)
        for k, p in enumerate(parts):
            pb = p.astype(BF16)
            dp_ref[:, k * SSM_W:(k + 1) * SSM_W] = pb
            dxn = dxn + _dot_nt(pb, w_ref[:, k * SSM_W:(k + 1) * SSM_W])
        x = x_ref[...]
        r = lax.rsqrt(jnp.mean(x * x, axis=-1, keepdims=True) + EPS)
        xh = x * r
        dg_ref[...] += jnp.sum(dxn * xh, axis=0, keepdims=True)
        dn = dxn * g_ref[...]
        gx_ref[...] = dh2_ref[...] + r * (dn - xh * jnp.mean(dn * xh, axis=-1, keepdims=True))

    tile_d = pl.BlockSpec((tm, D_MODEL), lambda i: (i, 0))
    tile_s = pl.BlockSpec((tm, SSM_W), lambda i: (i, 0))
    seg_of = lambda c: pl.BlockSpec((tm, SSM_W), lambda i: (i, c))
    nhalo = pl.BlockSpec((SUBLANES, SSM_W), lambda i: (jnp.minimum((i + 1) * rows8, n_blk8 - 1), 0))
    const = lambda shape: pl.BlockSpec(shape, lambda i: (0,) * len(shape))
    return _pcall(
        body, name="in_bwd", grid=(n_tiles,),
        out_shape=(_out((n, D_MODEL), F32), _out((n, IN_COLS), BF16),
                   _out((SUBLANES, D_MODEL), F32)),
        in_specs=[tile_d, tile_d, tile_s, tile_s, tile_s, nhalo, seg_of(SEG_H), seg_of(SEG_CC), tile_s, tile_s,
                  const((1, D_MODEL)), const((SUBLANES, CONV_W)), const((D_MODEL, IN_COLS))],
        out_specs=(tile_d, pl.BlockSpec((tm, IN_COLS), lambda i: (i, 0)), const((SUBLANES, D_MODEL))),
        compiler_params=_params(1),
    )(x2, dh2, du, dzs, dyc, dyc, proj, proj, dbc, dzc, g1, conv8, w_full)


_HALF_BLOCKS = ((0, 0), (0, 1), (1, 0), (2, 0), (1, 1), (2, 1), (3, 0), (3, 1))


def _dw_in_exchange(chips, xn, dproj, smalls):
    n = xn.shape[0]
    tk = min(1024, n)
    nk = n // tk
    piece = (D_MODEL, COLS_PER_DEV)
    hr = D_MODEL // 2
    n_half = len(_HALF_BLOCKS)
    n_small = len(smalls)
    assert _HALF_BLOCKS[0][1] == 0 and _HALF_BLOCKS[1][1] == 1
    order = jnp.stack([chips[b] for b, _ in _HALF_BLOCKS]
                      + [jnp.int32(t) for _, t in _HALF_BLOCKS]).astype(jnp.int32)

    def body(order_ref, xn_hbm, dp_ref, *refs):
        sm_refs = refs[:n_small]
        own_ref, rchip_ref = refs[n_small:n_small + 2]
        rsm_refs = refs[n_small + 2:2 * n_small + 2]
        (xn_ref, acc, stage, rbuf, kbuf, relay_in, xn_sems, give_send, give_recv, keep_send, keep_recv,
         relay_send, relay_recv, sm_send, sm_recv, sm_loc) = refs[2 * n_small + 2:]
        s = pl.program_id(0)

        def xn_copy(kk, t):
            rows = pl.ds(pl.multiple_of(kk * tk, tk), tk)
            return pltpu.make_async_copy(xn_hbm.at[rows, t * hr:(t + 1) * hr], xn_ref.at[t, rows, :],
                                         xn_sems.at[2 * kk + t])

        @pl.when(s == 0)
        def _():
            for kk in range(nk):
                for t in range(2):
                    xn_copy(kk, t).start()
            xn_copy(0, 0).wait()

        @pl.when(s == 1)
        def _():
            xn_copy(0, 1).wait()

        x, y, c = _mesh_pos()
        sib = (x, y, 1 - c)
        y_nbr, x_nbr = (x, 1 - y, c), (1 - x, y, c)
        gather = _TwoLevelGather(list(sm_refs), [functools.partial(lambda r, dev: r.at[dev], r) for r in rsm_refs],
                                 sm_send, sm_recv, sm_loc)

        def give(h):
            cols = pl.ds(pl.multiple_of((1 - c) * COLS_PER_DEV, LANES), COLS_PER_DEV)
            return pltpu.make_async_remote_copy(src_ref=acc.at[h % 2, :, cols], dst_ref=stage.at[h],
                                                send_sem=give_send.at[h], recv_sem=give_recv.at[h],
                                                device_id=sib, device_id_type=MESH)

        def relay(r):
            return pltpu.make_async_remote_copy(src_ref=rbuf.at[r], dst_ref=relay_in.at[r],
                                                send_sem=relay_send.at[r], recv_sem=relay_recv.at[r],
                                                device_id=(x_nbr, y_nbr)[r], device_id_type=MESH)

        def keep(q):
            return pltpu.make_async_remote_copy(src_ref=kbuf.at[q], dst_ref=rchip_ref.at[q // 2, pl.ds((q % 2) * hr, hr), :],
                                                send_sem=keep_send.at[q], recv_sem=keep_recv.at[q],
                                                device_id=(y_nbr, x_nbr)[q // 2], device_id_type=MESH)

        def chip_sum(h):
            give(h).wait_recv()
            mine = [acc[h % 2, :, cc * COLS_PER_DEV:(cc + 1) * COLS_PER_DEV] for cc in range(2)]
            return jnp.where(c == 0, mine[0], mine[1]) + stage[h]

        @pl.when(s == 0)
        def _():
            gather.start()

        @pl.when(s == 2)
        def _():
            gather.neighbours_landed()

        @pl.when(s == n_half - 2)
        def _():
            gather.diagonal_landed()

        for k in range(2, n_half):
            @pl.when(s == k)
            def _(k=k):
                give(k - 2).wait_send()

        slot = s % 2
        t_half = order_ref[n_half + s]
        acc[slot] = _dot_tn(xn_ref[t_half, pl.ds(0, tk), :], dp_ref[pl.ds(0, tk), :])

        def kstep(kk, carry):
            for t in range(2):
                @pl.when(s == t)
                def _(t=t):
                    xn_copy(kk, t).wait()

            off = pl.multiple_of(kk * tk, tk)
            acc[slot] += _dot_tn(xn_ref[t_half, pl.ds(off, tk), :], dp_ref[pl.ds(off, tk), :])
            return carry

        n_first = max(1, nk // 2)
        lax.fori_loop(1, n_first, kstep, 0)
        for k in range(1, n_half):
            @pl.when(s == k)
            def _(k=k):
                h = k - 1
                b, t = _HALF_BLOCKS[h]
                total = chip_sum(h)
                if b == 0:
                    rbuf[t] = total.astype(BF16)
                    relay(t).start()
                elif b < 3:
                    if (b, t) in ((1, 0), (2, 1)):
                        relay(t).wait_recv()
                        total = total + relay_in[t].astype(F32)
                    q = 2 * (b - 1) + t
                    kbuf[q] = total.astype(BF16)
                    keep(q).start()
                else:
                    own_ref[0:hr, :] = total

        lax.fori_loop(n_first, nk, kstep, 0)

        for k in range(n_half):
            @pl.when(s == k)
            def _(k=k):
                give(k).start()

        @pl.when(s == n_half - 1)
        def _():
            own_ref[hr:D_MODEL, :] = chip_sum(n_half - 1)
            give(n_half - 2).wait_send()
            give(n_half - 1).wait_send()
            for r in range(2):
                relay(r).wait_send()
            for q in range(4):
                keep(q).wait()
            gather.finish()

    half_piece = (hr, COLS_PER_DEV)
    grid_spec = pltpu.PrefetchScalarGridSpec(
        num_scalar_prefetch=1, grid=(n_half,),
        in_specs=[HBM_SPEC,
                  pl.BlockSpec((n, COLS_PER_CHIP), lambda s, order: (0, order[s])),
                  *([HBM_SPEC] * n_small)],
        out_specs=(pl.BlockSpec(piece, lambda s, order: (0, 0)), HBM_SPEC, *([HBM_SPEC] * n_small)),
        scratch_shapes=[pltpu.VMEM((2, n, hr), BF16),
                        pltpu.VMEM((2, hr, COLS_PER_CHIP), F32), pltpu.VMEM((n_half,) + half_piece, F32),
                        pltpu.VMEM((2,) + half_piece, BF16), pltpu.VMEM((4,) + half_piece, BF16),
                        pltpu.VMEM((2,) + half_piece, BF16),
                        pltpu.SemaphoreType.DMA((2 * nk,)),
                        pltpu.SemaphoreType.DMA((n_half,)), pltpu.SemaphoreType.DMA((n_half,)),
                        pltpu.SemaphoreType.DMA((4,)), pltpu.SemaphoreType.DMA((4,)),
                        pltpu.SemaphoreType.DMA((2,)), pltpu.SemaphoreType.DMA((2,)),
                        pltpu.SemaphoreType.DMA((7 * n_small,)), pltpu.SemaphoreType.DMA((7 * n_small,)),
                        pltpu.SemaphoreType.DMA((n_small,))])
    return _pcall(
        body, name="dw_in_exchange", grid_spec=grid_spec,
        out_shape=(_out(piece, F32), _out((2,) + piece, BF16),
                   *(_out((N_DEV,) + a.shape, a.dtype) for a in smalls)),
        compiler_params=_params(1),
    )(order, xn, dproj, *smalls)


def _adamw(g, w, m, v):
    m_new = ADAM_B1 * m + (1.0 - ADAM_B1) * g
    v_new = ADAM_B2 * v + (1.0 - ADAM_B2) * (g * g)
    m_hat = m_new / (1.0 - ADAM_B1 ** ADAM_STEP)
    v_hat = v_new / (1.0 - ADAM_B2 ** ADAM_STEP)
    delta = -ADAM_LR * (m_hat / (jnp.sqrt(v_hat) + ADAM_EPS) + ADAM_WD * w)
    return delta, m_new, v_new


def _reduce_adam_w_in(own, rchip, w, m, v):
    rows, cols = w.shape
    row_tile = 256

    def body(o_ref, r_ref, w_ref, m_ref, v_ref, g_ref, d_ref, nm_ref, nv_ref):
        g = o_ref[...]
        for s in range(2):
            g = g + r_ref[s].astype(F32)
        g_ref[...] = g
        d_ref[...], nm_ref[...], nv_ref[...] = _adamw(g, w_ref[...], m_ref[...], v_ref[...])

    tile = pl.BlockSpec((row_tile, cols), lambda i: (i, 0))
    shp = _out((rows, cols), F32)
    return _pcall(
        body, name="reduce_adam_w_in", grid=(rows // row_tile,),
        out_shape=(shp,) * 4,
        in_specs=[tile, pl.BlockSpec((2, row_tile, cols), lambda i: (0, i, 0)), tile, tile, tile],
        out_specs=(tile,) * 4,
        compiler_params=_params(1),
    )(own, rchip, w, m, v)


_SMALL_LEAVES = ("norm_gain", "final_norm_gain", "b_glu", "ssm_a_re", "ssm_a_im", "ssm_log_dt", "ssm_d", "conv_w",
                 "ssm_c_re", "ssm_c_im", "ssm_b_re", "ssm_b_im")


def _reduce_adam_small(r_pack, r_gc, r_gb, wmv, sharded):
    n_leaf = len(_SMALL_LEAVES)
    n_sh = len(sharded)

    def body(*refs):
        rp_ref, rgc_ref, rgb_ref = refs[:3]
        w_refs = refs[3:3 + 3 * n_leaf]
        sh_in = refs[3 + 3 * n_leaf:3 + 3 * n_leaf + 4 * n_sh]
        outs0 = 3 + 3 * n_leaf + 4 * n_sh
        loss_ref = refs[outs0]
        o_refs = refs[outs0 + 1:outs0 + 1 + 4 * n_leaf]
        sh_out = refs[outs0 + 1 + 4 * n_leaf:outs0 + 1 + 4 * n_leaf + 4 * n_sh]
        own_conv = refs[-1]

        def total(ref):
            acc = ref[0].astype(F32)
            for s in range(1, N_DEV):
                acc = acc + ref[s].astype(F32)
            return acc

        for i in range(n_sh):
            r_ref, w_ref, m_ref, v_ref = sh_in[4 * i:4 * i + 4]
            o_g, o_d, o_m, o_v = sh_out[4 * i:4 * i + 4]
            g = total(r_ref)
            o_g[...] = g
            o_d[...], o_m[...], o_v[...] = _adamw(g, w_ref[...], m_ref[...], v_ref[...])

        sp = total(rp_ref)
        sgc = total(rgc_ref)
        sgb = total(rgb_ref)
        loss_ref[...] = sp[ROW_LOSS:ROW_LOSS + 1, 0:1]

        def wide(r):
            return jnp.concatenate([sp[r:r + 1, :], sp[r + 1:r + 2, :]], axis=1)

        s5 = slice(ROW_S5, ROW_S5 + N_GROUPS)
        eye = (lax.broadcasted_iota(jnp.int32, (N_GROUPS, N_GROUPS), 0)
               == lax.broadcasted_iota(jnp.int32, (N_GROUPS, N_GROUPS), 1)).astype(F32)
        d_rows = jnp.broadcast_to(sp[ROW_BGLU_D + 1:ROW_BGLU_D + 2, :], (GROUP, SSM_W))
        own_p = (lax.broadcasted_iota(jnp.int32, (GROUP, SSM_W), 1) % GROUP
                 == lax.broadcasted_iota(jnp.int32, (GROUP, SSM_W), 0))
        of_group = (lax.broadcasted_iota(jnp.int32, (SSM_W, N_GROUPS), 0) // GROUP
                    == lax.broadcasted_iota(jnp.int32, (SSM_W, N_GROUPS), 1)).astype(BF16)
        d_pg = sum(_dot(t, of_group) for t in _split3(jnp.where(own_p, d_rows, 0.0)))
        me = 4 * lax.axis_index("x") + 2 * lax.axis_index("y") + lax.axis_index("c")
        for k in range(N_DEV):
            @pl.when(me == k)
            def _(k=k):
                own_conv[...] = sp[ROW_CONV:ROW_CONV + SUBLANES, k * CONV_COLS_PER_DEV:(k + 1) * CONV_COLS_PER_DEV]
        grads = {
            "norm_gain": wide(ROW_NORM_GAIN),
            "final_norm_gain": wide(ROW_FINAL_GAIN),
            "b_glu": sp[ROW_BGLU_D:ROW_BGLU_D + 1, :],
            "ssm_a_re": sp[s5, LANE_A_RE:LANE_A_RE + STATE],
            "ssm_a_im": sp[s5, LANE_A_IM:LANE_A_IM + STATE],
            "ssm_log_dt": jnp.sum(sp[s5, LANE_LOG_DT:LANE_LOG_DT + 1] * eye, axis=0, keepdims=True),
            "ssm_d": d_pg,
            "ssm_c_re": sgc[:, 0:STATE],
            "ssm_c_im": sgc[:, STATE:2 * STATE],
            "ssm_b_re": sgb[:, 0:STATE],
            "ssm_b_im": sgb[:, STATE:2 * STATE],
        }
        for i, name in enumerate(_SMALL_LEAVES):
            w_ref, m_ref, v_ref = w_refs[3 * i:3 * i + 3]
            o_g, o_d, o_m, o_v = o_refs[4 * i:4 * i + 4]
            if name == "conv_w":
                for k in range(w_ref.shape[0]):
                    g = own_conv[k:k + 1, :]
                    o_g[k] = g
                    o_d[k], o_m[k], o_v[k] = _adamw(g, w_ref[k], m_ref[k], v_ref[k])
                continue
            g = grads[name]
            o_g[...] = g
            o_d[...], o_m[...], o_v[...] = _adamw(g, w_ref[...], m_ref[...], v_ref[...])

    flat_w = [a for name in _SMALL_LEAVES for a in wmv[name]]
    leaf_shapes = [_out(wmv[name][0].shape, F32) for name in _SMALL_LEAVES for _ in range(4)]
    sh_shapes = [_out(entry[1].shape, F32) for entry in sharded for _ in range(4)]
    operands = (r_pack, r_gc, r_gb, *flat_w, *(a for entry in sharded for a in entry))
    out_shape = (_out((1, 1), F32), *leaf_shapes, *sh_shapes)
    outs = _pcall(
        body, name="reduce_adam_small", grid=(1,), out_shape=out_shape,
        in_specs=_whole_specs(operands), out_specs=tuple(_whole_specs(out_shape)),
        scratch_shapes=[pltpu.VMEM((SUBLANES, CONV_COLS_PER_DEV), F32)],
        compiler_params=_params(1),
    )(*operands)
    leaves = {name: outs[1 + 4 * i:5 + 4 * i] for i, name in enumerate(_SMALL_LEAVES)}
    first = 1 + 4 * n_leaf
    return outs[0], leaves, [outs[first + 4 * i:first + 4 * i + 4] for i in range(n_sh)]


def kernel(x, norm_gain, w_in, ssm_a_re, ssm_a_im, ssm_log_dt, ssm_b_re, ssm_b_im, ssm_c_re, ssm_c_im, ssm_d, w_glu, b_glu, conv_w, w_out, final_norm_gain, loss_target, m_norm_gain, m_w_in, m_ssm_a_re, m_ssm_a_im, m_ssm_log_dt, m_ssm_b_re, m_ssm_b_im, m_ssm_c_re, m_ssm_c_im, m_ssm_d, m_w_glu, m_b_glu, m_conv_w, m_w_out, m_final_norm_gain, v_norm_gain, v_w_in, v_ssm_a_re, v_ssm_a_im, v_ssm_log_dt, v_ssm_b_re, v_ssm_b_im, v_ssm_c_re, v_ssm_c_im, v_ssm_d, v_w_glu, v_b_glu, v_conv_w, v_w_out, v_final_norm_gain):
    n_seq, seq, _ = x.shape
    n = n_seq * seq

    gh_p = lambda b4: jnp.transpose(b4, (0, 1, 3, 2)).reshape(N_GROUPS * GROUP, STATE)
    c2 = lambda a: a.reshape(N_GROUPS * GROUP, STATE)
    b_re2, b_im2 = gh_p(ssm_b_re), gh_p(ssm_b_im)
    d_row = ssm_d[0].reshape(1, SSM_W)

    x2 = x.reshape(n, D_MODEL)
    tgt2 = loss_target.reshape(n, D_MODEL)
    mx, my, mc = lax.axis_index("x"), lax.axis_index("y"), lax.axis_index("c")
    chip_ids = [2 * cx + cy for cx, cy in ((mx, my), (1 - mx, my), (mx, 1 - my), (1 - mx, 1 - my))]
    arrival = chip_ids
    xn, proj, w_in_f, s5 = _in_proj(
        jnp.stack(arrival).astype(jnp.int32), x2, norm_gain, w_in[0].astype(BF16),
        (ssm_a_re[0], ssm_a_im[0], ssm_log_dt, b_re2, b_im2, c2(ssm_c_re), c2(ssm_c_im)))
    a_re_x, a_im_x, log_dt_x, ab_re, ab_im, bb_re_m, bb_im_m, c_re_m, c_imn_m = s5
    u3 = proj.reshape(n_seq, seq, IN_COLS)
    conv_p = jnp.pad(conv_w[0], ((0, SUBLANES - 3), (0, LANES - CONV_COLS_PER_DEV)))
    s_re, s_im, y3, w_out_f, w_glu_f, conv_all = _ssm_fwd(
        u3, bb_re_m, bb_im_m, c_re_m, c_imn_m, d_row, ab_re, ab_im,
        w_out[0], w_glu[0], conv_p, n_seq, seq)
    conv8 = jnp.transpose(conv_all[:, :, :CONV_COLS_PER_DEV], (1, 0, 2)).reshape(SUBLANES, CONV_W)
    (dh2, dy, dzs, dbc, dzc, dyc, dw_out, dw_glu, loss_t, dgf, dbg, dcw) = _mix(
        x2, tgt2, y3.reshape(n, SSM_W), proj, final_norm_gain.reshape(1, D_MODEL), b_glu, conv8,
        w_glu_f, w_out_f, seq)

    du3, dc_re_d, dc_im_d, dbb_re_d, dbb_im_d, dab_re, dab_im, dd, r_out, r_glu = _ssm_bwd(
        dy.reshape(n_seq, seq, SSM_W), u3, s_re, s_im, bb_re_m, bb_im_m, c_re_m, c_imn_m, d_row, ab_re, ab_im,
        dw_out.reshape(N_DEV, OUT_ROWS_PER_DEV, D_MODEL), dw_glu.reshape(N_DEV, GLU_ROWS_PER_DEV, SSM_W), n_seq, seq)
    du = du3.reshape(n, SSM_W)
    grad_x2, dproj, dg8 = _in_bwd(x2, dh2, du, dzs, dyc, proj, dbc, dzc, norm_gain, conv8, w_in_f, seq)
    pack, gc, gb = _ssm_disc_bwd_pack(
        a_re_x, a_im_x, log_dt_x, b_re2, b_im2, dab_re, dab_im,
        dbb_re_d, dbb_im_d, loss_t, dg8, dgf, dbg, dd, dcw, dc_re_d, dc_im_d)

    own_in, rchip_in, r_pack, r_gc, r_gb = _dw_in_exchange(
        [chip_ids[3], chip_ids[2], chip_ids[1], chip_ids[0]],
        xn, dproj, [pack, gc, gb])

    flat2 = lambda a: a.reshape(a.shape[-2:]) if a.ndim > 2 else a.reshape(1, -1)
    c2 = lambda a: a.reshape(N_GROUPS * GROUP, STATE)
    wmv = dict(norm_gain=(norm_gain, m_norm_gain, v_norm_gain),
               final_norm_gain=tuple(flat2(a) for a in (final_norm_gain, m_final_norm_gain, v_final_norm_gain)),
               b_glu=(b_glu, m_b_glu, v_b_glu),
               ssm_a_re=tuple(flat2(a) for a in (ssm_a_re, m_ssm_a_re, v_ssm_a_re)),
               ssm_a_im=tuple(flat2(a) for a in (ssm_a_im, m_ssm_a_im, v_ssm_a_im)),
               ssm_log_dt=(ssm_log_dt, m_ssm_log_dt, v_ssm_log_dt),
               ssm_d=tuple(jnp.transpose(a, (0, 2, 1)).reshape(GROUP, N_GROUPS) for a in (ssm_d, m_ssm_d, v_ssm_d)),
               conv_w=tuple(jnp.transpose(a, (1, 0, 2)) for a in (conv_w, m_conv_w, v_conv_w)),
               ssm_c_re=tuple(c2(a) for a in (ssm_c_re, m_ssm_c_re, v_ssm_c_re)),
               ssm_c_im=tuple(c2(a) for a in (ssm_c_im, m_ssm_c_im, v_ssm_c_im)),
               ssm_b_re=(b_re2, gh_p(m_ssm_b_re), gh_p(v_ssm_b_re)),
               ssm_b_im=(b_im2, gh_p(m_ssm_b_im), gh_p(v_ssm_b_im)))

    res_in = _reduce_adam_w_in(own_in, rchip_in, w_in[0], m_w_in[0], v_w_in[0])
    loss11, small, (res_out, res_glu) = _reduce_adam_small(
        r_pack, r_gc, r_gb, wmv,
        [(r_out, w_out[0], m_w_out[0], v_w_out[0]), (r_glu, w_glu[0], m_w_glu[0], v_w_glu[0])])
    loss = loss11.reshape(())

    shapes = dict(norm_gain=(1, D_MODEL), ssm_a_re=(1, N_GROUPS, STATE), ssm_a_im=(1, N_GROUPS, STATE),
                  ssm_log_dt=(1, N_GROUPS), ssm_c_re=(1, N_GROUPS, GROUP, STATE), ssm_c_im=(1, N_GROUPS, GROUP, STATE),
                  b_glu=(1, SSM_W), final_norm_gain=(D_MODEL,))
    big = dict(w_in=res_in, w_glu=res_glu, w_out=res_out)

    def leaf(kind, name):
        if name in big:
            return big[name][kind][None]
        if name in ("ssm_b_re", "ssm_b_im"):
            return jnp.transpose(small[name][kind].reshape(1, N_GROUPS, GROUP, STATE), (0, 1, 3, 2))
        if name == "ssm_d":
            return jnp.transpose(small[name][kind].reshape(1, GROUP, N_GROUPS), (0, 2, 1))
        if name == "conv_w":
            return jnp.transpose(small[name][kind], (1, 0, 2))
        return small[name][kind].reshape(shapes[name])

    order = ["norm_gain", "w_in", "ssm_a_re", "ssm_a_im", "ssm_log_dt", "ssm_b_re", "ssm_b_im", "ssm_c_re",
             "ssm_c_im", "ssm_d", "w_glu", "b_glu", "conv_w", "w_out", "final_norm_gain"]
    outs = [loss, grad_x2.reshape(x.shape)]
    for kind in range(4):
        outs += [leaf(kind, name) for name in order]
    return tuple(outs)
```

```python
import functools
import math

import jax
import jax.numpy as jnp
from jax import lax
from jax.experimental import pallas as pl
from jax.experimental.pallas import tpu as pltpu

F32 = jnp.float32
BF16 = jnp.bfloat16

N_DEV = 8
D_MODEL = 1024
SSM_W = 512
CONV_W = 512
N_GROUPS = 32
GROUP = 16
STATE = 64
IN_COLS = 3072
SEG_U, SEG_ZS, SEG_H, SEG_BC, SEG_CC, SEG_ZC = range(6)
COLS_PER_DEV = IN_COLS // N_DEV
N_CHIP = N_DEV // 2
COLS_PER_CHIP = 2 * COLS_PER_DEV
OUT_ROWS_PER_DEV = D_MODEL // N_DEV
GLU_ROWS_PER_DEV = SSM_W // N_DEV
CONV_COLS_PER_DEV = CONV_W // N_DEV
EPS = 1e-6

N_JBLK = 4
JB_CH = SSM_W // N_JBLK
JB_ST = N_GROUPS * STATE // N_JBLK

ADAM_LR = 0.001
ADAM_B1 = 0.9
ADAM_B2 = 0.999
ADAM_EPS = 1e-08
ADAM_WD = 0.01
ADAM_STEP = 10

SUBLANES = 8
LANES = 128
VMEM_LIMIT = 48 * 1024 * 1024
TOK_TILE = 256
IN_TILE = 1024
SCAN_TILE = 1024

MESH = pl.DeviceIdType.MESH
HBM_SPEC = pl.BlockSpec(memory_space=pltpu.HBM)


def _build(body, **kw):
    return pl.pallas_call(body, **kw)


def _pcall(body, **kw):
    def call(*operands):
        pinned = [a if jnp.issubdtype(a.dtype, jnp.integer) else pltpu.with_memory_space_constraint(a, pltpu.HBM)
                  for a in operands]
        return _build(body, **kw)(*pinned)
    return call


def _whole_specs(arrays):
    return [pl.BlockSpec(a.shape, functools.partial(lambda nd, i: (0,) * nd, len(a.shape))) for a in arrays]


def _out(shape, dtype):
    return pltpu.HBM(tuple(shape), dtype)


def _params(n_grid):
    return pltpu.CompilerParams(dimension_semantics=("arbitrary",) * n_grid,
                                vmem_limit_bytes=VMEM_LIMIT)


def _dot(a, b):
    return jnp.dot(a, b, preferred_element_type=F32)


def _dot_nt(a, b):
    return lax.dot_general(a, b, (((1,), (1,)), ((), ())), preferred_element_type=F32)


def _dot_tn(a, b):
    return lax.dot_general(a, b, (((0,), (0,)), ((), ())), preferred_element_type=F32)


def _sigmoid(z):
    return 1.0 / (1.0 + jnp.exp(-z))


_GELU_C = math.sqrt(2.0 / math.pi)


def _gelu_and_grad(y):
    inner = _GELU_C * (y + 0.044715 * (y * y * y))
    t = jnp.tanh(inner)
    g = 0.5 * y * (1.0 + t)
    dg = 0.5 * (1.0 + t) + 0.5 * y * (1.0 - t * t) * (_GELU_C * (1.0 + 3.0 * 0.044715 * (y * y)))
    return g, dg


def _silu_and_grad(z):
    s = _sigmoid(z)
    return z * s, s * (1.0 + z * (1.0 - s))


def _shift_down(v, halo, k):
    rolled = pltpu.roll(v, k, 0)
    row = lax.broadcasted_iota(jnp.int32, v.shape, 0)
    for r in range(k):
        rolled = jnp.where(row == r, halo[SUBLANES - k + r:SUBLANES - k + r + 1, :], rolled)
    return rolled


def _shift_up(v, halo, k):
    n = v.shape[0]
    rolled = pltpu.roll(v, n - k, 0)
    row = lax.broadcasted_iota(jnp.int32, v.shape, 0)
    for r in range(k):
        rolled = jnp.where(row == n - k + r, halo[r:r + 1, :], rolled)
    return rolled


def _mesh_pos():
    return lax.axis_index("x"), lax.axis_index("y"), lax.axis_index("c")


def _direct_copies(srcs_for, out_refs, send_sems, recv_sems, loc_sems):
    x, y, c = _mesh_pos()
    me_id = 4 * x + 2 * y + c
    n_arr = len(out_refs)
    dsts = [r.at[me_id] for r in out_refs]
    own = srcs_for(me_id)
    mine = [pltpu.make_async_copy(own[a], dsts[a], loc_sems.at[a]) for a in range(n_arr)]
    sends = []
    for k in range(1, N_DEV):
        px, py, pc = x ^ ((k >> 2) & 1), y ^ ((k >> 1) & 1), c ^ (k & 1)
        src = srcs_for(4 * px + 2 * py + pc)
        for a in range(n_arr):
            sends.append(pltpu.make_async_remote_copy(
                src_ref=src[a], dst_ref=dsts[a],
                send_sem=send_sems.at[(k - 1) * n_arr + a], recv_sem=recv_sems.at[(k - 1) * n_arr + a],
                device_id=(px, py, pc), device_id_type=MESH))
    return mine, sends


class _TwoLevelGather:
    def __init__(self, srcs, slots, send_sems, recv_sems, loc_sems):
        self.srcs, self.slots, self.n_arr = srcs, slots, len(srcs)
        self.send_sems, self.recv_sems, self.loc_sems = send_sems, recv_sems, loc_sems
        x, y, c = _mesh_pos()
        self.c = c
        self.me, self.sib = (x, y, c), (x, y, 1 - c)
        self.chips = [(1 - x, y), (x, 1 - y), (1 - x, 1 - y)]

    def _copies(self, k, block, to, from_src=False):
        dev = 4 * block[0] + 2 * block[1] + block[2]
        return [pltpu.make_async_remote_copy(
            src_ref=self.srcs[a] if from_src else self.slots[a](dev), dst_ref=self.slots[a](dev),
            send_sem=self.send_sems.at[k * self.n_arr + a], recv_sem=self.recv_sems.at[k * self.n_arr + a],
            device_id=to, device_id_type=MESH) for a in range(self.n_arr)]

    def _local(self):
        dev = 4 * self.me[0] + 2 * self.me[1] + self.me[2]
        return [pltpu.make_async_copy(self.srcs[a], self.slots[a](dev), self.loc_sems.at[a])
                for a in range(self.n_arr)]

    def start(self):
        for cp in self._local() + self._copies(0, self.me, self.sib, True):
            cp.start()
        for j in (0, 1):
            for cp in self._copies(1 + j, self.me, (*self.chips[j], self.c), True):
                cp.start()

    def wait_own(self):
        for cp in self._local():
            cp.wait()

    def wait_sibling(self):
        for cp in self._copies(0, self.sib, self.me):
            cp.wait_recv()

    def wait_and_pass_on(self, j):
        chip = self.chips[j]
        for cp in self._copies(1 + j, (*chip, self.c), self.me):
            cp.wait_recv()
        for cp in self._copies(4 + j, (*chip, self.c), self.sib):
            cp.start()

    def neighbours_landed(self):
        x, y, c = self.me
        self.wait_and_pass_on(0)
        self.wait_and_pass_on(1)
        for cp in self._copies(1 + 2, (x ^ c, y ^ (1 - c), c), (x ^ (1 - c), y ^ c, c)):
            cp.start()

    def diagonal_landed(self):
        self.wait_and_pass_on(2)

    def wait_passed_on(self, j):
        for cp in self._copies(4 + j, (*self.chips[j], 1 - self.c), self.me):
            cp.wait_recv()

    def wait_sends(self):
        for cp in self._copies(0, self.me, self.sib, True):
            cp.wait_send()
        for j, chip in enumerate(self.chips):
            for cp in self._copies(1 + j, self.me, (*chip, self.c), True) + self._copies(4 + j, (*chip, self.c), self.sib):
                cp.wait_send()

    def finish(self):
        self.wait_sibling()
        for j in range(3):
            self.wait_passed_on(j)
        self.wait_sends()
        self.wait_own()


def _disc(a_re, a_im, log_dt, b_re, b_im):
    dt = jnp.exp(log_dt)
    mag = jnp.exp(a_re * dt)
    ab_re = mag * jnp.cos(a_im * dt)
    ab_im = mag * jnp.sin(a_im * dt)
    den = a_re * a_re + a_im * a_im
    p_re = ab_re - 1.0
    p_im = ab_im
    q_re = (p_re * a_re + p_im * a_im) / den
    q_im = (p_im * a_re - p_re * a_im) / den
    bb_re = q_re * b_re - q_im * b_im
    bb_im = q_re * b_im + q_im * b_re
    return ab_re, ab_im, bb_re, bb_im


def _split3(v):
    hi = v.astype(BF16)
    r1 = v - hi.astype(F32)
    mid = r1.astype(BF16)
    lo = (r1 - mid.astype(F32)).astype(BF16)
    return hi, mid, lo


def _select_dot(sel, v):
    return sum(_dot(sel, t) for t in _split3(v))


PACK_ROWS = 72
PACK_W = 512
ROW_FINAL_GAIN, ROW_NORM_GAIN, ROW_BGLU_D, ROW_CONV, ROW_LOSS, ROW_S5 = 0, 8, 16, 24, 32, 40
LANE_A_RE, LANE_A_IM, LANE_LOG_DT = 0, 128, 256


def _ssm_disc_bwd_pack(a_re_x, a_im_x, log_dt_x, b_re, b_im, g_ab_re, g_ab_im, dbb_re_d, dbb_im_d,
                       loss_t, dg8, dgf, dbg, dd, dcw, dc_re_d, dc_im_d):
    rows_gh = N_GROUPS * GROUP

    def body(are, aim, ldt, bre, bim, gabre, gabim, dbbre_ref, dbbim_ref,
             loss_ref, dg8_ref, dgf_ref, dbg_ref, dd_ref, dcw_ref, dcre_ref, dcim_ref,
             p_ref, gc_ref, gb_ref, gbb_re, gbb_im):
        r_g = lax.broadcasted_iota(jnp.int32, (N_GROUPS, rows_gh), 0)
        c_gh = lax.broadcasted_iota(jnp.int32, (N_GROUPS, rows_gh), 1)
        group_sum = (c_gh // GROUP == r_g).astype(BF16)
        r_gh = lax.broadcasted_iota(jnp.int32, (rows_gh, N_GROUPS), 0)
        c_g = lax.broadcasted_iota(jnp.int32, (rows_gh, N_GROUPS), 1)
        first_row = (r_gh == c_g * GROUP).astype(BF16)

        def diag_block(ref, j, gi):
            return ref[j, gi * GROUP:(gi + 1) * GROUP, gi * STATE:(gi + 1) * STATE]

        for j in range(N_JBLK):
            for gi in range(SUBLANES):
                r0 = (j * SUBLANES + gi) * GROUP
                gbb_re[r0:r0 + GROUP, :] = diag_block(dbbre_ref, j, gi)
                gbb_im[r0:r0 + GROUP, :] = diag_block(dbbim_ref, j, gi)
                both = jnp.concatenate([diag_block(dcre_ref, j, gi), -diag_block(dcim_ref, j, gi)], axis=1)
                gc_ref[r0:r0 + GROUP, :] = both.astype(BF16)

        _, vjp = jax.vjp(_disc, are[...], aim[...], ldt[...], bre[...], bim[...])
        d_are, d_aim, d_ldt, d_bre, d_bim = vjp((_select_dot(first_row, gabre[...]), _select_dot(first_row, gabim[...]),
                                                 gbb_re[...], gbb_im[...]))
        gb_ref[...] = jnp.concatenate([d_bre, d_bim], axis=1).astype(BF16)

        p_ref[...] = jnp.zeros_like(p_ref)
        half = D_MODEL // 2
        for r, src in ((ROW_FINAL_GAIN, dgf_ref), (ROW_NORM_GAIN, dg8_ref)):
            p_ref[r:r + 1, :] = src[0:1, 0:half]
            p_ref[r + 1:r + 2, :] = src[0:1, half:D_MODEL]
        p_ref[ROW_BGLU_D:ROW_BGLU_D + 1, :] = dbg_ref[...]
        p_ref[ROW_BGLU_D + 1:ROW_BGLU_D + 2, :] = dd_ref[...]
        p_ref[ROW_CONV:ROW_CONV + SUBLANES, :] = dcw_ref[...]
        p_ref[ROW_LOSS:ROW_LOSS + SUBLANES, 0:LANES] = loss_ref[...]
        s5 = slice(ROW_S5, ROW_S5 + N_GROUPS)
        p_ref[s5, LANE_A_RE:LANE_A_RE + STATE] = _select_dot(group_sum, d_are)
        p_ref[s5, LANE_A_IM:LANE_A_IM + STATE] = _select_dot(group_sum, d_aim)
        p_ref[s5, LANE_LOG_DT:LANE_LOG_DT + LANES] = _select_dot(group_sum, jnp.broadcast_to(d_ldt, (rows_gh, LANES)))

    operands = (a_re_x, a_im_x, log_dt_x, b_re, b_im, g_ab_re, g_ab_im, dbb_re_d, dbb_im_d,
                loss_t, dg8, dgf, dbg, dd, dcw, dc_re_d, dc_im_d)
    out_shape = (_out((PACK_ROWS, PACK_W), F32),
                 _out((rows_gh, 2 * STATE), BF16),
                 _out((rows_gh, 2 * STATE), BF16))
    return _pcall(body, name="ssm_disc_bwd_pack", grid=(1,), out_shape=out_shape,
                  in_specs=_whole_specs(operands), out_specs=tuple(_whole_specs(out_shape)),
                  scratch_shapes=[pltpu.VMEM((rows_gh, STATE), F32), pltpu.VMEM((rows_gh, STATE), F32)],
                  compiler_params=_params(1))(*operands)


def _s5_prepare(are, aim, ldt, bre, bim, cre, cim,
                o_ax_re, o_ax_im, o_ldt_x, o_ab_re, o_ab_im, o_bb_re, o_bb_im, o_c_re, o_c_imn):
    rows_gh = N_GROUPS * GROUP
    rep = (lax.broadcasted_iota(jnp.int32, (rows_gh, N_GROUPS), 0) // GROUP
           == lax.broadcasted_iota(jnp.int32, (rows_gh, N_GROUPS), 1)).astype(BF16)
    eye = (lax.broadcasted_iota(jnp.int32, (N_GROUPS, N_GROUPS), 0)
           == lax.broadcasted_iota(jnp.int32, (N_GROUPS, N_GROUPS), 1)).astype(F32)
    ldt_col = jnp.sum(eye * ldt[...], axis=1, keepdims=True)
    a_re_x = _select_dot(rep, are[...])
    a_im_x = _select_dot(rep, aim[...])
    ldt_x = _select_dot(rep, jnp.broadcast_to(ldt_col, (N_GROUPS, LANES)))[:, 0:1]
    o_ax_re[...] = a_re_x
    o_ax_im[...] = a_im_x
    o_ldt_x[...] = ldt_x
    ab_re, ab_im, bb_re, bb_im = _disc(a_re_x, a_im_x, ldt_x, bre[...], bim[...])
    for j in range(N_JBLK):
        first = [(j * SUBLANES + gi) * GROUP for gi in range(SUBLANES)]
        o_ab_re[j] = jnp.concatenate([ab_re[r:r + 1, :] for r in first], axis=1)
        o_ab_im[j] = jnp.concatenate([ab_im[r:r + 1, :] for r in first], axis=1)
    for o, v in ((o_bb_re, bb_re), (o_bb_im, bb_im), (o_c_re, cre[...]), (o_c_imn, -cim[...])):
        for j in range(N_JBLK):
            for gi in range(SUBLANES):
                r0 = (j * SUBLANES + gi) * GROUP
                parts = [v[r0:r0 + GROUP, :] if k == gi else jnp.zeros((GROUP, STATE), F32) for k in range(SUBLANES)]
                o[j, gi * GROUP:(gi + 1) * GROUP, :] = jnp.concatenate(parts, axis=1).astype(BF16)


def _in_proj(order, x2, g1, w_in_b, s5):
    n = x2.shape[0]
    tm = min(IN_TILE, n)
    n_tiles = n // tm
    n_s5_in = len(s5)
    n_s5_out = 9

    def body(order_ref, x_ref, g_ref, w_ref, *refs):
        s5_in = refs[:n_s5_in]
        xn_ref, proj_ref, wall_ref = refs[n_s5_in:n_s5_in + 3]
        s5_out = refs[n_s5_in + 3:n_s5_in + 3 + n_s5_out]
        xn_scr, wbuf, send_sems, recv_sems, loc_sems, out_sems = refs[n_s5_in + 3 + n_s5_out:]
        k = pl.program_id(0)
        i = pl.program_id(1)

        def slot(dev):
            return wbuf.at[dev // 2, :, pl.ds(pl.multiple_of((dev % 2) * COLS_PER_DEV, LANES), COLS_PER_DEV)]

        gather = _TwoLevelGather([w_ref], [slot], send_sems, recv_sems, loc_sems)

        @pl.when((k == 0) & (i == 0))
        def _():
            gather.start()

        def own_chip():
            gather.wait_own()
            gather.wait_sibling()

        def x_chip():
            gather.neighbours_landed()
            gather.wait_passed_on(0)

        def diag_chip():
            gather.diagonal_landed()
            gather.wait_passed_on(2)

        arrivals = [own_chip, x_chip, functools.partial(gather.wait_passed_on, 1), diag_chip]
        for kk, arrived in enumerate(arrivals):
            @pl.when((k == kk) & (i == 0))
            def _(arrived=arrived):
                arrived()

        rows = pl.ds(pl.multiple_of(i * tm, tm), tm)

        @pl.when(k == 0)
        def _():
            x = x_ref[...]
            r = lax.rsqrt(jnp.mean(x * x, axis=-1, keepdims=True) + EPS)
            xn = ((x * r) * g_ref[...]).astype(BF16)
            xn_scr[rows, :] = xn
            xn_ref[...] = xn

        proj_ref[...] = _dot(xn_scr[rows, :], wbuf[order_ref[k]])

        @pl.when((k == 0) & (i == n_tiles - 1))
        def _():
            _s5_prepare(*s5_in, *s5_out)

        @pl.when((k == N_CHIP - 1) & (i == n_tiles - 1))
        def _():
            gather.wait_sends()
            outs = [pltpu.make_async_copy(wbuf.at[q], wall_ref.at[:, q * COLS_PER_CHIP:(q + 1) * COLS_PER_CHIP],
                                          out_sems.at[q]) for q in range(N_CHIP)]
            for cp in outs:
                cp.start()
            for cp in outs:
                cp.wait()

    tile_once = lambda k, i, order: (jnp.where(k == 0, i, n_tiles - 1), 0)
    whole = lambda shape: pl.BlockSpec(shape, lambda k, i, order: (0,) * len(shape))
    rows_gh = N_GROUPS * GROUP
    s5_out_shapes = ([(rows_gh, STATE), F32], [(rows_gh, STATE), F32], [(rows_gh, 1), F32],
                     [(N_JBLK, 1, JB_ST), F32], [(N_JBLK, 1, JB_ST), F32]) + ([(N_JBLK, JB_CH, JB_ST), BF16],) * 4
    grid_spec = pltpu.PrefetchScalarGridSpec(
        num_scalar_prefetch=1, grid=(N_CHIP, n_tiles),
        in_specs=[pl.BlockSpec((tm, D_MODEL), tile_once),
                  whole((1, D_MODEL)),
                  HBM_SPEC,
                  *(whole(a.shape) for a in s5)],
        out_specs=(pl.BlockSpec((tm, D_MODEL), tile_once),
                   pl.BlockSpec((tm, COLS_PER_CHIP), lambda k, i, order: (i, order[k])),
                   HBM_SPEC,
                   *(whole(shape) for shape, _ in s5_out_shapes)),
        scratch_shapes=[pltpu.VMEM((n, D_MODEL), BF16), pltpu.VMEM((N_CHIP, D_MODEL, COLS_PER_CHIP), BF16),
                        pltpu.SemaphoreType.DMA((7,)), pltpu.SemaphoreType.DMA((7,)), pltpu.SemaphoreType.DMA((1,)),
                        pltpu.SemaphoreType.DMA((N_CHIP,))])
    outs = _pcall(
        body, name="in_proj", grid_spec=grid_spec,
        out_shape=(_out((n, D_MODEL), BF16), _out((n, IN_COLS), F32),
                   _out((D_MODEL, IN_COLS), BF16),
                   *(_out(shape, dt) for shape, dt in s5_out_shapes)),
        compiler_params=_params(2),
    )(order, x2, g1, w_in_b, *s5)
    return outs[0], outs[1], outs[2], outs[3:]


def _cmul(p, q):
    return p[0] * q[0] - p[1] * q[1], p[0] * q[1] + p[1] * q[0]


def _scan_tables(ar, ai, width, reverse):
    pows = [(ar, ai)]
    for _ in range(SUBLANES - 1):
        pows.append(_cmul(pows[-1], (ar, ai)))
    row = lax.broadcasted_iota(jnp.int32, (SUBLANES, width), 0)

    def bc(v):
        return jnp.broadcast_to(v, (SUBLANES, width))

    levels = []
    for k in (1, 2, 4):
        keep = (row <= SUBLANES - 1 - k) if reverse else (row >= k)
        levels.append((jnp.where(keep, bc(pows[k - 1][0]), 0.0), jnp.where(keep, bc(pows[k - 1][1]), 0.0)))
    cre = jnp.zeros((SUBLANES, width), F32)
    cim = jnp.zeros((SUBLANES, width), F32)
    for r in range(SUBLANES):
        e = (SUBLANES - r) if reverse else (r + 1)
        cre = jnp.where(row == r, bc(pows[e - 1][0]), cre)
        cim = jnp.where(row == r, bc(pows[e - 1][1]), cim)
    return levels, (cre, cim)


def _load_chunked(src_ref, b, dst_ref, n_rows):
    n_blk = n_rows // SUBLANES
    for i in range(n_blk):
        dst_ref[b, i * SUBLANES:(i + 1) * SUBLANES, :] = src_ref[b, pl.ds(i, SUBLANES, stride=n_blk), :]


def _store_chunked(val, dst_ref, b, n_rows):
    n_blk = n_rows // SUBLANES
    for i in range(n_blk):
        dst_ref[b, pl.ds(i, SUBLANES, stride=n_blk), :] = val[i * SUBLANES:(i + 1) * SUBLANES, :]


def _chunk_scan(re_ref, im_ref, bs, car_ref, ar, ai, n_rows, reverse, on_block=None):
    width = re_ref.shape[2]
    n_blk = n_rows // SUBLANES
    shape = (SUBLANES, width)
    abr = jnp.broadcast_to(ar, shape)
    abi = jnp.broadcast_to(ai, shape)
    order = list(range(n_blk - 1, -1, -1)) if reverse else list(range(n_blk))

    def blk(ref, b, i):
        return ref[b, i * SUBLANES:(i + 1) * SUBLANES, :]

    def step(state, b, i):
        sr, si = state
        return abr * sr - abi * si + blk(re_ref, b, i), abr * si + abi * sr + blk(im_ref, b, i)

    finals = {b: (blk(re_ref, b, order[0]), blk(im_ref, b, order[0])) for b in bs}
    for i in order[1:]:
        for b in bs:
            finals[b] = step(finals[b], b, i)

    mr, mi = ar, ai
    for _ in range(n_blk.bit_length() - 1):
        mr, mi = _cmul((mr, mi), (mr, mi))
    levels, _ = _scan_tables(mr, mi, width, reverse)
    mbr = jnp.broadcast_to(mr, shape)
    mbi = jnp.broadcast_to(mi, shape)
    row = lax.broadcasted_iota(jnp.int32, shape, 0)
    edge_in = SUBLANES - 1 if reverse else 0
    edge_out = 0 if reverse else SUBLANES - 1
    sh1 = SUBLANES - 1 if reverse else 1
    states = {}
    for b in bs:
        fr, fi = finals[b]
        gr = jnp.where(row == edge_in, jnp.broadcast_to(car_ref[b, 0:1, :], shape), pltpu.roll(fr, sh1, 0))
        gi = jnp.where(row == edge_in, jnp.broadcast_to(car_ref[b, 1:2, :], shape), pltpu.roll(fi, sh1, 0))
        for (lr, li), k in zip(levels, (1, 2, 4)):
            sh = (SUBLANES - k) if reverse else k
            sr = pltpu.roll(gr, sh, 0)
            si = pltpu.roll(gi, sh, 0)
            gr, gi = gr + (lr * sr - li * si), gi + (lr * si + li * sr)
        car_ref[b, 0:1, :] = (fr + (mbr * gr - mbi * gi))[edge_out:edge_out + 1, :]
        car_ref[b, 1:2, :] = (fi + (mbr * gi + mbi * gr))[edge_out:edge_out + 1, :]
        states[b] = (gr, gi)

    for i in order:
        for b in bs:
            states[b] = step(states[b], b, i)
            re_ref[b, i * SUBLANES:(i + 1) * SUBLANES, :] = states[b][0]
            im_ref[b, i * SUBLANES:(i + 1) * SUBLANES, :] = states[b][1]
            if on_block is not None:
                on_block(b, i, *states[b])


def _ssm_fwd(u, bb_re, bb_im, c_re_t, c_imn_t, d_row, ab_re, ab_im, w_out_own, w_glu_own, conv_p, n_seq, seq):
    tt = min(SCAN_TILE, seq)
    nt = seq // tt

    def body(u_ref, bbre, bbim, cre, cimn, d_ref, are, aim, wout_ref, wglu_ref, cw_ref,
             sre_ref, sim_ref, y_ref, oout_ref, oglu_ref, ocw_ref,
             up_ref, car_ref, woutb_ref, wglub_ref, send_sems, recv_sems, loc_sems):
        j = pl.program_id(0)
        t = pl.program_id(1)
        gather = _TwoLevelGather(
            [woutb_ref, wglub_ref, cw_ref],
            [lambda dev: oout_ref.at[pl.ds(pl.multiple_of(dev * OUT_ROWS_PER_DEV, OUT_ROWS_PER_DEV), OUT_ROWS_PER_DEV), :],
             lambda dev: oglu_ref.at[pl.ds(pl.multiple_of(dev * GLU_ROWS_PER_DEV, GLU_ROWS_PER_DEV), GLU_ROWS_PER_DEV), :],
             lambda dev: ocw_ref.at[dev]],
            send_sems, recv_sems, loc_sems)

        @pl.when((j == 0) & (t == 0))
        def _():
            woutb_ref[...] = wout_ref[...].astype(BF16)
            wglub_ref[...] = wglu_ref[...].astype(BF16)
            gather.start()

        @pl.when((j == N_JBLK // 2) & (t == 0))
        def _():
            gather.neighbours_landed()

        @pl.when((j == N_JBLK - 1) & (t == 0))
        def _():
            gather.diagonal_landed()

        @pl.when(t == 0)
        def _():
            car_ref[...] = jnp.zeros_like(car_ref)

        bs = list(range(n_seq))
        for b in bs:
            _load_chunked(u_ref, b, up_ref, tt)
        for b in bs:
            ub = up_ref[b].astype(BF16)
            sre_ref[b] = _dot(ub, bbre[0])
            sim_ref[b] = _dot(ub, bbim[0])
            _chunk_scan(sre_ref, sim_ref, [b], car_ref, are[0], aim[0], tt, reverse=False)
        for b in bs:
            yp = (_dot_nt(sre_ref[b].astype(BF16), cre[0]) + _dot_nt(sim_ref[b].astype(BF16), cimn[0])
                  + d_ref[...] * up_ref[b])
            _store_chunked(yp, y_ref, b, tt)

        @pl.when((j == N_JBLK - 1) & (t == nt - 1))
        def _():
            gather.finish()

    tok = lambda j, t: (0, t, j)
    blk3 = lambda j, t: (j, 0, 0)
    row = lambda j, t: (0, j)
    whole = lambda j, t: (0, 0)
    st = _out((n_seq, seq, N_JBLK * JB_ST), F32)
    n_arr = 3
    return _pcall(
        body, name="ssm_fwd", grid=(N_JBLK, nt),
        out_shape=(st, st, _out((n_seq, seq, SSM_W), F32),
                   _out((D_MODEL, D_MODEL), BF16), _out((SSM_W, SSM_W), BF16),
                   _out((N_DEV, SUBLANES, LANES), F32)),
        in_specs=[pl.BlockSpec((n_seq, tt, JB_CH), tok),
                  pl.BlockSpec((1, JB_CH, JB_ST), blk3), pl.BlockSpec((1, JB_CH, JB_ST), blk3),
                  pl.BlockSpec((1, JB_CH, JB_ST), blk3), pl.BlockSpec((1, JB_CH, JB_ST), blk3),
                  pl.BlockSpec((1, JB_CH), row), pl.BlockSpec((1, 1, JB_ST), blk3), pl.BlockSpec((1, 1, JB_ST), blk3),
                  pl.BlockSpec(w_out_own.shape, whole), pl.BlockSpec(w_glu_own.shape, whole), HBM_SPEC],
        out_specs=(pl.BlockSpec((n_seq, tt, JB_ST), tok), pl.BlockSpec((n_seq, tt, JB_ST), tok),
                   pl.BlockSpec((n_seq, tt, JB_CH), tok), HBM_SPEC, HBM_SPEC, HBM_SPEC),
        scratch_shapes=[pltpu.VMEM((n_seq, tt, JB_CH), F32), pltpu.VMEM((n_seq, SUBLANES, JB_ST), F32),
                        pltpu.VMEM(w_out_own.shape, BF16), pltpu.VMEM(w_glu_own.shape, BF16),
                        pltpu.SemaphoreType.DMA((7 * n_arr,)), pltpu.SemaphoreType.DMA((7 * n_arr,)),
                        pltpu.SemaphoreType.DMA((n_arr,))],
        compiler_params=_params(2),
    )(u, bb_re, bb_im, c_re_t, c_imn_t, d_row, ab_re, ab_im, w_out_own, w_glu_own, conv_p)


def _ssm_bwd(dy, u, s_re, s_im, bb_re, bb_im, c_re_t, c_imn_t, d_row, ab_re, ab_im, g_out, g_glu, n_seq, seq):
    tt = min(SCAN_TILE, seq)
    nt = seq // tt
    rows8 = tt // SUBLANES

    def body(dy_ref, u_ref, sre_ref, sim_ref, pre_ref, pim_ref, bbre, bbim, cre, cimn, d_ref, are, aim,
             gout_ref, gglu_ref,
             du_ref, dcre_ref, dcim_ref, dbbre_ref, dbbim_ref, dare_ref, daim_ref, dd_ref, rout_ref, rglu_ref,
             lre_ref, lim_ref, dyp_ref, up_ref, car_ref, da_ref, send_sems, recv_sems, loc_sems):
        j = pl.program_id(0)
        tr = pl.program_id(1)

        def exchange():
            return _direct_copies(lambda pid: [gout_ref.at[pid], gglu_ref.at[pid]], [rout_ref, rglu_ref],
                                  send_sems, recv_sems, loc_sems)

        @pl.when((j == 0) & (tr == 0))
        def _():
            mine, sends = exchange()
            for cp in mine + sends:
                cp.start()

        @pl.when(tr == 0)
        def _():
            car_ref[...] = jnp.zeros_like(car_ref)
            for r in (dcre_ref, dcim_ref, dbbre_ref, dbbim_ref, da_ref, dd_ref):
                r[...] = jnp.zeros_like(r)

        first = tr == nt - 1
        row = lax.broadcasted_iota(jnp.int32, (SUBLANES, JB_ST), 0)
        n_blk = tt // SUBLANES
        bs = list(range(n_seq))
        for b in bs:
            _load_chunked(dy_ref, b, dyp_ref, tt)
            _load_chunked(u_ref, b, up_ref, tt)
        for b in bs:
            dyb = dyp_ref[b].astype(BF16)
            lre_ref[b] = _dot(dyb, cre[0])
            lim_ref[b] = _dot(dyb, cimn[0])
        acc = {b: [jnp.zeros((SUBLANES, JB_ST), F32), jnp.zeros((SUBLANES, JB_ST), F32)] for b in bs}

        def on_block(b, i, lr, li):
            if i > 0:
                spr = sre_ref[b, (i - 1) * SUBLANES:i * SUBLANES, :]
                spi = sim_ref[b, (i - 1) * SUBLANES:i * SUBLANES, :]
            else:
                hr = jnp.where(first, 0.0, pre_ref[b, SUBLANES - 1:SUBLANES, :])
                hi = jnp.where(first, 0.0, pim_ref[b, SUBLANES - 1:SUBLANES, :])
                last_r = sre_ref[b, (n_blk - 1) * SUBLANES:n_blk * SUBLANES, :]
                last_i = sim_ref[b, (n_blk - 1) * SUBLANES:n_blk * SUBLANES, :]
                spr = jnp.where(row == 0, jnp.broadcast_to(hr, row.shape), pltpu.roll(last_r, 1, 0))
                spi = jnp.where(row == 0, jnp.broadcast_to(hi, row.shape), pltpu.roll(last_i, 1, 0))
            acc[b][0] = acc[b][0] + (lr * spr + li * spi)
            acc[b][1] = acc[b][1] + (li * spr - lr * spi)

        _chunk_scan(lre_ref, lim_ref, bs, car_ref, are[0], -aim[0], tt, reverse=True, on_block=on_block)
        for b in bs:
            da_ref[0] += jnp.sum(acc[b][0], axis=0, keepdims=True)
            da_ref[1] += jnp.sum(acc[b][1], axis=0, keepdims=True)
            dyp = dyp_ref[b]
            up = up_ref[b]
            dyb = dyp.astype(BF16)
            ub = up.astype(BF16)
            lrb = lre_ref[b].astype(BF16)
            lib = lim_ref[b].astype(BF16)
            dup = d_ref[...] * dyp + _dot_nt(lrb, bbre[0]) + _dot_nt(lib, bbim[0])
            _store_chunked(dup, du_ref, b, tt)
            dbbre_ref[0] += _dot_tn(ub, lrb)
            dbbim_ref[0] += _dot_tn(ub, lib)
            dcre_ref[0] += _dot_tn(dyb, sre_ref[b].astype(BF16))
            dcim_ref[0] += _dot_tn(dyb, sim_ref[b].astype(BF16))
            dd_ref[...] += jnp.sum(dyp * up, axis=0, keepdims=True)

        @pl.when(tr == nt - 1)
        def _():
            def by_group(r):
                return jnp.concatenate([r[:, g * STATE:(g + 1) * STATE] for g in range(JB_ST // STATE)], axis=0)

            dare_ref[...] = by_group(da_ref[0])
            daim_ref[...] = by_group(da_ref[1])

        @pl.when((j == N_JBLK - 1) & (tr == nt - 1))
        def _():
            mine, sends = exchange()
            for cp in sends + mine:
                cp.wait()

    tok = lambda j, t: (0, nt - 1 - t, j)
    halo = lambda j, t: (0, jnp.maximum((nt - 1 - t) * rows8 - 1, 0), j)
    blk3 = lambda j, t: (j, 0, 0)
    row1 = lambda j, t: (0, j)
    grp = lambda j, t: (j, 0)
    acc_shape = _out((N_JBLK, JB_CH, JB_ST), F32)
    return _pcall(
        body, name="ssm_bwd", grid=(N_JBLK, nt),
        out_shape=(_out((n_seq, seq, SSM_W), F32), acc_shape, acc_shape, acc_shape, acc_shape,
                   _out((N_GROUPS, STATE), F32), _out((N_GROUPS, STATE), F32),
                   _out((1, SSM_W), F32),
                   _out((N_DEV,) + g_out.shape[1:], F32),
                   _out((N_DEV,) + g_glu.shape[1:], F32)),
        in_specs=[pl.BlockSpec((n_seq, tt, JB_CH), tok), pl.BlockSpec((n_seq, tt, JB_CH), tok),
                  pl.BlockSpec((n_seq, tt, JB_ST), tok), pl.BlockSpec((n_seq, tt, JB_ST), tok),
                  pl.BlockSpec((n_seq, SUBLANES, JB_ST), halo), pl.BlockSpec((n_seq, SUBLANES, JB_ST), halo),
                  pl.BlockSpec((1, JB_CH, JB_ST), blk3), pl.BlockSpec((1, JB_CH, JB_ST), blk3),
                  pl.BlockSpec((1, JB_CH, JB_ST), blk3), pl.BlockSpec((1, JB_CH, JB_ST), blk3),
                  pl.BlockSpec((1, JB_CH), row1), pl.BlockSpec((1, 1, JB_ST), blk3), pl.BlockSpec((1, 1, JB_ST), blk3),
                  HBM_SPEC, HBM_SPEC],
        out_specs=(pl.BlockSpec((n_seq, tt, JB_CH), tok),
                   pl.BlockSpec((1, JB_CH, JB_ST), blk3), pl.BlockSpec((1, JB_CH, JB_ST), blk3),
                   pl.BlockSpec((1, JB_CH, JB_ST), blk3), pl.BlockSpec((1, JB_CH, JB_ST), blk3),
                   pl.BlockSpec((JB_ST // STATE, STATE), grp), pl.BlockSpec((JB_ST // STATE, STATE), grp),
                   pl.BlockSpec((1, JB_CH), row1),
                   HBM_SPEC, HBM_SPEC),
        scratch_shapes=[pltpu.VMEM((n_seq, tt, JB_ST), F32), pltpu.VMEM((n_seq, tt, JB_ST), F32),
                        pltpu.VMEM((n_seq, tt, JB_CH), F32), pltpu.VMEM((n_seq, tt, JB_CH), F32),
                        pltpu.VMEM((n_seq, SUBLANES, JB_ST), F32), pltpu.VMEM((2, 1, JB_ST), F32),
                        pltpu.SemaphoreType.DMA((7 * 2,)), pltpu.SemaphoreType.DMA((7 * 2,)),
                        pltpu.SemaphoreType.DMA((2,))],
        compiler_params=_params(2),
    )(dy, u, s_re, s_im, s_re, s_im, bb_re, bb_im, c_re_t, c_imn_t, d_row, ab_re, ab_im, g_out, g_glu)


def _mix(x2, tgt2, y, proj, gf, b_glu, conv8, w_glu_f, w_out_f, seq):
    n = x2.shape[0]
    tm = TOK_TILE
    tiles_per_seq = seq // tm
    rows8 = tm // SUBLANES

    def body(x_ref, t_ref, y_ref, zs_ref, h_ref, bc_ref, cc_ref, zc_ref, hp_ref, ccp_ref,
             gf_ref, bg_ref, cw_ref, wg_ref, wo_ref,
             dh2_ref, dy_ref, dzs_ref, dbc_ref, dzc_ref, dyc_ref,
             dwo_ref, dwg_ref, loss_ref, dgf_ref, dbg_ref, dcw_ref):
        i = pl.program_id(0)

        @pl.when(i == 0)
        def _():
            for r in (dwo_ref, dwg_ref, loss_ref, dgf_ref, dbg_ref, dcw_ref):
                r[...] = jnp.zeros_like(r)

        yv = y_ref[...]
        y1, dgelu = _gelu_and_grad(yv)
        y1b = y1.astype(BF16)
        gate = _sigmoid(_dot(y1b, wg_ref[...]) + bg_ref[...])
        y2 = y1 * gate
        szs, dszs = _silu_and_grad(zs_ref[...])
        yssm = y2 * szs
        hv = h_ref[...]
        ccv = cc_ref[...]
        bcv = bc_ref[...]
        v = ccv * hv
        first = (i % tiles_per_seq) == 0
        vhalo = jnp.where(first, 0.0, ccp_ref[...] * hp_ref[...])
        v1 = _shift_down(v, vhalo, 1)
        v2 = _shift_down(v, vhalo, 2)
        w0 = cw_ref[0:1, :]
        w1 = cw_ref[1:2, :]
        w2 = cw_ref[2:3, :]
        yc = w0 * v2 + w1 * v1 + w2 * v
        szc, dszc = _silu_and_grad(zc_ref[...])
        yconv = (bcv * yc) * szc
        ysb = yssm.astype(BF16)
        ycb = yconv.astype(BF16)
        h2 = x_ref[...] + _dot(ysb, wo_ref[0:SSM_W, :]) + _dot(ycb, wo_ref[SSM_W:, :])
        r2 = lax.rsqrt(jnp.mean(h2 * h2, axis=-1, keepdims=True) + EPS)
        hn = h2 * r2
        gfv = gf_ref[...]
        err = hn * gfv - t_ref[...]
        loss_ref[...] += 0.5 * jnp.sum(jnp.mean(err * err, axis=-1, keepdims=True))
        dout = err * (1.0 / D_MODEL)
        dgf_ref[...] += jnp.sum(dout * hn, axis=0, keepdims=True)
        dn = dout * gfv
        dh2 = r2 * (dn - hn * jnp.mean(dn * hn, axis=-1, keepdims=True))
        dh2_ref[...] = dh2
        dh2b = dh2.astype(BF16)
        dwo_ref[0:SSM_W, :] += _dot_tn(ysb, dh2b)
        dwo_ref[SSM_W:, :] += _dot_tn(ycb, dh2b)
        dyssm = _dot_nt(dh2b, wo_ref[0:SSM_W, :])
        dyconv = _dot_nt(dh2b, wo_ref[SSM_W:, :])
        dy2 = dyssm * szs
        dzs_ref[...] = (dyssm * y2 * dszs).astype(BF16)
        dgp = dy2 * y1 * (gate * (1.0 - gate))
        dgpb = dgp.astype(BF16)
        dy1 = dy2 * gate + _dot_nt(dgpb, wg_ref[...])
        dwg_ref[...] += _dot_tn(y1b, dgpb)
        dbg_ref[...] += jnp.sum(dgp, axis=0, keepdims=True)
        dy_ref[...] = dy1 * dgelu
        dbc_ref[...] = (dyconv * yc * szc).astype(BF16)
        dyc = dyconv * bcv * szc
        dyc_ref[...] = dyc
        dzc_ref[...] = (dyconv * bcv * yc * dszc).astype(BF16)
        dcw_ref[0:1, :] += jnp.sum(dyc * v2, axis=0, keepdims=True)
        dcw_ref[1:2, :] += jnp.sum(dyc * v1, axis=0, keepdims=True)
        dcw_ref[2:3, :] += jnp.sum(dyc * v, axis=0, keepdims=True)

    tile_d = pl.BlockSpec((tm, D_MODEL), lambda i: (i, 0))
    tile_s = pl.BlockSpec((tm, SSM_W), lambda i: (i, 0))
    seg_of = lambda c: pl.BlockSpec((tm, SSM_W), lambda i: (i, c))
    halo_of = lambda c: pl.BlockSpec((SUBLANES, SSM_W), lambda i: (jnp.maximum(i * rows8 - 1, 0), c))
    const = lambda shape: pl.BlockSpec(shape, lambda i: (0,) * len(shape))
    seg = _out((n, SSM_W), F32)
    seg_b = _out((n, SSM_W), BF16)
    return _pcall(
        body, name="mix", grid=(n // tm,),
        out_shape=(_out((n, D_MODEL), F32), seg, seg_b, seg_b, seg_b, seg,
                   _out((D_MODEL, D_MODEL), F32), _out((SSM_W, SSM_W), F32),
                   _out((SUBLANES, LANES), F32), _out((1, D_MODEL), F32),
                   _out((1, SSM_W), F32), _out((SUBLANES, CONV_W), F32)),
        in_specs=[tile_d, tile_d, tile_s, seg_of(SEG_ZS), seg_of(SEG_H), seg_of(SEG_BC), seg_of(SEG_CC), seg_of(SEG_ZC),
                  halo_of(SEG_H), halo_of(SEG_CC),
                  const((1, D_MODEL)), const((1, SSM_W)), const((SUBLANES, CONV_W)),
                  const((SSM_W, SSM_W)), const((D_MODEL, D_MODEL))],
        out_specs=(tile_d, tile_s, tile_s, tile_s, tile_s, tile_s,
                   const((D_MODEL, D_MODEL)), const((SSM_W, SSM_W)), const((SUBLANES, LANES)),
                   const((1, D_MODEL)), const((1, SSM_W)), const((SUBLANES, CONV_W))),
        compiler_params=_params(1),
    )(x2, tgt2, y, proj, proj, proj, proj, proj, proj, proj, gf, b_glu, conv8, w_glu_f, w_out_f)


def _in_bwd(x2, dh2, du, dzs, dyc, proj, dbc, dzc, g1, conv8, w_full, seq):
    n = x2.shape[0]
    tm = TOK_TILE
    n_tiles = n // tm
    tiles_per_seq = seq // tm
    rows8 = tm // SUBLANES
    n_blk8 = n // SUBLANES

    def body(x_ref, dh2_ref, du_ref, dzs_ref, dyc_ref, dycn_ref, h_ref, cc_ref, dbc_ref, dzc_ref,
             g_ref, cw_ref, w_ref, gx_ref, dp_ref, dg_ref):
        i = pl.program_id(0)

        @pl.when(i == 0)
        def _():
            dg_ref[...] = jnp.zeros_like(dg_ref)

        dyc = dyc_ref[...]
        last = (i % tiles_per_seq) == tiles_per_seq - 1
        nhalo = jnp.where(last, 0.0, dycn_ref[...])
        dv = (cw_ref[2:3, :] * dyc + cw_ref[1:2, :] * _shift_up(dyc, nhalo, 1)
              + cw_ref[0:1, :] * _shift_up(dyc, nhalo, 2))
        parts = (du_ref[...], dzs_ref[...], dv * cc_ref[...], dbc_ref[...], dv * h_ref[...], dzc_ref[...])
        dxn = jnp.zeros((tm, D_MODEL), F32)
        for k, p in enumerate(parts):
            pb = p.astype(BF16)
            dp_ref[:, k * SSM_W:(k + 1) * SSM_W] = pb
            dxn = dxn + _dot_nt(pb, w_ref[:, k * SSM_W:(k + 1) * SSM_W])
        x = x_ref[...]
        r = lax.rsqrt(jnp.mean(x * x, axis=-1, keepdims=True) + EPS)
        xh = x * r
        dg_ref[...] += jnp.sum(dxn * xh, axis=0, keepdims=True)
        dn = dxn * g_ref[...]
        gx_ref[...] = dh2_ref[...] + r * (dn - xh * jnp.mean(dn * xh, axis=-1, keepdims=True))

    tile_d = pl.BlockSpec((tm, D_MODEL), lambda i: (i, 0))
    tile_s = pl.BlockSpec((tm, SSM_W), lambda i: (i, 0))
    seg_of = lambda c: pl.BlockSpec((tm, SSM_W), lambda i: (i, c))
    nhalo = pl.BlockSpec((SUBLANES, SSM_W), lambda i: (jnp.minimum((i + 1) * rows8, n_blk8 - 1), 0))
    const = lambda shape: pl.BlockSpec(shape, lambda i: (0,) * len(shape))
    return _pcall(
        body, name="in_bwd", grid=(n_tiles,),
        out_shape=(_out((n, D_MODEL), F32), _out((n, IN_COLS), BF16),
                   _out((SUBLANES, D_MODEL), F32)),
        in_specs=[tile_d, tile_d, tile_s, tile_s, tile_s, nhalo, seg_of(SEG_H), seg_of(SEG_CC), tile_s, tile_s,
                  const((1, D_MODEL)), const((SUBLANES, CONV_W)), const((D_MODEL, IN_COLS))],
        out_specs=(tile_d, pl.BlockSpec((tm, IN_COLS), lambda i: (i, 0)), const((SUBLANES, D_MODEL))),
        compiler_params=_params(1),
    )(x2, dh2, du, dzs, dyc, dyc, proj, proj, dbc, dzc, g1, conv8, w_full)


_HALF_BLOCKS = ((0, 0), (0, 1), (1, 0), (2, 0), (1, 1), (2, 1), (3, 0), (3, 1))


def _dw_in_exchange(chips, xn, dproj, smalls):
    n = xn.shape[0]
    tk = min(1024, n)
    nk = n // tk
    piece = (D_MODEL, COLS_PER_DEV)
    hr = D_MODEL // 2
    n_half = len(_HALF_BLOCKS)
    n_small = len(smalls)
    assert _HALF_BLOCKS[0][1] == 0 and _HALF_BLOCKS[1][1] == 1
    order = jnp.stack([chips[b] for b, _ in _HALF_BLOCKS]
                      + [jnp.int32(t) for _, t in _HALF_BLOCKS]).astype(jnp.int32)

    def body(order_ref, xn_hbm, dp_ref, *refs):
        sm_refs = refs[:n_small]
        own_ref, rchip_ref = refs[n_small:n_small + 2]
        rsm_refs = refs[n_small + 2:2 * n_small + 2]
        (xn_ref, acc, stage, rbuf, kbuf, relay_in, xn_sems, give_send, give_recv, keep_send, keep_recv,
         relay_send, relay_recv, sm_send, sm_recv, sm_loc) = refs[2 * n_small + 2:]
        s = pl.program_id(0)

        def xn_copy(kk, t):
            rows = pl.ds(pl.multiple_of(kk * tk, tk), tk)
            return pltpu.make_async_copy(xn_hbm.at[rows, t * hr:(t + 1) * hr], xn_ref.at[t, rows, :],
                                         xn_sems.at[2 * kk + t])

        @pl.when(s == 0)
        def _():
            for kk in range(nk):
                for t in range(2):
                    xn_copy(kk, t).start()
            xn_copy(0, 0).wait()

        @pl.when(s == 1)
        def _():
            xn_copy(0, 1).wait()

        x, y, c = _mesh_pos()
        sib = (x, y, 1 - c)
        y_nbr, x_nbr = (x, 1 - y, c), (1 - x, y, c)
        gather = _TwoLevelGather(list(sm_refs), [functools.partial(lambda r, dev: r.at[dev], r) for r in rsm_refs],
                                 sm_send, sm_recv, sm_loc)

        def give(h):
            cols = pl.ds(pl.multiple_of((1 - c) * COLS_PER_DEV, LANES), COLS_PER_DEV)
            return pltpu.make_async_remote_copy(src_ref=acc.at[h % 2, :, cols], dst_ref=stage.at[h],
                                                send_sem=give_send.at[h], recv_sem=give_recv.at[h],
                                                device_id=sib, device_id_type=MESH)

        def relay(r):
            return pltpu.make_async_remote_copy(src_ref=rbuf.at[r], dst_ref=relay_in.at[r],
                                                send_sem=relay_send.at[r], recv_sem=relay_recv.at[r],
                                                device_id=(x_nbr, y_nbr)[r], device_id_type=MESH)

        def keep(q):
            return pltpu.make_async_remote_copy(src_ref=kbuf.at[q], dst_ref=rchip_ref.at[q // 2, pl.ds((q % 2) * hr, hr), :],
                                                send_sem=keep_send.at[q], recv_sem=keep_recv.at[q],
                                                device_id=(y_nbr, x_nbr)[q // 2], device_id_type=MESH)

        def chip_sum(h):
            give(h).wait_recv()
            mine = [acc[h % 2, :, cc * COLS_PER_DEV:(cc + 1) * COLS_PER_DEV] for cc in range(2)]
            return jnp.where(c == 0, mine[0], mine[1]) + stage[h]

        @pl.when(s == 0)
        def _():
            gather.start()

        @pl.when(s == 2)
        def _():
            gather.neighbours_landed()

        @pl.when(s == n_half - 2)
        def _():
            gather.diagonal_landed()

        for k in range(2, n_half):
            @pl.when(s == k)
            def _(k=k):
                give(k - 2).wait_send()

        slot = s % 2
        t_half = order_ref[n_half + s]
        acc[slot] = _dot_tn(xn_ref[t_half, pl.ds(0, tk), :], dp_ref[pl.ds(0, tk), :])

        def kstep(kk, carry):
            for t in range(2):
                @pl.when(s == t)
                def _(t=t):
                    xn_copy(kk, t).wait()

            off = pl.multiple_of(kk * tk, tk)
            acc[slot] += _dot_tn(xn_ref[t_half, pl.ds(off, tk), :], dp_ref[pl.ds(off, tk), :])
            return carry

        n_first = max(1, nk // 2)
        lax.fori_loop(1, n_first, kstep, 0)
        for k in range(1, n_half):
            @pl.when(s == k)
            def _(k=k):
                h = k - 1
                b, t = _HALF_BLOCKS[h]
                total = chip_sum(h)
                if b == 0:
                    rbuf[t] = total.astype(BF16)
                    relay(t).start()
                elif b < 3:
                    if (b, t) in ((1, 0), (2, 1)):
                        relay(t).wait_recv()
                        total = total + relay_in[t].astype(F32)
                    q = 2 * (b - 1) + t
                    kbuf[q] = total.astype(BF16)
                    keep(q).start()
                else:
                    own_ref[0:hr, :] = total

        lax.fori_loop(n_first, nk, kstep, 0)

        for k in range(n_half):
            @pl.when(s == k)
            def _(k=k):
                give(k).start()

        @pl.when(s == n_half - 1)
        def _():
            own_ref[hr:D_MODEL, :] = chip_sum(n_half - 1)
            give(n_half - 2).wait_send()
            give(n_half - 1).wait_send()
            for r in range(2):
                relay(r).wait_send()
            for q in range(4):
                keep(q).wait()
            gather.finish()

    half_piece = (hr, COLS_PER_DEV)
    grid_spec = pltpu.PrefetchScalarGridSpec(
        num_scalar_prefetch=1, grid=(n_half,),
        in_specs=[HBM_SPEC,
                  pl.BlockSpec((n, COLS_PER_CHIP), lambda s, order: (0, order[s])),
                  *([HBM_SPEC] * n_small)],
        out_specs=(pl.BlockSpec(piece, lambda s, order: (0, 0)), HBM_SPEC, *([HBM_SPEC] * n_small)),
        scratch_shapes=[pltpu.VMEM((2, n, hr), BF16),
                        pltpu.VMEM((2, hr, COLS_PER_CHIP), F32), pltpu.VMEM((n_half,) + half_piece, F32),
                        pltpu.VMEM((2,) + half_piece, BF16), pltpu.VMEM((4,) + half_piece, BF16),
                        pltpu.VMEM((2,) + half_piece, BF16),
                        pltpu.SemaphoreType.DMA((2 * nk,)),
                        pltpu.SemaphoreType.DMA((n_half,)), pltpu.SemaphoreType.DMA((n_half,)),
                        pltpu.SemaphoreType.DMA((4,)), pltpu.SemaphoreType.DMA((4,)),
                        pltpu.SemaphoreType.DMA((2,)), pltpu.SemaphoreType.DMA((2,)),
                        pltpu.SemaphoreType.DMA((7 * n_small,)), pltpu.SemaphoreType.DMA((7 * n_small,)),
                        pltpu.SemaphoreType.DMA((n_small,))])
    return _pcall(
        body, name="dw_in_exchange", grid_spec=grid_spec,
        out_shape=(_out(piece, F32), _out((2,) + piece, BF16),
                   *(_out((N_DEV,) + a.shape, a.dtype) for a in smalls)),
        compiler_params=_params(1),
    )(order, xn, dproj, *smalls)


def _adamw(g, w, m, v):
    m_new = ADAM_B1 * m + (1.0 - ADAM_B1) * g
    v_new = ADAM_B2 * v + (1.0 - ADAM_B2) * (g * g)
    m_hat = m_new / (1.0 - ADAM_B1 ** ADAM_STEP)
    v_hat = v_new / (1.0 - ADAM_B2 ** ADAM_STEP)
    delta = -ADAM_LR * (m_hat / (jnp.sqrt(v_hat) + ADAM_EPS) + ADAM_WD * w)
    return delta, m_new, v_new


def _reduce_adam_w_in(own, rchip, w, m, v):
    rows, cols = w.shape
    row_tile = 256

    def body(o_ref, r_ref, w_ref, m_ref, v_ref, g_ref, d_ref, nm_ref, nv_ref):
        g = o_ref[...]
        for s in range(2):
            g = g + r_ref[s].astype(F32)
        g_ref[...] = g
        d_ref[...], nm_ref[...], nv_ref[...] = _adamw(g, w_ref[...], m_ref[...], v_ref[...])

    tile = pl.BlockSpec((row_tile, cols), lambda i: (i, 0))
    shp = _out((rows, cols), F32)
    return _pcall(
        body, name="reduce_adam_w_in", grid=(rows // row_tile,),
        out_shape=(shp,) * 4,
        in_specs=[tile, pl.BlockSpec((2, row_tile, cols), lambda i: (0, i, 0)), tile, tile, tile],
        out_specs=(tile,) * 4,
        compiler_params=_params(1),
    )(own, rchip, w, m, v)


_SMALL_LEAVES = ("norm_gain", "final_norm_gain", "b_glu", "ssm_a_re", "ssm_a_im", "ssm_log_dt", "ssm_d", "conv_w",
                 "ssm_c_re", "ssm_c_im", "ssm_b_re", "ssm_b_im")


def _reduce_adam_small(r_pack, r_gc, r_gb, wmv, sharded):
    n_leaf = len(_SMALL_LEAVES)
    n_sh = len(sharded)

    def body(*refs):
        rp_ref, rgc_ref, rgb_ref = refs[:3]
        w_refs = refs[3:3 + 3 * n_leaf]
        sh_in = refs[3 + 3 * n_leaf:3 + 3 * n_leaf + 4 * n_sh]
        outs0 = 3 + 3 * n_leaf + 4 * n_sh
        loss_ref = refs[outs0]
        o_refs = refs[outs0 + 1:outs0 + 1 + 4 * n_leaf]
        sh_out = refs[outs0 + 1 + 4 * n_leaf:outs0 + 1 + 4 * n_leaf + 4 * n_sh]
        own_conv = refs[-1]

        def total(ref):
            acc = ref[0].astype(F32)
            for s in range(1, N_DEV):
                acc = acc + ref[s].astype(F32)
            return acc

        for i in range(n_sh):
            r_ref, w_ref, m_ref, v_ref = sh_in[4 * i:4 * i + 4]
            o_g, o_d, o_m, o_v = sh_out[4 * i:4 * i + 4]
            g = total(r_ref)
            o_g[...] = g
            o_d[...], o_m[...], o_v[...] = _adamw(g, w_ref[...], m_ref[...], v_ref[...])

        sp = total(rp_ref)
        sgc = total(rgc_ref)
        sgb = total(rgb_ref)
        loss_ref[...] = sp[ROW_LOSS:ROW_LOSS + 1, 0:1]

        def wide(r):
            return jnp.concatenate([sp[r:r + 1, :], sp[r + 1:r + 2, :]], axis=1)

        s5 = slice(ROW_S5, ROW_S5 + N_GROUPS)
        eye = (lax.broadcasted_iota(jnp.int32, (N_GROUPS, N_GROUPS), 0)
               == lax.broadcasted_iota(jnp.int32, (N_GROUPS, N_GROUPS), 1)).astype(F32)
        d_rows = jnp.broadcast_to(sp[ROW_BGLU_D + 1:ROW_BGLU_D + 2, :], (GROUP, SSM_W))
        own_p = (lax.broadcasted_iota(jnp.int32, (GROUP, SSM_W), 1) % GROUP
                 == lax.broadcasted_iota(jnp.int32, (GROUP, SSM_W), 0))
        of_group = (lax.broadcasted_iota(jnp.int32, (SSM_W, N_GROUPS), 0) // GROUP
                    == lax.broadcasted_iota(jnp.int32, (SSM_W, N_GROUPS), 1)).astype(BF16)
        d_pg = sum(_dot(t, of_group) for t in _split3(jnp.where(own_p, d_rows, 0.0)))
        me = 4 * lax.axis_index("x") + 2 * lax.axis_index("y") + lax.axis_index("c")
        for k in range(N_DEV):
            @pl.when(me == k)
            def _(k=k):
                own_conv[...] = sp[ROW_CONV:ROW_CONV + SUBLANES, k * CONV_COLS_PER_DEV:(k + 1) * CONV_COLS_PER_DEV]
        grads = {
            "norm_gain": wide(ROW_NORM_GAIN),
            "final_norm_gain": wide(ROW_FINAL_GAIN),
            "b_glu": sp[ROW_BGLU_D:ROW_BGLU_D + 1, :],
            "ssm_a_re": sp[s5, LANE_A_RE:LANE_A_RE + STATE],
            "ssm_a_im": sp[s5, LANE_A_IM:LANE_A_IM + STATE],
            "ssm_log_dt": jnp.sum(sp[s5, LANE_LOG_DT:LANE_LOG_DT + 1] * eye, axis=0, keepdims=True),
            "ssm_d": d_pg,
            "ssm_c_re": sgc[:, 0:STATE],
            "ssm_c_im": sgc[:, STATE:2 * STATE],
            "ssm_b_re": sgb[:, 0:STATE],
            "ssm_b_im": sgb[:, STATE:2 * STATE],
        }
        for i, name in enumerate(_SMALL_LEAVES):
            w_ref, m_ref, v_ref = w_refs[3 * i:3 * i + 3]
            o_g, o_d, o_m, o_v = o_refs[4 * i:4 * i + 4]
            if name == "conv_w":
                for k in range(w_ref.shape[0]):
                    g = own_conv[k:k + 1, :]
                    o_g[k] = g
                    o_d[k], o_m[k], o_v[k] = _adamw(g, w_ref[k], m_ref[k], v_ref[k])
                continue
            g = grads[name]
            o_g[...] = g
            o_d[...], o_m[...], o_v[...] = _adamw(g, w_ref[...], m_ref[...], v_ref[...])

    flat_w = [a for name in _SMALL_LEAVES for a in wmv[name]]
    leaf_shapes = [_out(wmv[name][0].shape, F32) for name in _SMALL_LEAVES for _ in range(4)]
    sh_shapes = [_out(entry[1].shape, F32) for entry in sharded for _ in range(4)]
    operands = (r_pack, r_gc, r_gb, *flat_w, *(a for entry in sharded for a in entry))
    out_shape = (_out((1, 1), F32), *leaf_shapes, *sh_shapes)
    outs = _pcall(
        body, name="reduce_adam_small", grid=(1,), out_shape=out_shape,
        in_specs=_whole_specs(operands), out_specs=tuple(_whole_specs(out_shape)),
        scratch_shapes=[pltpu.VMEM((SUBLANES, CONV_COLS_PER_DEV), F32)],
        compiler_params=_params(1),
    )(*operands)
    leaves = {name: outs[1 + 4 * i:5 + 4 * i] for i, name in enumerate(_SMALL_LEAVES)}
    first = 1 + 4 * n_leaf
    return outs[0], leaves, [outs[first + 4 * i:first + 4 * i + 4] for i in range(n_sh)]


def kernel(x, norm_gain, w_in, ssm_a_re, ssm_a_im, ssm_log_dt, ssm_b_re, ssm_b_im, ssm_c_re, ssm_c_im, ssm_d, w_glu, b_glu, conv_w, w_out, final_norm_gain, loss_target, m_norm_gain, m_w_in, m_ssm_a_re, m_ssm_a_im, m_ssm_log_dt, m_ssm_b_re, m_ssm_b_im, m_ssm_c_re, m_ssm_c_im, m_ssm_d, m_w_glu, m_b_glu, m_conv_w, m_w_out, m_final_norm_gain, v_norm_gain, v_w_in, v_ssm_a_re, v_ssm_a_im, v_ssm_log_dt, v_ssm_b_re, v_ssm_b_im, v_ssm_c_re, v_ssm_c_im, v_ssm_d, v_w_glu, v_b_glu, v_conv_w, v_w_out, v_final_norm_gain):
    n_seq, seq, _ = x.shape
    n = n_seq * seq

    gh_p = lambda b4: jnp.transpose(b4, (0, 1, 3, 2)).reshape(N_GROUPS * GROUP, STATE)
    c2 = lambda a: a.reshape(N_GROUPS * GROUP, STATE)
    b_re2, b_im2 = gh_p(ssm_b_re), gh_p(ssm_b_im)
    d_row = ssm_d[0].reshape(1, SSM_W)

    x2 = x.reshape(n, D_MODEL)
    tgt2 = loss_target.reshape(n, D_MODEL)
    mx, my, mc = lax.axis_index("x"), lax.axis_index("y"), lax.axis_index("c")
    chip_ids = [2 * cx + cy for cx, cy in ((mx, my), (1 - mx, my), (mx, 1 - my), (1 - mx, 1 - my))]
    arrival = chip_ids
    xn, proj, w_in_f, s5 = _in_proj(
        jnp.stack(arrival).astype(jnp.int32), x2, norm_gain, w_in[0].astype(BF16),
        (ssm_a_re[0], ssm_a_im[0], ssm_log_dt, b_re2, b_im2, c2(ssm_c_re), c2(ssm_c_im)))
    a_re_x, a_im_x, log_dt_x, ab_re, ab_im, bb_re_m, bb_im_m, c_re_m, c_imn_m = s5
    u3 = proj.reshape(n_seq, seq, IN_COLS)
    conv_p = jnp.pad(conv_w[0], ((0, SUBLANES - 3), (0, LANES - CONV_COLS_PER_DEV)))
    s_re, s_im, y3, w_out_f, w_glu_f, conv_all = _ssm_fwd(
        u3, bb_re_m, bb_im_m, c_re_m, c_imn_m, d_row, ab_re, ab_im,
        w_out[0], w_glu[0], conv_p, n_seq, seq)
    conv8 = jnp.transpose(conv_all[:, :, :CONV_COLS_PER_DEV], (1, 0, 2)).reshape(SUBLANES, CONV_W)
    (dh2, dy, dzs, dbc, dzc, dyc, dw_out, dw_glu, loss_t, dgf, dbg, dcw) = _mix(
        x2, tgt2, y3.reshape(n, SSM_W), proj, final_norm_gain.reshape(1, D_MODEL), b_glu, conv8,
        w_glu_f, w_out_f, seq)

    du3, dc_re_d, dc_im_d, dbb_re_d, dbb_im_d, dab_re, dab_im, dd, r_out, r_glu = _ssm_bwd(
        dy.reshape(n_seq, seq, SSM_W), u3, s_re, s_im, bb_re_m, bb_im_m, c_re_m, c_imn_m, d_row, ab_re, ab_im,
        dw_out.reshape(N_DEV, OUT_ROWS_PER_DEV, D_MODEL), dw_glu.reshape(N_DEV, GLU_ROWS_PER_DEV, SSM_W), n_seq, seq)
    du = du3.reshape(n, SSM_W)
    grad_x2, dproj, dg8 = _in_bwd(x2, dh2, du, dzs, dyc, proj, dbc, dzc, norm_gain, conv8, w_in_f, seq)
    pack, gc, gb = _ssm_disc_bwd_pack(
        a_re_x, a_im_x, log_dt_x, b_re2, b_im2, dab_re, dab_im,
        dbb_re_d, dbb_im_d, loss_t, dg8, dgf, dbg, dd, dcw, dc_re_d, dc_im_d)

    own_in, rchip_in, r_pack, r_gc, r_gb = _dw_in_exchange(
        [chip_ids[3], chip_ids[2], chip_ids[1], chip_ids[0]],
        xn, dproj, [pack, gc, gb])

    flat2 = lambda a: a.reshape(a.shape[-2:]) if a.ndim > 2 else a.reshape(1, -1)
    c2 = lambda a: a.reshape(N_GROUPS * GROUP, STATE)
    wmv = dict(norm_gain=(norm_gain, m_norm_gain, v_norm_gain),
               final_norm_gain=tuple(flat2(a) for a in (final_norm_gain, m_final_norm_gain, v_final_norm_gain)),
               b_glu=(b_glu, m_b_glu, v_b_glu),
               ssm_a_re=tuple(flat2(a) for a in (ssm_a_re, m_ssm_a_re, v_ssm_a_re)),
               ssm_a_im=tuple(flat2(a) for a in (ssm_a_im, m_ssm_a_im, v_ssm_a_im)),
               ssm_log_dt=(ssm_log_dt, m_ssm_log_dt, v_ssm_log_dt),
               ssm_d=tuple(jnp.transpose(a, (0, 2, 1)).reshape(GROUP, N_GROUPS) for a in (ssm_d, m_ssm_d, v_ssm_d)),
               conv_w=tuple(jnp.transpose(a, (1, 0, 2)) for a in (conv_w, m_conv_w, v_conv_w)),
               ssm_c_re=tuple(c2(a) for a in (ssm_c_re, m_ssm_c_re, v_ssm_c_re)),
               ssm_c_im=tuple(c2(a) for a in (ssm_c_im, m_ssm_c_im, v_ssm_c_im)),
               ssm_b_re=(b_re2, gh_p(m_ssm_b_re), gh_p(v_ssm_b_re)),
               ssm_b_im=(b_im2, gh_p(m_ssm_b_im), gh_p(v_ssm_b_im)))

    res_in = _reduce_adam_w_in(own_in, rchip_in, w_in[0], m_w_in[0], v_w_in[0])
    loss11, small, (res_out, res_glu) = _reduce_adam_small(
        r_pack, r_gc, r_gb, wmv,
        [(r_out, w_out[0], m_w_out[0], v_w_out[0]), (r_glu, w_glu[0], m_w_glu[0], v_w_glu[0])])
    loss = loss11.reshape(())

    shapes = dict(norm_gain=(1, D_MODEL), ssm_a_re=(1, N_GROUPS, STATE), ssm_a_im=(1, N_GROUPS, STATE),
                  ssm_log_dt=(1, N_GROUPS), ssm_c_re=(1, N_GROUPS, GROUP, STATE), ssm_c_im=(1, N_GROUPS, GROUP, STATE),
                  b_glu=(1, SSM_W), final_norm_gain=(D_MODEL,))
    big = dict(w_in=res_in, w_glu=res_glu, w_out=res_out)

    def leaf(kind, name):
        if name in big:
            return big[name][kind][None]
        if name in ("ssm_b_re", "ssm_b_im"):
            return jnp.transpose(small[name][kind].reshape(1, N_GROUPS, GROUP, STATE), (0, 1, 3, 2))
        if name == "ssm_d":
            return jnp.transpose(small[name][kind].reshape(1, GROUP, N_GROUPS), (0, 2, 1))
        if name == "conv_w":
            return jnp.transpose(small[name][kind], (1, 0, 2))
        return small[name][kind].reshape(shapes[name])

    order = ["norm_gain", "w_in", "ssm_a_re", "ssm_a_im", "ssm_log_dt", "ssm_b_re", "ssm_b_im", "ssm_c_re",
             "ssm_c_im", "ssm_d", "w_glu", "b_glu", "conv_w", "w_out", "final_norm_gain"]
    outs = [loss, grad_x2.reshape(x.shape)]
    for kind in range(4):
        outs += [leaf(kind, name) for name in order]
    return tuple(outs)
```

```python
import functools
import math

import jax
import jax.numpy as jnp
from jax import lax
from jax.experimental import pallas as pl
from jax.experimental.pallas import tpu as pltpu

F32 = jnp.float32
BF16 = jnp.bfloat16

N_DEV = 8
D_MODEL = 1024
SSM_W = 512
CONV_W = 512
N_GROUPS = 32
GROUP = 16
STATE = 64
IN_COLS = 3072
SEG_U, SEG_ZS, SEG_H, SEG_BC, SEG_CC, SEG_ZC = range(6)
COLS_PER_DEV = IN_COLS // N_DEV
N_CHIP = N_DEV // 2
COLS_PER_CHIP = 2 * COLS_PER_DEV
OUT_ROWS_PER_DEV = D_MODEL // N_DEV
GLU_ROWS_PER_DEV = SSM_W // N_DEV
CONV_COLS_PER_DEV = CONV_W // N_DEV
EPS = 1e-6

N_JBLK = 4
JB_CH = SSM_W // N_JBLK
JB_ST = N_GROUPS * STATE // N_JBLK

ADAM_LR = 0.001
ADAM_B1 = 0.9
ADAM_B2 = 0.999
ADAM_EPS = 1e-08
ADAM_WD = 0.01
ADAM_STEP = 10

SUBLANES = 8
LANES = 128
VMEM_LIMIT = 48 * 1024 * 1024
TOK_TILE = 256
IN_TILE = 1024
SCAN_TILE = 1024

MESH = pl.DeviceIdType.MESH
HBM_SPEC = pl.BlockSpec(memory_space=pltpu.HBM)


def _build(body, **kw):
    return pl.pallas_call(body, **kw)


def _pcall(body, **kw):
    def call(*operands):
        pinned = [a if jnp.issubdtype(a.dtype, jnp.integer) else pltpu.with_memory_space_constraint(a, pltpu.HBM)
                  for a in operands]
        return _build(body, **kw)(*pinned)
    return call


def _whole_specs(arrays):
    return [pl.BlockSpec(a.shape, functools.partial(lambda nd, i: (0,) * nd, len(a.shape))) for a in arrays]


def _out(shape, dtype):
    return pltpu.HBM(tuple(shape), dtype)


def _params(n_grid):
    return pltpu.CompilerParams(dimension_semantics=("arbitrary",) * n_grid,
                                vmem_limit_bytes=VMEM_LIMIT)


def _dot(a, b):
    return jnp.dot(a, b, preferred_element_type=F32)


def _dot_nt(a, b):
    return lax.dot_general(a, b, (((1,), (1,)), ((), ())), preferred_element_type=F32)


def _dot_tn(a, b):
    return lax.dot_general(a, b, (((0,), (0,)), ((), ())), preferred_element_type=F32)


def _sigmoid(z):
    return 1.0 / (1.0 + jnp.exp(-z))


_GELU_C = math.sqrt(2.0 / math.pi)


def _gelu_and_grad(y):
    inner = _GELU_C * (y + 0.044715 * (y * y * y))
    t = jnp.tanh(inner)
    g = 0.5 * y * (1.0 + t)
    dg = 0.5 * (1.0 + t) + 0.5 * y * (1.0 - t * t) * (_GELU_C * (1.0 + 3.0 * 0.044715 * (y * y)))
    return g, dg


def _silu_and_grad(z):
    s = _sigmoid(z)
    return z * s, s * (1.0 + z * (1.0 - s))


def _shift_down(v, halo, k):
    rolled = pltpu.roll(v, k, 0)
    row = lax.broadcasted_iota(jnp.int32, v.shape, 0)
    for r in range(k):
        rolled = jnp.where(row == r, halo[SUBLANES - k + r:SUBLANES - k + r + 1, :], rolled)
    return rolled


def _shift_up(v, halo, k):
    n = v.shape[0]
    rolled = pltpu.roll(v, n - k, 0)
    row = lax.broadcasted_iota(jnp.int32, v.shape, 0)
    for r in range(k):
        rolled = jnp.where(row == n - k + r, halo[r:r + 1, :], rolled)
    return rolled


def _mesh_pos():
    return lax.axis_index("x"), lax.axis_index("y"), lax.axis_index("c")


def _direct_copies(srcs_for, out_refs, send_sems, recv_sems, loc_sems):
    x, y, c = _mesh_pos()
    me_id = 4 * x + 2 * y + c
    n_arr = len(out_refs)
    dsts = [r.at[me_id] for r in out_refs]
    own = srcs_for(me_id)
    mine = [pltpu.make_async_copy(own[a], dsts[a], loc_sems.at[a]) for a in range(n_arr)]
    sends = []
    for k in range(1, N_DEV):
        px, py, pc = x ^ ((k >> 2) & 1), y ^ ((k >> 1) & 1), c ^ (k & 1)
        src = srcs_for(4 * px + 2 * py + pc)
        for a in range(n_arr):
            sends.append(pltpu.make_async_remote_copy(
                src_ref=src[a], dst_ref=dsts[a],
                send_sem=send_sems.at[(k - 1) * n_arr + a], recv_sem=recv_sems.at[(k - 1) * n_arr + a],
                device_id=(px, py, pc), device_id_type=MESH))
    return mine, sends


class _TwoLevelGather:
    def __init__(self, srcs, slots, send_sems, recv_sems, loc_sems):
        self.srcs, self.slots, self.n_arr = srcs, slots, len(srcs)
        self.send_sems, self.recv_sems, self.loc_sems = send_sems, recv_sems, loc_sems
        x, y, c = _mesh_pos()
        self.c = c
        self.me, self.sib = (x, y, c), (x, y, 1 - c)
        self.chips = [(1 - x, y), (x, 1 - y), (1 - x, 1 - y)]

    def _copies(self, k, block, to, from_src=False):
        dev = 4 * block[0] + 2 * block[1] + block[2]
        return [pltpu.make_async_remote_copy(
            src_ref=self.srcs[a] if from_src else self.slots[a](dev), dst_ref=self.slots[a](dev),
            send_sem=self.send_sems.at[k * self.n_arr + a], recv_sem=self.recv_sems.at[k * self.n_arr + a],
            device_id=to, device_id_type=MESH) for a in range(self.n_arr)]

    def _local(self):
        dev = 4 * self.me[0] + 2 * self.me[1] + self.me[2]
        return [pltpu.make_async_copy(self.srcs[a], self.slots[a](dev), self.loc_sems.at[a])
                for a in range(self.n_arr)]

    def start(self):
        for cp in self._local() + self._copies(0, self.me, self.sib, True):
            cp.start()
        for j in (0, 1):
            for cp in self._copies(1 + j, self.me, (*self.chips[j], self.c), True):
                cp.start()

    def wait_own(self):
        for cp in self._local():
            cp.wait()

    def wait_sibling(self):
        for cp in self._copies(0, self.sib, self.me):
            cp.wait_recv()

    def wait_and_pass_on(self, j):
        chip = self.chips[j]
        for cp in self._copies(1 + j, (*chip, self.c), self.me):
            cp.wait_recv()
        for cp in self._copies(4 + j, (*chip, self.c), self.sib):
            cp.start()

    def neighbours_landed(self):
        x, y, c = self.me
        self.wait_and_pass_on(0)
        self.wait_and_pass_on(1)
        for cp in self._copies(1 + 2, (x ^ c, y ^ (1 - c), c), (x ^ (1 - c), y ^ c, c)):
            cp.start()

    def diagonal_landed(self):
        self.wait_and_pass_on(2)

    def wait_passed_on(self, j):
        for cp in self._copies(4 + j, (*self.chips[j], 1 - self.c), self.me):
            cp.wait_recv()

    def wait_sends(self):
        for cp in self._copies(0, self.me, self.sib, True):
            cp.wait_send()
        for j, chip in enumerate(self.chips):
            for cp in self._copies(1 + j, self.me, (*chip, self.c), True) + self._copies(4 + j, (*chip, self.c), self.sib):
                cp.wait_send()

    def finish(self):
        self.wait_sibling()
        for j in range(3):
            self.wait_passed_on(j)
        self.wait_sends()
        self.wait_own()


def _disc(a_re, a_im, log_dt, b_re, b_im):
    dt = jnp.exp(log_dt)
    mag = jnp.exp(a_re * dt)
    ab_re = mag * jnp.cos(a_im * dt)
    ab_im = mag * jnp.sin(a_im * dt)
    den = a_re * a_re + a_im * a_im
    p_re = ab_re - 1.0
    p_im = ab_im
    q_re = (p_re * a_re + p_im * a_im) / den
    q_im = (p_im * a_re - p_re * a_im) / den
    bb_re = q_re * b_re - q_im * b_im
    bb_im = q_re * b_im + q_im * b_re
    return ab_re, ab_im, bb_re, bb_im


def _split3(v):
    hi = v.astype(BF16)
    r1 = v - hi.astype(F32)
    mid = r1.astype(BF16)
    lo = (r1 - mid.astype(F32)).astype(BF16)
    return hi, mid, lo


def _select_dot(sel, v):
    return sum(_dot(sel, t) for t in _split3(v))


PACK_ROWS = 72
PACK_W = 512
ROW_FINAL_GAIN, ROW_NORM_GAIN, ROW_BGLU_D, ROW_CONV, ROW_LOSS, ROW_S5 = 0, 8, 16, 24, 32, 40
LANE_A_RE, LANE_A_IM, LANE_LOG_DT = 0, 128, 256


def _ssm_disc_bwd_pack(a_re_x, a_im_x, log_dt_x, b_re, b_im, g_ab_re, g_ab_im, dbb_re_d, dbb_im_d,
                       loss_t, dg8, dgf, dbg, dd, dcw, dc_re_d, dc_im_d):
    rows_gh = N_GROUPS * GROUP

    def body(are, aim, ldt, bre, bim, gabre, gabim, dbbre_ref, dbbim_ref,
             loss_ref, dg8_ref, dgf_ref, dbg_ref, dd_ref, dcw_ref, dcre_ref, dcim_ref,
             p_ref, gc_ref, gb_ref, gbb_re, gbb_im):
        r_g = lax.broadcasted_iota(jnp.int32, (N_GROUPS, rows_gh), 0)
        c_gh = lax.broadcasted_iota(jnp.int32, (N_GROUPS, rows_gh), 1)
        group_sum = (c_gh // GROUP == r_g).astype(BF16)
        r_gh = lax.broadcasted_iota(jnp.int32, (rows_gh, N_GROUPS), 0)
        c_g = lax.broadcasted_iota(jnp.int32, (rows_gh, N_GROUPS), 1)
        first_row = (r_gh == c_g * GROUP).astype(BF16)

        def diag_block(ref, j, gi):
            return ref[j, gi * GROUP:(gi + 1) * GROUP, gi * STATE:(gi + 1) * STATE]

        for j in range(N_JBLK):
            for gi in range(SUBLANES):
                r0 = (j * SUBLANES + gi) * GROUP
                gbb_re[r0:r0 + GROUP, :] = diag_block(dbbre_ref, j, gi)
                gbb_im[r0:r0 + GROUP, :] = diag_block(dbbim_ref, j, gi)
                both = jnp.concatenate([diag_block(dcre_ref, j, gi), -diag_block(dcim_ref, j, gi)], axis=1)
                gc_ref[r0:r0 + GROUP, :] = both.astype(BF16)

        def by_group(ref):
            return jnp.concatenate([ref[:, g * STATE:(g + 1) * STATE] for g in range(N_GROUPS)], axis=0)

        _, vjp = jax.vjp(_disc, are[...], aim[...], ldt[...], bre[...], bim[...])
        d_are, d_aim, d_ldt, d_bre, d_bim = vjp((_select_dot(first_row, by_group(gabre)),
                                                 _select_dot(first_row, by_group(gabim)),
                                                 gbb_re[...], gbb_im[...]))
        gb_ref[...] = jnp.concatenate([d_bre, d_bim], axis=1).astype(BF16)

        p_ref[...] = jnp.zeros_like(p_ref)
        half = D_MODEL // 2
        for r, src in ((ROW_FINAL_GAIN, dgf_ref), (ROW_NORM_GAIN, dg8_ref)):
            p_ref[r:r + 1, :] = src[0:1, 0:half]
            p_ref[r + 1:r + 2, :] = src[0:1, half:D_MODEL]
        p_ref[ROW_BGLU_D:ROW_BGLU_D + 1, :] = dbg_ref[...]
        p_ref[ROW_BGLU_D + 1:ROW_BGLU_D + 2, :] = dd_ref[...]
        p_ref[ROW_CONV:ROW_CONV + SUBLANES, :] = dcw_ref[...]
        p_ref[ROW_LOSS:ROW_LOSS + SUBLANES, 0:LANES] = loss_ref[...]
        s5 = slice(ROW_S5, ROW_S5 + N_GROUPS)
        p_ref[s5, LANE_A_RE:LANE_A_RE + STATE] = _select_dot(group_sum, d_are)
        p_ref[s5, LANE_A_IM:LANE_A_IM + STATE] = _select_dot(group_sum, d_aim)
        p_ref[s5, LANE_LOG_DT:LANE_LOG_DT + LANES] = _select_dot(group_sum, jnp.broadcast_to(d_ldt, (rows_gh, LANES)))

    operands = (a_re_x, a_im_x, log_dt_x, b_re, b_im, g_ab_re, g_ab_im, dbb_re_d, dbb_im_d,
                loss_t, dg8, dgf, dbg, dd, dcw, dc_re_d, dc_im_d)
    out_shape = (_out((PACK_ROWS, PACK_W), F32),
                 _out((rows_gh, 2 * STATE), BF16),
                 _out((rows_gh, 2 * STATE), BF16))
    return _pcall(body, name="ssm_disc_bwd_pack", grid=(1,), out_shape=out_shape,
                  in_specs=_whole_specs(operands), out_specs=tuple(_whole_specs(out_shape)),
                  scratch_shapes=[pltpu.VMEM((rows_gh, STATE), F32), pltpu.VMEM((rows_gh, STATE), F32)],
                  compiler_params=_params(1))(*operands)


def _s5_prepare(are, aim, ldt, bre, bim, cre, cim,
                o_ax_re, o_ax_im, o_ldt_x, o_ab_re, o_ab_im, o_bb_re, o_bb_im, o_c_re, o_c_imn):
    rows_gh = N_GROUPS * GROUP
    rep = (lax.broadcasted_iota(jnp.int32, (rows_gh, N_GROUPS), 0) // GROUP
           == lax.broadcasted_iota(jnp.int32, (rows_gh, N_GROUPS), 1)).astype(BF16)
    eye = (lax.broadcasted_iota(jnp.int32, (N_GROUPS, N_GROUPS), 0)
           == lax.broadcasted_iota(jnp.int32, (N_GROUPS, N_GROUPS), 1)).astype(F32)
    ldt_col = jnp.sum(eye * ldt[...], axis=1, keepdims=True)
    a_re_x = _select_dot(rep, are[...])
    a_im_x = _select_dot(rep, aim[...])
    ldt_x = _select_dot(rep, jnp.broadcast_to(ldt_col, (N_GROUPS, LANES)))[:, 0:1]
    o_ax_re[...] = a_re_x
    o_ax_im[...] = a_im_x
    o_ldt_x[...] = ldt_x
    ab_re, ab_im, bb_re, bb_im = _disc(a_re_x, a_im_x, ldt_x, bre[...], bim[...])
    for j in range(N_JBLK):
        first = [(j * SUBLANES + gi) * GROUP for gi in range(SUBLANES)]
        o_ab_re[j] = jnp.concatenate([ab_re[r:r + 1, :] for r in first], axis=1)
        o_ab_im[j] = jnp.concatenate([ab_im[r:r + 1, :] for r in first], axis=1)
    for o, v in ((o_bb_re, bb_re), (o_bb_im, bb_im), (o_c_re, cre[...]), (o_c_imn, -cim[...])):
        for j in range(N_JBLK):
            for gi in range(SUBLANES):
                r0 = (j * SUBLANES + gi) * GROUP
                parts = [v[r0:r0 + GROUP, :] if k == gi else jnp.zeros((GROUP, STATE), F32) for k in range(SUBLANES)]
                o[j, gi * GROUP:(gi + 1) * GROUP, :] = jnp.concatenate(parts, axis=1).astype(BF16)


def _in_proj(order, x2, g1, w_in_b, s5):
    n = x2.shape[0]
    tm = min(IN_TILE, n)
    n_tiles = n // tm
    n_s5_in = len(s5)
    n_s5_out = 9

    def body(order_ref, x_ref, g_ref, w_ref, *refs):
        s5_in = refs[:n_s5_in]
        xn_ref, proj_ref, wall_ref = refs[n_s5_in:n_s5_in + 3]
        s5_out = refs[n_s5_in + 3:n_s5_in + 3 + n_s5_out]
        xn_scr, wbuf, send_sems, recv_sems, loc_sems, out_sems = refs[n_s5_in + 3 + n_s5_out:]
        k = pl.program_id(0)
        i = pl.program_id(1)

        def slot(dev):
            return wbuf.at[dev // 2, :, pl.ds(pl.multiple_of((dev % 2) * COLS_PER_DEV, LANES), COLS_PER_DEV)]

        gather = _TwoLevelGather([w_ref], [slot], send_sems, recv_sems, loc_sems)

        @pl.when((k == 0) & (i == 0))
        def _():
            gather.start()

        def own_chip():
            gather.wait_own()
            gather.wait_sibling()

        def x_chip():
            gather.neighbours_landed()
            gather.wait_passed_on(0)

        def diag_chip():
            gather.diagonal_landed()
            gather.wait_passed_on(2)

        arrivals = [own_chip, x_chip, functools.partial(gather.wait_passed_on, 1), diag_chip]
        for kk, arrived in enumerate(arrivals):
            @pl.when((k == kk) & (i == 0))
            def _(arrived=arrived):
                arrived()

        rows = pl.ds(pl.multiple_of(i * tm, tm), tm)

        @pl.when(k == 0)
        def _():
            x = x_ref[...]
            r = lax.rsqrt(jnp.mean(x * x, axis=-1, keepdims=True) + EPS)
            xn = ((x * r) * g_ref[...]).astype(BF16)
            xn_scr[rows, :] = xn
            xn_ref[...] = xn

        proj_ref[...] = _dot(xn_scr[rows, :], wbuf[order_ref[k]])

        @pl.when((k == 0) & (i == n_tiles - 1))
        def _():
            _s5_prepare(*s5_in, *s5_out)

        @pl.when((k == N_CHIP - 1) & (i == n_tiles - 1))
        def _():
            gather.wait_sends()
            outs = [pltpu.make_async_copy(wbuf.at[q], wall_ref.at[:, q * COLS_PER_CHIP:(q + 1) * COLS_PER_CHIP],
                                          out_sems.at[q]) for q in range(N_CHIP)]
            for cp in outs:
                cp.start()
            for cp in outs:
                cp.wait()

    tile_once = lambda k, i, order: (jnp.where(k == 0, i, n_tiles - 1), 0)
    whole = lambda shape: pl.BlockSpec(shape, lambda k, i, order: (0,) * len(shape))
    rows_gh = N_GROUPS * GROUP
    s5_out_shapes = ([(rows_gh, STATE), F32], [(rows_gh, STATE), F32], [(rows_gh, 1), F32],
                     [(N_JBLK, 1, JB_ST), F32], [(N_JBLK, 1, JB_ST), F32]) + ([(N_JBLK, JB_CH, JB_ST), BF16],) * 4
    grid_spec = pltpu.PrefetchScalarGridSpec(
        num_scalar_prefetch=1, grid=(N_CHIP, n_tiles),
        in_specs=[pl.BlockSpec((tm, D_MODEL), tile_once),
                  whole((1, D_MODEL)),
                  HBM_SPEC,
                  *(whole(a.shape) for a in s5)],
        out_specs=(pl.BlockSpec((tm, D_MODEL), tile_once),
                   pl.BlockSpec((tm, COLS_PER_CHIP), lambda k, i, order: (i, order[k])),
                   HBM_SPEC,
                   *(whole(shape) for shape, _ in s5_out_shapes)),
        scratch_shapes=[pltpu.VMEM((n, D_MODEL), BF16), pltpu.VMEM((N_CHIP, D_MODEL, COLS_PER_CHIP), BF16),
                        pltpu.SemaphoreType.DMA((7,)), pltpu.SemaphoreType.DMA((7,)), pltpu.SemaphoreType.DMA((1,)),
                        pltpu.SemaphoreType.DMA((N_CHIP,))])
    outs = _pcall(
        body, name="in_proj", grid_spec=grid_spec,
        out_shape=(_out((n, D_MODEL), BF16), _out((n, IN_COLS), F32),
                   _out((D_MODEL, IN_COLS), BF16),
                   *(_out(shape, dt) for shape, dt in s5_out_shapes)),
        compiler_params=_params(2),
    )(order, x2, g1, w_in_b, *s5)
    return outs[0], outs[1], outs[2], outs[3:]


def _cmul(p, q):
    return p[0] * q[0] - p[1] * q[1], p[0] * q[1] + p[1] * q[0]


def _scan_tables(ar, ai, width, reverse):
    pows = [(ar, ai)]
    for _ in range(SUBLANES - 1):
        pows.append(_cmul(pows[-1], (ar, ai)))
    row = lax.broadcasted_iota(jnp.int32, (SUBLANES, width), 0)

    def bc(v):
        return jnp.broadcast_to(v, (SUBLANES, width))

    levels = []
    for k in (1, 2, 4):
        keep = (row <= SUBLANES - 1 - k) if reverse else (row >= k)
        levels.append((jnp.where(keep, bc(pows[k - 1][0]), 0.0), jnp.where(keep, bc(pows[k - 1][1]), 0.0)))
    cre = jnp.zeros((SUBLANES, width), F32)
    cim = jnp.zeros((SUBLANES, width), F32)
    for r in range(SUBLANES):
        e = (SUBLANES - r) if reverse else (r + 1)
        cre = jnp.where(row == r, bc(pows[e - 1][0]), cre)
        cim = jnp.where(row == r, bc(pows[e - 1][1]), cim)
    return levels, (cre, cim)


def _load_chunked(src_ref, b, dst_ref, n_rows):
    n_blk = n_rows // SUBLANES
    for i in range(n_blk):
        dst_ref[b, i * SUBLANES:(i + 1) * SUBLANES, :] = src_ref[b, pl.ds(i, SUBLANES, stride=n_blk), :]


def _store_chunked(val, dst_ref, b, n_rows):
    n_blk = n_rows // SUBLANES
    for i in range(n_blk):
        dst_ref[b, pl.ds(i, SUBLANES, stride=n_blk), :] = val[i * SUBLANES:(i + 1) * SUBLANES, :]


def _chunk_scan(re_ref, im_ref, bs, car_ref, ar, ai, n_rows, reverse, on_block=None):
    width = re_ref.shape[2]
    n_blk = n_rows // SUBLANES
    shape = (SUBLANES, width)
    abr = jnp.broadcast_to(ar, shape)
    abi = jnp.broadcast_to(ai, shape)
    order = list(range(n_blk - 1, -1, -1)) if reverse else list(range(n_blk))

    def blk(ref, b, i):
        return ref[b, i * SUBLANES:(i + 1) * SUBLANES, :]

    def step(state, b, i):
        sr, si = state
        return abr * sr - abi * si + blk(re_ref, b, i), abr * si + abi * sr + blk(im_ref, b, i)

    finals = {b: (blk(re_ref, b, order[0]), blk(im_ref, b, order[0])) for b in bs}
    for i in order[1:]:
        for b in bs:
            finals[b] = step(finals[b], b, i)

    mr, mi = ar, ai
    for _ in range(n_blk.bit_length() - 1):
        mr, mi = _cmul((mr, mi), (mr, mi))
    levels, _ = _scan_tables(mr, mi, width, reverse)
    mbr = jnp.broadcast_to(mr, shape)
    mbi = jnp.broadcast_to(mi, shape)
    row = lax.broadcasted_iota(jnp.int32, shape, 0)
    edge_in = SUBLANES - 1 if reverse else 0
    edge_out = 0 if reverse else SUBLANES - 1
    sh1 = SUBLANES - 1 if reverse else 1
    states = {}
    for b in bs:
        fr, fi = finals[b]
        gr = jnp.where(row == edge_in, jnp.broadcast_to(car_ref[b, 0:1, :], shape), pltpu.roll(fr, sh1, 0))
        gi = jnp.where(row == edge_in, jnp.broadcast_to(car_ref[b, 1:2, :], shape), pltpu.roll(fi, sh1, 0))
        for (lr, li), k in zip(levels, (1, 2, 4)):
            sh = (SUBLANES - k) if reverse else k
            sr = pltpu.roll(gr, sh, 0)
            si = pltpu.roll(gi, sh, 0)
            gr, gi = gr + (lr * sr - li * si), gi + (lr * si + li * sr)
        car_ref[b, 0:1, :] = (fr + (mbr * gr - mbi * gi))[edge_out:edge_out + 1, :]
        car_ref[b, 1:2, :] = (fi + (mbr * gi + mbi * gr))[edge_out:edge_out + 1, :]
        states[b] = (gr, gi)

    for i in order:
        for b in bs:
            states[b] = step(states[b], b, i)
            re_ref[b, i * SUBLANES:(i + 1) * SUBLANES, :] = states[b][0]
            im_ref[b, i * SUBLANES:(i + 1) * SUBLANES, :] = states[b][1]
            if on_block is not None:
                on_block(b, i, *states[b])


def _ssm_fwd(u, bb_re, bb_im, c_re_t, c_imn_t, d_row, ab_re, ab_im, w_out_own, w_glu_own, conv_p, n_seq, seq):
    tt = min(SCAN_TILE, seq)
    nt = seq // tt

    def body(u_ref, bbre, bbim, cre, cimn, d_ref, are, aim, wout_ref, wglu_ref, cw_ref,
             sre_ref, sim_ref, y_ref, oout_ref, oglu_ref, ocw_ref,
             up_ref, car_ref, woutb_ref, wglub_ref, send_sems, recv_sems, loc_sems):
        j = pl.program_id(0)
        t = pl.program_id(1)
        gather = _TwoLevelGather(
            [woutb_ref, wglub_ref, cw_ref],
            [lambda dev: oout_ref.at[pl.ds(pl.multiple_of(dev * OUT_ROWS_PER_DEV, OUT_ROWS_PER_DEV), OUT_ROWS_PER_DEV), :],
             lambda dev: oglu_ref.at[pl.ds(pl.multiple_of(dev * GLU_ROWS_PER_DEV, GLU_ROWS_PER_DEV), GLU_ROWS_PER_DEV), :],
             lambda dev: ocw_ref.at[dev]],
            send_sems, recv_sems, loc_sems)

        @pl.when((j == 0) & (t == 0))
        def _():
            woutb_ref[...] = wout_ref[...].astype(BF16)
            wglub_ref[...] = wglu_ref[...].astype(BF16)
            gather.start()

        @pl.when((j == N_JBLK // 2) & (t == 0))
        def _():
            gather.neighbours_landed()

        @pl.when((j == N_JBLK - 1) & (t == 0))
        def _():
            gather.diagonal_landed()

        @pl.when(t == 0)
        def _():
            car_ref[...] = jnp.zeros_like(car_ref)

        bs = list(range(n_seq))
        for b in bs:
            _load_chunked(u_ref, b, up_ref, tt)
        for b in bs:
            ub = up_ref[b].astype(BF16)
            sre_ref[b] = _dot(ub, bbre[0])
            sim_ref[b] = _dot(ub, bbim[0])
            _chunk_scan(sre_ref, sim_ref, [b], car_ref, are[0], aim[0], tt, reverse=False)
        for b in bs:
            yp = (_dot_nt(sre_ref[b].astype(BF16), cre[0]) + _dot_nt(sim_ref[b].astype(BF16), cimn[0])
                  + d_ref[...] * up_ref[b])
            _store_chunked(yp, y_ref, b, tt)

        @pl.when((j == N_JBLK - 1) & (t == nt - 1))
        def _():
            gather.finish()

    tok = lambda j, t: (0, t, j)
    blk3 = lambda j, t: (j, 0, 0)
    row = lambda j, t: (0, j)
    whole = lambda j, t: (0, 0)
    st = _out((n_seq, seq, N_JBLK * JB_ST), F32)
    n_arr = 3
    return _pcall(
        body, name="ssm_fwd", grid=(N_JBLK, nt),
        out_shape=(st, st, _out((n_seq, seq, SSM_W), F32),
                   _out((D_MODEL, D_MODEL), BF16), _out((SSM_W, SSM_W), BF16),
                   _out((N_DEV, SUBLANES, LANES), F32)),
        in_specs=[pl.BlockSpec((n_seq, tt, JB_CH), tok),
                  pl.BlockSpec((1, JB_CH, JB_ST), blk3), pl.BlockSpec((1, JB_CH, JB_ST), blk3),
                  pl.BlockSpec((1, JB_CH, JB_ST), blk3), pl.BlockSpec((1, JB_CH, JB_ST), blk3),
                  pl.BlockSpec((1, JB_CH), row), pl.BlockSpec((1, 1, JB_ST), blk3), pl.BlockSpec((1, 1, JB_ST), blk3),
                  pl.BlockSpec(w_out_own.shape, whole), pl.BlockSpec(w_glu_own.shape, whole), HBM_SPEC],
        out_specs=(pl.BlockSpec((n_seq, tt, JB_ST), tok), pl.BlockSpec((n_seq, tt, JB_ST), tok),
                   pl.BlockSpec((n_seq, tt, JB_CH), tok), HBM_SPEC, HBM_SPEC, HBM_SPEC),
        scratch_shapes=[pltpu.VMEM((n_seq, tt, JB_CH), F32), pltpu.VMEM((n_seq, SUBLANES, JB_ST), F32),
                        pltpu.VMEM(w_out_own.shape, BF16), pltpu.VMEM(w_glu_own.shape, BF16),
                        pltpu.SemaphoreType.DMA((7 * n_arr,)), pltpu.SemaphoreType.DMA((7 * n_arr,)),
                        pltpu.SemaphoreType.DMA((n_arr,))],
        compiler_params=_params(2),
    )(u, bb_re, bb_im, c_re_t, c_imn_t, d_row, ab_re, ab_im, w_out_own, w_glu_own, conv_p)


def _ssm_bwd(dy, u, s_re, s_im, bb_re, bb_im, c_re_t, c_imn_t, d_row, ab_re, ab_im, g_out, g_glu, n_seq, seq):
    tt = min(SCAN_TILE, seq)
    nt = seq // tt
    rows8 = tt // SUBLANES

    def body(dy_ref, u_ref, sre_ref, sim_ref, pre_ref, pim_ref, bbre, bbim, cre, cimn, d_ref, are, aim,
             gout_ref, gglu_ref,
             du_ref, dcre_ref, dcim_ref, dbbre_ref, dbbim_ref, dare_ref, daim_ref, dd_ref, rout_ref, rglu_ref,
             lre_ref, lim_ref, dyp_ref, up_ref, car_ref, send_sems, recv_sems, loc_sems):
        j = pl.program_id(0)
        tr = pl.program_id(1)

        def exchange():
            return _direct_copies(lambda pid: [gout_ref.at[pid], gglu_ref.at[pid]], [rout_ref, rglu_ref],
                                  send_sems, recv_sems, loc_sems)

        @pl.when((j == 0) & (tr == 0))
        def _():
            mine, sends = exchange()
            for cp in mine + sends:
                cp.start()

        @pl.when(tr == 0)
        def _():
            car_ref[...] = jnp.zeros_like(car_ref)
            for r in (dcre_ref, dcim_ref, dbbre_ref, dbbim_ref, dare_ref, daim_ref, dd_ref):
                r[...] = jnp.zeros_like(r)

        first = tr == nt - 1
        row = lax.broadcasted_iota(jnp.int32, (SUBLANES, JB_ST), 0)
        n_blk = tt // SUBLANES
        bs = list(range(n_seq))
        for b in bs:
            _load_chunked(dy_ref, b, dyp_ref, tt)
            _load_chunked(u_ref, b, up_ref, tt)
        for b in bs:
            dyb = dyp_ref[b].astype(BF16)
            lre_ref[b] = _dot(dyb, cre[0])
            lim_ref[b] = _dot(dyb, cimn[0])
        acc = {b: [jnp.zeros((SUBLANES, JB_ST), F32), jnp.zeros((SUBLANES, JB_ST), F32)] for b in bs}

        def on_block(b, i, lr, li):
            if i > 0:
                spr = sre_ref[b, (i - 1) * SUBLANES:i * SUBLANES, :]
                spi = sim_ref[b, (i - 1) * SUBLANES:i * SUBLANES, :]
            else:
                hr = jnp.where(first, 0.0, pre_ref[b, SUBLANES - 1:SUBLANES, :])
                hi = jnp.where(first, 0.0, pim_ref[b, SUBLANES - 1:SUBLANES, :])
                last_r = sre_ref[b, (n_blk - 1) * SUBLANES:n_blk * SUBLANES, :]
                last_i = sim_ref[b, (n_blk - 1) * SUBLANES:n_blk * SUBLANES, :]
                spr = jnp.where(row == 0, jnp.broadcast_to(hr, row.shape), pltpu.roll(last_r, 1, 0))
                spi = jnp.where(row == 0, jnp.broadcast_to(hi, row.shape), pltpu.roll(last_i, 1, 0))
            acc[b][0] = acc[b][0] + (lr * spr + li * spi)
            acc[b][1] = acc[b][1] + (li * spr - lr * spi)

        _chunk_scan(lre_ref, lim_ref, bs, car_ref, are[0], -aim[0], tt, reverse=True, on_block=on_block)
        for b in bs:
            dare_ref[...] += jnp.sum(acc[b][0], axis=0, keepdims=True)
            daim_ref[...] += jnp.sum(acc[b][1], axis=0, keepdims=True)
            dyp = dyp_ref[b]
            up = up_ref[b]
            dyb = dyp.astype(BF16)
            ub = up.astype(BF16)
            lrb = lre_ref[b].astype(BF16)
            lib = lim_ref[b].astype(BF16)
            dup = d_ref[...] * dyp + _dot_nt(lrb, bbre[0]) + _dot_nt(lib, bbim[0])
            _store_chunked(dup, du_ref, b, tt)
            dbbre_ref[0] += _dot_tn(ub, lrb)
            dbbim_ref[0] += _dot_tn(ub, lib)
            dcre_ref[0] += _dot_tn(dyb, sre_ref[b].astype(BF16))
            dcim_ref[0] += _dot_tn(dyb, sim_ref[b].astype(BF16))
            dd_ref[...] += jnp.sum(dyp * up, axis=0, keepdims=True)

        @pl.when((j == N_JBLK - 1) & (tr == nt - 1))
        def _():
            mine, sends = exchange()
            for cp in sends + mine:
                cp.wait()

    tok = lambda j, t: (0, nt - 1 - t, j)
    halo = lambda j, t: (0, jnp.maximum((nt - 1 - t) * rows8 - 1, 0), j)
    blk3 = lambda j, t: (j, 0, 0)
    row1 = lambda j, t: (0, j)
    acc_shape = _out((N_JBLK, JB_CH, JB_ST), F32)
    return _pcall(
        body, name="ssm_bwd", grid=(N_JBLK, nt),
        out_shape=(_out((n_seq, seq, SSM_W), F32), acc_shape, acc_shape, acc_shape, acc_shape,
                   _out((1, N_JBLK * JB_ST), F32), _out((1, N_JBLK * JB_ST), F32),
                   _out((1, SSM_W), F32),
                   _out((N_DEV,) + g_out.shape[1:], F32),
                   _out((N_DEV,) + g_glu.shape[1:], F32)),
        in_specs=[pl.BlockSpec((n_seq, tt, JB_CH), tok), pl.BlockSpec((n_seq, tt, JB_CH), tok),
                  pl.BlockSpec((n_seq, tt, JB_ST), tok), pl.BlockSpec((n_seq, tt, JB_ST), tok),
                  pl.BlockSpec((n_seq, SUBLANES, JB_ST), halo), pl.BlockSpec((n_seq, SUBLANES, JB_ST), halo),
                  pl.BlockSpec((1, JB_CH, JB_ST), blk3), pl.BlockSpec((1, JB_CH, JB_ST), blk3),
                  pl.BlockSpec((1, JB_CH, JB_ST), blk3), pl.BlockSpec((1, JB_CH, JB_ST), blk3),
                  pl.BlockSpec((1, JB_CH), row1), pl.BlockSpec((1, 1, JB_ST), blk3), pl.BlockSpec((1, 1, JB_ST), blk3),
                  HBM_SPEC, HBM_SPEC],
        out_specs=(pl.BlockSpec((n_seq, tt, JB_CH), tok),
                   pl.BlockSpec((1, JB_CH, JB_ST), blk3), pl.BlockSpec((1, JB_CH, JB_ST), blk3),
                   pl.BlockSpec((1, JB_CH, JB_ST), blk3), pl.BlockSpec((1, JB_CH, JB_ST), blk3),
                   pl.BlockSpec((1, JB_ST), row1), pl.BlockSpec((1, JB_ST), row1), pl.BlockSpec((1, JB_CH), row1),
                   HBM_SPEC, HBM_SPEC),
        scratch_shapes=[pltpu.VMEM((n_seq, tt, JB_ST), F32), pltpu.VMEM((n_seq, tt, JB_ST), F32),
                        pltpu.VMEM((n_seq, tt, JB_CH), F32), pltpu.VMEM((n_seq, tt, JB_CH), F32),
                        pltpu.VMEM((n_seq, SUBLANES, JB_ST), F32),
                        pltpu.SemaphoreType.DMA((7 * 2,)), pltpu.SemaphoreType.DMA((7 * 2,)),
                        pltpu.SemaphoreType.DMA((2,))],
        compiler_params=_params(2),
    )(dy, u, s_re, s_im, s_re, s_im, bb_re, bb_im, c_re_t, c_imn_t, d_row, ab_re, ab_im, g_out, g_glu)


def _mix(x2, tgt2, y, proj, gf, b_glu, conv8, w_glu_f, w_out_f, seq):
    n = x2.shape[0]
    tm = TOK_TILE
    tiles_per_seq = seq // tm
    rows8 = tm // SUBLANES

    def body(x_ref, t_ref, y_ref, zs_ref, h_ref, bc_ref, cc_ref, zc_ref, hp_ref, ccp_ref,
             gf_ref, bg_ref, cw_ref, wg_ref, wo_ref,
             dh2_ref, dy_ref, dzs_ref, dbc_ref, dzc_ref, dyc_ref,
             dwo_ref, dwg_ref, loss_ref, dgf_ref, dbg_ref, dcw_ref):
        i = pl.program_id(0)

        @pl.when(i == 0)
        def _():
            for r in (dwo_ref, dwg_ref, loss_ref, dgf_ref, dbg_ref, dcw_ref):
                r[...] = jnp.zeros_like(r)

        yv = y_ref[...]
        y1, dgelu = _gelu_and_grad(yv)
        y1b = y1.astype(BF16)
        gate = _sigmoid(_dot(y1b, wg_ref[...]) + bg_ref[...])
        y2 = y1 * gate
        szs, dszs = _silu_and_grad(zs_ref[...])
        yssm = y2 * szs
        hv = h_ref[...]
        ccv = cc_ref[...]
        bcv = bc_ref[...]
        v = ccv * hv
        first = (i % tiles_per_seq) == 0
        vhalo = jnp.where(first, 0.0, ccp_ref[...] * hp_ref[...])
        v1 = _shift_down(v, vhalo, 1)
        v2 = _shift_down(v, vhalo, 2)
        w0 = cw_ref[0:1, :]
        w1 = cw_ref[1:2, :]
        w2 = cw_ref[2:3, :]
        yc = w0 * v2 + w1 * v1 + w2 * v
        szc, dszc = _silu_and_grad(zc_ref[...])
        yconv = (bcv * yc) * szc
        ysb = yssm.astype(BF16)
        ycb = yconv.astype(BF16)
        h2 = x_ref[...] + _dot(ysb, wo_ref[0:SSM_W, :]) + _dot(ycb, wo_ref[SSM_W:, :])
        r2 = lax.rsqrt(jnp.mean(h2 * h2, axis=-1, keepdims=True) + EPS)
        hn = h2 * r2
        gfv = gf_ref[...]
        err = hn * gfv - t_ref[...]
        loss_ref[...] += 0.5 * jnp.sum(jnp.mean(err * err, axis=-1, keepdims=True))
        dout = err * (1.0 / D_MODEL)
        dgf_ref[...] += jnp.sum(dout * hn, axis=0, keepdims=True)
        dn = dout * gfv
        dh2 = r2 * (dn - hn * jnp.mean(dn * hn, axis=-1, keepdims=True))
        dh2_ref[...] = dh2
        dh2b = dh2.astype(BF16)
        dwo_ref[0:SSM_W, :] += _dot_tn(ysb, dh2b)
        dwo_ref[SSM_W:, :] += _dot_tn(ycb, dh2b)
        dyssm = _dot_nt(dh2b, wo_ref[0:SSM_W, :])
        dyconv = _dot_nt(dh2b, wo_ref[SSM_W:, :])
        dy2 = dyssm * szs
        dzs_ref[...] = (dyssm * y2 * dszs).astype(BF16)
        dgp = dy2 * y1 * (gate * (1.0 - gate))
        dgpb = dgp.astype(BF16)
        dy1 = dy2 * gate + _dot_nt(dgpb, wg_ref[...])
        dwg_ref[...] += _dot_tn(y1b, dgpb)
        dbg_ref[...] += jnp.sum(dgp, axis=0, keepdims=True)
        dy_ref[...] = dy1 * dgelu
        dbc_ref[...] = (dyconv * yc * szc).astype(BF16)
        dyc = dyconv * bcv * szc
        dyc_ref[...] = dyc
        dzc_ref[...] = (dyconv * bcv * yc * dszc).astype(BF16)
        dcw_ref[0:1, :] += jnp.sum(dyc * v2, axis=0, keepdims=True)
        dcw_ref[1:2, :] += jnp.sum(dyc * v1, axis=0, keepdims=True)
        dcw_ref[2:3, :] += jnp.sum(dyc * v, axis=0, keepdims=True)

    tile_d = pl.BlockSpec((tm, D_MODEL), lambda i: (i, 0))
    tile_s = pl.BlockSpec((tm, SSM_W), lambda i: (i, 0))
    seg_of = lambda c: pl.BlockSpec((tm, SSM_W), lambda i: (i, c))
    halo_of = lambda c: pl.BlockSpec((SUBLANES, SSM_W), lambda i: (jnp.maximum(i * rows8 - 1, 0), c))
    const = lambda shape: pl.BlockSpec(shape, lambda i: (0,) * len(shape))
    seg = _out((n, SSM_W), F32)
    seg_b = _out((n, SSM_W), BF16)
    return _pcall(
        body, name="mix", grid=(n // tm,),
        out_shape=(_out((n, D_MODEL), F32), seg, seg_b, seg_b, seg_b, seg,
                   _out((D_MODEL, D_MODEL), F32), _out((SSM_W, SSM_W), F32),
                   _out((SUBLANES, LANES), F32), _out((1, D_MODEL), F32),
                   _out((1, SSM_W), F32), _out((SUBLANES, CONV_W), F32)),
        in_specs=[tile_d, tile_d, tile_s, seg_of(SEG_ZS), seg_of(SEG_H), seg_of(SEG_BC), seg_of(SEG_CC), seg_of(SEG_ZC),
                  halo_of(SEG_H), halo_of(SEG_CC),
                  const((1, D_MODEL)), const((1, SSM_W)), const((SUBLANES, CONV_W)),
                  const((SSM_W, SSM_W)), const((D_MODEL, D_MODEL))],
        out_specs=(tile_d, tile_s, tile_s, tile_s, tile_s, tile_s,
                   const((D_MODEL, D_MODEL)), const((SSM_W, SSM_W)), const((SUBLANES, LANES)),
                   const((1, D_MODEL)), const((1, SSM_W)), const((SUBLANES, CONV_W))),
        compiler_params=_params(1),
    )(x2, tgt2, y, proj, proj, proj, proj, proj, proj, proj, gf, b_glu, conv8, w_glu_f, w_out_f)


def _in_bwd(x2, dh2, du, dzs, dyc, proj, dbc, dzc, g1, conv8, w_full, seq):
    n = x2.shape[0]
    tm = TOK_TILE
    n_tiles = n // tm
    tiles_per_seq = seq // tm
    rows8 = tm // SUBLANES
    n_blk8 = n // SUBLANES

    def body(x_ref, dh2_ref, du_ref, dzs_ref, dyc_ref, dycn_ref, h_ref, cc_ref, dbc_ref, dzc_ref,
             g_ref, cw_ref, w_ref, gx_ref, dp_ref, dg_ref):
        i = pl.program_id(0)

        @pl.when(i == 0)
        def _():
            dg_ref[...] = jnp.zeros_like(dg_ref)

        dyc = dyc_ref[...]
        last = (i % tiles_per_seq) == tiles_per_seq - 1
        nhalo = jnp.where(last, 0.0, dycn_ref[...])
        dv = (cw_ref[2:3, :] * dyc + cw_ref[1:2, :] * _shift_up(dyc, nhalo, 1)
              + cw_ref[0:1, :] * _shift_up(dyc, nhalo, 2))
        parts = (du_ref[...], dzs_ref[...], dv * cc_ref[...], dbc_ref[...], dv * h_ref[...], dzc_ref[...])
        dxn = jnp.zeros((tm, D_MODEL), F32)
        for k, p in enumerate(parts):
            pb = p.astype(BF16)
            dp_ref[:, k * SSM_W:(k + 1) * SSM_W] = pb
            dxn = dxn + _dot_nt(pb, w_ref[:, k * SSM_W:(k + 1) * SSM_W])
        x = x_ref[...]
        r = lax.rsqrt(jnp.mean(x * x, axis=-1, keepdims=True) + EPS)
        xh = x * r
        dg_ref[...] += jnp.sum(dxn * xh, axis=0, keepdims=True)
        dn = dxn * g_ref[...]
        gx_ref[...] = dh2_ref[...] + r * (dn - xh * jnp.mean(dn * xh, axis=-1, keepdims=True))

    tile_d = pl.BlockSpec((tm, D_MODEL), lambda i: (i, 0))
    tile_s = pl.BlockSpec((tm, SSM_W), lambda i: (i, 0))
    seg_of = lambda c: pl.BlockSpec((tm, SSM_W), lambda i: (i, c))
    nhalo = pl.BlockSpec((SUBLANES, SSM_W), lambda i: (jnp.minimum((i + 1) * rows8, n_blk8 - 1), 0))
    const = lambda shape: pl.BlockSpec(shape, lambda i: (0,) * len(shape))
    return _pcall(
        body, name="in_bwd", grid=(n_tiles,),
        out_shape=(_out((n, D_MODEL), F32), _out((n, IN_COLS), BF16),
                   _out((SUBLANES, D_MODEL), F32)),
        in_specs=[tile_d, tile_d, tile_s, tile_s, tile_s, nhalo, seg_of(SEG_H), seg_of(SEG_CC), tile_s, tile_s,
                  const((1, D_MODEL)), const((SUBLANES, CONV_W)), const((D_MODEL, IN_COLS))],
        out_specs=(tile_d, pl.BlockSpec((tm, IN_COLS), lambda i: (i, 0)), const((SUBLANES, D_MODEL))),
        compiler_params=_params(1),
    )(x2, dh2, du, dzs, dyc, dyc, proj, proj, dbc, dzc, g1, conv8, w_full)


_HALF_BLOCKS = ((0, 0), (0, 1), (1, 0), (2, 0), (1, 1), (2, 1), (3, 0), (3, 1))


def _dw_in_exchange(chips, xn, dproj, smalls):
    n = xn.shape[0]
    tk = min(1024, n)
    nk = n // tk
    piece = (D_MODEL, COLS_PER_DEV)
    hr = D_MODEL // 2
    n_half = len(_HALF_BLOCKS)
    n_small = len(smalls)
    assert _HALF_BLOCKS[0][1] == 0 and _HALF_BLOCKS[1][1] == 1
    order = jnp.stack([chips[b] for b, _ in _HALF_BLOCKS]
                      + [jnp.int32(t) for _, t in _HALF_BLOCKS]).astype(jnp.int32)

    def body(order_ref, xn_hbm, dp_ref, *refs):
        sm_refs = refs[:n_small]
        own_ref, rchip_ref = refs[n_small:n_small + 2]
        rsm_refs = refs[n_small + 2:2 * n_small + 2]
        (xn_ref, acc, stage, rbuf, kbuf, relay_in, xn_sems, give_send, give_recv, keep_send, keep_recv,
         relay_send, relay_recv, sm_send, sm_recv, sm_loc) = refs[2 * n_small + 2:]
        s = pl.program_id(0)

        def xn_copy(kk, t):
            rows = pl.ds(pl.multiple_of(kk * tk, tk), tk)
            return pltpu.make_async_copy(xn_hbm.at[rows, t * hr:(t + 1) * hr], xn_ref.at[t, rows, :],
                                         xn_sems.at[2 * kk + t])

        @pl.when(s == 0)
        def _():
            for kk in range(nk):
                for t in range(2):
                    xn_copy(kk, t).start()
            xn_copy(0, 0).wait()

        @pl.when(s == 1)
        def _():
            xn_copy(0, 1).wait()

        x, y, c = _mesh_pos()
        sib = (x, y, 1 - c)
        y_nbr, x_nbr = (x, 1 - y, c), (1 - x, y, c)
        gather = _TwoLevelGather(list(sm_refs), [functools.partial(lambda r, dev: r.at[dev], r) for r in rsm_refs],
                                 sm_send, sm_recv, sm_loc)

        def give(h):
            cols = pl.ds(pl.multiple_of((1 - c) * COLS_PER_DEV, LANES), COLS_PER_DEV)
            return pltpu.make_async_remote_copy(src_ref=acc.at[h % 2, :, cols], dst_ref=stage.at[h],
                                                send_sem=give_send.at[h], recv_sem=give_recv.at[h],
                                                device_id=sib, device_id_type=MESH)

        def relay(r):
            return pltpu.make_async_remote_copy(src_ref=rbuf.at[r], dst_ref=relay_in.at[r],
                                                send_sem=relay_send.at[r], recv_sem=relay_recv.at[r],
                                                device_id=(x_nbr, y_nbr)[r], device_id_type=MESH)

        def keep(q):
            return pltpu.make_async_remote_copy(src_ref=kbuf.at[q], dst_ref=rchip_ref.at[q // 2, pl.ds((q % 2) * hr, hr), :],
                                                send_sem=keep_send.at[q], recv_sem=keep_recv.at[q],
                                                device_id=(y_nbr, x_nbr)[q // 2], device_id_type=MESH)

        def chip_sum(h):
            give(h).wait_recv()
            mine = [acc[h % 2, :, cc * COLS_PER_DEV:(cc + 1) * COLS_PER_DEV] for cc in range(2)]
            return jnp.where(c == 0, mine[0], mine[1]) + stage[h]

        @pl.when(s == 0)
        def _():
            gather.start()

        @pl.when(s == 2)
        def _():
            gather.neighbours_landed()

        @pl.when(s == n_half - 2)
        def _():
            gather.diagonal_landed()

        for k in range(2, n_half):
            @pl.when(s == k)
            def _(k=k):
                give(k - 2).wait_send()

        slot = s % 2
        t_half = order_ref[n_half + s]
        acc[slot] = _dot_tn(xn_ref[t_half, pl.ds(0, tk), :], dp_ref[pl.ds(0, tk), :])

        def kstep(kk, carry):
            for t in range(2):
                @pl.when(s == t)
                def _(t=t):
                    xn_copy(kk, t).wait()

            off = pl.multiple_of(kk * tk, tk)
            acc[slot] += _dot_tn(xn_ref[t_half, pl.ds(off, tk), :], dp_ref[pl.ds(off, tk), :])
            return carry

        n_first = max(1, nk // 2)
        lax.fori_loop(1, n_first, kstep, 0)
        for k in range(1, n_half):
            @pl.when(s == k)
            def _(k=k):
                h = k - 1
                b, t = _HALF_BLOCKS[h]
                total = chip_sum(h)
                if b == 0:
                    rbuf[t] = total.astype(BF16)
                    relay(t).start()
                elif b < 3:
                    if (b, t) in ((1, 0), (2, 1)):
                        relay(t).wait_recv()
                        total = total + relay_in[t].astype(F32)
                    q = 2 * (b - 1) + t
                    kbuf[q] = total.astype(BF16)
                    keep(q).start()
                else:
                    own_ref[0:hr, :] = total

        lax.fori_loop(n_first, nk, kstep, 0)

        for k in range(n_half):
            @pl.when(s == k)
            def _(k=k):
                give(k).start()

        @pl.when(s == n_half - 1)
        def _():
            own_ref[hr:D_MODEL, :] = chip_sum(n_half - 1)
            give(n_half - 2).wait_send()
            give(n_half - 1).wait_send()
            for r in range(2):
                relay(r).wait_send()
            for q in range(4):
                keep(q).wait()
            gather.finish()

    half_piece = (hr, COLS_PER_DEV)
    grid_spec = pltpu.PrefetchScalarGridSpec(
        num_scalar_prefetch=1, grid=(n_half,),
        in_specs=[HBM_SPEC,
                  pl.BlockSpec((n, COLS_PER_CHIP), lambda s, order: (0, order[s])),
                  *([HBM_SPEC] * n_small)],
        out_specs=(pl.BlockSpec(piece, lambda s, order: (0, 0)), HBM_SPEC, *([HBM_SPEC] * n_small)),
        scratch_shapes=[pltpu.VMEM((2, n, hr), BF16),
                        pltpu.VMEM((2, hr, COLS_PER_CHIP), F32), pltpu.VMEM((n_half,) + half_piece, F32),
                        pltpu.VMEM((2,) + half_piece, BF16), pltpu.VMEM((4,) + half_piece, BF16),
                        pltpu.VMEM((2,) + half_piece, BF16),
                        pltpu.SemaphoreType.DMA((2 * nk,)),
                        pltpu.SemaphoreType.DMA((n_half,)), pltpu.SemaphoreType.DMA((n_half,)),
                        pltpu.SemaphoreType.DMA((4,)), pltpu.SemaphoreType.DMA((4,)),
                        pltpu.SemaphoreType.DMA((2,)), pltpu.SemaphoreType.DMA((2,)),
                        pltpu.SemaphoreType.DMA((7 * n_small,)), pltpu.SemaphoreType.DMA((7 * n_small,)),
                        pltpu.SemaphoreType.DMA((n_small,))])
    return _pcall(
        body, name="dw_in_exchange", grid_spec=grid_spec,
        out_shape=(_out(piece, F32), _out((2,) + piece, BF16),
                   *(_out((N_DEV,) + a.shape, a.dtype) for a in smalls)),
        compiler_params=_params(1),
    )(order, xn, dproj, *smalls)


def _adamw(g, w, m, v):
    m_new = ADAM_B1 * m + (1.0 - ADAM_B1) * g
    v_new = ADAM_B2 * v + (1.0 - ADAM_B2) * (g * g)
    m_hat = m_new / (1.0 - ADAM_B1 ** ADAM_STEP)
    v_hat = v_new / (1.0 - ADAM_B2 ** ADAM_STEP)
    delta = -ADAM_LR * (m_hat / (jnp.sqrt(v_hat) + ADAM_EPS) + ADAM_WD * w)
    return delta, m_new, v_new


def _reduce_adam_w_in(own, rchip, w, m, v):
    rows, cols = w.shape
    row_tile = 256

    def body(o_ref, r_ref, w_ref, m_ref, v_ref, g_ref, d_ref, nm_ref, nv_ref):
        g = o_ref[...]
        for s in range(2):
            g = g + r_ref[s].astype(F32)
        g_ref[...] = g
        d_ref[...], nm_ref[...], nv_ref[...] = _adamw(g, w_ref[...], m_ref[...], v_ref[...])

    tile = pl.BlockSpec((row_tile, cols), lambda i: (i, 0))
    shp = _out((rows, cols), F32)
    return _pcall(
        body, name="reduce_adam_w_in", grid=(rows // row_tile,),
        out_shape=(shp,) * 4,
        in_specs=[tile, pl.BlockSpec((2, row_tile, cols), lambda i: (0, i, 0)), tile, tile, tile],
        out_specs=(tile,) * 4,
        compiler_params=_params(1),
    )(own, rchip, w, m, v)


_SMALL_LEAVES = ("norm_gain", "final_norm_gain", "b_glu", "ssm_a_re", "ssm_a_im", "ssm_log_dt", "ssm_d", "conv_w",
                 "ssm_c_re", "ssm_c_im", "ssm_b_re", "ssm_b_im")


def _reduce_adam_small(r_pack, r_gc, r_gb, wmv, sharded):
    n_leaf = len(_SMALL_LEAVES)
    n_sh = len(sharded)

    def body(*refs):
        rp_ref, rgc_ref, rgb_ref = refs[:3]
        w_refs = refs[3:3 + 3 * n_leaf]
        sh_in = refs[3 + 3 * n_leaf:3 + 3 * n_leaf + 4 * n_sh]
        outs0 = 3 + 3 * n_leaf + 4 * n_sh
        loss_ref = refs[outs0]
        o_refs = refs[outs0 + 1:outs0 + 1 + 4 * n_leaf]
        sh_out = refs[outs0 + 1 + 4 * n_leaf:outs0 + 1 + 4 * n_leaf + 4 * n_sh]
        own_conv = refs[-1]

        def total(ref):
            acc = ref[0].astype(F32)
            for s in range(1, N_DEV):
                acc = acc + ref[s].astype(F32)
            return acc

        for i in range(n_sh):
            r_ref, w_ref, m_ref, v_ref = sh_in[4 * i:4 * i + 4]
            o_g, o_d, o_m, o_v = sh_out[4 * i:4 * i + 4]
            g = total(r_ref)
            o_g[...] = g
            o_d[...], o_m[...], o_v[...] = _adamw(g, w_ref[...], m_ref[...], v_ref[...])

        sp = total(rp_ref)
        sgc = total(rgc_ref)
        sgb = total(rgb_ref)
        loss_ref[...] = sp[ROW_LOSS:ROW_LOSS + 1, 0:1]

        def wide(r):
            return jnp.concatenate([sp[r:r + 1, :], sp[r + 1:r + 2, :]], axis=1)

        s5 = slice(ROW_S5, ROW_S5 + N_GROUPS)
        eye = (lax.broadcasted_iota(jnp.int32, (N_GROUPS, N_GROUPS), 0)
               == lax.broadcasted_iota(jnp.int32, (N_GROUPS, N_GROUPS), 1)).astype(F32)
        d_rows = jnp.broadcast_to(sp[ROW_BGLU_D + 1:ROW_BGLU_D + 2, :], (GROUP, SSM_W))
        own_p = (lax.broadcasted_iota(jnp.int32, (GROUP, SSM_W), 1) % GROUP
                 == lax.broadcasted_iota(jnp.int32, (GROUP, SSM_W), 0))
        of_group = (lax.broadcasted_iota(jnp.int32, (SSM_W, N_GROUPS), 0) // GROUP
                    == lax.broadcasted_iota(jnp.int32, (SSM_W, N_GROUPS), 1)).astype(BF16)
        d_pg = sum(_dot(t, of_group) for t in _split3(jnp.where(own_p, d_rows, 0.0)))
        me = 4 * lax.axis_index("x") + 2 * lax.axis_index("y") + lax.axis_index("c")
        for k in range(N_DEV):
            @pl.when(me == k)
            def _(k=k):
                own_conv[...] = sp[ROW_CONV:ROW_CONV + SUBLANES, k * CONV_COLS_PER_DEV:(k + 1) * CONV_COLS_PER_DEV]
        grads = {
            "norm_gain": wide(ROW_NORM_GAIN),
            "final_norm_gain": wide(ROW_FINAL_GAIN),
            "b_glu": sp[ROW_BGLU_D:ROW_BGLU_D + 1, :],
            "ssm_a_re": sp[s5, LANE_A_RE:LANE_A_RE + STATE],
            "ssm_a_im": sp[s5, LANE_A_IM:LANE_A_IM + STATE],
            "ssm_log_dt": jnp.sum(sp[s5, LANE_LOG_DT:LANE_LOG_DT + 1] * eye, axis=0, keepdims=True),
            "ssm_d": d_pg,
            "ssm_c_re": sgc[:, 0:STATE],
            "ssm_c_im": sgc[:, STATE:2 * STATE],
            "ssm_b_re": sgb[:, 0:STATE],
            "ssm_b_im": sgb[:, STATE:2 * STATE],
        }
        for i, name in enumerate(_SMALL_LEAVES):
            w_ref, m_ref, v_ref = w_refs[3 * i:3 * i + 3]
            o_g, o_d, o_m, o_v = o_refs[4 * i:4 * i + 4]
            if name == "conv_w":
                for k in range(w_ref.shape[0]):
                    g = own_conv[k:k + 1, :]
                    o_g[k] = g
                    o_d[k], o_m[k], o_v[k] = _adamw(g, w_ref[k], m_ref[k], v_ref[k])
                continue
            g = grads[name]
            o_g[...] = g
            o_d[...], o_m[...], o_v[...] = _adamw(g, w_ref[...], m_ref[...], v_ref[...])

    flat_w = [a for name in _SMALL_LEAVES for a in wmv[name]]
    leaf_shapes = [_out(wmv[name][0].shape, F32) for name in _SMALL_LEAVES for _ in range(4)]
    sh_shapes = [_out(entry[1].shape, F32) for entry in sharded for _ in range(4)]
    operands = (r_pack, r_gc, r_gb, *flat_w, *(a for entry in sharded for a in entry))
    out_shape = (_out((1, 1), F32), *leaf_shapes, *sh_shapes)
    outs = _pcall(
        body, name="reduce_adam_small", grid=(1,), out_shape=out_shape,
        in_specs=_whole_specs(operands), out_specs=tuple(_whole_specs(out_shape)),
        scratch_shapes=[pltpu.VMEM((SUBLANES, CONV_COLS_PER_DEV), F32)],
        compiler_params=_params(1),
    )(*operands)
    leaves = {name: outs[1 + 4 * i:5 + 4 * i] for i, name in enumerate(_SMALL_LEAVES)}
    first = 1 + 4 * n_leaf
    return outs[0], leaves, [outs[first + 4 * i:first + 4 * i + 4] for i in range(n_sh)]


def kernel(x, norm_gain, w_in, ssm_a_re, ssm_a_im, ssm_log_dt, ssm_b_re, ssm_b_im, ssm_c_re, ssm_c_im, ssm_d, w_glu, b_glu, conv_w, w_out, final_norm_gain, loss_target, m_norm_gain, m_w_in, m_ssm_a_re, m_ssm_a_im, m_ssm_log_dt, m_ssm_b_re, m_ssm_b_im, m_ssm_c_re, m_ssm_c_im, m_ssm_d, m_w_glu, m_b_glu, m_conv_w, m_w_out, m_final_norm_gain, v_norm_gain, v_w_in, v_ssm_a_re, v_ssm_a_im, v_ssm_log_dt, v_ssm_b_re, v_ssm_b_im, v_ssm_c_re, v_ssm_c_im, v_ssm_d, v_w_glu, v_b_glu, v_conv_w, v_w_out, v_final_norm_gain):
    n_seq, seq, _ = x.shape
    n = n_seq * seq

    gh_p = lambda b4: jnp.transpose(b4, (0, 1, 3, 2)).reshape(N_GROUPS * GROUP, STATE)
    c2 = lambda a: a.reshape(N_GROUPS * GROUP, STATE)
    b_re2, b_im2 = gh_p(ssm_b_re), gh_p(ssm_b_im)
    d_row = ssm_d[0].reshape(1, SSM_W)

    x2 = x.reshape(n, D_MODEL)
    tgt2 = loss_target.reshape(n, D_MODEL)
    mx, my, mc = lax.axis_index("x"), lax.axis_index("y"), lax.axis_index("c")
    chip_ids = [2 * cx + cy for cx, cy in ((mx, my), (1 - mx, my), (mx, 1 - my), (1 - mx, 1 - my))]
    arrival = chip_ids
    xn, proj, w_in_f, s5 = _in_proj(
        jnp.stack(arrival).astype(jnp.int32), x2, norm_gain, w_in[0].astype(BF16),
        (ssm_a_re[0], ssm_a_im[0], ssm_log_dt, b_re2, b_im2, c2(ssm_c_re), c2(ssm_c_im)))
    a_re_x, a_im_x, log_dt_x, ab_re, ab_im, bb_re_m, bb_im_m, c_re_m, c_imn_m = s5
    u3 = proj.reshape(n_seq, seq, IN_COLS)
    conv_p = jnp.pad(conv_w[0], ((0, SUBLANES - 3), (0, LANES - CONV_COLS_PER_DEV)))
    s_re, s_im, y3, w_out_f, w_glu_f, conv_all = _ssm_fwd(
        u3, bb_re_m, bb_im_m, c_re_m, c_imn_m, d_row, ab_re, ab_im,
        w_out[0], w_glu[0], conv_p, n_seq, seq)
    conv8 = jnp.transpose(conv_all[:, :, :CONV_COLS_PER_DEV], (1, 0, 2)).reshape(SUBLANES, CONV_W)
    (dh2, dy, dzs, dbc, dzc, dyc, dw_out, dw_glu, loss_t, dgf, dbg, dcw) = _mix(
        x2, tgt2, y3.reshape(n, SSM_W), proj, final_norm_gain.reshape(1, D_MODEL), b_glu, conv8,
        w_glu_f, w_out_f, seq)

    du3, dc_re_d, dc_im_d, dbb_re_d, dbb_im_d, dab_re, dab_im, dd, r_out, r_glu = _ssm_bwd(
        dy.reshape(n_seq, seq, SSM_W), u3, s_re, s_im, bb_re_m, bb_im_m, c_re_m, c_imn_m, d_row, ab_re, ab_im,
        dw_out.reshape(N_DEV, OUT_ROWS_PER_DEV, D_MODEL), dw_glu.reshape(N_DEV, GLU_ROWS_PER_DEV, SSM_W), n_seq, seq)
    du = du3.reshape(n, SSM_W)
    grad_x2, dproj, dg8 = _in_bwd(x2, dh2, du, dzs, dyc, proj, dbc, dzc, norm_gain, conv8, w_in_f, seq)
    pack, gc, gb = _ssm_disc_bwd_pack(
        a_re_x, a_im_x, log_dt_x, b_re2, b_im2, dab_re, dab_im,
        dbb_re_d, dbb_im_d, loss_t, dg8, dgf, dbg, dd, dcw, dc_re_d, dc_im_d)

    own_in, rchip_in, r_pack, r_gc, r_gb = _dw_in_exchange(
        [chip_ids[3], chip_ids[2], chip_ids[1], chip_ids[0]],
        xn, dproj, [pack, gc, gb])

    flat2 = lambda a: a.reshape(a.shape[-2:]) if a.ndim > 2 else a.reshape(1, -1)
    c2 = lambda a: a.reshape(N_GROUPS * GROUP, STATE)
    wmv = dict(norm_gain=(norm_gain, m_norm_gain, v_norm_gain),
               final_norm_gain=tuple(flat2(a) for a in (final_norm_gain, m_final_norm_gain, v_final_norm_gain)),
               b_glu=(b_glu, m_b_glu, v_b_glu),
               ssm_a_re=tuple(flat2(a) for a in (ssm_a_re, m_ssm_a_re, v_ssm_a_re)),
               ssm_a_im=tuple(flat2(a) for a in (ssm_a_im, m_ssm_a_im, v_ssm_a_im)),
               ssm_log_dt=(ssm_log_dt, m_ssm_log_dt, v_ssm_log_dt),
               ssm_d=tuple(jnp.transpose(a, (0, 2, 1)).reshape(GROUP, N_GROUPS) for a in (ssm_d, m_ssm_d, v_ssm_d)),
               conv_w=tuple(jnp.transpose(a, (1, 0, 2)) for a in (conv_w, m_conv_w, v_conv_w)),
               ssm_c_re=tuple(c2(a) for a in (ssm_c_re, m_ssm_c_re, v_ssm_c_re)),
               ssm_c_im=tuple(c2(a) for a in (ssm_c_im, m_ssm_c_im, v_ssm_c_im)),
               ssm_b_re=(b_re2, gh_p(m_ssm_b_re), gh_p(v_ssm_b_re)),
               ssm_b_im=(b_im2, gh_p(m_ssm_b_im), gh_p(v_ssm_b_im)))

    res_in = _reduce_adam_w_in(own_in, rchip_in, w_in[0], m_w_in[0], v_w_in[0])
    loss11, small, (res_out, res_glu) = _reduce_adam_small(
        r_pack, r_gc, r_gb, wmv,
        [(r_out, w_out[0], m_w_out[0], v_w_out[0]), (r_glu, w_glu[0], m_w_glu[0], v_w_glu[0])])
    loss = loss11.reshape(())

    shapes = dict(norm_gain=(1, D_MODEL), ssm_a_re=(1, N_GROUPS, STATE), ssm_a_im=(1, N_GROUPS, STATE),
                  ssm_log_dt=(1, N_GROUPS), ssm_c_re=(1, N_GROUPS, GROUP, STATE), ssm_c_im=(1, N_GROUPS, GROUP, STATE),
                  b_glu=(1, SSM_W), final_norm_gain=(D_MODEL,))
    big = dict(w_in=res_in, w_glu=res_glu, w_out=res_out)

    def leaf(kind, name):
        if name in big:
            return big[name][kind][None]
        if name in ("ssm_b_re", "ssm_b_im"):
            return jnp.transpose(small[name][kind].reshape(1, N_GROUPS, GROUP, STATE), (0, 1, 3, 2))
        if name == "ssm_d":
            return jnp.transpose(small[name][kind].reshape(1, GROUP, N_GROUPS), (0, 2, 1))
        if name == "conv_w":
            return jnp.transpose(small[name][kind], (1, 0, 2))
        return small[name][kind].reshape(shapes[name])

    order = ["norm_gain", "w_in", "ssm_a_re", "ssm_a_im", "ssm_log_dt", "ssm_b_re", "ssm_b_im", "ssm_c_re",
             "ssm_c_im", "ssm_d", "w_glu", "b_glu", "conv_w", "w_out", "final_norm_gain"]
    outs = [loss, grad_x2.reshape(x.shape)]
    for kind in range(4):
        outs += [leaf(kind, name) for name in order]
    return tuple(outs)
```

```python
import functools
import math

import jax
import jax.numpy as jnp
from jax import lax
from jax.experimental import pallas as pl
from jax.experimental.pallas import tpu as pltpu

F32 = jnp.float32
BF16 = jnp.bfloat16

N_DEV = 8
D_MODEL = 1024
SSM_W = 512
CONV_W = 512
N_GROUPS = 32
GROUP = 16
STATE = 64
IN_COLS = 3072
SEG_U, SEG_ZS, SEG_H, SEG_BC, SEG_CC, SEG_ZC = range(6)
COLS_PER_DEV = IN_COLS // N_DEV
N_CHIP = N_DEV // 2
COLS_PER_CHIP = 2 * COLS_PER_DEV
OUT_ROWS_PER_DEV = D_MODEL // N_DEV
GLU_ROWS_PER_DEV = SSM_W // N_DEV
CONV_COLS_PER_DEV = CONV_W // N_DEV
EPS = 1e-6

N_JBLK = 4
JB_CH = SSM_W // N_JBLK
JB_ST = N_GROUPS * STATE // N_JBLK

ADAM_LR = 0.001
ADAM_B1 = 0.9
ADAM_B2 = 0.999
ADAM_EPS = 1e-08
ADAM_WD = 0.01
ADAM_STEP = 10

SUBLANES = 8
LANES = 128
VMEM_LIMIT = 48 * 1024 * 1024
TOK_TILE = 256
IN_TILE = 1024
SCAN_TILE = 1024

MESH = pl.DeviceIdType.MESH
HBM_SPEC = pl.BlockSpec(memory_space=pltpu.HBM)


def _build(body, **kw):
    return pl.pallas_call(body, **kw)


def _pcall(body, **kw):
    def call(*operands):
        pinned = [a if jnp.issubdtype(a.dtype, jnp.integer) else pltpu.with_memory_space_constraint(a, pltpu.HBM)
                  for a in operands]
        return _build(body, **kw)(*pinned)
    return call


def _whole_specs(arrays):
    return [pl.BlockSpec(a.shape, functools.partial(lambda nd, i: (0,) * nd, len(a.shape))) for a in arrays]


def _out(shape, dtype):
    return pltpu.HBM(tuple(shape), dtype)


def _params(n_grid):
    return pltpu.CompilerParams(dimension_semantics=("arbitrary",) * n_grid,
                                vmem_limit_bytes=VMEM_LIMIT)


def _dot(a, b):
    return jnp.dot(a, b, preferred_element_type=F32)


def _dot_nt(a, b):
    return lax.dot_general(a, b, (((1,), (1,)), ((), ())), preferred_element_type=F32)


def _dot_tn(a, b):
    return lax.dot_general(a, b, (((0,), (0,)), ((), ())), preferred_element_type=F32)


def _sigmoid(z):
    return 1.0 / (1.0 + jnp.exp(-z))


_GELU_C = math.sqrt(2.0 / math.pi)


def _gelu_and_grad(y):
    inner = _GELU_C * (y + 0.044715 * (y * y * y))
    t = jnp.tanh(inner)
    g = 0.5 * y * (1.0 + t)
    dg = 0.5 * (1.0 + t) + 0.5 * y * (1.0 - t * t) * (_GELU_C * (1.0 + 3.0 * 0.044715 * (y * y)))
    return g, dg


def _silu_and_grad(z):
    s = _sigmoid(z)
    return z * s, s * (1.0 + z * (1.0 - s))


def _shift_down(v, halo, k):
    rolled = pltpu.roll(v, k, 0)
    row = lax.broadcasted_iota(jnp.int32, v.shape, 0)
    for r in range(k):
        rolled = jnp.where(row == r, halo[SUBLANES - k + r:SUBLANES - k + r + 1, :], rolled)
    return rolled


def _shift_up(v, halo, k):
    n = v.shape[0]
    rolled = pltpu.roll(v, n - k, 0)
    row = lax.broadcasted_iota(jnp.int32, v.shape, 0)
    for r in range(k):
        rolled = jnp.where(row == n - k + r, halo[r:r + 1, :], rolled)
    return rolled


def _mesh_pos():
    return lax.axis_index("x"), lax.axis_index("y"), lax.axis_index("c")


def _direct_copies(srcs_for, out_refs, send_sems, recv_sems, loc_sems):
    x, y, c = _mesh_pos()
    me_id = 4 * x + 2 * y + c
    n_arr = len(out_refs)
    dsts = [r.at[me_id] for r in out_refs]
    own = srcs_for(me_id)
    mine = [pltpu.make_async_copy(own[a], dsts[a], loc_sems.at[a]) for a in range(n_arr)]
    sends = []
    for k in range(1, N_DEV):
        px, py, pc = x ^ ((k >> 2) & 1), y ^ ((k >> 1) & 1), c ^ (k & 1)
        src = srcs_for(4 * px + 2 * py + pc)
        for a in range(n_arr):
            sends.append(pltpu.make_async_remote_copy(
                src_ref=src[a], dst_ref=dsts[a],
                send_sem=send_sems.at[(k - 1) * n_arr + a], recv_sem=recv_sems.at[(k - 1) * n_arr + a],
                device_id=(px, py, pc), device_id_type=MESH))
    return mine, sends


class _TwoLevelGather:
    def __init__(self, srcs, slots, send_sems, recv_sems, loc_sems):
        self.srcs, self.slots, self.n_arr = srcs, slots, len(srcs)
        self.send_sems, self.recv_sems, self.loc_sems = send_sems, recv_sems, loc_sems
        x, y, c = _mesh_pos()
        self.c = c
        self.me, self.sib = (x, y, c), (x, y, 1 - c)
        self.chips = [(1 - x, y), (x, 1 - y), (1 - x, 1 - y)]

    def _copies(self, k, block, to, from_src=False):
        dev = 4 * block[0] + 2 * block[1] + block[2]
        return [pltpu.make_async_remote_copy(
            src_ref=self.srcs[a] if from_src else self.slots[a](dev), dst_ref=self.slots[a](dev),
            send_sem=self.send_sems.at[k * self.n_arr + a], recv_sem=self.recv_sems.at[k * self.n_arr + a],
            device_id=to, device_id_type=MESH) for a in range(self.n_arr)]

    def _local(self):
        dev = 4 * self.me[0] + 2 * self.me[1] + self.me[2]
        return [pltpu.make_async_copy(self.srcs[a], self.slots[a](dev), self.loc_sems.at[a])
                for a in range(self.n_arr)]

    def start(self):
        for cp in self._local() + self._copies(0, self.me, self.sib, True):
            cp.start()
        for j in (0, 1):
            for cp in self._copies(1 + j, self.me, (*self.chips[j], self.c), True):
                cp.start()

    def wait_own(self):
        for cp in self._local():
            cp.wait()

    def wait_sibling(self):
        for cp in self._copies(0, self.sib, self.me):
            cp.wait_recv()

    def wait_and_pass_on(self, j):
        chip = self.chips[j]
        for cp in self._copies(1 + j, (*chip, self.c), self.me):
            cp.wait_recv()
        for cp in self._copies(4 + j, (*chip, self.c), self.sib):
            cp.start()

    def neighbours_landed(self):
        x, y, c = self.me
        self.wait_and_pass_on(0)
        self.wait_and_pass_on(1)
        for cp in self._copies(1 + 2, (x ^ c, y ^ (1 - c), c), (x ^ (1 - c), y ^ c, c)):
            cp.start()

    def diagonal_landed(self):
        self.wait_and_pass_on(2)

    def wait_passed_on(self, j):
        for cp in self._copies(4 + j, (*self.chips[j], 1 - self.c), self.me):
            cp.wait_recv()

    def wait_sends(self):
        for cp in self._copies(0, self.me, self.sib, True):
            cp.wait_send()
        for j, chip in enumerate(self.chips):
            for cp in self._copies(1 + j, self.me, (*chip, self.c), True) + self._copies(4 + j, (*chip, self.c), self.sib):
                cp.wait_send()

    def finish(self):
        self.wait_sibling()
        for j in range(3):
            self.wait_passed_on(j)
        self.wait_sends()
        self.wait_own()


def _disc(a_re, a_im, log_dt, b_re, b_im):
    dt = jnp.exp(log_dt)
    mag = jnp.exp(a_re * dt)
    ab_re = mag * jnp.cos(a_im * dt)
    ab_im = mag * jnp.sin(a_im * dt)
    den = a_re * a_re + a_im * a_im
    p_re = ab_re - 1.0
    p_im = ab_im
    q_re = (p_re * a_re + p_im * a_im) / den
    q_im = (p_im * a_re - p_re * a_im) / den
    bb_re = q_re * b_re - q_im * b_im
    bb_im = q_re * b_im + q_im * b_re
    return ab_re, ab_im, bb_re, bb_im


def _split3(v):
    hi = v.astype(BF16)
    r1 = v - hi.astype(F32)
    mid = r1.astype(BF16)
    lo = (r1 - mid.astype(F32)).astype(BF16)
    return hi, mid, lo


def _select_dot(sel, v):
    return sum(_dot(sel, t) for t in _split3(v))


PACK_ROWS = 72
PACK_W = 512
ROW_FINAL_GAIN, ROW_NORM_GAIN, ROW_BGLU_D, ROW_CONV, ROW_LOSS, ROW_S5 = 0, 8, 16, 24, 32, 40
LANE_A_RE, LANE_A_IM, LANE_LOG_DT = 0, 128, 256


def _ssm_disc_bwd_pack(a_re_x, a_im_x, log_dt_x, b_re, b_im, g_ab_re, g_ab_im, dbb_re_d, dbb_im_d,
                       loss_t, dg8, dgf, dbg, dd, dcw, dc_re_d, dc_im_d):
    rows_gh = N_GROUPS * GROUP

    def body(are, aim, ldt, bre, bim, gabre, gabim, dbbre_ref, dbbim_ref,
             loss_ref, dg8_ref, dgf_ref, dbg_ref, dd_ref, dcw_ref, dcre_ref, dcim_ref,
             p_ref, gc_ref, gb_ref, gbb_re, gbb_im):
        r_g = lax.broadcasted_iota(jnp.int32, (N_GROUPS, rows_gh), 0)
        c_gh = lax.broadcasted_iota(jnp.int32, (N_GROUPS, rows_gh), 1)
        group_sum = (c_gh // GROUP == r_g).astype(BF16)
        r_gh = lax.broadcasted_iota(jnp.int32, (rows_gh, N_GROUPS), 0)
        c_g = lax.broadcasted_iota(jnp.int32, (rows_gh, N_GROUPS), 1)
        first_row = (r_gh == c_g * GROUP).astype(BF16)

        def diag_block(ref, j, gi):
            return ref[j, gi * GROUP:(gi + 1) * GROUP, gi * STATE:(gi + 1) * STATE]

        for j in range(N_JBLK):
            for gi in range(SUBLANES):
                r0 = (j * SUBLANES + gi) * GROUP
                gbb_re[r0:r0 + GROUP, :] = diag_block(dbbre_ref, j, gi)
                gbb_im[r0:r0 + GROUP, :] = diag_block(dbbim_ref, j, gi)
                both = jnp.concatenate([diag_block(dcre_ref, j, gi), -diag_block(dcim_ref, j, gi)], axis=1)
                gc_ref[r0:r0 + GROUP, :] = both.astype(BF16)

        def by_group(ref):
            return jnp.concatenate([ref[:, g * STATE:(g + 1) * STATE] for g in range(N_GROUPS)], axis=0)

        _, vjp = jax.vjp(_disc, are[...], aim[...], ldt[...], bre[...], bim[...])
        d_are, d_aim, d_ldt, d_bre, d_bim = vjp((_select_dot(first_row, by_group(gabre)),
                                                 _select_dot(first_row, by_group(gabim)),
                                                 gbb_re[...], gbb_im[...]))
        gb_ref[...] = jnp.concatenate([d_bre, d_bim], axis=1).astype(BF16)

        p_ref[...] = jnp.zeros_like(p_ref)
        half = D_MODEL // 2
        for r, src in ((ROW_FINAL_GAIN, dgf_ref), (ROW_NORM_GAIN, dg8_ref)):
            p_ref[r:r + 1, :] = src[0:1, 0:half]
            p_ref[r + 1:r + 2, :] = src[0:1, half:D_MODEL]
        p_ref[ROW_BGLU_D:ROW_BGLU_D + 1, :] = dbg_ref[...]
        p_ref[ROW_BGLU_D + 1:ROW_BGLU_D + 2, :] = dd_ref[...]
        p_ref[ROW_CONV:ROW_CONV + SUBLANES, :] = dcw_ref[...]
        p_ref[ROW_LOSS:ROW_LOSS + SUBLANES, 0:LANES] = loss_ref[...]
        s5 = slice(ROW_S5, ROW_S5 + N_GROUPS)
        p_ref[s5, LANE_A_RE:LANE_A_RE + STATE] = _select_dot(group_sum, d_are)
        p_ref[s5, LANE_A_IM:LANE_A_IM + STATE] = _select_dot(group_sum, d_aim)
        p_ref[s5, LANE_LOG_DT:LANE_LOG_DT + LANES] = _select_dot(group_sum, jnp.broadcast_to(d_ldt, (rows_gh, LANES)))

    operands = (a_re_x, a_im_x, log_dt_x, b_re, b_im, g_ab_re, g_ab_im, dbb_re_d, dbb_im_d,
                loss_t, dg8, dgf, dbg, dd, dcw, dc_re_d, dc_im_d)
    out_shape = (_out((PACK_ROWS, PACK_W), F32),
                 _out((rows_gh, 2 * STATE), BF16),
                 _out((rows_gh, 2 * STATE), BF16))
    return _pcall(body, name="ssm_disc_bwd_pack", grid=(1,), out_shape=out_shape,
                  in_specs=_whole_specs(operands), out_specs=tuple(_whole_specs(out_shape)),
                  scratch_shapes=[pltpu.VMEM((rows_gh, STATE), F32), pltpu.VMEM((rows_gh, STATE), F32)],
                  compiler_params=_params(1))(*operands)


def _s5_prepare(are, aim, ldt, bre, bim, cre, cim,
                o_ax_re, o_ax_im, o_ldt_x, o_ab_re, o_ab_im, o_bb_re, o_bb_im, o_c_re, o_c_imn):
    rows_gh = N_GROUPS * GROUP
    rep = (lax.broadcasted_iota(jnp.int32, (rows_gh, N_GROUPS), 0) // GROUP
           == lax.broadcasted_iota(jnp.int32, (rows_gh, N_GROUPS), 1)).astype(BF16)
    eye = (lax.broadcasted_iota(jnp.int32, (N_GROUPS, N_GROUPS), 0)
           == lax.broadcasted_iota(jnp.int32, (N_GROUPS, N_GROUPS), 1)).astype(F32)
    ldt_col = jnp.sum(eye * ldt[...], axis=1, keepdims=True)
    a_re_x = _select_dot(rep, are[...])
    a_im_x = _select_dot(rep, aim[...])
    ldt_x = _select_dot(rep, jnp.broadcast_to(ldt_col, (N_GROUPS, LANES)))[:, 0:1]
    o_ax_re[...] = a_re_x
    o_ax_im[...] = a_im_x
    o_ldt_x[...] = ldt_x
    ab_re, ab_im, bb_re, bb_im = _disc(a_re_x, a_im_x, ldt_x, bre[...], bim[...])
    for j in range(N_JBLK):
        first = [(j * SUBLANES + gi) * GROUP for gi in range(SUBLANES)]
        o_ab_re[j] = jnp.concatenate([ab_re[r:r + 1, :] for r in first], axis=1)
        o_ab_im[j] = jnp.concatenate([ab_im[r:r + 1, :] for r in first], axis=1)
    for o, v in ((o_bb_re, bb_re), (o_bb_im, bb_im), (o_c_re, cre[...]), (o_c_imn, -cim[...])):
        for j in range(N_JBLK):
            for gi in range(SUBLANES):
                r0 = (j * SUBLANES + gi) * GROUP
                parts = [v[r0:r0 + GROUP, :] if k == gi else jnp.zeros((GROUP, STATE), F32) for k in range(SUBLANES)]
                o[j, gi * GROUP:(gi + 1) * GROUP, :] = jnp.concatenate(parts, axis=1).astype(BF16)


def _in_proj(order, x2, g1, w_in_own, s5):
    n = x2.shape[0]
    tm = min(IN_TILE, n)
    n_tiles = n // tm
    n_s5_in = len(s5)
    n_s5_out = 9

    def body(order_ref, x_ref, g_ref, w_ref, *refs):
        s5_in = refs[:n_s5_in]
        xn_ref, proj_ref, wall_ref = refs[n_s5_in:n_s5_in + 3]
        s5_out = refs[n_s5_in + 3:n_s5_in + 3 + n_s5_out]
        xn_scr, wbuf, wown, send_sems, recv_sems, loc_sems, out_sems = refs[n_s5_in + 3 + n_s5_out:]
        k = pl.program_id(0)
        i = pl.program_id(1)

        def slot(dev):
            return wbuf.at[dev // 2, :, pl.ds(pl.multiple_of((dev % 2) * COLS_PER_DEV, LANES), COLS_PER_DEV)]

        gather = _TwoLevelGather([wown], [slot], send_sems, recv_sems, loc_sems)

        @pl.when((k == 0) & (i == 0))
        def _():
            wown[...] = w_ref[...].astype(BF16)
            gather.start()

        def own_chip():
            gather.wait_own()
            gather.wait_sibling()

        def x_chip():
            gather.neighbours_landed()
            gather.wait_passed_on(0)

        def diag_chip():
            gather.diagonal_landed()
            gather.wait_passed_on(2)

        rows = pl.ds(pl.multiple_of(i * tm, tm), tm)

        @pl.when(k == 0)
        def _():
            x = x_ref[...]
            r = lax.rsqrt(jnp.mean(x * x, axis=-1, keepdims=True) + EPS)
            xn = ((x * r) * g_ref[...]).astype(BF16)
            xn_scr[rows, :] = xn
            xn_ref[...] = xn

        arrivals = [own_chip, x_chip, functools.partial(gather.wait_passed_on, 1), diag_chip]
        for kk, arrived in enumerate(arrivals):
            @pl.when((k == kk) & (i == 0))
            def _(arrived=arrived):
                arrived()

        proj_ref[...] = _dot(xn_scr[rows, :], wbuf[order_ref[k]])

        @pl.when((k == 0) & (i == n_tiles - 1))
        def _():
            _s5_prepare(*s5_in, *s5_out)

        @pl.when((k == N_CHIP - 1) & (i == n_tiles - 1))
        def _():
            gather.wait_sends()
            outs = [pltpu.make_async_copy(wbuf.at[q], wall_ref.at[:, q * COLS_PER_CHIP:(q + 1) * COLS_PER_CHIP],
                                          out_sems.at[q]) for q in range(N_CHIP)]
            for cp in outs:
                cp.start()
            for cp in outs:
                cp.wait()

    tile_once = lambda k, i, order: (jnp.where(k == 0, i, n_tiles - 1), 0)
    whole = lambda shape: pl.BlockSpec(shape, lambda k, i, order: (0,) * len(shape))
    rows_gh = N_GROUPS * GROUP
    s5_out_shapes = ([(rows_gh, STATE), F32], [(rows_gh, STATE), F32], [(rows_gh, 1), F32],
                     [(N_JBLK, 1, JB_ST), F32], [(N_JBLK, 1, JB_ST), F32]) + ([(N_JBLK, JB_CH, JB_ST), BF16],) * 4
    grid_spec = pltpu.PrefetchScalarGridSpec(
        num_scalar_prefetch=1, grid=(N_CHIP, n_tiles),
        in_specs=[pl.BlockSpec((tm, D_MODEL), tile_once),
                  whole((1, D_MODEL)),
                  whole(w_in_own.shape),
                  *(whole(a.shape) for a in s5)],
        out_specs=(pl.BlockSpec((tm, D_MODEL), tile_once),
                   pl.BlockSpec((tm, COLS_PER_CHIP), lambda k, i, order: (i, order[k])),
                   HBM_SPEC,
                   *(whole(shape) for shape, _ in s5_out_shapes)),
        scratch_shapes=[pltpu.VMEM((n, D_MODEL), BF16), pltpu.VMEM((N_CHIP, D_MODEL, COLS_PER_CHIP), BF16),
                        pltpu.VMEM(w_in_own.shape, BF16),
                        pltpu.SemaphoreType.DMA((7,)), pltpu.SemaphoreType.DMA((7,)), pltpu.SemaphoreType.DMA((1,)),
                        pltpu.SemaphoreType.DMA((N_CHIP,))])
    outs = _pcall(
        body, name="in_proj", grid_spec=grid_spec,
        out_shape=(_out((n, D_MODEL), BF16), _out((n, IN_COLS), F32),
                   _out((D_MODEL, IN_COLS), BF16),
                   *(_out(shape, dt) for shape, dt in s5_out_shapes)),
        compiler_params=_params(2),
    )(order, x2, g1, w_in_own, *s5)
    return outs[0], outs[1], outs[2], outs[3:]


def _cmul(p, q):
    return p[0] * q[0] - p[1] * q[1], p[0] * q[1] + p[1] * q[0]


def _scan_tables(ar, ai, width, reverse):
    pows = [(ar, ai)]
    for _ in range(SUBLANES - 1):
        pows.append(_cmul(pows[-1], (ar, ai)))
    row = lax.broadcasted_iota(jnp.int32, (SUBLANES, width), 0)

    def bc(v):
        return jnp.broadcast_to(v, (SUBLANES, width))

    levels = []
    for k in (1, 2, 4):
        keep = (row <= SUBLANES - 1 - k) if reverse else (row >= k)
        levels.append((jnp.where(keep, bc(pows[k - 1][0]), 0.0), jnp.where(keep, bc(pows[k - 1][1]), 0.0)))
    cre = jnp.zeros((SUBLANES, width), F32)
    cim = jnp.zeros((SUBLANES, width), F32)
    for r in range(SUBLANES):
        e = (SUBLANES - r) if reverse else (r + 1)
        cre = jnp.where(row == r, bc(pows[e - 1][0]), cre)
        cim = jnp.where(row == r, bc(pows[e - 1][1]), cim)
    return levels, (cre, cim)


def _load_chunked(src_ref, b, dst_ref, n_rows):
    n_blk = n_rows // SUBLANES
    for i in range(n_blk):
        dst_ref[b, i * SUBLANES:(i + 1) * SUBLANES, :] = src_ref[b, pl.ds(i, SUBLANES, stride=n_blk), :]


def _store_chunked(val, dst_ref, b, n_rows):
    n_blk = n_rows // SUBLANES
    for i in range(n_blk):
        dst_ref[b, pl.ds(i, SUBLANES, stride=n_blk), :] = val[i * SUBLANES:(i + 1) * SUBLANES, :]


def _chunk_scan(re_ref, im_ref, bs, car_ref, ar, ai, n_rows, reverse, on_block=None):
    width = re_ref.shape[2]
    n_blk = n_rows // SUBLANES
    shape = (SUBLANES, width)
    abr = jnp.broadcast_to(ar, shape)
    abi = jnp.broadcast_to(ai, shape)
    order = list(range(n_blk - 1, -1, -1)) if reverse else list(range(n_blk))

    def blk(ref, b, i):
        return ref[b, i * SUBLANES:(i + 1) * SUBLANES, :]

    def step(state, b, i):
        sr, si = state
        return abr * sr - abi * si + blk(re_ref, b, i), abr * si + abi * sr + blk(im_ref, b, i)

    finals = {b: (blk(re_ref, b, order[0]), blk(im_ref, b, order[0])) for b in bs}
    for i in order[1:]:
        for b in bs:
            finals[b] = step(finals[b], b, i)

    mr, mi = ar, ai
    for _ in range(n_blk.bit_length() - 1):
        mr, mi = _cmul((mr, mi), (mr, mi))
    levels, _ = _scan_tables(mr, mi, width, reverse)
    mbr = jnp.broadcast_to(mr, shape)
    mbi = jnp.broadcast_to(mi, shape)
    row = lax.broadcasted_iota(jnp.int32, shape, 0)
    edge_in = SUBLANES - 1 if reverse else 0
    edge_out = 0 if reverse else SUBLANES - 1
    sh1 = SUBLANES - 1 if reverse else 1
    states = {}
    for b in bs:
        fr, fi = finals[b]
        gr = jnp.where(row == edge_in, jnp.broadcast_to(car_ref[b, 0:1, :], shape), pltpu.roll(fr, sh1, 0))
        gi = jnp.where(row == edge_in, jnp.broadcast_to(car_ref[b, 1:2, :], shape), pltpu.roll(fi, sh1, 0))
        for (lr, li), k in zip(levels, (1, 2, 4)):
            sh = (SUBLANES - k) if reverse else k
            sr = pltpu.roll(gr, sh, 0)
            si = pltpu.roll(gi, sh, 0)
            gr, gi = gr + (lr * sr - li * si), gi + (lr * si + li * sr)
        car_ref[b, 0:1, :] = (fr + (mbr * gr - mbi * gi))[edge_out:edge_out + 1, :]
        car_ref[b, 1:2, :] = (fi + (mbr * gi + mbi * gr))[edge_out:edge_out + 1, :]
        states[b] = (gr, gi)

    for i in order:
        for b in bs:
            states[b] = step(states[b], b, i)
            re_ref[b, i * SUBLANES:(i + 1) * SUBLANES, :] = states[b][0]
            im_ref[b, i * SUBLANES:(i + 1) * SUBLANES, :] = states[b][1]
            if on_block is not None:
                on_block(b, i, *states[b])


def _ssm_fwd(u, bb_re, bb_im, c_re_t, c_imn_t, d_row, ab_re, ab_im, w_out_own, w_glu_own, conv_p, n_seq, seq):
    tt = min(SCAN_TILE, seq)
    nt = seq // tt

    def body(u_ref, bbre, bbim, cre, cimn, d_ref, are, aim, wout_ref, wglu_ref, cw_ref,
             sre_ref, sim_ref, y_ref, oout_ref, oglu_ref, ocw_ref,
             up_ref, car_ref, woutb_ref, wglub_ref, send_sems, recv_sems, loc_sems):
        j = pl.program_id(0)
        t = pl.program_id(1)
        gather = _TwoLevelGather(
            [woutb_ref, wglub_ref, cw_ref],
            [lambda dev: oout_ref.at[pl.ds(pl.multiple_of(dev * OUT_ROWS_PER_DEV, OUT_ROWS_PER_DEV), OUT_ROWS_PER_DEV), :],
             lambda dev: oglu_ref.at[pl.ds(pl.multiple_of(dev * GLU_ROWS_PER_DEV, GLU_ROWS_PER_DEV), GLU_ROWS_PER_DEV), :],
             lambda dev: ocw_ref.at[dev]],
            send_sems, recv_sems, loc_sems)

        @pl.when((j == 0) & (t == 0))
        def _():
            woutb_ref[...] = wout_ref[...].astype(BF16)
            wglub_ref[...] = wglu_ref[...].astype(BF16)
            gather.start()

        @pl.when((j == N_JBLK // 2) & (t == 0))
        def _():
            gather.neighbours_landed()

        @pl.when((j == N_JBLK - 1) & (t == 0))
        def _():
            gather.diagonal_landed()

        @pl.when(t == 0)
        def _():
            car_ref[...] = jnp.zeros_like(car_ref)

        bs = list(range(n_seq))
        for b in bs:
            _load_chunked(u_ref, b, up_ref, tt)
        for b in bs:
            ub = up_ref[b].astype(BF16)
            sre_ref[b] = _dot(ub, bbre[0])
            sim_ref[b] = _dot(ub, bbim[0])
            _chunk_scan(sre_ref, sim_ref, [b], car_ref, are[0], aim[0], tt, reverse=False)
        for b in bs:
            yp = (_dot_nt(sre_ref[b].astype(BF16), cre[0]) + _dot_nt(sim_ref[b].astype(BF16), cimn[0])
                  + d_ref[...] * up_ref[b])
            _store_chunked(yp, y_ref, b, tt)

        @pl.when((j == N_JBLK - 1) & (t == nt - 1))
        def _():
            gather.finish()

    tok = lambda j, t: (0, t, j)
    blk3 = lambda j, t: (j, 0, 0)
    row = lambda j, t: (0, j)
    whole = lambda j, t: (0, 0)
    st = _out((n_seq, seq, N_JBLK * JB_ST), F32)
    n_arr = 3
    return _pcall(
        body, name="ssm_fwd", grid=(N_JBLK, nt),
        out_shape=(st, st, _out((n_seq, seq, SSM_W), F32),
                   _out((D_MODEL, D_MODEL), BF16), _out((SSM_W, SSM_W), BF16),
                   _out((N_DEV, SUBLANES, LANES), F32)),
        in_specs=[pl.BlockSpec((n_seq, tt, JB_CH), tok),
                  pl.BlockSpec((1, JB_CH, JB_ST), blk3), pl.BlockSpec((1, JB_CH, JB_ST), blk3),
                  pl.BlockSpec((1, JB_CH, JB_ST), blk3), pl.BlockSpec((1, JB_CH, JB_ST), blk3),
                  pl.BlockSpec((1, JB_CH), row), pl.BlockSpec((1, 1, JB_ST), blk3), pl.BlockSpec((1, 1, JB_ST), blk3),
                  pl.BlockSpec(w_out_own.shape, whole), pl.BlockSpec(w_glu_own.shape, whole), HBM_SPEC],
        out_specs=(pl.BlockSpec((n_seq, tt, JB_ST), tok), pl.BlockSpec((n_seq, tt, JB_ST), tok),
                   pl.BlockSpec((n_seq, tt, JB_CH), tok), HBM_SPEC, HBM_SPEC, HBM_SPEC),
        scratch_shapes=[pltpu.VMEM((n_seq, tt, JB_CH), F32), pltpu.VMEM((n_seq, SUBLANES, JB_ST), F32),
                        pltpu.VMEM(w_out_own.shape, BF16), pltpu.VMEM(w_glu_own.shape, BF16),
                        pltpu.SemaphoreType.DMA((7 * n_arr,)), pltpu.SemaphoreType.DMA((7 * n_arr,)),
                        pltpu.SemaphoreType.DMA((n_arr,))],
        compiler_params=_params(2),
    )(u, bb_re, bb_im, c_re_t, c_imn_t, d_row, ab_re, ab_im, w_out_own, w_glu_own, conv_p)


def _ssm_bwd(dy, u, s_re, s_im, bb_re, bb_im, c_re_t, c_imn_t, d_row, ab_re, ab_im, g_out, g_glu, n_seq, seq):
    tt = min(SCAN_TILE, seq)
    nt = seq // tt
    rows8 = tt // SUBLANES

    def body(dy_ref, u_ref, sre_ref, sim_ref, pre_ref, pim_ref, bbre, bbim, cre, cimn, d_ref, are, aim,
             gout_ref, gglu_ref,
             du_ref, dcre_ref, dcim_ref, dbbre_ref, dbbim_ref, dare_ref, daim_ref, dd_ref, rout_ref, rglu_ref,
             lre_ref, lim_ref, dyp_ref, up_ref, car_ref, send_sems, recv_sems, loc_sems):
        j = pl.program_id(0)
        tr = pl.program_id(1)

        def exchange():
            return _direct_copies(lambda pid: [gout_ref.at[pid], gglu_ref.at[pid]], [rout_ref, rglu_ref],
                                  send_sems, recv_sems, loc_sems)

        @pl.when((j == 0) & (tr == 0))
        def _():
            mine, sends = exchange()
            for cp in mine + sends:
                cp.start()

        @pl.when(tr == 0)
        def _():
            car_ref[...] = jnp.zeros_like(car_ref)
            for r in (dcre_ref, dcim_ref, dbbre_ref, dbbim_ref, dare_ref, daim_ref, dd_ref):
                r[...] = jnp.zeros_like(r)

        first = tr == nt - 1
        row = lax.broadcasted_iota(jnp.int32, (SUBLANES, JB_ST), 0)
        n_blk = tt // SUBLANES
        bs = list(range(n_seq))
        for b in bs:
            _load_chunked(dy_ref, b, dyp_ref, tt)
            _load_chunked(u_ref, b, up_ref, tt)
        for b in bs:
            dyb = dyp_ref[b].astype(BF16)
            lre_ref[b] = _dot(dyb, cre[0])
            lim_ref[b] = _dot(dyb, cimn[0])
        acc = {b: [jnp.zeros((SUBLANES, JB_ST), F32), jnp.zeros((SUBLANES, JB_ST), F32)] for b in bs}

        def on_block(b, i, lr, li):
            if i > 0:
                spr = sre_ref[b, (i - 1) * SUBLANES:i * SUBLANES, :]
                spi = sim_ref[b, (i - 1) * SUBLANES:i * SUBLANES, :]
            else:
                hr = jnp.where(first, 0.0, pre_ref[b, SUBLANES - 1:SUBLANES, :])
                hi = jnp.where(first, 0.0, pim_ref[b, SUBLANES - 1:SUBLANES, :])
                last_r = sre_ref[b, (n_blk - 1) * SUBLANES:n_blk * SUBLANES, :]
                last_i = sim_ref[b, (n_blk - 1) * SUBLANES:n_blk * SUBLANES, :]
                spr = jnp.where(row == 0, jnp.broadcast_to(hr, row.shape), pltpu.roll(last_r, 1, 0))
                spi = jnp.where(row == 0, jnp.broadcast_to(hi, row.shape), pltpu.roll(last_i, 1, 0))
            acc[b][0] = acc[b][0] + (lr * spr + li * spi)
            acc[b][1] = acc[b][1] + (li * spr - lr * spi)

        _chunk_scan(lre_ref, lim_ref, bs, car_ref, are[0], -aim[0], tt, reverse=True, on_block=on_block)
        for b in bs:
            dare_ref[...] += jnp.sum(acc[b][0], axis=0, keepdims=True)
            daim_ref[...] += jnp.sum(acc[b][1], axis=0, keepdims=True)
            dyp = dyp_ref[b]
            up = up_ref[b]
            dyb = dyp.astype(BF16)
            ub = up.astype(BF16)
            lrb = lre_ref[b].astype(BF16)
            lib = lim_ref[b].astype(BF16)
            dup = d_ref[...] * dyp + _dot_nt(lrb, bbre[0]) + _dot_nt(lib, bbim[0])
            _store_chunked(dup, du_ref, b, tt)
            dbbre_ref[0] += _dot_tn(ub, lrb)
            dbbim_ref[0] += _dot_tn(ub, lib)
            dcre_ref[0] += _dot_tn(dyb, sre_ref[b].astype(BF16))
            dcim_ref[0] += _dot_tn(dyb, sim_ref[b].astype(BF16))
            dd_ref[...] += jnp.sum(dyp * up, axis=0, keepdims=True)

        @pl.when((j == N_JBLK - 1) & (tr == nt - 1))
        def _():
            mine, sends = exchange()
            for cp in sends + mine:
                cp.wait()

    tok = lambda j, t: (0, nt - 1 - t, j)
    halo = lambda j, t: (0, jnp.maximum((nt - 1 - t) * rows8 - 1, 0), j)
    blk3 = lambda j, t: (j, 0, 0)
    row1 = lambda j, t: (0, j)
    acc_shape = _out((N_JBLK, JB_CH, JB_ST), F32)
    return _pcall(
        body, name="ssm_bwd", grid=(N_JBLK, nt),
        out_shape=(_out((n_seq, seq, SSM_W), F32), acc_shape, acc_shape, acc_shape, acc_shape,
                   _out((1, N_JBLK * JB_ST), F32), _out((1, N_JBLK * JB_ST), F32),
                   _out((1, SSM_W), F32),
                   _out((N_DEV,) + g_out.shape[1:], F32),
                   _out((N_DEV,) + g_glu.shape[1:], F32)),
        in_specs=[pl.BlockSpec((n_seq, tt, JB_CH), tok), pl.BlockSpec((n_seq, tt, JB_CH), tok),
                  pl.BlockSpec((n_seq, tt, JB_ST), tok), pl.BlockSpec((n_seq, tt, JB_ST), tok),
                  pl.BlockSpec((n_seq, SUBLANES, JB_ST), halo), pl.BlockSpec((n_seq, SUBLANES, JB_ST), halo),
                  pl.BlockSpec((1, JB_CH, JB_ST), blk3), pl.BlockSpec((1, JB_CH, JB_ST), blk3),
                  pl.BlockSpec((1, JB_CH, JB_ST), blk3), pl.BlockSpec((1, JB_CH, JB_ST), blk3),
                  pl.BlockSpec((1, JB_CH), row1), pl.BlockSpec((1, 1, JB_ST), blk3), pl.BlockSpec((1, 1, JB_ST), blk3),
                  HBM_SPEC, HBM_SPEC],
        out_specs=(pl.BlockSpec((n_seq, tt, JB_CH), tok),
                   pl.BlockSpec((1, JB_CH, JB_ST), blk3), pl.BlockSpec((1, JB_CH, JB_ST), blk3),
                   pl.BlockSpec((1, JB_CH, JB_ST), blk3), pl.BlockSpec((1, JB_CH, JB_ST), blk3),
                   pl.BlockSpec((1, JB_ST), row1), pl.BlockSpec((1, JB_ST), row1), pl.BlockSpec((1, JB_CH), row1),
                   HBM_SPEC, HBM_SPEC),
        scratch_shapes=[pltpu.VMEM((n_seq, tt, JB_ST), F32), pltpu.VMEM((n_seq, tt, JB_ST), F32),
                        pltpu.VMEM((n_seq, tt, JB_CH), F32), pltpu.VMEM((n_seq, tt, JB_CH), F32),
                        pltpu.VMEM((n_seq, SUBLANES, JB_ST), F32),
                        pltpu.SemaphoreType.DMA((7 * 2,)), pltpu.SemaphoreType.DMA((7 * 2,)),
                        pltpu.SemaphoreType.DMA((2,))],
        compiler_params=_params(2),
    )(dy, u, s_re, s_im, s_re, s_im, bb_re, bb_im, c_re_t, c_imn_t, d_row, ab_re, ab_im, g_out, g_glu)


def _mix(x2, tgt2, y, proj, gf, b_glu, conv8, w_glu_f, w_out_f, seq):
    n = x2.shape[0]
    tm = TOK_TILE
    tiles_per_seq = seq // tm
    rows8 = tm // SUBLANES

    def body(x_ref, t_ref, y_ref, zs_ref, h_ref, bc_ref, cc_ref, zc_ref, hp_ref, ccp_ref,
             gf_ref, bg_ref, cw_ref, wg_ref, wo_ref,
             dh2_ref, dy_ref, dzs_ref, dbc_ref, dzc_ref, dyc_ref,
             dwo_ref, dwg_ref, loss_ref, dgf_ref, dbg_ref, dcw_ref):
        i = pl.program_id(0)

        @pl.when(i == 0)
        def _():
            for r in (dwo_ref, dwg_ref, loss_ref, dgf_ref, dbg_ref, dcw_ref):
                r[...] = jnp.zeros_like(r)

        yv = y_ref[...]
        y1, dgelu = _gelu_and_grad(yv)
        y1b = y1.astype(BF16)
        gate = _sigmoid(_dot(y1b, wg_ref[...]) + bg_ref[...])
        y2 = y1 * gate
        szs, dszs = _silu_and_grad(zs_ref[...])
        yssm = y2 * szs
        hv = h_ref[...]
        ccv = cc_ref[...]
        bcv = bc_ref[...]
        v = ccv * hv
        first = (i % tiles_per_seq) == 0
        vhalo = jnp.where(first, 0.0, ccp_ref[...] * hp_ref[...])
        v1 = _shift_down(v, vhalo, 1)
        v2 = _shift_down(v, vhalo, 2)
        w0 = cw_ref[0:1, :]
        w1 = cw_ref[1:2, :]
        w2 = cw_ref[2:3, :]
        yc = w0 * v2 + w1 * v1 + w2 * v
        szc, dszc = _silu_and_grad(zc_ref[...])
        yconv = (bcv * yc) * szc
        ysb = yssm.astype(BF16)
        ycb = yconv.astype(BF16)
        h2 = x_ref[...] + _dot(ysb, wo_ref[0:SSM_W, :]) + _dot(ycb, wo_ref[SSM_W:, :])
        r2 = lax.rsqrt(jnp.mean(h2 * h2, axis=-1, keepdims=True) + EPS)
        hn = h2 * r2
        gfv = gf_ref[...]
        err = hn * gfv - t_ref[...]
        loss_ref[...] += 0.5 * jnp.sum(jnp.mean(err * err, axis=-1, keepdims=True))
        dout = err * (1.0 / D_MODEL)
        dgf_ref[...] += jnp.sum(dout * hn, axis=0, keepdims=True)
        dn = dout * gfv
        dh2 = r2 * (dn - hn * jnp.mean(dn * hn, axis=-1, keepdims=True))
        dh2_ref[...] = dh2
        dh2b = dh2.astype(BF16)
        dwo_ref[0:SSM_W, :] += _dot_tn(ysb, dh2b)
        dwo_ref[SSM_W:, :] += _dot_tn(ycb, dh2b)
        dyssm = _dot_nt(dh2b, wo_ref[0:SSM_W, :])
        dyconv = _dot_nt(dh2b, wo_ref[SSM_W:, :])
        dy2 = dyssm * szs
        dzs_ref[...] = (dyssm * y2 * dszs).astype(BF16)
        dgp = dy2 * y1 * (gate * (1.0 - gate))
        dgpb = dgp.astype(BF16)
        dy1 = dy2 * gate + _dot_nt(dgpb, wg_ref[...])
        dwg_ref[...] += _dot_tn(y1b, dgpb)
        dbg_ref[...] += jnp.sum(dgp, axis=0, keepdims=True)
        dy_ref[...] = dy1 * dgelu
        dbc_ref[...] = (dyconv * yc * szc).astype(BF16)
        dyc = dyconv * bcv * szc
        dyc_ref[...] = dyc
        dzc_ref[...] = (dyconv * bcv * yc * dszc).astype(BF16)
        dcw_ref[0:1, :] += jnp.sum(dyc * v2, axis=0, keepdims=True)
        dcw_ref[1:2, :] += jnp.sum(dyc * v1, axis=0, keepdims=True)
        dcw_ref[2:3, :] += jnp.sum(dyc * v, axis=0, keepdims=True)

    tile_d = pl.BlockSpec((tm, D_MODEL), lambda i: (i, 0))
    tile_s = pl.BlockSpec((tm, SSM_W), lambda i: (i, 0))
    seg_of = lambda c: pl.BlockSpec((tm, SSM_W), lambda i: (i, c))
    halo_of = lambda c: pl.BlockSpec((SUBLANES, SSM_W), lambda i: (jnp.maximum(i * rows8 - 1, 0), c))
    const = lambda shape: pl.BlockSpec(shape, lambda i: (0,) * len(shape))
    seg = _out((n, SSM_W), F32)
    seg_b = _out((n, SSM_W), BF16)
    return _pcall(
        body, name="mix", grid=(n // tm,),
        out_shape=(_out((n, D_MODEL), F32), seg, seg_b, seg_b, seg_b, seg,
                   _out((D_MODEL, D_MODEL), F32), _out((SSM_W, SSM_W), F32),
                   _out((SUBLANES, LANES), F32), _out((1, D_MODEL), F32),
                   _out((1, SSM_W), F32), _out((SUBLANES, CONV_W), F32)),
        in_specs=[tile_d, tile_d, tile_s, seg_of(SEG_ZS), seg_of(SEG_H), seg_of(SEG_BC), seg_of(SEG_CC), seg_of(SEG_ZC),
                  halo_of(SEG_H), halo_of(SEG_CC),
                  const((1, D_MODEL)), const((1, SSM_W)), const((SUBLANES, CONV_W)),
                  const((SSM_W, SSM_W)), const((D_MODEL, D_MODEL))],
        out_specs=(tile_d, tile_s, tile_s, tile_s, tile_s, tile_s,
                   const((D_MODEL, D_MODEL)), const((SSM_W, SSM_W)), const((SUBLANES, LANES)),
                   const((1, D_MODEL)), const((1, SSM_W)), const((SUBLANES, CONV_W))),
        compiler_params=_params(1),
    )(x2, tgt2, y, proj, proj, proj, proj, proj, proj, proj, gf, b_glu, conv8, w_glu_f, w_out_f)


def _in_bwd(x2, dh2, du, dzs, dyc, proj, dbc, dzc, g1, conv8, w_full, seq):
    n = x2.shape[0]
    tm = TOK_TILE
    n_tiles = n // tm
    tiles_per_seq = seq // tm
    rows8 = tm // SUBLANES
    n_blk8 = n // SUBLANES

    def body(x_ref, dh2_ref, du_ref, dzs_ref, dyc_ref, dycn_ref, h_ref, cc_ref, dbc_ref, dzc_ref,
             g_ref, cw_ref, w_ref, gx_ref, dp_ref, dg_ref):
        i = pl.program_id(0)

        @pl.when(i == 0)
        def _():
            dg_ref[...] = jnp.zeros_like(dg_ref)

        dyc = dyc_ref[...]
        last = (i % tiles_per_seq) == tiles_per_seq - 1
        nhalo = jnp.where(last, 0.0, dycn_ref[...])
        dv = (cw_ref[2:3, :] * dyc + cw_ref[1:2, :] * _shift_up(dyc, nhalo, 1)
              + cw_ref[0:1, :] * _shift_up(dyc, nhalo, 2))
        parts = (du_ref[...], dzs_ref[...], dv * cc_ref[...], dbc_ref[...], dv * h_ref[...], dzc_ref[...])
        dxn = jnp.zeros((tm, D_MODEL), F32)
        for k, p in enumerate(parts):
            pb = p.astype(BF16)
            dp_ref[:, k * SSM_W:(k + 1) * SSM_W] = pb
            dxn = dxn + _dot_nt(pb, w_ref[:, k * SSM_W:(k + 1) * SSM_W])
        x = x_ref[...]
        r = lax.rsqrt(jnp.mean(x * x, axis=-1, keepdims=True) + EPS)
        xh = x * r
        dg_ref[...] += jnp.sum(dxn * xh, axis=0, keepdims=True)
        dn = dxn * g_ref[...]
        gx_ref[...] = dh2_ref[...] + r * (dn - xh * jnp.mean(dn * xh, axis=-1, keepdims=True))

    tile_d = pl.BlockSpec((tm, D_MODEL), lambda i: (i, 0))
    tile_s = pl.BlockSpec((tm, SSM_W), lambda i: (i, 0))
    seg_of = lambda c: pl.BlockSpec((tm, SSM_W), lambda i: (i, c))
    nhalo = pl.BlockSpec((SUBLANES, SSM_W), lambda i: (jnp.minimum((i + 1) * rows8, n_blk8 - 1), 0))
    const = lambda shape: pl.BlockSpec(shape, lambda i: (0,) * len(shape))
    return _pcall(
        body, name="in_bwd", grid=(n_tiles,),
        out_shape=(_out((n, D_MODEL), F32), _out((n, IN_COLS), BF16),
                   _out((SUBLANES, D_MODEL), F32)),
        in_specs=[tile_d, tile_d, tile_s, tile_s, tile_s, nhalo, seg_of(SEG_H), seg_of(SEG_CC), tile_s, tile_s,
                  const((1, D_MODEL)), const((SUBLANES, CONV_W)), const((D_MODEL, IN_COLS))],
        out_specs=(tile_d, pl.BlockSpec((tm, IN_COLS), lambda i: (i, 0)), const((SUBLANES, D_MODEL))),
        compiler_params=_params(1),
    )(x2, dh2, du, dzs, dyc, dyc, proj, proj, dbc, dzc, g1, conv8, w_full)


_HALF_BLOCKS = ((0, 0), (0, 1), (1, 0), (2, 0), (1, 1), (2, 1), (3, 0), (3, 1))


def _dw_in_exchange(chips, xn, dproj, smalls):
    n = xn.shape[0]
    tk = min(1024, n)
    nk = n // tk
    piece = (D_MODEL, COLS_PER_DEV)
    hr = D_MODEL // 2
    n_half = len(_HALF_BLOCKS)
    n_small = len(smalls)
    assert _HALF_BLOCKS[0][1] == 0 and _HALF_BLOCKS[1][1] == 1
    order = jnp.stack([chips[b] for b, _ in _HALF_BLOCKS]
                      + [jnp.int32(t) for _, t in _HALF_BLOCKS]).astype(jnp.int32)

    def body(order_ref, xn_hbm, dp_ref, *refs):
        sm_refs = refs[:n_small]
        own_ref, rchip_ref = refs[n_small:n_small + 2]
        rsm_refs = refs[n_small + 2:2 * n_small + 2]
        (xn_ref, acc, stage, rbuf, kbuf, relay_in, xn_sems, give_send, give_recv, keep_send, keep_recv,
         relay_send, relay_recv, sm_send, sm_recv, sm_loc) = refs[2 * n_small + 2:]
        s = pl.program_id(0)

        def xn_copy(kk, t):
            rows = pl.ds(pl.multiple_of(kk * tk, tk), tk)
            return pltpu.make_async_copy(xn_hbm.at[rows, t * hr:(t + 1) * hr], xn_ref.at[t, rows, :],
                                         xn_sems.at[2 * kk + t])

        @pl.when(s == 0)
        def _():
            for kk in range(nk):
                for t in range(2):
                    xn_copy(kk, t).start()
            xn_copy(0, 0).wait()

        @pl.when(s == 1)
        def _():
            xn_copy(0, 1).wait()

        x, y, c = _mesh_pos()
        sib = (x, y, 1 - c)
        y_nbr, x_nbr = (x, 1 - y, c), (1 - x, y, c)
        gather = _TwoLevelGather(list(sm_refs), [functools.partial(lambda r, dev: r.at[dev], r) for r in rsm_refs],
                                 sm_send, sm_recv, sm_loc)

        def give(h):
            cols = pl.ds(pl.multiple_of((1 - c) * COLS_PER_DEV, LANES), COLS_PER_DEV)
            return pltpu.make_async_remote_copy(src_ref=acc.at[h % 2, :, cols], dst_ref=stage.at[h],
                                                send_sem=give_send.at[h], recv_sem=give_recv.at[h],
                                                device_id=sib, device_id_type=MESH)

        def relay(r):
            return pltpu.make_async_remote_copy(src_ref=rbuf.at[r], dst_ref=relay_in.at[r],
                                                send_sem=relay_send.at[r], recv_sem=relay_recv.at[r],
                                                device_id=(x_nbr, y_nbr)[r], device_id_type=MESH)

        def keep(q):
            return pltpu.make_async_remote_copy(src_ref=kbuf.at[q], dst_ref=rchip_ref.at[q // 2, pl.ds((q % 2) * hr, hr), :],
                                                send_sem=keep_send.at[q], recv_sem=keep_recv.at[q],
                                                device_id=(y_nbr, x_nbr)[q // 2], device_id_type=MESH)

        def chip_sum(h):
            give(h).wait_recv()
            mine = [acc[h % 2, :, cc * COLS_PER_DEV:(cc + 1) * COLS_PER_DEV] for cc in range(2)]
            return jnp.where(c == 0, mine[0], mine[1]) + stage[h]

        @pl.when(s == 0)
        def _():
            gather.start()

        @pl.when(s == 2)
        def _():
            gather.neighbours_landed()

        @pl.when(s == n_half - 2)
        def _():
            gather.diagonal_landed()

        for k in range(2, n_half):
            @pl.when(s == k)
            def _(k=k):
                give(k - 2).wait_send()

        slot = s % 2
        t_half = order_ref[n_half + s]
        acc[slot] = _dot_tn(xn_ref[t_half, pl.ds(0, tk), :], dp_ref[pl.ds(0, tk), :])

        def kstep(kk, carry):
            for t in range(2):
                @pl.when(s == t)
                def _(t=t):
                    xn_copy(kk, t).wait()

            off = pl.multiple_of(kk * tk, tk)
            acc[slot] += _dot_tn(xn_ref[t_half, pl.ds(off, tk), :], dp_ref[pl.ds(off, tk), :])
            return carry

        n_first = max(1, nk // 2)
        lax.fori_loop(1, n_first, kstep, 0)
        for k in range(1, n_half):
            @pl.when(s == k)
            def _(k=k):
                h = k - 1
                b, t = _HALF_BLOCKS[h]
                total = chip_sum(h)
                if b == 0:
                    rbuf[t] = total.astype(BF16)
                    relay(t).start()
                elif b < 3:
                    if (b, t) in ((1, 0), (2, 1)):
                        relay(t).wait_recv()
                        total = total + relay_in[t].astype(F32)
                    q = 2 * (b - 1) + t
                    kbuf[q] = total.astype(BF16)
                    keep(q).start()
                else:
                    own_ref[0:hr, :] = total

        lax.fori_loop(n_first, nk, kstep, 0)

        for k in range(n_half):
            @pl.when(s == k)
            def _(k=k):
                give(k).start()

        @pl.when(s == n_half - 1)
        def _():
            own_ref[hr:D_MODEL, :] = chip_sum(n_half - 1)
            give(n_half - 2).wait_send()
            give(n_half - 1).wait_send()
            for r in range(2):
                relay(r).wait_send()
            for q in range(4):
                keep(q).wait()
            gather.finish()

    half_piece = (hr, COLS_PER_DEV)
    grid_spec = pltpu.PrefetchScalarGridSpec(
        num_scalar_prefetch=1, grid=(n_half,),
        in_specs=[HBM_SPEC,
                  pl.BlockSpec((n, COLS_PER_CHIP), lambda s, order: (0, order[s])),
                  *([HBM_SPEC] * n_small)],
        out_specs=(pl.BlockSpec(piece, lambda s, order: (0, 0)), HBM_SPEC, *([HBM_SPEC] * n_small)),
        scratch_shapes=[pltpu.VMEM((2, n, hr), BF16),
                        pltpu.VMEM((2, hr, COLS_PER_CHIP), F32), pltpu.VMEM((n_half,) + half_piece, F32),
                        pltpu.VMEM((2,) + half_piece, BF16), pltpu.VMEM((4,) + half_piece, BF16),
                        pltpu.VMEM((2,) + half_piece, BF16),
                        pltpu.SemaphoreType.DMA((2 * nk,)),
                        pltpu.SemaphoreType.DMA((n_half,)), pltpu.SemaphoreType.DMA((n_half,)),
                        pltpu.SemaphoreType.DMA((4,)), pltpu.SemaphoreType.DMA((4,)),
                        pltpu.SemaphoreType.DMA((2,)), pltpu.SemaphoreType.DMA((2,)),
                        pltpu.SemaphoreType.DMA((7 * n_small,)), pltpu.SemaphoreType.DMA((7 * n_small,)),
                        pltpu.SemaphoreType.DMA((n_small,))])
    return _pcall(
        body, name="dw_in_exchange", grid_spec=grid_spec,
        out_shape=(_out(piece, F32), _out((2,) + piece, BF16),
                   *(_out((N_DEV,) + a.shape, a.dtype) for a in smalls)),
        compiler_params=_params(1),
    )(order, xn, dproj, *smalls)


def _adamw(g, w, m, v):
    m_new = ADAM_B1 * m + (1.0 - ADAM_B1) * g
    v_new = ADAM_B2 * v + (1.0 - ADAM_B2) * (g * g)
    m_hat = m_new / (1.0 - ADAM_B1 ** ADAM_STEP)
    v_hat = v_new / (1.0 - ADAM_B2 ** ADAM_STEP)
    delta = -ADAM_LR * (m_hat / (jnp.sqrt(v_hat) + ADAM_EPS) + ADAM_WD * w)
    return delta, m_new, v_new


def _reduce_adam_w_in(own, rchip, w, m, v):
    rows, cols = w.shape
    row_tile = 256

    def body(o_ref, r_ref, w_ref, m_ref, v_ref, g_ref, d_ref, nm_ref, nv_ref):
        g = o_ref[...]
        for s in range(2):
            g = g + r_ref[s].astype(F32)
        g_ref[...] = g
        d_ref[...], nm_ref[...], nv_ref[...] = _adamw(g, w_ref[...], m_ref[...], v_ref[...])

    tile = pl.BlockSpec((row_tile, cols), lambda i: (i, 0))
    shp = _out((rows, cols), F32)
    return _pcall(
        body, name="reduce_adam_w_in", grid=(rows // row_tile,),
        out_shape=(shp,) * 4,
        in_specs=[tile, pl.BlockSpec((2, row_tile, cols), lambda i: (0, i, 0)), tile, tile, tile],
        out_specs=(tile,) * 4,
        compiler_params=_params(1),
    )(own, rchip, w, m, v)


_SMALL_LEAVES = ("norm_gain", "final_norm_gain", "b_glu", "ssm_a_re", "ssm_a_im", "ssm_log_dt", "ssm_d", "conv_w",
                 "ssm_c_re", "ssm_c_im", "ssm_b_re", "ssm_b_im")


def _reduce_adam_small(r_pack, r_gc, r_gb, wmv, sharded):
    n_leaf = len(_SMALL_LEAVES)
    n_sh = len(sharded)

    def body(*refs):
        rp_ref, rgc_ref, rgb_ref = refs[:3]
        w_refs = refs[3:3 + 3 * n_leaf]
        sh_in = refs[3 + 3 * n_leaf:3 + 3 * n_leaf + 4 * n_sh]
        outs0 = 3 + 3 * n_leaf + 4 * n_sh
        loss_ref = refs[outs0]
        o_refs = refs[outs0 + 1:outs0 + 1 + 4 * n_leaf]
        sh_out = refs[outs0 + 1 + 4 * n_leaf:outs0 + 1 + 4 * n_leaf + 4 * n_sh]
        own_conv = refs[-1]

        def total(ref):
            acc = ref[0].astype(F32)
            for s in range(1, N_DEV):
                acc = acc + ref[s].astype(F32)
            return acc

        for i in range(n_sh):
            r_ref, w_ref, m_ref, v_ref = sh_in[4 * i:4 * i + 4]
            o_g, o_d, o_m, o_v = sh_out[4 * i:4 * i + 4]
            g = total(r_ref)
            o_g[...] = g
            o_d[...], o_m[...], o_v[...] = _adamw(g, w_ref[...], m_ref[...], v_ref[...])

        sp = total(rp_ref)
        sgc = total(rgc_ref)
        sgb = total(rgb_ref)
        loss_ref[...] = sp[ROW_LOSS:ROW_LOSS + 1, 0:1]

        def wide(r):
            return jnp.concatenate([sp[r:r + 1, :], sp[r + 1:r + 2, :]], axis=1)

        s5 = slice(ROW_S5, ROW_S5 + N_GROUPS)
        eye = (lax.broadcasted_iota(jnp.int32, (N_GROUPS, N_GROUPS), 0)
               == lax.broadcasted_iota(jnp.int32, (N_GROUPS, N_GROUPS), 1)).astype(F32)
        d_rows = jnp.broadcast_to(sp[ROW_BGLU_D + 1:ROW_BGLU_D + 2, :], (GROUP, SSM_W))
        own_p = (lax.broadcasted_iota(jnp.int32, (GROUP, SSM_W), 1) % GROUP
                 == lax.broadcasted_iota(jnp.int32, (GROUP, SSM_W), 0))
        of_group = (lax.broadcasted_iota(jnp.int32, (SSM_W, N_GROUPS), 0) // GROUP
                    == lax.broadcasted_iota(jnp.int32, (SSM_W, N_GROUPS), 1)).astype(BF16)
        d_pg = sum(_dot(t, of_group) for t in _split3(jnp.where(own_p, d_rows, 0.0)))
        me = 4 * lax.axis_index("x") + 2 * lax.axis_index("y") + lax.axis_index("c")
        for k in range(N_DEV):
            @pl.when(me == k)
            def _(k=k):
                own_conv[...] = sp[ROW_CONV:ROW_CONV + SUBLANES, k * CONV_COLS_PER_DEV:(k + 1) * CONV_COLS_PER_DEV]
        grads = {
            "norm_gain": wide(ROW_NORM_GAIN),
            "final_norm_gain": wide(ROW_FINAL_GAIN),
            "b_glu": sp[ROW_BGLU_D:ROW_BGLU_D + 1, :],
            "ssm_a_re": sp[s5, LANE_A_RE:LANE_A_RE + STATE],
            "ssm_a_im": sp[s5, LANE_A_IM:LANE_A_IM + STATE],
            "ssm_log_dt": jnp.sum(sp[s5, LANE_LOG_DT:LANE_LOG_DT + 1] * eye, axis=0, keepdims=True),
            "ssm_d": d_pg,
            "ssm_c_re": sgc[:, 0:STATE],
            "ssm_c_im": sgc[:, STATE:2 * STATE],
            "ssm_b_re": sgb[:, 0:STATE],
            "ssm_b_im": sgb[:, STATE:2 * STATE],
        }
        for i, name in enumerate(_SMALL_LEAVES):
            w_ref, m_ref, v_ref = w_refs[3 * i:3 * i + 3]
            o_g, o_d, o_m, o_v = o_refs[4 * i:4 * i + 4]
            if name == "conv_w":
                for k in range(w_ref.shape[0]):
                    g = own_conv[k:k + 1, :]
                    o_g[k] = g
                    o_d[k], o_m[k], o_v[k] = _adamw(g, w_ref[k], m_ref[k], v_ref[k])
                continue
            g = grads[name]
            o_g[...] = g
            o_d[...], o_m[...], o_v[...] = _adamw(g, w_ref[...], m_ref[...], v_ref[...])

    flat_w = [a for name in _SMALL_LEAVES for a in wmv[name]]
    leaf_shapes = [_out(wmv[name][0].shape, F32) for name in _SMALL_LEAVES for _ in range(4)]
    sh_shapes = [_out(entry[1].shape, F32) for entry in sharded for _ in range(4)]
    operands = (r_pack, r_gc, r_gb, *flat_w, *(a for entry in sharded for a in entry))
    out_shape = (_out((1, 1), F32), *leaf_shapes, *sh_shapes)
    outs = _pcall(
        body, name="reduce_adam_small", grid=(1,), out_shape=out_shape,
        in_specs=_whole_specs(operands), out_specs=tuple(_whole_specs(out_shape)),
        scratch_shapes=[pltpu.VMEM((SUBLANES, CONV_COLS_PER_DEV), F32)],
        compiler_params=_params(1),
    )(*operands)
    leaves = {name: outs[1 + 4 * i:5 + 4 * i] for i, name in enumerate(_SMALL_LEAVES)}
    first = 1 + 4 * n_leaf
    return outs[0], leaves, [outs[first + 4 * i:first + 4 * i + 4] for i in range(n_sh)]


def kernel(x, norm_gain, w_in, ssm_a_re, ssm_a_im, ssm_log_dt, ssm_b_re, ssm_b_im, ssm_c_re, ssm_c_im, ssm_d, w_glu, b_glu, conv_w, w_out, final_norm_gain, loss_target, m_norm_gain, m_w_in, m_ssm_a_re, m_ssm_a_im, m_ssm_log_dt, m_ssm_b_re, m_ssm_b_im, m_ssm_c_re, m_ssm_c_im, m_ssm_d, m_w_glu, m_b_glu, m_conv_w, m_w_out, m_final_norm_gain, v_norm_gain, v_w_in, v_ssm_a_re, v_ssm_a_im, v_ssm_log_dt, v_ssm_b_re, v_ssm_b_im, v_ssm_c_re, v_ssm_c_im, v_ssm_d, v_w_glu, v_b_glu, v_conv_w, v_w_out, v_final_norm_gain):
    n_seq, seq, _ = x.shape
    n = n_seq * seq

    gh_p = lambda b4: jnp.transpose(b4, (0, 1, 3, 2)).reshape(N_GROUPS * GROUP, STATE)
    c2 = lambda a: a.reshape(N_GROUPS * GROUP, STATE)
    b_re2, b_im2 = gh_p(ssm_b_re), gh_p(ssm_b_im)
    d_row = ssm_d[0].reshape(1, SSM_W)

    x2 = x.reshape(n, D_MODEL)
    tgt2 = loss_target.reshape(n, D_MODEL)
    mx, my, mc = lax.axis_index("x"), lax.axis_index("y"), lax.axis_index("c")
    chip_ids = [2 * cx + cy for cx, cy in ((mx, my), (1 - mx, my), (mx, 1 - my), (1 - mx, 1 - my))]
    arrival = chip_ids
    xn, proj, w_in_f, s5 = _in_proj(
        jnp.stack(arrival).astype(jnp.int32), x2, norm_gain, w_in[0],
        (ssm_a_re[0], ssm_a_im[0], ssm_log_dt, b_re2, b_im2, c2(ssm_c_re), c2(ssm_c_im)))
    a_re_x, a_im_x, log_dt_x, ab_re, ab_im, bb_re_m, bb_im_m, c_re_m, c_imn_m = s5
    u3 = proj.reshape(n_seq, seq, IN_COLS)
    conv_p = jnp.pad(conv_w[0], ((0, SUBLANES - 3), (0, LANES - CONV_COLS_PER_DEV)))
    s_re, s_im, y3, w_out_f, w_glu_f, conv_all = _ssm_fwd(
        u3, bb_re_m, bb_im_m, c_re_m, c_imn_m, d_row, ab_re, ab_im,
        w_out[0], w_glu[0], conv_p, n_seq, seq)
    conv8 = jnp.transpose(conv_all[:, :, :CONV_COLS_PER_DEV], (1, 0, 2)).reshape(SUBLANES, CONV_W)
    (dh2, dy, dzs, dbc, dzc, dyc, dw_out, dw_glu, loss_t, dgf, dbg, dcw) = _mix(
        x2, tgt2, y3.reshape(n, SSM_W), proj, final_norm_gain.reshape(1, D_MODEL), b_glu, conv8,
        w_glu_f, w_out_f, seq)

    du3, dc_re_d, dc_im_d, dbb_re_d, dbb_im_d, dab_re, dab_im, dd, r_out, r_glu = _ssm_bwd(
        dy.reshape(n_seq, seq, SSM_W), u3, s_re, s_im, bb_re_m, bb_im_m, c_re_m, c_imn_m, d_row, ab_re, ab_im,
        dw_out.reshape(N_DEV, OUT_ROWS_PER_DEV, D_MODEL), dw_glu.reshape(N_DEV, GLU_ROWS_PER_DEV, SSM_W), n_seq, seq)
    du = du3.reshape(n, SSM_W)
    grad_x2, dproj, dg8 = _in_bwd(x2, dh2, du, dzs, dyc, proj, dbc, dzc, norm_gain, conv8, w_in_f, seq)
    pack, gc, gb = _ssm_disc_bwd_pack(
        a_re_x, a_im_x, log_dt_x, b_re2, b_im2, dab_re, dab_im,
        dbb_re_d, dbb_im_d, loss_t, dg8, dgf, dbg, dd, dcw, dc_re_d, dc_im_d)

    own_in, rchip_in, r_pack, r_gc, r_gb = _dw_in_exchange(
        [chip_ids[3], chip_ids[2], chip_ids[1], chip_ids[0]],
        xn, dproj, [pack, gc, gb])

    flat2 = lambda a: a.reshape(a.shape[-2:]) if a.ndim > 2 else a.reshape(1, -1)
    c2 = lambda a: a.reshape(N_GROUPS * GROUP, STATE)
    wmv = dict(norm_gain=(norm_gain, m_norm_gain, v_norm_gain),
               final_norm_gain=tuple(flat2(a) for a in (final_norm_gain, m_final_norm_gain, v_final_norm_gain)),
               b_glu=(b_glu, m_b_glu, v_b_glu),
               ssm_a_re=tuple(flat2(a) for a in (ssm_a_re, m_ssm_a_re, v_ssm_a_re)),
               ssm_a_im=tuple(flat2(a) for a in (ssm_a_im, m_ssm_a_im, v_ssm_a_im)),
               ssm_log_dt=(ssm_log_dt, m_ssm_log_dt, v_ssm_log_dt),
               ssm_d=tuple(jnp.transpose(a, (0, 2, 1)).reshape(GROUP, N_GROUPS) for a in (ssm_d, m_ssm_d, v_ssm_d)),
               conv_w=tuple(jnp.transpose(a, (1, 0, 2)) for a in (conv_w, m_conv_w, v_conv_w)),
               ssm_c_re=tuple(c2(a) for a in (ssm_c_re, m_ssm_c_re, v_ssm_c_re)),
               ssm_c_im=tuple(c2(a) for a in (ssm_c_im, m_ssm_c_im, v_ssm_c_im)),
               ssm_b_re=(b_re2, gh_p(m_ssm_b_re), gh_p(v_ssm_b_re)),
               ssm_b_im=(b_im2, gh_p(m_ssm_b_im), gh_p(v_ssm_b_im)))

    res_in = _reduce_adam_w_in(own_in, rchip_in, w_in[0], m_w_in[0], v_w_in[0])
    loss11, small, (res_out, res_glu) = _reduce_adam_small(
        r_pack, r_gc, r_gb, wmv,
        [(r_out, w_out[0], m_w_out[0], v_w_out[0]), (r_glu, w_glu[0], m_w_glu[0], v_w_glu[0])])
    loss = loss11.reshape(())

    shapes = dict(norm_gain=(1, D_MODEL), ssm_a_re=(1, N_GROUPS, STATE), ssm_a_im=(1, N_GROUPS, STATE),
                  ssm_log_dt=(1, N_GROUPS), ssm_c_re=(1, N_GROUPS, GROUP, STATE), ssm_c_im=(1, N_GROUPS, GROUP, STATE),
                  b_glu=(1, SSM_W), final_norm_gain=(D_MODEL,))
    big = dict(w_in=res_in, w_glu=res_glu, w_out=res_out)

    def leaf(kind, name):
        if name in big:
            return big[name][kind][None]
        if name in ("ssm_b_re", "ssm_b_im"):
            return jnp.transpose(small[name][kind].reshape(1, N_GROUPS, GROUP, STATE), (0, 1, 3, 2))
        if name == "ssm_d":
            return jnp.transpose(small[name][kind].reshape(1, GROUP, N_GROUPS), (0, 2, 1))
        if name == "conv_w":
            return jnp.transpose(small[name][kind], (1, 0, 2))
        return small[name][kind].reshape(shapes[name])

    order = ["norm_gain", "w_in", "ssm_a_re", "ssm_a_im", "ssm_log_dt", "ssm_b_re", "ssm_b_im", "ssm_c_re",
             "ssm_c_im", "ssm_d", "w_glu", "b_glu", "conv_w", "w_out", "final_norm_gain"]
    outs = [loss, grad_x2.reshape(x.shape)]
    for kind in range(4):
        outs += [leaf(kind, name) for name in order]
    return tuple(outs)
```

```python
import functools
import math

import jax
import jax.numpy as jnp
from jax import lax
from jax.experimental import pallas as pl
from jax.experimental.pallas import tpu as pltpu

F32 = jnp.float32
BF16 = jnp.bfloat16

N_DEV = 8
D_MODEL = 1024
SSM_W = 512
CONV_W = 512
N_GROUPS = 32
GROUP = 16
STATE = 64
IN_COLS = 3072
SEG_U, SEG_ZS, SEG_H, SEG_BC, SEG_CC, SEG_ZC = range(6)
COLS_PER_DEV = IN_COLS // N_DEV
N_CHIP = N_DEV // 2
COLS_PER_CHIP = 2 * COLS_PER_DEV
OUT_ROWS_PER_DEV = D_MODEL // N_DEV
GLU_ROWS_PER_DEV = SSM_W // N_DEV
CONV_COLS_PER_DEV = CONV_W // N_DEV
EPS = 1e-6

N_JBLK = 4
JB_CH = SSM_W // N_JBLK
JB_ST = N_GROUPS * STATE // N_JBLK

ADAM_LR = 0.001
ADAM_B1 = 0.9
ADAM_B2 = 0.999
ADAM_EPS = 1e-08
ADAM_WD = 0.01
ADAM_STEP = 10

SUBLANES = 8
LANES = 128
VMEM_LIMIT = 48 * 1024 * 1024
TOK_TILE = 256
IN_TILE = 1024
SCAN_TILE = 1024

MESH = pl.DeviceIdType.MESH
HBM_SPEC = pl.BlockSpec(memory_space=pltpu.HBM)


def _build(body, **kw):
    return pl.pallas_call(body, **kw)


def _pcall(body, **kw):
    def call(*operands):
        pinned = [a if jnp.issubdtype(a.dtype, jnp.integer) else pltpu.with_memory_space_constraint(a, pltpu.HBM)
                  for a in operands]
        return _build(body, **kw)(*pinned)
    return call


def _whole_specs(arrays):
    return [pl.BlockSpec(a.shape, functools.partial(lambda nd, i: (0,) * nd, len(a.shape))) for a in arrays]


def _out(shape, dtype):
    return pltpu.HBM(tuple(shape), dtype)


def _params(n_grid):
    return pltpu.CompilerParams(dimension_semantics=("arbitrary",) * n_grid,
                                vmem_limit_bytes=VMEM_LIMIT)


def _dot(a, b):
    return jnp.dot(a, b, preferred_element_type=F32)


def _dot_nt(a, b):
    return lax.dot_general(a, b, (((1,), (1,)), ((), ())), preferred_element_type=F32)


def _dot_tn(a, b):
    return lax.dot_general(a, b, (((0,), (0,)), ((), ())), preferred_element_type=F32)


def _sigmoid(z):
    return 1.0 / (1.0 + jnp.exp(-z))


_GELU_C = math.sqrt(2.0 / math.pi)


def _gelu_and_grad(y):
    inner = _GELU_C * (y + 0.044715 * (y * y * y))
    t = jnp.tanh(inner)
    g = 0.5 * y * (1.0 + t)
    dg = 0.5 * (1.0 + t) + 0.5 * y * (1.0 - t * t) * (_GELU_C * (1.0 + 3.0 * 0.044715 * (y * y)))
    return g, dg


def _silu_and_grad(z):
    s = _sigmoid(z)
    return z * s, s * (1.0 + z * (1.0 - s))


def _shift_down(v, halo, k):
    rolled = pltpu.roll(v, k, 0)
    row = lax.broadcasted_iota(jnp.int32, v.shape, 0)
    for r in range(k):
        rolled = jnp.where(row == r, halo[SUBLANES - k + r:SUBLANES - k + r + 1, :], rolled)
    return rolled


def _shift_up(v, halo, k):
    n = v.shape[0]
    rolled = pltpu.roll(v, n - k, 0)
    row = lax.broadcasted_iota(jnp.int32, v.shape, 0)
    for r in range(k):
        rolled = jnp.where(row == n - k + r, halo[r:r + 1, :], rolled)
    return rolled


def _mesh_pos():
    return lax.axis_index("x"), lax.axis_index("y"), lax.axis_index("c")


def _direct_copies(srcs_for, out_refs, send_sems, recv_sems, loc_sems):
    x, y, c = _mesh_pos()
    me_id = 4 * x + 2 * y + c
    n_arr = len(out_refs)
    dsts = [r.at[me_id] for r in out_refs]
    own = srcs_for(me_id)
    mine = [pltpu.make_async_copy(own[a], dsts[a], loc_sems.at[a]) for a in range(n_arr)]
    sends = []
    for k in range(1, N_DEV):
        px, py, pc = x ^ ((k >> 2) & 1), y ^ ((k >> 1) & 1), c ^ (k & 1)
        src = srcs_for(4 * px + 2 * py + pc)
        for a in range(n_arr):
            sends.append(pltpu.make_async_remote_copy(
                src_ref=src[a], dst_ref=dsts[a],
                send_sem=send_sems.at[(k - 1) * n_arr + a], recv_sem=recv_sems.at[(k - 1) * n_arr + a],
                device_id=(px, py, pc), device_id_type=MESH))
    return mine, sends


class _TwoLevelGather:
    def __init__(self, srcs, slots, send_sems, recv_sems, loc_sems):
        self.srcs, self.slots, self.n_arr = srcs, slots, len(srcs)
        self.send_sems, self.recv_sems, self.loc_sems = send_sems, recv_sems, loc_sems
        x, y, c = _mesh_pos()
        self.c = c
        self.me, self.sib = (x, y, c), (x, y, 1 - c)
        self.chips = [(1 - x, y), (x, 1 - y), (1 - x, 1 - y)]

    def _copies(self, k, block, to, from_src=False):
        dev = 4 * block[0] + 2 * block[1] + block[2]
        return [pltpu.make_async_remote_copy(
            src_ref=self.srcs[a] if from_src else self.slots[a](dev), dst_ref=self.slots[a](dev),
            send_sem=self.send_sems.at[k * self.n_arr + a], recv_sem=self.recv_sems.at[k * self.n_arr + a],
            device_id=to, device_id_type=MESH) for a in range(self.n_arr)]

    def _local(self):
        dev = 4 * self.me[0] + 2 * self.me[1] + self.me[2]
        return [pltpu.make_async_copy(self.srcs[a], self.slots[a](dev), self.loc_sems.at[a])
                for a in range(self.n_arr)]

    def start(self):
        for cp in self._local() + self._copies(0, self.me, self.sib, True):
            cp.start()
        for j in (0, 1):
            for cp in self._copies(1 + j, self.me, (*self.chips[j], self.c), True):
                cp.start()

    def wait_own(self):
        for cp in self._local():
            cp.wait()

    def wait_sibling(self):
        for cp in self._copies(0, self.sib, self.me):
            cp.wait_recv()

    def wait_and_pass_on(self, j):
        chip = self.chips[j]
        for cp in self._copies(1 + j, (*chip, self.c), self.me):
            cp.wait_recv()
        for cp in self._copies(4 + j, (*chip, self.c), self.sib):
            cp.start()

    def neighbours_landed(self):
        x, y, c = self.me
        self.wait_and_pass_on(0)
        self.wait_and_pass_on(1)
        for cp in self._copies(1 + 2, (x ^ c, y ^ (1 - c), c), (x ^ (1 - c), y ^ c, c)):
            cp.start()

    def diagonal_landed(self):
        self.wait_and_pass_on(2)

    def wait_passed_on(self, j):
        for cp in self._copies(4 + j, (*self.chips[j], 1 - self.c), self.me):
            cp.wait_recv()

    def wait_sends(self):
        for cp in self._copies(0, self.me, self.sib, True):
            cp.wait_send()
        for j, chip in enumerate(self.chips):
            for cp in self._copies(1 + j, self.me, (*chip, self.c), True) + self._copies(4 + j, (*chip, self.c), self.sib):
                cp.wait_send()

    def finish(self):
        self.wait_sibling()
        for j in range(3):
            self.wait_passed_on(j)
        self.wait_sends()
        self.wait_own()


def _disc(a_re, a_im, log_dt, b_re, b_im):
    dt = jnp.exp(log_dt)
    mag = jnp.exp(a_re * dt)
    ab_re = mag * jnp.cos(a_im * dt)
    ab_im = mag * jnp.sin(a_im * dt)
    den = a_re * a_re + a_im * a_im
    p_re = ab_re - 1.0
    p_im = ab_im
    q_re = (p_re * a_re + p_im * a_im) / den
    q_im = (p_im * a_re - p_re * a_im) / den
    bb_re = q_re * b_re - q_im * b_im
    bb_im = q_re * b_im + q_im * b_re
    return ab_re, ab_im, bb_re, bb_im


def _split3(v):
    hi = v.astype(BF16)
    r1 = v - hi.astype(F32)
    mid = r1.astype(BF16)
    lo = (r1 - mid.astype(F32)).astype(BF16)
    return hi, mid, lo


def _select_dot(sel, v):
    return sum(_dot(sel, t) for t in _split3(v))


PACK_ROWS = 72
PACK_W = 512
ROW_FINAL_GAIN, ROW_NORM_GAIN, ROW_BGLU_D, ROW_CONV, ROW_LOSS, ROW_S5 = 0, 8, 16, 24, 32, 40
LANE_A_RE, LANE_A_IM, LANE_LOG_DT = 0, 128, 256


def _ssm_disc_bwd_pack(a_re_x, a_im_x, log_dt_x, b_re, b_im, g_ab_re, g_ab_im, dbb_re_d, dbb_im_d,
                       loss_t, dg8, dgf, dbg, dd, dcw, dc_re_d, dc_im_d):
    rows_gh = N_GROUPS * GROUP

    def body(are, aim, ldt, bre, bim, gabre, gabim, dbbre_ref, dbbim_ref,
             loss_ref, dg8_ref, dgf_ref, dbg_ref, dd_ref, dcw_ref, dcre_ref, dcim_ref,
             p_ref, gc_ref, gb_ref, gbb_re, gbb_im):
        r_g = lax.broadcasted_iota(jnp.int32, (N_GROUPS, rows_gh), 0)
        c_gh = lax.broadcasted_iota(jnp.int32, (N_GROUPS, rows_gh), 1)
        group_sum = (c_gh // GROUP == r_g).astype(BF16)
        r_gh = lax.broadcasted_iota(jnp.int32, (rows_gh, N_GROUPS), 0)
        c_g = lax.broadcasted_iota(jnp.int32, (rows_gh, N_GROUPS), 1)
        first_row = (r_gh == c_g * GROUP).astype(BF16)

        def diag_block(ref, j, gi):
            return ref[j, gi * GROUP:(gi + 1) * GROUP, gi * STATE:(gi + 1) * STATE]

        for j in range(N_JBLK):
            for gi in range(SUBLANES):
                r0 = (j * SUBLANES + gi) * GROUP
                gbb_re[r0:r0 + GROUP, :] = diag_block(dbbre_ref, j, gi)
                gbb_im[r0:r0 + GROUP, :] = diag_block(dbbim_ref, j, gi)
                both = jnp.concatenate([diag_block(dcre_ref, j, gi), -diag_block(dcim_ref, j, gi)], axis=1)
                gc_ref[r0:r0 + GROUP, :] = both.astype(BF16)

        def by_group(ref):
            return jnp.concatenate([ref[:, g * STATE:(g + 1) * STATE] for g in range(N_GROUPS)], axis=0)

        _, vjp = jax.vjp(_disc, are[...], aim[...], ldt[...], bre[...], bim[...])
        d_are, d_aim, d_ldt, d_bre, d_bim = vjp((_select_dot(first_row, by_group(gabre)),
                                                 _select_dot(first_row, by_group(gabim)),
                                                 gbb_re[...], gbb_im[...]))
        gb_ref[...] = jnp.concatenate([d_bre, d_bim], axis=1).astype(BF16)

        p_ref[...] = jnp.zeros_like(p_ref)
        half = D_MODEL // 2
        for r, src in ((ROW_FINAL_GAIN, dgf_ref), (ROW_NORM_GAIN, dg8_ref)):
            p_ref[r:r + 1, :] = src[0:1, 0:half]
            p_ref[r + 1:r + 2, :] = src[0:1, half:D_MODEL]
        p_ref[ROW_BGLU_D:ROW_BGLU_D + 1, :] = dbg_ref[...]
        p_ref[ROW_BGLU_D + 1:ROW_BGLU_D + 2, :] = dd_ref[...]
        p_ref[ROW_CONV:ROW_CONV + SUBLANES, :] = dcw_ref[...]
        p_ref[ROW_LOSS:ROW_LOSS + SUBLANES, 0:LANES] = loss_ref[...]
        s5 = slice(ROW_S5, ROW_S5 + N_GROUPS)
        p_ref[s5, LANE_A_RE:LANE_A_RE + STATE] = _select_dot(group_sum, d_are)
        p_ref[s5, LANE_A_IM:LANE_A_IM + STATE] = _select_dot(group_sum, d_aim)
        p_ref[s5, LANE_LOG_DT:LANE_LOG_DT + LANES] = _select_dot(group_sum, jnp.broadcast_to(d_ldt, (rows_gh, LANES)))

    operands = (a_re_x, a_im_x, log_dt_x, b_re, b_im, g_ab_re, g_ab_im, dbb_re_d, dbb_im_d,
                loss_t, dg8, dgf, dbg, dd, dcw, dc_re_d, dc_im_d)
    out_shape = (_out((PACK_ROWS, PACK_W), F32),
                 _out((rows_gh, 2 * STATE), BF16),
                 _out((rows_gh, 2 * STATE), BF16))
    return _pcall(body, name="ssm_disc_bwd_pack", grid=(1,), out_shape=out_shape,
                  in_specs=_whole_specs(operands), out_specs=tuple(_whole_specs(out_shape)),
                  scratch_shapes=[pltpu.VMEM((rows_gh, STATE), F32), pltpu.VMEM((rows_gh, STATE), F32)],
                  compiler_params=_params(1))(*operands)


def _s5_prepare(are, aim, ldt, bre, bim, cre, cim,
                o_ax_re, o_ax_im, o_ldt_x, o_ab_re, o_ab_im, o_bb_re, o_bb_im, o_c_re, o_c_imn):
    rows_gh = N_GROUPS * GROUP
    rep = (lax.broadcasted_iota(jnp.int32, (rows_gh, N_GROUPS), 0) // GROUP
           == lax.broadcasted_iota(jnp.int32, (rows_gh, N_GROUPS), 1)).astype(BF16)
    eye = (lax.broadcasted_iota(jnp.int32, (N_GROUPS, N_GROUPS), 0)
           == lax.broadcasted_iota(jnp.int32, (N_GROUPS, N_GROUPS), 1)).astype(F32)
    ldt_col = jnp.sum(eye * ldt[...], axis=1, keepdims=True)
    a_re_x = _select_dot(rep, are[...])
    a_im_x = _select_dot(rep, aim[...])
    ldt_x = _select_dot(rep, jnp.broadcast_to(ldt_col, (N_GROUPS, LANES)))[:, 0:1]
    o_ax_re[...] = a_re_x
    o_ax_im[...] = a_im_x
    o_ldt_x[...] = ldt_x
    ab_re, ab_im, bb_re, bb_im = _disc(a_re_x, a_im_x, ldt_x, bre[...], bim[...])
    for j in range(N_JBLK):
        first = [(j * SUBLANES + gi) * GROUP for gi in range(SUBLANES)]
        o_ab_re[j] = jnp.concatenate([ab_re[r:r + 1, :] for r in first], axis=1)
        o_ab_im[j] = jnp.concatenate([ab_im[r:r + 1, :] for r in first], axis=1)
    for o, v in ((o_bb_re, bb_re), (o_bb_im, bb_im), (o_c_re, cre[...]), (o_c_imn, -cim[...])):
        for j in range(N_JBLK):
            for gi in range(SUBLANES):
                r0 = (j * SUBLANES + gi) * GROUP
                parts = [v[r0:r0 + GROUP, :] if k == gi else jnp.zeros((GROUP, STATE), F32) for k in range(SUBLANES)]
                o[j, gi * GROUP:(gi + 1) * GROUP, :] = jnp.concatenate(parts, axis=1).astype(BF16)


def _in_proj(order, x2, g1, w_in_own, s5):
    n = x2.shape[0]
    tm = min(IN_TILE, n)
    n_tiles = n // tm
    n_s5_in = len(s5)
    n_s5_out = 9

    def body(order_ref, x_ref, g_ref, w_ref, *refs):
        s5_in = refs[:n_s5_in]
        xn_ref, proj_ref, wall_ref = refs[n_s5_in:n_s5_in + 3]
        s5_out = refs[n_s5_in + 3:n_s5_in + 3 + n_s5_out]
        xn_scr, wbuf, wown, send_sems, recv_sems, loc_sems, out_sems = refs[n_s5_in + 3 + n_s5_out:]
        k = pl.program_id(0)
        i = pl.program_id(1)

        def slot(dev):
            return wbuf.at[dev // 2, :, pl.ds(pl.multiple_of((dev % 2) * COLS_PER_DEV, LANES), COLS_PER_DEV)]

        gather = _TwoLevelGather([wown], [slot], send_sems, recv_sems, loc_sems)

        @pl.when((k == 0) & (i == 0))
        def _():
            wown[...] = w_ref[...].astype(BF16)
            gather.start()

        def own_chip():
            gather.wait_own()
            gather.wait_sibling()

        def x_chip():
            gather.neighbours_landed()
            gather.wait_passed_on(0)

        def diag_chip():
            gather.diagonal_landed()
            gather.wait_passed_on(2)

        rows = pl.ds(pl.multiple_of(i * tm, tm), tm)

        @pl.when(k == 0)
        def _():
            x = x_ref[...]
            r = lax.rsqrt(jnp.mean(x * x, axis=-1, keepdims=True) + EPS)
            xn = ((x * r) * g_ref[...]).astype(BF16)
            xn_scr[rows, :] = xn
            xn_ref[...] = xn

        def keep_copy(kk):
            q = order_ref[kk]
            cols = pl.ds(pl.multiple_of(q * COLS_PER_CHIP, LANES), COLS_PER_CHIP)
            return pltpu.make_async_copy(wbuf.at[q], wall_ref.at[:, cols], out_sems.at[kk])

        arrivals = [own_chip, x_chip, functools.partial(gather.wait_passed_on, 1), diag_chip]
        for kk, arrived in enumerate(arrivals):
            @pl.when((k == kk) & (i == 0))
            def _(kk=kk, arrived=arrived):
                arrived()
                keep_copy(kk).start()

        proj_ref[...] = _dot(xn_scr[rows, :], wbuf[order_ref[k]])

        @pl.when((k == 0) & (i == n_tiles - 1))
        def _():
            _s5_prepare(*s5_in, *s5_out)

        @pl.when((k == N_CHIP - 1) & (i == n_tiles - 1))
        def _():
            gather.wait_sends()
            for kk in range(N_CHIP):
                keep_copy(kk).wait()

    tile_once = lambda k, i, order: (jnp.where(k == 0, i, n_tiles - 1), 0)
    whole = lambda shape: pl.BlockSpec(shape, lambda k, i, order: (0,) * len(shape))
    rows_gh = N_GROUPS * GROUP
    s5_out_shapes = ([(rows_gh, STATE), F32], [(rows_gh, STATE), F32], [(rows_gh, 1), F32],
                     [(N_JBLK, 1, JB_ST), F32], [(N_JBLK, 1, JB_ST), F32]) + ([(N_JBLK, JB_CH, JB_ST), BF16],) * 4
    grid_spec = pltpu.PrefetchScalarGridSpec(
        num_scalar_prefetch=1, grid=(N_CHIP, n_tiles),
        in_specs=[pl.BlockSpec((tm, D_MODEL), tile_once),
                  whole((1, D_MODEL)),
                  whole(w_in_own.shape),
                  *(whole(a.shape) for a in s5)],
        out_specs=(pl.BlockSpec((tm, D_MODEL), tile_once),
                   pl.BlockSpec((tm, COLS_PER_CHIP), lambda k, i, order: (i, order[k])),
                   HBM_SPEC,
                   *(whole(shape) for shape, _ in s5_out_shapes)),
        scratch_shapes=[pltpu.VMEM((n, D_MODEL), BF16), pltpu.VMEM((N_CHIP, D_MODEL, COLS_PER_CHIP), BF16),
                        pltpu.VMEM(w_in_own.shape, BF16),
                        pltpu.SemaphoreType.DMA((7,)), pltpu.SemaphoreType.DMA((7,)), pltpu.SemaphoreType.DMA((1,)),
                        pltpu.SemaphoreType.DMA((N_CHIP,))])
    outs = _pcall(
        body, name="in_proj", grid_spec=grid_spec,
        out_shape=(_out((n, D_MODEL), BF16), _out((n, IN_COLS), F32),
                   _out((D_MODEL, IN_COLS), BF16),
                   *(_out(shape, dt) for shape, dt in s5_out_shapes)),
        compiler_params=_params(2),
    )(order, x2, g1, w_in_own, *s5)
    return outs[0], outs[1], outs[2], outs[3:]


def _cmul(p, q):
    return p[0] * q[0] - p[1] * q[1], p[0] * q[1] + p[1] * q[0]


def _scan_tables(ar, ai, width, reverse):
    pows = [(ar, ai)]
    for _ in range(SUBLANES - 1):
        pows.append(_cmul(pows[-1], (ar, ai)))
    row = lax.broadcasted_iota(jnp.int32, (SUBLANES, width), 0)

    def bc(v):
        return jnp.broadcast_to(v, (SUBLANES, width))

    levels = []
    for k in (1, 2, 4):
        keep = (row <= SUBLANES - 1 - k) if reverse else (row >= k)
        levels.append((jnp.where(keep, bc(pows[k - 1][0]), 0.0), jnp.where(keep, bc(pows[k - 1][1]), 0.0)))
    cre = jnp.zeros((SUBLANES, width), F32)
    cim = jnp.zeros((SUBLANES, width), F32)
    for r in range(SUBLANES):
        e = (SUBLANES - r) if reverse else (r + 1)
        cre = jnp.where(row == r, bc(pows[e - 1][0]), cre)
        cim = jnp.where(row == r, bc(pows[e - 1][1]), cim)
    return levels, (cre, cim)


def _load_chunked(src_ref, b, dst_ref, n_rows):
    n_blk = n_rows // SUBLANES
    for i in range(n_blk):
        dst_ref[b, i * SUBLANES:(i + 1) * SUBLANES, :] = src_ref[b, pl.ds(i, SUBLANES, stride=n_blk), :]


def _store_chunked(val, dst_ref, b, n_rows):
    n_blk = n_rows // SUBLANES
    for i in range(n_blk):
        dst_ref[b, pl.ds(i, SUBLANES, stride=n_blk), :] = val[i * SUBLANES:(i + 1) * SUBLANES, :]


def _chunk_scan(re_ref, im_ref, bs, car_ref, ar, ai, n_rows, reverse, on_block=None):
    width = re_ref.shape[2]
    n_blk = n_rows // SUBLANES
    shape = (SUBLANES, width)
    abr = jnp.broadcast_to(ar, shape)
    abi = jnp.broadcast_to(ai, shape)
    order = list(range(n_blk - 1, -1, -1)) if reverse else list(range(n_blk))

    def blk(ref, b, i):
        return ref[b, i * SUBLANES:(i + 1) * SUBLANES, :]

    def step(state, b, i):
        sr, si = state
        return abr * sr - abi * si + blk(re_ref, b, i), abr * si + abi * sr + blk(im_ref, b, i)

    finals = {b: (blk(re_ref, b, order[0]), blk(im_ref, b, order[0])) for b in bs}
    for i in order[1:]:
        for b in bs:
            finals[b] = step(finals[b], b, i)

    mr, mi = ar, ai
    for _ in range(n_blk.bit_length() - 1):
        mr, mi = _cmul((mr, mi), (mr, mi))
    levels, _ = _scan_tables(mr, mi, width, reverse)
    mbr = jnp.broadcast_to(mr, shape)
    mbi = jnp.broadcast_to(mi, shape)
    row = lax.broadcasted_iota(jnp.int32, shape, 0)
    edge_in = SUBLANES - 1 if reverse else 0
    edge_out = 0 if reverse else SUBLANES - 1
    sh1 = SUBLANES - 1 if reverse else 1
    states = {}
    for b in bs:
        fr, fi = finals[b]
        gr = jnp.where(row == edge_in, jnp.broadcast_to(car_ref[b, 0:1, :], shape), pltpu.roll(fr, sh1, 0))
        gi = jnp.where(row == edge_in, jnp.broadcast_to(car_ref[b, 1:2, :], shape), pltpu.roll(fi, sh1, 0))
        for (lr, li), k in zip(levels, (1, 2, 4)):
            sh = (SUBLANES - k) if reverse else k
            sr = pltpu.roll(gr, sh, 0)
            si = pltpu.roll(gi, sh, 0)
            gr, gi = gr + (lr * sr - li * si), gi + (lr * si + li * sr)
        car_ref[b, 0:1, :] = (fr + (mbr * gr - mbi * gi))[edge_out:edge_out + 1, :]
        car_ref[b, 1:2, :] = (fi + (mbr * gi + mbi * gr))[edge_out:edge_out + 1, :]
        states[b] = (gr, gi)

    for i in order:
        for b in bs:
            states[b] = step(states[b], b, i)
            re_ref[b, i * SUBLANES:(i + 1) * SUBLANES, :] = states[b][0]
            im_ref[b, i * SUBLANES:(i + 1) * SUBLANES, :] = states[b][1]
            if on_block is not None:
                on_block(b, i, *states[b])


def _ssm_fwd(u, bb_re, bb_im, c_re_t, c_imn_t, d_row, ab_re, ab_im, w_out_own, w_glu_own, conv_p, n_seq, seq):
    tt = min(SCAN_TILE, seq)
    nt = seq // tt

    def body(u_ref, bbre, bbim, cre, cimn, d_ref, are, aim, wout_ref, wglu_ref, cw_ref,
             sre_ref, sim_ref, y_ref, oout_ref, oglu_ref, ocw_ref,
             up_ref, car_ref, woutb_ref, wglub_ref, send_sems, recv_sems, loc_sems):
        j = pl.program_id(0)
        t = pl.program_id(1)
        gather = _TwoLevelGather(
            [woutb_ref, wglub_ref, cw_ref],
            [lambda dev: oout_ref.at[pl.ds(pl.multiple_of(dev * OUT_ROWS_PER_DEV, OUT_ROWS_PER_DEV), OUT_ROWS_PER_DEV), :],
             lambda dev: oglu_ref.at[pl.ds(pl.multiple_of(dev * GLU_ROWS_PER_DEV, GLU_ROWS_PER_DEV), GLU_ROWS_PER_DEV), :],
             lambda dev: ocw_ref.at[dev]],
            send_sems, recv_sems, loc_sems)

        @pl.when((j == 0) & (t == 0))
        def _():
            woutb_ref[...] = wout_ref[...].astype(BF16)
            wglub_ref[...] = wglu_ref[...].astype(BF16)
            gather.start()

        @pl.when((j == N_JBLK // 2) & (t == 0))
        def _():
            gather.neighbours_landed()

        @pl.when((j == N_JBLK - 1) & (t == 0))
        def _():
            gather.diagonal_landed()

        @pl.when(t == 0)
        def _():
            car_ref[...] = jnp.zeros_like(car_ref)

        bs = list(range(n_seq))
        for b in bs:
            _load_chunked(u_ref, b, up_ref, tt)
        for b in bs:
            ub = up_ref[b].astype(BF16)
            sre_ref[b] = _dot(ub, bbre[0])
            sim_ref[b] = _dot(ub, bbim[0])
            _chunk_scan(sre_ref, sim_ref, [b], car_ref, are[0], aim[0], tt, reverse=False)
        for b in bs:
            yp = (_dot_nt(sre_ref[b].astype(BF16), cre[0]) + _dot_nt(sim_ref[b].astype(BF16), cimn[0])
                  + d_ref[...] * up_ref[b])
            _store_chunked(yp, y_ref, b, tt)

        @pl.when((j == N_JBLK - 1) & (t == nt - 1))
        def _():
            gather.finish()

    tok = lambda j, t: (0, t, j)
    blk3 = lambda j, t: (j, 0, 0)
    row = lambda j, t: (0, j)
    whole = lambda j, t: (0, 0)
    st = _out((n_seq, seq, N_JBLK * JB_ST), F32)
    n_arr = 3
    return _pcall(
        body, name="ssm_fwd", grid=(N_JBLK, nt),
        out_shape=(st, st, _out((n_seq, seq, SSM_W), F32),
                   _out((D_MODEL, D_MODEL), BF16), _out((SSM_W, SSM_W), BF16),
                   _out((N_DEV, SUBLANES, LANES), F32)),
        in_specs=[pl.BlockSpec((n_seq, tt, JB_CH), tok),
                  pl.BlockSpec((1, JB_CH, JB_ST), blk3), pl.BlockSpec((1, JB_CH, JB_ST), blk3),
                  pl.BlockSpec((1, JB_CH, JB_ST), blk3), pl.BlockSpec((1, JB_CH, JB_ST), blk3),
                  pl.BlockSpec((1, JB_CH), row), pl.BlockSpec((1, 1, JB_ST), blk3), pl.BlockSpec((1, 1, JB_ST), blk3),
                  pl.BlockSpec(w_out_own.shape, whole), pl.BlockSpec(w_glu_own.shape, whole), HBM_SPEC],
        out_specs=(pl.BlockSpec((n_seq, tt, JB_ST), tok), pl.BlockSpec((n_seq, tt, JB_ST), tok),
                   pl.BlockSpec((n_seq, tt, JB_CH), tok), HBM_SPEC, HBM_SPEC, HBM_SPEC),
        scratch_shapes=[pltpu.VMEM((n_seq, tt, JB_CH), F32), pltpu.VMEM((n_seq, SUBLANES, JB_ST), F32),
                        pltpu.VMEM(w_out_own.shape, BF16), pltpu.VMEM(w_glu_own.shape, BF16),
                        pltpu.SemaphoreType.DMA((7 * n_arr,)), pltpu.SemaphoreType.DMA((7 * n_arr,)),
                        pltpu.SemaphoreType.DMA((n_arr,))],
        compiler_params=_params(2),
    )(u, bb_re, bb_im, c_re_t, c_imn_t, d_row, ab_re, ab_im, w_out_own, w_glu_own, conv_p)


def _ssm_bwd(dy, u, s_re, s_im, bb_re, bb_im, c_re_t, c_imn_t, d_row, ab_re, ab_im, g_out, g_glu, n_seq, seq):
    tt = min(SCAN_TILE, seq)
    nt = seq // tt
    rows8 = tt // SUBLANES

    def body(dy_ref, u_ref, sre_ref, sim_ref, pre_ref, pim_ref, bbre, bbim, cre, cimn, d_ref, are, aim,
             gout_ref, gglu_ref,
             du_ref, dcre_ref, dcim_ref, dbbre_ref, dbbim_ref, dare_ref, daim_ref, dd_ref, rout_ref, rglu_ref,
             lre_ref, lim_ref, dyp_ref, up_ref, car_ref, send_sems, recv_sems, loc_sems):
        j = pl.program_id(0)
        tr = pl.program_id(1)

        def exchange():
            return _direct_copies(lambda pid: [gout_ref.at[pid], gglu_ref.at[pid]], [rout_ref, rglu_ref],
                                  send_sems, recv_sems, loc_sems)

        @pl.when((j == 0) & (tr == 0))
        def _():
            mine, sends = exchange()
            for cp in mine + sends:
                cp.start()

        @pl.when(tr == 0)
        def _():
            car_ref[...] = jnp.zeros_like(car_ref)
            for r in (dcre_ref, dcim_ref, dbbre_ref, dbbim_ref, dare_ref, daim_ref, dd_ref):
                r[...] = jnp.zeros_like(r)

        first = tr == nt - 1
        row = lax.broadcasted_iota(jnp.int32, (SUBLANES, JB_ST), 0)
        n_blk = tt // SUBLANES
        bs = list(range(n_seq))
        for b in bs:
            _load_chunked(dy_ref, b, dyp_ref, tt)
            _load_chunked(u_ref, b, up_ref, tt)
        for b in bs:
            dyb = dyp_ref[b].astype(BF16)
            lre_ref[b] = _dot(dyb, cre[0])
            lim_ref[b] = _dot(dyb, cimn[0])
        acc = {b: [jnp.zeros((SUBLANES, JB_ST), F32), jnp.zeros((SUBLANES, JB_ST), F32)] for b in bs}

        def on_block(b, i, lr, li):
            if i > 0:
                spr = sre_ref[b, (i - 1) * SUBLANES:i * SUBLANES, :]
                spi = sim_ref[b, (i - 1) * SUBLANES:i * SUBLANES, :]
            else:
                hr = jnp.where(first, 0.0, pre_ref[b, SUBLANES - 1:SUBLANES, :])
                hi = jnp.where(first, 0.0, pim_ref[b, SUBLANES - 1:SUBLANES, :])
                last_r = sre_ref[b, (n_blk - 1) * SUBLANES:n_blk * SUBLANES, :]
                last_i = sim_ref[b, (n_blk - 1) * SUBLANES:n_blk * SUBLANES, :]
                spr = jnp.where(row == 0, jnp.broadcast_to(hr, row.shape), pltpu.roll(last_r, 1, 0))
                spi = jnp.where(row == 0, jnp.broadcast_to(hi, row.shape), pltpu.roll(last_i, 1, 0))
            acc[b][0] = acc[b][0] + (lr * spr + li * spi)
            acc[b][1] = acc[b][1] + (li * spr - lr * spi)

        _chunk_scan(lre_ref, lim_ref, bs, car_ref, are[0], -aim[0], tt, reverse=True, on_block=on_block)
        for b in bs:
            dare_ref[...] += jnp.sum(acc[b][0], axis=0, keepdims=True)
            daim_ref[...] += jnp.sum(acc[b][1], axis=0, keepdims=True)
            dyp = dyp_ref[b]
            up = up_ref[b]
            dyb = dyp.astype(BF16)
            ub = up.astype(BF16)
            lrb = lre_ref[b].astype(BF16)
            lib = lim_ref[b].astype(BF16)
            dup = d_ref[...] * dyp + _dot_nt(lrb, bbre[0]) + _dot_nt(lib, bbim[0])
            _store_chunked(dup, du_ref, b, tt)
            dbbre_ref[0] += _dot_tn(ub, lrb)
            dbbim_ref[0] += _dot_tn(ub, lib)
            dcre_ref[0] += _dot_tn(dyb, sre_ref[b].astype(BF16))
            dcim_ref[0] += _dot_tn(dyb, sim_ref[b].astype(BF16))
            dd_ref[...] += jnp.sum(dyp * up, axis=0, keepdims=True)

        @pl.when((j == N_JBLK - 1) & (tr == nt - 1))
        def _():
            mine, sends = exchange()
            for cp in sends + mine:
                cp.wait()

    tok = lambda j, t: (0, nt - 1 - t, j)
    halo = lambda j, t: (0, jnp.maximum((nt - 1 - t) * rows8 - 1, 0), j)
    blk3 = lambda j, t: (j, 0, 0)
    row1 = lambda j, t: (0, j)
    acc_shape = _out((N_JBLK, JB_CH, JB_ST), F32)
    return _pcall(
        body, name="ssm_bwd", grid=(N_JBLK, nt),
        out_shape=(_out((n_seq, seq, SSM_W), F32), acc_shape, acc_shape, acc_shape, acc_shape,
                   _out((1, N_JBLK * JB_ST), F32), _out((1, N_JBLK * JB_ST), F32),
                   _out((1, SSM_W), F32),
                   _out((N_DEV,) + g_out.shape[1:], F32),
                   _out((N_DEV,) + g_glu.shape[1:], F32)),
        in_specs=[pl.BlockSpec((n_seq, tt, JB_CH), tok), pl.BlockSpec((n_seq, tt, JB_CH), tok),
                  pl.BlockSpec((n_seq, tt, JB_ST), tok), pl.BlockSpec((n_seq, tt, JB_ST), tok),
                  pl.BlockSpec((n_seq, SUBLANES, JB_ST), halo), pl.BlockSpec((n_seq, SUBLANES, JB_ST), halo),
                  pl.BlockSpec((1, JB_CH, JB_ST), blk3), pl.BlockSpec((1, JB_CH, JB_ST), blk3),
                  pl.BlockSpec((1, JB_CH, JB_ST), blk3), pl.BlockSpec((1, JB_CH, JB_ST), blk3),
                  pl.BlockSpec((1, JB_CH), row1), pl.BlockSpec((1, 1, JB_ST), blk3), pl.BlockSpec((1, 1, JB_ST), blk3),
                  HBM_SPEC, HBM_SPEC],
        out_specs=(pl.BlockSpec((n_seq, tt, JB_CH), tok),
                   pl.BlockSpec((1, JB_CH, JB_ST), blk3), pl.BlockSpec((1, JB_CH, JB_ST), blk3),
                   pl.BlockSpec((1, JB_CH, JB_ST), blk3), pl.BlockSpec((1, JB_CH, JB_ST), blk3),
                   pl.BlockSpec((1, JB_ST), row1), pl.BlockSpec((1, JB_ST), row1), pl.BlockSpec((1, JB_CH), row1),
                   HBM_SPEC, HBM_SPEC),
        scratch_shapes=[pltpu.VMEM((n_seq, tt, JB_ST), F32), pltpu.VMEM((n_seq, tt, JB_ST), F32),
                        pltpu.VMEM((n_seq, tt, JB_CH), F32), pltpu.VMEM((n_seq, tt, JB_CH), F32),
                        pltpu.VMEM((n_seq, SUBLANES, JB_ST), F32),
                        pltpu.SemaphoreType.DMA((7 * 2,)), pltpu.SemaphoreType.DMA((7 * 2,)),
                        pltpu.SemaphoreType.DMA((2,))],
        compiler_params=_params(2),
    )(dy, u, s_re, s_im, s_re, s_im, bb_re, bb_im, c_re_t, c_imn_t, d_row, ab_re, ab_im, g_out, g_glu)


def _mix(x2, tgt2, y, proj, gf, b_glu, conv8, w_glu_f, w_out_f, seq):
    n = x2.shape[0]
    tm = TOK_TILE
    tiles_per_seq = seq // tm
    rows8 = tm // SUBLANES

    def body(x_ref, t_ref, y_ref, zs_ref, h_ref, bc_ref, cc_ref, zc_ref, hp_ref, ccp_ref,
             gf_ref, bg_ref, cw_ref, wg_ref, wo_ref,
             dh2_ref, dy_ref, dzs_ref, dbc_ref, dzc_ref, dyc_ref,
             dwo_ref, dwg_ref, loss_ref, dgf_ref, dbg_ref, dcw_ref):
        i = pl.program_id(0)

        @pl.when(i == 0)
        def _():
            for r in (dwo_ref, dwg_ref, loss_ref, dgf_ref, dbg_ref, dcw_ref):
                r[...] = jnp.zeros_like(r)

        yv = y_ref[...]
        y1, dgelu = _gelu_and_grad(yv)
        y1b = y1.astype(BF16)
        gate = _sigmoid(_dot(y1b, wg_ref[...]) + bg_ref[...])
        y2 = y1 * gate
        szs, dszs = _silu_and_grad(zs_ref[...])
        yssm = y2 * szs
        hv = h_ref[...]
        ccv = cc_ref[...]
        bcv = bc_ref[...]
        v = ccv * hv
        first = (i % tiles_per_seq) == 0
        vhalo = jnp.where(first, 0.0, ccp_ref[...] * hp_ref[...])
        v1 = _shift_down(v, vhalo, 1)
        v2 = _shift_down(v, vhalo, 2)
        w0 = cw_ref[0:1, :]
        w1 = cw_ref[1:2, :]
        w2 = cw_ref[2:3, :]
        yc = w0 * v2 + w1 * v1 + w2 * v
        szc, dszc = _silu_and_grad(zc_ref[...])
        yconv = (bcv * yc) * szc
        ysb = yssm.astype(BF16)
        ycb = yconv.astype(BF16)
        h2 = x_ref[...] + _dot(ysb, wo_ref[0:SSM_W, :]) + _dot(ycb, wo_ref[SSM_W:, :])
        r2 = lax.rsqrt(jnp.mean(h2 * h2, axis=-1, keepdims=True) + EPS)
        hn = h2 * r2
        gfv = gf_ref[...]
        err = hn * gfv - t_ref[...]
        loss_ref[...] += 0.5 * jnp.sum(jnp.mean(err * err, axis=-1, keepdims=True))
        dout = err * (1.0 / D_MODEL)
        dgf_ref[...] += jnp.sum(dout * hn, axis=0, keepdims=True)
        dn = dout * gfv
        dh2 = r2 * (dn - hn * jnp.mean(dn * hn, axis=-1, keepdims=True))
        dh2_ref[...] = dh2
        dh2b = dh2.astype(BF16)
        dwo_ref[0:SSM_W, :] += _dot_tn(ysb, dh2b)
        dwo_ref[SSM_W:, :] += _dot_tn(ycb, dh2b)
        dyssm = _dot_nt(dh2b, wo_ref[0:SSM_W, :])
        dyconv = _dot_nt(dh2b, wo_ref[SSM_W:, :])
        dy2 = dyssm * szs
        dzs_ref[...] = (dyssm * y2 * dszs).astype(BF16)
        dgp = dy2 * y1 * (gate * (1.0 - gate))
        dgpb = dgp.astype(BF16)
        dy1 = dy2 * gate + _dot_nt(dgpb, wg_ref[...])
        dwg_ref[...] += _dot_tn(y1b, dgpb)
        dbg_ref[...] += jnp.sum(dgp, axis=0, keepdims=True)
        dy_ref[...] = dy1 * dgelu
        dbc_ref[...] = (dyconv * yc * szc).astype(BF16)
        dyc = dyconv * bcv * szc
        dyc_ref[...] = dyc
        dzc_ref[...] = (dyconv * bcv * yc * dszc).astype(BF16)
        dcw_ref[0:1, :] += jnp.sum(dyc * v2, axis=0, keepdims=True)
        dcw_ref[1:2, :] += jnp.sum(dyc * v1, axis=0, keepdims=True)
        dcw_ref[2:3, :] += jnp.sum(dyc * v, axis=0, keepdims=True)

    tile_d = pl.BlockSpec((tm, D_MODEL), lambda i: (i, 0))
    tile_s = pl.BlockSpec((tm, SSM_W), lambda i: (i, 0))
    seg_of = lambda c: pl.BlockSpec((tm, SSM_W), lambda i: (i, c))
    halo_of = lambda c: pl.BlockSpec((SUBLANES, SSM_W), lambda i: (jnp.maximum(i * rows8 - 1, 0), c))
    const = lambda shape: pl.BlockSpec(shape, lambda i: (0,) * len(shape))
    seg = _out((n, SSM_W), F32)
    seg_b = _out((n, SSM_W), BF16)
    return _pcall(
        body, name="mix", grid=(n // tm,),
        out_shape=(_out((n, D_MODEL), F32), seg, seg_b, seg_b, seg_b, seg,
                   _out((D_MODEL, D_MODEL), F32), _out((SSM_W, SSM_W), F32),
                   _out((SUBLANES, LANES), F32), _out((1, D_MODEL), F32),
                   _out((1, SSM_W), F32), _out((SUBLANES, CONV_W), F32)),
        in_specs=[tile_d, tile_d, tile_s, seg_of(SEG_ZS), seg_of(SEG_H), seg_of(SEG_BC), seg_of(SEG_CC), seg_of(SEG_ZC),
                  halo_of(SEG_H), halo_of(SEG_CC),
                  const((1, D_MODEL)), const((1, SSM_W)), const((SUBLANES, CONV_W)),
                  const((SSM_W, SSM_W)), const((D_MODEL, D_MODEL))],
        out_specs=(tile_d, tile_s, tile_s, tile_s, tile_s, tile_s,
                   const((D_MODEL, D_MODEL)), const((SSM_W, SSM_W)), const((SUBLANES, LANES)),
                   const((1, D_MODEL)), const((1, SSM_W)), const((SUBLANES, CONV_W))),
        compiler_params=_params(1),
    )(x2, tgt2, y, proj, proj, proj, proj, proj, proj, proj, gf, b_glu, conv8, w_glu_f, w_out_f)


def _in_bwd(x2, dh2, du, dzs, dyc, proj, dbc, dzc, g1, conv8, w_full, seq):
    n = x2.shape[0]
    tm = TOK_TILE
    n_tiles = n // tm
    tiles_per_seq = seq // tm
    rows8 = tm // SUBLANES
    n_blk8 = n // SUBLANES

    def body(x_ref, dh2_ref, du_ref, dzs_ref, dyc_ref, dycn_ref, h_ref, cc_ref, dbc_ref, dzc_ref,
             g_ref, cw_ref, w_ref, gx_ref, dp_ref, dg_ref):
        i = pl.program_id(0)

        @pl.when(i == 0)
        def _():
            dg_ref[...] = jnp.zeros_like(dg_ref)

        dyc = dyc_ref[...]
        last = (i % tiles_per_seq) == tiles_per_seq - 1
        nhalo = jnp.where(last, 0.0, dycn_ref[...])
        dv = (cw_ref[2:3, :] * dyc + cw_ref[1:2, :] * _shift_up(dyc, nhalo, 1)
              + cw_ref[0:1, :] * _shift_up(dyc, nhalo, 2))
        parts = (du_ref[...], dzs_ref[...], dv * cc_ref[...], dbc_ref[...], dv * h_ref[...], dzc_ref[...])
        dxn = jnp.zeros((tm, D_MODEL), F32)
        for k, p in enumerate(parts):
            pb = p.astype(BF16)
            dp_ref[:, k * SSM_W:(k + 1) * SSM_W] = pb
            dxn = dxn + _dot_nt(pb, w_ref[:, k * SSM_W:(k + 1) * SSM_W])
        x = x_ref[...]
        r = lax.rsqrt(jnp.mean(x * x, axis=-1, keepdims=True) + EPS)
        xh = x * r
        dg_ref[...] += jnp.sum(dxn * xh, axis=0, keepdims=True)
        dn = dxn * g_ref[...]
        gx_ref[...] = dh2_ref[...] + r * (dn - xh * jnp.mean(dn * xh, axis=-1, keepdims=True))

    tile_d = pl.BlockSpec((tm, D_MODEL), lambda i: (i, 0))
    tile_s = pl.BlockSpec((tm, SSM_W), lambda i: (i, 0))
    seg_of = lambda c: pl.BlockSpec((tm, SSM_W), lambda i: (i, c))
    nhalo = pl.BlockSpec((SUBLANES, SSM_W), lambda i: (jnp.minimum((i + 1) * rows8, n_blk8 - 1), 0))
    const = lambda shape: pl.BlockSpec(shape, lambda i: (0,) * len(shape))
    return _pcall(
        body, name="in_bwd", grid=(n_tiles,),
        out_shape=(_out((n, D_MODEL), F32), _out((n, IN_COLS), BF16),
                   _out((SUBLANES, D_MODEL), F32)),
        in_specs=[tile_d, tile_d, tile_s, tile_s, tile_s, nhalo, seg_of(SEG_H), seg_of(SEG_CC), tile_s, tile_s,
                  const((1, D_MODEL)), const((SUBLANES, CONV_W)), const((D_MODEL, IN_COLS))],
        out_specs=(tile_d, pl.BlockSpec((tm, IN_COLS), lambda i: (i, 0)), const((SUBLANES, D_MODEL))),
        compiler_params=_params(1),
    )(x2, dh2, du, dzs, dyc, dyc, proj, proj, dbc, dzc, g1, conv8, w_full)


_HALF_BLOCKS = ((0, 0), (0, 1), (1, 0), (2, 0), (1, 1), (2, 1), (3, 0), (3, 1))


def _dw_in_exchange(chips, xn, dproj, smalls):
    n = xn.shape[0]
    tk = min(1024, n)
    nk = n // tk
    piece = (D_MODEL, COLS_PER_DEV)
    hr = D_MODEL // 2
    n_half = len(_HALF_BLOCKS)
    n_small = len(smalls)
    assert _HALF_BLOCKS[0][1] == 0 and _HALF_BLOCKS[1][1] == 1
    order = jnp.stack([chips[b] for b, _ in _HALF_BLOCKS]
                      + [jnp.int32(t) for _, t in _HALF_BLOCKS]).astype(jnp.int32)

    def body(order_ref, xn_hbm, dp_ref, *refs):
        sm_refs = refs[:n_small]
        own_ref, rchip_ref = refs[n_small:n_small + 2]
        rsm_refs = refs[n_small + 2:2 * n_small + 2]
        (xn_ref, acc, stage, rbuf, kbuf, relay_in, xn_sems, give_send, give_recv, keep_send, keep_recv,
         relay_send, relay_recv, sm_send, sm_recv, sm_loc) = refs[2 * n_small + 2:]
        s = pl.program_id(0)

        def xn_copy(kk, t):
            rows = pl.ds(pl.multiple_of(kk * tk, tk), tk)
            return pltpu.make_async_copy(xn_hbm.at[rows, t * hr:(t + 1) * hr], xn_ref.at[t, rows, :],
                                         xn_sems.at[2 * kk + t])

        @pl.when(s == 0)
        def _():
            for kk in range(nk):
                for t in range(2):
                    xn_copy(kk, t).start()
            xn_copy(0, 0).wait()

        @pl.when(s == 1)
        def _():
            xn_copy(0, 1).wait()

        x, y, c = _mesh_pos()
        sib = (x, y, 1 - c)
        y_nbr, x_nbr = (x, 1 - y, c), (1 - x, y, c)
        gather = _TwoLevelGather(list(sm_refs), [functools.partial(lambda r, dev: r.at[dev], r) for r in rsm_refs],
                                 sm_send, sm_recv, sm_loc)

        def give(h):
            cols = pl.ds(pl.multiple_of((1 - c) * COLS_PER_DEV, LANES), COLS_PER_DEV)
            return pltpu.make_async_remote_copy(src_ref=acc.at[h % 2, :, cols], dst_ref=stage.at[h],
                                                send_sem=give_send.at[h], recv_sem=give_recv.at[h],
                                                device_id=sib, device_id_type=MESH)

        def relay(r):
            return pltpu.make_async_remote_copy(src_ref=rbuf.at[r], dst_ref=relay_in.at[r],
                                                send_sem=relay_send.at[r], recv_sem=relay_recv.at[r],
                                                device_id=(x_nbr, y_nbr)[r], device_id_type=MESH)

        def keep(q):
            return pltpu.make_async_remote_copy(src_ref=kbuf.at[q], dst_ref=rchip_ref.at[q // 2, pl.ds((q % 2) * hr, hr), :],
                                                send_sem=keep_send.at[q], recv_sem=keep_recv.at[q],
                                                device_id=(y_nbr, x_nbr)[q // 2], device_id_type=MESH)

        def chip_sum(h):
            give(h).wait_recv()
            mine = [acc[h % 2, :, cc * COLS_PER_DEV:(cc + 1) * COLS_PER_DEV] for cc in range(2)]
            return jnp.where(c == 0, mine[0], mine[1]) + stage[h]

        @pl.when(s == 0)
        def _():
            gather.start()

        @pl.when(s == 2)
        def _():
            gather.neighbours_landed()

        @pl.when(s == n_half - 2)
        def _():
            gather.diagonal_landed()

        for k in range(2, n_half):
            @pl.when(s == k)
            def _(k=k):
                give(k - 2).wait_send()

        slot = s % 2
        t_half = order_ref[n_half + s]
        acc[slot] = _dot_tn(xn_ref[t_half, pl.ds(0, tk), :], dp_ref[pl.ds(0, tk), :])

        def kstep(kk, carry):
            for t in range(2):
                @pl.when(s == t)
                def _(t=t):
                    xn_copy(kk, t).wait()

            off = pl.multiple_of(kk * tk, tk)
            acc[slot] += _dot_tn(xn_ref[t_half, pl.ds(off, tk), :], dp_ref[pl.ds(off, tk), :])
            return carry

        n_first = max(1, nk // 2)
        lax.fori_loop(1, n_first, kstep, 0)
        for k in range(1, n_half):
            @pl.when(s == k)
            def _(k=k):
                h = k - 1
                b, t = _HALF_BLOCKS[h]
                total = chip_sum(h)
                if b == 0:
                    rbuf[t] = total.astype(BF16)
                    relay(t).start()
                elif b < 3:
                    if (b, t) in ((1, 0), (2, 1)):
                        relay(t).wait_recv()
                        total = total + relay_in[t].astype(F32)
                    q = 2 * (b - 1) + t
                    kbuf[q] = total.astype(BF16)
                    keep(q).start()
                else:
                    own_ref[0:hr, :] = total

        lax.fori_loop(n_first, nk, kstep, 0)

        for k in range(n_half):
            @pl.when(s == k)
            def _(k=k):
                give(k).start()

        @pl.when(s == n_half - 1)
        def _():
            own_ref[hr:D_MODEL, :] = chip_sum(n_half - 1)
            give(n_half - 2).wait_send()
            give(n_half - 1).wait_send()
            for r in range(2):
                relay(r).wait_send()
            for q in range(4):
                keep(q).wait()
            gather.finish()

    half_piece = (hr, COLS_PER_DEV)
    grid_spec = pltpu.PrefetchScalarGridSpec(
        num_scalar_prefetch=1, grid=(n_half,),
        in_specs=[HBM_SPEC,
                  pl.BlockSpec((n, COLS_PER_CHIP), lambda s, order: (0, order[s])),
                  *([HBM_SPEC] * n_small)],
        out_specs=(pl.BlockSpec(piece, lambda s, order: (0, 0)), HBM_SPEC, *([HBM_SPEC] * n_small)),
        scratch_shapes=[pltpu.VMEM((2, n, hr), BF16),
                        pltpu.VMEM((2, hr, COLS_PER_CHIP), F32), pltpu.VMEM((n_half,) + half_piece, F32),
                        pltpu.VMEM((2,) + half_piece, BF16), pltpu.VMEM((4,) + half_piece, BF16),
                        pltpu.VMEM((2,) + half_piece, BF16),
                        pltpu.SemaphoreType.DMA((2 * nk,)),
                        pltpu.SemaphoreType.DMA((n_half,)), pltpu.SemaphoreType.DMA((n_half,)),
                        pltpu.SemaphoreType.DMA((4,)), pltpu.SemaphoreType.DMA((4,)),
                        pltpu.SemaphoreType.DMA((2,)), pltpu.SemaphoreType.DMA((2,)),
                        pltpu.SemaphoreType.DMA((7 * n_small,)), pltpu.SemaphoreType.DMA((7 * n_small,)),
                        pltpu.SemaphoreType.DMA((n_small,))])
    return _pcall(
        body, name="dw_in_exchange", grid_spec=grid_spec,
        out_shape=(_out(piece, F32), _out((2,) + piece, BF16),
                   *(_out((N_DEV,) + a.shape, a.dtype) for a in smalls)),
        compiler_params=_params(1),
    )(order, xn, dproj, *smalls)


def _adamw(g, w, m, v):
    m_new = ADAM_B1 * m + (1.0 - ADAM_B1) * g
    v_new = ADAM_B2 * v + (1.0 - ADAM_B2) * (g * g)
    m_hat = m_new / (1.0 - ADAM_B1 ** ADAM_STEP)
    v_hat = v_new / (1.0 - ADAM_B2 ** ADAM_STEP)
    delta = -ADAM_LR * (m_hat / (jnp.sqrt(v_hat) + ADAM_EPS) + ADAM_WD * w)
    return delta, m_new, v_new


def _reduce_adam_w_in(own, rchip, w, m, v):
    rows, cols = w.shape
    row_tile = 256

    def body(o_ref, r_ref, w_ref, m_ref, v_ref, g_ref, d_ref, nm_ref, nv_ref):
        g = o_ref[...]
        for s in range(2):
            g = g + r_ref[s].astype(F32)
        g_ref[...] = g
        d_ref[...], nm_ref[...], nv_ref[...] = _adamw(g, w_ref[...], m_ref[...], v_ref[...])

    tile = pl.BlockSpec((row_tile, cols), lambda i: (i, 0))
    shp = _out((rows, cols), F32)
    return _pcall(
        body, name="reduce_adam_w_in", grid=(rows // row_tile,),
        out_shape=(shp,) * 4,
        in_specs=[tile, pl.BlockSpec((2, row_tile, cols), lambda i: (0, i, 0)), tile, tile, tile],
        out_specs=(tile,) * 4,
        compiler_params=_params(1),
    )(own, rchip, w, m, v)


_SMALL_LEAVES = ("norm_gain", "final_norm_gain", "b_glu", "ssm_a_re", "ssm_a_im", "ssm_log_dt", "ssm_d", "conv_w",
                 "ssm_c_re", "ssm_c_im", "ssm_b_re", "ssm_b_im")


def _reduce_adam_small(r_pack, r_gc, r_gb, wmv, sharded):
    n_leaf = len(_SMALL_LEAVES)
    n_sh = len(sharded)

    def body(*refs):
        rp_ref, rgc_ref, rgb_ref = refs[:3]
        w_refs = refs[3:3 + 3 * n_leaf]
        sh_in = refs[3 + 3 * n_leaf:3 + 3 * n_leaf + 4 * n_sh]
        outs0 = 3 + 3 * n_leaf + 4 * n_sh
        loss_ref = refs[outs0]
        o_refs = refs[outs0 + 1:outs0 + 1 + 4 * n_leaf]
        sh_out = refs[outs0 + 1 + 4 * n_leaf:outs0 + 1 + 4 * n_leaf + 4 * n_sh]
        own_conv = refs[-1]

        def total(ref):
            acc = ref[0].astype(F32)
            for s in range(1, N_DEV):
                acc = acc + ref[s].astype(F32)
            return acc

        for i in range(n_sh):
            r_ref, w_ref, m_ref, v_ref = sh_in[4 * i:4 * i + 4]
            o_g, o_d, o_m, o_v = sh_out[4 * i:4 * i + 4]
            g = total(r_ref)
            o_g[...] = g
            o_d[...], o_m[...], o_v[...] = _adamw(g, w_ref[...], m_ref[...], v_ref[...])

        sp = total(rp_ref)
        sgc = total(rgc_ref)
        sgb = total(rgb_ref)
        loss_ref[...] = sp[ROW_LOSS:ROW_LOSS + 1, 0:1]

        def wide(r):
            return jnp.concatenate([sp[r:r + 1, :], sp[r + 1:r + 2, :]], axis=1)

        s5 = slice(ROW_S5, ROW_S5 + N_GROUPS)
        eye = (lax.broadcasted_iota(jnp.int32, (N_GROUPS, N_GROUPS), 0)
               == lax.broadcasted_iota(jnp.int32, (N_GROUPS, N_GROUPS), 1)).astype(F32)
        d_rows = jnp.broadcast_to(sp[ROW_BGLU_D + 1:ROW_BGLU_D + 2, :], (GROUP, SSM_W))
        own_p = (lax.broadcasted_iota(jnp.int32, (GROUP, SSM_W), 1) % GROUP
                 == lax.broadcasted_iota(jnp.int32, (GROUP, SSM_W), 0))
        of_group = (lax.broadcasted_iota(jnp.int32, (SSM_W, N_GROUPS), 0) // GROUP
                    == lax.broadcasted_iota(jnp.int32, (SSM_W, N_GROUPS), 1)).astype(BF16)
        d_pg = sum(_dot(t, of_group) for t in _split3(jnp.where(own_p, d_rows, 0.0)))
        me = 4 * lax.axis_index("x") + 2 * lax.axis_index("y") + lax.axis_index("c")
        for k in range(N_DEV):
            @pl.when(me == k)
            def _(k=k):
                own_conv[...] = sp[ROW_CONV:ROW_CONV + SUBLANES, k * CONV_COLS_PER_DEV:(k + 1) * CONV_COLS_PER_DEV]
        grads = {
            "norm_gain": wide(ROW_NORM_GAIN),
            "final_norm_gain": wide(ROW_FINAL_GAIN),
            "b_glu": sp[ROW_BGLU_D:ROW_BGLU_D + 1, :],
            "ssm_a_re": sp[s5, LANE_A_RE:LANE_A_RE + STATE],
            "ssm_a_im": sp[s5, LANE_A_IM:LANE_A_IM + STATE],
            "ssm_log_dt": jnp.sum(sp[s5, LANE_LOG_DT:LANE_LOG_DT + 1] * eye, axis=0, keepdims=True),
            "ssm_d": d_pg,
            "ssm_c_re": sgc[:, 0:STATE],
            "ssm_c_im": sgc[:, STATE:2 * STATE],
            "ssm_b_re": sgb[:, 0:STATE],
            "ssm_b_im": sgb[:, STATE:2 * STATE],
        }
        for i, name in enumerate(_SMALL_LEAVES):
            w_ref, m_ref, v_ref = w_refs[3 * i:3 * i + 3]
            o_g, o_d, o_m, o_v = o_refs[4 * i:4 * i + 4]
            if name == "conv_w":
                for k in range(w_ref.shape[0]):
                    g = own_conv[k:k + 1, :]
                    o_g[k] = g
                    o_d[k], o_m[k], o_v[k] = _adamw(g, w_ref[k], m_ref[k], v_ref[k])
                continue
            g = grads[name]
            o_g[...] = g
            o_d[...], o_m[...], o_v[...] = _adamw(g, w_ref[...], m_ref[...], v_ref[...])

    flat_w = [a for name in _SMALL_LEAVES for a in wmv[name]]
    leaf_shapes = [_out(wmv[name][0].shape, F32) for name in _SMALL_LEAVES for _ in range(4)]
    sh_shapes = [_out(entry[1].shape, F32) for entry in sharded for _ in range(4)]
    operands = (r_pack, r_gc, r_gb, *flat_w, *(a for entry in sharded for a in entry))
    out_shape = (_out((1, 1), F32), *leaf_shapes, *sh_shapes)
    outs = _pcall(
        body, name="reduce_adam_small", grid=(1,), out_shape=out_shape,
        in_specs=_whole_specs(operands), out_specs=tuple(_whole_specs(out_shape)),
        scratch_shapes=[pltpu.VMEM((SUBLANES, CONV_COLS_PER_DEV), F32)],
        compiler_params=_params(1),
    )(*operands)
    leaves = {name: outs[1 + 4 * i:5 + 4 * i] for i, name in enumerate(_SMALL_LEAVES)}
    first = 1 + 4 * n_leaf
    return outs[0], leaves, [outs[first + 4 * i:first + 4 * i + 4] for i in range(n_sh)]


def kernel(x, norm_gain, w_in, ssm_a_re, ssm_a_im, ssm_log_dt, ssm_b_re, ssm_b_im, ssm_c_re, ssm_c_im, ssm_d, w_glu, b_glu, conv_w, w_out, final_norm_gain, loss_target, m_norm_gain, m_w_in, m_ssm_a_re, m_ssm_a_im, m_ssm_log_dt, m_ssm_b_re, m_ssm_b_im, m_ssm_c_re, m_ssm_c_im, m_ssm_d, m_w_glu, m_b_glu, m_conv_w, m_w_out, m_final_norm_gain, v_norm_gain, v_w_in, v_ssm_a_re, v_ssm_a_im, v_ssm_log_dt, v_ssm_b_re, v_ssm_b_im, v_ssm_c_re, v_ssm_c_im, v_ssm_d, v_w_glu, v_b_glu, v_conv_w, v_w_out, v_final_norm_gain):
    n_seq, seq, _ = x.shape
    n = n_seq * seq

    gh_p = lambda b4: jnp.transpose(b4, (0, 1, 3, 2)).reshape(N_GROUPS * GROUP, STATE)
    c2 = lambda a: a.reshape(N_GROUPS * GROUP, STATE)
    b_re2, b_im2 = gh_p(ssm_b_re), gh_p(ssm_b_im)
    d_row = ssm_d[0].reshape(1, SSM_W)

    x2 = x.reshape(n, D_MODEL)
    tgt2 = loss_target.reshape(n, D_MODEL)
    mx, my, mc = lax.axis_index("x"), lax.axis_index("y"), lax.axis_index("c")
    chip_ids = [2 * cx + cy for cx, cy in ((mx, my), (1 - mx, my), (mx, 1 - my), (1 - mx, 1 - my))]
    arrival = chip_ids
    xn, proj, w_in_f, s5 = _in_proj(
        jnp.stack(arrival).astype(jnp.int32), x2, norm_gain, w_in[0],
        (ssm_a_re[0], ssm_a_im[0], ssm_log_dt, b_re2, b_im2, c2(ssm_c_re), c2(ssm_c_im)))
    a_re_x, a_im_x, log_dt_x, ab_re, ab_im, bb_re_m, bb_im_m, c_re_m, c_imn_m = s5
    u3 = proj.reshape(n_seq, seq, IN_COLS)
    conv_p = jnp.pad(conv_w[0], ((0, SUBLANES - 3), (0, LANES - CONV_COLS_PER_DEV)))
    s_re, s_im, y3, w_out_f, w_glu_f, conv_all = _ssm_fwd(
        u3, bb_re_m, bb_im_m, c_re_m, c_imn_m, d_row, ab_re, ab_im,
        w_out[0], w_glu[0], conv_p, n_seq, seq)
    conv8 = jnp.transpose(conv_all[:, :, :CONV_COLS_PER_DEV], (1, 0, 2)).reshape(SUBLANES, CONV_W)
    (dh2, dy, dzs, dbc, dzc, dyc, dw_out, dw_glu, loss_t, dgf, dbg, dcw) = _mix(
        x2, tgt2, y3.reshape(n, SSM_W), proj, final_norm_gain.reshape(1, D_MODEL), b_glu, conv8,
        w_glu_f, w_out_f, seq)

    du3, dc_re_d, dc_im_d, dbb_re_d, dbb_im_d, dab_re, dab_im, dd, r_out, r_glu = _ssm_bwd(
        dy.reshape(n_seq, seq, SSM_W), u3, s_re, s_im, bb_re_m, bb_im_m, c_re_m, c_imn_m, d_row, ab_re, ab_im,
        dw_out.reshape(N_DEV, OUT_ROWS_PER_DEV, D_MODEL), dw_glu.reshape(N_DEV, GLU_ROWS_PER_DEV, SSM_W), n_seq, seq)
    du = du3.reshape(n, SSM_W)
    grad_x2, dproj, dg8 = _in_bwd(x2, dh2, du, dzs, dyc, proj, dbc, dzc, norm_gain, conv8, w_in_f, seq)
    pack, gc, gb = _ssm_disc_bwd_pack(
        a_re_x, a_im_x, log_dt_x, b_re2, b_im2, dab_re, dab_im,
        dbb_re_d, dbb_im_d, loss_t, dg8, dgf, dbg, dd, dcw, dc_re_d, dc_im_d)

    own_in, rchip_in, r_pack, r_gc, r_gb = _dw_in_exchange(
        [chip_ids[3], chip_ids[2], chip_ids[1], chip_ids[0]],
        xn, dproj, [pack, gc, gb])

    flat2 = lambda a: a.reshape(a.shape[-2:]) if a.ndim > 2 else a.reshape(1, -1)
    c2 = lambda a: a.reshape(N_GROUPS * GROUP, STATE)
    wmv = dict(norm_gain=(norm_gain, m_norm_gain, v_norm_gain),
               final_norm_gain=tuple(flat2(a) for a in (final_norm_gain, m_final_norm_gain, v_final_norm_gain)),
               b_glu=(b_glu, m_b_glu, v_b_glu),
               ssm_a_re=tuple(flat2(a) for a in (ssm_a_re, m_ssm_a_re, v_ssm_a_re)),
               ssm_a_im=tuple(flat2(a) for a in (ssm_a_im, m_ssm_a_im, v_ssm_a_im)),
               ssm_log_dt=(ssm_log_dt, m_ssm_log_dt, v_ssm_log_dt),
               ssm_d=tuple(jnp.transpose(a, (0, 2, 1)).reshape(GROUP, N_GROUPS) for a in (ssm_d, m_ssm_d, v_ssm_d)),
               conv_w=tuple(jnp.transpose(a, (1, 0, 2)) for a in (conv_w, m_conv_w, v_conv_w)),
               ssm_c_re=tuple(c2(a) for a in (ssm_c_re, m_ssm_c_re, v_ssm_c_re)),
               ssm_c_im=tuple(c2(a) for a in (ssm_c_im, m_ssm_c_im, v_ssm_c_im)),
               ssm_b_re=(b_re2, gh_p(m_ssm_b_re), gh_p(v_ssm_b_re)),
               ssm_b_im=(b_im2, gh_p(m_ssm_b_im), gh_p(v_ssm_b_im)))

    res_in = _reduce_adam_w_in(own_in, rchip_in, w_in[0], m_w_in[0], v_w_in[0])
    loss11, small, (res_out, res_glu) = _reduce_adam_small(
        r_pack, r_gc, r_gb, wmv,
        [(r_out, w_out[0], m_w_out[0], v_w_out[0]), (r_glu, w_glu[0], m_w_glu[0], v_w_glu[0])])
    loss = loss11.reshape(())

    shapes = dict(norm_gain=(1, D_MODEL), ssm_a_re=(1, N_GROUPS, STATE), ssm_a_im=(1, N_GROUPS, STATE),
                  ssm_log_dt=(1, N_GROUPS), ssm_c_re=(1, N_GROUPS, GROUP, STATE), ssm_c_im=(1, N_GROUPS, GROUP, STATE),
                  b_glu=(1, SSM_W), final_norm_gain=(D_MODEL,))
    big = dict(w_in=res_in, w_glu=res_glu, w_out=res_out)

    def leaf(kind, name):
        if name in big:
            return big[name][kind][None]
        if name in ("ssm_b_re", "ssm_b_im"):
            return jnp.transpose(small[name][kind].reshape(1, N_GROUPS, GROUP, STATE), (0, 1, 3, 2))
        if name == "ssm_d":
            return jnp.transpose(small[name][kind].reshape(1, GROUP, N_GROUPS), (0, 2, 1))
        if name == "conv_w":
            return jnp.transpose(small[name][kind], (1, 0, 2))
        return small[name][kind].reshape(shapes[name])

    order = ["norm_gain", "w_in", "ssm_a_re", "ssm_a_im", "ssm_log_dt", "ssm_b_re", "ssm_b_im", "ssm_c_re",
             "ssm_c_im", "ssm_d", "w_glu", "b_glu", "conv_w", "w_out", "final_norm_gain"]
    outs = [loss, grad_x2.reshape(x.shape)]
    for kind in range(4):
        outs += [leaf(kind, name) for name in order]
    return tuple(outs)
```

```python
import functools
import math

import jax
import jax.numpy as jnp
from jax import lax
from jax.experimental import pallas as pl
from jax.experimental.pallas import tpu as pltpu

F32 = jnp.float32
BF16 = jnp.bfloat16

N_DEV = 8
D_MODEL = 1024
SSM_W = 512
CONV_W = 512
N_GROUPS = 32
GROUP = 16
STATE = 64
IN_COLS = 3072
SEG_U, SEG_ZS, SEG_H, SEG_BC, SEG_CC, SEG_ZC = range(6)
COLS_PER_DEV = IN_COLS // N_DEV
N_CHIP = N_DEV // 2
COLS_PER_CHIP = 2 * COLS_PER_DEV
OUT_ROWS_PER_DEV = D_MODEL // N_DEV
GLU_ROWS_PER_DEV = SSM_W // N_DEV
CONV_COLS_PER_DEV = CONV_W // N_DEV
EPS = 1e-6

N_JBLK = 4
JB_CH = SSM_W // N_JBLK
JB_ST = N_GROUPS * STATE // N_JBLK

ADAM_LR = 0.001
ADAM_B1 = 0.9
ADAM_B2 = 0.999
ADAM_EPS = 1e-08
ADAM_WD = 0.01
ADAM_STEP = 10

SUBLANES = 8
LANES = 128
VMEM_LIMIT = 48 * 1024 * 1024
TOK_TILE = 256
IN_TILE = 1024
SCAN_TILE = 1024

MESH = pl.DeviceIdType.MESH
HBM_SPEC = pl.BlockSpec(memory_space=pltpu.HBM)


def _build(body, **kw):
    return pl.pallas_call(body, **kw)


def _pcall(body, **kw):
    def call(*operands):
        pinned = [a if jnp.issubdtype(a.dtype, jnp.integer) else pltpu.with_memory_space_constraint(a, pltpu.HBM)
                  for a in operands]
        return _build(body, **kw)(*pinned)
    return call


def _whole_specs(arrays):
    return [pl.BlockSpec(a.shape, functools.partial(lambda nd, i: (0,) * nd, len(a.shape))) for a in arrays]


def _out(shape, dtype):
    return pltpu.HBM(tuple(shape), dtype)


def _params(n_grid):
    return pltpu.CompilerParams(dimension_semantics=("arbitrary",) * n_grid,
                                vmem_limit_bytes=VMEM_LIMIT)


def _dot(a, b):
    return jnp.dot(a, b, preferred_element_type=F32)


def _dot_nt(a, b):
    return lax.dot_general(a, b, (((1,), (1,)), ((), ())), preferred_element_type=F32)


def _dot_tn(a, b):
    return lax.dot_general(a, b, (((0,), (0,)), ((), ())), preferred_element_type=F32)


def _sigmoid(z):
    return 1.0 / (1.0 + jnp.exp(-z))


_GELU_C = math.sqrt(2.0 / math.pi)


def _gelu_and_grad(y):
    inner = _GELU_C * (y + 0.044715 * (y * y * y))
    t = jnp.tanh(inner)
    g = 0.5 * y * (1.0 + t)
    dg = 0.5 * (1.0 + t) + 0.5 * y * (1.0 - t * t) * (_GELU_C * (1.0 + 3.0 * 0.044715 * (y * y)))
    return g, dg


def _silu_and_grad(z):
    s = _sigmoid(z)
    return z * s, s * (1.0 + z * (1.0 - s))


def _shift_down(v, halo, k):
    rolled = pltpu.roll(v, k, 0)
    row = lax.broadcasted_iota(jnp.int32, v.shape, 0)
    for r in range(k):
        rolled = jnp.where(row == r, halo[SUBLANES - k + r:SUBLANES - k + r + 1, :], rolled)
    return rolled


def _shift_up(v, halo, k):
    n = v.shape[0]
    rolled = pltpu.roll(v, n - k, 0)
    row = lax.broadcasted_iota(jnp.int32, v.shape, 0)
    for r in range(k):
        rolled = jnp.where(row == n - k + r, halo[r:r + 1, :], rolled)
    return rolled


def _mesh_pos():
    return lax.axis_index("x"), lax.axis_index("y"), lax.axis_index("c")


def _direct_copies(srcs_for, out_refs, send_sems, recv_sems, loc_sems):
    x, y, c = _mesh_pos()
    me_id = 4 * x + 2 * y + c
    n_arr = len(out_refs)
    dsts = [r.at[me_id] for r in out_refs]
    own = srcs_for(me_id)
    mine = [pltpu.make_async_copy(own[a], dsts[a], loc_sems.at[a]) for a in range(n_arr)]
    sends = []
    for k in range(1, N_DEV):
        px, py, pc = x ^ ((k >> 2) & 1), y ^ ((k >> 1) & 1), c ^ (k & 1)
        src = srcs_for(4 * px + 2 * py + pc)
        for a in range(n_arr):
            sends.append(pltpu.make_async_remote_copy(
                src_ref=src[a], dst_ref=dsts[a],
                send_sem=send_sems.at[(k - 1) * n_arr + a], recv_sem=recv_sems.at[(k - 1) * n_arr + a],
                device_id=(px, py, pc), device_id_type=MESH))
    return mine, sends


class _TwoLevelGather:
    def __init__(self, srcs, slots, send_sems, recv_sems, loc_sems):
        self.srcs, self.slots, self.n_arr = srcs, slots, len(srcs)
        self.send_sems, self.recv_sems, self.loc_sems = send_sems, recv_sems, loc_sems
        x, y, c = _mesh_pos()
        self.c = c
        self.me, self.sib = (x, y, c), (x, y, 1 - c)
        self.chips = [(1 - x, y), (x, 1 - y), (1 - x, 1 - y)]

    def _copies(self, k, block, to, from_src=False):
        dev = 4 * block[0] + 2 * block[1] + block[2]
        return [pltpu.make_async_remote_copy(
            src_ref=self.srcs[a] if from_src else self.slots[a](dev), dst_ref=self.slots[a](dev),
            send_sem=self.send_sems.at[k * self.n_arr + a], recv_sem=self.recv_sems.at[k * self.n_arr + a],
            device_id=to, device_id_type=MESH) for a in range(self.n_arr)]

    def _local(self):
        dev = 4 * self.me[0] + 2 * self.me[1] + self.me[2]
        return [pltpu.make_async_copy(self.srcs[a], self.slots[a](dev), self.loc_sems.at[a])
                for a in range(self.n_arr)]

    def start(self):
        for cp in self._local() + self._copies(0, self.me, self.sib, True):
            cp.start()
        for j in (0, 1):
            for cp in self._copies(1 + j, self.me, (*self.chips[j], self.c), True):
                cp.start()

    def wait_own(self):
        for cp in self._local():
            cp.wait()

    def wait_sibling(self):
        for cp in self._copies(0, self.sib, self.me):
            cp.wait_recv()

    def wait_and_pass_on(self, j):
        chip = self.chips[j]
        for cp in self._copies(1 + j, (*chip, self.c), self.me):
            cp.wait_recv()
        for cp in self._copies(4 + j, (*chip, self.c), self.sib):
            cp.start()

    def neighbours_landed(self):
        x, y, c = self.me
        self.wait_and_pass_on(0)
        self.wait_and_pass_on(1)
        for cp in self._copies(1 + 2, (x ^ c, y ^ (1 - c), c), (x ^ (1 - c), y ^ c, c)):
            cp.start()

    def diagonal_landed(self):
        self.wait_and_pass_on(2)

    def wait_passed_on(self, j):
        for cp in self._copies(4 + j, (*self.chips[j], 1 - self.c), self.me):
            cp.wait_recv()

    def wait_sends(self):
        for cp in self._copies(0, self.me, self.sib, True):
            cp.wait_send()
        for j, chip in enumerate(self.chips):
            for cp in self._copies(1 + j, self.me, (*chip, self.c), True) + self._copies(4 + j, (*chip, self.c), self.sib):
                cp.wait_send()

    def finish(self):
        self.wait_sibling()
        for j in range(3):
            self.wait_passed_on(j)
        self.wait_sends()
        self.wait_own()


def _disc(a_re, a_im, log_dt, b_re, b_im):
    dt = jnp.exp(log_dt)
    mag = jnp.exp(a_re * dt)
    ab_re = mag * jnp.cos(a_im * dt)
    ab_im = mag * jnp.sin(a_im * dt)
    den = a_re * a_re + a_im * a_im
    p_re = ab_re - 1.0
    p_im = ab_im
    q_re = (p_re * a_re + p_im * a_im) / den
    q_im = (p_im * a_re - p_re * a_im) / den
    bb_re = q_re * b_re - q_im * b_im
    bb_im = q_re * b_im + q_im * b_re
    return ab_re, ab_im, bb_re, bb_im


def _split3(v):
    hi = v.astype(BF16)
    r1 = v - hi.astype(F32)
    mid = r1.astype(BF16)
    lo = (r1 - mid.astype(F32)).astype(BF16)
    return hi, mid, lo


def _select_dot(sel, v):
    return sum(_dot(sel, t) for t in _split3(v))


PACK_ROWS = 72
PACK_W = 512
ROW_FINAL_GAIN, ROW_NORM_GAIN, ROW_BGLU_D, ROW_CONV, ROW_LOSS, ROW_S5 = 0, 8, 16, 24, 32, 40
LANE_A_RE, LANE_A_IM, LANE_LOG_DT = 0, 128, 256


def _ssm_disc_bwd_pack(a_re_x, a_im_x, log_dt_x, b_re, b_im, g_ab_re, g_ab_im, dbb_re_d, dbb_im_d,
                       loss_t, dg8, dgf, dbg, dd, dcw, dc_re_d, dc_im_d):
    rows_gh = N_GROUPS * GROUP

    def body(are, aim, ldt, bre, bim, gabre, gabim, dbbre_ref, dbbim_ref,
             loss_ref, dg8_ref, dgf_ref, dbg_ref, dd_ref, dcw_ref, dcre_ref, dcim_ref,
             p_ref, gc_ref, gb_ref, gbb_re, gbb_im):
        r_g = lax.broadcasted_iota(jnp.int32, (N_GROUPS, rows_gh), 0)
        c_gh = lax.broadcasted_iota(jnp.int32, (N_GROUPS, rows_gh), 1)
        group_sum = (c_gh // GROUP == r_g).astype(BF16)
        r_gh = lax.broadcasted_iota(jnp.int32, (rows_gh, N_GROUPS), 0)
        c_g = lax.broadcasted_iota(jnp.int32, (rows_gh, N_GROUPS), 1)
        first_row = (r_gh == c_g * GROUP).astype(BF16)

        def diag_block(ref, j, gi):
            return ref[j, gi * GROUP:(gi + 1) * GROUP, gi * STATE:(gi + 1) * STATE]

        for j in range(N_JBLK):
            for gi in range(SUBLANES):
                r0 = (j * SUBLANES + gi) * GROUP
                gbb_re[r0:r0 + GROUP, :] = diag_block(dbbre_ref, j, gi)
                gbb_im[r0:r0 + GROUP, :] = diag_block(dbbim_ref, j, gi)
                both = jnp.concatenate([diag_block(dcre_ref, j, gi), -diag_block(dcim_ref, j, gi)], axis=1)
                gc_ref[r0:r0 + GROUP, :] = both.astype(BF16)

        def by_group(ref):
            return jnp.concatenate([ref[:, g * STATE:(g + 1) * STATE] for g in range(N_GROUPS)], axis=0)

        _, vjp = jax.vjp(_disc, are[...], aim[...], ldt[...], bre[...], bim[...])
        d_are, d_aim, d_ldt, d_bre, d_bim = vjp((_select_dot(first_row, by_group(gabre)),
                                                 _select_dot(first_row, by_group(gabim)),
                                                 gbb_re[...], gbb_im[...]))
        gb_ref[...] = jnp.concatenate([d_bre, d_bim], axis=1).astype(BF16)

        p_ref[...] = jnp.zeros_like(p_ref)
        half = D_MODEL // 2
        for r, src in ((ROW_FINAL_GAIN, dgf_ref), (ROW_NORM_GAIN, dg8_ref)):
            p_ref[r:r + 1, :] = src[0:1, 0:half]
            p_ref[r + 1:r + 2, :] = src[0:1, half:D_MODEL]
        p_ref[ROW_BGLU_D:ROW_BGLU_D + 1, :] = dbg_ref[...]
        p_ref[ROW_BGLU_D + 1:ROW_BGLU_D + 2, :] = dd_ref[...]
        p_ref[ROW_CONV:ROW_CONV + SUBLANES, :] = dcw_ref[...]
        p_ref[ROW_LOSS:ROW_LOSS + SUBLANES, 0:LANES] = loss_ref[...]
        s5 = slice(ROW_S5, ROW_S5 + N_GROUPS)
        p_ref[s5, LANE_A_RE:LANE_A_RE + STATE] = _select_dot(group_sum, d_are)
        p_ref[s5, LANE_A_IM:LANE_A_IM + STATE] = _select_dot(group_sum, d_aim)
        p_ref[s5, LANE_LOG_DT:LANE_LOG_DT + LANES] = _select_dot(group_sum, jnp.broadcast_to(d_ldt, (rows_gh, LANES)))

    operands = (a_re_x, a_im_x, log_dt_x, b_re, b_im, g_ab_re, g_ab_im, dbb_re_d, dbb_im_d,
                loss_t, dg8, dgf, dbg, dd, dcw, dc_re_d, dc_im_d)
    out_shape = (_out((PACK_ROWS, PACK_W), F32),
                 _out((rows_gh, 2 * STATE), BF16),
                 _out((rows_gh, 2 * STATE), BF16))
    return _pcall(body, name="ssm_disc_bwd_pack", grid=(1,), out_shape=out_shape,
                  in_specs=_whole_specs(operands), out_specs=tuple(_whole_specs(out_shape)),
                  scratch_shapes=[pltpu.VMEM((rows_gh, STATE), F32), pltpu.VMEM((rows_gh, STATE), F32)],
                  compiler_params=_params(1))(*operands)


def _s5_prepare(are, aim, ldt, bre, bim, cre, cim,
                o_ax_re, o_ax_im, o_ldt_x, o_ab_re, o_ab_im, o_bb_re, o_bb_im, o_c_re, o_c_imn):
    rows_gh = N_GROUPS * GROUP
    rep = (lax.broadcasted_iota(jnp.int32, (rows_gh, N_GROUPS), 0) // GROUP
           == lax.broadcasted_iota(jnp.int32, (rows_gh, N_GROUPS), 1)).astype(BF16)
    eye = (lax.broadcasted_iota(jnp.int32, (N_GROUPS, N_GROUPS), 0)
           == lax.broadcasted_iota(jnp.int32, (N_GROUPS, N_GROUPS), 1)).astype(F32)
    ldt_col = jnp.sum(eye * ldt[...], axis=1, keepdims=True)
    a_re_x = _select_dot(rep, are[...])
    a_im_x = _select_dot(rep, aim[...])
    ldt_x = _select_dot(rep, jnp.broadcast_to(ldt_col, (N_GROUPS, LANES)))[:, 0:1]
    o_ax_re[...] = a_re_x
    o_ax_im[...] = a_im_x
    o_ldt_x[...] = ldt_x
    ab_re, ab_im, bb_re, bb_im = _disc(a_re_x, a_im_x, ldt_x, bre[...], bim[...])
    for j in range(N_JBLK):
        first = [(j * SUBLANES + gi) * GROUP for gi in range(SUBLANES)]
        o_ab_re[j] = jnp.concatenate([ab_re[r:r + 1, :] for r in first], axis=1)
        o_ab_im[j] = jnp.concatenate([ab_im[r:r + 1, :] for r in first], axis=1)
    for o, v in ((o_bb_re, bb_re), (o_bb_im, bb_im), (o_c_re, cre[...]), (o_c_imn, -cim[...])):
        for j in range(N_JBLK):
            for gi in range(SUBLANES):
                r0 = (j * SUBLANES + gi) * GROUP
                parts = [v[r0:r0 + GROUP, :] if k == gi else jnp.zeros((GROUP, STATE), F32) for k in range(SUBLANES)]
                o[j, gi * GROUP:(gi + 1) * GROUP, :] = jnp.concatenate(parts, axis=1).astype(BF16)


def _in_proj(order, x2, g1, w_in_own, s5):
    n = x2.shape[0]
    tm = min(IN_TILE, n)
    n_tiles = n // tm
    n_s5_in = len(s5)
    n_s5_out = 9

    def body(order_ref, x_ref, g_ref, w_ref, *refs):
        s5_in = refs[:n_s5_in]
        xn_ref, proj_ref, wall_ref = refs[n_s5_in:n_s5_in + 3]
        s5_out = refs[n_s5_in + 3:n_s5_in + 3 + n_s5_out]
        xn_scr, wbuf, wown, send_sems, recv_sems, loc_sems, out_sems = refs[n_s5_in + 3 + n_s5_out:]
        k = pl.program_id(0)
        i = pl.program_id(1)

        def slot(dev):
            return wbuf.at[dev // 2, :, pl.ds(pl.multiple_of((dev % 2) * COLS_PER_DEV, LANES), COLS_PER_DEV)]

        gather = _TwoLevelGather([wown], [slot], send_sems, recv_sems, loc_sems)

        @pl.when((k == 0) & (i == 0))
        def _():
            wown[...] = w_ref[...].astype(BF16)
            gather.start()

        def own_chip():
            gather.wait_own()
            gather.wait_sibling()

        def x_chip():
            gather.neighbours_landed()
            gather.wait_passed_on(0)

        def diag_chip():
            gather.diagonal_landed()
            gather.wait_passed_on(2)

        rows = pl.ds(pl.multiple_of(i * tm, tm), tm)

        @pl.when(k == 0)
        def _():
            x = x_ref[...]
            r = lax.rsqrt(jnp.mean(x * x, axis=-1, keepdims=True) + EPS)
            xn = ((x * r) * g_ref[...]).astype(BF16)
            xn_scr[rows, :] = xn
            xn_ref[...] = xn

        def keep_copy(kk):
            q = order_ref[kk]
            cols = pl.ds(pl.multiple_of(q * COLS_PER_CHIP, LANES), COLS_PER_CHIP)
            return pltpu.make_async_copy(wbuf.at[q], wall_ref.at[:, cols], out_sems.at[kk])

        arrivals = [own_chip, x_chip, functools.partial(gather.wait_passed_on, 1), diag_chip]
        for kk, arrived in enumerate(arrivals):
            @pl.when((k == kk) & (i == 0))
            def _(kk=kk, arrived=arrived):
                arrived()
                keep_copy(kk).start()

        proj_ref[...] = _dot(xn_scr[rows, :], wbuf[order_ref[k]])

        @pl.when((k == 0) & (i == n_tiles - 1))
        def _():
            _s5_prepare(*s5_in, *s5_out)

        @pl.when((k == N_CHIP - 1) & (i == n_tiles - 1))
        def _():
            gather.wait_sends()
            for kk in range(N_CHIP):
                keep_copy(kk).wait()

    tile_once = lambda k, i, order: (jnp.where(k == 0, i, n_tiles - 1), 0)
    whole = lambda shape: pl.BlockSpec(shape, lambda k, i, order: (0,) * len(shape))
    rows_gh = N_GROUPS * GROUP
    s5_out_shapes = ([(rows_gh, STATE), F32], [(rows_gh, STATE), F32], [(rows_gh, 1), F32],
                     [(N_JBLK, 1, JB_ST), F32], [(N_JBLK, 1, JB_ST), F32]) + ([(N_JBLK, JB_CH, JB_ST), BF16],) * 4
    grid_spec = pltpu.PrefetchScalarGridSpec(
        num_scalar_prefetch=1, grid=(N_CHIP, n_tiles),
        in_specs=[pl.BlockSpec((tm, D_MODEL), tile_once),
                  whole((1, D_MODEL)),
                  whole(w_in_own.shape),
                  *(whole(a.shape) for a in s5)],
        out_specs=(pl.BlockSpec((tm, D_MODEL), tile_once),
                   pl.BlockSpec((tm, COLS_PER_CHIP), lambda k, i, order: (i, order[k])),
                   HBM_SPEC,
                   *(whole(shape) for shape, _ in s5_out_shapes)),
        scratch_shapes=[pltpu.VMEM((n, D_MODEL), BF16), pltpu.VMEM((N_CHIP, D_MODEL, COLS_PER_CHIP), BF16),
                        pltpu.VMEM(w_in_own.shape, BF16),
                        pltpu.SemaphoreType.DMA((7,)), pltpu.SemaphoreType.DMA((7,)), pltpu.SemaphoreType.DMA((1,)),
                        pltpu.SemaphoreType.DMA((N_CHIP,))])
    outs = _pcall(
        body, name="in_proj", grid_spec=grid_spec,
        out_shape=(_out((n, D_MODEL), BF16), _out((n, IN_COLS), F32),
                   _out((D_MODEL, IN_COLS), BF16),
                   *(_out(shape, dt) for shape, dt in s5_out_shapes)),
        compiler_params=_params(2),
    )(order, x2, g1, w_in_own, *s5)
    return outs[0], outs[1], outs[2], outs[3:]


def _cmul(p, q):
    return p[0] * q[0] - p[1] * q[1], p[0] * q[1] + p[1] * q[0]


def _scan_tables(ar, ai, width, reverse):
    pows = [(ar, ai)]
    for _ in range(SUBLANES - 1):
        pows.append(_cmul(pows[-1], (ar, ai)))
    row = lax.broadcasted_iota(jnp.int32, (SUBLANES, width), 0)

    def bc(v):
        return jnp.broadcast_to(v, (SUBLANES, width))

    levels = []
    for k in (1, 2, 4):
        keep = (row <= SUBLANES - 1 - k) if reverse else (row >= k)
        levels.append((jnp.where(keep, bc(pows[k - 1][0]), 0.0), jnp.where(keep, bc(pows[k - 1][1]), 0.0)))
    cre = jnp.zeros((SUBLANES, width), F32)
    cim = jnp.zeros((SUBLANES, width), F32)
    for r in range(SUBLANES):
        e = (SUBLANES - r) if reverse else (r + 1)
        cre = jnp.where(row == r, bc(pows[e - 1][0]), cre)
        cim = jnp.where(row == r, bc(pows[e - 1][1]), cim)
    return levels, (cre, cim)


def _load_chunked(src_ref, b, dst_ref, n_rows):
    n_blk = n_rows // SUBLANES
    for i in range(n_blk):
        dst_ref[b, i * SUBLANES:(i + 1) * SUBLANES, :] = src_ref[b, pl.ds(i, SUBLANES, stride=n_blk), :]


def _store_chunked(val, dst_ref, b, n_rows):
    n_blk = n_rows // SUBLANES
    for i in range(n_blk):
        dst_ref[b, pl.ds(i, SUBLANES, stride=n_blk), :] = val[i * SUBLANES:(i + 1) * SUBLANES, :]


def _chunk_scan(re_ref, im_ref, bs, car_ref, ar, ai, n_rows, reverse, on_block=None):
    width = re_ref.shape[2]
    n_blk = n_rows // SUBLANES
    shape = (SUBLANES, width)
    abr = jnp.broadcast_to(ar, shape)
    abi = jnp.broadcast_to(ai, shape)
    order = list(range(n_blk - 1, -1, -1)) if reverse else list(range(n_blk))

    def blk(ref, b, i):
        return ref[b, i * SUBLANES:(i + 1) * SUBLANES, :]

    def step(state, b, i):
        sr, si = state
        return abr * sr - abi * si + blk(re_ref, b, i), abr * si + abi * sr + blk(im_ref, b, i)

    finals = {b: (blk(re_ref, b, order[0]), blk(im_ref, b, order[0])) for b in bs}
    for i in order[1:]:
        for b in bs:
            finals[b] = step(finals[b], b, i)

    mr, mi = ar, ai
    for _ in range(n_blk.bit_length() - 1):
        mr, mi = _cmul((mr, mi), (mr, mi))
    levels, _ = _scan_tables(mr, mi, width, reverse)
    mbr = jnp.broadcast_to(mr, shape)
    mbi = jnp.broadcast_to(mi, shape)
    row = lax.broadcasted_iota(jnp.int32, shape, 0)
    edge_in = SUBLANES - 1 if reverse else 0
    edge_out = 0 if reverse else SUBLANES - 1
    sh1 = SUBLANES - 1 if reverse else 1
    states = {}
    for b in bs:
        fr, fi = finals[b]
        gr = jnp.where(row == edge_in, jnp.broadcast_to(car_ref[b, 0:1, :], shape), pltpu.roll(fr, sh1, 0))
        gi = jnp.where(row == edge_in, jnp.broadcast_to(car_ref[b, 1:2, :], shape), pltpu.roll(fi, sh1, 0))
        for (lr, li), k in zip(levels, (1, 2, 4)):
            sh = (SUBLANES - k) if reverse else k
            sr = pltpu.roll(gr, sh, 0)
            si = pltpu.roll(gi, sh, 0)
            gr, gi = gr + (lr * sr - li * si), gi + (lr * si + li * sr)
        car_ref[b, 0:1, :] = (fr + (mbr * gr - mbi * gi))[edge_out:edge_out + 1, :]
        car_ref[b, 1:2, :] = (fi + (mbr * gi + mbi * gr))[edge_out:edge_out + 1, :]
        states[b] = (gr, gi)

    for i in order:
        for b in bs:
            states[b] = step(states[b], b, i)
            re_ref[b, i * SUBLANES:(i + 1) * SUBLANES, :] = states[b][0]
            im_ref[b, i * SUBLANES:(i + 1) * SUBLANES, :] = states[b][1]
            if on_block is not None:
                on_block(b, i, *states[b])


def _ssm_fwd(u, bb_re, bb_im, c_re_t, c_imn_t, d_row, ab_re, ab_im, w_out_own, w_glu_own, conv_p, n_seq, seq):
    tt = min(SCAN_TILE, seq)
    nt = seq // tt

    def body(u_ref, bbre, bbim, cre, cimn, d_ref, are, aim, wout_ref, wglu_ref, cw_ref,
             sre_ref, sim_ref, y_ref, oout_ref, oglu_ref, ocw_ref,
             up_ref, car_ref, woutb_ref, wglub_ref, send_sems, recv_sems, loc_sems):
        j = pl.program_id(0)
        t = pl.program_id(1)
        gather = _TwoLevelGather(
            [woutb_ref, wglub_ref, cw_ref],
            [lambda dev: oout_ref.at[pl.ds(pl.multiple_of(dev * OUT_ROWS_PER_DEV, OUT_ROWS_PER_DEV), OUT_ROWS_PER_DEV), :],
             lambda dev: oglu_ref.at[pl.ds(pl.multiple_of(dev * GLU_ROWS_PER_DEV, GLU_ROWS_PER_DEV), GLU_ROWS_PER_DEV), :],
             lambda dev: ocw_ref.at[dev]],
            send_sems, recv_sems, loc_sems)

        @pl.when((j == 0) & (t == 0))
        def _():
            woutb_ref[...] = wout_ref[...].astype(BF16)
            wglub_ref[...] = wglu_ref[...].astype(BF16)
            gather.start()

        @pl.when((j == N_JBLK // 2) & (t == 0))
        def _():
            gather.neighbours_landed()

        @pl.when((j == N_JBLK - 1) & (t == 0))
        def _():
            gather.diagonal_landed()

        @pl.when(t == 0)
        def _():
            car_ref[...] = jnp.zeros_like(car_ref)

        bs = list(range(n_seq))
        for b in bs:
            _load_chunked(u_ref, b, up_ref, tt)
        for b in bs:
            ub = up_ref[b].astype(BF16)
            sre_ref[b] = _dot(ub, bbre[0])
            sim_ref[b] = _dot(ub, bbim[0])
            _chunk_scan(sre_ref, sim_ref, [b], car_ref, are[0], aim[0], tt, reverse=False)
        for b in bs:
            yp = (_dot_nt(sre_ref[b].astype(BF16), cre[0]) + _dot_nt(sim_ref[b].astype(BF16), cimn[0])
                  + d_ref[...] * up_ref[b])
            _store_chunked(yp, y_ref, b, tt)

        @pl.when((j == N_JBLK - 1) & (t == nt - 1))
        def _():
            gather.finish()

    tok = lambda j, t: (0, t, j)
    blk3 = lambda j, t: (j, 0, 0)
    row = lambda j, t: (0, j)
    whole = lambda j, t: (0, 0)
    st = _out((n_seq, seq, N_JBLK * JB_ST), F32)
    n_arr = 3
    return _pcall(
        body, name="ssm_fwd", grid=(N_JBLK, nt),
        out_shape=(st, st, _out((n_seq, seq, SSM_W), F32),
                   _out((D_MODEL, D_MODEL), BF16), _out((SSM_W, SSM_W), BF16),
                   _out((N_DEV, SUBLANES, LANES), F32)),
        in_specs=[pl.BlockSpec((n_seq, tt, JB_CH), tok),
                  pl.BlockSpec((1, JB_CH, JB_ST), blk3), pl.BlockSpec((1, JB_CH, JB_ST), blk3),
                  pl.BlockSpec((1, JB_CH, JB_ST), blk3), pl.BlockSpec((1, JB_CH, JB_ST), blk3),
                  pl.BlockSpec((1, JB_CH), row), pl.BlockSpec((1, 1, JB_ST), blk3), pl.BlockSpec((1, 1, JB_ST), blk3),
                  pl.BlockSpec(w_out_own.shape, whole), pl.BlockSpec(w_glu_own.shape, whole), HBM_SPEC],
        out_specs=(pl.BlockSpec((n_seq, tt, JB_ST), tok), pl.BlockSpec((n_seq, tt, JB_ST), tok),
                   pl.BlockSpec((n_seq, tt, JB_CH), tok), HBM_SPEC, HBM_SPEC, HBM_SPEC),
        scratch_shapes=[pltpu.VMEM((n_seq, tt, JB_CH), F32), pltpu.VMEM((n_seq, SUBLANES, JB_ST), F32),
                        pltpu.VMEM(w_out_own.shape, BF16), pltpu.VMEM(w_glu_own.shape, BF16),
                        pltpu.SemaphoreType.DMA((7 * n_arr,)), pltpu.SemaphoreType.DMA((7 * n_arr,)),
                        pltpu.SemaphoreType.DMA((n_arr,))],
        compiler_params=_params(2),
    )(u, bb_re, bb_im, c_re_t, c_imn_t, d_row, ab_re, ab_im, w_out_own, w_glu_own, conv_p)


def _ssm_bwd(dy, u, s_re, s_im, bb_re, bb_im, c_re_t, c_imn_t, d_row, ab_re, ab_im, g_out, g_glu, n_seq, seq):
    tt = min(SCAN_TILE, seq)
    nt = seq // tt
    rows8 = tt // SUBLANES

    def body(dy_ref, u_ref, sre_ref, sim_ref, pre_ref, pim_ref, bbre, bbim, cre, cimn, d_ref, are, aim,
             gout_ref, gglu_ref,
             du_ref, dcre_ref, dcim_ref, dbbre_ref, dbbim_ref, dare_ref, daim_ref, dd_ref, rout_ref, rglu_ref,
             lre_ref, lim_ref, dyp_ref, up_ref, car_ref, send_sems, recv_sems, loc_sems):
        j = pl.program_id(0)
        tr = pl.program_id(1)

        def exchange():
            return _direct_copies(lambda pid: [gout_ref.at[pid], gglu_ref.at[pid]], [rout_ref, rglu_ref],
                                  send_sems, recv_sems, loc_sems)

        @pl.when((j == 0) & (tr == 0))
        def _():
            mine, sends = exchange()
            for cp in mine + sends:
                cp.start()

        @pl.when(tr == 0)
        def _():
            car_ref[...] = jnp.zeros_like(car_ref)
            for r in (dcre_ref, dcim_ref, dbbre_ref, dbbim_ref, dare_ref, daim_ref, dd_ref):
                r[...] = jnp.zeros_like(r)

        first = tr == nt - 1
        row = lax.broadcasted_iota(jnp.int32, (SUBLANES, JB_ST), 0)
        n_blk = tt // SUBLANES
        bs = list(range(n_seq))
        for b in bs:
            _load_chunked(dy_ref, b, dyp_ref, tt)
            _load_chunked(u_ref, b, up_ref, tt)
        for b in bs:
            dyb = dyp_ref[b].astype(BF16)
            lre_ref[b] = _dot(dyb, cre[0])
            lim_ref[b] = _dot(dyb, cimn[0])
        acc = {b: [jnp.zeros((SUBLANES, JB_ST), F32), jnp.zeros((SUBLANES, JB_ST), F32)] for b in bs}

        def on_block(b, i, lr, li):
            if i > 0:
                spr = sre_ref[b, (i - 1) * SUBLANES:i * SUBLANES, :]
                spi = sim_ref[b, (i - 1) * SUBLANES:i * SUBLANES, :]
            else:
                hr = jnp.where(first, 0.0, pre_ref[b, SUBLANES - 1:SUBLANES, :])
                hi = jnp.where(first, 0.0, pim_ref[b, SUBLANES - 1:SUBLANES, :])
                last_r = sre_ref[b, (n_blk - 1) * SUBLANES:n_blk * SUBLANES, :]
                last_i = sim_ref[b, (n_blk - 1) * SUBLANES:n_blk * SUBLANES, :]
                spr = jnp.where(row == 0, jnp.broadcast_to(hr, row.shape), pltpu.roll(last_r, 1, 0))
                spi = jnp.where(row == 0, jnp.broadcast_to(hi, row.shape), pltpu.roll(last_i, 1, 0))
            acc[b][0] = acc[b][0] + (lr * spr + li * spi)
            acc[b][1] = acc[b][1] + (li * spr - lr * spi)

        _chunk_scan(lre_ref, lim_ref, bs, car_ref, are[0], -aim[0], tt, reverse=True, on_block=on_block)
        for b in bs:
            dare_ref[...] += jnp.sum(acc[b][0], axis=0, keepdims=True)
            daim_ref[...] += jnp.sum(acc[b][1], axis=0, keepdims=True)
            dyp = dyp_ref[b]
            up = up_ref[b]
            dyb = dyp.astype(BF16)
            ub = up.astype(BF16)
            lrb = lre_ref[b].astype(BF16)
            lib = lim_ref[b].astype(BF16)
            dup = d_ref[...] * dyp + _dot_nt(lrb, bbre[0]) + _dot_nt(lib, bbim[0])
            _store_chunked(dup, du_ref, b, tt)
            dbbre_ref[0] += _dot_tn(ub, lrb)
            dbbim_ref[0] += _dot_tn(ub, lib)
            dcre_ref[0] += _dot_tn(dyb, sre_ref[b].astype(BF16))
            dcim_ref[0] += _dot_tn(dyb, sim_ref[b].astype(BF16))
            dd_ref[...] += jnp.sum(dyp * up, axis=0, keepdims=True)

        @pl.when((j == N_JBLK - 1) & (tr == nt - 1))
        def _():
            mine, sends = exchange()
            for cp in sends + mine:
                cp.wait()

    tok = lambda j, t: (0, nt - 1 - t, j)
    halo = lambda j, t: (0, jnp.maximum((nt - 1 - t) * rows8 - 1, 0), j)
    blk3 = lambda j, t: (j, 0, 0)
    row1 = lambda j, t: (0, j)
    acc_shape = _out((N_JBLK, JB_CH, JB_ST), F32)
    return _pcall(
        body, name="ssm_bwd", grid=(N_JBLK, nt),
        out_shape=(_out((n_seq, seq, SSM_W), F32), acc_shape, acc_shape, acc_shape, acc_shape,
                   _out((1, N_JBLK * JB_ST), F32), _out((1, N_JBLK * JB_ST), F32),
                   _out((1, SSM_W), F32),
                   _out((N_DEV,) + g_out.shape[1:], F32),
                   _out((N_DEV,) + g_glu.shape[1:], F32)),
        in_specs=[pl.BlockSpec((n_seq, tt, JB_CH), tok), pl.BlockSpec((n_seq, tt, JB_CH), tok),
                  pl.BlockSpec((n_seq, tt, JB_ST), tok), pl.BlockSpec((n_seq, tt, JB_ST), tok),
                  pl.BlockSpec((n_seq, SUBLANES, JB_ST), halo), pl.BlockSpec((n_seq, SUBLANES, JB_ST), halo),
                  pl.BlockSpec((1, JB_CH, JB_ST), blk3), pl.BlockSpec((1, JB_CH, JB_ST), blk3),
                  pl.BlockSpec((1, JB_CH, JB_ST), blk3), pl.BlockSpec((1, JB_CH, JB_ST), blk3),
                  pl.BlockSpec((1, JB_CH), row1), pl.BlockSpec((1, 1, JB_ST), blk3), pl.BlockSpec((1, 1, JB_ST), blk3),
                  HBM_SPEC, HBM_SPEC],
        out_specs=(pl.BlockSpec((n_seq, tt, JB_CH), tok),
                   pl.BlockSpec((1, JB_CH, JB_ST), blk3), pl.BlockSpec((1, JB_CH, JB_ST), blk3),
                   pl.BlockSpec((1, JB_CH, JB_ST), blk3), pl.BlockSpec((1, JB_CH, JB_ST), blk3),
                   pl.BlockSpec((1, JB_ST), row1), pl.BlockSpec((1, JB_ST), row1), pl.BlockSpec((1, JB_CH), row1),
                   HBM_SPEC, HBM_SPEC),
        scratch_shapes=[pltpu.VMEM((n_seq, tt, JB_ST), F32), pltpu.VMEM((n_seq, tt, JB_ST), F32),
                        pltpu.VMEM((n_seq, tt, JB_CH), F32), pltpu.VMEM((n_seq, tt, JB_CH), F32),
                        pltpu.VMEM((n_seq, SUBLANES, JB_ST), F32),
                        pltpu.SemaphoreType.DMA((7 * 2,)), pltpu.SemaphoreType.DMA((7 * 2,)),
                        pltpu.SemaphoreType.DMA((2,))],
        compiler_params=_params(2),
    )(dy, u, s_re, s_im, s_re, s_im, bb_re, bb_im, c_re_t, c_imn_t, d_row, ab_re, ab_im, g_out, g_glu)


def _mix(x2, tgt2, y, proj, gf, b_glu, conv8, w_glu_f, w_out_f, seq):
    n = x2.shape[0]
    tm = TOK_TILE
    tiles_per_seq = seq // tm
    rows8 = tm // SUBLANES

    def body(x_ref, t_ref, y_ref, zs_ref, h_ref, bc_ref, cc_ref, zc_ref, hp_ref, ccp_ref,
             gf_ref, bg_ref, cw_ref, wg_ref, wo_ref,
             dh2_ref, dy_ref, dzs_ref, dbc_ref, dzc_ref, dyc_ref,
             dwo_ref, dwg_ref, loss_ref, dgf_ref, dbg_ref, dcw_ref):
        i = pl.program_id(0)

        @pl.when(i == 0)
        def _():
            for r in (dwo_ref, dwg_ref, loss_ref, dgf_ref, dbg_ref, dcw_ref):
                r[...] = jnp.zeros_like(r)

        yv = y_ref[...]
        y1, dgelu = _gelu_and_grad(yv)
        y1b = y1.astype(BF16)
        gate = _sigmoid(_dot(y1b, wg_ref[...]) + bg_ref[...])
        y2 = y1 * gate
        szs, dszs = _silu_and_grad(zs_ref[...])
        yssm = y2 * szs
        hv = h_ref[...]
        ccv = cc_ref[...]
        bcv = bc_ref[...]
        v = ccv * hv
        first = (i % tiles_per_seq) == 0
        vhalo = jnp.where(first, 0.0, ccp_ref[...] * hp_ref[...])
        v1 = _shift_down(v, vhalo, 1)
        v2 = _shift_down(v, vhalo, 2)
        w0 = cw_ref[0:1, :]
        w1 = cw_ref[1:2, :]
        w2 = cw_ref[2:3, :]
        yc = w0 * v2 + w1 * v1 + w2 * v
        szc, dszc = _silu_and_grad(zc_ref[...])
        yconv = (bcv * yc) * szc
        ysb = yssm.astype(BF16)
        ycb = yconv.astype(BF16)
        h2 = x_ref[...] + _dot(ysb, wo_ref[0:SSM_W, :]) + _dot(ycb, wo_ref[SSM_W:, :])
        r2 = lax.rsqrt(jnp.mean(h2 * h2, axis=-1, keepdims=True) + EPS)
        hn = h2 * r2
        gfv = gf_ref[...]
        err = hn * gfv - t_ref[...]
        loss_ref[...] += 0.5 * jnp.sum(jnp.mean(err * err, axis=-1, keepdims=True))
        dout = err * (1.0 / D_MODEL)
        dgf_ref[...] += jnp.sum(dout * hn, axis=0, keepdims=True)
        dn = dout * gfv
        dh2 = r2 * (dn - hn * jnp.mean(dn * hn, axis=-1, keepdims=True))
        dh2_ref[...] = dh2
        dh2b = dh2.astype(BF16)
        dwo_ref[0:SSM_W, :] += _dot_tn(ysb, dh2b)
        dwo_ref[SSM_W:, :] += _dot_tn(ycb, dh2b)
        dyssm = _dot_nt(dh2b, wo_ref[0:SSM_W, :])
        dyconv = _dot_nt(dh2b, wo_ref[SSM_W:, :])
        dy2 = dyssm * szs
        dzs_ref[...] = (dyssm * y2 * dszs).astype(BF16)
        dgp = dy2 * y1 * (gate * (1.0 - gate))
        dgpb = dgp.astype(BF16)
        dy1 = dy2 * gate + _dot_nt(dgpb, wg_ref[...])
        dwg_ref[...] += _dot_tn(y1b, dgpb)
        dbg_ref[...] += jnp.sum(dgp, axis=0, keepdims=True)
        dy_ref[...] = dy1 * dgelu
        dbc_ref[...] = (dyconv * yc * szc).astype(BF16)
        dyc = dyconv * bcv * szc
        dyc_ref[...] = dyc
        dzc_ref[...] = (dyconv * bcv * yc * dszc).astype(BF16)
        dcw_ref[0:1, :] += jnp.sum(dyc * v2, axis=0, keepdims=True)
        dcw_ref[1:2, :] += jnp.sum(dyc * v1, axis=0, keepdims=True)
        dcw_ref[2:3, :] += jnp.sum(dyc * v, axis=0, keepdims=True)

    tile_d = pl.BlockSpec((tm, D_MODEL), lambda i: (i, 0))
    tile_s = pl.BlockSpec((tm, SSM_W), lambda i: (i, 0))
    seg_of = lambda c: pl.BlockSpec((tm, SSM_W), lambda i: (i, c))
    halo_of = lambda c: pl.BlockSpec((SUBLANES, SSM_W), lambda i: (jnp.maximum(i * rows8 - 1, 0), c))
    const = lambda shape: pl.BlockSpec(shape, lambda i: (0,) * len(shape))
    seg = _out((n, SSM_W), F32)
    seg_b = _out((n, SSM_W), BF16)
    return _pcall(
        body, name="mix", grid=(n // tm,),
        out_shape=(_out((n, D_MODEL), F32), seg, seg_b, seg_b, seg_b, seg,
                   _out((D_MODEL, D_MODEL), F32), _out((SSM_W, SSM_W), F32),
                   _out((SUBLANES, LANES), F32), _out((1, D_MODEL), F32),
                   _out((1, SSM_W), F32), _out((SUBLANES, CONV_W), F32)),
        in_specs=[tile_d, tile_d, tile_s, seg_of(SEG_ZS), seg_of(SEG_H), seg_of(SEG_BC), seg_of(SEG_CC), seg_of(SEG_ZC),
                  halo_of(SEG_H), halo_of(SEG_CC),
                  const((1, D_MODEL)), const((1, SSM_W)), const((SUBLANES, CONV_W)),
                  const((SSM_W, SSM_W)), const((D_MODEL, D_MODEL))],
        out_specs=(tile_d, tile_s, tile_s, tile_s, tile_s, tile_s,
                   const((D_MODEL, D_MODEL)), const((SSM_W, SSM_W)), const((SUBLANES, LANES)),
                   const((1, D_MODEL)), const((1, SSM_W)), const((SUBLANES, CONV_W))),
        compiler_params=_params(1),
    )(x2, tgt2, y, proj, proj, proj, proj, proj, proj, proj, gf, b_glu, conv8, w_glu_f, w_out_f)


def _in_bwd(x2, dh2, du, dzs, dyc, proj, dbc, dzc, g1, conv8, w_full, seq):
    n = x2.shape[0]
    tm = TOK_TILE
    n_tiles = n // tm
    tiles_per_seq = seq // tm
    rows8 = tm // SUBLANES
    n_blk8 = n // SUBLANES

    def body(x_ref, dh2_ref, du_ref, dzs_ref, dyc_ref, dycn_ref, h_ref, cc_ref, dbc_ref, dzc_ref,
             g_ref, cw_ref, w_ref, gx_ref, dp_ref, dg_ref):
        i = pl.program_id(0)

        @pl.when(i == 0)
        def _():
            dg_ref[...] = jnp.zeros_like(dg_ref)

        dyc = dyc_ref[...]
        last = (i % tiles_per_seq) == tiles_per_seq - 1
        nhalo = jnp.where(last, 0.0, dycn_ref[...])
        dv = (cw_ref[2:3, :] * dyc + cw_ref[1:2, :] * _shift_up(dyc, nhalo, 1)
              + cw_ref[0:1, :] * _shift_up(dyc, nhalo, 2))
        parts = (du_ref[...], dzs_ref[...], dv * cc_ref[...], dbc_ref[...], dv * h_ref[...], dzc_ref[...])
        dxn = jnp.zeros((tm, D_MODEL), F32)
        for k, p in enumerate(parts):
            pb = p.astype(BF16)
            dp_ref[:, k * SSM_W:(k + 1) * SSM_W] = pb
            dxn = dxn + _dot_nt(pb, w_ref[:, k * SSM_W:(k + 1) * SSM_W])
        x = x_ref[...]
        r = lax.rsqrt(jnp.mean(x * x, axis=-1, keepdims=True) + EPS)
        xh = x * r
        dg_ref[...] += jnp.sum(dxn * xh, axis=0, keepdims=True)
        dn = dxn * g_ref[...]
        gx_ref[...] = dh2_ref[...] + r * (dn - xh * jnp.mean(dn * xh, axis=-1, keepdims=True))

    tile_d = pl.BlockSpec((tm, D_MODEL), lambda i: (i, 0))
    tile_s = pl.BlockSpec((tm, SSM_W), lambda i: (i, 0))
    seg_of = lambda c: pl.BlockSpec((tm, SSM_W), lambda i: (i, c))
    nhalo = pl.BlockSpec((SUBLANES, SSM_W), lambda i: (jnp.minimum((i + 1) * rows8, n_blk8 - 1), 0))
    const = lambda shape: pl.BlockSpec(shape, lambda i: (0,) * len(shape))
    return _pcall(
        body, name="in_bwd", grid=(n_tiles,),
        out_shape=(_out((n, D_MODEL), F32), _out((n, IN_COLS), BF16),
                   _out((SUBLANES, D_MODEL), F32)),
        in_specs=[tile_d, tile_d, tile_s, tile_s, tile_s, nhalo, seg_of(SEG_H), seg_of(SEG_CC), tile_s, tile_s,
                  const((1, D_MODEL)), const((SUBLANES, CONV_W)), const((D_MODEL, IN_COLS))],
        out_specs=(tile_d, pl.BlockSpec((tm, IN_COLS), lambda i: (i, 0)), const((SUBLANES, D_MODEL))),
        compiler_params=_params(1),
    )(x2, dh2, du, dzs, dyc, dyc, proj, proj, dbc, dzc, g1, conv8, w_full)


_HALF_BLOCKS = ((0, 0), (0, 1), (1, 0), (2, 0), (1, 1), (2, 1), (3, 0), (3, 1))


def _dw_in_exchange(chips, xn, dproj, smalls):
    n = xn.shape[0]
    tk = min(1024, n)
    nk = n // tk
    piece = (D_MODEL, COLS_PER_DEV)
    hr = D_MODEL // 2
    n_half = len(_HALF_BLOCKS)
    n_small = len(smalls)
    assert _HALF_BLOCKS[0][1] == 0 and _HALF_BLOCKS[1][1] == 1
    order = jnp.stack([chips[b] for b, _ in _HALF_BLOCKS]
                      + [jnp.int32(t) for _, t in _HALF_BLOCKS]).astype(jnp.int32)

    def body(order_ref, xn_hbm, dp_ref, *refs):
        sm_refs = refs[:n_small]
        own_ref, rchip_ref = refs[n_small:n_small + 2]
        rsm_refs = refs[n_small + 2:2 * n_small + 2]
        (xn_ref, acc, stage, rbuf, kbuf, relay_in, xn_sems, give_send, give_recv, keep_send, keep_recv,
         relay_send, relay_recv, sm_send, sm_recv, sm_loc) = refs[2 * n_small + 2:]
        s = pl.program_id(0)

        def xn_copy(kk, t):
            rows = pl.ds(pl.multiple_of(kk * tk, tk), tk)
            return pltpu.make_async_copy(xn_hbm.at[rows, t * hr:(t + 1) * hr], xn_ref.at[t, rows, :],
                                         xn_sems.at[2 * kk + t])

        @pl.when(s == 0)
        def _():
            for kk in range(nk):
                for t in range(2):
                    xn_copy(kk, t).start()
            xn_copy(0, 0).wait()

        @pl.when(s == 1)
        def _():
            xn_copy(0, 1).wait()

        x, y, c = _mesh_pos()
        sib = (x, y, 1 - c)
        y_nbr, x_nbr = (x, 1 - y, c), (1 - x, y, c)
        gather = _TwoLevelGather(list(sm_refs), [functools.partial(lambda r, dev: r.at[dev], r) for r in rsm_refs],
                                 sm_send, sm_recv, sm_loc)

        def give(h):
            cols = pl.ds(pl.multiple_of((1 - c) * COLS_PER_DEV, LANES), COLS_PER_DEV)
            return pltpu.make_async_remote_copy(src_ref=acc.at[h % 2, :, cols], dst_ref=stage.at[h],
                                                send_sem=give_send.at[h], recv_sem=give_recv.at[h],
                                                device_id=sib, device_id_type=MESH)

        def relay(r):
            return pltpu.make_async_remote_copy(src_ref=rbuf.at[r], dst_ref=relay_in.at[r],
                                                send_sem=relay_send.at[r], recv_sem=relay_recv.at[r],
                                                device_id=(x_nbr, y_nbr)[r], device_id_type=MESH)

        def keep(q):
            return pltpu.make_async_remote_copy(src_ref=kbuf.at[q], dst_ref=rchip_ref.at[q // 2, pl.ds((q % 2) * hr, hr), :],
                                                send_sem=keep_send.at[q], recv_sem=keep_recv.at[q],
                                                device_id=(y_nbr, x_nbr)[q // 2], device_id_type=MESH)

        def chip_sum(h):
            give(h).wait_recv()
            mine = [acc[h % 2, :, cc * COLS_PER_DEV:(cc + 1) * COLS_PER_DEV] for cc in range(2)]
            return jnp.where(c == 0, mine[0], mine[1]) + stage[h]

        @pl.when(s == 0)
        def _():
            gather.start()

        @pl.when(s == 2)
        def _():
            gather.neighbours_landed()

        @pl.when(s == n_half - 2)
        def _():
            gather.diagonal_landed()

        for k in range(2, n_half):
            @pl.when(s == k)
            def _(k=k):
                give(k - 2).wait_send()

        slot = s % 2
        t_half = order_ref[n_half + s]
        acc[slot] = _dot_tn(xn_ref[t_half, pl.ds(0, tk), :], dp_ref[pl.ds(0, tk), :])

        def kstep(kk, carry):
            for t in range(2):
                @pl.when(s == t)
                def _(t=t):
                    xn_copy(kk, t).wait()

            off = pl.multiple_of(kk * tk, tk)
            acc[slot] += _dot_tn(xn_ref[t_half, pl.ds(off, tk), :], dp_ref[pl.ds(off, tk), :])
            return carry

        n_first = max(1, (3 * nk) // 4)
        lax.fori_loop(1, n_first, kstep, 0)
        for k in range(1, n_half):
            @pl.when(s == k)
            def _(k=k):
                h = k - 1
                b, t = _HALF_BLOCKS[h]
                total = chip_sum(h)
                if b == 0:
                    rbuf[t] = total.astype(BF16)
                    relay(t).start()
                elif b < 3:
                    if (b, t) in ((1, 0), (2, 1)):
                        relay(t).wait_recv()
                        total = total + relay_in[t].astype(F32)
                    q = 2 * (b - 1) + t
                    kbuf[q] = total.astype(BF16)
                    keep(q).start()
                else:
                    own_ref[0:hr, :] = total

        lax.fori_loop(n_first, nk, kstep, 0)

        for k in range(n_half):
            @pl.when(s == k)
            def _(k=k):
                give(k).start()

        @pl.when(s == n_half - 1)
        def _():
            own_ref[hr:D_MODEL, :] = chip_sum(n_half - 1)
            give(n_half - 2).wait_send()
            give(n_half - 1).wait_send()
            for r in range(2):
                relay(r).wait_send()
            for q in range(4):
                keep(q).wait()
            gather.finish()

    half_piece = (hr, COLS_PER_DEV)
    grid_spec = pltpu.PrefetchScalarGridSpec(
        num_scalar_prefetch=1, grid=(n_half,),
        in_specs=[HBM_SPEC,
                  pl.BlockSpec((n, COLS_PER_CHIP), lambda s, order: (0, order[s])),
                  *([HBM_SPEC] * n_small)],
        out_specs=(pl.BlockSpec(piece, lambda s, order: (0, 0)), HBM_SPEC, *([HBM_SPEC] * n_small)),
        scratch_shapes=[pltpu.VMEM((2, n, hr), BF16),
                        pltpu.VMEM((2, hr, COLS_PER_CHIP), F32), pltpu.VMEM((n_half,) + half_piece, F32),
                        pltpu.VMEM((2,) + half_piece, BF16), pltpu.VMEM((4,) + half_piece, BF16),
                        pltpu.VMEM((2,) + half_piece, BF16),
                        pltpu.SemaphoreType.DMA((2 * nk,)),
                        pltpu.SemaphoreType.DMA((n_half,)), pltpu.SemaphoreType.DMA((n_half,)),
                        pltpu.SemaphoreType.DMA((4,)), pltpu.SemaphoreType.DMA((4,)),
                        pltpu.SemaphoreType.DMA((2,)), pltpu.SemaphoreType.DMA((2,)),
                        pltpu.SemaphoreType.DMA((7 * n_small,)), pltpu.SemaphoreType.DMA((7 * n_small,)),
                        pltpu.SemaphoreType.DMA((n_small,))])
    return _pcall(
        body, name="dw_in_exchange", grid_spec=grid_spec,
        out_shape=(_out(piece, F32), _out((2,) + piece, BF16),
                   *(_out((N_DEV,) + a.shape, a.dtype) for a in smalls)),
        compiler_params=_params(1),
    )(order, xn, dproj, *smalls)


def _adamw(g, w, m, v):
    m_new = ADAM_B1 * m + (1.0 - ADAM_B1) * g
    v_new = ADAM_B2 * v + (1.0 - ADAM_B2) * (g * g)
    m_hat = m_new / (1.0 - ADAM_B1 ** ADAM_STEP)
    v_hat = v_new / (1.0 - ADAM_B2 ** ADAM_STEP)
    delta = -ADAM_LR * (m_hat / (jnp.sqrt(v_hat) + ADAM_EPS) + ADAM_WD * w)
    return delta, m_new, v_new


def _reduce_adam_w_in(own, rchip, w, m, v):
    rows, cols = w.shape
    row_tile = 256

    def body(o_ref, r_ref, w_ref, m_ref, v_ref, g_ref, d_ref, nm_ref, nv_ref):
        g = o_ref[...]
        for s in range(2):
            g = g + r_ref[s].astype(F32)
        g_ref[...] = g
        d_ref[...], nm_ref[...], nv_ref[...] = _adamw(g, w_ref[...], m_ref[...], v_ref[...])

    tile = pl.BlockSpec((row_tile, cols), lambda i: (i, 0))
    shp = _out((rows, cols), F32)
    return _pcall(
        body, name="reduce_adam_w_in", grid=(rows // row_tile,),
        out_shape=(shp,) * 4,
        in_specs=[tile, pl.BlockSpec((2, row_tile, cols), lambda i: (0, i, 0)), tile, tile, tile],
        out_specs=(tile,) * 4,
        compiler_params=_params(1),
    )(own, rchip, w, m, v)


_SMALL_LEAVES = ("norm_gain", "final_norm_gain", "b_glu", "ssm_a_re", "ssm_a_im", "ssm_log_dt", "ssm_d", "conv_w",
                 "ssm_c_re", "ssm_c_im", "ssm_b_re", "ssm_b_im")


def _reduce_adam_small(r_pack, r_gc, r_gb, wmv, sharded):
    n_leaf = len(_SMALL_LEAVES)
    n_sh = len(sharded)

    def body(*refs):
        rp_ref, rgc_ref, rgb_ref = refs[:3]
        w_refs = refs[3:3 + 3 * n_leaf]
        sh_in = refs[3 + 3 * n_leaf:3 + 3 * n_leaf + 4 * n_sh]
        outs0 = 3 + 3 * n_leaf + 4 * n_sh
        loss_ref = refs[outs0]
        o_refs = refs[outs0 + 1:outs0 + 1 + 4 * n_leaf]
        sh_out = refs[outs0 + 1 + 4 * n_leaf:outs0 + 1 + 4 * n_leaf + 4 * n_sh]
        own_conv = refs[-1]

        def total(ref):
            acc = ref[0].astype(F32)
            for s in range(1, N_DEV):
                acc = acc + ref[s].astype(F32)
            return acc

        for i in range(n_sh):
            r_ref, w_ref, m_ref, v_ref = sh_in[4 * i:4 * i + 4]
            o_g, o_d, o_m, o_v = sh_out[4 * i:4 * i + 4]
            g = total(r_ref)
            o_g[...] = g
            o_d[...], o_m[...], o_v[...] = _adamw(g, w_ref[...], m_ref[...], v_ref[...])

        sp = total(rp_ref)
        sgc = total(rgc_ref)
        sgb = total(rgb_ref)
        loss_ref[...] = sp[ROW_LOSS:ROW_LOSS + 1, 0:1]

        def wide(r):
            return jnp.concatenate([sp[r:r + 1, :], sp[r + 1:r + 2, :]], axis=1)

        s5 = slice(ROW_S5, ROW_S5 + N_GROUPS)
        eye = (lax.broadcasted_iota(jnp.int32, (N_GROUPS, N_GROUPS), 0)
               == lax.broadcasted_iota(jnp.int32, (N_GROUPS, N_GROUPS), 1)).astype(F32)
        d_rows = jnp.broadcast_to(sp[ROW_BGLU_D + 1:ROW_BGLU_D + 2, :], (GROUP, SSM_W))
        own_p = (lax.broadcasted_iota(jnp.int32, (GROUP, SSM_W), 1) % GROUP
                 == lax.broadcasted_iota(jnp.int32, (GROUP, SSM_W), 0))
        of_group = (lax.broadcasted_iota(jnp.int32, (SSM_W, N_GROUPS), 0) // GROUP
                    == lax.broadcasted_iota(jnp.int32, (SSM_W, N_GROUPS), 1)).astype(BF16)
        d_pg = sum(_dot(t, of_group) for t in _split3(jnp.where(own_p, d_rows, 0.0)))
        me = 4 * lax.axis_index("x") + 2 * lax.axis_index("y") + lax.axis_index("c")
        for k in range(N_DEV):
            @pl.when(me == k)
            def _(k=k):
                own_conv[...] = sp[ROW_CONV:ROW_CONV + SUBLANES, k * CONV_COLS_PER_DEV:(k + 1) * CONV_COLS_PER_DEV]
        grads = {
            "norm_gain": wide(ROW_NORM_GAIN),
            "final_norm_gain": wide(ROW_FINAL_GAIN),
            "b_glu": sp[ROW_BGLU_D:ROW_BGLU_D + 1, :],
            "ssm_a_re": sp[s5, LANE_A_RE:LANE_A_RE + STATE],
            "ssm_a_im": sp[s5, LANE_A_IM:LANE_A_IM + STATE],
            "ssm_log_dt": jnp.sum(sp[s5, LANE_LOG_DT:LANE_LOG_DT + 1] * eye, axis=0, keepdims=True),
            "ssm_d": d_pg,
            "ssm_c_re": sgc[:, 0:STATE],
            "ssm_c_im": sgc[:, STATE:2 * STATE],
            "ssm_b_re": sgb[:, 0:STATE],
            "ssm_b_im": sgb[:, STATE:2 * STATE],
        }
        for i, name in enumerate(_SMALL_LEAVES):
            w_ref, m_ref, v_ref = w_refs[3 * i:3 * i + 3]
            o_g, o_d, o_m, o_v = o_refs[4 * i:4 * i + 4]
            if name == "conv_w":
                for k in range(w_ref.shape[0]):
                    g = own_conv[k:k + 1, :]
                    o_g[k] = g
                    o_d[k], o_m[k], o_v[k] = _adamw(g, w_ref[k], m_ref[k], v_ref[k])
                continue
            g = grads[name]
            o_g[...] = g
            o_d[...], o_m[...], o_v[...] = _adamw(g, w_ref[...], m_ref[...], v_ref[...])

    flat_w = [a for name in _SMALL_LEAVES for a in wmv[name]]
    leaf_shapes = [_out(wmv[name][0].shape, F32) for name in _SMALL_LEAVES for _ in range(4)]
    sh_shapes = [_out(entry[1].shape, F32) for entry in sharded for _ in range(4)]
    operands = (r_pack, r_gc, r_gb, *flat_w, *(a for entry in sharded for a in entry))
    out_shape = (_out((1, 1), F32), *leaf_shapes, *sh_shapes)
    outs = _pcall(
        body, name="reduce_adam_small", grid=(1,), out_shape=out_shape,
        in_specs=_whole_specs(operands), out_specs=tuple(_whole_specs(out_shape)),
        scratch_shapes=[pltpu.VMEM((SUBLANES, CONV_COLS_PER_DEV), F32)],
        compiler_params=_params(1),
    )(*operands)
    leaves = {name: outs[1 + 4 * i:5 + 4 * i] for i, name in enumerate(_SMALL_LEAVES)}
    first = 1 + 4 * n_leaf
    return outs[0], leaves, [outs[first + 4 * i:first + 4 * i + 4] for i in range(n_sh)]


def kernel(x, norm_gain, w_in, ssm_a_re, ssm_a_im, ssm_log_dt, ssm_b_re, ssm_b_im, ssm_c_re, ssm_c_im, ssm_d, w_glu, b_glu, conv_w, w_out, final_norm_gain, loss_target, m_norm_gain, m_w_in, m_ssm_a_re, m_ssm_a_im, m_ssm_log_dt, m_ssm_b_re, m_ssm_b_im, m_ssm_c_re, m_ssm_c_im, m_ssm_d, m_w_glu, m_b_glu, m_conv_w, m_w_out, m_final_norm_gain, v_norm_gain, v_w_in, v_ssm_a_re, v_ssm_a_im, v_ssm_log_dt, v_ssm_b_re, v_ssm_b_im, v_ssm_c_re, v_ssm_c_im, v_ssm_d, v_w_glu, v_b_glu, v_conv_w, v_w_out, v_final_norm_gain):
    n_seq, seq, _ = x.shape
    n = n_seq * seq

    gh_p = lambda b4: jnp.transpose(b4, (0, 1, 3, 2)).reshape(N_GROUPS * GROUP, STATE)
    c2 = lambda a: a.reshape(N_GROUPS * GROUP, STATE)
    b_re2, b_im2 = gh_p(ssm_b_re), gh_p(ssm_b_im)
    d_row = ssm_d[0].reshape(1, SSM_W)

    x2 = x.reshape(n, D_MODEL)
    tgt2 = loss_target.reshape(n, D_MODEL)
    mx, my, mc = lax.axis_index("x"), lax.axis_index("y"), lax.axis_index("c")
    chip_ids = [2 * cx + cy for cx, cy in ((mx, my), (1 - mx, my), (mx, 1 - my), (1 - mx, 1 - my))]
    arrival = chip_ids
    xn, proj, w_in_f, s5 = _in_proj(
        jnp.stack(arrival).astype(jnp.int32), x2, norm_gain, w_in[0],
        (ssm_a_re[0], ssm_a_im[0], ssm_log_dt, b_re2, b_im2, c2(ssm_c_re), c2(ssm_c_im)))
    a_re_x, a_im_x, log_dt_x, ab_re, ab_im, bb_re_m, bb_im_m, c_re_m, c_imn_m = s5
    u3 = proj.reshape(n_seq, seq, IN_COLS)
    conv_p = jnp.pad(conv_w[0], ((0, SUBLANES - 3), (0, LANES - CONV_COLS_PER_DEV)))
    s_re, s_im, y3, w_out_f, w_glu_f, conv_all = _ssm_fwd(
        u3, bb_re_m, bb_im_m, c_re_m, c_imn_m, d_row, ab_re, ab_im,
        w_out[0], w_glu[0], conv_p, n_seq, seq)
    conv8 = jnp.transpose(conv_all[:, :, :CONV_COLS_PER_DEV], (1, 0, 2)).reshape(SUBLANES, CONV_W)
    (dh2, dy, dzs, dbc, dzc, dyc, dw_out, dw_glu, loss_t, dgf, dbg, dcw) = _mix(
        x2, tgt2, y3.reshape(n, SSM_W), proj, final_norm_gain.reshape(1, D_MODEL), b_glu, conv8,
        w_glu_f, w_out_f, seq)

    du3, dc_re_d, dc_im_d, dbb_re_d, dbb_im_d, dab_re, dab_im, dd, r_out, r_glu = _ssm_bwd(
        dy.reshape(n_seq, seq, SSM_W), u3, s_re, s_im, bb_re_m, bb_im_m, c_re_m, c_imn_m, d_row, ab_re, ab_im,
        dw_out.reshape(N_DEV, OUT_ROWS_PER_DEV, D_MODEL), dw_glu.reshape(N_DEV, GLU_ROWS_PER_DEV, SSM_W), n_seq, seq)
    du = du3.reshape(n, SSM_W)
    grad_x2, dproj, dg8 = _in_bwd(x2, dh2, du, dzs, dyc, proj, dbc, dzc, norm_gain, conv8, w_in_f, seq)
    pack, gc, gb = _ssm_disc_bwd_pack(
        a_re_x, a_im_x, log_dt_x, b_re2, b_im2, dab_re, dab_im,
        dbb_re_d, dbb_im_d, loss_t, dg8, dgf, dbg, dd, dcw, dc_re_d, dc_im_d)

    own_in, rchip_in, r_pack, r_gc, r_gb = _dw_in_exchange(
        [chip_ids[3], chip_ids[2], chip_ids[1], chip_ids[0]],
        xn, dproj, [pack, gc, gb])

    flat2 = lambda a: a.reshape(a.shape[-2:]) if a.ndim > 2 else a.reshape(1, -1)
    c2 = lambda a: a.reshape(N_GROUPS * GROUP, STATE)
    wmv = dict(norm_gain=(norm_gain, m_norm_gain, v_norm_gain),
               final_norm_gain=tuple(flat2(a) for a in (final_norm_gain, m_final_norm_gain, v_final_norm_gain)),
               b_glu=(b_glu, m_b_glu, v_b_glu),
               ssm_a_re=tuple(flat2(a) for a in (ssm_a_re, m_ssm_a_re, v_ssm_a_re)),
               ssm_a_im=tuple(flat2(a) for a in (ssm_a_im, m_ssm_a_im, v_ssm_a_im)),
               ssm_log_dt=(ssm_log_dt, m_ssm_log_dt, v_ssm_log_dt),
               ssm_d=tuple(jnp.transpose(a, (0, 2, 1)).reshape(GROUP, N_GROUPS) for a in (ssm_d, m_ssm_d, v_ssm_d)),
               conv_w=tuple(jnp.transpose(a, (1, 0, 2)) for a in (conv_w, m_conv_w, v_conv_w)),
               ssm_c_re=tuple(c2(a) for a in (ssm_c_re, m_ssm_c_re, v_ssm_c_re)),
               ssm_c_im=tuple(c2(a) for a in (ssm_c_im, m_ssm_c_im, v_ssm_c_im)),
               ssm_b_re=(b_re2, gh_p(m_ssm_b_re), gh_p(v_ssm_b_re)),
               ssm_b_im=(b_im2, gh_p(m_ssm_b_im), gh_p(v_ssm_b_im)))

    res_in = _reduce_adam_w_in(own_in, rchip_in, w_in[0], m_w_in[0], v_w_in[0])
    loss11, small, (res_out, res_glu) = _reduce_adam_small(
        r_pack, r_gc, r_gb, wmv,
        [(r_out, w_out[0], m_w_out[0], v_w_out[0]), (r_glu, w_glu[0], m_w_glu[0], v_w_glu[0])])
    loss = loss11.reshape(())

    shapes = dict(norm_gain=(1, D_MODEL), ssm_a_re=(1, N_GROUPS, STATE), ssm_a_im=(1, N_GROUPS, STATE),
                  ssm_log_dt=(1, N_GROUPS), ssm_c_re=(1, N_GROUPS, GROUP, STATE), ssm_c_im=(1, N_GROUPS, GROUP, STATE),
                  b_glu=(1, SSM_W), final_norm_gain=(D_MODEL,))
    big = dict(w_in=res_in, w_glu=res_glu, w_out=res_out)

    def leaf(kind, name):
        if name in big:
            return big[name][kind][None]
        if name in ("ssm_b_re", "ssm_b_im"):
            return jnp.transpose(small[name][kind].reshape(1, N_GROUPS, GROUP, STATE), (0, 1, 3, 2))
        if name == "ssm_d":
            return jnp.transpose(small[name][kind].reshape(1, GROUP, N_GROUPS), (0, 2, 1))
        if name == "conv_w":
            return jnp.transpose(small[name][kind], (1, 0, 2))
        return small[name][kind].reshape(shapes[name])

    order = ["norm_gain", "w_in", "ssm_a_re", "ssm_a_im", "ssm_log_dt", "ssm_b_re", "ssm_b_im", "ssm_c_re",
             "ssm_c_im", "ssm_d", "w_glu", "b_glu", "conv_w", "w_out", "final_norm_gain"]
    outs = [loss, grad_x2.reshape(x.shape)]
    for kind in range(4):
        outs += [leaf(kind, name) for name in order]
    return tuple(outs)
```

```python
import functools
import math

import jax
import jax.numpy as jnp
from jax import lax
from jax.experimental import pallas as pl
from jax.experimental.pallas import tpu as pltpu

F32 = jnp.float32
BF16 = jnp.bfloat16

N_DEV = 8
D_MODEL = 1024
SSM_W = 512
CONV_W = 512
N_GROUPS = 32
GROUP = 16
STATE = 64
IN_COLS = 3072
SEG_U, SEG_ZS, SEG_H, SEG_BC, SEG_CC, SEG_ZC = range(6)
COLS_PER_DEV = IN_COLS // N_DEV
N_CHIP = N_DEV // 2
COLS_PER_CHIP = 2 * COLS_PER_DEV
OUT_ROWS_PER_DEV = D_MODEL // N_DEV
GLU_ROWS_PER_DEV = SSM_W // N_DEV
CONV_COLS_PER_DEV = CONV_W // N_DEV
EPS = 1e-6

N_JBLK = 4
JB_CH = SSM_W // N_JBLK
JB_ST = N_GROUPS * STATE // N_JBLK

ADAM_LR = 0.001
ADAM_B1 = 0.9
ADAM_B2 = 0.999
ADAM_EPS = 1e-08
ADAM_WD = 0.01
ADAM_STEP = 10

SUBLANES = 8
LANES = 128
VMEM_LIMIT = 48 * 1024 * 1024
TOK_TILE = 256
IN_TILE = 1024
SCAN_TILE = 1024

MESH = pl.DeviceIdType.MESH
HBM_SPEC = pl.BlockSpec(memory_space=pltpu.HBM)


def _build(body, **kw):
    return pl.pallas_call(body, **kw)


def _pcall(body, **kw):
    def call(*operands):
        pinned = [a if jnp.issubdtype(a.dtype, jnp.integer) else pltpu.with_memory_space_constraint(a, pltpu.HBM)
                  for a in operands]
        return _build(body, **kw)(*pinned)
    return call


def _whole_specs(arrays):
    return [pl.BlockSpec(a.shape, functools.partial(lambda nd, i: (0,) * nd, len(a.shape))) for a in arrays]


def _out(shape, dtype):
    return pltpu.HBM(tuple(shape), dtype)


def _params(n_grid):
    return pltpu.CompilerParams(dimension_semantics=("arbitrary",) * n_grid,
                                vmem_limit_bytes=VMEM_LIMIT)


def _dot(a, b):
    return jnp.dot(a, b, preferred_element_type=F32)


def _dot_nt(a, b):
    return lax.dot_general(a, b, (((1,), (1,)), ((), ())), preferred_element_type=F32)


def _dot_tn(a, b):
    return lax.dot_general(a, b, (((0,), (0,)), ((), ())), preferred_element_type=F32)


def _sigmoid(z):
    return 1.0 / (1.0 + jnp.exp(-z))


_GELU_C = math.sqrt(2.0 / math.pi)


def _gelu_and_grad(y):
    inner = _GELU_C * (y + 0.044715 * (y * y * y))
    t = jnp.tanh(inner)
    g = 0.5 * y * (1.0 + t)
    dg = 0.5 * (1.0 + t) + 0.5 * y * (1.0 - t * t) * (_GELU_C * (1.0 + 3.0 * 0.044715 * (y * y)))
    return g, dg


def _silu_and_grad(z):
    s = _sigmoid(z)
    return z * s, s * (1.0 + z * (1.0 - s))


def _shift_down(v, halo, k):
    rolled = pltpu.roll(v, k, 0)
    row = lax.broadcasted_iota(jnp.int32, v.shape, 0)
    for r in range(k):
        rolled = jnp.where(row == r, halo[SUBLANES - k + r:SUBLANES - k + r + 1, :], rolled)
    return rolled


def _shift_up(v, halo, k):
    n = v.shape[0]
    rolled = pltpu.roll(v, n - k, 0)
    row = lax.broadcasted_iota(jnp.int32, v.shape, 0)
    for r in range(k):
        rolled = jnp.where(row == n - k + r, halo[r:r + 1, :], rolled)
    return rolled


def _mesh_pos():
    return lax.axis_index("x"), lax.axis_index("y"), lax.axis_index("c")


def _direct_copies(srcs_for, out_refs, send_sems, recv_sems, loc_sems):
    x, y, c = _mesh_pos()
    me_id = 4 * x + 2 * y + c
    n_arr = len(out_refs)
    dsts = [r.at[me_id] for r in out_refs]
    own = srcs_for(me_id)
    mine = [pltpu.make_async_copy(own[a], dsts[a], loc_sems.at[a]) for a in range(n_arr)]
    sends = []
    for k in range(1, N_DEV):
        px, py, pc = x ^ ((k >> 2) & 1), y ^ ((k >> 1) & 1), c ^ (k & 1)
        src = srcs_for(4 * px + 2 * py + pc)
        for a in range(n_arr):
            sends.append(pltpu.make_async_remote_copy(
                src_ref=src[a], dst_ref=dsts[a],
                send_sem=send_sems.at[(k - 1) * n_arr + a], recv_sem=recv_sems.at[(k - 1) * n_arr + a],
                device_id=(px, py, pc), device_id_type=MESH))
    return mine, sends


class _TwoLevelGather:
    def __init__(self, srcs, slots, send_sems, recv_sems, loc_sems):
        self.srcs, self.slots, self.n_arr = srcs, slots, len(srcs)
        self.send_sems, self.recv_sems, self.loc_sems = send_sems, recv_sems, loc_sems
        x, y, c = _mesh_pos()
        self.c = c
        self.me, self.sib = (x, y, c), (x, y, 1 - c)
        self.chips = [(1 - x, y), (x, 1 - y), (1 - x, 1 - y)]

    def _copies(self, k, block, to, from_src=False):
        dev = 4 * block[0] + 2 * block[1] + block[2]
        return [pltpu.make_async_remote_copy(
            src_ref=self.srcs[a] if from_src else self.slots[a](dev), dst_ref=self.slots[a](dev),
            send_sem=self.send_sems.at[k * self.n_arr + a], recv_sem=self.recv_sems.at[k * self.n_arr + a],
            device_id=to, device_id_type=MESH) for a in range(self.n_arr)]

    def _local(self):
        dev = 4 * self.me[0] + 2 * self.me[1] + self.me[2]
        return [pltpu.make_async_copy(self.srcs[a], self.slots[a](dev), self.loc_sems.at[a])
                for a in range(self.n_arr)]

    def start(self):
        for cp in self._local() + self._copies(0, self.me, self.sib, True):
            cp.start()
        for j in (0, 1):
            for cp in self._copies(1 + j, self.me, (*self.chips[j], self.c), True):
                cp.start()

    def wait_own(self):
        for cp in self._local():
            cp.wait()

    def wait_sibling(self):
        for cp in self._copies(0, self.sib, self.me):
            cp.wait_recv()

    def wait_and_pass_on(self, j):
        chip = self.chips[j]
        for cp in self._copies(1 + j, (*chip, self.c), self.me):
            cp.wait_recv()
        for cp in self._copies(4 + j, (*chip, self.c), self.sib):
            cp.start()

    def neighbours_landed(self):
        x, y, c = self.me
        self.wait_and_pass_on(0)
        self.wait_and_pass_on(1)
        for cp in self._copies(1 + 2, (x ^ c, y ^ (1 - c), c), (x ^ (1 - c), y ^ c, c)):
            cp.start()

    def diagonal_landed(self):
        self.wait_and_pass_on(2)

    def wait_passed_on(self, j):
        for cp in self._copies(4 + j, (*self.chips[j], 1 - self.c), self.me):
            cp.wait_recv()

    def wait_sends(self):
        for cp in self._copies(0, self.me, self.sib, True):
            cp.wait_send()
        for j, chip in enumerate(self.chips):
            for cp in self._copies(1 + j, self.me, (*chip, self.c), True) + self._copies(4 + j, (*chip, self.c), self.sib):
                cp.wait_send()

    def finish(self):
        self.wait_sibling()
        for j in range(3):
            self.wait_passed_on(j)
        self.wait_sends()
        self.wait_own()


def _disc(a_re, a_im, log_dt, b_re, b_im):
    dt = jnp.exp(log_dt)
    mag = jnp.exp(a_re * dt)
    ab_re = mag * jnp.cos(a_im * dt)
    ab_im = mag * jnp.sin(a_im * dt)
    den = a_re * a_re + a_im * a_im
    p_re = ab_re - 1.0
    p_im = ab_im
    q_re = (p_re * a_re + p_im * a_im) / den
    q_im = (p_im * a_re - p_re * a_im) / den
    bb_re = q_re * b_re - q_im * b_im
    bb_im = q_re * b_im + q_im * b_re
    return ab_re, ab_im, bb_re, bb_im


def _split3(v):
    hi = v.astype(BF16)
    r1 = v - hi.astype(F32)
    mid = r1.astype(BF16)
    lo = (r1 - mid.astype(F32)).astype(BF16)
    return hi, mid, lo


def _select_dot(sel, v):
    return sum(_dot(sel, t) for t in _split3(v))


PACK_ROWS = 72
PACK_W = 512
ROW_FINAL_GAIN, ROW_NORM_GAIN, ROW_BGLU_D, ROW_CONV, ROW_LOSS, ROW_S5 = 0, 8, 16, 24, 32, 40
LANE_A_RE, LANE_A_IM, LANE_LOG_DT = 0, 128, 256


def _ssm_disc_bwd_pack(a_re_x, a_im_x, log_dt_x, b_re, b_im, g_ab_re, g_ab_im, dbb_re_d, dbb_im_d,
                       loss_t, dg8, dgf, dbg, dd, dcw, dc_re_d, dc_im_d):
    rows_gh = N_GROUPS * GROUP

    def body(are, aim, ldt, bre, bim, gabre, gabim, dbbre_ref, dbbim_ref,
             loss_ref, dg8_ref, dgf_ref, dbg_ref, dd_ref, dcw_ref, dcre_ref, dcim_ref,
             p_ref, gc_ref, gb_ref, gbb_re, gbb_im):
        r_g = lax.broadcasted_iota(jnp.int32, (N_GROUPS, rows_gh), 0)
        c_gh = lax.broadcasted_iota(jnp.int32, (N_GROUPS, rows_gh), 1)
        group_sum = (c_gh // GROUP == r_g).astype(BF16)
        r_gh = lax.broadcasted_iota(jnp.int32, (rows_gh, N_GROUPS), 0)
        c_g = lax.broadcasted_iota(jnp.int32, (rows_gh, N_GROUPS), 1)
        first_row = (r_gh == c_g * GROUP).astype(BF16)

        def diag_block(ref, j, gi):
            return ref[j, gi * GROUP:(gi + 1) * GROUP, gi * STATE:(gi + 1) * STATE]

        for j in range(N_JBLK):
            for gi in range(SUBLANES):
                r0 = (j * SUBLANES + gi) * GROUP
                gbb_re[r0:r0 + GROUP, :] = diag_block(dbbre_ref, j, gi)
                gbb_im[r0:r0 + GROUP, :] = diag_block(dbbim_ref, j, gi)
                both = jnp.concatenate([diag_block(dcre_ref, j, gi), -diag_block(dcim_ref, j, gi)], axis=1)
                gc_ref[r0:r0 + GROUP, :] = both.astype(BF16)

        def by_group(ref):
            return jnp.concatenate([ref[:, g * STATE:(g + 1) * STATE] for g in range(N_GROUPS)], axis=0)

        _, vjp = jax.vjp(_disc, are[...], aim[...], ldt[...], bre[...], bim[...])
        d_are, d_aim, d_ldt, d_bre, d_bim = vjp((_select_dot(first_row, by_group(gabre)),
                                                 _select_dot(first_row, by_group(gabim)),
                                                 gbb_re[...], gbb_im[...]))
        gb_ref[...] = jnp.concatenate([d_bre, d_bim], axis=1).astype(BF16)

        p_ref[...] = jnp.zeros_like(p_ref)
        half = D_MODEL // 2
        for r, src in ((ROW_FINAL_GAIN, dgf_ref), (ROW_NORM_GAIN, dg8_ref)):
            p_ref[r:r + 1, :] = src[0:1, 0:half]
            p_ref[r + 1:r + 2, :] = src[0:1, half:D_MODEL]
        p_ref[ROW_BGLU_D:ROW_BGLU_D + 1, :] = dbg_ref[...]
        p_ref[ROW_BGLU_D + 1:ROW_BGLU_D + 2, :] = dd_ref[...]
        p_ref[ROW_CONV:ROW_CONV + SUBLANES, :] = dcw_ref[...]
        p_ref[ROW_LOSS:ROW_LOSS + SUBLANES, 0:LANES] = loss_ref[...]
        s5 = slice(ROW_S5, ROW_S5 + N_GROUPS)
        p_ref[s5, LANE_A_RE:LANE_A_RE + STATE] = _select_dot(group_sum, d_are)
        p_ref[s5, LANE_A_IM:LANE_A_IM + STATE] = _select_dot(group_sum, d_aim)
        p_ref[s5, LANE_LOG_DT:LANE_LOG_DT + LANES] = _select_dot(group_sum, jnp.broadcast_to(d_ldt, (rows_gh, LANES)))

    operands = (a_re_x, a_im_x, log_dt_x, b_re, b_im, g_ab_re, g_ab_im, dbb_re_d, dbb_im_d,
                loss_t, dg8, dgf, dbg, dd, dcw, dc_re_d, dc_im_d)
    out_shape = (_out((PACK_ROWS, PACK_W), F32),
                 _out((rows_gh, 2 * STATE), BF16),
                 _out((rows_gh, 2 * STATE), BF16))
    return _pcall(body, name="ssm_disc_bwd_pack", grid=(1,), out_shape=out_shape,
                  in_specs=_whole_specs(operands), out_specs=tuple(_whole_specs(out_shape)),
                  scratch_shapes=[pltpu.VMEM((rows_gh, STATE), F32), pltpu.VMEM((rows_gh, STATE), F32)],
                  compiler_params=_params(1))(*operands)


def _s5_prepare(are, aim, ldt, bre, bim, cre, cim,
                o_ax_re, o_ax_im, o_ldt_x, o_ab_re, o_ab_im, o_bb_re, o_bb_im, o_c_re, o_c_imn):
    rows_gh = N_GROUPS * GROUP
    rep = (lax.broadcasted_iota(jnp.int32, (rows_gh, N_GROUPS), 0) // GROUP
           == lax.broadcasted_iota(jnp.int32, (rows_gh, N_GROUPS), 1)).astype(BF16)
    eye = (lax.broadcasted_iota(jnp.int32, (N_GROUPS, N_GROUPS), 0)
           == lax.broadcasted_iota(jnp.int32, (N_GROUPS, N_GROUPS), 1)).astype(F32)
    ldt_col = jnp.sum(eye * ldt[...], axis=1, keepdims=True)
    a_re_x = _select_dot(rep, are[...])
    a_im_x = _select_dot(rep, aim[...])
    ldt_x = _select_dot(rep, jnp.broadcast_to(ldt_col, (N_GROUPS, LANES)))[:, 0:1]
    o_ax_re[...] = a_re_x
    o_ax_im[...] = a_im_x
    o_ldt_x[...] = ldt_x
    ab_re, ab_im, bb_re, bb_im = _disc(a_re_x, a_im_x, ldt_x, bre[...], bim[...])
    for j in range(N_JBLK):
        first = [(j * SUBLANES + gi) * GROUP for gi in range(SUBLANES)]
        o_ab_re[j] = jnp.concatenate([ab_re[r:r + 1, :] for r in first], axis=1)
        o_ab_im[j] = jnp.concatenate([ab_im[r:r + 1, :] for r in first], axis=1)
    for o, v in ((o_bb_re, bb_re), (o_bb_im, bb_im), (o_c_re, cre[...]), (o_c_imn, -cim[...])):
        for j in range(N_JBLK):
            for gi in range(SUBLANES):
                r0 = (j * SUBLANES + gi) * GROUP
                parts = [v[r0:r0 + GROUP, :] if k == gi else jnp.zeros((GROUP, STATE), F32) for k in range(SUBLANES)]
                o[j, gi * GROUP:(gi + 1) * GROUP, :] = jnp.concatenate(parts, axis=1).astype(BF16)


def _in_proj(order, x2, g1, w_in_own, s5):
    n = x2.shape[0]
    tm = min(IN_TILE, n)
    n_tiles = n // tm
    n_s5_in = len(s5)
    n_s5_out = 9

    def body(order_ref, x_ref, g_ref, w_ref, *refs):
        s5_in = refs[:n_s5_in]
        xn_ref, proj_ref, wall_ref = refs[n_s5_in:n_s5_in + 3]
        s5_out = refs[n_s5_in + 3:n_s5_in + 3 + n_s5_out]
        xn_scr, wbuf, wown, send_sems, recv_sems, loc_sems, out_sems = refs[n_s5_in + 3 + n_s5_out:]
        k = pl.program_id(0)
        i = pl.program_id(1)

        def slot(dev):
            return wbuf.at[dev // 2, :, pl.ds(pl.multiple_of((dev % 2) * COLS_PER_DEV, LANES), COLS_PER_DEV)]

        gather = _TwoLevelGather([wown], [slot], send_sems, recv_sems, loc_sems)

        @pl.when((k == 0) & (i == 0))
        def _():
            wown[...] = w_ref[...].astype(BF16)
            gather.start()

        def own_chip():
            gather.wait_own()
            gather.wait_sibling()

        def x_chip():
            gather.neighbours_landed()
            gather.wait_passed_on(0)

        def diag_chip():
            gather.diagonal_landed()
            gather.wait_passed_on(2)

        rows = pl.ds(pl.multiple_of(i * tm, tm), tm)

        @pl.when(k == 0)
        def _():
            x = x_ref[...]
            r = lax.rsqrt(jnp.mean(x * x, axis=-1, keepdims=True) + EPS)
            xn = ((x * r) * g_ref[...]).astype(BF16)
            xn_scr[rows, :] = xn
            xn_ref[...] = xn

        def keep_copy(kk):
            q = order_ref[kk]
            cols = pl.ds(pl.multiple_of(q * COLS_PER_CHIP, LANES), COLS_PER_CHIP)
            return pltpu.make_async_copy(wbuf.at[q], wall_ref.at[:, cols], out_sems.at[kk])

        arrivals = [own_chip, x_chip, functools.partial(gather.wait_passed_on, 1), diag_chip]
        for kk, arrived in enumerate(arrivals):
            @pl.when((k == kk) & (i == 0))
            def _(kk=kk, arrived=arrived):
                arrived()
                keep_copy(kk).start()

        proj_ref[...] = _dot(xn_scr[rows, :], wbuf[order_ref[k]])

        @pl.when((k == 0) & (i == n_tiles - 1))
        def _():
            _s5_prepare(*s5_in, *s5_out)

        @pl.when((k == N_CHIP - 1) & (i == n_tiles - 1))
        def _():
            gather.wait_sends()
            for kk in range(N_CHIP):
                keep_copy(kk).wait()

    tile_once = lambda k, i, order: (jnp.where(k == 0, i, n_tiles - 1), 0)
    whole = lambda shape: pl.BlockSpec(shape, lambda k, i, order: (0,) * len(shape))
    rows_gh = N_GROUPS * GROUP
    s5_out_shapes = ([(rows_gh, STATE), F32], [(rows_gh, STATE), F32], [(rows_gh, 1), F32],
                     [(N_JBLK, 1, JB_ST), F32], [(N_JBLK, 1, JB_ST), F32]) + ([(N_JBLK, JB_CH, JB_ST), BF16],) * 4
    grid_spec = pltpu.PrefetchScalarGridSpec(
        num_scalar_prefetch=1, grid=(N_CHIP, n_tiles),
        in_specs=[pl.BlockSpec((tm, D_MODEL), tile_once),
                  whole((1, D_MODEL)),
                  whole(w_in_own.shape),
                  *(whole(a.shape) for a in s5)],
        out_specs=(pl.BlockSpec((tm, D_MODEL), tile_once),
                   pl.BlockSpec((tm, COLS_PER_CHIP), lambda k, i, order: (i, order[k])),
                   HBM_SPEC,
                   *(whole(shape) for shape, _ in s5_out_shapes)),
        scratch_shapes=[pltpu.VMEM((n, D_MODEL), BF16), pltpu.VMEM((N_CHIP, D_MODEL, COLS_PER_CHIP), BF16),
                        pltpu.VMEM(w_in_own.shape, BF16),
                        pltpu.SemaphoreType.DMA((7,)), pltpu.SemaphoreType.DMA((7,)), pltpu.SemaphoreType.DMA((1,)),
                        pltpu.SemaphoreType.DMA((N_CHIP,))])
    outs = _pcall(
        body, name="in_proj", grid_spec=grid_spec,
        out_shape=(_out((n, D_MODEL), BF16), _out((n, IN_COLS), F32),
                   _out((D_MODEL, IN_COLS), BF16),
                   *(_out(shape, dt) for shape, dt in s5_out_shapes)),
        compiler_params=_params(2),
    )(order, x2, g1, w_in_own, *s5)
    return outs[0], outs[1], outs[2], outs[3:]


def _cmul(p, q):
    return p[0] * q[0] - p[1] * q[1], p[0] * q[1] + p[1] * q[0]


def _scan_tables(ar, ai, width, reverse):
    pows = [(ar, ai)]
    for _ in range(SUBLANES - 1):
        pows.append(_cmul(pows[-1], (ar, ai)))
    row = lax.broadcasted_iota(jnp.int32, (SUBLANES, width), 0)

    def bc(v):
        return jnp.broadcast_to(v, (SUBLANES, width))

    levels = []
    for k in (1, 2, 4):
        keep = (row <= SUBLANES - 1 - k) if reverse else (row >= k)
        levels.append((jnp.where(keep, bc(pows[k - 1][0]), 0.0), jnp.where(keep, bc(pows[k - 1][1]), 0.0)))
    cre = jnp.zeros((SUBLANES, width), F32)
    cim = jnp.zeros((SUBLANES, width), F32)
    for r in range(SUBLANES):
        e = (SUBLANES - r) if reverse else (r + 1)
        cre = jnp.where(row == r, bc(pows[e - 1][0]), cre)
        cim = jnp.where(row == r, bc(pows[e - 1][1]), cim)
    return levels, (cre, cim)


def _load_chunked(src_ref, b, dst_ref, n_rows):
    n_blk = n_rows // SUBLANES
    for i in range(n_blk):
        dst_ref[b, i * SUBLANES:(i + 1) * SUBLANES, :] = src_ref[b, pl.ds(i, SUBLANES, stride=n_blk), :]


def _store_chunked(val, dst_ref, b, n_rows):
    n_blk = n_rows // SUBLANES
    for i in range(n_blk):
        dst_ref[b, pl.ds(i, SUBLANES, stride=n_blk), :] = val[i * SUBLANES:(i + 1) * SUBLANES, :]


def _chunk_scan(re_ref, im_ref, bs, car_ref, ar, ai, n_rows, reverse, on_block=None):
    width = re_ref.shape[2]
    n_blk = n_rows // SUBLANES
    shape = (SUBLANES, width)
    abr = jnp.broadcast_to(ar, shape)
    abi = jnp.broadcast_to(ai, shape)
    order = list(range(n_blk - 1, -1, -1)) if reverse else list(range(n_blk))

    def blk(ref, b, i):
        return ref[b, i * SUBLANES:(i + 1) * SUBLANES, :]

    def step(state, b, i):
        sr, si = state
        return abr * sr - abi * si + blk(re_ref, b, i), abr * si + abi * sr + blk(im_ref, b, i)

    finals = {b: (blk(re_ref, b, order[0]), blk(im_ref, b, order[0])) for b in bs}
    for i in order[1:]:
        for b in bs:
            finals[b] = step(finals[b], b, i)

    mr, mi = ar, ai
    for _ in range(n_blk.bit_length() - 1):
        mr, mi = _cmul((mr, mi), (mr, mi))
    levels, _ = _scan_tables(mr, mi, width, reverse)
    mbr = jnp.broadcast_to(mr, shape)
    mbi = jnp.broadcast_to(mi, shape)
    row = lax.broadcasted_iota(jnp.int32, shape, 0)
    edge_in = SUBLANES - 1 if reverse else 0
    edge_out = 0 if reverse else SUBLANES - 1
    sh1 = SUBLANES - 1 if reverse else 1
    states = {}
    for b in bs:
        fr, fi = finals[b]
        gr = jnp.where(row == edge_in, jnp.broadcast_to(car_ref[b, 0:1, :], shape), pltpu.roll(fr, sh1, 0))
        gi = jnp.where(row == edge_in, jnp.broadcast_to(car_ref[b, 1:2, :], shape), pltpu.roll(fi, sh1, 0))
        for (lr, li), k in zip(levels, (1, 2, 4)):
            sh = (SUBLANES - k) if reverse else k
            sr = pltpu.roll(gr, sh, 0)
            si = pltpu.roll(gi, sh, 0)
            gr, gi = gr + (lr * sr - li * si), gi + (lr * si + li * sr)
        car_ref[b, 0:1, :] = (fr + (mbr * gr - mbi * gi))[edge_out:edge_out + 1, :]
        car_ref[b, 1:2, :] = (fi + (mbr * gi + mbi * gr))[edge_out:edge_out + 1, :]
        states[b] = (gr, gi)

    for i in order:
        for b in bs:
            states[b] = step(states[b], b, i)
            re_ref[b, i * SUBLANES:(i + 1) * SUBLANES, :] = states[b][0]
            im_ref[b, i * SUBLANES:(i + 1) * SUBLANES, :] = states[b][1]
            if on_block is not None:
                on_block(b, i, *states[b])


def _ssm_fwd(u, bb_re, bb_im, c_re_t, c_imn_t, d_row, ab_re, ab_im, w_out_own, w_glu_own, conv_p, n_seq, seq):
    tt = min(SCAN_TILE, seq)
    nt = seq // tt

    def body(u_ref, bbre, bbim, cre, cimn, d_ref, are, aim, wout_ref, wglu_ref, cw_ref,
             sre_ref, sim_ref, y_ref, oout_ref, oglu_ref, ocw_ref,
             up_ref, car_ref, woutb_ref, wglub_ref, send_sems, recv_sems, loc_sems):
        j = pl.program_id(0)
        t = pl.program_id(1)
        gather = _TwoLevelGather(
            [woutb_ref, wglub_ref, cw_ref],
            [lambda dev: oout_ref.at[pl.ds(pl.multiple_of(dev * OUT_ROWS_PER_DEV, OUT_ROWS_PER_DEV), OUT_ROWS_PER_DEV), :],
             lambda dev: oglu_ref.at[pl.ds(pl.multiple_of(dev * GLU_ROWS_PER_DEV, GLU_ROWS_PER_DEV), GLU_ROWS_PER_DEV), :],
             lambda dev: ocw_ref.at[dev]],
            send_sems, recv_sems, loc_sems)

        @pl.when((j == 0) & (t == 0))
        def _():
            woutb_ref[...] = wout_ref[...].astype(BF16)
            wglub_ref[...] = wglu_ref[...].astype(BF16)
            gather.start()

        @pl.when((j == N_JBLK // 2) & (t == 0))
        def _():
            gather.neighbours_landed()

        @pl.when((j == N_JBLK - 1) & (t == 0))
        def _():
            gather.diagonal_landed()

        @pl.when(t == 0)
        def _():
            car_ref[...] = jnp.zeros_like(car_ref)

        bs = list(range(n_seq))
        for b in bs:
            _load_chunked(u_ref, b, up_ref, tt)
        for b in bs:
            ub = up_ref[b].astype(BF16)
            sre_ref[b] = _dot(ub, bbre[0])
            sim_ref[b] = _dot(ub, bbim[0])
            _chunk_scan(sre_ref, sim_ref, [b], car_ref, are[0], aim[0], tt, reverse=False)
        for b in bs:
            yp = (_dot_nt(sre_ref[b].astype(BF16), cre[0]) + _dot_nt(sim_ref[b].astype(BF16), cimn[0])
                  + d_ref[...] * up_ref[b])
            _store_chunked(yp, y_ref, b, tt)

        @pl.when((j == N_JBLK - 1) & (t == nt - 1))
        def _():
            gather.finish()

    tok = lambda j, t: (0, t, j)
    blk3 = lambda j, t: (j, 0, 0)
    row = lambda j, t: (0, j)
    whole = lambda j, t: (0, 0)
    st = _out((n_seq, seq, N_JBLK * JB_ST), F32)
    n_arr = 3
    return _pcall(
        body, name="ssm_fwd", grid=(N_JBLK, nt),
        out_shape=(st, st, _out((n_seq, seq, SSM_W), F32),
                   _out((D_MODEL, D_MODEL), BF16), _out((SSM_W, SSM_W), BF16),
                   _out((N_DEV, SUBLANES, LANES), F32)),
        in_specs=[pl.BlockSpec((n_seq, tt, JB_CH), tok),
                  pl.BlockSpec((1, JB_CH, JB_ST), blk3), pl.BlockSpec((1, JB_CH, JB_ST), blk3),
                  pl.BlockSpec((1, JB_CH, JB_ST), blk3), pl.BlockSpec((1, JB_CH, JB_ST), blk3),
                  pl.BlockSpec((1, JB_CH), row), pl.BlockSpec((1, 1, JB_ST), blk3), pl.BlockSpec((1, 1, JB_ST), blk3),
                  pl.BlockSpec(w_out_own.shape, whole), pl.BlockSpec(w_glu_own.shape, whole), HBM_SPEC],
        out_specs=(pl.BlockSpec((n_seq, tt, JB_ST), tok), pl.BlockSpec((n_seq, tt, JB_ST), tok),
                   pl.BlockSpec((n_seq, tt, JB_CH), tok), HBM_SPEC, HBM_SPEC, HBM_SPEC),
        scratch_shapes=[pltpu.VMEM((n_seq, tt, JB_CH), F32), pltpu.VMEM((n_seq, SUBLANES, JB_ST), F32),
                        pltpu.VMEM(w_out_own.shape, BF16), pltpu.VMEM(w_glu_own.shape, BF16),
                        pltpu.SemaphoreType.DMA((7 * n_arr,)), pltpu.SemaphoreType.DMA((7 * n_arr,)),
                        pltpu.SemaphoreType.DMA((n_arr,))],
        compiler_params=_params(2),
    )(u, bb_re, bb_im, c_re_t, c_imn_t, d_row, ab_re, ab_im, w_out_own, w_glu_own, conv_p)


def _ssm_bwd(dy, u, s_re, s_im, bb_re, bb_im, c_re_t, c_imn_t, d_row, ab_re, ab_im, g_out, g_glu, n_seq, seq):
    tt = min(SCAN_TILE, seq)
    nt = seq // tt
    rows8 = tt // SUBLANES

    def body(dy_ref, u_ref, sre_ref, sim_ref, pre_ref, pim_ref, bbre, bbim, cre, cimn, d_ref, are, aim,
             gout_ref, gglu_ref,
             du_ref, dcre_ref, dcim_ref, dbbre_ref, dbbim_ref, dare_ref, daim_ref, dd_ref, rout_ref, rglu_ref,
             lre_ref, lim_ref, dyp_ref, up_ref, car_ref, send_sems, recv_sems, loc_sems):
        j = pl.program_id(0)
        tr = pl.program_id(1)

        def exchange():
            return _direct_copies(lambda pid: [gout_ref.at[pid], gglu_ref.at[pid]], [rout_ref, rglu_ref],
                                  send_sems, recv_sems, loc_sems)

        @pl.when((j == 0) & (tr == 0))
        def _():
            mine, sends = exchange()
            for cp in mine + sends:
                cp.start()

        @pl.when(tr == 0)
        def _():
            car_ref[...] = jnp.zeros_like(car_ref)
            for r in (dcre_ref, dcim_ref, dbbre_ref, dbbim_ref, dare_ref, daim_ref, dd_ref):
                r[...] = jnp.zeros_like(r)

        first = tr == nt - 1
        row = lax.broadcasted_iota(jnp.int32, (SUBLANES, JB_ST), 0)
        n_blk = tt // SUBLANES
        bs = list(range(n_seq))
        for b in bs:
            _load_chunked(dy_ref, b, dyp_ref, tt)
            _load_chunked(u_ref, b, up_ref, tt)
        for b in bs:
            dyb = dyp_ref[b].astype(BF16)
            lre_ref[b] = _dot(dyb, cre[0])
            lim_ref[b] = _dot(dyb, cimn[0])
        acc = {b: [jnp.zeros((SUBLANES, JB_ST), F32), jnp.zeros((SUBLANES, JB_ST), F32)] for b in bs}

        def on_block(b, i, lr, li):
            if i > 0:
                spr = sre_ref[b, (i - 1) * SUBLANES:i * SUBLANES, :]
                spi = sim_ref[b, (i - 1) * SUBLANES:i * SUBLANES, :]
            else:
                hr = jnp.where(first, 0.0, pre_ref[b, SUBLANES - 1:SUBLANES, :])
                hi = jnp.where(first, 0.0, pim_ref[b, SUBLANES - 1:SUBLANES, :])
                last_r = sre_ref[b, (n_blk - 1) * SUBLANES:n_blk * SUBLANES, :]
                last_i = sim_ref[b, (n_blk - 1) * SUBLANES:n_blk * SUBLANES, :]
                spr = jnp.where(row == 0, jnp.broadcast_to(hr, row.shape), pltpu.roll(last_r, 1, 0))
                spi = jnp.where(row == 0, jnp.broadcast_to(hi, row.shape), pltpu.roll(last_i, 1, 0))
            acc[b][0] = acc[b][0] + (lr * spr + li * spi)
            acc[b][1] = acc[b][1] + (li * spr - lr * spi)

        _chunk_scan(lre_ref, lim_ref, bs, car_ref, are[0], -aim[0], tt, reverse=True, on_block=on_block)
        for b in bs:
            dare_ref[...] += jnp.sum(acc[b][0], axis=0, keepdims=True)
            daim_ref[...] += jnp.sum(acc[b][1], axis=0, keepdims=True)
            dyp = dyp_ref[b]
            up = up_ref[b]
            dyb = dyp.astype(BF16)
            ub = up.astype(BF16)
            lrb = lre_ref[b].astype(BF16)
            lib = lim_ref[b].astype(BF16)
            dup = d_ref[...] * dyp + _dot_nt(lrb, bbre[0]) + _dot_nt(lib, bbim[0])
            _store_chunked(dup, du_ref, b, tt)
            dbbre_ref[0] += _dot_tn(ub, lrb)
            dbbim_ref[0] += _dot_tn(ub, lib)
            dcre_ref[0] += _dot_tn(dyb, sre_ref[b].astype(BF16))
            dcim_ref[0] += _dot_tn(dyb, sim_ref[b].astype(BF16))
            dd_ref[...] += jnp.sum(dyp * up, axis=0, keepdims=True)

        @pl.when((j == N_JBLK - 1) & (tr == nt - 1))
        def _():
            mine, sends = exchange()
            for cp in sends + mine:
                cp.wait()

    tok = lambda j, t: (0, nt - 1 - t, j)
    halo = lambda j, t: (0, jnp.maximum((nt - 1 - t) * rows8 - 1, 0), j)
    blk3 = lambda j, t: (j, 0, 0)
    row1 = lambda j, t: (0, j)
    acc_shape = _out((N_JBLK, JB_CH, JB_ST), F32)
    return _pcall(
        body, name="ssm_bwd", grid=(N_JBLK, nt),
        out_shape=(_out((n_seq, seq, SSM_W), F32), acc_shape, acc_shape, acc_shape, acc_shape,
                   _out((1, N_JBLK * JB_ST), F32), _out((1, N_JBLK * JB_ST), F32),
                   _out((1, SSM_W), F32),
                   _out((N_DEV,) + g_out.shape[1:], F32),
                   _out((N_DEV,) + g_glu.shape[1:], F32)),
        in_specs=[pl.BlockSpec((n_seq, tt, JB_CH), tok), pl.BlockSpec((n_seq, tt, JB_CH), tok),
                  pl.BlockSpec((n_seq, tt, JB_ST), tok), pl.BlockSpec((n_seq, tt, JB_ST), tok),
                  pl.BlockSpec((n_seq, SUBLANES, JB_ST), halo), pl.BlockSpec((n_seq, SUBLANES, JB_ST), halo),
                  pl.BlockSpec((1, JB_CH, JB_ST), blk3), pl.BlockSpec((1, JB_CH, JB_ST), blk3),
                  pl.BlockSpec((1, JB_CH, JB_ST), blk3), pl.BlockSpec((1, JB_CH, JB_ST), blk3),
                  pl.BlockSpec((1, JB_CH), row1), pl.BlockSpec((1, 1, JB_ST), blk3), pl.BlockSpec((1, 1, JB_ST), blk3),
                  HBM_SPEC, HBM_SPEC],
        out_specs=(pl.BlockSpec((n_seq, tt, JB_CH), tok),
                   pl.BlockSpec((1, JB_CH, JB_ST), blk3), pl.BlockSpec((1, JB_CH, JB_ST), blk3),
                   pl.BlockSpec((1, JB_CH, JB_ST), blk3), pl.BlockSpec((1, JB_CH, JB_ST), blk3),
                   pl.BlockSpec((1, JB_ST), row1), pl.BlockSpec((1, JB_ST), row1), pl.BlockSpec((1, JB_CH), row1),
                   HBM_SPEC, HBM_SPEC),
        scratch_shapes=[pltpu.VMEM((n_seq, tt, JB_ST), F32), pltpu.VMEM((n_seq, tt, JB_ST), F32),
                        pltpu.VMEM((n_seq, tt, JB_CH), F32), pltpu.VMEM((n_seq, tt, JB_CH), F32),
                        pltpu.VMEM((n_seq, SUBLANES, JB_ST), F32),
                        pltpu.SemaphoreType.DMA((7 * 2,)), pltpu.SemaphoreType.DMA((7 * 2,)),
                        pltpu.SemaphoreType.DMA((2,))],
        compiler_params=_params(2),
    )(dy, u, s_re, s_im, s_re, s_im, bb_re, bb_im, c_re_t, c_imn_t, d_row, ab_re, ab_im, g_out, g_glu)


def _mix(x2, tgt2, y, proj, gf, b_glu, conv8, w_glu_f, w_out_f, seq):
    n = x2.shape[0]
    tm = TOK_TILE
    tiles_per_seq = seq // tm
    rows8 = tm // SUBLANES

    def body(x_ref, t_ref, y_ref, zs_ref, h_ref, bc_ref, cc_ref, zc_ref, hp_ref, ccp_ref,
             gf_ref, bg_ref, cw_ref, wg_ref, wo_ref,
             dh2_ref, dy_ref, dzs_ref, dbc_ref, dzc_ref, dyc_ref,
             dwo_ref, dwg_ref, loss_ref, dgf_ref, dbg_ref, dcw_ref):
        i = pl.program_id(0)

        @pl.when(i == 0)
        def _():
            for r in (dwo_ref, dwg_ref, loss_ref, dgf_ref, dbg_ref, dcw_ref):
                r[...] = jnp.zeros_like(r)

        yv = y_ref[...]
        y1, dgelu = _gelu_and_grad(yv)
        y1b = y1.astype(BF16)
        gate = _sigmoid(_dot(y1b, wg_ref[...]) + bg_ref[...])
        y2 = y1 * gate
        szs, dszs = _silu_and_grad(zs_ref[...])
        yssm = y2 * szs
        hv = h_ref[...]
        ccv = cc_ref[...]
        bcv = bc_ref[...]
        v = ccv * hv
        first = (i % tiles_per_seq) == 0
        vhalo = jnp.where(first, 0.0, ccp_ref[...] * hp_ref[...])
        v1 = _shift_down(v, vhalo, 1)
        v2 = _shift_down(v, vhalo, 2)
        w0 = cw_ref[0:1, :]
        w1 = cw_ref[1:2, :]
        w2 = cw_ref[2:3, :]
        yc = w0 * v2 + w1 * v1 + w2 * v
        szc, dszc = _silu_and_grad(zc_ref[...])
        yconv = (bcv * yc) * szc
        ysb = yssm.astype(BF16)
        ycb = yconv.astype(BF16)
        h2 = x_ref[...] + _dot(ysb, wo_ref[0:SSM_W, :]) + _dot(ycb, wo_ref[SSM_W:, :])
        r2 = lax.rsqrt(jnp.mean(h2 * h2, axis=-1, keepdims=True) + EPS)
        hn = h2 * r2
        gfv = gf_ref[...]
        err = hn * gfv - t_ref[...]
        loss_ref[...] += 0.5 * jnp.sum(jnp.mean(err * err, axis=-1, keepdims=True))
        dout = err * (1.0 / D_MODEL)
        dgf_ref[...] += jnp.sum(dout * hn, axis=0, keepdims=True)
        dn = dout * gfv
        dh2 = r2 * (dn - hn * jnp.mean(dn * hn, axis=-1, keepdims=True))
        dh2_ref[...] = dh2
        dh2b = dh2.astype(BF16)
        dwo_ref[0:SSM_W, :] += _dot_tn(ysb, dh2b)
        dwo_ref[SSM_W:, :] += _dot_tn(ycb, dh2b)
        dyssm = _dot_nt(dh2b, wo_ref[0:SSM_W, :])
        dyconv = _dot_nt(dh2b, wo_ref[SSM_W:, :])
        dy2 = dyssm * szs
        dzs_ref[...] = (dyssm * y2 * dszs).astype(BF16)
        dgp = dy2 * y1 * (gate * (1.0 - gate))
        dgpb = dgp.astype(BF16)
        dy1 = dy2 * gate + _dot_nt(dgpb, wg_ref[...])
        dwg_ref[...] += _dot_tn(y1b, dgpb)
        dbg_ref[...] += jnp.sum(dgp, axis=0, keepdims=True)
        dy_ref[...] = dy1 * dgelu
        dbc_ref[...] = (dyconv * yc * szc).astype(BF16)
        dyc = dyconv * bcv * szc
        dyc_ref[...] = dyc
        dzc_ref[...] = (dyconv * bcv * yc * dszc).astype(BF16)
        dcw_ref[0:1, :] += jnp.sum(dyc * v2, axis=0, keepdims=True)
        dcw_ref[1:2, :] += jnp.sum(dyc * v1, axis=0, keepdims=True)
        dcw_ref[2:3, :] += jnp.sum(dyc * v, axis=0, keepdims=True)

    tile_d = pl.BlockSpec((tm, D_MODEL), lambda i: (i, 0))
    tile_s = pl.BlockSpec((tm, SSM_W), lambda i: (i, 0))
    seg_of = lambda c: pl.BlockSpec((tm, SSM_W), lambda i: (i, c))
    halo_of = lambda c: pl.BlockSpec((SUBLANES, SSM_W), lambda i: (jnp.maximum(i * rows8 - 1, 0), c))
    const = lambda shape: pl.BlockSpec(shape, lambda i: (0,) * len(shape))
    seg = _out((n, SSM_W), F32)
    seg_b = _out((n, SSM_W), BF16)
    return _pcall(
        body, name="mix", grid=(n // tm,),
        out_shape=(_out((n, D_MODEL), F32), seg, seg_b, seg_b, seg_b, seg,
                   _out((D_MODEL, D_MODEL), F32), _out((SSM_W, SSM_W), F32),
                   _out((SUBLANES, LANES), F32), _out((1, D_MODEL), F32),
                   _out((1, SSM_W), F32), _out((SUBLANES, CONV_W), F32)),
        in_specs=[tile_d, tile_d, tile_s, seg_of(SEG_ZS), seg_of(SEG_H), seg_of(SEG_BC), seg_of(SEG_CC), seg_of(SEG_ZC),
                  halo_of(SEG_H), halo_of(SEG_CC),
                  const((1, D_MODEL)), const((1, SSM_W)), const((SUBLANES, CONV_W)),
                  const((SSM_W, SSM_W)), const((D_MODEL, D_MODEL))],
        out_specs=(tile_d, tile_s, tile_s, tile_s, tile_s, tile_s,
                   const((D_MODEL, D_MODEL)), const((SSM_W, SSM_W)), const((SUBLANES, LANES)),
                   const((1, D_MODEL)), const((1, SSM_W)), const((SUBLANES, CONV_W))),
        compiler_params=_params(1),
    )(x2, tgt2, y, proj, proj, proj, proj, proj, proj, proj, gf, b_glu, conv8, w_glu_f, w_out_f)


def _in_bwd(x2, dh2, du, dzs, dyc, proj, dbc, dzc, g1, conv8, w_full, seq):
    n = x2.shape[0]
    tm = TOK_TILE
    n_tiles = n // tm
    tiles_per_seq = seq // tm
    rows8 = tm // SUBLANES
    n_blk8 = n // SUBLANES

    def body(x_ref, dh2_ref, du_ref, dzs_ref, dyc_ref, dycn_ref, h_ref, cc_ref, dbc_ref, dzc_ref,
             g_ref, cw_ref, w_ref, gx_ref, dp_ref, dg_ref):
        i = pl.program_id(0)

        @pl.when(i == 0)
        def _():
            dg_ref[...] = jnp.zeros_like(dg_ref)

        dyc = dyc_ref[...]
        last = (i % tiles_per_seq) == tiles_per_seq - 1
        nhalo = jnp.where(last, 0.0, dycn_ref[...])
        dv = (cw_ref[2:3, :] * dyc + cw_ref[1:2, :] * _shift_up(dyc, nhalo, 1)
              + cw_ref[0:1, :] * _shift_up(dyc, nhalo, 2))
        parts = (du_ref[...], dzs_ref[...], dv * cc_ref[...], dbc_ref[...], dv * h_ref[...], dzc_ref[...])
        dxn = jnp.zeros((tm, D_MODEL), F32)
        for k, p in enumerate(parts):
            pb = p.astype(BF16)
            dp_ref[:, k * SSM_W:(k + 1) * SSM_W] = pb
            dxn = dxn + _dot_nt(pb, w_ref[:, k * SSM_W:(k + 1) * SSM_W])
        x = x_ref[...]
        r = lax.rsqrt(jnp.mean(x * x, axis=-1, keepdims=True) + EPS)
        xh = x * r
        dg_ref[...] += jnp.sum(dxn * xh, axis=0, keepdims=True)
        dn = dxn * g_ref[...]
        gx_ref[...] = dh2_ref[...] + r * (dn - xh * jnp.mean(dn * xh, axis=-1, keepdims=True))

    tile_d = pl.BlockSpec((tm, D_MODEL), lambda i: (i, 0))
    tile_s = pl.BlockSpec((tm, SSM_W), lambda i: (i, 0))
    seg_of = lambda c: pl.BlockSpec((tm, SSM_W), lambda i: (i, c))
    nhalo = pl.BlockSpec((SUBLANES, SSM_W), lambda i: (jnp.minimum((i + 1) * rows8, n_blk8 - 1), 0))
    const = lambda shape: pl.BlockSpec(shape, lambda i: (0,) * len(shape))
    return _pcall(
        body, name="in_bwd", grid=(n_tiles,),
        out_shape=(_out((n, D_MODEL), F32), _out((n, IN_COLS), BF16),
                   _out((SUBLANES, D_MODEL), F32)),
        in_specs=[tile_d, tile_d, tile_s, tile_s, tile_s, nhalo, seg_of(SEG_H), seg_of(SEG_CC), tile_s, tile_s,
                  const((1, D_MODEL)), const((SUBLANES, CONV_W)), const((D_MODEL, IN_COLS))],
        out_specs=(tile_d, pl.BlockSpec((tm, IN_COLS), lambda i: (i, 0)), const((SUBLANES, D_MODEL))),
        compiler_params=_params(1),
    )(x2, dh2, du, dzs, dyc, dyc, proj, proj, dbc, dzc, g1, conv8, w_full)


_HALF_BLOCKS = ((0, 0), (0, 1), (1, 0), (2, 0), (1, 1), (2, 1), (3, 0), (3, 1))


def _dw_in_exchange(chips, xn, dproj, smalls):
    n = xn.shape[0]
    tk = min(2048, n)
    nk = n // tk
    piece = (D_MODEL, COLS_PER_DEV)
    hr = D_MODEL // 2
    n_half = len(_HALF_BLOCKS)
    n_small = len(smalls)
    assert _HALF_BLOCKS[0][1] == 0 and _HALF_BLOCKS[1][1] == 1
    order = jnp.stack([chips[b] for b, _ in _HALF_BLOCKS]
                      + [jnp.int32(t) for _, t in _HALF_BLOCKS]).astype(jnp.int32)

    def body(order_ref, xn_hbm, dp_ref, *refs):
        sm_refs = refs[:n_small]
        own_ref, rchip_ref = refs[n_small:n_small + 2]
        rsm_refs = refs[n_small + 2:2 * n_small + 2]
        (xn_ref, acc, stage, rbuf, kbuf, relay_in, xn_sems, give_send, give_recv, keep_send, keep_recv,
         relay_send, relay_recv, sm_send, sm_recv, sm_loc) = refs[2 * n_small + 2:]
        s = pl.program_id(0)

        def xn_copy(kk, t):
            rows = pl.ds(pl.multiple_of(kk * tk, tk), tk)
            return pltpu.make_async_copy(xn_hbm.at[rows, t * hr:(t + 1) * hr], xn_ref.at[t, rows, :],
                                         xn_sems.at[2 * kk + t])

        @pl.when(s == 0)
        def _():
            for kk in range(nk):
                for t in range(2):
                    xn_copy(kk, t).start()
            xn_copy(0, 0).wait()

        @pl.when(s == 1)
        def _():
            xn_copy(0, 1).wait()

        x, y, c = _mesh_pos()
        sib = (x, y, 1 - c)
        y_nbr, x_nbr = (x, 1 - y, c), (1 - x, y, c)
        gather = _TwoLevelGather(list(sm_refs), [functools.partial(lambda r, dev: r.at[dev], r) for r in rsm_refs],
                                 sm_send, sm_recv, sm_loc)

        def give(h):
            cols = pl.ds(pl.multiple_of((1 - c) * COLS_PER_DEV, LANES), COLS_PER_DEV)
            return pltpu.make_async_remote_copy(src_ref=acc.at[h % 2, :, cols], dst_ref=stage.at[h],
                                                send_sem=give_send.at[h], recv_sem=give_recv.at[h],
                                                device_id=sib, device_id_type=MESH)

        def relay(r):
            return pltpu.make_async_remote_copy(src_ref=rbuf.at[r], dst_ref=relay_in.at[r],
                                                send_sem=relay_send.at[r], recv_sem=relay_recv.at[r],
                                                device_id=(x_nbr, y_nbr)[r], device_id_type=MESH)

        def keep(q):
            return pltpu.make_async_remote_copy(src_ref=kbuf.at[q], dst_ref=rchip_ref.at[q // 2, pl.ds((q % 2) * hr, hr), :],
                                                send_sem=keep_send.at[q], recv_sem=keep_recv.at[q],
                                                device_id=(y_nbr, x_nbr)[q // 2], device_id_type=MESH)

        def chip_sum(h):
            give(h).wait_recv()
            mine = [acc[h % 2, :, cc * COLS_PER_DEV:(cc + 1) * COLS_PER_DEV] for cc in range(2)]
            return jnp.where(c == 0, mine[0], mine[1]) + stage[h]

        @pl.when(s == 0)
        def _():
            gather.start()

        @pl.when(s == 2)
        def _():
            gather.neighbours_landed()

        @pl.when(s == n_half - 2)
        def _():
            gather.diagonal_landed()

        for k in range(2, n_half):
            @pl.when(s == k)
            def _(k=k):
                give(k - 2).wait_send()

        slot = s % 2
        t_half = order_ref[n_half + s]
        acc[slot] = _dot_tn(xn_ref[t_half, pl.ds(0, tk), :], dp_ref[pl.ds(0, tk), :])

        def kstep(kk, carry):
            for t in range(2):
                @pl.when(s == t)
                def _(t=t):
                    xn_copy(kk, t).wait()

            off = pl.multiple_of(kk * tk, tk)
            acc[slot] += _dot_tn(xn_ref[t_half, pl.ds(off, tk), :], dp_ref[pl.ds(off, tk), :])
            return carry

        n_first = max(1, (3 * nk) // 4)
        lax.fori_loop(1, n_first, kstep, 0)
        for k in range(1, n_half):
            @pl.when(s == k)
            def _(k=k):
                h = k - 1
                b, t = _HALF_BLOCKS[h]
                total = chip_sum(h)
                if b == 0:
                    rbuf[t] = total.astype(BF16)
                    relay(t).start()
                elif b < 3:
                    if (b, t) in ((1, 0), (2, 1)):
                        relay(t).wait_recv()
                        total = total + relay_in[t].astype(F32)
                    q = 2 * (b - 1) + t
                    kbuf[q] = total.astype(BF16)
                    keep(q).start()
                else:
                    own_ref[0:hr, :] = total

        lax.fori_loop(n_first, nk, kstep, 0)

        for k in range(n_half):
            @pl.when(s == k)
            def _(k=k):
                give(k).start()

        @pl.when(s == n_half - 1)
        def _():
            own_ref[hr:D_MODEL, :] = chip_sum(n_half - 1)
            give(n_half - 2).wait_send()
            give(n_half - 1).wait_send()
            for r in range(2):
                relay(r).wait_send()
            for q in range(4):
                keep(q).wait()
            gather.finish()

    half_piece = (hr, COLS_PER_DEV)
    grid_spec = pltpu.PrefetchScalarGridSpec(
        num_scalar_prefetch=1, grid=(n_half,),
        in_specs=[HBM_SPEC,
                  pl.BlockSpec((n, COLS_PER_CHIP), lambda s, order: (0, order[s])),
                  *([HBM_SPEC] * n_small)],
        out_specs=(pl.BlockSpec(piece, lambda s, order: (0, 0)), HBM_SPEC, *([HBM_SPEC] * n_small)),
        scratch_shapes=[pltpu.VMEM((2, n, hr), BF16),
                        pltpu.VMEM((2, hr, COLS_PER_CHIP), F32), pltpu.VMEM((n_half,) + half_piece, F32),
                        pltpu.VMEM((2,) + half_piece, BF16), pltpu.VMEM((4,) + half_piece, BF16),
                        pltpu.VMEM((2,) + half_piece, BF16),
                        pltpu.SemaphoreType.DMA((2 * nk,)),
                        pltpu.SemaphoreType.DMA((n_half,)), pltpu.SemaphoreType.DMA((n_half,)),
                        pltpu.SemaphoreType.DMA((4,)), pltpu.SemaphoreType.DMA((4,)),
                        pltpu.SemaphoreType.DMA((2,)), pltpu.SemaphoreType.DMA((2,)),
                        pltpu.SemaphoreType.DMA((7 * n_small,)), pltpu.SemaphoreType.DMA((7 * n_small,)),
                        pltpu.SemaphoreType.DMA((n_small,))])
    return _pcall(
        body, name="dw_in_exchange", grid_spec=grid_spec,
        out_shape=(_out(piece, F32), _out((2,) + piece, BF16),
                   *(_out((N_DEV,) + a.shape, a.dtype) for a in smalls)),
        compiler_params=_params(1),
    )(order, xn, dproj, *smalls)


def _adamw(g, w, m, v):
    m_new = ADAM_B1 * m + (1.0 - ADAM_B1) * g
    v_new = ADAM_B2 * v + (1.0 - ADAM_B2) * (g * g)
    m_hat = m_new / (1.0 - ADAM_B1 ** ADAM_STEP)
    v_hat = v_new / (1.0 - ADAM_B2 ** ADAM_STEP)
    delta = -ADAM_LR * (m_hat / (jnp.sqrt(v_hat) + ADAM_EPS) + ADAM_WD * w)
    return delta, m_new, v_new


def _reduce_adam_w_in(own, rchip, w, m, v):
    rows, cols = w.shape
    row_tile = 256

    def body(o_ref, r_ref, w_ref, m_ref, v_ref, g_ref, d_ref, nm_ref, nv_ref):
        g = o_ref[...]
        for s in range(2):
            g = g + r_ref[s].astype(F32)
        g_ref[...] = g
        d_ref[...], nm_ref[...], nv_ref[...] = _adamw(g, w_ref[...], m_ref[...], v_ref[...])

    tile = pl.BlockSpec((row_tile, cols), lambda i: (i, 0))
    shp = _out((rows, cols), F32)
    return _pcall(
        body, name="reduce_adam_w_in", grid=(rows // row_tile,),
        out_shape=(shp,) * 4,
        in_specs=[tile, pl.BlockSpec((2, row_tile, cols), lambda i: (0, i, 0)), tile, tile, tile],
        out_specs=(tile,) * 4,
        compiler_params=_params(1),
    )(own, rchip, w, m, v)


_SMALL_LEAVES = ("norm_gain", "final_norm_gain", "b_glu", "ssm_a_re", "ssm_a_im", "ssm_log_dt", "ssm_d", "conv_w",
                 "ssm_c_re", "ssm_c_im", "ssm_b_re", "ssm_b_im")


def _reduce_adam_small(r_pack, r_gc, r_gb, wmv, sharded):
    n_leaf = len(_SMALL_LEAVES)
    n_sh = len(sharded)

    def body(*refs):
        rp_ref, rgc_ref, rgb_ref = refs[:3]
        w_refs = refs[3:3 + 3 * n_leaf]
        sh_in = refs[3 + 3 * n_leaf:3 + 3 * n_leaf + 4 * n_sh]
        outs0 = 3 + 3 * n_leaf + 4 * n_sh
        loss_ref = refs[outs0]
        o_refs = refs[outs0 + 1:outs0 + 1 + 4 * n_leaf]
        sh_out = refs[outs0 + 1 + 4 * n_leaf:outs0 + 1 + 4 * n_leaf + 4 * n_sh]
        own_conv = refs[-1]

        def total(ref):
            acc = ref[0].astype(F32)
            for s in range(1, N_DEV):
                acc = acc + ref[s].astype(F32)
            return acc

        for i in range(n_sh):
            r_ref, w_ref, m_ref, v_ref = sh_in[4 * i:4 * i + 4]
            o_g, o_d, o_m, o_v = sh_out[4 * i:4 * i + 4]
            g = total(r_ref)
            o_g[...] = g
            o_d[...], o_m[...], o_v[...] = _adamw(g, w_ref[...], m_ref[...], v_ref[...])

        sp = total(rp_ref)
        sgc = total(rgc_ref)
        sgb = total(rgb_ref)
        loss_ref[...] = sp[ROW_LOSS:ROW_LOSS + 1, 0:1]

        def wide(r):
            return jnp.concatenate([sp[r:r + 1, :], sp[r + 1:r + 2, :]], axis=1)

        s5 = slice(ROW_S5, ROW_S5 + N_GROUPS)
        eye = (lax.broadcasted_iota(jnp.int32, (N_GROUPS, N_GROUPS), 0)
               == lax.broadcasted_iota(jnp.int32, (N_GROUPS, N_GROUPS), 1)).astype(F32)
        d_rows = jnp.broadcast_to(sp[ROW_BGLU_D + 1:ROW_BGLU_D + 2, :], (GROUP, SSM_W))
        own_p = (lax.broadcasted_iota(jnp.int32, (GROUP, SSM_W), 1) % GROUP
                 == lax.broadcasted_iota(jnp.int32, (GROUP, SSM_W), 0))
        of_group = (lax.broadcasted_iota(jnp.int32, (SSM_W, N_GROUPS), 0) // GROUP
                    == lax.broadcasted_iota(jnp.int32, (SSM_W, N_GROUPS), 1)).astype(BF16)
        d_pg = sum(_dot(t, of_group) for t in _split3(jnp.where(own_p, d_rows, 0.0)))
        me = 4 * lax.axis_index("x") + 2 * lax.axis_index("y") + lax.axis_index("c")
        for k in range(N_DEV):
            @pl.when(me == k)
            def _(k=k):
                own_conv[...] = sp[ROW_CONV:ROW_CONV + SUBLANES, k * CONV_COLS_PER_DEV:(k + 1) * CONV_COLS_PER_DEV]
        grads = {
            "norm_gain": wide(ROW_NORM_GAIN),
            "final_norm_gain": wide(ROW_FINAL_GAIN),
            "b_glu": sp[ROW_BGLU_D:ROW_BGLU_D + 1, :],
            "ssm_a_re": sp[s5, LANE_A_RE:LANE_A_RE + STATE],
            "ssm_a_im": sp[s5, LANE_A_IM:LANE_A_IM + STATE],
            "ssm_log_dt": jnp.sum(sp[s5, LANE_LOG_DT:LANE_LOG_DT + 1] * eye, axis=0, keepdims=True),
            "ssm_d": d_pg,
            "ssm_c_re": sgc[:, 0:STATE],
            "ssm_c_im": sgc[:, STATE:2 * STATE],
            "ssm_b_re": sgb[:, 0:STATE],
            "ssm_b_im": sgb[:, STATE:2 * STATE],
        }
        for i, name in enumerate(_SMALL_LEAVES):
            w_ref, m_ref, v_ref = w_refs[3 * i:3 * i + 3]
            o_g, o_d, o_m, o_v = o_refs[4 * i:4 * i + 4]
            if name == "conv_w":
                for k in range(w_ref.shape[0]):
                    g = own_conv[k:k + 1, :]
                    o_g[k] = g
                    o_d[k], o_m[k], o_v[k] = _adamw(g, w_ref[k], m_ref[k], v_ref[k])
                continue
            g = grads[name]
            o_g[...] = g
            o_d[...], o_m[...], o_v[...] = _adamw(g, w_ref[...], m_ref[...], v_ref[...])

    flat_w = [a for name in _SMALL_LEAVES for a in wmv[name]]
    leaf_shapes = [_out(wmv[name][0].shape, F32) for name in _SMALL_LEAVES for _ in range(4)]
    sh_shapes = [_out(entry[1].shape, F32) for entry in sharded for _ in range(4)]
    operands = (r_pack, r_gc, r_gb, *flat_w, *(a for entry in sharded for a in entry))
    out_shape = (_out((1, 1), F32), *leaf_shapes, *sh_shapes)
    outs = _pcall(
        body, name="reduce_adam_small", grid=(1,), out_shape=out_shape,
        in_specs=_whole_specs(operands), out_specs=tuple(_whole_specs(out_shape)),
        scratch_shapes=[pltpu.VMEM((SUBLANES, CONV_COLS_PER_DEV), F32)],
        compiler_params=_params(1),
    )(*operands)
    leaves = {name: outs[1 + 4 * i:5 + 4 * i] for i, name in enumerate(_SMALL_LEAVES)}
    first = 1 + 4 * n_leaf
    return outs[0], leaves, [outs[first + 4 * i:first + 4 * i + 4] for i in range(n_sh)]


def kernel(x, norm_gain, w_in, ssm_a_re, ssm_a_im, ssm_log_dt, ssm_b_re, ssm_b_im, ssm_c_re, ssm_c_im, ssm_d, w_glu, b_glu, conv_w, w_out, final_norm_gain, loss_target, m_norm_gain, m_w_in, m_ssm_a_re, m_ssm_a_im, m_ssm_log_dt, m_ssm_b_re, m_ssm_b_im, m_ssm_c_re, m_ssm_c_im, m_ssm_d, m_w_glu, m_b_glu, m_conv_w, m_w_out, m_final_norm_gain, v_norm_gain, v_w_in, v_ssm_a_re, v_ssm_a_im, v_ssm_log_dt, v_ssm_b_re, v_ssm_b_im, v_ssm_c_re, v_ssm_c_im, v_ssm_d, v_w_glu, v_b_glu, v_conv_w, v_w_out, v_final_norm_gain):
    n_seq, seq, _ = x.shape
    n = n_seq * seq

    gh_p = lambda b4: jnp.transpose(b4, (0, 1, 3, 2)).reshape(N_GROUPS * GROUP, STATE)
    c2 = lambda a: a.reshape(N_GROUPS * GROUP, STATE)
    b_re2, b_im2 = gh_p(ssm_b_re), gh_p(ssm_b_im)
    d_row = ssm_d[0].reshape(1, SSM_W)

    x2 = x.reshape(n, D_MODEL)
    tgt2 = loss_target.reshape(n, D_MODEL)
    mx, my, mc = lax.axis_index("x"), lax.axis_index("y"), lax.axis_index("c")
    chip_ids = [2 * cx + cy for cx, cy in ((mx, my), (1 - mx, my), (mx, 1 - my), (1 - mx, 1 - my))]
    arrival = chip_ids
    xn, proj, w_in_f, s5 = _in_proj(
        jnp.stack(arrival).astype(jnp.int32), x2, norm_gain, w_in[0],
        (ssm_a_re[0], ssm_a_im[0], ssm_log_dt, b_re2, b_im2, c2(ssm_c_re), c2(ssm_c_im)))
    a_re_x, a_im_x, log_dt_x, ab_re, ab_im, bb_re_m, bb_im_m, c_re_m, c_imn_m = s5
    u3 = proj.reshape(n_seq, seq, IN_COLS)
    conv_p = jnp.pad(conv_w[0], ((0, SUBLANES - 3), (0, LANES - CONV_COLS_PER_DEV)))
    s_re, s_im, y3, w_out_f, w_glu_f, conv_all = _ssm_fwd(
        u3, bb_re_m, bb_im_m, c_re_m, c_imn_m, d_row, ab_re, ab_im,
        w_out[0], w_glu[0], conv_p, n_seq, seq)
    conv8 = jnp.transpose(conv_all[:, :, :CONV_COLS_PER_DEV], (1, 0, 2)).reshape(SUBLANES, CONV_W)
    (dh2, dy, dzs, dbc, dzc, dyc, dw_out, dw_glu, loss_t, dgf, dbg, dcw) = _mix(
        x2, tgt2, y3.reshape(n, SSM_W), proj, final_norm_gain.reshape(1, D_MODEL), b_glu, conv8,
        w_glu_f, w_out_f, seq)

    du3, dc_re_d, dc_im_d, dbb_re_d, dbb_im_d, dab_re, dab_im, dd, r_out, r_glu = _ssm_bwd(
        dy.reshape(n_seq, seq, SSM_W), u3, s_re, s_im, bb_re_m, bb_im_m, c_re_m, c_imn_m, d_row, ab_re, ab_im,
        dw_out.reshape(N_DEV, OUT_ROWS_PER_DEV, D_MODEL), dw_glu.reshape(N_DEV, GLU_ROWS_PER_DEV, SSM_W), n_seq, seq)
    du = du3.reshape(n, SSM_W)
    grad_x2, dproj, dg8 = _in_bwd(x2, dh2, du, dzs, dyc, proj, dbc, dzc, norm_gain, conv8, w_in_f, seq)
    pack, gc, gb = _ssm_disc_bwd_pack(
        a_re_x, a_im_x, log_dt_x, b_re2, b_im2, dab_re, dab_im,
        dbb_re_d, dbb_im_d, loss_t, dg8, dgf, dbg, dd, dcw, dc_re_d, dc_im_d)

    own_in, rchip_in, r_pack, r_gc, r_gb = _dw_in_exchange(
        [chip_ids[3], chip_ids[2], chip_ids[1], chip_ids[0]],
        xn, dproj, [pack, gc, gb])

    flat2 = lambda a: a.reshape(a.shape[-2:]) if a.ndim > 2 else a.reshape(1, -1)
    c2 = lambda a: a.reshape(N_GROUPS * GROUP, STATE)
    wmv = dict(norm_gain=(norm_gain, m_norm_gain, v_norm_gain),
               final_norm_gain=tuple(flat2(a) for a in (final_norm_gain, m_final_norm_gain, v_final_norm_gain)),
               b_glu=(b_glu, m_b_glu, v_b_glu),
               ssm_a_re=tuple(flat2(a) for a in (ssm_a_re, m_ssm_a_re, v_ssm_a_re)),
               ssm_a_im=tuple(flat2(a) for a in (ssm_a_im, m_ssm_a_im, v_ssm_a_im)),
               ssm_log_dt=(ssm_log_dt, m_ssm_log_dt, v_ssm_log_dt),
               ssm_d=tuple(jnp.transpose(a, (0, 2, 1)).reshape(GROUP, N_GROUPS) for a in (ssm_d, m_ssm_d, v_ssm_d)),
               conv_w=tuple(jnp.transpose(a, (1, 0, 2)) for a in (conv_w, m_conv_w, v_conv_w)),
               ssm_c_re=tuple(c2(a) for a in (ssm_c_re, m_ssm_c_re, v_ssm_c_re)),
               ssm_c_im=tuple(c2(a) for a in (ssm_c_im, m_ssm_c_im, v_ssm_c_im)),
               ssm_b_re=(b_re2, gh_p(m_ssm_b_re), gh_p(v_ssm_b_re)),
               ssm_b_im=(b_im2, gh_p(m_ssm_b_im), gh_p(v_ssm_b_im)))

    res_in = _reduce_adam_w_in(own_in, rchip_in, w_in[0], m_w_in[0], v_w_in[0])
    loss11, small, (res_out, res_glu) = _reduce_adam_small(
        r_pack, r_gc, r_gb, wmv,
        [(r_out, w_out[0], m_w_out[0], v_w_out[0]), (r_glu, w_glu[0], m_w_glu[0], v_w_glu[0])])
    loss = loss11.reshape(())

    shapes = dict(norm_gain=(1, D_MODEL), ssm_a_re=(1, N_GROUPS, STATE), ssm_a_im=(1, N_GROUPS, STATE),
                  ssm_log_dt=(1, N_GROUPS), ssm_c_re=(1, N_GROUPS, GROUP, STATE), ssm_c_im=(1, N_GROUPS, GROUP, STATE),
                  b_glu=(1, SSM_W), final_norm_gain=(D_MODEL,))
    big = dict(w_in=res_in, w_glu=res_glu, w_out=res_out)

    def leaf(kind, name):
        if name in big:
            return big[name][kind][None]
        if name in ("ssm_b_re", "ssm_b_im"):
            return jnp.transpose(small[name][kind].reshape(1, N_GROUPS, GROUP, STATE), (0, 1, 3, 2))
        if name == "ssm_d":
            return jnp.transpose(small[name][kind].reshape(1, GROUP, N_GROUPS), (0, 2, 1))
        if name == "conv_w":
            return jnp.transpose(small[name][kind], (1, 0, 2))
        return small[name][kind].reshape(shapes[name])

    order = ["norm_gain", "w_in", "ssm_a_re", "ssm_a_im", "ssm_log_dt", "ssm_b_re", "ssm_b_im", "ssm_c_re",
             "ssm_c_im", "ssm_d", "w_glu", "b_glu", "conv_w", "w_out", "final_norm_gain"]
    outs = [loss, grad_x2.reshape(x.shape)]
    for kind in range(4):
        outs += [leaf(kind, name) for name in order]
    return tuple(outs)
```

```python
import functools
import math

import jax
import jax.numpy as jnp
from jax import lax
from jax.experimental import pallas as pl
from jax.experimental.pallas import tpu as pltpu

F32 = jnp.float32
BF16 = jnp.bfloat16

N_DEV = 8
D_MODEL = 1024
SSM_W = 512
CONV_W = 512
N_GROUPS = 32
GROUP = 16
STATE = 64
IN_COLS = 3072
SEG_U, SEG_ZS, SEG_H, SEG_BC, SEG_CC, SEG_ZC = range(6)
COLS_PER_DEV = IN_COLS // N_DEV
N_CHIP = N_DEV // 2
COLS_PER_CHIP = 2 * COLS_PER_DEV
OUT_ROWS_PER_DEV = D_MODEL // N_DEV
GLU_ROWS_PER_DEV = SSM_W // N_DEV
CONV_COLS_PER_DEV = CONV_W // N_DEV
EPS = 1e-6

N_JBLK = 4
JB_CH = SSM_W // N_JBLK
JB_ST = N_GROUPS * STATE // N_JBLK

ADAM_LR = 0.001
ADAM_B1 = 0.9
ADAM_B2 = 0.999
ADAM_EPS = 1e-08
ADAM_WD = 0.01
ADAM_STEP = 10

SUBLANES = 8
LANES = 128
VMEM_LIMIT = 48 * 1024 * 1024
TOK_TILE = 256
IN_TILE = 1024
SCAN_TILE = 1024

MESH = pl.DeviceIdType.MESH
HBM_SPEC = pl.BlockSpec(memory_space=pltpu.HBM)


def _build(body, **kw):
    return pl.pallas_call(body, **kw)


def _pcall(body, **kw):
    def call(*operands):
        pinned = [a if jnp.issubdtype(a.dtype, jnp.integer) else pltpu.with_memory_space_constraint(a, pltpu.HBM)
                  for a in operands]
        return _build(body, **kw)(*pinned)
    return call


def _whole_specs(arrays):
    return [pl.BlockSpec(a.shape, functools.partial(lambda nd, i: (0,) * nd, len(a.shape))) for a in arrays]


def _out(shape, dtype):
    return pltpu.HBM(tuple(shape), dtype)


def _params(n_grid):
    return pltpu.CompilerParams(dimension_semantics=("arbitrary",) * n_grid,
                                vmem_limit_bytes=VMEM_LIMIT)


def _dot(a, b):
    return jnp.dot(a, b, preferred_element_type=F32)


def _dot_nt(a, b):
    return lax.dot_general(a, b, (((1,), (1,)), ((), ())), preferred_element_type=F32)


def _dot_tn(a, b):
    return lax.dot_general(a, b, (((0,), (0,)), ((), ())), preferred_element_type=F32)


def _sigmoid(z):
    return 1.0 / (1.0 + jnp.exp(-z))


_GELU_C = math.sqrt(2.0 / math.pi)


def _gelu_and_grad(y):
    inner = _GELU_C * (y + 0.044715 * (y * y * y))
    t = jnp.tanh(inner)
    g = 0.5 * y * (1.0 + t)
    dg = 0.5 * (1.0 + t) + 0.5 * y * (1.0 - t * t) * (_GELU_C * (1.0 + 3.0 * 0.044715 * (y * y)))
    return g, dg


def _silu_and_grad(z):
    s = _sigmoid(z)
    return z * s, s * (1.0 + z * (1.0 - s))


def _shift_down(v, halo, k):
    rolled = pltpu.roll(v, k, 0)
    row = lax.broadcasted_iota(jnp.int32, v.shape, 0)
    for r in range(k):
        rolled = jnp.where(row == r, halo[SUBLANES - k + r:SUBLANES - k + r + 1, :], rolled)
    return rolled


def _shift_up(v, halo, k):
    n = v.shape[0]
    rolled = pltpu.roll(v, n - k, 0)
    row = lax.broadcasted_iota(jnp.int32, v.shape, 0)
    for r in range(k):
        rolled = jnp.where(row == n - k + r, halo[r:r + 1, :], rolled)
    return rolled


def _mesh_pos():
    return lax.axis_index("x"), lax.axis_index("y"), lax.axis_index("c")


def _direct_copies(srcs_for, out_refs, send_sems, recv_sems, loc_sems):
    x, y, c = _mesh_pos()
    me_id = 4 * x + 2 * y + c
    n_arr = len(out_refs)
    dsts = [r.at[me_id] for r in out_refs]
    own = srcs_for(me_id)
    mine = [pltpu.make_async_copy(own[a], dsts[a], loc_sems.at[a]) for a in range(n_arr)]
    sends = []
    for k in range(1, N_DEV):
        px, py, pc = x ^ ((k >> 2) & 1), y ^ ((k >> 1) & 1), c ^ (k & 1)
        src = srcs_for(4 * px + 2 * py + pc)
        for a in range(n_arr):
            sends.append(pltpu.make_async_remote_copy(
                src_ref=src[a], dst_ref=dsts[a],
                send_sem=send_sems.at[(k - 1) * n_arr + a], recv_sem=recv_sems.at[(k - 1) * n_arr + a],
                device_id=(px, py, pc), device_id_type=MESH))
    return mine, sends


class _TwoLevelGather:
    def __init__(self, srcs, slots, send_sems, recv_sems, loc_sems):
        self.srcs, self.slots, self.n_arr = srcs, slots, len(srcs)
        self.send_sems, self.recv_sems, self.loc_sems = send_sems, recv_sems, loc_sems
        x, y, c = _mesh_pos()
        self.c = c
        self.me, self.sib = (x, y, c), (x, y, 1 - c)
        self.chips = [(1 - x, y), (x, 1 - y), (1 - x, 1 - y)]

    def _copies(self, k, block, to, from_src=False):
        dev = 4 * block[0] + 2 * block[1] + block[2]
        return [pltpu.make_async_remote_copy(
            src_ref=self.srcs[a] if from_src else self.slots[a](dev), dst_ref=self.slots[a](dev),
            send_sem=self.send_sems.at[k * self.n_arr + a], recv_sem=self.recv_sems.at[k * self.n_arr + a],
            device_id=to, device_id_type=MESH) for a in range(self.n_arr)]

    def _local(self):
        dev = 4 * self.me[0] + 2 * self.me[1] + self.me[2]
        return [pltpu.make_async_copy(self.srcs[a], self.slots[a](dev), self.loc_sems.at[a])
                for a in range(self.n_arr)]

    def start(self):
        for cp in self._local() + self._copies(0, self.me, self.sib, True):
            cp.start()
        for j in (0, 1):
            for cp in self._copies(1 + j, self.me, (*self.chips[j], self.c), True):
                cp.start()

    def wait_own(self):
        for cp in self._local():
            cp.wait()

    def wait_sibling(self):
        for cp in self._copies(0, self.sib, self.me):
            cp.wait_recv()

    def wait_and_pass_on(self, j):
        chip = self.chips[j]
        for cp in self._copies(1 + j, (*chip, self.c), self.me):
            cp.wait_recv()
        for cp in self._copies(4 + j, (*chip, self.c), self.sib):
            cp.start()

    def neighbours_landed(self):
        x, y, c = self.me
        self.wait_and_pass_on(0)
        self.wait_and_pass_on(1)
        for cp in self._copies(1 + 2, (x ^ c, y ^ (1 - c), c), (x ^ (1 - c), y ^ c, c)):
            cp.start()

    def diagonal_landed(self):
        self.wait_and_pass_on(2)

    def wait_passed_on(self, j):
        for cp in self._copies(4 + j, (*self.chips[j], 1 - self.c), self.me):
            cp.wait_recv()

    def wait_sends(self):
        for cp in self._copies(0, self.me, self.sib, True):
            cp.wait_send()
        for j, chip in enumerate(self.chips):
            for cp in self._copies(1 + j, self.me, (*chip, self.c), True) + self._copies(4 + j, (*chip, self.c), self.sib):
                cp.wait_send()

    def finish(self):
        self.wait_sibling()
        for j in range(3):
            self.wait_passed_on(j)
        self.wait_sends()
        self.wait_own()


def _disc(a_re, a_im, log_dt, b_re, b_im):
    dt = jnp.exp(log_dt)
    mag = jnp.exp(a_re * dt)
    ab_re = mag * jnp.cos(a_im * dt)
    ab_im = mag * jnp.sin(a_im * dt)
    den = a_re * a_re + a_im * a_im
    p_re = ab_re - 1.0
    p_im = ab_im
    q_re = (p_re * a_re + p_im * a_im) / den
    q_im = (p_im * a_re - p_re * a_im) / den
    bb_re = q_re * b_re - q_im * b_im
    bb_im = q_re * b_im + q_im * b_re
    return ab_re, ab_im, bb_re, bb_im


def _split3(v):
    hi = v.astype(BF16)
    r1 = v - hi.astype(F32)
    mid = r1.astype(BF16)
    lo = (r1 - mid.astype(F32)).astype(BF16)
    return hi, mid, lo


def _select_dot(sel, v):
    return sum(_dot(sel, t) for t in _split3(v))


PACK_ROWS = 72
PACK_W = 512
ROW_FINAL_GAIN, ROW_NORM_GAIN, ROW_BGLU_D, ROW_CONV, ROW_LOSS, ROW_S5 = 0, 8, 16, 24, 32, 40
LANE_A_RE, LANE_A_IM, LANE_LOG_DT = 0, 128, 256


def _ssm_disc_bwd_pack(a_re_x, a_im_x, log_dt_x, b_re, b_im, g_ab_re, g_ab_im, dbb_re_d, dbb_im_d,
                       loss_t, dg8, dgf, dbg, dd, dcw, dc_re_d, dc_im_d):
    rows_gh = N_GROUPS * GROUP

    def body(are, aim, ldt, bre, bim, gabre, gabim, dbbre_ref, dbbim_ref,
             loss_ref, dg8_ref, dgf_ref, dbg_ref, dd_ref, dcw_ref, dcre_ref, dcim_ref,
             p_ref, gc_ref, gb_ref, gbb_re, gbb_im):
        r_g = lax.broadcasted_iota(jnp.int32, (N_GROUPS, rows_gh), 0)
        c_gh = lax.broadcasted_iota(jnp.int32, (N_GROUPS, rows_gh), 1)
        group_sum = (c_gh // GROUP == r_g).astype(BF16)
        r_gh = lax.broadcasted_iota(jnp.int32, (rows_gh, N_GROUPS), 0)
        c_g = lax.broadcasted_iota(jnp.int32, (rows_gh, N_GROUPS), 1)
        first_row = (r_gh == c_g * GROUP).astype(BF16)

        def diag_block(ref, j, gi):
            return ref[j, gi * GROUP:(gi + 1) * GROUP, gi * STATE:(gi + 1) * STATE]

        for j in range(N_JBLK):
            for gi in range(SUBLANES):
                r0 = (j * SUBLANES + gi) * GROUP
                gbb_re[r0:r0 + GROUP, :] = diag_block(dbbre_ref, j, gi)
                gbb_im[r0:r0 + GROUP, :] = diag_block(dbbim_ref, j, gi)
                both = jnp.concatenate([diag_block(dcre_ref, j, gi), -diag_block(dcim_ref, j, gi)], axis=1)
                gc_ref[r0:r0 + GROUP, :] = both.astype(BF16)

        def by_group(ref):
            return jnp.concatenate([ref[:, g * STATE:(g + 1) * STATE] for g in range(N_GROUPS)], axis=0)

        _, vjp = jax.vjp(_disc, are[...], aim[...], ldt[...], bre[...], bim[...])
        d_are, d_aim, d_ldt, d_bre, d_bim = vjp((_select_dot(first_row, by_group(gabre)),
                                                 _select_dot(first_row, by_group(gabim)),
                                                 gbb_re[...], gbb_im[...]))
        gb_ref[...] = jnp.concatenate([d_bre, d_bim], axis=1).astype(BF16)

        p_ref[...] = jnp.zeros_like(p_ref)
        half = D_MODEL // 2
        for r, src in ((ROW_FINAL_GAIN, dgf_ref), (ROW_NORM_GAIN, dg8_ref)):
            p_ref[r:r + 1, :] = src[0:1, 0:half]
            p_ref[r + 1:r + 2, :] = src[0:1, half:D_MODEL]
        p_ref[ROW_BGLU_D:ROW_BGLU_D + 1, :] = dbg_ref[...]
        p_ref[ROW_BGLU_D + 1:ROW_BGLU_D + 2, :] = dd_ref[...]
        p_ref[ROW_CONV:ROW_CONV + SUBLANES, :] = dcw_ref[...]
        p_ref[ROW_LOSS:ROW_LOSS + SUBLANES, 0:LANES] = loss_ref[...]
        s5 = slice(ROW_S5, ROW_S5 + N_GROUPS)
        p_ref[s5, LANE_A_RE:LANE_A_RE + STATE] = _select_dot(group_sum, d_are)
        p_ref[s5, LANE_A_IM:LANE_A_IM + STATE] = _select_dot(group_sum, d_aim)
        p_ref[s5, LANE_LOG_DT:LANE_LOG_DT + LANES] = _select_dot(group_sum, jnp.broadcast_to(d_ldt, (rows_gh, LANES)))

    operands = (a_re_x, a_im_x, log_dt_x, b_re, b_im, g_ab_re, g_ab_im, dbb_re_d, dbb_im_d,
                loss_t, dg8, dgf, dbg, dd, dcw, dc_re_d, dc_im_d)
    out_shape = (_out((PACK_ROWS, PACK_W), F32),
                 _out((rows_gh, 2 * STATE), BF16),
                 _out((rows_gh, 2 * STATE), BF16))
    return _pcall(body, name="ssm_disc_bwd_pack", grid=(1,), out_shape=out_shape,
                  in_specs=_whole_specs(operands), out_specs=tuple(_whole_specs(out_shape)),
                  scratch_shapes=[pltpu.VMEM((rows_gh, STATE), F32), pltpu.VMEM((rows_gh, STATE), F32)],
                  compiler_params=_params(1))(*operands)


def _s5_prepare(are, aim, ldt, bre, bim, cre, cim,
                o_ax_re, o_ax_im, o_ldt_x, o_ab_re, o_ab_im, o_bb_re, o_bb_im, o_c_re, o_c_imn):
    rows_gh = N_GROUPS * GROUP
    rep = (lax.broadcasted_iota(jnp.int32, (rows_gh, N_GROUPS), 0) // GROUP
           == lax.broadcasted_iota(jnp.int32, (rows_gh, N_GROUPS), 1)).astype(BF16)
    eye = (lax.broadcasted_iota(jnp.int32, (N_GROUPS, N_GROUPS), 0)
           == lax.broadcasted_iota(jnp.int32, (N_GROUPS, N_GROUPS), 1)).astype(F32)
    ldt_col = jnp.sum(eye * ldt[...], axis=1, keepdims=True)
    a_re_x = _select_dot(rep, are[...])
    a_im_x = _select_dot(rep, aim[...])
    ldt_x = _select_dot(rep, jnp.broadcast_to(ldt_col, (N_GROUPS, LANES)))[:, 0:1]
    o_ax_re[...] = a_re_x
    o_ax_im[...] = a_im_x
    o_ldt_x[...] = ldt_x
    ab_re, ab_im, bb_re, bb_im = _disc(a_re_x, a_im_x, ldt_x, bre[...], bim[...])
    for j in range(N_JBLK):
        first = [(j * SUBLANES + gi) * GROUP for gi in range(SUBLANES)]
        o_ab_re[j] = jnp.concatenate([ab_re[r:r + 1, :] for r in first], axis=1)
        o_ab_im[j] = jnp.concatenate([ab_im[r:r + 1, :] for r in first], axis=1)
    for o, v in ((o_bb_re, bb_re), (o_bb_im, bb_im), (o_c_re, cre[...]), (o_c_imn, -cim[...])):
        for j in range(N_JBLK):
            for gi in range(SUBLANES):
                r0 = (j * SUBLANES + gi) * GROUP
                parts = [v[r0:r0 + GROUP, :] if k == gi else jnp.zeros((GROUP, STATE), F32) for k in range(SUBLANES)]
                o[j, gi * GROUP:(gi + 1) * GROUP, :] = jnp.concatenate(parts, axis=1).astype(BF16)


def _in_proj(order, x2, g1, w_in_own, s5):
    n = x2.shape[0]
    tm = min(IN_TILE, n)
    n_tiles = n // tm
    n_s5_in = len(s5)
    n_s5_out = 9

    def body(order_ref, x_ref, g_ref, w_ref, *refs):
        s5_in = refs[:n_s5_in]
        xn_ref, proj_ref, wall_ref = refs[n_s5_in:n_s5_in + 3]
        s5_out = refs[n_s5_in + 3:n_s5_in + 3 + n_s5_out]
        xn_scr, wbuf, wown, send_sems, recv_sems, loc_sems, out_sems = refs[n_s5_in + 3 + n_s5_out:]
        k = pl.program_id(0)
        i = pl.program_id(1)

        def slot(dev):
            return wbuf.at[dev // 2, :, pl.ds(pl.multiple_of((dev % 2) * COLS_PER_DEV, LANES), COLS_PER_DEV)]

        gather = _TwoLevelGather([wown], [slot], send_sems, recv_sems, loc_sems)

        @pl.when((k == 0) & (i == 0))
        def _():
            wown[...] = w_ref[...].astype(BF16)
            gather.start()

        def own_chip():
            gather.wait_own()
            gather.wait_sibling()

        def x_chip():
            gather.neighbours_landed()
            gather.wait_passed_on(0)

        def diag_chip():
            gather.diagonal_landed()
            gather.wait_passed_on(2)

        rows = pl.ds(pl.multiple_of(i * tm, tm), tm)

        @pl.when(k == 0)
        def _():
            x = x_ref[...]
            r = lax.rsqrt(jnp.mean(x * x, axis=-1, keepdims=True) + EPS)
            xn = ((x * r) * g_ref[...]).astype(BF16)
            xn_scr[rows, :] = xn
            xn_ref[...] = xn

        def keep_copy(kk):
            q = order_ref[kk]
            cols = pl.ds(pl.multiple_of(q * COLS_PER_CHIP, LANES), COLS_PER_CHIP)
            return pltpu.make_async_copy(wbuf.at[q], wall_ref.at[:, cols], out_sems.at[kk])

        arrivals = [own_chip, x_chip, functools.partial(gather.wait_passed_on, 1), diag_chip]
        for kk, arrived in enumerate(arrivals):
            @pl.when((k == kk) & (i == 0))
            def _(kk=kk, arrived=arrived):
                arrived()
                keep_copy(kk).start()

        proj_ref[...] = _dot(xn_scr[rows, :], wbuf[order_ref[k]])

        @pl.when((k == 0) & (i == n_tiles - 1))
        def _():
            _s5_prepare(*s5_in, *s5_out)

        @pl.when((k == N_CHIP - 1) & (i == n_tiles - 1))
        def _():
            gather.wait_sends()
            for kk in range(N_CHIP):
                keep_copy(kk).wait()

    tile_once = lambda k, i, order: (jnp.where(k == 0, i, n_tiles - 1), 0)
    whole = lambda shape: pl.BlockSpec(shape, lambda k, i, order: (0,) * len(shape))
    rows_gh = N_GROUPS * GROUP
    s5_out_shapes = ([(rows_gh, STATE), F32], [(rows_gh, STATE), F32], [(rows_gh, 1), F32],
                     [(N_JBLK, 1, JB_ST), F32], [(N_JBLK, 1, JB_ST), F32]) + ([(N_JBLK, JB_CH, JB_ST), BF16],) * 4
    grid_spec = pltpu.PrefetchScalarGridSpec(
        num_scalar_prefetch=1, grid=(N_CHIP, n_tiles),
        in_specs=[pl.BlockSpec((tm, D_MODEL), tile_once),
                  whole((1, D_MODEL)),
                  whole(w_in_own.shape),
                  *(whole(a.shape) for a in s5)],
        out_specs=(pl.BlockSpec((tm, D_MODEL), tile_once),
                   pl.BlockSpec((tm, COLS_PER_CHIP), lambda k, i, order: (i, order[k])),
                   HBM_SPEC,
                   *(whole(shape) for shape, _ in s5_out_shapes)),
        scratch_shapes=[pltpu.VMEM((n, D_MODEL), BF16), pltpu.VMEM((N_CHIP, D_MODEL, COLS_PER_CHIP), BF16),
                        pltpu.VMEM(w_in_own.shape, BF16),
                        pltpu.SemaphoreType.DMA((7,)), pltpu.SemaphoreType.DMA((7,)), pltpu.SemaphoreType.DMA((1,)),
                        pltpu.SemaphoreType.DMA((N_CHIP,))])
    outs = _pcall(
        body, name="in_proj", grid_spec=grid_spec,
        out_shape=(_out((n, D_MODEL), BF16), _out((n, IN_COLS), F32),
                   _out((D_MODEL, IN_COLS), BF16),
                   *(_out(shape, dt) for shape, dt in s5_out_shapes)),
        compiler_params=_params(2),
    )(order, x2, g1, w_in_own, *s5)
    return outs[0], outs[1], outs[2], outs[3:]


def _cmul(p, q):
    return p[0] * q[0] - p[1] * q[1], p[0] * q[1] + p[1] * q[0]


def _scan_tables(ar, ai, width, reverse):
    pows = [(ar, ai)]
    for _ in range(SUBLANES - 1):
        pows.append(_cmul(pows[-1], (ar, ai)))
    row = lax.broadcasted_iota(jnp.int32, (SUBLANES, width), 0)

    def bc(v):
        return jnp.broadcast_to(v, (SUBLANES, width))

    levels = []
    for k in (1, 2, 4):
        keep = (row <= SUBLANES - 1 - k) if reverse else (row >= k)
        levels.append((jnp.where(keep, bc(pows[k - 1][0]), 0.0), jnp.where(keep, bc(pows[k - 1][1]), 0.0)))
    cre = jnp.zeros((SUBLANES, width), F32)
    cim = jnp.zeros((SUBLANES, width), F32)
    for r in range(SUBLANES):
        e = (SUBLANES - r) if reverse else (r + 1)
        cre = jnp.where(row == r, bc(pows[e - 1][0]), cre)
        cim = jnp.where(row == r, bc(pows[e - 1][1]), cim)
    return levels, (cre, cim)


def _load_chunked(src_ref, b, dst_ref, n_rows):
    n_blk = n_rows // SUBLANES
    for i in range(n_blk):
        dst_ref[b, i * SUBLANES:(i + 1) * SUBLANES, :] = src_ref[b, pl.ds(i, SUBLANES, stride=n_blk), :]


def _store_chunked(val, dst_ref, b, n_rows):
    n_blk = n_rows // SUBLANES
    for i in range(n_blk):
        dst_ref[b, pl.ds(i, SUBLANES, stride=n_blk), :] = val[i * SUBLANES:(i + 1) * SUBLANES, :]


def _chunk_scan(re_ref, im_ref, bs, car_ref, ar, ai, n_rows, reverse, on_block=None):
    width = re_ref.shape[2]
    n_blk = n_rows // SUBLANES
    shape = (SUBLANES, width)
    abr = jnp.broadcast_to(ar, shape)
    abi = jnp.broadcast_to(ai, shape)
    order = list(range(n_blk - 1, -1, -1)) if reverse else list(range(n_blk))

    def blk(ref, b, i):
        return ref[b, i * SUBLANES:(i + 1) * SUBLANES, :]

    def step(state, b, i):
        sr, si = state
        return abr * sr - abi * si + blk(re_ref, b, i), abr * si + abi * sr + blk(im_ref, b, i)

    finals = {b: (blk(re_ref, b, order[0]), blk(im_ref, b, order[0])) for b in bs}
    for i in order[1:]:
        for b in bs:
            finals[b] = step(finals[b], b, i)

    mr, mi = ar, ai
    for _ in range(n_blk.bit_length() - 1):
        mr, mi = _cmul((mr, mi), (mr, mi))
    levels, _ = _scan_tables(mr, mi, width, reverse)
    mbr = jnp.broadcast_to(mr, shape)
    mbi = jnp.broadcast_to(mi, shape)
    row = lax.broadcasted_iota(jnp.int32, shape, 0)
    edge_in = SUBLANES - 1 if reverse else 0
    edge_out = 0 if reverse else SUBLANES - 1
    sh1 = SUBLANES - 1 if reverse else 1
    states = {}
    for b in bs:
        fr, fi = finals[b]
        gr = jnp.where(row == edge_in, jnp.broadcast_to(car_ref[b, 0:1, :], shape), pltpu.roll(fr, sh1, 0))
        gi = jnp.where(row == edge_in, jnp.broadcast_to(car_ref[b, 1:2, :], shape), pltpu.roll(fi, sh1, 0))
        for (lr, li), k in zip(levels, (1, 2, 4)):
            sh = (SUBLANES - k) if reverse else k
            sr = pltpu.roll(gr, sh, 0)
            si = pltpu.roll(gi, sh, 0)
            gr, gi = gr + (lr * sr - li * si), gi + (lr * si + li * sr)
        car_ref[b, 0:1, :] = (fr + (mbr * gr - mbi * gi))[edge_out:edge_out + 1, :]
        car_ref[b, 1:2, :] = (fi + (mbr * gi + mbi * gr))[edge_out:edge_out + 1, :]
        states[b] = (gr, gi)

    for i in order:
        for b in bs:
            states[b] = step(states[b], b, i)
            re_ref[b, i * SUBLANES:(i + 1) * SUBLANES, :] = states[b][0]
            im_ref[b, i * SUBLANES:(i + 1) * SUBLANES, :] = states[b][1]
            if on_block is not None:
                on_block(b, i, *states[b])


def _ssm_fwd(u, bb_re, bb_im, c_re_t, c_imn_t, d_row, ab_re, ab_im, w_out_own, w_glu_own, conv_p, n_seq, seq):
    tt = min(SCAN_TILE, seq)
    nt = seq // tt

    def body(u_ref, bbre, bbim, cre, cimn, d_ref, are, aim, wout_ref, wglu_ref, cw_ref,
             sre_ref, sim_ref, y_ref, oout_ref, oglu_ref, ocw_ref,
             up_ref, car_ref, woutb_ref, wglub_ref, send_sems, recv_sems, loc_sems):
        j = pl.program_id(0)
        t = pl.program_id(1)
        gather = _TwoLevelGather(
            [woutb_ref, wglub_ref, cw_ref],
            [lambda dev: oout_ref.at[pl.ds(pl.multiple_of(dev * OUT_ROWS_PER_DEV, OUT_ROWS_PER_DEV), OUT_ROWS_PER_DEV), :],
             lambda dev: oglu_ref.at[pl.ds(pl.multiple_of(dev * GLU_ROWS_PER_DEV, GLU_ROWS_PER_DEV), GLU_ROWS_PER_DEV), :],
             lambda dev: ocw_ref.at[dev]],
            send_sems, recv_sems, loc_sems)

        @pl.when((j == 0) & (t == 0))
        def _():
            woutb_ref[...] = wout_ref[...].astype(BF16)
            wglub_ref[...] = wglu_ref[...].astype(BF16)
            gather.start()

        @pl.when((j == N_JBLK // 2) & (t == 0))
        def _():
            gather.neighbours_landed()

        @pl.when((j == N_JBLK - 1) & (t == 0))
        def _():
            gather.diagonal_landed()

        @pl.when(t == 0)
        def _():
            car_ref[...] = jnp.zeros_like(car_ref)

        bs = list(range(n_seq))
        for b in bs:
            _load_chunked(u_ref, b, up_ref, tt)
        for b in bs:
            ub = up_ref[b].astype(BF16)
            sre_ref[b] = _dot(ub, bbre[0])
            sim_ref[b] = _dot(ub, bbim[0])
            _chunk_scan(sre_ref, sim_ref, [b], car_ref, are[0], aim[0], tt, reverse=False)
        for b in bs:
            yp = (_dot_nt(sre_ref[b].astype(BF16), cre[0]) + _dot_nt(sim_ref[b].astype(BF16), cimn[0])
                  + d_ref[...] * up_ref[b])
            _store_chunked(yp, y_ref, b, tt)

        @pl.when((j == N_JBLK - 1) & (t == nt - 1))
        def _():
            gather.finish()

    tok = lambda j, t: (0, t, j)
    blk3 = lambda j, t: (j, 0, 0)
    row = lambda j, t: (0, j)
    whole = lambda j, t: (0, 0)
    st = _out((n_seq, seq, N_JBLK * JB_ST), F32)
    n_arr = 3
    return _pcall(
        body, name="ssm_fwd", grid=(N_JBLK, nt),
        out_shape=(st, st, _out((n_seq, seq, SSM_W), F32),
                   _out((D_MODEL, D_MODEL), BF16), _out((SSM_W, SSM_W), BF16),
                   _out((N_DEV, SUBLANES, LANES), F32)),
        in_specs=[pl.BlockSpec((n_seq, tt, JB_CH), tok),
                  pl.BlockSpec((1, JB_CH, JB_ST), blk3), pl.BlockSpec((1, JB_CH, JB_ST), blk3),
                  pl.BlockSpec((1, JB_CH, JB_ST), blk3), pl.BlockSpec((1, JB_CH, JB_ST), blk3),
                  pl.BlockSpec((1, JB_CH), row), pl.BlockSpec((1, 1, JB_ST), blk3), pl.BlockSpec((1, 1, JB_ST), blk3),
                  pl.BlockSpec(w_out_own.shape, whole), pl.BlockSpec(w_glu_own.shape, whole), HBM_SPEC],
        out_specs=(pl.BlockSpec((n_seq, tt, JB_ST), tok), pl.BlockSpec((n_seq, tt, JB_ST), tok),
                   pl.BlockSpec((n_seq, tt, JB_CH), tok), HBM_SPEC, HBM_SPEC, HBM_SPEC),
        scratch_shapes=[pltpu.VMEM((n_seq, tt, JB_CH), F32), pltpu.VMEM((n_seq, SUBLANES, JB_ST), F32),
                        pltpu.VMEM(w_out_own.shape, BF16), pltpu.VMEM(w_glu_own.shape, BF16),
                        pltpu.SemaphoreType.DMA((7 * n_arr,)), pltpu.SemaphoreType.DMA((7 * n_arr,)),
                        pltpu.SemaphoreType.DMA((n_arr,))],
        compiler_params=_params(2),
    )(u, bb_re, bb_im, c_re_t, c_imn_t, d_row, ab_re, ab_im, w_out_own, w_glu_own, conv_p)


def _ssm_bwd(dy, u, s_re, s_im, bb_re, bb_im, c_re_t, c_imn_t, d_row, ab_re, ab_im, g_out, g_glu, n_seq, seq):
    tt = min(SCAN_TILE, seq)
    nt = seq // tt
    rows8 = tt // SUBLANES

    def body(dy_ref, u_ref, sre_ref, sim_ref, pre_ref, pim_ref, bbre, bbim, cre, cimn, d_ref, are, aim,
             gout_ref, gglu_ref,
             du_ref, dcre_ref, dcim_ref, dbbre_ref, dbbim_ref, dare_ref, daim_ref, dd_ref, rout_ref, rglu_ref,
             lre_ref, lim_ref, dyp_ref, up_ref, car_ref, send_sems, recv_sems, loc_sems):
        j = pl.program_id(0)
        tr = pl.program_id(1)

        def exchange():
            return _direct_copies(lambda pid: [gout_ref.at[pid], gglu_ref.at[pid]], [rout_ref, rglu_ref],
                                  send_sems, recv_sems, loc_sems)

        @pl.when((j == 0) & (tr == 0))
        def _():
            mine, sends = exchange()
            for cp in mine + sends:
                cp.start()

        @pl.when(tr == 0)
        def _():
            car_ref[...] = jnp.zeros_like(car_ref)
            for r in (dcre_ref, dcim_ref, dbbre_ref, dbbim_ref, dare_ref, daim_ref, dd_ref):
                r[...] = jnp.zeros_like(r)

        first = tr == nt - 1
        row = lax.broadcasted_iota(jnp.int32, (SUBLANES, JB_ST), 0)
        n_blk = tt // SUBLANES
        bs = list(range(n_seq))
        for b in bs:
            _load_chunked(dy_ref, b, dyp_ref, tt)
            _load_chunked(u_ref, b, up_ref, tt)
        for b in bs:
            dyb = dyp_ref[b].astype(BF16)
            lre_ref[b] = _dot(dyb, cre[0])
            lim_ref[b] = _dot(dyb, cimn[0])
        acc = {b: [jnp.zeros((SUBLANES, JB_ST), F32), jnp.zeros((SUBLANES, JB_ST), F32)] for b in bs}

        def on_block(b, i, lr, li):
            if i > 0:
                spr = sre_ref[b, (i - 1) * SUBLANES:i * SUBLANES, :]
                spi = sim_ref[b, (i - 1) * SUBLANES:i * SUBLANES, :]
            else:
                hr = jnp.where(first, 0.0, pre_ref[b, SUBLANES - 1:SUBLANES, :])
                hi = jnp.where(first, 0.0, pim_ref[b, SUBLANES - 1:SUBLANES, :])
                last_r = sre_ref[b, (n_blk - 1) * SUBLANES:n_blk * SUBLANES, :]
                last_i = sim_ref[b, (n_blk - 1) * SUBLANES:n_blk * SUBLANES, :]
                spr = jnp.where(row == 0, jnp.broadcast_to(hr, row.shape), pltpu.roll(last_r, 1, 0))
                spi = jnp.where(row == 0, jnp.broadcast_to(hi, row.shape), pltpu.roll(last_i, 1, 0))
            acc[b][0] = acc[b][0] + (lr * spr + li * spi)
            acc[b][1] = acc[b][1] + (li * spr - lr * spi)

        _chunk_scan(lre_ref, lim_ref, bs, car_ref, are[0], -aim[0], tt, reverse=True, on_block=on_block)
        for b in bs:
            dare_ref[...] += jnp.sum(acc[b][0], axis=0, keepdims=True)
            daim_ref[...] += jnp.sum(acc[b][1], axis=0, keepdims=True)
            dyp = dyp_ref[b]
            up = up_ref[b]
            dyb = dyp.astype(BF16)
            ub = up.astype(BF16)
            lrb = lre_ref[b].astype(BF16)
            lib = lim_ref[b].astype(BF16)
            dup = d_ref[...] * dyp + _dot_nt(lrb, bbre[0]) + _dot_nt(lib, bbim[0])
            _store_chunked(dup, du_ref, b, tt)
            dbbre_ref[0] += _dot_tn(ub, lrb)
            dbbim_ref[0] += _dot_tn(ub, lib)
            dcre_ref[0] += _dot_tn(dyb, sre_ref[b].astype(BF16))
            dcim_ref[0] += _dot_tn(dyb, sim_ref[b].astype(BF16))
            dd_ref[...] += jnp.sum(dyp * up, axis=0, keepdims=True)

        @pl.when((j == N_JBLK - 1) & (tr == nt - 1))
        def _():
            mine, sends = exchange()
            for cp in sends + mine:
                cp.wait()

    tok = lambda j, t: (0, nt - 1 - t, j)
    halo = lambda j, t: (0, jnp.maximum((nt - 1 - t) * rows8 - 1, 0), j)
    blk3 = lambda j, t: (j, 0, 0)
    row1 = lambda j, t: (0, j)
    acc_shape = _out((N_JBLK, JB_CH, JB_ST), F32)
    return _pcall(
        body, name="ssm_bwd", grid=(N_JBLK, nt),
        out_shape=(_out((n_seq, seq, SSM_W), F32), acc_shape, acc_shape, acc_shape, acc_shape,
                   _out((1, N_JBLK * JB_ST), F32), _out((1, N_JBLK * JB_ST), F32),
                   _out((1, SSM_W), F32),
                   _out((N_DEV,) + g_out.shape[1:], F32),
                   _out((N_DEV,) + g_glu.shape[1:], F32)),
        in_specs=[pl.BlockSpec((n_seq, tt, JB_CH), tok), pl.BlockSpec((n_seq, tt, JB_CH), tok),
                  pl.BlockSpec((n_seq, tt, JB_ST), tok), pl.BlockSpec((n_seq, tt, JB_ST), tok),
                  pl.BlockSpec((n_seq, SUBLANES, JB_ST), halo), pl.BlockSpec((n_seq, SUBLANES, JB_ST), halo),
                  pl.BlockSpec((1, JB_CH, JB_ST), blk3), pl.BlockSpec((1, JB_CH, JB_ST), blk3),
                  pl.BlockSpec((1, JB_CH, JB_ST), blk3), pl.BlockSpec((1, JB_CH, JB_ST), blk3),
                  pl.BlockSpec((1, JB_CH), row1), pl.BlockSpec((1, 1, JB_ST), blk3), pl.BlockSpec((1, 1, JB_ST), blk3),
                  HBM_SPEC, HBM_SPEC],
        out_specs=(pl.BlockSpec((n_seq, tt, JB_CH), tok),
                   pl.BlockSpec((1, JB_CH, JB_ST), blk3), pl.BlockSpec((1, JB_CH, JB_ST), blk3),
                   pl.BlockSpec((1, JB_CH, JB_ST), blk3), pl.BlockSpec((1, JB_CH, JB_ST), blk3),
                   pl.BlockSpec((1, JB_ST), row1), pl.BlockSpec((1, JB_ST), row1), pl.BlockSpec((1, JB_CH), row1),
                   HBM_SPEC, HBM_SPEC),
        scratch_shapes=[pltpu.VMEM((n_seq, tt, JB_ST), F32), pltpu.VMEM((n_seq, tt, JB_ST), F32),
                        pltpu.VMEM((n_seq, tt, JB_CH), F32), pltpu.VMEM((n_seq, tt, JB_CH), F32),
                        pltpu.VMEM((n_seq, SUBLANES, JB_ST), F32),
                        pltpu.SemaphoreType.DMA((7 * 2,)), pltpu.SemaphoreType.DMA((7 * 2,)),
                        pltpu.SemaphoreType.DMA((2,))],
        compiler_params=_params(2),
    )(dy, u, s_re, s_im, s_re, s_im, bb_re, bb_im, c_re_t, c_imn_t, d_row, ab_re, ab_im, g_out, g_glu)


def _mix(x2, tgt2, y, proj, gf, b_glu, conv8, w_glu_f, w_out_f, seq):
    n = x2.shape[0]
    tm = TOK_TILE
    tiles_per_seq = seq // tm
    rows8 = tm // SUBLANES

    def body(x_ref, t_ref, y_ref, zs_ref, h_ref, bc_ref, cc_ref, zc_ref, hp_ref, ccp_ref,
             gf_ref, bg_ref, cw_ref, wg_ref, wo_ref,
             dh2_ref, dy_ref, dzs_ref, dbc_ref, dzc_ref, dyc_ref,
             dwo_ref, dwg_ref, loss_ref, dgf_ref, dbg_ref, dcw_ref):
        i = pl.program_id(0)

        @pl.when(i == 0)
        def _():
            for r in (dwo_ref, dwg_ref, loss_ref, dgf_ref, dbg_ref, dcw_ref):
                r[...] = jnp.zeros_like(r)

        yv = y_ref[...]
        y1, dgelu = _gelu_and_grad(yv)
        y1b = y1.astype(BF16)
        gate = _sigmoid(_dot(y1b, wg_ref[...]) + bg_ref[...])
        y2 = y1 * gate
        szs, dszs = _silu_and_grad(zs_ref[...])
        yssm = y2 * szs
        hv = h_ref[...]
        ccv = cc_ref[...]
        bcv = bc_ref[...]
        v = ccv * hv
        first = (i % tiles_per_seq) == 0
        vhalo = jnp.where(first, 0.0, ccp_ref[...] * hp_ref[...])
        v1 = _shift_down(v, vhalo, 1)
        v2 = _shift_down(v, vhalo, 2)
        w0 = cw_ref[0:1, :]
        w1 = cw_ref[1:2, :]
        w2 = cw_ref[2:3, :]
        yc = w0 * v2 + w1 * v1 + w2 * v
        szc, dszc = _silu_and_grad(zc_ref[...])
        yconv = (bcv * yc) * szc
        ysb = yssm.astype(BF16)
        ycb = yconv.astype(BF16)
        h2 = x_ref[...] + _dot(ysb, wo_ref[0:SSM_W, :]) + _dot(ycb, wo_ref[SSM_W:, :])
        r2 = lax.rsqrt(jnp.mean(h2 * h2, axis=-1, keepdims=True) + EPS)
        hn = h2 * r2
        gfv = gf_ref[...]
        err = hn * gfv - t_ref[...]
        loss_ref[...] += 0.5 * jnp.sum(jnp.mean(err * err, axis=-1, keepdims=True))
        dout = err * (1.0 / D_MODEL)
        dgf_ref[...] += jnp.sum(dout * hn, axis=0, keepdims=True)
        dn = dout * gfv
        dh2 = r2 * (dn - hn * jnp.mean(dn * hn, axis=-1, keepdims=True))
        dh2_ref[...] = dh2
        dh2b = dh2.astype(BF16)
        dwo_ref[0:SSM_W, :] += _dot_tn(ysb, dh2b)
        dwo_ref[SSM_W:, :] += _dot_tn(ycb, dh2b)
        dyssm = _dot_nt(dh2b, wo_ref[0:SSM_W, :])
        dyconv = _dot_nt(dh2b, wo_ref[SSM_W:, :])
        dy2 = dyssm * szs
        dzs_ref[...] = (dyssm * y2 * dszs).astype(BF16)
        dgp = dy2 * y1 * (gate * (1.0 - gate))
        dgpb = dgp.astype(BF16)
        dy1 = dy2 * gate + _dot_nt(dgpb, wg_ref[...])
        dwg_ref[...] += _dot_tn(y1b, dgpb)
        dbg_ref[...] += jnp.sum(dgp, axis=0, keepdims=True)
        dy_ref[...] = dy1 * dgelu
        dbc_ref[...] = (dyconv * yc * szc).astype(BF16)
        dyc = dyconv * bcv * szc
        dyc_ref[...] = dyc
        dzc_ref[...] = (dyconv * bcv * yc * dszc).astype(BF16)
        dcw_ref[0:1, :] += jnp.sum(dyc * v2, axis=0, keepdims=True)
        dcw_ref[1:2, :] += jnp.sum(dyc * v1, axis=0, keepdims=True)
        dcw_ref[2:3, :] += jnp.sum(dyc * v, axis=0, keepdims=True)

    tile_d = pl.BlockSpec((tm, D_MODEL), lambda i: (i, 0))
    tile_s = pl.BlockSpec((tm, SSM_W), lambda i: (i, 0))
    seg_of = lambda c: pl.BlockSpec((tm, SSM_W), lambda i: (i, c))
    halo_of = lambda c: pl.BlockSpec((SUBLANES, SSM_W), lambda i: (jnp.maximum(i * rows8 - 1, 0), c))
    const = lambda shape: pl.BlockSpec(shape, lambda i: (0,) * len(shape))
    seg = _out((n, SSM_W), F32)
    seg_b = _out((n, SSM_W), BF16)
    return _pcall(
        body, name="mix", grid=(n // tm,),
        out_shape=(_out((n, D_MODEL), F32), seg, seg_b, seg_b, seg_b, seg,
                   _out((D_MODEL, D_MODEL), F32), _out((SSM_W, SSM_W), F32),
                   _out((SUBLANES, LANES), F32), _out((1, D_MODEL), F32),
                   _out((1, SSM_W), F32), _out((SUBLANES, CONV_W), F32)),
        in_specs=[tile_d, tile_d, tile_s, seg_of(SEG_ZS), seg_of(SEG_H), seg_of(SEG_BC), seg_of(SEG_CC), seg_of(SEG_ZC),
                  halo_of(SEG_H), halo_of(SEG_CC),
                  const((1, D_MODEL)), const((1, SSM_W)), const((SUBLANES, CONV_W)),
                  const((SSM_W, SSM_W)), const((D_MODEL, D_MODEL))],
        out_specs=(tile_d, tile_s, tile_s, tile_s, tile_s, tile_s,
                   const((D_MODEL, D_MODEL)), const((SSM_W, SSM_W)), const((SUBLANES, LANES)),
                   const((1, D_MODEL)), const((1, SSM_W)), const((SUBLANES, CONV_W))),
        compiler_params=_params(1),
    )(x2, tgt2, y, proj, proj, proj, proj, proj, proj, proj, gf, b_glu, conv8, w_glu_f, w_out_f)


def _in_bwd(x2, dh2, du, dzs, dyc, proj, dbc, dzc, g1, conv8, w_full, seq):
    n = x2.shape[0]
    tm = TOK_TILE
    n_tiles = n // tm
    tiles_per_seq = seq // tm
    rows8 = tm // SUBLANES
    n_blk8 = n // SUBLANES

    ring = 3

    def body(x_hbm, dh2_hbm, du_ref, dzs_ref, dyc_ref, dycn_ref, h_ref, cc_ref, dbc_ref, dzc_ref,
             g_ref, cw_ref, w_ref, gx_ref, dp_ref, dg_ref, x_ring, dh2_ring, x_sems, dh2_sems):
        i = pl.program_id(0)

        def fetch(step):
            slot = step % ring
            rows = pl.ds(pl.multiple_of(step * tm, tm), tm)
            return [pltpu.make_async_copy(x_hbm.at[rows, :], x_ring.at[slot], x_sems.at[slot]),
                    pltpu.make_async_copy(dh2_hbm.at[rows, :], dh2_ring.at[slot], dh2_sems.at[slot])]

        @pl.when(i == 0)
        def _():
            dg_ref[...] = jnp.zeros_like(dg_ref)
            for step in range(min(ring - 1, n_tiles)):
                for cp in fetch(step):
                    cp.start()

        @pl.when(i + ring - 1 < n_tiles)
        def _():
            for cp in fetch(i + ring - 1):
                cp.start()

        dyc = dyc_ref[...]
        last = (i % tiles_per_seq) == tiles_per_seq - 1
        nhalo = jnp.where(last, 0.0, dycn_ref[...])
        dv = (cw_ref[2:3, :] * dyc + cw_ref[1:2, :] * _shift_up(dyc, nhalo, 1)
              + cw_ref[0:1, :] * _shift_up(dyc, nhalo, 2))
        parts = (du_ref[...], dzs_ref[...], dv * cc_ref[...], dbc_ref[...], dv * h_ref[...], dzc_ref[...])
        dxn = jnp.zeros((tm, D_MODEL), F32)
        for k, p in enumerate(parts):
            pb = p.astype(BF16)
            dp_ref[:, k * SSM_W:(k + 1) * SSM_W] = pb
            dxn = dxn + _dot_nt(pb, w_ref[:, k * SSM_W:(k + 1) * SSM_W])
        for cp in fetch(i):
            cp.wait()
        x = x_ring[i % ring]
        r = lax.rsqrt(jnp.mean(x * x, axis=-1, keepdims=True) + EPS)
        xh = x * r
        dg_ref[...] += jnp.sum(dxn * xh, axis=0, keepdims=True)
        dn = dxn * g_ref[...]
        gx_ref[...] = dh2_ring[i % ring] + r * (dn - xh * jnp.mean(dn * xh, axis=-1, keepdims=True))

    tile_d = pl.BlockSpec((tm, D_MODEL), lambda i: (i, 0))
    tile_s = pl.BlockSpec((tm, SSM_W), lambda i: (i, 0))
    seg_of = lambda c: pl.BlockSpec((tm, SSM_W), lambda i: (i, c))
    nhalo = pl.BlockSpec((SUBLANES, SSM_W), lambda i: (jnp.minimum((i + 1) * rows8, n_blk8 - 1), 0))
    const = lambda shape: pl.BlockSpec(shape, lambda i: (0,) * len(shape))
    return _pcall(
        body, name="in_bwd", grid=(n_tiles,),
        out_shape=(_out((n, D_MODEL), F32), _out((n, IN_COLS), BF16),
                   _out((SUBLANES, D_MODEL), F32)),
        in_specs=[HBM_SPEC, HBM_SPEC, tile_s, tile_s, tile_s, nhalo, seg_of(SEG_H), seg_of(SEG_CC), tile_s, tile_s,
                  const((1, D_MODEL)), const((SUBLANES, CONV_W)), const((D_MODEL, IN_COLS))],
        out_specs=(tile_d, pl.BlockSpec((tm, IN_COLS), lambda i: (i, 0)), const((SUBLANES, D_MODEL))),
        scratch_shapes=[pltpu.VMEM((ring, tm, D_MODEL), F32), pltpu.VMEM((ring, tm, D_MODEL), F32),
                        pltpu.SemaphoreType.DMA((ring,)), pltpu.SemaphoreType.DMA((ring,))],
        compiler_params=_params(1),
    )(x2, dh2, du, dzs, dyc, dyc, proj, proj, dbc, dzc, g1, conv8, w_full)


_HALF_BLOCKS = ((0, 0), (0, 1), (1, 0), (2, 0), (1, 1), (2, 1), (3, 0), (3, 1))


def _dw_in_exchange(chips, xn, dproj, smalls):
    n = xn.shape[0]
    tk = min(1024, n)
    nk = n // tk
    piece = (D_MODEL, COLS_PER_DEV)
    hr = D_MODEL // 2
    n_half = len(_HALF_BLOCKS)
    n_small = len(smalls)
    assert _HALF_BLOCKS[0][1] == 0 and _HALF_BLOCKS[1][1] == 1
    order = jnp.stack([chips[b] for b, _ in _HALF_BLOCKS]
                      + [jnp.int32(t) for _, t in _HALF_BLOCKS]).astype(jnp.int32)

    def body(order_ref, xn_hbm, dp_ref, *refs):
        sm_refs = refs[:n_small]
        own_ref, rchip_ref = refs[n_small:n_small + 2]
        rsm_refs = refs[n_small + 2:2 * n_small + 2]
        (xn_ref, acc, stage, rbuf, kbuf, relay_in, xn_sems, give_send, give_recv, keep_send, keep_recv,
         relay_send, relay_recv, sm_send, sm_recv, sm_loc) = refs[2 * n_small + 2:]
        s = pl.program_id(0)

        def xn_copy(kk, t):
            rows = pl.ds(pl.multiple_of(kk * tk, tk), tk)
            return pltpu.make_async_copy(xn_hbm.at[rows, t * hr:(t + 1) * hr], xn_ref.at[t, rows, :],
                                         xn_sems.at[2 * kk + t])

        @pl.when(s == 0)
        def _():
            for kk in range(nk):
                for t in range(2):
                    xn_copy(kk, t).start()
            xn_copy(0, 0).wait()

        @pl.when(s == 1)
        def _():
            xn_copy(0, 1).wait()

        x, y, c = _mesh_pos()
        sib = (x, y, 1 - c)
        y_nbr, x_nbr = (x, 1 - y, c), (1 - x, y, c)
        gather = _TwoLevelGather(list(sm_refs), [functools.partial(lambda r, dev: r.at[dev], r) for r in rsm_refs],
                                 sm_send, sm_recv, sm_loc)

        def give(h):
            cols = pl.ds(pl.multiple_of((1 - c) * COLS_PER_DEV, LANES), COLS_PER_DEV)
            return pltpu.make_async_remote_copy(src_ref=acc.at[h % 2, :, cols], dst_ref=stage.at[h],
                                                send_sem=give_send.at[h], recv_sem=give_recv.at[h],
                                                device_id=sib, device_id_type=MESH)

        def relay(r):
            return pltpu.make_async_remote_copy(src_ref=rbuf.at[r], dst_ref=relay_in.at[r],
                                                send_sem=relay_send.at[r], recv_sem=relay_recv.at[r],
                                                device_id=(x_nbr, y_nbr)[r], device_id_type=MESH)

        def keep(q):
            return pltpu.make_async_remote_copy(src_ref=kbuf.at[q], dst_ref=rchip_ref.at[q // 2, pl.ds((q % 2) * hr, hr), :],
                                                send_sem=keep_send.at[q], recv_sem=keep_recv.at[q],
                                                device_id=(y_nbr, x_nbr)[q // 2], device_id_type=MESH)

        def chip_sum(h):
            give(h).wait_recv()
            mine = [acc[h % 2, :, cc * COLS_PER_DEV:(cc + 1) * COLS_PER_DEV] for cc in range(2)]
            return jnp.where(c == 0, mine[0], mine[1]) + stage[h]

        @pl.when(s == 0)
        def _():
            gather.start()

        @pl.when(s == 2)
        def _():
            gather.neighbours_landed()

        @pl.when(s == n_half - 2)
        def _():
            gather.diagonal_landed()

        for k in range(2, n_half):
            @pl.when(s == k)
            def _(k=k):
                give(k - 2).wait_send()

        slot = s % 2
        t_half = order_ref[n_half + s]
        acc[slot] = _dot_tn(xn_ref[t_half, pl.ds(0, tk), :], dp_ref[pl.ds(0, tk), :])

        def kstep(kk, carry):
            for t in range(2):
                @pl.when(s == t)
                def _(t=t):
                    xn_copy(kk, t).wait()

            off = pl.multiple_of(kk * tk, tk)
            acc[slot] += _dot_tn(xn_ref[t_half, pl.ds(off, tk), :], dp_ref[pl.ds(off, tk), :])
            return carry

        n_first = max(1, (3 * nk) // 4)
        lax.fori_loop(1, n_first, kstep, 0)
        for k in range(1, n_half):
            @pl.when(s == k)
            def _(k=k):
                h = k - 1
                b, t = _HALF_BLOCKS[h]
                total = chip_sum(h)
                if b == 0:
                    rbuf[t] = total.astype(BF16)
                    relay(t).start()
                elif b < 3:
                    if (b, t) in ((1, 0), (2, 1)):
                        relay(t).wait_recv()
                        total = total + relay_in[t].astype(F32)
                    q = 2 * (b - 1) + t
                    kbuf[q] = total.astype(BF16)
                    keep(q).start()
                else:
                    own_ref[0:hr, :] = total

        lax.fori_loop(n_first, nk, kstep, 0)

        for k in range(n_half):
            @pl.when(s == k)
            def _(k=k):
                give(k).start()

        @pl.when(s == n_half - 1)
        def _():
            own_ref[hr:D_MODEL, :] = chip_sum(n_half - 1)
            give(n_half - 2).wait_send()
            give(n_half - 1).wait_send()
            for r in range(2):
                relay(r).wait_send()
            for q in range(4):
                keep(q).wait()
            gather.finish()

    half_piece = (hr, COLS_PER_DEV)
    grid_spec = pltpu.PrefetchScalarGridSpec(
        num_scalar_prefetch=1, grid=(n_half,),
        in_specs=[HBM_SPEC,
                  pl.BlockSpec((n, COLS_PER_CHIP), lambda s, order: (0, order[s])),
                  *([HBM_SPEC] * n_small)],
        out_specs=(pl.BlockSpec(piece, lambda s, order: (0, 0)), HBM_SPEC, *([HBM_SPEC] * n_small)),
        scratch_shapes=[pltpu.VMEM((2, n, hr), BF16),
                        pltpu.VMEM((2, hr, COLS_PER_CHIP), F32), pltpu.VMEM((n_half,) + half_piece, F32),
                        pltpu.VMEM((2,) + half_piece, BF16), pltpu.VMEM((4,) + half_piece, BF16),
                        pltpu.VMEM((2,) + half_piece, BF16),
                        pltpu.SemaphoreType.DMA((2 * nk,)),
                        pltpu.SemaphoreType.DMA((n_half,)), pltpu.SemaphoreType.DMA((n_half,)),
                        pltpu.SemaphoreType.DMA((4,)), pltpu.SemaphoreType.DMA((4,)),
                        pltpu.SemaphoreType.DMA((2,)), pltpu.SemaphoreType.DMA((2,)),
                        pltpu.SemaphoreType.DMA((7 * n_small,)), pltpu.SemaphoreType.DMA((7 * n_small,)),
                        pltpu.SemaphoreType.DMA((n_small,))])
    return _pcall(
        body, name="dw_in_exchange", grid_spec=grid_spec,
        out_shape=(_out(piece, F32), _out((2,) + piece, BF16),
                   *(_out((N_DEV,) + a.shape, a.dtype) for a in smalls)),
        compiler_params=_params(1),
    )(order, xn, dproj, *smalls)


def _adamw(g, w, m, v):
    m_new = ADAM_B1 * m + (1.0 - ADAM_B1) * g
    v_new = ADAM_B2 * v + (1.0 - ADAM_B2) * (g * g)
    m_hat = m_new / (1.0 - ADAM_B1 ** ADAM_STEP)
    v_hat = v_new / (1.0 - ADAM_B2 ** ADAM_STEP)
    delta = -ADAM_LR * (m_hat / (jnp.sqrt(v_hat) + ADAM_EPS) + ADAM_WD * w)
    return delta, m_new, v_new


def _reduce_adam_w_in(own, rchip, w, m, v):
    rows, cols = w.shape
    row_tile = 256

    def body(o_ref, r_ref, w_ref, m_ref, v_ref, g_ref, d_ref, nm_ref, nv_ref):
        g = o_ref[...]
        for s in range(2):
            g = g + r_ref[s].astype(F32)
        g_ref[...] = g
        d_ref[...], nm_ref[...], nv_ref[...] = _adamw(g, w_ref[...], m_ref[...], v_ref[...])

    tile = pl.BlockSpec((row_tile, cols), lambda i: (i, 0))
    shp = _out((rows, cols), F32)
    return _pcall(
        body, name="reduce_adam_w_in", grid=(rows // row_tile,),
        out_shape=(shp,) * 4,
        in_specs=[tile, pl.BlockSpec((2, row_tile, cols), lambda i: (0, i, 0)), tile, tile, tile],
        out_specs=(tile,) * 4,
        compiler_params=_params(1),
    )(own, rchip, w, m, v)


_SMALL_LEAVES = ("norm_gain", "final_norm_gain", "b_glu", "ssm_a_re", "ssm_a_im", "ssm_log_dt", "ssm_d", "conv_w",
                 "ssm_c_re", "ssm_c_im", "ssm_b_re", "ssm_b_im")


def _reduce_adam_small(r_pack, r_gc, r_gb, wmv, sharded):
    n_leaf = len(_SMALL_LEAVES)
    n_sh = len(sharded)

    def body(*refs):
        rp_ref, rgc_ref, rgb_ref = refs[:3]
        w_refs = refs[3:3 + 3 * n_leaf]
        sh_in = refs[3 + 3 * n_leaf:3 + 3 * n_leaf + 4 * n_sh]
        outs0 = 3 + 3 * n_leaf + 4 * n_sh
        loss_ref = refs[outs0]
        o_refs = refs[outs0 + 1:outs0 + 1 + 4 * n_leaf]
        sh_out = refs[outs0 + 1 + 4 * n_leaf:outs0 + 1 + 4 * n_leaf + 4 * n_sh]
        own_conv = refs[-1]

        def total(ref):
            acc = ref[0].astype(F32)
            for s in range(1, N_DEV):
                acc = acc + ref[s].astype(F32)
            return acc

        for i in range(n_sh):
            r_ref, w_ref, m_ref, v_ref = sh_in[4 * i:4 * i + 4]
            o_g, o_d, o_m, o_v = sh_out[4 * i:4 * i + 4]
            g = total(r_ref)
            o_g[...] = g
            o_d[...], o_m[...], o_v[...] = _adamw(g, w_ref[...], m_ref[...], v_ref[...])

        sp = total(rp_ref)
        sgc = total(rgc_ref)
        sgb = total(rgb_ref)
        loss_ref[...] = sp[ROW_LOSS:ROW_LOSS + 1, 0:1]

        def wide(r):
            return jnp.concatenate([sp[r:r + 1, :], sp[r + 1:r + 2, :]], axis=1)

        s5 = slice(ROW_S5, ROW_S5 + N_GROUPS)
        eye = (lax.broadcasted_iota(jnp.int32, (N_GROUPS, N_GROUPS), 0)
               == lax.broadcasted_iota(jnp.int32, (N_GROUPS, N_GROUPS), 1)).astype(F32)
        d_rows = jnp.broadcast_to(sp[ROW_BGLU_D + 1:ROW_BGLU_D + 2, :], (GROUP, SSM_W))
        own_p = (lax.broadcasted_iota(jnp.int32, (GROUP, SSM_W), 1) % GROUP
                 == lax.broadcasted_iota(jnp.int32, (GROUP, SSM_W), 0))
        of_group = (lax.broadcasted_iota(jnp.int32, (SSM_W, N_GROUPS), 0) // GROUP
                    == lax.broadcasted_iota(jnp.int32, (SSM_W, N_GROUPS), 1)).astype(BF16)
        d_pg = sum(_dot(t, of_group) for t in _split3(jnp.where(own_p, d_rows, 0.0)))
        me = 4 * lax.axis_index("x") + 2 * lax.axis_index("y") + lax.axis_index("c")
        for k in range(N_DEV):
            @pl.when(me == k)
            def _(k=k):
                own_conv[...] = sp[ROW_CONV:ROW_CONV + SUBLANES, k * CONV_COLS_PER_DEV:(k + 1) * CONV_COLS_PER_DEV]
        grads = {
            "norm_gain": wide(ROW_NORM_GAIN),
            "final_norm_gain": wide(ROW_FINAL_GAIN),
            "b_glu": sp[ROW_BGLU_D:ROW_BGLU_D + 1, :],
            "ssm_a_re": sp[s5, LANE_A_RE:LANE_A_RE + STATE],
            "ssm_a_im": sp[s5, LANE_A_IM:LANE_A_IM + STATE],
            "ssm_log_dt": jnp.sum(sp[s5, LANE_LOG_DT:LANE_LOG_DT + 1] * eye, axis=0, keepdims=True),
            "ssm_d": d_pg,
            "ssm_c_re": sgc[:, 0:STATE],
            "ssm_c_im": sgc[:, STATE:2 * STATE],
            "ssm_b_re": sgb[:, 0:STATE],
            "ssm_b_im": sgb[:, STATE:2 * STATE],
        }
        for i, name in enumerate(_SMALL_LEAVES):
            w_ref, m_ref, v_ref = w_refs[3 * i:3 * i + 3]
            o_g, o_d, o_m, o_v = o_refs[4 * i:4 * i + 4]
            if name == "conv_w":
                for k in range(w_ref.shape[0]):
                    g = own_conv[k:k + 1, :]
                    o_g[k] = g
                    o_d[k], o_m[k], o_v[k] = _adamw(g, w_ref[k], m_ref[k], v_ref[k])
                continue
            g = grads[name]
            o_g[...] = g
            o_d[...], o_m[...], o_v[...] = _adamw(g, w_ref[...], m_ref[...], v_ref[...])

    flat_w = [a for name in _SMALL_LEAVES for a in wmv[name]]
    leaf_shapes = [_out(wmv[name][0].shape, F32) for name in _SMALL_LEAVES for _ in range(4)]
    sh_shapes = [_out(entry[1].shape, F32) for entry in sharded for _ in range(4)]
    operands = (r_pack, r_gc, r_gb, *flat_w, *(a for entry in sharded for a in entry))
    out_shape = (_out((1, 1), F32), *leaf_shapes, *sh_shapes)
    outs = _pcall(
        body, name="reduce_adam_small", grid=(1,), out_shape=out_shape,
        in_specs=_whole_specs(operands), out_specs=tuple(_whole_specs(out_shape)),
        scratch_shapes=[pltpu.VMEM((SUBLANES, CONV_COLS_PER_DEV), F32)],
        compiler_params=_params(1),
    )(*operands)
    leaves = {name: outs[1 + 4 * i:5 + 4 * i] for i, name in enumerate(_SMALL_LEAVES)}
    first = 1 + 4 * n_leaf
    return outs[0], leaves, [outs[first + 4 * i:first + 4 * i + 4] for i in range(n_sh)]


def kernel(x, norm_gain, w_in, ssm_a_re, ssm_a_im, ssm_log_dt, ssm_b_re, ssm_b_im, ssm_c_re, ssm_c_im, ssm_d, w_glu, b_glu, conv_w, w_out, final_norm_gain, loss_target, m_norm_gain, m_w_in, m_ssm_a_re, m_ssm_a_im, m_ssm_log_dt, m_ssm_b_re, m_ssm_b_im, m_ssm_c_re, m_ssm_c_im, m_ssm_d, m_w_glu, m_b_glu, m_conv_w, m_w_out, m_final_norm_gain, v_norm_gain, v_w_in, v_ssm_a_re, v_ssm_a_im, v_ssm_log_dt, v_ssm_b_re, v_ssm_b_im, v_ssm_c_re, v_ssm_c_im, v_ssm_d, v_w_glu, v_b_glu, v_conv_w, v_w_out, v_final_norm_gain):
    n_seq, seq, _ = x.shape
    n = n_seq * seq

    gh_p = lambda b4: jnp.transpose(b4, (0, 1, 3, 2)).reshape(N_GROUPS * GROUP, STATE)
    c2 = lambda a: a.reshape(N_GROUPS * GROUP, STATE)
    b_re2, b_im2 = gh_p(ssm_b_re), gh_p(ssm_b_im)
    d_row = ssm_d[0].reshape(1, SSM_W)

    x2 = x.reshape(n, D_MODEL)
    tgt2 = loss_target.reshape(n, D_MODEL)
    mx, my, mc = lax.axis_index("x"), lax.axis_index("y"), lax.axis_index("c")
    chip_ids = [2 * cx + cy for cx, cy in ((mx, my), (1 - mx, my), (mx, 1 - my), (1 - mx, 1 - my))]
    arrival = chip_ids
    xn, proj, w_in_f, s5 = _in_proj(
        jnp.stack(arrival).astype(jnp.int32), x2, norm_gain, w_in[0],
        (ssm_a_re[0], ssm_a_im[0], ssm_log_dt, b_re2, b_im2, c2(ssm_c_re), c2(ssm_c_im)))
    a_re_x, a_im_x, log_dt_x, ab_re, ab_im, bb_re_m, bb_im_m, c_re_m, c_imn_m = s5
    u3 = proj.reshape(n_seq, seq, IN_COLS)
    conv_p = jnp.pad(conv_w[0], ((0, SUBLANES - 3), (0, LANES - CONV_COLS_PER_DEV)))
    s_re, s_im, y3, w_out_f, w_glu_f, conv_all = _ssm_fwd(
        u3, bb_re_m, bb_im_m, c_re_m, c_imn_m, d_row, ab_re, ab_im,
        w_out[0], w_glu[0], conv_p, n_seq, seq)
    conv8 = jnp.transpose(conv_all[:, :, :CONV_COLS_PER_DEV], (1, 0, 2)).reshape(SUBLANES, CONV_W)
    (dh2, dy, dzs, dbc, dzc, dyc, dw_out, dw_glu, loss_t, dgf, dbg, dcw) = _mix(
        x2, tgt2, y3.reshape(n, SSM_W), proj, final_norm_gain.reshape(1, D_MODEL), b_glu, conv8,
        w_glu_f, w_out_f, seq)

    du3, dc_re_d, dc_im_d, dbb_re_d, dbb_im_d, dab_re, dab_im, dd, r_out, r_glu = _ssm_bwd(
        dy.reshape(n_seq, seq, SSM_W), u3, s_re, s_im, bb_re_m, bb_im_m, c_re_m, c_imn_m, d_row, ab_re, ab_im,
        dw_out.reshape(N_DEV, OUT_ROWS_PER_DEV, D_MODEL), dw_glu.reshape(N_DEV, GLU_ROWS_PER_DEV, SSM_W), n_seq, seq)
    du = du3.reshape(n, SSM_W)
    grad_x2, dproj, dg8 = _in_bwd(x2, dh2, du, dzs, dyc, proj, dbc, dzc, norm_gain, conv8, w_in_f, seq)
    pack, gc, gb = _ssm_disc_bwd_pack(
        a_re_x, a_im_x, log_dt_x, b_re2, b_im2, dab_re, dab_im,
        dbb_re_d, dbb_im_d, loss_t, dg8, dgf, dbg, dd, dcw, dc_re_d, dc_im_d)

    own_in, rchip_in, r_pack, r_gc, r_gb = _dw_in_exchange(
        [chip_ids[3], chip_ids[2], chip_ids[1], chip_ids[0]],
        xn, dproj, [pack, gc, gb])

    flat2 = lambda a: a.reshape(a.shape[-2:]) if a.ndim > 2 else a.reshape(1, -1)
    c2 = lambda a: a.reshape(N_GROUPS * GROUP, STATE)
    wmv = dict(norm_gain=(norm_gain, m_norm_gain, v_norm_gain),
               final_norm_gain=tuple(flat2(a) for a in (final_norm_gain, m_final_norm_gain, v_final_norm_gain)),
               b_glu=(b_glu, m_b_glu, v_b_glu),
               ssm_a_re=tuple(flat2(a) for a in (ssm_a_re, m_ssm_a_re, v_ssm_a_re)),
               ssm_a_im=tuple(flat2(a) for a in (ssm_a_im, m_ssm_a_im, v_ssm_a_im)),
               ssm_log_dt=(ssm_log_dt, m_ssm_log_dt, v_ssm_log_dt),
               ssm_d=tuple(jnp.transpose(a, (0, 2, 1)).reshape(GROUP, N_GROUPS) for a in (ssm_d, m_ssm_d, v_ssm_d)),
               conv_w=tuple(jnp.transpose(a, (1, 0, 2)) for a in (conv_w, m_conv_w, v_conv_w)),
               ssm_c_re=tuple(c2(a) for a in (ssm_c_re, m_ssm_c_re, v_ssm_c_re)),
               ssm_c_im=tuple(c2(a) for a in (ssm_c_im, m_ssm_c_im, v_ssm_c_im)),
               ssm_b_re=(b_re2, gh_p(m_ssm_b_re), gh_p(v_ssm_b_re)),
               ssm_b_im=(b_im2, gh_p(m_ssm_b_im), gh_p(v_ssm_b_im)))

    res_in = _reduce_adam_w_in(own_in, rchip_in, w_in[0], m_w_in[0], v_w_in[0])
    loss11, small, (res_out, res_glu) = _reduce_adam_small(
        r_pack, r_gc, r_gb, wmv,
        [(r_out, w_out[0], m_w_out[0], v_w_out[0]), (r_glu, w_glu[0], m_w_glu[0], v_w_glu[0])])
    loss = loss11.reshape(())

    shapes = dict(norm_gain=(1, D_MODEL), ssm_a_re=(1, N_GROUPS, STATE), ssm_a_im=(1, N_GROUPS, STATE),
                  ssm_log_dt=(1, N_GROUPS), ssm_c_re=(1, N_GROUPS, GROUP, STATE), ssm_c_im=(1, N_GROUPS, GROUP, STATE),
                  b_glu=(1, SSM_W), final_norm_gain=(D_MODEL,))
    big = dict(w_in=res_in, w_glu=res_glu, w_out=res_out)

    def leaf(kind, name):
        if name in big:
            return big[name][kind][None]
        if name in ("ssm_b_re", "ssm_b_im"):
            return jnp.transpose(small[name][kind].reshape(1, N_GROUPS, GROUP, STATE), (0, 1, 3, 2))
        if name == "ssm_d":
            return jnp.transpose(small[name][kind].reshape(1, GROUP, N_GROUPS), (0, 2, 1))
        if name == "conv_w":
            return jnp.transpose(small[name][kind], (1, 0, 2))
        return small[name][kind].reshape(shapes[name])

    order = ["norm_gain", "w_in", "ssm_a_re", "ssm_a_im", "ssm_log_dt", "ssm_b_re", "ssm_b_im", "ssm_c_re",
             "ssm_c_im", "ssm_d", "w_glu", "b_glu", "conv_w", "w_out", "final_norm_gain"]
    outs = [loss, grad_x2.reshape(x.shape)]
    for kind in range(4):
        outs += [leaf(kind, name) for name in order]
    return tuple(outs)
```

```python
import functools
import math

import jax
import jax.numpy as jnp
from jax import lax
from jax.experimental import pallas as pl
from jax.experimental.pallas import tpu as pltpu

F32 = jnp.float32
BF16 = jnp.bfloat16

N_DEV = 8
D_MODEL = 1024
SSM_W = 512
CONV_W = 512
N_GROUPS = 32
GROUP = 16
STATE = 64
IN_COLS = 3072
SEG_U, SEG_ZS, SEG_H, SEG_BC, SEG_CC, SEG_ZC = range(6)
COLS_PER_DEV = IN_COLS // N_DEV
N_CHIP = N_DEV // 2
COLS_PER_CHIP = 2 * COLS_PER_DEV
OUT_ROWS_PER_DEV = D_MODEL // N_DEV
GLU_ROWS_PER_DEV = SSM_W // N_DEV
CONV_COLS_PER_DEV = CONV_W // N_DEV
EPS = 1e-6

N_JBLK = 4
JB_CH = SSM_W // N_JBLK
JB_ST = N_GROUPS * STATE // N_JBLK

ADAM_LR = 0.001
ADAM_B1 = 0.9
ADAM_B2 = 0.999
ADAM_EPS = 1e-08
ADAM_WD = 0.01
ADAM_STEP = 10

SUBLANES = 8
LANES = 128
VMEM_LIMIT = 48 * 1024 * 1024
TOK_TILE = 256
IN_TILE = 1024
SCAN_TILE = 1024

MESH = pl.DeviceIdType.MESH
HBM_SPEC = pl.BlockSpec(memory_space=pltpu.HBM)


def _build(body, **kw):
    return pl.pallas_call(body, **kw)


def _pcall(body, **kw):
    def call(*operands):
        pinned = [a if jnp.issubdtype(a.dtype, jnp.integer) else pltpu.with_memory_space_constraint(a, pltpu.HBM)
                  for a in operands]
        return _build(body, **kw)(*pinned)
    return call


def _whole_specs(arrays):
    return [pl.BlockSpec(a.shape, functools.partial(lambda nd, i: (0,) * nd, len(a.shape))) for a in arrays]


def _out(shape, dtype):
    return pltpu.HBM(tuple(shape), dtype)


def _params(n_grid):
    return pltpu.CompilerParams(dimension_semantics=("arbitrary",) * n_grid,
                                vmem_limit_bytes=VMEM_LIMIT)


def _dot(a, b):
    return jnp.dot(a, b, preferred_element_type=F32)


def _dot_nt(a, b):
    return lax.dot_general(a, b, (((1,), (1,)), ((), ())), preferred_element_type=F32)


def _dot_tn(a, b):
    return lax.dot_general(a, b, (((0,), (0,)), ((), ())), preferred_element_type=F32)


def _sigmoid(z):
    return 1.0 / (1.0 + jnp.exp(-z))


_GELU_C = math.sqrt(2.0 / math.pi)


def _gelu_and_grad(y):
    inner = _GELU_C * (y + 0.044715 * (y * y * y))
    t = jnp.tanh(inner)
    g = 0.5 * y * (1.0 + t)
    dg = 0.5 * (1.0 + t) + 0.5 * y * (1.0 - t * t) * (_GELU_C * (1.0 + 3.0 * 0.044715 * (y * y)))
    return g, dg


def _silu_and_grad(z):
    s = _sigmoid(z)
    return z * s, s * (1.0 + z * (1.0 - s))


def _shift_down(v, halo, k):
    rolled = pltpu.roll(v, k, 0)
    row = lax.broadcasted_iota(jnp.int32, v.shape, 0)
    for r in range(k):
        rolled = jnp.where(row == r, halo[SUBLANES - k + r:SUBLANES - k + r + 1, :], rolled)
    return rolled


def _shift_up(v, halo, k):
    n = v.shape[0]
    rolled = pltpu.roll(v, n - k, 0)
    row = lax.broadcasted_iota(jnp.int32, v.shape, 0)
    for r in range(k):
        rolled = jnp.where(row == n - k + r, halo[r:r + 1, :], rolled)
    return rolled


def _mesh_pos():
    return lax.axis_index("x"), lax.axis_index("y"), lax.axis_index("c")


def _direct_copies(srcs_for, out_refs, send_sems, recv_sems, loc_sems):
    x, y, c = _mesh_pos()
    me_id = 4 * x + 2 * y + c
    n_arr = len(out_refs)
    dsts = [r.at[me_id] for r in out_refs]
    own = srcs_for(me_id)
    mine = [pltpu.make_async_copy(own[a], dsts[a], loc_sems.at[a]) for a in range(n_arr)]
    sends = []
    for k in range(1, N_DEV):
        px, py, pc = x ^ ((k >> 2) & 1), y ^ ((k >> 1) & 1), c ^ (k & 1)
        src = srcs_for(4 * px + 2 * py + pc)
        for a in range(n_arr):
            sends.append(pltpu.make_async_remote_copy(
                src_ref=src[a], dst_ref=dsts[a],
                send_sem=send_sems.at[(k - 1) * n_arr + a], recv_sem=recv_sems.at[(k - 1) * n_arr + a],
                device_id=(px, py, pc), device_id_type=MESH))
    return mine, sends


class _TwoLevelGather:
    def __init__(self, srcs, slots, send_sems, recv_sems, loc_sems):
        self.srcs, self.slots, self.n_arr = srcs, slots, len(srcs)
        self.send_sems, self.recv_sems, self.loc_sems = send_sems, recv_sems, loc_sems
        x, y, c = _mesh_pos()
        self.c = c
        self.me, self.sib = (x, y, c), (x, y, 1 - c)
        self.chips = [(1 - x, y), (x, 1 - y), (1 - x, 1 - y)]

    def _copies(self, k, block, to, from_src=False):
        dev = 4 * block[0] + 2 * block[1] + block[2]
        return [pltpu.make_async_remote_copy(
            src_ref=self.srcs[a] if from_src else self.slots[a](dev), dst_ref=self.slots[a](dev),
            send_sem=self.send_sems.at[k * self.n_arr + a], recv_sem=self.recv_sems.at[k * self.n_arr + a],
            device_id=to, device_id_type=MESH) for a in range(self.n_arr)]

    def _local(self):
        dev = 4 * self.me[0] + 2 * self.me[1] + self.me[2]
        return [pltpu.make_async_copy(self.srcs[a], self.slots[a](dev), self.loc_sems.at[a])
                for a in range(self.n_arr)]

    def start(self):
        for cp in self._local() + self._copies(0, self.me, self.sib, True):
            cp.start()
        for j in (0, 1):
            for cp in self._copies(1 + j, self.me, (*self.chips[j], self.c), True):
                cp.start()

    def wait_own(self):
        for cp in self._local():
            cp.wait()

    def wait_sibling(self):
        for cp in self._copies(0, self.sib, self.me):
            cp.wait_recv()

    def wait_and_pass_on(self, j):
        chip = self.chips[j]
        for cp in self._copies(1 + j, (*chip, self.c), self.me):
            cp.wait_recv()
        for cp in self._copies(4 + j, (*chip, self.c), self.sib):
            cp.start()

    def neighbours_landed(self):
        x, y, c = self.me
        self.wait_and_pass_on(0)
        self.wait_and_pass_on(1)
        for cp in self._copies(1 + 2, (x ^ c, y ^ (1 - c), c), (x ^ (1 - c), y ^ c, c)):
            cp.start()

    def diagonal_landed(self):
        self.wait_and_pass_on(2)

    def wait_passed_on(self, j):
        for cp in self._copies(4 + j, (*self.chips[j], 1 - self.c), self.me):
            cp.wait_recv()

    def wait_sends(self):
        for cp in self._copies(0, self.me, self.sib, True):
            cp.wait_send()
        for j, chip in enumerate(self.chips):
            for cp in self._copies(1 + j, self.me, (*chip, self.c), True) + self._copies(4 + j, (*chip, self.c), self.sib):
                cp.wait_send()

    def finish(self):
        self.wait_sibling()
        for j in range(3):
            self.wait_passed_on(j)
        self.wait_sends()
        self.wait_own()


def _disc(a_re, a_im, log_dt, b_re, b_im):
    dt = jnp.exp(log_dt)
    mag = jnp.exp(a_re * dt)
    ab_re = mag * jnp.cos(a_im * dt)
    ab_im = mag * jnp.sin(a_im * dt)
    den = a_re * a_re + a_im * a_im
    p_re = ab_re - 1.0
    p_im = ab_im
    q_re = (p_re * a_re + p_im * a_im) / den
    q_im = (p_im * a_re - p_re * a_im) / den
    bb_re = q_re * b_re - q_im * b_im
    bb_im = q_re * b_im + q_im * b_re
    return ab_re, ab_im, bb_re, bb_im


def _split3(v):
    hi = v.astype(BF16)
    r1 = v - hi.astype(F32)
    mid = r1.astype(BF16)
    lo = (r1 - mid.astype(F32)).astype(BF16)
    return hi, mid, lo


def _select_dot(sel, v):
    return sum(_dot(sel, t) for t in _split3(v))


PACK_ROWS = 72
PACK_W = 512
ROW_FINAL_GAIN, ROW_NORM_GAIN, ROW_BGLU_D, ROW_CONV, ROW_LOSS, ROW_S5 = 0, 8, 16, 24, 32, 40
LANE_A_RE, LANE_A_IM, LANE_LOG_DT = 0, 128, 256


def _ssm_disc_bwd_pack(a_re_x, a_im_x, log_dt_x, b_re, b_im, g_ab_re, g_ab_im, dbb_re_d, dbb_im_d,
                       loss_t, dg8, dgf, dbg, dd, dcw, dc_re_d, dc_im_d):
    rows_gh = N_GROUPS * GROUP

    def body(are, aim, ldt, bre, bim, gabre, gabim, dbbre_ref, dbbim_ref,
             loss_ref, dg8_ref, dgf_ref, dbg_ref, dd_ref, dcw_ref, dcre_ref, dcim_ref,
             p_ref, gc_ref, gb_ref, gbb_re, gbb_im):
        r_g = lax.broadcasted_iota(jnp.int32, (N_GROUPS, rows_gh), 0)
        c_gh = lax.broadcasted_iota(jnp.int32, (N_GROUPS, rows_gh), 1)
        group_sum = (c_gh // GROUP == r_g).astype(BF16)
        r_gh = lax.broadcasted_iota(jnp.int32, (rows_gh, N_GROUPS), 0)
        c_g = lax.broadcasted_iota(jnp.int32, (rows_gh, N_GROUPS), 1)
        first_row = (r_gh == c_g * GROUP).astype(BF16)

        def diag_block(ref, j, gi):
            return ref[j, gi * GROUP:(gi + 1) * GROUP, gi * STATE:(gi + 1) * STATE]

        for j in range(N_JBLK):
            for gi in range(SUBLANES):
                r0 = (j * SUBLANES + gi) * GROUP
                gbb_re[r0:r0 + GROUP, :] = diag_block(dbbre_ref, j, gi)
                gbb_im[r0:r0 + GROUP, :] = diag_block(dbbim_ref, j, gi)
                both = jnp.concatenate([diag_block(dcre_ref, j, gi), -diag_block(dcim_ref, j, gi)], axis=1)
                gc_ref[r0:r0 + GROUP, :] = both.astype(BF16)

        def by_group(ref):
            return jnp.concatenate([ref[:, g * STATE:(g + 1) * STATE] for g in range(N_GROUPS)], axis=0)

        _, vjp = jax.vjp(_disc, are[...], aim[...], ldt[...], bre[...], bim[...])
        d_are, d_aim, d_ldt, d_bre, d_bim = vjp((_select_dot(first_row, by_group(gabre)),
                                                 _select_dot(first_row, by_group(gabim)),
                                                 gbb_re[...], gbb_im[...]))
        gb_ref[...] = jnp.concatenate([d_bre, d_bim], axis=1).astype(BF16)

        p_ref[...] = jnp.zeros_like(p_ref)
        half = D_MODEL // 2
        for r, src in ((ROW_FINAL_GAIN, dgf_ref), (ROW_NORM_GAIN, dg8_ref)):
            p_ref[r:r + 1, :] = src[0:1, 0:half]
            p_ref[r + 1:r + 2, :] = src[0:1, half:D_MODEL]
        p_ref[ROW_BGLU_D:ROW_BGLU_D + 1, :] = dbg_ref[...]
        p_ref[ROW_BGLU_D + 1:ROW_BGLU_D + 2, :] = dd_ref[...]
        p_ref[ROW_CONV:ROW_CONV + SUBLANES, :] = dcw_ref[...]
        p_ref[ROW_LOSS:ROW_LOSS + SUBLANES, 0:LANES] = loss_ref[...]
        s5 = slice(ROW_S5, ROW_S5 + N_GROUPS)
        p_ref[s5, LANE_A_RE:LANE_A_RE + STATE] = _select_dot(group_sum, d_are)
        p_ref[s5, LANE_A_IM:LANE_A_IM + STATE] = _select_dot(group_sum, d_aim)
        p_ref[s5, LANE_LOG_DT:LANE_LOG_DT + LANES] = _select_dot(group_sum, jnp.broadcast_to(d_ldt, (rows_gh, LANES)))

    operands = (a_re_x, a_im_x, log_dt_x, b_re, b_im, g_ab_re, g_ab_im, dbb_re_d, dbb_im_d,
                loss_t, dg8, dgf, dbg, dd, dcw, dc_re_d, dc_im_d)
    out_shape = (_out((PACK_ROWS, PACK_W), F32),
                 _out((rows_gh, 2 * STATE), BF16),
                 _out((rows_gh, 2 * STATE), BF16))
    return _pcall(body, name="ssm_disc_bwd_pack", grid=(1,), out_shape=out_shape,
                  in_specs=_whole_specs(operands), out_specs=tuple(_whole_specs(out_shape)),
                  scratch_shapes=[pltpu.VMEM((rows_gh, STATE), F32), pltpu.VMEM((rows_gh, STATE), F32)],
                  compiler_params=_params(1))(*operands)


def _s5_prepare(are, aim, ldt, bre, bim, cre, cim,
                o_ax_re, o_ax_im, o_ldt_x, o_ab_re, o_ab_im, o_bb_re, o_bb_im, o_c_re, o_c_imn):
    rows_gh = N_GROUPS * GROUP
    rep = (lax.broadcasted_iota(jnp.int32, (rows_gh, N_GROUPS), 0) // GROUP
           == lax.broadcasted_iota(jnp.int32, (rows_gh, N_GROUPS), 1)).astype(BF16)
    eye = (lax.broadcasted_iota(jnp.int32, (N_GROUPS, N_GROUPS), 0)
           == lax.broadcasted_iota(jnp.int32, (N_GROUPS, N_GROUPS), 1)).astype(F32)
    ldt_col = jnp.sum(eye * ldt[...], axis=1, keepdims=True)
    a_re_x = _select_dot(rep, are[...])
    a_im_x = _select_dot(rep, aim[...])
    ldt_x = _select_dot(rep, jnp.broadcast_to(ldt_col, (N_GROUPS, LANES)))[:, 0:1]
    o_ax_re[...] = a_re_x
    o_ax_im[...] = a_im_x
    o_ldt_x[...] = ldt_x
    ab_re, ab_im, bb_re, bb_im = _disc(a_re_x, a_im_x, ldt_x, bre[...], bim[...])
    for j in range(N_JBLK):
        first = [(j * SUBLANES + gi) * GROUP for gi in range(SUBLANES)]
        o_ab_re[j] = jnp.concatenate([ab_re[r:r + 1, :] for r in first], axis=1)
        o_ab_im[j] = jnp.concatenate([ab_im[r:r + 1, :] for r in first], axis=1)
    for o, v in ((o_bb_re, bb_re), (o_bb_im, bb_im), (o_c_re, cre[...]), (o_c_imn, -cim[...])):
        for j in range(N_JBLK):
            for gi in range(SUBLANES):
                r0 = (j * SUBLANES + gi) * GROUP
                parts = [v[r0:r0 + GROUP, :] if k == gi else jnp.zeros((GROUP, STATE), F32) for k in range(SUBLANES)]
                o[j, gi * GROUP:(gi + 1) * GROUP, :] = jnp.concatenate(parts, axis=1).astype(BF16)


def _in_proj(order, x2, g1, w_in_own, s5):
    n = x2.shape[0]
    tm = min(IN_TILE, n)
    n_tiles = n // tm
    n_s5_in = len(s5)
    n_s5_out = 9

    def body(order_ref, x_ref, g_ref, w_ref, *refs):
        s5_in = refs[:n_s5_in]
        xn_ref, proj_ref, wall_ref = refs[n_s5_in:n_s5_in + 3]
        s5_out = refs[n_s5_in + 3:n_s5_in + 3 + n_s5_out]
        xn_scr, wbuf, wown, send_sems, recv_sems, loc_sems, out_sems = refs[n_s5_in + 3 + n_s5_out:]
        k = pl.program_id(0)
        i = pl.program_id(1)

        def slot(dev):
            return wbuf.at[dev // 2, :, pl.ds(pl.multiple_of((dev % 2) * COLS_PER_DEV, LANES), COLS_PER_DEV)]

        gather = _TwoLevelGather([wown], [slot], send_sems, recv_sems, loc_sems)

        @pl.when((k == 0) & (i == 0))
        def _():
            wown[...] = w_ref[...].astype(BF16)
            gather.start()

        def own_chip():
            gather.wait_own()
            gather.wait_sibling()

        def x_chip():
            gather.neighbours_landed()
            gather.wait_passed_on(0)

        def diag_chip():
            gather.diagonal_landed()
            gather.wait_passed_on(2)

        rows = pl.ds(pl.multiple_of(i * tm, tm), tm)

        @pl.when(k == 0)
        def _():
            x = x_ref[...]
            r = lax.rsqrt(jnp.mean(x * x, axis=-1, keepdims=True) + EPS)
            xn = ((x * r) * g_ref[...]).astype(BF16)
            xn_scr[rows, :] = xn
            xn_ref[...] = xn

        def keep_copy(kk):
            q = order_ref[kk]
            cols = pl.ds(pl.multiple_of(q * COLS_PER_CHIP, LANES), COLS_PER_CHIP)
            return pltpu.make_async_copy(wbuf.at[q], wall_ref.at[:, cols], out_sems.at[kk])

        arrivals = [own_chip, x_chip, functools.partial(gather.wait_passed_on, 1), diag_chip]
        for kk, arrived in enumerate(arrivals):
            @pl.when((k == kk) & (i == 0))
            def _(kk=kk, arrived=arrived):
                arrived()
                keep_copy(kk).start()

        proj_ref[...] = _dot(xn_scr[rows, :], wbuf[order_ref[k]])

        @pl.when((k == 0) & (i == n_tiles - 1))
        def _():
            _s5_prepare(*s5_in, *s5_out)

        @pl.when((k == N_CHIP - 1) & (i == n_tiles - 1))
        def _():
            gather.wait_sends()
            for kk in range(N_CHIP):
                keep_copy(kk).wait()

    tile_once = lambda k, i, order: (jnp.where(k == 0, i, n_tiles - 1), 0)
    whole = lambda shape: pl.BlockSpec(shape, lambda k, i, order: (0,) * len(shape))
    rows_gh = N_GROUPS * GROUP
    s5_out_shapes = ([(rows_gh, STATE), F32], [(rows_gh, STATE), F32], [(rows_gh, 1), F32],
                     [(N_JBLK, 1, JB_ST), F32], [(N_JBLK, 1, JB_ST), F32]) + ([(N_JBLK, JB_CH, JB_ST), BF16],) * 4
    grid_spec = pltpu.PrefetchScalarGridSpec(
        num_scalar_prefetch=1, grid=(N_CHIP, n_tiles),
        in_specs=[pl.BlockSpec((tm, D_MODEL), tile_once),
                  whole((1, D_MODEL)),
                  whole(w_in_own.shape),
                  *(whole(a.shape) for a in s5)],
        out_specs=(pl.BlockSpec((tm, D_MODEL), tile_once),
                   pl.BlockSpec((tm, COLS_PER_CHIP), lambda k, i, order: (i, order[k])),
                   HBM_SPEC,
                   *(whole(shape) for shape, _ in s5_out_shapes)),
        scratch_shapes=[pltpu.VMEM((n, D_MODEL), BF16), pltpu.VMEM((N_CHIP, D_MODEL, COLS_PER_CHIP), BF16),
                        pltpu.VMEM(w_in_own.shape, BF16),
                        pltpu.SemaphoreType.DMA((7,)), pltpu.SemaphoreType.DMA((7,)), pltpu.SemaphoreType.DMA((1,)),
                        pltpu.SemaphoreType.DMA((N_CHIP,))])
    outs = _pcall(
        body, name="in_proj", grid_spec=grid_spec,
        out_shape=(_out((n, D_MODEL), BF16), _out((n, IN_COLS), F32),
                   _out((D_MODEL, IN_COLS), BF16),
                   *(_out(shape, dt) for shape, dt in s5_out_shapes)),
        compiler_params=_params(2),
    )(order, x2, g1, w_in_own, *s5)
    return outs[0], outs[1], outs[2], outs[3:]


def _cmul(p, q):
    return p[0] * q[0] - p[1] * q[1], p[0] * q[1] + p[1] * q[0]


def _scan_tables(ar, ai, width, reverse):
    pows = [(ar, ai)]
    for _ in range(SUBLANES - 1):
        pows.append(_cmul(pows[-1], (ar, ai)))
    row = lax.broadcasted_iota(jnp.int32, (SUBLANES, width), 0)

    def bc(v):
        return jnp.broadcast_to(v, (SUBLANES, width))

    levels = []
    for k in (1, 2, 4):
        keep = (row <= SUBLANES - 1 - k) if reverse else (row >= k)
        levels.append((jnp.where(keep, bc(pows[k - 1][0]), 0.0), jnp.where(keep, bc(pows[k - 1][1]), 0.0)))
    cre = jnp.zeros((SUBLANES, width), F32)
    cim = jnp.zeros((SUBLANES, width), F32)
    for r in range(SUBLANES):
        e = (SUBLANES - r) if reverse else (r + 1)
        cre = jnp.where(row == r, bc(pows[e - 1][0]), cre)
        cim = jnp.where(row == r, bc(pows[e - 1][1]), cim)
    return levels, (cre, cim)


def _load_chunked(src_ref, b, dst_ref, n_rows):
    n_blk = n_rows // SUBLANES
    for i in range(n_blk):
        dst_ref[b, i * SUBLANES:(i + 1) * SUBLANES, :] = src_ref[b, pl.ds(i, SUBLANES, stride=n_blk), :]


def _store_chunked(val, dst_ref, b, n_rows):
    n_blk = n_rows // SUBLANES
    for i in range(n_blk):
        dst_ref[b, pl.ds(i, SUBLANES, stride=n_blk), :] = val[i * SUBLANES:(i + 1) * SUBLANES, :]


def _chunk_scan(re_ref, im_ref, bs, car_ref, ar, ai, n_rows, reverse, on_block=None):
    width = re_ref.shape[2]
    n_blk = n_rows // SUBLANES
    shape = (SUBLANES, width)
    abr = jnp.broadcast_to(ar, shape)
    abi = jnp.broadcast_to(ai, shape)
    order = list(range(n_blk - 1, -1, -1)) if reverse else list(range(n_blk))

    def blk(ref, b, i):
        return ref[b, i * SUBLANES:(i + 1) * SUBLANES, :]

    def step(state, b, i):
        sr, si = state
        return abr * sr - abi * si + blk(re_ref, b, i), abr * si + abi * sr + blk(im_ref, b, i)

    finals = {b: (blk(re_ref, b, order[0]), blk(im_ref, b, order[0])) for b in bs}
    for i in order[1:]:
        for b in bs:
            finals[b] = step(finals[b], b, i)

    mr, mi = ar, ai
    for _ in range(n_blk.bit_length() - 1):
        mr, mi = _cmul((mr, mi), (mr, mi))
    levels, _ = _scan_tables(mr, mi, width, reverse)
    mbr = jnp.broadcast_to(mr, shape)
    mbi = jnp.broadcast_to(mi, shape)
    row = lax.broadcasted_iota(jnp.int32, shape, 0)
    edge_in = SUBLANES - 1 if reverse else 0
    edge_out = 0 if reverse else SUBLANES - 1
    sh1 = SUBLANES - 1 if reverse else 1
    states = {}
    for b in bs:
        fr, fi = finals[b]
        gr = jnp.where(row == edge_in, jnp.broadcast_to(car_ref[b, 0:1, :], shape), pltpu.roll(fr, sh1, 0))
        gi = jnp.where(row == edge_in, jnp.broadcast_to(car_ref[b, 1:2, :], shape), pltpu.roll(fi, sh1, 0))
        for (lr, li), k in zip(levels, (1, 2, 4)):
            sh = (SUBLANES - k) if reverse else k
            sr = pltpu.roll(gr, sh, 0)
            si = pltpu.roll(gi, sh, 0)
            gr, gi = gr + (lr * sr - li * si), gi + (lr * si + li * sr)
        car_ref[b, 0:1, :] = (fr + (mbr * gr - mbi * gi))[edge_out:edge_out + 1, :]
        car_ref[b, 1:2, :] = (fi + (mbr * gi + mbi * gr))[edge_out:edge_out + 1, :]
        states[b] = (gr, gi)

    for i in order:
        for b in bs:
            states[b] = step(states[b], b, i)
            re_ref[b, i * SUBLANES:(i + 1) * SUBLANES, :] = states[b][0]
            im_ref[b, i * SUBLANES:(i + 1) * SUBLANES, :] = states[b][1]
            if on_block is not None:
                on_block(b, i, *states[b])


def _ssm_fwd(u, bb_re, bb_im, c_re_t, c_imn_t, d_row, ab_re, ab_im, w_out_own, w_glu_own, conv_p, n_seq, seq):
    tt = min(SCAN_TILE, seq)
    nt = seq // tt

    def body(u_ref, bbre, bbim, cre, cimn, d_ref, are, aim, wout_ref, wglu_ref, cw_ref,
             sre_ref, sim_ref, y_ref, oout_ref, oglu_ref, ocw_ref,
             up_ref, car_ref, woutb_ref, wglub_ref, send_sems, recv_sems, loc_sems):
        j = pl.program_id(0)
        t = pl.program_id(1)
        gather = _TwoLevelGather(
            [woutb_ref, wglub_ref, cw_ref],
            [lambda dev: oout_ref.at[pl.ds(pl.multiple_of(dev * OUT_ROWS_PER_DEV, OUT_ROWS_PER_DEV), OUT_ROWS_PER_DEV), :],
             lambda dev: oglu_ref.at[pl.ds(pl.multiple_of(dev * GLU_ROWS_PER_DEV, GLU_ROWS_PER_DEV), GLU_ROWS_PER_DEV), :],
             lambda dev: ocw_ref.at[dev]],
            send_sems, recv_sems, loc_sems)

        @pl.when((j == 0) & (t == 0))
        def _():
            woutb_ref[...] = wout_ref[...].astype(BF16)
            wglub_ref[...] = wglu_ref[...].astype(BF16)
            gather.start()

        @pl.when((j == N_JBLK // 2) & (t == 0))
        def _():
            gather.neighbours_landed()

        @pl.when((j == N_JBLK - 1) & (t == 0))
        def _():
            gather.diagonal_landed()

        @pl.when(t == 0)
        def _():
            car_ref[...] = jnp.zeros_like(car_ref)

        bs = list(range(n_seq))
        for b in bs:
            _load_chunked(u_ref, b, up_ref, tt)
        for b in bs:
            ub = up_ref[b].astype(BF16)
            sre_ref[b] = _dot(ub, bbre[0])
            sim_ref[b] = _dot(ub, bbim[0])
            _chunk_scan(sre_ref, sim_ref, [b], car_ref, are[0], aim[0], tt, reverse=False)
        for b in bs:
            yp = (_dot_nt(sre_ref[b].astype(BF16), cre[0]) + _dot_nt(sim_ref[b].astype(BF16), cimn[0])
                  + d_ref[...] * up_ref[b])
            _store_chunked(yp, y_ref, b, tt)

        @pl.when((j == N_JBLK - 1) & (t == nt - 1))
        def _():
            gather.finish()

    tok = lambda j, t: (0, t, j)
    blk3 = lambda j, t: (j, 0, 0)
    row = lambda j, t: (0, j)
    whole = lambda j, t: (0, 0)
    st = _out((n_seq, seq, N_JBLK * JB_ST), F32)
    n_arr = 3
    return _pcall(
        body, name="ssm_fwd", grid=(N_JBLK, nt),
        out_shape=(st, st, _out((n_seq, seq, SSM_W), F32),
                   _out((D_MODEL, D_MODEL), BF16), _out((SSM_W, SSM_W), BF16),
                   _out((N_DEV, SUBLANES, LANES), F32)),
        in_specs=[pl.BlockSpec((n_seq, tt, JB_CH), tok),
                  pl.BlockSpec((1, JB_CH, JB_ST), blk3), pl.BlockSpec((1, JB_CH, JB_ST), blk3),
                  pl.BlockSpec((1, JB_CH, JB_ST), blk3), pl.BlockSpec((1, JB_CH, JB_ST), blk3),
                  pl.BlockSpec((1, JB_CH), row), pl.BlockSpec((1, 1, JB_ST), blk3), pl.BlockSpec((1, 1, JB_ST), blk3),
                  pl.BlockSpec(w_out_own.shape, whole), pl.BlockSpec(w_glu_own.shape, whole), HBM_SPEC],
        out_specs=(pl.BlockSpec((n_seq, tt, JB_ST), tok), pl.BlockSpec((n_seq, tt, JB_ST), tok),
                   pl.BlockSpec((n_seq, tt, JB_CH), tok), HBM_SPEC, HBM_SPEC, HBM_SPEC),
        scratch_shapes=[pltpu.VMEM((n_seq, tt, JB_CH), F32), pltpu.VMEM((n_seq, SUBLANES, JB_ST), F32),
                        pltpu.VMEM(w_out_own.shape, BF16), pltpu.VMEM(w_glu_own.shape, BF16),
                        pltpu.SemaphoreType.DMA((7 * n_arr,)), pltpu.SemaphoreType.DMA((7 * n_arr,)),
                        pltpu.SemaphoreType.DMA((n_arr,))],
        compiler_params=_params(2),
    )(u, bb_re, bb_im, c_re_t, c_imn_t, d_row, ab_re, ab_im, w_out_own, w_glu_own, conv_p)


def _ssm_bwd(dy, u, s_re, s_im, bb_re, bb_im, c_re_t, c_imn_t, d_row, ab_re, ab_im, g_out, g_glu, n_seq, seq):
    tt = min(SCAN_TILE, seq)
    nt = seq // tt
    rows8 = tt // SUBLANES

    def body(dy_ref, u_ref, sre_ref, sim_ref, pre_ref, pim_ref, bbre, bbim, cre, cimn, d_ref, are, aim,
             gout_ref, gglu_ref,
             du_ref, dcre_ref, dcim_ref, dbbre_ref, dbbim_ref, dare_ref, daim_ref, dd_ref, rout_ref, rglu_ref,
             lre_ref, lim_ref, dyp_ref, up_ref, car_ref, send_sems, recv_sems, loc_sems):
        j = pl.program_id(0)
        tr = pl.program_id(1)

        def exchange():
            return _direct_copies(lambda pid: [gout_ref.at[pid], gglu_ref.at[pid]], [rout_ref, rglu_ref],
                                  send_sems, recv_sems, loc_sems)

        @pl.when((j == 0) & (tr == 0))
        def _():
            mine, sends = exchange()
            for cp in mine + sends:
                cp.start()

        @pl.when(tr == 0)
        def _():
            car_ref[...] = jnp.zeros_like(car_ref)
            for r in (dcre_ref, dcim_ref, dbbre_ref, dbbim_ref, dare_ref, daim_ref, dd_ref):
                r[...] = jnp.zeros_like(r)

        first = tr == nt - 1
        row = lax.broadcasted_iota(jnp.int32, (SUBLANES, JB_ST), 0)
        n_blk = tt // SUBLANES
        bs = list(range(n_seq))
        for b in bs:
            _load_chunked(dy_ref, b, dyp_ref, tt)
            _load_chunked(u_ref, b, up_ref, tt)
        for b in bs:
            dyb = dyp_ref[b].astype(BF16)
            lre_ref[b] = _dot(dyb, cre[0])
            lim_ref[b] = _dot(dyb, cimn[0])
        acc = {b: [jnp.zeros((SUBLANES, JB_ST), F32), jnp.zeros((SUBLANES, JB_ST), F32)] for b in bs}

        def on_block(b, i, lr, li):
            if i > 0:
                spr = sre_ref[b, (i - 1) * SUBLANES:i * SUBLANES, :]
                spi = sim_ref[b, (i - 1) * SUBLANES:i * SUBLANES, :]
            else:
                hr = jnp.where(first, 0.0, pre_ref[b, SUBLANES - 1:SUBLANES, :])
                hi = jnp.where(first, 0.0, pim_ref[b, SUBLANES - 1:SUBLANES, :])
                last_r = sre_ref[b, (n_blk - 1) * SUBLANES:n_blk * SUBLANES, :]
                last_i = sim_ref[b, (n_blk - 1) * SUBLANES:n_blk * SUBLANES, :]
                spr = jnp.where(row == 0, jnp.broadcast_to(hr, row.shape), pltpu.roll(last_r, 1, 0))
                spi = jnp.where(row == 0, jnp.broadcast_to(hi, row.shape), pltpu.roll(last_i, 1, 0))
            acc[b][0] = acc[b][0] + (lr * spr + li * spi)
            acc[b][1] = acc[b][1] + (li * spr - lr * spi)

        _chunk_scan(lre_ref, lim_ref, bs, car_ref, are[0], -aim[0], tt, reverse=True, on_block=on_block)
        for b in bs:
            dare_ref[...] += jnp.sum(acc[b][0], axis=0, keepdims=True)
            daim_ref[...] += jnp.sum(acc[b][1], axis=0, keepdims=True)
            dyp = dyp_ref[b]
            up = up_ref[b]
            dyb = dyp.astype(BF16)
            ub = up.astype(BF16)
            lrb = lre_ref[b].astype(BF16)
            lib = lim_ref[b].astype(BF16)
            dup = d_ref[...] * dyp + _dot_nt(lrb, bbre[0]) + _dot_nt(lib, bbim[0])
            _store_chunked(dup, du_ref, b, tt)
            dbbre_ref[0] += _dot_tn(ub, lrb)
            dbbim_ref[0] += _dot_tn(ub, lib)
            dcre_ref[0] += _dot_tn(dyb, sre_ref[b].astype(BF16))
            dcim_ref[0] += _dot_tn(dyb, sim_ref[b].astype(BF16))
            dd_ref[...] += jnp.sum(dyp * up, axis=0, keepdims=True)

        @pl.when((j == N_JBLK - 1) & (tr == nt - 1))
        def _():
            mine, sends = exchange()
            for cp in sends + mine:
                cp.wait()

    tok = lambda j, t: (0, nt - 1 - t, j)
    halo = lambda j, t: (0, jnp.maximum((nt - 1 - t) * rows8 - 1, 0), j)
    blk3 = lambda j, t: (j, 0, 0)
    row1 = lambda j, t: (0, j)
    acc_shape = _out((N_JBLK, JB_CH, JB_ST), F32)
    return _pcall(
        body, name="ssm_bwd", grid=(N_JBLK, nt),
        out_shape=(_out((n_seq, seq, SSM_W), F32), acc_shape, acc_shape, acc_shape, acc_shape,
                   _out((1, N_JBLK * JB_ST), F32), _out((1, N_JBLK * JB_ST), F32),
                   _out((1, SSM_W), F32),
                   _out((N_DEV,) + g_out.shape[1:], F32),
                   _out((N_DEV,) + g_glu.shape[1:], F32)),
        in_specs=[pl.BlockSpec((n_seq, tt, JB_CH), tok), pl.BlockSpec((n_seq, tt, JB_CH), tok),
                  pl.BlockSpec((n_seq, tt, JB_ST), tok), pl.BlockSpec((n_seq, tt, JB_ST), tok),
                  pl.BlockSpec((n_seq, SUBLANES, JB_ST), halo), pl.BlockSpec((n_seq, SUBLANES, JB_ST), halo),
                  pl.BlockSpec((1, JB_CH, JB_ST), blk3), pl.BlockSpec((1, JB_CH, JB_ST), blk3),
                  pl.BlockSpec((1, JB_CH, JB_ST), blk3), pl.BlockSpec((1, JB_CH, JB_ST), blk3),
                  pl.BlockSpec((1, JB_CH), row1), pl.BlockSpec((1, 1, JB_ST), blk3), pl.BlockSpec((1, 1, JB_ST), blk3),
                  HBM_SPEC, HBM_SPEC],
        out_specs=(pl.BlockSpec((n_seq, tt, JB_CH), tok),
                   pl.BlockSpec((1, JB_CH, JB_ST), blk3), pl.BlockSpec((1, JB_CH, JB_ST), blk3),
                   pl.BlockSpec((1, JB_CH, JB_ST), blk3), pl.BlockSpec((1, JB_CH, JB_ST), blk3),
                   pl.BlockSpec((1, JB_ST), row1), pl.BlockSpec((1, JB_ST), row1), pl.BlockSpec((1, JB_CH), row1),
                   HBM_SPEC, HBM_SPEC),
        scratch_shapes=[pltpu.VMEM((n_seq, tt, JB_ST), F32), pltpu.VMEM((n_seq, tt, JB_ST), F32),
                        pltpu.VMEM((n_seq, tt, JB_CH), F32), pltpu.VMEM((n_seq, tt, JB_CH), F32),
                        pltpu.VMEM((n_seq, SUBLANES, JB_ST), F32),
                        pltpu.SemaphoreType.DMA((7 * 2,)), pltpu.SemaphoreType.DMA((7 * 2,)),
                        pltpu.SemaphoreType.DMA((2,))],
        compiler_params=_params(2),
    )(dy, u, s_re, s_im, s_re, s_im, bb_re, bb_im, c_re_t, c_imn_t, d_row, ab_re, ab_im, g_out, g_glu)


def _mix(x2, tgt2, y, proj, gf, b_glu, conv8, w_glu_f, w_out_f, seq):
    n = x2.shape[0]
    tm = TOK_TILE
    tiles_per_seq = seq // tm
    rows8 = tm // SUBLANES

    ring = 3
    n_tiles = n // tm

    def body(x_hbm, t_hbm, y_ref, zs_ref, h_ref, bc_ref, cc_ref, zc_ref, hp_ref, ccp_ref,
             gf_ref, bg_ref, cw_ref, wg_ref, wo_ref,
             dh2_ref, dy_ref, dzs_ref, dbc_ref, dzc_ref, dyc_ref,
             dwo_ref, dwg_ref, loss_ref, dgf_ref, dbg_ref, dcw_ref, x_ring, t_ring, x_sems, t_sems):
        i = pl.program_id(0)

        def fetch(step):
            slot = step % ring
            rows = pl.ds(pl.multiple_of(step * tm, tm), tm)
            return [pltpu.make_async_copy(x_hbm.at[rows, :], x_ring.at[slot], x_sems.at[slot]),
                    pltpu.make_async_copy(t_hbm.at[rows, :], t_ring.at[slot], t_sems.at[slot])]

        @pl.when(i == 0)
        def _():
            for r in (dwo_ref, dwg_ref, loss_ref, dgf_ref, dbg_ref, dcw_ref):
                r[...] = jnp.zeros_like(r)
            for step in range(min(ring - 1, n_tiles)):
                for cp in fetch(step):
                    cp.start()

        @pl.when(i + ring - 1 < n_tiles)
        def _():
            for cp in fetch(i + ring - 1):
                cp.start()

        yv = y_ref[...]
        y1, dgelu = _gelu_and_grad(yv)
        y1b = y1.astype(BF16)
        gate = _sigmoid(_dot(y1b, wg_ref[...]) + bg_ref[...])
        y2 = y1 * gate
        szs, dszs = _silu_and_grad(zs_ref[...])
        yssm = y2 * szs
        hv = h_ref[...]
        ccv = cc_ref[...]
        bcv = bc_ref[...]
        v = ccv * hv
        first = (i % tiles_per_seq) == 0
        vhalo = jnp.where(first, 0.0, ccp_ref[...] * hp_ref[...])
        v1 = _shift_down(v, vhalo, 1)
        v2 = _shift_down(v, vhalo, 2)
        w0 = cw_ref[0:1, :]
        w1 = cw_ref[1:2, :]
        w2 = cw_ref[2:3, :]
        yc = w0 * v2 + w1 * v1 + w2 * v
        szc, dszc = _silu_and_grad(zc_ref[...])
        yconv = (bcv * yc) * szc
        ysb = yssm.astype(BF16)
        ycb = yconv.astype(BF16)
        for cp in fetch(i):
            cp.wait()
        h2 = x_ring[i % ring] + _dot(ysb, wo_ref[0:SSM_W, :]) + _dot(ycb, wo_ref[SSM_W:, :])
        r2 = lax.rsqrt(jnp.mean(h2 * h2, axis=-1, keepdims=True) + EPS)
        hn = h2 * r2
        gfv = gf_ref[...]
        err = hn * gfv - t_ring[i % ring]
        loss_ref[...] += 0.5 * jnp.sum(jnp.mean(err * err, axis=-1, keepdims=True))
        dout = err * (1.0 / D_MODEL)
        dgf_ref[...] += jnp.sum(dout * hn, axis=0, keepdims=True)
        dn = dout * gfv
        dh2 = r2 * (dn - hn * jnp.mean(dn * hn, axis=-1, keepdims=True))
        dh2_ref[...] = dh2
        dh2b = dh2.astype(BF16)
        dwo_ref[0:SSM_W, :] += _dot_tn(ysb, dh2b)
        dwo_ref[SSM_W:, :] += _dot_tn(ycb, dh2b)
        dyssm = _dot_nt(dh2b, wo_ref[0:SSM_W, :])
        dyconv = _dot_nt(dh2b, wo_ref[SSM_W:, :])
        dy2 = dyssm * szs
        dzs_ref[...] = (dyssm * y2 * dszs).astype(BF16)
        dgp = dy2 * y1 * (gate * (1.0 - gate))
        dgpb = dgp.astype(BF16)
        dy1 = dy2 * gate + _dot_nt(dgpb, wg_ref[...])
        dwg_ref[...] += _dot_tn(y1b, dgpb)
        dbg_ref[...] += jnp.sum(dgp, axis=0, keepdims=True)
        dy_ref[...] = dy1 * dgelu
        dbc_ref[...] = (dyconv * yc * szc).astype(BF16)
        dyc = dyconv * bcv * szc
        dyc_ref[...] = dyc
        dzc_ref[...] = (dyconv * bcv * yc * dszc).astype(BF16)
        dcw_ref[0:1, :] += jnp.sum(dyc * v2, axis=0, keepdims=True)
        dcw_ref[1:2, :] += jnp.sum(dyc * v1, axis=0, keepdims=True)
        dcw_ref[2:3, :] += jnp.sum(dyc * v, axis=0, keepdims=True)

    tile_d = pl.BlockSpec((tm, D_MODEL), lambda i: (i, 0))
    tile_s = pl.BlockSpec((tm, SSM_W), lambda i: (i, 0))
    seg_of = lambda c: pl.BlockSpec((tm, SSM_W), lambda i: (i, c))
    halo_of = lambda c: pl.BlockSpec((SUBLANES, SSM_W), lambda i: (jnp.maximum(i * rows8 - 1, 0), c))
    const = lambda shape: pl.BlockSpec(shape, lambda i: (0,) * len(shape))
    seg = _out((n, SSM_W), F32)
    seg_b = _out((n, SSM_W), BF16)
    return _pcall(
        body, name="mix", grid=(n // tm,),
        out_shape=(_out((n, D_MODEL), F32), seg, seg_b, seg_b, seg_b, seg,
                   _out((D_MODEL, D_MODEL), F32), _out((SSM_W, SSM_W), F32),
                   _out((SUBLANES, LANES), F32), _out((1, D_MODEL), F32),
                   _out((1, SSM_W), F32), _out((SUBLANES, CONV_W), F32)),
        in_specs=[HBM_SPEC, HBM_SPEC, tile_s, seg_of(SEG_ZS), seg_of(SEG_H), seg_of(SEG_BC), seg_of(SEG_CC), seg_of(SEG_ZC),
                  halo_of(SEG_H), halo_of(SEG_CC),
                  const((1, D_MODEL)), const((1, SSM_W)), const((SUBLANES, CONV_W)),
                  const((SSM_W, SSM_W)), const((D_MODEL, D_MODEL))],
        out_specs=(tile_d, tile_s, tile_s, tile_s, tile_s, tile_s,
                   const((D_MODEL, D_MODEL)), const((SSM_W, SSM_W)), const((SUBLANES, LANES)),
                   const((1, D_MODEL)), const((1, SSM_W)), const((SUBLANES, CONV_W))),
        scratch_shapes=[pltpu.VMEM((ring, tm, D_MODEL), F32), pltpu.VMEM((ring, tm, D_MODEL), F32),
                        pltpu.SemaphoreType.DMA((ring,)), pltpu.SemaphoreType.DMA((ring,))],
        compiler_params=_params(1),
    )(x2, tgt2, y, proj, proj, proj, proj, proj, proj, proj, gf, b_glu, conv8, w_glu_f, w_out_f)


def _in_bwd(x2, dh2, du, dzs, dyc, proj, dbc, dzc, g1, conv8, w_full, seq):
    n = x2.shape[0]
    tm = TOK_TILE
    n_tiles = n // tm
    tiles_per_seq = seq // tm
    rows8 = tm // SUBLANES
    n_blk8 = n // SUBLANES

    ring = 3

    def body(x_hbm, dh2_hbm, du_ref, dzs_ref, dyc_ref, dycn_ref, h_ref, cc_ref, dbc_ref, dzc_ref,
             g_ref, cw_ref, w_ref, gx_ref, dp_ref, dg_ref, x_ring, dh2_ring, x_sems, dh2_sems):
        i = pl.program_id(0)

        def fetch(step):
            slot = step % ring
            rows = pl.ds(pl.multiple_of(step * tm, tm), tm)
            return [pltpu.make_async_copy(x_hbm.at[rows, :], x_ring.at[slot], x_sems.at[slot]),
                    pltpu.make_async_copy(dh2_hbm.at[rows, :], dh2_ring.at[slot], dh2_sems.at[slot])]

        @pl.when(i == 0)
        def _():
            dg_ref[...] = jnp.zeros_like(dg_ref)
            for step in range(min(ring - 1, n_tiles)):
                for cp in fetch(step):
                    cp.start()

        @pl.when(i + ring - 1 < n_tiles)
        def _():
            for cp in fetch(i + ring - 1):
                cp.start()

        dyc = dyc_ref[...]
        last = (i % tiles_per_seq) == tiles_per_seq - 1
        nhalo = jnp.where(last, 0.0, dycn_ref[...])
        dv = (cw_ref[2:3, :] * dyc + cw_ref[1:2, :] * _shift_up(dyc, nhalo, 1)
              + cw_ref[0:1, :] * _shift_up(dyc, nhalo, 2))
        parts = (du_ref[...], dzs_ref[...], dv * cc_ref[...], dbc_ref[...], dv * h_ref[...], dzc_ref[...])
        dxn = jnp.zeros((tm, D_MODEL), F32)
        for k, p in enumerate(parts):
            pb = p.astype(BF16)
            dp_ref[:, k * SSM_W:(k + 1) * SSM_W] = pb
            dxn = dxn + _dot_nt(pb, w_ref[:, k * SSM_W:(k + 1) * SSM_W])
        for cp in fetch(i):
            cp.wait()
        x = x_ring[i % ring]
        r = lax.rsqrt(jnp.mean(x * x, axis=-1, keepdims=True) + EPS)
        xh = x * r
        dg_ref[...] += jnp.sum(dxn * xh, axis=0, keepdims=True)
        dn = dxn * g_ref[...]
        gx_ref[...] = dh2_ring[i % ring] + r * (dn - xh * jnp.mean(dn * xh, axis=-1, keepdims=True))

    tile_d = pl.BlockSpec((tm, D_MODEL), lambda i: (i, 0))
    tile_s = pl.BlockSpec((tm, SSM_W), lambda i: (i, 0))
    seg_of = lambda c: pl.BlockSpec((tm, SSM_W), lambda i: (i, c))
    nhalo = pl.BlockSpec((SUBLANES, SSM_W), lambda i: (jnp.minimum((i + 1) * rows8, n_blk8 - 1), 0))
    const = lambda shape: pl.BlockSpec(shape, lambda i: (0,) * len(shape))
    return _pcall(
        body, name="in_bwd", grid=(n_tiles,),
        out_shape=(_out((n, D_MODEL), F32), _out((n, IN_COLS), BF16),
                   _out((SUBLANES, D_MODEL), F32)),
        in_specs=[HBM_SPEC, HBM_SPEC, tile_s, tile_s, tile_s, nhalo, seg_of(SEG_H), seg_of(SEG_CC), tile_s, tile_s,
                  const((1, D_MODEL)), const((SUBLANES, CONV_W)), const((D_MODEL, IN_COLS))],
        out_specs=(tile_d, pl.BlockSpec((tm, IN_COLS), lambda i: (i, 0)), const((SUBLANES, D_MODEL))),
        scratch_shapes=[pltpu.VMEM((ring, tm, D_MODEL), F32), pltpu.VMEM((ring, tm, D_MODEL), F32),
                        pltpu.SemaphoreType.DMA((ring,)), pltpu.SemaphoreType.DMA((ring,))],
        compiler_params=_params(1),
    )(x2, dh2, du, dzs, dyc, dyc, proj, proj, dbc, dzc, g1, conv8, w_full)


_HALF_BLOCKS = ((0, 0), (0, 1), (1, 0), (2, 0), (1, 1), (2, 1), (3, 0), (3, 1))


def _dw_in_exchange(chips, xn, dproj, smalls):
    n = xn.shape[0]
    tk = min(1024, n)
    nk = n // tk
    piece = (D_MODEL, COLS_PER_DEV)
    hr = D_MODEL // 2
    n_half = len(_HALF_BLOCKS)
    n_small = len(smalls)
    assert _HALF_BLOCKS[0][1] == 0 and _HALF_BLOCKS[1][1] == 1
    order = jnp.stack([chips[b] for b, _ in _HALF_BLOCKS]
                      + [jnp.int32(t) for _, t in _HALF_BLOCKS]).astype(jnp.int32)

    def body(order_ref, xn_hbm, dp_ref, *refs):
        sm_refs = refs[:n_small]
        own_ref, rchip_ref = refs[n_small:n_small + 2]
        rsm_refs = refs[n_small + 2:2 * n_small + 2]
        (xn_ref, acc, stage, rbuf, kbuf, relay_in, xn_sems, give_send, give_recv, keep_send, keep_recv,
         relay_send, relay_recv, sm_send, sm_recv, sm_loc) = refs[2 * n_small + 2:]
        s = pl.program_id(0)

        def xn_copy(kk, t):
            rows = pl.ds(pl.multiple_of(kk * tk, tk), tk)
            return pltpu.make_async_copy(xn_hbm.at[rows, t * hr:(t + 1) * hr], xn_ref.at[t, rows, :],
                                         xn_sems.at[2 * kk + t])

        @pl.when(s == 0)
        def _():
            for kk in range(nk):
                for t in range(2):
                    xn_copy(kk, t).start()
            xn_copy(0, 0).wait()

        @pl.when(s == 1)
        def _():
            xn_copy(0, 1).wait()

        x, y, c = _mesh_pos()
        sib = (x, y, 1 - c)
        y_nbr, x_nbr = (x, 1 - y, c), (1 - x, y, c)
        gather = _TwoLevelGather(list(sm_refs), [functools.partial(lambda r, dev: r.at[dev], r) for r in rsm_refs],
                                 sm_send, sm_recv, sm_loc)

        def give(h):
            cols = pl.ds(pl.multiple_of((1 - c) * COLS_PER_DEV, LANES), COLS_PER_DEV)
            return pltpu.make_async_remote_copy(src_ref=acc.at[h % 2, :, cols], dst_ref=stage.at[h],
                                                send_sem=give_send.at[h], recv_sem=give_recv.at[h],
                                                device_id=sib, device_id_type=MESH)

        def relay(r):
            return pltpu.make_async_remote_copy(src_ref=rbuf.at[r], dst_ref=relay_in.at[r],
                                                send_sem=relay_send.at[r], recv_sem=relay_recv.at[r],
                                                device_id=(x_nbr, y_nbr)[r], device_id_type=MESH)

        def keep(q):
            return pltpu.make_async_remote_copy(src_ref=kbuf.at[q], dst_ref=rchip_ref.at[q // 2, pl.ds((q % 2) * hr, hr), :],
                                                send_sem=keep_send.at[q], recv_sem=keep_recv.at[q],
                                                device_id=(y_nbr, x_nbr)[q // 2], device_id_type=MESH)

        def chip_sum(h):
            give(h).wait_recv()
            mine = [acc[h % 2, :, cc * COLS_PER_DEV:(cc + 1) * COLS_PER_DEV] for cc in range(2)]
            return jnp.where(c == 0, mine[0], mine[1]) + stage[h]

        @pl.when(s == 0)
        def _():
            gather.start()

        @pl.when(s == 2)
        def _():
            gather.neighbours_landed()

        @pl.when(s == n_half - 2)
        def _():
            gather.diagonal_landed()

        for k in range(2, n_half):
            @pl.when(s == k)
            def _(k=k):
                give(k - 2).wait_send()

        slot = s % 2
        t_half = order_ref[n_half + s]
        acc[slot] = _dot_tn(xn_ref[t_half, pl.ds(0, tk), :], dp_ref[pl.ds(0, tk), :])

        def kstep(kk, carry):
            for t in range(2):
                @pl.when(s == t)
                def _(t=t):
                    xn_copy(kk, t).wait()

            off = pl.multiple_of(kk * tk, tk)
            acc[slot] += _dot_tn(xn_ref[t_half, pl.ds(off, tk), :], dp_ref[pl.ds(off, tk), :])
            return carry

        n_first = max(1, (3 * nk) // 4)
        lax.fori_loop(1, n_first, kstep, 0)
        for k in range(1, n_half):
            @pl.when(s == k)
            def _(k=k):
                h = k - 1
                b, t = _HALF_BLOCKS[h]
                total = chip_sum(h)
                if b == 0:
                    rbuf[t] = total.astype(BF16)
                    relay(t).start()
                elif b < 3:
                    if (b, t) in ((1, 0), (2, 1)):
                        relay(t).wait_recv()
                        total = total + relay_in[t].astype(F32)
                    q = 2 * (b - 1) + t
                    kbuf[q] = total.astype(BF16)
                    keep(q).start()
                else:
                    own_ref[0:hr, :] = total

        lax.fori_loop(n_first, nk, kstep, 0)

        for k in range(n_half):
            @pl.when(s == k)
            def _(k=k):
                give(k).start()

        @pl.when(s == n_half - 1)
        def _():
            own_ref[hr:D_MODEL, :] = chip_sum(n_half - 1)
            give(n_half - 2).wait_send()
            give(n_half - 1).wait_send()
            for r in range(2):
                relay(r).wait_send()
            for q in range(4):
                keep(q).wait()
            gather.finish()

    half_piece = (hr, COLS_PER_DEV)
    grid_spec = pltpu.PrefetchScalarGridSpec(
        num_scalar_prefetch=1, grid=(n_half,),
        in_specs=[HBM_SPEC,
                  pl.BlockSpec((n, COLS_PER_CHIP), lambda s, order: (0, order[s])),
                  *([HBM_SPEC] * n_small)],
        out_specs=(pl.BlockSpec(piece, lambda s, order: (0, 0)), HBM_SPEC, *([HBM_SPEC] * n_small)),
        scratch_shapes=[pltpu.VMEM((2, n, hr), BF16),
                        pltpu.VMEM((2, hr, COLS_PER_CHIP), F32), pltpu.VMEM((n_half,) + half_piece, F32),
                        pltpu.VMEM((2,) + half_piece, BF16), pltpu.VMEM((4,) + half_piece, BF16),
                        pltpu.VMEM((2,) + half_piece, BF16),
                        pltpu.SemaphoreType.DMA((2 * nk,)),
                        pltpu.SemaphoreType.DMA((n_half,)), pltpu.SemaphoreType.DMA((n_half,)),
                        pltpu.SemaphoreType.DMA((4,)), pltpu.SemaphoreType.DMA((4,)),
                        pltpu.SemaphoreType.DMA((2,)), pltpu.SemaphoreType.DMA((2,)),
                        pltpu.SemaphoreType.DMA((7 * n_small,)), pltpu.SemaphoreType.DMA((7 * n_small,)),
                        pltpu.SemaphoreType.DMA((n_small,))])
    return _pcall(
        body, name="dw_in_exchange", grid_spec=grid_spec,
        out_shape=(_out(piece, F32), _out((2,) + piece, BF16),
                   *(_out((N_DEV,) + a.shape, a.dtype) for a in smalls)),
        compiler_params=_params(1),
    )(order, xn, dproj, *smalls)


def _adamw(g, w, m, v):
    m_new = ADAM_B1 * m + (1.0 - ADAM_B1) * g
    v_new = ADAM_B2 * v + (1.0 - ADAM_B2) * (g * g)
    m_hat = m_new / (1.0 - ADAM_B1 ** ADAM_STEP)
    v_hat = v_new / (1.0 - ADAM_B2 ** ADAM_STEP)
    delta = -ADAM_LR * (m_hat / (jnp.sqrt(v_hat) + ADAM_EPS) + ADAM_WD * w)
    return delta, m_new, v_new


def _reduce_adam_w_in(own, rchip, w, m, v):
    rows, cols = w.shape
    row_tile = 256

    def body(o_ref, r_ref, w_ref, m_ref, v_ref, g_ref, d_ref, nm_ref, nv_ref):
        g = o_ref[...]
        for s in range(2):
            g = g + r_ref[s].astype(F32)
        g_ref[...] = g
        d_ref[...], nm_ref[...], nv_ref[...] = _adamw(g, w_ref[...], m_ref[...], v_ref[...])

    tile = pl.BlockSpec((row_tile, cols), lambda i: (i, 0))
    shp = _out((rows, cols), F32)
    return _pcall(
        body, name="reduce_adam_w_in", grid=(rows // row_tile,),
        out_shape=(shp,) * 4,
        in_specs=[tile, pl.BlockSpec((2, row_tile, cols), lambda i: (0, i, 0)), tile, tile, tile],
        out_specs=(tile,) * 4,
        compiler_params=_params(1),
    )(own, rchip, w, m, v)


_SMALL_LEAVES = ("norm_gain", "final_norm_gain", "b_glu", "ssm_a_re", "ssm_a_im", "ssm_log_dt", "ssm_d", "conv_w",
                 "ssm_c_re", "ssm_c_im", "ssm_b_re", "ssm_b_im")


def _reduce_adam_small(r_pack, r_gc, r_gb, wmv, sharded):
    n_leaf = len(_SMALL_LEAVES)
    n_sh = len(sharded)

    def body(*refs):
        rp_ref, rgc_ref, rgb_ref = refs[:3]
        w_refs = refs[3:3 + 3 * n_leaf]
        sh_in = refs[3 + 3 * n_leaf:3 + 3 * n_leaf + 4 * n_sh]
        outs0 = 3 + 3 * n_leaf + 4 * n_sh
        loss_ref = refs[outs0]
        o_refs = refs[outs0 + 1:outs0 + 1 + 4 * n_leaf]
        sh_out = refs[outs0 + 1 + 4 * n_leaf:outs0 + 1 + 4 * n_leaf + 4 * n_sh]
        own_conv = refs[-1]

        def total(ref):
            acc = ref[0].astype(F32)
            for s in range(1, N_DEV):
                acc = acc + ref[s].astype(F32)
            return acc

        for i in range(n_sh):
            r_ref, w_ref, m_ref, v_ref = sh_in[4 * i:4 * i + 4]
            o_g, o_d, o_m, o_v = sh_out[4 * i:4 * i + 4]
            g = total(r_ref)
            o_g[...] = g
            o_d[...], o_m[...], o_v[...] = _adamw(g, w_ref[...], m_ref[...], v_ref[...])

        sp = total(rp_ref)
        sgc = total(rgc_ref)
        sgb = total(rgb_ref)
        loss_ref[...] = sp[ROW_LOSS:ROW_LOSS + 1, 0:1]

        def wide(r):
            return jnp.concatenate([sp[r:r + 1, :], sp[r + 1:r + 2, :]], axis=1)

        s5 = slice(ROW_S5, ROW_S5 + N_GROUPS)
        eye = (lax.broadcasted_iota(jnp.int32, (N_GROUPS, N_GROUPS), 0)
               == lax.broadcasted_iota(jnp.int32, (N_GROUPS, N_GROUPS), 1)).astype(F32)
        d_rows = jnp.broadcast_to(sp[ROW_BGLU_D + 1:ROW_BGLU_D + 2, :], (GROUP, SSM_W))
        own_p = (lax.broadcasted_iota(jnp.int32, (GROUP, SSM_W), 1) % GROUP
                 == lax.broadcasted_iota(jnp.int32, (GROUP, SSM_W), 0))
        of_group = (lax.broadcasted_iota(jnp.int32, (SSM_W, N_GROUPS), 0) // GROUP
                    == lax.broadcasted_iota(jnp.int32, (SSM_W, N_GROUPS), 1)).astype(BF16)
        d_pg = sum(_dot(t, of_group) for t in _split3(jnp.where(own_p, d_rows, 0.0)))
        me = 4 * lax.axis_index("x") + 2 * lax.axis_index("y") + lax.axis_index("c")
        for k in range(N_DEV):
            @pl.when(me == k)
            def _(k=k):
                own_conv[...] = sp[ROW_CONV:ROW_CONV + SUBLANES, k * CONV_COLS_PER_DEV:(k + 1) * CONV_COLS_PER_DEV]
        grads = {
            "norm_gain": wide(ROW_NORM_GAIN),
            "final_norm_gain": wide(ROW_FINAL_GAIN),
            "b_glu": sp[ROW_BGLU_D:ROW_BGLU_D + 1, :],
            "ssm_a_re": sp[s5, LANE_A_RE:LANE_A_RE + STATE],
            "ssm_a_im": sp[s5, LANE_A_IM:LANE_A_IM + STATE],
            "ssm_log_dt": jnp.sum(sp[s5, LANE_LOG_DT:LANE_LOG_DT + 1] * eye, axis=0, keepdims=True),
            "ssm_d": d_pg,
            "ssm_c_re": sgc[:, 0:STATE],
            "ssm_c_im": sgc[:, STATE:2 * STATE],
            "ssm_b_re": sgb[:, 0:STATE],
            "ssm_b_im": sgb[:, STATE:2 * STATE],
        }
        for i, name in enumerate(_SMALL_LEAVES):
            w_ref, m_ref, v_ref = w_refs[3 * i:3 * i + 3]
            o_g, o_d, o_m, o_v = o_refs[4 * i:4 * i + 4]
            if name == "conv_w":
                for k in range(w_ref.shape[0]):
                    g = own_conv[k:k + 1, :]
                    o_g[k] = g
                    o_d[k], o_m[k], o_v[k] = _adamw(g, w_ref[k], m_ref[k], v_ref[k])
                continue
            g = grads[name]
            o_g[...] = g
            o_d[...], o_m[...], o_v[...] = _adamw(g, w_ref[...], m_ref[...], v_ref[...])

    flat_w = [a for name in _SMALL_LEAVES for a in wmv[name]]
    leaf_shapes = [_out(wmv[name][0].shape, F32) for name in _SMALL_LEAVES for _ in range(4)]
    sh_shapes = [_out(entry[1].shape, F32) for entry in sharded for _ in range(4)]
    operands = (r_pack, r_gc, r_gb, *flat_w, *(a for entry in sharded for a in entry))
    out_shape = (_out((1, 1), F32), *leaf_shapes, *sh_shapes)
    outs = _pcall(
        body, name="reduce_adam_small", grid=(1,), out_shape=out_shape,
        in_specs=_whole_specs(operands), out_specs=tuple(_whole_specs(out_shape)),
        scratch_shapes=[pltpu.VMEM((SUBLANES, CONV_COLS_PER_DEV), F32)],
        compiler_params=_params(1),
    )(*operands)
    leaves = {name: outs[1 + 4 * i:5 + 4 * i] for i, name in enumerate(_SMALL_LEAVES)}
    first = 1 + 4 * n_leaf
    return outs[0], leaves, [outs[first + 4 * i:first + 4 * i + 4] for i in range(n_sh)]


def kernel(x, norm_gain, w_in, ssm_a_re, ssm_a_im, ssm_log_dt, ssm_b_re, ssm_b_im, ssm_c_re, ssm_c_im, ssm_d, w_glu, b_glu, conv_w, w_out, final_norm_gain, loss_target, m_norm_gain, m_w_in, m_ssm_a_re, m_ssm_a_im, m_ssm_log_dt, m_ssm_b_re, m_ssm_b_im, m_ssm_c_re, m_ssm_c_im, m_ssm_d, m_w_glu, m_b_glu, m_conv_w, m_w_out, m_final_norm_gain, v_norm_gain, v_w_in, v_ssm_a_re, v_ssm_a_im, v_ssm_log_dt, v_ssm_b_re, v_ssm_b_im, v_ssm_c_re, v_ssm_c_im, v_ssm_d, v_w_glu, v_b_glu, v_conv_w, v_w_out, v_final_norm_gain):
    n_seq, seq, _ = x.shape
    n = n_seq * seq

    gh_p = lambda b4: jnp.transpose(b4, (0, 1, 3, 2)).reshape(N_GROUPS * GROUP, STATE)
    c2 = lambda a: a.reshape(N_GROUPS * GROUP, STATE)
    b_re2, b_im2 = gh_p(ssm_b_re), gh_p(ssm_b_im)
    d_row = ssm_d[0].reshape(1, SSM_W)

    x2 = x.reshape(n, D_MODEL)
    tgt2 = loss_target.reshape(n, D_MODEL)
    mx, my, mc = lax.axis_index("x"), lax.axis_index("y"), lax.axis_index("c")
    chip_ids = [2 * cx + cy for cx, cy in ((mx, my), (1 - mx, my), (mx, 1 - my), (1 - mx, 1 - my))]
    arrival = chip_ids
    xn, proj, w_in_f, s5 = _in_proj(
        jnp.stack(arrival).astype(jnp.int32), x2, norm_gain, w_in[0],
        (ssm_a_re[0], ssm_a_im[0], ssm_log_dt, b_re2, b_im2, c2(ssm_c_re), c2(ssm_c_im)))
    a_re_x, a_im_x, log_dt_x, ab_re, ab_im, bb_re_m, bb_im_m, c_re_m, c_imn_m = s5
    u3 = proj.reshape(n_seq, seq, IN_COLS)
    conv_p = jnp.pad(conv_w[0], ((0, SUBLANES - 3), (0, LANES - CONV_COLS_PER_DEV)))
    s_re, s_im, y3, w_out_f, w_glu_f, conv_all = _ssm_fwd(
        u3, bb_re_m, bb_im_m, c_re_m, c_imn_m, d_row, ab_re, ab_im,
        w_out[0], w_glu[0], conv_p, n_seq, seq)
    conv8 = jnp.transpose(conv_all[:, :, :CONV_COLS_PER_DEV], (1, 0, 2)).reshape(SUBLANES, CONV_W)
    (dh2, dy, dzs, dbc, dzc, dyc, dw_out, dw_glu, loss_t, dgf, dbg, dcw) = _mix(
        x2, tgt2, y3.reshape(n, SSM_W), proj, final_norm_gain.reshape(1, D_MODEL), b_glu, conv8,
        w_glu_f, w_out_f, seq)

    du3, dc_re_d, dc_im_d, dbb_re_d, dbb_im_d, dab_re, dab_im, dd, r_out, r_glu = _ssm_bwd(
        dy.reshape(n_seq, seq, SSM_W), u3, s_re, s_im, bb_re_m, bb_im_m, c_re_m, c_imn_m, d_row, ab_re, ab_im,
        dw_out.reshape(N_DEV, OUT_ROWS_PER_DEV, D_MODEL), dw_glu.reshape(N_DEV, GLU_ROWS_PER_DEV, SSM_W), n_seq, seq)
    du = du3.reshape(n, SSM_W)
    grad_x2, dproj, dg8 = _in_bwd(x2, dh2, du, dzs, dyc, proj, dbc, dzc, norm_gain, conv8, w_in_f, seq)
    pack, gc, gb = _ssm_disc_bwd_pack(
        a_re_x, a_im_x, log_dt_x, b_re2, b_im2, dab_re, dab_im,
        dbb_re_d, dbb_im_d, loss_t, dg8, dgf, dbg, dd, dcw, dc_re_d, dc_im_d)

    own_in, rchip_in, r_pack, r_gc, r_gb = _dw_in_exchange(
        [chip_ids[3], chip_ids[2], chip_ids[1], chip_ids[0]],
        xn, dproj, [pack, gc, gb])

    flat2 = lambda a: a.reshape(a.shape[-2:]) if a.ndim > 2 else a.reshape(1, -1)
    c2 = lambda a: a.reshape(N_GROUPS * GROUP, STATE)
    wmv = dict(norm_gain=(norm_gain, m_norm_gain, v_norm_gain),
               final_norm_gain=tuple(flat2(a) for a in (final_norm_gain, m_final_norm_gain, v_final_norm_gain)),
               b_glu=(b_glu, m_b_glu, v_b_glu),
               ssm_a_re=tuple(flat2(a) for a in (ssm_a_re, m_ssm_a_re, v_ssm_a_re)),
               ssm_a_im=tuple(flat2(a) for a in (ssm_a_im, m_ssm_a_im, v_ssm_a_im)),
               ssm_log_dt=(ssm_log_dt, m_ssm_log_dt, v_ssm_log_dt),
               ssm_d=tuple(jnp.transpose(a, (0, 2, 1)).reshape(GROUP, N_GROUPS) for a in (ssm_d, m_ssm_d, v_ssm_d)),
               conv_w=tuple(jnp.transpose(a, (1, 0, 2)) for a in (conv_w, m_conv_w, v_conv_w)),
               ssm_c_re=tuple(c2(a) for a in (ssm_c_re, m_ssm_c_re, v_ssm_c_re)),
               ssm_c_im=tuple(c2(a) for a in (ssm_c_im, m_ssm_c_im, v_ssm_c_im)),
               ssm_b_re=(b_re2, gh_p(m_ssm_b_re), gh_p(v_ssm_b_re)),
               ssm_b_im=(b_im2, gh_p(m_ssm_b_im), gh_p(v_ssm_b_im)))

    res_in = _reduce_adam_w_in(own_in, rchip_in, w_in[0], m_w_in[0], v_w_in[0])
    loss11, small, (res_out, res_glu) = _reduce_adam_small(
        r_pack, r_gc, r_gb, wmv,
        [(r_out, w_out[0], m_w_out[0], v_w_out[0]), (r_glu, w_glu[0], m_w_glu[0], v_w_glu[0])])
    loss = loss11.reshape(())

    shapes = dict(norm_gain=(1, D_MODEL), ssm_a_re=(1, N_GROUPS, STATE), ssm_a_im=(1, N_GROUPS, STATE),
                  ssm_log_dt=(1, N_GROUPS), ssm_c_re=(1, N_GROUPS, GROUP, STATE), ssm_c_im=(1, N_GROUPS, GROUP, STATE),
                  b_glu=(1, SSM_W), final_norm_gain=(D_MODEL,))
    big = dict(w_in=res_in, w_glu=res_glu, w_out=res_out)

    def leaf(kind, name):
        if name in big:
            return big[name][kind][None]
        if name in ("ssm_b_re", "ssm_b_im"):
            return jnp.transpose(small[name][kind].reshape(1, N_GROUPS, GROUP, STATE), (0, 1, 3, 2))
        if name == "ssm_d":
            return jnp.transpose(small[name][kind].reshape(1, GROUP, N_GROUPS), (0, 2, 1))
        if name == "conv_w":
            return jnp.transpose(small[name][kind], (1, 0, 2))
        return small[name][kind].reshape(shapes[name])

    order = ["norm_gain", "w_in", "ssm_a_re", "ssm_a_im", "ssm_log_dt", "ssm_b_re", "ssm_b_im", "ssm_c_re",
             "ssm_c_im", "ssm_d", "w_glu", "b_glu", "conv_w", "w_out", "final_norm_gain"]
    outs = [loss, grad_x2.reshape(x.shape)]
    for kind in range(4):
        outs += [leaf(kind, name) for name in order]
    return tuple(outs)
```
